```python
import math
import jax, jax.numpy as jnp
from jax import lax
import numpy as np

D_MODEL = 1024
BATCH = 8
SEQ = 2048
DEPTH = 1

CHUNK = 64
D_MIX = D_MODEL
D_LRU = D_MIX // 2
D_CONV = D_MIX - D_LRU
LRU_HEADS = 8
LRU_HEAD_DIM = D_LRU // LRU_HEADS
CONV_GROUPS = 8
CONV_GROUP_DIM = D_CONV // CONV_GROUPS
LRU_CONV_WIDTH = 4
SHORT_CONV_WIDTH = 3
D_FF = 4 * D_MODEL
C_GATE = 8.0
MIN_RAD = 0.9
MAX_RAD = 0.999
EPS = 1e-6
N_ADA = 6
D_IN = 2 * D_LRU + 3 * D_CONV

kernel_name = "hybrid_rglru_shortconv_adaln_block"


def rmsnorm(x, g):
    xf = x.astype(jnp.float32)
    y = xf * lax.rsqrt(jnp.mean(xf * xf, axis=-1, keepdims=True) + EPS)
    return (y * g.astype(jnp.float32)).astype(x.dtype)


def headwise_rmsnorm(y, g, n_heads):
    b, s, w = y.shape
    yh = y.reshape(b, s, n_heads, w // n_heads).astype(jnp.float32)
    yh = yh * lax.rsqrt(jnp.mean(yh * yh, axis=-1, keepdims=True) + EPS)
    return (yh.reshape(b, s, w) * g.astype(jnp.float32)).astype(y.dtype)


def causal_depthwise_conv(x, w):
    k, ch = w.shape
    rhs = w[:, None, :].astype(x.dtype)
    return lax.conv_general_dilated(
        x, rhs, window_strides=(1,), padding=[(k - 1, 0)],
        dimension_numbers=("NWC", "WIO", "NWC"), feature_group_count=ch)


def chunked_linear_scan(a, b):
    bn, s, w = a.shape
    nc = s // CHUNK
    a = a.reshape(bn, nc, CHUNK, w)
    b = b.reshape(bn, nc, CHUNK, w)

    def combine(left, right):
        al, bl = left
        ar, br = right
        return al * ar, ar * bl + br

    a_cum, h_loc = lax.associative_scan(combine, (a, b), axis=2)

    def step(h, inp):
        a_last, h_last = inp
        return a_last * h + h_last, h

    _, h_in = lax.scan(step, jnp.zeros((bn, w), jnp.float32),
                       (jnp.swapaxes(a_cum[:, :, -1], 0, 1), jnp.swapaxes(h_loc[:, :, -1], 0, 1)))
    h_in = jnp.swapaxes(h_in, 0, 1)
    h = h_loc + a_cum * h_in[:, :, None, :]
    return h.reshape(bn, s, w)


def rg_lru(xl, gate_a_w, gate_a_b, gate_x_w, gate_x_b, a_param):
    bn, s, w = xl.shape
    xh = xl.reshape(bn, s, LRU_HEADS, LRU_HEAD_DIM)
    r = jax.nn.sigmoid(jnp.einsum("bshi,hij->bshj", xh, gate_a_w).reshape(bn, s, w) + gate_a_b)
    i = jax.nn.sigmoid(jnp.einsum("bshi,hij->bshj", xh, gate_x_w).reshape(bn, s, w) + gate_x_b)
    log_a = -C_GATE * r.astype(jnp.float32) * jax.nn.softplus(a_param.astype(jnp.float32))
    a = jnp.exp(log_a)
    mult = jnp.sqrt(-jnp.expm1(2.0 * log_a))
    is_first = (jnp.arange(s) == 0)[None, :, None]
    mult = jnp.where(is_first, jnp.ones_like(mult), mult)
    bx = mult * (i * xl).astype(jnp.float32)
    return chunked_linear_scan(a, bx).astype(xl.dtype)


def _fwd_setup_inputs(seed: int = 0) -> dict:
    key = jax.random.key(seed)
    ks = jax.random.split(key, 24)
    f32 = jnp.float32
    nrm = lambda k, shape, scale: (jax.random.normal(k, shape, f32) * scale)
    x = jax.random.normal(ks[0], (BATCH, SEQ, D_MODEL), f32)
    c = jax.random.normal(ks[1], (BATCH, D_MODEL), f32)
    ada_w = nrm(ks[2], (DEPTH, D_MODEL, N_ADA * D_MODEL), 0.5 * D_MODEL ** -0.5)
    ada_b = nrm(ks[3], (DEPTH, N_ADA * D_MODEL), 0.01)
    norm1_g = 1.0 + nrm(ks[4], (DEPTH, D_MODEL), 0.02)
    w_in = nrm(ks[5], (DEPTH, D_MODEL, D_IN), D_MODEL ** -0.5)
    lru_conv_w = nrm(ks[6], (DEPTH, LRU_CONV_WIDTH, D_LRU), LRU_CONV_WIDTH ** -0.5)
    lru_conv_b = nrm(ks[7], (DEPTH, D_LRU), 0.01)
    gate_a_w = nrm(ks[8], (DEPTH, LRU_HEADS, LRU_HEAD_DIM, LRU_HEAD_DIM), LRU_HEAD_DIM ** -0.5)
    gate_a_b = nrm(ks[9], (DEPTH, D_LRU), 0.01)
    gate_x_w = nrm(ks[10], (DEPTH, LRU_HEADS, LRU_HEAD_DIM, LRU_HEAD_DIM), LRU_HEAD_DIM ** -0.5)
    gate_x_b = nrm(ks[11], (DEPTH, D_LRU), 0.01)
    u = jax.random.uniform(ks[12], (DEPTH, D_LRU), f32, MIN_RAD ** 2, MAX_RAD ** 2)
    a_param = jnp.log(jnp.expm1(-0.5 * jnp.log(u)))
    short_conv_w = nrm(ks[13], (DEPTH, SHORT_CONV_WIDTH, D_CONV), SHORT_CONV_WIDTH ** -0.5)
    lru_out_g = 1.0 + nrm(ks[14], (DEPTH, D_LRU), 0.02)
    conv_out_g = 1.0 + nrm(ks[15], (DEPTH, D_CONV), 0.02)
    w_out = nrm(ks[16], (DEPTH, D_MIX, D_MODEL), D_MIX ** -0.5)
    norm2_g = 1.0 + nrm(ks[17], (DEPTH, D_MODEL), 0.02)
    w_mlp1 = nrm(ks[18], (DEPTH, D_MODEL, D_FF), D_MODEL ** -0.5)
    w_mlp2 = nrm(ks[19], (DEPTH, D_FF, D_MODEL), D_FF ** -0.5)
    final_g = 1.0 + nrm(ks[20], (D_MODEL,), 0.02)
    return {"x": x, "c": c, "ada_w": ada_w, "ada_b": ada_b, "norm1_g": norm1_g,
            "w_in": w_in, "lru_conv_w": lru_conv_w, "lru_conv_b": lru_conv_b,
            "gate_a_w": gate_a_w, "gate_a_b": gate_a_b, "gate_x_w": gate_x_w,
            "gate_x_b": gate_x_b, "a_param": a_param, "short_conv_w": short_conv_w,
            "lru_out_g": lru_out_g, "conv_out_g": conv_out_g, "w_out": w_out,
            "norm2_g": norm2_g, "w_mlp1": w_mlp1, "w_mlp2": w_mlp2, "final_g": final_g}


def _fwd_reference(x, c, ada_w, ada_b, norm1_g, w_in, lru_conv_w, lru_conv_b, gate_a_w, gate_a_b,
              gate_x_w, gate_x_b, a_param, short_conv_w, lru_out_g, conv_out_g, w_out,
              norm2_g, w_mlp1, w_mlp2, final_g):
    sc = jax.nn.silu(c)
    for l in range(DEPTH):
        mod = sc @ ada_w[l] + ada_b[l]
        shift1, scale1, gate1, shift2, scale2, gate2 = jnp.split(mod[:, None, :], N_ADA, axis=-1)

        h = rmsnorm(x, norm1_g[l]) * (1.0 + scale1) + shift1
        proj = h @ w_in[l]
        u_lx, u_ly, u_b, u_c, u_v = jnp.split(
            proj, np.cumsum([D_LRU, D_LRU, D_CONV, D_CONV])[:].tolist(), axis=-1)

        xl = causal_depthwise_conv(u_lx, lru_conv_w[l]) + lru_conv_b[l]
        hl = rg_lru(xl, gate_a_w[l], gate_a_b[l], gate_x_w[l], gate_x_b[l], a_param[l])
        y_lru = headwise_rmsnorm(jax.nn.gelu(u_ly) * hl, lru_out_g[l], LRU_HEADS)

        y_conv = u_b * causal_depthwise_conv(u_c * u_v, short_conv_w[l])
        y_conv = headwise_rmsnorm(y_conv, conv_out_g[l], CONV_GROUPS)

        mixed = jnp.concatenate([y_lru, y_conv], axis=-1) @ w_out[l]
        x = x + gate1 * mixed

        h2 = rmsnorm(x, norm2_g[l]) * (1.0 + scale2) + shift2
        x = x + gate2 * (jnp.square(jax.nn.relu(h2 @ w_mlp1[l])) @ w_mlp2[l])
    return rmsnorm(x, final_g)


import jax as _jax
import jax.numpy as _jnp

TWIN_FORMAT = 'train_step'
FWD_PARAMS = ['x', 'c', 'ada_w', 'ada_b', 'norm1_g', 'w_in', 'lru_conv_w', 'lru_conv_b', 'gate_a_w', 'gate_a_b', 'gate_x_w', 'gate_x_b', 'a_param', 'short_conv_w', 'lru_out_g', 'conv_out_g', 'w_out', 'norm2_g', 'w_mlp1', 'w_mlp2', 'final_g']
TWIN_WEIGHTS = ['ada_w', 'ada_b', 'norm1_g', 'w_in', 'lru_conv_w', 'lru_conv_b', 'gate_a_w', 'gate_a_b', 'gate_x_w', 'gate_x_b', 'a_param', 'short_conv_w', 'lru_out_g', 'conv_out_g', 'w_out', 'norm2_g', 'w_mlp1', 'w_mlp2', 'final_g']
TWIN_DIFF_INPUT = 'x'
TWIN_INPUTS = ['x', 'c', 'ada_w', 'ada_b', 'norm1_g', 'w_in', 'lru_conv_w', 'lru_conv_b', 'gate_a_w', 'gate_a_b', 'gate_x_w', 'gate_x_b', 'a_param', 'short_conv_w', 'lru_out_g', 'conv_out_g', 'w_out', 'norm2_g', 'w_mlp1', 'w_mlp2', 'final_g', 'loss_target', 'm_ada_w', 'm_ada_b', 'm_norm1_g', 'm_w_in', 'm_lru_conv_w', 'm_lru_conv_b', 'm_gate_a_w', 'm_gate_a_b', 'm_gate_x_w', 'm_gate_x_b', 'm_a_param', 'm_short_conv_w', 'm_lru_out_g', 'm_conv_out_g', 'm_w_out', 'm_norm2_g', 'm_w_mlp1', 'm_w_mlp2', 'm_final_g', 'v_ada_w', 'v_ada_b', 'v_norm1_g', 'v_w_in', 'v_lru_conv_w', 'v_lru_conv_b', 'v_gate_a_w', 'v_gate_a_b', 'v_gate_x_w', 'v_gate_x_b', 'v_a_param', 'v_short_conv_w', 'v_lru_out_g', 'v_conv_out_g', 'v_w_out', 'v_norm2_g', 'v_w_mlp1', 'v_w_mlp2', 'v_final_g']
TWIN_OUTPUTS = ['loss', 'grad_x', 'grad_ada_w', 'grad_ada_b', 'grad_norm1_g', 'grad_w_in', 'grad_lru_conv_w', 'grad_lru_conv_b', 'grad_gate_a_w', 'grad_gate_a_b', 'grad_gate_x_w', 'grad_gate_x_b', 'grad_a_param', 'grad_short_conv_w', 'grad_lru_out_g', 'grad_conv_out_g', 'grad_w_out', 'grad_norm2_g', 'grad_w_mlp1', 'grad_w_mlp2', 'grad_final_g', 'delta_ada_w', 'delta_ada_b', 'delta_norm1_g', 'delta_w_in', 'delta_lru_conv_w', 'delta_lru_conv_b', 'delta_gate_a_w', 'delta_gate_a_b', 'delta_gate_x_w', 'delta_gate_x_b', 'delta_a_param', 'delta_short_conv_w', 'delta_lru_out_g', 'delta_conv_out_g', 'delta_w_out', 'delta_norm2_g', 'delta_w_mlp1', 'delta_w_mlp2', 'delta_final_g', 'new_m_ada_w', 'new_m_ada_b', 'new_m_norm1_g', 'new_m_w_in', 'new_m_lru_conv_w', 'new_m_lru_conv_b', 'new_m_gate_a_w', 'new_m_gate_a_b', 'new_m_gate_x_w', 'new_m_gate_x_b', 'new_m_a_param', 'new_m_short_conv_w', 'new_m_lru_out_g', 'new_m_conv_out_g', 'new_m_w_out', 'new_m_norm2_g', 'new_m_w_mlp1', 'new_m_w_mlp2', 'new_m_final_g', 'new_v_ada_w', 'new_v_ada_b', 'new_v_norm1_g', 'new_v_w_in', 'new_v_lru_conv_w', 'new_v_lru_conv_b', 'new_v_gate_a_w', 'new_v_gate_a_b', 'new_v_gate_x_w', 'new_v_gate_x_b', 'new_v_a_param', 'new_v_short_conv_w', 'new_v_lru_out_g', 'new_v_conv_out_g', 'new_v_w_out', 'new_v_norm2_g', 'new_v_w_mlp1', 'new_v_w_mlp2', 'new_v_final_g']
TWIN_LEAF_KINDS = {'loss': 'loss', 'grad_x': 'grad_x', 'grad_ada_w': 'grad_w', 'grad_ada_b': 'grad_w', 'grad_norm1_g': 'grad_w', 'grad_w_in': 'grad_w', 'grad_lru_conv_w': 'grad_w', 'grad_lru_conv_b': 'grad_w', 'grad_gate_a_w': 'grad_w', 'grad_gate_a_b': 'grad_w', 'grad_gate_x_w': 'grad_w', 'grad_gate_x_b': 'grad_w', 'grad_a_param': 'grad_w', 'grad_short_conv_w': 'grad_w', 'grad_lru_out_g': 'grad_w', 'grad_conv_out_g': 'grad_w', 'grad_w_out': 'grad_w', 'grad_norm2_g': 'grad_w', 'grad_w_mlp1': 'grad_w', 'grad_w_mlp2': 'grad_w', 'grad_final_g': 'grad_w', 'delta_ada_w': 'delta_w', 'delta_ada_b': 'delta_w', 'delta_norm1_g': 'delta_w', 'delta_w_in': 'delta_w', 'delta_lru_conv_w': 'delta_w', 'delta_lru_conv_b': 'delta_w', 'delta_gate_a_w': 'delta_w', 'delta_gate_a_b': 'delta_w', 'delta_gate_x_w': 'delta_w', 'delta_gate_x_b': 'delta_w', 'delta_a_param': 'delta_w', 'delta_short_conv_w': 'delta_w', 'delta_lru_out_g': 'delta_w', 'delta_conv_out_g': 'delta_w', 'delta_w_out': 'delta_w', 'delta_norm2_g': 'delta_w', 'delta_w_mlp1': 'delta_w', 'delta_w_mlp2': 'delta_w', 'delta_final_g': 'delta_w', 'new_m_ada_w': 'new_m', 'new_m_ada_b': 'new_m', 'new_m_norm1_g': 'new_m', 'new_m_w_in': 'new_m', 'new_m_lru_conv_w': 'new_m', 'new_m_lru_conv_b': 'new_m', 'new_m_gate_a_w': 'new_m', 'new_m_gate_a_b': 'new_m', 'new_m_gate_x_w': 'new_m', 'new_m_gate_x_b': 'new_m', 'new_m_a_param': 'new_m', 'new_m_short_conv_w': 'new_m', 'new_m_lru_out_g': 'new_m', 'new_m_conv_out_g': 'new_m', 'new_m_w_out': 'new_m', 'new_m_norm2_g': 'new_m', 'new_m_w_mlp1': 'new_m', 'new_m_w_mlp2': 'new_m', 'new_m_final_g': 'new_m', 'new_v_ada_w': 'new_v', 'new_v_ada_b': 'new_v', 'new_v_norm1_g': 'new_v', 'new_v_w_in': 'new_v', 'new_v_lru_conv_w': 'new_v', 'new_v_lru_conv_b': 'new_v', 'new_v_gate_a_w': 'new_v', 'new_v_gate_a_b': 'new_v', 'new_v_gate_x_w': 'new_v', 'new_v_gate_x_b': 'new_v', 'new_v_a_param': 'new_v', 'new_v_short_conv_w': 'new_v', 'new_v_lru_out_g': 'new_v', 'new_v_conv_out_g': 'new_v', 'new_v_w_out': 'new_v', 'new_v_norm2_g': 'new_v', 'new_v_w_mlp1': 'new_v', 'new_v_w_mlp2': 'new_v', 'new_v_final_g': 'new_v'}


def _forward(args):
    return _fwd_reference(*[args[k] for k in FWD_PARAMS])


def _output_shape():
    out = _jax.eval_shape(lambda: _forward(_fwd_setup_inputs(0)))
    return out.shape, out.dtype

N_MICROBATCH = 1
ADAM_LR = 0.001
ADAM_B1 = 0.9
ADAM_B2 = 0.999
ADAM_EPS = 1e-08
ADAM_WD = 0.01
ADAM_STEP = 10
PER_EXAMPLE_BATCH_AXIS = {'x': 0, 'c': 0, 'loss_target': 0}
SHARED_INPUTS = []
_WEIGHT_DTYPES = {'ada_w': _jnp.float32, 'ada_b': _jnp.float32, 'norm1_g': _jnp.float32, 'w_in': _jnp.float32, 'lru_conv_w': _jnp.float32, 'lru_conv_b': _jnp.float32, 'gate_a_w': _jnp.float32, 'gate_a_b': _jnp.float32, 'gate_x_w': _jnp.float32, 'gate_x_b': _jnp.float32, 'a_param': _jnp.float32, 'short_conv_w': _jnp.float32, 'lru_out_g': _jnp.float32, 'conv_out_g': _jnp.float32, 'w_out': _jnp.float32, 'norm2_g': _jnp.float32, 'w_mlp1': _jnp.float32, 'w_mlp2': _jnp.float32, 'final_g': _jnp.float32}
MOMENT_SCALE = {'ada_w': 6.678095e-02, 'ada_b': 1.080639e-01, 'norm1_g': 6.192669e-02, 'w_in': 4.168050e-02, 'lru_conv_w': 4.499979e-02, 'lru_conv_b': 1.171895e-01, 'gate_a_w': 8.357273e-03, 'gate_a_b': 7.709774e-03, 'gate_x_w': 1.439511e-02, 'gate_x_b': 1.363397e-02, 'a_param': 1.654298e-02, 'short_conv_w': 4.019865e-02, 'lru_out_g': 3.872860e-02, 'conv_out_g': 3.876420e-02, 'w_out': 3.754012e-02, 'norm2_g': 5.579212e-02, 'w_mlp1': 2.860656e-02, 'w_mlp2': 5.021965e-02, 'final_g': 1.617704e+01}


def _to_microbatches(a, axis):
    t = _jnp.moveaxis(a, axis, 0)
    t = t.reshape((N_MICROBATCH, t.shape[0] // N_MICROBATCH) + t.shape[1:])
    return _jnp.moveaxis(t, 1, axis + 1)


def setup_inputs(seed: int = 0) -> dict:
    inp = _fwd_setup_inputs(seed)
    key = _jax.random.fold_in(_jax.random.key(seed), 7919)
    shape, _ = _output_shape()
    out = dict(inp)
    out["loss_target"] = _jax.random.normal(_jax.random.fold_in(key, 0), shape, _jnp.float32)
    for i, name in enumerate(TWIN_WEIGHTS):
        w = inp[name].astype(_jnp.float32)
        if MOMENT_SCALE is None:
            s = _jnp.sqrt(_jnp.mean(_jnp.square(w)) + 1e-30)
        else:
            s = MOMENT_SCALE[name]
        km, kv = _jax.random.split(_jax.random.fold_in(key, i + 1))
        out[name] = w
        out["m_" + name] = s * _jax.random.normal(km, w.shape, _jnp.float32)
        out["v_" + name] = (s * s) * _jax.random.uniform(kv, w.shape, _jnp.float32, 0.5, 1.5)
    if N_MICROBATCH > 1:
        for name, axis in PER_EXAMPLE_BATCH_AXIS.items():
            out[name] = _to_microbatches(out[name], axis)
    return {'x': out['x'], 'c': out['c'], 'ada_w': out['ada_w'], 'ada_b': out['ada_b'], 'norm1_g': out['norm1_g'], 'w_in': out['w_in'], 'lru_conv_w': out['lru_conv_w'], 'lru_conv_b': out['lru_conv_b'], 'gate_a_w': out['gate_a_w'], 'gate_a_b': out['gate_a_b'], 'gate_x_w': out['gate_x_w'], 'gate_x_b': out['gate_x_b'], 'a_param': out['a_param'], 'short_conv_w': out['short_conv_w'], 'lru_out_g': out['lru_out_g'], 'conv_out_g': out['conv_out_g'], 'w_out': out['w_out'], 'norm2_g': out['norm2_g'], 'w_mlp1': out['w_mlp1'], 'w_mlp2': out['w_mlp2'], 'final_g': out['final_g'], 'loss_target': out['loss_target'], 'm_ada_w': out['m_ada_w'], 'm_ada_b': out['m_ada_b'], 'm_norm1_g': out['m_norm1_g'], 'm_w_in': out['m_w_in'], 'm_lru_conv_w': out['m_lru_conv_w'], 'm_lru_conv_b': out['m_lru_conv_b'], 'm_gate_a_w': out['m_gate_a_w'], 'm_gate_a_b': out['m_gate_a_b'], 'm_gate_x_w': out['m_gate_x_w'], 'm_gate_x_b': out['m_gate_x_b'], 'm_a_param': out['m_a_param'], 'm_short_conv_w': out['m_short_conv_w'], 'm_lru_out_g': out['m_lru_out_g'], 'm_conv_out_g': out['m_conv_out_g'], 'm_w_out': out['m_w_out'], 'm_norm2_g': out['m_norm2_g'], 'm_w_mlp1': out['m_w_mlp1'], 'm_w_mlp2': out['m_w_mlp2'], 'm_final_g': out['m_final_g'], 'v_ada_w': out['v_ada_w'], 'v_ada_b': out['v_ada_b'], 'v_norm1_g': out['v_norm1_g'], 'v_w_in': out['v_w_in'], 'v_lru_conv_w': out['v_lru_conv_w'], 'v_lru_conv_b': out['v_lru_conv_b'], 'v_gate_a_w': out['v_gate_a_w'], 'v_gate_a_b': out['v_gate_a_b'], 'v_gate_x_w': out['v_gate_x_w'], 'v_gate_x_b': out['v_gate_x_b'], 'v_a_param': out['v_a_param'], 'v_short_conv_w': out['v_short_conv_w'], 'v_lru_out_g': out['v_lru_out_g'], 'v_conv_out_g': out['v_conv_out_g'], 'v_w_out': out['v_w_out'], 'v_norm2_g': out['v_norm2_g'], 'v_w_mlp1': out['v_w_mlp1'], 'v_w_mlp2': out['v_w_mlp2'], 'v_final_g': out['v_final_g']}


def _loss(weights, diff, rest, loss_target):
    with _jax.named_scope("forward"):
        args = {**rest, TWIN_DIFF_INPUT: diff, **{k: w.astype(_WEIGHT_DTYPES[k]) for k, w in weights.items()}}
        y = _forward(args)
    with _jax.named_scope("loss_head"):
        err = _jnp.square(y.astype(_jnp.float32) - loss_target)
        return 0.5 * _jnp.sum(_jnp.mean(err, axis=-1)) if err.ndim else 0.5 * err


def _adamw(w, g, m, v):
    m = ADAM_B1 * m + (1.0 - ADAM_B1) * g
    v = ADAM_B2 * v + (1.0 - ADAM_B2) * _jnp.square(g)
    m_hat = m / (1.0 - ADAM_B1 ** ADAM_STEP)
    v_hat = v / (1.0 - ADAM_B2 ** ADAM_STEP)
    delta = -ADAM_LR * (m_hat / (_jnp.sqrt(v_hat) + ADAM_EPS) + ADAM_WD * w)
    return delta, m, v


def reference(x, c, ada_w, ada_b, norm1_g, w_in, lru_conv_w, lru_conv_b, gate_a_w, gate_a_b, gate_x_w, gate_x_b, a_param, short_conv_w, lru_out_g, conv_out_g, w_out, norm2_g, w_mlp1, w_mlp2, final_g, loss_target, m_ada_w, m_ada_b, m_norm1_g, m_w_in, m_lru_conv_w, m_lru_conv_b, m_gate_a_w, m_gate_a_b, m_gate_x_w, m_gate_x_b, m_a_param, m_short_conv_w, m_lru_out_g, m_conv_out_g, m_w_out, m_norm2_g, m_w_mlp1, m_w_mlp2, m_final_g, v_ada_w, v_ada_b, v_norm1_g, v_w_in, v_lru_conv_w, v_lru_conv_b, v_gate_a_w, v_gate_a_b, v_gate_x_w, v_gate_x_b, v_a_param, v_short_conv_w, v_lru_out_g, v_conv_out_g, v_w_out, v_norm2_g, v_w_mlp1, v_w_mlp2, v_final_g):
    given = dict(x=x, c=c, ada_w=ada_w, ada_b=ada_b, norm1_g=norm1_g, w_in=w_in, lru_conv_w=lru_conv_w, lru_conv_b=lru_conv_b, gate_a_w=gate_a_w, gate_a_b=gate_a_b, gate_x_w=gate_x_w, gate_x_b=gate_x_b, a_param=a_param, short_conv_w=short_conv_w, lru_out_g=lru_out_g, conv_out_g=conv_out_g, w_out=w_out, norm2_g=norm2_g, w_mlp1=w_mlp1, w_mlp2=w_mlp2, final_g=final_g, loss_target=loss_target, m_ada_w=m_ada_w, m_ada_b=m_ada_b, m_norm1_g=m_norm1_g, m_w_in=m_w_in, m_lru_conv_w=m_lru_conv_w, m_lru_conv_b=m_lru_conv_b, m_gate_a_w=m_gate_a_w, m_gate_a_b=m_gate_a_b, m_gate_x_w=m_gate_x_w, m_gate_x_b=m_gate_x_b, m_a_param=m_a_param, m_short_conv_w=m_short_conv_w, m_lru_out_g=m_lru_out_g, m_conv_out_g=m_conv_out_g, m_w_out=m_w_out, m_norm2_g=m_norm2_g, m_w_mlp1=m_w_mlp1, m_w_mlp2=m_w_mlp2, m_final_g=m_final_g, v_ada_w=v_ada_w, v_ada_b=v_ada_b, v_norm1_g=v_norm1_g, v_w_in=v_w_in, v_lru_conv_w=v_lru_conv_w, v_lru_conv_b=v_lru_conv_b, v_gate_a_w=v_gate_a_w, v_gate_a_b=v_gate_a_b, v_gate_x_w=v_gate_x_w, v_gate_x_b=v_gate_x_b, v_a_param=v_a_param, v_short_conv_w=v_short_conv_w, v_lru_out_g=v_lru_out_g, v_conv_out_g=v_conv_out_g, v_w_out=v_w_out, v_norm2_g=v_norm2_g, v_w_mlp1=v_w_mlp1, v_w_mlp2=v_w_mlp2, v_final_g=v_final_g)
    weights = {n: given[n] for n in TWIN_WEIGHTS}
    shared = {n: given[n] for n in SHARED_INPUTS}
    per_example = {n: given[n] for n in ['x', 'c']}
    grad_fn = _jax.value_and_grad(_loss, argnums=(0, 1))

    def one_microbatch(ex, loss_target):
        ex = dict(ex)
        diff = ex.pop(TWIN_DIFF_INPUT)
        return grad_fn(weights, diff, {**shared, **ex}, loss_target)

    if N_MICROBATCH == 1:
        loss, (grad_w, grad_x) = one_microbatch(per_example, given["loss_target"])
    else:
        def body(carry, xs):
            loss_sum, grad_sum = carry
            l_k, (gw_k, gx_k) = one_microbatch(xs[0], xs[1])
            with _jax.named_scope("update"):
                return (loss_sum + l_k, _jax.tree.map(_jnp.add, grad_sum, gw_k)), gx_k

        init = (_jnp.zeros((), _jnp.float32), _jax.tree.map(_jnp.zeros_like, weights))
        (loss, grad_w), grad_x = _jax.lax.scan(body, init, (per_example, given["loss_target"]))
    with _jax.named_scope("update"):
        delta_w, new_m, new_v = {}, {}, {}
        for n in TWIN_WEIGHTS:
            delta_w[n], new_m[n], new_v[n] = _adamw(weights[n], grad_w[n], given["m_" + n], given["v_" + n])
    return (loss, grad_x, *[grad_w[n] for n in TWIN_WEIGHTS], *[delta_w[n] for n in TWIN_WEIGHTS],
            *[new_m[n] for n in TWIN_WEIGHTS], *[new_v[n] for n in TWIN_WEIGHTS])
```

```python
import functools

import jax
import jax.numpy as jnp
from jax import lax
from jax.experimental import pallas as pl
from jax.experimental.pallas import tpu as pltpu

F32 = jnp.float32
BF16 = jnp.bfloat16

D_MODEL = 1024
D_LRU = 512
D_IN = 2560
D_FF = 4096
N_CHIP = 4
WIN_BLK = D_IN // N_CHIP
WOUT_BLK = D_MODEL // N_CHIP
FF_BLK = D_FF // N_CHIP
HEAD = 64
EPS = 1e-6
C_GATE = 8.0
TOKEN_TILE = 256
HALO = 8
VMEM_LIMIT = 60 * 1024 * 1024

ADAM_LR = 0.001
ADAM_B1 = 0.9
ADAM_B2 = 0.999
ADAM_EPS = 1e-08
ADAM_WD = 0.01
ADAM_STEP = 10

MESH = pl.DeviceIdType.MESH
ANY = pl.BlockSpec(memory_space=pl.ANY)
VMEM = pl.BlockSpec(memory_space=pltpu.VMEM)


def _full(shape, single=False):
    nd = len(shape)
    if single:
        return pl.BlockSpec(shape, lambda *_: (0,) * nd, pipeline_mode=pl.Buffered(1))
    return pl.BlockSpec(shape, lambda *_: (0,) * nd)


def _dot(a, b):
    return jnp.dot(a, b, preferred_element_type=F32)


def _dot_nt(a, b):
    return lax.dot_general(a, b, (((1,), (1,)), ((), ())), preferred_element_type=F32)


def _dot_tn(a, b):
    return lax.dot_general(a, b, (((0,), (0,)), ((), ())), preferred_element_type=F32)


def _gmean(v, a64):
    hi = v.astype(BF16)
    lo = (v - hi.astype(F32)).astype(BF16)
    return _dot(hi, a64) + _dot(lo, a64)


def _gelu(x):
    u = 0.7978845608028654 * (x + 0.044715 * x * x * x)
    t = jnp.tanh(u)
    return 0.5 * x * (1.0 + t), t


def _gelu_grad(x, t):
    du = 0.7978845608028654 * (1.0 + 3.0 * 0.044715 * x * x)
    return 0.5 * (1.0 + t) + 0.5 * x * (1.0 - t * t) * du


def _log1p_pos(y):
    return jnp.where(y < 1e-2, y * (1.0 - y * (0.5 - y * (1.0 / 3.0 - y * 0.25))), jnp.log(1.0 + y))


def _softplus(a):
    return jnp.maximum(a, 0.0) + _log1p_pos(jnp.exp(-jnp.abs(a)))


def _neg_expm1(z):
    series = -z * (1.0 + z * (0.5 + z * (1.0 / 6.0 + z * (1.0 / 24.0 + z * (1.0 / 120.0)))))
    return jnp.where(z > -0.02, series, 1.0 - jnp.exp(z))


def _scan_fwd(a, b, row):
    n = a.shape[0]
    d = 1
    while d < n:
        m = row >= d
        b = jnp.where(m, a * pltpu.roll(b, d, 0) + b, b)
        a = jnp.where(m, a * pltpu.roll(a, d, 0), a)
        d *= 2
    return a, b


def _scan_rev(a, b, row):
    n = a.shape[0]
    d = 1
    while d < n:
        m = row < n - d
        b = jnp.where(m, b + a * pltpu.roll(b, n - d, 0), b)
        a = jnp.where(m, a * pltpu.roll(a, n - d, 0), a)
        d *= 2
    return a, b


def _colsum(v):
    return jnp.sum(v, axis=0, keepdims=True)


def _lru_gates(xlb, gab, gbias, sp, first_row):
    g = _dot(xlb, gab) + gbias
    r = jax.nn.sigmoid(g[:, :D_LRU])
    ig = jax.nn.sigmoid(g[:, D_LRU:])
    la = (-C_GATE) * r * sp
    a = jnp.exp(la)
    msq = jnp.sqrt(_neg_expm1(2.0 * la))
    mult = jnp.where(first_row, 1.0, msq)
    return r, ig, a, msq, mult


def _mix_fwd(x, mod, vecd, vecl, win_g, gab, a64, wout_g):
    s = x.shape[0]
    ts = TOKEN_TILE
    nt = s // ts

    def body(x_ref, mod_ref, vd_ref, vl_ref, win_ref, gab_ref, a64_ref, wout_ref,
             hb_ref, proj_ref, hl_ref, ycat_ref, mixed_ref, x1_ref, ext_lx, ext_cv, hcar):
        i = pl.program_id(0)

        @pl.when(i == 0)
        def _():
            ext_lx[0:HALO, :] = jnp.zeros((HALO, D_LRU), F32)
            ext_cv[0:HALO, :] = jnp.zeros((HALO, D_LRU), F32)
            hcar[...] = jnp.zeros_like(hcar)

        row = lax.broadcasted_iota(jnp.int32, (ts, D_LRU), 0)
        first_row = jnp.logical_and(row == 0, i == 0)
        xt = x_ref[...]
        shift1, scale1, gate1 = mod_ref[0:1, :], mod_ref[1:2, :], mod_ref[2:3, :]
        r1 = lax.rsqrt(jnp.mean(xt * xt, axis=-1, keepdims=True) + EPS)
        h = (xt * r1) * vd_ref[0:1, :] * (1.0 + scale1) + shift1
        hb = h.astype(BF16)
        hb_ref[...] = hb
        for j in range(N_CHIP):
            proj_ref[:, j * WIN_BLK:(j + 1) * WIN_BLK] = _dot(hb, win_ref[j])
        u_ly = proj_ref[:, 512:1024]
        u_b = proj_ref[:, 1024:1536]

        ext_lx[HALO:HALO + ts, :] = proj_ref[:, 0:512]
        xl = vl_ref[4:5, :] + vl_ref[0:1, :] * ext_lx[pl.ds(5, ts), :]
        for k in range(1, 4):
            xl = xl + vl_ref[k:k + 1, :] * ext_lx[pl.ds(5 + k, ts), :]
        ext_lx[0:HALO, :] = ext_lx[ts:ts + HALO, :]
        sp = _softplus(vl_ref[8:9, :])
        _, ig, a, _, mult = _lru_gates(xl.astype(BF16), gab_ref[...], vd_ref[3:4, :], sp, first_row)
        acum, hloc = _scan_fwd(a, mult * (ig * xl), row)
        hl = hloc + acum * hcar[0:1, :]
        hl_ref[...] = hl
        hcar[0:1, :] = hl_ref[ts - 1:ts, :]
        ge, _ = _gelu(u_ly)
        p = ge * hl
        y_lru = p * lax.rsqrt(_gmean(p * p, a64_ref[...]) + EPS) * vl_ref[9:10, :]
        ycat_ref[:, 0:512] = y_lru.astype(BF16)

        ext_cv[HALO:HALO + ts, :] = proj_ref[:, 1536:2048] * proj_ref[:, 2048:2560]
        q = vl_ref[5:6, :] * ext_cv[pl.ds(6, ts), :]
        for k in range(1, 3):
            q = q + vl_ref[5 + k:6 + k, :] * ext_cv[pl.ds(6 + k, ts), :]
        ext_cv[0:HALO, :] = ext_cv[ts:ts + HALO, :]
        yc = u_b * q
        y_conv = yc * lax.rsqrt(_gmean(yc * yc, a64_ref[...]) + EPS) * vl_ref[10:11, :]
        ycat_ref[:, 512:1024] = y_conv.astype(BF16)

        mixed = _dot(ycat_ref[...], wout_ref[...])
        mixed_ref[...] = mixed
        x1_ref[...] = xt + gate1 * mixed

    tile = lambda w: pl.BlockSpec((ts, w), lambda i: (i, 0))
    return pl.pallas_call(
        body, name="mix_fwd", grid=(nt,),
        in_specs=[tile(D_MODEL), _full((8, D_MODEL)), _full((8, D_MODEL)), _full((16, D_LRU)),
                  _full((N_CHIP, D_MODEL, WIN_BLK), True), _full((D_LRU, 2 * D_LRU), True),
                  _full((D_LRU, D_LRU), True), _full((D_MODEL, D_MODEL), True)],
        out_specs=[tile(D_MODEL), tile(D_IN), tile(D_LRU), tile(D_MODEL), tile(D_MODEL), tile(D_MODEL)],
        out_shape=[jax.ShapeDtypeStruct((s, D_MODEL), BF16), jax.ShapeDtypeStruct((s, D_IN), F32),
                   jax.ShapeDtypeStruct((s, D_LRU), F32), jax.ShapeDtypeStruct((s, D_MODEL), BF16),
                   jax.ShapeDtypeStruct((s, D_MODEL), F32), jax.ShapeDtypeStruct((s, D_MODEL), F32)],
        scratch_shapes=[pltpu.VMEM((ts + HALO, D_LRU), F32), pltpu.VMEM((ts + HALO, D_LRU), F32),
                        pltpu.VMEM((HALO, D_LRU), F32)],
        compiler_params=pltpu.CompilerParams(dimension_semantics=("arbitrary",), vmem_limit_bytes=VMEM_LIMIT),
    )(x, mod, vecd, vecl, win_g, gab, a64, wout_g)


def _mlp_fwd_bwd(x1, target, mod, vecd, w1_g, w2_g):
    s = x1.shape[0]
    ts = TOKEN_TILE
    nt = s // ts

    def body(x1_ref, tg_ref, mod_ref, vd_ref, w1_hbm, w2_hbm,
             dx1_ref, act_ref, dz_ref, dmo_ref, h2_ref, acc_ref, w1_v, w2_v, rz_v, sem):
        i = pl.program_id(0)

        @pl.when(i == 0)
        def _():
            c1 = pltpu.make_async_copy(w1_hbm, w1_v, sem.at[0])
            c2 = pltpu.make_async_copy(w2_hbm, w2_v, sem.at[1])
            c1.start()
            c2.start()
            acc_ref[...] = jnp.zeros_like(acc_ref)
            c1.wait()
            c2.wait()

        xt = x1_ref[...]
        shift2, scale2, gate2 = mod_ref[3:4, :], mod_ref[4:5, :], mod_ref[5:6, :]
        g2, gf = vd_ref[1:2, :], vd_ref[2:3, :]
        r2 = lax.rsqrt(jnp.mean(xt * xt, axis=-1, keepdims=True) + EPS)
        n2 = xt * r2
        h2b = (n2 * g2 * (1.0 + scale2) + shift2).astype(BF16)
        h2_ref[...] = h2b
        mo = jnp.zeros((ts, D_MODEL), F32)
        for j in range(N_CHIP):
            rz = jnp.maximum(_dot(h2b, w1_v[j]), 0.0)
            rz_v[j] = rz
            actb = (rz * rz).astype(BF16)
            act_ref[:, j * FF_BLK:(j + 1) * FF_BLK] = actb
            mo = mo + _dot(actb, w2_v[j])
        x2 = xt + gate2 * mo
        r3 = lax.rsqrt(jnp.mean(x2 * x2, axis=-1, keepdims=True) + EPS)
        n3 = x2 * r3
        e = n3 * gf - tg_ref[...]
        loss = (0.5 / D_MODEL) * jnp.sum(_colsum(e * e), axis=1, keepdims=True)
        dy = e * (1.0 / D_MODEL)
        acc_ref[4:5, :] += _colsum(dy * n3)
        acc_ref[5:6, :] += jnp.broadcast_to(loss, (1, D_MODEL))
        dn3 = dy * gf
        dx2 = r3 * (dn3 - n3 * jnp.mean(dn3 * n3, axis=-1, keepdims=True))
        acc_ref[2:3, :] += _colsum(dx2 * mo)
        dmob = (dx2 * gate2).astype(BF16)
        dmo_ref[...] = dmob
        dh2 = jnp.zeros((ts, D_MODEL), F32)
        for j in range(N_CHIP):
            dzb = (_dot_nt(dmob, w2_v[j]) * (2.0 * rz_v[j])).astype(BF16)
            dz_ref[:, j * FF_BLK:(j + 1) * FF_BLK] = dzb
            dh2 = dh2 + _dot_nt(dzb, w1_v[j])
        acc_ref[1:2, :] += _colsum(dh2 * (n2 * g2))
        acc_ref[0:1, :] += _colsum(dh2)
        dhn2 = dh2 * (1.0 + scale2)
        acc_ref[3:4, :] += _colsum(dhn2 * n2)
        dn2 = dhn2 * g2
        dx1_ref[...] = dx2 + r2 * (dn2 - n2 * jnp.mean(dn2 * n2, axis=-1, keepdims=True))

    tile = lambda w: pl.BlockSpec((ts, w), lambda i: (i, 0))
    return pl.pallas_call(
        body, name="mlp_fwd_bwd", grid=(nt,),
        in_specs=[tile(D_MODEL), tile(D_MODEL), _full((8, D_MODEL)), _full((8, D_MODEL)), ANY, ANY],
        out_specs=[tile(D_MODEL), tile(D_FF), tile(D_FF), tile(D_MODEL), tile(D_MODEL), _full((8, D_MODEL))],
        out_shape=[jax.ShapeDtypeStruct((s, D_MODEL), F32), jax.ShapeDtypeStruct((s, D_FF), BF16),
                   jax.ShapeDtypeStruct((s, D_FF), BF16), jax.ShapeDtypeStruct((s, D_MODEL), BF16),
                   jax.ShapeDtypeStruct((s, D_MODEL), BF16), jax.ShapeDtypeStruct((8, D_MODEL), F32)],
        scratch_shapes=[pltpu.VMEM((N_CHIP, D_MODEL, FF_BLK), BF16), pltpu.VMEM((N_CHIP, FF_BLK, D_MODEL), BF16),
                        pltpu.VMEM((N_CHIP, ts, FF_BLK), F32), pltpu.SemaphoreType.DMA((2,))],
        compiler_params=pltpu.CompilerParams(dimension_semantics=("arbitrary",), vmem_limit_bytes=VMEM_LIMIT),
    )(x1, target, mod, vecd, w1_g, w2_g)


def _mix_bwd(dx1, x, mixed, proj, hl, mod, vecd, vecl, win_g, gab, a64, wout_g):
    s = x.shape[0]
    ts = TOKEN_TILE
    nt = s // ts
    hpt = ts // HALO

    def body(dx1_ref, x_ref, mixed_ref, proj_ref, projh_ref, hl_ref, hlh_ref, mod_ref, vd_ref, vl_ref,
             win_ref, gab_ref, a64_ref, wout_ref,
             gx_ref, dproj_ref, dmixed_ref, xlb_ref, dgb_ref, accd_ref, accl_ref,
             ext_lx, ext_cv, ext_hl, ext_dxl, ext_dq, gbuf, gcar, acar):
        i = pl.program_id(0)
        ri = nt - 1 - i

        @pl.when(i == 0)
        def _():
            accd_ref[...] = jnp.zeros_like(accd_ref)
            accl_ref[...] = jnp.zeros_like(accl_ref)
            ext_dxl[ts:ts + HALO, :] = jnp.zeros((HALO, D_LRU), F32)
            ext_dq[ts:ts + HALO, :] = jnp.zeros((HALO, D_LRU), F32)
            gcar[...] = jnp.zeros_like(gcar)
            acar[...] = jnp.zeros_like(acar)

        row = lax.broadcasted_iota(jnp.int32, (ts, D_LRU), 0)
        first_row = jnp.logical_and(row == 0, ri == 0)
        halo_on = jnp.where(ri == 0, 0.0, 1.0)
        shift1, scale1, gate1 = mod_ref[0:1, :], mod_ref[1:2, :], mod_ref[2:3, :]
        g1 = vd_ref[0:1, :]
        a64m = a64_ref[...]
        lg, cg = vl_ref[9:10, :], vl_ref[10:11, :]

        dx1 = dx1_ref[...]
        accd_ref[2:3, :] += _colsum(dx1 * mixed_ref[...])
        dmb = (dx1 * gate1).astype(BF16)
        dmixed_ref[...] = dmb
        dycat = _dot_nt(dmb, wout_ref[...])
        dyl = dycat[:, 0:512]
        dyv = dycat[:, 512:1024]

        u_ly = proj_ref[:, 512:1024]
        u_b = proj_ref[:, 1024:1536]
        u_c = proj_ref[:, 1536:2048]
        u_v = proj_ref[:, 2048:2560]
        ext_lx[0:HALO, :] = projh_ref[:, 0:512] * halo_on
        ext_lx[HALO:HALO + ts, :] = proj_ref[:, 0:512]
        xl = vl_ref[4:5, :] + vl_ref[0:1, :] * ext_lx[pl.ds(5, ts), :]
        for k in range(1, 4):
            xl = xl + vl_ref[k:k + 1, :] * ext_lx[pl.ds(5 + k, ts), :]
        xlb = xl.astype(BF16)
        xlb_ref[...] = xlb
        sp = _softplus(vl_ref[8:9, :])
        r, ig, a, msq, mult = _lru_gates(xlb, gab_ref[...], vd_ref[3:4, :], sp, first_row)
        hl = hl_ref[...]
        ge, th = _gelu(u_ly)
        p = ge * hl
        rl = lax.rsqrt(_gmean(p * p, a64m) + EPS)
        nl = p * rl
        ext_cv[0:HALO, :] = projh_ref[:, 1536:2048] * projh_ref[:, 2048:2560] * halo_on
        ext_cv[HALO:HALO + ts, :] = u_c * u_v
        q = vl_ref[5:6, :] * ext_cv[pl.ds(6, ts), :]
        for k in range(1, 3):
            q = q + vl_ref[5 + k:6 + k, :] * ext_cv[pl.ds(6 + k, ts), :]
        yc = u_b * q
        rc = lax.rsqrt(_gmean(yc * yc, a64m) + EPS)
        nc = yc * rc

        accl_ref[9:10, :] += _colsum(dyl * nl)
        dnl = dyl * lg
        dp = rl * (dnl - nl * _gmean(dnl * nl, a64m))
        dproj_ref[:, 512:1024] = ((dp * hl) * _gelu_grad(u_ly, th)).astype(BF16)
        a_next = jnp.where(row == ts - 1, acar[0:1, :], pltpu.roll(a, ts - 1, 0))
        acum, gloc = _scan_rev(a_next, dp * ge, row)
        gbuf[...] = gloc + acum * gcar[0:1, :]
        gcar[0:1, :] = gbuf[0:1, :]
        ext_hl[0:HALO, :] = hlh_ref[...] * halo_on
        ext_hl[HALO:HALO + ts, :] = hl
        acar[...] = a[0:HALO, :]
        gt = gbuf[...]
        da = gt * ext_hl[pl.ds(HALO - 1, ts), :]
        dmult = gt * ig * xl
        di = gt * mult * xl
        dxl = gt * mult * ig
        dla = da * a - jnp.where(first_row, 0.0, dmult * a * a / msq)
        accl_ref[8:9, :] += _colsum(dla * ((-C_GATE) * r))
        dra = dla * ((-C_GATE) * sp) * r * (1.0 - r)
        dia = di * ig * (1.0 - ig)
        accd_ref[4:5, 0:D_LRU] += _colsum(dra)
        accd_ref[4:5, D_LRU:2 * D_LRU] += _colsum(dia)
        dgb_ref[:, 0:D_LRU] = dra.astype(BF16)
        dgb_ref[:, D_LRU:2 * D_LRU] = dia.astype(BF16)
        dxl = dxl + _dot_nt(dgb_ref[...], gab_ref[...])
        accl_ref[4:5, :] += _colsum(dxl)
        for k in range(4):
            accl_ref[k:k + 1, :] += _colsum(dxl * ext_lx[pl.ds(5 + k, ts), :])
        ext_dxl[0:ts, :] = dxl
        du_lx = vl_ref[0:1, :] * ext_dxl[pl.ds(3, ts), :]
        for k in range(1, 4):
            du_lx = du_lx + vl_ref[k:k + 1, :] * ext_dxl[pl.ds(3 - k, ts), :]
        ext_dxl[ts:ts + HALO, :] = ext_dxl[0:HALO, :]
        dproj_ref[:, 0:512] = du_lx.astype(BF16)

        accl_ref[10:11, :] += _colsum(dyv * nc)
        dnc = dyv * cg
        dyc = rc * (dnc - nc * _gmean(dnc * nc, a64m))
        dproj_ref[:, 1024:1536] = (dyc * q).astype(BF16)
        dq = dyc * u_b
        for k in range(3):
            accl_ref[5 + k:6 + k, :] += _colsum(dq * ext_cv[pl.ds(6 + k, ts), :])
        ext_dq[0:ts, :] = dq
        dcv = vl_ref[5:6, :] * ext_dq[pl.ds(2, ts), :]
        for k in range(1, 3):
            dcv = dcv + vl_ref[5 + k:6 + k, :] * ext_dq[pl.ds(2 - k, ts), :]
        ext_dq[ts:ts + HALO, :] = ext_dq[0:HALO, :]
        dproj_ref[:, 1536:2048] = (dcv * u_v).astype(BF16)
        dproj_ref[:, 2048:2560] = (dcv * u_c).astype(BF16)

        dh = _dot_nt(dproj_ref[:, 0:WIN_BLK], win_ref[0])
        for j in range(1, N_CHIP):
            dh = dh + _dot_nt(dproj_ref[:, j * WIN_BLK:(j + 1) * WIN_BLK], win_ref[j])
        xt = x_ref[...]
        r1 = lax.rsqrt(jnp.mean(xt * xt, axis=-1, keepdims=True) + EPS)
        n1 = xt * r1
        accd_ref[1:2, :] += _colsum(dh * (n1 * g1))
        accd_ref[0:1, :] += _colsum(dh)
        dhn1 = dh * (1.0 + scale1)
        accd_ref[3:4, :] += _colsum(dhn1 * n1)
        dn1 = dhn1 * g1
        gx_ref[...] = dx1 + r1 * (dn1 - n1 * jnp.mean(dn1 * n1, axis=-1, keepdims=True))

    tile = lambda w: pl.BlockSpec((ts, w), lambda i: (nt - 1 - i, 0))
    halo = lambda w: pl.BlockSpec((HALO, w), lambda i: (jnp.maximum((nt - 1 - i) * hpt - 1, 0), 0))
    ext = pltpu.VMEM((ts + HALO, D_LRU), F32)
    return pl.pallas_call(
        body, name="mix_bwd", grid=(nt,),
        in_specs=[tile(D_MODEL), tile(D_MODEL), tile(D_MODEL), tile(D_IN), halo(D_IN), tile(D_LRU), halo(D_LRU),
                  _full((8, D_MODEL)), _full((8, D_MODEL)), _full((16, D_LRU)),
                  _full((N_CHIP, D_MODEL, WIN_BLK), True), _full((D_LRU, 2 * D_LRU), True),
                  _full((D_LRU, D_LRU), True), _full((D_MODEL, D_MODEL), True)],
        out_specs=[tile(D_MODEL), tile(D_IN), tile(D_MODEL), tile(D_LRU), tile(2 * D_LRU),
                   _full((8, D_MODEL)), _full((16, D_LRU))],
        out_shape=[jax.ShapeDtypeStruct((s, D_MODEL), F32), jax.ShapeDtypeStruct((s, D_IN), BF16),
                   jax.ShapeDtypeStruct((s, D_MODEL), BF16), jax.ShapeDtypeStruct((s, D_LRU), BF16),
                   jax.ShapeDtypeStruct((s, 2 * D_LRU), BF16),
                   jax.ShapeDtypeStruct((8, D_MODEL), F32), jax.ShapeDtypeStruct((16, D_LRU), F32)],
        scratch_shapes=[ext, ext, ext, ext, ext, pltpu.VMEM((ts, D_LRU), F32),
                        pltpu.VMEM((HALO, D_LRU), F32), pltpu.VMEM((HALO, D_LRU), F32)],
        compiler_params=pltpu.CompilerParams(dimension_semantics=("arbitrary",), vmem_limit_bytes=VMEM_LIMIT),
    )(dx1, x, mixed, proj, proj, hl, hl, mod, vecd, vecl, win_g, gab, a64, wout_g)


def _wgrad(name, a, b, a_blk, b_blk, out_dtype):
    s = a.shape[0]
    aw = a_blk or a.shape[1]
    bw = b_blk or b.shape[1]
    nblk = N_CHIP if (a_blk or b_blk) else 1

    def body(a_ref, b_ref, o_ref):
        o_ref[0] = _dot_tn(a_ref[...], b_ref[...]).astype(out_dtype)

    return pl.pallas_call(
        body, name=name, grid=(nblk,),
        in_specs=[pl.BlockSpec((s, aw), (lambda j: (0, j)) if a_blk else (lambda j: (0, 0))),
                  pl.BlockSpec((s, bw), (lambda j: (0, j)) if b_blk else (lambda j: (0, 0)))],
        out_specs=pl.BlockSpec((1, aw, bw), lambda j: (j, 0, 0)),
        out_shape=jax.ShapeDtypeStruct((nblk, aw, bw), out_dtype),
        compiler_params=pltpu.CompilerParams(dimension_semantics=("arbitrary",), vmem_limit_bytes=VMEM_LIMIT),
    )(a, b)


def _mod_matmul(c_all, ada_w_loc):
    n = ada_w_loc.shape[1]
    cb = 512

    def body(c_ref, w_ref, o_ref):
        c = c_ref[...]
        sc = c * jax.nn.sigmoid(c)
        o_ref[...] = _dot(sc.astype(BF16), w_ref[...].astype(BF16))

    return pl.pallas_call(
        body, name="mod_matmul", grid=(n // cb,),
        in_specs=[_full((8, D_MODEL)), pl.BlockSpec((D_MODEL, cb), lambda j: (0, j))],
        out_specs=pl.BlockSpec((8, cb), lambda j: (0, j)),
        out_shape=jax.ShapeDtypeStruct((8, n), F32),
        compiler_params=pltpu.CompilerParams(dimension_semantics=("arbitrary",), vmem_limit_bytes=VMEM_LIMIT),
    )(c_all, ada_w_loc)


def _adam_math(w, g, m, v):
    m = ADAM_B1 * m + (1.0 - ADAM_B1) * g
    v = ADAM_B2 * v + (1.0 - ADAM_B2) * (g * g)
    m_hat = m / (1.0 - ADAM_B1 ** ADAM_STEP)
    v_hat = v / (1.0 - ADAM_B2 ** ADAM_STEP)
    delta = (-ADAM_LR) * (m_hat / (jnp.sqrt(v_hat) + ADAM_EPS) + ADAM_WD * w)
    return delta, m, v


def _adam(name, w, g, m, v):
    r, c = w.shape
    rb = 256 if r % 256 == 0 else r

    def body(w_ref, g_ref, m_ref, v_ref, d_ref, mo_ref, vo_ref):
        d_ref[...], mo_ref[...], vo_ref[...] = _adam_math(w_ref[...], g_ref[...], m_ref[...], v_ref[...])

    spec = pl.BlockSpec((rb, c), lambda i: (i, 0))
    sds = jax.ShapeDtypeStruct((r, c), F32)
    return pl.pallas_call(
        body, name=name, grid=(r // rb,), in_specs=[spec] * 4, out_specs=[spec] * 3, out_shape=[sds] * 3,
        compiler_params=pltpu.CompilerParams(dimension_semantics=("arbitrary",), vmem_limit_bytes=VMEM_LIMIT),
    )(w, g, m, v)


def _ada_grad_adam(sct, dmod_loc, w, m, v):
    r, c = w.shape
    rb = 128

    def body(s_ref, dm_ref, w_ref, m_ref, v_ref, g_ref, d_ref, mo_ref, vo_ref):
        g = s_ref[:, 0:1] * dm_ref[0:1, :]
        for b in range(1, 8):
            g = g + s_ref[:, b:b + 1] * dm_ref[b:b + 1, :]
        g_ref[...] = g
        d_ref[...], mo_ref[...], vo_ref[...] = _adam_math(w_ref[...], g, m_ref[...], v_ref[...])

    spec = pl.BlockSpec((rb, c), lambda i: (i, 0))
    sds = jax.ShapeDtypeStruct((r, c), F32)
    return pl.pallas_call(
        body, name="ada_grad_adam", grid=(r // rb,),
        in_specs=[pl.BlockSpec((rb, 8), lambda i: (i, 0)), _full((8, c)), spec, spec, spec],
        out_specs=[spec] * 4, out_shape=[sds] * 4,
        compiler_params=pltpu.CompilerParams(dimension_semantics=("arbitrary",), vmem_limit_bytes=VMEM_LIMIT),
    )(sct, dmod_loc, w, m, v)


def _position():
    x, y, c = lax.axis_index("x"), lax.axis_index("y"), lax.axis_index("c")
    chips = [(1 - x, y), (x, 1 - y), (1 - x, 1 - y)]
    return x, y, c, chips


def _allgather8(name, arrs):
    na = len(arrs)

    def body(*refs):
        ins, outs = refs[:na], refs[na:2 * na]
        send_sems, recv_sems, local_sems = refs[2 * na:]
        x, y, c, chips = _position()
        me, sibling = (x, y, c), (x, y, 1 - c)
        first, passed, local = [], [], []
        for a in range(na):
            m_per = ins[a].shape[0]

            def rows(px, py, pc, a=a, m_per=m_per):
                return outs[a].at[pl.ds((4 * px + 2 * py + pc) * m_per, m_per), :]

            def copy(k, block, to, src=None, a=a, rows=rows):
                return pltpu.make_async_remote_copy(
                    src_ref=rows(*block) if src is None else src, dst_ref=rows(*block),
                    send_sem=send_sems.at[7 * a + k], recv_sem=recv_sems.at[7 * a + k],
                    device_id=to, device_id_type=MESH)

            mine = pltpu.make_async_copy(ins[a], rows(*me), local_sems.at[a])
            mine.start()
            local.append(mine)
            f = [copy(0, me, sibling, src=ins[a])]
            f += [copy(1 + j, me, (*chip, c), src=ins[a]) for j, chip in enumerate(chips)]
            for cp in f:
                cp.start()
            first.append((f, copy))
        for a in range(na):
            f, copy = first[a]
            p = [copy(4 + j, (*chip, c), sibling) for j, chip in enumerate(chips)]
            for j, chip in enumerate(chips):
                copy(1 + j, (*chip, c), me).wait_recv()
                p[j].start()
            passed.append(p)
        for a in range(na):
            f, copy = first[a]
            copy(0, sibling, me).wait_recv()
            for j, chip in enumerate(chips):
                copy(4 + j, (*chip, 1 - c), me).wait_recv()
            for cp in f + passed[a]:
                cp.wait_send()
            local[a].wait()

    return pl.pallas_call(
        body, name=name,
        out_shape=[jax.ShapeDtypeStruct((8 * a.shape[0], a.shape[1]), a.dtype) for a in arrs],
        in_specs=[VMEM] * na, out_specs=[VMEM] * na,
        scratch_shapes=[pltpu.SemaphoreType.DMA((7 * na,)), pltpu.SemaphoreType.DMA((7 * na,)),
                        pltpu.SemaphoreType.DMA((na,))],
        compiler_params=pltpu.CompilerParams(vmem_limit_bytes=VMEM_LIMIT),
    )(*arrs)


def _allgather_weights(shards):
    na = len(shards)

    def body(*refs):
        ins, outs = refs[:na], refs[na:2 * na]
        send_sems, recv_sems, local_sems = refs[2 * na:]
        x, y, c, chips = _position()
        sibling = (x, y, 1 - c)
        mychip = 2 * x + y
        first, passed, local = [], [], []
        for a in range(na):
            half = ins[a].shape[0] // 2

            def part(chip, pc, a=a, half=half):
                return outs[a].at[chip, pl.ds(pc * half, half), :]

            def copy(k, chip, pc, to, src=None, a=a, part=part):
                return pltpu.make_async_remote_copy(
                    src_ref=part(chip, pc) if src is None else src, dst_ref=part(chip, pc),
                    send_sem=send_sems.at[6 * a + k], recv_sem=recv_sems.at[6 * a + k],
                    device_id=to, device_id_type=MESH)

            mine = pltpu.make_async_copy(ins[a], outs[a].at[mychip], local_sems.at[a])
            mine.start()
            local.append(mine)
            src = ins[a].at[pl.ds(c * half, half), :]
            f = [copy(j, mychip, c, (*chip, c), src=src) for j, chip in enumerate(chips)]
            for cp in f:
                cp.start()
            first.append((f, copy))
        for a in range(na):
            f, copy = first[a]
            p = [copy(3 + j, 2 * chip[0] + chip[1], c, sibling) for j, chip in enumerate(chips)]
            for j, chip in enumerate(chips):
                copy(j, 2 * chip[0] + chip[1], c, sibling).wait_recv()
                p[j].start()
            passed.append(p)
        for a in range(na):
            f, copy = first[a]
            for j, chip in enumerate(chips):
                copy(3 + j, 2 * chip[0] + chip[1], 1 - c, sibling).wait_recv()
            for cp in f + passed[a]:
                cp.wait_send()
            local[a].wait()

    return pl.pallas_call(
        body, name="allgather_weights",
        out_shape=[jax.ShapeDtypeStruct((N_CHIP,) + s.shape, s.dtype) for s in shards],
        in_specs=[ANY] * na, out_specs=[ANY] * na,
        scratch_shapes=[pltpu.SemaphoreType.DMA((6 * na,)), pltpu.SemaphoreType.DMA((6 * na,)),
                        pltpu.SemaphoreType.DMA((na,))],
    )(*shards)


def _swap_halves(grads):
    na = len(grads)

    def body(*refs):
        ins, outs = refs[:na], refs[na:2 * na]
        send_sems, recv_sems = refs[2 * na:]
        x, y, c, _ = _position()
        cps = []
        for a in range(na):
            half = ins[a].shape[1] // 2
            cp = pltpu.make_async_remote_copy(
                src_ref=ins[a].at[:, pl.ds((1 - c) * half, half), :], dst_ref=outs[a],
                send_sem=send_sems.at[a], recv_sem=recv_sems.at[a],
                device_id=(x, y, 1 - c), device_id_type=MESH)
            cp.start()
            cps.append(cp)
        for cp in cps:
            cp.wait()

    return pl.pallas_call(
        body, name="rs_swap_halves",
        out_shape=[jax.ShapeDtypeStruct((g.shape[0], g.shape[1] // 2, g.shape[2]), g.dtype) for g in grads],
        in_specs=[ANY] * na, out_specs=[ANY] * na,
        scratch_shapes=[pltpu.SemaphoreType.DMA((na,)), pltpu.SemaphoreType.DMA((na,))],
    )(*grads)


def _exchange_chips(parts):
    na = len(parts)

    def body(*refs):
        ins, outs = refs[:na], refs[na:2 * na]
        send_sems, recv_sems, local_sems = refs[2 * na:]
        x, y, c, chips = _position()
        mychip = 2 * x + y
        cps, local = [], []
        for a in range(na):
            mine = pltpu.make_async_copy(ins[a].at[mychip], outs[a].at[mychip], local_sems.at[a])
            mine.start()
            local.append(mine)
            for j, chip in enumerate(chips):
                cp = pltpu.make_async_remote_copy(
                    src_ref=ins[a].at[2 * chip[0] + chip[1]], dst_ref=outs[a].at[mychip],
                    send_sem=send_sems.at[3 * a + j], recv_sem=recv_sems.at[3 * a + j],
                    device_id=(*chip, c), device_id_type=MESH)
                cp.start()
                cps.append(cp)
        for a in range(na):
            for j, chip in enumerate(chips):
                pltpu.make_async_remote_copy(
                    src_ref=ins[a].at[mychip], dst_ref=outs[a].at[2 * chip[0] + chip[1]],
                    send_sem=send_sems.at[3 * a + j], recv_sem=recv_sems.at[3 * a + j],
                    device_id=(*chip, c), device_id_type=MESH).wait_recv()
        for cp in cps:
            cp.wait_send()
        for mine in local:
            mine.wait()

    return pl.pallas_call(
        body, name="rs_exchange_chips",
        out_shape=[jax.ShapeDtypeStruct(p.shape, p.dtype) for p in parts],
        in_specs=[ANY] * na, out_specs=[ANY] * na,
        scratch_shapes=[pltpu.SemaphoreType.DMA((3 * na,)), pltpu.SemaphoreType.DMA((3 * na,)),
                        pltpu.SemaphoreType.DMA((na,))],
    )(*parts)


def _join_halves(halves):
    na = len(halves)

    def body(*refs):
        ins, outs = refs[:na], refs[na:2 * na]
        send_sems, recv_sems, local_sems = refs[2 * na:]
        x, y, c, _ = _position()
        cps, local = [], []
        for a in range(na):
            half = ins[a].shape[0]
            mine = pltpu.make_async_copy(ins[a], outs[a].at[pl.ds(c * half, half), :], local_sems.at[a])
            mine.start()
            local.append(mine)
            cp = pltpu.make_async_remote_copy(
                src_ref=ins[a], dst_ref=outs[a].at[pl.ds(c * half, half), :],
                send_sem=send_sems.at[a], recv_sem=recv_sems.at[a],
                device_id=(x, y, 1 - c), device_id_type=MESH)
            cp.start()
            cps.append(cp)
        for a in range(na):
            half = ins[a].shape[0]
            pltpu.make_async_remote_copy(
                src_ref=ins[a], dst_ref=outs[a].at[pl.ds((1 - c) * half, half), :],
                send_sem=send_sems.at[a], recv_sem=recv_sems.at[a],
                device_id=(x, y, 1 - c), device_id_type=MESH).wait_recv()
        for cp in cps:
            cp.wait_send()
        for mine in local:
            mine.wait()

    return pl.pallas_call(
        body, name="rs_join_halves",
        out_shape=[jax.ShapeDtypeStruct((2 * h.shape[0], h.shape[1]), h.dtype) for h in halves],
        in_specs=[ANY] * na, out_specs=[ANY] * na,
        scratch_shapes=[pltpu.SemaphoreType.DMA((na,)), pltpu.SemaphoreType.DMA((na,)),
                        pltpu.SemaphoreType.DMA((na,))],
    )(*halves)


def _add_sibling(name, grad, recv, core):
    _, r, c = grad.shape
    half = r // 2
    rb = min(half, 256)
    nrb = half // rb

    def body(core_ref, g_ref, r_ref, o_ref):
        o_ref[...] = (g_ref[...].astype(F32) + r_ref[...].astype(F32)).astype(BF16)

    return pl.pallas_call(
        body, name=name,
        grid_spec=pltpu.PrefetchScalarGridSpec(
            num_scalar_prefetch=1, grid=(N_CHIP, nrb),
            in_specs=[pl.BlockSpec((1, rb, c), lambda j, i, core_ref: (j, core_ref[0] * nrb + i, 0)),
                      pl.BlockSpec((1, rb, c), lambda j, i, core_ref: (j, i, 0))],
            out_specs=pl.BlockSpec((1, rb, c), lambda j, i, core_ref: (j, i, 0))),
        out_shape=jax.ShapeDtypeStruct((N_CHIP, half, c), BF16),
        compiler_params=pltpu.CompilerParams(dimension_semantics=("arbitrary", "arbitrary"),
                                             vmem_limit_bytes=VMEM_LIMIT),
    )(core, grad, recv)


def _add_chips(name, q):
    _, half, c = q.shape
    rb = min(half, 256)

    def body(q_ref, o_ref):
        acc = q_ref[0].astype(F32)
        for j in range(1, N_CHIP):
            acc = acc + q_ref[j].astype(F32)
        o_ref[...] = acc

    return pl.pallas_call(
        body, name=name, grid=(half // rb,),
        in_specs=[pl.BlockSpec((N_CHIP, rb, c), lambda i: (0, i, 0))],
        out_specs=pl.BlockSpec((rb, c), lambda i: (i, 0)),
        out_shape=jax.ShapeDtypeStruct((half, c), F32),
        compiler_params=pltpu.CompilerParams(dimension_semantics=("arbitrary",), vmem_limit_bytes=VMEM_LIMIT),
    )(q)


def _small_update(gad, gam, gl, gga, ggx, mychip, params):
    names = ["ada_b", "norm1_g", "lru_conv_b", "gate_a_w", "gate_a_b", "gate_x_w", "gate_x_b", "a_param",
             "lru_conv_w", "short_conv_w", "lru_out_g", "conv_out_g", "norm2_g", "final_g"]
    flat = [t for n in names for t in params[n]]
    nin = len(flat)

    def body(chip_ref, gad_ref, gam_ref, gl_ref, gga_ref, ggx_ref, *refs):
        ins = {n: refs[3 * k:3 * k + 3] for k, n in enumerate(names)}
        outs = {n: refs[nin + 4 * k:nin + 4 * k + 4] for k, n in enumerate(names)}
        loss_ref, dmod_ref = refs[nin + 4 * len(names):nin + 4 * len(names) + 2]

        def dsum(ref, lo, n):
            acc = ref[lo:lo + n, :]
            for dev in range(1, 8):
                acc = acc + ref[dev * (ref.shape[0] // 8) + lo:dev * (ref.shape[0] // 8) + lo + n, :]
            return acc

        def update(n, g):
            w_ref, m_ref, v_ref = ins[n]
            g_ref, d_ref, mo_ref, vo_ref = outs[n]
            g_ref[...] = g
            d_ref[...], mo_ref[...], vo_ref[...] = _adam_math(w_ref[...], g, m_ref[...], v_ref[...])

        d, dm, l, lw = refs[-4:]
        d[...] = dsum(gad_ref, 0, 8)
        dm[...] = dsum(gam_ref, 0, 8)
        l[...] = dsum(gl_ref, 0, 16)
        for dev in range(8):
            for k in range(3):
                dmod_ref[dev:dev + 1, k * D_MODEL:(k + 1) * D_MODEL] = gad_ref[dev * 8 + k:dev * 8 + k + 1, :]
                dmod_ref[dev:dev + 1, (3 + k) * D_MODEL:(4 + k) * D_MODEL] = gam_ref[dev * 8 + k:dev * 8 + k + 1, :]
        w_ref, m_ref, v_ref = ins["ada_b"]
        g_ref, d_ref, mo_ref, vo_ref = outs["ada_b"]
        for k in range(3):
            g_ref[:, k * D_MODEL:(k + 1) * D_MODEL] = d[k:k + 1, :]
            g_ref[:, (3 + k) * D_MODEL:(4 + k) * D_MODEL] = dm[k:k + 1, :]
        d_ref[...], mo_ref[...], vo_ref[...] = _adam_math(w_ref[...], g_ref[...], m_ref[...], v_ref[...])
        update("norm1_g", d[3:4, :])
        update("norm2_g", dm[3:4, :])
        update("final_g", dm[4:5, :])
        update("gate_a_b", d[4:5, 0:D_LRU])
        update("gate_x_b", d[4:5, D_LRU:2 * D_LRU])
        update("lru_conv_b", l[4:5, :])
        update("a_param", l[8:9, :] * jax.nn.sigmoid(ins["a_param"][0][...]))
        update("lru_out_g", l[9:10, :])
        update("conv_out_g", l[10:11, :])
        loss_ref[...] = jnp.broadcast_to(dm[5:6, 0:128], (8, 128))
        chip = chip_ref[0]
        acc = jnp.zeros((8, 128), F32)
        for j in range(N_CHIP):
            acc = acc + jnp.where(chip == j, l[0:8, j * 128:(j + 1) * 128], 0.0)
        lw[...] = acc
        update("lru_conv_w", lw[0:4, :])
        update("short_conv_w", lw[5:8, :])
        update("gate_a_w", dsum(gga_ref, 0, D_LRU))
        update("gate_x_w", dsum(ggx_ref, 0, D_LRU))

    out_shape = []
    for n in names:
        out_shape += [jax.ShapeDtypeStruct(params[n][0].shape, F32)] * 4
    out_shape += [jax.ShapeDtypeStruct((8, 128), F32), jax.ShapeDtypeStruct((8, 6 * D_MODEL), F32)]
    res = pl.pallas_call(
        body, name="small_update", out_shape=out_shape,
        in_specs=[pl.BlockSpec(memory_space=pltpu.SMEM)] + [VMEM] * (5 + nin),
        out_specs=[VMEM] * len(out_shape),
        scratch_shapes=[pltpu.VMEM((8, D_MODEL), F32), pltpu.VMEM((8, D_MODEL), F32), pltpu.VMEM((16, D_LRU), F32),
                        pltpu.VMEM((8, 128), F32)],
        compiler_params=pltpu.CompilerParams(vmem_limit_bytes=VMEM_LIMIT),
    )(mychip, gad, gam, gl, gga, ggx, *flat)
    per = {n: res[4 * k:4 * k + 4] for k, n in enumerate(names)}
    return per, res[-2], res[-1]


def _local_step(x, target, mod, vecd, vecl, gab, a64, win_g, wout_g, w1_g, w2_g):
    hb, proj, hl, ycat, mixed, x1 = _mix_fwd(x, mod, vecd, vecl, win_g, gab, a64, wout_g)
    dx1, act, dz, dmo, h2b, accm = _mlp_fwd_bwd(x1, target, mod, vecd, w1_g, w2_g)
    grad_x, dproj, dmixed, xlb, dgb, accd, accl = _mix_bwd(
        dx1, x, mixed, proj, hl, mod, vecd, vecl, win_g, gab, a64, wout_g)
    g_win = _wgrad("wgrad_in", hb, dproj, 0, WIN_BLK, BF16)
    g_wout = _wgrad("wgrad_out", ycat, dmixed, WOUT_BLK, 0, BF16)
    g_w1 = _wgrad("wgrad_mlp1", h2b, dz, 0, FF_BLK, BF16)
    g_w2 = _wgrad("wgrad_mlp2", act, dmo, FF_BLK, 0, BF16)
    g_gate = _wgrad("wgrad_gate", xlb, dgb, 0, 0, F32)[0]
    return grad_x, (g_win, g_wout, g_w1, g_w2), g_gate, accm, accd, accl


def _block_diag(w):
    eye = jnp.eye(8, dtype=w.dtype)
    return (eye[:, None, :, None] * w[:, :, None, :]).reshape(8 * HEAD, 8 * HEAD)


def _diag_blocks(g):
    return jnp.concatenate([g[h * HEAD:(h + 1) * HEAD, h * HEAD:(h + 1) * HEAD] for h in range(8)], axis=0)


def kernel(x, c, ada_w, ada_b, norm1_g, w_in, lru_conv_w, lru_conv_b, gate_a_w, gate_a_b, gate_x_w, gate_x_b, a_param, short_conv_w, lru_out_g, conv_out_g, w_out, norm2_g, w_mlp1, w_mlp2, final_g, loss_target, m_ada_w, m_ada_b, m_norm1_g, m_w_in, m_lru_conv_w, m_lru_conv_b, m_gate_a_w, m_gate_a_b, m_gate_x_w, m_gate_x_b, m_a_param, m_short_conv_w, m_lru_out_g, m_conv_out_g, m_w_out, m_norm2_g, m_w_mlp1, m_w_mlp2, m_final_g, v_ada_w, v_ada_b, v_norm1_g, v_w_in, v_lru_conv_w, v_lru_conv_b, v_gate_a_w, v_gate_a_b, v_gate_x_w, v_gate_x_b, v_a_param, v_short_conv_w, v_lru_out_g, v_conv_out_g, v_w_out, v_norm2_g, v_w_mlp1, v_w_mlp2, v_final_g):
    xi, yi, ci = lax.axis_index("x"), lax.axis_index("y"), lax.axis_index("c")
    mychip = 2 * xi + yi
    me = 4 * xi + 2 * yi + ci

    c_blk = jnp.zeros((8, D_MODEL), F32).at[0:1].set(c)
    cw_blk = jnp.zeros((8, 128), F32).at[0:4].set(lru_conv_w[0]).at[4:7].set(short_conv_w[0])
    c_g, cw_g = _allgather8("allgather_cond", [c_blk, cw_blk])
    c_all = c_g.reshape(8, 8, D_MODEL)[:, 0]
    cw_g = cw_g.reshape(4, 2, 8, 128)[:, 0]
    lcw = cw_g[:, 0:4].transpose(1, 0, 2).reshape(4, D_LRU)
    scw = cw_g[:, 4:7].transpose(1, 0, 2).reshape(3, D_LRU)

    mod_loc = _mod_matmul(c_all, ada_w[0])
    (mod_g,) = _allgather8("allgather_mod", [mod_loc])
    mod_all = mod_g.reshape(4, 2, 8, 6 * D_MODEL // 4)[:, 0].transpose(1, 0, 2).reshape(8, 6 * D_MODEL) + ada_b
    mod_pad = jnp.pad(mod_all.reshape(8, 6, D_MODEL), ((0, 0), (0, 2), (0, 0)))
    mod = lax.dynamic_slice_in_dim(mod_pad, me, 1, axis=0).reshape(8, D_MODEL)

    win_g, wout_g, w1_g, w2_g = _allgather_weights(
        [w_in[0].astype(BF16), w_out[0].astype(BF16), w_mlp1[0].astype(BF16), w_mlp2[0].astype(BF16)])

    vecd = jnp.concatenate([norm1_g, norm2_g, final_g[None, :], jnp.concatenate([gate_a_b, gate_x_b], axis=1),
                            jnp.zeros((4, D_MODEL), F32)], axis=0)
    vecl = jnp.concatenate([lcw, lru_conv_b, scw, a_param, lru_out_g, conv_out_g, jnp.zeros((5, D_LRU), F32)], axis=0)
    gab = jnp.concatenate([_block_diag(gate_a_w[0]), _block_diag(gate_x_w[0])], axis=1).astype(BF16)
    a64 = _block_diag(jnp.full((8, HEAD, HEAD), 1.0 / HEAD, F32)).astype(BF16)

    grad_x, big, g_gate, accm, accd, accl = _local_step(
        x[0], loss_target[0], mod, vecd, vecl, gab, a64, win_g, wout_g.reshape(D_MODEL, D_MODEL), w1_g, w2_g)

    core = ci.reshape(1).astype(jnp.int32)
    recv = _swap_halves(list(big))
    parts = [_add_sibling("rs_add_sibling_%d" % k, g, r, core) for k, (g, r) in enumerate(zip(big, recv))]
    gathered = _exchange_chips(parts)
    halves = [_add_chips("rs_add_chips_%d" % k, q) for k, q in enumerate(gathered)]
    g_win, g_wout, g_w1, g_w2 = _join_halves(halves)

    gga_blk = _diag_blocks(g_gate[:, 0:D_LRU])
    ggx_blk = _diag_blocks(g_gate[:, D_LRU:2 * D_LRU])
    gad, gam, gl, gga, ggx = _allgather8("allgather_small_grads", [accd, accm, accl, gga_blk, ggx_blk])

    params = {
        "ada_b": (ada_b, m_ada_b, v_ada_b), "norm1_g": (norm1_g, m_norm1_g, v_norm1_g),
        "lru_conv_b": (lru_conv_b, m_lru_conv_b, v_lru_conv_b),
        "gate_a_w": tuple(t.reshape(D_LRU, HEAD) for t in (gate_a_w, m_gate_a_w, v_gate_a_w)),
        "gate_a_b": (gate_a_b, m_gate_a_b, v_gate_a_b),
        "gate_x_w": tuple(t.reshape(D_LRU, HEAD) for t in (gate_x_w, m_gate_x_w, v_gate_x_w)),
        "gate_x_b": (gate_x_b, m_gate_x_b, v_gate_x_b), "a_param": (a_param, m_a_param, v_a_param),
        "lru_conv_w": tuple(t[0] for t in (lru_conv_w, m_lru_conv_w, v_lru_conv_w)),
        "short_conv_w": tuple(t[0] for t in (short_conv_w, m_short_conv_w, v_short_conv_w)),
        "lru_out_g": (lru_out_g, m_lru_out_g, v_lru_out_g), "conv_out_g": (conv_out_g, m_conv_out_g, v_conv_out_g),
        "norm2_g": (norm2_g, m_norm2_g, v_norm2_g),
        "final_g": tuple(t[None, :] for t in (final_g, m_final_g, v_final_g)),
    }
    small, loss_blk, dmod_cols = _small_update(gad, gam, gl, gga, ggx, mychip.reshape(1).astype(jnp.int32), params)
    loss = loss_blk[0, 0]

    ncol = 6 * D_MODEL // N_CHIP
    dmod_loc = lax.dynamic_slice_in_dim(dmod_cols, mychip * ncol, ncol, axis=1)
    sct = (c_all * jax.nn.sigmoid(c_all)).T
    ada = _ada_grad_adam(sct, dmod_loc, ada_w[0], m_ada_w[0], v_ada_w[0])

    big_w = {"w_in": (w_in, m_w_in, v_w_in, g_win.reshape(D_MODEL, WIN_BLK)),
             "w_out": (w_out, m_w_out, v_w_out, g_wout), "w_mlp1": (w_mlp1, m_w_mlp1, v_w_mlp1, g_w1),
             "w_mlp2": (w_mlp2, m_w_mlp2, v_w_mlp2, g_w2)}
    res = {"ada_w": tuple(t[None] for t in ada)}
    for n, (w, m, v, g) in big_w.items():
        d, mo, vo = _adam("adam_" + n, w[0], g, m[0], v[0])
        res[n] = (g[None], d[None], mo[None], vo[None])
    shapes = {"gate_a_w": gate_a_w.shape, "gate_x_w": gate_x_w.shape, "lru_conv_w": lru_conv_w.shape,
              "short_conv_w": short_conv_w.shape, "final_g": final_g.shape}
    for n, t in small.items():
        res[n] = tuple(u.reshape(shapes[n]) if n in shapes else u for u in t)

    order = ["ada_w", "ada_b", "norm1_g", "w_in", "lru_conv_w", "lru_conv_b", "gate_a_w", "gate_a_b", "gate_x_w",
             "gate_x_b", "a_param", "short_conv_w", "lru_out_g", "conv_out_g", "w_out", "norm2_g", "w_mlp1",
             "w_mlp2", "final_g"]
    return (loss, grad_x[None], *[res[n][0] for n in order], *[res[n][1] for n in order],
            *[res[n][2] for n in order], *[res[n][3] for n in order])
```

```python
import jax
import jax.numpy as jnp
from jax import lax
from jax.experimental import pallas as pl
from jax.experimental.pallas import tpu as pltpu

F32 = jnp.float32
BF16 = jnp.bfloat16

D_MODEL = 1024
D_LRU = 512
D_IN = 2560
D_FF = 4096
N_CHIP = 4
WIN_BLK = D_IN // N_CHIP
WOUT_BLK = D_MODEL // N_CHIP
FF_BLK = D_FF // N_CHIP
HEAD = 64
EPS = 1e-6
C_GATE = 8.0
TOKEN_TILE = 256
HALO = 8
VMEM_LIMIT = 60 * 1024 * 1024

ADAM_LR = 0.001
ADAM_B1 = 0.9
ADAM_B2 = 0.999
ADAM_EPS = 1e-08
ADAM_WD = 0.01
ADAM_STEP = 10

MESH = pl.DeviceIdType.MESH
ANY = pl.BlockSpec(memory_space=pl.ANY)
VMEM = pl.BlockSpec(memory_space=pltpu.VMEM)
SMEM = pl.BlockSpec(memory_space=pltpu.SMEM)


def _full(shape, single=False):
    nd = len(shape)
    if single:
        return pl.BlockSpec(shape, lambda *_: (0,) * nd, pipeline_mode=pl.Buffered(1))
    return pl.BlockSpec(shape, lambda *_: (0,) * nd)


def _dot(a, b):
    return jnp.dot(a, b, preferred_element_type=F32)


def _dot_nt(a, b):
    return lax.dot_general(a, b, (((1,), (1,)), ((), ())), preferred_element_type=F32)


def _dot_tn(a, b):
    return lax.dot_general(a, b, (((0,), (0,)), ((), ())), preferred_element_type=F32)


def _gmean(v, a64):
    hi = v.astype(BF16)
    lo = (v - hi.astype(F32)).astype(BF16)
    return _dot(hi, a64) + _dot(lo, a64)


def _gelu(x):
    u = 0.7978845608028654 * (x + 0.044715 * x * x * x)
    t = jnp.tanh(u)
    return 0.5 * x * (1.0 + t), t


def _gelu_grad(x, t):
    du = 0.7978845608028654 * (1.0 + 3.0 * 0.044715 * x * x)
    return 0.5 * (1.0 + t) + 0.5 * x * (1.0 - t * t) * du


def _log1p_pos(y):
    return jnp.where(y < 1e-2, y * (1.0 - y * (0.5 - y * (1.0 / 3.0 - y * 0.25))), jnp.log(1.0 + y))


def _softplus(a):
    return jnp.maximum(a, 0.0) + _log1p_pos(jnp.exp(-jnp.abs(a)))


def _neg_expm1(z):
    series = -z * (1.0 + z * (0.5 + z * (1.0 / 6.0 + z * (1.0 / 24.0 + z * (1.0 / 120.0)))))
    return jnp.where(z > -0.02, series, 1.0 - jnp.exp(z))


def _scan_fwd(a, b, row):
    n = a.shape[0]
    d = 1
    while d < n:
        m = row >= d
        b = jnp.where(m, a * pltpu.roll(b, d, 0) + b, b)
        a = jnp.where(m, a * pltpu.roll(a, d, 0), a)
        d *= 2
    return a, b


def _scan_rev(a, b, row):
    n = a.shape[0]
    d = 1
    while d < n:
        m = row < n - d
        b = jnp.where(m, b + a * pltpu.roll(b, n - d, 0), b)
        a = jnp.where(m, a * pltpu.roll(a, n - d, 0), a)
        d *= 2
    return a, b


def _colsum(v):
    return jnp.sum(v, axis=0, keepdims=True)


def _load_gathered(chip, gathered, own, slot, sems):
    copies = []
    for j in range(N_CHIP):
        @pl.when(chip == j)
        def _(j=j):
            pltpu.make_async_copy(own, slot(j), sems.at[j]).start()

        @pl.when(chip != j)
        def _(j=j):
            pltpu.make_async_copy(gathered.at[j], slot(j), sems.at[j]).start()

        copies.append(pltpu.make_async_copy(own, slot(j), sems.at[j]))
    return copies


def _lru_gates(xlb, gab, gbias, sp, first_row):
    g = _dot(xlb, gab) + gbias
    r = jax.nn.sigmoid(g[:, :D_LRU])
    ig = jax.nn.sigmoid(g[:, D_LRU:])
    la = (-C_GATE) * r * sp
    a = jnp.exp(la)
    msq = jnp.sqrt(_neg_expm1(2.0 * la))
    mult = jnp.where(first_row, 1.0, msq)
    return r, ig, a, msq, mult


def _mix_fwd(chip, x, mod, vecd, vecl, win, wout, gab, a64):
    s = x.shape[0]
    ts = TOKEN_TILE
    nt = s // ts

    def body(chip_ref, x_ref, mod_ref, vd_ref, vl_ref, win_hbm, win_own, wout_hbm, wout_own, gab_ref, a64_ref,
             hb_ref, proj_ref, hl_ref, ycat_ref, mixed_ref, x1_ref,
             win_ref, wout_ref, ext_lx, ext_cv, hcar, sems):
        i = pl.program_id(0)

        @pl.when(i == 0)
        def _():
            cps = _load_gathered(chip_ref[0], win_hbm, win_own, lambda j: win_ref.at[j], sems.at[pl.ds(0, N_CHIP)])
            cps += _load_gathered(chip_ref[0], wout_hbm, wout_own,
                                  lambda j: wout_ref.at[pl.ds(j * WOUT_BLK, WOUT_BLK), :],
                                  sems.at[pl.ds(N_CHIP, N_CHIP)])
            ext_lx[0:HALO, :] = jnp.zeros((HALO, D_LRU), F32)
            ext_cv[0:HALO, :] = jnp.zeros((HALO, D_LRU), F32)
            hcar[...] = jnp.zeros_like(hcar)
            for cp in cps:
                cp.wait()

        row = lax.broadcasted_iota(jnp.int32, (ts, D_LRU), 0)
        first_row = jnp.logical_and(row == 0, i == 0)
        xt = x_ref[...]
        shift1, scale1, gate1 = mod_ref[0:1, :], mod_ref[1:2, :], mod_ref[2:3, :]
        r1 = lax.rsqrt(jnp.mean(xt * xt, axis=-1, keepdims=True) + EPS)
        h = (xt * r1) * vd_ref[0:1, :] * (1.0 + scale1) + shift1
        hb = h.astype(BF16)
        hb_ref[...] = hb
        for j in range(N_CHIP):
            proj_ref[:, j * WIN_BLK:(j + 1) * WIN_BLK] = _dot(hb, win_ref[j])
        u_ly = proj_ref[:, 512:1024]
        u_b = proj_ref[:, 1024:1536]

        ext_lx[HALO:HALO + ts, :] = proj_ref[:, 0:512]
        xl = vl_ref[4:5, :] + vl_ref[0:1, :] * ext_lx[pl.ds(5, ts), :]
        for k in range(1, 4):
            xl = xl + vl_ref[k:k + 1, :] * ext_lx[pl.ds(5 + k, ts), :]
        ext_lx[0:HALO, :] = ext_lx[ts:ts + HALO, :]
        sp = _softplus(vl_ref[8:9, :])
        _, ig, a, _, mult = _lru_gates(xl.astype(BF16), gab_ref[...], vd_ref[3:4, :], sp, first_row)
        acum, hloc = _scan_fwd(a, mult * (ig * xl), row)
        hl = hloc + acum * hcar[0:1, :]
        hl_ref[...] = hl
        hcar[0:1, :] = hl_ref[ts - 1:ts, :]
        ge, _ = _gelu(u_ly)
        p = ge * hl
        y_lru = p * lax.rsqrt(_gmean(p * p, a64_ref[...]) + EPS) * vl_ref[9:10, :]
        ycat_ref[:, 0:512] = y_lru.astype(BF16)

        ext_cv[HALO:HALO + ts, :] = proj_ref[:, 1536:2048] * proj_ref[:, 2048:2560]
        q = vl_ref[5:6, :] * ext_cv[pl.ds(6, ts), :]
        for k in range(1, 3):
            q = q + vl_ref[5 + k:6 + k, :] * ext_cv[pl.ds(6 + k, ts), :]
        ext_cv[0:HALO, :] = ext_cv[ts:ts + HALO, :]
        yc = u_b * q
        y_conv = yc * lax.rsqrt(_gmean(yc * yc, a64_ref[...]) + EPS) * vl_ref[10:11, :]
        ycat_ref[:, 512:1024] = y_conv.astype(BF16)

        mixed = _dot(ycat_ref[...], wout_ref[...])
        mixed_ref[...] = mixed
        x1_ref[...] = xt + gate1 * mixed

    tile = lambda w: pl.BlockSpec((ts, w), lambda i: (i, 0))
    return pl.pallas_call(
        body, name="mix_fwd", grid=(nt,),
        in_specs=[SMEM, tile(D_MODEL), _full((8, D_MODEL)), _full((8, D_MODEL)), _full((16, D_LRU)),
                  ANY, ANY, ANY, ANY, _full((D_LRU, 2 * D_LRU), True), _full((D_LRU, D_LRU), True)],
        out_specs=[tile(D_MODEL), tile(D_IN), tile(D_LRU), tile(D_MODEL), tile(D_MODEL), tile(D_MODEL)],
        out_shape=[jax.ShapeDtypeStruct((s, D_MODEL), BF16), jax.ShapeDtypeStruct((s, D_IN), F32),
                   jax.ShapeDtypeStruct((s, D_LRU), F32), jax.ShapeDtypeStruct((s, D_MODEL), BF16),
                   jax.ShapeDtypeStruct((s, D_MODEL), F32), jax.ShapeDtypeStruct((s, D_MODEL), F32)],
        scratch_shapes=[pltpu.VMEM((N_CHIP, D_MODEL, WIN_BLK), BF16), pltpu.VMEM((D_MODEL, D_MODEL), BF16),
                        pltpu.VMEM((ts + HALO, D_LRU), F32), pltpu.VMEM((ts + HALO, D_LRU), F32),
                        pltpu.VMEM((HALO, D_LRU), F32), pltpu.SemaphoreType.DMA((2 * N_CHIP,))],
        compiler_params=pltpu.CompilerParams(dimension_semantics=("arbitrary",), vmem_limit_bytes=VMEM_LIMIT),
    )(chip, x, mod, vecd, vecl, *win, *wout, gab, a64)


def _mlp_fwd_bwd(chip, x1, target, mod, vecd, w1, w2):
    s = x1.shape[0]
    ts = TOKEN_TILE
    nt = s // ts

    def body(chip_ref, x1_ref, tg_ref, mod_ref, vd_ref, w1_hbm, w1_own, w2_hbm, w2_own,
             dx1_ref, act_ref, dz_ref, dmo_ref, h2_ref, acc_ref, w1_v, w2_v, rz_v, sems):
        i = pl.program_id(0)

        @pl.when(i == 0)
        def _():
            cps = _load_gathered(chip_ref[0], w1_hbm, w1_own, lambda j: w1_v.at[j], sems.at[pl.ds(0, N_CHIP)])
            cps += _load_gathered(chip_ref[0], w2_hbm, w2_own, lambda j: w2_v.at[j], sems.at[pl.ds(N_CHIP, N_CHIP)])
            acc_ref[...] = jnp.zeros_like(acc_ref)
            for cp in cps:
                cp.wait()

        xt = x1_ref[...]
        shift2, scale2, gate2 = mod_ref[3:4, :], mod_ref[4:5, :], mod_ref[5:6, :]
        g2, gf = vd_ref[1:2, :], vd_ref[2:3, :]
        r2 = lax.rsqrt(jnp.mean(xt * xt, axis=-1, keepdims=True) + EPS)
        n2 = xt * r2
        h2b = (n2 * g2 * (1.0 + scale2) + shift2).astype(BF16)
        h2_ref[...] = h2b
        mo = jnp.zeros((ts, D_MODEL), F32)
        for j in range(N_CHIP):
            rz = jnp.maximum(_dot(h2b, w1_v[j]), 0.0)
            rz_v[j] = rz
            actb = (rz * rz).astype(BF16)
            act_ref[:, j * FF_BLK:(j + 1) * FF_BLK] = actb
            mo = mo + _dot(actb, w2_v[j])
        x2 = xt + gate2 * mo
        r3 = lax.rsqrt(jnp.mean(x2 * x2, axis=-1, keepdims=True) + EPS)
        n3 = x2 * r3
        e = n3 * gf - tg_ref[...]
        loss = (0.5 / D_MODEL) * jnp.sum(_colsum(e * e), axis=1, keepdims=True)
        dy = e * (1.0 / D_MODEL)
        acc_ref[4:5, :] += _colsum(dy * n3)
        acc_ref[5:6, :] += jnp.broadcast_to(loss, (1, D_MODEL))
        dn3 = dy * gf
        dx2 = r3 * (dn3 - n3 * jnp.mean(dn3 * n3, axis=-1, keepdims=True))
        acc_ref[2:3, :] += _colsum(dx2 * mo)
        dmob = (dx2 * gate2).astype(BF16)
        dmo_ref[...] = dmob
        dh2 = jnp.zeros((ts, D_MODEL), F32)
        for j in range(N_CHIP):
            dzb = (_dot_nt(dmob, w2_v[j]) * (2.0 * rz_v[j])).astype(BF16)
            dz_ref[:, j * FF_BLK:(j + 1) * FF_BLK] = dzb
            dh2 = dh2 + _dot_nt(dzb, w1_v[j])
        acc_ref[1:2, :] += _colsum(dh2 * (n2 * g2))
        acc_ref[0:1, :] += _colsum(dh2)
        dhn2 = dh2 * (1.0 + scale2)
        acc_ref[3:4, :] += _colsum(dhn2 * n2)
        dn2 = dhn2 * g2
        dx1_ref[...] = dx2 + r2 * (dn2 - n2 * jnp.mean(dn2 * n2, axis=-1, keepdims=True))

    tile = lambda w: pl.BlockSpec((ts, w), lambda i: (i, 0))
    return pl.pallas_call(
        body, name="mlp_fwd_bwd", grid=(nt,),
        in_specs=[SMEM, tile(D_MODEL), tile(D_MODEL), _full((8, D_MODEL)), _full((8, D_MODEL)), ANY, ANY, ANY, ANY],
        out_specs=[tile(D_MODEL), tile(D_FF), tile(D_FF), tile(D_MODEL), tile(D_MODEL), _full((8, D_MODEL))],
        out_shape=[jax.ShapeDtypeStruct((s, D_MODEL), F32), jax.ShapeDtypeStruct((s, D_FF), BF16),
                   jax.ShapeDtypeStruct((s, D_FF), BF16), jax.ShapeDtypeStruct((s, D_MODEL), BF16),
                   jax.ShapeDtypeStruct((s, D_MODEL), BF16), jax.ShapeDtypeStruct((8, D_MODEL), F32)],
        scratch_shapes=[pltpu.VMEM((N_CHIP, D_MODEL, FF_BLK), BF16), pltpu.VMEM((N_CHIP, FF_BLK, D_MODEL), BF16),
                        pltpu.VMEM((N_CHIP, ts, FF_BLK), F32), pltpu.SemaphoreType.DMA((2 * N_CHIP,))],
        compiler_params=pltpu.CompilerParams(dimension_semantics=("arbitrary",), vmem_limit_bytes=VMEM_LIMIT),
    )(chip, x1, target, mod, vecd, *w1, *w2)


def _mix_bwd(chip, dx1, x, mixed, proj, hl, mod, vecd, vecl, win, wout, gab, a64):
    s = x.shape[0]
    ts = TOKEN_TILE
    nt = s // ts
    hpt = ts // HALO

    def body(chip_ref, dx1_ref, x_ref, mixed_ref, proj_ref, projh_ref, hl_ref, hlh_ref, mod_ref, vd_ref, vl_ref,
             win_hbm, win_own, wout_hbm, wout_own, gab_ref, a64_ref,
             gx_ref, dproj_ref, dmixed_ref, xlb_ref, dgb_ref, accd_ref, accl_ref,
             win_ref, wout_ref, ext_lx, ext_cv, ext_hl, ext_dxl, ext_dq, gbuf, gcar, acar, sems):
        i = pl.program_id(0)
        ri = nt - 1 - i

        @pl.when(i == 0)
        def _():
            cps = _load_gathered(chip_ref[0], win_hbm, win_own, lambda j: win_ref.at[j], sems.at[pl.ds(0, N_CHIP)])
            cps += _load_gathered(chip_ref[0], wout_hbm, wout_own,
                                  lambda j: wout_ref.at[pl.ds(j * WOUT_BLK, WOUT_BLK), :],
                                  sems.at[pl.ds(N_CHIP, N_CHIP)])
            for cp in cps:
                cp.wait()
            accd_ref[...] = jnp.zeros_like(accd_ref)
            accl_ref[...] = jnp.zeros_like(accl_ref)
            ext_dxl[ts:ts + HALO, :] = jnp.zeros((HALO, D_LRU), F32)
            ext_dq[ts:ts + HALO, :] = jnp.zeros((HALO, D_LRU), F32)
            gcar[...] = jnp.zeros_like(gcar)
            acar[...] = jnp.zeros_like(acar)

        row = lax.broadcasted_iota(jnp.int32, (ts, D_LRU), 0)
        first_row = jnp.logical_and(row == 0, ri == 0)
        halo_on = jnp.where(ri == 0, 0.0, 1.0)
        shift1, scale1, gate1 = mod_ref[0:1, :], mod_ref[1:2, :], mod_ref[2:3, :]
        g1 = vd_ref[0:1, :]
        a64m = a64_ref[...]
        lg, cg = vl_ref[9:10, :], vl_ref[10:11, :]

        dx1 = dx1_ref[...]
        accd_ref[2:3, :] += _colsum(dx1 * mixed_ref[...])
        dmb = (dx1 * gate1).astype(BF16)
        dmixed_ref[...] = dmb
        dycat = _dot_nt(dmb, wout_ref[...])
        dyl = dycat[:, 0:512]
        dyv = dycat[:, 512:1024]

        u_ly = proj_ref[:, 512:1024]
        u_b = proj_ref[:, 1024:1536]
        u_c = proj_ref[:, 1536:2048]
        u_v = proj_ref[:, 2048:2560]
        ext_lx[0:HALO, :] = projh_ref[:, 0:512] * halo_on
        ext_lx[HALO:HALO + ts, :] = proj_ref[:, 0:512]
        xl = vl_ref[4:5, :] + vl_ref[0:1, :] * ext_lx[pl.ds(5, ts), :]
        for k in range(1, 4):
            xl = xl + vl_ref[k:k + 1, :] * ext_lx[pl.ds(5 + k, ts), :]
        xlb = xl.astype(BF16)
        xlb_ref[...] = xlb
        sp = _softplus(vl_ref[8:9, :])
        r, ig, a, msq, mult = _lru_gates(xlb, gab_ref[...], vd_ref[3:4, :], sp, first_row)
        hl = hl_ref[...]
        ge, th = _gelu(u_ly)
        p = ge * hl
        rl = lax.rsqrt(_gmean(p * p, a64m) + EPS)
        nl = p * rl
        ext_cv[0:HALO, :] = projh_ref[:, 1536:2048] * projh_ref[:, 2048:2560] * halo_on
        ext_cv[HALO:HALO + ts, :] = u_c * u_v
        q = vl_ref[5:6, :] * ext_cv[pl.ds(6, ts), :]
        for k in range(1, 3):
            q = q + vl_ref[5 + k:6 + k, :] * ext_cv[pl.ds(6 + k, ts), :]
        yc = u_b * q
        rc = lax.rsqrt(_gmean(yc * yc, a64m) + EPS)
        nc = yc * rc

        accl_ref[9:10, :] += _colsum(dyl * nl)
        dnl = dyl * lg
        dp = rl * (dnl - nl * _gmean(dnl * nl, a64m))
        dproj_ref[:, 512:1024] = ((dp * hl) * _gelu_grad(u_ly, th)).astype(BF16)
        a_next = jnp.where(row == ts - 1, acar[0:1, :], pltpu.roll(a, ts - 1, 0))
        acum, gloc = _scan_rev(a_next, dp * ge, row)
        gbuf[...] = gloc + acum * gcar[0:1, :]
        gcar[0:1, :] = gbuf[0:1, :]
        ext_hl[0:HALO, :] = hlh_ref[...] * halo_on
        ext_hl[HALO:HALO + ts, :] = hl
        acar[...] = a[0:HALO, :]
        gt = gbuf[...]
        da = gt * ext_hl[pl.ds(HALO - 1, ts), :]
        dmult = gt * ig * xl
        di = gt * mult * xl
        dxl = gt * mult * ig
        dla = da * a - jnp.where(first_row, 0.0, dmult * a * a / msq)
        accl_ref[8:9, :] += _colsum(dla * ((-C_GATE) * r))
        dra = dla * ((-C_GATE) * sp) * r * (1.0 - r)
        dia = di * ig * (1.0 - ig)
        accd_ref[4:5, 0:D_LRU] += _colsum(dra)
        accd_ref[4:5, D_LRU:2 * D_LRU] += _colsum(dia)
        dgb_ref[:, 0:D_LRU] = dra.astype(BF16)
        dgb_ref[:, D_LRU:2 * D_LRU] = dia.astype(BF16)
        dxl = dxl + _dot_nt(dgb_ref[...], gab_ref[...])
        accl_ref[4:5, :] += _colsum(dxl)
        for k in range(4):
            accl_ref[k:k + 1, :] += _colsum(dxl * ext_lx[pl.ds(5 + k, ts), :])
        ext_dxl[0:ts, :] = dxl
        du_lx = vl_ref[0:1, :] * ext_dxl[pl.ds(3, ts), :]
        for k in range(1, 4):
            du_lx = du_lx + vl_ref[k:k + 1, :] * ext_dxl[pl.ds(3 - k, ts), :]
        ext_dxl[ts:ts + HALO, :] = ext_dxl[0:HALO, :]
        dproj_ref[:, 0:512] = du_lx.astype(BF16)

        accl_ref[10:11, :] += _colsum(dyv * nc)
        dnc = dyv * cg
        dyc = rc * (dnc - nc * _gmean(dnc * nc, a64m))
        dproj_ref[:, 1024:1536] = (dyc * q).astype(BF16)
        dq = dyc * u_b
        for k in range(3):
            accl_ref[5 + k:6 + k, :] += _colsum(dq * ext_cv[pl.ds(6 + k, ts), :])
        ext_dq[0:ts, :] = dq
        dcv = vl_ref[5:6, :] * ext_dq[pl.ds(2, ts), :]
        for k in range(1, 3):
            dcv = dcv + vl_ref[5 + k:6 + k, :] * ext_dq[pl.ds(2 - k, ts), :]
        ext_dq[ts:ts + HALO, :] = ext_dq[0:HALO, :]
        dproj_ref[:, 1536:2048] = (dcv * u_v).astype(BF16)
        dproj_ref[:, 2048:2560] = (dcv * u_c).astype(BF16)

        dh = _dot_nt(dproj_ref[:, 0:WIN_BLK], win_ref[0])
        for j in range(1, N_CHIP):
            dh = dh + _dot_nt(dproj_ref[:, j * WIN_BLK:(j + 1) * WIN_BLK], win_ref[j])
        xt = x_ref[...]
        r1 = lax.rsqrt(jnp.mean(xt * xt, axis=-1, keepdims=True) + EPS)
        n1 = xt * r1
        accd_ref[1:2, :] += _colsum(dh * (n1 * g1))
        accd_ref[0:1, :] += _colsum(dh)
        dhn1 = dh * (1.0 + scale1)
        accd_ref[3:4, :] += _colsum(dhn1 * n1)
        dn1 = dhn1 * g1
        gx_ref[...] = dx1 + r1 * (dn1 - n1 * jnp.mean(dn1 * n1, axis=-1, keepdims=True))

    tile = lambda w: pl.BlockSpec((ts, w), lambda i: (nt - 1 - i, 0))
    halo = lambda w: pl.BlockSpec((HALO, w), lambda i: (jnp.maximum((nt - 1 - i) * hpt - 1, 0), 0))
    ext = pltpu.VMEM((ts + HALO, D_LRU), F32)
    return pl.pallas_call(
        body, name="mix_bwd", grid=(nt,),
        in_specs=[SMEM, tile(D_MODEL), tile(D_MODEL), tile(D_MODEL), tile(D_IN), halo(D_IN), tile(D_LRU), halo(D_LRU),
                  _full((8, D_MODEL)), _full((8, D_MODEL)), _full((16, D_LRU)),
                  ANY, ANY, ANY, ANY, _full((D_LRU, 2 * D_LRU), True), _full((D_LRU, D_LRU), True)],
        out_specs=[tile(D_MODEL), tile(D_IN), tile(D_MODEL), tile(D_LRU), tile(2 * D_LRU),
                   _full((8, D_MODEL)), _full((16, D_LRU))],
        out_shape=[jax.ShapeDtypeStruct((s, D_MODEL), F32), jax.ShapeDtypeStruct((s, D_IN), BF16),
                   jax.ShapeDtypeStruct((s, D_MODEL), BF16), jax.ShapeDtypeStruct((s, D_LRU), BF16),
                   jax.ShapeDtypeStruct((s, 2 * D_LRU), BF16),
                   jax.ShapeDtypeStruct((8, D_MODEL), F32), jax.ShapeDtypeStruct((16, D_LRU), F32)],
        scratch_shapes=[pltpu.VMEM((N_CHIP, D_MODEL, WIN_BLK), BF16), pltpu.VMEM((D_MODEL, D_MODEL), BF16),
                        ext, ext, ext, ext, ext, pltpu.VMEM((ts, D_LRU), F32),
                        pltpu.VMEM((HALO, D_LRU), F32), pltpu.VMEM((HALO, D_LRU), F32),
                        pltpu.SemaphoreType.DMA((2 * N_CHIP,))],
        compiler_params=pltpu.CompilerParams(dimension_semantics=("arbitrary",), vmem_limit_bytes=VMEM_LIMIT),
    )(chip, dx1, x, mixed, proj, proj, hl, hl, mod, vecd, vecl, *win, *wout, gab, a64)


def _wgrad(name, a, b, a_blk, b_blk, out_dtype):
    s = a.shape[0]
    aw = a_blk or a.shape[1]
    bw = b_blk or b.shape[1]
    nblk = N_CHIP if (a_blk or b_blk) else 1

    def body(a_ref, b_ref, o_ref):
        o_ref[0] = _dot_tn(a_ref[...], b_ref[...]).astype(out_dtype)

    return pl.pallas_call(
        body, name=name, grid=(nblk,),
        in_specs=[pl.BlockSpec((s, aw), (lambda j: (0, j)) if a_blk else (lambda j: (0, 0))),
                  pl.BlockSpec((s, bw), (lambda j: (0, j)) if b_blk else (lambda j: (0, 0)))],
        out_specs=pl.BlockSpec((1, aw, bw), lambda j: (j, 0, 0)),
        out_shape=jax.ShapeDtypeStruct((nblk, aw, bw), out_dtype),
        compiler_params=pltpu.CompilerParams(dimension_semantics=("arbitrary",), vmem_limit_bytes=VMEM_LIMIT),
    )(a, b)


def _mod_matmul(c_all, ada_w_loc):
    n = ada_w_loc.shape[1]
    cb = 512

    def body(c_ref, w_ref, o_ref):
        c = c_ref[...]
        sc = c * jax.nn.sigmoid(c)
        o_ref[...] = _dot(sc.astype(BF16), w_ref[...].astype(BF16))

    return pl.pallas_call(
        body, name="mod_matmul", grid=(n // cb,),
        in_specs=[_full((8, D_MODEL)), pl.BlockSpec((D_MODEL, cb), lambda j: (0, j))],
        out_specs=pl.BlockSpec((8, cb), lambda j: (0, j)),
        out_shape=jax.ShapeDtypeStruct((8, n), F32),
        compiler_params=pltpu.CompilerParams(dimension_semantics=("arbitrary",), vmem_limit_bytes=VMEM_LIMIT),
    )(c_all, ada_w_loc)


def _adam_math(w, g, m, v):
    m = ADAM_B1 * m + (1.0 - ADAM_B1) * g
    v = ADAM_B2 * v + (1.0 - ADAM_B2) * (g * g)
    m_hat = m / (1.0 - ADAM_B1 ** ADAM_STEP)
    v_hat = v / (1.0 - ADAM_B2 ** ADAM_STEP)
    delta = (-ADAM_LR) * (m_hat / (jnp.sqrt(v_hat) + ADAM_EPS) + ADAM_WD * w)
    return delta, m, v


def _adam(name, core, w, g_own, g_sib, m, v):
    r, c = w.shape
    half = r // 2
    rb = min(half, 256)
    nh = half // rb

    def body(core_ref, w_ref, go_ref, gs_ref, m_ref, v_ref, g_ref, d_ref, mo_ref, vo_ref):
        mine = (pl.program_id(0) // nh) == core_ref[0]
        g = jnp.where(mine, go_ref[...], gs_ref[...])
        g_ref[...] = g
        d_ref[...], mo_ref[...], vo_ref[...] = _adam_math(w_ref[...], g, m_ref[...], v_ref[...])

    spec = pl.BlockSpec((rb, c), lambda i, core_ref: (i, 0))
    hspec = pl.BlockSpec((rb, c), lambda i, core_ref: (i % nh, 0))
    sds = jax.ShapeDtypeStruct((r, c), F32)
    return pl.pallas_call(
        body, name=name,
        grid_spec=pltpu.PrefetchScalarGridSpec(
            num_scalar_prefetch=1, grid=(r // rb,),
            in_specs=[spec, hspec, hspec, spec, spec], out_specs=[spec] * 4),
        out_shape=[sds] * 4,
        compiler_params=pltpu.CompilerParams(dimension_semantics=("arbitrary",), vmem_limit_bytes=VMEM_LIMIT),
    )(core, w, g_own, g_sib, m, v)


def _ada_grad_adam(sct, dmod_loc, w, m, v):
    r, c = w.shape
    rb = 128

    def body(s_ref, dm_ref, w_ref, m_ref, v_ref, g_ref, d_ref, mo_ref, vo_ref):
        g = s_ref[:, 0:1] * dm_ref[0:1, :]
        for b in range(1, 8):
            g = g + s_ref[:, b:b + 1] * dm_ref[b:b + 1, :]
        g_ref[...] = g
        d_ref[...], mo_ref[...], vo_ref[...] = _adam_math(w_ref[...], g, m_ref[...], v_ref[...])

    spec = pl.BlockSpec((rb, c), lambda i: (i, 0))
    sds = jax.ShapeDtypeStruct((r, c), F32)
    return pl.pallas_call(
        body, name="ada_grad_adam", grid=(r // rb,),
        in_specs=[pl.BlockSpec((rb, 8), lambda i: (i, 0)), _full((8, c)), spec, spec, spec],
        out_specs=[spec] * 4, out_shape=[sds] * 4,
        compiler_params=pltpu.CompilerParams(dimension_semantics=("arbitrary",), vmem_limit_bytes=VMEM_LIMIT),
    )(sct, dmod_loc, w, m, v)


def _position():
    x, y, c = lax.axis_index("x"), lax.axis_index("y"), lax.axis_index("c")
    chips = [(1 - x, y), (x, 1 - y), (1 - x, 1 - y)]
    return x, y, c, chips


def _allgather8(name, arrs):
    na = len(arrs)

    def body(*refs):
        ins, outs = refs[:na], refs[na:2 * na]
        send_sems, recv_sems, local_sems = refs[2 * na:]
        x, y, c, chips = _position()
        me, sibling = (x, y, c), (x, y, 1 - c)
        first, passed, local = [], [], []
        for a in range(na):
            m_per = ins[a].shape[0]

            def rows(px, py, pc, a=a, m_per=m_per):
                return outs[a].at[pl.ds((4 * px + 2 * py + pc) * m_per, m_per), :]

            def copy(k, block, to, src=None, a=a, rows=rows):
                return pltpu.make_async_remote_copy(
                    src_ref=rows(*block) if src is None else src, dst_ref=rows(*block),
                    send_sem=send_sems.at[7 * a + k], recv_sem=recv_sems.at[7 * a + k],
                    device_id=to, device_id_type=MESH)

            mine = pltpu.make_async_copy(ins[a], rows(*me), local_sems.at[a])
            mine.start()
            local.append(mine)
            f = [copy(0, me, sibling, src=ins[a])]
            f += [copy(1 + j, me, (*chip, c), src=ins[a]) for j, chip in enumerate(chips)]
            for cp in f:
                cp.start()
            first.append((f, copy))
        for a in range(na):
            f, copy = first[a]
            p = [copy(4 + j, (*chip, c), sibling) for j, chip in enumerate(chips)]
            for j, chip in enumerate(chips):
                copy(1 + j, (*chip, c), me).wait_recv()
                p[j].start()
            passed.append(p)
        for a in range(na):
            f, copy = first[a]
            copy(0, sibling, me).wait_recv()
            for j, chip in enumerate(chips):
                copy(4 + j, (*chip, 1 - c), me).wait_recv()
            for cp in f + passed[a]:
                cp.wait_send()
            local[a].wait()

    return pl.pallas_call(
        body, name=name,
        out_shape=[jax.ShapeDtypeStruct((8 * a.shape[0], a.shape[1]), a.dtype) for a in arrs],
        in_specs=[VMEM] * na, out_specs=[VMEM] * na,
        scratch_shapes=[pltpu.SemaphoreType.DMA((7 * na,)), pltpu.SemaphoreType.DMA((7 * na,)),
                        pltpu.SemaphoreType.DMA((na,))],
        compiler_params=pltpu.CompilerParams(vmem_limit_bytes=VMEM_LIMIT),
    )(*arrs)


def _allgather_weights(shards):
    na = len(shards)

    def body(*refs):
        ins, outs = refs[:na], refs[na:2 * na]
        send_sems, recv_sems = refs[2 * na:]
        x, y, c, chips = _position()
        sibling = (x, y, 1 - c)
        mychip = 2 * x + y
        first, passed = [], []
        for a in range(na):
            half = ins[a].shape[0] // 2

            def part(chip, pc, a=a, half=half):
                return outs[a].at[chip, pl.ds(pc * half, half), :]

            def copy(k, chip, pc, to, src=None, a=a, part=part):
                return pltpu.make_async_remote_copy(
                    src_ref=part(chip, pc) if src is None else src, dst_ref=part(chip, pc),
                    send_sem=send_sems.at[6 * a + k], recv_sem=recv_sems.at[6 * a + k],
                    device_id=to, device_id_type=MESH)

            src = ins[a].at[pl.ds(c * half, half), :]
            f = [copy(j, mychip, c, (*chip, c), src=src) for j, chip in enumerate(chips)]
            for cp in f:
                cp.start()
            first.append((f, copy))
        for a in range(na):
            f, copy = first[a]
            p = [copy(3 + j, 2 * chip[0] + chip[1], c, sibling) for j, chip in enumerate(chips)]
            for j, chip in enumerate(chips):
                copy(j, 2 * chip[0] + chip[1], c, sibling).wait_recv()
                p[j].start()
            passed.append(p)
        for a in range(na):
            f, copy = first[a]
            for j, chip in enumerate(chips):
                copy(3 + j, 2 * chip[0] + chip[1], 1 - c, sibling).wait_recv()
            for cp in f + passed[a]:
                cp.wait_send()

    return pl.pallas_call(
        body, name="allgather_weights",
        out_shape=[jax.ShapeDtypeStruct((N_CHIP,) + s.shape, s.dtype) for s in shards],
        in_specs=[ANY] * na, out_specs=[ANY] * na,
        scratch_shapes=[pltpu.SemaphoreType.DMA((6 * na,)), pltpu.SemaphoreType.DMA((6 * na,))],
    )(*shards)


def _swap_halves(grads):
    na = len(grads)

    def body(*refs):
        ins, outs = refs[:na], refs[na:2 * na]
        send_sems, recv_sems = refs[2 * na:]
        x, y, c, _ = _position()
        cps = []
        for a in range(na):
            half = ins[a].shape[1] // 2
            cp = pltpu.make_async_remote_copy(
                src_ref=ins[a].at[:, pl.ds((1 - c) * half, half), :], dst_ref=outs[a],
                send_sem=send_sems.at[a], recv_sem=recv_sems.at[a],
                device_id=(x, y, 1 - c), device_id_type=MESH)
            cp.start()
            cps.append(cp)
        for cp in cps:
            cp.wait()

    return pl.pallas_call(
        body, name="rs_swap_halves",
        out_shape=[jax.ShapeDtypeStruct((g.shape[0], g.shape[1] // 2, g.shape[2]), g.dtype) for g in grads],
        in_specs=[ANY] * na, out_specs=[ANY] * na,
        scratch_shapes=[pltpu.SemaphoreType.DMA((na,)), pltpu.SemaphoreType.DMA((na,))],
    )(*grads)


def _exchange_chips(parts):
    na = len(parts)

    def body(*refs):
        ins, outs = refs[:na], refs[na:2 * na]
        send_sems, recv_sems = refs[2 * na:]
        x, y, c, chips = _position()
        cps = []
        for a in range(na):
            for j, chip in enumerate(chips):
                cp = pltpu.make_async_remote_copy(
                    src_ref=ins[a].at[2 * chip[0] + chip[1]], dst_ref=outs[a].at[j],
                    send_sem=send_sems.at[3 * a + j], recv_sem=recv_sems.at[3 * a + j],
                    device_id=(*chip, c), device_id_type=MESH)
                cp.start()
                cps.append(cp)
        for cp in cps:
            cp.wait()

    return pl.pallas_call(
        body, name="rs_exchange_chips",
        out_shape=[jax.ShapeDtypeStruct((3,) + p.shape[1:], p.dtype) for p in parts],
        in_specs=[ANY] * na, out_specs=[ANY] * na,
        scratch_shapes=[pltpu.SemaphoreType.DMA((3 * na,)), pltpu.SemaphoreType.DMA((3 * na,))],
    )(*parts)


def _swap_reduced(halves):
    na = len(halves)

    def body(*refs):
        ins, outs = refs[:na], refs[na:2 * na]
        send_sems, recv_sems = refs[2 * na:]
        x, y, c, _ = _position()
        cps = []
        for a in range(na):
            cp = pltpu.make_async_remote_copy(
                src_ref=ins[a], dst_ref=outs[a], send_sem=send_sems.at[a], recv_sem=recv_sems.at[a],
                device_id=(x, y, 1 - c), device_id_type=MESH)
            cp.start()
            cps.append(cp)
        for cp in cps:
            cp.wait()

    return pl.pallas_call(
        body, name="rs_swap_reduced",
        out_shape=[jax.ShapeDtypeStruct(h.shape, h.dtype) for h in halves],
        in_specs=[ANY] * na, out_specs=[ANY] * na,
        scratch_shapes=[pltpu.SemaphoreType.DMA((na,)), pltpu.SemaphoreType.DMA((na,))],
    )(*halves)


def _add_sibling(name, grad, recv, core):
    _, r, c = grad.shape
    half = r // 2
    rb = min(half, 256)
    nrb = half // rb

    def body(core_ref, g_ref, r_ref, o_ref):
        o_ref[...] = (g_ref[...].astype(F32) + r_ref[...].astype(F32)).astype(BF16)

    return pl.pallas_call(
        body, name=name,
        grid_spec=pltpu.PrefetchScalarGridSpec(
            num_scalar_prefetch=1, grid=(N_CHIP, nrb),
            in_specs=[pl.BlockSpec((1, rb, c), lambda j, i, core_ref: (j, core_ref[0] * nrb + i, 0)),
                      pl.BlockSpec((1, rb, c), lambda j, i, core_ref: (j, i, 0))],
            out_specs=pl.BlockSpec((1, rb, c), lambda j, i, core_ref: (j, i, 0))),
        out_shape=jax.ShapeDtypeStruct((N_CHIP, half, c), BF16),
        compiler_params=pltpu.CompilerParams(dimension_semantics=("arbitrary", "arbitrary"),
                                             vmem_limit_bytes=VMEM_LIMIT),
    )(core, grad, recv)


def _add_chips(name, chip, p, q):
    _, half, c = q.shape
    rb = min(half, 256)

    def body(chip_ref, p_ref, q_ref, o_ref):
        acc = p_ref[0].astype(F32)
        for j in range(3):
            acc = acc + q_ref[j].astype(F32)
        o_ref[...] = acc

    return pl.pallas_call(
        body, name=name,
        grid_spec=pltpu.PrefetchScalarGridSpec(
            num_scalar_prefetch=1, grid=(half // rb,),
            in_specs=[pl.BlockSpec((1, rb, c), lambda i, chip_ref: (chip_ref[0], i, 0)),
                      pl.BlockSpec((3, rb, c), lambda i, chip_ref: (0, i, 0))],
            out_specs=pl.BlockSpec((rb, c), lambda i, chip_ref: (i, 0))),
        out_shape=jax.ShapeDtypeStruct((half, c), F32),
        compiler_params=pltpu.CompilerParams(dimension_semantics=("arbitrary",), vmem_limit_bytes=VMEM_LIMIT),
    )(chip, p, q)


def _small_update(gad, gam, gl, gga, ggx, mychip, params):
    names = ["ada_b", "norm1_g", "lru_conv_b", "gate_a_w", "gate_a_b", "gate_x_w", "gate_x_b", "a_param",
             "lru_conv_w", "short_conv_w", "lru_out_g", "conv_out_g", "norm2_g", "final_g"]
    flat = [t for n in names for t in params[n]]
    nin = len(flat)

    def body(chip_ref, gad_ref, gam_ref, gl_ref, gga_ref, ggx_ref, *refs):
        ins = {n: refs[3 * k:3 * k + 3] for k, n in enumerate(names)}
        outs = {n: refs[nin + 4 * k:nin + 4 * k + 4] for k, n in enumerate(names)}
        loss_ref, dmod_ref = refs[nin + 4 * len(names):nin + 4 * len(names) + 2]

        def dsum(ref, lo, n):
            acc = ref[lo:lo + n, :]
            for dev in range(1, 8):
                acc = acc + ref[dev * (ref.shape[0] // 8) + lo:dev * (ref.shape[0] // 8) + lo + n, :]
            return acc

        def update(n, g):
            w_ref, m_ref, v_ref = ins[n]
            g_ref, d_ref, mo_ref, vo_ref = outs[n]
            g_ref[...] = g
            d_ref[...], mo_ref[...], vo_ref[...] = _adam_math(w_ref[...], g, m_ref[...], v_ref[...])

        d, dm, l, lw = refs[-4:]
        d[...] = dsum(gad_ref, 0, 8)
        dm[...] = dsum(gam_ref, 0, 8)
        l[...] = dsum(gl_ref, 0, 16)
        for dev in range(8):
            for k in range(3):
                dmod_ref[dev:dev + 1, k * D_MODEL:(k + 1) * D_MODEL] = gad_ref[dev * 8 + k:dev * 8 + k + 1, :]
                dmod_ref[dev:dev + 1, (3 + k) * D_MODEL:(4 + k) * D_MODEL] = gam_ref[dev * 8 + k:dev * 8 + k + 1, :]
        w_ref, m_ref, v_ref = ins["ada_b"]
        g_ref, d_ref, mo_ref, vo_ref = outs["ada_b"]
        for k in range(3):
            g_ref[:, k * D_MODEL:(k + 1) * D_MODEL] = d[k:k + 1, :]
            g_ref[:, (3 + k) * D_MODEL:(4 + k) * D_MODEL] = dm[k:k + 1, :]
        d_ref[...], mo_ref[...], vo_ref[...] = _adam_math(w_ref[...], g_ref[...], m_ref[...], v_ref[...])
        update("norm1_g", d[3:4, :])
        update("norm2_g", dm[3:4, :])
        update("final_g", dm[4:5, :])
        update("gate_a_b", d[4:5, 0:D_LRU])
        update("gate_x_b", d[4:5, D_LRU:2 * D_LRU])
        update("lru_conv_b", l[4:5, :])
        update("a_param", l[8:9, :] * jax.nn.sigmoid(ins["a_param"][0][...]))
        update("lru_out_g", l[9:10, :])
        update("conv_out_g", l[10:11, :])
        loss_ref[...] = jnp.broadcast_to(dm[5:6, 0:128], (8, 128))
        chip = chip_ref[0]
        acc = jnp.zeros((8, 128), F32)
        for j in range(N_CHIP):
            acc = acc + jnp.where(chip == j, l[0:8, j * 128:(j + 1) * 128], 0.0)
        lw[...] = acc
        update("lru_conv_w", lw[0:4, :])
        update("short_conv_w", lw[5:8, :])
        update("gate_a_w", dsum(gga_ref, 0, D_LRU))
        update("gate_x_w", dsum(ggx_ref, 0, D_LRU))

    out_shape = []
    for n in names:
        out_shape += [jax.ShapeDtypeStruct(params[n][0].shape, F32)] * 4
    out_shape += [jax.ShapeDtypeStruct((8, 128), F32), jax.ShapeDtypeStruct((8, 6 * D_MODEL), F32)]
    res = pl.pallas_call(
        body, name="small_update", out_shape=out_shape,
        in_specs=[pl.BlockSpec(memory_space=pltpu.SMEM)] + [VMEM] * (5 + nin),
        out_specs=[VMEM] * len(out_shape),
        scratch_shapes=[pltpu.VMEM((8, D_MODEL), F32), pltpu.VMEM((8, D_MODEL), F32), pltpu.VMEM((16, D_LRU), F32),
                        pltpu.VMEM((8, 128), F32)],
        compiler_params=pltpu.CompilerParams(vmem_limit_bytes=VMEM_LIMIT),
    )(mychip, gad, gam, gl, gga, ggx, *flat)
    per = {n: res[4 * k:4 * k + 4] for k, n in enumerate(names)}
    return per, res[-2], res[-1]


def _local_step(chip, x, target, mod, vecd, vecl, gab, a64, win, wout, w1, w2):
    hb, proj, hl, ycat, mixed, x1 = _mix_fwd(chip, x, mod, vecd, vecl, win, wout, gab, a64)
    dx1, act, dz, dmo, h2b, accm = _mlp_fwd_bwd(chip, x1, target, mod, vecd, w1, w2)
    grad_x, dproj, dmixed, xlb, dgb, accd, accl = _mix_bwd(
        chip, dx1, x, mixed, proj, hl, mod, vecd, vecl, win, wout, gab, a64)
    g_win = _wgrad("wgrad_in", hb, dproj, 0, WIN_BLK, BF16)
    g_wout = _wgrad("wgrad_out", ycat, dmixed, WOUT_BLK, 0, BF16)
    g_w1 = _wgrad("wgrad_mlp1", h2b, dz, 0, FF_BLK, BF16)
    g_w2 = _wgrad("wgrad_mlp2", act, dmo, FF_BLK, 0, BF16)
    g_gate = _wgrad("wgrad_gate", xlb, dgb, 0, 0, F32)[0]
    return grad_x, (g_win, g_wout, g_w1, g_w2), g_gate, accm, accd, accl


def _block_diag(w):
    eye = jnp.eye(8, dtype=w.dtype)
    return (eye[:, None, :, None] * w[:, :, None, :]).reshape(8 * HEAD, 8 * HEAD)


def _diag_blocks(g):
    return jnp.concatenate([g[h * HEAD:(h + 1) * HEAD, h * HEAD:(h + 1) * HEAD] for h in range(8)], axis=0)


def kernel(x, c, ada_w, ada_b, norm1_g, w_in, lru_conv_w, lru_conv_b, gate_a_w, gate_a_b, gate_x_w, gate_x_b, a_param, short_conv_w, lru_out_g, conv_out_g, w_out, norm2_g, w_mlp1, w_mlp2, final_g, loss_target, m_ada_w, m_ada_b, m_norm1_g, m_w_in, m_lru_conv_w, m_lru_conv_b, m_gate_a_w, m_gate_a_b, m_gate_x_w, m_gate_x_b, m_a_param, m_short_conv_w, m_lru_out_g, m_conv_out_g, m_w_out, m_norm2_g, m_w_mlp1, m_w_mlp2, m_final_g, v_ada_w, v_ada_b, v_norm1_g, v_w_in, v_lru_conv_w, v_lru_conv_b, v_gate_a_w, v_gate_a_b, v_gate_x_w, v_gate_x_b, v_a_param, v_short_conv_w, v_lru_out_g, v_conv_out_g, v_w_out, v_norm2_g, v_w_mlp1, v_w_mlp2, v_final_g):
    xi, yi, ci = lax.axis_index("x"), lax.axis_index("y"), lax.axis_index("c")
    mychip = 2 * xi + yi
    me = 4 * xi + 2 * yi + ci

    c_blk = jnp.zeros((8, D_MODEL), F32).at[0:1].set(c)
    cw_blk = jnp.zeros((8, 128), F32).at[0:4].set(lru_conv_w[0]).at[4:7].set(short_conv_w[0])
    c_g, cw_g = _allgather8("allgather_cond", [c_blk, cw_blk])
    c_all = c_g.reshape(8, 8, D_MODEL)[:, 0]
    cw_g = cw_g.reshape(4, 2, 8, 128)[:, 0]
    lcw = cw_g[:, 0:4].transpose(1, 0, 2).reshape(4, D_LRU)
    scw = cw_g[:, 4:7].transpose(1, 0, 2).reshape(3, D_LRU)

    mod_loc = _mod_matmul(c_all, ada_w[0])
    (mod_g,) = _allgather8("allgather_mod", [mod_loc])
    mod_all = mod_g.reshape(4, 2, 8, 6 * D_MODEL // 4)[:, 0].transpose(1, 0, 2).reshape(8, 6 * D_MODEL) + ada_b
    mod_pad = jnp.pad(mod_all.reshape(8, 6, D_MODEL), ((0, 0), (0, 2), (0, 0)))
    mod = lax.dynamic_slice_in_dim(mod_pad, me, 1, axis=0).reshape(8, D_MODEL)

    own = [w_in[0].astype(BF16), w_out[0].astype(BF16), w_mlp1[0].astype(BF16), w_mlp2[0].astype(BF16)]
    gathered_w = _allgather_weights(own)
    chip = mychip.reshape(1).astype(jnp.int32)
    core = ci.reshape(1).astype(jnp.int32)

    vecd = jnp.concatenate([norm1_g, norm2_g, final_g[None, :], jnp.concatenate([gate_a_b, gate_x_b], axis=1),
                            jnp.zeros((4, D_MODEL), F32)], axis=0)
    vecl = jnp.concatenate([lcw, lru_conv_b, scw, a_param, lru_out_g, conv_out_g, jnp.zeros((5, D_LRU), F32)], axis=0)
    gab = jnp.concatenate([_block_diag(gate_a_w[0]), _block_diag(gate_x_w[0])], axis=1).astype(BF16)
    a64 = _block_diag(jnp.full((8, HEAD, HEAD), 1.0 / HEAD, F32)).astype(BF16)

    grad_x, big, g_gate, accm, accd, accl = _local_step(
        chip, x[0], loss_target[0], mod, vecd, vecl, gab, a64, *zip(gathered_w, own))

    recv = _swap_halves(list(big))
    parts = [_add_sibling("rs_add_sibling_%d" % k, g, r, core) for k, (g, r) in enumerate(zip(big, recv))]
    landed = _exchange_chips(parts)
    g_own = [_add_chips("rs_add_chips_%d" % k, chip, p, q) for k, (p, q) in enumerate(zip(parts, landed))]
    g_sib = _swap_reduced(g_own)

    gga_blk = _diag_blocks(g_gate[:, 0:D_LRU])
    ggx_blk = _diag_blocks(g_gate[:, D_LRU:2 * D_LRU])
    gad, gam, gl, gga, ggx = _allgather8("allgather_small_grads", [accd, accm, accl, gga_blk, ggx_blk])

    params = {
        "ada_b": (ada_b, m_ada_b, v_ada_b), "norm1_g": (norm1_g, m_norm1_g, v_norm1_g),
        "lru_conv_b": (lru_conv_b, m_lru_conv_b, v_lru_conv_b),
        "gate_a_w": tuple(t.reshape(D_LRU, HEAD) for t in (gate_a_w, m_gate_a_w, v_gate_a_w)),
        "gate_a_b": (gate_a_b, m_gate_a_b, v_gate_a_b),
        "gate_x_w": tuple(t.reshape(D_LRU, HEAD) for t in (gate_x_w, m_gate_x_w, v_gate_x_w)),
        "gate_x_b": (gate_x_b, m_gate_x_b, v_gate_x_b), "a_param": (a_param, m_a_param, v_a_param),
        "lru_conv_w": tuple(t[0] for t in (lru_conv_w, m_lru_conv_w, v_lru_conv_w)),
        "short_conv_w": tuple(t[0] for t in (short_conv_w, m_short_conv_w, v_short_conv_w)),
        "lru_out_g": (lru_out_g, m_lru_out_g, v_lru_out_g), "conv_out_g": (conv_out_g, m_conv_out_g, v_conv_out_g),
        "norm2_g": (norm2_g, m_norm2_g, v_norm2_g),
        "final_g": tuple(t[None, :] for t in (final_g, m_final_g, v_final_g)),
    }
    small, loss_blk, dmod_cols = _small_update(gad, gam, gl, gga, ggx, mychip.reshape(1).astype(jnp.int32), params)
    loss = loss_blk[0, 0]

    ncol = 6 * D_MODEL // N_CHIP
    dmod_loc = lax.dynamic_slice_in_dim(dmod_cols, mychip * ncol, ncol, axis=1)
    sct = (c_all * jax.nn.sigmoid(c_all)).T
    ada = _ada_grad_adam(sct, dmod_loc, ada_w[0], m_ada_w[0], v_ada_w[0])

    big_w = {"w_in": (w_in, m_w_in, v_w_in), "w_out": (w_out, m_w_out, v_w_out),
             "w_mlp1": (w_mlp1, m_w_mlp1, v_w_mlp1), "w_mlp2": (w_mlp2, m_w_mlp2, v_w_mlp2)}
    res = {"ada_w": tuple(t[None] for t in ada)}
    for k, (n, (w, m, v)) in enumerate(big_w.items()):
        res[n] = tuple(t[None] for t in _adam("adam_" + n, core, w[0], g_own[k], g_sib[k], m[0], v[0]))
    shapes = {"gate_a_w": gate_a_w.shape, "gate_x_w": gate_x_w.shape, "lru_conv_w": lru_conv_w.shape,
              "short_conv_w": short_conv_w.shape, "final_g": final_g.shape}
    for n, t in small.items():
        res[n] = tuple(u.reshape(shapes[n]) if n in shapes else u for u in t)

    order = ["ada_w", "ada_b", "norm1_g", "w_in", "lru_conv_w", "lru_conv_b", "gate_a_w", "gate_a_b", "gate_x_w",
             "gate_x_b", "a_param", "short_conv_w", "lru_out_g", "conv_out_g", "w_out", "norm2_g", "w_mlp1",
             "w_mlp2", "final_g"]
    return (loss, grad_x[None], *[res[n][0] for n in order], *[res[n][1] for n in order],
            *[res[n][2] for n in order], *[res[n][3] for n in order])
```

```python
import jax
import jax.numpy as jnp
from jax import lax
from jax.experimental import pallas as pl
from jax.experimental.pallas import tpu as pltpu

F32 = jnp.float32
BF16 = jnp.bfloat16

D_MODEL = 1024
D_LRU = 512
D_IN = 2560
D_FF = 4096
N_CHIP = 4
WIN_BLK = D_IN // N_CHIP
WOUT_BLK = D_MODEL // N_CHIP
FF_BLK = D_FF // N_CHIP
HEAD = 64
EPS = 1e-6
C_GATE = 8.0
TOKEN_TILE = 256
HALO = 8
VMEM_LIMIT = 60 * 1024 * 1024

ADAM_LR = 0.001
ADAM_B1 = 0.9
ADAM_B2 = 0.999
ADAM_EPS = 1e-08
ADAM_WD = 0.01
ADAM_STEP = 10

MESH = pl.DeviceIdType.MESH
ANY = pl.BlockSpec(memory_space=pl.ANY)
VMEM = pl.BlockSpec(memory_space=pltpu.VMEM)
SMEM = pl.BlockSpec(memory_space=pltpu.SMEM)


def _full(shape, single=False):
    nd = len(shape)
    if single:
        return pl.BlockSpec(shape, lambda *_: (0,) * nd, pipeline_mode=pl.Buffered(1))
    return pl.BlockSpec(shape, lambda *_: (0,) * nd)


def _dot(a, b):
    return jnp.dot(a, b, preferred_element_type=F32)


def _dot_nt(a, b):
    return lax.dot_general(a, b, (((1,), (1,)), ((), ())), preferred_element_type=F32)


def _dot_tn(a, b):
    return lax.dot_general(a, b, (((0,), (0,)), ((), ())), preferred_element_type=F32)


def _gmean(v, a64):
    hi = v.astype(BF16)
    lo = (v - hi.astype(F32)).astype(BF16)
    return _dot(hi, a64) + _dot(lo, a64)


def _gelu(x):
    u = 0.7978845608028654 * (x + 0.044715 * x * x * x)
    t = jnp.tanh(u)
    return 0.5 * x * (1.0 + t), t


def _gelu_grad(x, t):
    du = 0.7978845608028654 * (1.0 + 3.0 * 0.044715 * x * x)
    return 0.5 * (1.0 + t) + 0.5 * x * (1.0 - t * t) * du


def _log1p_pos(y):
    return jnp.where(y < 1e-2, y * (1.0 - y * (0.5 - y * (1.0 / 3.0 - y * 0.25))), jnp.log(1.0 + y))


def _softplus(a):
    return jnp.maximum(a, 0.0) + _log1p_pos(jnp.exp(-jnp.abs(a)))


def _neg_expm1(z):
    series = -z * (1.0 + z * (0.5 + z * (1.0 / 6.0 + z * (1.0 / 24.0 + z * (1.0 / 120.0)))))
    return jnp.where(z > -0.02, series, 1.0 - jnp.exp(z))


def _scan_fwd(a, b, row):
    n = a.shape[0]
    d = 1
    while d < n:
        m = row >= d
        b = jnp.where(m, a * pltpu.roll(b, d, 0) + b, b)
        a = jnp.where(m, a * pltpu.roll(a, d, 0), a)
        d *= 2
    return a, b


def _scan_rev(a, b, row):
    n = a.shape[0]
    d = 1
    while d < n:
        m = row < n - d
        b = jnp.where(m, b + a * pltpu.roll(b, n - d, 0), b)
        a = jnp.where(m, a * pltpu.roll(a, n - d, 0), a)
        d *= 2
    return a, b


def _colsum(v):
    return jnp.sum(v, axis=0, keepdims=True)


def _load_gathered(chip, gathered, own, slot, sems):
    copies = []
    for j in range(N_CHIP):
        @pl.when(chip == j)
        def _(j=j):
            pltpu.make_async_copy(own, slot(j), sems.at[j]).start()

        @pl.when(chip != j)
        def _(j=j):
            pltpu.make_async_copy(gathered.at[j], slot(j), sems.at[j]).start()

        copies.append(pltpu.make_async_copy(own, slot(j), sems.at[j]))
    return copies


def _lru_gates(xlb, gab, gbias, sp, first_row):
    g = _dot(xlb, gab) + gbias
    r = jax.nn.sigmoid(g[:, :D_LRU])
    ig = jax.nn.sigmoid(g[:, D_LRU:])
    la = (-C_GATE) * r * sp
    a = jnp.exp(la)
    msq = jnp.sqrt(_neg_expm1(2.0 * la))
    mult = jnp.where(first_row, 1.0, msq)
    return r, ig, a, msq, mult


def _mix_fwd(chip, x, mod, vecd, vecl, win, wout, gab, a64, mlp_shards):
    s = x.shape[0]
    ts = TOKEN_TILE
    nt = s // ts

    def body(chip_ref, x_ref, mod_ref, vd_ref, vl_ref, win_hbm, win_own, wout_hbm, wout_own, gab_ref, a64_ref,
             w1_own, w2_own,
             hb_ref, proj_ref, hl_ref, ycat_ref, mixed_ref, x1_ref, w1_all, w2_all,
             win_ref, wout_ref, ext_lx, ext_cv, hcar, sems, ag_send, ag_recv):
        i = pl.program_id(0)

        @pl.when(i == 0)
        def _():
            _ag_start((w1_own, w2_own), (w1_all, w2_all), ag_send, ag_recv)
            cps = _load_gathered(chip_ref[0], win_hbm, win_own, lambda j: win_ref.at[j], sems.at[pl.ds(0, N_CHIP)])
            cps += _load_gathered(chip_ref[0], wout_hbm, wout_own,
                                  lambda j: wout_ref.at[pl.ds(j * WOUT_BLK, WOUT_BLK), :],
                                  sems.at[pl.ds(N_CHIP, N_CHIP)])
            ext_lx[0:HALO, :] = jnp.zeros((HALO, D_LRU), F32)
            ext_cv[0:HALO, :] = jnp.zeros((HALO, D_LRU), F32)
            hcar[...] = jnp.zeros_like(hcar)
            for cp in cps:
                cp.wait()

        row = lax.broadcasted_iota(jnp.int32, (ts, D_LRU), 0)
        first_row = jnp.logical_and(row == 0, i == 0)
        xt = x_ref[...]
        shift1, scale1, gate1 = mod_ref[0:1, :], mod_ref[1:2, :], mod_ref[2:3, :]
        r1 = lax.rsqrt(jnp.mean(xt * xt, axis=-1, keepdims=True) + EPS)
        h = (xt * r1) * vd_ref[0:1, :] * (1.0 + scale1) + shift1
        hb = h.astype(BF16)
        hb_ref[...] = hb
        for j in range(N_CHIP):
            proj_ref[:, j * WIN_BLK:(j + 1) * WIN_BLK] = _dot(hb, win_ref[j])
        u_ly = proj_ref[:, 512:1024]
        u_b = proj_ref[:, 1024:1536]

        ext_lx[HALO:HALO + ts, :] = proj_ref[:, 0:512]
        xl = vl_ref[4:5, :] + vl_ref[0:1, :] * ext_lx[pl.ds(5, ts), :]
        for k in range(1, 4):
            xl = xl + vl_ref[k:k + 1, :] * ext_lx[pl.ds(5 + k, ts), :]
        ext_lx[0:HALO, :] = ext_lx[ts:ts + HALO, :]
        sp = _softplus(vl_ref[8:9, :])
        _, ig, a, _, mult = _lru_gates(xl.astype(BF16), gab_ref[...], vd_ref[3:4, :], sp, first_row)
        acum, hloc = _scan_fwd(a, mult * (ig * xl), row)
        hl = hloc + acum * hcar[0:1, :]
        hl_ref[...] = hl
        hcar[0:1, :] = hl_ref[ts - 1:ts, :]
        ge, _ = _gelu(u_ly)
        p = ge * hl
        y_lru = p * lax.rsqrt(_gmean(p * p, a64_ref[...]) + EPS) * vl_ref[9:10, :]
        ycat_ref[:, 0:512] = y_lru.astype(BF16)

        ext_cv[HALO:HALO + ts, :] = proj_ref[:, 1536:2048] * proj_ref[:, 2048:2560]
        q = vl_ref[5:6, :] * ext_cv[pl.ds(6, ts), :]
        for k in range(1, 3):
            q = q + vl_ref[5 + k:6 + k, :] * ext_cv[pl.ds(6 + k, ts), :]
        ext_cv[0:HALO, :] = ext_cv[ts:ts + HALO, :]
        yc = u_b * q
        y_conv = yc * lax.rsqrt(_gmean(yc * yc, a64_ref[...]) + EPS) * vl_ref[10:11, :]
        ycat_ref[:, 512:1024] = y_conv.astype(BF16)

        mixed = _dot(ycat_ref[...], wout_ref[...])
        mixed_ref[...] = mixed
        x1_ref[...] = xt + gate1 * mixed

        @pl.when(i == nt - 1)
        def _():
            _ag_finish((w1_own, w2_own), (w1_all, w2_all), ag_send, ag_recv)

    tile = lambda w: pl.BlockSpec((ts, w), lambda i: (i, 0))
    return pl.pallas_call(
        body, name="mix_fwd", grid=(nt,),
        in_specs=[SMEM, tile(D_MODEL), _full((8, D_MODEL)), _full((8, D_MODEL)), _full((16, D_LRU)),
                  ANY, ANY, ANY, ANY, _full((D_LRU, 2 * D_LRU), True), _full((D_LRU, D_LRU), True), ANY, ANY],
        out_specs=[tile(D_MODEL), tile(D_IN), tile(D_LRU), tile(D_MODEL), tile(D_MODEL), tile(D_MODEL), ANY, ANY],
        out_shape=[jax.ShapeDtypeStruct((s, D_MODEL), BF16), jax.ShapeDtypeStruct((s, D_IN), F32),
                   jax.ShapeDtypeStruct((s, D_LRU), F32), jax.ShapeDtypeStruct((s, D_MODEL), BF16),
                   jax.ShapeDtypeStruct((s, D_MODEL), F32), jax.ShapeDtypeStruct((s, D_MODEL), F32)]
        + [jax.ShapeDtypeStruct((N_CHIP,) + w.shape, w.dtype) for w in mlp_shards],
        scratch_shapes=[pltpu.VMEM((N_CHIP, D_MODEL, WIN_BLK), BF16), pltpu.VMEM((D_MODEL, D_MODEL), BF16),
                        pltpu.VMEM((ts + HALO, D_LRU), F32), pltpu.VMEM((ts + HALO, D_LRU), F32),
                        pltpu.VMEM((HALO, D_LRU), F32), pltpu.SemaphoreType.DMA((2 * N_CHIP,)),
                        pltpu.SemaphoreType.DMA((12,)), pltpu.SemaphoreType.DMA((12,))],
        compiler_params=pltpu.CompilerParams(dimension_semantics=("arbitrary",), vmem_limit_bytes=VMEM_LIMIT),
    )(chip, x, mod, vecd, vecl, *win, *wout, gab, a64, *mlp_shards)


def _mlp_fwd_bwd(chip, x1, target, mod, vecd, w1, w2):
    s = x1.shape[0]
    ts = TOKEN_TILE
    nt = s // ts

    def body(chip_ref, x1_ref, tg_ref, mod_ref, vd_ref, w1_hbm, w1_own, w2_hbm, w2_own,
             dx1_ref, act_ref, dz_ref, dmo_ref, h2_ref, acc_ref, w1_v, w2_v, rz_v, sems):
        i = pl.program_id(0)

        @pl.when(i == 0)
        def _():
            cps = _load_gathered(chip_ref[0], w1_hbm, w1_own, lambda j: w1_v.at[j], sems.at[pl.ds(0, N_CHIP)])
            cps += _load_gathered(chip_ref[0], w2_hbm, w2_own, lambda j: w2_v.at[j], sems.at[pl.ds(N_CHIP, N_CHIP)])
            acc_ref[...] = jnp.zeros_like(acc_ref)
            for cp in cps:
                cp.wait()

        xt = x1_ref[...]
        shift2, scale2, gate2 = mod_ref[3:4, :], mod_ref[4:5, :], mod_ref[5:6, :]
        g2, gf = vd_ref[1:2, :], vd_ref[2:3, :]
        r2 = lax.rsqrt(jnp.mean(xt * xt, axis=-1, keepdims=True) + EPS)
        n2 = xt * r2
        h2b = (n2 * g2 * (1.0 + scale2) + shift2).astype(BF16)
        h2_ref[...] = h2b
        mo = jnp.zeros((ts, D_MODEL), F32)
        for j in range(N_CHIP):
            rz = jnp.maximum(_dot(h2b, w1_v[j]), 0.0)
            rz_v[j] = rz
            actb = (rz * rz).astype(BF16)
            act_ref[:, j * FF_BLK:(j + 1) * FF_BLK] = actb
            mo = mo + _dot(actb, w2_v[j])
        x2 = xt + gate2 * mo
        r3 = lax.rsqrt(jnp.mean(x2 * x2, axis=-1, keepdims=True) + EPS)
        n3 = x2 * r3
        e = n3 * gf - tg_ref[...]
        loss = (0.5 / D_MODEL) * jnp.sum(_colsum(e * e), axis=1, keepdims=True)
        dy = e * (1.0 / D_MODEL)
        acc_ref[4:5, :] += _colsum(dy * n3)
        acc_ref[5:6, :] += jnp.broadcast_to(loss, (1, D_MODEL))
        dn3 = dy * gf
        dx2 = r3 * (dn3 - n3 * jnp.mean(dn3 * n3, axis=-1, keepdims=True))
        acc_ref[2:3, :] += _colsum(dx2 * mo)
        dmob = (dx2 * gate2).astype(BF16)
        dmo_ref[...] = dmob
        dh2 = jnp.zeros((ts, D_MODEL), F32)
        for j in range(N_CHIP):
            dzb = (_dot_nt(dmob, w2_v[j]) * (2.0 * rz_v[j])).astype(BF16)
            dz_ref[:, j * FF_BLK:(j + 1) * FF_BLK] = dzb
            dh2 = dh2 + _dot_nt(dzb, w1_v[j])
        acc_ref[1:2, :] += _colsum(dh2 * (n2 * g2))
        acc_ref[0:1, :] += _colsum(dh2)
        dhn2 = dh2 * (1.0 + scale2)
        acc_ref[3:4, :] += _colsum(dhn2 * n2)
        dn2 = dhn2 * g2
        dx1_ref[...] = dx2 + r2 * (dn2 - n2 * jnp.mean(dn2 * n2, axis=-1, keepdims=True))

    tile = lambda w: pl.BlockSpec((ts, w), lambda i: (i, 0))
    return pl.pallas_call(
        body, name="mlp_fwd_bwd", grid=(nt,),
        in_specs=[SMEM, tile(D_MODEL), tile(D_MODEL), _full((8, D_MODEL)), _full((8, D_MODEL)), ANY, ANY, ANY, ANY],
        out_specs=[tile(D_MODEL), tile(D_FF), tile(D_FF), tile(D_MODEL), tile(D_MODEL), _full((8, D_MODEL))],
        out_shape=[jax.ShapeDtypeStruct((s, D_MODEL), F32), jax.ShapeDtypeStruct((s, D_FF), BF16),
                   jax.ShapeDtypeStruct((s, D_FF), BF16), jax.ShapeDtypeStruct((s, D_MODEL), BF16),
                   jax.ShapeDtypeStruct((s, D_MODEL), BF16), jax.ShapeDtypeStruct((8, D_MODEL), F32)],
        scratch_shapes=[pltpu.VMEM((N_CHIP, D_MODEL, FF_BLK), BF16), pltpu.VMEM((N_CHIP, FF_BLK, D_MODEL), BF16),
                        pltpu.VMEM((N_CHIP, ts, FF_BLK), F32), pltpu.SemaphoreType.DMA((2 * N_CHIP,))],
        compiler_params=pltpu.CompilerParams(dimension_semantics=("arbitrary",), vmem_limit_bytes=VMEM_LIMIT),
    )(chip, x1, target, mod, vecd, *w1, *w2)


def _mix_bwd(chip, dx1, x, mixed, proj, hl, mod, vecd, vecl, win, wout, gab, a64, mlp_parts):
    s = x.shape[0]
    ts = TOKEN_TILE
    nt = s // ts
    hpt = ts // HALO

    def body(chip_ref, dx1_ref, x_ref, mixed_ref, proj_ref, projh_ref, hl_ref, hlh_ref, mod_ref, vd_ref, vl_ref,
             win_hbm, win_own, wout_hbm, wout_own, gab_ref, a64_ref, p1_ref, p2_ref,
             gx_ref, dproj_ref, dmixed_ref, xlb_ref, dgb_ref, accd_ref, accl_ref, q1_ref, q2_ref,
             win_ref, wout_ref, ext_lx, ext_cv, ext_hl, ext_dxl, ext_dq, gbuf, gcar, acar, sems, x_send, x_recv):
        i = pl.program_id(0)
        ri = nt - 1 - i

        @pl.when(i == 0)
        def _():
            for cp in _xchg_copies((p1_ref, p2_ref), (q1_ref, q2_ref), x_send, x_recv):
                cp.start()
            cps = _load_gathered(chip_ref[0], win_hbm, win_own, lambda j: win_ref.at[j], sems.at[pl.ds(0, N_CHIP)])
            cps += _load_gathered(chip_ref[0], wout_hbm, wout_own,
                                  lambda j: wout_ref.at[pl.ds(j * WOUT_BLK, WOUT_BLK), :],
                                  sems.at[pl.ds(N_CHIP, N_CHIP)])
            for cp in cps:
                cp.wait()
            accd_ref[...] = jnp.zeros_like(accd_ref)
            accl_ref[...] = jnp.zeros_like(accl_ref)
            ext_dxl[ts:ts + HALO, :] = jnp.zeros((HALO, D_LRU), F32)
            ext_dq[ts:ts + HALO, :] = jnp.zeros((HALO, D_LRU), F32)
            gcar[...] = jnp.zeros_like(gcar)
            acar[...] = jnp.zeros_like(acar)

        row = lax.broadcasted_iota(jnp.int32, (ts, D_LRU), 0)
        first_row = jnp.logical_and(row == 0, ri == 0)
        halo_on = jnp.where(ri == 0, 0.0, 1.0)
        shift1, scale1, gate1 = mod_ref[0:1, :], mod_ref[1:2, :], mod_ref[2:3, :]
        g1 = vd_ref[0:1, :]
        a64m = a64_ref[...]
        lg, cg = vl_ref[9:10, :], vl_ref[10:11, :]

        dx1 = dx1_ref[...]
        accd_ref[2:3, :] += _colsum(dx1 * mixed_ref[...])
        dmb = (dx1 * gate1).astype(BF16)
        dmixed_ref[...] = dmb
        dycat = _dot_nt(dmb, wout_ref[...])
        dyl = dycat[:, 0:512]
        dyv = dycat[:, 512:1024]

        u_ly = proj_ref[:, 512:1024]
        u_b = proj_ref[:, 1024:1536]
        u_c = proj_ref[:, 1536:2048]
        u_v = proj_ref[:, 2048:2560]
        ext_lx[0:HALO, :] = projh_ref[:, 0:512] * halo_on
        ext_lx[HALO:HALO + ts, :] = proj_ref[:, 0:512]
        xl = vl_ref[4:5, :] + vl_ref[0:1, :] * ext_lx[pl.ds(5, ts), :]
        for k in range(1, 4):
            xl = xl + vl_ref[k:k + 1, :] * ext_lx[pl.ds(5 + k, ts), :]
        xlb = xl.astype(BF16)
        xlb_ref[...] = xlb
        sp = _softplus(vl_ref[8:9, :])
        r, ig, a, msq, mult = _lru_gates(xlb, gab_ref[...], vd_ref[3:4, :], sp, first_row)
        hl = hl_ref[...]
        ge, th = _gelu(u_ly)
        p = ge * hl
        rl = lax.rsqrt(_gmean(p * p, a64m) + EPS)
        nl = p * rl
        ext_cv[0:HALO, :] = projh_ref[:, 1536:2048] * projh_ref[:, 2048:2560] * halo_on
        ext_cv[HALO:HALO + ts, :] = u_c * u_v
        q = vl_ref[5:6, :] * ext_cv[pl.ds(6, ts), :]
        for k in range(1, 3):
            q = q + vl_ref[5 + k:6 + k, :] * ext_cv[pl.ds(6 + k, ts), :]
        yc = u_b * q
        rc = lax.rsqrt(_gmean(yc * yc, a64m) + EPS)
        nc = yc * rc

        accl_ref[9:10, :] += _colsum(dyl * nl)
        dnl = dyl * lg
        dp = rl * (dnl - nl * _gmean(dnl * nl, a64m))
        dproj_ref[:, 512:1024] = ((dp * hl) * _gelu_grad(u_ly, th)).astype(BF16)
        a_next = jnp.where(row == ts - 1, acar[0:1, :], pltpu.roll(a, ts - 1, 0))
        acum, gloc = _scan_rev(a_next, dp * ge, row)
        gbuf[...] = gloc + acum * gcar[0:1, :]
        gcar[0:1, :] = gbuf[0:1, :]
        ext_hl[0:HALO, :] = hlh_ref[...] * halo_on
        ext_hl[HALO:HALO + ts, :] = hl
        acar[...] = a[0:HALO, :]
        gt = gbuf[...]
        da = gt * ext_hl[pl.ds(HALO - 1, ts), :]
        dmult = gt * ig * xl
        di = gt * mult * xl
        dxl = gt * mult * ig
        dla = da * a - jnp.where(first_row, 0.0, dmult * a * a / msq)
        accl_ref[8:9, :] += _colsum(dla * ((-C_GATE) * r))
        dra = dla * ((-C_GATE) * sp) * r * (1.0 - r)
        dia = di * ig * (1.0 - ig)
        accd_ref[4:5, 0:D_LRU] += _colsum(dra)
        accd_ref[4:5, D_LRU:2 * D_LRU] += _colsum(dia)
        dgb_ref[:, 0:D_LRU] = dra.astype(BF16)
        dgb_ref[:, D_LRU:2 * D_LRU] = dia.astype(BF16)
        dxl = dxl + _dot_nt(dgb_ref[...], gab_ref[...])
        accl_ref[4:5, :] += _colsum(dxl)
        for k in range(4):
            accl_ref[k:k + 1, :] += _colsum(dxl * ext_lx[pl.ds(5 + k, ts), :])
        ext_dxl[0:ts, :] = dxl
        du_lx = vl_ref[0:1, :] * ext_dxl[pl.ds(3, ts), :]
        for k in range(1, 4):
            du_lx = du_lx + vl_ref[k:k + 1, :] * ext_dxl[pl.ds(3 - k, ts), :]
        ext_dxl[ts:ts + HALO, :] = ext_dxl[0:HALO, :]
        dproj_ref[:, 0:512] = du_lx.astype(BF16)

        accl_ref[10:11, :] += _colsum(dyv * nc)
        dnc = dyv * cg
        dyc = rc * (dnc - nc * _gmean(dnc * nc, a64m))
        dproj_ref[:, 1024:1536] = (dyc * q).astype(BF16)
        dq = dyc * u_b
        for k in range(3):
            accl_ref[5 + k:6 + k, :] += _colsum(dq * ext_cv[pl.ds(6 + k, ts), :])
        ext_dq[0:ts, :] = dq
        dcv = vl_ref[5:6, :] * ext_dq[pl.ds(2, ts), :]
        for k in range(1, 3):
            dcv = dcv + vl_ref[5 + k:6 + k, :] * ext_dq[pl.ds(2 - k, ts), :]
        ext_dq[ts:ts + HALO, :] = ext_dq[0:HALO, :]
        dproj_ref[:, 1536:2048] = (dcv * u_v).astype(BF16)
        dproj_ref[:, 2048:2560] = (dcv * u_c).astype(BF16)

        dh = _dot_nt(dproj_ref[:, 0:WIN_BLK], win_ref[0])
        for j in range(1, N_CHIP):
            dh = dh + _dot_nt(dproj_ref[:, j * WIN_BLK:(j + 1) * WIN_BLK], win_ref[j])
        xt = x_ref[...]
        r1 = lax.rsqrt(jnp.mean(xt * xt, axis=-1, keepdims=True) + EPS)
        n1 = xt * r1
        accd_ref[1:2, :] += _colsum(dh * (n1 * g1))
        accd_ref[0:1, :] += _colsum(dh)
        dhn1 = dh * (1.0 + scale1)
        accd_ref[3:4, :] += _colsum(dhn1 * n1)
        dn1 = dhn1 * g1
        gx_ref[...] = dx1 + r1 * (dn1 - n1 * jnp.mean(dn1 * n1, axis=-1, keepdims=True))

        @pl.when(i == nt - 1)
        def _():
            for cp in _xchg_copies((p1_ref, p2_ref), (q1_ref, q2_ref), x_send, x_recv):
                cp.wait()

    tile = lambda w: pl.BlockSpec((ts, w), lambda i: (nt - 1 - i, 0))
    halo = lambda w: pl.BlockSpec((HALO, w), lambda i: (jnp.maximum((nt - 1 - i) * hpt - 1, 0), 0))
    ext = pltpu.VMEM((ts + HALO, D_LRU), F32)
    return pl.pallas_call(
        body, name="mix_bwd", grid=(nt,),
        in_specs=[SMEM, tile(D_MODEL), tile(D_MODEL), tile(D_MODEL), tile(D_IN), halo(D_IN), tile(D_LRU), halo(D_LRU),
                  _full((8, D_MODEL)), _full((8, D_MODEL)), _full((16, D_LRU)),
                  ANY, ANY, ANY, ANY, _full((D_LRU, 2 * D_LRU), True), _full((D_LRU, D_LRU), True), ANY, ANY],
        out_specs=[tile(D_MODEL), tile(D_IN), tile(D_MODEL), tile(D_LRU), tile(2 * D_LRU),
                   _full((8, D_MODEL)), _full((16, D_LRU)), ANY, ANY],
        out_shape=[jax.ShapeDtypeStruct((s, D_MODEL), F32), jax.ShapeDtypeStruct((s, D_IN), BF16),
                   jax.ShapeDtypeStruct((s, D_MODEL), BF16), jax.ShapeDtypeStruct((s, D_LRU), BF16),
                   jax.ShapeDtypeStruct((s, 2 * D_LRU), BF16),
                   jax.ShapeDtypeStruct((8, D_MODEL), F32), jax.ShapeDtypeStruct((16, D_LRU), F32)]
        + [jax.ShapeDtypeStruct((3,) + p.shape[1:], p.dtype) for p in mlp_parts],
        scratch_shapes=[pltpu.VMEM((N_CHIP, D_MODEL, WIN_BLK), BF16), pltpu.VMEM((D_MODEL, D_MODEL), BF16),
                        ext, ext, ext, ext, ext, pltpu.VMEM((ts, D_LRU), F32),
                        pltpu.VMEM((HALO, D_LRU), F32), pltpu.VMEM((HALO, D_LRU), F32),
                        pltpu.SemaphoreType.DMA((2 * N_CHIP,)),
                        pltpu.SemaphoreType.DMA((6,)), pltpu.SemaphoreType.DMA((6,))],
        compiler_params=pltpu.CompilerParams(dimension_semantics=("arbitrary",), vmem_limit_bytes=VMEM_LIMIT),
    )(chip, dx1, x, mixed, proj, proj, hl, hl, mod, vecd, vecl, *win, *wout, gab, a64, *mlp_parts)


def _wgrad(name, a, b, a_blk, b_blk, out_dtype):
    s = a.shape[0]
    aw = a_blk or a.shape[1]
    bw = b_blk or b.shape[1]
    nblk = N_CHIP if (a_blk or b_blk) else 1

    def body(a_ref, b_ref, o_ref):
        o_ref[0] = _dot_tn(a_ref[...], b_ref[...]).astype(out_dtype)

    return pl.pallas_call(
        body, name=name, grid=(nblk,),
        in_specs=[pl.BlockSpec((s, aw), (lambda j: (0, j)) if a_blk else (lambda j: (0, 0))),
                  pl.BlockSpec((s, bw), (lambda j: (0, j)) if b_blk else (lambda j: (0, 0)))],
        out_specs=pl.BlockSpec((1, aw, bw), lambda j: (j, 0, 0)),
        out_shape=jax.ShapeDtypeStruct((nblk, aw, bw), out_dtype),
        compiler_params=pltpu.CompilerParams(dimension_semantics=("arbitrary",), vmem_limit_bytes=VMEM_LIMIT),
    )(a, b)


def _mod_matmul(c_all, ada_w_loc):
    n = ada_w_loc.shape[1]
    cb = 512

    def body(c_ref, w_ref, o_ref):
        c = c_ref[...]
        sc = c * jax.nn.sigmoid(c)
        o_ref[...] = _dot(sc.astype(BF16), w_ref[...].astype(BF16))

    return pl.pallas_call(
        body, name="mod_matmul", grid=(n // cb,),
        in_specs=[_full((8, D_MODEL)), pl.BlockSpec((D_MODEL, cb), lambda j: (0, j))],
        out_specs=pl.BlockSpec((8, cb), lambda j: (0, j)),
        out_shape=jax.ShapeDtypeStruct((8, n), F32),
        compiler_params=pltpu.CompilerParams(dimension_semantics=("arbitrary",), vmem_limit_bytes=VMEM_LIMIT),
    )(c_all, ada_w_loc)


def _adam_math(w, g, m, v):
    m = ADAM_B1 * m + (1.0 - ADAM_B1) * g
    v = ADAM_B2 * v + (1.0 - ADAM_B2) * (g * g)
    m_hat = m / (1.0 - ADAM_B1 ** ADAM_STEP)
    v_hat = v / (1.0 - ADAM_B2 ** ADAM_STEP)
    delta = (-ADAM_LR) * (m_hat / (jnp.sqrt(v_hat) + ADAM_EPS) + ADAM_WD * w)
    return delta, m, v


def _adam(name, core, w, g_own, g_sib, m, v):
    r, c = w.shape
    half = r // 2
    rb = min(half, 256)
    nh = half // rb

    def body(core_ref, w_ref, go_ref, gs_ref, m_ref, v_ref, g_ref, d_ref, mo_ref, vo_ref):
        mine = (pl.program_id(0) // nh) == core_ref[0]
        g = jnp.where(mine, go_ref[...], gs_ref[...])
        g_ref[...] = g
        d_ref[...], mo_ref[...], vo_ref[...] = _adam_math(w_ref[...], g, m_ref[...], v_ref[...])

    spec = pl.BlockSpec((rb, c), lambda i, core_ref: (i, 0))
    hspec = pl.BlockSpec((rb, c), lambda i, core_ref: (i % nh, 0))
    sds = jax.ShapeDtypeStruct((r, c), F32)
    return pl.pallas_call(
        body, name=name,
        grid_spec=pltpu.PrefetchScalarGridSpec(
            num_scalar_prefetch=1, grid=(r // rb,),
            in_specs=[spec, hspec, hspec, spec, spec], out_specs=[spec] * 4),
        out_shape=[sds] * 4,
        compiler_params=pltpu.CompilerParams(dimension_semantics=("arbitrary",), vmem_limit_bytes=VMEM_LIMIT),
    )(core, w, g_own, g_sib, m, v)


def _ada_grad_adam(sct, dmod_loc, w, m, v):
    r, c = w.shape
    rb = 128

    def body(s_ref, dm_ref, w_ref, m_ref, v_ref, g_ref, d_ref, mo_ref, vo_ref):
        g = s_ref[:, 0:1] * dm_ref[0:1, :]
        for b in range(1, 8):
            g = g + s_ref[:, b:b + 1] * dm_ref[b:b + 1, :]
        g_ref[...] = g
        d_ref[...], mo_ref[...], vo_ref[...] = _adam_math(w_ref[...], g, m_ref[...], v_ref[...])

    spec = pl.BlockSpec((rb, c), lambda i: (i, 0))
    sds = jax.ShapeDtypeStruct((r, c), F32)
    return pl.pallas_call(
        body, name="ada_grad_adam", grid=(r // rb,),
        in_specs=[pl.BlockSpec((rb, 8), lambda i: (i, 0)), _full((8, c)), spec, spec, spec],
        out_specs=[spec] * 4, out_shape=[sds] * 4,
        compiler_params=pltpu.CompilerParams(dimension_semantics=("arbitrary",), vmem_limit_bytes=VMEM_LIMIT),
    )(sct, dmod_loc, w, m, v)


def _position():
    x, y, c = lax.axis_index("x"), lax.axis_index("y"), lax.axis_index("c")
    chips = [(1 - x, y), (x, 1 - y), (1 - x, 1 - y)]
    return x, y, c, chips


def _allgather8(name, arrs):
    na = len(arrs)

    def body(*refs):
        ins, outs = refs[:na], refs[na:2 * na]
        send_sems, recv_sems, local_sems = refs[2 * na:]
        x, y, c, chips = _position()
        me, sibling = (x, y, c), (x, y, 1 - c)
        first, passed, local = [], [], []
        for a in range(na):
            m_per = ins[a].shape[0]

            def rows(px, py, pc, a=a, m_per=m_per):
                return outs[a].at[pl.ds((4 * px + 2 * py + pc) * m_per, m_per), :]

            def copy(k, block, to, src=None, a=a, rows=rows):
                return pltpu.make_async_remote_copy(
                    src_ref=rows(*block) if src is None else src, dst_ref=rows(*block),
                    send_sem=send_sems.at[7 * a + k], recv_sem=recv_sems.at[7 * a + k],
                    device_id=to, device_id_type=MESH)

            mine = pltpu.make_async_copy(ins[a], rows(*me), local_sems.at[a])
            mine.start()
            local.append(mine)
            f = [copy(0, me, sibling, src=ins[a])]
            f += [copy(1 + j, me, (*chip, c), src=ins[a]) for j, chip in enumerate(chips)]
            for cp in f:
                cp.start()
            first.append((f, copy))
        for a in range(na):
            f, copy = first[a]
            p = [copy(4 + j, (*chip, c), sibling) for j, chip in enumerate(chips)]
            for j, chip in enumerate(chips):
                copy(1 + j, (*chip, c), me).wait_recv()
                p[j].start()
            passed.append(p)
        for a in range(na):
            f, copy = first[a]
            copy(0, sibling, me).wait_recv()
            for j, chip in enumerate(chips):
                copy(4 + j, (*chip, 1 - c), me).wait_recv()
            for cp in f + passed[a]:
                cp.wait_send()
            local[a].wait()

    return pl.pallas_call(
        body, name=name,
        out_shape=[jax.ShapeDtypeStruct((8 * a.shape[0], a.shape[1]), a.dtype) for a in arrs],
        in_specs=[VMEM] * na, out_specs=[VMEM] * na,
        scratch_shapes=[pltpu.SemaphoreType.DMA((7 * na,)), pltpu.SemaphoreType.DMA((7 * na,)),
                        pltpu.SemaphoreType.DMA((na,))],
        compiler_params=pltpu.CompilerParams(vmem_limit_bytes=VMEM_LIMIT),
    )(*arrs)


def _ag_copies(ins, outs, send_sems, recv_sems):
    x, y, c, chips = _position()
    sibling = (x, y, 1 - c)
    mychip = 2 * x + y
    res = []
    for a in range(len(ins)):
        half = ins[a].shape[0] // 2

        def copy(k, chip, pc, to, src=None, a=a, half=half):
            dst = outs[a].at[chip, pl.ds(pc * half, half), :]
            return pltpu.make_async_remote_copy(
                src_ref=dst if src is None else src, dst_ref=dst,
                send_sem=send_sems.at[6 * a + k], recv_sem=recv_sems.at[6 * a + k],
                device_id=to, device_id_type=MESH)

        src = ins[a].at[pl.ds(c * half, half), :]
        peer = [2 * chip[0] + chip[1] for chip in chips]
        res.append(([copy(j, mychip, c, (*chip, c), src=src) for j, chip in enumerate(chips)],
                    [copy(j, peer[j], c, sibling) for j in range(3)],
                    [copy(3 + j, peer[j], c, sibling) for j in range(3)],
                    [copy(3 + j, peer[j], 1 - c, sibling) for j in range(3)]))
    return res


def _ag_start(ins, outs, send_sems, recv_sems):
    for sends, _, _, _ in _ag_copies(ins, outs, send_sems, recv_sems):
        for cp in sends:
            cp.start()


def _ag_finish(ins, outs, send_sems, recv_sems):
    copies = _ag_copies(ins, outs, send_sems, recv_sems)
    for _, arrivals, forwards, _ in copies:
        for j in range(3):
            arrivals[j].wait_recv()
            forwards[j].start()
    for sends, _, forwards, forwarded in copies:
        for cp in forwarded:
            cp.wait_recv()
        for cp in sends + forwards:
            cp.wait_send()


def _allgather_weights(shards):
    na = len(shards)

    def body(*refs):
        ins, outs = refs[:na], refs[na:2 * na]
        send_sems, recv_sems = refs[2 * na:]
        _ag_start(ins, outs, send_sems, recv_sems)
        _ag_finish(ins, outs, send_sems, recv_sems)

    return pl.pallas_call(
        body, name="allgather_weights",
        out_shape=[jax.ShapeDtypeStruct((N_CHIP,) + s.shape, s.dtype) for s in shards],
        in_specs=[ANY] * na, out_specs=[ANY] * na,
        scratch_shapes=[pltpu.SemaphoreType.DMA((6 * na,)), pltpu.SemaphoreType.DMA((6 * na,))],
    )(*shards)


def _swap_halves(name, grads):
    na = len(grads)

    def body(*refs):
        ins, outs = refs[:na], refs[na:2 * na]
        send_sems, recv_sems = refs[2 * na:]
        x, y, c, _ = _position()
        cps = []
        for a in range(na):
            half = ins[a].shape[1] // 2
            cp = pltpu.make_async_remote_copy(
                src_ref=ins[a].at[:, pl.ds((1 - c) * half, half), :], dst_ref=outs[a],
                send_sem=send_sems.at[a], recv_sem=recv_sems.at[a],
                device_id=(x, y, 1 - c), device_id_type=MESH)
            cp.start()
            cps.append(cp)
        for cp in cps:
            cp.wait()

    return pl.pallas_call(
        body, name=name,
        out_shape=[jax.ShapeDtypeStruct((g.shape[0], g.shape[1] // 2, g.shape[2]), g.dtype) for g in grads],
        in_specs=[ANY] * na, out_specs=[ANY] * na,
        scratch_shapes=[pltpu.SemaphoreType.DMA((na,)), pltpu.SemaphoreType.DMA((na,))],
    )(*grads)


def _xchg_copies(ins, outs, send_sems, recv_sems):
    x, y, c, chips = _position()
    return [pltpu.make_async_remote_copy(
        src_ref=ins[a].at[2 * chip[0] + chip[1]], dst_ref=outs[a].at[j],
        send_sem=send_sems.at[3 * a + j], recv_sem=recv_sems.at[3 * a + j],
        device_id=(*chip, c), device_id_type=MESH) for a in range(len(ins)) for j, chip in enumerate(chips)]


def _exchange_chips(parts):
    na = len(parts)

    def body(*refs):
        ins, outs = refs[:na], refs[na:2 * na]
        send_sems, recv_sems = refs[2 * na:]
        for cp in _xchg_copies(ins, outs, send_sems, recv_sems):
            cp.start()
        for cp in _xchg_copies(ins, outs, send_sems, recv_sems):
            cp.wait()

    return pl.pallas_call(
        body, name="rs_exchange_chips",
        out_shape=[jax.ShapeDtypeStruct((3,) + p.shape[1:], p.dtype) for p in parts],
        in_specs=[ANY] * na, out_specs=[ANY] * na,
        scratch_shapes=[pltpu.SemaphoreType.DMA((3 * na,)), pltpu.SemaphoreType.DMA((3 * na,))],
    )(*parts)


def _swap_reduced(halves):
    na = len(halves)

    def body(*refs):
        ins, outs = refs[:na], refs[na:2 * na]
        send_sems, recv_sems = refs[2 * na:]
        x, y, c, _ = _position()
        cps = []
        for a in range(na):
            cp = pltpu.make_async_remote_copy(
                src_ref=ins[a], dst_ref=outs[a], send_sem=send_sems.at[a], recv_sem=recv_sems.at[a],
                device_id=(x, y, 1 - c), device_id_type=MESH)
            cp.start()
            cps.append(cp)
        for cp in cps:
            cp.wait()

    return pl.pallas_call(
        body, name="rs_swap_reduced",
        out_shape=[jax.ShapeDtypeStruct(h.shape, h.dtype) for h in halves],
        in_specs=[ANY] * na, out_specs=[ANY] * na,
        scratch_shapes=[pltpu.SemaphoreType.DMA((na,)), pltpu.SemaphoreType.DMA((na,))],
    )(*halves)


def _add_sibling(name, grad, recv, core):
    _, r, c = grad.shape
    half = r // 2
    rb = min(half, 256)
    nrb = half // rb

    def body(core_ref, g_ref, r_ref, o_ref):
        o_ref[...] = (g_ref[...].astype(F32) + r_ref[...].astype(F32)).astype(BF16)

    return pl.pallas_call(
        body, name=name,
        grid_spec=pltpu.PrefetchScalarGridSpec(
            num_scalar_prefetch=1, grid=(N_CHIP, nrb),
            in_specs=[pl.BlockSpec((1, rb, c), lambda j, i, core_ref: (j, core_ref[0] * nrb + i, 0)),
                      pl.BlockSpec((1, rb, c), lambda j, i, core_ref: (j, i, 0))],
            out_specs=pl.BlockSpec((1, rb, c), lambda j, i, core_ref: (j, i, 0))),
        out_shape=jax.ShapeDtypeStruct((N_CHIP, half, c), BF16),
        compiler_params=pltpu.CompilerParams(dimension_semantics=("arbitrary", "arbitrary"),
                                             vmem_limit_bytes=VMEM_LIMIT),
    )(core, grad, recv)


def _add_chips(name, chip, p, q):
    _, half, c = q.shape
    rb = min(half, 256)

    def body(chip_ref, p_ref, q_ref, o_ref):
        acc = p_ref[0].astype(F32)
        for j in range(3):
            acc = acc + q_ref[j].astype(F32)
        o_ref[...] = acc

    return pl.pallas_call(
        body, name=name,
        grid_spec=pltpu.PrefetchScalarGridSpec(
            num_scalar_prefetch=1, grid=(half // rb,),
            in_specs=[pl.BlockSpec((1, rb, c), lambda i, chip_ref: (chip_ref[0], i, 0)),
                      pl.BlockSpec((3, rb, c), lambda i, chip_ref: (0, i, 0))],
            out_specs=pl.BlockSpec((rb, c), lambda i, chip_ref: (i, 0))),
        out_shape=jax.ShapeDtypeStruct((half, c), F32),
        compiler_params=pltpu.CompilerParams(dimension_semantics=("arbitrary",), vmem_limit_bytes=VMEM_LIMIT),
    )(chip, p, q)


def _small_update(gad, gam, gl, gga, ggx, mychip, params):
    names = ["ada_b", "norm1_g", "lru_conv_b", "gate_a_w", "gate_a_b", "gate_x_w", "gate_x_b", "a_param",
             "lru_conv_w", "short_conv_w", "lru_out_g", "conv_out_g", "norm2_g", "final_g"]
    flat = [t for n in names for t in params[n]]
    nin = len(flat)

    def body(chip_ref, gad_ref, gam_ref, gl_ref, gga_ref, ggx_ref, *refs):
        ins = {n: refs[3 * k:3 * k + 3] for k, n in enumerate(names)}
        outs = {n: refs[nin + 4 * k:nin + 4 * k + 4] for k, n in enumerate(names)}
        loss_ref, dmod_ref = refs[nin + 4 * len(names):nin + 4 * len(names) + 2]

        def dsum(ref, lo, n):
            acc = ref[lo:lo + n, :]
            for dev in range(1, 8):
                acc = acc + ref[dev * (ref.shape[0] // 8) + lo:dev * (ref.shape[0] // 8) + lo + n, :]
            return acc

        def update(n, g):
            w_ref, m_ref, v_ref = ins[n]
            g_ref, d_ref, mo_ref, vo_ref = outs[n]
            g_ref[...] = g
            d_ref[...], mo_ref[...], vo_ref[...] = _adam_math(w_ref[...], g, m_ref[...], v_ref[...])

        d, dm, l, lw = refs[-4:]
        d[...] = dsum(gad_ref, 0, 8)
        dm[...] = dsum(gam_ref, 0, 8)
        l[...] = dsum(gl_ref, 0, 16)
        for dev in range(8):
            for k in range(3):
                dmod_ref[dev:dev + 1, k * D_MODEL:(k + 1) * D_MODEL] = gad_ref[dev * 8 + k:dev * 8 + k + 1, :]
                dmod_ref[dev:dev + 1, (3 + k) * D_MODEL:(4 + k) * D_MODEL] = gam_ref[dev * 8 + k:dev * 8 + k + 1, :]
        w_ref, m_ref, v_ref = ins["ada_b"]
        g_ref, d_ref, mo_ref, vo_ref = outs["ada_b"]
        for k in range(3):
            g_ref[:, k * D_MODEL:(k + 1) * D_MODEL] = d[k:k + 1, :]
            g_ref[:, (3 + k) * D_MODEL:(4 + k) * D_MODEL] = dm[k:k + 1, :]
        d_ref[...], mo_ref[...], vo_ref[...] = _adam_math(w_ref[...], g_ref[...], m_ref[...], v_ref[...])
        update("norm1_g", d[3:4, :])
        update("norm2_g", dm[3:4, :])
        update("final_g", dm[4:5, :])
        update("gate_a_b", d[4:5, 0:D_LRU])
        update("gate_x_b", d[4:5, D_LRU:2 * D_LRU])
        update("lru_conv_b", l[4:5, :])
        update("a_param", l[8:9, :] * jax.nn.sigmoid(ins["a_param"][0][...]))
        update("lru_out_g", l[9:10, :])
        update("conv_out_g", l[10:11, :])
        loss_ref[...] = jnp.broadcast_to(dm[5:6, 0:128], (8, 128))
        chip = chip_ref[0]
        acc = jnp.zeros((8, 128), F32)
        for j in range(N_CHIP):
            acc = acc + jnp.where(chip == j, l[0:8, j * 128:(j + 1) * 128], 0.0)
        lw[...] = acc
        update("lru_conv_w", lw[0:4, :])
        update("short_conv_w", lw[5:8, :])
        update("gate_a_w", dsum(gga_ref, 0, D_LRU))
        update("gate_x_w", dsum(ggx_ref, 0, D_LRU))

    out_shape = []
    for n in names:
        out_shape += [jax.ShapeDtypeStruct(params[n][0].shape, F32)] * 4
    out_shape += [jax.ShapeDtypeStruct((8, 128), F32), jax.ShapeDtypeStruct((8, 6 * D_MODEL), F32)]
    res = pl.pallas_call(
        body, name="small_update", out_shape=out_shape,
        in_specs=[pl.BlockSpec(memory_space=pltpu.SMEM)] + [VMEM] * (5 + nin),
        out_specs=[VMEM] * len(out_shape),
        scratch_shapes=[pltpu.VMEM((8, D_MODEL), F32), pltpu.VMEM((8, D_MODEL), F32), pltpu.VMEM((16, D_LRU), F32),
                        pltpu.VMEM((8, 128), F32)],
        compiler_params=pltpu.CompilerParams(vmem_limit_bytes=VMEM_LIMIT),
    )(mychip, gad, gam, gl, gga, ggx, *flat)
    per = {n: res[4 * k:4 * k + 4] for k, n in enumerate(names)}
    return per, res[-2], res[-1]


def _block_diag(w):
    eye = jnp.eye(8, dtype=w.dtype)
    return (eye[:, None, :, None] * w[:, :, None, :]).reshape(8 * HEAD, 8 * HEAD)


def _diag_blocks(g):
    return jnp.concatenate([g[h * HEAD:(h + 1) * HEAD, h * HEAD:(h + 1) * HEAD] for h in range(8)], axis=0)


def kernel(x, c, ada_w, ada_b, norm1_g, w_in, lru_conv_w, lru_conv_b, gate_a_w, gate_a_b, gate_x_w, gate_x_b, a_param, short_conv_w, lru_out_g, conv_out_g, w_out, norm2_g, w_mlp1, w_mlp2, final_g, loss_target, m_ada_w, m_ada_b, m_norm1_g, m_w_in, m_lru_conv_w, m_lru_conv_b, m_gate_a_w, m_gate_a_b, m_gate_x_w, m_gate_x_b, m_a_param, m_short_conv_w, m_lru_out_g, m_conv_out_g, m_w_out, m_norm2_g, m_w_mlp1, m_w_mlp2, m_final_g, v_ada_w, v_ada_b, v_norm1_g, v_w_in, v_lru_conv_w, v_lru_conv_b, v_gate_a_w, v_gate_a_b, v_gate_x_w, v_gate_x_b, v_a_param, v_short_conv_w, v_lru_out_g, v_conv_out_g, v_w_out, v_norm2_g, v_w_mlp1, v_w_mlp2, v_final_g):
    xi, yi, ci = lax.axis_index("x"), lax.axis_index("y"), lax.axis_index("c")
    mychip = 2 * xi + yi
    me = 4 * xi + 2 * yi + ci

    c_blk = jnp.zeros((8, D_MODEL), F32).at[0:1].set(c)
    cw_blk = jnp.zeros((8, 128), F32).at[0:4].set(lru_conv_w[0]).at[4:7].set(short_conv_w[0])
    c_g, cw_g = _allgather8("allgather_cond", [c_blk, cw_blk])
    c_all = c_g.reshape(8, 8, D_MODEL)[:, 0]
    cw_g = cw_g.reshape(4, 2, 8, 128)[:, 0]
    lcw = cw_g[:, 0:4].transpose(1, 0, 2).reshape(4, D_LRU)
    scw = cw_g[:, 4:7].transpose(1, 0, 2).reshape(3, D_LRU)

    mod_loc = _mod_matmul(c_all, ada_w[0])
    (mod_g,) = _allgather8("allgather_mod", [mod_loc])
    mod_all = mod_g.reshape(4, 2, 8, 6 * D_MODEL // 4)[:, 0].transpose(1, 0, 2).reshape(8, 6 * D_MODEL) + ada_b
    mod_pad = jnp.pad(mod_all.reshape(8, 6, D_MODEL), ((0, 0), (0, 2), (0, 0)))
    mod = lax.dynamic_slice_in_dim(mod_pad, me, 1, axis=0).reshape(8, D_MODEL)

    own_in, own_out, own_w1, own_w2 = [w[0].astype(BF16) for w in (w_in, w_out, w_mlp1, w_mlp2)]
    win_all, wout_all = _allgather_weights([own_in, own_out])
    win, wout = (win_all, own_in), (wout_all, own_out)
    chip = mychip.reshape(1).astype(jnp.int32)
    core = ci.reshape(1).astype(jnp.int32)

    vecd = jnp.concatenate([norm1_g, norm2_g, final_g[None, :], jnp.concatenate([gate_a_b, gate_x_b], axis=1),
                            jnp.zeros((4, D_MODEL), F32)], axis=0)
    vecl = jnp.concatenate([lcw, lru_conv_b, scw, a_param, lru_out_g, conv_out_g, jnp.zeros((5, D_LRU), F32)], axis=0)
    gab = jnp.concatenate([_block_diag(gate_a_w[0]), _block_diag(gate_x_w[0])], axis=1).astype(BF16)
    a64 = _block_diag(jnp.full((8, HEAD, HEAD), 1.0 / HEAD, F32)).astype(BF16)

    hb, proj, hl, ycat, mixed, x1, w1_all, w2_all = _mix_fwd(
        chip, x[0], mod, vecd, vecl, win, wout, gab, a64, [own_w1, own_w2])
    dx1, act, dz, dmo, h2b, accm = _mlp_fwd_bwd(
        chip, x1, loss_target[0], mod, vecd, (w1_all, own_w1), (w2_all, own_w2))

    def sibling_sum(tag, grads):
        recv = _swap_halves("rs_swap_halves_" + tag, grads)
        return [_add_sibling("rs_add_sibling_%s%d" % (tag, k), g, r, core) for k, (g, r) in enumerate(zip(grads, recv))]

    parts_mlp = sibling_sum("mlp", [_wgrad("wgrad_mlp1", h2b, dz, 0, FF_BLK, BF16),
                                    _wgrad("wgrad_mlp2", act, dmo, FF_BLK, 0, BF16)])
    grad_x, dproj, dmixed, xlb, dgb, accd, accl, q_w1, q_w2 = _mix_bwd(
        chip, dx1, x[0], mixed, proj, hl, mod, vecd, vecl, win, wout, gab, a64, parts_mlp)
    parts_mix = sibling_sum("mix", [_wgrad("wgrad_in", hb, dproj, 0, WIN_BLK, BF16),
                                    _wgrad("wgrad_out", ycat, dmixed, WOUT_BLK, 0, BF16)])
    g_gate = _wgrad("wgrad_gate", xlb, dgb, 0, 0, F32)[0]
    landed = list(_exchange_chips(parts_mix)) + [q_w1, q_w2]
    g_own = [_add_chips("rs_add_chips_%d" % k, chip, p, q) for k, (p, q) in enumerate(zip(parts_mix + parts_mlp, landed))]
    g_sib = _swap_reduced(g_own)

    gga_blk = _diag_blocks(g_gate[:, 0:D_LRU])
    ggx_blk = _diag_blocks(g_gate[:, D_LRU:2 * D_LRU])
    gad, gam, gl, gga, ggx = _allgather8("allgather_small_grads", [accd, accm, accl, gga_blk, ggx_blk])

    params = {
        "ada_b": (ada_b, m_ada_b, v_ada_b), "norm1_g": (norm1_g, m_norm1_g, v_norm1_g),
        "lru_conv_b": (lru_conv_b, m_lru_conv_b, v_lru_conv_b),
        "gate_a_w": tuple(t.reshape(D_LRU, HEAD) for t in (gate_a_w, m_gate_a_w, v_gate_a_w)),
        "gate_a_b": (gate_a_b, m_gate_a_b, v_gate_a_b),
        "gate_x_w": tuple(t.reshape(D_LRU, HEAD) for t in (gate_x_w, m_gate_x_w, v_gate_x_w)),
        "gate_x_b": (gate_x_b, m_gate_x_b, v_gate_x_b), "a_param": (a_param, m_a_param, v_a_param),
        "lru_conv_w": tuple(t[0] for t in (lru_conv_w, m_lru_conv_w, v_lru_conv_w)),
        "short_conv_w": tuple(t[0] for t in (short_conv_w, m_short_conv_w, v_short_conv_w)),
        "lru_out_g": (lru_out_g, m_lru_out_g, v_lru_out_g), "conv_out_g": (conv_out_g, m_conv_out_g, v_conv_out_g),
        "norm2_g": (norm2_g, m_norm2_g, v_norm2_g),
        "final_g": tuple(t[None, :] for t in (final_g, m_final_g, v_final_g)),
    }
    small, loss_blk, dmod_cols = _small_update(gad, gam, gl, gga, ggx, mychip.reshape(1).astype(jnp.int32), params)
    loss = loss_blk[0, 0]

    ncol = 6 * D_MODEL // N_CHIP
    dmod_loc = lax.dynamic_slice_in_dim(dmod_cols, mychip * ncol, ncol, axis=1)
    sct = (c_all * jax.nn.sigmoid(c_all)).T
    ada = _ada_grad_adam(sct, dmod_loc, ada_w[0], m_ada_w[0], v_ada_w[0])

    big_w = {"w_in": (w_in, m_w_in, v_w_in), "w_out": (w_out, m_w_out, v_w_out),
             "w_mlp1": (w_mlp1, m_w_mlp1, v_w_mlp1), "w_mlp2": (w_mlp2, m_w_mlp2, v_w_mlp2)}
    res = {"ada_w": tuple(t[None] for t in ada)}
    for k, (n, (w, m, v)) in enumerate(big_w.items()):
        res[n] = tuple(t[None] for t in _adam("adam_" + n, core, w[0], g_own[k], g_sib[k], m[0], v[0]))
    shapes = {"gate_a_w": gate_a_w.shape, "gate_x_w": gate_x_w.shape, "lru_conv_w": lru_conv_w.shape,
              "short_conv_w": short_conv_w.shape, "final_g": final_g.shape}
    for n, t in small.items():
        res[n] = tuple(u.reshape(shapes[n]) if n in shapes else u for u in t)

    order = ["ada_w", "ada_b", "norm1_g", "w_in", "lru_conv_w", "lru_conv_b", "gate_a_w", "gate_a_b", "gate_x_w",
             "gate_x_b", "a_param", "short_conv_w", "lru_out_g", "conv_out_g", "w_out", "norm2_g", "w_mlp1",
             "w_mlp2", "final_g"]
    return (loss, grad_x[None], *[res[n][0] for n in order], *[res[n][1] for n in order],
            *[res[n][2] for n in order], *[res[n][3] for n in order])
```

```python
import jax
import jax.numpy as jnp
from jax import lax
from jax.experimental import pallas as pl
from jax.experimental.pallas import tpu as pltpu

F32 = jnp.float32
BF16 = jnp.bfloat16

D_MODEL = 1024
D_LRU = 512
D_IN = 2560
D_FF = 4096
N_CHIP = 4
WIN_BLK = D_IN // N_CHIP
WOUT_BLK = D_MODEL // N_CHIP
FF_BLK = D_FF // N_CHIP
HEAD = 64
EPS = 1e-6
C_GATE = 8.0
TOKEN_TILE = 256
HALO = 8
VMEM_LIMIT = 60 * 1024 * 1024

ADAM_LR = 0.001
ADAM_B1 = 0.9
ADAM_B2 = 0.999
ADAM_EPS = 1e-08
ADAM_WD = 0.01
ADAM_STEP = 10

MESH = pl.DeviceIdType.MESH
ANY = pl.BlockSpec(memory_space=pl.ANY)
VMEM = pl.BlockSpec(memory_space=pltpu.VMEM)
SMEM = pl.BlockSpec(memory_space=pltpu.SMEM)


def _full(shape, single=False):
    nd = len(shape)
    if single:
        return pl.BlockSpec(shape, lambda *_: (0,) * nd, pipeline_mode=pl.Buffered(1))
    return pl.BlockSpec(shape, lambda *_: (0,) * nd)


def _dot(a, b):
    return jnp.dot(a, b, preferred_element_type=F32)


def _dot_nt(a, b):
    return lax.dot_general(a, b, (((1,), (1,)), ((), ())), preferred_element_type=F32)


def _dot_tn(a, b):
    return lax.dot_general(a, b, (((0,), (0,)), ((), ())), preferred_element_type=F32)


def _gmean(v, a64):
    hi = v.astype(BF16)
    lo = (v - hi.astype(F32)).astype(BF16)
    return _dot(hi, a64) + _dot(lo, a64)


def _gelu(x):
    u = 0.7978845608028654 * (x + 0.044715 * x * x * x)
    t = jnp.tanh(u)
    return 0.5 * x * (1.0 + t), t


def _gelu_grad(x, t):
    du = 0.7978845608028654 * (1.0 + 3.0 * 0.044715 * x * x)
    return 0.5 * (1.0 + t) + 0.5 * x * (1.0 - t * t) * du


def _log1p_pos(y):
    return jnp.where(y < 1e-2, y * (1.0 - y * (0.5 - y * (1.0 / 3.0 - y * 0.25))), jnp.log(1.0 + y))


def _softplus(a):
    return jnp.maximum(a, 0.0) + _log1p_pos(jnp.exp(-jnp.abs(a)))


def _neg_expm1(z):
    series = -z * (1.0 + z * (0.5 + z * (1.0 / 6.0 + z * (1.0 / 24.0 + z * (1.0 / 120.0)))))
    return jnp.where(z > -0.02, series, 1.0 - jnp.exp(z))


def _scan_fwd(a, b, row):
    n = a.shape[0]
    d = 1
    while d < n:
        m = row >= d
        b = jnp.where(m, a * pltpu.roll(b, d, 0) + b, b)
        a = jnp.where(m, a * pltpu.roll(a, d, 0), a)
        d *= 2
    return a, b


def _scan_rev(a, b, row):
    n = a.shape[0]
    d = 1
    while d < n:
        m = row < n - d
        b = jnp.where(m, b + a * pltpu.roll(b, n - d, 0), b)
        a = jnp.where(m, a * pltpu.roll(a, n - d, 0), a)
        d *= 2
    return a, b


def _colsum(v):
    return jnp.sum(v, axis=0, keepdims=True)


def _load_gathered(chip, gathered, own, slot, sems):
    copies = []
    for j in range(N_CHIP):
        @pl.when(chip == j)
        def _(j=j):
            pltpu.make_async_copy(own, slot(j), sems.at[j]).start()

        @pl.when(chip != j)
        def _(j=j):
            pltpu.make_async_copy(gathered.at[j], slot(j), sems.at[j]).start()

        copies.append(pltpu.make_async_copy(own, slot(j), sems.at[j]))
    return copies


def _lru_gates(xlb, gab, gbias, sp, first_row):
    g = _dot(xlb, gab) + gbias
    r = jax.nn.sigmoid(g[:, :D_LRU])
    ig = jax.nn.sigmoid(g[:, D_LRU:])
    la = (-C_GATE) * r * sp
    a = jnp.exp(la)
    msq = jnp.sqrt(_neg_expm1(2.0 * la))
    mult = jnp.where(first_row, 1.0, msq)
    return r, ig, a, msq, mult


def _mix_fwd(chip, x, mod, vecd, vecl, win, wout, gab, a64, mlp_shards):
    s = x.shape[0]
    ts = TOKEN_TILE
    nt = s // ts

    def body(chip_ref, x_ref, mod_ref, vd_ref, vl_ref, win_hbm, win_own, wout_hbm, wout_own, gab_ref, a64_ref,
             w1_own, w2_own,
             hb_ref, proj_ref, hl_ref, ycat_ref, mixed_ref, x1_ref, w1_all, w2_all,
             win_ref, wout_ref, ext_lx, ext_cv, hcar, sems, ag_send, ag_recv):
        i = pl.program_id(0)

        @pl.when(i == 0)
        def _():
            _ag_start((w1_own, w2_own), (w1_all, w2_all), ag_send, ag_recv)
            cps = _load_gathered(chip_ref[0], win_hbm, win_own, lambda j: win_ref.at[j], sems.at[pl.ds(0, N_CHIP)])
            cps += _load_gathered(chip_ref[0], wout_hbm, wout_own,
                                  lambda j: wout_ref.at[pl.ds(j * WOUT_BLK, WOUT_BLK), :],
                                  sems.at[pl.ds(N_CHIP, N_CHIP)])
            ext_lx[0:HALO, :] = jnp.zeros((HALO, D_LRU), F32)
            ext_cv[0:HALO, :] = jnp.zeros((HALO, D_LRU), F32)
            hcar[...] = jnp.zeros_like(hcar)
            for cp in cps:
                cp.wait()

        row = lax.broadcasted_iota(jnp.int32, (ts, D_LRU), 0)
        first_row = jnp.logical_and(row == 0, i == 0)
        xt = x_ref[...]
        shift1, scale1, gate1 = mod_ref[0:1, :], mod_ref[1:2, :], mod_ref[2:3, :]
        r1 = lax.rsqrt(jnp.mean(xt * xt, axis=-1, keepdims=True) + EPS)
        h = (xt * r1) * vd_ref[0:1, :] * (1.0 + scale1) + shift1
        hb = h.astype(BF16)
        hb_ref[...] = hb
        for j in range(N_CHIP):
            proj_ref[:, j * WIN_BLK:(j + 1) * WIN_BLK] = _dot(hb, win_ref[j])
        u_ly = proj_ref[:, 512:1024]
        u_b = proj_ref[:, 1024:1536]

        ext_lx[HALO:HALO + ts, :] = proj_ref[:, 0:512]
        xl = vl_ref[4:5, :] + vl_ref[0:1, :] * ext_lx[pl.ds(5, ts), :]
        for k in range(1, 4):
            xl = xl + vl_ref[k:k + 1, :] * ext_lx[pl.ds(5 + k, ts), :]
        ext_lx[0:HALO, :] = ext_lx[ts:ts + HALO, :]
        sp = _softplus(vl_ref[8:9, :])
        _, ig, a, _, mult = _lru_gates(xl.astype(BF16), gab_ref[...], vd_ref[3:4, :], sp, first_row)
        acum, hloc = _scan_fwd(a, mult * (ig * xl), row)
        hl = hloc + acum * hcar[0:1, :]
        hl_ref[...] = hl
        hcar[0:1, :] = hl_ref[ts - 1:ts, :]
        ge, _ = _gelu(u_ly)
        p = ge * hl
        y_lru = p * lax.rsqrt(_gmean(p * p, a64_ref[...]) + EPS) * vl_ref[9:10, :]
        ycat_ref[:, 0:512] = y_lru.astype(BF16)

        ext_cv[HALO:HALO + ts, :] = proj_ref[:, 1536:2048] * proj_ref[:, 2048:2560]
        q = vl_ref[5:6, :] * ext_cv[pl.ds(6, ts), :]
        for k in range(1, 3):
            q = q + vl_ref[5 + k:6 + k, :] * ext_cv[pl.ds(6 + k, ts), :]
        ext_cv[0:HALO, :] = ext_cv[ts:ts + HALO, :]
        yc = u_b * q
        y_conv = yc * lax.rsqrt(_gmean(yc * yc, a64_ref[...]) + EPS) * vl_ref[10:11, :]
        ycat_ref[:, 512:1024] = y_conv.astype(BF16)

        mixed = _dot(ycat_ref[...], wout_ref[...])
        mixed_ref[...] = mixed
        x1_ref[...] = xt + gate1 * mixed

        @pl.when(i == nt - 1)
        def _():
            _ag_finish((w1_own, w2_own), (w1_all, w2_all), ag_send, ag_recv)

    tile = lambda w: pl.BlockSpec((ts, w), lambda i: (i, 0))
    return pl.pallas_call(
        body, name="mix_fwd", grid=(nt,),
        in_specs=[SMEM, tile(D_MODEL), _full((8, D_MODEL)), _full((8, D_MODEL)), _full((16, D_LRU)),
                  ANY, ANY, ANY, ANY, _full((D_LRU, 2 * D_LRU), True), _full((D_LRU, D_LRU), True), ANY, ANY],
        out_specs=[tile(D_MODEL), tile(D_IN), tile(D_LRU), tile(D_MODEL), tile(D_MODEL), tile(D_MODEL), ANY, ANY],
        out_shape=[jax.ShapeDtypeStruct((s, D_MODEL), BF16), jax.ShapeDtypeStruct((s, D_IN), F32),
                   jax.ShapeDtypeStruct((s, D_LRU), F32), jax.ShapeDtypeStruct((s, D_MODEL), BF16),
                   jax.ShapeDtypeStruct((s, D_MODEL), F32), jax.ShapeDtypeStruct((s, D_MODEL), F32)]
        + [jax.ShapeDtypeStruct((N_CHIP,) + w.shape, w.dtype) for w in mlp_shards],
        scratch_shapes=[pltpu.VMEM((N_CHIP, D_MODEL, WIN_BLK), BF16), pltpu.VMEM((D_MODEL, D_MODEL), BF16),
                        pltpu.VMEM((ts + HALO, D_LRU), F32), pltpu.VMEM((ts + HALO, D_LRU), F32),
                        pltpu.VMEM((HALO, D_LRU), F32), pltpu.SemaphoreType.DMA((2 * N_CHIP,)),
                        pltpu.SemaphoreType.DMA((12,)), pltpu.SemaphoreType.DMA((12,))],
        compiler_params=pltpu.CompilerParams(dimension_semantics=("arbitrary",), vmem_limit_bytes=VMEM_LIMIT),
    )(chip, x, mod, vecd, vecl, *win, *wout, gab, a64, *mlp_shards)


def _mlp_fwd_bwd(chip, x1, target, mod, vecd, w1, w2):
    s = x1.shape[0]
    ts = TOKEN_TILE
    nt = s // ts

    def body(chip_ref, x1_ref, tg_ref, mod_ref, vd_ref, w1_hbm, w1_own, w2_hbm, w2_own,
             dx1_ref, act_ref, dz_ref, dmo_ref, h2_ref, acc_ref, w1_v, w2_v, rz_v, sems):
        i = pl.program_id(0)

        @pl.when(i == 0)
        def _():
            cps = _load_gathered(chip_ref[0], w1_hbm, w1_own, lambda j: w1_v.at[j], sems.at[pl.ds(0, N_CHIP)])
            cps += _load_gathered(chip_ref[0], w2_hbm, w2_own, lambda j: w2_v.at[j], sems.at[pl.ds(N_CHIP, N_CHIP)])
            acc_ref[...] = jnp.zeros_like(acc_ref)
            for cp in cps:
                cp.wait()

        xt = x1_ref[...]
        shift2, scale2, gate2 = mod_ref[3:4, :], mod_ref[4:5, :], mod_ref[5:6, :]
        g2, gf = vd_ref[1:2, :], vd_ref[2:3, :]
        r2 = lax.rsqrt(jnp.mean(xt * xt, axis=-1, keepdims=True) + EPS)
        n2 = xt * r2
        h2b = (n2 * g2 * (1.0 + scale2) + shift2).astype(BF16)
        h2_ref[...] = h2b
        mo = jnp.zeros((ts, D_MODEL), F32)
        for j in range(N_CHIP):
            rz = jnp.maximum(_dot(h2b, w1_v[j]), 0.0)
            rz_v[j] = rz
            actb = (rz * rz).astype(BF16)
            act_ref[:, j * FF_BLK:(j + 1) * FF_BLK] = actb
            mo = mo + _dot(actb, w2_v[j])
        x2 = xt + gate2 * mo
        r3 = lax.rsqrt(jnp.mean(x2 * x2, axis=-1, keepdims=True) + EPS)
        n3 = x2 * r3
        e = n3 * gf - tg_ref[...]
        loss = (0.5 / D_MODEL) * jnp.sum(_colsum(e * e), axis=1, keepdims=True)
        dy = e * (1.0 / D_MODEL)
        acc_ref[4:5, :] += _colsum(dy * n3)
        acc_ref[5:6, :] += jnp.broadcast_to(loss, (1, D_MODEL))
        dn3 = dy * gf
        dx2 = r3 * (dn3 - n3 * jnp.mean(dn3 * n3, axis=-1, keepdims=True))
        acc_ref[2:3, :] += _colsum(dx2 * mo)
        dmob = (dx2 * gate2).astype(BF16)
        dmo_ref[...] = dmob
        dh2 = jnp.zeros((ts, D_MODEL), F32)
        for j in range(N_CHIP):
            dzb = (_dot_nt(dmob, w2_v[j]) * (2.0 * rz_v[j])).astype(BF16)
            dz_ref[:, j * FF_BLK:(j + 1) * FF_BLK] = dzb
            dh2 = dh2 + _dot_nt(dzb, w1_v[j])
        acc_ref[1:2, :] += _colsum(dh2 * (n2 * g2))
        acc_ref[0:1, :] += _colsum(dh2)
        dhn2 = dh2 * (1.0 + scale2)
        acc_ref[3:4, :] += _colsum(dhn2 * n2)
        dn2 = dhn2 * g2
        dx1_ref[...] = dx2 + r2 * (dn2 - n2 * jnp.mean(dn2 * n2, axis=-1, keepdims=True))

    tile = lambda w: pl.BlockSpec((ts, w), lambda i: (i, 0))
    return pl.pallas_call(
        body, name="mlp_fwd_bwd", grid=(nt,),
        in_specs=[SMEM, tile(D_MODEL), tile(D_MODEL), _full((8, D_MODEL)), _full((8, D_MODEL)), ANY, ANY, ANY, ANY],
        out_specs=[tile(D_MODEL), tile(D_FF), tile(D_FF), tile(D_MODEL), tile(D_MODEL), _full((8, D_MODEL))],
        out_shape=[jax.ShapeDtypeStruct((s, D_MODEL), F32), jax.ShapeDtypeStruct((s, D_FF), BF16),
                   jax.ShapeDtypeStruct((s, D_FF), BF16), jax.ShapeDtypeStruct((s, D_MODEL), BF16),
                   jax.ShapeDtypeStruct((s, D_MODEL), BF16), jax.ShapeDtypeStruct((8, D_MODEL), F32)],
        scratch_shapes=[pltpu.VMEM((N_CHIP, D_MODEL, FF_BLK), BF16), pltpu.VMEM((N_CHIP, FF_BLK, D_MODEL), BF16),
                        pltpu.VMEM((N_CHIP, ts, FF_BLK), F32), pltpu.SemaphoreType.DMA((2 * N_CHIP,))],
        compiler_params=pltpu.CompilerParams(dimension_semantics=("arbitrary",), vmem_limit_bytes=VMEM_LIMIT),
    )(chip, x1, target, mod, vecd, *w1, *w2)


def _mix_bwd(chip, dx1, x, mixed, proj, hl, mod, vecd, vecl, win, wout, gab, a64, mlp_parts):
    s = x.shape[0]
    ts = TOKEN_TILE
    nt = s // ts
    hpt = ts // HALO

    def body(chip_ref, dx1_ref, x_ref, mixed_ref, proj_ref, projh_ref, hl_ref, hlh_ref, mod_ref, vd_ref, vl_ref,
             win_hbm, win_own, wout_hbm, wout_own, gab_ref, a64_ref, p1_ref, p2_ref,
             gx_ref, dproj_ref, dmixed_ref, xlb_ref, dgb_ref, accd_ref, accl_ref, q1_ref, q2_ref,
             win_ref, wout_ref, ext_lx, ext_cv, ext_hl, ext_dxl, ext_dq, gbuf, gcar, acar, sems, x_send, x_recv):
        i = pl.program_id(0)
        ri = nt - 1 - i

        @pl.when(i == 0)
        def _():
            for cp in _xchg_copies((p1_ref, p2_ref), (q1_ref, q2_ref), x_send, x_recv):
                cp.start()
            cps = _load_gathered(chip_ref[0], win_hbm, win_own, lambda j: win_ref.at[j], sems.at[pl.ds(0, N_CHIP)])
            cps += _load_gathered(chip_ref[0], wout_hbm, wout_own,
                                  lambda j: wout_ref.at[pl.ds(j * WOUT_BLK, WOUT_BLK), :],
                                  sems.at[pl.ds(N_CHIP, N_CHIP)])
            for cp in cps:
                cp.wait()
            accd_ref[...] = jnp.zeros_like(accd_ref)
            accl_ref[...] = jnp.zeros_like(accl_ref)
            ext_dxl[ts:ts + HALO, :] = jnp.zeros((HALO, D_LRU), F32)
            ext_dq[ts:ts + HALO, :] = jnp.zeros((HALO, D_LRU), F32)
            gcar[...] = jnp.zeros_like(gcar)
            acar[...] = jnp.zeros_like(acar)

        row = lax.broadcasted_iota(jnp.int32, (ts, D_LRU), 0)
        first_row = jnp.logical_and(row == 0, ri == 0)
        halo_on = jnp.where(ri == 0, 0.0, 1.0)
        shift1, scale1, gate1 = mod_ref[0:1, :], mod_ref[1:2, :], mod_ref[2:3, :]
        g1 = vd_ref[0:1, :]
        a64m = a64_ref[...]
        lg, cg = vl_ref[9:10, :], vl_ref[10:11, :]

        dx1 = dx1_ref[...]
        accd_ref[2:3, :] += _colsum(dx1 * mixed_ref[...])
        dmb = (dx1 * gate1).astype(BF16)
        dmixed_ref[...] = dmb
        dycat = _dot_nt(dmb, wout_ref[...])
        dyl = dycat[:, 0:512]
        dyv = dycat[:, 512:1024]

        u_ly = proj_ref[:, 512:1024]
        u_b = proj_ref[:, 1024:1536]
        u_c = proj_ref[:, 1536:2048]
        u_v = proj_ref[:, 2048:2560]
        ext_lx[0:HALO, :] = projh_ref[:, 0:512] * halo_on
        ext_lx[HALO:HALO + ts, :] = proj_ref[:, 0:512]
        xl = vl_ref[4:5, :] + vl_ref[0:1, :] * ext_lx[pl.ds(5, ts), :]
        for k in range(1, 4):
            xl = xl + vl_ref[k:k + 1, :] * ext_lx[pl.ds(5 + k, ts), :]
        xlb = xl.astype(BF16)
        xlb_ref[...] = xlb
        sp = _softplus(vl_ref[8:9, :])
        r, ig, a, msq, mult = _lru_gates(xlb, gab_ref[...], vd_ref[3:4, :], sp, first_row)
        hl = hl_ref[...]
        ge, th = _gelu(u_ly)
        p = ge * hl
        rl = lax.rsqrt(_gmean(p * p, a64m) + EPS)
        nl = p * rl
        ext_cv[0:HALO, :] = projh_ref[:, 1536:2048] * projh_ref[:, 2048:2560] * halo_on
        ext_cv[HALO:HALO + ts, :] = u_c * u_v
        q = vl_ref[5:6, :] * ext_cv[pl.ds(6, ts), :]
        for k in range(1, 3):
            q = q + vl_ref[5 + k:6 + k, :] * ext_cv[pl.ds(6 + k, ts), :]
        yc = u_b * q
        rc = lax.rsqrt(_gmean(yc * yc, a64m) + EPS)
        nc = yc * rc

        accl_ref[9:10, :] += _colsum(dyl * nl)
        dnl = dyl * lg
        dp = rl * (dnl - nl * _gmean(dnl * nl, a64m))
        dproj_ref[:, 512:1024] = ((dp * hl) * _gelu_grad(u_ly, th)).astype(BF16)
        a_next = jnp.where(row == ts - 1, acar[0:1, :], pltpu.roll(a, ts - 1, 0))
        acum, gloc = _scan_rev(a_next, dp * ge, row)
        gbuf[...] = gloc + acum * gcar[0:1, :]
        gcar[0:1, :] = gbuf[0:1, :]
        ext_hl[0:HALO, :] = hlh_ref[...] * halo_on
        ext_hl[HALO:HALO + ts, :] = hl
        acar[...] = a[0:HALO, :]
        gt = gbuf[...]
        da = gt * ext_hl[pl.ds(HALO - 1, ts), :]
        dmult = gt * ig * xl
        di = gt * mult * xl
        dxl = gt * mult * ig
        dla = da * a - jnp.where(first_row, 0.0, dmult * a * a / msq)
        accl_ref[8:9, :] += _colsum(dla * ((-C_GATE) * r))
        dra = dla * ((-C_GATE) * sp) * r * (1.0 - r)
        dia = di * ig * (1.0 - ig)
        accd_ref[4:5, 0:D_LRU] += _colsum(dra)
        accd_ref[4:5, D_LRU:2 * D_LRU] += _colsum(dia)
        dgb_ref[:, 0:D_LRU] = dra.astype(BF16)
        dgb_ref[:, D_LRU:2 * D_LRU] = dia.astype(BF16)
        dxl = dxl + _dot_nt(dgb_ref[...], gab_ref[...])
        accl_ref[4:5, :] += _colsum(dxl)
        for k in range(4):
            accl_ref[k:k + 1, :] += _colsum(dxl * ext_lx[pl.ds(5 + k, ts), :])
        ext_dxl[0:ts, :] = dxl
        du_lx = vl_ref[0:1, :] * ext_dxl[pl.ds(3, ts), :]
        for k in range(1, 4):
            du_lx = du_lx + vl_ref[k:k + 1, :] * ext_dxl[pl.ds(3 - k, ts), :]
        ext_dxl[ts:ts + HALO, :] = ext_dxl[0:HALO, :]
        dproj_ref[:, 0:512] = du_lx.astype(BF16)

        accl_ref[10:11, :] += _colsum(dyv * nc)
        dnc = dyv * cg
        dyc = rc * (dnc - nc * _gmean(dnc * nc, a64m))
        dproj_ref[:, 1024:1536] = (dyc * q).astype(BF16)
        dq = dyc * u_b
        for k in range(3):
            accl_ref[5 + k:6 + k, :] += _colsum(dq * ext_cv[pl.ds(6 + k, ts), :])
        ext_dq[0:ts, :] = dq
        dcv = vl_ref[5:6, :] * ext_dq[pl.ds(2, ts), :]
        for k in range(1, 3):
            dcv = dcv + vl_ref[5 + k:6 + k, :] * ext_dq[pl.ds(2 - k, ts), :]
        ext_dq[ts:ts + HALO, :] = ext_dq[0:HALO, :]
        dproj_ref[:, 1536:2048] = (dcv * u_v).astype(BF16)
        dproj_ref[:, 2048:2560] = (dcv * u_c).astype(BF16)

        dh = _dot_nt(dproj_ref[:, 0:WIN_BLK], win_ref[0])
        for j in range(1, N_CHIP):
            dh = dh + _dot_nt(dproj_ref[:, j * WIN_BLK:(j + 1) * WIN_BLK], win_ref[j])
        xt = x_ref[...]
        r1 = lax.rsqrt(jnp.mean(xt * xt, axis=-1, keepdims=True) + EPS)
        n1 = xt * r1
        accd_ref[1:2, :] += _colsum(dh * (n1 * g1))
        accd_ref[0:1, :] += _colsum(dh)
        dhn1 = dh * (1.0 + scale1)
        accd_ref[3:4, :] += _colsum(dhn1 * n1)
        dn1 = dhn1 * g1
        gx_ref[...] = dx1 + r1 * (dn1 - n1 * jnp.mean(dn1 * n1, axis=-1, keepdims=True))

        @pl.when(i == nt - 1)
        def _():
            for cp in _xchg_copies((p1_ref, p2_ref), (q1_ref, q2_ref), x_send, x_recv):
                cp.wait()

    tile = lambda w: pl.BlockSpec((ts, w), lambda i: (nt - 1 - i, 0))
    halo = lambda w: pl.BlockSpec((HALO, w), lambda i: (jnp.maximum((nt - 1 - i) * hpt - 1, 0), 0))
    ext = pltpu.VMEM((ts + HALO, D_LRU), F32)
    return pl.pallas_call(
        body, name="mix_bwd", grid=(nt,),
        in_specs=[SMEM, tile(D_MODEL), tile(D_MODEL), tile(D_MODEL), tile(D_IN), halo(D_IN), tile(D_LRU), halo(D_LRU),
                  _full((8, D_MODEL)), _full((8, D_MODEL)), _full((16, D_LRU)),
                  ANY, ANY, ANY, ANY, _full((D_LRU, 2 * D_LRU), True), _full((D_LRU, D_LRU), True), ANY, ANY],
        out_specs=[tile(D_MODEL), tile(D_IN), tile(D_MODEL), tile(D_LRU), tile(2 * D_LRU),
                   _full((8, D_MODEL)), _full((16, D_LRU)), ANY, ANY],
        out_shape=[jax.ShapeDtypeStruct((s, D_MODEL), F32), jax.ShapeDtypeStruct((s, D_IN), BF16),
                   jax.ShapeDtypeStruct((s, D_MODEL), BF16), jax.ShapeDtypeStruct((s, D_LRU), BF16),
                   jax.ShapeDtypeStruct((s, 2 * D_LRU), BF16),
                   jax.ShapeDtypeStruct((8, D_MODEL), F32), jax.ShapeDtypeStruct((16, D_LRU), F32)]
        + [jax.ShapeDtypeStruct((3,) + p.shape[1:], p.dtype) for p in mlp_parts],
        scratch_shapes=[pltpu.VMEM((N_CHIP, D_MODEL, WIN_BLK), BF16), pltpu.VMEM((D_MODEL, D_MODEL), BF16),
                        ext, ext, ext, ext, ext, pltpu.VMEM((ts, D_LRU), F32),
                        pltpu.VMEM((HALO, D_LRU), F32), pltpu.VMEM((HALO, D_LRU), F32),
                        pltpu.SemaphoreType.DMA((2 * N_CHIP,)),
                        pltpu.SemaphoreType.DMA((6,)), pltpu.SemaphoreType.DMA((6,))],
        compiler_params=pltpu.CompilerParams(dimension_semantics=("arbitrary",), vmem_limit_bytes=VMEM_LIMIT),
    )(chip, dx1, x, mixed, proj, proj, hl, hl, mod, vecd, vecl, *win, *wout, gab, a64, *mlp_parts)


def _wgrad(name, a, b, a_blk, b_blk, out_dtype):
    s = a.shape[0]
    aw = a_blk or a.shape[1]
    bw = b_blk or b.shape[1]
    nblk = N_CHIP if (a_blk or b_blk) else 1

    def body(a_ref, b_ref, o_ref):
        o_ref[0] = _dot_tn(a_ref[...], b_ref[...]).astype(out_dtype)

    return pl.pallas_call(
        body, name=name, grid=(nblk,),
        in_specs=[pl.BlockSpec((s, aw), (lambda j: (0, j)) if a_blk else (lambda j: (0, 0))),
                  pl.BlockSpec((s, bw), (lambda j: (0, j)) if b_blk else (lambda j: (0, 0)))],
        out_specs=pl.BlockSpec((1, aw, bw), lambda j: (j, 0, 0)),
        out_shape=jax.ShapeDtypeStruct((nblk, aw, bw), out_dtype),
        compiler_params=pltpu.CompilerParams(dimension_semantics=("arbitrary",), vmem_limit_bytes=VMEM_LIMIT),
    )(a, b)


def _mod_matmul(c_all, ada_w_loc):
    n = ada_w_loc.shape[1]
    cb = 512

    def body(c_ref, w_ref, o_ref):
        c = c_ref[...]
        sc = c * jax.nn.sigmoid(c)
        o_ref[...] = _dot(sc.astype(BF16), w_ref[...].astype(BF16))

    return pl.pallas_call(
        body, name="mod_matmul", grid=(n // cb,),
        in_specs=[_full((8, D_MODEL)), pl.BlockSpec((D_MODEL, cb), lambda j: (0, j))],
        out_specs=pl.BlockSpec((8, cb), lambda j: (0, j)),
        out_shape=jax.ShapeDtypeStruct((8, n), F32),
        compiler_params=pltpu.CompilerParams(dimension_semantics=("arbitrary",), vmem_limit_bytes=VMEM_LIMIT),
    )(c_all, ada_w_loc)


def _adam_math(w, g, m, v):
    m = ADAM_B1 * m + (1.0 - ADAM_B1) * g
    v = ADAM_B2 * v + (1.0 - ADAM_B2) * (g * g)
    m_hat = m / (1.0 - ADAM_B1 ** ADAM_STEP)
    v_hat = v / (1.0 - ADAM_B2 ** ADAM_STEP)
    delta = (-ADAM_LR) * (m_hat / (jnp.sqrt(v_hat) + ADAM_EPS) + ADAM_WD * w)
    return delta, m, v


def _adam(name, core, shards, carry=()):
    n, nc = len(shards), len(carry)
    r, c = shards[0][0].shape
    half = r // 2
    rb = min(half, 128)
    nh = half // rb
    nblk = r // rb

    def body(core_ref, *refs):
        ins, parts = refs[:5 * n], refs[5 * n:5 * n + nc]
        outs, landed = refs[5 * n + nc:9 * n + nc], refs[9 * n + nc:9 * n + 2 * nc]
        sems = refs[9 * n + 2 * nc:]
        i = pl.program_id(0)
        if nc:
            @pl.when(i == 0)
            def _():
                for cp in _xchg_copies(parts, landed, *sems):
                    cp.start()

        mine = (i // nh) == core_ref[0]
        for k in range(n):
            w_ref, go_ref, gs_ref, m_ref, v_ref = ins[5 * k:5 * k + 5]
            g_ref, d_ref, mo_ref, vo_ref = outs[4 * k:4 * k + 4]
            g = jnp.where(mine, go_ref[...], gs_ref[...])
            g_ref[...] = g
            d_ref[...], mo_ref[...], vo_ref[...] = _adam_math(w_ref[...], g, m_ref[...], v_ref[...])
        if nc:
            @pl.when(i == nblk - 1)
            def _():
                for cp in _xchg_copies(parts, landed, *sems):
                    cp.wait()

    spec = pl.BlockSpec((rb, c), lambda i, core_ref: (i, 0))
    hspec = pl.BlockSpec((rb, c), lambda i, core_ref: (i % nh, 0))
    sds = jax.ShapeDtypeStruct((r, c), F32)
    res = pl.pallas_call(
        body, name=name,
        grid_spec=pltpu.PrefetchScalarGridSpec(
            num_scalar_prefetch=1, grid=(nblk,),
            in_specs=[spec, hspec, hspec, spec, spec] * n + [ANY] * nc,
            out_specs=[spec] * (4 * n) + [ANY] * nc,
            scratch_shapes=[pltpu.SemaphoreType.DMA((3 * nc,))] * 2 if nc else []),
        out_shape=[sds] * (4 * n) + [jax.ShapeDtypeStruct((3,) + p.shape[1:], p.dtype) for p in carry],
        compiler_params=pltpu.CompilerParams(dimension_semantics=("arbitrary",), vmem_limit_bytes=VMEM_LIMIT),
    )(core, *[t for s in shards for t in s], *carry)
    return [res[4 * k:4 * k + 4] for k in range(n)], list(res[4 * n:])


def _ada_grad_adam(sct, dmod_loc, w, m, v):
    r, c = w.shape
    rb = 128

    def body(s_ref, dm_ref, w_ref, m_ref, v_ref, g_ref, d_ref, mo_ref, vo_ref):
        g = s_ref[:, 0:1] * dm_ref[0:1, :]
        for b in range(1, 8):
            g = g + s_ref[:, b:b + 1] * dm_ref[b:b + 1, :]
        g_ref[...] = g
        d_ref[...], mo_ref[...], vo_ref[...] = _adam_math(w_ref[...], g, m_ref[...], v_ref[...])

    spec = pl.BlockSpec((rb, c), lambda i: (i, 0))
    sds = jax.ShapeDtypeStruct((r, c), F32)
    return pl.pallas_call(
        body, name="ada_grad_adam", grid=(r // rb,),
        in_specs=[pl.BlockSpec((rb, 8), lambda i: (i, 0)), _full((8, c)), spec, spec, spec],
        out_specs=[spec] * 4, out_shape=[sds] * 4,
        compiler_params=pltpu.CompilerParams(dimension_semantics=("arbitrary",), vmem_limit_bytes=VMEM_LIMIT),
    )(sct, dmod_loc, w, m, v)


def _position():
    x, y, c = lax.axis_index("x"), lax.axis_index("y"), lax.axis_index("c")
    chips = [(1 - x, y), (x, 1 - y), (1 - x, 1 - y)]
    return x, y, c, chips


def _allgather8(name, arrs):
    na = len(arrs)

    def body(*refs):
        ins, outs = refs[:na], refs[na:2 * na]
        send_sems, recv_sems, local_sems = refs[2 * na:]
        x, y, c, chips = _position()
        me, sibling = (x, y, c), (x, y, 1 - c)
        first, passed, local = [], [], []
        for a in range(na):
            m_per = ins[a].shape[0]

            def rows(px, py, pc, a=a, m_per=m_per):
                return outs[a].at[pl.ds((4 * px + 2 * py + pc) * m_per, m_per), :]

            def copy(k, block, to, src=None, a=a, rows=rows):
                return pltpu.make_async_remote_copy(
                    src_ref=rows(*block) if src is None else src, dst_ref=rows(*block),
                    send_sem=send_sems.at[7 * a + k], recv_sem=recv_sems.at[7 * a + k],
                    device_id=to, device_id_type=MESH)

            mine = pltpu.make_async_copy(ins[a], rows(*me), local_sems.at[a])
            mine.start()
            local.append(mine)
            f = [copy(0, me, sibling, src=ins[a])]
            f += [copy(1 + j, me, (*chip, c), src=ins[a]) for j, chip in enumerate(chips)]
            for cp in f:
                cp.start()
            first.append((f, copy))
        for a in range(na):
            f, copy = first[a]
            p = [copy(4 + j, (*chip, c), sibling) for j, chip in enumerate(chips)]
            for j, chip in enumerate(chips):
                copy(1 + j, (*chip, c), me).wait_recv()
                p[j].start()
            passed.append(p)
        for a in range(na):
            f, copy = first[a]
            copy(0, sibling, me).wait_recv()
            for j, chip in enumerate(chips):
                copy(4 + j, (*chip, 1 - c), me).wait_recv()
            for cp in f + passed[a]:
                cp.wait_send()
            local[a].wait()

    return pl.pallas_call(
        body, name=name,
        out_shape=[jax.ShapeDtypeStruct((8 * a.shape[0], a.shape[1]), a.dtype) for a in arrs],
        in_specs=[VMEM] * na, out_specs=[VMEM] * na,
        scratch_shapes=[pltpu.SemaphoreType.DMA((7 * na,)), pltpu.SemaphoreType.DMA((7 * na,)),
                        pltpu.SemaphoreType.DMA((na,))],
        compiler_params=pltpu.CompilerParams(vmem_limit_bytes=VMEM_LIMIT),
    )(*arrs)


def _ag_copies(ins, outs, send_sems, recv_sems):
    x, y, c, chips = _position()
    sibling = (x, y, 1 - c)
    mychip = 2 * x + y
    res = []
    for a in range(len(ins)):
        half = ins[a].shape[0] // 2

        def copy(k, chip, pc, to, src=None, a=a, half=half):
            dst = outs[a].at[chip, pl.ds(pc * half, half), :]
            return pltpu.make_async_remote_copy(
                src_ref=dst if src is None else src, dst_ref=dst,
                send_sem=send_sems.at[6 * a + k], recv_sem=recv_sems.at[6 * a + k],
                device_id=to, device_id_type=MESH)

        src = ins[a].at[pl.ds(c * half, half), :]
        peer = [2 * chip[0] + chip[1] for chip in chips]
        res.append(([copy(j, mychip, c, (*chip, c), src=src) for j, chip in enumerate(chips)],
                    [copy(j, peer[j], c, sibling) for j in range(3)],
                    [copy(3 + j, peer[j], c, sibling) for j in range(3)],
                    [copy(3 + j, peer[j], 1 - c, sibling) for j in range(3)]))
    return res


def _ag_start(ins, outs, send_sems, recv_sems):
    for sends, _, _, _ in _ag_copies(ins, outs, send_sems, recv_sems):
        for cp in sends:
            cp.start()


def _ag_finish(ins, outs, send_sems, recv_sems):
    copies = _ag_copies(ins, outs, send_sems, recv_sems)
    for _, arrivals, forwards, _ in copies:
        for j in range(3):
            arrivals[j].wait_recv()
            forwards[j].start()
    for sends, _, forwards, forwarded in copies:
        for cp in forwarded:
            cp.wait_recv()
        for cp in sends + forwards:
            cp.wait_send()


def _allgather_weights(shards):
    na = len(shards)

    def body(*refs):
        ins, outs = refs[:na], refs[na:2 * na]
        send_sems, recv_sems = refs[2 * na:]
        _ag_start(ins, outs, send_sems, recv_sems)
        _ag_finish(ins, outs, send_sems, recv_sems)

    return pl.pallas_call(
        body, name="allgather_weights",
        out_shape=[jax.ShapeDtypeStruct((N_CHIP,) + s.shape, s.dtype) for s in shards],
        in_specs=[ANY] * na, out_specs=[ANY] * na,
        scratch_shapes=[pltpu.SemaphoreType.DMA((6 * na,)), pltpu.SemaphoreType.DMA((6 * na,))],
    )(*shards)


def _swap_halves(name, grads):
    na = len(grads)

    def body(*refs):
        ins, outs = refs[:na], refs[na:2 * na]
        send_sems, recv_sems = refs[2 * na:]
        x, y, c, _ = _position()
        cps = []
        for a in range(na):
            half = ins[a].shape[1] // 2
            cp = pltpu.make_async_remote_copy(
                src_ref=ins[a].at[:, pl.ds((1 - c) * half, half), :], dst_ref=outs[a],
                send_sem=send_sems.at[a], recv_sem=recv_sems.at[a],
                device_id=(x, y, 1 - c), device_id_type=MESH)
            cp.start()
            cps.append(cp)
        for cp in cps:
            cp.wait()

    return pl.pallas_call(
        body, name=name,
        out_shape=[jax.ShapeDtypeStruct((g.shape[0], g.shape[1] // 2, g.shape[2]), g.dtype) for g in grads],
        in_specs=[ANY] * na, out_specs=[ANY] * na,
        scratch_shapes=[pltpu.SemaphoreType.DMA((na,)), pltpu.SemaphoreType.DMA((na,))],
    )(*grads)


def _xchg_copies(ins, outs, send_sems, recv_sems):
    x, y, c, chips = _position()
    return [pltpu.make_async_remote_copy(
        src_ref=ins[a].at[2 * chip[0] + chip[1]], dst_ref=outs[a].at[j],
        send_sem=send_sems.at[3 * a + j], recv_sem=recv_sems.at[3 * a + j],
        device_id=(*chip, c), device_id_type=MESH) for a in range(len(ins)) for j, chip in enumerate(chips)]


def _exchange_chips(parts):
    na = len(parts)

    def body(*refs):
        ins, outs = refs[:na], refs[na:2 * na]
        send_sems, recv_sems = refs[2 * na:]
        for cp in _xchg_copies(ins, outs, send_sems, recv_sems):
            cp.start()
        for cp in _xchg_copies(ins, outs, send_sems, recv_sems):
            cp.wait()

    return pl.pallas_call(
        body, name="rs_exchange_chips",
        out_shape=[jax.ShapeDtypeStruct((3,) + p.shape[1:], p.dtype) for p in parts],
        in_specs=[ANY] * na, out_specs=[ANY] * na,
        scratch_shapes=[pltpu.SemaphoreType.DMA((3 * na,)), pltpu.SemaphoreType.DMA((3 * na,))],
    )(*parts)


def _swap_reduced(name, halves):
    na = len(halves)

    def body(*refs):
        ins, outs = refs[:na], refs[na:2 * na]
        send_sems, recv_sems = refs[2 * na:]
        x, y, c, _ = _position()
        cps = []
        for a in range(na):
            cp = pltpu.make_async_remote_copy(
                src_ref=ins[a], dst_ref=outs[a], send_sem=send_sems.at[a], recv_sem=recv_sems.at[a],
                device_id=(x, y, 1 - c), device_id_type=MESH)
            cp.start()
            cps.append(cp)
        for cp in cps:
            cp.wait()

    return pl.pallas_call(
        body, name=name,
        out_shape=[jax.ShapeDtypeStruct(h.shape, h.dtype) for h in halves],
        in_specs=[ANY] * na, out_specs=[ANY] * na,
        scratch_shapes=[pltpu.SemaphoreType.DMA((na,)), pltpu.SemaphoreType.DMA((na,))],
    )(*halves)


def _add_sibling(name, grad, recv, core):
    _, r, c = grad.shape
    half = r // 2
    rb = min(half, 256)
    nrb = half // rb

    def body(core_ref, g_ref, r_ref, o_ref):
        o_ref[...] = (g_ref[...].astype(F32) + r_ref[...].astype(F32)).astype(BF16)

    return pl.pallas_call(
        body, name=name,
        grid_spec=pltpu.PrefetchScalarGridSpec(
            num_scalar_prefetch=1, grid=(N_CHIP, nrb),
            in_specs=[pl.BlockSpec((1, rb, c), lambda j, i, core_ref: (j, core_ref[0] * nrb + i, 0)),
                      pl.BlockSpec((1, rb, c), lambda j, i, core_ref: (j, i, 0))],
            out_specs=pl.BlockSpec((1, rb, c), lambda j, i, core_ref: (j, i, 0))),
        out_shape=jax.ShapeDtypeStruct((N_CHIP, half, c), BF16),
        compiler_params=pltpu.CompilerParams(dimension_semantics=("arbitrary", "arbitrary"),
                                             vmem_limit_bytes=VMEM_LIMIT),
    )(core, grad, recv)


def _add_chips(name, chip, p, q):
    _, half, c = q.shape
    rb = min(half, 256)

    def body(chip_ref, p_ref, q_ref, o_ref):
        acc = p_ref[0].astype(F32)
        for j in range(3):
            acc = acc + q_ref[j].astype(F32)
        o_ref[...] = acc

    return pl.pallas_call(
        body, name=name,
        grid_spec=pltpu.PrefetchScalarGridSpec(
            num_scalar_prefetch=1, grid=(half // rb,),
            in_specs=[pl.BlockSpec((1, rb, c), lambda i, chip_ref: (chip_ref[0], i, 0)),
                      pl.BlockSpec((3, rb, c), lambda i, chip_ref: (0, i, 0))],
            out_specs=pl.BlockSpec((rb, c), lambda i, chip_ref: (i, 0))),
        out_shape=jax.ShapeDtypeStruct((half, c), F32),
        compiler_params=pltpu.CompilerParams(dimension_semantics=("arbitrary",), vmem_limit_bytes=VMEM_LIMIT),
    )(chip, p, q)


def _small_update(gad, gam, gl, gg, mychip, params):
    names = ["ada_b", "norm1_g", "lru_conv_b", "gate_a_w", "gate_a_b", "gate_x_w", "gate_x_b", "a_param",
             "lru_conv_w", "short_conv_w", "lru_out_g", "conv_out_g", "norm2_g", "final_g"]
    flat = [t for n in names for t in params[n]]
    nin = len(flat)

    def body(chip_ref, gad_ref, gam_ref, gl_ref, gg_ref, *refs):
        ins = {n: refs[3 * k:3 * k + 3] for k, n in enumerate(names)}
        outs = {n: refs[nin + 4 * k:nin + 4 * k + 4] for k, n in enumerate(names)}
        loss_ref, dmod_ref = refs[nin + 4 * len(names):nin + 4 * len(names) + 2]

        def dsum(ref, lo, n):
            acc = ref[lo:lo + n, :]
            for dev in range(1, 8):
                acc = acc + ref[dev * (ref.shape[0] // 8) + lo:dev * (ref.shape[0] // 8) + lo + n, :]
            return acc

        def update(n, g):
            w_ref, m_ref, v_ref = ins[n]
            g_ref, d_ref, mo_ref, vo_ref = outs[n]
            g_ref[...] = g
            d_ref[...], mo_ref[...], vo_ref[...] = _adam_math(w_ref[...], g, m_ref[...], v_ref[...])

        d, dm, l, lw = refs[-4:]
        d[...] = dsum(gad_ref, 0, 8)
        dm[...] = dsum(gam_ref, 0, 8)
        l[...] = dsum(gl_ref, 0, 16)
        for dev in range(8):
            for k in range(3):
                dmod_ref[dev:dev + 1, k * D_MODEL:(k + 1) * D_MODEL] = gad_ref[dev * 8 + k:dev * 8 + k + 1, :]
                dmod_ref[dev:dev + 1, (3 + k) * D_MODEL:(4 + k) * D_MODEL] = gam_ref[dev * 8 + k:dev * 8 + k + 1, :]
        w_ref, m_ref, v_ref = ins["ada_b"]
        g_ref, d_ref, mo_ref, vo_ref = outs["ada_b"]
        for k in range(3):
            g_ref[:, k * D_MODEL:(k + 1) * D_MODEL] = d[k:k + 1, :]
            g_ref[:, (3 + k) * D_MODEL:(4 + k) * D_MODEL] = dm[k:k + 1, :]
        d_ref[...], mo_ref[...], vo_ref[...] = _adam_math(w_ref[...], g_ref[...], m_ref[...], v_ref[...])
        update("norm1_g", d[3:4, :])
        update("norm2_g", dm[3:4, :])
        update("final_g", dm[4:5, :])
        update("gate_a_b", d[4:5, 0:D_LRU])
        update("gate_x_b", d[4:5, D_LRU:2 * D_LRU])
        update("lru_conv_b", l[4:5, :])
        update("a_param", l[8:9, :] * jax.nn.sigmoid(ins["a_param"][0][...]))
        update("lru_out_g", l[9:10, :])
        update("conv_out_g", l[10:11, :])
        loss_ref[...] = jnp.broadcast_to(dm[5:6, 0:128], (8, 128))
        chip = chip_ref[0]
        acc = jnp.zeros((8, 128), F32)
        for j in range(N_CHIP):
            acc = acc + jnp.where(chip == j, l[0:8, j * 128:(j + 1) * 128], 0.0)
        lw[...] = acc
        update("lru_conv_w", lw[0:4, :])
        update("short_conv_w", lw[5:8, :])
        gates = dsum(gg_ref, 0, D_LRU)
        update("gate_a_w", gates[:, 0:HEAD])
        update("gate_x_w", gates[:, HEAD:2 * HEAD])

    out_shape = []
    for n in names:
        out_shape += [jax.ShapeDtypeStruct(params[n][0].shape, F32)] * 4
    out_shape += [jax.ShapeDtypeStruct((8, 128), F32), jax.ShapeDtypeStruct((8, 6 * D_MODEL), F32)]
    res = pl.pallas_call(
        body, name="small_update", out_shape=out_shape,
        in_specs=[SMEM] + [VMEM] * (4 + nin),
        out_specs=[VMEM] * len(out_shape),
        scratch_shapes=[pltpu.VMEM((8, D_MODEL), F32), pltpu.VMEM((8, D_MODEL), F32), pltpu.VMEM((16, D_LRU), F32),
                        pltpu.VMEM((8, 128), F32)],
        compiler_params=pltpu.CompilerParams(vmem_limit_bytes=VMEM_LIMIT),
    )(mychip, gad, gam, gl, gg, *flat)
    per = {n: res[4 * k:4 * k + 4] for k, n in enumerate(names)}
    return per, res[-2], res[-1]


def _block_diag(w):
    eye = jnp.eye(8, dtype=w.dtype)
    return (eye[:, None, :, None] * w[:, :, None, :]).reshape(8 * HEAD, 8 * HEAD)


def _diag_blocks(g):
    return jnp.concatenate([g[h * HEAD:(h + 1) * HEAD, h * HEAD:(h + 1) * HEAD] for h in range(8)], axis=0)


def kernel(x, c, ada_w, ada_b, norm1_g, w_in, lru_conv_w, lru_conv_b, gate_a_w, gate_a_b, gate_x_w, gate_x_b, a_param, short_conv_w, lru_out_g, conv_out_g, w_out, norm2_g, w_mlp1, w_mlp2, final_g, loss_target, m_ada_w, m_ada_b, m_norm1_g, m_w_in, m_lru_conv_w, m_lru_conv_b, m_gate_a_w, m_gate_a_b, m_gate_x_w, m_gate_x_b, m_a_param, m_short_conv_w, m_lru_out_g, m_conv_out_g, m_w_out, m_norm2_g, m_w_mlp1, m_w_mlp2, m_final_g, v_ada_w, v_ada_b, v_norm1_g, v_w_in, v_lru_conv_w, v_lru_conv_b, v_gate_a_w, v_gate_a_b, v_gate_x_w, v_gate_x_b, v_a_param, v_short_conv_w, v_lru_out_g, v_conv_out_g, v_w_out, v_norm2_g, v_w_mlp1, v_w_mlp2, v_final_g):
    xi, yi, ci = lax.axis_index("x"), lax.axis_index("y"), lax.axis_index("c")
    mychip = 2 * xi + yi
    me = 4 * xi + 2 * yi + ci

    c_blk = jnp.zeros((8, D_MODEL), F32).at[0:1].set(c)
    cw_blk = jnp.zeros((8, 128), F32).at[0:4].set(lru_conv_w[0]).at[4:7].set(short_conv_w[0])
    c_g, cw_g = _allgather8("allgather_cond", [c_blk, cw_blk])
    c_all = c_g.reshape(8, 8, D_MODEL)[:, 0]
    cw_g = cw_g.reshape(4, 2, 8, 128)[:, 0]
    lcw = cw_g[:, 0:4].transpose(1, 0, 2).reshape(4, D_LRU)
    scw = cw_g[:, 4:7].transpose(1, 0, 2).reshape(3, D_LRU)

    mod_loc = _mod_matmul(c_all, ada_w[0])
    (mod_g,) = _allgather8("allgather_mod", [mod_loc])
    mod_all = mod_g.reshape(4, 2, 8, 6 * D_MODEL // 4)[:, 0].transpose(1, 0, 2).reshape(8, 6 * D_MODEL) + ada_b
    mod_pad = jnp.pad(mod_all.reshape(8, 6, D_MODEL), ((0, 0), (0, 2), (0, 0)))
    mod = lax.dynamic_slice_in_dim(mod_pad, me, 1, axis=0).reshape(8, D_MODEL)

    own_in, own_out, own_w1, own_w2 = [w[0].astype(BF16) for w in (w_in, w_out, w_mlp1, w_mlp2)]
    win_all, wout_all = _allgather_weights([own_in, own_out])
    win, wout = (win_all, own_in), (wout_all, own_out)
    chip = mychip.reshape(1).astype(jnp.int32)
    core = ci.reshape(1).astype(jnp.int32)

    vecd = jnp.concatenate([norm1_g, norm2_g, final_g[None, :], jnp.concatenate([gate_a_b, gate_x_b], axis=1),
                            jnp.zeros((4, D_MODEL), F32)], axis=0)
    vecl = jnp.concatenate([lcw, lru_conv_b, scw, a_param, lru_out_g, conv_out_g, jnp.zeros((5, D_LRU), F32)], axis=0)
    gab = jnp.concatenate([_block_diag(gate_a_w[0]), _block_diag(gate_x_w[0])], axis=1).astype(BF16)
    a64 = _block_diag(jnp.full((8, HEAD, HEAD), 1.0 / HEAD, F32)).astype(BF16)

    hb, proj, hl, ycat, mixed, x1, w1_all, w2_all = _mix_fwd(
        chip, x[0], mod, vecd, vecl, win, wout, gab, a64, [own_w1, own_w2])
    dx1, act, dz, dmo, h2b, accm = _mlp_fwd_bwd(
        chip, x1, loss_target[0], mod, vecd, (w1_all, own_w1), (w2_all, own_w2))

    def sibling_sum(tag, grads):
        recv = _swap_halves("rs_swap_halves_" + tag, grads)
        return [_add_sibling("rs_add_sibling_%s%d" % (tag, k), g, r, core) for k, (g, r) in enumerate(zip(grads, recv))]

    parts_mlp = sibling_sum("mlp", [_wgrad("wgrad_mlp1", h2b, dz, 0, FF_BLK, BF16),
                                    _wgrad("wgrad_mlp2", act, dmo, FF_BLK, 0, BF16)])
    grad_x, dproj, dmixed, xlb, dgb, accd, accl, q_w1, q_w2 = _mix_bwd(
        chip, dx1, x[0], mixed, proj, hl, mod, vecd, vecl, win, wout, gab, a64, parts_mlp)
    parts_mix = sibling_sum("mix", [_wgrad("wgrad_in", hb, dproj, 0, WIN_BLK, BF16),
                                    _wgrad("wgrad_out", ycat, dmixed, WOUT_BLK, 0, BF16)])
    g_gate = _wgrad("wgrad_gate", xlb, dgb, 0, 0, F32)[0]

    def reduced(tag, parts, landed):
        own = [_add_chips("rs_add_chips_%s%d" % (tag, k), chip, p, q) for k, (p, q) in enumerate(zip(parts, landed))]
        return own, _swap_reduced("rs_swap_reduced_" + tag, own)

    own_mlp, sib_mlp = reduced("mlp", parts_mlp, [q_w1, q_w2])
    (res_w1, res_w2), landed_mix = _adam(
        "adam_mlp", core, [(w_mlp1[0], own_mlp[0], sib_mlp[0], m_w_mlp1[0], v_w_mlp1[0]),
                           (w_mlp2[0], own_mlp[1], sib_mlp[1], m_w_mlp2[0], v_w_mlp2[0])], carry=parts_mix)
    own_mix, sib_mix = reduced("mix", parts_mix, landed_mix)
    (res_win,), _ = _adam("adam_w_in", core, [(w_in[0], own_mix[0], sib_mix[0], m_w_in[0], v_w_in[0])])
    (res_wout,), _ = _adam("adam_w_out", core, [(w_out[0], own_mix[1], sib_mix[1], m_w_out[0], v_w_out[0])])

    gg_blk = jnp.concatenate([_diag_blocks(g_gate[:, 0:D_LRU]), _diag_blocks(g_gate[:, D_LRU:2 * D_LRU])], axis=1)
    gad, gam, gl, gg = _allgather8("allgather_small_grads", [accd, accm, accl, gg_blk])

    params = {
        "ada_b": (ada_b, m_ada_b, v_ada_b), "norm1_g": (norm1_g, m_norm1_g, v_norm1_g),
        "lru_conv_b": (lru_conv_b, m_lru_conv_b, v_lru_conv_b),
        "gate_a_w": tuple(t.reshape(D_LRU, HEAD) for t in (gate_a_w, m_gate_a_w, v_gate_a_w)),
        "gate_a_b": (gate_a_b, m_gate_a_b, v_gate_a_b),
        "gate_x_w": tuple(t.reshape(D_LRU, HEAD) for t in (gate_x_w, m_gate_x_w, v_gate_x_w)),
        "gate_x_b": (gate_x_b, m_gate_x_b, v_gate_x_b), "a_param": (a_param, m_a_param, v_a_param),
        "lru_conv_w": tuple(t[0] for t in (lru_conv_w, m_lru_conv_w, v_lru_conv_w)),
        "short_conv_w": tuple(t[0] for t in (short_conv_w, m_short_conv_w, v_short_conv_w)),
        "lru_out_g": (lru_out_g, m_lru_out_g, v_lru_out_g), "conv_out_g": (conv_out_g, m_conv_out_g, v_conv_out_g),
        "norm2_g": (norm2_g, m_norm2_g, v_norm2_g),
        "final_g": tuple(t[None, :] for t in (final_g, m_final_g, v_final_g)),
    }
    small, loss_blk, dmod_cols = _small_update(gad, gam, gl, gg, chip, params)
    loss = loss_blk[0, 0]

    ncol = 6 * D_MODEL // N_CHIP
    dmod_loc = lax.dynamic_slice_in_dim(dmod_cols, mychip * ncol, ncol, axis=1)
    sct = (c_all * jax.nn.sigmoid(c_all)).T
    ada = _ada_grad_adam(sct, dmod_loc, ada_w[0], m_ada_w[0], v_ada_w[0])

    res = {"ada_w": ada, "w_in": res_win, "w_out": res_wout, "w_mlp1": res_w1, "w_mlp2": res_w2}
    res = {n: tuple(t[None] for t in r) for n, r in res.items()}
    shapes = {"gate_a_w": gate_a_w.shape, "gate_x_w": gate_x_w.shape, "lru_conv_w": lru_conv_w.shape,
              "short_conv_w": short_conv_w.shape, "final_g": final_g.shape}
    for n, t in small.items():
        res[n] = tuple(u.reshape(shapes[n]) if n in shapes else u for u in t)

    order = ["ada_w", "ada_b", "norm1_g", "w_in", "lru_conv_w", "lru_conv_b", "gate_a_w", "gate_a_b", "gate_x_w",
             "gate_x_b", "a_param", "short_conv_w", "lru_out_g", "conv_out_g", "w_out", "norm2_g", "w_mlp1",
             "w_mlp2", "final_g"]
    return (loss, grad_x[None], *[res[n][0] for n in order], *[res[n][1] for n in order],
            *[res[n][2] for n in order], *[res[n][3] for n in order])
```

```python
import jax
import jax.numpy as jnp
from jax import lax
from jax.experimental import pallas as pl
from jax.experimental.pallas import tpu as pltpu

F32 = jnp.float32
BF16 = jnp.bfloat16

D_MODEL = 1024
D_LRU = 512
D_IN = 2560
D_FF = 4096
N_CHIP = 4
WIN_BLK = D_IN // N_CHIP
WOUT_BLK = D_MODEL // N_CHIP
FF_BLK = D_FF // N_CHIP
HEAD = 64
EPS = 1e-6
C_GATE = 8.0
TOKEN_TILE = 256
HALO = 8
VMEM_LIMIT = 60 * 1024 * 1024

ADAM_LR = 0.001
ADAM_B1 = 0.9
ADAM_B2 = 0.999
ADAM_EPS = 1e-08
ADAM_WD = 0.01
ADAM_STEP = 10

MESH = pl.DeviceIdType.MESH
ANY = pl.BlockSpec(memory_space=pl.ANY)
VMEM = pl.BlockSpec(memory_space=pltpu.VMEM)
SMEM = pl.BlockSpec(memory_space=pltpu.SMEM)


def _full(shape, single=False):
    nd = len(shape)
    if single:
        return pl.BlockSpec(shape, lambda *_: (0,) * nd, pipeline_mode=pl.Buffered(1))
    return pl.BlockSpec(shape, lambda *_: (0,) * nd)


def _dot(a, b):
    return jnp.dot(a, b, preferred_element_type=F32)


def _dot_nt(a, b):
    return lax.dot_general(a, b, (((1,), (1,)), ((), ())), preferred_element_type=F32)


def _dot_tn(a, b):
    return lax.dot_general(a, b, (((0,), (0,)), ((), ())), preferred_element_type=F32)


def _gmean(v, a64):
    hi = v.astype(BF16)
    lo = (v - hi.astype(F32)).astype(BF16)
    return _dot(hi, a64) + _dot(lo, a64)


def _gelu(x):
    u = 0.7978845608028654 * (x + 0.044715 * x * x * x)
    t = jnp.tanh(u)
    return 0.5 * x * (1.0 + t), t


def _gelu_grad(x, t):
    du = 0.7978845608028654 * (1.0 + 3.0 * 0.044715 * x * x)
    return 0.5 * (1.0 + t) + 0.5 * x * (1.0 - t * t) * du


def _log1p_pos(y):
    return jnp.where(y < 1e-2, y * (1.0 - y * (0.5 - y * (1.0 / 3.0 - y * 0.25))), jnp.log(1.0 + y))


def _softplus(a):
    return jnp.maximum(a, 0.0) + _log1p_pos(jnp.exp(-jnp.abs(a)))


def _neg_expm1(z):
    series = -z * (1.0 + z * (0.5 + z * (1.0 / 6.0 + z * (1.0 / 24.0 + z * (1.0 / 120.0)))))
    return jnp.where(z > -0.02, series, 1.0 - jnp.exp(z))


def _scan_fwd(a, b, row):
    n = a.shape[0]
    d = 1
    while d < n:
        m = row >= d
        b = jnp.where(m, a * pltpu.roll(b, d, 0) + b, b)
        a = jnp.where(m, a * pltpu.roll(a, d, 0), a)
        d *= 2
    return a, b


def _scan_rev(a, b, row):
    n = a.shape[0]
    d = 1
    while d < n:
        m = row < n - d
        b = jnp.where(m, b + a * pltpu.roll(b, n - d, 0), b)
        a = jnp.where(m, a * pltpu.roll(a, n - d, 0), a)
        d *= 2
    return a, b


def _colsum(v):
    return jnp.sum(v, axis=0, keepdims=True)


def _load_gathered(chip, gathered, own, slot, sems):
    copies = []
    for j in range(N_CHIP):
        @pl.when(chip == j)
        def _(j=j):
            pltpu.make_async_copy(own, slot(j), sems.at[j]).start()

        @pl.when(chip != j)
        def _(j=j):
            pltpu.make_async_copy(gathered.at[j], slot(j), sems.at[j]).start()

        copies.append(pltpu.make_async_copy(own, slot(j), sems.at[j]))
    return copies


def _lru_gates(xlb, gab, gbias, sp, first_row):
    g = _dot(xlb, gab) + gbias
    r = jax.nn.sigmoid(g[:, :D_LRU])
    ig = jax.nn.sigmoid(g[:, D_LRU:])
    la = (-C_GATE) * r * sp
    a = jnp.exp(la)
    msq = jnp.sqrt(_neg_expm1(2.0 * la))
    mult = jnp.where(first_row, 1.0, msq)
    return r, ig, a, msq, mult


def _mix_fwd(chip, x, mod, vecd, vecl, win, wout, gab, a64, mlp_shards):
    s = x.shape[0]
    ts = TOKEN_TILE
    nt = s // ts

    def body(chip_ref, x_ref, mod_ref, vd_ref, vl_ref, win_hbm, win_own, wout_hbm, wout_own, gab_ref, a64_ref,
             w1_own, w2_own,
             hb_ref, proj_ref, hl_ref, ycat_ref, mixed_ref, x1_ref, w1_all, w2_all,
             win_ref, wout_ref, ext_lx, ext_cv, hcar, sems, ag_send, ag_recv):
        i = pl.program_id(0)

        @pl.when(i == 0)
        def _():
            _ag_start((w1_own, w2_own), (w1_all, w2_all), ag_send, ag_recv)
            cps = _load_gathered(chip_ref[0], win_hbm, win_own, lambda j: win_ref.at[j], sems.at[pl.ds(0, N_CHIP)])
            cps += _load_gathered(chip_ref[0], wout_hbm, wout_own,
                                  lambda j: wout_ref.at[pl.ds(j * WOUT_BLK, WOUT_BLK), :],
                                  sems.at[pl.ds(N_CHIP, N_CHIP)])
            ext_lx[0:HALO, :] = jnp.zeros((HALO, D_LRU), F32)
            ext_cv[0:HALO, :] = jnp.zeros((HALO, D_LRU), F32)
            hcar[...] = jnp.zeros_like(hcar)
            for cp in cps:
                cp.wait()

        row = lax.broadcasted_iota(jnp.int32, (ts, D_LRU), 0)
        first_row = jnp.logical_and(row == 0, i == 0)
        xt = x_ref[...]
        shift1, scale1, gate1 = mod_ref[0:1, :], mod_ref[1:2, :], mod_ref[2:3, :]
        r1 = lax.rsqrt(jnp.mean(xt * xt, axis=-1, keepdims=True) + EPS)
        h = (xt * r1) * vd_ref[0:1, :] * (1.0 + scale1) + shift1
        hb = h.astype(BF16)
        hb_ref[...] = hb
        for j in range(N_CHIP):
            proj_ref[:, j * WIN_BLK:(j + 1) * WIN_BLK] = _dot(hb, win_ref[j])
        u_ly = proj_ref[:, 512:1024]
        u_b = proj_ref[:, 1024:1536]

        ext_lx[HALO:HALO + ts, :] = proj_ref[:, 0:512]
        xl = vl_ref[4:5, :] + vl_ref[0:1, :] * ext_lx[pl.ds(5, ts), :]
        for k in range(1, 4):
            xl = xl + vl_ref[k:k + 1, :] * ext_lx[pl.ds(5 + k, ts), :]
        ext_lx[0:HALO, :] = ext_lx[ts:ts + HALO, :]
        sp = _softplus(vl_ref[8:9, :])
        _, ig, a, _, mult = _lru_gates(xl.astype(BF16), gab_ref[...], vd_ref[3:4, :], sp, first_row)
        acum, hloc = _scan_fwd(a, mult * (ig * xl), row)
        hl = hloc + acum * hcar[0:1, :]
        hl_ref[...] = hl
        hcar[0:1, :] = hl_ref[ts - 1:ts, :]
        ge, _ = _gelu(u_ly)
        p = ge * hl
        y_lru = p * lax.rsqrt(_gmean(p * p, a64_ref[...]) + EPS) * vl_ref[9:10, :]
        ycat_ref[:, 0:512] = y_lru.astype(BF16)

        ext_cv[HALO:HALO + ts, :] = proj_ref[:, 1536:2048] * proj_ref[:, 2048:2560]
        q = vl_ref[5:6, :] * ext_cv[pl.ds(6, ts), :]
        for k in range(1, 3):
            q = q + vl_ref[5 + k:6 + k, :] * ext_cv[pl.ds(6 + k, ts), :]
        ext_cv[0:HALO, :] = ext_cv[ts:ts + HALO, :]
        yc = u_b * q
        y_conv = yc * lax.rsqrt(_gmean(yc * yc, a64_ref[...]) + EPS) * vl_ref[10:11, :]
        ycat_ref[:, 512:1024] = y_conv.astype(BF16)

        mixed = _dot(ycat_ref[...], wout_ref[...])
        mixed_ref[...] = mixed
        x1_ref[...] = xt + gate1 * mixed

        @pl.when(i == nt - 1)
        def _():
            _ag_finish((w1_own, w2_own), (w1_all, w2_all), ag_send, ag_recv)

    tile = lambda w: pl.BlockSpec((ts, w), lambda i: (i, 0))
    return pl.pallas_call(
        body, name="mix_fwd", grid=(nt,),
        in_specs=[SMEM, tile(D_MODEL), _full((8, D_MODEL)), _full((8, D_MODEL)), _full((16, D_LRU)),
                  ANY, ANY, ANY, ANY, _full((D_LRU, 2 * D_LRU), True), _full((D_LRU, D_LRU), True), ANY, ANY],
        out_specs=[tile(D_MODEL), tile(D_IN), tile(D_LRU), tile(D_MODEL), tile(D_MODEL), tile(D_MODEL), ANY, ANY],
        out_shape=[jax.ShapeDtypeStruct((s, D_MODEL), BF16), jax.ShapeDtypeStruct((s, D_IN), F32),
                   jax.ShapeDtypeStruct((s, D_LRU), F32), jax.ShapeDtypeStruct((s, D_MODEL), BF16),
                   jax.ShapeDtypeStruct((s, D_MODEL), F32), jax.ShapeDtypeStruct((s, D_MODEL), F32)]
        + [jax.ShapeDtypeStruct((N_CHIP,) + w.shape, w.dtype) for w in mlp_shards],
        scratch_shapes=[pltpu.VMEM((N_CHIP, D_MODEL, WIN_BLK), BF16), pltpu.VMEM((D_MODEL, D_MODEL), BF16),
                        pltpu.VMEM((ts + HALO, D_LRU), F32), pltpu.VMEM((ts + HALO, D_LRU), F32),
                        pltpu.VMEM((HALO, D_LRU), F32), pltpu.SemaphoreType.DMA((2 * N_CHIP,)),
                        pltpu.SemaphoreType.DMA((2 * AG_SEMS,)), pltpu.SemaphoreType.DMA((2 * AG_SEMS,))],
        compiler_params=pltpu.CompilerParams(dimension_semantics=("arbitrary",), vmem_limit_bytes=VMEM_LIMIT),
    )(chip, x, mod, vecd, vecl, *win, *wout, gab, a64, *mlp_shards)


def _mlp_fwd_bwd(chip, x1, target, mod, vecd, w1, w2):
    s = x1.shape[0]
    ts = TOKEN_TILE
    nt = s // ts

    def body(chip_ref, x1_ref, tg_ref, mod_ref, vd_ref, w1_hbm, w1_own, w2_hbm, w2_own,
             dx1_ref, act_ref, dz_ref, dmo_ref, h2_ref, acc_ref, w1_v, w2_v, rz_v, sems):
        i = pl.program_id(0)

        @pl.when(i == 0)
        def _():
            cps = _load_gathered(chip_ref[0], w1_hbm, w1_own, lambda j: w1_v.at[j], sems.at[pl.ds(0, N_CHIP)])
            cps += _load_gathered(chip_ref[0], w2_hbm, w2_own, lambda j: w2_v.at[j], sems.at[pl.ds(N_CHIP, N_CHIP)])
            acc_ref[...] = jnp.zeros_like(acc_ref)
            for cp in cps:
                cp.wait()

        xt = x1_ref[...]
        shift2, scale2, gate2 = mod_ref[3:4, :], mod_ref[4:5, :], mod_ref[5:6, :]
        g2, gf = vd_ref[1:2, :], vd_ref[2:3, :]
        r2 = lax.rsqrt(jnp.mean(xt * xt, axis=-1, keepdims=True) + EPS)
        n2 = xt * r2
        h2b = (n2 * g2 * (1.0 + scale2) + shift2).astype(BF16)
        h2_ref[...] = h2b
        mo = jnp.zeros((ts, D_MODEL), F32)
        for j in range(N_CHIP):
            rz = jnp.maximum(_dot(h2b, w1_v[j]), 0.0)
            rz_v[j] = rz
            actb = (rz * rz).astype(BF16)
            act_ref[:, j * FF_BLK:(j + 1) * FF_BLK] = actb
            mo = mo + _dot(actb, w2_v[j])
        x2 = xt + gate2 * mo
        r3 = lax.rsqrt(jnp.mean(x2 * x2, axis=-1, keepdims=True) + EPS)
        n3 = x2 * r3
        e = n3 * gf - tg_ref[...]
        loss = (0.5 / D_MODEL) * jnp.sum(_colsum(e * e), axis=1, keepdims=True)
        dy = e * (1.0 / D_MODEL)
        acc_ref[4:5, :] += _colsum(dy * n3)
        acc_ref[5:6, :] += jnp.broadcast_to(loss, (1, D_MODEL))
        dn3 = dy * gf
        dx2 = r3 * (dn3 - n3 * jnp.mean(dn3 * n3, axis=-1, keepdims=True))
        acc_ref[2:3, :] += _colsum(dx2 * mo)
        dmob = (dx2 * gate2).astype(BF16)
        dmo_ref[...] = dmob
        dh2 = jnp.zeros((ts, D_MODEL), F32)
        for j in range(N_CHIP):
            dzb = (_dot_nt(dmob, w2_v[j]) * (2.0 * rz_v[j])).astype(BF16)
            dz_ref[:, j * FF_BLK:(j + 1) * FF_BLK] = dzb
            dh2 = dh2 + _dot_nt(dzb, w1_v[j])
        acc_ref[1:2, :] += _colsum(dh2 * (n2 * g2))
        acc_ref[0:1, :] += _colsum(dh2)
        dhn2 = dh2 * (1.0 + scale2)
        acc_ref[3:4, :] += _colsum(dhn2 * n2)
        dn2 = dhn2 * g2
        dx1_ref[...] = dx2 + r2 * (dn2 - n2 * jnp.mean(dn2 * n2, axis=-1, keepdims=True))

    tile = lambda w: pl.BlockSpec((ts, w), lambda i: (i, 0))
    return pl.pallas_call(
        body, name="mlp_fwd_bwd", grid=(nt,),
        in_specs=[SMEM, tile(D_MODEL), tile(D_MODEL), _full((8, D_MODEL)), _full((8, D_MODEL)), ANY, ANY, ANY, ANY],
        out_specs=[tile(D_MODEL), tile(D_FF), tile(D_FF), tile(D_MODEL), tile(D_MODEL), _full((8, D_MODEL))],
        out_shape=[jax.ShapeDtypeStruct((s, D_MODEL), F32), jax.ShapeDtypeStruct((s, D_FF), BF16),
                   jax.ShapeDtypeStruct((s, D_FF), BF16), jax.ShapeDtypeStruct((s, D_MODEL), BF16),
                   jax.ShapeDtypeStruct((s, D_MODEL), BF16), jax.ShapeDtypeStruct((8, D_MODEL), F32)],
        scratch_shapes=[pltpu.VMEM((N_CHIP, D_MODEL, FF_BLK), BF16), pltpu.VMEM((N_CHIP, FF_BLK, D_MODEL), BF16),
                        pltpu.VMEM((N_CHIP, ts, FF_BLK), F32), pltpu.SemaphoreType.DMA((2 * N_CHIP,))],
        compiler_params=pltpu.CompilerParams(dimension_semantics=("arbitrary",), vmem_limit_bytes=VMEM_LIMIT),
    )(chip, x1, target, mod, vecd, *w1, *w2)


def _mix_bwd(chip, dx1, x, mixed, proj, hl, mod, vecd, vecl, win, wout, gab, a64, mlp_parts):
    s = x.shape[0]
    ts = TOKEN_TILE
    nt = s // ts
    hpt = ts // HALO

    def body(chip_ref, dx1_ref, x_ref, mixed_ref, proj_ref, projh_ref, hl_ref, hlh_ref, mod_ref, vd_ref, vl_ref,
             win_hbm, win_own, wout_hbm, wout_own, gab_ref, a64_ref, p1_ref, p2_ref,
             gx_ref, dproj_ref, dmixed_ref, xlb_ref, dgb_ref, accd_ref, accl_ref, q1_ref, q2_ref,
             win_ref, wout_ref, ext_lx, ext_cv, ext_hl, ext_dxl, ext_dq, gbuf, gcar, acar, sems, x_send, x_recv):
        i = pl.program_id(0)
        ri = nt - 1 - i

        @pl.when(i == 0)
        def _():
            for cp in _xchg_copies((p1_ref, p2_ref), (q1_ref, q2_ref), x_send, x_recv):
                cp.start()
            cps = _load_gathered(chip_ref[0], win_hbm, win_own, lambda j: win_ref.at[j], sems.at[pl.ds(0, N_CHIP)])
            cps += _load_gathered(chip_ref[0], wout_hbm, wout_own,
                                  lambda j: wout_ref.at[pl.ds(j * WOUT_BLK, WOUT_BLK), :],
                                  sems.at[pl.ds(N_CHIP, N_CHIP)])
            for cp in cps:
                cp.wait()
            accd_ref[...] = jnp.zeros_like(accd_ref)
            accl_ref[...] = jnp.zeros_like(accl_ref)
            ext_dxl[ts:ts + HALO, :] = jnp.zeros((HALO, D_LRU), F32)
            ext_dq[ts:ts + HALO, :] = jnp.zeros((HALO, D_LRU), F32)
            gcar[...] = jnp.zeros_like(gcar)
            acar[...] = jnp.zeros_like(acar)

        row = lax.broadcasted_iota(jnp.int32, (ts, D_LRU), 0)
        first_row = jnp.logical_and(row == 0, ri == 0)
        halo_on = jnp.where(ri == 0, 0.0, 1.0)
        shift1, scale1, gate1 = mod_ref[0:1, :], mod_ref[1:2, :], mod_ref[2:3, :]
        g1 = vd_ref[0:1, :]
        a64m = a64_ref[...]
        lg, cg = vl_ref[9:10, :], vl_ref[10:11, :]

        dx1 = dx1_ref[...]
        accd_ref[2:3, :] += _colsum(dx1 * mixed_ref[...])
        dmb = (dx1 * gate1).astype(BF16)
        dmixed_ref[...] = dmb
        dycat = _dot_nt(dmb, wout_ref[...])
        dyl = dycat[:, 0:512]
        dyv = dycat[:, 512:1024]

        u_ly = proj_ref[:, 512:1024]
        u_b = proj_ref[:, 1024:1536]
        u_c = proj_ref[:, 1536:2048]
        u_v = proj_ref[:, 2048:2560]
        ext_lx[0:HALO, :] = projh_ref[:, 0:512] * halo_on
        ext_lx[HALO:HALO + ts, :] = proj_ref[:, 0:512]
        xl = vl_ref[4:5, :] + vl_ref[0:1, :] * ext_lx[pl.ds(5, ts), :]
        for k in range(1, 4):
            xl = xl + vl_ref[k:k + 1, :] * ext_lx[pl.ds(5 + k, ts), :]
        xlb = xl.astype(BF16)
        xlb_ref[...] = xlb
        sp = _softplus(vl_ref[8:9, :])
        r, ig, a, msq, mult = _lru_gates(xlb, gab_ref[...], vd_ref[3:4, :], sp, first_row)
        hl = hl_ref[...]
        ge, th = _gelu(u_ly)
        p = ge * hl
        rl = lax.rsqrt(_gmean(p * p, a64m) + EPS)
        nl = p * rl
        ext_cv[0:HALO, :] = projh_ref[:, 1536:2048] * projh_ref[:, 2048:2560] * halo_on
        ext_cv[HALO:HALO + ts, :] = u_c * u_v
        q = vl_ref[5:6, :] * ext_cv[pl.ds(6, ts), :]
        for k in range(1, 3):
            q = q + vl_ref[5 + k:6 + k, :] * ext_cv[pl.ds(6 + k, ts), :]
        yc = u_b * q
        rc = lax.rsqrt(_gmean(yc * yc, a64m) + EPS)
        nc = yc * rc

        accl_ref[9:10, :] += _colsum(dyl * nl)
        dnl = dyl * lg
        dp = rl * (dnl - nl * _gmean(dnl * nl, a64m))
        dproj_ref[:, 512:1024] = ((dp * hl) * _gelu_grad(u_ly, th)).astype(BF16)
        a_next = jnp.where(row == ts - 1, acar[0:1, :], pltpu.roll(a, ts - 1, 0))
        acum, gloc = _scan_rev(a_next, dp * ge, row)
        gbuf[...] = gloc + acum * gcar[0:1, :]
        gcar[0:1, :] = gbuf[0:1, :]
        ext_hl[0:HALO, :] = hlh_ref[...] * halo_on
        ext_hl[HALO:HALO + ts, :] = hl
        acar[...] = a[0:HALO, :]
        gt = gbuf[...]
        da = gt * ext_hl[pl.ds(HALO - 1, ts), :]
        dmult = gt * ig * xl
        di = gt * mult * xl
        dxl = gt * mult * ig
        dla = da * a - jnp.where(first_row, 0.0, dmult * a * a / msq)
        accl_ref[8:9, :] += _colsum(dla * ((-C_GATE) * r))
        dra = dla * ((-C_GATE) * sp) * r * (1.0 - r)
        dia = di * ig * (1.0 - ig)
        accd_ref[4:5, 0:D_LRU] += _colsum(dra)
        accd_ref[4:5, D_LRU:2 * D_LRU] += _colsum(dia)
        dgb_ref[:, 0:D_LRU] = dra.astype(BF16)
        dgb_ref[:, D_LRU:2 * D_LRU] = dia.astype(BF16)
        dxl = dxl + _dot_nt(dgb_ref[...], gab_ref[...])
        accl_ref[4:5, :] += _colsum(dxl)
        for k in range(4):
            accl_ref[k:k + 1, :] += _colsum(dxl * ext_lx[pl.ds(5 + k, ts), :])
        ext_dxl[0:ts, :] = dxl
        du_lx = vl_ref[0:1, :] * ext_dxl[pl.ds(3, ts), :]
        for k in range(1, 4):
            du_lx = du_lx + vl_ref[k:k + 1, :] * ext_dxl[pl.ds(3 - k, ts), :]
        ext_dxl[ts:ts + HALO, :] = ext_dxl[0:HALO, :]
        dproj_ref[:, 0:512] = du_lx.astype(BF16)

        accl_ref[10:11, :] += _colsum(dyv * nc)
        dnc = dyv * cg
        dyc = rc * (dnc - nc * _gmean(dnc * nc, a64m))
        dproj_ref[:, 1024:1536] = (dyc * q).astype(BF16)
        dq = dyc * u_b
        for k in range(3):
            accl_ref[5 + k:6 + k, :] += _colsum(dq * ext_cv[pl.ds(6 + k, ts), :])
        ext_dq[0:ts, :] = dq
        dcv = vl_ref[5:6, :] * ext_dq[pl.ds(2, ts), :]
        for k in range(1, 3):
            dcv = dcv + vl_ref[5 + k:6 + k, :] * ext_dq[pl.ds(2 - k, ts), :]
        ext_dq[ts:ts + HALO, :] = ext_dq[0:HALO, :]
        dproj_ref[:, 1536:2048] = (dcv * u_v).astype(BF16)
        dproj_ref[:, 2048:2560] = (dcv * u_c).astype(BF16)

        dh = _dot_nt(dproj_ref[:, 0:WIN_BLK], win_ref[0])
        for j in range(1, N_CHIP):
            dh = dh + _dot_nt(dproj_ref[:, j * WIN_BLK:(j + 1) * WIN_BLK], win_ref[j])
        xt = x_ref[...]
        r1 = lax.rsqrt(jnp.mean(xt * xt, axis=-1, keepdims=True) + EPS)
        n1 = xt * r1
        accd_ref[1:2, :] += _colsum(dh * (n1 * g1))
        accd_ref[0:1, :] += _colsum(dh)
        dhn1 = dh * (1.0 + scale1)
        accd_ref[3:4, :] += _colsum(dhn1 * n1)
        dn1 = dhn1 * g1
        gx_ref[...] = dx1 + r1 * (dn1 - n1 * jnp.mean(dn1 * n1, axis=-1, keepdims=True))

        @pl.when(i == nt - 1)
        def _():
            for cp in _xchg_copies((p1_ref, p2_ref), (q1_ref, q2_ref), x_send, x_recv):
                cp.wait()

    tile = lambda w: pl.BlockSpec((ts, w), lambda i: (nt - 1 - i, 0))
    halo = lambda w: pl.BlockSpec((HALO, w), lambda i: (jnp.maximum((nt - 1 - i) * hpt - 1, 0), 0))
    ext = pltpu.VMEM((ts + HALO, D_LRU), F32)
    return pl.pallas_call(
        body, name="mix_bwd", grid=(nt,),
        in_specs=[SMEM, tile(D_MODEL), tile(D_MODEL), tile(D_MODEL), tile(D_IN), halo(D_IN), tile(D_LRU), halo(D_LRU),
                  _full((8, D_MODEL)), _full((8, D_MODEL)), _full((16, D_LRU)),
                  ANY, ANY, ANY, ANY, _full((D_LRU, 2 * D_LRU), True), _full((D_LRU, D_LRU), True), ANY, ANY],
        out_specs=[tile(D_MODEL), tile(D_IN), tile(D_MODEL), tile(D_LRU), tile(2 * D_LRU),
                   _full((8, D_MODEL)), _full((16, D_LRU)), ANY, ANY],
        out_shape=[jax.ShapeDtypeStruct((s, D_MODEL), F32), jax.ShapeDtypeStruct((s, D_IN), BF16),
                   jax.ShapeDtypeStruct((s, D_MODEL), BF16), jax.ShapeDtypeStruct((s, D_LRU), BF16),
                   jax.ShapeDtypeStruct((s, 2 * D_LRU), BF16),
                   jax.ShapeDtypeStruct((8, D_MODEL), F32), jax.ShapeDtypeStruct((16, D_LRU), F32)]
        + [jax.ShapeDtypeStruct((3,) + p.shape[1:], p.dtype) for p in mlp_parts],
        scratch_shapes=[pltpu.VMEM((N_CHIP, D_MODEL, WIN_BLK), BF16), pltpu.VMEM((D_MODEL, D_MODEL), BF16),
                        ext, ext, ext, ext, ext, pltpu.VMEM((ts, D_LRU), F32),
                        pltpu.VMEM((HALO, D_LRU), F32), pltpu.VMEM((HALO, D_LRU), F32),
                        pltpu.SemaphoreType.DMA((2 * N_CHIP,)),
                        pltpu.SemaphoreType.DMA((6,)), pltpu.SemaphoreType.DMA((6,))],
        compiler_params=pltpu.CompilerParams(dimension_semantics=("arbitrary",), vmem_limit_bytes=VMEM_LIMIT),
    )(chip, dx1, x, mixed, proj, proj, hl, hl, mod, vecd, vecl, *win, *wout, gab, a64, *mlp_parts)


def _wgrad(name, a, b, a_blk, b_blk, out_dtype):
    s = a.shape[0]
    aw = a_blk or a.shape[1]
    bw = b_blk or b.shape[1]
    nblk = N_CHIP if (a_blk or b_blk) else 1

    def body(a_ref, b_ref, o_ref):
        o_ref[0] = _dot_tn(a_ref[...], b_ref[...]).astype(out_dtype)

    return pl.pallas_call(
        body, name=name, grid=(nblk,),
        in_specs=[pl.BlockSpec((s, aw), (lambda j: (0, j)) if a_blk else (lambda j: (0, 0))),
                  pl.BlockSpec((s, bw), (lambda j: (0, j)) if b_blk else (lambda j: (0, 0)))],
        out_specs=pl.BlockSpec((1, aw, bw), lambda j: (j, 0, 0)),
        out_shape=jax.ShapeDtypeStruct((nblk, aw, bw), out_dtype),
        compiler_params=pltpu.CompilerParams(dimension_semantics=("arbitrary",), vmem_limit_bytes=VMEM_LIMIT),
    )(a, b)


def _mod_matmul(c_all, ada_w_loc):
    n = ada_w_loc.shape[1]
    cb = 512

    def body(c_ref, w_ref, o_ref):
        c = c_ref[...]
        sc = c * jax.nn.sigmoid(c)
        o_ref[...] = _dot(sc.astype(BF16), w_ref[...].astype(BF16))

    return pl.pallas_call(
        body, name="mod_matmul", grid=(n // cb,),
        in_specs=[_full((8, D_MODEL)), pl.BlockSpec((D_MODEL, cb), lambda j: (0, j))],
        out_specs=pl.BlockSpec((8, cb), lambda j: (0, j)),
        out_shape=jax.ShapeDtypeStruct((8, n), F32),
        compiler_params=pltpu.CompilerParams(dimension_semantics=("arbitrary",), vmem_limit_bytes=VMEM_LIMIT),
    )(c_all, ada_w_loc)


def _adam_math(w, g, m, v):
    m = ADAM_B1 * m + (1.0 - ADAM_B1) * g
    v = ADAM_B2 * v + (1.0 - ADAM_B2) * (g * g)
    m_hat = m / (1.0 - ADAM_B1 ** ADAM_STEP)
    v_hat = v / (1.0 - ADAM_B2 ** ADAM_STEP)
    delta = (-ADAM_LR) * (m_hat / (jnp.sqrt(v_hat) + ADAM_EPS) + ADAM_WD * w)
    return delta, m, v


def _adam(name, core, shards):
    n = len(shards)
    r, c = shards[0][0].shape
    half = r // 2
    rb = min(half, 128)
    nh = half // rb

    def body(core_ref, *refs):
        ins, outs = refs[:5 * n], refs[5 * n:]
        mine = (pl.program_id(0) // nh) == core_ref[0]
        for k in range(n):
            w_ref, go_ref, gs_ref, m_ref, v_ref = ins[5 * k:5 * k + 5]
            g_ref, d_ref, mo_ref, vo_ref = outs[4 * k:4 * k + 4]
            g = jnp.where(mine, go_ref[...], gs_ref[...])
            g_ref[...] = g
            d_ref[...], mo_ref[...], vo_ref[...] = _adam_math(w_ref[...], g, m_ref[...], v_ref[...])

    spec = pl.BlockSpec((rb, c), lambda i, core_ref: (i, 0))
    hspec = pl.BlockSpec((rb, c), lambda i, core_ref: (i % nh, 0))
    sds = jax.ShapeDtypeStruct((r, c), F32)
    res = pl.pallas_call(
        body, name=name,
        grid_spec=pltpu.PrefetchScalarGridSpec(
            num_scalar_prefetch=1, grid=(r // rb,),
            in_specs=[spec, hspec, hspec, spec, spec] * n, out_specs=[spec] * (4 * n)),
        out_shape=[sds] * (4 * n),
        compiler_params=pltpu.CompilerParams(dimension_semantics=("arbitrary",), vmem_limit_bytes=VMEM_LIMIT),
    )(core, *[t for s in shards for t in s])
    return [res[4 * k:4 * k + 4] for k in range(n)]


def _ada_grad_adam(sct, dmod_loc, w, m, v):
    r, c = w.shape
    rb = 128

    def body(s_ref, dm_ref, w_ref, m_ref, v_ref, g_ref, d_ref, mo_ref, vo_ref):
        g = s_ref[:, 0:1] * dm_ref[0:1, :]
        for b in range(1, 8):
            g = g + s_ref[:, b:b + 1] * dm_ref[b:b + 1, :]
        g_ref[...] = g
        d_ref[...], mo_ref[...], vo_ref[...] = _adam_math(w_ref[...], g, m_ref[...], v_ref[...])

    spec = pl.BlockSpec((rb, c), lambda i: (i, 0))
    sds = jax.ShapeDtypeStruct((r, c), F32)
    return pl.pallas_call(
        body, name="ada_grad_adam", grid=(r // rb,),
        in_specs=[pl.BlockSpec((rb, 8), lambda i: (i, 0)), _full((8, c)), spec, spec, spec],
        out_specs=[spec] * 4, out_shape=[sds] * 4,
        compiler_params=pltpu.CompilerParams(dimension_semantics=("arbitrary",), vmem_limit_bytes=VMEM_LIMIT),
    )(sct, dmod_loc, w, m, v)


def _position():
    x, y, c = lax.axis_index("x"), lax.axis_index("y"), lax.axis_index("c")
    chips = [(1 - x, y), (x, 1 - y), (1 - x, 1 - y)]
    return x, y, c, chips


def _allgather8(name, arrs):
    na = len(arrs)

    def body(*refs):
        ins, outs = refs[:na], refs[na:2 * na]
        send_sems, recv_sems, local_sems = refs[2 * na:]
        x, y, c, chips = _position()
        me, sibling = (x, y, c), (x, y, 1 - c)
        first, passed, local = [], [], []
        for a in range(na):
            m_per = ins[a].shape[0]

            def rows(px, py, pc, a=a, m_per=m_per):
                return outs[a].at[pl.ds((4 * px + 2 * py + pc) * m_per, m_per), :]

            def copy(k, block, to, src=None, a=a, rows=rows):
                return pltpu.make_async_remote_copy(
                    src_ref=rows(*block) if src is None else src, dst_ref=rows(*block),
                    send_sem=send_sems.at[7 * a + k], recv_sem=recv_sems.at[7 * a + k],
                    device_id=to, device_id_type=MESH)

            mine = pltpu.make_async_copy(ins[a], rows(*me), local_sems.at[a])
            mine.start()
            local.append(mine)
            f = [copy(0, me, sibling, src=ins[a])]
            f += [copy(1 + j, me, (*chip, c), src=ins[a]) for j, chip in enumerate(chips)]
            for cp in f:
                cp.start()
            first.append((f, copy))
        for a in range(na):
            f, copy = first[a]
            p = [copy(4 + j, (*chip, c), sibling) for j, chip in enumerate(chips)]
            for j, chip in enumerate(chips):
                copy(1 + j, (*chip, c), me).wait_recv()
                p[j].start()
            passed.append(p)
        for a in range(na):
            f, copy = first[a]
            copy(0, sibling, me).wait_recv()
            for j, chip in enumerate(chips):
                copy(4 + j, (*chip, 1 - c), me).wait_recv()
            for cp in f + passed[a]:
                cp.wait_send()
            local[a].wait()

    return pl.pallas_call(
        body, name=name,
        out_shape=[jax.ShapeDtypeStruct((8 * a.shape[0], a.shape[1]), a.dtype) for a in arrs],
        in_specs=[VMEM] * na, out_specs=[VMEM] * na,
        scratch_shapes=[pltpu.SemaphoreType.DMA((7 * na,)), pltpu.SemaphoreType.DMA((7 * na,)),
                        pltpu.SemaphoreType.DMA((na,))],
        compiler_params=pltpu.CompilerParams(vmem_limit_bytes=VMEM_LIMIT),
    )(*arrs)


AG_SEMS = 7


def _ag_copies(ins, outs, send_sems, recv_sems):
    x, y, c, chips = _position()
    sibling = (x, y, 1 - c)
    xn, yn, dg = [2 * chip[0] + chip[1] for chip in chips]
    to_x, to_y = (1 - x, y, c), (x, 1 - y, c)
    res = []
    for a in range(len(ins)):
        half = ins[a].shape[0] // 2
        quarter = half // 2

        def copy(k, dst, to, src=None, a=a):
            return pltpu.make_async_remote_copy(
                src_ref=dst if src is None else src, dst_ref=dst,
                send_sem=send_sems.at[AG_SEMS * a + k], recv_sem=recv_sems.at[AG_SEMS * a + k],
                device_id=to, device_id_type=MESH)

        def rows(chip, pc, q=None, a=a, half=half, quarter=quarter):
            if q is None:
                return outs[a].at[chip, pl.ds(pc * half, half), :]
            return outs[a].at[chip, pl.ds(pc * half + q * quarter, quarter), :]

        own = ins[a].at[pl.ds(c * half, half), :]
        mine = rows(2 * x + y, c)
        res.append(dict(
            sends=[copy(0, mine, to_x, src=own), copy(1, mine, to_y, src=own)],
            from_x=copy(0, rows(xn, c), to_x), from_y=copy(1, rows(yn, c), to_y),
            relay_y=copy(2, rows(xn, c, 0), to_y), relay_x=copy(3, rows(yn, c, 1), to_x),
            from_y_relay=copy(2, rows(dg, c, 0), to_y), from_x_relay=copy(3, rows(dg, c, 1), to_x),
            pass_on=[copy(4, rows(xn, c), sibling), copy(5, rows(yn, c), sibling), copy(6, rows(dg, c), sibling)],
            from_sibling=[copy(4, rows(xn, 1 - c), sibling), copy(5, rows(yn, 1 - c), sibling),
                          copy(6, rows(dg, 1 - c), sibling)]))
    return res


def _ag_start(ins, outs, send_sems, recv_sems):
    for cps in _ag_copies(ins, outs, send_sems, recv_sems):
        for cp in cps["sends"]:
            cp.start()


def _ag_finish(ins, outs, send_sems, recv_sems):
    copies = _ag_copies(ins, outs, send_sems, recv_sems)
    for cps in copies:
        cps["from_x"].wait_recv()
        cps["relay_y"].start()
        cps["pass_on"][0].start()
        cps["from_y"].wait_recv()
        cps["relay_x"].start()
        cps["pass_on"][1].start()
    for cps in copies:
        cps["from_y_relay"].wait_recv()
        cps["from_x_relay"].wait_recv()
        cps["pass_on"][2].start()
    for cps in copies:
        for cp in cps["from_sibling"]:
            cp.wait_recv()
        for cp in cps["sends"] + [cps["relay_y"], cps["relay_x"]] + cps["pass_on"]:
            cp.wait_send()


def _allgather_weights(shards):
    na = len(shards)

    def body(*refs):
        ins, outs = refs[:na], refs[na:2 * na]
        send_sems, recv_sems = refs[2 * na:]
        _ag_start(ins, outs, send_sems, recv_sems)
        _ag_finish(ins, outs, send_sems, recv_sems)

    return pl.pallas_call(
        body, name="allgather_weights",
        out_shape=[jax.ShapeDtypeStruct((N_CHIP,) + s.shape, s.dtype) for s in shards],
        in_specs=[ANY] * na, out_specs=[ANY] * na,
        scratch_shapes=[pltpu.SemaphoreType.DMA((AG_SEMS * na,)), pltpu.SemaphoreType.DMA((AG_SEMS * na,))],
    )(*shards)


def _swap_halves(name, grads):
    na = len(grads)

    def body(*refs):
        ins, outs = refs[:na], refs[na:2 * na]
        send_sems, recv_sems = refs[2 * na:]
        x, y, c, _ = _position()
        cps = []
        for a in range(na):
            half = ins[a].shape[1] // 2
            cp = pltpu.make_async_remote_copy(
                src_ref=ins[a].at[:, pl.ds((1 - c) * half, half), :], dst_ref=outs[a],
                send_sem=send_sems.at[a], recv_sem=recv_sems.at[a],
                device_id=(x, y, 1 - c), device_id_type=MESH)
            cp.start()
            cps.append(cp)
        for cp in cps:
            cp.wait()

    return pl.pallas_call(
        body, name=name,
        out_shape=[jax.ShapeDtypeStruct((g.shape[0], g.shape[1] // 2, g.shape[2]), g.dtype) for g in grads],
        in_specs=[ANY] * na, out_specs=[ANY] * na,
        scratch_shapes=[pltpu.SemaphoreType.DMA((na,)), pltpu.SemaphoreType.DMA((na,))],
    )(*grads)


def _xchg_copies(ins, outs, send_sems, recv_sems):
    x, y, c, chips = _position()
    return [pltpu.make_async_remote_copy(
        src_ref=ins[a].at[2 * chip[0] + chip[1]], dst_ref=outs[a].at[j],
        send_sem=send_sems.at[3 * a + j], recv_sem=recv_sems.at[3 * a + j],
        device_id=(*chip, c), device_id_type=MESH) for a in range(len(ins)) for j, chip in enumerate(chips)]


def _exchange_chips(parts):
    na = len(parts)

    def body(*refs):
        ins, outs = refs[:na], refs[na:2 * na]
        send_sems, recv_sems = refs[2 * na:]
        for cp in _xchg_copies(ins, outs, send_sems, recv_sems):
            cp.start()
        for cp in _xchg_copies(ins, outs, send_sems, recv_sems):
            cp.wait()

    return pl.pallas_call(
        body, name="rs_exchange_chips",
        out_shape=[jax.ShapeDtypeStruct((3,) + p.shape[1:], p.dtype) for p in parts],
        in_specs=[ANY] * na, out_specs=[ANY] * na,
        scratch_shapes=[pltpu.SemaphoreType.DMA((3 * na,)), pltpu.SemaphoreType.DMA((3 * na,))],
    )(*parts)


def _swap_reduced(name, halves):
    na = len(halves)

    def body(*refs):
        ins, outs = refs[:na], refs[na:2 * na]
        send_sems, recv_sems = refs[2 * na:]
        x, y, c, _ = _position()
        cps = []
        for a in range(na):
            cp = pltpu.make_async_remote_copy(
                src_ref=ins[a], dst_ref=outs[a], send_sem=send_sems.at[a], recv_sem=recv_sems.at[a],
                device_id=(x, y, 1 - c), device_id_type=MESH)
            cp.start()
            cps.append(cp)
        for cp in cps:
            cp.wait()

    return pl.pallas_call(
        body, name=name,
        out_shape=[jax.ShapeDtypeStruct(h.shape, h.dtype) for h in halves],
        in_specs=[ANY] * na, out_specs=[ANY] * na,
        scratch_shapes=[pltpu.SemaphoreType.DMA((na,)), pltpu.SemaphoreType.DMA((na,))],
    )(*halves)


def _add_sibling(name, grad, recv, core):
    _, r, c = grad.shape
    half = r // 2
    rb = min(half, 256)
    nrb = half // rb

    def body(core_ref, g_ref, r_ref, o_ref):
        o_ref[...] = (g_ref[...].astype(F32) + r_ref[...].astype(F32)).astype(BF16)

    return pl.pallas_call(
        body, name=name,
        grid_spec=pltpu.PrefetchScalarGridSpec(
            num_scalar_prefetch=1, grid=(N_CHIP, nrb),
            in_specs=[pl.BlockSpec((1, rb, c), lambda j, i, core_ref: (j, core_ref[0] * nrb + i, 0)),
                      pl.BlockSpec((1, rb, c), lambda j, i, core_ref: (j, i, 0))],
            out_specs=pl.BlockSpec((1, rb, c), lambda j, i, core_ref: (j, i, 0))),
        out_shape=jax.ShapeDtypeStruct((N_CHIP, half, c), BF16),
        compiler_params=pltpu.CompilerParams(dimension_semantics=("arbitrary", "arbitrary"),
                                             vmem_limit_bytes=VMEM_LIMIT),
    )(core, grad, recv)


def _add_chips(name, chip, p, q):
    _, half, c = q.shape
    rb = min(half, 256)

    def body(chip_ref, p_ref, q_ref, o_ref):
        acc = p_ref[0].astype(F32)
        for j in range(3):
            acc = acc + q_ref[j].astype(F32)
        o_ref[...] = acc

    return pl.pallas_call(
        body, name=name,
        grid_spec=pltpu.PrefetchScalarGridSpec(
            num_scalar_prefetch=1, grid=(half // rb,),
            in_specs=[pl.BlockSpec((1, rb, c), lambda i, chip_ref: (chip_ref[0], i, 0)),
                      pl.BlockSpec((3, rb, c), lambda i, chip_ref: (0, i, 0))],
            out_specs=pl.BlockSpec((rb, c), lambda i, chip_ref: (i, 0))),
        out_shape=jax.ShapeDtypeStruct((half, c), F32),
        compiler_params=pltpu.CompilerParams(dimension_semantics=("arbitrary",), vmem_limit_bytes=VMEM_LIMIT),
    )(chip, p, q)


def _small_update(gad, gam, gl, gg, mychip, params):
    names = ["ada_b", "norm1_g", "lru_conv_b", "gate_a_w", "gate_a_b", "gate_x_w", "gate_x_b", "a_param",
             "lru_conv_w", "short_conv_w", "lru_out_g", "conv_out_g", "norm2_g", "final_g"]
    flat = [t for n in names for t in params[n]]
    nin = len(flat)

    def body(chip_ref, gad_ref, gam_ref, gl_ref, gg_ref, *refs):
        ins = {n: refs[3 * k:3 * k + 3] for k, n in enumerate(names)}
        outs = {n: refs[nin + 4 * k:nin + 4 * k + 4] for k, n in enumerate(names)}
        loss_ref, dmod_ref = refs[nin + 4 * len(names):nin + 4 * len(names) + 2]

        def dsum(ref, lo, n):
            acc = ref[lo:lo + n, :]
            for dev in range(1, 8):
                acc = acc + ref[dev * (ref.shape[0] // 8) + lo:dev * (ref.shape[0] // 8) + lo + n, :]
            return acc

        def update(n, g):
            w_ref, m_ref, v_ref = ins[n]
            g_ref, d_ref, mo_ref, vo_ref = outs[n]
            g_ref[...] = g
            d_ref[...], mo_ref[...], vo_ref[...] = _adam_math(w_ref[...], g, m_ref[...], v_ref[...])

        d, dm, l, lw = refs[-4:]
        d[...] = dsum(gad_ref, 0, 8)
        dm[...] = dsum(gam_ref, 0, 8)
        l[...] = dsum(gl_ref, 0, 16)
        for dev in range(8):
            for k in range(3):
                dmod_ref[dev:dev + 1, k * D_MODEL:(k + 1) * D_MODEL] = gad_ref[dev * 8 + k:dev * 8 + k + 1, :]
                dmod_ref[dev:dev + 1, (3 + k) * D_MODEL:(4 + k) * D_MODEL] = gam_ref[dev * 8 + k:dev * 8 + k + 1, :]
        w_ref, m_ref, v_ref = ins["ada_b"]
        g_ref, d_ref, mo_ref, vo_ref = outs["ada_b"]
        for k in range(3):
            g_ref[:, k * D_MODEL:(k + 1) * D_MODEL] = d[k:k + 1, :]
            g_ref[:, (3 + k) * D_MODEL:(4 + k) * D_MODEL] = dm[k:k + 1, :]
        d_ref[...], mo_ref[...], vo_ref[...] = _adam_math(w_ref[...], g_ref[...], m_ref[...], v_ref[...])
        update("norm1_g", d[3:4, :])
        update("norm2_g", dm[3:4, :])
        update("final_g", dm[4:5, :])
        update("gate_a_b", d[4:5, 0:D_LRU])
        update("gate_x_b", d[4:5, D_LRU:2 * D_LRU])
        update("lru_conv_b", l[4:5, :])
        update("a_param", l[8:9, :] * jax.nn.sigmoid(ins["a_param"][0][...]))
        update("lru_out_g", l[9:10, :])
        update("conv_out_g", l[10:11, :])
        loss_ref[...] = jnp.broadcast_to(dm[5:6, 0:128], (8, 128))
        chip = chip_ref[0]
        acc = jnp.zeros((8, 128), F32)
        for j in range(N_CHIP):
            acc = acc + jnp.where(chip == j, l[0:8, j * 128:(j + 1) * 128], 0.0)
        lw[...] = acc
        update("lru_conv_w", lw[0:4, :])
        update("short_conv_w", lw[5:8, :])
        gates = dsum(gg_ref, 0, D_LRU)
        update("gate_a_w", gates[:, 0:HEAD])
        update("gate_x_w", gates[:, HEAD:2 * HEAD])

    out_shape = []
    for n in names:
        out_shape += [jax.ShapeDtypeStruct(params[n][0].shape, F32)] * 4
    out_shape += [jax.ShapeDtypeStruct((8, 128), F32), jax.ShapeDtypeStruct((8, 6 * D_MODEL), F32)]
    res = pl.pallas_call(
        body, name="small_update", out_shape=out_shape,
        in_specs=[SMEM] + [VMEM] * (4 + nin),
        out_specs=[VMEM] * len(out_shape),
        scratch_shapes=[pltpu.VMEM((8, D_MODEL), F32), pltpu.VMEM((8, D_MODEL), F32), pltpu.VMEM((16, D_LRU), F32),
                        pltpu.VMEM((8, 128), F32)],
        compiler_params=pltpu.CompilerParams(vmem_limit_bytes=VMEM_LIMIT),
    )(mychip, gad, gam, gl, gg, *flat)
    per = {n: res[4 * k:4 * k + 4] for k, n in enumerate(names)}
    return per, res[-2], res[-1]


def _block_diag(w):
    eye = jnp.eye(8, dtype=w.dtype)
    return (eye[:, None, :, None] * w[:, :, None, :]).reshape(8 * HEAD, 8 * HEAD)


def _diag_blocks(g):
    return jnp.concatenate([g[h * HEAD:(h + 1) * HEAD, h * HEAD:(h + 1) * HEAD] for h in range(8)], axis=0)


def kernel(x, c, ada_w, ada_b, norm1_g, w_in, lru_conv_w, lru_conv_b, gate_a_w, gate_a_b, gate_x_w, gate_x_b, a_param, short_conv_w, lru_out_g, conv_out_g, w_out, norm2_g, w_mlp1, w_mlp2, final_g, loss_target, m_ada_w, m_ada_b, m_norm1_g, m_w_in, m_lru_conv_w, m_lru_conv_b, m_gate_a_w, m_gate_a_b, m_gate_x_w, m_gate_x_b, m_a_param, m_short_conv_w, m_lru_out_g, m_conv_out_g, m_w_out, m_norm2_g, m_w_mlp1, m_w_mlp2, m_final_g, v_ada_w, v_ada_b, v_norm1_g, v_w_in, v_lru_conv_w, v_lru_conv_b, v_gate_a_w, v_gate_a_b, v_gate_x_w, v_gate_x_b, v_a_param, v_short_conv_w, v_lru_out_g, v_conv_out_g, v_w_out, v_norm2_g, v_w_mlp1, v_w_mlp2, v_final_g):
    xi, yi, ci = lax.axis_index("x"), lax.axis_index("y"), lax.axis_index("c")
    mychip = 2 * xi + yi
    me = 4 * xi + 2 * yi + ci

    c_blk = jnp.zeros((8, D_MODEL), F32).at[0:1].set(c)
    cw_blk = jnp.zeros((8, 128), F32).at[0:4].set(lru_conv_w[0]).at[4:7].set(short_conv_w[0])
    c_g, cw_g = _allgather8("allgather_cond", [c_blk, cw_blk])
    c_all = c_g.reshape(8, 8, D_MODEL)[:, 0]
    cw_g = cw_g.reshape(4, 2, 8, 128)[:, 0]
    lcw = cw_g[:, 0:4].transpose(1, 0, 2).reshape(4, D_LRU)
    scw = cw_g[:, 4:7].transpose(1, 0, 2).reshape(3, D_LRU)

    mod_loc = _mod_matmul(c_all, ada_w[0])
    (mod_g,) = _allgather8("allgather_mod", [mod_loc])
    mod_all = mod_g.reshape(4, 2, 8, 6 * D_MODEL // 4)[:, 0].transpose(1, 0, 2).reshape(8, 6 * D_MODEL) + ada_b
    mod_pad = jnp.pad(mod_all.reshape(8, 6, D_MODEL), ((0, 0), (0, 2), (0, 0)))
    mod = lax.dynamic_slice_in_dim(mod_pad, me, 1, axis=0).reshape(8, D_MODEL)

    own_in, own_out, own_w1, own_w2 = [w[0].astype(BF16) for w in (w_in, w_out, w_mlp1, w_mlp2)]
    win_all, wout_all = _allgather_weights([own_in, own_out])
    win, wout = (win_all, own_in), (wout_all, own_out)
    chip = mychip.reshape(1).astype(jnp.int32)
    core = ci.reshape(1).astype(jnp.int32)

    vecd = jnp.concatenate([norm1_g, norm2_g, final_g[None, :], jnp.concatenate([gate_a_b, gate_x_b], axis=1),
                            jnp.zeros((4, D_MODEL), F32)], axis=0)
    vecl = jnp.concatenate([lcw, lru_conv_b, scw, a_param, lru_out_g, conv_out_g, jnp.zeros((5, D_LRU), F32)], axis=0)
    gab = jnp.concatenate([_block_diag(gate_a_w[0]), _block_diag(gate_x_w[0])], axis=1).astype(BF16)
    a64 = _block_diag(jnp.full((8, HEAD, HEAD), 1.0 / HEAD, F32)).astype(BF16)

    hb, proj, hl, ycat, mixed, x1, w1_all, w2_all = _mix_fwd(
        chip, x[0], mod, vecd, vecl, win, wout, gab, a64, [own_w1, own_w2])
    dx1, act, dz, dmo, h2b, accm = _mlp_fwd_bwd(
        chip, x1, loss_target[0], mod, vecd, (w1_all, own_w1), (w2_all, own_w2))

    def sibling_sum(tag, grads):
        recv = _swap_halves("rs_swap_halves_" + tag, grads)
        return [_add_sibling("rs_add_sibling_%s%d" % (tag, k), g, r, core) for k, (g, r) in enumerate(zip(grads, recv))]

    parts_mlp = sibling_sum("mlp", [_wgrad("wgrad_mlp1", h2b, dz, 0, FF_BLK, BF16),
                                    _wgrad("wgrad_mlp2", act, dmo, FF_BLK, 0, BF16)])
    grad_x, dproj, dmixed, xlb, dgb, accd, accl, q_w1, q_w2 = _mix_bwd(
        chip, dx1, x[0], mixed, proj, hl, mod, vecd, vecl, win, wout, gab, a64, parts_mlp)
    parts_mix = sibling_sum("mix", [_wgrad("wgrad_in", hb, dproj, 0, WIN_BLK, BF16),
                                    _wgrad("wgrad_out", ycat, dmixed, WOUT_BLK, 0, BF16)])
    g_gate = _wgrad("wgrad_gate", xlb, dgb, 0, 0, F32)[0]

    def reduced(tag, parts, landed):
        own = [_add_chips("rs_add_chips_%s%d" % (tag, k), chip, p, q) for k, (p, q) in enumerate(zip(parts, landed))]
        return own, _swap_reduced("rs_swap_reduced_" + tag, own)

    landed_mix = _exchange_chips(parts_mix)
    own_mlp, sib_mlp = reduced("mlp", parts_mlp, [q_w1, q_w2])
    own_mix, sib_mix = reduced("mix", parts_mix, landed_mix)
    res_w1, res_w2 = _adam("adam_mlp", core, [(w_mlp1[0], own_mlp[0], sib_mlp[0], m_w_mlp1[0], v_w_mlp1[0]),
                                              (w_mlp2[0], own_mlp[1], sib_mlp[1], m_w_mlp2[0], v_w_mlp2[0])])
    (res_win,) = _adam("adam_w_in", core, [(w_in[0], own_mix[0], sib_mix[0], m_w_in[0], v_w_in[0])])
    (res_wout,) = _adam("adam_w_out", core, [(w_out[0], own_mix[1], sib_mix[1], m_w_out[0], v_w_out[0])])

    gg_blk = jnp.concatenate([_diag_blocks(g_gate[:, 0:D_LRU]), _diag_blocks(g_gate[:, D_LRU:2 * D_LRU])], axis=1)
    gad, gam, gl, gg = _allgather8("allgather_small_grads", [accd, accm, accl, gg_blk])

    params = {
        "ada_b": (ada_b, m_ada_b, v_ada_b), "norm1_g": (norm1_g, m_norm1_g, v_norm1_g),
        "lru_conv_b": (lru_conv_b, m_lru_conv_b, v_lru_conv_b),
        "gate_a_w": tuple(t.reshape(D_LRU, HEAD) for t in (gate_a_w, m_gate_a_w, v_gate_a_w)),
        "gate_a_b": (gate_a_b, m_gate_a_b, v_gate_a_b),
        "gate_x_w": tuple(t.reshape(D_LRU, HEAD) for t in (gate_x_w, m_gate_x_w, v_gate_x_w)),
        "gate_x_b": (gate_x_b, m_gate_x_b, v_gate_x_b), "a_param": (a_param, m_a_param, v_a_param),
        "lru_conv_w": tuple(t[0] for t in (lru_conv_w, m_lru_conv_w, v_lru_conv_w)),
        "short_conv_w": tuple(t[0] for t in (short_conv_w, m_short_conv_w, v_short_conv_w)),
        "lru_out_g": (lru_out_g, m_lru_out_g, v_lru_out_g), "conv_out_g": (conv_out_g, m_conv_out_g, v_conv_out_g),
        "norm2_g": (norm2_g, m_norm2_g, v_norm2_g),
        "final_g": tuple(t[None, :] for t in (final_g, m_final_g, v_final_g)),
    }
    small, loss_blk, dmod_cols = _small_update(gad, gam, gl, gg, chip, params)
    loss = loss_blk[0, 0]

    ncol = 6 * D_MODEL // N_CHIP
    dmod_loc = lax.dynamic_slice_in_dim(dmod_cols, mychip * ncol, ncol, axis=1)
    sct = (c_all * jax.nn.sigmoid(c_all)).T
    ada = _ada_grad_adam(sct, dmod_loc, ada_w[0], m_ada_w[0], v_ada_w[0])

    res = {"ada_w": ada, "w_in": res_win, "w_out": res_wout, "w_mlp1": res_w1, "w_mlp2": res_w2}
    res = {n: tuple(t[None] for t in r) for n, r in res.items()}
    shapes = {"gate_a_w": gate_a_w.shape, "gate_x_w": gate_x_w.shape, "lru_conv_w": lru_conv_w.shape,
              "short_conv_w": short_conv_w.shape, "final_g": final_g.shape}
    for n, t in small.items():
        res[n] = tuple(u.reshape(shapes[n]) if n in shapes else u for u in t)

    order = ["ada_w", "ada_b", "norm1_g", "w_in", "lru_conv_w", "lru_conv_b", "gate_a_w", "gate_a_b", "gate_x_w",
             "gate_x_b", "a_param", "short_conv_w", "lru_out_g", "conv_out_g", "w_out", "norm2_g", "w_mlp1",
             "w_mlp2", "final_g"]
    return (loss, grad_x[None], *[res[n][0] for n in order], *[res[n][1] for n in order],
            *[res[n][2] for n in order], *[res[n][3] for n in order])
```

```python
import jax
import jax.numpy as jnp
from jax import lax
from jax.experimental import pallas as pl
from jax.experimental.pallas import tpu as pltpu

F32 = jnp.float32
BF16 = jnp.bfloat16

D_MODEL = 1024
D_LRU = 512
D_IN = 2560
D_FF = 4096
N_CHIP = 4
WIN_BLK = D_IN // N_CHIP
WOUT_BLK = D_MODEL // N_CHIP
FF_BLK = D_FF // N_CHIP
HEAD = 64
EPS = 1e-6
C_GATE = 8.0
TOKEN_TILE = 256
HALO = 8
VMEM_LIMIT = 60 * 1024 * 1024

ADAM_LR = 0.001
ADAM_B1 = 0.9
ADAM_B2 = 0.999
ADAM_EPS = 1e-08
ADAM_WD = 0.01
ADAM_STEP = 10

MESH = pl.DeviceIdType.MESH
ANY = pl.BlockSpec(memory_space=pl.ANY)
VMEM = pl.BlockSpec(memory_space=pltpu.VMEM)
SMEM = pl.BlockSpec(memory_space=pltpu.SMEM)


def _full(shape, single=False):
    nd = len(shape)
    if single:
        return pl.BlockSpec(shape, lambda *_: (0,) * nd, pipeline_mode=pl.Buffered(1))
    return pl.BlockSpec(shape, lambda *_: (0,) * nd)


def _dot(a, b):
    return jnp.dot(a, b, preferred_element_type=F32)


def _dot_nt(a, b):
    return lax.dot_general(a, b, (((1,), (1,)), ((), ())), preferred_element_type=F32)


def _dot_tn(a, b):
    return lax.dot_general(a, b, (((0,), (0,)), ((), ())), preferred_element_type=F32)


def _gmean(v, a64):
    hi = v.astype(BF16)
    lo = (v - hi.astype(F32)).astype(BF16)
    return _dot(hi, a64) + _dot(lo, a64)


def _gelu(x):
    u = 0.7978845608028654 * (x + 0.044715 * x * x * x)
    t = jnp.tanh(u)
    return 0.5 * x * (1.0 + t), t


def _gelu_grad(x, t):
    du = 0.7978845608028654 * (1.0 + 3.0 * 0.044715 * x * x)
    return 0.5 * (1.0 + t) + 0.5 * x * (1.0 - t * t) * du


def _log1p_pos(y):
    return jnp.where(y < 1e-2, y * (1.0 - y * (0.5 - y * (1.0 / 3.0 - y * 0.25))), jnp.log(1.0 + y))


def _softplus(a):
    return jnp.maximum(a, 0.0) + _log1p_pos(jnp.exp(-jnp.abs(a)))


def _neg_expm1(z):
    series = -z * (1.0 + z * (0.5 + z * (1.0 / 6.0 + z * (1.0 / 24.0 + z * (1.0 / 120.0)))))
    return jnp.where(z > -0.02, series, 1.0 - jnp.exp(z))


def _scan_fwd(a, b, row):
    n = a.shape[0]
    d = 1
    while d < n:
        m = row >= d
        b = jnp.where(m, a * pltpu.roll(b, d, 0) + b, b)
        a = jnp.where(m, a * pltpu.roll(a, d, 0), a)
        d *= 2
    return a, b


def _scan_rev(a, b, row):
    n = a.shape[0]
    d = 1
    while d < n:
        m = row < n - d
        b = jnp.where(m, b + a * pltpu.roll(b, n - d, 0), b)
        a = jnp.where(m, a * pltpu.roll(a, n - d, 0), a)
        d *= 2
    return a, b


def _colsum(v):
    return jnp.sum(v, axis=0, keepdims=True)


def _load_gathered(chip, gathered, own, slot, sems):
    copies = []
    for j in range(N_CHIP):
        @pl.when(chip == j)
        def _(j=j):
            pltpu.make_async_copy(own, slot(j), sems.at[j]).start()

        @pl.when(chip != j)
        def _(j=j):
            pltpu.make_async_copy(gathered.at[j], slot(j), sems.at[j]).start()

        copies.append(pltpu.make_async_copy(own, slot(j), sems.at[j]))
    return copies


def _lru_gates(xlb, gab, gbias, sp, first_row):
    g = _dot(xlb, gab) + gbias
    r = jax.nn.sigmoid(g[:, :D_LRU])
    ig = jax.nn.sigmoid(g[:, D_LRU:])
    la = (-C_GATE) * r * sp
    a = jnp.exp(la)
    msq = jnp.sqrt(_neg_expm1(2.0 * la))
    mult = jnp.where(first_row, 1.0, msq)
    return r, ig, a, msq, mult


def _mix_fwd(chip, x, mod, vecd, vecl, win, wout, gab, a64, mlp_shards):
    s = x.shape[0]
    ts = TOKEN_TILE
    nt = s // ts

    def body(chip_ref, x_ref, mod_ref, vd_ref, vl_ref, win_hbm, win_own, wout_hbm, wout_own, gab_ref, a64_ref,
             w1_own, w2_own,
             hb_ref, proj_ref, hl_ref, ycat_ref, mixed_ref, x1_ref, w1_all, w2_all,
             win_ref, wout_ref, ext_lx, ext_cv, hcar, sems, ag_send, ag_recv):
        i = pl.program_id(0)

        @pl.when(i == 0)
        def _():
            _ag_start((w1_own, w2_own), (w1_all, w2_all), ag_send, ag_recv)
            cps = _load_gathered(chip_ref[0], win_hbm, win_own, lambda j: win_ref.at[j], sems.at[pl.ds(0, N_CHIP)])
            cps += _load_gathered(chip_ref[0], wout_hbm, wout_own,
                                  lambda j: wout_ref.at[pl.ds(j * WOUT_BLK, WOUT_BLK), :],
                                  sems.at[pl.ds(N_CHIP, N_CHIP)])
            ext_lx[0:HALO, :] = jnp.zeros((HALO, D_LRU), F32)
            ext_cv[0:HALO, :] = jnp.zeros((HALO, D_LRU), F32)
            hcar[...] = jnp.zeros_like(hcar)
            for cp in cps:
                cp.wait()

        row = lax.broadcasted_iota(jnp.int32, (ts, D_LRU), 0)
        first_row = jnp.logical_and(row == 0, i == 0)
        xt = x_ref[...]
        shift1, scale1, gate1 = mod_ref[0:1, :], mod_ref[1:2, :], mod_ref[2:3, :]
        r1 = lax.rsqrt(jnp.mean(xt * xt, axis=-1, keepdims=True) + EPS)
        h = (xt * r1) * vd_ref[0:1, :] * (1.0 + scale1) + shift1
        hb = h.astype(BF16)
        hb_ref[...] = hb
        for j in range(N_CHIP):
            proj_ref[:, j * WIN_BLK:(j + 1) * WIN_BLK] = _dot(hb, win_ref[j])
        u_ly = proj_ref[:, 512:1024]
        u_b = proj_ref[:, 1024:1536]

        ext_lx[HALO:HALO + ts, :] = proj_ref[:, 0:512]
        xl = vl_ref[4:5, :] + vl_ref[0:1, :] * ext_lx[pl.ds(5, ts), :]
        for k in range(1, 4):
            xl = xl + vl_ref[k:k + 1, :] * ext_lx[pl.ds(5 + k, ts), :]
        ext_lx[0:HALO, :] = ext_lx[ts:ts + HALO, :]
        sp = _softplus(vl_ref[8:9, :])
        _, ig, a, _, mult = _lru_gates(xl.astype(BF16), gab_ref[...], vd_ref[3:4, :], sp, first_row)
        acum, hloc = _scan_fwd(a, mult * (ig * xl), row)
        hl = hloc + acum * hcar[0:1, :]
        hl_ref[...] = hl
        hcar[0:1, :] = hl_ref[ts - 1:ts, :]
        ge, _ = _gelu(u_ly)
        p = ge * hl
        y_lru = p * lax.rsqrt(_gmean(p * p, a64_ref[...]) + EPS) * vl_ref[9:10, :]
        ycat_ref[:, 0:512] = y_lru.astype(BF16)

        ext_cv[HALO:HALO + ts, :] = proj_ref[:, 1536:2048] * proj_ref[:, 2048:2560]
        q = vl_ref[5:6, :] * ext_cv[pl.ds(6, ts), :]
        for k in range(1, 3):
            q = q + vl_ref[5 + k:6 + k, :] * ext_cv[pl.ds(6 + k, ts), :]
        ext_cv[0:HALO, :] = ext_cv[ts:ts + HALO, :]
        yc = u_b * q
        y_conv = yc * lax.rsqrt(_gmean(yc * yc, a64_ref[...]) + EPS) * vl_ref[10:11, :]
        ycat_ref[:, 512:1024] = y_conv.astype(BF16)

        mixed = _dot(ycat_ref[...], wout_ref[...])
        mixed_ref[...] = mixed
        x1_ref[...] = xt + gate1 * mixed

        @pl.when(i == max(nt - 3, 0))
        def _():
            _ag_relay((w1_own, w2_own), (w1_all, w2_all), ag_send, ag_recv, [0])

        @pl.when(i == nt - 1)
        def _():
            _ag_relay((w1_own, w2_own), (w1_all, w2_all), ag_send, ag_recv, [1])
            _ag_complete((w1_own, w2_own), (w1_all, w2_all), ag_send, ag_recv)

    tile = lambda w: pl.BlockSpec((ts, w), lambda i: (i, 0))
    return pl.pallas_call(
        body, name="mix_fwd", grid=(nt,),
        in_specs=[SMEM, tile(D_MODEL), _full((8, D_MODEL)), _full((8, D_MODEL)), _full((16, D_LRU)),
                  ANY, ANY, ANY, ANY, _full((D_LRU, 2 * D_LRU), True), _full((D_LRU, D_LRU), True), ANY, ANY],
        out_specs=[tile(D_MODEL), tile(D_IN), tile(D_LRU), tile(D_MODEL), tile(D_MODEL), tile(D_MODEL), ANY, ANY],
        out_shape=[jax.ShapeDtypeStruct((s, D_MODEL), BF16), jax.ShapeDtypeStruct((s, D_IN), F32),
                   jax.ShapeDtypeStruct((s, D_LRU), F32), jax.ShapeDtypeStruct((s, D_MODEL), BF16),
                   jax.ShapeDtypeStruct((s, D_MODEL), F32), jax.ShapeDtypeStruct((s, D_MODEL), F32)]
        + [jax.ShapeDtypeStruct((N_CHIP,) + w.shape, w.dtype) for w in mlp_shards],
        scratch_shapes=[pltpu.VMEM((N_CHIP, D_MODEL, WIN_BLK), BF16), pltpu.VMEM((D_MODEL, D_MODEL), BF16),
                        pltpu.VMEM((ts + HALO, D_LRU), F32), pltpu.VMEM((ts + HALO, D_LRU), F32),
                        pltpu.VMEM((HALO, D_LRU), F32), pltpu.SemaphoreType.DMA((2 * N_CHIP,)),
                        pltpu.SemaphoreType.DMA((2 * AG_SEMS,)), pltpu.SemaphoreType.DMA((2 * AG_SEMS,))],
        compiler_params=pltpu.CompilerParams(dimension_semantics=("arbitrary",), vmem_limit_bytes=VMEM_LIMIT),
    )(chip, x, mod, vecd, vecl, *win, *wout, gab, a64, *mlp_shards)


def _mlp_fwd_bwd(chip, x1, target, mod, vecd, w1, w2):
    s = x1.shape[0]
    ts = TOKEN_TILE
    nt = s // ts

    def body(chip_ref, x1_ref, tg_ref, mod_ref, vd_ref, w1_hbm, w1_own, w2_hbm, w2_own,
             dx1_ref, act_ref, dz_ref, dmo_ref, h2_ref, acc_ref, w1_v, w2_v, rz_v, sems):
        i = pl.program_id(0)

        @pl.when(i == 0)
        def _():
            cps = _load_gathered(chip_ref[0], w1_hbm, w1_own, lambda j: w1_v.at[j], sems.at[pl.ds(0, N_CHIP)])
            cps += _load_gathered(chip_ref[0], w2_hbm, w2_own, lambda j: w2_v.at[j], sems.at[pl.ds(N_CHIP, N_CHIP)])
            acc_ref[...] = jnp.zeros_like(acc_ref)
            for cp in cps:
                cp.wait()

        xt = x1_ref[...]
        shift2, scale2, gate2 = mod_ref[3:4, :], mod_ref[4:5, :], mod_ref[5:6, :]
        g2, gf = vd_ref[1:2, :], vd_ref[2:3, :]
        r2 = lax.rsqrt(jnp.mean(xt * xt, axis=-1, keepdims=True) + EPS)
        n2 = xt * r2
        h2b = (n2 * g2 * (1.0 + scale2) + shift2).astype(BF16)
        h2_ref[...] = h2b
        mo = jnp.zeros((ts, D_MODEL), F32)
        for j in range(N_CHIP):
            rz = jnp.maximum(_dot(h2b, w1_v[j]), 0.0)
            rz_v[j] = rz
            actb = (rz * rz).astype(BF16)
            act_ref[:, j * FF_BLK:(j + 1) * FF_BLK] = actb
            mo = mo + _dot(actb, w2_v[j])
        x2 = xt + gate2 * mo
        r3 = lax.rsqrt(jnp.mean(x2 * x2, axis=-1, keepdims=True) + EPS)
        n3 = x2 * r3
        e = n3 * gf - tg_ref[...]
        loss = (0.5 / D_MODEL) * jnp.sum(_colsum(e * e), axis=1, keepdims=True)
        dy = e * (1.0 / D_MODEL)
        acc_ref[4:5, :] += _colsum(dy * n3)
        acc_ref[5:6, :] += jnp.broadcast_to(loss, (1, D_MODEL))
        dn3 = dy * gf
        dx2 = r3 * (dn3 - n3 * jnp.mean(dn3 * n3, axis=-1, keepdims=True))
        acc_ref[2:3, :] += _colsum(dx2 * mo)
        dmob = (dx2 * gate2).astype(BF16)
        dmo_ref[...] = dmob
        dh2 = jnp.zeros((ts, D_MODEL), F32)
        for j in range(N_CHIP):
            dzb = (_dot_nt(dmob, w2_v[j]) * (2.0 * rz_v[j])).astype(BF16)
            dz_ref[:, j * FF_BLK:(j + 1) * FF_BLK] = dzb
            dh2 = dh2 + _dot_nt(dzb, w1_v[j])
        acc_ref[1:2, :] += _colsum(dh2 * (n2 * g2))
        acc_ref[0:1, :] += _colsum(dh2)
        dhn2 = dh2 * (1.0 + scale2)
        acc_ref[3:4, :] += _colsum(dhn2 * n2)
        dn2 = dhn2 * g2
        dx1_ref[...] = dx2 + r2 * (dn2 - n2 * jnp.mean(dn2 * n2, axis=-1, keepdims=True))

    tile = lambda w: pl.BlockSpec((ts, w), lambda i: (i, 0))
    return pl.pallas_call(
        body, name="mlp_fwd_bwd", grid=(nt,),
        in_specs=[SMEM, tile(D_MODEL), tile(D_MODEL), _full((8, D_MODEL)), _full((8, D_MODEL)), ANY, ANY, ANY, ANY],
        out_specs=[tile(D_MODEL), tile(D_FF), tile(D_FF), tile(D_MODEL), tile(D_MODEL), _full((8, D_MODEL))],
        out_shape=[jax.ShapeDtypeStruct((s, D_MODEL), F32), jax.ShapeDtypeStruct((s, D_FF), BF16),
                   jax.ShapeDtypeStruct((s, D_FF), BF16), jax.ShapeDtypeStruct((s, D_MODEL), BF16),
                   jax.ShapeDtypeStruct((s, D_MODEL), BF16), jax.ShapeDtypeStruct((8, D_MODEL), F32)],
        scratch_shapes=[pltpu.VMEM((N_CHIP, D_MODEL, FF_BLK), BF16), pltpu.VMEM((N_CHIP, FF_BLK, D_MODEL), BF16),
                        pltpu.VMEM((N_CHIP, ts, FF_BLK), F32), pltpu.SemaphoreType.DMA((2 * N_CHIP,))],
        compiler_params=pltpu.CompilerParams(dimension_semantics=("arbitrary",), vmem_limit_bytes=VMEM_LIMIT),
    )(chip, x1, target, mod, vecd, *w1, *w2)


def _mix_bwd(chip, dx1, x, mixed, proj, hl, mod, vecd, vecl, win, wout, gab, a64, mlp_parts):
    s = x.shape[0]
    ts = TOKEN_TILE
    nt = s // ts
    hpt = ts // HALO

    def body(chip_ref, dx1_ref, x_ref, mixed_ref, proj_ref, projh_ref, hl_ref, hlh_ref, mod_ref, vd_ref, vl_ref,
             win_hbm, win_own, wout_hbm, wout_own, gab_ref, a64_ref, p1_ref, p2_ref,
             gx_ref, dproj_ref, dmixed_ref, xlb_ref, dgb_ref, accd_ref, accl_ref, q1_ref, q2_ref,
             win_ref, wout_ref, ext_lx, ext_cv, ext_hl, ext_dxl, ext_dq, gbuf, gcar, acar, sems, x_send, x_recv):
        i = pl.program_id(0)
        ri = nt - 1 - i

        @pl.when(i == 0)
        def _():
            for cp in _xchg_copies((p1_ref, p2_ref), (q1_ref, q2_ref), x_send, x_recv):
                cp.start()
            cps = _load_gathered(chip_ref[0], win_hbm, win_own, lambda j: win_ref.at[j], sems.at[pl.ds(0, N_CHIP)])
            cps += _load_gathered(chip_ref[0], wout_hbm, wout_own,
                                  lambda j: wout_ref.at[pl.ds(j * WOUT_BLK, WOUT_BLK), :],
                                  sems.at[pl.ds(N_CHIP, N_CHIP)])
            for cp in cps:
                cp.wait()
            accd_ref[...] = jnp.zeros_like(accd_ref)
            accl_ref[...] = jnp.zeros_like(accl_ref)
            ext_dxl[ts:ts + HALO, :] = jnp.zeros((HALO, D_LRU), F32)
            ext_dq[ts:ts + HALO, :] = jnp.zeros((HALO, D_LRU), F32)
            gcar[...] = jnp.zeros_like(gcar)
            acar[...] = jnp.zeros_like(acar)

        row = lax.broadcasted_iota(jnp.int32, (ts, D_LRU), 0)
        first_row = jnp.logical_and(row == 0, ri == 0)
        halo_on = jnp.where(ri == 0, 0.0, 1.0)
        shift1, scale1, gate1 = mod_ref[0:1, :], mod_ref[1:2, :], mod_ref[2:3, :]
        g1 = vd_ref[0:1, :]
        a64m = a64_ref[...]
        lg, cg = vl_ref[9:10, :], vl_ref[10:11, :]

        dx1 = dx1_ref[...]
        accd_ref[2:3, :] += _colsum(dx1 * mixed_ref[...])
        dmb = (dx1 * gate1).astype(BF16)
        dmixed_ref[...] = dmb
        dycat = _dot_nt(dmb, wout_ref[...])
        dyl = dycat[:, 0:512]
        dyv = dycat[:, 512:1024]

        u_ly = proj_ref[:, 512:1024]
        u_b = proj_ref[:, 1024:1536]
        u_c = proj_ref[:, 1536:2048]
        u_v = proj_ref[:, 2048:2560]
        ext_lx[0:HALO, :] = projh_ref[:, 0:512] * halo_on
        ext_lx[HALO:HALO + ts, :] = proj_ref[:, 0:512]
        xl = vl_ref[4:5, :] + vl_ref[0:1, :] * ext_lx[pl.ds(5, ts), :]
        for k in range(1, 4):
            xl = xl + vl_ref[k:k + 1, :] * ext_lx[pl.ds(5 + k, ts), :]
        xlb = xl.astype(BF16)
        xlb_ref[...] = xlb
        sp = _softplus(vl_ref[8:9, :])
        r, ig, a, msq, mult = _lru_gates(xlb, gab_ref[...], vd_ref[3:4, :], sp, first_row)
        hl = hl_ref[...]
        ge, th = _gelu(u_ly)
        p = ge * hl
        rl = lax.rsqrt(_gmean(p * p, a64m) + EPS)
        nl = p * rl
        ext_cv[0:HALO, :] = projh_ref[:, 1536:2048] * projh_ref[:, 2048:2560] * halo_on
        ext_cv[HALO:HALO + ts, :] = u_c * u_v
        q = vl_ref[5:6, :] * ext_cv[pl.ds(6, ts), :]
        for k in range(1, 3):
            q = q + vl_ref[5 + k:6 + k, :] * ext_cv[pl.ds(6 + k, ts), :]
        yc = u_b * q
        rc = lax.rsqrt(_gmean(yc * yc, a64m) + EPS)
        nc = yc * rc

        accl_ref[9:10, :] += _colsum(dyl * nl)
        dnl = dyl * lg
        dp = rl * (dnl - nl * _gmean(dnl * nl, a64m))
        dproj_ref[:, 512:1024] = ((dp * hl) * _gelu_grad(u_ly, th)).astype(BF16)
        a_next = jnp.where(row == ts - 1, acar[0:1, :], pltpu.roll(a, ts - 1, 0))
        acum, gloc = _scan_rev(a_next, dp * ge, row)
        gbuf[...] = gloc + acum * gcar[0:1, :]
        gcar[0:1, :] = gbuf[0:1, :]
        ext_hl[0:HALO, :] = hlh_ref[...] * halo_on
        ext_hl[HALO:HALO + ts, :] = hl
        acar[...] = a[0:HALO, :]
        gt = gbuf[...]
        da = gt * ext_hl[pl.ds(HALO - 1, ts), :]
        dmult = gt * ig * xl
        di = gt * mult * xl
        dxl = gt * mult * ig
        dla = da * a - jnp.where(first_row, 0.0, dmult * a * a / msq)
        accl_ref[8:9, :] += _colsum(dla * ((-C_GATE) * r))
        dra = dla * ((-C_GATE) * sp) * r * (1.0 - r)
        dia = di * ig * (1.0 - ig)
        accd_ref[4:5, 0:D_LRU] += _colsum(dra)
        accd_ref[4:5, D_LRU:2 * D_LRU] += _colsum(dia)
        dgb_ref[:, 0:D_LRU] = dra.astype(BF16)
        dgb_ref[:, D_LRU:2 * D_LRU] = dia.astype(BF16)
        dxl = dxl + _dot_nt(dgb_ref[...], gab_ref[...])
        accl_ref[4:5, :] += _colsum(dxl)
        for k in range(4):
            accl_ref[k:k + 1, :] += _colsum(dxl * ext_lx[pl.ds(5 + k, ts), :])
        ext_dxl[0:ts, :] = dxl
        du_lx = vl_ref[0:1, :] * ext_dxl[pl.ds(3, ts), :]
        for k in range(1, 4):
            du_lx = du_lx + vl_ref[k:k + 1, :] * ext_dxl[pl.ds(3 - k, ts), :]
        ext_dxl[ts:ts + HALO, :] = ext_dxl[0:HALO, :]
        dproj_ref[:, 0:512] = du_lx.astype(BF16)

        accl_ref[10:11, :] += _colsum(dyv * nc)
        dnc = dyv * cg
        dyc = rc * (dnc - nc * _gmean(dnc * nc, a64m))
        dproj_ref[:, 1024:1536] = (dyc * q).astype(BF16)
        dq = dyc * u_b
        for k in range(3):
            accl_ref[5 + k:6 + k, :] += _colsum(dq * ext_cv[pl.ds(6 + k, ts), :])
        ext_dq[0:ts, :] = dq
        dcv = vl_ref[5:6, :] * ext_dq[pl.ds(2, ts), :]
        for k in range(1, 3):
            dcv = dcv + vl_ref[5 + k:6 + k, :] * ext_dq[pl.ds(2 - k, ts), :]
        ext_dq[ts:ts + HALO, :] = ext_dq[0:HALO, :]
        dproj_ref[:, 1536:2048] = (dcv * u_v).astype(BF16)
        dproj_ref[:, 2048:2560] = (dcv * u_c).astype(BF16)

        dh = _dot_nt(dproj_ref[:, 0:WIN_BLK], win_ref[0])
        for j in range(1, N_CHIP):
            dh = dh + _dot_nt(dproj_ref[:, j * WIN_BLK:(j + 1) * WIN_BLK], win_ref[j])
        xt = x_ref[...]
        r1 = lax.rsqrt(jnp.mean(xt * xt, axis=-1, keepdims=True) + EPS)
        n1 = xt * r1
        accd_ref[1:2, :] += _colsum(dh * (n1 * g1))
        accd_ref[0:1, :] += _colsum(dh)
        dhn1 = dh * (1.0 + scale1)
        accd_ref[3:4, :] += _colsum(dhn1 * n1)
        dn1 = dhn1 * g1
        gx_ref[...] = dx1 + r1 * (dn1 - n1 * jnp.mean(dn1 * n1, axis=-1, keepdims=True))

        @pl.when(i == nt - 1)
        def _():
            for cp in _xchg_copies((p1_ref, p2_ref), (q1_ref, q2_ref), x_send, x_recv):
                cp.wait()

    tile = lambda w: pl.BlockSpec((ts, w), lambda i: (nt - 1 - i, 0))
    halo = lambda w: pl.BlockSpec((HALO, w), lambda i: (jnp.maximum((nt - 1 - i) * hpt - 1, 0), 0))
    ext = pltpu.VMEM((ts + HALO, D_LRU), F32)
    return pl.pallas_call(
        body, name="mix_bwd", grid=(nt,),
        in_specs=[SMEM, tile(D_MODEL), tile(D_MODEL), tile(D_MODEL), tile(D_IN), halo(D_IN), tile(D_LRU), halo(D_LRU),
                  _full((8, D_MODEL)), _full((8, D_MODEL)), _full((16, D_LRU)),
                  ANY, ANY, ANY, ANY, _full((D_LRU, 2 * D_LRU), True), _full((D_LRU, D_LRU), True), ANY, ANY],
        out_specs=[tile(D_MODEL), tile(D_IN), tile(D_MODEL), tile(D_LRU), tile(2 * D_LRU),
                   _full((8, D_MODEL)), _full((16, D_LRU)), ANY, ANY],
        out_shape=[jax.ShapeDtypeStruct((s, D_MODEL), F32), jax.ShapeDtypeStruct((s, D_IN), BF16),
                   jax.ShapeDtypeStruct((s, D_MODEL), BF16), jax.ShapeDtypeStruct((s, D_LRU), BF16),
                   jax.ShapeDtypeStruct((s, 2 * D_LRU), BF16),
                   jax.ShapeDtypeStruct((8, D_MODEL), F32), jax.ShapeDtypeStruct((16, D_LRU), F32)]
        + [jax.ShapeDtypeStruct((3,) + p.shape[1:], p.dtype) for p in mlp_parts],
        scratch_shapes=[pltpu.VMEM((N_CHIP, D_MODEL, WIN_BLK), BF16), pltpu.VMEM((D_MODEL, D_MODEL), BF16),
                        ext, ext, ext, ext, ext, pltpu.VMEM((ts, D_LRU), F32),
                        pltpu.VMEM((HALO, D_LRU), F32), pltpu.VMEM((HALO, D_LRU), F32),
                        pltpu.SemaphoreType.DMA((2 * N_CHIP,)),
                        pltpu.SemaphoreType.DMA((6,)), pltpu.SemaphoreType.DMA((6,))],
        compiler_params=pltpu.CompilerParams(dimension_semantics=("arbitrary",), vmem_limit_bytes=VMEM_LIMIT),
    )(chip, dx1, x, mixed, proj, proj, hl, hl, mod, vecd, vecl, *win, *wout, gab, a64, *mlp_parts)


def _wgrad(name, a, b, a_blk, b_blk, out_dtype):
    s = a.shape[0]
    aw = a_blk or a.shape[1]
    bw = b_blk or b.shape[1]
    nblk = N_CHIP if (a_blk or b_blk) else 1

    def body(a_ref, b_ref, o_ref):
        o_ref[0] = _dot_tn(a_ref[...], b_ref[...]).astype(out_dtype)

    return pl.pallas_call(
        body, name=name, grid=(nblk,),
        in_specs=[pl.BlockSpec((s, aw), (lambda j: (0, j)) if a_blk else (lambda j: (0, 0))),
                  pl.BlockSpec((s, bw), (lambda j: (0, j)) if b_blk else (lambda j: (0, 0)))],
        out_specs=pl.BlockSpec((1, aw, bw), lambda j: (j, 0, 0)),
        out_shape=jax.ShapeDtypeStruct((nblk, aw, bw), out_dtype),
        compiler_params=pltpu.CompilerParams(dimension_semantics=("arbitrary",), vmem_limit_bytes=VMEM_LIMIT),
    )(a, b)


def _mod_matmul(c_all, ada_w_loc):
    n = ada_w_loc.shape[1]
    cb = 512

    def body(c_ref, w_ref, o_ref):
        c = c_ref[...]
        sc = c * jax.nn.sigmoid(c)
        o_ref[...] = _dot(sc.astype(BF16), w_ref[...].astype(BF16))

    return pl.pallas_call(
        body, name="mod_matmul", grid=(n // cb,),
        in_specs=[_full((8, D_MODEL)), pl.BlockSpec((D_MODEL, cb), lambda j: (0, j))],
        out_specs=pl.BlockSpec((8, cb), lambda j: (0, j)),
        out_shape=jax.ShapeDtypeStruct((8, n), F32),
        compiler_params=pltpu.CompilerParams(dimension_semantics=("arbitrary",), vmem_limit_bytes=VMEM_LIMIT),
    )(c_all, ada_w_loc)


def _adam_math(w, g, m, v):
    m = ADAM_B1 * m + (1.0 - ADAM_B1) * g
    v = ADAM_B2 * v + (1.0 - ADAM_B2) * (g * g)
    m_hat = m / (1.0 - ADAM_B1 ** ADAM_STEP)
    v_hat = v / (1.0 - ADAM_B2 ** ADAM_STEP)
    delta = (-ADAM_LR) * (m_hat / (jnp.sqrt(v_hat) + ADAM_EPS) + ADAM_WD * w)
    return delta, m, v


def _adam(name, core, shards):
    n = len(shards)
    r, c = shards[0][0].shape
    half = r // 2
    rb = min(half, 128)
    nh = half // rb

    def body(core_ref, *refs):
        ins, outs = refs[:5 * n], refs[5 * n:]
        mine = (pl.program_id(0) // nh) == core_ref[0]
        for k in range(n):
            w_ref, go_ref, gs_ref, m_ref, v_ref = ins[5 * k:5 * k + 5]
            g_ref, d_ref, mo_ref, vo_ref = outs[4 * k:4 * k + 4]
            g = jnp.where(mine, go_ref[...], gs_ref[...])
            g_ref[...] = g
            d_ref[...], mo_ref[...], vo_ref[...] = _adam_math(w_ref[...], g, m_ref[...], v_ref[...])

    spec = pl.BlockSpec((rb, c), lambda i, core_ref: (i, 0))
    hspec = pl.BlockSpec((rb, c), lambda i, core_ref: (i % nh, 0))
    sds = jax.ShapeDtypeStruct((r, c), F32)
    res = pl.pallas_call(
        body, name=name,
        grid_spec=pltpu.PrefetchScalarGridSpec(
            num_scalar_prefetch=1, grid=(r // rb,),
            in_specs=[spec, hspec, hspec, spec, spec] * n, out_specs=[spec] * (4 * n)),
        out_shape=[sds] * (4 * n),
        compiler_params=pltpu.CompilerParams(dimension_semantics=("arbitrary",), vmem_limit_bytes=VMEM_LIMIT),
    )(core, *[t for s in shards for t in s])
    return [res[4 * k:4 * k + 4] for k in range(n)]


def _ada_grad_adam(sct, dmod_loc, w, m, v):
    r, c = w.shape
    rb = 128

    def body(s_ref, dm_ref, w_ref, m_ref, v_ref, g_ref, d_ref, mo_ref, vo_ref):
        g = s_ref[:, 0:1] * dm_ref[0:1, :]
        for b in range(1, 8):
            g = g + s_ref[:, b:b + 1] * dm_ref[b:b + 1, :]
        g_ref[...] = g
        d_ref[...], mo_ref[...], vo_ref[...] = _adam_math(w_ref[...], g, m_ref[...], v_ref[...])

    spec = pl.BlockSpec((rb, c), lambda i: (i, 0))
    sds = jax.ShapeDtypeStruct((r, c), F32)
    return pl.pallas_call(
        body, name="ada_grad_adam", grid=(r // rb,),
        in_specs=[pl.BlockSpec((rb, 8), lambda i: (i, 0)), _full((8, c)), spec, spec, spec],
        out_specs=[spec] * 4, out_shape=[sds] * 4,
        compiler_params=pltpu.CompilerParams(dimension_semantics=("arbitrary",), vmem_limit_bytes=VMEM_LIMIT),
    )(sct, dmod_loc, w, m, v)


def _position():
    x, y, c = lax.axis_index("x"), lax.axis_index("y"), lax.axis_index("c")
    chips = [(1 - x, y), (x, 1 - y), (1 - x, 1 - y)]
    return x, y, c, chips


def _allgather8(name, arrs, shards=()):
    na, ns = len(arrs), len(shards)

    def body(*refs):
        ins, w_own = refs[:na], refs[na:na + ns]
        outs, w_all = refs[na + ns:2 * na + ns], refs[2 * na + ns:2 * (na + ns)]
        send_sems, recv_sems, local_sems = refs[2 * (na + ns):2 * (na + ns) + 3]
        if ns:
            _ag_start(w_own, w_all, *refs[-2:])
        x, y, c, chips = _position()
        me, sibling = (x, y, c), (x, y, 1 - c)
        first, passed, local = [], [], []
        for a in range(na):
            m_per = ins[a].shape[0]

            def rows(px, py, pc, a=a, m_per=m_per):
                return outs[a].at[pl.ds((4 * px + 2 * py + pc) * m_per, m_per), :]

            def copy(k, block, to, src=None, a=a, rows=rows):
                return pltpu.make_async_remote_copy(
                    src_ref=rows(*block) if src is None else src, dst_ref=rows(*block),
                    send_sem=send_sems.at[7 * a + k], recv_sem=recv_sems.at[7 * a + k],
                    device_id=to, device_id_type=MESH)

            mine = pltpu.make_async_copy(ins[a], rows(*me), local_sems.at[a])
            mine.start()
            local.append(mine)
            f = [copy(0, me, sibling, src=ins[a])]
            f += [copy(1 + j, me, (*chip, c), src=ins[a]) for j, chip in enumerate(chips)]
            for cp in f:
                cp.start()
            first.append((f, copy))
        for a in range(na):
            f, copy = first[a]
            p = [copy(4 + j, (*chip, c), sibling) for j, chip in enumerate(chips)]
            for j, chip in enumerate(chips):
                copy(1 + j, (*chip, c), me).wait_recv()
                p[j].start()
            passed.append(p)
        for a in range(na):
            f, copy = first[a]
            copy(0, sibling, me).wait_recv()
            for j, chip in enumerate(chips):
                copy(4 + j, (*chip, 1 - c), me).wait_recv()
            for cp in f + passed[a]:
                cp.wait_send()
            local[a].wait()
        if ns:
            _ag_finish(w_own, w_all, *refs[-2:])

    return pl.pallas_call(
        body, name=name,
        out_shape=[jax.ShapeDtypeStruct((8 * a.shape[0], a.shape[1]), a.dtype) for a in arrs]
        + [jax.ShapeDtypeStruct((N_CHIP,) + s.shape, s.dtype) for s in shards],
        in_specs=[VMEM] * na + [ANY] * ns, out_specs=[VMEM] * na + [ANY] * ns,
        scratch_shapes=[pltpu.SemaphoreType.DMA((7 * na,)), pltpu.SemaphoreType.DMA((7 * na,)),
                        pltpu.SemaphoreType.DMA((na,))]
        + [pltpu.SemaphoreType.DMA((AG_SEMS * ns,))] * (2 if ns else 0),
        compiler_params=pltpu.CompilerParams(vmem_limit_bytes=VMEM_LIMIT),
    )(*arrs, *shards)


AG_SEMS = 7


def _ag_copies(ins, outs, send_sems, recv_sems):
    x, y, c, chips = _position()
    sibling = (x, y, 1 - c)
    xn, yn, dg = [2 * chip[0] + chip[1] for chip in chips]
    to_x, to_y = (1 - x, y, c), (x, 1 - y, c)
    res = []
    for a in range(len(ins)):
        half = ins[a].shape[0] // 2
        quarter = half // 2

        def copy(k, dst, to, src=None, a=a):
            return pltpu.make_async_remote_copy(
                src_ref=dst if src is None else src, dst_ref=dst,
                send_sem=send_sems.at[AG_SEMS * a + k], recv_sem=recv_sems.at[AG_SEMS * a + k],
                device_id=to, device_id_type=MESH)

        def rows(chip, pc, q=None, a=a, half=half, quarter=quarter):
            if q is None:
                return outs[a].at[chip, pl.ds(pc * half, half), :]
            return outs[a].at[chip, pl.ds(pc * half + q * quarter, quarter), :]

        own = ins[a].at[pl.ds(c * half, half), :]
        mine = rows(2 * x + y, c)
        res.append(dict(
            sends=[copy(0, mine, to_x, src=own), copy(1, mine, to_y, src=own)],
            from_x=copy(0, rows(xn, c), to_x), from_y=copy(1, rows(yn, c), to_y),
            relay_y=copy(2, rows(xn, c, 0), to_y), relay_x=copy(3, rows(yn, c, 1), to_x),
            from_y_relay=copy(2, rows(dg, c, 0), to_y), from_x_relay=copy(3, rows(dg, c, 1), to_x),
            pass_on=[copy(4, rows(xn, c), sibling), copy(5, rows(yn, c), sibling), copy(6, rows(dg, c), sibling)],
            from_sibling=[copy(4, rows(xn, 1 - c), sibling), copy(5, rows(yn, 1 - c), sibling),
                          copy(6, rows(dg, 1 - c), sibling)]))
    return res


def _ag_start(ins, outs, send_sems, recv_sems):
    for cps in _ag_copies(ins, outs, send_sems, recv_sems):
        for cp in cps["sends"]:
            cp.start()


def _ag_relay(ins, outs, send_sems, recv_sems, which):
    copies = _ag_copies(ins, outs, send_sems, recv_sems)
    for a in which:
        cps = copies[a]
        cps["from_x"].wait_recv()
        cps["relay_y"].start()
        cps["pass_on"][0].start()
        cps["from_y"].wait_recv()
        cps["relay_x"].start()
        cps["pass_on"][1].start()


def _ag_complete(ins, outs, send_sems, recv_sems):
    copies = _ag_copies(ins, outs, send_sems, recv_sems)
    for cps in copies:
        cps["from_y_relay"].wait_recv()
        cps["from_x_relay"].wait_recv()
        cps["pass_on"][2].start()
    for cps in copies:
        for cp in cps["from_sibling"]:
            cp.wait_recv()
        for cp in cps["sends"] + [cps["relay_y"], cps["relay_x"]] + cps["pass_on"]:
            cp.wait_send()


def _ag_finish(ins, outs, send_sems, recv_sems):
    _ag_relay(ins, outs, send_sems, recv_sems, range(len(ins)))
    _ag_complete(ins, outs, send_sems, recv_sems)


def _swap_halves(name, grads):
    na = len(grads)

    def body(*refs):
        ins, outs = refs[:na], refs[na:2 * na]
        send_sems, recv_sems = refs[2 * na:]
        x, y, c, _ = _position()
        cps = []
        for a in range(na):
            half = ins[a].shape[1] // 2
            cp = pltpu.make_async_remote_copy(
                src_ref=ins[a].at[:, pl.ds((1 - c) * half, half), :], dst_ref=outs[a],
                send_sem=send_sems.at[a], recv_sem=recv_sems.at[a],
                device_id=(x, y, 1 - c), device_id_type=MESH)
            cp.start()
            cps.append(cp)
        for cp in cps:
            cp.wait()

    return pl.pallas_call(
        body, name=name,
        out_shape=[jax.ShapeDtypeStruct((g.shape[0], g.shape[1] // 2, g.shape[2]), g.dtype) for g in grads],
        in_specs=[ANY] * na, out_specs=[ANY] * na,
        scratch_shapes=[pltpu.SemaphoreType.DMA((na,)), pltpu.SemaphoreType.DMA((na,))],
    )(*grads)


def _xchg_copies(ins, outs, send_sems, recv_sems):
    x, y, c, chips = _position()
    return [pltpu.make_async_remote_copy(
        src_ref=ins[a].at[2 * chip[0] + chip[1]], dst_ref=outs[a].at[j],
        send_sem=send_sems.at[3 * a + j], recv_sem=recv_sems.at[3 * a + j],
        device_id=(*chip, c), device_id_type=MESH) for a in range(len(ins)) for j, chip in enumerate(chips)]


def _exchange_chips(parts):
    na = len(parts)

    def body(*refs):
        ins, outs = refs[:na], refs[na:2 * na]
        send_sems, recv_sems = refs[2 * na:]
        for cp in _xchg_copies(ins, outs, send_sems, recv_sems):
            cp.start()
        for cp in _xchg_copies(ins, outs, send_sems, recv_sems):
            cp.wait()

    return pl.pallas_call(
        body, name="rs_exchange_chips",
        out_shape=[jax.ShapeDtypeStruct((3,) + p.shape[1:], p.dtype) for p in parts],
        in_specs=[ANY] * na, out_specs=[ANY] * na,
        scratch_shapes=[pltpu.SemaphoreType.DMA((3 * na,)), pltpu.SemaphoreType.DMA((3 * na,))],
    )(*parts)


def _swap_reduced(name, halves):
    na = len(halves)

    def body(*refs):
        ins, outs = refs[:na], refs[na:2 * na]
        send_sems, recv_sems = refs[2 * na:]
        x, y, c, _ = _position()
        cps = []
        for a in range(na):
            cp = pltpu.make_async_remote_copy(
                src_ref=ins[a], dst_ref=outs[a], send_sem=send_sems.at[a], recv_sem=recv_sems.at[a],
                device_id=(x, y, 1 - c), device_id_type=MESH)
            cp.start()
            cps.append(cp)
        for cp in cps:
            cp.wait()

    return pl.pallas_call(
        body, name=name,
        out_shape=[jax.ShapeDtypeStruct(h.shape, h.dtype) for h in halves],
        in_specs=[ANY] * na, out_specs=[ANY] * na,
        scratch_shapes=[pltpu.SemaphoreType.DMA((na,)), pltpu.SemaphoreType.DMA((na,))],
    )(*halves)


def _add_sibling(name, grad, recv, core):
    _, r, c = grad.shape
    half = r // 2
    rb = min(half, 256)
    nrb = half // rb

    def body(core_ref, g_ref, r_ref, o_ref):
        o_ref[...] = (g_ref[...].astype(F32) + r_ref[...].astype(F32)).astype(BF16)

    return pl.pallas_call(
        body, name=name,
        grid_spec=pltpu.PrefetchScalarGridSpec(
            num_scalar_prefetch=1, grid=(N_CHIP, nrb),
            in_specs=[pl.BlockSpec((1, rb, c), lambda j, i, core_ref: (j, core_ref[0] * nrb + i, 0)),
                      pl.BlockSpec((1, rb, c), lambda j, i, core_ref: (j, i, 0))],
            out_specs=pl.BlockSpec((1, rb, c), lambda j, i, core_ref: (j, i, 0))),
        out_shape=jax.ShapeDtypeStruct((N_CHIP, half, c), BF16),
        compiler_params=pltpu.CompilerParams(dimension_semantics=("arbitrary", "arbitrary"),
                                             vmem_limit_bytes=VMEM_LIMIT),
    )(core, grad, recv)


def _add_chips(name, chip, p, q):
    _, half, c = q.shape
    rb = min(half, 256)

    def body(chip_ref, p_ref, q_ref, o_ref):
        acc = p_ref[0].astype(F32)
        for j in range(3):
            acc = acc + q_ref[j].astype(F32)
        o_ref[...] = acc

    return pl.pallas_call(
        body, name=name,
        grid_spec=pltpu.PrefetchScalarGridSpec(
            num_scalar_prefetch=1, grid=(half // rb,),
            in_specs=[pl.BlockSpec((1, rb, c), lambda i, chip_ref: (chip_ref[0], i, 0)),
                      pl.BlockSpec((3, rb, c), lambda i, chip_ref: (0, i, 0))],
            out_specs=pl.BlockSpec((rb, c), lambda i, chip_ref: (i, 0))),
        out_shape=jax.ShapeDtypeStruct((half, c), F32),
        compiler_params=pltpu.CompilerParams(dimension_semantics=("arbitrary",), vmem_limit_bytes=VMEM_LIMIT),
    )(chip, p, q)


def _small_update(gad, gam, gl, gg, mychip, params):
    names = ["ada_b", "norm1_g", "lru_conv_b", "gate_a_w", "gate_a_b", "gate_x_w", "gate_x_b", "a_param",
             "lru_conv_w", "short_conv_w", "lru_out_g", "conv_out_g", "norm2_g", "final_g"]
    flat = [t for n in names for t in params[n]]
    nin = len(flat)

    def body(chip_ref, gad_ref, gam_ref, gl_ref, gg_ref, *refs):
        ins = {n: refs[3 * k:3 * k + 3] for k, n in enumerate(names)}
        outs = {n: refs[nin + 4 * k:nin + 4 * k + 4] for k, n in enumerate(names)}
        loss_ref, dmod_ref = refs[nin + 4 * len(names):nin + 4 * len(names) + 2]

        def dsum(ref, lo, n):
            per = ref.shape[0] // 8
            acc = ref[lo:lo + n, :].astype(F32)
            for dev in range(1, 8):
                acc = acc + ref[dev * per + lo:dev * per + lo + n, :].astype(F32)
            return acc

        def update(n, g):
            w_ref, m_ref, v_ref = ins[n]
            g_ref, d_ref, mo_ref, vo_ref = outs[n]
            g_ref[...] = g
            d_ref[...], mo_ref[...], vo_ref[...] = _adam_math(w_ref[...], g, m_ref[...], v_ref[...])

        d, dm, l, lw = refs[-4:]
        d[...] = dsum(gad_ref, 0, 8)
        dm[...] = dsum(gam_ref, 0, 8)
        l[...] = dsum(gl_ref, 0, 16)
        for dev in range(8):
            for k in range(3):
                dmod_ref[dev:dev + 1, k * D_MODEL:(k + 1) * D_MODEL] = gad_ref[dev * 8 + k:dev * 8 + k + 1, :]
                dmod_ref[dev:dev + 1, (3 + k) * D_MODEL:(4 + k) * D_MODEL] = gam_ref[dev * 8 + k:dev * 8 + k + 1, :]
        w_ref, m_ref, v_ref = ins["ada_b"]
        g_ref, d_ref, mo_ref, vo_ref = outs["ada_b"]
        for k in range(3):
            g_ref[:, k * D_MODEL:(k + 1) * D_MODEL] = d[k:k + 1, :]
            g_ref[:, (3 + k) * D_MODEL:(4 + k) * D_MODEL] = dm[k:k + 1, :]
        d_ref[...], mo_ref[...], vo_ref[...] = _adam_math(w_ref[...], g_ref[...], m_ref[...], v_ref[...])
        update("norm1_g", d[3:4, :])
        update("norm2_g", dm[3:4, :])
        update("final_g", dm[4:5, :])
        update("gate_a_b", d[4:5, 0:D_LRU])
        update("gate_x_b", d[4:5, D_LRU:2 * D_LRU])
        update("lru_conv_b", l[4:5, :])
        update("a_param", l[8:9, :] * jax.nn.sigmoid(ins["a_param"][0][...]))
        update("lru_out_g", l[9:10, :])
        update("conv_out_g", l[10:11, :])
        loss_ref[...] = jnp.broadcast_to(dm[5:6, 0:128], (8, 128))
        chip = chip_ref[0]
        acc = jnp.zeros((8, 128), F32)
        for j in range(N_CHIP):
            acc = acc + jnp.where(chip == j, l[0:8, j * 128:(j + 1) * 128], 0.0)
        lw[...] = acc
        update("lru_conv_w", lw[0:4, :])
        update("short_conv_w", lw[5:8, :])
        gates = dsum(gg_ref, 0, D_LRU)
        update("gate_a_w", gates[:, 0:HEAD])
        update("gate_x_w", gates[:, HEAD:2 * HEAD])

    out_shape = []
    for n in names:
        out_shape += [jax.ShapeDtypeStruct(params[n][0].shape, F32)] * 4
    out_shape += [jax.ShapeDtypeStruct((8, 128), F32), jax.ShapeDtypeStruct((8, 6 * D_MODEL), F32)]
    res = pl.pallas_call(
        body, name="small_update", out_shape=out_shape,
        in_specs=[SMEM] + [VMEM] * (4 + nin),
        out_specs=[VMEM] * len(out_shape),
        scratch_shapes=[pltpu.VMEM((8, D_MODEL), F32), pltpu.VMEM((8, D_MODEL), F32), pltpu.VMEM((16, D_LRU), F32),
                        pltpu.VMEM((8, 128), F32)],
        compiler_params=pltpu.CompilerParams(vmem_limit_bytes=VMEM_LIMIT),
    )(mychip, gad, gam, gl, gg, *flat)
    per = {n: res[4 * k:4 * k + 4] for k, n in enumerate(names)}
    return per, res[-2], res[-1]


def _block_diag(w):
    eye = jnp.eye(8, dtype=w.dtype)
    return (eye[:, None, :, None] * w[:, :, None, :]).reshape(8 * HEAD, 8 * HEAD)


def _diag_blocks(g):
    return jnp.concatenate([g[h * HEAD:(h + 1) * HEAD, h * HEAD:(h + 1) * HEAD] for h in range(8)], axis=0)


def kernel(x, c, ada_w, ada_b, norm1_g, w_in, lru_conv_w, lru_conv_b, gate_a_w, gate_a_b, gate_x_w, gate_x_b, a_param, short_conv_w, lru_out_g, conv_out_g, w_out, norm2_g, w_mlp1, w_mlp2, final_g, loss_target, m_ada_w, m_ada_b, m_norm1_g, m_w_in, m_lru_conv_w, m_lru_conv_b, m_gate_a_w, m_gate_a_b, m_gate_x_w, m_gate_x_b, m_a_param, m_short_conv_w, m_lru_out_g, m_conv_out_g, m_w_out, m_norm2_g, m_w_mlp1, m_w_mlp2, m_final_g, v_ada_w, v_ada_b, v_norm1_g, v_w_in, v_lru_conv_w, v_lru_conv_b, v_gate_a_w, v_gate_a_b, v_gate_x_w, v_gate_x_b, v_a_param, v_short_conv_w, v_lru_out_g, v_conv_out_g, v_w_out, v_norm2_g, v_w_mlp1, v_w_mlp2, v_final_g):
    xi, yi, ci = lax.axis_index("x"), lax.axis_index("y"), lax.axis_index("c")
    mychip = 2 * xi + yi
    me = 4 * xi + 2 * yi + ci

    own_in, own_out, own_w1, own_w2 = [w[0].astype(BF16) for w in (w_in, w_out, w_mlp1, w_mlp2)]
    c_blk = jnp.zeros((8, D_MODEL), F32).at[0:1].set(c)
    cw_blk = jnp.zeros((8, 128), F32).at[0:4].set(lru_conv_w[0]).at[4:7].set(short_conv_w[0])
    c_g, cw_g, win_all, wout_all = _allgather8("allgather_cond_weights", [c_blk, cw_blk], [own_in, own_out])
    c_all = c_g.reshape(8, 8, D_MODEL)[:, 0]
    cw_g = cw_g.reshape(4, 2, 8, 128)[:, 0]
    lcw = cw_g[:, 0:4].transpose(1, 0, 2).reshape(4, D_LRU)
    scw = cw_g[:, 4:7].transpose(1, 0, 2).reshape(3, D_LRU)

    mod_loc = _mod_matmul(c_all, ada_w[0])
    (mod_g,) = _allgather8("allgather_mod", [mod_loc])
    mod_all = mod_g.reshape(4, 2, 8, 6 * D_MODEL // 4)[:, 0].transpose(1, 0, 2).reshape(8, 6 * D_MODEL) + ada_b
    mod_pad = jnp.pad(mod_all.reshape(8, 6, D_MODEL), ((0, 0), (0, 2), (0, 0)))
    mod = lax.dynamic_slice_in_dim(mod_pad, me, 1, axis=0).reshape(8, D_MODEL)

    win, wout = (win_all, own_in), (wout_all, own_out)
    chip = mychip.reshape(1).astype(jnp.int32)
    core = ci.reshape(1).astype(jnp.int32)

    vecd = jnp.concatenate([norm1_g, norm2_g, final_g[None, :], jnp.concatenate([gate_a_b, gate_x_b], axis=1),
                            jnp.zeros((4, D_MODEL), F32)], axis=0)
    vecl = jnp.concatenate([lcw, lru_conv_b, scw, a_param, lru_out_g, conv_out_g, jnp.zeros((5, D_LRU), F32)], axis=0)
    gab = jnp.concatenate([_block_diag(gate_a_w[0]), _block_diag(gate_x_w[0])], axis=1).astype(BF16)
    a64 = _block_diag(jnp.full((8, HEAD, HEAD), 1.0 / HEAD, F32)).astype(BF16)

    hb, proj, hl, ycat, mixed, x1, w1_all, w2_all = _mix_fwd(
        chip, x[0], mod, vecd, vecl, win, wout, gab, a64, [own_w1, own_w2])
    dx1, act, dz, dmo, h2b, accm = _mlp_fwd_bwd(
        chip, x1, loss_target[0], mod, vecd, (w1_all, own_w1), (w2_all, own_w2))

    def sibling_sum(tag, grads):
        recv = _swap_halves("rs_swap_halves_" + tag, grads)
        return [_add_sibling("rs_add_sibling_%s%d" % (tag, k), g, r, core) for k, (g, r) in enumerate(zip(grads, recv))]

    parts_mlp = sibling_sum("mlp", [_wgrad("wgrad_mlp1", h2b, dz, 0, FF_BLK, BF16),
                                    _wgrad("wgrad_mlp2", act, dmo, FF_BLK, 0, BF16)])
    grad_x, dproj, dmixed, xlb, dgb, accd, accl, q_w1, q_w2 = _mix_bwd(
        chip, dx1, x[0], mixed, proj, hl, mod, vecd, vecl, win, wout, gab, a64, parts_mlp)
    parts_mix = sibling_sum("mix", [_wgrad("wgrad_in", hb, dproj, 0, WIN_BLK, BF16),
                                    _wgrad("wgrad_out", ycat, dmixed, WOUT_BLK, 0, BF16)])
    g_gate = _wgrad("wgrad_gate", xlb, dgb, 0, 0, F32)[0]

    landed = list(_exchange_chips(parts_mix)) + [q_w1, q_w2]
    g_own = [_add_chips("rs_add_chips_%d" % k, chip, p, q) for k, (p, q) in enumerate(zip(parts_mix + parts_mlp, landed))]
    g_sib = _swap_reduced("rs_swap_reduced", g_own)
    res_w1, res_w2 = _adam("adam_mlp", core, [(w_mlp1[0], g_own[2], g_sib[2], m_w_mlp1[0], v_w_mlp1[0]),
                                              (w_mlp2[0], g_own[3], g_sib[3], m_w_mlp2[0], v_w_mlp2[0])])
    (res_win,) = _adam("adam_w_in", core, [(w_in[0], g_own[0], g_sib[0], m_w_in[0], v_w_in[0])])
    (res_wout,) = _adam("adam_w_out", core, [(w_out[0], g_own[1], g_sib[1], m_w_out[0], v_w_out[0])])

    gg_blk = jnp.concatenate([_diag_blocks(g_gate[:, 0:D_LRU]), _diag_blocks(g_gate[:, D_LRU:2 * D_LRU])], axis=1)
    gad, gam, gl, gg = _allgather8("allgather_small_grads", [accd, accm, accl, gg_blk.astype(BF16)])

    params = {
        "ada_b": (ada_b, m_ada_b, v_ada_b), "norm1_g": (norm1_g, m_norm1_g, v_norm1_g),
        "lru_conv_b": (lru_conv_b, m_lru_conv_b, v_lru_conv_b),
        "gate_a_w": tuple(t.reshape(D_LRU, HEAD) for t in (gate_a_w, m_gate_a_w, v_gate_a_w)),
        "gate_a_b": (gate_a_b, m_gate_a_b, v_gate_a_b),
        "gate_x_w": tuple(t.reshape(D_LRU, HEAD) for t in (gate_x_w, m_gate_x_w, v_gate_x_w)),
        "gate_x_b": (gate_x_b, m_gate_x_b, v_gate_x_b), "a_param": (a_param, m_a_param, v_a_param),
        "lru_conv_w": tuple(t[0] for t in (lru_conv_w, m_lru_conv_w, v_lru_conv_w)),
        "short_conv_w": tuple(t[0] for t in (short_conv_w, m_short_conv_w, v_short_conv_w)),
        "lru_out_g": (lru_out_g, m_lru_out_g, v_lru_out_g), "conv_out_g": (conv_out_g, m_conv_out_g, v_conv_out_g),
        "norm2_g": (norm2_g, m_norm2_g, v_norm2_g),
        "final_g": tuple(t[None, :] for t in (final_g, m_final_g, v_final_g)),
    }
    small, loss_blk, dmod_cols = _small_update(gad, gam, gl, gg, chip, params)
    loss = loss_blk[0, 0]

    ncol = 6 * D_MODEL // N_CHIP
    dmod_loc = lax.dynamic_slice_in_dim(dmod_cols, mychip * ncol, ncol, axis=1)
    sct = (c_all * jax.nn.sigmoid(c_all)).T
    ada = _ada_grad_adam(sct, dmod_loc, ada_w[0], m_ada_w[0], v_ada_w[0])

    res = {"ada_w": ada, "w_in": res_win, "w_out": res_wout, "w_mlp1": res_w1, "w_mlp2": res_w2}
    res = {n: tuple(t[None] for t in r) for n, r in res.items()}
    shapes = {"gate_a_w": gate_a_w.shape, "gate_x_w": gate_x_w.shape, "lru_conv_w": lru_conv_w.shape,
              "short_conv_w": short_conv_w.shape, "final_g": final_g.shape}
    for n, t in small.items():
        res[n] = tuple(u.reshape(shapes[n]) if n in shapes else u for u in t)

    order = ["ada_w", "ada_b", "norm1_g", "w_in", "lru_conv_w", "lru_conv_b", "gate_a_w", "gate_a_b", "gate_x_w",
             "gate_x_b", "a_param", "short_conv_w", "lru_out_g", "conv_out_g", "w_out", "norm2_g", "w_mlp1",
             "w_mlp2", "final_g"]
    return (loss, grad_x[None], *[res[n][0] for n in order], *[res[n][1] for n in order],
            *[res[n][2] for n in order], *[res[n][3] for n in order])
```

```python
import jax
import jax.numpy as jnp
from jax import lax
from jax.experimental import pallas as pl
from jax.experimental.pallas import tpu as pltpu

F32 = jnp.float32
BF16 = jnp.bfloat16

D_MODEL = 1024
D_LRU = 512
D_IN = 2560
D_FF = 4096
N_CHIP = 4
WIN_BLK = D_IN // N_CHIP
WOUT_BLK = D_MODEL // N_CHIP
FF_BLK = D_FF // N_CHIP
HEAD = 64
EPS = 1e-6
C_GATE = 8.0
TOKEN_TILE = 256
HALO = 8
VMEM_LIMIT = 60 * 1024 * 1024

ADAM_LR = 0.001
ADAM_B1 = 0.9
ADAM_B2 = 0.999
ADAM_EPS = 1e-08
ADAM_WD = 0.01
ADAM_STEP = 10

MESH = pl.DeviceIdType.MESH
ANY = pl.BlockSpec(memory_space=pl.ANY)
VMEM = pl.BlockSpec(memory_space=pltpu.VMEM)
SMEM = pl.BlockSpec(memory_space=pltpu.SMEM)


def _full(shape, single=False):
    nd = len(shape)
    if single:
        return pl.BlockSpec(shape, lambda *_: (0,) * nd, pipeline_mode=pl.Buffered(1))
    return pl.BlockSpec(shape, lambda *_: (0,) * nd)


def _dot(a, b):
    return jnp.dot(a, b, preferred_element_type=F32)


def _dot_nt(a, b):
    return lax.dot_general(a, b, (((1,), (1,)), ((), ())), preferred_element_type=F32)


def _dot_tn(a, b):
    return lax.dot_general(a, b, (((0,), (0,)), ((), ())), preferred_element_type=F32)


def _gmean(v, a64):
    hi = v.astype(BF16)
    lo = (v - hi.astype(F32)).astype(BF16)
    return _dot(hi, a64) + _dot(lo, a64)


def _gelu(x):
    u = 0.7978845608028654 * (x + 0.044715 * x * x * x)
    t = jnp.tanh(u)
    return 0.5 * x * (1.0 + t), t


def _gelu_grad(x, t):
    du = 0.7978845608028654 * (1.0 + 3.0 * 0.044715 * x * x)
    return 0.5 * (1.0 + t) + 0.5 * x * (1.0 - t * t) * du


def _log1p_pos(y):
    return jnp.where(y < 1e-2, y * (1.0 - y * (0.5 - y * (1.0 / 3.0 - y * 0.25))), jnp.log(1.0 + y))


def _softplus(a):
    return jnp.maximum(a, 0.0) + _log1p_pos(jnp.exp(-jnp.abs(a)))


def _neg_expm1(z):
    series = -z * (1.0 + z * (0.5 + z * (1.0 / 6.0 + z * (1.0 / 24.0 + z * (1.0 / 120.0)))))
    return jnp.where(z > -0.02, series, 1.0 - jnp.exp(z))


def _scan_fwd(a, b, row):
    n = a.shape[0]
    d = 1
    while d < n:
        m = row >= d
        b = jnp.where(m, a * pltpu.roll(b, d, 0) + b, b)
        a = jnp.where(m, a * pltpu.roll(a, d, 0), a)
        d *= 2
    return a, b


def _scan_rev(a, b, row):
    n = a.shape[0]
    d = 1
    while d < n:
        m = row < n - d
        b = jnp.where(m, b + a * pltpu.roll(b, n - d, 0), b)
        a = jnp.where(m, a * pltpu.roll(a, n - d, 0), a)
        d *= 2
    return a, b


def _colsum(v):
    return jnp.sum(v, axis=0, keepdims=True)


def _load_gathered(chip, gathered, own, slot, sems):
    copies = []
    for j in range(N_CHIP):
        @pl.when(chip == j)
        def _(j=j):
            pltpu.make_async_copy(own, slot(j), sems.at[j]).start()

        @pl.when(chip != j)
        def _(j=j):
            pltpu.make_async_copy(gathered.at[j], slot(j), sems.at[j]).start()

        copies.append(pltpu.make_async_copy(own, slot(j), sems.at[j]))
    return copies


def _lru_gates(xlb, gab, gbias, sp, first_row):
    g = _dot(xlb, gab) + gbias
    r = jax.nn.sigmoid(g[:, :D_LRU])
    ig = jax.nn.sigmoid(g[:, D_LRU:])
    la = (-C_GATE) * r * sp
    a = jnp.exp(la)
    msq = jnp.sqrt(_neg_expm1(2.0 * la))
    mult = jnp.where(first_row, 1.0, msq)
    return r, ig, a, msq, mult


def _mix_fwd(chip, x, mod, vecd, vecl, win, wout, gab, a64, mlp_shards):
    s = x.shape[0]
    ts = TOKEN_TILE
    nt = s // ts

    def body(chip_ref, x_ref, mod_ref, vd_ref, vl_ref, win_hbm, win_own, wout_hbm, wout_own, gab_ref, a64_ref,
             w1_own, w2_own,
             hb_ref, proj_ref, hl_ref, ycat_ref, mixed_ref, x1_ref, w1_all, w2_all,
             win_ref, wout_ref, ext_lx, ext_cv, hcar, sems, ag_send, ag_recv):
        i = pl.program_id(0)

        @pl.when(i == 0)
        def _():
            _ag_start((w1_own, w2_own), (w1_all, w2_all), ag_send, ag_recv)
            cps = _load_gathered(chip_ref[0], win_hbm, win_own, lambda j: win_ref.at[j], sems.at[pl.ds(0, N_CHIP)])
            cps += _load_gathered(chip_ref[0], wout_hbm, wout_own,
                                  lambda j: wout_ref.at[pl.ds(j * WOUT_BLK, WOUT_BLK), :],
                                  sems.at[pl.ds(N_CHIP, N_CHIP)])
            ext_lx[0:HALO, :] = jnp.zeros((HALO, D_LRU), F32)
            ext_cv[0:HALO, :] = jnp.zeros((HALO, D_LRU), F32)
            hcar[...] = jnp.zeros_like(hcar)
            for cp in cps:
                cp.wait()

        row = lax.broadcasted_iota(jnp.int32, (ts, D_LRU), 0)
        first_row = jnp.logical_and(row == 0, i == 0)
        xt = x_ref[...]
        shift1, scale1, gate1 = mod_ref[0:1, :], mod_ref[1:2, :], mod_ref[2:3, :]
        r1 = lax.rsqrt(jnp.mean(xt * xt, axis=-1, keepdims=True) + EPS)
        h = (xt * r1) * vd_ref[0:1, :] * (1.0 + scale1) + shift1
        hb = h.astype(BF16)
        hb_ref[...] = hb
        for j in range(N_CHIP):
            proj_ref[:, j * WIN_BLK:(j + 1) * WIN_BLK] = _dot(hb, win_ref[j])
        u_ly = proj_ref[:, 512:1024]
        u_b = proj_ref[:, 1024:1536]

        ext_lx[HALO:HALO + ts, :] = proj_ref[:, 0:512]
        xl = vl_ref[4:5, :] + vl_ref[0:1, :] * ext_lx[pl.ds(5, ts), :]
        for k in range(1, 4):
            xl = xl + vl_ref[k:k + 1, :] * ext_lx[pl.ds(5 + k, ts), :]
        ext_lx[0:HALO, :] = ext_lx[ts:ts + HALO, :]
        sp = _softplus(vl_ref[8:9, :])
        _, ig, a, _, mult = _lru_gates(xl.astype(BF16), gab_ref[...], vd_ref[3:4, :], sp, first_row)
        acum, hloc = _scan_fwd(a, mult * (ig * xl), row)
        hl = hloc + acum * hcar[0:1, :]
        hl_ref[...] = hl
        hcar[0:1, :] = hl_ref[ts - 1:ts, :]
        ge, _ = _gelu(u_ly)
        p = ge * hl
        y_lru = p * lax.rsqrt(_gmean(p * p, a64_ref[...]) + EPS) * vl_ref[9:10, :]
        ycat_ref[:, 0:512] = y_lru.astype(BF16)

        ext_cv[HALO:HALO + ts, :] = proj_ref[:, 1536:2048] * proj_ref[:, 2048:2560]
        q = vl_ref[5:6, :] * ext_cv[pl.ds(6, ts), :]
        for k in range(1, 3):
            q = q + vl_ref[5 + k:6 + k, :] * ext_cv[pl.ds(6 + k, ts), :]
        ext_cv[0:HALO, :] = ext_cv[ts:ts + HALO, :]
        yc = u_b * q
        y_conv = yc * lax.rsqrt(_gmean(yc * yc, a64_ref[...]) + EPS) * vl_ref[10:11, :]
        ycat_ref[:, 512:1024] = y_conv.astype(BF16)

        mixed = _dot(ycat_ref[...], wout_ref[...])
        mixed_ref[...] = mixed
        x1_ref[...] = xt + gate1 * mixed

        @pl.when(i == max(nt - 3, 0))
        def _():
            _ag_relay((w1_own, w2_own), (w1_all, w2_all), ag_send, ag_recv, [0])

        @pl.when(i == nt - 1)
        def _():
            _ag_relay((w1_own, w2_own), (w1_all, w2_all), ag_send, ag_recv, [1])
            _ag_complete((w1_own, w2_own), (w1_all, w2_all), ag_send, ag_recv)

    tile = lambda w: pl.BlockSpec((ts, w), lambda i: (i, 0))
    return pl.pallas_call(
        body, name="mix_fwd", grid=(nt,),
        in_specs=[SMEM, tile(D_MODEL), _full((8, D_MODEL)), _full((8, D_MODEL)), _full((16, D_LRU)),
                  ANY, ANY, ANY, ANY, _full((D_LRU, 2 * D_LRU), True), _full((D_LRU, D_LRU), True), ANY, ANY],
        out_specs=[tile(D_MODEL), tile(D_IN), tile(D_LRU), tile(D_MODEL), tile(D_MODEL), tile(D_MODEL), ANY, ANY],
        out_shape=[jax.ShapeDtypeStruct((s, D_MODEL), BF16), jax.ShapeDtypeStruct((s, D_IN), F32),
                   jax.ShapeDtypeStruct((s, D_LRU), F32), jax.ShapeDtypeStruct((s, D_MODEL), BF16),
                   jax.ShapeDtypeStruct((s, D_MODEL), F32), jax.ShapeDtypeStruct((s, D_MODEL), F32)]
        + [jax.ShapeDtypeStruct((N_CHIP,) + w.shape, w.dtype) for w in mlp_shards],
        scratch_shapes=[pltpu.VMEM((N_CHIP, D_MODEL, WIN_BLK), BF16), pltpu.VMEM((D_MODEL, D_MODEL), BF16),
                        pltpu.VMEM((ts + HALO, D_LRU), F32), pltpu.VMEM((ts + HALO, D_LRU), F32),
                        pltpu.VMEM((HALO, D_LRU), F32), pltpu.SemaphoreType.DMA((2 * N_CHIP,)),
                        pltpu.SemaphoreType.DMA((2 * AG_SEMS,)), pltpu.SemaphoreType.DMA((2 * AG_SEMS,))],
        compiler_params=pltpu.CompilerParams(dimension_semantics=("arbitrary",), vmem_limit_bytes=VMEM_LIMIT),
    )(chip, x, mod, vecd, vecl, *win, *wout, gab, a64, *mlp_shards)


def _mlp_fwd_bwd(chip, x1, target, mod, vecd, w1, w2):
    s = x1.shape[0]
    ts = TOKEN_TILE
    nt = s // ts

    def body(chip_ref, x1_ref, tg_ref, mod_ref, vd_ref, w1_hbm, w1_own, w2_hbm, w2_own,
             dx1_ref, act_ref, dz_ref, dmo_ref, h2_ref, acc_ref, w1_v, w2_v, rz_v, sems):
        i = pl.program_id(0)

        @pl.when(i == 0)
        def _():
            cps = _load_gathered(chip_ref[0], w1_hbm, w1_own, lambda j: w1_v.at[j], sems.at[pl.ds(0, N_CHIP)])
            cps += _load_gathered(chip_ref[0], w2_hbm, w2_own, lambda j: w2_v.at[j], sems.at[pl.ds(N_CHIP, N_CHIP)])
            acc_ref[...] = jnp.zeros_like(acc_ref)
            for cp in cps:
                cp.wait()

        xt = x1_ref[...]
        shift2, scale2, gate2 = mod_ref[3:4, :], mod_ref[4:5, :], mod_ref[5:6, :]
        g2, gf = vd_ref[1:2, :], vd_ref[2:3, :]
        r2 = lax.rsqrt(jnp.mean(xt * xt, axis=-1, keepdims=True) + EPS)
        n2 = xt * r2
        h2b = (n2 * g2 * (1.0 + scale2) + shift2).astype(BF16)
        h2_ref[...] = h2b
        for j in range(N_CHIP):
            rz_v[j] = jnp.maximum(_dot(h2b, w1_v[j]), 0.0)
        mo = jnp.zeros((ts, D_MODEL), F32)
        for j in range(N_CHIP):
            rz = rz_v[j]
            actb = (rz * rz).astype(BF16)
            act_ref[:, j * FF_BLK:(j + 1) * FF_BLK] = actb
            mo = mo + _dot(actb, w2_v[j])
        x2 = xt + gate2 * mo
        r3 = lax.rsqrt(jnp.mean(x2 * x2, axis=-1, keepdims=True) + EPS)
        n3 = x2 * r3
        e = n3 * gf - tg_ref[...]
        loss = (0.5 / D_MODEL) * jnp.sum(_colsum(e * e), axis=1, keepdims=True)
        dy = e * (1.0 / D_MODEL)
        acc_ref[4:5, :] += _colsum(dy * n3)
        acc_ref[5:6, :] += jnp.broadcast_to(loss, (1, D_MODEL))
        dn3 = dy * gf
        dx2 = r3 * (dn3 - n3 * jnp.mean(dn3 * n3, axis=-1, keepdims=True))
        acc_ref[2:3, :] += _colsum(dx2 * mo)
        dmob = (dx2 * gate2).astype(BF16)
        dmo_ref[...] = dmob
        for j in range(N_CHIP):
            dz_ref[:, j * FF_BLK:(j + 1) * FF_BLK] = (_dot_nt(dmob, w2_v[j]) * (2.0 * rz_v[j])).astype(BF16)
        dh2 = jnp.zeros((ts, D_MODEL), F32)
        for j in range(N_CHIP):
            dh2 = dh2 + _dot_nt(dz_ref[:, j * FF_BLK:(j + 1) * FF_BLK], w1_v[j])
        acc_ref[1:2, :] += _colsum(dh2 * (n2 * g2))
        acc_ref[0:1, :] += _colsum(dh2)
        dhn2 = dh2 * (1.0 + scale2)
        acc_ref[3:4, :] += _colsum(dhn2 * n2)
        dn2 = dhn2 * g2
        dx1_ref[...] = dx2 + r2 * (dn2 - n2 * jnp.mean(dn2 * n2, axis=-1, keepdims=True))

    tile = lambda w: pl.BlockSpec((ts, w), lambda i: (i, 0))
    return pl.pallas_call(
        body, name="mlp_fwd_bwd", grid=(nt,),
        in_specs=[SMEM, tile(D_MODEL), tile(D_MODEL), _full((8, D_MODEL)), _full((8, D_MODEL)), ANY, ANY, ANY, ANY],
        out_specs=[tile(D_MODEL), tile(D_FF), tile(D_FF), tile(D_MODEL), tile(D_MODEL), _full((8, D_MODEL))],
        out_shape=[jax.ShapeDtypeStruct((s, D_MODEL), F32), jax.ShapeDtypeStruct((s, D_FF), BF16),
                   jax.ShapeDtypeStruct((s, D_FF), BF16), jax.ShapeDtypeStruct((s, D_MODEL), BF16),
                   jax.ShapeDtypeStruct((s, D_MODEL), BF16), jax.ShapeDtypeStruct((8, D_MODEL), F32)],
        scratch_shapes=[pltpu.VMEM((N_CHIP, D_MODEL, FF_BLK), BF16), pltpu.VMEM((N_CHIP, FF_BLK, D_MODEL), BF16),
                        pltpu.VMEM((N_CHIP, ts, FF_BLK), F32), pltpu.SemaphoreType.DMA((2 * N_CHIP,))],
        compiler_params=pltpu.CompilerParams(dimension_semantics=("arbitrary",), vmem_limit_bytes=VMEM_LIMIT),
    )(chip, x1, target, mod, vecd, *w1, *w2)


def _mix_bwd(chip, dx1, x, mixed, proj, hl, hb, ycat, mod, vecd, vecl, win, wout, gab, a64, mlp_parts):
    s = x.shape[0]
    ts = TOKEN_TILE
    nt = s // ts
    hpt = ts // HALO

    def body(chip_ref, dx1_ref, x_ref, mixed_ref, proj_ref, projh_ref, hl_ref, hlh_ref, hb_ref, ycat_ref,
             mod_ref, vd_ref, vl_ref, win_hbm, win_own, wout_hbm, wout_own, gab_ref, a64_ref, p1_ref, p2_ref,
             gx_ref, accd_ref, accl_ref, gwin_hbm, gwout_hbm, ggate_hbm, q1_ref, q2_ref,
             win_ref, wout_ref, dproj_ref, dgb_ref, gwin_acc, gwout_acc, ggate_acc,
             ext_lx, ext_cv, ext_hl, ext_dxl, ext_dq, gbuf, gcar, acar, sems, x_send, x_recv):
        i = pl.program_id(0)
        ri = nt - 1 - i

        @pl.when(i == 0)
        def _():
            for cp in _xchg_copies((p1_ref, p2_ref), (q1_ref, q2_ref), x_send, x_recv):
                cp.start()
            gwin_acc[...] = jnp.zeros_like(gwin_acc)
            gwout_acc[...] = jnp.zeros_like(gwout_acc)
            ggate_acc[...] = jnp.zeros_like(ggate_acc)
            cps = _load_gathered(chip_ref[0], win_hbm, win_own, lambda j: win_ref.at[j], sems.at[pl.ds(0, N_CHIP)])
            cps += _load_gathered(chip_ref[0], wout_hbm, wout_own,
                                  lambda j: wout_ref.at[pl.ds(j * WOUT_BLK, WOUT_BLK), :],
                                  sems.at[pl.ds(N_CHIP, N_CHIP)])
            for cp in cps:
                cp.wait()
            accd_ref[...] = jnp.zeros_like(accd_ref)
            accl_ref[...] = jnp.zeros_like(accl_ref)
            ext_dxl[ts:ts + HALO, :] = jnp.zeros((HALO, D_LRU), F32)
            ext_dq[ts:ts + HALO, :] = jnp.zeros((HALO, D_LRU), F32)
            gcar[...] = jnp.zeros_like(gcar)
            acar[...] = jnp.zeros_like(acar)

        row = lax.broadcasted_iota(jnp.int32, (ts, D_LRU), 0)
        first_row = jnp.logical_and(row == 0, ri == 0)
        halo_on = jnp.where(ri == 0, 0.0, 1.0)
        shift1, scale1, gate1 = mod_ref[0:1, :], mod_ref[1:2, :], mod_ref[2:3, :]
        g1 = vd_ref[0:1, :]
        a64m = a64_ref[...]
        lg, cg = vl_ref[9:10, :], vl_ref[10:11, :]

        dx1 = dx1_ref[...]
        accd_ref[2:3, :] += _colsum(dx1 * mixed_ref[...])
        dmb = (dx1 * gate1).astype(BF16)
        gwout_acc[...] += _dot_tn(ycat_ref[...], dmb)
        dycat = _dot_nt(dmb, wout_ref[...])
        dyl = dycat[:, 0:512]
        dyv = dycat[:, 512:1024]

        u_ly = proj_ref[:, 512:1024]
        u_b = proj_ref[:, 1024:1536]
        u_c = proj_ref[:, 1536:2048]
        u_v = proj_ref[:, 2048:2560]
        ext_lx[0:HALO, :] = projh_ref[:, 0:512] * halo_on
        ext_lx[HALO:HALO + ts, :] = proj_ref[:, 0:512]
        xl = vl_ref[4:5, :] + vl_ref[0:1, :] * ext_lx[pl.ds(5, ts), :]
        for k in range(1, 4):
            xl = xl + vl_ref[k:k + 1, :] * ext_lx[pl.ds(5 + k, ts), :]
        xlb = xl.astype(BF16)
        sp = _softplus(vl_ref[8:9, :])
        r, ig, a, msq, mult = _lru_gates(xlb, gab_ref[...], vd_ref[3:4, :], sp, first_row)
        hl = hl_ref[...]
        ge, th = _gelu(u_ly)
        p = ge * hl
        rl = lax.rsqrt(_gmean(p * p, a64m) + EPS)
        nl = p * rl
        ext_cv[0:HALO, :] = projh_ref[:, 1536:2048] * projh_ref[:, 2048:2560] * halo_on
        ext_cv[HALO:HALO + ts, :] = u_c * u_v
        q = vl_ref[5:6, :] * ext_cv[pl.ds(6, ts), :]
        for k in range(1, 3):
            q = q + vl_ref[5 + k:6 + k, :] * ext_cv[pl.ds(6 + k, ts), :]
        yc = u_b * q
        rc = lax.rsqrt(_gmean(yc * yc, a64m) + EPS)
        nc = yc * rc

        accl_ref[9:10, :] += _colsum(dyl * nl)
        dnl = dyl * lg
        dp = rl * (dnl - nl * _gmean(dnl * nl, a64m))
        dproj_ref[:, 512:1024] = ((dp * hl) * _gelu_grad(u_ly, th)).astype(BF16)
        a_next = jnp.where(row == ts - 1, acar[0:1, :], pltpu.roll(a, ts - 1, 0))
        acum, gloc = _scan_rev(a_next, dp * ge, row)
        gbuf[...] = gloc + acum * gcar[0:1, :]
        gcar[0:1, :] = gbuf[0:1, :]
        ext_hl[0:HALO, :] = hlh_ref[...] * halo_on
        ext_hl[HALO:HALO + ts, :] = hl
        acar[...] = a[0:HALO, :]
        gt = gbuf[...]
        da = gt * ext_hl[pl.ds(HALO - 1, ts), :]
        dmult = gt * ig * xl
        di = gt * mult * xl
        dxl = gt * mult * ig
        dla = da * a - jnp.where(first_row, 0.0, dmult * a * a / msq)
        accl_ref[8:9, :] += _colsum(dla * ((-C_GATE) * r))
        dra = dla * ((-C_GATE) * sp) * r * (1.0 - r)
        dia = di * ig * (1.0 - ig)
        accd_ref[4:5, 0:D_LRU] += _colsum(dra)
        accd_ref[4:5, D_LRU:2 * D_LRU] += _colsum(dia)
        dgb_ref[:, 0:D_LRU] = dra.astype(BF16)
        dgb_ref[:, D_LRU:2 * D_LRU] = dia.astype(BF16)
        dxl = dxl + _dot_nt(dgb_ref[...], gab_ref[...])
        ggate_acc[...] += _dot_tn(xlb, dgb_ref[...])
        accl_ref[4:5, :] += _colsum(dxl)
        for k in range(4):
            accl_ref[k:k + 1, :] += _colsum(dxl * ext_lx[pl.ds(5 + k, ts), :])
        ext_dxl[0:ts, :] = dxl
        du_lx = vl_ref[0:1, :] * ext_dxl[pl.ds(3, ts), :]
        for k in range(1, 4):
            du_lx = du_lx + vl_ref[k:k + 1, :] * ext_dxl[pl.ds(3 - k, ts), :]
        ext_dxl[ts:ts + HALO, :] = ext_dxl[0:HALO, :]
        dproj_ref[:, 0:512] = du_lx.astype(BF16)

        accl_ref[10:11, :] += _colsum(dyv * nc)
        dnc = dyv * cg
        dyc = rc * (dnc - nc * _gmean(dnc * nc, a64m))
        dproj_ref[:, 1024:1536] = (dyc * q).astype(BF16)
        dq = dyc * u_b
        for k in range(3):
            accl_ref[5 + k:6 + k, :] += _colsum(dq * ext_cv[pl.ds(6 + k, ts), :])
        ext_dq[0:ts, :] = dq
        dcv = vl_ref[5:6, :] * ext_dq[pl.ds(2, ts), :]
        for k in range(1, 3):
            dcv = dcv + vl_ref[5 + k:6 + k, :] * ext_dq[pl.ds(2 - k, ts), :]
        ext_dq[ts:ts + HALO, :] = ext_dq[0:HALO, :]
        dproj_ref[:, 1536:2048] = (dcv * u_v).astype(BF16)
        dproj_ref[:, 2048:2560] = (dcv * u_c).astype(BF16)

        dh = _dot_nt(dproj_ref[:, 0:WIN_BLK], win_ref[0])
        for j in range(1, N_CHIP):
            dh = dh + _dot_nt(dproj_ref[:, j * WIN_BLK:(j + 1) * WIN_BLK], win_ref[j])
        for j in range(N_CHIP):
            gwin_acc[j] += _dot_tn(hb_ref[...], dproj_ref[:, j * WIN_BLK:(j + 1) * WIN_BLK])
        xt = x_ref[...]
        r1 = lax.rsqrt(jnp.mean(xt * xt, axis=-1, keepdims=True) + EPS)
        n1 = xt * r1
        accd_ref[1:2, :] += _colsum(dh * (n1 * g1))
        accd_ref[0:1, :] += _colsum(dh)
        dhn1 = dh * (1.0 + scale1)
        accd_ref[3:4, :] += _colsum(dhn1 * n1)
        dn1 = dhn1 * g1
        gx_ref[...] = dx1 + r1 * (dn1 - n1 * jnp.mean(dn1 * n1, axis=-1, keepdims=True))

        @pl.when(i == nt - 1)
        def _():
            outs = [pltpu.make_async_copy(acc, dst, sems.at[k]) for k, (acc, dst) in enumerate(
                ((gwin_acc, gwin_hbm), (gwout_acc, gwout_hbm), (ggate_acc, ggate_hbm)))]
            for cp in outs:
                cp.start()
            for cp in _xchg_copies((p1_ref, p2_ref), (q1_ref, q2_ref), x_send, x_recv):
                cp.wait()
            for cp in outs:
                cp.wait()

    tile = lambda w: pl.BlockSpec((ts, w), lambda i: (nt - 1 - i, 0))
    halo = lambda w: pl.BlockSpec((HALO, w), lambda i: (jnp.maximum((nt - 1 - i) * hpt - 1, 0), 0))
    ext = pltpu.VMEM((ts + HALO, D_LRU), F32)
    return pl.pallas_call(
        body, name="mix_bwd", grid=(nt,),
        in_specs=[SMEM, tile(D_MODEL), tile(D_MODEL), tile(D_MODEL), tile(D_IN), halo(D_IN), tile(D_LRU), halo(D_LRU),
                  tile(D_MODEL), tile(D_MODEL), _full((8, D_MODEL)), _full((8, D_MODEL)), _full((16, D_LRU)),
                  ANY, ANY, ANY, ANY, _full((D_LRU, 2 * D_LRU), True), _full((D_LRU, D_LRU), True), ANY, ANY],
        out_specs=[tile(D_MODEL), _full((8, D_MODEL)), _full((16, D_LRU)), ANY, ANY, ANY, ANY, ANY],
        out_shape=[jax.ShapeDtypeStruct((s, D_MODEL), F32),
                   jax.ShapeDtypeStruct((8, D_MODEL), F32), jax.ShapeDtypeStruct((16, D_LRU), F32),
                   jax.ShapeDtypeStruct((N_CHIP, D_MODEL, WIN_BLK), F32), jax.ShapeDtypeStruct((D_MODEL, D_MODEL), F32),
                   jax.ShapeDtypeStruct((D_LRU, 2 * D_LRU), F32)]
        + [jax.ShapeDtypeStruct((3,) + p.shape[1:], p.dtype) for p in mlp_parts],
        scratch_shapes=[pltpu.VMEM((N_CHIP, D_MODEL, WIN_BLK), BF16), pltpu.VMEM((D_MODEL, D_MODEL), BF16),
                        pltpu.VMEM((ts, D_IN), BF16), pltpu.VMEM((ts, 2 * D_LRU), BF16),
                        pltpu.VMEM((N_CHIP, D_MODEL, WIN_BLK), F32), pltpu.VMEM((D_MODEL, D_MODEL), F32),
                        pltpu.VMEM((D_LRU, 2 * D_LRU), F32),
                        ext, ext, ext, ext, ext, pltpu.VMEM((ts, D_LRU), F32),
                        pltpu.VMEM((HALO, D_LRU), F32), pltpu.VMEM((HALO, D_LRU), F32),
                        pltpu.SemaphoreType.DMA((2 * N_CHIP,)),
                        pltpu.SemaphoreType.DMA((6,)), pltpu.SemaphoreType.DMA((6,))],
        compiler_params=pltpu.CompilerParams(dimension_semantics=("arbitrary",), vmem_limit_bytes=VMEM_LIMIT),
    )(chip, dx1, x, mixed, proj, proj, hl, hl, hb, ycat, mod, vecd, vecl, *win, *wout, gab, a64, *mlp_parts)


def _wgrad(name, a, b, a_blk, b_blk, out_dtype):
    s = a.shape[0]
    aw = a_blk or a.shape[1]
    bw = b_blk or b.shape[1]
    nblk = N_CHIP if (a_blk or b_blk) else 1

    def body(a_ref, b_ref, o_ref):
        o_ref[0] = _dot_tn(a_ref[...], b_ref[...]).astype(out_dtype)

    return pl.pallas_call(
        body, name=name, grid=(nblk,),
        in_specs=[pl.BlockSpec((s, aw), (lambda j: (0, j)) if a_blk else (lambda j: (0, 0))),
                  pl.BlockSpec((s, bw), (lambda j: (0, j)) if b_blk else (lambda j: (0, 0)))],
        out_specs=pl.BlockSpec((1, aw, bw), lambda j: (j, 0, 0)),
        out_shape=jax.ShapeDtypeStruct((nblk, aw, bw), out_dtype),
        compiler_params=pltpu.CompilerParams(dimension_semantics=("arbitrary",), vmem_limit_bytes=VMEM_LIMIT),
    )(a, b)


def _mod_matmul(c_all, ada_w_loc):
    n = ada_w_loc.shape[1]
    cb = 512

    def body(c_ref, w_ref, o_ref):
        c = c_ref[...]
        sc = c * jax.nn.sigmoid(c)
        o_ref[...] = _dot(sc.astype(BF16), w_ref[...].astype(BF16))

    return pl.pallas_call(
        body, name="mod_matmul", grid=(n // cb,),
        in_specs=[_full((8, D_MODEL)), pl.BlockSpec((D_MODEL, cb), lambda j: (0, j))],
        out_specs=pl.BlockSpec((8, cb), lambda j: (0, j)),
        out_shape=jax.ShapeDtypeStruct((8, n), F32),
        compiler_params=pltpu.CompilerParams(dimension_semantics=("arbitrary",), vmem_limit_bytes=VMEM_LIMIT),
    )(c_all, ada_w_loc)


def _adam_math(w, g, m, v):
    m = ADAM_B1 * m + (1.0 - ADAM_B1) * g
    v = ADAM_B2 * v + (1.0 - ADAM_B2) * (g * g)
    m_hat = m / (1.0 - ADAM_B1 ** ADAM_STEP)
    v_hat = v / (1.0 - ADAM_B2 ** ADAM_STEP)
    delta = (-ADAM_LR) * (m_hat / (jnp.sqrt(v_hat) + ADAM_EPS) + ADAM_WD * w)
    return delta, m, v


def _adam(name, core, shards):
    n = len(shards)
    r, c = shards[0][0].shape
    half = r // 2
    rb = min(half, 128)
    nh = half // rb

    def body(core_ref, *refs):
        ins, outs = refs[:5 * n], refs[5 * n:]
        mine = (pl.program_id(0) // nh) == core_ref[0]
        for k in range(n):
            w_ref, go_ref, gs_ref, m_ref, v_ref = ins[5 * k:5 * k + 5]
            g_ref, d_ref, mo_ref, vo_ref = outs[4 * k:4 * k + 4]
            g = jnp.where(mine, go_ref[...], gs_ref[...])
            g_ref[...] = g
            d_ref[...], mo_ref[...], vo_ref[...] = _adam_math(w_ref[...], g, m_ref[...], v_ref[...])

    spec = pl.BlockSpec((rb, c), lambda i, core_ref: (i, 0))
    hspec = pl.BlockSpec((rb, c), lambda i, core_ref: (i % nh, 0))
    sds = jax.ShapeDtypeStruct((r, c), F32)
    res = pl.pallas_call(
        body, name=name,
        grid_spec=pltpu.PrefetchScalarGridSpec(
            num_scalar_prefetch=1, grid=(r // rb,),
            in_specs=[spec, hspec, hspec, spec, spec] * n, out_specs=[spec] * (4 * n)),
        out_shape=[sds] * (4 * n),
        compiler_params=pltpu.CompilerParams(dimension_semantics=("arbitrary",), vmem_limit_bytes=VMEM_LIMIT),
    )(core, *[t for s in shards for t in s])
    return [res[4 * k:4 * k + 4] for k in range(n)]


def _ada_grad_adam(sct, dmod_loc, w, m, v):
    r, c = w.shape
    rb = 128

    def body(s_ref, dm_ref, w_ref, m_ref, v_ref, g_ref, d_ref, mo_ref, vo_ref):
        g = s_ref[:, 0:1] * dm_ref[0:1, :]
        for b in range(1, 8):
            g = g + s_ref[:, b:b + 1] * dm_ref[b:b + 1, :]
        g_ref[...] = g
        d_ref[...], mo_ref[...], vo_ref[...] = _adam_math(w_ref[...], g, m_ref[...], v_ref[...])

    spec = pl.BlockSpec((rb, c), lambda i: (i, 0))
    sds = jax.ShapeDtypeStruct((r, c), F32)
    return pl.pallas_call(
        body, name="ada_grad_adam", grid=(r // rb,),
        in_specs=[pl.BlockSpec((rb, 8), lambda i: (i, 0)), _full((8, c)), spec, spec, spec],
        out_specs=[spec] * 4, out_shape=[sds] * 4,
        compiler_params=pltpu.CompilerParams(dimension_semantics=("arbitrary",), vmem_limit_bytes=VMEM_LIMIT),
    )(sct, dmod_loc, w, m, v)


def _position():
    x, y, c = lax.axis_index("x"), lax.axis_index("y"), lax.axis_index("c")
    chips = [(1 - x, y), (x, 1 - y), (1 - x, 1 - y)]
    return x, y, c, chips


def _allgather8(name, arrs, shards=()):
    na, ns = len(arrs), len(shards)

    def body(*refs):
        ins, w_own = refs[:na], refs[na:na + ns]
        outs, w_all = refs[na + ns:2 * na + ns], refs[2 * na + ns:2 * (na + ns)]
        send_sems, recv_sems, local_sems = refs[2 * (na + ns):2 * (na + ns) + 3]
        if ns:
            _ag_start(w_own, w_all, *refs[-2:])
        x, y, c, chips = _position()
        me, sibling = (x, y, c), (x, y, 1 - c)
        first, passed, local = [], [], []
        for a in range(na):
            m_per = ins[a].shape[0]

            def rows(px, py, pc, a=a, m_per=m_per):
                return outs[a].at[pl.ds((4 * px + 2 * py + pc) * m_per, m_per), :]

            def copy(k, block, to, src=None, a=a, rows=rows):
                return pltpu.make_async_remote_copy(
                    src_ref=rows(*block) if src is None else src, dst_ref=rows(*block),
                    send_sem=send_sems.at[7 * a + k], recv_sem=recv_sems.at[7 * a + k],
                    device_id=to, device_id_type=MESH)

            mine = pltpu.make_async_copy(ins[a], rows(*me), local_sems.at[a])
            mine.start()
            local.append(mine)
            f = [copy(0, me, sibling, src=ins[a])]
            f += [copy(1 + j, me, (*chip, c), src=ins[a]) for j, chip in enumerate(chips)]
            for cp in f:
                cp.start()
            first.append((f, copy))
        for a in range(na):
            f, copy = first[a]
            p = [copy(4 + j, (*chip, c), sibling) for j, chip in enumerate(chips)]
            for j, chip in enumerate(chips):
                copy(1 + j, (*chip, c), me).wait_recv()
                p[j].start()
            passed.append(p)
        for a in range(na):
            f, copy = first[a]
            copy(0, sibling, me).wait_recv()
            for j, chip in enumerate(chips):
                copy(4 + j, (*chip, 1 - c), me).wait_recv()
            for cp in f + passed[a]:
                cp.wait_send()
            local[a].wait()
        if ns:
            _ag_finish(w_own, w_all, *refs[-2:])

    return pl.pallas_call(
        body, name=name,
        out_shape=[jax.ShapeDtypeStruct((8 * a.shape[0], a.shape[1]), a.dtype) for a in arrs]
        + [jax.ShapeDtypeStruct((N_CHIP,) + s.shape, s.dtype) for s in shards],
        in_specs=[VMEM] * na + [ANY] * ns, out_specs=[VMEM] * na + [ANY] * ns,
        scratch_shapes=[pltpu.SemaphoreType.DMA((7 * na,)), pltpu.SemaphoreType.DMA((7 * na,)),
                        pltpu.SemaphoreType.DMA((na,))]
        + [pltpu.SemaphoreType.DMA((AG_SEMS * ns,))] * (2 if ns else 0),
        compiler_params=pltpu.CompilerParams(vmem_limit_bytes=VMEM_LIMIT),
    )(*arrs, *shards)


AG_SEMS = 7


def _ag_copies(ins, outs, send_sems, recv_sems):
    x, y, c, chips = _position()
    sibling = (x, y, 1 - c)
    xn, yn, dg = [2 * chip[0] + chip[1] for chip in chips]
    to_x, to_y = (1 - x, y, c), (x, 1 - y, c)
    res = []
    for a in range(len(ins)):
        half = ins[a].shape[0] // 2
        quarter = half // 2

        def copy(k, dst, to, src=None, a=a):
            return pltpu.make_async_remote_copy(
                src_ref=dst if src is None else src, dst_ref=dst,
                send_sem=send_sems.at[AG_SEMS * a + k], recv_sem=recv_sems.at[AG_SEMS * a + k],
                device_id=to, device_id_type=MESH)

        def rows(chip, pc, q=None, a=a, half=half, quarter=quarter):
            if q is None:
                return outs[a].at[chip, pl.ds(pc * half, half), :]
            return outs[a].at[chip, pl.ds(pc * half + q * quarter, quarter), :]

        own = ins[a].at[pl.ds(c * half, half), :]
        mine = rows(2 * x + y, c)
        res.append(dict(
            sends=[copy(0, mine, to_x, src=own), copy(1, mine, to_y, src=own)],
            from_x=copy(0, rows(xn, c), to_x), from_y=copy(1, rows(yn, c), to_y),
            relay_y=copy(2, rows(xn, c, 0), to_y), relay_x=copy(3, rows(yn, c, 1), to_x),
            from_y_relay=copy(2, rows(dg, c, 0), to_y), from_x_relay=copy(3, rows(dg, c, 1), to_x),
            pass_on=[copy(4, rows(xn, c), sibling), copy(5, rows(yn, c), sibling), copy(6, rows(dg, c), sibling)],
            from_sibling=[copy(4, rows(xn, 1 - c), sibling), copy(5, rows(yn, 1 - c), sibling),
                          copy(6, rows(dg, 1 - c), sibling)]))
    return res


def _ag_start(ins, outs, send_sems, recv_sems):
    for cps in _ag_copies(ins, outs, send_sems, recv_sems):
        for cp in cps["sends"]:
            cp.start()


def _ag_relay(ins, outs, send_sems, recv_sems, which):
    copies = _ag_copies(ins, outs, send_sems, recv_sems)
    for a in which:
        cps = copies[a]
        cps["from_x"].wait_recv()
        cps["relay_y"].start()
        cps["pass_on"][0].start()
        cps["from_y"].wait_recv()
        cps["relay_x"].start()
        cps["pass_on"][1].start()


def _ag_complete(ins, outs, send_sems, recv_sems):
    copies = _ag_copies(ins, outs, send_sems, recv_sems)
    for cps in copies:
        cps["from_y_relay"].wait_recv()
        cps["from_x_relay"].wait_recv()
        cps["pass_on"][2].start()
    for cps in copies:
        for cp in cps["from_sibling"]:
            cp.wait_recv()
        for cp in cps["sends"] + [cps["relay_y"], cps["relay_x"]] + cps["pass_on"]:
            cp.wait_send()


def _ag_finish(ins, outs, send_sems, recv_sems):
    _ag_relay(ins, outs, send_sems, recv_sems, range(len(ins)))
    _ag_complete(ins, outs, send_sems, recv_sems)


def _swap_halves(name, grads):
    na = len(grads)

    def body(*refs):
        ins, outs = refs[:na], refs[na:2 * na]
        send_sems, recv_sems = refs[2 * na:]
        x, y, c, _ = _position()
        cps = []
        for a in range(na):
            half = ins[a].shape[1] // 2
            cp = pltpu.make_async_remote_copy(
                src_ref=ins[a].at[:, pl.ds((1 - c) * half, half), :], dst_ref=outs[a],
                send_sem=send_sems.at[a], recv_sem=recv_sems.at[a],
                device_id=(x, y, 1 - c), device_id_type=MESH)
            cp.start()
            cps.append(cp)
        for cp in cps:
            cp.wait()

    return pl.pallas_call(
        body, name=name,
        out_shape=[jax.ShapeDtypeStruct((g.shape[0], g.shape[1] // 2, g.shape[2]), g.dtype) for g in grads],
        in_specs=[ANY] * na, out_specs=[ANY] * na,
        scratch_shapes=[pltpu.SemaphoreType.DMA((na,)), pltpu.SemaphoreType.DMA((na,))],
    )(*grads)


def _xchg_copies(ins, outs, send_sems, recv_sems):
    x, y, c, chips = _position()
    return [pltpu.make_async_remote_copy(
        src_ref=ins[a].at[2 * chip[0] + chip[1]], dst_ref=outs[a].at[j],
        send_sem=send_sems.at[3 * a + j], recv_sem=recv_sems.at[3 * a + j],
        device_id=(*chip, c), device_id_type=MESH) for a in range(len(ins)) for j, chip in enumerate(chips)]


def _exchange_chips(parts):
    na = len(parts)

    def body(*refs):
        ins, outs = refs[:na], refs[na:2 * na]
        send_sems, recv_sems = refs[2 * na:]
        for cp in _xchg_copies(ins, outs, send_sems, recv_sems):
            cp.start()
        for cp in _xchg_copies(ins, outs, send_sems, recv_sems):
            cp.wait()

    return pl.pallas_call(
        body, name="rs_exchange_chips",
        out_shape=[jax.ShapeDtypeStruct((3,) + p.shape[1:], p.dtype) for p in parts],
        in_specs=[ANY] * na, out_specs=[ANY] * na,
        scratch_shapes=[pltpu.SemaphoreType.DMA((3 * na,)), pltpu.SemaphoreType.DMA((3 * na,))],
    )(*parts)


def _swap_reduced(name, halves):
    na = len(halves)

    def body(*refs):
        ins, outs = refs[:na], refs[na:2 * na]
        send_sems, recv_sems = refs[2 * na:]
        x, y, c, _ = _position()
        cps = []
        for a in range(na):
            cp = pltpu.make_async_remote_copy(
                src_ref=ins[a], dst_ref=outs[a], send_sem=send_sems.at[a], recv_sem=recv_sems.at[a],
                device_id=(x, y, 1 - c), device_id_type=MESH)
            cp.start()
            cps.append(cp)
        for cp in cps:
            cp.wait()

    return pl.pallas_call(
        body, name=name,
        out_shape=[jax.ShapeDtypeStruct(h.shape, h.dtype) for h in halves],
        in_specs=[ANY] * na, out_specs=[ANY] * na,
        scratch_shapes=[pltpu.SemaphoreType.DMA((na,)), pltpu.SemaphoreType.DMA((na,))],
    )(*halves)


def _add_sibling(name, grad, recv, core):
    _, r, c = grad.shape
    half = r // 2
    rb = min(half, 256)
    nrb = half // rb

    def body(core_ref, g_ref, r_ref, o_ref):
        o_ref[...] = (g_ref[...].astype(F32) + r_ref[...].astype(F32)).astype(BF16)

    return pl.pallas_call(
        body, name=name,
        grid_spec=pltpu.PrefetchScalarGridSpec(
            num_scalar_prefetch=1, grid=(N_CHIP, nrb),
            in_specs=[pl.BlockSpec((1, rb, c), lambda j, i, core_ref: (j, core_ref[0] * nrb + i, 0)),
                      pl.BlockSpec((1, rb, c), lambda j, i, core_ref: (j, i, 0))],
            out_specs=pl.BlockSpec((1, rb, c), lambda j, i, core_ref: (j, i, 0))),
        out_shape=jax.ShapeDtypeStruct((N_CHIP, half, c), BF16),
        compiler_params=pltpu.CompilerParams(dimension_semantics=("arbitrary", "arbitrary"),
                                             vmem_limit_bytes=VMEM_LIMIT),
    )(core, grad, recv)


def _add_chips(name, chip, p, q):
    _, half, c = q.shape
    rb = min(half, 256)

    def body(chip_ref, p_ref, q_ref, o_ref):
        acc = p_ref[0].astype(F32)
        for j in range(3):
            acc = acc + q_ref[j].astype(F32)
        o_ref[...] = acc

    return pl.pallas_call(
        body, name=name,
        grid_spec=pltpu.PrefetchScalarGridSpec(
            num_scalar_prefetch=1, grid=(half // rb,),
            in_specs=[pl.BlockSpec((1, rb, c), lambda i, chip_ref: (chip_ref[0], i, 0)),
                      pl.BlockSpec((3, rb, c), lambda i, chip_ref: (0, i, 0))],
            out_specs=pl.BlockSpec((rb, c), lambda i, chip_ref: (i, 0))),
        out_shape=jax.ShapeDtypeStruct((half, c), F32),
        compiler_params=pltpu.CompilerParams(dimension_semantics=("arbitrary",), vmem_limit_bytes=VMEM_LIMIT),
    )(chip, p, q)


def _small_update(gad, gam, gl, gg, mychip, params):
    names = ["ada_b", "norm1_g", "lru_conv_b", "gate_a_w", "gate_a_b", "gate_x_w", "gate_x_b", "a_param",
             "lru_conv_w", "short_conv_w", "lru_out_g", "conv_out_g", "norm2_g", "final_g"]
    flat = [t for n in names for t in params[n]]
    nin = len(flat)

    def body(chip_ref, gad_ref, gam_ref, gl_ref, gg_ref, *refs):
        ins = {n: refs[3 * k:3 * k + 3] for k, n in enumerate(names)}
        outs = {n: refs[nin + 4 * k:nin + 4 * k + 4] for k, n in enumerate(names)}
        loss_ref, dmod_ref = refs[nin + 4 * len(names):nin + 4 * len(names) + 2]

        def dsum(ref, lo, n):
            per = ref.shape[0] // 8
            acc = ref[lo:lo + n, :].astype(F32)
            for dev in range(1, 8):
                acc = acc + ref[dev * per + lo:dev * per + lo + n, :].astype(F32)
            return acc

        def update(n, g):
            w_ref, m_ref, v_ref = ins[n]
            g_ref, d_ref, mo_ref, vo_ref = outs[n]
            g_ref[...] = g
            d_ref[...], mo_ref[...], vo_ref[...] = _adam_math(w_ref[...], g, m_ref[...], v_ref[...])

        d, dm, l, lw = refs[-4:]
        d[...] = dsum(gad_ref, 0, 8)
        dm[...] = dsum(gam_ref, 0, 8)
        l[...] = dsum(gl_ref, 0, 16)
        for dev in range(8):
            for k in range(3):
                dmod_ref[dev:dev + 1, k * D_MODEL:(k + 1) * D_MODEL] = gad_ref[dev * 8 + k:dev * 8 + k + 1, :]
                dmod_ref[dev:dev + 1, (3 + k) * D_MODEL:(4 + k) * D_MODEL] = gam_ref[dev * 8 + k:dev * 8 + k + 1, :]
        w_ref, m_ref, v_ref = ins["ada_b"]
        g_ref, d_ref, mo_ref, vo_ref = outs["ada_b"]
        for k in range(3):
            g_ref[:, k * D_MODEL:(k + 1) * D_MODEL] = d[k:k + 1, :]
            g_ref[:, (3 + k) * D_MODEL:(4 + k) * D_MODEL] = dm[k:k + 1, :]
        d_ref[...], mo_ref[...], vo_ref[...] = _adam_math(w_ref[...], g_ref[...], m_ref[...], v_ref[...])
        update("norm1_g", d[3:4, :])
        update("norm2_g", dm[3:4, :])
        update("final_g", dm[4:5, :])
        update("gate_a_b", d[4:5, 0:D_LRU])
        update("gate_x_b", d[4:5, D_LRU:2 * D_LRU])
        update("lru_conv_b", l[4:5, :])
        update("a_param", l[8:9, :] * jax.nn.sigmoid(ins["a_param"][0][...]))
        update("lru_out_g", l[9:10, :])
        update("conv_out_g", l[10:11, :])
        loss_ref[...] = jnp.broadcast_to(dm[5:6, 0:128], (8, 128))
        chip = chip_ref[0]
        acc = jnp.zeros((8, 128), F32)
        for j in range(N_CHIP):
            acc = acc + jnp.where(chip == j, l[0:8, j * 128:(j + 1) * 128], 0.0)
        lw[...] = acc
        update("lru_conv_w", lw[0:4, :])
        update("short_conv_w", lw[5:8, :])
        gates = dsum(gg_ref, 0, D_LRU)
        update("gate_a_w", gates[:, 0:HEAD])
        update("gate_x_w", gates[:, HEAD:2 * HEAD])

    out_shape = []
    for n in names:
        out_shape += [jax.ShapeDtypeStruct(params[n][0].shape, F32)] * 4
    out_shape += [jax.ShapeDtypeStruct((8, 128), F32), jax.ShapeDtypeStruct((8, 6 * D_MODEL), F32)]
    res = pl.pallas_call(
        body, name="small_update", out_shape=out_shape,
        in_specs=[SMEM] + [VMEM] * (4 + nin),
        out_specs=[VMEM] * len(out_shape),
        scratch_shapes=[pltpu.VMEM((8, D_MODEL), F32), pltpu.VMEM((8, D_MODEL), F32), pltpu.VMEM((16, D_LRU), F32),
                        pltpu.VMEM((8, 128), F32)],
        compiler_params=pltpu.CompilerParams(vmem_limit_bytes=VMEM_LIMIT),
    )(mychip, gad, gam, gl, gg, *flat)
    per = {n: res[4 * k:4 * k + 4] for k, n in enumerate(names)}
    return per, res[-2], res[-1]


def _block_diag(w):
    eye = jnp.eye(8, dtype=w.dtype)
    return (eye[:, None, :, None] * w[:, :, None, :]).reshape(8 * HEAD, 8 * HEAD)


def _diag_blocks(g):
    return jnp.concatenate([g[h * HEAD:(h + 1) * HEAD, h * HEAD:(h + 1) * HEAD] for h in range(8)], axis=0)


def kernel(x, c, ada_w, ada_b, norm1_g, w_in, lru_conv_w, lru_conv_b, gate_a_w, gate_a_b, gate_x_w, gate_x_b, a_param, short_conv_w, lru_out_g, conv_out_g, w_out, norm2_g, w_mlp1, w_mlp2, final_g, loss_target, m_ada_w, m_ada_b, m_norm1_g, m_w_in, m_lru_conv_w, m_lru_conv_b, m_gate_a_w, m_gate_a_b, m_gate_x_w, m_gate_x_b, m_a_param, m_short_conv_w, m_lru_out_g, m_conv_out_g, m_w_out, m_norm2_g, m_w_mlp1, m_w_mlp2, m_final_g, v_ada_w, v_ada_b, v_norm1_g, v_w_in, v_lru_conv_w, v_lru_conv_b, v_gate_a_w, v_gate_a_b, v_gate_x_w, v_gate_x_b, v_a_param, v_short_conv_w, v_lru_out_g, v_conv_out_g, v_w_out, v_norm2_g, v_w_mlp1, v_w_mlp2, v_final_g):
    xi, yi, ci = lax.axis_index("x"), lax.axis_index("y"), lax.axis_index("c")
    mychip = 2 * xi + yi
    me = 4 * xi + 2 * yi + ci

    own_in, own_out, own_w1, own_w2 = [w[0].astype(BF16) for w in (w_in, w_out, w_mlp1, w_mlp2)]
    c_blk = jnp.zeros((8, D_MODEL), F32).at[0:1].set(c)
    cw_blk = jnp.zeros((8, 128), F32).at[0:4].set(lru_conv_w[0]).at[4:7].set(short_conv_w[0])
    c_g, cw_g, win_all, wout_all = _allgather8("allgather_cond_weights", [c_blk, cw_blk], [own_in, own_out])
    c_all = c_g.reshape(8, 8, D_MODEL)[:, 0]
    cw_g = cw_g.reshape(4, 2, 8, 128)[:, 0]
    lcw = cw_g[:, 0:4].transpose(1, 0, 2).reshape(4, D_LRU)
    scw = cw_g[:, 4:7].transpose(1, 0, 2).reshape(3, D_LRU)

    mod_loc = _mod_matmul(c_all, ada_w[0])
    (mod_g,) = _allgather8("allgather_mod", [mod_loc])
    mod_all = mod_g.reshape(4, 2, 8, 6 * D_MODEL // 4)[:, 0].transpose(1, 0, 2).reshape(8, 6 * D_MODEL) + ada_b
    mod_pad = jnp.pad(mod_all.reshape(8, 6, D_MODEL), ((0, 0), (0, 2), (0, 0)))
    mod = lax.dynamic_slice_in_dim(mod_pad, me, 1, axis=0).reshape(8, D_MODEL)

    win, wout = (win_all, own_in), (wout_all, own_out)
    chip = mychip.reshape(1).astype(jnp.int32)
    core = ci.reshape(1).astype(jnp.int32)

    vecd = jnp.concatenate([norm1_g, norm2_g, final_g[None, :], jnp.concatenate([gate_a_b, gate_x_b], axis=1),
                            jnp.zeros((4, D_MODEL), F32)], axis=0)
    vecl = jnp.concatenate([lcw, lru_conv_b, scw, a_param, lru_out_g, conv_out_g, jnp.zeros((5, D_LRU), F32)], axis=0)
    gab = jnp.concatenate([_block_diag(gate_a_w[0]), _block_diag(gate_x_w[0])], axis=1).astype(BF16)
    a64 = _block_diag(jnp.full((8, HEAD, HEAD), 1.0 / HEAD, F32)).astype(BF16)

    hb, proj, hl, ycat, mixed, x1, w1_all, w2_all = _mix_fwd(
        chip, x[0], mod, vecd, vecl, win, wout, gab, a64, [own_w1, own_w2])
    dx1, act, dz, dmo, h2b, accm = _mlp_fwd_bwd(
        chip, x1, loss_target[0], mod, vecd, (w1_all, own_w1), (w2_all, own_w2))

    def sibling_sum(tag, grads):
        recv = _swap_halves("rs_swap_halves_" + tag, grads)
        return [_add_sibling("rs_add_sibling_%s%d" % (tag, k), g, r, core) for k, (g, r) in enumerate(zip(grads, recv))]

    parts_mlp = sibling_sum("mlp", [_wgrad("wgrad_mlp1", h2b, dz, 0, FF_BLK, BF16),
                                    _wgrad("wgrad_mlp2", act, dmo, FF_BLK, 0, BF16)])
    grad_x, accd, accl, g_win, g_wout, g_gate, q_w1, q_w2 = _mix_bwd(
        chip, dx1, x[0], mixed, proj, hl, hb, ycat, mod, vecd, vecl, win, wout, gab, a64, parts_mlp)
    parts_mix = sibling_sum("mix", [g_win, g_wout.reshape(N_CHIP, WOUT_BLK, D_MODEL)])

    landed = list(_exchange_chips(parts_mix)) + [q_w1, q_w2]
    g_own = [_add_chips("rs_add_chips_%d" % k, chip, p, q) for k, (p, q) in enumerate(zip(parts_mix + parts_mlp, landed))]
    g_sib = _swap_reduced("rs_swap_reduced", g_own)
    res_w1, res_w2 = _adam("adam_mlp", core, [(w_mlp1[0], g_own[2], g_sib[2], m_w_mlp1[0], v_w_mlp1[0]),
                                              (w_mlp2[0], g_own[3], g_sib[3], m_w_mlp2[0], v_w_mlp2[0])])
    (res_win,) = _adam("adam_w_in", core, [(w_in[0], g_own[0], g_sib[0], m_w_in[0], v_w_in[0])])
    (res_wout,) = _adam("adam_w_out", core, [(w_out[0], g_own[1], g_sib[1], m_w_out[0], v_w_out[0])])

    gg_blk = jnp.concatenate([_diag_blocks(g_gate[:, 0:D_LRU]), _diag_blocks(g_gate[:, D_LRU:2 * D_LRU])], axis=1)
    gad, gam, gl, gg = _allgather8("allgather_small_grads", [accd, accm, accl, gg_blk.astype(BF16)])

    params = {
        "ada_b": (ada_b, m_ada_b, v_ada_b), "norm1_g": (norm1_g, m_norm1_g, v_norm1_g),
        "lru_conv_b": (lru_conv_b, m_lru_conv_b, v_lru_conv_b),
        "gate_a_w": tuple(t.reshape(D_LRU, HEAD) for t in (gate_a_w, m_gate_a_w, v_gate_a_w)),
        "gate_a_b": (gate_a_b, m_gate_a_b, v_gate_a_b),
        "gate_x_w": tuple(t.reshape(D_LRU, HEAD) for t in (gate_x_w, m_gate_x_w, v_gate_x_w)),
        "gate_x_b": (gate_x_b, m_gate_x_b, v_gate_x_b), "a_param": (a_param, m_a_param, v_a_param),
        "lru_conv_w": tuple(t[0] for t in (lru_conv_w, m_lru_conv_w, v_lru_conv_w)),
        "short_conv_w": tuple(t[0] for t in (short_conv_w, m_short_conv_w, v_short_conv_w)),
        "lru_out_g": (lru_out_g, m_lru_out_g, v_lru_out_g), "conv_out_g": (conv_out_g, m_conv_out_g, v_conv_out_g),
        "norm2_g": (norm2_g, m_norm2_g, v_norm2_g),
        "final_g": tuple(t[None, :] for t in (final_g, m_final_g, v_final_g)),
    }
    small, loss_blk, dmod_cols = _small_update(gad, gam, gl, gg, chip, params)
    loss = loss_blk[0, 0]

    ncol = 6 * D_MODEL // N_CHIP
    dmod_loc = lax.dynamic_slice_in_dim(dmod_cols, mychip * ncol, ncol, axis=1)
    sct = (c_all * jax.nn.sigmoid(c_all)).T
    ada = _ada_grad_adam(sct, dmod_loc, ada_w[0], m_ada_w[0], v_ada_w[0])

    res = {"ada_w": ada, "w_in": res_win, "w_out": res_wout, "w_mlp1": res_w1, "w_mlp2": res_w2}
    res = {n: tuple(t[None] for t in r) for n, r in res.items()}
    shapes = {"gate_a_w": gate_a_w.shape, "gate_x_w": gate_x_w.shape, "lru_conv_w": lru_conv_w.shape,
              "short_conv_w": short_conv_w.shape, "final_g": final_g.shape}
    for n, t in small.items():
        res[n] = tuple(u.reshape(shapes[n]) if n in shapes else u for u in t)

    order = ["ada_w", "ada_b", "norm1_g", "w_in", "lru_conv_w", "lru_conv_b", "gate_a_w", "gate_a_b", "gate_x_w",
             "gate_x_b", "a_param", "short_conv_w", "lru_out_g", "conv_out_g", "w_out", "norm2_g", "w_mlp1",
             "w_mlp2", "final_g"]
    return (loss, grad_x[None], *[res[n][0] for n in order], *[res[n][1] for n in order],
            *[res[n][2] for n in order], *[res[n][3] for n in order])
```

```python
import jax
import jax.numpy as jnp
from jax import lax
from jax.experimental import pallas as pl
from jax.experimental.pallas import tpu as pltpu
from jax.experimental.pallas import tpu_sc as plsc

F32 = jnp.float32
BF16 = jnp.bfloat16

D_MODEL = 1024
D_LRU = 512
D_IN = 2560
D_FF = 4096
N_CHIP = 4
WIN_BLK = D_IN // N_CHIP
WOUT_BLK = D_MODEL // N_CHIP
FF_BLK = D_FF // N_CHIP
HEAD = 64
EPS = 1e-6
C_GATE = 8.0
TOKEN_TILE = 256
HALO = 8
VMEM_LIMIT = 60 * 1024 * 1024

ADAM_LR = 0.001
ADAM_B1 = 0.9
ADAM_B2 = 0.999
ADAM_EPS = 1e-08
ADAM_WD = 0.01
ADAM_STEP = 10

MESH = pl.DeviceIdType.MESH
ANY = pl.BlockSpec(memory_space=pl.ANY)
VMEM = pl.BlockSpec(memory_space=pltpu.VMEM)
SMEM = pl.BlockSpec(memory_space=pltpu.SMEM)


def _full(shape, single=False):
    nd = len(shape)
    if single:
        return pl.BlockSpec(shape, lambda *_: (0,) * nd, pipeline_mode=pl.Buffered(1))
    return pl.BlockSpec(shape, lambda *_: (0,) * nd)


def _dot(a, b):
    return jnp.dot(a, b, preferred_element_type=F32)


def _dot_nt(a, b):
    return lax.dot_general(a, b, (((1,), (1,)), ((), ())), preferred_element_type=F32)


def _dot_tn(a, b):
    return lax.dot_general(a, b, (((0,), (0,)), ((), ())), preferred_element_type=F32)


def _gmean(v, a64):
    hi = v.astype(BF16)
    lo = (v - hi.astype(F32)).astype(BF16)
    return _dot(hi, a64) + _dot(lo, a64)


def _gelu(x):
    u = 0.7978845608028654 * (x + 0.044715 * x * x * x)
    t = jnp.tanh(u)
    return 0.5 * x * (1.0 + t), t


def _gelu_grad(x, t):
    du = 0.7978845608028654 * (1.0 + 3.0 * 0.044715 * x * x)
    return 0.5 * (1.0 + t) + 0.5 * x * (1.0 - t * t) * du


def _log1p_pos(y):
    return jnp.where(y < 1e-2, y * (1.0 - y * (0.5 - y * (1.0 / 3.0 - y * 0.25))), jnp.log(1.0 + y))


def _softplus(a):
    return jnp.maximum(a, 0.0) + _log1p_pos(jnp.exp(-jnp.abs(a)))


def _neg_expm1(z):
    series = -z * (1.0 + z * (0.5 + z * (1.0 / 6.0 + z * (1.0 / 24.0 + z * (1.0 / 120.0)))))
    return jnp.where(z > -0.02, series, 1.0 - jnp.exp(z))


def _scan_fwd(a, b, row):
    n = a.shape[0]
    d = 1
    while d < n:
        m = row >= d
        b = jnp.where(m, a * pltpu.roll(b, d, 0) + b, b)
        a = jnp.where(m, a * pltpu.roll(a, d, 0), a)
        d *= 2
    return a, b


def _scan_rev(a, b, row):
    n = a.shape[0]
    d = 1
    while d < n:
        m = row < n - d
        b = jnp.where(m, b + a * pltpu.roll(b, n - d, 0), b)
        a = jnp.where(m, a * pltpu.roll(a, n - d, 0), a)
        d *= 2
    return a, b


def _colsum(v):
    return jnp.sum(v, axis=0, keepdims=True)


def _load_gathered(chip, gathered, own, slot, sems):
    copies = []
    for j in range(N_CHIP):
        @pl.when(chip == j)
        def _(j=j):
            pltpu.make_async_copy(own, slot(j), sems.at[j]).start()

        @pl.when(chip != j)
        def _(j=j):
            pltpu.make_async_copy(gathered.at[j], slot(j), sems.at[j]).start()

        copies.append(pltpu.make_async_copy(own, slot(j), sems.at[j]))
    return copies


def _lru_gates(xlb, gab, gbias, sp, first_row):
    g = _dot(xlb, gab) + gbias
    r = jax.nn.sigmoid(g[:, :D_LRU])
    ig = jax.nn.sigmoid(g[:, D_LRU:])
    la = (-C_GATE) * r * sp
    a = jnp.exp(la)
    msq = jnp.sqrt(_neg_expm1(2.0 * la))
    mult = jnp.where(first_row, 1.0, msq)
    return r, ig, a, msq, mult


def _mix_fwd(chip, x, mod, vecd, vecl, win, wout, gab, a64, mlp_shards):
    s = x.shape[0]
    ts = TOKEN_TILE
    nt = s // ts

    def body(chip_ref, x_ref, mod_ref, vd_ref, vl_ref, win_hbm, win_own, wout_hbm, wout_own, gab_ref, a64_ref,
             w1_own, w2_own,
             hb_ref, proj_ref, hl_ref, ycat_ref, mixed_ref, x1_ref, w1_all, w2_all,
             win_ref, wout_ref, ext_lx, ext_cv, hcar, sems, ag_send, ag_recv):
        i = pl.program_id(0)

        @pl.when(i == 0)
        def _():
            _ag_start((w1_own, w2_own), (w1_all, w2_all), ag_send, ag_recv)
            cps = _load_gathered(chip_ref[0], win_hbm, win_own, lambda j: win_ref.at[j], sems.at[pl.ds(0, N_CHIP)])
            cps += _load_gathered(chip_ref[0], wout_hbm, wout_own,
                                  lambda j: wout_ref.at[pl.ds(j * WOUT_BLK, WOUT_BLK), :],
                                  sems.at[pl.ds(N_CHIP, N_CHIP)])
            ext_lx[0:HALO, :] = jnp.zeros((HALO, D_LRU), F32)
            ext_cv[0:HALO, :] = jnp.zeros((HALO, D_LRU), F32)
            hcar[...] = jnp.zeros_like(hcar)
            for cp in cps:
                cp.wait()

        row = lax.broadcasted_iota(jnp.int32, (ts, D_LRU), 0)
        first_row = jnp.logical_and(row == 0, i == 0)
        xt = x_ref[...]
        shift1, scale1, gate1 = mod_ref[0:1, :], mod_ref[1:2, :], mod_ref[2:3, :]
        r1 = lax.rsqrt(jnp.mean(xt * xt, axis=-1, keepdims=True) + EPS)
        h = (xt * r1) * vd_ref[0:1, :] * (1.0 + scale1) + shift1
        hb = h.astype(BF16)
        hb_ref[...] = hb
        for j in range(N_CHIP):
            proj_ref[:, j * WIN_BLK:(j + 1) * WIN_BLK] = _dot(hb, win_ref[j])
        u_ly = proj_ref[:, 512:1024]
        u_b = proj_ref[:, 1024:1536]

        ext_lx[HALO:HALO + ts, :] = proj_ref[:, 0:512]
        xl = vl_ref[4:5, :] + vl_ref[0:1, :] * ext_lx[pl.ds(5, ts), :]
        for k in range(1, 4):
            xl = xl + vl_ref[k:k + 1, :] * ext_lx[pl.ds(5 + k, ts), :]
        ext_lx[0:HALO, :] = ext_lx[ts:ts + HALO, :]
        sp = _softplus(vl_ref[8:9, :])
        _, ig, a, _, mult = _lru_gates(xl.astype(BF16), gab_ref[...], vd_ref[3:4, :], sp, first_row)
        acum, hloc = _scan_fwd(a, mult * (ig * xl), row)
        hl = hloc + acum * hcar[0:1, :]
        hl_ref[...] = hl
        hcar[0:1, :] = hl_ref[ts - 1:ts, :]
        ge, _ = _gelu(u_ly)
        p = ge * hl
        y_lru = p * lax.rsqrt(_gmean(p * p, a64_ref[...]) + EPS) * vl_ref[9:10, :]
        ycat_ref[:, 0:512] = y_lru.astype(BF16)

        ext_cv[HALO:HALO + ts, :] = proj_ref[:, 1536:2048] * proj_ref[:, 2048:2560]
        q = vl_ref[5:6, :] * ext_cv[pl.ds(6, ts), :]
        for k in range(1, 3):
            q = q + vl_ref[5 + k:6 + k, :] * ext_cv[pl.ds(6 + k, ts), :]
        ext_cv[0:HALO, :] = ext_cv[ts:ts + HALO, :]
        yc = u_b * q
        y_conv = yc * lax.rsqrt(_gmean(yc * yc, a64_ref[...]) + EPS) * vl_ref[10:11, :]
        ycat_ref[:, 512:1024] = y_conv.astype(BF16)

        mixed = _dot(ycat_ref[...], wout_ref[...])
        mixed_ref[...] = mixed
        x1_ref[...] = xt + gate1 * mixed

        @pl.when(i == max(nt - 3, 0))
        def _():
            _ag_relay((w1_own, w2_own), (w1_all, w2_all), ag_send, ag_recv, [0])

        @pl.when(i == nt - 1)
        def _():
            _ag_relay((w1_own, w2_own), (w1_all, w2_all), ag_send, ag_recv, [1])
            _ag_complete((w1_own, w2_own), (w1_all, w2_all), ag_send, ag_recv)

    tile = lambda w: pl.BlockSpec((ts, w), lambda i: (i, 0))
    return pl.pallas_call(
        body, name="mix_fwd", grid=(nt,),
        in_specs=[SMEM, tile(D_MODEL), _full((8, D_MODEL)), _full((8, D_MODEL)), _full((16, D_LRU)),
                  ANY, ANY, ANY, ANY, _full((D_LRU, 2 * D_LRU), True), _full((D_LRU, D_LRU), True), ANY, ANY],
        out_specs=[tile(D_MODEL), tile(D_IN), tile(D_LRU), tile(D_MODEL), tile(D_MODEL), tile(D_MODEL), ANY, ANY],
        out_shape=[jax.ShapeDtypeStruct((s, D_MODEL), BF16), jax.ShapeDtypeStruct((s, D_IN), F32),
                   jax.ShapeDtypeStruct((s, D_LRU), F32), jax.ShapeDtypeStruct((s, D_MODEL), BF16),
                   jax.ShapeDtypeStruct((s, D_MODEL), F32), jax.ShapeDtypeStruct((s, D_MODEL), F32)]
        + [jax.ShapeDtypeStruct((N_CHIP,) + w.shape, w.dtype) for w in mlp_shards],
        scratch_shapes=[pltpu.VMEM((N_CHIP, D_MODEL, WIN_BLK), BF16), pltpu.VMEM((D_MODEL, D_MODEL), BF16),
                        pltpu.VMEM((ts + HALO, D_LRU), F32), pltpu.VMEM((ts + HALO, D_LRU), F32),
                        pltpu.VMEM((HALO, D_LRU), F32), pltpu.SemaphoreType.DMA((2 * N_CHIP,)),
                        pltpu.SemaphoreType.DMA((2 * AG_SEMS,)), pltpu.SemaphoreType.DMA((2 * AG_SEMS,))],
        compiler_params=pltpu.CompilerParams(dimension_semantics=("arbitrary",), vmem_limit_bytes=VMEM_LIMIT),
    )(chip, x, mod, vecd, vecl, *win, *wout, gab, a64, *mlp_shards)


def _mlp_fwd_bwd(chip, x1, target, mod, vecd, w1, w2):
    s = x1.shape[0]
    ts = TOKEN_TILE
    nt = s // ts

    def body(chip_ref, x1_ref, tg_ref, mod_ref, vd_ref, w1_hbm, w1_own, w2_hbm, w2_own,
             dx1_ref, act_ref, dz_ref, dmo_ref, h2_ref, acc_ref, w1_v, w2_v, rz_v, sems):
        i = pl.program_id(0)

        @pl.when(i == 0)
        def _():
            cps = _load_gathered(chip_ref[0], w1_hbm, w1_own, lambda j: w1_v.at[j], sems.at[pl.ds(0, N_CHIP)])
            cps += _load_gathered(chip_ref[0], w2_hbm, w2_own, lambda j: w2_v.at[j], sems.at[pl.ds(N_CHIP, N_CHIP)])
            acc_ref[...] = jnp.zeros_like(acc_ref)
            for cp in cps:
                cp.wait()

        xt = x1_ref[...]
        shift2, scale2, gate2 = mod_ref[3:4, :], mod_ref[4:5, :], mod_ref[5:6, :]
        g2, gf = vd_ref[1:2, :], vd_ref[2:3, :]
        r2 = lax.rsqrt(jnp.mean(xt * xt, axis=-1, keepdims=True) + EPS)
        n2 = xt * r2
        h2b = (n2 * g2 * (1.0 + scale2) + shift2).astype(BF16)
        h2_ref[...] = h2b
        for j in range(N_CHIP):
            rz_v[j] = jnp.maximum(_dot(h2b, w1_v[j]), 0.0)
        mo = jnp.zeros((ts, D_MODEL), F32)
        for j in range(N_CHIP):
            rz = rz_v[j]
            actb = (rz * rz).astype(BF16)
            act_ref[:, j * FF_BLK:(j + 1) * FF_BLK] = actb
            mo = mo + _dot(actb, w2_v[j])
        x2 = xt + gate2 * mo
        r3 = lax.rsqrt(jnp.mean(x2 * x2, axis=-1, keepdims=True) + EPS)
        n3 = x2 * r3
        e = n3 * gf - tg_ref[...]
        loss = (0.5 / D_MODEL) * jnp.sum(_colsum(e * e), axis=1, keepdims=True)
        dy = e * (1.0 / D_MODEL)
        acc_ref[4:5, :] += _colsum(dy * n3)
        acc_ref[5:6, :] += jnp.broadcast_to(loss, (1, D_MODEL))
        dn3 = dy * gf
        dx2 = r3 * (dn3 - n3 * jnp.mean(dn3 * n3, axis=-1, keepdims=True))
        acc_ref[2:3, :] += _colsum(dx2 * mo)
        dmob = (dx2 * gate2).astype(BF16)
        dmo_ref[...] = dmob
        for j in range(N_CHIP):
            dz_ref[:, j * FF_BLK:(j + 1) * FF_BLK] = (_dot_nt(dmob, w2_v[j]) * (2.0 * rz_v[j])).astype(BF16)
        dh2 = jnp.zeros((ts, D_MODEL), F32)
        for j in range(N_CHIP):
            dh2 = dh2 + _dot_nt(dz_ref[:, j * FF_BLK:(j + 1) * FF_BLK], w1_v[j])
        acc_ref[1:2, :] += _colsum(dh2 * (n2 * g2))
        acc_ref[0:1, :] += _colsum(dh2)
        dhn2 = dh2 * (1.0 + scale2)
        acc_ref[3:4, :] += _colsum(dhn2 * n2)
        dn2 = dhn2 * g2
        dx1_ref[...] = dx2 + r2 * (dn2 - n2 * jnp.mean(dn2 * n2, axis=-1, keepdims=True))

    tile = lambda w: pl.BlockSpec((ts, w), lambda i: (i, 0))
    return pl.pallas_call(
        body, name="mlp_fwd_bwd", grid=(nt,),
        in_specs=[SMEM, tile(D_MODEL), tile(D_MODEL), _full((8, D_MODEL)), _full((8, D_MODEL)), ANY, ANY, ANY, ANY],
        out_specs=[tile(D_MODEL), tile(D_FF), tile(D_FF), tile(D_MODEL), tile(D_MODEL), _full((8, D_MODEL))],
        out_shape=[jax.ShapeDtypeStruct((s, D_MODEL), F32), jax.ShapeDtypeStruct((s, D_FF), BF16),
                   jax.ShapeDtypeStruct((s, D_FF), BF16), jax.ShapeDtypeStruct((s, D_MODEL), BF16),
                   jax.ShapeDtypeStruct((s, D_MODEL), BF16), jax.ShapeDtypeStruct((8, D_MODEL), F32)],
        scratch_shapes=[pltpu.VMEM((N_CHIP, D_MODEL, FF_BLK), BF16), pltpu.VMEM((N_CHIP, FF_BLK, D_MODEL), BF16),
                        pltpu.VMEM((N_CHIP, ts, FF_BLK), F32), pltpu.SemaphoreType.DMA((2 * N_CHIP,))],
        compiler_params=pltpu.CompilerParams(dimension_semantics=("arbitrary",), vmem_limit_bytes=VMEM_LIMIT),
    )(chip, x1, target, mod, vecd, *w1, *w2)


def _mix_bwd(chip, dx1, x, mixed, proj, hl, hb, ycat, mod, vecd, vecl, win, wout, gab, a64, mlp_parts):
    s = x.shape[0]
    ts = TOKEN_TILE
    nt = s // ts
    hpt = ts // HALO

    def body(chip_ref, dx1_ref, x_ref, mixed_ref, proj_ref, projh_ref, hl_ref, hlh_ref, hb_ref, ycat_ref,
             mod_ref, vd_ref, vl_ref, win_hbm, win_own, wout_hbm, wout_own, gab_ref, a64_ref, p1_ref, p2_ref,
             gx_ref, accd_ref, accl_ref, gwin_hbm, gwout_hbm, ggate_hbm, q1_ref, q2_ref,
             win_ref, wout_ref, dproj_ref, dgb_ref, gwin_acc, gwout_acc, ggate_acc,
             ext_lx, ext_cv, ext_hl, ext_dxl, ext_dq, gbuf, gcar, acar, sems, x_send, x_recv):
        i = pl.program_id(0)
        ri = nt - 1 - i

        @pl.when(i == 0)
        def _():
            for cp in _xchg_copies((p1_ref, p2_ref), (q1_ref, q2_ref), x_send, x_recv):
                cp.start()
            gwin_acc[...] = jnp.zeros_like(gwin_acc)
            gwout_acc[...] = jnp.zeros_like(gwout_acc)
            ggate_acc[...] = jnp.zeros_like(ggate_acc)
            cps = _load_gathered(chip_ref[0], win_hbm, win_own, lambda j: win_ref.at[j], sems.at[pl.ds(0, N_CHIP)])
            cps += _load_gathered(chip_ref[0], wout_hbm, wout_own,
                                  lambda j: wout_ref.at[pl.ds(j * WOUT_BLK, WOUT_BLK), :],
                                  sems.at[pl.ds(N_CHIP, N_CHIP)])
            for cp in cps:
                cp.wait()
            accd_ref[...] = jnp.zeros_like(accd_ref)
            accl_ref[...] = jnp.zeros_like(accl_ref)
            ext_dxl[ts:ts + HALO, :] = jnp.zeros((HALO, D_LRU), F32)
            ext_dq[ts:ts + HALO, :] = jnp.zeros((HALO, D_LRU), F32)
            gcar[...] = jnp.zeros_like(gcar)
            acar[...] = jnp.zeros_like(acar)

        row = lax.broadcasted_iota(jnp.int32, (ts, D_LRU), 0)
        first_row = jnp.logical_and(row == 0, ri == 0)
        halo_on = jnp.where(ri == 0, 0.0, 1.0)
        shift1, scale1, gate1 = mod_ref[0:1, :], mod_ref[1:2, :], mod_ref[2:3, :]
        g1 = vd_ref[0:1, :]
        a64m = a64_ref[...]
        lg, cg = vl_ref[9:10, :], vl_ref[10:11, :]

        dx1 = dx1_ref[...]
        accd_ref[2:3, :] += _colsum(dx1 * mixed_ref[...])
        dmb = (dx1 * gate1).astype(BF16)
        gwout_acc[...] += _dot_tn(ycat_ref[...], dmb)
        dycat = _dot_nt(dmb, wout_ref[...])
        dyl = dycat[:, 0:512]
        dyv = dycat[:, 512:1024]

        u_ly = proj_ref[:, 512:1024]
        u_b = proj_ref[:, 1024:1536]
        u_c = proj_ref[:, 1536:2048]
        u_v = proj_ref[:, 2048:2560]
        ext_lx[0:HALO, :] = projh_ref[:, 0:512] * halo_on
        ext_lx[HALO:HALO + ts, :] = proj_ref[:, 0:512]
        xl = vl_ref[4:5, :] + vl_ref[0:1, :] * ext_lx[pl.ds(5, ts), :]
        for k in range(1, 4):
            xl = xl + vl_ref[k:k + 1, :] * ext_lx[pl.ds(5 + k, ts), :]
        xlb = xl.astype(BF16)
        sp = _softplus(vl_ref[8:9, :])
        r, ig, a, msq, mult = _lru_gates(xlb, gab_ref[...], vd_ref[3:4, :], sp, first_row)
        hl = hl_ref[...]
        ge, th = _gelu(u_ly)
        p = ge * hl
        rl = lax.rsqrt(_gmean(p * p, a64m) + EPS)
        nl = p * rl
        ext_cv[0:HALO, :] = projh_ref[:, 1536:2048] * projh_ref[:, 2048:2560] * halo_on
        ext_cv[HALO:HALO + ts, :] = u_c * u_v
        q = vl_ref[5:6, :] * ext_cv[pl.ds(6, ts), :]
        for k in range(1, 3):
            q = q + vl_ref[5 + k:6 + k, :] * ext_cv[pl.ds(6 + k, ts), :]
        yc = u_b * q
        rc = lax.rsqrt(_gmean(yc * yc, a64m) + EPS)
        nc = yc * rc

        accl_ref[9:10, :] += _colsum(dyl * nl)
        dnl = dyl * lg
        dp = rl * (dnl - nl * _gmean(dnl * nl, a64m))
        dproj_ref[:, 512:1024] = ((dp * hl) * _gelu_grad(u_ly, th)).astype(BF16)
        a_next = jnp.where(row == ts - 1, acar[0:1, :], pltpu.roll(a, ts - 1, 0))
        acum, gloc = _scan_rev(a_next, dp * ge, row)
        gbuf[...] = gloc + acum * gcar[0:1, :]
        gcar[0:1, :] = gbuf[0:1, :]
        ext_hl[0:HALO, :] = hlh_ref[...] * halo_on
        ext_hl[HALO:HALO + ts, :] = hl
        acar[...] = a[0:HALO, :]
        gt = gbuf[...]
        da = gt * ext_hl[pl.ds(HALO - 1, ts), :]
        dmult = gt * ig * xl
        di = gt * mult * xl
        dxl = gt * mult * ig
        dla = da * a - jnp.where(first_row, 0.0, dmult * a * a / msq)
        accl_ref[8:9, :] += _colsum(dla * ((-C_GATE) * r))
        dra = dla * ((-C_GATE) * sp) * r * (1.0 - r)
        dia = di * ig * (1.0 - ig)
        accd_ref[4:5, 0:D_LRU] += _colsum(dra)
        accd_ref[4:5, D_LRU:2 * D_LRU] += _colsum(dia)
        dgb_ref[:, 0:D_LRU] = dra.astype(BF16)
        dgb_ref[:, D_LRU:2 * D_LRU] = dia.astype(BF16)
        dxl = dxl + _dot_nt(dgb_ref[...], gab_ref[...])
        ggate_acc[...] += _dot_tn(xlb, dgb_ref[...])
        accl_ref[4:5, :] += _colsum(dxl)
        for k in range(4):
            accl_ref[k:k + 1, :] += _colsum(dxl * ext_lx[pl.ds(5 + k, ts), :])
        ext_dxl[0:ts, :] = dxl
        du_lx = vl_ref[0:1, :] * ext_dxl[pl.ds(3, ts), :]
        for k in range(1, 4):
            du_lx = du_lx + vl_ref[k:k + 1, :] * ext_dxl[pl.ds(3 - k, ts), :]
        ext_dxl[ts:ts + HALO, :] = ext_dxl[0:HALO, :]
        dproj_ref[:, 0:512] = du_lx.astype(BF16)

        accl_ref[10:11, :] += _colsum(dyv * nc)
        dnc = dyv * cg
        dyc = rc * (dnc - nc * _gmean(dnc * nc, a64m))
        dproj_ref[:, 1024:1536] = (dyc * q).astype(BF16)
        dq = dyc * u_b
        for k in range(3):
            accl_ref[5 + k:6 + k, :] += _colsum(dq * ext_cv[pl.ds(6 + k, ts), :])
        ext_dq[0:ts, :] = dq
        dcv = vl_ref[5:6, :] * ext_dq[pl.ds(2, ts), :]
        for k in range(1, 3):
            dcv = dcv + vl_ref[5 + k:6 + k, :] * ext_dq[pl.ds(2 - k, ts), :]
        ext_dq[ts:ts + HALO, :] = ext_dq[0:HALO, :]
        dproj_ref[:, 1536:2048] = (dcv * u_v).astype(BF16)
        dproj_ref[:, 2048:2560] = (dcv * u_c).astype(BF16)

        dh = _dot_nt(dproj_ref[:, 0:WIN_BLK], win_ref[0])
        for j in range(1, N_CHIP):
            dh = dh + _dot_nt(dproj_ref[:, j * WIN_BLK:(j + 1) * WIN_BLK], win_ref[j])
        for j in range(N_CHIP):
            gwin_acc[j] += _dot_tn(hb_ref[...], dproj_ref[:, j * WIN_BLK:(j + 1) * WIN_BLK])
        xt = x_ref[...]
        r1 = lax.rsqrt(jnp.mean(xt * xt, axis=-1, keepdims=True) + EPS)
        n1 = xt * r1
        accd_ref[1:2, :] += _colsum(dh * (n1 * g1))
        accd_ref[0:1, :] += _colsum(dh)
        dhn1 = dh * (1.0 + scale1)
        accd_ref[3:4, :] += _colsum(dhn1 * n1)
        dn1 = dhn1 * g1
        gx_ref[...] = dx1 + r1 * (dn1 - n1 * jnp.mean(dn1 * n1, axis=-1, keepdims=True))

        @pl.when(i == nt - 1)
        def _():
            outs = [pltpu.make_async_copy(acc, dst, sems.at[k]) for k, (acc, dst) in enumerate(
                ((gwin_acc, gwin_hbm), (gwout_acc, gwout_hbm), (ggate_acc, ggate_hbm)))]
            for cp in outs:
                cp.start()
            for cp in _xchg_copies((p1_ref, p2_ref), (q1_ref, q2_ref), x_send, x_recv):
                cp.wait()
            for cp in outs:
                cp.wait()

    tile = lambda w: pl.BlockSpec((ts, w), lambda i: (nt - 1 - i, 0))
    halo = lambda w: pl.BlockSpec((HALO, w), lambda i: (jnp.maximum((nt - 1 - i) * hpt - 1, 0), 0))
    ext = pltpu.VMEM((ts + HALO, D_LRU), F32)
    return pl.pallas_call(
        body, name="mix_bwd", grid=(nt,),
        in_specs=[SMEM, tile(D_MODEL), tile(D_MODEL), tile(D_MODEL), tile(D_IN), halo(D_IN), tile(D_LRU), halo(D_LRU),
                  tile(D_MODEL), tile(D_MODEL), _full((8, D_MODEL)), _full((8, D_MODEL)), _full((16, D_LRU)),
                  ANY, ANY, ANY, ANY, _full((D_LRU, 2 * D_LRU), True), _full((D_LRU, D_LRU), True), ANY, ANY],
        out_specs=[tile(D_MODEL), _full((8, D_MODEL)), _full((16, D_LRU)), ANY, ANY, ANY, ANY, ANY],
        out_shape=[jax.ShapeDtypeStruct((s, D_MODEL), F32),
                   jax.ShapeDtypeStruct((8, D_MODEL), F32), jax.ShapeDtypeStruct((16, D_LRU), F32),
                   jax.ShapeDtypeStruct((N_CHIP, D_MODEL, WIN_BLK), F32), jax.ShapeDtypeStruct((D_MODEL, D_MODEL), F32),
                   jax.ShapeDtypeStruct((D_LRU, 2 * D_LRU), F32)]
        + [jax.ShapeDtypeStruct((3,) + p.shape[1:], p.dtype) for p in mlp_parts],
        scratch_shapes=[pltpu.VMEM((N_CHIP, D_MODEL, WIN_BLK), BF16), pltpu.VMEM((D_MODEL, D_MODEL), BF16),
                        pltpu.VMEM((ts, D_IN), BF16), pltpu.VMEM((ts, 2 * D_LRU), BF16),
                        pltpu.VMEM((N_CHIP, D_MODEL, WIN_BLK), F32), pltpu.VMEM((D_MODEL, D_MODEL), F32),
                        pltpu.VMEM((D_LRU, 2 * D_LRU), F32),
                        ext, ext, ext, ext, ext, pltpu.VMEM((ts, D_LRU), F32),
                        pltpu.VMEM((HALO, D_LRU), F32), pltpu.VMEM((HALO, D_LRU), F32),
                        pltpu.SemaphoreType.DMA((2 * N_CHIP,)),
                        pltpu.SemaphoreType.DMA((6,)), pltpu.SemaphoreType.DMA((6,))],
        compiler_params=pltpu.CompilerParams(dimension_semantics=("arbitrary",), vmem_limit_bytes=VMEM_LIMIT),
    )(chip, dx1, x, mixed, proj, proj, hl, hl, hb, ycat, mod, vecd, vecl, *win, *wout, gab, a64, *mlp_parts)


def _wgrad(name, a, b, a_blk, b_blk, out_dtype):
    s = a.shape[0]
    aw = a_blk or a.shape[1]
    bw = b_blk or b.shape[1]
    nblk = N_CHIP if (a_blk or b_blk) else 1

    def body(a_ref, b_ref, o_ref):
        o_ref[0] = _dot_tn(a_ref[...], b_ref[...]).astype(out_dtype)

    return pl.pallas_call(
        body, name=name, grid=(nblk,),
        in_specs=[pl.BlockSpec((s, aw), (lambda j: (0, j)) if a_blk else (lambda j: (0, 0))),
                  pl.BlockSpec((s, bw), (lambda j: (0, j)) if b_blk else (lambda j: (0, 0)))],
        out_specs=pl.BlockSpec((1, aw, bw), lambda j: (j, 0, 0)),
        out_shape=jax.ShapeDtypeStruct((nblk, aw, bw), out_dtype),
        compiler_params=pltpu.CompilerParams(dimension_semantics=("arbitrary",), vmem_limit_bytes=VMEM_LIMIT),
    )(a, b)


def _mod_matmul(c_all, ada_w_loc):
    n = ada_w_loc.shape[1]
    cb = 512

    def body(c_ref, w_ref, o_ref):
        c = c_ref[...]
        sc = c * jax.nn.sigmoid(c)
        o_ref[...] = _dot(sc.astype(BF16), w_ref[...].astype(BF16))

    return pl.pallas_call(
        body, name="mod_matmul", grid=(n // cb,),
        in_specs=[_full((8, D_MODEL)), pl.BlockSpec((D_MODEL, cb), lambda j: (0, j))],
        out_specs=pl.BlockSpec((8, cb), lambda j: (0, j)),
        out_shape=jax.ShapeDtypeStruct((8, n), F32),
        compiler_params=pltpu.CompilerParams(dimension_semantics=("arbitrary",), vmem_limit_bytes=VMEM_LIMIT),
    )(c_all, ada_w_loc)


def _adam_math(w, g, m, v):
    m = ADAM_B1 * m + (1.0 - ADAM_B1) * g
    v = ADAM_B2 * v + (1.0 - ADAM_B2) * (g * g)
    m_hat = m / (1.0 - ADAM_B1 ** ADAM_STEP)
    v_hat = v / (1.0 - ADAM_B2 ** ADAM_STEP)
    delta = (-ADAM_LR) * (m_hat / (jnp.sqrt(v_hat) + ADAM_EPS) + ADAM_WD * w)
    return delta, m, v


def _adam(name, core, shards):
    n = len(shards)
    r, c = shards[0][0].shape
    half = r // 2
    rb = min(half, 128)
    nh = half // rb

    def body(core_ref, *refs):
        ins, outs = refs[:5 * n], refs[5 * n:]
        mine = (pl.program_id(0) // nh) == core_ref[0]
        for k in range(n):
            w_ref, go_ref, gs_ref, m_ref, v_ref = ins[5 * k:5 * k + 5]
            g_ref, d_ref, mo_ref, vo_ref = outs[4 * k:4 * k + 4]
            g = jnp.where(mine, go_ref[...], gs_ref[...])
            g_ref[...] = g
            d_ref[...], mo_ref[...], vo_ref[...] = _adam_math(w_ref[...], g, m_ref[...], v_ref[...])

    spec = pl.BlockSpec((rb, c), lambda i, core_ref: (i, 0))
    hspec = pl.BlockSpec((rb, c), lambda i, core_ref: (i % nh, 0))
    sds = jax.ShapeDtypeStruct((r, c), F32)
    res = pl.pallas_call(
        body, name=name,
        grid_spec=pltpu.PrefetchScalarGridSpec(
            num_scalar_prefetch=1, grid=(r // rb,),
            in_specs=[spec, hspec, hspec, spec, spec] * n, out_specs=[spec] * (4 * n)),
        out_shape=[sds] * (4 * n),
        compiler_params=pltpu.CompilerParams(dimension_semantics=("arbitrary",), vmem_limit_bytes=VMEM_LIMIT),
    )(core, *[t for s in shards for t in s])
    return [res[4 * k:4 * k + 4] for k in range(n)]


def _ada_grad_adam(sct, dmod_loc, w, m, v):
    r, c = w.shape
    rb = 128

    def body(s_ref, dm_ref, w_ref, m_ref, v_ref, g_ref, d_ref, mo_ref, vo_ref):
        g = s_ref[:, 0:1] * dm_ref[0:1, :]
        for b in range(1, 8):
            g = g + s_ref[:, b:b + 1] * dm_ref[b:b + 1, :]
        g_ref[...] = g
        d_ref[...], mo_ref[...], vo_ref[...] = _adam_math(w_ref[...], g, m_ref[...], v_ref[...])

    spec = pl.BlockSpec((rb, c), lambda i: (i, 0))
    sds = jax.ShapeDtypeStruct((r, c), F32)
    return pl.pallas_call(
        body, name="ada_grad_adam", grid=(r // rb,),
        in_specs=[pl.BlockSpec((rb, 8), lambda i: (i, 0)), _full((8, c)), spec, spec, spec],
        out_specs=[spec] * 4, out_shape=[sds] * 4,
        compiler_params=pltpu.CompilerParams(dimension_semantics=("arbitrary",), vmem_limit_bytes=VMEM_LIMIT),
    )(sct, dmod_loc, w, m, v)


def _position():
    x, y, c = lax.axis_index("x"), lax.axis_index("y"), lax.axis_index("c")
    chips = [(1 - x, y), (x, 1 - y), (1 - x, 1 - y)]
    return x, y, c, chips


def _allgather8(name, arrs, shards=()):
    na, ns = len(arrs), len(shards)

    def body(*refs):
        ins, w_own = refs[:na], refs[na:na + ns]
        outs, w_all = refs[na + ns:2 * na + ns], refs[2 * na + ns:2 * (na + ns)]
        send_sems, recv_sems, local_sems = refs[2 * (na + ns):2 * (na + ns) + 3]
        if ns:
            _ag_start(w_own, w_all, *refs[-2:])
        x, y, c, chips = _position()
        me, sibling = (x, y, c), (x, y, 1 - c)
        first, passed, local = [], [], []
        for a in range(na):
            m_per = ins[a].shape[0]

            def rows(px, py, pc, a=a, m_per=m_per):
                return outs[a].at[pl.ds((4 * px + 2 * py + pc) * m_per, m_per), :]

            def copy(k, block, to, src=None, a=a, rows=rows):
                return pltpu.make_async_remote_copy(
                    src_ref=rows(*block) if src is None else src, dst_ref=rows(*block),
                    send_sem=send_sems.at[7 * a + k], recv_sem=recv_sems.at[7 * a + k],
                    device_id=to, device_id_type=MESH)

            mine = pltpu.make_async_copy(ins[a], rows(*me), local_sems.at[a])
            mine.start()
            local.append(mine)
            f = [copy(0, me, sibling, src=ins[a])]
            f += [copy(1 + j, me, (*chip, c), src=ins[a]) for j, chip in enumerate(chips)]
            for cp in f:
                cp.start()
            first.append((f, copy))
        for a in range(na):
            f, copy = first[a]
            p = [copy(4 + j, (*chip, c), sibling) for j, chip in enumerate(chips)]
            for j, chip in enumerate(chips):
                copy(1 + j, (*chip, c), me).wait_recv()
                p[j].start()
            passed.append(p)
        for a in range(na):
            f, copy = first[a]
            copy(0, sibling, me).wait_recv()
            for j, chip in enumerate(chips):
                copy(4 + j, (*chip, 1 - c), me).wait_recv()
            for cp in f + passed[a]:
                cp.wait_send()
            local[a].wait()
        if ns:
            _ag_finish(w_own, w_all, *refs[-2:])

    return pl.pallas_call(
        body, name=name,
        out_shape=[jax.ShapeDtypeStruct((8 * a.shape[0], a.shape[1]), a.dtype) for a in arrs]
        + [jax.ShapeDtypeStruct((N_CHIP,) + s.shape, s.dtype) for s in shards],
        in_specs=[VMEM] * na + [ANY] * ns, out_specs=[VMEM] * na + [ANY] * ns,
        scratch_shapes=[pltpu.SemaphoreType.DMA((7 * na,)), pltpu.SemaphoreType.DMA((7 * na,)),
                        pltpu.SemaphoreType.DMA((na,))]
        + [pltpu.SemaphoreType.DMA((AG_SEMS * ns,))] * (2 if ns else 0),
        compiler_params=pltpu.CompilerParams(vmem_limit_bytes=VMEM_LIMIT),
    )(*arrs, *shards)


AG_SEMS = 7


def _ag_copies(ins, outs, send_sems, recv_sems):
    x, y, c, chips = _position()
    sibling = (x, y, 1 - c)
    xn, yn, dg = [2 * chip[0] + chip[1] for chip in chips]
    to_x, to_y = (1 - x, y, c), (x, 1 - y, c)
    res = []
    for a in range(len(ins)):
        half = ins[a].shape[0] // 2
        quarter = half // 2

        def copy(k, dst, to, src=None, a=a):
            return pltpu.make_async_remote_copy(
                src_ref=dst if src is None else src, dst_ref=dst,
                send_sem=send_sems.at[AG_SEMS * a + k], recv_sem=recv_sems.at[AG_SEMS * a + k],
                device_id=to, device_id_type=MESH)

        def rows(chip, pc, q=None, a=a, half=half, quarter=quarter):
            if q is None:
                return outs[a].at[chip, pl.ds(pc * half, half), :]
            return outs[a].at[chip, pl.ds(pc * half + q * quarter, quarter), :]

        own = ins[a].at[pl.ds(c * half, half), :]
        mine = rows(2 * x + y, c)
        res.append(dict(
            sends=[copy(0, mine, to_x, src=own), copy(1, mine, to_y, src=own)],
            from_x=copy(0, rows(xn, c), to_x), from_y=copy(1, rows(yn, c), to_y),
            relay_y=copy(2, rows(xn, c, 0), to_y), relay_x=copy(3, rows(yn, c, 1), to_x),
            from_y_relay=copy(2, rows(dg, c, 0), to_y), from_x_relay=copy(3, rows(dg, c, 1), to_x),
            pass_on=[copy(4, rows(xn, c), sibling), copy(5, rows(yn, c), sibling), copy(6, rows(dg, c), sibling)],
            from_sibling=[copy(4, rows(xn, 1 - c), sibling), copy(5, rows(yn, 1 - c), sibling),
                          copy(6, rows(dg, 1 - c), sibling)]))
    return res


def _ag_start(ins, outs, send_sems, recv_sems):
    for cps in _ag_copies(ins, outs, send_sems, recv_sems):
        for cp in cps["sends"]:
            cp.start()


def _ag_relay(ins, outs, send_sems, recv_sems, which):
    copies = _ag_copies(ins, outs, send_sems, recv_sems)
    for a in which:
        cps = copies[a]
        cps["from_x"].wait_recv()
        cps["relay_y"].start()
        cps["pass_on"][0].start()
        cps["from_y"].wait_recv()
        cps["relay_x"].start()
        cps["pass_on"][1].start()


def _ag_complete(ins, outs, send_sems, recv_sems):
    copies = _ag_copies(ins, outs, send_sems, recv_sems)
    for cps in copies:
        cps["from_y_relay"].wait_recv()
        cps["from_x_relay"].wait_recv()
        cps["pass_on"][2].start()
    for cps in copies:
        for cp in cps["from_sibling"]:
            cp.wait_recv()
        for cp in cps["sends"] + [cps["relay_y"], cps["relay_x"]] + cps["pass_on"]:
            cp.wait_send()


def _ag_finish(ins, outs, send_sems, recv_sems):
    _ag_relay(ins, outs, send_sems, recv_sems, range(len(ins)))
    _ag_complete(ins, outs, send_sems, recv_sems)


def _swap_halves(name, grads):
    na = len(grads)

    def body(*refs):
        ins, outs = refs[:na], refs[na:2 * na]
        send_sems, recv_sems = refs[2 * na:]
        x, y, c, _ = _position()
        cps = []
        for a in range(na):
            half = ins[a].shape[1] // 2
            cp = pltpu.make_async_remote_copy(
                src_ref=ins[a].at[:, pl.ds((1 - c) * half, half), :], dst_ref=outs[a],
                send_sem=send_sems.at[a], recv_sem=recv_sems.at[a],
                device_id=(x, y, 1 - c), device_id_type=MESH)
            cp.start()
            cps.append(cp)
        for cp in cps:
            cp.wait()

    return pl.pallas_call(
        body, name=name,
        out_shape=[jax.ShapeDtypeStruct((g.shape[0], g.shape[1] // 2, g.shape[2]), g.dtype) for g in grads],
        in_specs=[ANY] * na, out_specs=[ANY] * na,
        scratch_shapes=[pltpu.SemaphoreType.DMA((na,)), pltpu.SemaphoreType.DMA((na,))],
    )(*grads)


def _xchg_copies(ins, outs, send_sems, recv_sems):
    x, y, c, chips = _position()
    return [pltpu.make_async_remote_copy(
        src_ref=ins[a].at[2 * chip[0] + chip[1]], dst_ref=outs[a].at[j],
        send_sem=send_sems.at[3 * a + j], recv_sem=recv_sems.at[3 * a + j],
        device_id=(*chip, c), device_id_type=MESH) for a in range(len(ins)) for j, chip in enumerate(chips)]


def _exchange_chips(parts):
    na = len(parts)
    hbm = pltpu.MemorySpace.HBM
    ins = [jax.new_ref(p, memory_space=hbm) for p in parts]
    outs = [jax.empty_ref(jax.ShapeDtypeStruct((3,) + p.shape[1:], p.dtype), memory_space=hbm) for p in parts]

    @pl.kernel(mesh=plsc.ScalarSubcoreMesh(axis_name="sequencer", num_cores=1), name="rs_exchange_chips",
               scratch_types=(pltpu.SemaphoreType.DMA((3 * na,)), pltpu.SemaphoreType.DMA((3 * na,))),
               compiler_params=pltpu.CompilerParams(collective_id=0))
    def launch(send_sems, recv_sems):
        x, y, c, chips = _position()
        barrier = pltpu.get_barrier_semaphore()
        for chip in chips:
            pl.semaphore_signal(barrier, inc=1, device_id=(*chip, c), device_id_type=MESH)
        pl.semaphore_wait(barrier, len(chips))
        for cp in _xchg_copies(ins, outs, send_sems, recv_sems):
            cp.start()
        for cp in _xchg_copies(ins, outs, send_sems, recv_sems):
            cp.wait()

    launch()
    return [q[...] for q in outs]


def _swap_reduced(name, halves):
    na = len(halves)

    def body(*refs):
        ins, outs = refs[:na], refs[na:2 * na]
        send_sems, recv_sems = refs[2 * na:]
        x, y, c, _ = _position()
        cps = []
        for a in range(na):
            cp = pltpu.make_async_remote_copy(
                src_ref=ins[a], dst_ref=outs[a], send_sem=send_sems.at[a], recv_sem=recv_sems.at[a],
                device_id=(x, y, 1 - c), device_id_type=MESH)
            cp.start()
            cps.append(cp)
        for cp in cps:
            cp.wait()

    return pl.pallas_call(
        body, name=name,
        out_shape=[jax.ShapeDtypeStruct(h.shape, h.dtype) for h in halves],
        in_specs=[ANY] * na, out_specs=[ANY] * na,
        scratch_shapes=[pltpu.SemaphoreType.DMA((na,)), pltpu.SemaphoreType.DMA((na,))],
    )(*halves)


def _add_sibling(name, grad, recv, core):
    _, r, c = grad.shape
    half = r // 2
    rb = min(half, 256)
    nrb = half // rb

    def body(core_ref, g_ref, r_ref, o_ref):
        o_ref[...] = (g_ref[...].astype(F32) + r_ref[...].astype(F32)).astype(BF16)

    return pl.pallas_call(
        body, name=name,
        grid_spec=pltpu.PrefetchScalarGridSpec(
            num_scalar_prefetch=1, grid=(N_CHIP, nrb),
            in_specs=[pl.BlockSpec((1, rb, c), lambda j, i, core_ref: (j, core_ref[0] * nrb + i, 0)),
                      pl.BlockSpec((1, rb, c), lambda j, i, core_ref: (j, i, 0))],
            out_specs=pl.BlockSpec((1, rb, c), lambda j, i, core_ref: (j, i, 0))),
        out_shape=jax.ShapeDtypeStruct((N_CHIP, half, c), BF16),
        compiler_params=pltpu.CompilerParams(dimension_semantics=("arbitrary", "arbitrary"),
                                             vmem_limit_bytes=VMEM_LIMIT),
    )(core, grad, recv)


def _add_chips(name, chip, p, q):
    _, half, c = q.shape
    rb = min(half, 256)

    def body(chip_ref, p_ref, q_ref, o_ref):
        acc = p_ref[0].astype(F32)
        for j in range(3):
            acc = acc + q_ref[j].astype(F32)
        o_ref[...] = acc

    return pl.pallas_call(
        body, name=name,
        grid_spec=pltpu.PrefetchScalarGridSpec(
            num_scalar_prefetch=1, grid=(half // rb,),
            in_specs=[pl.BlockSpec((1, rb, c), lambda i, chip_ref: (chip_ref[0], i, 0)),
                      pl.BlockSpec((3, rb, c), lambda i, chip_ref: (0, i, 0))],
            out_specs=pl.BlockSpec((rb, c), lambda i, chip_ref: (i, 0))),
        out_shape=jax.ShapeDtypeStruct((half, c), F32),
        compiler_params=pltpu.CompilerParams(dimension_semantics=("arbitrary",), vmem_limit_bytes=VMEM_LIMIT),
    )(chip, p, q)


def _small_update(gad, gam, gl, gg, mychip, params):
    names = ["ada_b", "norm1_g", "lru_conv_b", "gate_a_w", "gate_a_b", "gate_x_w", "gate_x_b", "a_param",
             "lru_conv_w", "short_conv_w", "lru_out_g", "conv_out_g", "norm2_g", "final_g"]
    flat = [t for n in names for t in params[n]]
    nin = len(flat)

    def body(chip_ref, gad_ref, gam_ref, gl_ref, gg_ref, *refs):
        ins = {n: refs[3 * k:3 * k + 3] for k, n in enumerate(names)}
        outs = {n: refs[nin + 4 * k:nin + 4 * k + 4] for k, n in enumerate(names)}
        loss_ref, dmod_ref = refs[nin + 4 * len(names):nin + 4 * len(names) + 2]

        def dsum(ref, lo, n):
            per = ref.shape[0] // 8
            acc = ref[lo:lo + n, :].astype(F32)
            for dev in range(1, 8):
                acc = acc + ref[dev * per + lo:dev * per + lo + n, :].astype(F32)
            return acc

        def update(n, g):
            w_ref, m_ref, v_ref = ins[n]
            g_ref, d_ref, mo_ref, vo_ref = outs[n]
            g_ref[...] = g
            d_ref[...], mo_ref[...], vo_ref[...] = _adam_math(w_ref[...], g, m_ref[...], v_ref[...])

        d, dm, l, lw = refs[-4:]
        d[...] = dsum(gad_ref, 0, 8)
        dm[...] = dsum(gam_ref, 0, 8)
        l[...] = dsum(gl_ref, 0, 16)
        for dev in range(8):
            for k in range(3):
                dmod_ref[dev:dev + 1, k * D_MODEL:(k + 1) * D_MODEL] = gad_ref[dev * 8 + k:dev * 8 + k + 1, :]
                dmod_ref[dev:dev + 1, (3 + k) * D_MODEL:(4 + k) * D_MODEL] = gam_ref[dev * 8 + k:dev * 8 + k + 1, :]
        w_ref, m_ref, v_ref = ins["ada_b"]
        g_ref, d_ref, mo_ref, vo_ref = outs["ada_b"]
        for k in range(3):
            g_ref[:, k * D_MODEL:(k + 1) * D_MODEL] = d[k:k + 1, :]
            g_ref[:, (3 + k) * D_MODEL:(4 + k) * D_MODEL] = dm[k:k + 1, :]
        d_ref[...], mo_ref[...], vo_ref[...] = _adam_math(w_ref[...], g_ref[...], m_ref[...], v_ref[...])
        update("norm1_g", d[3:4, :])
        update("norm2_g", dm[3:4, :])
        update("final_g", dm[4:5, :])
        update("gate_a_b", d[4:5, 0:D_LRU])
        update("gate_x_b", d[4:5, D_LRU:2 * D_LRU])
        update("lru_conv_b", l[4:5, :])
        update("a_param", l[8:9, :] * jax.nn.sigmoid(ins["a_param"][0][...]))
        update("lru_out_g", l[9:10, :])
        update("conv_out_g", l[10:11, :])
        loss_ref[...] = jnp.broadcast_to(dm[5:6, 0:128], (8, 128))
        chip = chip_ref[0]
        acc = jnp.zeros((8, 128), F32)
        for j in range(N_CHIP):
            acc = acc + jnp.where(chip == j, l[0:8, j * 128:(j + 1) * 128], 0.0)
        lw[...] = acc
        update("lru_conv_w", lw[0:4, :])
        update("short_conv_w", lw[5:8, :])
        gates = dsum(gg_ref, 0, D_LRU)
        update("gate_a_w", gates[:, 0:HEAD])
        update("gate_x_w", gates[:, HEAD:2 * HEAD])

    out_shape = []
    for n in names:
        out_shape += [jax.ShapeDtypeStruct(params[n][0].shape, F32)] * 4
    out_shape += [jax.ShapeDtypeStruct((8, 128), F32), jax.ShapeDtypeStruct((8, 6 * D_MODEL), F32)]
    res = pl.pallas_call(
        body, name="small_update", out_shape=out_shape,
        in_specs=[SMEM] + [VMEM] * (4 + nin),
        out_specs=[VMEM] * len(out_shape),
        scratch_shapes=[pltpu.VMEM((8, D_MODEL), F32), pltpu.VMEM((8, D_MODEL), F32), pltpu.VMEM((16, D_LRU), F32),
                        pltpu.VMEM((8, 128), F32)],
        compiler_params=pltpu.CompilerParams(vmem_limit_bytes=VMEM_LIMIT),
    )(mychip, gad, gam, gl, gg, *flat)
    per = {n: res[4 * k:4 * k + 4] for k, n in enumerate(names)}
    return per, res[-2], res[-1]


def _block_diag(w):
    eye = jnp.eye(8, dtype=w.dtype)
    return (eye[:, None, :, None] * w[:, :, None, :]).reshape(8 * HEAD, 8 * HEAD)


def _diag_blocks(g):
    return jnp.concatenate([g[h * HEAD:(h + 1) * HEAD, h * HEAD:(h + 1) * HEAD] for h in range(8)], axis=0)


def kernel(x, c, ada_w, ada_b, norm1_g, w_in, lru_conv_w, lru_conv_b, gate_a_w, gate_a_b, gate_x_w, gate_x_b, a_param, short_conv_w, lru_out_g, conv_out_g, w_out, norm2_g, w_mlp1, w_mlp2, final_g, loss_target, m_ada_w, m_ada_b, m_norm1_g, m_w_in, m_lru_conv_w, m_lru_conv_b, m_gate_a_w, m_gate_a_b, m_gate_x_w, m_gate_x_b, m_a_param, m_short_conv_w, m_lru_out_g, m_conv_out_g, m_w_out, m_norm2_g, m_w_mlp1, m_w_mlp2, m_final_g, v_ada_w, v_ada_b, v_norm1_g, v_w_in, v_lru_conv_w, v_lru_conv_b, v_gate_a_w, v_gate_a_b, v_gate_x_w, v_gate_x_b, v_a_param, v_short_conv_w, v_lru_out_g, v_conv_out_g, v_w_out, v_norm2_g, v_w_mlp1, v_w_mlp2, v_final_g):
    xi, yi, ci = lax.axis_index("x"), lax.axis_index("y"), lax.axis_index("c")
    mychip = 2 * xi + yi
    me = 4 * xi + 2 * yi + ci

    own_in, own_out, own_w1, own_w2 = [w[0].astype(BF16) for w in (w_in, w_out, w_mlp1, w_mlp2)]
    c_blk = jnp.zeros((8, D_MODEL), F32).at[0:1].set(c)
    cw_blk = jnp.zeros((8, 128), F32).at[0:4].set(lru_conv_w[0]).at[4:7].set(short_conv_w[0])
    c_g, cw_g, win_all, wout_all = _allgather8("allgather_cond_weights", [c_blk, cw_blk], [own_in, own_out])
    c_all = c_g.reshape(8, 8, D_MODEL)[:, 0]
    cw_g = cw_g.reshape(4, 2, 8, 128)[:, 0]
    lcw = cw_g[:, 0:4].transpose(1, 0, 2).reshape(4, D_LRU)
    scw = cw_g[:, 4:7].transpose(1, 0, 2).reshape(3, D_LRU)

    mod_loc = _mod_matmul(c_all, ada_w[0])
    (mod_g,) = _allgather8("allgather_mod", [mod_loc])
    mod_all = mod_g.reshape(4, 2, 8, 6 * D_MODEL // 4)[:, 0].transpose(1, 0, 2).reshape(8, 6 * D_MODEL) + ada_b
    mod_pad = jnp.pad(mod_all.reshape(8, 6, D_MODEL), ((0, 0), (0, 2), (0, 0)))
    mod = lax.dynamic_slice_in_dim(mod_pad, me, 1, axis=0).reshape(8, D_MODEL)

    win, wout = (win_all, own_in), (wout_all, own_out)
    chip = mychip.reshape(1).astype(jnp.int32)
    core = ci.reshape(1).astype(jnp.int32)

    vecd = jnp.concatenate([norm1_g, norm2_g, final_g[None, :], jnp.concatenate([gate_a_b, gate_x_b], axis=1),
                            jnp.zeros((4, D_MODEL), F32)], axis=0)
    vecl = jnp.concatenate([lcw, lru_conv_b, scw, a_param, lru_out_g, conv_out_g, jnp.zeros((5, D_LRU), F32)], axis=0)
    gab = jnp.concatenate([_block_diag(gate_a_w[0]), _block_diag(gate_x_w[0])], axis=1).astype(BF16)
    a64 = _block_diag(jnp.full((8, HEAD, HEAD), 1.0 / HEAD, F32)).astype(BF16)

    hb, proj, hl, ycat, mixed, x1, w1_all, w2_all = _mix_fwd(
        chip, x[0], mod, vecd, vecl, win, wout, gab, a64, [own_w1, own_w2])
    dx1, act, dz, dmo, h2b, accm = _mlp_fwd_bwd(
        chip, x1, loss_target[0], mod, vecd, (w1_all, own_w1), (w2_all, own_w2))

    def sibling_sum(tag, grads):
        recv = _swap_halves("rs_swap_halves_" + tag, grads)
        return [_add_sibling("rs_add_sibling_%s%d" % (tag, k), g, r, core) for k, (g, r) in enumerate(zip(grads, recv))]

    parts_mlp = sibling_sum("mlp", [_wgrad("wgrad_mlp1", h2b, dz, 0, FF_BLK, BF16),
                                    _wgrad("wgrad_mlp2", act, dmo, FF_BLK, 0, BF16)])
    grad_x, accd, accl, g_win, g_wout, g_gate, q_w1, q_w2 = _mix_bwd(
        chip, dx1, x[0], mixed, proj, hl, hb, ycat, mod, vecd, vecl, win, wout, gab, a64, parts_mlp)
    parts_mix = sibling_sum("mix", [g_win, g_wout.reshape(N_CHIP, WOUT_BLK, D_MODEL)])

    def reduced(tag, parts, landed):
        own = [_add_chips("rs_add_chips_%s%d" % (tag, k), chip, p, q) for k, (p, q) in enumerate(zip(parts, landed))]
        return own, _swap_reduced("rs_swap_reduced_" + tag, own)

    landed_mix = _exchange_chips(parts_mix)
    own_mlp, sib_mlp = reduced("mlp", parts_mlp, [q_w1, q_w2])
    res_w1, res_w2 = _adam("adam_mlp", core, [(w_mlp1[0], own_mlp[0], sib_mlp[0], m_w_mlp1[0], v_w_mlp1[0]),
                                              (w_mlp2[0], own_mlp[1], sib_mlp[1], m_w_mlp2[0], v_w_mlp2[0])])
    own_mix, sib_mix = reduced("mix", parts_mix, landed_mix)
    (res_win,) = _adam("adam_w_in", core, [(w_in[0], own_mix[0], sib_mix[0], m_w_in[0], v_w_in[0])])
    (res_wout,) = _adam("adam_w_out", core, [(w_out[0], own_mix[1], sib_mix[1], m_w_out[0], v_w_out[0])])

    gg_blk = jnp.concatenate([_diag_blocks(g_gate[:, 0:D_LRU]), _diag_blocks(g_gate[:, D_LRU:2 * D_LRU])], axis=1)
    gad, gam, gl, gg = _allgather8("allgather_small_grads", [accd, accm, accl, gg_blk.astype(BF16)])

    params = {
        "ada_b": (ada_b, m_ada_b, v_ada_b), "norm1_g": (norm1_g, m_norm1_g, v_norm1_g),
        "lru_conv_b": (lru_conv_b, m_lru_conv_b, v_lru_conv_b),
        "gate_a_w": tuple(t.reshape(D_LRU, HEAD) for t in (gate_a_w, m_gate_a_w, v_gate_a_w)),
        "gate_a_b": (gate_a_b, m_gate_a_b, v_gate_a_b),
        "gate_x_w": tuple(t.reshape(D_LRU, HEAD) for t in (gate_x_w, m_gate_x_w, v_gate_x_w)),
        "gate_x_b": (gate_x_b, m_gate_x_b, v_gate_x_b), "a_param": (a_param, m_a_param, v_a_param),
        "lru_conv_w": tuple(t[0] for t in (lru_conv_w, m_lru_conv_w, v_lru_conv_w)),
        "short_conv_w": tuple(t[0] for t in (short_conv_w, m_short_conv_w, v_short_conv_w)),
        "lru_out_g": (lru_out_g, m_lru_out_g, v_lru_out_g), "conv_out_g": (conv_out_g, m_conv_out_g, v_conv_out_g),
        "norm2_g": (norm2_g, m_norm2_g, v_norm2_g),
        "final_g": tuple(t[None, :] for t in (final_g, m_final_g, v_final_g)),
    }
    small, loss_blk, dmod_cols = _small_update(gad, gam, gl, gg, chip, params)
    loss = loss_blk[0, 0]

    ncol = 6 * D_MODEL // N_CHIP
    dmod_loc = lax.dynamic_slice_in_dim(dmod_cols, mychip * ncol, ncol, axis=1)
    sct = (c_all * jax.nn.sigmoid(c_all)).T
    ada = _ada_grad_adam(sct, dmod_loc, ada_w[0], m_ada_w[0], v_ada_w[0])

    res = {"ada_w": ada, "w_in": res_win, "w_out": res_wout, "w_mlp1": res_w1, "w_mlp2": res_w2}
    res = {n: tuple(t[None] for t in r) for n, r in res.items()}
    shapes = {"gate_a_w": gate_a_w.shape, "gate_x_w": gate_x_w.shape, "lru_conv_w": lru_conv_w.shape,
              "short_conv_w": short_conv_w.shape, "final_g": final_g.shape}
    for n, t in small.items():
        res[n] = tuple(u.reshape(shapes[n]) if n in shapes else u for u in t)

    order = ["ada_w", "ada_b", "norm1_g", "w_in", "lru_conv_w", "lru_conv_b", "gate_a_w", "gate_a_b", "gate_x_w",
             "gate_x_b", "a_param", "short_conv_w", "lru_out_g", "conv_out_g", "w_out", "norm2_g", "w_mlp1",
             "w_mlp2", "final_g"]
    return (loss, grad_x[None], *[res[n][0] for n in order], *[res[n][1] for n in order],
            *[res[n][2] for n in order], *[res[n][3] for n in order])
```

```python
import jax
import jax.numpy as jnp
from jax import lax
from jax.experimental import pallas as pl
from jax.experimental.pallas import tpu as pltpu
from jax.experimental.pallas import tpu_sc as plsc

F32 = jnp.float32
BF16 = jnp.bfloat16

D_MODEL = 1024
D_LRU = 512
D_IN = 2560
D_FF = 4096
N_CHIP = 4
WIN_BLK = D_IN // N_CHIP
WOUT_BLK = D_MODEL // N_CHIP
FF_BLK = D_FF // N_CHIP
HEAD = 64
EPS = 1e-6
C_GATE = 8.0
TOKEN_TILE = 256
HALO = 8
VMEM_LIMIT = 60 * 1024 * 1024

ADAM_LR = 0.001
ADAM_B1 = 0.9
ADAM_B2 = 0.999
ADAM_EPS = 1e-08
ADAM_WD = 0.01
ADAM_STEP = 10

MESH = pl.DeviceIdType.MESH
ANY = pl.BlockSpec(memory_space=pl.ANY)
VMEM = pl.BlockSpec(memory_space=pltpu.VMEM)
SMEM = pl.BlockSpec(memory_space=pltpu.SMEM)


def _full(shape, single=False):
    nd = len(shape)
    if single:
        return pl.BlockSpec(shape, lambda *_: (0,) * nd, pipeline_mode=pl.Buffered(1))
    return pl.BlockSpec(shape, lambda *_: (0,) * nd)


def _dot(a, b):
    return jnp.dot(a, b, preferred_element_type=F32)


def _dot_nt(a, b):
    return lax.dot_general(a, b, (((1,), (1,)), ((), ())), preferred_element_type=F32)


def _dot_tn(a, b):
    return lax.dot_general(a, b, (((0,), (0,)), ((), ())), preferred_element_type=F32)


def _gmean(v, a64):
    hi = v.astype(BF16)
    lo = (v - hi.astype(F32)).astype(BF16)
    return _dot(hi, a64) + _dot(lo, a64)


def _gelu(x):
    u = 0.7978845608028654 * (x + 0.044715 * x * x * x)
    t = jnp.tanh(u)
    return 0.5 * x * (1.0 + t), t


def _gelu_grad(x, t):
    du = 0.7978845608028654 * (1.0 + 3.0 * 0.044715 * x * x)
    return 0.5 * (1.0 + t) + 0.5 * x * (1.0 - t * t) * du


def _log1p_pos(y):
    return jnp.where(y < 1e-2, y * (1.0 - y * (0.5 - y * (1.0 / 3.0 - y * 0.25))), jnp.log(1.0 + y))


def _softplus(a):
    return jnp.maximum(a, 0.0) + _log1p_pos(jnp.exp(-jnp.abs(a)))


def _neg_expm1(z):
    series = -z * (1.0 + z * (0.5 + z * (1.0 / 6.0 + z * (1.0 / 24.0 + z * (1.0 / 120.0)))))
    return jnp.where(z > -0.02, series, 1.0 - jnp.exp(z))


def _scan_fwd(a, b, row):
    n = a.shape[0]
    d = 1
    while d < n:
        m = row >= d
        b = jnp.where(m, a * pltpu.roll(b, d, 0) + b, b)
        a = jnp.where(m, a * pltpu.roll(a, d, 0), a)
        d *= 2
    return a, b


def _scan_rev(a, b, row):
    n = a.shape[0]
    d = 1
    while d < n:
        m = row < n - d
        b = jnp.where(m, b + a * pltpu.roll(b, n - d, 0), b)
        a = jnp.where(m, a * pltpu.roll(a, n - d, 0), a)
        d *= 2
    return a, b


def _colsum(v):
    return jnp.sum(v, axis=0, keepdims=True)


def _load_gathered(chip, gathered, own, slot, sems):
    copies = []
    for j in range(N_CHIP):
        @pl.when(chip == j)
        def _(j=j):
            pltpu.make_async_copy(own, slot(j), sems.at[j]).start()

        @pl.when(chip != j)
        def _(j=j):
            pltpu.make_async_copy(gathered.at[j], slot(j), sems.at[j]).start()

        copies.append(pltpu.make_async_copy(own, slot(j), sems.at[j]))
    return copies


def _lru_gates(xlb, gab, gbias, sp, first_row):
    g = _dot(xlb, gab) + gbias
    r = jax.nn.sigmoid(g[:, :D_LRU])
    ig = jax.nn.sigmoid(g[:, D_LRU:])
    la = (-C_GATE) * r * sp
    a = jnp.exp(la)
    msq = jnp.sqrt(_neg_expm1(2.0 * la))
    mult = jnp.where(first_row, 1.0, msq)
    return r, ig, a, msq, mult


def _mix_fwd(chip, x, mod, vecd, vecl, win, wout, gab, a64, mlp_shards):
    s = x.shape[0]
    ts = TOKEN_TILE
    nt = s // ts

    def body(chip_ref, x_ref, mod_ref, vd_ref, vl_ref, win_hbm, win_own, wout_hbm, wout_own, gab_ref, a64_ref,
             w1_own, w2_own,
             hb_ref, proj_ref, hl_ref, ycat_ref, mixed_ref, x1_ref, w1_all, w2_all,
             win_ref, wout_ref, ext_lx, ext_cv, hcar, sems, ag_send, ag_recv):
        i = pl.program_id(0)

        @pl.when(i == 0)
        def _():
            _ag_start((w1_own, w2_own), (w1_all, w2_all), ag_send, ag_recv)
            cps = _load_gathered(chip_ref[0], win_hbm, win_own, lambda j: win_ref.at[j], sems.at[pl.ds(0, N_CHIP)])
            cps += _load_gathered(chip_ref[0], wout_hbm, wout_own,
                                  lambda j: wout_ref.at[pl.ds(j * WOUT_BLK, WOUT_BLK), :],
                                  sems.at[pl.ds(N_CHIP, N_CHIP)])
            ext_lx[0:HALO, :] = jnp.zeros((HALO, D_LRU), F32)
            ext_cv[0:HALO, :] = jnp.zeros((HALO, D_LRU), F32)
            hcar[...] = jnp.zeros_like(hcar)
            for cp in cps:
                cp.wait()

        row = lax.broadcasted_iota(jnp.int32, (ts, D_LRU), 0)
        first_row = jnp.logical_and(row == 0, i == 0)
        xt = x_ref[...]
        shift1, scale1, gate1 = mod_ref[0:1, :], mod_ref[1:2, :], mod_ref[2:3, :]
        r1 = lax.rsqrt(jnp.mean(xt * xt, axis=-1, keepdims=True) + EPS)
        h = (xt * r1) * vd_ref[0:1, :] * (1.0 + scale1) + shift1
        hb = h.astype(BF16)
        hb_ref[...] = hb
        for j in range(N_CHIP):
            proj_ref[:, j * WIN_BLK:(j + 1) * WIN_BLK] = _dot(hb, win_ref[j])
        u_ly = proj_ref[:, 512:1024]
        u_b = proj_ref[:, 1024:1536]

        ext_lx[HALO:HALO + ts, :] = proj_ref[:, 0:512]
        xl = vl_ref[4:5, :] + vl_ref[0:1, :] * ext_lx[pl.ds(5, ts), :]
        for k in range(1, 4):
            xl = xl + vl_ref[k:k + 1, :] * ext_lx[pl.ds(5 + k, ts), :]
        ext_lx[0:HALO, :] = ext_lx[ts:ts + HALO, :]
        sp = _softplus(vl_ref[8:9, :])
        _, ig, a, _, mult = _lru_gates(xl.astype(BF16), gab_ref[...], vd_ref[3:4, :], sp, first_row)
        acum, hloc = _scan_fwd(a, mult * (ig * xl), row)
        hl = hloc + acum * hcar[0:1, :]
        hl_ref[...] = hl
        hcar[0:1, :] = hl_ref[ts - 1:ts, :]
        ge, _ = _gelu(u_ly)
        p = ge * hl
        y_lru = p * lax.rsqrt(_gmean(p * p, a64_ref[...]) + EPS) * vl_ref[9:10, :]
        ycat_ref[:, 0:512] = y_lru.astype(BF16)

        ext_cv[HALO:HALO + ts, :] = proj_ref[:, 1536:2048] * proj_ref[:, 2048:2560]
        q = vl_ref[5:6, :] * ext_cv[pl.ds(6, ts), :]
        for k in range(1, 3):
            q = q + vl_ref[5 + k:6 + k, :] * ext_cv[pl.ds(6 + k, ts), :]
        ext_cv[0:HALO, :] = ext_cv[ts:ts + HALO, :]
        yc = u_b * q
        y_conv = yc * lax.rsqrt(_gmean(yc * yc, a64_ref[...]) + EPS) * vl_ref[10:11, :]
        ycat_ref[:, 512:1024] = y_conv.astype(BF16)

        mixed = _dot(ycat_ref[...], wout_ref[...])
        mixed_ref[...] = mixed
        x1_ref[...] = xt + gate1 * mixed

        @pl.when(i == max(nt - 3, 0))
        def _():
            _ag_relay((w1_own, w2_own), (w1_all, w2_all), ag_send, ag_recv, [0])

        @pl.when(i == nt - 1)
        def _():
            _ag_relay((w1_own, w2_own), (w1_all, w2_all), ag_send, ag_recv, [1])
            _ag_complete((w1_own, w2_own), (w1_all, w2_all), ag_send, ag_recv)

    tile = lambda w: pl.BlockSpec((ts, w), lambda i: (i, 0))
    return pl.pallas_call(
        body, name="mix_fwd", grid=(nt,),
        in_specs=[SMEM, tile(D_MODEL), _full((8, D_MODEL)), _full((8, D_MODEL)), _full((16, D_LRU)),
                  ANY, ANY, ANY, ANY, _full((D_LRU, 2 * D_LRU), True), _full((D_LRU, D_LRU), True), ANY, ANY],
        out_specs=[tile(D_MODEL), tile(D_IN), tile(D_LRU), tile(D_MODEL), tile(D_MODEL), tile(D_MODEL), ANY, ANY],
        out_shape=[jax.ShapeDtypeStruct((s, D_MODEL), BF16), jax.ShapeDtypeStruct((s, D_IN), F32),
                   jax.ShapeDtypeStruct((s, D_LRU), F32), jax.ShapeDtypeStruct((s, D_MODEL), BF16),
                   jax.ShapeDtypeStruct((s, D_MODEL), F32), jax.ShapeDtypeStruct((s, D_MODEL), F32)]
        + [jax.ShapeDtypeStruct((N_CHIP,) + w.shape, w.dtype) for w in mlp_shards],
        scratch_shapes=[pltpu.VMEM((N_CHIP, D_MODEL, WIN_BLK), BF16), pltpu.VMEM((D_MODEL, D_MODEL), BF16),
                        pltpu.VMEM((ts + HALO, D_LRU), F32), pltpu.VMEM((ts + HALO, D_LRU), F32),
                        pltpu.VMEM((HALO, D_LRU), F32), pltpu.SemaphoreType.DMA((2 * N_CHIP,)),
                        pltpu.SemaphoreType.DMA((2 * AG_SEMS,)), pltpu.SemaphoreType.DMA((2 * AG_SEMS,))],
        compiler_params=pltpu.CompilerParams(dimension_semantics=("arbitrary",), vmem_limit_bytes=VMEM_LIMIT),
    )(chip, x, mod, vecd, vecl, *win, *wout, gab, a64, *mlp_shards)


def _mlp_fwd_bwd(chip, x1, target, mod, vecd, w1, w2):
    s = x1.shape[0]
    ts = TOKEN_TILE
    nt = s // ts

    def body(chip_ref, x1_ref, tg_ref, mod_ref, vd_ref, w1_hbm, w1_own, w2_hbm, w2_own,
             dx1_ref, act_ref, dz_ref, dmo_ref, h2_ref, acc_ref, w1_v, w2_v, rz_v, sems):
        i = pl.program_id(0)

        @pl.when(i == 0)
        def _():
            cps = _load_gathered(chip_ref[0], w1_hbm, w1_own, lambda j: w1_v.at[j], sems.at[pl.ds(0, N_CHIP)])
            cps += _load_gathered(chip_ref[0], w2_hbm, w2_own, lambda j: w2_v.at[j], sems.at[pl.ds(N_CHIP, N_CHIP)])
            acc_ref[...] = jnp.zeros_like(acc_ref)
            for cp in cps:
                cp.wait()

        xt = x1_ref[...]
        shift2, scale2, gate2 = mod_ref[3:4, :], mod_ref[4:5, :], mod_ref[5:6, :]
        g2, gf = vd_ref[1:2, :], vd_ref[2:3, :]
        r2 = lax.rsqrt(jnp.mean(xt * xt, axis=-1, keepdims=True) + EPS)
        n2 = xt * r2
        h2b = (n2 * g2 * (1.0 + scale2) + shift2).astype(BF16)
        h2_ref[...] = h2b
        for j in range(N_CHIP):
            rz_v[j] = jnp.maximum(_dot(h2b, w1_v[j]), 0.0)
        mo = jnp.zeros((ts, D_MODEL), F32)
        for j in range(N_CHIP):
            rz = rz_v[j]
            actb = (rz * rz).astype(BF16)
            act_ref[:, j * FF_BLK:(j + 1) * FF_BLK] = actb
            mo = mo + _dot(actb, w2_v[j])
        x2 = xt + gate2 * mo
        r3 = lax.rsqrt(jnp.mean(x2 * x2, axis=-1, keepdims=True) + EPS)
        n3 = x2 * r3
        e = n3 * gf - tg_ref[...]
        loss = (0.5 / D_MODEL) * jnp.sum(_colsum(e * e), axis=1, keepdims=True)
        dy = e * (1.0 / D_MODEL)
        acc_ref[4:5, :] += _colsum(dy * n3)
        acc_ref[5:6, :] += jnp.broadcast_to(loss, (1, D_MODEL))
        dn3 = dy * gf
        dx2 = r3 * (dn3 - n3 * jnp.mean(dn3 * n3, axis=-1, keepdims=True))
        acc_ref[2:3, :] += _colsum(dx2 * mo)
        dmob = (dx2 * gate2).astype(BF16)
        dmo_ref[...] = dmob
        for j in range(N_CHIP):
            dz_ref[:, j * FF_BLK:(j + 1) * FF_BLK] = (_dot_nt(dmob, w2_v[j]) * (2.0 * rz_v[j])).astype(BF16)
        dh2 = jnp.zeros((ts, D_MODEL), F32)
        for j in range(N_CHIP):
            dh2 = dh2 + _dot_nt(dz_ref[:, j * FF_BLK:(j + 1) * FF_BLK], w1_v[j])
        acc_ref[1:2, :] += _colsum(dh2 * (n2 * g2))
        acc_ref[0:1, :] += _colsum(dh2)
        dhn2 = dh2 * (1.0 + scale2)
        acc_ref[3:4, :] += _colsum(dhn2 * n2)
        dn2 = dhn2 * g2
        dx1_ref[...] = dx2 + r2 * (dn2 - n2 * jnp.mean(dn2 * n2, axis=-1, keepdims=True))

    tile = lambda w: pl.BlockSpec((ts, w), lambda i: (i, 0))
    return pl.pallas_call(
        body, name="mlp_fwd_bwd", grid=(nt,),
        in_specs=[SMEM, tile(D_MODEL), tile(D_MODEL), _full((8, D_MODEL)), _full((8, D_MODEL)), ANY, ANY, ANY, ANY],
        out_specs=[tile(D_MODEL), tile(D_FF), tile(D_FF), tile(D_MODEL), tile(D_MODEL), _full((8, D_MODEL))],
        out_shape=[jax.ShapeDtypeStruct((s, D_MODEL), F32), jax.ShapeDtypeStruct((s, D_FF), BF16),
                   jax.ShapeDtypeStruct((s, D_FF), BF16), jax.ShapeDtypeStruct((s, D_MODEL), BF16),
                   jax.ShapeDtypeStruct((s, D_MODEL), BF16), jax.ShapeDtypeStruct((8, D_MODEL), F32)],
        scratch_shapes=[pltpu.VMEM((N_CHIP, D_MODEL, FF_BLK), BF16), pltpu.VMEM((N_CHIP, FF_BLK, D_MODEL), BF16),
                        pltpu.VMEM((N_CHIP, ts, FF_BLK), F32), pltpu.SemaphoreType.DMA((2 * N_CHIP,))],
        compiler_params=pltpu.CompilerParams(dimension_semantics=("arbitrary",), vmem_limit_bytes=VMEM_LIMIT),
    )(chip, x1, target, mod, vecd, *w1, *w2)


def _mix_bwd(chip, dx1, x, mixed, proj, hl, hb, ycat, mod, vecd, vecl, win, wout, gab, a64, mlp_parts):
    s = x.shape[0]
    ts = TOKEN_TILE
    nt = s // ts
    hpt = ts // HALO

    def body(chip_ref, dx1_ref, x_ref, mixed_ref, proj_ref, projh_ref, hl_ref, hlh_ref, hb_ref, ycat_ref,
             mod_ref, vd_ref, vl_ref, win_hbm, win_own, wout_hbm, wout_own, gab_ref, a64_ref, p1_ref, p2_ref,
             gx_ref, accd_ref, accl_ref, gwin_hbm, gwout_hbm, ggate_hbm, q1_ref, q2_ref,
             win_ref, wout_ref, dproj_ref, dgb_ref, gwin_acc, gwout_acc, ggate_acc,
             ext_lx, ext_cv, ext_hl, ext_dxl, ext_dq, gbuf, gcar, acar, sems, x_send, x_recv):
        i = pl.program_id(0)
        ri = nt - 1 - i

        @pl.when(i == 0)
        def _():
            for cp in _xchg_copies((p1_ref, p2_ref), (q1_ref, q2_ref), x_send, x_recv):
                cp.start()
            gwin_acc[...] = jnp.zeros_like(gwin_acc)
            gwout_acc[...] = jnp.zeros_like(gwout_acc)
            ggate_acc[...] = jnp.zeros_like(ggate_acc)
            cps = _load_gathered(chip_ref[0], win_hbm, win_own, lambda j: win_ref.at[j], sems.at[pl.ds(0, N_CHIP)])
            cps += _load_gathered(chip_ref[0], wout_hbm, wout_own,
                                  lambda j: wout_ref.at[pl.ds(j * WOUT_BLK, WOUT_BLK), :],
                                  sems.at[pl.ds(N_CHIP, N_CHIP)])
            for cp in cps:
                cp.wait()
            accd_ref[...] = jnp.zeros_like(accd_ref)
            accl_ref[...] = jnp.zeros_like(accl_ref)
            ext_dxl[ts:ts + HALO, :] = jnp.zeros((HALO, D_LRU), F32)
            ext_dq[ts:ts + HALO, :] = jnp.zeros((HALO, D_LRU), F32)
            gcar[...] = jnp.zeros_like(gcar)
            acar[...] = jnp.zeros_like(acar)

        row = lax.broadcasted_iota(jnp.int32, (ts, D_LRU), 0)
        first_row = jnp.logical_and(row == 0, ri == 0)
        halo_on = jnp.where(ri == 0, 0.0, 1.0)
        shift1, scale1, gate1 = mod_ref[0:1, :], mod_ref[1:2, :], mod_ref[2:3, :]
        g1 = vd_ref[0:1, :]
        a64m = a64_ref[...]
        lg, cg = vl_ref[9:10, :], vl_ref[10:11, :]

        dx1 = dx1_ref[...]
        accd_ref[2:3, :] += _colsum(dx1 * mixed_ref[...])
        dmb = (dx1 * gate1).astype(BF16)
        gwout_acc[...] += _dot_tn(ycat_ref[...], dmb)
        dycat = _dot_nt(dmb, wout_ref[...])
        dyl = dycat[:, 0:512]
        dyv = dycat[:, 512:1024]

        u_ly = proj_ref[:, 512:1024]
        u_b = proj_ref[:, 1024:1536]
        u_c = proj_ref[:, 1536:2048]
        u_v = proj_ref[:, 2048:2560]
        ext_lx[0:HALO, :] = projh_ref[:, 0:512] * halo_on
        ext_lx[HALO:HALO + ts, :] = proj_ref[:, 0:512]
        xl = vl_ref[4:5, :] + vl_ref[0:1, :] * ext_lx[pl.ds(5, ts), :]
        for k in range(1, 4):
            xl = xl + vl_ref[k:k + 1, :] * ext_lx[pl.ds(5 + k, ts), :]
        xlb = xl.astype(BF16)
        sp = _softplus(vl_ref[8:9, :])
        r, ig, a, msq, mult = _lru_gates(xlb, gab_ref[...], vd_ref[3:4, :], sp, first_row)
        hl = hl_ref[...]
        ge, th = _gelu(u_ly)
        p = ge * hl
        rl = lax.rsqrt(_gmean(p * p, a64m) + EPS)
        nl = p * rl
        ext_cv[0:HALO, :] = projh_ref[:, 1536:2048] * projh_ref[:, 2048:2560] * halo_on
        ext_cv[HALO:HALO + ts, :] = u_c * u_v
        q = vl_ref[5:6, :] * ext_cv[pl.ds(6, ts), :]
        for k in range(1, 3):
            q = q + vl_ref[5 + k:6 + k, :] * ext_cv[pl.ds(6 + k, ts), :]
        yc = u_b * q
        rc = lax.rsqrt(_gmean(yc * yc, a64m) + EPS)
        nc = yc * rc

        accl_ref[9:10, :] += _colsum(dyl * nl)
        dnl = dyl * lg
        dp = rl * (dnl - nl * _gmean(dnl * nl, a64m))
        dproj_ref[:, 512:1024] = ((dp * hl) * _gelu_grad(u_ly, th)).astype(BF16)
        a_next = jnp.where(row == ts - 1, acar[0:1, :], pltpu.roll(a, ts - 1, 0))
        acum, gloc = _scan_rev(a_next, dp * ge, row)
        gbuf[...] = gloc + acum * gcar[0:1, :]
        gcar[0:1, :] = gbuf[0:1, :]
        ext_hl[0:HALO, :] = hlh_ref[...] * halo_on
        ext_hl[HALO:HALO + ts, :] = hl
        acar[...] = a[0:HALO, :]
        gt = gbuf[...]
        da = gt * ext_hl[pl.ds(HALO - 1, ts), :]
        dmult = gt * ig * xl
        di = gt * mult * xl
        dxl = gt * mult * ig
        dla = da * a - jnp.where(first_row, 0.0, dmult * a * a / msq)
        accl_ref[8:9, :] += _colsum(dla * ((-C_GATE) * r))
        dra = dla * ((-C_GATE) * sp) * r * (1.0 - r)
        dia = di * ig * (1.0 - ig)
        accd_ref[4:5, 0:D_LRU] += _colsum(dra)
        accd_ref[4:5, D_LRU:2 * D_LRU] += _colsum(dia)
        dgb_ref[:, 0:D_LRU] = dra.astype(BF16)
        dgb_ref[:, D_LRU:2 * D_LRU] = dia.astype(BF16)
        dxl = dxl + _dot_nt(dgb_ref[...], gab_ref[...])
        ggate_acc[...] += _dot_tn(xlb, dgb_ref[...])
        accl_ref[4:5, :] += _colsum(dxl)
        for k in range(4):
            accl_ref[k:k + 1, :] += _colsum(dxl * ext_lx[pl.ds(5 + k, ts), :])
        ext_dxl[0:ts, :] = dxl
        du_lx = vl_ref[0:1, :] * ext_dxl[pl.ds(3, ts), :]
        for k in range(1, 4):
            du_lx = du_lx + vl_ref[k:k + 1, :] * ext_dxl[pl.ds(3 - k, ts), :]
        ext_dxl[ts:ts + HALO, :] = ext_dxl[0:HALO, :]
        dproj_ref[:, 0:512] = du_lx.astype(BF16)

        accl_ref[10:11, :] += _colsum(dyv * nc)
        dnc = dyv * cg
        dyc = rc * (dnc - nc * _gmean(dnc * nc, a64m))
        dproj_ref[:, 1024:1536] = (dyc * q).astype(BF16)
        dq = dyc * u_b
        for k in range(3):
            accl_ref[5 + k:6 + k, :] += _colsum(dq * ext_cv[pl.ds(6 + k, ts), :])
        ext_dq[0:ts, :] = dq
        dcv = vl_ref[5:6, :] * ext_dq[pl.ds(2, ts), :]
        for k in range(1, 3):
            dcv = dcv + vl_ref[5 + k:6 + k, :] * ext_dq[pl.ds(2 - k, ts), :]
        ext_dq[ts:ts + HALO, :] = ext_dq[0:HALO, :]
        dproj_ref[:, 1536:2048] = (dcv * u_v).astype(BF16)
        dproj_ref[:, 2048:2560] = (dcv * u_c).astype(BF16)

        dh = _dot_nt(dproj_ref[:, 0:WIN_BLK], win_ref[0])
        for j in range(1, N_CHIP):
            dh = dh + _dot_nt(dproj_ref[:, j * WIN_BLK:(j + 1) * WIN_BLK], win_ref[j])
        for j in range(N_CHIP):
            gwin_acc[j] += _dot_tn(hb_ref[...], dproj_ref[:, j * WIN_BLK:(j + 1) * WIN_BLK])
        xt = x_ref[...]
        r1 = lax.rsqrt(jnp.mean(xt * xt, axis=-1, keepdims=True) + EPS)
        n1 = xt * r1
        accd_ref[1:2, :] += _colsum(dh * (n1 * g1))
        accd_ref[0:1, :] += _colsum(dh)
        dhn1 = dh * (1.0 + scale1)
        accd_ref[3:4, :] += _colsum(dhn1 * n1)
        dn1 = dhn1 * g1
        gx_ref[...] = dx1 + r1 * (dn1 - n1 * jnp.mean(dn1 * n1, axis=-1, keepdims=True))

        @pl.when(i == nt - 1)
        def _():
            outs = [pltpu.make_async_copy(acc, dst, sems.at[k]) for k, (acc, dst) in enumerate(
                ((gwin_acc, gwin_hbm), (gwout_acc, gwout_hbm), (ggate_acc, ggate_hbm)))]
            for cp in outs:
                cp.start()
            for cp in _xchg_copies((p1_ref, p2_ref), (q1_ref, q2_ref), x_send, x_recv):
                cp.wait()
            for cp in outs:
                cp.wait()

    tile = lambda w: pl.BlockSpec((ts, w), lambda i: (nt - 1 - i, 0))
    halo = lambda w: pl.BlockSpec((HALO, w), lambda i: (jnp.maximum((nt - 1 - i) * hpt - 1, 0), 0))
    ext = pltpu.VMEM((ts + HALO, D_LRU), F32)
    return pl.pallas_call(
        body, name="mix_bwd", grid=(nt,),
        in_specs=[SMEM, tile(D_MODEL), tile(D_MODEL), tile(D_MODEL), tile(D_IN), halo(D_IN), tile(D_LRU), halo(D_LRU),
                  tile(D_MODEL), tile(D_MODEL), _full((8, D_MODEL)), _full((8, D_MODEL)), _full((16, D_LRU)),
                  ANY, ANY, ANY, ANY, _full((D_LRU, 2 * D_LRU), True), _full((D_LRU, D_LRU), True), ANY, ANY],
        out_specs=[tile(D_MODEL), _full((8, D_MODEL)), _full((16, D_LRU)), ANY, ANY, ANY, ANY, ANY],
        out_shape=[jax.ShapeDtypeStruct((s, D_MODEL), F32),
                   jax.ShapeDtypeStruct((8, D_MODEL), F32), jax.ShapeDtypeStruct((16, D_LRU), F32),
                   jax.ShapeDtypeStruct((N_CHIP, D_MODEL, WIN_BLK), F32), jax.ShapeDtypeStruct((D_MODEL, D_MODEL), F32),
                   jax.ShapeDtypeStruct((D_LRU, 2 * D_LRU), F32)]
        + [jax.ShapeDtypeStruct((3,) + p.shape[1:], p.dtype) for p in mlp_parts],
        scratch_shapes=[pltpu.VMEM((N_CHIP, D_MODEL, WIN_BLK), BF16), pltpu.VMEM((D_MODEL, D_MODEL), BF16),
                        pltpu.VMEM((ts, D_IN), BF16), pltpu.VMEM((ts, 2 * D_LRU), BF16),
                        pltpu.VMEM((N_CHIP, D_MODEL, WIN_BLK), F32), pltpu.VMEM((D_MODEL, D_MODEL), F32),
                        pltpu.VMEM((D_LRU, 2 * D_LRU), F32),
                        ext, ext, ext, ext, ext, pltpu.VMEM((ts, D_LRU), F32),
                        pltpu.VMEM((HALO, D_LRU), F32), pltpu.VMEM((HALO, D_LRU), F32),
                        pltpu.SemaphoreType.DMA((2 * N_CHIP,)),
                        pltpu.SemaphoreType.DMA((6,)), pltpu.SemaphoreType.DMA((6,))],
        compiler_params=pltpu.CompilerParams(dimension_semantics=("arbitrary",), vmem_limit_bytes=VMEM_LIMIT),
    )(chip, dx1, x, mixed, proj, proj, hl, hl, hb, ycat, mod, vecd, vecl, *win, *wout, gab, a64, *mlp_parts)


def _wgrad(name, a, b, a_blk, b_blk, out_dtype):
    s = a.shape[0]
    aw = a_blk or a.shape[1]
    bw = b_blk or b.shape[1]
    nblk = N_CHIP if (a_blk or b_blk) else 1

    def body(a_ref, b_ref, o_ref):
        o_ref[0] = _dot_tn(a_ref[...], b_ref[...]).astype(out_dtype)

    return pl.pallas_call(
        body, name=name, grid=(nblk,),
        in_specs=[pl.BlockSpec((s, aw), (lambda j: (0, j)) if a_blk else (lambda j: (0, 0))),
                  pl.BlockSpec((s, bw), (lambda j: (0, j)) if b_blk else (lambda j: (0, 0)))],
        out_specs=pl.BlockSpec((1, aw, bw), lambda j: (j, 0, 0)),
        out_shape=jax.ShapeDtypeStruct((nblk, aw, bw), out_dtype),
        compiler_params=pltpu.CompilerParams(dimension_semantics=("arbitrary",), vmem_limit_bytes=VMEM_LIMIT),
    )(a, b)


def _mod_matmul(c_all, ada_w_loc):
    n = ada_w_loc.shape[1]
    cb = 512

    def body(c_ref, w_ref, o_ref):
        c = c_ref[...]
        sc = c * jax.nn.sigmoid(c)
        o_ref[...] = _dot(sc.astype(BF16), w_ref[...].astype(BF16))

    return pl.pallas_call(
        body, name="mod_matmul", grid=(n // cb,),
        in_specs=[_full((8, D_MODEL)), pl.BlockSpec((D_MODEL, cb), lambda j: (0, j))],
        out_specs=pl.BlockSpec((8, cb), lambda j: (0, j)),
        out_shape=jax.ShapeDtypeStruct((8, n), F32),
        compiler_params=pltpu.CompilerParams(dimension_semantics=("arbitrary",), vmem_limit_bytes=VMEM_LIMIT),
    )(c_all, ada_w_loc)


def _adam_math(w, g, m, v):
    m = ADAM_B1 * m + (1.0 - ADAM_B1) * g
    v = ADAM_B2 * v + (1.0 - ADAM_B2) * (g * g)
    m_hat = m / (1.0 - ADAM_B1 ** ADAM_STEP)
    v_hat = v / (1.0 - ADAM_B2 ** ADAM_STEP)
    delta = (-ADAM_LR) * (m_hat / (jnp.sqrt(v_hat) + ADAM_EPS) + ADAM_WD * w)
    return delta, m, v


def _adam(name, core, shards):
    n = len(shards)
    r, c = shards[0][0].shape
    half = r // 2
    rb = min(half, 128)
    nh = half // rb

    def body(core_ref, *refs):
        ins, outs = refs[:5 * n], refs[5 * n:]
        mine = (pl.program_id(0) // nh) == core_ref[0]
        for k in range(n):
            w_ref, go_ref, gs_ref, m_ref, v_ref = ins[5 * k:5 * k + 5]
            g_ref, d_ref, mo_ref, vo_ref = outs[4 * k:4 * k + 4]
            g = jnp.where(mine, go_ref[...], gs_ref[...])
            g_ref[...] = g
            d_ref[...], mo_ref[...], vo_ref[...] = _adam_math(w_ref[...], g, m_ref[...], v_ref[...])

    spec = pl.BlockSpec((rb, c), lambda i, core_ref: (i, 0))
    hspec = pl.BlockSpec((rb, c), lambda i, core_ref: (i % nh, 0))
    sds = jax.ShapeDtypeStruct((r, c), F32)
    res = pl.pallas_call(
        body, name=name,
        grid_spec=pltpu.PrefetchScalarGridSpec(
            num_scalar_prefetch=1, grid=(r // rb,),
            in_specs=[spec, hspec, hspec, spec, spec] * n, out_specs=[spec] * (4 * n)),
        out_shape=[sds] * (4 * n),
        compiler_params=pltpu.CompilerParams(dimension_semantics=("arbitrary",), vmem_limit_bytes=VMEM_LIMIT),
    )(core, *[t for s in shards for t in s])
    return [res[4 * k:4 * k + 4] for k in range(n)]


def _ada_grad_adam(sct, dmod_loc, w, m, v):
    r, c = w.shape
    rb = 128

    def body(s_ref, dm_ref, w_ref, m_ref, v_ref, g_ref, d_ref, mo_ref, vo_ref):
        g = s_ref[:, 0:1] * dm_ref[0:1, :]
        for b in range(1, 8):
            g = g + s_ref[:, b:b + 1] * dm_ref[b:b + 1, :]
        g_ref[...] = g
        d_ref[...], mo_ref[...], vo_ref[...] = _adam_math(w_ref[...], g, m_ref[...], v_ref[...])

    spec = pl.BlockSpec((rb, c), lambda i: (i, 0))
    sds = jax.ShapeDtypeStruct((r, c), F32)
    return pl.pallas_call(
        body, name="ada_grad_adam", grid=(r // rb,),
        in_specs=[pl.BlockSpec((rb, 8), lambda i: (i, 0)), _full((8, c)), spec, spec, spec],
        out_specs=[spec] * 4, out_shape=[sds] * 4,
        compiler_params=pltpu.CompilerParams(dimension_semantics=("arbitrary",), vmem_limit_bytes=VMEM_LIMIT),
    )(sct, dmod_loc, w, m, v)


def _position():
    x, y, c = lax.axis_index("x"), lax.axis_index("y"), lax.axis_index("c")
    chips = [(1 - x, y), (x, 1 - y), (1 - x, 1 - y)]
    return x, y, c, chips


def _allgather8(name, arrs, shards=()):
    na, ns = len(arrs), len(shards)

    def body(*refs):
        ins, w_own = refs[:na], refs[na:na + ns]
        outs, w_all = refs[na + ns:2 * na + ns], refs[2 * na + ns:2 * (na + ns)]
        send_sems, recv_sems, local_sems = refs[2 * (na + ns):2 * (na + ns) + 3]
        if ns:
            _ag_start(w_own, w_all, *refs[-2:])
        x, y, c, chips = _position()
        me, sibling = (x, y, c), (x, y, 1 - c)
        first, passed, local = [], [], []
        for a in range(na):
            m_per = ins[a].shape[0]

            def rows(px, py, pc, a=a, m_per=m_per):
                return outs[a].at[pl.ds((4 * px + 2 * py + pc) * m_per, m_per), :]

            def copy(k, block, to, src=None, a=a, rows=rows):
                return pltpu.make_async_remote_copy(
                    src_ref=rows(*block) if src is None else src, dst_ref=rows(*block),
                    send_sem=send_sems.at[7 * a + k], recv_sem=recv_sems.at[7 * a + k],
                    device_id=to, device_id_type=MESH)

            mine = pltpu.make_async_copy(ins[a], rows(*me), local_sems.at[a])
            mine.start()
            local.append(mine)
            f = [copy(0, me, sibling, src=ins[a])]
            f += [copy(1 + j, me, (*chip, c), src=ins[a]) for j, chip in enumerate(chips)]
            for cp in f:
                cp.start()
            first.append((f, copy))
        for a in range(na):
            f, copy = first[a]
            p = [copy(4 + j, (*chip, c), sibling) for j, chip in enumerate(chips)]
            for j, chip in enumerate(chips):
                copy(1 + j, (*chip, c), me).wait_recv()
                p[j].start()
            passed.append(p)
        for a in range(na):
            f, copy = first[a]
            copy(0, sibling, me).wait_recv()
            for j, chip in enumerate(chips):
                copy(4 + j, (*chip, 1 - c), me).wait_recv()
            for cp in f + passed[a]:
                cp.wait_send()
            local[a].wait()
        if ns:
            _ag_finish(w_own, w_all, *refs[-2:])

    return pl.pallas_call(
        body, name=name,
        out_shape=[jax.ShapeDtypeStruct((8 * a.shape[0], a.shape[1]), a.dtype) for a in arrs]
        + [jax.ShapeDtypeStruct((N_CHIP,) + s.shape, s.dtype) for s in shards],
        in_specs=[VMEM] * na + [ANY] * ns, out_specs=[VMEM] * na + [ANY] * ns,
        scratch_shapes=[pltpu.SemaphoreType.DMA((7 * na,)), pltpu.SemaphoreType.DMA((7 * na,)),
                        pltpu.SemaphoreType.DMA((na,))]
        + [pltpu.SemaphoreType.DMA((AG_SEMS * ns,))] * (2 if ns else 0),
        compiler_params=pltpu.CompilerParams(vmem_limit_bytes=VMEM_LIMIT),
    )(*arrs, *shards)


AG_SEMS = 7


def _ag_copies(ins, outs, send_sems, recv_sems):
    x, y, c, chips = _position()
    sibling = (x, y, 1 - c)
    xn, yn, dg = [2 * chip[0] + chip[1] for chip in chips]
    to_x, to_y = (1 - x, y, c), (x, 1 - y, c)
    res = []
    for a in range(len(ins)):
        half = ins[a].shape[0] // 2
        quarter = half // 2

        def copy(k, dst, to, src=None, a=a):
            return pltpu.make_async_remote_copy(
                src_ref=dst if src is None else src, dst_ref=dst,
                send_sem=send_sems.at[AG_SEMS * a + k], recv_sem=recv_sems.at[AG_SEMS * a + k],
                device_id=to, device_id_type=MESH)

        def rows(chip, pc, q=None, a=a, half=half, quarter=quarter):
            if q is None:
                return outs[a].at[chip, pl.ds(pc * half, half), :]
            return outs[a].at[chip, pl.ds(pc * half + q * quarter, quarter), :]

        own = ins[a].at[pl.ds(c * half, half), :]
        mine = rows(2 * x + y, c)
        res.append(dict(
            sends=[copy(0, mine, to_x, src=own), copy(1, mine, to_y, src=own)],
            from_x=copy(0, rows(xn, c), to_x), from_y=copy(1, rows(yn, c), to_y),
            relay_y=copy(2, rows(xn, c, 0), to_y), relay_x=copy(3, rows(yn, c, 1), to_x),
            from_y_relay=copy(2, rows(dg, c, 0), to_y), from_x_relay=copy(3, rows(dg, c, 1), to_x),
            pass_on=[copy(4, rows(xn, c), sibling), copy(5, rows(yn, c), sibling), copy(6, rows(dg, c), sibling)],
            from_sibling=[copy(4, rows(xn, 1 - c), sibling), copy(5, rows(yn, 1 - c), sibling),
                          copy(6, rows(dg, 1 - c), sibling)]))
    return res


def _ag_start(ins, outs, send_sems, recv_sems):
    for cps in _ag_copies(ins, outs, send_sems, recv_sems):
        for cp in cps["sends"]:
            cp.start()


def _ag_relay(ins, outs, send_sems, recv_sems, which):
    copies = _ag_copies(ins, outs, send_sems, recv_sems)
    for a in which:
        cps = copies[a]
        cps["from_x"].wait_recv()
        cps["relay_y"].start()
        cps["pass_on"][0].start()
        cps["from_y"].wait_recv()
        cps["relay_x"].start()
        cps["pass_on"][1].start()


def _ag_complete(ins, outs, send_sems, recv_sems):
    copies = _ag_copies(ins, outs, send_sems, recv_sems)
    for cps in copies:
        cps["from_y_relay"].wait_recv()
        cps["from_x_relay"].wait_recv()
        cps["pass_on"][2].start()
    for cps in copies:
        for cp in cps["from_sibling"]:
            cp.wait_recv()
        for cp in cps["sends"] + [cps["relay_y"], cps["relay_x"]] + cps["pass_on"]:
            cp.wait_send()


def _ag_finish(ins, outs, send_sems, recv_sems):
    _ag_relay(ins, outs, send_sems, recv_sems, range(len(ins)))
    _ag_complete(ins, outs, send_sems, recv_sems)


def _swap_halves(name, grads, after=()):
    na, nw = len(grads), len(after)

    def body(*refs):
        ins, outs = refs[:na], refs[na + nw:2 * na + nw]
        send_sems, recv_sems = refs[2 * na + nw:]
        x, y, c, _ = _position()
        cps = []
        for a in range(na):
            half = ins[a].shape[1] // 2
            cp = pltpu.make_async_remote_copy(
                src_ref=ins[a].at[:, pl.ds((1 - c) * half, half), :], dst_ref=outs[a],
                send_sem=send_sems.at[a], recv_sem=recv_sems.at[a],
                device_id=(x, y, 1 - c), device_id_type=MESH)
            cp.start()
            cps.append(cp)
        for cp in cps:
            cp.wait()

    return pl.pallas_call(
        body, name=name,
        out_shape=[jax.ShapeDtypeStruct((g.shape[0], g.shape[1] // 2, g.shape[2]), g.dtype) for g in grads],
        in_specs=[ANY] * (na + nw), out_specs=[ANY] * na,
        scratch_shapes=[pltpu.SemaphoreType.DMA((na,)), pltpu.SemaphoreType.DMA((na,))],
    )(*grads, *after)


def _xchg_copies(ins, outs, send_sems, recv_sems):
    x, y, c, chips = _position()
    return [pltpu.make_async_remote_copy(
        src_ref=ins[a].at[2 * chip[0] + chip[1]], dst_ref=outs[a].at[j],
        send_sem=send_sems.at[3 * a + j], recv_sem=recv_sems.at[3 * a + j],
        device_id=(*chip, c), device_id_type=MESH) for a in range(len(ins)) for j, chip in enumerate(chips)]


def _exchange_chips(parts):
    na = len(parts)
    hbm = pltpu.MemorySpace.HBM
    ins = [jax.new_ref(p, memory_space=hbm) for p in parts]
    outs = [jax.empty_ref(jax.ShapeDtypeStruct((3,) + p.shape[1:], p.dtype), memory_space=hbm) for p in parts]

    @pl.kernel(mesh=plsc.ScalarSubcoreMesh(axis_name="sequencer", num_cores=1), name="rs_exchange_chips",
               scratch_types=(pltpu.SemaphoreType.DMA((3 * na,)), pltpu.SemaphoreType.DMA((3 * na,))),
               compiler_params=pltpu.CompilerParams(collective_id=0))
    def launch(send_sems, recv_sems):
        x, y, c, chips = _position()
        barrier = pltpu.get_barrier_semaphore()
        for chip in chips:
            pl.semaphore_signal(barrier, inc=1, device_id=(*chip, c), device_id_type=MESH)
        pl.semaphore_wait(barrier, len(chips))
        for cp in _xchg_copies(ins, outs, send_sems, recv_sems):
            cp.start()
        for cp in _xchg_copies(ins, outs, send_sems, recv_sems):
            cp.wait()

    launch()
    return [q[...] for q in outs]


def _swap_reduced(name, halves):
    na = len(halves)

    def body(*refs):
        ins, outs = refs[:na], refs[na:2 * na]
        send_sems, recv_sems = refs[2 * na:]
        x, y, c, _ = _position()
        cps = []
        for a in range(na):
            cp = pltpu.make_async_remote_copy(
                src_ref=ins[a], dst_ref=outs[a], send_sem=send_sems.at[a], recv_sem=recv_sems.at[a],
                device_id=(x, y, 1 - c), device_id_type=MESH)
            cp.start()
            cps.append(cp)
        for cp in cps:
            cp.wait()

    return pl.pallas_call(
        body, name=name,
        out_shape=[jax.ShapeDtypeStruct(h.shape, h.dtype) for h in halves],
        in_specs=[ANY] * na, out_specs=[ANY] * na,
        scratch_shapes=[pltpu.SemaphoreType.DMA((na,)), pltpu.SemaphoreType.DMA((na,))],
    )(*halves)


def _add_sibling(name, grad, recv, core):
    _, r, c = grad.shape
    half = r // 2
    rb = min(half, 256)
    nrb = half // rb

    def body(core_ref, g_ref, r_ref, o_ref):
        o_ref[...] = (g_ref[...].astype(F32) + r_ref[...].astype(F32)).astype(BF16)

    return pl.pallas_call(
        body, name=name,
        grid_spec=pltpu.PrefetchScalarGridSpec(
            num_scalar_prefetch=1, grid=(N_CHIP, nrb),
            in_specs=[pl.BlockSpec((1, rb, c), lambda j, i, core_ref: (j, core_ref[0] * nrb + i, 0)),
                      pl.BlockSpec((1, rb, c), lambda j, i, core_ref: (j, i, 0))],
            out_specs=pl.BlockSpec((1, rb, c), lambda j, i, core_ref: (j, i, 0))),
        out_shape=jax.ShapeDtypeStruct((N_CHIP, half, c), BF16),
        compiler_params=pltpu.CompilerParams(dimension_semantics=("arbitrary", "arbitrary"),
                                             vmem_limit_bytes=VMEM_LIMIT),
    )(core, grad, recv)


def _add_chips(name, chip, p, q):
    _, half, c = q.shape
    rb = min(half, 256)

    def body(chip_ref, p_ref, q_ref, o_ref):
        acc = p_ref[0].astype(F32)
        for j in range(3):
            acc = acc + q_ref[j].astype(F32)
        o_ref[...] = acc

    return pl.pallas_call(
        body, name=name,
        grid_spec=pltpu.PrefetchScalarGridSpec(
            num_scalar_prefetch=1, grid=(half // rb,),
            in_specs=[pl.BlockSpec((1, rb, c), lambda i, chip_ref: (chip_ref[0], i, 0)),
                      pl.BlockSpec((3, rb, c), lambda i, chip_ref: (0, i, 0))],
            out_specs=pl.BlockSpec((rb, c), lambda i, chip_ref: (i, 0))),
        out_shape=jax.ShapeDtypeStruct((half, c), F32),
        compiler_params=pltpu.CompilerParams(dimension_semantics=("arbitrary",), vmem_limit_bytes=VMEM_LIMIT),
    )(chip, p, q)


def _small_update(gad, gam, gl, gg, mychip, params):
    names = ["ada_b", "norm1_g", "lru_conv_b", "gate_a_w", "gate_a_b", "gate_x_w", "gate_x_b", "a_param",
             "lru_conv_w", "short_conv_w", "lru_out_g", "conv_out_g", "norm2_g", "final_g"]
    flat = [t for n in names for t in params[n]]
    nin = len(flat)

    def body(chip_ref, gad_ref, gam_ref, gl_ref, gg_ref, *refs):
        ins = {n: refs[3 * k:3 * k + 3] for k, n in enumerate(names)}
        outs = {n: refs[nin + 4 * k:nin + 4 * k + 4] for k, n in enumerate(names)}
        loss_ref, dmod_ref = refs[nin + 4 * len(names):nin + 4 * len(names) + 2]

        def dsum(ref, lo, n):
            per = ref.shape[0] // 8
            acc = ref[lo:lo + n, :].astype(F32)
            for dev in range(1, 8):
                acc = acc + ref[dev * per + lo:dev * per + lo + n, :].astype(F32)
            return acc

        def update(n, g):
            w_ref, m_ref, v_ref = ins[n]
            g_ref, d_ref, mo_ref, vo_ref = outs[n]
            g_ref[...] = g
            d_ref[...], mo_ref[...], vo_ref[...] = _adam_math(w_ref[...], g, m_ref[...], v_ref[...])

        d, dm, l, lw = refs[-4:]
        d[...] = dsum(gad_ref, 0, 8)
        dm[...] = dsum(gam_ref, 0, 8)
        l[...] = dsum(gl_ref, 0, 16)
        for dev in range(8):
            for k in range(3):
                dmod_ref[dev:dev + 1, k * D_MODEL:(k + 1) * D_MODEL] = gad_ref[dev * 8 + k:dev * 8 + k + 1, :]
                dmod_ref[dev:dev + 1, (3 + k) * D_MODEL:(4 + k) * D_MODEL] = gam_ref[dev * 8 + k:dev * 8 + k + 1, :]
        w_ref, m_ref, v_ref = ins["ada_b"]
        g_ref, d_ref, mo_ref, vo_ref = outs["ada_b"]
        for k in range(3):
            g_ref[:, k * D_MODEL:(k + 1) * D_MODEL] = d[k:k + 1, :]
            g_ref[:, (3 + k) * D_MODEL:(4 + k) * D_MODEL] = dm[k:k + 1, :]
        d_ref[...], mo_ref[...], vo_ref[...] = _adam_math(w_ref[...], g_ref[...], m_ref[...], v_ref[...])
        update("norm1_g", d[3:4, :])
        update("norm2_g", dm[3:4, :])
        update("final_g", dm[4:5, :])
        update("gate_a_b", d[4:5, 0:D_LRU])
        update("gate_x_b", d[4:5, D_LRU:2 * D_LRU])
        update("lru_conv_b", l[4:5, :])
        update("a_param", l[8:9, :] * jax.nn.sigmoid(ins["a_param"][0][...]))
        update("lru_out_g", l[9:10, :])
        update("conv_out_g", l[10:11, :])
        loss_ref[...] = jnp.broadcast_to(dm[5:6, 0:128], (8, 128))
        chip = chip_ref[0]
        acc = jnp.zeros((8, 128), F32)
        for j in range(N_CHIP):
            acc = acc + jnp.where(chip == j, l[0:8, j * 128:(j + 1) * 128], 0.0)
        lw[...] = acc
        update("lru_conv_w", lw[0:4, :])
        update("short_conv_w", lw[5:8, :])
        gates = dsum(gg_ref, 0, D_LRU)
        update("gate_a_w", gates[:, 0:HEAD])
        update("gate_x_w", gates[:, HEAD:2 * HEAD])

    out_shape = []
    for n in names:
        out_shape += [jax.ShapeDtypeStruct(params[n][0].shape, F32)] * 4
    out_shape += [jax.ShapeDtypeStruct((8, 128), F32), jax.ShapeDtypeStruct((8, 6 * D_MODEL), F32)]
    res = pl.pallas_call(
        body, name="small_update", out_shape=out_shape,
        in_specs=[SMEM] + [VMEM] * (4 + nin),
        out_specs=[VMEM] * len(out_shape),
        scratch_shapes=[pltpu.VMEM((8, D_MODEL), F32), pltpu.VMEM((8, D_MODEL), F32), pltpu.VMEM((16, D_LRU), F32),
                        pltpu.VMEM((8, 128), F32)],
        compiler_params=pltpu.CompilerParams(vmem_limit_bytes=VMEM_LIMIT),
    )(mychip, gad, gam, gl, gg, *flat)
    per = {n: res[4 * k:4 * k + 4] for k, n in enumerate(names)}
    return per, res[-2], res[-1]


def _block_diag(w):
    eye = jnp.eye(8, dtype=w.dtype)
    return (eye[:, None, :, None] * w[:, :, None, :]).reshape(8 * HEAD, 8 * HEAD)


def _diag_blocks(g):
    return jnp.concatenate([g[h * HEAD:(h + 1) * HEAD, h * HEAD:(h + 1) * HEAD] for h in range(8)], axis=0)


def kernel(x, c, ada_w, ada_b, norm1_g, w_in, lru_conv_w, lru_conv_b, gate_a_w, gate_a_b, gate_x_w, gate_x_b, a_param, short_conv_w, lru_out_g, conv_out_g, w_out, norm2_g, w_mlp1, w_mlp2, final_g, loss_target, m_ada_w, m_ada_b, m_norm1_g, m_w_in, m_lru_conv_w, m_lru_conv_b, m_gate_a_w, m_gate_a_b, m_gate_x_w, m_gate_x_b, m_a_param, m_short_conv_w, m_lru_out_g, m_conv_out_g, m_w_out, m_norm2_g, m_w_mlp1, m_w_mlp2, m_final_g, v_ada_w, v_ada_b, v_norm1_g, v_w_in, v_lru_conv_w, v_lru_conv_b, v_gate_a_w, v_gate_a_b, v_gate_x_w, v_gate_x_b, v_a_param, v_short_conv_w, v_lru_out_g, v_conv_out_g, v_w_out, v_norm2_g, v_w_mlp1, v_w_mlp2, v_final_g):
    xi, yi, ci = lax.axis_index("x"), lax.axis_index("y"), lax.axis_index("c")
    mychip = 2 * xi + yi
    me = 4 * xi + 2 * yi + ci

    own_in, own_out, own_w1, own_w2 = [w[0].astype(BF16) for w in (w_in, w_out, w_mlp1, w_mlp2)]
    c_blk = jnp.zeros((8, D_MODEL), F32).at[0:1].set(c)
    cw_blk = jnp.zeros((8, 128), F32).at[0:4].set(lru_conv_w[0]).at[4:7].set(short_conv_w[0])
    c_g, cw_g, win_all, wout_all = _allgather8("allgather_cond_weights", [c_blk, cw_blk], [own_in, own_out])
    c_all = c_g.reshape(8, 8, D_MODEL)[:, 0]
    cw_g = cw_g.reshape(4, 2, 8, 128)[:, 0]
    lcw = cw_g[:, 0:4].transpose(1, 0, 2).reshape(4, D_LRU)
    scw = cw_g[:, 4:7].transpose(1, 0, 2).reshape(3, D_LRU)

    mod_loc = _mod_matmul(c_all, ada_w[0])
    (mod_g,) = _allgather8("allgather_mod", [mod_loc])
    mod_all = mod_g.reshape(4, 2, 8, 6 * D_MODEL // 4)[:, 0].transpose(1, 0, 2).reshape(8, 6 * D_MODEL) + ada_b
    mod_pad = jnp.pad(mod_all.reshape(8, 6, D_MODEL), ((0, 0), (0, 2), (0, 0)))
    mod = lax.dynamic_slice_in_dim(mod_pad, me, 1, axis=0).reshape(8, D_MODEL)

    win, wout = (win_all, own_in), (wout_all, own_out)
    chip = mychip.reshape(1).astype(jnp.int32)
    core = ci.reshape(1).astype(jnp.int32)

    vecd = jnp.concatenate([norm1_g, norm2_g, final_g[None, :], jnp.concatenate([gate_a_b, gate_x_b], axis=1),
                            jnp.zeros((4, D_MODEL), F32)], axis=0)
    vecl = jnp.concatenate([lcw, lru_conv_b, scw, a_param, lru_out_g, conv_out_g, jnp.zeros((5, D_LRU), F32)], axis=0)
    gab = jnp.concatenate([_block_diag(gate_a_w[0]), _block_diag(gate_x_w[0])], axis=1).astype(BF16)
    a64 = _block_diag(jnp.full((8, HEAD, HEAD), 1.0 / HEAD, F32)).astype(BF16)

    hb, proj, hl, ycat, mixed, x1, w1_all, w2_all = _mix_fwd(
        chip, x[0], mod, vecd, vecl, win, wout, gab, a64, [own_w1, own_w2])
    dx1, act, dz, dmo, h2b, accm = _mlp_fwd_bwd(
        chip, x1, loss_target[0], mod, vecd, (w1_all, own_w1), (w2_all, own_w2))

    def sibling_sum(tag, grads, after=()):
        recv = _swap_halves("rs_swap_halves_" + tag, grads, after)
        return [_add_sibling("rs_add_sibling_%s%d" % (tag, k), g, r, core) for k, (g, r) in enumerate(zip(grads, recv))]

    parts_mlp = sibling_sum("mlp", [_wgrad("wgrad_mlp1", h2b, dz, 0, FF_BLK, BF16),
                                    _wgrad("wgrad_mlp2", act, dmo, FF_BLK, 0, BF16)])
    grad_x, accd, accl, g_win, g_wout, g_gate, q_w1, q_w2 = _mix_bwd(
        chip, dx1, x[0], mixed, proj, hl, hb, ycat, mod, vecd, vecl, win, wout, gab, a64, parts_mlp)

    gg_blk = jnp.concatenate([_diag_blocks(g_gate[:, 0:D_LRU]), _diag_blocks(g_gate[:, D_LRU:2 * D_LRU])], axis=1)
    gad, gam, gl, gg = _allgather8("allgather_small_grads", [accd, accm, accl, gg_blk.astype(BF16)])
    parts_mix = sibling_sum("mix", [g_win, g_wout.reshape(N_CHIP, WOUT_BLK, D_MODEL)], after=[gad])

    def reduced(tag, parts, landed):
        own = [_add_chips("rs_add_chips_%s%d" % (tag, k), chip, p, q) for k, (p, q) in enumerate(zip(parts, landed))]
        return own, _swap_reduced("rs_swap_reduced_" + tag, own)

    landed_mix = _exchange_chips(parts_mix)
    own_mlp, sib_mlp = reduced("mlp", parts_mlp, [q_w1, q_w2])
    res_w1, res_w2 = _adam("adam_mlp", core, [(w_mlp1[0], own_mlp[0], sib_mlp[0], m_w_mlp1[0], v_w_mlp1[0]),
                                              (w_mlp2[0], own_mlp[1], sib_mlp[1], m_w_mlp2[0], v_w_mlp2[0])])
    own_mix, sib_mix = reduced("mix", parts_mix, landed_mix)
    (res_win,) = _adam("adam_w_in", core, [(w_in[0], own_mix[0], sib_mix[0], m_w_in[0], v_w_in[0])])
    (res_wout,) = _adam("adam_w_out", core, [(w_out[0], own_mix[1], sib_mix[1], m_w_out[0], v_w_out[0])])

    params = {
        "ada_b": (ada_b, m_ada_b, v_ada_b), "norm1_g": (norm1_g, m_norm1_g, v_norm1_g),
        "lru_conv_b": (lru_conv_b, m_lru_conv_b, v_lru_conv_b),
        "gate_a_w": tuple(t.reshape(D_LRU, HEAD) for t in (gate_a_w, m_gate_a_w, v_gate_a_w)),
        "gate_a_b": (gate_a_b, m_gate_a_b, v_gate_a_b),
        "gate_x_w": tuple(t.reshape(D_LRU, HEAD) for t in (gate_x_w, m_gate_x_w, v_gate_x_w)),
        "gate_x_b": (gate_x_b, m_gate_x_b, v_gate_x_b), "a_param": (a_param, m_a_param, v_a_param),
        "lru_conv_w": tuple(t[0] for t in (lru_conv_w, m_lru_conv_w, v_lru_conv_w)),
        "short_conv_w": tuple(t[0] for t in (short_conv_w, m_short_conv_w, v_short_conv_w)),
        "lru_out_g": (lru_out_g, m_lru_out_g, v_lru_out_g), "conv_out_g": (conv_out_g, m_conv_out_g, v_conv_out_g),
        "norm2_g": (norm2_g, m_norm2_g, v_norm2_g),
        "final_g": tuple(t[None, :] for t in (final_g, m_final_g, v_final_g)),
    }
    small, loss_blk, dmod_cols = _small_update(gad, gam, gl, gg, chip, params)
    loss = loss_blk[0, 0]

    ncol = 6 * D_MODEL // N_CHIP
    dmod_loc = lax.dynamic_slice_in_dim(dmod_cols, mychip * ncol, ncol, axis=1)
    sct = (c_all * jax.nn.sigmoid(c_all)).T
    ada = _ada_grad_adam(sct, dmod_loc, ada_w[0], m_ada_w[0], v_ada_w[0])

    res = {"ada_w": ada, "w_in": res_win, "w_out": res_wout, "w_mlp1": res_w1, "w_mlp2": res_w2}
    res = {n: tuple(t[None] for t in r) for n, r in res.items()}
    shapes = {"gate_a_w": gate_a_w.shape, "gate_x_w": gate_x_w.shape, "lru_conv_w": lru_conv_w.shape,
              "short_conv_w": short_conv_w.shape, "final_g": final_g.shape}
    for n, t in small.items():
        res[n] = tuple(u.reshape(shapes[n]) if n in shapes else u for u in t)

    order = ["ada_w", "ada_b", "norm1_g", "w_in", "lru_conv_w", "lru_conv_b", "gate_a_w", "gate_a_b", "gate_x_w",
             "gate_x_b", "a_param", "short_conv_w", "lru_out_g", "conv_out_g", "w_out", "norm2_g", "w_mlp1",
             "w_mlp2", "final_g"]
    return (loss, grad_x[None], *[res[n][0] for n in order], *[res[n][1] for n in order],
            *[res[n][2] for n in order], *[res[n][3] for n in order])
```

```python
import jax
import jax.numpy as jnp
from jax import lax
from jax.experimental import pallas as pl
from jax.experimental.pallas import tpu as pltpu
from jax.experimental.pallas import tpu_sc as plsc

F32 = jnp.float32
BF16 = jnp.bfloat16

D_MODEL = 1024
D_LRU = 512
D_IN = 2560
D_FF = 4096
N_CHIP = 4
WIN_BLK = D_IN // N_CHIP
WOUT_BLK = D_MODEL // N_CHIP
FF_BLK = D_FF // N_CHIP
HEAD = 64
EPS = 1e-6
C_GATE = 8.0
TOKEN_TILE = 256
HALO = 8
VMEM_LIMIT = 60 * 1024 * 1024

ADAM_LR = 0.001
ADAM_B1 = 0.9
ADAM_B2 = 0.999
ADAM_EPS = 1e-08
ADAM_WD = 0.01
ADAM_STEP = 10

MESH = pl.DeviceIdType.MESH
ANY = pl.BlockSpec(memory_space=pl.ANY)
VMEM = pl.BlockSpec(memory_space=pltpu.VMEM)
SMEM = pl.BlockSpec(memory_space=pltpu.SMEM)


def _full(shape, single=False):
    nd = len(shape)
    if single:
        return pl.BlockSpec(shape, lambda *_: (0,) * nd, pipeline_mode=pl.Buffered(1))
    return pl.BlockSpec(shape, lambda *_: (0,) * nd)


def _dot(a, b):
    return jnp.dot(a, b, preferred_element_type=F32)


def _dot_nt(a, b):
    return lax.dot_general(a, b, (((1,), (1,)), ((), ())), preferred_element_type=F32)


def _dot_tn(a, b):
    return lax.dot_general(a, b, (((0,), (0,)), ((), ())), preferred_element_type=F32)


def _gmean(v, a64):
    hi = v.astype(BF16)
    lo = (v - hi.astype(F32)).astype(BF16)
    return _dot(hi, a64) + _dot(lo, a64)


def _gelu(x):
    u = 0.7978845608028654 * (x + 0.044715 * x * x * x)
    t = jnp.tanh(u)
    return 0.5 * x * (1.0 + t), t


def _gelu_grad(x, t):
    du = 0.7978845608028654 * (1.0 + 3.0 * 0.044715 * x * x)
    return 0.5 * (1.0 + t) + 0.5 * x * (1.0 - t * t) * du


def _log1p_pos(y):
    return jnp.where(y < 1e-2, y * (1.0 - y * (0.5 - y * (1.0 / 3.0 - y * 0.25))), jnp.log(1.0 + y))


def _softplus(a):
    return jnp.maximum(a, 0.0) + _log1p_pos(jnp.exp(-jnp.abs(a)))


def _neg_expm1(z):
    series = -z * (1.0 + z * (0.5 + z * (1.0 / 6.0 + z * (1.0 / 24.0 + z * (1.0 / 120.0)))))
    return jnp.where(z > -0.02, series, 1.0 - jnp.exp(z))


def _scan_fwd(a, b, row):
    n = a.shape[0]
    d = 1
    while d < n:
        m = row >= d
        b = jnp.where(m, a * pltpu.roll(b, d, 0) + b, b)
        a = jnp.where(m, a * pltpu.roll(a, d, 0), a)
        d *= 2
    return a, b


def _scan_rev(a, b, row):
    n = a.shape[0]
    d = 1
    while d < n:
        m = row < n - d
        b = jnp.where(m, b + a * pltpu.roll(b, n - d, 0), b)
        a = jnp.where(m, a * pltpu.roll(a, n - d, 0), a)
        d *= 2
    return a, b


def _colsum(v):
    return jnp.sum(v, axis=0, keepdims=True)


def _load_gathered(chip, gathered, own, slot, sems):
    copies = []
    for j in range(N_CHIP):
        @pl.when(chip == j)
        def _(j=j):
            pltpu.make_async_copy(own, slot(j), sems.at[j]).start()

        @pl.when(chip != j)
        def _(j=j):
            pltpu.make_async_copy(gathered.at[j], slot(j), sems.at[j]).start()

        copies.append(pltpu.make_async_copy(own, slot(j), sems.at[j]))
    return copies


def _lru_gates(xlb, gab, gbias, sp, first_row):
    g = _dot(xlb, gab) + gbias
    r = jax.nn.sigmoid(g[:, :D_LRU])
    ig = jax.nn.sigmoid(g[:, D_LRU:])
    la = (-C_GATE) * r * sp
    a = jnp.exp(la)
    msq = jnp.sqrt(_neg_expm1(2.0 * la))
    mult = jnp.where(first_row, 1.0, msq)
    return r, ig, a, msq, mult


def _mix_fwd(chip, x, mod, vecd, vecl, win, wout, gab, a64):
    s = x.shape[0]
    ts = TOKEN_TILE
    nt = s // ts

    def body(chip_ref, x_ref, mod_ref, vd_ref, vl_ref, win_hbm, win_own, wout_hbm, wout_own, gab_ref, a64_ref,
             hb_ref, proj_ref, hl_ref, ycat_ref, mixed_ref, x1_ref,
             win_ref, wout_ref, ext_lx, ext_cv, hcar, sems):
        i = pl.program_id(0)

        @pl.when(i == 0)
        def _():
            cps = _load_gathered(chip_ref[0], win_hbm, win_own, lambda j: win_ref.at[j], sems.at[pl.ds(0, N_CHIP)])
            cps += _load_gathered(chip_ref[0], wout_hbm, wout_own,
                                  lambda j: wout_ref.at[pl.ds(j * WOUT_BLK, WOUT_BLK), :],
                                  sems.at[pl.ds(N_CHIP, N_CHIP)])
            ext_lx[0:HALO, :] = jnp.zeros((HALO, D_LRU), F32)
            ext_cv[0:HALO, :] = jnp.zeros((HALO, D_LRU), F32)
            hcar[...] = jnp.zeros_like(hcar)
            for cp in cps:
                cp.wait()

        row = lax.broadcasted_iota(jnp.int32, (ts, D_LRU), 0)
        first_row = jnp.logical_and(row == 0, i == 0)
        xt = x_ref[...]
        shift1, scale1, gate1 = mod_ref[0:1, :], mod_ref[1:2, :], mod_ref[2:3, :]
        r1 = lax.rsqrt(jnp.mean(xt * xt, axis=-1, keepdims=True) + EPS)
        h = (xt * r1) * vd_ref[0:1, :] * (1.0 + scale1) + shift1
        hb = h.astype(BF16)
        hb_ref[...] = hb
        for j in range(N_CHIP):
            proj_ref[:, j * WIN_BLK:(j + 1) * WIN_BLK] = _dot(hb, win_ref[j])
        u_ly = proj_ref[:, 512:1024]
        u_b = proj_ref[:, 1024:1536]

        ext_lx[HALO:HALO + ts, :] = proj_ref[:, 0:512]
        xl = vl_ref[4:5, :] + vl_ref[0:1, :] * ext_lx[pl.ds(5, ts), :]
        for k in range(1, 4):
            xl = xl + vl_ref[k:k + 1, :] * ext_lx[pl.ds(5 + k, ts), :]
        ext_lx[0:HALO, :] = ext_lx[ts:ts + HALO, :]
        sp = _softplus(vl_ref[8:9, :])
        _, ig, a, _, mult = _lru_gates(xl.astype(BF16), gab_ref[...], vd_ref[3:4, :], sp, first_row)
        acum, hloc = _scan_fwd(a, mult * (ig * xl), row)
        hl = hloc + acum * hcar[0:1, :]
        hl_ref[...] = hl
        hcar[0:1, :] = hl_ref[ts - 1:ts, :]
        ge, _ = _gelu(u_ly)
        p = ge * hl
        y_lru = p * lax.rsqrt(_gmean(p * p, a64_ref[...]) + EPS) * vl_ref[9:10, :]
        ycat_ref[:, 0:512] = y_lru.astype(BF16)

        ext_cv[HALO:HALO + ts, :] = proj_ref[:, 1536:2048] * proj_ref[:, 2048:2560]
        q = vl_ref[5:6, :] * ext_cv[pl.ds(6, ts), :]
        for k in range(1, 3):
            q = q + vl_ref[5 + k:6 + k, :] * ext_cv[pl.ds(6 + k, ts), :]
        ext_cv[0:HALO, :] = ext_cv[ts:ts + HALO, :]
        yc = u_b * q
        y_conv = yc * lax.rsqrt(_gmean(yc * yc, a64_ref[...]) + EPS) * vl_ref[10:11, :]
        ycat_ref[:, 512:1024] = y_conv.astype(BF16)

        mixed = _dot(ycat_ref[...], wout_ref[...])
        mixed_ref[...] = mixed
        x1_ref[...] = xt + gate1 * mixed

    tile = lambda w: pl.BlockSpec((ts, w), lambda i: (i, 0))
    return pl.pallas_call(
        body, name="mix_fwd", grid=(nt,),
        in_specs=[SMEM, tile(D_MODEL), _full((8, D_MODEL)), _full((8, D_MODEL)), _full((16, D_LRU)),
                  ANY, ANY, ANY, ANY, _full((D_LRU, 2 * D_LRU), True), _full((D_LRU, D_LRU), True)],
        out_specs=[tile(D_MODEL), tile(D_IN), tile(D_LRU), tile(D_MODEL), tile(D_MODEL), tile(D_MODEL)],
        out_shape=[jax.ShapeDtypeStruct((s, D_MODEL), BF16), jax.ShapeDtypeStruct((s, D_IN), F32),
                   jax.ShapeDtypeStruct((s, D_LRU), F32), jax.ShapeDtypeStruct((s, D_MODEL), BF16),
                   jax.ShapeDtypeStruct((s, D_MODEL), F32), jax.ShapeDtypeStruct((s, D_MODEL), F32)],
        scratch_shapes=[pltpu.VMEM((N_CHIP, D_MODEL, WIN_BLK), BF16), pltpu.VMEM((D_MODEL, D_MODEL), BF16),
                        pltpu.VMEM((ts + HALO, D_LRU), F32), pltpu.VMEM((ts + HALO, D_LRU), F32),
                        pltpu.VMEM((HALO, D_LRU), F32), pltpu.SemaphoreType.DMA((2 * N_CHIP,))],
        compiler_params=pltpu.CompilerParams(dimension_semantics=("arbitrary",), vmem_limit_bytes=VMEM_LIMIT),
    )(chip, x, mod, vecd, vecl, *win, *wout, gab, a64)


def _mlp_fwd_bwd(chip, x1, target, mod, vecd, w1, w2):
    s = x1.shape[0]
    ts = TOKEN_TILE
    nt = s // ts

    def body(chip_ref, x1_ref, tg_ref, mod_ref, vd_ref, w1_hbm, w1_own, w2_hbm, w2_own,
             dx1_ref, act_ref, dz_ref, dmo_ref, h2_ref, acc_ref, w1_v, w2_v, rz_v, sems):
        i = pl.program_id(0)

        @pl.when(i == 0)
        def _():
            cps = _load_gathered(chip_ref[0], w1_hbm, w1_own, lambda j: w1_v.at[j], sems.at[pl.ds(0, N_CHIP)])
            cps += _load_gathered(chip_ref[0], w2_hbm, w2_own, lambda j: w2_v.at[j], sems.at[pl.ds(N_CHIP, N_CHIP)])
            acc_ref[...] = jnp.zeros_like(acc_ref)
            for cp in cps:
                cp.wait()

        xt = x1_ref[...]
        shift2, scale2, gate2 = mod_ref[3:4, :], mod_ref[4:5, :], mod_ref[5:6, :]
        g2, gf = vd_ref[1:2, :], vd_ref[2:3, :]
        r2 = lax.rsqrt(jnp.mean(xt * xt, axis=-1, keepdims=True) + EPS)
        n2 = xt * r2
        h2b = (n2 * g2 * (1.0 + scale2) + shift2).astype(BF16)
        h2_ref[...] = h2b
        for j in range(N_CHIP):
            rz_v[j] = jnp.maximum(_dot(h2b, w1_v[j]), 0.0)
        mo = jnp.zeros((ts, D_MODEL), F32)
        for j in range(N_CHIP):
            rz = rz_v[j]
            actb = (rz * rz).astype(BF16)
            act_ref[:, j * FF_BLK:(j + 1) * FF_BLK] = actb
            mo = mo + _dot(actb, w2_v[j])
        x2 = xt + gate2 * mo
        r3 = lax.rsqrt(jnp.mean(x2 * x2, axis=-1, keepdims=True) + EPS)
        n3 = x2 * r3
        e = n3 * gf - tg_ref[...]
        loss = (0.5 / D_MODEL) * jnp.sum(_colsum(e * e), axis=1, keepdims=True)
        dy = e * (1.0 / D_MODEL)
        acc_ref[4:5, :] += _colsum(dy * n3)
        acc_ref[5:6, :] += jnp.broadcast_to(loss, (1, D_MODEL))
        dn3 = dy * gf
        dx2 = r3 * (dn3 - n3 * jnp.mean(dn3 * n3, axis=-1, keepdims=True))
        acc_ref[2:3, :] += _colsum(dx2 * mo)
        dmob = (dx2 * gate2).astype(BF16)
        dmo_ref[...] = dmob
        for j in range(N_CHIP):
            dz_ref[:, j * FF_BLK:(j + 1) * FF_BLK] = (_dot_nt(dmob, w2_v[j]) * (2.0 * rz_v[j])).astype(BF16)
        dh2 = jnp.zeros((ts, D_MODEL), F32)
        for j in range(N_CHIP):
            dh2 = dh2 + _dot_nt(dz_ref[:, j * FF_BLK:(j + 1) * FF_BLK], w1_v[j])
        acc_ref[1:2, :] += _colsum(dh2 * (n2 * g2))
        acc_ref[0:1, :] += _colsum(dh2)
        dhn2 = dh2 * (1.0 + scale2)
        acc_ref[3:4, :] += _colsum(dhn2 * n2)
        dn2 = dhn2 * g2
        dx1_ref[...] = dx2 + r2 * (dn2 - n2 * jnp.mean(dn2 * n2, axis=-1, keepdims=True))

    tile = lambda w: pl.BlockSpec((ts, w), lambda i: (i, 0))
    return pl.pallas_call(
        body, name="mlp_fwd_bwd", grid=(nt,),
        in_specs=[SMEM, tile(D_MODEL), tile(D_MODEL), _full((8, D_MODEL)), _full((8, D_MODEL)), ANY, ANY, ANY, ANY],
        out_specs=[tile(D_MODEL), tile(D_FF), tile(D_FF), tile(D_MODEL), tile(D_MODEL), _full((8, D_MODEL))],
        out_shape=[jax.ShapeDtypeStruct((s, D_MODEL), F32), jax.ShapeDtypeStruct((s, D_FF), BF16),
                   jax.ShapeDtypeStruct((s, D_FF), BF16), jax.ShapeDtypeStruct((s, D_MODEL), BF16),
                   jax.ShapeDtypeStruct((s, D_MODEL), BF16), jax.ShapeDtypeStruct((8, D_MODEL), F32)],
        scratch_shapes=[pltpu.VMEM((N_CHIP, D_MODEL, FF_BLK), BF16), pltpu.VMEM((N_CHIP, FF_BLK, D_MODEL), BF16),
                        pltpu.VMEM((N_CHIP, ts, FF_BLK), F32), pltpu.SemaphoreType.DMA((2 * N_CHIP,))],
        compiler_params=pltpu.CompilerParams(dimension_semantics=("arbitrary",), vmem_limit_bytes=VMEM_LIMIT),
    )(chip, x1, target, mod, vecd, *w1, *w2)


def _mix_bwd(chip, dx1, x, mixed, proj, hl, hb, ycat, mod, vecd, vecl, win, wout, gab, a64, mlp_parts):
    s = x.shape[0]
    ts = TOKEN_TILE
    nt = s // ts
    hpt = ts // HALO

    def body(chip_ref, dx1_ref, x_ref, mixed_ref, proj_ref, projh_ref, hl_ref, hlh_ref, hb_ref, ycat_ref,
             mod_ref, vd_ref, vl_ref, win_hbm, win_own, wout_hbm, wout_own, gab_ref, a64_ref, p1_ref, p2_ref,
             gx_ref, accd_ref, accl_ref, gwin_hbm, gwout_hbm, ggate_hbm, q1_ref, q2_ref,
             win_ref, wout_ref, dproj_ref, dgb_ref, gwin_acc, gwout_acc, ggate_acc,
             ext_lx, ext_cv, ext_hl, ext_dxl, ext_dq, gbuf, gcar, acar, sems, x_send, x_recv):
        i = pl.program_id(0)
        ri = nt - 1 - i

        @pl.when(i == 0)
        def _():
            for cp in _xchg_copies((p1_ref, p2_ref), (q1_ref, q2_ref), x_send, x_recv):
                cp.start()
            gwin_acc[...] = jnp.zeros_like(gwin_acc)
            gwout_acc[...] = jnp.zeros_like(gwout_acc)
            ggate_acc[...] = jnp.zeros_like(ggate_acc)
            cps = _load_gathered(chip_ref[0], win_hbm, win_own, lambda j: win_ref.at[j], sems.at[pl.ds(0, N_CHIP)])
            cps += _load_gathered(chip_ref[0], wout_hbm, wout_own,
                                  lambda j: wout_ref.at[pl.ds(j * WOUT_BLK, WOUT_BLK), :],
                                  sems.at[pl.ds(N_CHIP, N_CHIP)])
            for cp in cps:
                cp.wait()
            accd_ref[...] = jnp.zeros_like(accd_ref)
            accl_ref[...] = jnp.zeros_like(accl_ref)
            ext_dxl[ts:ts + HALO, :] = jnp.zeros((HALO, D_LRU), F32)
            ext_dq[ts:ts + HALO, :] = jnp.zeros((HALO, D_LRU), F32)
            gcar[...] = jnp.zeros_like(gcar)
            acar[...] = jnp.zeros_like(acar)

        row = lax.broadcasted_iota(jnp.int32, (ts, D_LRU), 0)
        first_row = jnp.logical_and(row == 0, ri == 0)
        halo_on = jnp.where(ri == 0, 0.0, 1.0)
        shift1, scale1, gate1 = mod_ref[0:1, :], mod_ref[1:2, :], mod_ref[2:3, :]
        g1 = vd_ref[0:1, :]
        a64m = a64_ref[...]
        lg, cg = vl_ref[9:10, :], vl_ref[10:11, :]

        dx1 = dx1_ref[...]
        accd_ref[2:3, :] += _colsum(dx1 * mixed_ref[...])
        dmb = (dx1 * gate1).astype(BF16)
        gwout_acc[...] += _dot_tn(ycat_ref[...], dmb)
        dycat = _dot_nt(dmb, wout_ref[...])
        dyl = dycat[:, 0:512]
        dyv = dycat[:, 512:1024]

        u_ly = proj_ref[:, 512:1024]
        u_b = proj_ref[:, 1024:1536]
        u_c = proj_ref[:, 1536:2048]
        u_v = proj_ref[:, 2048:2560]
        ext_lx[0:HALO, :] = projh_ref[:, 0:512] * halo_on
        ext_lx[HALO:HALO + ts, :] = proj_ref[:, 0:512]
        xl = vl_ref[4:5, :] + vl_ref[0:1, :] * ext_lx[pl.ds(5, ts), :]
        for k in range(1, 4):
            xl = xl + vl_ref[k:k + 1, :] * ext_lx[pl.ds(5 + k, ts), :]
        xlb = xl.astype(BF16)
        sp = _softplus(vl_ref[8:9, :])
        r, ig, a, msq, mult = _lru_gates(xlb, gab_ref[...], vd_ref[3:4, :], sp, first_row)
        hl = hl_ref[...]
        ge, th = _gelu(u_ly)
        p = ge * hl
        rl = lax.rsqrt(_gmean(p * p, a64m) + EPS)
        nl = p * rl
        ext_cv[0:HALO, :] = projh_ref[:, 1536:2048] * projh_ref[:, 2048:2560] * halo_on
        ext_cv[HALO:HALO + ts, :] = u_c * u_v
        q = vl_ref[5:6, :] * ext_cv[pl.ds(6, ts), :]
        for k in range(1, 3):
            q = q + vl_ref[5 + k:6 + k, :] * ext_cv[pl.ds(6 + k, ts), :]
        yc = u_b * q
        rc = lax.rsqrt(_gmean(yc * yc, a64m) + EPS)
        nc = yc * rc

        accl_ref[9:10, :] += _colsum(dyl * nl)
        dnl = dyl * lg
        dp = rl * (dnl - nl * _gmean(dnl * nl, a64m))
        dproj_ref[:, 512:1024] = ((dp * hl) * _gelu_grad(u_ly, th)).astype(BF16)
        a_next = jnp.where(row == ts - 1, acar[0:1, :], pltpu.roll(a, ts - 1, 0))
        acum, gloc = _scan_rev(a_next, dp * ge, row)
        gbuf[...] = gloc + acum * gcar[0:1, :]
        gcar[0:1, :] = gbuf[0:1, :]
        ext_hl[0:HALO, :] = hlh_ref[...] * halo_on
        ext_hl[HALO:HALO + ts, :] = hl
        acar[...] = a[0:HALO, :]
        gt = gbuf[...]
        da = gt * ext_hl[pl.ds(HALO - 1, ts), :]
        dmult = gt * ig * xl
        di = gt * mult * xl
        dxl = gt * mult * ig
        dla = da * a - jnp.where(first_row, 0.0, dmult * a * a / msq)
        accl_ref[8:9, :] += _colsum(dla * ((-C_GATE) * r))
        dra = dla * ((-C_GATE) * sp) * r * (1.0 - r)
        dia = di * ig * (1.0 - ig)
        accd_ref[4:5, 0:D_LRU] += _colsum(dra)
        accd_ref[4:5, D_LRU:2 * D_LRU] += _colsum(dia)
        dgb_ref[:, 0:D_LRU] = dra.astype(BF16)
        dgb_ref[:, D_LRU:2 * D_LRU] = dia.astype(BF16)
        dxl = dxl + _dot_nt(dgb_ref[...], gab_ref[...])
        ggate_acc[...] += _dot_tn(xlb, dgb_ref[...])
        accl_ref[4:5, :] += _colsum(dxl)
        for k in range(4):
            accl_ref[k:k + 1, :] += _colsum(dxl * ext_lx[pl.ds(5 + k, ts), :])
        ext_dxl[0:ts, :] = dxl
        du_lx = vl_ref[0:1, :] * ext_dxl[pl.ds(3, ts), :]
        for k in range(1, 4):
            du_lx = du_lx + vl_ref[k:k + 1, :] * ext_dxl[pl.ds(3 - k, ts), :]
        ext_dxl[ts:ts + HALO, :] = ext_dxl[0:HALO, :]
        dproj_ref[:, 0:512] = du_lx.astype(BF16)

        accl_ref[10:11, :] += _colsum(dyv * nc)
        dnc = dyv * cg
        dyc = rc * (dnc - nc * _gmean(dnc * nc, a64m))
        dproj_ref[:, 1024:1536] = (dyc * q).astype(BF16)
        dq = dyc * u_b
        for k in range(3):
            accl_ref[5 + k:6 + k, :] += _colsum(dq * ext_cv[pl.ds(6 + k, ts), :])
        ext_dq[0:ts, :] = dq
        dcv = vl_ref[5:6, :] * ext_dq[pl.ds(2, ts), :]
        for k in range(1, 3):
            dcv = dcv + vl_ref[5 + k:6 + k, :] * ext_dq[pl.ds(2 - k, ts), :]
        ext_dq[ts:ts + HALO, :] = ext_dq[0:HALO, :]
        dproj_ref[:, 1536:2048] = (dcv * u_v).astype(BF16)
        dproj_ref[:, 2048:2560] = (dcv * u_c).astype(BF16)

        dh = _dot_nt(dproj_ref[:, 0:WIN_BLK], win_ref[0])
        for j in range(1, N_CHIP):
            dh = dh + _dot_nt(dproj_ref[:, j * WIN_BLK:(j + 1) * WIN_BLK], win_ref[j])
        for j in range(N_CHIP):
            gwin_acc[j] += _dot_tn(hb_ref[...], dproj_ref[:, j * WIN_BLK:(j + 1) * WIN_BLK])
        xt = x_ref[...]
        r1 = lax.rsqrt(jnp.mean(xt * xt, axis=-1, keepdims=True) + EPS)
        n1 = xt * r1
        accd_ref[1:2, :] += _colsum(dh * (n1 * g1))
        accd_ref[0:1, :] += _colsum(dh)
        dhn1 = dh * (1.0 + scale1)
        accd_ref[3:4, :] += _colsum(dhn1 * n1)
        dn1 = dhn1 * g1
        gx_ref[...] = dx1 + r1 * (dn1 - n1 * jnp.mean(dn1 * n1, axis=-1, keepdims=True))

        @pl.when(i == nt - 1)
        def _():
            outs = [pltpu.make_async_copy(acc, dst, sems.at[k]) for k, (acc, dst) in enumerate(
                ((gwin_acc, gwin_hbm), (gwout_acc, gwout_hbm), (ggate_acc, ggate_hbm)))]
            for cp in outs:
                cp.start()
            for cp in _xchg_copies((p1_ref, p2_ref), (q1_ref, q2_ref), x_send, x_recv):
                cp.wait()
            for cp in outs:
                cp.wait()

    tile = lambda w: pl.BlockSpec((ts, w), lambda i: (nt - 1 - i, 0))
    halo = lambda w: pl.BlockSpec((HALO, w), lambda i: (jnp.maximum((nt - 1 - i) * hpt - 1, 0), 0))
    ext = pltpu.VMEM((ts + HALO, D_LRU), F32)
    return pl.pallas_call(
        body, name="mix_bwd", grid=(nt,),
        in_specs=[SMEM, tile(D_MODEL), tile(D_MODEL), tile(D_MODEL), tile(D_IN), halo(D_IN), tile(D_LRU), halo(D_LRU),
                  tile(D_MODEL), tile(D_MODEL), _full((8, D_MODEL)), _full((8, D_MODEL)), _full((16, D_LRU)),
                  ANY, ANY, ANY, ANY, _full((D_LRU, 2 * D_LRU), True), _full((D_LRU, D_LRU), True), ANY, ANY],
        out_specs=[tile(D_MODEL), _full((8, D_MODEL)), _full((16, D_LRU)), ANY, ANY, ANY, ANY, ANY],
        out_shape=[jax.ShapeDtypeStruct((s, D_MODEL), F32),
                   jax.ShapeDtypeStruct((8, D_MODEL), F32), jax.ShapeDtypeStruct((16, D_LRU), F32),
                   jax.ShapeDtypeStruct((N_CHIP, D_MODEL, WIN_BLK), F32), jax.ShapeDtypeStruct((D_MODEL, D_MODEL), F32),
                   jax.ShapeDtypeStruct((D_LRU, 2 * D_LRU), F32)]
        + [jax.ShapeDtypeStruct((3,) + p.shape[1:], p.dtype) for p in mlp_parts],
        scratch_shapes=[pltpu.VMEM((N_CHIP, D_MODEL, WIN_BLK), BF16), pltpu.VMEM((D_MODEL, D_MODEL), BF16),
                        pltpu.VMEM((ts, D_IN), BF16), pltpu.VMEM((ts, 2 * D_LRU), BF16),
                        pltpu.VMEM((N_CHIP, D_MODEL, WIN_BLK), F32), pltpu.VMEM((D_MODEL, D_MODEL), F32),
                        pltpu.VMEM((D_LRU, 2 * D_LRU), F32),
                        ext, ext, ext, ext, ext, pltpu.VMEM((ts, D_LRU), F32),
                        pltpu.VMEM((HALO, D_LRU), F32), pltpu.VMEM((HALO, D_LRU), F32),
                        pltpu.SemaphoreType.DMA((2 * N_CHIP,)),
                        pltpu.SemaphoreType.DMA((6,)), pltpu.SemaphoreType.DMA((6,))],
        compiler_params=pltpu.CompilerParams(dimension_semantics=("arbitrary",), vmem_limit_bytes=VMEM_LIMIT),
    )(chip, dx1, x, mixed, proj, proj, hl, hl, hb, ycat, mod, vecd, vecl, *win, *wout, gab, a64, *mlp_parts)


def _wgrad(name, a, b, a_blk, b_blk, out_dtype):
    s = a.shape[0]
    aw = a_blk or a.shape[1]
    bw = b_blk or b.shape[1]
    nblk = N_CHIP if (a_blk or b_blk) else 1

    def body(a_ref, b_ref, o_ref):
        o_ref[0] = _dot_tn(a_ref[...], b_ref[...]).astype(out_dtype)

    return pl.pallas_call(
        body, name=name, grid=(nblk,),
        in_specs=[pl.BlockSpec((s, aw), (lambda j: (0, j)) if a_blk else (lambda j: (0, 0))),
                  pl.BlockSpec((s, bw), (lambda j: (0, j)) if b_blk else (lambda j: (0, 0)))],
        out_specs=pl.BlockSpec((1, aw, bw), lambda j: (j, 0, 0)),
        out_shape=jax.ShapeDtypeStruct((nblk, aw, bw), out_dtype),
        compiler_params=pltpu.CompilerParams(dimension_semantics=("arbitrary",), vmem_limit_bytes=VMEM_LIMIT),
    )(a, b)


def _mod_matmul(c_all, ada_w_loc):
    n = ada_w_loc.shape[1]
    cb = 512

    def body(c_ref, w_ref, o_ref):
        c = c_ref[...]
        sc = c * jax.nn.sigmoid(c)
        o_ref[...] = _dot(sc.astype(BF16), w_ref[...].astype(BF16))

    return pl.pallas_call(
        body, name="mod_matmul", grid=(n // cb,),
        in_specs=[_full((8, D_MODEL)), pl.BlockSpec((D_MODEL, cb), lambda j: (0, j))],
        out_specs=pl.BlockSpec((8, cb), lambda j: (0, j)),
        out_shape=jax.ShapeDtypeStruct((8, n), F32),
        compiler_params=pltpu.CompilerParams(dimension_semantics=("arbitrary",), vmem_limit_bytes=VMEM_LIMIT),
    )(c_all, ada_w_loc)


def _adam_math(w, g, m, v):
    m = ADAM_B1 * m + (1.0 - ADAM_B1) * g
    v = ADAM_B2 * v + (1.0 - ADAM_B2) * (g * g)
    m_hat = m / (1.0 - ADAM_B1 ** ADAM_STEP)
    v_hat = v / (1.0 - ADAM_B2 ** ADAM_STEP)
    delta = (-ADAM_LR) * (m_hat / (jnp.sqrt(v_hat) + ADAM_EPS) + ADAM_WD * w)
    return delta, m, v


def _adam(name, core, shards):
    n = len(shards)
    r, c = shards[0][0].shape
    half = r // 2
    rb = min(half, 128)
    nh = half // rb

    def body(core_ref, *refs):
        ins, outs = refs[:5 * n], refs[5 * n:]
        mine = (pl.program_id(0) // nh) == core_ref[0]
        for k in range(n):
            w_ref, go_ref, gs_ref, m_ref, v_ref = ins[5 * k:5 * k + 5]
            g_ref, d_ref, mo_ref, vo_ref = outs[4 * k:4 * k + 4]
            g = jnp.where(mine, go_ref[...], gs_ref[...])
            g_ref[...] = g
            d_ref[...], mo_ref[...], vo_ref[...] = _adam_math(w_ref[...], g, m_ref[...], v_ref[...])

    spec = pl.BlockSpec((rb, c), lambda i, core_ref: (i, 0))
    hspec = pl.BlockSpec((rb, c), lambda i, core_ref: (i % nh, 0))
    sds = jax.ShapeDtypeStruct((r, c), F32)
    res = pl.pallas_call(
        body, name=name,
        grid_spec=pltpu.PrefetchScalarGridSpec(
            num_scalar_prefetch=1, grid=(r // rb,),
            in_specs=[spec, hspec, hspec, spec, spec] * n, out_specs=[spec] * (4 * n)),
        out_shape=[sds] * (4 * n),
        compiler_params=pltpu.CompilerParams(dimension_semantics=("arbitrary",), vmem_limit_bytes=VMEM_LIMIT),
    )(core, *[t for s in shards for t in s])
    return [res[4 * k:4 * k + 4] for k in range(n)]


def _ada_grad_adam(sct, dmod_loc, w, m, v):
    r, c = w.shape
    rb = 128

    def body(s_ref, dm_ref, w_ref, m_ref, v_ref, g_ref, d_ref, mo_ref, vo_ref):
        g = s_ref[:, 0:1] * dm_ref[0:1, :]
        for b in range(1, 8):
            g = g + s_ref[:, b:b + 1] * dm_ref[b:b + 1, :]
        g_ref[...] = g
        d_ref[...], mo_ref[...], vo_ref[...] = _adam_math(w_ref[...], g, m_ref[...], v_ref[...])

    spec = pl.BlockSpec((rb, c), lambda i: (i, 0))
    sds = jax.ShapeDtypeStruct((r, c), F32)
    return pl.pallas_call(
        body, name="ada_grad_adam", grid=(r // rb,),
        in_specs=[pl.BlockSpec((rb, 8), lambda i: (i, 0)), _full((8, c)), spec, spec, spec],
        out_specs=[spec] * 4, out_shape=[sds] * 4,
        compiler_params=pltpu.CompilerParams(dimension_semantics=("arbitrary",), vmem_limit_bytes=VMEM_LIMIT),
    )(sct, dmod_loc, w, m, v)


def _position():
    x, y, c = lax.axis_index("x"), lax.axis_index("y"), lax.axis_index("c")
    chips = [(1 - x, y), (x, 1 - y), (1 - x, 1 - y)]
    return x, y, c, chips


def _allgather8(name, arrs):
    na = len(arrs)

    def body(*refs):
        ins, outs = refs[:na], refs[na:2 * na]
        send_sems, recv_sems, local_sems = refs[2 * na:]
        x, y, c, chips = _position()
        me, sibling = (x, y, c), (x, y, 1 - c)
        first, passed, local = [], [], []
        for a in range(na):
            m_per = ins[a].shape[0]

            def rows(px, py, pc, a=a, m_per=m_per):
                return outs[a].at[pl.ds((4 * px + 2 * py + pc) * m_per, m_per), :]

            def copy(k, block, to, src=None, a=a, rows=rows):
                return pltpu.make_async_remote_copy(
                    src_ref=rows(*block) if src is None else src, dst_ref=rows(*block),
                    send_sem=send_sems.at[7 * a + k], recv_sem=recv_sems.at[7 * a + k],
                    device_id=to, device_id_type=MESH)

            mine = pltpu.make_async_copy(ins[a], rows(*me), local_sems.at[a])
            mine.start()
            local.append(mine)
            f = [copy(0, me, sibling, src=ins[a])]
            f += [copy(1 + j, me, (*chip, c), src=ins[a]) for j, chip in enumerate(chips)]
            for cp in f:
                cp.start()
            first.append((f, copy))
        for a in range(na):
            f, copy = first[a]
            p = [copy(4 + j, (*chip, c), sibling) for j, chip in enumerate(chips)]
            for j, chip in enumerate(chips):
                copy(1 + j, (*chip, c), me).wait_recv()
                p[j].start()
            passed.append(p)
        for a in range(na):
            f, copy = first[a]
            copy(0, sibling, me).wait_recv()
            for j, chip in enumerate(chips):
                copy(4 + j, (*chip, 1 - c), me).wait_recv()
            for cp in f + passed[a]:
                cp.wait_send()
            local[a].wait()

    return pl.pallas_call(
        body, name=name,
        out_shape=[jax.ShapeDtypeStruct((8 * a.shape[0], a.shape[1]), a.dtype) for a in arrs],
        in_specs=[VMEM] * na, out_specs=[VMEM] * na,
        scratch_shapes=[pltpu.SemaphoreType.DMA((7 * na,)), pltpu.SemaphoreType.DMA((7 * na,)),
                        pltpu.SemaphoreType.DMA((na,))],
        compiler_params=pltpu.CompilerParams(vmem_limit_bytes=VMEM_LIMIT),
    )(*arrs)


AG_SEMS = 7


def _ag_copies(ins, outs, send_sems, recv_sems):
    x, y, c, chips = _position()
    sibling = (x, y, 1 - c)
    xn, yn, dg = [2 * chip[0] + chip[1] for chip in chips]
    to_x, to_y = (1 - x, y, c), (x, 1 - y, c)
    res = []
    for a in range(len(ins)):
        half = ins[a].shape[0] // 2
        quarter = half // 2

        def copy(k, dst, to, src=None, a=a):
            return pltpu.make_async_remote_copy(
                src_ref=dst if src is None else src, dst_ref=dst,
                send_sem=send_sems.at[AG_SEMS * a + k], recv_sem=recv_sems.at[AG_SEMS * a + k],
                device_id=to, device_id_type=MESH)

        def rows(chip, pc, q=None, a=a, half=half, quarter=quarter):
            if q is None:
                return outs[a].at[chip, pl.ds(pc * half, half), :]
            return outs[a].at[chip, pl.ds(pc * half + q * quarter, quarter), :]

        own = ins[a].at[pl.ds(c * half, half), :]
        mine = rows(2 * x + y, c)
        res.append(dict(
            sends=[copy(0, mine, to_x, src=own), copy(1, mine, to_y, src=own)],
            from_x=copy(0, rows(xn, c), to_x), from_y=copy(1, rows(yn, c), to_y),
            relay_y=copy(2, rows(xn, c, 0), to_y), relay_x=copy(3, rows(yn, c, 1), to_x),
            from_y_relay=copy(2, rows(dg, c, 0), to_y), from_x_relay=copy(3, rows(dg, c, 1), to_x),
            pass_on=[copy(4, rows(xn, c), sibling), copy(5, rows(yn, c), sibling), copy(6, rows(dg, c), sibling)],
            from_sibling=[copy(4, rows(xn, 1 - c), sibling), copy(5, rows(yn, 1 - c), sibling),
                          copy(6, rows(dg, 1 - c), sibling)]))
    return res


def _ag_start(ins, outs, send_sems, recv_sems):
    for cps in _ag_copies(ins, outs, send_sems, recv_sems):
        for cp in cps["sends"]:
            cp.start()


def _ag_relay(ins, outs, send_sems, recv_sems, which):
    copies = _ag_copies(ins, outs, send_sems, recv_sems)
    for a in which:
        cps = copies[a]
        cps["from_x"].wait_recv()
        cps["relay_y"].start()
        cps["pass_on"][0].start()
        cps["from_y"].wait_recv()
        cps["relay_x"].start()
        cps["pass_on"][1].start()


def _ag_complete(ins, outs, send_sems, recv_sems):
    copies = _ag_copies(ins, outs, send_sems, recv_sems)
    for cps in copies:
        cps["from_y_relay"].wait_recv()
        cps["from_x_relay"].wait_recv()
        cps["pass_on"][2].start()
    for cps in copies:
        for cp in cps["from_sibling"]:
            cp.wait_recv()
        for cp in cps["sends"] + [cps["relay_y"], cps["relay_x"]] + cps["pass_on"]:
            cp.wait_send()


def _ag_finish(ins, outs, send_sems, recv_sems):
    _ag_relay(ins, outs, send_sems, recv_sems, range(len(ins)))
    _ag_complete(ins, outs, send_sems, recv_sems)


def _allgather_weights(name, collective_id, shards):
    na = len(shards)
    hbm = pltpu.MemorySpace.HBM
    ins = [jax.new_ref(s, memory_space=hbm) for s in shards]
    outs = [jax.empty_ref(jax.ShapeDtypeStruct((N_CHIP,) + s.shape, s.dtype), memory_space=hbm) for s in shards]

    @pl.kernel(mesh=plsc.ScalarSubcoreMesh(axis_name="sequencer", num_cores=1), name=name,
               scratch_types=(pltpu.SemaphoreType.DMA((AG_SEMS * na,)), pltpu.SemaphoreType.DMA((AG_SEMS * na,))),
               compiler_params=pltpu.CompilerParams(collective_id=collective_id))
    def launch(send_sems, recv_sems):
        x, y, c, _ = _position()
        peers = [(1 - x, y, c), (x, 1 - y, c), (x, y, 1 - c)]
        barrier = pltpu.get_barrier_semaphore()
        for peer in peers:
            pl.semaphore_signal(barrier, inc=1, device_id=peer, device_id_type=MESH)
        pl.semaphore_wait(barrier, len(peers))
        _ag_start(ins, outs, send_sems, recv_sems)
        _ag_finish(ins, outs, send_sems, recv_sems)

    launch()
    return [o[...] for o in outs]


def _swap_halves(name, grads, after=()):
    na, nw = len(grads), len(after)

    def body(*refs):
        ins, outs = refs[:na], refs[na + nw:2 * na + nw]
        send_sems, recv_sems = refs[2 * na + nw:]
        x, y, c, _ = _position()
        cps = []
        for a in range(na):
            half = ins[a].shape[1] // 2
            cp = pltpu.make_async_remote_copy(
                src_ref=ins[a].at[:, pl.ds((1 - c) * half, half), :], dst_ref=outs[a],
                send_sem=send_sems.at[a], recv_sem=recv_sems.at[a],
                device_id=(x, y, 1 - c), device_id_type=MESH)
            cp.start()
            cps.append(cp)
        for cp in cps:
            cp.wait()

    return pl.pallas_call(
        body, name=name,
        out_shape=[jax.ShapeDtypeStruct((g.shape[0], g.shape[1] // 2, g.shape[2]), g.dtype) for g in grads],
        in_specs=[ANY] * (na + nw), out_specs=[ANY] * na,
        scratch_shapes=[pltpu.SemaphoreType.DMA((na,)), pltpu.SemaphoreType.DMA((na,))],
    )(*grads, *after)


def _xchg_copies(ins, outs, send_sems, recv_sems):
    x, y, c, chips = _position()
    return [pltpu.make_async_remote_copy(
        src_ref=ins[a].at[2 * chip[0] + chip[1]], dst_ref=outs[a].at[j],
        send_sem=send_sems.at[3 * a + j], recv_sem=recv_sems.at[3 * a + j],
        device_id=(*chip, c), device_id_type=MESH) for a in range(len(ins)) for j, chip in enumerate(chips)]


def _exchange_chips(parts):
    na = len(parts)
    hbm = pltpu.MemorySpace.HBM
    ins = [jax.new_ref(p, memory_space=hbm) for p in parts]
    outs = [jax.empty_ref(jax.ShapeDtypeStruct((3,) + p.shape[1:], p.dtype), memory_space=hbm) for p in parts]

    @pl.kernel(mesh=plsc.ScalarSubcoreMesh(axis_name="sequencer", num_cores=1), name="rs_exchange_chips",
               scratch_types=(pltpu.SemaphoreType.DMA((3 * na,)), pltpu.SemaphoreType.DMA((3 * na,))),
               compiler_params=pltpu.CompilerParams(collective_id=0))
    def launch(send_sems, recv_sems):
        x, y, c, chips = _position()
        barrier = pltpu.get_barrier_semaphore()
        for chip in chips:
            pl.semaphore_signal(barrier, inc=1, device_id=(*chip, c), device_id_type=MESH)
        pl.semaphore_wait(barrier, len(chips))
        for cp in _xchg_copies(ins, outs, send_sems, recv_sems):
            cp.start()
        for cp in _xchg_copies(ins, outs, send_sems, recv_sems):
            cp.wait()

    launch()
    return [q[...] for q in outs]


def _swap_reduced(name, halves):
    na = len(halves)

    def body(*refs):
        ins, outs = refs[:na], refs[na:2 * na]
        send_sems, recv_sems = refs[2 * na:]
        x, y, c, _ = _position()
        cps = []
        for a in range(na):
            cp = pltpu.make_async_remote_copy(
                src_ref=ins[a], dst_ref=outs[a], send_sem=send_sems.at[a], recv_sem=recv_sems.at[a],
                device_id=(x, y, 1 - c), device_id_type=MESH)
            cp.start()
            cps.append(cp)
        for cp in cps:
            cp.wait()

    return pl.pallas_call(
        body, name=name,
        out_shape=[jax.ShapeDtypeStruct(h.shape, h.dtype) for h in halves],
        in_specs=[ANY] * na, out_specs=[ANY] * na,
        scratch_shapes=[pltpu.SemaphoreType.DMA((na,)), pltpu.SemaphoreType.DMA((na,))],
    )(*halves)


def _add_sibling(name, grad, recv, core):
    _, r, c = grad.shape
    half = r // 2
    rb = min(half, 256)
    nrb = half // rb

    def body(core_ref, g_ref, r_ref, o_ref):
        o_ref[...] = (g_ref[...].astype(F32) + r_ref[...].astype(F32)).astype(BF16)

    return pl.pallas_call(
        body, name=name,
        grid_spec=pltpu.PrefetchScalarGridSpec(
            num_scalar_prefetch=1, grid=(N_CHIP, nrb),
            in_specs=[pl.BlockSpec((1, rb, c), lambda j, i, core_ref: (j, core_ref[0] * nrb + i, 0)),
                      pl.BlockSpec((1, rb, c), lambda j, i, core_ref: (j, i, 0))],
            out_specs=pl.BlockSpec((1, rb, c), lambda j, i, core_ref: (j, i, 0))),
        out_shape=jax.ShapeDtypeStruct((N_CHIP, half, c), BF16),
        compiler_params=pltpu.CompilerParams(dimension_semantics=("arbitrary", "arbitrary"),
                                             vmem_limit_bytes=VMEM_LIMIT),
    )(core, grad, recv)


def _add_chips(name, chip, p, q):
    _, half, c = q.shape
    rb = min(half, 256)

    def body(chip_ref, p_ref, q_ref, o_ref):
        acc = p_ref[0].astype(F32)
        for j in range(3):
            acc = acc + q_ref[j].astype(F32)
        o_ref[...] = acc

    return pl.pallas_call(
        body, name=name,
        grid_spec=pltpu.PrefetchScalarGridSpec(
            num_scalar_prefetch=1, grid=(half // rb,),
            in_specs=[pl.BlockSpec((1, rb, c), lambda i, chip_ref: (chip_ref[0], i, 0)),
                      pl.BlockSpec((3, rb, c), lambda i, chip_ref: (0, i, 0))],
            out_specs=pl.BlockSpec((rb, c), lambda i, chip_ref: (i, 0))),
        out_shape=jax.ShapeDtypeStruct((half, c), F32),
        compiler_params=pltpu.CompilerParams(dimension_semantics=("arbitrary",), vmem_limit_bytes=VMEM_LIMIT),
    )(chip, p, q)


def _small_update(gad, gam, gl, gg, mychip, params):
    names = ["ada_b", "norm1_g", "lru_conv_b", "gate_a_w", "gate_a_b", "gate_x_w", "gate_x_b", "a_param",
             "lru_conv_w", "short_conv_w", "lru_out_g", "conv_out_g", "norm2_g", "final_g"]
    flat = [t for n in names for t in params[n]]
    nin = len(flat)

    def body(chip_ref, gad_ref, gam_ref, gl_ref, gg_ref, *refs):
        ins = {n: refs[3 * k:3 * k + 3] for k, n in enumerate(names)}
        outs = {n: refs[nin + 4 * k:nin + 4 * k + 4] for k, n in enumerate(names)}
        loss_ref, dmod_ref = refs[nin + 4 * len(names):nin + 4 * len(names) + 2]

        def dsum(ref, lo, n):
            per = ref.shape[0] // 8
            acc = ref[lo:lo + n, :].astype(F32)
            for dev in range(1, 8):
                acc = acc + ref[dev * per + lo:dev * per + lo + n, :].astype(F32)
            return acc

        def update(n, g):
            w_ref, m_ref, v_ref = ins[n]
            g_ref, d_ref, mo_ref, vo_ref = outs[n]
            g_ref[...] = g
            d_ref[...], mo_ref[...], vo_ref[...] = _adam_math(w_ref[...], g, m_ref[...], v_ref[...])

        d, dm, l, lw = refs[-4:]
        d[...] = dsum(gad_ref, 0, 8)
        dm[...] = dsum(gam_ref, 0, 8)
        l[...] = dsum(gl_ref, 0, 16)
        for dev in range(8):
            for k in range(3):
                dmod_ref[dev:dev + 1, k * D_MODEL:(k + 1) * D_MODEL] = gad_ref[dev * 8 + k:dev * 8 + k + 1, :]
                dmod_ref[dev:dev + 1, (3 + k) * D_MODEL:(4 + k) * D_MODEL] = gam_ref[dev * 8 + k:dev * 8 + k + 1, :]
        w_ref, m_ref, v_ref = ins["ada_b"]
        g_ref, d_ref, mo_ref, vo_ref = outs["ada_b"]
        for k in range(3):
            g_ref[:, k * D_MODEL:(k + 1) * D_MODEL] = d[k:k + 1, :]
            g_ref[:, (3 + k) * D_MODEL:(4 + k) * D_MODEL] = dm[k:k + 1, :]
        d_ref[...], mo_ref[...], vo_ref[...] = _adam_math(w_ref[...], g_ref[...], m_ref[...], v_ref[...])
        update("norm1_g", d[3:4, :])
        update("norm2_g", dm[3:4, :])
        update("final_g", dm[4:5, :])
        update("gate_a_b", d[4:5, 0:D_LRU])
        update("gate_x_b", d[4:5, D_LRU:2 * D_LRU])
        update("lru_conv_b", l[4:5, :])
        update("a_param", l[8:9, :] * jax.nn.sigmoid(ins["a_param"][0][...]))
        update("lru_out_g", l[9:10, :])
        update("conv_out_g", l[10:11, :])
        loss_ref[...] = jnp.broadcast_to(dm[5:6, 0:128], (8, 128))
        chip = chip_ref[0]
        acc = jnp.zeros((8, 128), F32)
        for j in range(N_CHIP):
            acc = acc + jnp.where(chip == j, l[0:8, j * 128:(j + 1) * 128], 0.0)
        lw[...] = acc
        update("lru_conv_w", lw[0:4, :])
        update("short_conv_w", lw[5:8, :])
        gates = dsum(gg_ref, 0, D_LRU)
        update("gate_a_w", gates[:, 0:HEAD])
        update("gate_x_w", gates[:, HEAD:2 * HEAD])

    out_shape = []
    for n in names:
        out_shape += [jax.ShapeDtypeStruct(params[n][0].shape, F32)] * 4
    out_shape += [jax.ShapeDtypeStruct((8, 128), F32), jax.ShapeDtypeStruct((8, 6 * D_MODEL), F32)]
    res = pl.pallas_call(
        body, name="small_update", out_shape=out_shape,
        in_specs=[SMEM] + [VMEM] * (4 + nin),
        out_specs=[VMEM] * len(out_shape),
        scratch_shapes=[pltpu.VMEM((8, D_MODEL), F32), pltpu.VMEM((8, D_MODEL), F32), pltpu.VMEM((16, D_LRU), F32),
                        pltpu.VMEM((8, 128), F32)],
        compiler_params=pltpu.CompilerParams(vmem_limit_bytes=VMEM_LIMIT),
    )(mychip, gad, gam, gl, gg, *flat)
    per = {n: res[4 * k:4 * k + 4] for k, n in enumerate(names)}
    return per, res[-2], res[-1]


def _block_diag(w):
    eye = jnp.eye(8, dtype=w.dtype)
    return (eye[:, None, :, None] * w[:, :, None, :]).reshape(8 * HEAD, 8 * HEAD)


def _diag_blocks(g):
    return jnp.concatenate([g[h * HEAD:(h + 1) * HEAD, h * HEAD:(h + 1) * HEAD] for h in range(8)], axis=0)


def kernel(x, c, ada_w, ada_b, norm1_g, w_in, lru_conv_w, lru_conv_b, gate_a_w, gate_a_b, gate_x_w, gate_x_b, a_param, short_conv_w, lru_out_g, conv_out_g, w_out, norm2_g, w_mlp1, w_mlp2, final_g, loss_target, m_ada_w, m_ada_b, m_norm1_g, m_w_in, m_lru_conv_w, m_lru_conv_b, m_gate_a_w, m_gate_a_b, m_gate_x_w, m_gate_x_b, m_a_param, m_short_conv_w, m_lru_out_g, m_conv_out_g, m_w_out, m_norm2_g, m_w_mlp1, m_w_mlp2, m_final_g, v_ada_w, v_ada_b, v_norm1_g, v_w_in, v_lru_conv_w, v_lru_conv_b, v_gate_a_w, v_gate_a_b, v_gate_x_w, v_gate_x_b, v_a_param, v_short_conv_w, v_lru_out_g, v_conv_out_g, v_w_out, v_norm2_g, v_w_mlp1, v_w_mlp2, v_final_g):
    xi, yi, ci = lax.axis_index("x"), lax.axis_index("y"), lax.axis_index("c")
    mychip = 2 * xi + yi
    me = 4 * xi + 2 * yi + ci

    own_in, own_out = w_in[0].astype(BF16), w_out[0].astype(BF16)
    win_all, wout_all = _allgather_weights("allgather_mixer_weights", 1, [own_in, own_out])
    own_w1, own_w2 = w_mlp1[0].astype(BF16), w_mlp2[0].astype(BF16)
    w1_all, w2_all = _allgather_weights("allgather_mlp_weights", 2, [own_w1, own_w2])

    c_blk = jnp.zeros((8, D_MODEL), F32).at[0:1].set(c)
    cw_blk = jnp.zeros((8, 128), F32).at[0:4].set(lru_conv_w[0]).at[4:7].set(short_conv_w[0])
    c_g, cw_g = _allgather8("allgather_cond", [c_blk, cw_blk])
    c_all = c_g.reshape(8, 8, D_MODEL)[:, 0]
    cw_g = cw_g.reshape(4, 2, 8, 128)[:, 0]
    lcw = cw_g[:, 0:4].transpose(1, 0, 2).reshape(4, D_LRU)
    scw = cw_g[:, 4:7].transpose(1, 0, 2).reshape(3, D_LRU)

    mod_loc = _mod_matmul(c_all, ada_w[0])
    (mod_g,) = _allgather8("allgather_mod", [mod_loc])
    mod_all = mod_g.reshape(4, 2, 8, 6 * D_MODEL // 4)[:, 0].transpose(1, 0, 2).reshape(8, 6 * D_MODEL) + ada_b
    mod_pad = jnp.pad(mod_all.reshape(8, 6, D_MODEL), ((0, 0), (0, 2), (0, 0)))
    mod = lax.dynamic_slice_in_dim(mod_pad, me, 1, axis=0).reshape(8, D_MODEL)

    win, wout = (win_all, own_in), (wout_all, own_out)
    chip = mychip.reshape(1).astype(jnp.int32)
    core = ci.reshape(1).astype(jnp.int32)

    vecd = jnp.concatenate([norm1_g, norm2_g, final_g[None, :], jnp.concatenate([gate_a_b, gate_x_b], axis=1),
                            jnp.zeros((4, D_MODEL), F32)], axis=0)
    vecl = jnp.concatenate([lcw, lru_conv_b, scw, a_param, lru_out_g, conv_out_g, jnp.zeros((5, D_LRU), F32)], axis=0)
    gab = jnp.concatenate([_block_diag(gate_a_w[0]), _block_diag(gate_x_w[0])], axis=1).astype(BF16)
    a64 = _block_diag(jnp.full((8, HEAD, HEAD), 1.0 / HEAD, F32)).astype(BF16)

    hb, proj, hl, ycat, mixed, x1 = _mix_fwd(chip, x[0], mod, vecd, vecl, win, wout, gab, a64)
    dx1, act, dz, dmo, h2b, accm = _mlp_fwd_bwd(
        chip, x1, loss_target[0], mod, vecd, (w1_all, own_w1), (w2_all, own_w2))

    def sibling_sum(tag, grads, after=()):
        recv = _swap_halves("rs_swap_halves_" + tag, grads, after)
        return [_add_sibling("rs_add_sibling_%s%d" % (tag, k), g, r, core) for k, (g, r) in enumerate(zip(grads, recv))]

    parts_mlp = sibling_sum("mlp", [_wgrad("wgrad_mlp1", h2b, dz, 0, FF_BLK, BF16),
                                    _wgrad("wgrad_mlp2", act, dmo, FF_BLK, 0, BF16)])
    grad_x, accd, accl, g_win, g_wout, g_gate, q_w1, q_w2 = _mix_bwd(
        chip, dx1, x[0], mixed, proj, hl, hb, ycat, mod, vecd, vecl, win, wout, gab, a64, parts_mlp)

    gg_blk = jnp.concatenate([_diag_blocks(g_gate[:, 0:D_LRU]), _diag_blocks(g_gate[:, D_LRU:2 * D_LRU])], axis=1)
    gad, gam, gl, gg = _allgather8("allgather_small_grads", [accd, accm, accl, gg_blk.astype(BF16)])
    parts_mix = sibling_sum("mix", [g_win, g_wout.reshape(N_CHIP, WOUT_BLK, D_MODEL)], after=[gad])

    def reduced(tag, parts, landed):
        own = [_add_chips("rs_add_chips_%s%d" % (tag, k), chip, p, q) for k, (p, q) in enumerate(zip(parts, landed))]
        return own, _swap_reduced("rs_swap_reduced_" + tag, own)

    landed_mix = _exchange_chips(parts_mix)
    own_mlp, sib_mlp = reduced("mlp", parts_mlp, [q_w1, q_w2])
    res_w1, res_w2 = _adam("adam_mlp", core, [(w_mlp1[0], own_mlp[0], sib_mlp[0], m_w_mlp1[0], v_w_mlp1[0]),
                                              (w_mlp2[0], own_mlp[1], sib_mlp[1], m_w_mlp2[0], v_w_mlp2[0])])
    own_mix, sib_mix = reduced("mix", parts_mix, landed_mix)
    (res_win,) = _adam("adam_w_in", core, [(w_in[0], own_mix[0], sib_mix[0], m_w_in[0], v_w_in[0])])
    (res_wout,) = _adam("adam_w_out", core, [(w_out[0], own_mix[1], sib_mix[1], m_w_out[0], v_w_out[0])])

    params = {
        "ada_b": (ada_b, m_ada_b, v_ada_b), "norm1_g": (norm1_g, m_norm1_g, v_norm1_g),
        "lru_conv_b": (lru_conv_b, m_lru_conv_b, v_lru_conv_b),
        "gate_a_w": tuple(t.reshape(D_LRU, HEAD) for t in (gate_a_w, m_gate_a_w, v_gate_a_w)),
        "gate_a_b": (gate_a_b, m_gate_a_b, v_gate_a_b),
        "gate_x_w": tuple(t.reshape(D_LRU, HEAD) for t in (gate_x_w, m_gate_x_w, v_gate_x_w)),
        "gate_x_b": (gate_x_b, m_gate_x_b, v_gate_x_b), "a_param": (a_param, m_a_param, v_a_param),
        "lru_conv_w": tuple(t[0] for t in (lru_conv_w, m_lru_conv_w, v_lru_conv_w)),
        "short_conv_w": tuple(t[0] for t in (short_conv_w, m_short_conv_w, v_short_conv_w)),
        "lru_out_g": (lru_out_g, m_lru_out_g, v_lru_out_g), "conv_out_g": (conv_out_g, m_conv_out_g, v_conv_out_g),
        "norm2_g": (norm2_g, m_norm2_g, v_norm2_g),
        "final_g": tuple(t[None, :] for t in (final_g, m_final_g, v_final_g)),
    }
    small, loss_blk, dmod_cols = _small_update(gad, gam, gl, gg, chip, params)
    loss = loss_blk[0, 0]

    ncol = 6 * D_MODEL // N_CHIP
    dmod_loc = lax.dynamic_slice_in_dim(dmod_cols, mychip * ncol, ncol, axis=1)
    sct = (c_all * jax.nn.sigmoid(c_all)).T
    ada = _ada_grad_adam(sct, dmod_loc, ada_w[0], m_ada_w[0], v_ada_w[0])

    res = {"ada_w": ada, "w_in": res_win, "w_out": res_wout, "w_mlp1": res_w1, "w_mlp2": res_w2}
    res = {n: tuple(t[None] for t in r) for n, r in res.items()}
    shapes = {"gate_a_w": gate_a_w.shape, "gate_x_w": gate_x_w.shape, "lru_conv_w": lru_conv_w.shape,
              "short_conv_w": short_conv_w.shape, "final_g": final_g.shape}
    for n, t in small.items():
        res[n] = tuple(u.reshape(shapes[n]) if n in shapes else u for u in t)

    order = ["ada_w", "ada_b", "norm1_g", "w_in", "lru_conv_w", "lru_conv_b", "gate_a_w", "gate_a_b", "gate_x_w",
             "gate_x_b", "a_param", "short_conv_w", "lru_out_g", "conv_out_g", "w_out", "norm2_g", "w_mlp1",
             "w_mlp2", "final_g"]
    return (loss, grad_x[None], *[res[n][0] for n in order], *[res[n][1] for n in order],
            *[res[n][2] for n in order], *[res[n][3] for n in order])
```

```python
import jax
import jax.numpy as jnp
from jax import lax
from jax.experimental import pallas as pl
from jax.experimental.pallas import tpu as pltpu
from jax.experimental.pallas import tpu_sc as plsc

F32 = jnp.float32
BF16 = jnp.bfloat16

D_MODEL = 1024
D_LRU = 512
D_IN = 2560
D_FF = 4096
N_CHIP = 4
WIN_BLK = D_IN // N_CHIP
WOUT_BLK = D_MODEL // N_CHIP
FF_BLK = D_FF // N_CHIP
HEAD = 64
EPS = 1e-6
C_GATE = 8.0
TOKEN_TILE = 256
HALO = 8
VMEM_LIMIT = 60 * 1024 * 1024

ADAM_LR = 0.001
ADAM_B1 = 0.9
ADAM_B2 = 0.999
ADAM_EPS = 1e-08
ADAM_WD = 0.01
ADAM_STEP = 10

MESH = pl.DeviceIdType.MESH
ANY = pl.BlockSpec(memory_space=pl.ANY)
VMEM = pl.BlockSpec(memory_space=pltpu.VMEM)
SMEM = pl.BlockSpec(memory_space=pltpu.SMEM)


def _full(shape, single=False):
    nd = len(shape)
    if single:
        return pl.BlockSpec(shape, lambda *_: (0,) * nd, pipeline_mode=pl.Buffered(1))
    return pl.BlockSpec(shape, lambda *_: (0,) * nd)


def _dot(a, b):
    return jnp.dot(a, b, preferred_element_type=F32)


def _dot_nt(a, b):
    return lax.dot_general(a, b, (((1,), (1,)), ((), ())), preferred_element_type=F32)


def _dot_tn(a, b):
    return lax.dot_general(a, b, (((0,), (0,)), ((), ())), preferred_element_type=F32)


def _gmean(v, a64):
    hi = v.astype(BF16)
    lo = (v - hi.astype(F32)).astype(BF16)
    return _dot(hi, a64) + _dot(lo, a64)


def _gelu(x):
    u = 0.7978845608028654 * (x + 0.044715 * x * x * x)
    t = jnp.tanh(u)
    return 0.5 * x * (1.0 + t), t


def _gelu_grad(x, t):
    du = 0.7978845608028654 * (1.0 + 3.0 * 0.044715 * x * x)
    return 0.5 * (1.0 + t) + 0.5 * x * (1.0 - t * t) * du


def _log1p_pos(y):
    return jnp.where(y < 1e-2, y * (1.0 - y * (0.5 - y * (1.0 / 3.0 - y * 0.25))), jnp.log(1.0 + y))


def _softplus(a):
    return jnp.maximum(a, 0.0) + _log1p_pos(jnp.exp(-jnp.abs(a)))


def _neg_expm1(z):
    series = -z * (1.0 + z * (0.5 + z * (1.0 / 6.0 + z * (1.0 / 24.0 + z * (1.0 / 120.0)))))
    return jnp.where(z > -0.02, series, 1.0 - jnp.exp(z))


def _scan_fwd(a, b, row):
    n = a.shape[0]
    d = 1
    while d < n:
        m = row >= d
        b = jnp.where(m, a * pltpu.roll(b, d, 0) + b, b)
        a = jnp.where(m, a * pltpu.roll(a, d, 0), a)
        d *= 2
    return a, b


def _scan_rev(a, b, row):
    n = a.shape[0]
    d = 1
    while d < n:
        m = row < n - d
        b = jnp.where(m, b + a * pltpu.roll(b, n - d, 0), b)
        a = jnp.where(m, a * pltpu.roll(a, n - d, 0), a)
        d *= 2
    return a, b


def _colsum(v):
    return jnp.sum(v, axis=0, keepdims=True)


def _load_gathered(chip, gathered, own, slot, sems):
    copies = []
    for j in range(N_CHIP):
        @pl.when(chip == j)
        def _(j=j):
            pltpu.make_async_copy(own, slot(j), sems.at[j]).start()

        @pl.when(chip != j)
        def _(j=j):
            pltpu.make_async_copy(gathered.at[j], slot(j), sems.at[j]).start()

        copies.append(pltpu.make_async_copy(own, slot(j), sems.at[j]))
    return copies


def _lru_gates(xlb, gab, gbias, sp, first_row):
    g = _dot(xlb, gab) + gbias
    r = jax.nn.sigmoid(g[:, :D_LRU])
    ig = jax.nn.sigmoid(g[:, D_LRU:])
    la = (-C_GATE) * r * sp
    a = jnp.exp(la)
    msq = jnp.sqrt(_neg_expm1(2.0 * la))
    mult = jnp.where(first_row, 1.0, msq)
    return r, ig, a, msq, mult


def _mix_fwd(chip, x, mod, vecd, vecl, win, wout, gab, a64):
    s = x.shape[0]
    ts = TOKEN_TILE
    nt = s // ts

    def body(chip_ref, x_ref, mod_ref, vd_ref, vl_ref, win_hbm, win_own, wout_hbm, wout_own, gab_ref, a64_ref,
             hb_ref, proj_ref, hl_ref, ycat_ref, mixed_ref, x1_ref,
             win_ref, wout_ref, ext_lx, ext_cv, hcar, sems):
        i = pl.program_id(0)

        @pl.when(i == 0)
        def _():
            cps = _load_gathered(chip_ref[0], win_hbm, win_own, lambda j: win_ref.at[j], sems.at[pl.ds(0, N_CHIP)])
            cps += _load_gathered(chip_ref[0], wout_hbm, wout_own,
                                  lambda j: wout_ref.at[pl.ds(j * WOUT_BLK, WOUT_BLK), :],
                                  sems.at[pl.ds(N_CHIP, N_CHIP)])
            ext_lx[0:HALO, :] = jnp.zeros((HALO, D_LRU), F32)
            ext_cv[0:HALO, :] = jnp.zeros((HALO, D_LRU), F32)
            hcar[...] = jnp.zeros_like(hcar)
            for cp in cps:
                cp.wait()

        row = lax.broadcasted_iota(jnp.int32, (ts, D_LRU), 0)
        first_row = jnp.logical_and(row == 0, i == 0)
        xt = x_ref[...]
        shift1, scale1, gate1 = mod_ref[0:1, :], mod_ref[1:2, :], mod_ref[2:3, :]
        r1 = lax.rsqrt(jnp.mean(xt * xt, axis=-1, keepdims=True) + EPS)
        h = (xt * r1) * vd_ref[0:1, :] * (1.0 + scale1) + shift1
        hb = h.astype(BF16)
        hb_ref[...] = hb
        for j in range(N_CHIP):
            proj_ref[:, j * WIN_BLK:(j + 1) * WIN_BLK] = _dot(hb, win_ref[j])
        u_ly = proj_ref[:, 512:1024]
        u_b = proj_ref[:, 1024:1536]

        ext_lx[HALO:HALO + ts, :] = proj_ref[:, 0:512]
        xl = vl_ref[4:5, :] + vl_ref[0:1, :] * ext_lx[pl.ds(5, ts), :]
        for k in range(1, 4):
            xl = xl + vl_ref[k:k + 1, :] * ext_lx[pl.ds(5 + k, ts), :]
        ext_lx[0:HALO, :] = ext_lx[ts:ts + HALO, :]
        sp = _softplus(vl_ref[8:9, :])
        _, ig, a, _, mult = _lru_gates(xl.astype(BF16), gab_ref[...], vd_ref[3:4, :], sp, first_row)
        acum, hloc = _scan_fwd(a, mult * (ig * xl), row)
        hl = hloc + acum * hcar[0:1, :]
        hl_ref[...] = hl
        hcar[0:1, :] = hl_ref[ts - 1:ts, :]
        ge, _ = _gelu(u_ly)
        p = ge * hl
        y_lru = p * lax.rsqrt(_gmean(p * p, a64_ref[...]) + EPS) * vl_ref[9:10, :]
        ycat_ref[:, 0:512] = y_lru.astype(BF16)

        ext_cv[HALO:HALO + ts, :] = proj_ref[:, 1536:2048] * proj_ref[:, 2048:2560]
        q = vl_ref[5:6, :] * ext_cv[pl.ds(6, ts), :]
        for k in range(1, 3):
            q = q + vl_ref[5 + k:6 + k, :] * ext_cv[pl.ds(6 + k, ts), :]
        ext_cv[0:HALO, :] = ext_cv[ts:ts + HALO, :]
        yc = u_b * q
        y_conv = yc * lax.rsqrt(_gmean(yc * yc, a64_ref[...]) + EPS) * vl_ref[10:11, :]
        ycat_ref[:, 512:1024] = y_conv.astype(BF16)

        mixed = _dot(ycat_ref[...], wout_ref[...])
        mixed_ref[...] = mixed
        x1_ref[...] = xt + gate1 * mixed

    tile = lambda w: pl.BlockSpec((ts, w), lambda i: (i, 0))
    return pl.pallas_call(
        body, name="mix_fwd", grid=(nt,),
        in_specs=[SMEM, tile(D_MODEL), _full((8, D_MODEL)), _full((8, D_MODEL)), _full((16, D_LRU)),
                  ANY, ANY, ANY, ANY, _full((D_LRU, 2 * D_LRU), True), _full((D_LRU, D_LRU), True)],
        out_specs=[tile(D_MODEL), tile(D_IN), tile(D_LRU), tile(D_MODEL), tile(D_MODEL), tile(D_MODEL)],
        out_shape=[jax.ShapeDtypeStruct((s, D_MODEL), BF16), jax.ShapeDtypeStruct((s, D_IN), F32),
                   jax.ShapeDtypeStruct((s, D_LRU), F32), jax.ShapeDtypeStruct((s, D_MODEL), BF16),
                   jax.ShapeDtypeStruct((s, D_MODEL), F32), jax.ShapeDtypeStruct((s, D_MODEL), F32)],
        scratch_shapes=[pltpu.VMEM((N_CHIP, D_MODEL, WIN_BLK), BF16), pltpu.VMEM((D_MODEL, D_MODEL), BF16),
                        pltpu.VMEM((ts + HALO, D_LRU), F32), pltpu.VMEM((ts + HALO, D_LRU), F32),
                        pltpu.VMEM((HALO, D_LRU), F32), pltpu.SemaphoreType.DMA((2 * N_CHIP,))],
        compiler_params=pltpu.CompilerParams(dimension_semantics=("arbitrary",), vmem_limit_bytes=VMEM_LIMIT),
    )(chip, x, mod, vecd, vecl, *win, *wout, gab, a64)


def _mlp_fwd_bwd(chip, x1, target, mod, vecd, w1, w2):
    s = x1.shape[0]
    ts = TOKEN_TILE
    nt = s // ts

    def body(chip_ref, x1_ref, tg_ref, mod_ref, vd_ref, w1_hbm, w1_own, w2_hbm, w2_own,
             dx1_ref, act_ref, dz_ref, dmo_ref, h2_ref, acc_ref, w1_v, w2_v, rz_v, sems):
        i = pl.program_id(0)

        @pl.when(i == 0)
        def _():
            cps = _load_gathered(chip_ref[0], w1_hbm, w1_own, lambda j: w1_v.at[j], sems.at[pl.ds(0, N_CHIP)])
            cps += _load_gathered(chip_ref[0], w2_hbm, w2_own, lambda j: w2_v.at[j], sems.at[pl.ds(N_CHIP, N_CHIP)])
            acc_ref[...] = jnp.zeros_like(acc_ref)
            for cp in cps:
                cp.wait()

        xt = x1_ref[...]
        shift2, scale2, gate2 = mod_ref[3:4, :], mod_ref[4:5, :], mod_ref[5:6, :]
        g2, gf = vd_ref[1:2, :], vd_ref[2:3, :]
        r2 = lax.rsqrt(jnp.mean(xt * xt, axis=-1, keepdims=True) + EPS)
        n2 = xt * r2
        h2b = (n2 * g2 * (1.0 + scale2) + shift2).astype(BF16)
        h2_ref[...] = h2b
        for j in range(N_CHIP):
            rz_v[j] = jnp.maximum(_dot(h2b, w1_v[j]), 0.0)
        mo = jnp.zeros((ts, D_MODEL), F32)
        for j in range(N_CHIP):
            rz = rz_v[j]
            actb = (rz * rz).astype(BF16)
            act_ref[:, j * FF_BLK:(j + 1) * FF_BLK] = actb
            mo = mo + _dot(actb, w2_v[j])
        x2 = xt + gate2 * mo
        r3 = lax.rsqrt(jnp.mean(x2 * x2, axis=-1, keepdims=True) + EPS)
        n3 = x2 * r3
        e = n3 * gf - tg_ref[...]
        loss = (0.5 / D_MODEL) * jnp.sum(_colsum(e * e), axis=1, keepdims=True)
        dy = e * (1.0 / D_MODEL)
        acc_ref[4:5, :] += _colsum(dy * n3)
        acc_ref[5:6, :] += jnp.broadcast_to(loss, (1, D_MODEL))
        dn3 = dy * gf
        dx2 = r3 * (dn3 - n3 * jnp.mean(dn3 * n3, axis=-1, keepdims=True))
        acc_ref[2:3, :] += _colsum(dx2 * mo)
        dmob = (dx2 * gate2).astype(BF16)
        dmo_ref[...] = dmob
        for j in range(N_CHIP):
            dz_ref[:, j * FF_BLK:(j + 1) * FF_BLK] = (_dot_nt(dmob, w2_v[j]) * (2.0 * rz_v[j])).astype(BF16)
        dh2 = jnp.zeros((ts, D_MODEL), F32)
        for j in range(N_CHIP):
            dh2 = dh2 + _dot_nt(dz_ref[:, j * FF_BLK:(j + 1) * FF_BLK], w1_v[j])
        acc_ref[1:2, :] += _colsum(dh2 * (n2 * g2))
        acc_ref[0:1, :] += _colsum(dh2)
        dhn2 = dh2 * (1.0 + scale2)
        acc_ref[3:4, :] += _colsum(dhn2 * n2)
        dn2 = dhn2 * g2
        dx1_ref[...] = dx2 + r2 * (dn2 - n2 * jnp.mean(dn2 * n2, axis=-1, keepdims=True))

    tile = lambda w: pl.BlockSpec((ts, w), lambda i: (i, 0))
    return pl.pallas_call(
        body, name="mlp_fwd_bwd", grid=(nt,),
        in_specs=[SMEM, tile(D_MODEL), tile(D_MODEL), _full((8, D_MODEL)), _full((8, D_MODEL)), ANY, ANY, ANY, ANY],
        out_specs=[tile(D_MODEL), tile(D_FF), tile(D_FF), tile(D_MODEL), tile(D_MODEL), _full((8, D_MODEL))],
        out_shape=[jax.ShapeDtypeStruct((s, D_MODEL), F32), jax.ShapeDtypeStruct((s, D_FF), BF16),
                   jax.ShapeDtypeStruct((s, D_FF), BF16), jax.ShapeDtypeStruct((s, D_MODEL), BF16),
                   jax.ShapeDtypeStruct((s, D_MODEL), BF16), jax.ShapeDtypeStruct((8, D_MODEL), F32)],
        scratch_shapes=[pltpu.VMEM((N_CHIP, D_MODEL, FF_BLK), BF16), pltpu.VMEM((N_CHIP, FF_BLK, D_MODEL), BF16),
                        pltpu.VMEM((N_CHIP, ts, FF_BLK), F32), pltpu.SemaphoreType.DMA((2 * N_CHIP,))],
        compiler_params=pltpu.CompilerParams(dimension_semantics=("arbitrary",), vmem_limit_bytes=VMEM_LIMIT),
    )(chip, x1, target, mod, vecd, *w1, *w2)


def _mix_bwd(chip, dx1, x, mixed, proj, hl, hb, ycat, mod, vecd, vecl, win, wout, gab, a64):
    s = x.shape[0]
    ts = TOKEN_TILE
    nt = s // ts
    hpt = ts // HALO

    def body(chip_ref, dx1_ref, x_ref, mixed_ref, proj_ref, projh_ref, hl_ref, hlh_ref, hb_ref, ycat_ref,
             mod_ref, vd_ref, vl_ref, win_hbm, win_own, wout_hbm, wout_own, gab_ref, a64_ref,
             gx_ref, accd_ref, accl_ref, gwin_hbm, gwout_hbm, ggate_hbm,
             win_ref, wout_ref, dproj_ref, dgb_ref, gwin_acc, gwout_acc, ggate_acc,
             ext_lx, ext_cv, ext_hl, ext_dxl, ext_dq, gbuf, gcar, acar, sems):
        i = pl.program_id(0)
        ri = nt - 1 - i

        @pl.when(i == 0)
        def _():
            gwin_acc[...] = jnp.zeros_like(gwin_acc)
            gwout_acc[...] = jnp.zeros_like(gwout_acc)
            ggate_acc[...] = jnp.zeros_like(ggate_acc)
            cps = _load_gathered(chip_ref[0], win_hbm, win_own, lambda j: win_ref.at[j], sems.at[pl.ds(0, N_CHIP)])
            cps += _load_gathered(chip_ref[0], wout_hbm, wout_own,
                                  lambda j: wout_ref.at[pl.ds(j * WOUT_BLK, WOUT_BLK), :],
                                  sems.at[pl.ds(N_CHIP, N_CHIP)])
            for cp in cps:
                cp.wait()
            accd_ref[...] = jnp.zeros_like(accd_ref)
            accl_ref[...] = jnp.zeros_like(accl_ref)
            ext_dxl[ts:ts + HALO, :] = jnp.zeros((HALO, D_LRU), F32)
            ext_dq[ts:ts + HALO, :] = jnp.zeros((HALO, D_LRU), F32)
            gcar[...] = jnp.zeros_like(gcar)
            acar[...] = jnp.zeros_like(acar)

        row = lax.broadcasted_iota(jnp.int32, (ts, D_LRU), 0)
        first_row = jnp.logical_and(row == 0, ri == 0)
        halo_on = jnp.where(ri == 0, 0.0, 1.0)
        shift1, scale1, gate1 = mod_ref[0:1, :], mod_ref[1:2, :], mod_ref[2:3, :]
        g1 = vd_ref[0:1, :]
        a64m = a64_ref[...]
        lg, cg = vl_ref[9:10, :], vl_ref[10:11, :]

        dx1 = dx1_ref[...]
        accd_ref[2:3, :] += _colsum(dx1 * mixed_ref[...])
        dmb = (dx1 * gate1).astype(BF16)
        gwout_acc[...] += _dot_tn(ycat_ref[...], dmb)
        dycat = _dot_nt(dmb, wout_ref[...])
        dyl = dycat[:, 0:512]
        dyv = dycat[:, 512:1024]

        u_ly = proj_ref[:, 512:1024]
        u_b = proj_ref[:, 1024:1536]
        u_c = proj_ref[:, 1536:2048]
        u_v = proj_ref[:, 2048:2560]
        ext_lx[0:HALO, :] = projh_ref[:, 0:512] * halo_on
        ext_lx[HALO:HALO + ts, :] = proj_ref[:, 0:512]
        xl = vl_ref[4:5, :] + vl_ref[0:1, :] * ext_lx[pl.ds(5, ts), :]
        for k in range(1, 4):
            xl = xl + vl_ref[k:k + 1, :] * ext_lx[pl.ds(5 + k, ts), :]
        xlb = xl.astype(BF16)
        sp = _softplus(vl_ref[8:9, :])
        r, ig, a, msq, mult = _lru_gates(xlb, gab_ref[...], vd_ref[3:4, :], sp, first_row)
        hl = hl_ref[...]
        ge, th = _gelu(u_ly)
        p = ge * hl
        rl = lax.rsqrt(_gmean(p * p, a64m) + EPS)
        nl = p * rl
        ext_cv[0:HALO, :] = projh_ref[:, 1536:2048] * projh_ref[:, 2048:2560] * halo_on
        ext_cv[HALO:HALO + ts, :] = u_c * u_v
        q = vl_ref[5:6, :] * ext_cv[pl.ds(6, ts), :]
        for k in range(1, 3):
            q = q + vl_ref[5 + k:6 + k, :] * ext_cv[pl.ds(6 + k, ts), :]
        yc = u_b * q
        rc = lax.rsqrt(_gmean(yc * yc, a64m) + EPS)
        nc = yc * rc

        accl_ref[9:10, :] += _colsum(dyl * nl)
        dnl = dyl * lg
        dp = rl * (dnl - nl * _gmean(dnl * nl, a64m))
        dproj_ref[:, 512:1024] = ((dp * hl) * _gelu_grad(u_ly, th)).astype(BF16)
        a_next = jnp.where(row == ts - 1, acar[0:1, :], pltpu.roll(a, ts - 1, 0))
        acum, gloc = _scan_rev(a_next, dp * ge, row)
        gbuf[...] = gloc + acum * gcar[0:1, :]
        gcar[0:1, :] = gbuf[0:1, :]
        ext_hl[0:HALO, :] = hlh_ref[...] * halo_on
        ext_hl[HALO:HALO + ts, :] = hl
        acar[...] = a[0:HALO, :]
        gt = gbuf[...]
        da = gt * ext_hl[pl.ds(HALO - 1, ts), :]
        dmult = gt * ig * xl
        di = gt * mult * xl
        dxl = gt * mult * ig
        dla = da * a - jnp.where(first_row, 0.0, dmult * a * a / msq)
        accl_ref[8:9, :] += _colsum(dla * ((-C_GATE) * r))
        dra = dla * ((-C_GATE) * sp) * r * (1.0 - r)
        dia = di * ig * (1.0 - ig)
        accd_ref[4:5, 0:D_LRU] += _colsum(dra)
        accd_ref[4:5, D_LRU:2 * D_LRU] += _colsum(dia)
        dgb_ref[:, 0:D_LRU] = dra.astype(BF16)
        dgb_ref[:, D_LRU:2 * D_LRU] = dia.astype(BF16)
        dxl = dxl + _dot_nt(dgb_ref[...], gab_ref[...])
        ggate_acc[...] += _dot_tn(xlb, dgb_ref[...])
        accl_ref[4:5, :] += _colsum(dxl)
        for k in range(4):
            accl_ref[k:k + 1, :] += _colsum(dxl * ext_lx[pl.ds(5 + k, ts), :])
        ext_dxl[0:ts, :] = dxl
        du_lx = vl_ref[0:1, :] * ext_dxl[pl.ds(3, ts), :]
        for k in range(1, 4):
            du_lx = du_lx + vl_ref[k:k + 1, :] * ext_dxl[pl.ds(3 - k, ts), :]
        ext_dxl[ts:ts + HALO, :] = ext_dxl[0:HALO, :]
        dproj_ref[:, 0:512] = du_lx.astype(BF16)

        accl_ref[10:11, :] += _colsum(dyv * nc)
        dnc = dyv * cg
        dyc = rc * (dnc - nc * _gmean(dnc * nc, a64m))
        dproj_ref[:, 1024:1536] = (dyc * q).astype(BF16)
        dq = dyc * u_b
        for k in range(3):
            accl_ref[5 + k:6 + k, :] += _colsum(dq * ext_cv[pl.ds(6 + k, ts), :])
        ext_dq[0:ts, :] = dq
        dcv = vl_ref[5:6, :] * ext_dq[pl.ds(2, ts), :]
        for k in range(1, 3):
            dcv = dcv + vl_ref[5 + k:6 + k, :] * ext_dq[pl.ds(2 - k, ts), :]
        ext_dq[ts:ts + HALO, :] = ext_dq[0:HALO, :]
        dproj_ref[:, 1536:2048] = (dcv * u_v).astype(BF16)
        dproj_ref[:, 2048:2560] = (dcv * u_c).astype(BF16)

        dh = _dot_nt(dproj_ref[:, 0:WIN_BLK], win_ref[0])
        for j in range(1, N_CHIP):
            dh = dh + _dot_nt(dproj_ref[:, j * WIN_BLK:(j + 1) * WIN_BLK], win_ref[j])
        for j in range(N_CHIP):
            gwin_acc[j] += _dot_tn(hb_ref[...], dproj_ref[:, j * WIN_BLK:(j + 1) * WIN_BLK])
        xt = x_ref[...]
        r1 = lax.rsqrt(jnp.mean(xt * xt, axis=-1, keepdims=True) + EPS)
        n1 = xt * r1
        accd_ref[1:2, :] += _colsum(dh * (n1 * g1))
        accd_ref[0:1, :] += _colsum(dh)
        dhn1 = dh * (1.0 + scale1)
        accd_ref[3:4, :] += _colsum(dhn1 * n1)
        dn1 = dhn1 * g1
        gx_ref[...] = dx1 + r1 * (dn1 - n1 * jnp.mean(dn1 * n1, axis=-1, keepdims=True))

        @pl.when(i == nt - 1)
        def _():
            outs = [pltpu.make_async_copy(acc, dst, sems.at[k]) for k, (acc, dst) in enumerate(
                ((gwin_acc, gwin_hbm), (gwout_acc, gwout_hbm), (ggate_acc, ggate_hbm)))]
            for cp in outs:
                cp.start()
            for cp in outs:
                cp.wait()

    tile = lambda w: pl.BlockSpec((ts, w), lambda i: (nt - 1 - i, 0))
    halo = lambda w: pl.BlockSpec((HALO, w), lambda i: (jnp.maximum((nt - 1 - i) * hpt - 1, 0), 0))
    ext = pltpu.VMEM((ts + HALO, D_LRU), F32)
    return pl.pallas_call(
        body, name="mix_bwd", grid=(nt,),
        in_specs=[SMEM, tile(D_MODEL), tile(D_MODEL), tile(D_MODEL), tile(D_IN), halo(D_IN), tile(D_LRU), halo(D_LRU),
                  tile(D_MODEL), tile(D_MODEL), _full((8, D_MODEL)), _full((8, D_MODEL)), _full((16, D_LRU)),
                  ANY, ANY, ANY, ANY, _full((D_LRU, 2 * D_LRU), True), _full((D_LRU, D_LRU), True)],
        out_specs=[tile(D_MODEL), _full((8, D_MODEL)), _full((16, D_LRU)), ANY, ANY, ANY],
        out_shape=[jax.ShapeDtypeStruct((s, D_MODEL), F32),
                   jax.ShapeDtypeStruct((8, D_MODEL), F32), jax.ShapeDtypeStruct((16, D_LRU), F32),
                   jax.ShapeDtypeStruct((N_CHIP, D_MODEL, WIN_BLK), F32), jax.ShapeDtypeStruct((D_MODEL, D_MODEL), F32),
                   jax.ShapeDtypeStruct((D_LRU, 2 * D_LRU), F32)],
        scratch_shapes=[pltpu.VMEM((N_CHIP, D_MODEL, WIN_BLK), BF16), pltpu.VMEM((D_MODEL, D_MODEL), BF16),
                        pltpu.VMEM((ts, D_IN), BF16), pltpu.VMEM((ts, 2 * D_LRU), BF16),
                        pltpu.VMEM((N_CHIP, D_MODEL, WIN_BLK), F32), pltpu.VMEM((D_MODEL, D_MODEL), F32),
                        pltpu.VMEM((D_LRU, 2 * D_LRU), F32),
                        ext, ext, ext, ext, ext, pltpu.VMEM((ts, D_LRU), F32),
                        pltpu.VMEM((HALO, D_LRU), F32), pltpu.VMEM((HALO, D_LRU), F32),
                        pltpu.SemaphoreType.DMA((2 * N_CHIP,))],
        compiler_params=pltpu.CompilerParams(dimension_semantics=("arbitrary",), vmem_limit_bytes=VMEM_LIMIT),
    )(chip, dx1, x, mixed, proj, proj, hl, hl, hb, ycat, mod, vecd, vecl, *win, *wout, gab, a64)


def _wgrad(name, a, b, a_blk, b_blk, out_dtype):
    s = a.shape[0]
    aw = a_blk or a.shape[1]
    bw = b_blk or b.shape[1]
    nblk = N_CHIP if (a_blk or b_blk) else 1

    def body(a_ref, b_ref, o_ref):
        o_ref[0] = _dot_tn(a_ref[...], b_ref[...]).astype(out_dtype)

    return pl.pallas_call(
        body, name=name, grid=(nblk,),
        in_specs=[pl.BlockSpec((s, aw), (lambda j: (0, j)) if a_blk else (lambda j: (0, 0))),
                  pl.BlockSpec((s, bw), (lambda j: (0, j)) if b_blk else (lambda j: (0, 0)))],
        out_specs=pl.BlockSpec((1, aw, bw), lambda j: (j, 0, 0)),
        out_shape=jax.ShapeDtypeStruct((nblk, aw, bw), out_dtype),
        compiler_params=pltpu.CompilerParams(dimension_semantics=("arbitrary",), vmem_limit_bytes=VMEM_LIMIT),
    )(a, b)


def _mod_matmul(c_all, ada_w_loc):
    n = ada_w_loc.shape[1]
    cb = 512

    def body(c_ref, w_ref, o_ref):
        c = c_ref[...]
        sc = c * jax.nn.sigmoid(c)
        o_ref[...] = _dot(sc.astype(BF16), w_ref[...].astype(BF16))

    return pl.pallas_call(
        body, name="mod_matmul", grid=(n // cb,),
        in_specs=[_full((8, D_MODEL)), pl.BlockSpec((D_MODEL, cb), lambda j: (0, j))],
        out_specs=pl.BlockSpec((8, cb), lambda j: (0, j)),
        out_shape=jax.ShapeDtypeStruct((8, n), F32),
        compiler_params=pltpu.CompilerParams(dimension_semantics=("arbitrary",), vmem_limit_bytes=VMEM_LIMIT),
    )(c_all, ada_w_loc)


def _adam_math(w, g, m, v):
    m = ADAM_B1 * m + (1.0 - ADAM_B1) * g
    v = ADAM_B2 * v + (1.0 - ADAM_B2) * (g * g)
    m_hat = m / (1.0 - ADAM_B1 ** ADAM_STEP)
    v_hat = v / (1.0 - ADAM_B2 ** ADAM_STEP)
    delta = (-ADAM_LR) * (m_hat / (jnp.sqrt(v_hat) + ADAM_EPS) + ADAM_WD * w)
    return delta, m, v


def _adam(name, core, shards):
    n = len(shards)
    r, c = shards[0][0].shape
    half = r // 2
    rb = min(half, 128)
    nh = half // rb

    def body(core_ref, *refs):
        ins, outs = refs[:5 * n], refs[5 * n:]
        mine = (pl.program_id(0) // nh) == core_ref[0]
        for k in range(n):
            w_ref, go_ref, gs_ref, m_ref, v_ref = ins[5 * k:5 * k + 5]
            g_ref, d_ref, mo_ref, vo_ref = outs[4 * k:4 * k + 4]
            g = jnp.where(mine, go_ref[...], gs_ref[...])
            g_ref[...] = g
            d_ref[...], mo_ref[...], vo_ref[...] = _adam_math(w_ref[...], g, m_ref[...], v_ref[...])

    spec = pl.BlockSpec((rb, c), lambda i, core_ref: (i, 0))
    hspec = pl.BlockSpec((rb, c), lambda i, core_ref: (i % nh, 0))
    sds = jax.ShapeDtypeStruct((r, c), F32)
    res = pl.pallas_call(
        body, name=name,
        grid_spec=pltpu.PrefetchScalarGridSpec(
            num_scalar_prefetch=1, grid=(r // rb,),
            in_specs=[spec, hspec, hspec, spec, spec] * n, out_specs=[spec] * (4 * n)),
        out_shape=[sds] * (4 * n),
        compiler_params=pltpu.CompilerParams(dimension_semantics=("arbitrary",), vmem_limit_bytes=VMEM_LIMIT),
    )(core, *[t for s in shards for t in s])
    return [res[4 * k:4 * k + 4] for k in range(n)]


def _ada_grad_adam(sct, dmod_loc, w, m, v):
    r, c = w.shape
    rb = 128

    def body(s_ref, dm_ref, w_ref, m_ref, v_ref, g_ref, d_ref, mo_ref, vo_ref):
        g = s_ref[:, 0:1] * dm_ref[0:1, :]
        for b in range(1, 8):
            g = g + s_ref[:, b:b + 1] * dm_ref[b:b + 1, :]
        g_ref[...] = g
        d_ref[...], mo_ref[...], vo_ref[...] = _adam_math(w_ref[...], g, m_ref[...], v_ref[...])

    spec = pl.BlockSpec((rb, c), lambda i: (i, 0))
    sds = jax.ShapeDtypeStruct((r, c), F32)
    return pl.pallas_call(
        body, name="ada_grad_adam", grid=(r // rb,),
        in_specs=[pl.BlockSpec((rb, 8), lambda i: (i, 0)), _full((8, c)), spec, spec, spec],
        out_specs=[spec] * 4, out_shape=[sds] * 4,
        compiler_params=pltpu.CompilerParams(dimension_semantics=("arbitrary",), vmem_limit_bytes=VMEM_LIMIT),
    )(sct, dmod_loc, w, m, v)


def _position():
    x, y, c = lax.axis_index("x"), lax.axis_index("y"), lax.axis_index("c")
    chips = [(1 - x, y), (x, 1 - y), (1 - x, 1 - y)]
    return x, y, c, chips


def _allgather8(name, arrs):
    na = len(arrs)

    def body(*refs):
        ins, outs = refs[:na], refs[na:2 * na]
        send_sems, recv_sems, local_sems = refs[2 * na:]
        x, y, c, chips = _position()
        me, sibling = (x, y, c), (x, y, 1 - c)
        first, passed, local = [], [], []
        for a in range(na):
            m_per = ins[a].shape[0]

            def rows(px, py, pc, a=a, m_per=m_per):
                return outs[a].at[pl.ds((4 * px + 2 * py + pc) * m_per, m_per), :]

            def copy(k, block, to, src=None, a=a, rows=rows):
                return pltpu.make_async_remote_copy(
                    src_ref=rows(*block) if src is None else src, dst_ref=rows(*block),
                    send_sem=send_sems.at[7 * a + k], recv_sem=recv_sems.at[7 * a + k],
                    device_id=to, device_id_type=MESH)

            mine = pltpu.make_async_copy(ins[a], rows(*me), local_sems.at[a])
            mine.start()
            local.append(mine)
            f = [copy(0, me, sibling, src=ins[a])]
            f += [copy(1 + j, me, (*chip, c), src=ins[a]) for j, chip in enumerate(chips)]
            for cp in f:
                cp.start()
            first.append((f, copy))
        for a in range(na):
            f, copy = first[a]
            p = [copy(4 + j, (*chip, c), sibling) for j, chip in enumerate(chips)]
            for j, chip in enumerate(chips):
                copy(1 + j, (*chip, c), me).wait_recv()
                p[j].start()
            passed.append(p)
        for a in range(na):
            f, copy = first[a]
            copy(0, sibling, me).wait_recv()
            for j, chip in enumerate(chips):
                copy(4 + j, (*chip, 1 - c), me).wait_recv()
            for cp in f + passed[a]:
                cp.wait_send()
            local[a].wait()

    return pl.pallas_call(
        body, name=name,
        out_shape=[jax.ShapeDtypeStruct((8 * a.shape[0], a.shape[1]), a.dtype) for a in arrs],
        in_specs=[VMEM] * na, out_specs=[VMEM] * na,
        scratch_shapes=[pltpu.SemaphoreType.DMA((7 * na,)), pltpu.SemaphoreType.DMA((7 * na,)),
                        pltpu.SemaphoreType.DMA((na,))],
        compiler_params=pltpu.CompilerParams(vmem_limit_bytes=VMEM_LIMIT),
    )(*arrs)


AG_SEMS = 7


def _ag_copies(ins, outs, send_sems, recv_sems):
    x, y, c, chips = _position()
    sibling = (x, y, 1 - c)
    xn, yn, dg = [2 * chip[0] + chip[1] for chip in chips]
    to_x, to_y = (1 - x, y, c), (x, 1 - y, c)
    res = []
    for a in range(len(ins)):
        half = ins[a].shape[0] // 2
        quarter = half // 2

        def copy(k, dst, to, src=None, a=a):
            return pltpu.make_async_remote_copy(
                src_ref=dst if src is None else src, dst_ref=dst,
                send_sem=send_sems.at[AG_SEMS * a + k], recv_sem=recv_sems.at[AG_SEMS * a + k],
                device_id=to, device_id_type=MESH)

        def rows(chip, pc, q=None, a=a, half=half, quarter=quarter):
            if q is None:
                return outs[a].at[chip, pl.ds(pc * half, half), :]
            return outs[a].at[chip, pl.ds(pc * half + q * quarter, quarter), :]

        own = ins[a].at[pl.ds(c * half, half), :]
        mine = rows(2 * x + y, c)
        res.append(dict(
            sends=[copy(0, mine, to_x, src=own), copy(1, mine, to_y, src=own)],
            from_x=copy(0, rows(xn, c), to_x), from_y=copy(1, rows(yn, c), to_y),
            relay_y=copy(2, rows(xn, c, 0), to_y), relay_x=copy(3, rows(yn, c, 1), to_x),
            from_y_relay=copy(2, rows(dg, c, 0), to_y), from_x_relay=copy(3, rows(dg, c, 1), to_x),
            pass_on=[copy(4, rows(xn, c), sibling), copy(5, rows(yn, c), sibling), copy(6, rows(dg, c), sibling)],
            from_sibling=[copy(4, rows(xn, 1 - c), sibling), copy(5, rows(yn, 1 - c), sibling),
                          copy(6, rows(dg, 1 - c), sibling)]))
    return res


def _ag_start(ins, outs, send_sems, recv_sems):
    for cps in _ag_copies(ins, outs, send_sems, recv_sems):
        for cp in cps["sends"]:
            cp.start()


def _ag_relay(ins, outs, send_sems, recv_sems, which):
    copies = _ag_copies(ins, outs, send_sems, recv_sems)
    for a in which:
        cps = copies[a]
        cps["from_x"].wait_recv()
        cps["relay_y"].start()
        cps["pass_on"][0].start()
        cps["from_y"].wait_recv()
        cps["relay_x"].start()
        cps["pass_on"][1].start()


def _ag_complete(ins, outs, send_sems, recv_sems):
    copies = _ag_copies(ins, outs, send_sems, recv_sems)
    for cps in copies:
        cps["from_y_relay"].wait_recv()
        cps["from_x_relay"].wait_recv()
        cps["pass_on"][2].start()
    for cps in copies:
        for cp in cps["from_sibling"]:
            cp.wait_recv()
        for cp in cps["sends"] + [cps["relay_y"], cps["relay_x"]] + cps["pass_on"]:
            cp.wait_send()


def _ag_finish(ins, outs, send_sems, recv_sems):
    _ag_relay(ins, outs, send_sems, recv_sems, range(len(ins)))
    _ag_complete(ins, outs, send_sems, recv_sems)


def _allgather_weights(name, collective_id, shards):
    na = len(shards)
    hbm = pltpu.MemorySpace.HBM
    ins = [jax.new_ref(s, memory_space=hbm) for s in shards]
    outs = [jax.empty_ref(jax.ShapeDtypeStruct((N_CHIP,) + s.shape, s.dtype), memory_space=hbm) for s in shards]

    @pl.kernel(mesh=plsc.ScalarSubcoreMesh(axis_name="sequencer", num_cores=1), name=name,
               scratch_types=(pltpu.SemaphoreType.DMA((AG_SEMS * na,)), pltpu.SemaphoreType.DMA((AG_SEMS * na,))),
               compiler_params=pltpu.CompilerParams(collective_id=collective_id))
    def launch(send_sems, recv_sems):
        x, y, c, _ = _position()
        peers = [(1 - x, y, c), (x, 1 - y, c), (x, y, 1 - c)]
        barrier = pltpu.get_barrier_semaphore()
        for peer in peers:
            pl.semaphore_signal(barrier, inc=1, device_id=peer, device_id_type=MESH)
        pl.semaphore_wait(barrier, len(peers))
        _ag_start(ins, outs, send_sems, recv_sems)
        _ag_finish(ins, outs, send_sems, recv_sems)

    launch()
    return [o[...] for o in outs]


def _swap_halves(name, grads, after=()):
    na, nw = len(grads), len(after)

    def body(*refs):
        ins, outs = refs[:na], refs[na + nw:2 * na + nw]
        send_sems, recv_sems = refs[2 * na + nw:]
        x, y, c, _ = _position()
        cps = []
        for a in range(na):
            half = ins[a].shape[1] // 2
            cp = pltpu.make_async_remote_copy(
                src_ref=ins[a].at[:, pl.ds((1 - c) * half, half), :], dst_ref=outs[a],
                send_sem=send_sems.at[a], recv_sem=recv_sems.at[a],
                device_id=(x, y, 1 - c), device_id_type=MESH)
            cp.start()
            cps.append(cp)
        for cp in cps:
            cp.wait()

    return pl.pallas_call(
        body, name=name,
        out_shape=[jax.ShapeDtypeStruct((g.shape[0], g.shape[1] // 2, g.shape[2]), g.dtype) for g in grads],
        in_specs=[ANY] * (na + nw), out_specs=[ANY] * na,
        scratch_shapes=[pltpu.SemaphoreType.DMA((na,)), pltpu.SemaphoreType.DMA((na,))],
    )(*grads, *after)


def _xchg_copies(ins, outs, send_sems, recv_sems):
    x, y, c, chips = _position()
    return [pltpu.make_async_remote_copy(
        src_ref=ins[a].at[2 * chip[0] + chip[1]], dst_ref=outs[a].at[j],
        send_sem=send_sems.at[3 * a + j], recv_sem=recv_sems.at[3 * a + j],
        device_id=(*chip, c), device_id_type=MESH) for a in range(len(ins)) for j, chip in enumerate(chips)]


def _exchange_chips(name, collective_id, parts):
    na = len(parts)
    hbm = pltpu.MemorySpace.HBM
    ins = [jax.new_ref(p, memory_space=hbm) for p in parts]
    outs = [jax.empty_ref(jax.ShapeDtypeStruct((3,) + p.shape[1:], p.dtype), memory_space=hbm) for p in parts]

    @pl.kernel(mesh=plsc.ScalarSubcoreMesh(axis_name="sequencer", num_cores=1), name=name,
               scratch_types=(pltpu.SemaphoreType.DMA((3 * na,)), pltpu.SemaphoreType.DMA((3 * na,))),
               compiler_params=pltpu.CompilerParams(collective_id=collective_id))
    def launch(send_sems, recv_sems):
        x, y, c, chips = _position()
        barrier = pltpu.get_barrier_semaphore()
        for chip in chips:
            pl.semaphore_signal(barrier, inc=1, device_id=(*chip, c), device_id_type=MESH)
        pl.semaphore_wait(barrier, len(chips))
        for cp in _xchg_copies(ins, outs, send_sems, recv_sems):
            cp.start()
        for cp in _xchg_copies(ins, outs, send_sems, recv_sems):
            cp.wait()

    launch()
    return [q[...] for q in outs]


def _swap_reduced(name, halves):
    na = len(halves)

    def body(*refs):
        ins, outs = refs[:na], refs[na:2 * na]
        send_sems, recv_sems = refs[2 * na:]
        x, y, c, _ = _position()
        cps = []
        for a in range(na):
            cp = pltpu.make_async_remote_copy(
                src_ref=ins[a], dst_ref=outs[a], send_sem=send_sems.at[a], recv_sem=recv_sems.at[a],
                device_id=(x, y, 1 - c), device_id_type=MESH)
            cp.start()
            cps.append(cp)
        for cp in cps:
            cp.wait()

    return pl.pallas_call(
        body, name=name,
        out_shape=[jax.ShapeDtypeStruct(h.shape, h.dtype) for h in halves],
        in_specs=[ANY] * na, out_specs=[ANY] * na,
        scratch_shapes=[pltpu.SemaphoreType.DMA((na,)), pltpu.SemaphoreType.DMA((na,))],
    )(*halves)


def _add_sibling(name, grad, recv, core):
    _, r, c = grad.shape
    half = r // 2
    rb = min(half, 256)
    nrb = half // rb

    def body(core_ref, g_ref, r_ref, o_ref):
        o_ref[...] = (g_ref[...].astype(F32) + r_ref[...].astype(F32)).astype(BF16)

    return pl.pallas_call(
        body, name=name,
        grid_spec=pltpu.PrefetchScalarGridSpec(
            num_scalar_prefetch=1, grid=(N_CHIP, nrb),
            in_specs=[pl.BlockSpec((1, rb, c), lambda j, i, core_ref: (j, core_ref[0] * nrb + i, 0)),
                      pl.BlockSpec((1, rb, c), lambda j, i, core_ref: (j, i, 0))],
            out_specs=pl.BlockSpec((1, rb, c), lambda j, i, core_ref: (j, i, 0))),
        out_shape=jax.ShapeDtypeStruct((N_CHIP, half, c), BF16),
        compiler_params=pltpu.CompilerParams(dimension_semantics=("arbitrary", "arbitrary"),
                                             vmem_limit_bytes=VMEM_LIMIT),
    )(core, grad, recv)


def _add_chips(name, chip, p, q):
    _, half, c = q.shape
    rb = min(half, 256)

    def body(chip_ref, p_ref, q_ref, o_ref):
        acc = p_ref[0].astype(F32)
        for j in range(3):
            acc = acc + q_ref[j].astype(F32)
        o_ref[...] = acc

    return pl.pallas_call(
        body, name=name,
        grid_spec=pltpu.PrefetchScalarGridSpec(
            num_scalar_prefetch=1, grid=(half // rb,),
            in_specs=[pl.BlockSpec((1, rb, c), lambda i, chip_ref: (chip_ref[0], i, 0)),
                      pl.BlockSpec((3, rb, c), lambda i, chip_ref: (0, i, 0))],
            out_specs=pl.BlockSpec((rb, c), lambda i, chip_ref: (i, 0))),
        out_shape=jax.ShapeDtypeStruct((half, c), F32),
        compiler_params=pltpu.CompilerParams(dimension_semantics=("arbitrary",), vmem_limit_bytes=VMEM_LIMIT),
    )(chip, p, q)


def _small_update(gad, gam, gl, gg, mychip, params):
    names = ["ada_b", "norm1_g", "lru_conv_b", "gate_a_w", "gate_a_b", "gate_x_w", "gate_x_b", "a_param",
             "lru_conv_w", "short_conv_w", "lru_out_g", "conv_out_g", "norm2_g", "final_g"]
    flat = [t for n in names for t in params[n]]
    nin = len(flat)

    def body(chip_ref, gad_ref, gam_ref, gl_ref, gg_ref, *refs):
        ins = {n: refs[3 * k:3 * k + 3] for k, n in enumerate(names)}
        outs = {n: refs[nin + 4 * k:nin + 4 * k + 4] for k, n in enumerate(names)}
        loss_ref, dmod_ref = refs[nin + 4 * len(names):nin + 4 * len(names) + 2]

        def dsum(ref, lo, n):
            per = ref.shape[0] // 8
            acc = ref[lo:lo + n, :].astype(F32)
            for dev in range(1, 8):
                acc = acc + ref[dev * per + lo:dev * per + lo + n, :].astype(F32)
            return acc

        def update(n, g):
            w_ref, m_ref, v_ref = ins[n]
            g_ref, d_ref, mo_ref, vo_ref = outs[n]
            g_ref[...] = g
            d_ref[...], mo_ref[...], vo_ref[...] = _adam_math(w_ref[...], g, m_ref[...], v_ref[...])

        d, dm, l, lw = refs[-4:]
        d[...] = dsum(gad_ref, 0, 8)
        dm[...] = dsum(gam_ref, 0, 8)
        l[...] = dsum(gl_ref, 0, 16)
        for dev in range(8):
            for k in range(3):
                dmod_ref[dev:dev + 1, k * D_MODEL:(k + 1) * D_MODEL] = gad_ref[dev * 8 + k:dev * 8 + k + 1, :]
                dmod_ref[dev:dev + 1, (3 + k) * D_MODEL:(4 + k) * D_MODEL] = gam_ref[dev * 8 + k:dev * 8 + k + 1, :]
        w_ref, m_ref, v_ref = ins["ada_b"]
        g_ref, d_ref, mo_ref, vo_ref = outs["ada_b"]
        for k in range(3):
            g_ref[:, k * D_MODEL:(k + 1) * D_MODEL] = d[k:k + 1, :]
            g_ref[:, (3 + k) * D_MODEL:(4 + k) * D_MODEL] = dm[k:k + 1, :]
        d_ref[...], mo_ref[...], vo_ref[...] = _adam_math(w_ref[...], g_ref[...], m_ref[...], v_ref[...])
        update("norm1_g", d[3:4, :])
        update("norm2_g", dm[3:4, :])
        update("final_g", dm[4:5, :])
        update("gate_a_b", d[4:5, 0:D_LRU])
        update("gate_x_b", d[4:5, D_LRU:2 * D_LRU])
        update("lru_conv_b", l[4:5, :])
        update("a_param", l[8:9, :] * jax.nn.sigmoid(ins["a_param"][0][...]))
        update("lru_out_g", l[9:10, :])
        update("conv_out_g", l[10:11, :])
        loss_ref[...] = jnp.broadcast_to(dm[5:6, 0:128], (8, 128))
        chip = chip_ref[0]
        acc = jnp.zeros((8, 128), F32)
        for j in range(N_CHIP):
            acc = acc + jnp.where(chip == j, l[0:8, j * 128:(j + 1) * 128], 0.0)
        lw[...] = acc
        update("lru_conv_w", lw[0:4, :])
        update("short_conv_w", lw[5:8, :])
        gates = dsum(gg_ref, 0, D_LRU)
        update("gate_a_w", gates[:, 0:HEAD])
        update("gate_x_w", gates[:, HEAD:2 * HEAD])

    out_shape = []
    for n in names:
        out_shape += [jax.ShapeDtypeStruct(params[n][0].shape, F32)] * 4
    out_shape += [jax.ShapeDtypeStruct((8, 128), F32), jax.ShapeDtypeStruct((8, 6 * D_MODEL), F32)]
    res = pl.pallas_call(
        body, name="small_update", out_shape=out_shape,
        in_specs=[SMEM] + [VMEM] * (4 + nin),
        out_specs=[VMEM] * len(out_shape),
        scratch_shapes=[pltpu.VMEM((8, D_MODEL), F32), pltpu.VMEM((8, D_MODEL), F32), pltpu.VMEM((16, D_LRU), F32),
                        pltpu.VMEM((8, 128), F32)],
        compiler_params=pltpu.CompilerParams(vmem_limit_bytes=VMEM_LIMIT),
    )(mychip, gad, gam, gl, gg, *flat)
    per = {n: res[4 * k:4 * k + 4] for k, n in enumerate(names)}
    return per, res[-2], res[-1]


def _block_diag(w):
    eye = jnp.eye(8, dtype=w.dtype)
    return (eye[:, None, :, None] * w[:, :, None, :]).reshape(8 * HEAD, 8 * HEAD)


def _diag_blocks(g):
    return jnp.concatenate([g[h * HEAD:(h + 1) * HEAD, h * HEAD:(h + 1) * HEAD] for h in range(8)], axis=0)


def kernel(x, c, ada_w, ada_b, norm1_g, w_in, lru_conv_w, lru_conv_b, gate_a_w, gate_a_b, gate_x_w, gate_x_b, a_param, short_conv_w, lru_out_g, conv_out_g, w_out, norm2_g, w_mlp1, w_mlp2, final_g, loss_target, m_ada_w, m_ada_b, m_norm1_g, m_w_in, m_lru_conv_w, m_lru_conv_b, m_gate_a_w, m_gate_a_b, m_gate_x_w, m_gate_x_b, m_a_param, m_short_conv_w, m_lru_out_g, m_conv_out_g, m_w_out, m_norm2_g, m_w_mlp1, m_w_mlp2, m_final_g, v_ada_w, v_ada_b, v_norm1_g, v_w_in, v_lru_conv_w, v_lru_conv_b, v_gate_a_w, v_gate_a_b, v_gate_x_w, v_gate_x_b, v_a_param, v_short_conv_w, v_lru_out_g, v_conv_out_g, v_w_out, v_norm2_g, v_w_mlp1, v_w_mlp2, v_final_g):
    xi, yi, ci = lax.axis_index("x"), lax.axis_index("y"), lax.axis_index("c")
    mychip = 2 * xi + yi
    me = 4 * xi + 2 * yi + ci

    own_in, own_out = w_in[0].astype(BF16), w_out[0].astype(BF16)
    win_all, wout_all = _allgather_weights("allgather_mixer_weights", 1, [own_in, own_out])
    own_w1, own_w2 = w_mlp1[0].astype(BF16), w_mlp2[0].astype(BF16)
    w1_all, w2_all = _allgather_weights("allgather_mlp_weights", 2, [own_w1, own_w2])

    c_blk = jnp.zeros((8, D_MODEL), F32).at[0:1].set(c)
    cw_blk = jnp.zeros((8, 128), F32).at[0:4].set(lru_conv_w[0]).at[4:7].set(short_conv_w[0])
    c_g, cw_g = _allgather8("allgather_cond", [c_blk, cw_blk])
    c_all = c_g.reshape(8, 8, D_MODEL)[:, 0]
    cw_g = cw_g.reshape(4, 2, 8, 128)[:, 0]
    lcw = cw_g[:, 0:4].transpose(1, 0, 2).reshape(4, D_LRU)
    scw = cw_g[:, 4:7].transpose(1, 0, 2).reshape(3, D_LRU)

    mod_loc = _mod_matmul(c_all, ada_w[0])
    (mod_g,) = _allgather8("allgather_mod", [mod_loc])
    mod_all = mod_g.reshape(4, 2, 8, 6 * D_MODEL // 4)[:, 0].transpose(1, 0, 2).reshape(8, 6 * D_MODEL) + ada_b
    mod_pad = jnp.pad(mod_all.reshape(8, 6, D_MODEL), ((0, 0), (0, 2), (0, 0)))
    mod = lax.dynamic_slice_in_dim(mod_pad, me, 1, axis=0).reshape(8, D_MODEL)

    win, wout = (win_all, own_in), (wout_all, own_out)
    chip = mychip.reshape(1).astype(jnp.int32)
    core = ci.reshape(1).astype(jnp.int32)

    vecd = jnp.concatenate([norm1_g, norm2_g, final_g[None, :], jnp.concatenate([gate_a_b, gate_x_b], axis=1),
                            jnp.zeros((4, D_MODEL), F32)], axis=0)
    vecl = jnp.concatenate([lcw, lru_conv_b, scw, a_param, lru_out_g, conv_out_g, jnp.zeros((5, D_LRU), F32)], axis=0)
    gab = jnp.concatenate([_block_diag(gate_a_w[0]), _block_diag(gate_x_w[0])], axis=1).astype(BF16)
    a64 = _block_diag(jnp.full((8, HEAD, HEAD), 1.0 / HEAD, F32)).astype(BF16)

    hb, proj, hl, ycat, mixed, x1 = _mix_fwd(chip, x[0], mod, vecd, vecl, win, wout, gab, a64)
    dx1, act, dz, dmo, h2b, accm = _mlp_fwd_bwd(
        chip, x1, loss_target[0], mod, vecd, (w1_all, own_w1), (w2_all, own_w2))

    def sibling_sum(tag, grads, after=()):
        recv = _swap_halves("rs_swap_halves_" + tag, grads, after)
        return [_add_sibling("rs_add_sibling_%s%d" % (tag, k), g, r, core) for k, (g, r) in enumerate(zip(grads, recv))]

    parts_mlp = sibling_sum("mlp", [_wgrad("wgrad_mlp1", h2b, dz, 0, FF_BLK, BF16),
                                    _wgrad("wgrad_mlp2", act, dmo, FF_BLK, 0, BF16)])
    q_w1, q_w2 = _exchange_chips("rs_exchange_mlp", 0, parts_mlp)
    grad_x, accd, accl, g_win, g_wout, g_gate = _mix_bwd(
        chip, dx1, x[0], mixed, proj, hl, hb, ycat, mod, vecd, vecl, win, wout, gab, a64)

    gg_blk = jnp.concatenate([_diag_blocks(g_gate[:, 0:D_LRU]), _diag_blocks(g_gate[:, D_LRU:2 * D_LRU])], axis=1)
    gad, gam, gl, gg = _allgather8("allgather_small_grads", [accd, accm, accl, gg_blk.astype(BF16)])
    parts_mix = sibling_sum("mix", [g_win, g_wout.reshape(N_CHIP, WOUT_BLK, D_MODEL)], after=[gad])

    def reduced(tag, parts, landed):
        own = [_add_chips("rs_add_chips_%s%d" % (tag, k), chip, p, q) for k, (p, q) in enumerate(zip(parts, landed))]
        return own, _swap_reduced("rs_swap_reduced_" + tag, own)

    landed_mix = _exchange_chips("rs_exchange_mix", 3, parts_mix)
    own_mlp, sib_mlp = reduced("mlp", parts_mlp, [q_w1, q_w2])
    res_w1, res_w2 = _adam("adam_mlp", core, [(w_mlp1[0], own_mlp[0], sib_mlp[0], m_w_mlp1[0], v_w_mlp1[0]),
                                              (w_mlp2[0], own_mlp[1], sib_mlp[1], m_w_mlp2[0], v_w_mlp2[0])])
    own_mix, sib_mix = reduced("mix", parts_mix, landed_mix)
    (res_win,) = _adam("adam_w_in", core, [(w_in[0], own_mix[0], sib_mix[0], m_w_in[0], v_w_in[0])])
    (res_wout,) = _adam("adam_w_out", core, [(w_out[0], own_mix[1], sib_mix[1], m_w_out[0], v_w_out[0])])

    params = {
        "ada_b": (ada_b, m_ada_b, v_ada_b), "norm1_g": (norm1_g, m_norm1_g, v_norm1_g),
        "lru_conv_b": (lru_conv_b, m_lru_conv_b, v_lru_conv_b),
        "gate_a_w": tuple(t.reshape(D_LRU, HEAD) for t in (gate_a_w, m_gate_a_w, v_gate_a_w)),
        "gate_a_b": (gate_a_b, m_gate_a_b, v_gate_a_b),
        "gate_x_w": tuple(t.reshape(D_LRU, HEAD) for t in (gate_x_w, m_gate_x_w, v_gate_x_w)),
        "gate_x_b": (gate_x_b, m_gate_x_b, v_gate_x_b), "a_param": (a_param, m_a_param, v_a_param),
        "lru_conv_w": tuple(t[0] for t in (lru_conv_w, m_lru_conv_w, v_lru_conv_w)),
        "short_conv_w": tuple(t[0] for t in (short_conv_w, m_short_conv_w, v_short_conv_w)),
        "lru_out_g": (lru_out_g, m_lru_out_g, v_lru_out_g), "conv_out_g": (conv_out_g, m_conv_out_g, v_conv_out_g),
        "norm2_g": (norm2_g, m_norm2_g, v_norm2_g),
        "final_g": tuple(t[None, :] for t in (final_g, m_final_g, v_final_g)),
    }
    small, loss_blk, dmod_cols = _small_update(gad, gam, gl, gg, chip, params)
    loss = loss_blk[0, 0]

    ncol = 6 * D_MODEL // N_CHIP
    dmod_loc = lax.dynamic_slice_in_dim(dmod_cols, mychip * ncol, ncol, axis=1)
    sct = (c_all * jax.nn.sigmoid(c_all)).T
    ada = _ada_grad_adam(sct, dmod_loc, ada_w[0], m_ada_w[0], v_ada_w[0])

    res = {"ada_w": ada, "w_in": res_win, "w_out": res_wout, "w_mlp1": res_w1, "w_mlp2": res_w2}
    res = {n: tuple(t[None] for t in r) for n, r in res.items()}
    shapes = {"gate_a_w": gate_a_w.shape, "gate_x_w": gate_x_w.shape, "lru_conv_w": lru_conv_w.shape,
              "short_conv_w": short_conv_w.shape, "final_g": final_g.shape}
    for n, t in small.items():
        res[n] = tuple(u.reshape(shapes[n]) if n in shapes else u for u in t)

    order = ["ada_w", "ada_b", "norm1_g", "w_in", "lru_conv_w", "lru_conv_b", "gate_a_w", "gate_a_b", "gate_x_w",
             "gate_x_b", "a_param", "short_conv_w", "lru_out_g", "conv_out_g", "w_out", "norm2_g", "w_mlp1",
             "w_mlp2", "final_g"]
    return (loss, grad_x[None], *[res[n][0] for n in order], *[res[n][1] for n in order],
            *[res[n][2] for n in order], *[res[n][3] for n in order])
```

```python
import jax
import jax.numpy as jnp
from jax import lax
from jax.experimental import pallas as pl
from jax.experimental.pallas import tpu as pltpu
from jax.experimental.pallas import tpu_sc as plsc

F32 = jnp.float32
BF16 = jnp.bfloat16

D_MODEL = 1024
D_LRU = 512
D_IN = 2560
D_FF = 4096
N_CHIP = 4
WIN_BLK = D_IN // N_CHIP
WOUT_BLK = D_MODEL // N_CHIP
FF_BLK = D_FF // N_CHIP
HEAD = 64
EPS = 1e-6
C_GATE = 8.0
TOKEN_TILE = 256
HALO = 8
VMEM_LIMIT = 60 * 1024 * 1024

ADAM_LR = 0.001
ADAM_B1 = 0.9
ADAM_B2 = 0.999
ADAM_EPS = 1e-08
ADAM_WD = 0.01
ADAM_STEP = 10

MESH = pl.DeviceIdType.MESH
ANY = pl.BlockSpec(memory_space=pl.ANY)
VMEM = pl.BlockSpec(memory_space=pltpu.VMEM)
SMEM = pl.BlockSpec(memory_space=pltpu.SMEM)


def _full(shape, single=False):
    nd = len(shape)
    if single:
        return pl.BlockSpec(shape, lambda *_: (0,) * nd, pipeline_mode=pl.Buffered(1))
    return pl.BlockSpec(shape, lambda *_: (0,) * nd)


def _dot(a, b):
    return jnp.dot(a, b, preferred_element_type=F32)


def _dot_nt(a, b):
    return lax.dot_general(a, b, (((1,), (1,)), ((), ())), preferred_element_type=F32)


def _dot_tn(a, b):
    return lax.dot_general(a, b, (((0,), (0,)), ((), ())), preferred_element_type=F32)


def _gmean(v, a64):
    hi = v.astype(BF16)
    lo = (v - hi.astype(F32)).astype(BF16)
    return _dot(hi, a64) + _dot(lo, a64)


def _gelu(x):
    u = 0.7978845608028654 * (x + 0.044715 * x * x * x)
    t = jnp.tanh(u)
    return 0.5 * x * (1.0 + t), t


def _gelu_grad(x, t):
    du = 0.7978845608028654 * (1.0 + 3.0 * 0.044715 * x * x)
    return 0.5 * (1.0 + t) + 0.5 * x * (1.0 - t * t) * du


def _log1p_pos(y):
    return jnp.where(y < 1e-2, y * (1.0 - y * (0.5 - y * (1.0 / 3.0 - y * 0.25))), jnp.log(1.0 + y))


def _softplus(a):
    return jnp.maximum(a, 0.0) + _log1p_pos(jnp.exp(-jnp.abs(a)))


def _neg_expm1(z):
    series = -z * (1.0 + z * (0.5 + z * (1.0 / 6.0 + z * (1.0 / 24.0 + z * (1.0 / 120.0)))))
    return jnp.where(z > -0.02, series, 1.0 - jnp.exp(z))


def _scan_fwd(a, b, row):
    n = a.shape[0]
    d = 1
    while d < n:
        m = row >= d
        b = jnp.where(m, a * pltpu.roll(b, d, 0) + b, b)
        a = jnp.where(m, a * pltpu.roll(a, d, 0), a)
        d *= 2
    return a, b


def _scan_rev(a, b, row):
    n = a.shape[0]
    d = 1
    while d < n:
        m = row < n - d
        b = jnp.where(m, b + a * pltpu.roll(b, n - d, 0), b)
        a = jnp.where(m, a * pltpu.roll(a, n - d, 0), a)
        d *= 2
    return a, b


def _colsum(v):
    return jnp.sum(v, axis=0, keepdims=True)


def _load_gathered(chip, gathered, own, slot, sems):
    copies = []
    for j in range(N_CHIP):
        @pl.when(chip == j)
        def _(j=j):
            pltpu.make_async_copy(own, slot(j), sems.at[j]).start()

        @pl.when(chip != j)
        def _(j=j):
            pltpu.make_async_copy(gathered.at[j], slot(j), sems.at[j]).start()

        copies.append(pltpu.make_async_copy(own, slot(j), sems.at[j]))
    return copies


def _lru_gates(xlb, gab, gbias, sp, first_row):
    g = _dot(xlb, gab) + gbias
    r = jax.nn.sigmoid(g[:, :D_LRU])
    ig = jax.nn.sigmoid(g[:, D_LRU:])
    la = (-C_GATE) * r * sp
    a = jnp.exp(la)
    msq = jnp.sqrt(_neg_expm1(2.0 * la))
    mult = jnp.where(first_row, 1.0, msq)
    return r, ig, a, msq, mult


def _mix_fwd(chip, x, mod, vecd, vecl, win, wout, gab, a64):
    s = x.shape[0]
    ts = TOKEN_TILE
    nt = s // ts

    def body(chip_ref, x_ref, mod_ref, vd_ref, vl_ref, win_hbm, win_own, wout_hbm, wout_own, gab_ref, a64_ref,
             hb_ref, proj_ref, hl_ref, ycat_ref, mixed_ref, x1_ref,
             win_ref, wout_ref, ext_lx, ext_cv, hcar, sems):
        i = pl.program_id(0)

        @pl.when(i == 0)
        def _():
            cps = _load_gathered(chip_ref[0], win_hbm, win_own, lambda j: win_ref.at[j], sems.at[pl.ds(0, N_CHIP)])
            cps += _load_gathered(chip_ref[0], wout_hbm, wout_own,
                                  lambda j: wout_ref.at[pl.ds(j * WOUT_BLK, WOUT_BLK), :],
                                  sems.at[pl.ds(N_CHIP, N_CHIP)])
            ext_lx[0:HALO, :] = jnp.zeros((HALO, D_LRU), F32)
            ext_cv[0:HALO, :] = jnp.zeros((HALO, D_LRU), F32)
            hcar[...] = jnp.zeros_like(hcar)
            for cp in cps:
                cp.wait()

        row = lax.broadcasted_iota(jnp.int32, (ts, D_LRU), 0)
        first_row = jnp.logical_and(row == 0, i == 0)
        xt = x_ref[...]
        shift1, scale1, gate1 = mod_ref[0:1, :], mod_ref[1:2, :], mod_ref[2:3, :]
        r1 = lax.rsqrt(jnp.mean(xt * xt, axis=-1, keepdims=True) + EPS)
        h = (xt * r1) * vd_ref[0:1, :] * (1.0 + scale1) + shift1
        hb = h.astype(BF16)
        hb_ref[...] = hb
        for j in range(N_CHIP):
            proj_ref[:, j * WIN_BLK:(j + 1) * WIN_BLK] = _dot(hb, win_ref[j])
        u_ly = proj_ref[:, 512:1024]
        u_b = proj_ref[:, 1024:1536]

        ext_lx[HALO:HALO + ts, :] = proj_ref[:, 0:512]
        xl = vl_ref[4:5, :] + vl_ref[0:1, :] * ext_lx[pl.ds(5, ts), :]
        for k in range(1, 4):
            xl = xl + vl_ref[k:k + 1, :] * ext_lx[pl.ds(5 + k, ts), :]
        ext_lx[0:HALO, :] = ext_lx[ts:ts + HALO, :]
        sp = _softplus(vl_ref[8:9, :])
        _, ig, a, _, mult = _lru_gates(xl.astype(BF16), gab_ref[...], vd_ref[3:4, :], sp, first_row)
        acum, hloc = _scan_fwd(a, mult * (ig * xl), row)
        hl = hloc + acum * hcar[0:1, :]
        hl_ref[...] = hl
        hcar[0:1, :] = hl_ref[ts - 1:ts, :]
        ge, _ = _gelu(u_ly)
        p = ge * hl
        y_lru = p * lax.rsqrt(_gmean(p * p, a64_ref[...]) + EPS) * vl_ref[9:10, :]
        ycat_ref[:, 0:512] = y_lru.astype(BF16)

        ext_cv[HALO:HALO + ts, :] = proj_ref[:, 1536:2048] * proj_ref[:, 2048:2560]
        q = vl_ref[5:6, :] * ext_cv[pl.ds(6, ts), :]
        for k in range(1, 3):
            q = q + vl_ref[5 + k:6 + k, :] * ext_cv[pl.ds(6 + k, ts), :]
        ext_cv[0:HALO, :] = ext_cv[ts:ts + HALO, :]
        yc = u_b * q
        y_conv = yc * lax.rsqrt(_gmean(yc * yc, a64_ref[...]) + EPS) * vl_ref[10:11, :]
        ycat_ref[:, 512:1024] = y_conv.astype(BF16)

        mixed = _dot(ycat_ref[...], wout_ref[...])
        mixed_ref[...] = mixed
        x1_ref[...] = xt + gate1 * mixed

    tile = lambda w: pl.BlockSpec((ts, w), lambda i: (i, 0))
    return pl.pallas_call(
        body, name="mix_fwd", grid=(nt,),
        in_specs=[SMEM, tile(D_MODEL), _full((8, D_MODEL)), _full((8, D_MODEL)), _full((16, D_LRU)),
                  ANY, ANY, ANY, ANY, _full((D_LRU, 2 * D_LRU), True), _full((D_LRU, D_LRU), True)],
        out_specs=[tile(D_MODEL), tile(D_IN), tile(D_LRU), tile(D_MODEL), tile(D_MODEL), tile(D_MODEL)],
        out_shape=[jax.ShapeDtypeStruct((s, D_MODEL), BF16), jax.ShapeDtypeStruct((s, D_IN), F32),
                   jax.ShapeDtypeStruct((s, D_LRU), F32), jax.ShapeDtypeStruct((s, D_MODEL), BF16),
                   jax.ShapeDtypeStruct((s, D_MODEL), F32), jax.ShapeDtypeStruct((s, D_MODEL), F32)],
        scratch_shapes=[pltpu.VMEM((N_CHIP, D_MODEL, WIN_BLK), BF16), pltpu.VMEM((D_MODEL, D_MODEL), BF16),
                        pltpu.VMEM((ts + HALO, D_LRU), F32), pltpu.VMEM((ts + HALO, D_LRU), F32),
                        pltpu.VMEM((HALO, D_LRU), F32), pltpu.SemaphoreType.DMA((2 * N_CHIP,))],
        compiler_params=pltpu.CompilerParams(dimension_semantics=("arbitrary",), vmem_limit_bytes=VMEM_LIMIT),
    )(chip, x, mod, vecd, vecl, *win, *wout, gab, a64)


def _mlp_fwd_bwd(chip, x1, target, mod, vecd, w1, w2):
    s = x1.shape[0]
    ts = TOKEN_TILE
    nt = s // ts

    def body(chip_ref, x1_ref, tg_ref, mod_ref, vd_ref, w1_hbm, w1_own, w2_hbm, w2_own,
             dx1_ref, act_ref, dz_ref, dmo_ref, h2_ref, acc_ref, w1_v, w2_v, rz_v, sems):
        i = pl.program_id(0)

        @pl.when(i == 0)
        def _():
            cps = _load_gathered(chip_ref[0], w1_hbm, w1_own, lambda j: w1_v.at[j], sems.at[pl.ds(0, N_CHIP)])
            cps += _load_gathered(chip_ref[0], w2_hbm, w2_own, lambda j: w2_v.at[j], sems.at[pl.ds(N_CHIP, N_CHIP)])
            acc_ref[...] = jnp.zeros_like(acc_ref)
            for cp in cps:
                cp.wait()

        xt = x1_ref[...]
        shift2, scale2, gate2 = mod_ref[3:4, :], mod_ref[4:5, :], mod_ref[5:6, :]
        g2, gf = vd_ref[1:2, :], vd_ref[2:3, :]
        r2 = lax.rsqrt(jnp.mean(xt * xt, axis=-1, keepdims=True) + EPS)
        n2 = xt * r2
        h2b = (n2 * g2 * (1.0 + scale2) + shift2).astype(BF16)
        h2_ref[...] = h2b
        for j in range(N_CHIP):
            rz_v[j] = jnp.maximum(_dot(h2b, w1_v[j]), 0.0)
        mo = jnp.zeros((ts, D_MODEL), F32)
        for j in range(N_CHIP):
            rz = rz_v[j]
            actb = (rz * rz).astype(BF16)
            act_ref[:, j * FF_BLK:(j + 1) * FF_BLK] = actb
            mo = mo + _dot(actb, w2_v[j])
        x2 = xt + gate2 * mo
        r3 = lax.rsqrt(jnp.mean(x2 * x2, axis=-1, keepdims=True) + EPS)
        n3 = x2 * r3
        e = n3 * gf - tg_ref[...]
        loss = (0.5 / D_MODEL) * jnp.sum(_colsum(e * e), axis=1, keepdims=True)
        dy = e * (1.0 / D_MODEL)
        acc_ref[4:5, :] += _colsum(dy * n3)
        acc_ref[5:6, :] += jnp.broadcast_to(loss, (1, D_MODEL))
        dn3 = dy * gf
        dx2 = r3 * (dn3 - n3 * jnp.mean(dn3 * n3, axis=-1, keepdims=True))
        acc_ref[2:3, :] += _colsum(dx2 * mo)
        dmob = (dx2 * gate2).astype(BF16)
        dmo_ref[...] = dmob
        for j in range(N_CHIP):
            dz_ref[:, j * FF_BLK:(j + 1) * FF_BLK] = (_dot_nt(dmob, w2_v[j]) * (2.0 * rz_v[j])).astype(BF16)
        dh2 = jnp.zeros((ts, D_MODEL), F32)
        for j in range(N_CHIP):
            dh2 = dh2 + _dot_nt(dz_ref[:, j * FF_BLK:(j + 1) * FF_BLK], w1_v[j])
        acc_ref[1:2, :] += _colsum(dh2 * (n2 * g2))
        acc_ref[0:1, :] += _colsum(dh2)
        dhn2 = dh2 * (1.0 + scale2)
        acc_ref[3:4, :] += _colsum(dhn2 * n2)
        dn2 = dhn2 * g2
        dx1_ref[...] = dx2 + r2 * (dn2 - n2 * jnp.mean(dn2 * n2, axis=-1, keepdims=True))

    tile = lambda w: pl.BlockSpec((ts, w), lambda i: (i, 0))
    return pl.pallas_call(
        body, name="mlp_fwd_bwd", grid=(nt,),
        in_specs=[SMEM, tile(D_MODEL), tile(D_MODEL), _full((8, D_MODEL)), _full((8, D_MODEL)), ANY, ANY, ANY, ANY],
        out_specs=[tile(D_MODEL), tile(D_FF), tile(D_FF), tile(D_MODEL), tile(D_MODEL), _full((8, D_MODEL))],
        out_shape=[jax.ShapeDtypeStruct((s, D_MODEL), F32), jax.ShapeDtypeStruct((s, D_FF), BF16),
                   jax.ShapeDtypeStruct((s, D_FF), BF16), jax.ShapeDtypeStruct((s, D_MODEL), BF16),
                   jax.ShapeDtypeStruct((s, D_MODEL), BF16), jax.ShapeDtypeStruct((8, D_MODEL), F32)],
        scratch_shapes=[pltpu.VMEM((N_CHIP, D_MODEL, FF_BLK), BF16), pltpu.VMEM((N_CHIP, FF_BLK, D_MODEL), BF16),
                        pltpu.VMEM((N_CHIP, ts, FF_BLK), F32), pltpu.SemaphoreType.DMA((2 * N_CHIP,))],
        compiler_params=pltpu.CompilerParams(dimension_semantics=("arbitrary",), vmem_limit_bytes=VMEM_LIMIT),
    )(chip, x1, target, mod, vecd, *w1, *w2)


def _mix_bwd(chip, dx1, x, mixed, proj, hl, hb, ycat, mod, vecd, vecl, win, wout, gab, a64):
    s = x.shape[0]
    ts = TOKEN_TILE
    nt = s // ts
    hpt = ts // HALO

    def body(chip_ref, dx1_ref, x_ref, mixed_ref, proj_ref, projh_ref, hl_ref, hlh_ref, hb_ref, ycat_ref,
             mod_ref, vd_ref, vl_ref, win_hbm, win_own, wout_hbm, wout_own, gab_ref, a64_ref,
             gx_ref, accd_ref, accl_ref, gwin_hbm, gwout_hbm, ggate_hbm,
             win_ref, wout_ref, dproj_ref, dgb_ref, gwin_acc, gwout_acc, ggate_acc,
             ext_lx, ext_cv, ext_hl, ext_dxl, ext_dq, gbuf, gcar, acar, sems):
        i = pl.program_id(0)
        ri = nt - 1 - i

        @pl.when(i == 0)
        def _():
            gwin_acc[...] = jnp.zeros_like(gwin_acc)
            gwout_acc[...] = jnp.zeros_like(gwout_acc)
            ggate_acc[...] = jnp.zeros_like(ggate_acc)
            cps = _load_gathered(chip_ref[0], win_hbm, win_own, lambda j: win_ref.at[j], sems.at[pl.ds(0, N_CHIP)])
            cps += _load_gathered(chip_ref[0], wout_hbm, wout_own,
                                  lambda j: wout_ref.at[pl.ds(j * WOUT_BLK, WOUT_BLK), :],
                                  sems.at[pl.ds(N_CHIP, N_CHIP)])
            for cp in cps:
                cp.wait()
            accd_ref[...] = jnp.zeros_like(accd_ref)
            accl_ref[...] = jnp.zeros_like(accl_ref)
            ext_dxl[ts:ts + HALO, :] = jnp.zeros((HALO, D_LRU), F32)
            ext_dq[ts:ts + HALO, :] = jnp.zeros((HALO, D_LRU), F32)
            gcar[...] = jnp.zeros_like(gcar)
            acar[...] = jnp.zeros_like(acar)

        row = lax.broadcasted_iota(jnp.int32, (ts, D_LRU), 0)
        first_row = jnp.logical_and(row == 0, ri == 0)
        halo_on = jnp.where(ri == 0, 0.0, 1.0)
        shift1, scale1, gate1 = mod_ref[0:1, :], mod_ref[1:2, :], mod_ref[2:3, :]
        g1 = vd_ref[0:1, :]
        a64m = a64_ref[...]
        lg, cg = vl_ref[9:10, :], vl_ref[10:11, :]

        dx1 = dx1_ref[...]
        accd_ref[2:3, :] += _colsum(dx1 * mixed_ref[...])
        dmb = (dx1 * gate1).astype(BF16)
        gwout_acc[...] += _dot_tn(ycat_ref[...], dmb)
        dycat = _dot_nt(dmb, wout_ref[...])
        dyl = dycat[:, 0:512]
        dyv = dycat[:, 512:1024]

        u_ly = proj_ref[:, 512:1024]
        u_b = proj_ref[:, 1024:1536]
        u_c = proj_ref[:, 1536:2048]
        u_v = proj_ref[:, 2048:2560]
        ext_lx[0:HALO, :] = projh_ref[:, 0:512] * halo_on
        ext_lx[HALO:HALO + ts, :] = proj_ref[:, 0:512]
        xl = vl_ref[4:5, :] + vl_ref[0:1, :] * ext_lx[pl.ds(5, ts), :]
        for k in range(1, 4):
            xl = xl + vl_ref[k:k + 1, :] * ext_lx[pl.ds(5 + k, ts), :]
        xlb = xl.astype(BF16)
        sp = _softplus(vl_ref[8:9, :])
        r, ig, a, msq, mult = _lru_gates(xlb, gab_ref[...], vd_ref[3:4, :], sp, first_row)
        hl = hl_ref[...]
        ge, th = _gelu(u_ly)
        p = ge * hl
        rl = lax.rsqrt(_gmean(p * p, a64m) + EPS)
        nl = p * rl
        ext_cv[0:HALO, :] = projh_ref[:, 1536:2048] * projh_ref[:, 2048:2560] * halo_on
        ext_cv[HALO:HALO + ts, :] = u_c * u_v
        q = vl_ref[5:6, :] * ext_cv[pl.ds(6, ts), :]
        for k in range(1, 3):
            q = q + vl_ref[5 + k:6 + k, :] * ext_cv[pl.ds(6 + k, ts), :]
        yc = u_b * q
        rc = lax.rsqrt(_gmean(yc * yc, a64m) + EPS)
        nc = yc * rc

        accl_ref[9:10, :] += _colsum(dyl * nl)
        dnl = dyl * lg
        dp = rl * (dnl - nl * _gmean(dnl * nl, a64m))
        dproj_ref[:, 512:1024] = ((dp * hl) * _gelu_grad(u_ly, th)).astype(BF16)
        a_next = jnp.where(row == ts - 1, acar[0:1, :], pltpu.roll(a, ts - 1, 0))
        acum, gloc = _scan_rev(a_next, dp * ge, row)
        gbuf[...] = gloc + acum * gcar[0:1, :]
        gcar[0:1, :] = gbuf[0:1, :]
        ext_hl[0:HALO, :] = hlh_ref[...] * halo_on
        ext_hl[HALO:HALO + ts, :] = hl
        acar[...] = a[0:HALO, :]
        gt = gbuf[...]
        da = gt * ext_hl[pl.ds(HALO - 1, ts), :]
        dmult = gt * ig * xl
        di = gt * mult * xl
        dxl = gt * mult * ig
        dla = da * a - jnp.where(first_row, 0.0, dmult * a * a / msq)
        accl_ref[8:9, :] += _colsum(dla * ((-C_GATE) * r))
        dra = dla * ((-C_GATE) * sp) * r * (1.0 - r)
        dia = di * ig * (1.0 - ig)
        accd_ref[4:5, 0:D_LRU] += _colsum(dra)
        accd_ref[4:5, D_LRU:2 * D_LRU] += _colsum(dia)
        dgb_ref[:, 0:D_LRU] = dra.astype(BF16)
        dgb_ref[:, D_LRU:2 * D_LRU] = dia.astype(BF16)
        dxl = dxl + _dot_nt(dgb_ref[...], gab_ref[...])
        ggate_acc[...] += _dot_tn(xlb, dgb_ref[...])
        accl_ref[4:5, :] += _colsum(dxl)
        for k in range(4):
            accl_ref[k:k + 1, :] += _colsum(dxl * ext_lx[pl.ds(5 + k, ts), :])
        ext_dxl[0:ts, :] = dxl
        du_lx = vl_ref[0:1, :] * ext_dxl[pl.ds(3, ts), :]
        for k in range(1, 4):
            du_lx = du_lx + vl_ref[k:k + 1, :] * ext_dxl[pl.ds(3 - k, ts), :]
        ext_dxl[ts:ts + HALO, :] = ext_dxl[0:HALO, :]
        dproj_ref[:, 0:512] = du_lx.astype(BF16)

        accl_ref[10:11, :] += _colsum(dyv * nc)
        dnc = dyv * cg
        dyc = rc * (dnc - nc * _gmean(dnc * nc, a64m))
        dproj_ref[:, 1024:1536] = (dyc * q).astype(BF16)
        dq = dyc * u_b
        for k in range(3):
            accl_ref[5 + k:6 + k, :] += _colsum(dq * ext_cv[pl.ds(6 + k, ts), :])
        ext_dq[0:ts, :] = dq
        dcv = vl_ref[5:6, :] * ext_dq[pl.ds(2, ts), :]
        for k in range(1, 3):
            dcv = dcv + vl_ref[5 + k:6 + k, :] * ext_dq[pl.ds(2 - k, ts), :]
        ext_dq[ts:ts + HALO, :] = ext_dq[0:HALO, :]
        dproj_ref[:, 1536:2048] = (dcv * u_v).astype(BF16)
        dproj_ref[:, 2048:2560] = (dcv * u_c).astype(BF16)

        dh = _dot_nt(dproj_ref[:, 0:WIN_BLK], win_ref[0])
        for j in range(1, N_CHIP):
            dh = dh + _dot_nt(dproj_ref[:, j * WIN_BLK:(j + 1) * WIN_BLK], win_ref[j])
        for j in range(N_CHIP):
            gwin_acc[j] += _dot_tn(hb_ref[...], dproj_ref[:, j * WIN_BLK:(j + 1) * WIN_BLK])
        xt = x_ref[...]
        r1 = lax.rsqrt(jnp.mean(xt * xt, axis=-1, keepdims=True) + EPS)
        n1 = xt * r1
        accd_ref[1:2, :] += _colsum(dh * (n1 * g1))
        accd_ref[0:1, :] += _colsum(dh)
        dhn1 = dh * (1.0 + scale1)
        accd_ref[3:4, :] += _colsum(dhn1 * n1)
        dn1 = dhn1 * g1
        gx_ref[...] = dx1 + r1 * (dn1 - n1 * jnp.mean(dn1 * n1, axis=-1, keepdims=True))

        @pl.when(i == nt - 1)
        def _():
            outs = [pltpu.make_async_copy(acc, dst, sems.at[k]) for k, (acc, dst) in enumerate(
                ((gwin_acc, gwin_hbm), (gwout_acc, gwout_hbm), (ggate_acc, ggate_hbm)))]
            for cp in outs:
                cp.start()
            for cp in outs:
                cp.wait()

    tile = lambda w: pl.BlockSpec((ts, w), lambda i: (nt - 1 - i, 0))
    halo = lambda w: pl.BlockSpec((HALO, w), lambda i: (jnp.maximum((nt - 1 - i) * hpt - 1, 0), 0))
    ext = pltpu.VMEM((ts + HALO, D_LRU), F32)
    return pl.pallas_call(
        body, name="mix_bwd", grid=(nt,),
        in_specs=[SMEM, tile(D_MODEL), tile(D_MODEL), tile(D_MODEL), tile(D_IN), halo(D_IN), tile(D_LRU), halo(D_LRU),
                  tile(D_MODEL), tile(D_MODEL), _full((8, D_MODEL)), _full((8, D_MODEL)), _full((16, D_LRU)),
                  ANY, ANY, ANY, ANY, _full((D_LRU, 2 * D_LRU), True), _full((D_LRU, D_LRU), True)],
        out_specs=[tile(D_MODEL), _full((8, D_MODEL)), _full((16, D_LRU)), ANY, ANY, ANY],
        out_shape=[jax.ShapeDtypeStruct((s, D_MODEL), F32),
                   jax.ShapeDtypeStruct((8, D_MODEL), F32), jax.ShapeDtypeStruct((16, D_LRU), F32),
                   jax.ShapeDtypeStruct((N_CHIP, D_MODEL, WIN_BLK), F32), jax.ShapeDtypeStruct((D_MODEL, D_MODEL), F32),
                   jax.ShapeDtypeStruct((D_LRU, 2 * D_LRU), F32)],
        scratch_shapes=[pltpu.VMEM((N_CHIP, D_MODEL, WIN_BLK), BF16), pltpu.VMEM((D_MODEL, D_MODEL), BF16),
                        pltpu.VMEM((ts, D_IN), BF16), pltpu.VMEM((ts, 2 * D_LRU), BF16),
                        pltpu.VMEM((N_CHIP, D_MODEL, WIN_BLK), F32), pltpu.VMEM((D_MODEL, D_MODEL), F32),
                        pltpu.VMEM((D_LRU, 2 * D_LRU), F32),
                        ext, ext, ext, ext, ext, pltpu.VMEM((ts, D_LRU), F32),
                        pltpu.VMEM((HALO, D_LRU), F32), pltpu.VMEM((HALO, D_LRU), F32),
                        pltpu.SemaphoreType.DMA((2 * N_CHIP,))],
        compiler_params=pltpu.CompilerParams(dimension_semantics=("arbitrary",), vmem_limit_bytes=VMEM_LIMIT),
    )(chip, dx1, x, mixed, proj, proj, hl, hl, hb, ycat, mod, vecd, vecl, *win, *wout, gab, a64)


def _wgrad(name, a, b, a_blk, b_blk, out_dtype):
    s = a.shape[0]
    aw = a_blk or a.shape[1]
    bw = b_blk or b.shape[1]
    nblk = N_CHIP if (a_blk or b_blk) else 1

    def body(a_ref, b_ref, o_ref):
        o_ref[0] = _dot_tn(a_ref[...], b_ref[...]).astype(out_dtype)

    return pl.pallas_call(
        body, name=name, grid=(nblk,),
        in_specs=[pl.BlockSpec((s, aw), (lambda j: (0, j)) if a_blk else (lambda j: (0, 0))),
                  pl.BlockSpec((s, bw), (lambda j: (0, j)) if b_blk else (lambda j: (0, 0)))],
        out_specs=pl.BlockSpec((1, aw, bw), lambda j: (j, 0, 0)),
        out_shape=jax.ShapeDtypeStruct((nblk, aw, bw), out_dtype),
        compiler_params=pltpu.CompilerParams(dimension_semantics=("arbitrary",), vmem_limit_bytes=VMEM_LIMIT),
    )(a, b)


def _mod_matmul(c_all, ada_w_loc):
    n = ada_w_loc.shape[1]
    cb = 512

    def body(c_ref, w_ref, o_ref):
        c = c_ref[...]
        sc = c * jax.nn.sigmoid(c)
        o_ref[...] = _dot(sc.astype(BF16), w_ref[...].astype(BF16))

    return pl.pallas_call(
        body, name="mod_matmul", grid=(n // cb,),
        in_specs=[_full((8, D_MODEL)), pl.BlockSpec((D_MODEL, cb), lambda j: (0, j))],
        out_specs=pl.BlockSpec((8, cb), lambda j: (0, j)),
        out_shape=jax.ShapeDtypeStruct((8, n), F32),
        compiler_params=pltpu.CompilerParams(dimension_semantics=("arbitrary",), vmem_limit_bytes=VMEM_LIMIT),
    )(c_all, ada_w_loc)


def _adam_math(w, g, m, v):
    m = ADAM_B1 * m + (1.0 - ADAM_B1) * g
    v = ADAM_B2 * v + (1.0 - ADAM_B2) * (g * g)
    m_hat = m / (1.0 - ADAM_B1 ** ADAM_STEP)
    v_hat = v / (1.0 - ADAM_B2 ** ADAM_STEP)
    delta = (-ADAM_LR) * (m_hat / (jnp.sqrt(v_hat) + ADAM_EPS) + ADAM_WD * w)
    return delta, m, v


def _adam(name, core, shards):
    n = len(shards)
    r, c = shards[0][0].shape
    half = r // 2
    rb = min(half, 128)
    nh = half // rb

    def body(core_ref, *refs):
        ins, outs = refs[:5 * n], refs[5 * n:]
        mine = (pl.program_id(0) // nh) == core_ref[0]
        for k in range(n):
            w_ref, go_ref, gs_ref, m_ref, v_ref = ins[5 * k:5 * k + 5]
            g_ref, d_ref, mo_ref, vo_ref = outs[4 * k:4 * k + 4]
            g = jnp.where(mine, go_ref[...], gs_ref[...])
            g_ref[...] = g
            d_ref[...], mo_ref[...], vo_ref[...] = _adam_math(w_ref[...], g, m_ref[...], v_ref[...])

    spec = pl.BlockSpec((rb, c), lambda i, core_ref: (i, 0))
    hspec = pl.BlockSpec((rb, c), lambda i, core_ref: (i % nh, 0))
    sds = jax.ShapeDtypeStruct((r, c), F32)
    res = pl.pallas_call(
        body, name=name,
        grid_spec=pltpu.PrefetchScalarGridSpec(
            num_scalar_prefetch=1, grid=(r // rb,),
            in_specs=[spec, hspec, hspec, spec, spec] * n, out_specs=[spec] * (4 * n)),
        out_shape=[sds] * (4 * n),
        compiler_params=pltpu.CompilerParams(dimension_semantics=("arbitrary",), vmem_limit_bytes=VMEM_LIMIT),
    )(core, *[t for s in shards for t in s])
    return [res[4 * k:4 * k + 4] for k in range(n)]


def _ada_grad_adam(sct, dmod_loc, w, m, v):
    r, c = w.shape
    rb = 128

    def body(s_ref, dm_ref, w_ref, m_ref, v_ref, g_ref, d_ref, mo_ref, vo_ref):
        g = s_ref[:, 0:1] * dm_ref[0:1, :]
        for b in range(1, 8):
            g = g + s_ref[:, b:b + 1] * dm_ref[b:b + 1, :]
        g_ref[...] = g
        d_ref[...], mo_ref[...], vo_ref[...] = _adam_math(w_ref[...], g, m_ref[...], v_ref[...])

    spec = pl.BlockSpec((rb, c), lambda i: (i, 0))
    sds = jax.ShapeDtypeStruct((r, c), F32)
    return pl.pallas_call(
        body, name="ada_grad_adam", grid=(r // rb,),
        in_specs=[pl.BlockSpec((rb, 8), lambda i: (i, 0)), _full((8, c)), spec, spec, spec],
        out_specs=[spec] * 4, out_shape=[sds] * 4,
        compiler_params=pltpu.CompilerParams(dimension_semantics=("arbitrary",), vmem_limit_bytes=VMEM_LIMIT),
    )(sct, dmod_loc, w, m, v)


def _position():
    x, y, c = lax.axis_index("x"), lax.axis_index("y"), lax.axis_index("c")
    chips = [(1 - x, y), (x, 1 - y), (1 - x, 1 - y)]
    return x, y, c, chips


def _ag8_run(ins, outs, send_sems, recv_sems, local_sems):
    na = len(ins)
    x, y, c, chips = _position()
    me, sibling = (x, y, c), (x, y, 1 - c)
    first, passed, local = [], [], []
    for a in range(na):
        m_per = ins[a].shape[0]

        def rows(px, py, pc, a=a, m_per=m_per):
            return outs[a].at[pl.ds((4 * px + 2 * py + pc) * m_per, m_per), :]

        def copy(k, block, to, src=None, a=a, rows=rows):
            return pltpu.make_async_remote_copy(
                src_ref=rows(*block) if src is None else src, dst_ref=rows(*block),
                send_sem=send_sems.at[7 * a + k], recv_sem=recv_sems.at[7 * a + k],
                device_id=to, device_id_type=MESH)

        mine = pltpu.make_async_copy(ins[a], rows(*me), local_sems.at[a])
        mine.start()
        local.append(mine)
        f = [copy(0, me, sibling, src=ins[a])]
        f += [copy(1 + j, me, (*chip, c), src=ins[a]) for j, chip in enumerate(chips)]
        for cp in f:
            cp.start()
        first.append((f, copy))
    for a in range(na):
        f, copy = first[a]
        p = [copy(4 + j, (*chip, c), sibling) for j, chip in enumerate(chips)]
        for j, chip in enumerate(chips):
            copy(1 + j, (*chip, c), me).wait_recv()
            p[j].start()
        passed.append(p)
    for a in range(na):
        f, copy = first[a]
        copy(0, sibling, me).wait_recv()
        for j, chip in enumerate(chips):
            copy(4 + j, (*chip, 1 - c), me).wait_recv()
        for cp in f + passed[a]:
            cp.wait_send()
        local[a].wait()


def _allgather8_seq(name, collective_id, arrs):
    na = len(arrs)
    hbm = pltpu.MemorySpace.HBM
    ins = [jax.new_ref(a, memory_space=hbm) for a in arrs]
    outs = [jax.empty_ref(jax.ShapeDtypeStruct((8 * a.shape[0], a.shape[1]), a.dtype), memory_space=hbm) for a in arrs]

    @pl.kernel(mesh=plsc.ScalarSubcoreMesh(axis_name="sequencer", num_cores=1), name=name,
               scratch_types=(pltpu.SemaphoreType.DMA((7 * na,)), pltpu.SemaphoreType.DMA((7 * na,)),
                              pltpu.SemaphoreType.DMA((na,))),
               compiler_params=pltpu.CompilerParams(collective_id=collective_id))
    def launch(send_sems, recv_sems, local_sems):
        x, y, c, chips = _position()
        peers = [(x, y, 1 - c)] + [(*chip, c) for chip in chips]
        barrier = pltpu.get_barrier_semaphore()
        for peer in peers:
            pl.semaphore_signal(barrier, inc=1, device_id=peer, device_id_type=MESH)
        pl.semaphore_wait(barrier, len(peers))
        _ag8_run(ins, outs, send_sems, recv_sems, local_sems)

    launch()
    return [o[...] for o in outs]


def _allgather8(name, arrs):
    na = len(arrs)

    def body(*refs):
        _ag8_run(refs[:na], refs[na:2 * na], *refs[2 * na:])

    return pl.pallas_call(
        body, name=name,
        out_shape=[jax.ShapeDtypeStruct((8 * a.shape[0], a.shape[1]), a.dtype) for a in arrs],
        in_specs=[VMEM] * na, out_specs=[VMEM] * na,
        scratch_shapes=[pltpu.SemaphoreType.DMA((7 * na,)), pltpu.SemaphoreType.DMA((7 * na,)),
                        pltpu.SemaphoreType.DMA((na,))],
        compiler_params=pltpu.CompilerParams(vmem_limit_bytes=VMEM_LIMIT),
    )(*arrs)


AG_SEMS = 7


def _ag_copies(ins, outs, send_sems, recv_sems):
    x, y, c, chips = _position()
    sibling = (x, y, 1 - c)
    xn, yn, dg = [2 * chip[0] + chip[1] for chip in chips]
    to_x, to_y = (1 - x, y, c), (x, 1 - y, c)
    res = []
    for a in range(len(ins)):
        half = ins[a].shape[0] // 2
        quarter = half // 2

        def copy(k, dst, to, src=None, a=a):
            return pltpu.make_async_remote_copy(
                src_ref=dst if src is None else src, dst_ref=dst,
                send_sem=send_sems.at[AG_SEMS * a + k], recv_sem=recv_sems.at[AG_SEMS * a + k],
                device_id=to, device_id_type=MESH)

        def rows(chip, pc, q=None, a=a, half=half, quarter=quarter):
            if q is None:
                return outs[a].at[chip, pl.ds(pc * half, half), :]
            return outs[a].at[chip, pl.ds(pc * half + q * quarter, quarter), :]

        own = ins[a].at[pl.ds(c * half, half), :]
        mine = rows(2 * x + y, c)
        res.append(dict(
            sends=[copy(0, mine, to_x, src=own), copy(1, mine, to_y, src=own)],
            from_x=copy(0, rows(xn, c), to_x), from_y=copy(1, rows(yn, c), to_y),
            relay_y=copy(2, rows(xn, c, 0), to_y), relay_x=copy(3, rows(yn, c, 1), to_x),
            from_y_relay=copy(2, rows(dg, c, 0), to_y), from_x_relay=copy(3, rows(dg, c, 1), to_x),
            pass_on=[copy(4, rows(xn, c), sibling), copy(5, rows(yn, c), sibling), copy(6, rows(dg, c), sibling)],
            from_sibling=[copy(4, rows(xn, 1 - c), sibling), copy(5, rows(yn, 1 - c), sibling),
                          copy(6, rows(dg, 1 - c), sibling)]))
    return res


def _ag_start(ins, outs, send_sems, recv_sems):
    for cps in _ag_copies(ins, outs, send_sems, recv_sems):
        for cp in cps["sends"]:
            cp.start()


def _ag_relay(ins, outs, send_sems, recv_sems, which):
    copies = _ag_copies(ins, outs, send_sems, recv_sems)
    for a in which:
        cps = copies[a]
        cps["from_x"].wait_recv()
        cps["relay_y"].start()
        cps["pass_on"][0].start()
        cps["from_y"].wait_recv()
        cps["relay_x"].start()
        cps["pass_on"][1].start()


def _ag_complete(ins, outs, send_sems, recv_sems):
    copies = _ag_copies(ins, outs, send_sems, recv_sems)
    for cps in copies:
        cps["from_y_relay"].wait_recv()
        cps["from_x_relay"].wait_recv()
        cps["pass_on"][2].start()
    for cps in copies:
        for cp in cps["from_sibling"]:
            cp.wait_recv()
        for cp in cps["sends"] + [cps["relay_y"], cps["relay_x"]] + cps["pass_on"]:
            cp.wait_send()


def _ag_finish(ins, outs, send_sems, recv_sems):
    _ag_relay(ins, outs, send_sems, recv_sems, range(len(ins)))
    _ag_complete(ins, outs, send_sems, recv_sems)


def _allgather_weights(name, collective_id, shards):
    na = len(shards)
    hbm = pltpu.MemorySpace.HBM
    ins = [jax.new_ref(s, memory_space=hbm) for s in shards]
    outs = [jax.empty_ref(jax.ShapeDtypeStruct((N_CHIP,) + s.shape, s.dtype), memory_space=hbm) for s in shards]

    @pl.kernel(mesh=plsc.ScalarSubcoreMesh(axis_name="sequencer", num_cores=1), name=name,
               scratch_types=(pltpu.SemaphoreType.DMA((AG_SEMS * na,)), pltpu.SemaphoreType.DMA((AG_SEMS * na,))),
               compiler_params=pltpu.CompilerParams(collective_id=collective_id))
    def launch(send_sems, recv_sems):
        x, y, c, _ = _position()
        peers = [(1 - x, y, c), (x, 1 - y, c), (x, y, 1 - c)]
        barrier = pltpu.get_barrier_semaphore()
        for peer in peers:
            pl.semaphore_signal(barrier, inc=1, device_id=peer, device_id_type=MESH)
        pl.semaphore_wait(barrier, len(peers))
        _ag_start(ins, outs, send_sems, recv_sems)
        _ag_finish(ins, outs, send_sems, recv_sems)

    launch()
    return [o[...] for o in outs]


def _swap_halves(name, grads, after=()):
    na, nw = len(grads), len(after)

    def body(*refs):
        ins, outs = refs[:na], refs[na + nw:2 * na + nw]
        send_sems, recv_sems = refs[2 * na + nw:]
        x, y, c, _ = _position()
        cps = []
        for a in range(na):
            half = ins[a].shape[1] // 2
            cp = pltpu.make_async_remote_copy(
                src_ref=ins[a].at[:, pl.ds((1 - c) * half, half), :], dst_ref=outs[a],
                send_sem=send_sems.at[a], recv_sem=recv_sems.at[a],
                device_id=(x, y, 1 - c), device_id_type=MESH)
            cp.start()
            cps.append(cp)
        for cp in cps:
            cp.wait()

    return pl.pallas_call(
        body, name=name,
        out_shape=[jax.ShapeDtypeStruct((g.shape[0], g.shape[1] // 2, g.shape[2]), g.dtype) for g in grads],
        in_specs=[ANY] * (na + nw), out_specs=[ANY] * na,
        scratch_shapes=[pltpu.SemaphoreType.DMA((na,)), pltpu.SemaphoreType.DMA((na,))],
    )(*grads, *after)


def _xchg_copies(ins, outs, send_sems, recv_sems):
    x, y, c, chips = _position()
    return [pltpu.make_async_remote_copy(
        src_ref=ins[a].at[2 * chip[0] + chip[1]], dst_ref=outs[a].at[j],
        send_sem=send_sems.at[3 * a + j], recv_sem=recv_sems.at[3 * a + j],
        device_id=(*chip, c), device_id_type=MESH) for a in range(len(ins)) for j, chip in enumerate(chips)]


def _exchange_chips(name, collective_id, parts):
    na = len(parts)
    hbm = pltpu.MemorySpace.HBM
    ins = [jax.new_ref(p, memory_space=hbm) for p in parts]
    outs = [jax.empty_ref(jax.ShapeDtypeStruct((3,) + p.shape[1:], p.dtype), memory_space=hbm) for p in parts]

    @pl.kernel(mesh=plsc.ScalarSubcoreMesh(axis_name="sequencer", num_cores=1), name=name,
               scratch_types=(pltpu.SemaphoreType.DMA((3 * na,)), pltpu.SemaphoreType.DMA((3 * na,))),
               compiler_params=pltpu.CompilerParams(collective_id=collective_id))
    def launch(send_sems, recv_sems):
        x, y, c, chips = _position()
        barrier = pltpu.get_barrier_semaphore()
        for chip in chips:
            pl.semaphore_signal(barrier, inc=1, device_id=(*chip, c), device_id_type=MESH)
        pl.semaphore_wait(barrier, len(chips))
        for cp in _xchg_copies(ins, outs, send_sems, recv_sems):
            cp.start()
        for cp in _xchg_copies(ins, outs, send_sems, recv_sems):
            cp.wait()

    launch()
    return [q[...] for q in outs]


def _swap_reduced(name, halves):
    na = len(halves)

    def body(*refs):
        ins, outs = refs[:na], refs[na:2 * na]
        send_sems, recv_sems = refs[2 * na:]
        x, y, c, _ = _position()
        cps = []
        for a in range(na):
            cp = pltpu.make_async_remote_copy(
                src_ref=ins[a], dst_ref=outs[a], send_sem=send_sems.at[a], recv_sem=recv_sems.at[a],
                device_id=(x, y, 1 - c), device_id_type=MESH)
            cp.start()
            cps.append(cp)
        for cp in cps:
            cp.wait()

    return pl.pallas_call(
        body, name=name,
        out_shape=[jax.ShapeDtypeStruct(h.shape, h.dtype) for h in halves],
        in_specs=[ANY] * na, out_specs=[ANY] * na,
        scratch_shapes=[pltpu.SemaphoreType.DMA((na,)), pltpu.SemaphoreType.DMA((na,))],
    )(*halves)


def _add_sibling(name, grad, recv, core):
    _, r, c = grad.shape
    half = r // 2
    rb = min(half, 256)
    nrb = half // rb

    def body(core_ref, g_ref, r_ref, o_ref):
        o_ref[...] = (g_ref[...].astype(F32) + r_ref[...].astype(F32)).astype(BF16)

    return pl.pallas_call(
        body, name=name,
        grid_spec=pltpu.PrefetchScalarGridSpec(
            num_scalar_prefetch=1, grid=(N_CHIP, nrb),
            in_specs=[pl.BlockSpec((1, rb, c), lambda j, i, core_ref: (j, core_ref[0] * nrb + i, 0)),
                      pl.BlockSpec((1, rb, c), lambda j, i, core_ref: (j, i, 0))],
            out_specs=pl.BlockSpec((1, rb, c), lambda j, i, core_ref: (j, i, 0))),
        out_shape=jax.ShapeDtypeStruct((N_CHIP, half, c), BF16),
        compiler_params=pltpu.CompilerParams(dimension_semantics=("arbitrary", "arbitrary"),
                                             vmem_limit_bytes=VMEM_LIMIT),
    )(core, grad, recv)


def _add_chips(name, chip, p, q):
    _, half, c = q.shape
    rb = min(half, 256)

    def body(chip_ref, p_ref, q_ref, o_ref):
        acc = p_ref[0].astype(F32)
        for j in range(3):
            acc = acc + q_ref[j].astype(F32)
        o_ref[...] = acc

    return pl.pallas_call(
        body, name=name,
        grid_spec=pltpu.PrefetchScalarGridSpec(
            num_scalar_prefetch=1, grid=(half // rb,),
            in_specs=[pl.BlockSpec((1, rb, c), lambda i, chip_ref: (chip_ref[0], i, 0)),
                      pl.BlockSpec((3, rb, c), lambda i, chip_ref: (0, i, 0))],
            out_specs=pl.BlockSpec((rb, c), lambda i, chip_ref: (i, 0))),
        out_shape=jax.ShapeDtypeStruct((half, c), F32),
        compiler_params=pltpu.CompilerParams(dimension_semantics=("arbitrary",), vmem_limit_bytes=VMEM_LIMIT),
    )(chip, p, q)


def _small_update(gad, gam, gl, gg, mychip, params):
    names = ["ada_b", "norm1_g", "lru_conv_b", "gate_a_w", "gate_a_b", "gate_x_w", "gate_x_b", "a_param",
             "lru_conv_w", "short_conv_w", "lru_out_g", "conv_out_g", "norm2_g", "final_g"]
    flat = [t for n in names for t in params[n]]
    nin = len(flat)

    def body(chip_ref, gad_ref, gam_ref, gl_ref, gg_ref, *refs):
        ins = {n: refs[3 * k:3 * k + 3] for k, n in enumerate(names)}
        outs = {n: refs[nin + 4 * k:nin + 4 * k + 4] for k, n in enumerate(names)}
        loss_ref, dmod_ref = refs[nin + 4 * len(names):nin + 4 * len(names) + 2]

        def dsum(ref, lo, n):
            per = ref.shape[0] // 8
            acc = ref[lo:lo + n, :].astype(F32)
            for dev in range(1, 8):
                acc = acc + ref[dev * per + lo:dev * per + lo + n, :].astype(F32)
            return acc

        def update(n, g):
            w_ref, m_ref, v_ref = ins[n]
            g_ref, d_ref, mo_ref, vo_ref = outs[n]
            g_ref[...] = g
            d_ref[...], mo_ref[...], vo_ref[...] = _adam_math(w_ref[...], g, m_ref[...], v_ref[...])

        d, dm, l, lw = refs[-4:]
        d[...] = dsum(gad_ref, 0, 8)
        dm[...] = dsum(gam_ref, 0, 8)
        l[...] = dsum(gl_ref, 0, 16)
        for dev in range(8):
            for k in range(3):
                dmod_ref[dev:dev + 1, k * D_MODEL:(k + 1) * D_MODEL] = gad_ref[dev * 8 + k:dev * 8 + k + 1, :]
                dmod_ref[dev:dev + 1, (3 + k) * D_MODEL:(4 + k) * D_MODEL] = gam_ref[dev * 8 + k:dev * 8 + k + 1, :]
        w_ref, m_ref, v_ref = ins["ada_b"]
        g_ref, d_ref, mo_ref, vo_ref = outs["ada_b"]
        for k in range(3):
            g_ref[:, k * D_MODEL:(k + 1) * D_MODEL] = d[k:k + 1, :]
            g_ref[:, (3 + k) * D_MODEL:(4 + k) * D_MODEL] = dm[k:k + 1, :]
        d_ref[...], mo_ref[...], vo_ref[...] = _adam_math(w_ref[...], g_ref[...], m_ref[...], v_ref[...])
        update("norm1_g", d[3:4, :])
        update("norm2_g", dm[3:4, :])
        update("final_g", dm[4:5, :])
        update("gate_a_b", d[4:5, 0:D_LRU])
        update("gate_x_b", d[4:5, D_LRU:2 * D_LRU])
        update("lru_conv_b", l[4:5, :])
        update("a_param", l[8:9, :] * jax.nn.sigmoid(ins["a_param"][0][...]))
        update("lru_out_g", l[9:10, :])
        update("conv_out_g", l[10:11, :])
        loss_ref[...] = jnp.broadcast_to(dm[5:6, 0:128], (8, 128))
        chip = chip_ref[0]
        acc = jnp.zeros((8, 128), F32)
        for j in range(N_CHIP):
            acc = acc + jnp.where(chip == j, l[0:8, j * 128:(j + 1) * 128], 0.0)
        lw[...] = acc
        update("lru_conv_w", lw[0:4, :])
        update("short_conv_w", lw[5:8, :])
        gates = dsum(gg_ref, 0, D_LRU)
        update("gate_a_w", gates[:, 0:HEAD])
        update("gate_x_w", gates[:, HEAD:2 * HEAD])

    out_shape = []
    for n in names:
        out_shape += [jax.ShapeDtypeStruct(params[n][0].shape, F32)] * 4
    out_shape += [jax.ShapeDtypeStruct((8, 128), F32), jax.ShapeDtypeStruct((8, 6 * D_MODEL), F32)]
    res = pl.pallas_call(
        body, name="small_update", out_shape=out_shape,
        in_specs=[SMEM] + [VMEM] * (4 + nin),
        out_specs=[VMEM] * len(out_shape),
        scratch_shapes=[pltpu.VMEM((8, D_MODEL), F32), pltpu.VMEM((8, D_MODEL), F32), pltpu.VMEM((16, D_LRU), F32),
                        pltpu.VMEM((8, 128), F32)],
        compiler_params=pltpu.CompilerParams(vmem_limit_bytes=VMEM_LIMIT),
    )(mychip, gad, gam, gl, gg, *flat)
    per = {n: res[4 * k:4 * k + 4] for k, n in enumerate(names)}
    return per, res[-2], res[-1]


def _block_diag(w):
    eye = jnp.eye(8, dtype=w.dtype)
    return (eye[:, None, :, None] * w[:, :, None, :]).reshape(8 * HEAD, 8 * HEAD)


def _diag_blocks(g):
    return jnp.concatenate([g[h * HEAD:(h + 1) * HEAD, h * HEAD:(h + 1) * HEAD] for h in range(8)], axis=0)


def kernel(x, c, ada_w, ada_b, norm1_g, w_in, lru_conv_w, lru_conv_b, gate_a_w, gate_a_b, gate_x_w, gate_x_b, a_param, short_conv_w, lru_out_g, conv_out_g, w_out, norm2_g, w_mlp1, w_mlp2, final_g, loss_target, m_ada_w, m_ada_b, m_norm1_g, m_w_in, m_lru_conv_w, m_lru_conv_b, m_gate_a_w, m_gate_a_b, m_gate_x_w, m_gate_x_b, m_a_param, m_short_conv_w, m_lru_out_g, m_conv_out_g, m_w_out, m_norm2_g, m_w_mlp1, m_w_mlp2, m_final_g, v_ada_w, v_ada_b, v_norm1_g, v_w_in, v_lru_conv_w, v_lru_conv_b, v_gate_a_w, v_gate_a_b, v_gate_x_w, v_gate_x_b, v_a_param, v_short_conv_w, v_lru_out_g, v_conv_out_g, v_w_out, v_norm2_g, v_w_mlp1, v_w_mlp2, v_final_g):
    xi, yi, ci = lax.axis_index("x"), lax.axis_index("y"), lax.axis_index("c")
    mychip = 2 * xi + yi
    me = 4 * xi + 2 * yi + ci

    own_in, own_out = w_in[0].astype(BF16), w_out[0].astype(BF16)
    win_all, wout_all = _allgather_weights("allgather_mixer_weights", 1, [own_in, own_out])
    own_w1, own_w2 = w_mlp1[0].astype(BF16), w_mlp2[0].astype(BF16)
    w1_all, w2_all = _allgather_weights("allgather_mlp_weights", 2, [own_w1, own_w2])

    c_blk = jnp.zeros((8, D_MODEL), F32).at[0:1].set(c)
    cw_blk = jnp.zeros((8, 128), F32).at[0:4].set(lru_conv_w[0]).at[4:7].set(short_conv_w[0])
    c_g, cw_g = _allgather8("allgather_cond", [c_blk, cw_blk])
    c_all = c_g.reshape(8, 8, D_MODEL)[:, 0]
    cw_g = cw_g.reshape(4, 2, 8, 128)[:, 0]
    lcw = cw_g[:, 0:4].transpose(1, 0, 2).reshape(4, D_LRU)
    scw = cw_g[:, 4:7].transpose(1, 0, 2).reshape(3, D_LRU)

    mod_loc = _mod_matmul(c_all, ada_w[0])
    (mod_g,) = _allgather8("allgather_mod", [mod_loc])
    mod_all = mod_g.reshape(4, 2, 8, 6 * D_MODEL // 4)[:, 0].transpose(1, 0, 2).reshape(8, 6 * D_MODEL) + ada_b
    mod_pad = jnp.pad(mod_all.reshape(8, 6, D_MODEL), ((0, 0), (0, 2), (0, 0)))
    mod = lax.dynamic_slice_in_dim(mod_pad, me, 1, axis=0).reshape(8, D_MODEL)

    win, wout = (win_all, own_in), (wout_all, own_out)
    chip = mychip.reshape(1).astype(jnp.int32)
    core = ci.reshape(1).astype(jnp.int32)

    vecd = jnp.concatenate([norm1_g, norm2_g, final_g[None, :], jnp.concatenate([gate_a_b, gate_x_b], axis=1),
                            jnp.zeros((4, D_MODEL), F32)], axis=0)
    vecl = jnp.concatenate([lcw, lru_conv_b, scw, a_param, lru_out_g, conv_out_g, jnp.zeros((5, D_LRU), F32)], axis=0)
    gab = jnp.concatenate([_block_diag(gate_a_w[0]), _block_diag(gate_x_w[0])], axis=1).astype(BF16)
    a64 = _block_diag(jnp.full((8, HEAD, HEAD), 1.0 / HEAD, F32)).astype(BF16)

    hb, proj, hl, ycat, mixed, x1 = _mix_fwd(chip, x[0], mod, vecd, vecl, win, wout, gab, a64)
    dx1, act, dz, dmo, h2b, accm = _mlp_fwd_bwd(
        chip, x1, loss_target[0], mod, vecd, (w1_all, own_w1), (w2_all, own_w2))

    def sibling_sum(tag, grads, after=()):
        recv = _swap_halves("rs_swap_halves_" + tag, grads, after)
        return [_add_sibling("rs_add_sibling_%s%d" % (tag, k), g, r, core) for k, (g, r) in enumerate(zip(grads, recv))]

    parts_mlp = sibling_sum("mlp", [_wgrad("wgrad_mlp1", h2b, dz, 0, FF_BLK, BF16),
                                    _wgrad("wgrad_mlp2", act, dmo, FF_BLK, 0, BF16)])
    q_w1, q_w2 = _exchange_chips("rs_exchange_mlp", 0, parts_mlp)
    grad_x, accd, accl, g_win, g_wout, g_gate = _mix_bwd(
        chip, dx1, x[0], mixed, proj, hl, hb, ycat, mod, vecd, vecl, win, wout, gab, a64)

    gg_blk = jnp.concatenate([_diag_blocks(g_gate[:, 0:D_LRU]), _diag_blocks(g_gate[:, D_LRU:2 * D_LRU])], axis=1)
    gad, gam, gl, gg = _allgather8_seq("allgather_small_grads", 4, [accd, accm, accl, gg_blk.astype(BF16)])
    parts_mix = sibling_sum("mix", [g_win, g_wout.reshape(N_CHIP, WOUT_BLK, D_MODEL)])

    def reduced(tag, parts, landed):
        own = [_add_chips("rs_add_chips_%s%d" % (tag, k), chip, p, q) for k, (p, q) in enumerate(zip(parts, landed))]
        return own, _swap_reduced("rs_swap_reduced_" + tag, own)

    landed_mix = _exchange_chips("rs_exchange_mix", 3, parts_mix)
    own_mlp, sib_mlp = reduced("mlp", parts_mlp, [q_w1, q_w2])
    res_w1, res_w2 = _adam("adam_mlp", core, [(w_mlp1[0], own_mlp[0], sib_mlp[0], m_w_mlp1[0], v_w_mlp1[0]),
                                              (w_mlp2[0], own_mlp[1], sib_mlp[1], m_w_mlp2[0], v_w_mlp2[0])])
    own_mix, sib_mix = reduced("mix", parts_mix, landed_mix)
    (res_win,) = _adam("adam_w_in", core, [(w_in[0], own_mix[0], sib_mix[0], m_w_in[0], v_w_in[0])])
    (res_wout,) = _adam("adam_w_out", core, [(w_out[0], own_mix[1], sib_mix[1], m_w_out[0], v_w_out[0])])

    params = {
        "ada_b": (ada_b, m_ada_b, v_ada_b), "norm1_g": (norm1_g, m_norm1_g, v_norm1_g),
        "lru_conv_b": (lru_conv_b, m_lru_conv_b, v_lru_conv_b),
        "gate_a_w": tuple(t.reshape(D_LRU, HEAD) for t in (gate_a_w, m_gate_a_w, v_gate_a_w)),
        "gate_a_b": (gate_a_b, m_gate_a_b, v_gate_a_b),
        "gate_x_w": tuple(t.reshape(D_LRU, HEAD) for t in (gate_x_w, m_gate_x_w, v_gate_x_w)),
        "gate_x_b": (gate_x_b, m_gate_x_b, v_gate_x_b), "a_param": (a_param, m_a_param, v_a_param),
        "lru_conv_w": tuple(t[0] for t in (lru_conv_w, m_lru_conv_w, v_lru_conv_w)),
        "short_conv_w": tuple(t[0] for t in (short_conv_w, m_short_conv_w, v_short_conv_w)),
        "lru_out_g": (lru_out_g, m_lru_out_g, v_lru_out_g), "conv_out_g": (conv_out_g, m_conv_out_g, v_conv_out_g),
        "norm2_g": (norm2_g, m_norm2_g, v_norm2_g),
        "final_g": tuple(t[None, :] for t in (final_g, m_final_g, v_final_g)),
    }
    small, loss_blk, dmod_cols = _small_update(gad, gam, gl, gg, chip, params)
    loss = loss_blk[0, 0]

    ncol = 6 * D_MODEL // N_CHIP
    dmod_loc = lax.dynamic_slice_in_dim(dmod_cols, mychip * ncol, ncol, axis=1)
    sct = (c_all * jax.nn.sigmoid(c_all)).T
    ada = _ada_grad_adam(sct, dmod_loc, ada_w[0], m_ada_w[0], v_ada_w[0])

    res = {"ada_w": ada, "w_in": res_win, "w_out": res_wout, "w_mlp1": res_w1, "w_mlp2": res_w2}
    res = {n: tuple(t[None] for t in r) for n, r in res.items()}
    shapes = {"gate_a_w": gate_a_w.shape, "gate_x_w": gate_x_w.shape, "lru_conv_w": lru_conv_w.shape,
              "short_conv_w": short_conv_w.shape, "final_g": final_g.shape}
    for n, t in small.items():
        res[n] = tuple(u.reshape(shapes[n]) if n in shapes else u for u in t)

    order = ["ada_w", "ada_b", "norm1_g", "w_in", "lru_conv_w", "lru_conv_b", "gate_a_w", "gate_a_b", "gate_x_w",
             "gate_x_b", "a_param", "short_conv_w", "lru_out_g", "conv_out_g", "w_out", "norm2_g", "w_mlp1",
             "w_mlp2", "final_g"]
    return (loss, grad_x[None], *[res[n][0] for n in order], *[res[n][1] for n in order],
            *[res[n][2] for n in order], *[res[n][3] for n in order])
```

```python
import jax
import jax.numpy as jnp
from jax import lax
from jax.experimental import pallas as pl
from jax.experimental.pallas import tpu as pltpu
from jax.experimental.pallas import tpu_sc as plsc

F32 = jnp.float32
BF16 = jnp.bfloat16

D_MODEL = 1024
D_LRU = 512
D_IN = 2560
D_FF = 4096
N_CHIP = 4
WIN_BLK = D_IN // N_CHIP
WOUT_BLK = D_MODEL // N_CHIP
FF_BLK = D_FF // N_CHIP
HEAD = 64
EPS = 1e-6
C_GATE = 8.0
TOKEN_TILE = 256
HALO = 8
VMEM_LIMIT = 60 * 1024 * 1024

ADAM_LR = 0.001
ADAM_B1 = 0.9
ADAM_B2 = 0.999
ADAM_EPS = 1e-08
ADAM_WD = 0.01
ADAM_STEP = 10

MESH = pl.DeviceIdType.MESH
ANY = pl.BlockSpec(memory_space=pl.ANY)
VMEM = pl.BlockSpec(memory_space=pltpu.VMEM)
SMEM = pl.BlockSpec(memory_space=pltpu.SMEM)


def _full(shape, single=False):
    nd = len(shape)
    if single:
        return pl.BlockSpec(shape, lambda *_: (0,) * nd, pipeline_mode=pl.Buffered(1))
    return pl.BlockSpec(shape, lambda *_: (0,) * nd)


def _dot(a, b):
    return jnp.dot(a, b, preferred_element_type=F32)


def _dot_nt(a, b):
    return lax.dot_general(a, b, (((1,), (1,)), ((), ())), preferred_element_type=F32)


def _dot_tn(a, b):
    return lax.dot_general(a, b, (((0,), (0,)), ((), ())), preferred_element_type=F32)


def _gmean(v, a64):
    hi = v.astype(BF16)
    lo = (v - hi.astype(F32)).astype(BF16)
    return _dot(hi, a64) + _dot(lo, a64)


def _gelu(x):
    u = 0.7978845608028654 * (x + 0.044715 * x * x * x)
    t = jnp.tanh(u)
    return 0.5 * x * (1.0 + t), t


def _gelu_grad(x, t):
    du = 0.7978845608028654 * (1.0 + 3.0 * 0.044715 * x * x)
    return 0.5 * (1.0 + t) + 0.5 * x * (1.0 - t * t) * du


def _log1p_pos(y):
    return jnp.where(y < 1e-2, y * (1.0 - y * (0.5 - y * (1.0 / 3.0 - y * 0.25))), jnp.log(1.0 + y))


def _softplus(a):
    return jnp.maximum(a, 0.0) + _log1p_pos(jnp.exp(-jnp.abs(a)))


def _neg_expm1(z):
    series = -z * (1.0 + z * (0.5 + z * (1.0 / 6.0 + z * (1.0 / 24.0 + z * (1.0 / 120.0)))))
    return jnp.where(z > -0.02, series, 1.0 - jnp.exp(z))


def _scan_fwd(a, b, row):
    n = a.shape[0]
    d = 1
    while d < n:
        m = row >= d
        b = jnp.where(m, a * pltpu.roll(b, d, 0) + b, b)
        a = jnp.where(m, a * pltpu.roll(a, d, 0), a)
        d *= 2
    return a, b


def _scan_rev(a, b, row):
    n = a.shape[0]
    d = 1
    while d < n:
        m = row < n - d
        b = jnp.where(m, b + a * pltpu.roll(b, n - d, 0), b)
        a = jnp.where(m, a * pltpu.roll(a, n - d, 0), a)
        d *= 2
    return a, b


def _colsum(v):
    return jnp.sum(v, axis=0, keepdims=True)


def _load_gathered(chip, gathered, own, slot, sems):
    copies = []
    for j in range(N_CHIP):
        @pl.when(chip == j)
        def _(j=j):
            pltpu.make_async_copy(own, slot(j), sems.at[j]).start()

        @pl.when(chip != j)
        def _(j=j):
            pltpu.make_async_copy(gathered.at[j], slot(j), sems.at[j]).start()

        copies.append(pltpu.make_async_copy(own, slot(j), sems.at[j]))
    return copies


def _lru_gates(xlb, gab, gbias, sp, first_row):
    g = _dot(xlb, gab) + gbias
    r = jax.nn.sigmoid(g[:, :D_LRU])
    ig = jax.nn.sigmoid(g[:, D_LRU:])
    la = (-C_GATE) * r * sp
    a = jnp.exp(la)
    msq = jnp.sqrt(_neg_expm1(2.0 * la))
    mult = jnp.where(first_row, 1.0, msq)
    return r, ig, a, msq, mult


def _mix_fwd(chip, x, mod, vecd, vecl, win, wout, gab, a64):
    s = x.shape[0]
    ts = TOKEN_TILE
    nt = s // ts

    def body(chip_ref, x_ref, mod_ref, vd_ref, vl_ref, win_hbm, win_own, wout_hbm, wout_own, gab_ref, a64_ref,
             hb_ref, proj_ref, hl_ref, ycat_ref, mixed_ref, x1_ref,
             win_ref, wout_ref, ext_lx, ext_cv, hcar, sems):
        i = pl.program_id(0)

        @pl.when(i == 0)
        def _():
            cps = _load_gathered(chip_ref[0], win_hbm, win_own, lambda j: win_ref.at[j], sems.at[pl.ds(0, N_CHIP)])
            cps += _load_gathered(chip_ref[0], wout_hbm, wout_own,
                                  lambda j: wout_ref.at[pl.ds(j * WOUT_BLK, WOUT_BLK), :],
                                  sems.at[pl.ds(N_CHIP, N_CHIP)])
            ext_lx[0:HALO, :] = jnp.zeros((HALO, D_LRU), F32)
            ext_cv[0:HALO, :] = jnp.zeros((HALO, D_LRU), F32)
            hcar[...] = jnp.zeros_like(hcar)
            for cp in cps:
                cp.wait()

        row = lax.broadcasted_iota(jnp.int32, (ts, D_LRU), 0)
        first_row = jnp.logical_and(row == 0, i == 0)
        xt = x_ref[...]
        shift1, scale1, gate1 = mod_ref[0:1, :], mod_ref[1:2, :], mod_ref[2:3, :]
        r1 = lax.rsqrt(jnp.mean(xt * xt, axis=-1, keepdims=True) + EPS)
        h = (xt * r1) * vd_ref[0:1, :] * (1.0 + scale1) + shift1
        hb = h.astype(BF16)
        hb_ref[...] = hb
        for j in range(N_CHIP):
            proj_ref[:, j * WIN_BLK:(j + 1) * WIN_BLK] = _dot(hb, win_ref[j])
        u_ly = proj_ref[:, 512:1024]
        u_b = proj_ref[:, 1024:1536]

        ext_lx[HALO:HALO + ts, :] = proj_ref[:, 0:512]
        xl = vl_ref[4:5, :] + vl_ref[0:1, :] * ext_lx[pl.ds(5, ts), :]
        for k in range(1, 4):
            xl = xl + vl_ref[k:k + 1, :] * ext_lx[pl.ds(5 + k, ts), :]
        ext_lx[0:HALO, :] = ext_lx[ts:ts + HALO, :]
        sp = _softplus(vl_ref[8:9, :])
        _, ig, a, _, mult = _lru_gates(xl.astype(BF16), gab_ref[...], vd_ref[3:4, :], sp, first_row)
        acum, hloc = _scan_fwd(a, mult * (ig * xl), row)
        hl = hloc + acum * hcar[0:1, :]
        hl_ref[...] = hl
        hcar[0:1, :] = hl_ref[ts - 1:ts, :]
        ge, _ = _gelu(u_ly)
        p = ge * hl
        y_lru = p * lax.rsqrt(_gmean(p * p, a64_ref[...]) + EPS) * vl_ref[9:10, :]
        ycat_ref[:, 0:512] = y_lru.astype(BF16)

        ext_cv[HALO:HALO + ts, :] = proj_ref[:, 1536:2048] * proj_ref[:, 2048:2560]
        q = vl_ref[5:6, :] * ext_cv[pl.ds(6, ts), :]
        for k in range(1, 3):
            q = q + vl_ref[5 + k:6 + k, :] * ext_cv[pl.ds(6 + k, ts), :]
        ext_cv[0:HALO, :] = ext_cv[ts:ts + HALO, :]
        yc = u_b * q
        y_conv = yc * lax.rsqrt(_gmean(yc * yc, a64_ref[...]) + EPS) * vl_ref[10:11, :]
        ycat_ref[:, 512:1024] = y_conv.astype(BF16)

        mixed = _dot(ycat_ref[...], wout_ref[...])
        mixed_ref[...] = mixed
        x1_ref[...] = xt + gate1 * mixed

    tile = lambda w: pl.BlockSpec((ts, w), lambda i: (i, 0))
    return pl.pallas_call(
        body, name="mix_fwd", grid=(nt,),
        in_specs=[SMEM, tile(D_MODEL), _full((8, D_MODEL)), _full((8, D_MODEL)), _full((16, D_LRU)),
                  ANY, ANY, ANY, ANY, _full((D_LRU, 2 * D_LRU), True), _full((D_LRU, D_LRU), True)],
        out_specs=[tile(D_MODEL), tile(D_IN), tile(D_LRU), tile(D_MODEL), tile(D_MODEL), tile(D_MODEL)],
        out_shape=[jax.ShapeDtypeStruct((s, D_MODEL), BF16), jax.ShapeDtypeStruct((s, D_IN), F32),
                   jax.ShapeDtypeStruct((s, D_LRU), F32), jax.ShapeDtypeStruct((s, D_MODEL), BF16),
                   jax.ShapeDtypeStruct((s, D_MODEL), F32), jax.ShapeDtypeStruct((s, D_MODEL), F32)],
        scratch_shapes=[pltpu.VMEM((N_CHIP, D_MODEL, WIN_BLK), BF16), pltpu.VMEM((D_MODEL, D_MODEL), BF16),
                        pltpu.VMEM((ts + HALO, D_LRU), F32), pltpu.VMEM((ts + HALO, D_LRU), F32),
                        pltpu.VMEM((HALO, D_LRU), F32), pltpu.SemaphoreType.DMA((2 * N_CHIP,))],
        compiler_params=pltpu.CompilerParams(dimension_semantics=("arbitrary",), vmem_limit_bytes=VMEM_LIMIT),
    )(chip, x, mod, vecd, vecl, *win, *wout, gab, a64)


def _mlp_fwd_bwd(chip, x1, target, mod, vecd, w1, w2):
    s = x1.shape[0]
    ts = TOKEN_TILE
    nt = s // ts

    def body(chip_ref, x1_ref, tg_ref, mod_ref, vd_ref, w1_hbm, w1_own, w2_hbm, w2_own,
             dx1_ref, act_ref, dz_ref, dmo_ref, h2_ref, acc_ref, w1_v, w2_v, rz_v, sems):
        i = pl.program_id(0)

        @pl.when(i == 0)
        def _():
            cps = _load_gathered(chip_ref[0], w1_hbm, w1_own, lambda j: w1_v.at[j], sems.at[pl.ds(0, N_CHIP)])
            cps += _load_gathered(chip_ref[0], w2_hbm, w2_own, lambda j: w2_v.at[j], sems.at[pl.ds(N_CHIP, N_CHIP)])
            acc_ref[...] = jnp.zeros_like(acc_ref)
            for cp in cps:
                cp.wait()

        xt = x1_ref[...]
        shift2, scale2, gate2 = mod_ref[3:4, :], mod_ref[4:5, :], mod_ref[5:6, :]
        g2, gf = vd_ref[1:2, :], vd_ref[2:3, :]
        r2 = lax.rsqrt(jnp.mean(xt * xt, axis=-1, keepdims=True) + EPS)
        n2 = xt * r2
        h2b = (n2 * g2 * (1.0 + scale2) + shift2).astype(BF16)
        h2_ref[...] = h2b
        for j in range(N_CHIP):
            rz_v[j] = jnp.maximum(_dot(h2b, w1_v[j]), 0.0)
        mo = jnp.zeros((ts, D_MODEL), F32)
        for j in range(N_CHIP):
            rz = rz_v[j]
            actb = (rz * rz).astype(BF16)
            act_ref[:, j * FF_BLK:(j + 1) * FF_BLK] = actb
            mo = mo + _dot(actb, w2_v[j])
        x2 = xt + gate2 * mo
        r3 = lax.rsqrt(jnp.mean(x2 * x2, axis=-1, keepdims=True) + EPS)
        n3 = x2 * r3
        e = n3 * gf - tg_ref[...]
        loss = (0.5 / D_MODEL) * jnp.sum(_colsum(e * e), axis=1, keepdims=True)
        dy = e * (1.0 / D_MODEL)
        acc_ref[4:5, :] += _colsum(dy * n3)
        acc_ref[5:6, :] += jnp.broadcast_to(loss, (1, D_MODEL))
        dn3 = dy * gf
        dx2 = r3 * (dn3 - n3 * jnp.mean(dn3 * n3, axis=-1, keepdims=True))
        acc_ref[2:3, :] += _colsum(dx2 * mo)
        dmob = (dx2 * gate2).astype(BF16)
        dmo_ref[...] = dmob
        for j in range(N_CHIP):
            dz_ref[:, j * FF_BLK:(j + 1) * FF_BLK] = (_dot_nt(dmob, w2_v[j]) * (2.0 * rz_v[j])).astype(BF16)
        dh2 = jnp.zeros((ts, D_MODEL), F32)
        for j in range(N_CHIP):
            dh2 = dh2 + _dot_nt(dz_ref[:, j * FF_BLK:(j + 1) * FF_BLK], w1_v[j])
        acc_ref[1:2, :] += _colsum(dh2 * (n2 * g2))
        acc_ref[0:1, :] += _colsum(dh2)
        dhn2 = dh2 * (1.0 + scale2)
        acc_ref[3:4, :] += _colsum(dhn2 * n2)
        dn2 = dhn2 * g2
        dx1_ref[...] = dx2 + r2 * (dn2 - n2 * jnp.mean(dn2 * n2, axis=-1, keepdims=True))

    tile = lambda w: pl.BlockSpec((ts, w), lambda i: (i, 0))
    return pl.pallas_call(
        body, name="mlp_fwd_bwd", grid=(nt,),
        in_specs=[SMEM, tile(D_MODEL), tile(D_MODEL), _full((8, D_MODEL)), _full((8, D_MODEL)), ANY, ANY, ANY, ANY],
        out_specs=[tile(D_MODEL), tile(D_FF), tile(D_FF), tile(D_MODEL), tile(D_MODEL), _full((8, D_MODEL))],
        out_shape=[jax.ShapeDtypeStruct((s, D_MODEL), F32), jax.ShapeDtypeStruct((s, D_FF), BF16),
                   jax.ShapeDtypeStruct((s, D_FF), BF16), jax.ShapeDtypeStruct((s, D_MODEL), BF16),
                   jax.ShapeDtypeStruct((s, D_MODEL), BF16), jax.ShapeDtypeStruct((8, D_MODEL), F32)],
        scratch_shapes=[pltpu.VMEM((N_CHIP, D_MODEL, FF_BLK), BF16), pltpu.VMEM((N_CHIP, FF_BLK, D_MODEL), BF16),
                        pltpu.VMEM((N_CHIP, ts, FF_BLK), F32), pltpu.SemaphoreType.DMA((2 * N_CHIP,))],
        compiler_params=pltpu.CompilerParams(dimension_semantics=("arbitrary",), vmem_limit_bytes=VMEM_LIMIT),
    )(chip, x1, target, mod, vecd, *w1, *w2)


def _mix_bwd(chip, dx1, x, mixed, proj, hl, hb, ycat, mod, vecd, vecl, win, wout, gab, a64):
    s = x.shape[0]
    ts = TOKEN_TILE
    nt = s // ts
    hpt = ts // HALO

    def body(chip_ref, dx1_ref, x_ref, mixed_ref, proj_ref, projh_ref, hl_ref, hlh_ref, hb_ref, ycat_ref,
             mod_ref, vd_ref, vl_ref, win_hbm, win_own, wout_hbm, wout_own, gab_ref, a64_ref,
             gx_ref, accd_ref, accl_ref, gwin_hbm, gwout_hbm, ggate_hbm,
             win_ref, wout_ref, dproj_ref, dgb_ref, gwin_acc, gwout_acc, ggate_acc,
             ext_lx, ext_cv, ext_hl, ext_dxl, ext_dq, gbuf, gcar, acar, sems):
        i = pl.program_id(0)
        ri = nt - 1 - i

        @pl.when(i == 0)
        def _():
            gwin_acc[...] = jnp.zeros_like(gwin_acc)
            gwout_acc[...] = jnp.zeros_like(gwout_acc)
            ggate_acc[...] = jnp.zeros_like(ggate_acc)
            cps = _load_gathered(chip_ref[0], win_hbm, win_own, lambda j: win_ref.at[j], sems.at[pl.ds(0, N_CHIP)])
            cps += _load_gathered(chip_ref[0], wout_hbm, wout_own,
                                  lambda j: wout_ref.at[pl.ds(j * WOUT_BLK, WOUT_BLK), :],
                                  sems.at[pl.ds(N_CHIP, N_CHIP)])
            for cp in cps:
                cp.wait()
            accd_ref[...] = jnp.zeros_like(accd_ref)
            accl_ref[...] = jnp.zeros_like(accl_ref)
            ext_dxl[ts:ts + HALO, :] = jnp.zeros((HALO, D_LRU), F32)
            ext_dq[ts:ts + HALO, :] = jnp.zeros((HALO, D_LRU), F32)
            gcar[...] = jnp.zeros_like(gcar)
            acar[...] = jnp.zeros_like(acar)

        row = lax.broadcasted_iota(jnp.int32, (ts, D_LRU), 0)
        first_row = jnp.logical_and(row == 0, ri == 0)
        halo_on = jnp.where(ri == 0, 0.0, 1.0)
        shift1, scale1, gate1 = mod_ref[0:1, :], mod_ref[1:2, :], mod_ref[2:3, :]
        g1 = vd_ref[0:1, :]
        a64m = a64_ref[...]
        lg, cg = vl_ref[9:10, :], vl_ref[10:11, :]

        dx1 = dx1_ref[...]
        accd_ref[2:3, :] += _colsum(dx1 * mixed_ref[...])
        dmb = (dx1 * gate1).astype(BF16)
        gwout_acc[...] += _dot_tn(ycat_ref[...], dmb)
        dycat = _dot_nt(dmb, wout_ref[...])
        dyl = dycat[:, 0:512]
        dyv = dycat[:, 512:1024]

        u_ly = proj_ref[:, 512:1024]
        u_b = proj_ref[:, 1024:1536]
        u_c = proj_ref[:, 1536:2048]
        u_v = proj_ref[:, 2048:2560]
        ext_lx[0:HALO, :] = projh_ref[:, 0:512] * halo_on
        ext_lx[HALO:HALO + ts, :] = proj_ref[:, 0:512]
        xl = vl_ref[4:5, :] + vl_ref[0:1, :] * ext_lx[pl.ds(5, ts), :]
        for k in range(1, 4):
            xl = xl + vl_ref[k:k + 1, :] * ext_lx[pl.ds(5 + k, ts), :]
        xlb = xl.astype(BF16)
        sp = _softplus(vl_ref[8:9, :])
        r, ig, a, msq, mult = _lru_gates(xlb, gab_ref[...], vd_ref[3:4, :], sp, first_row)
        hl = hl_ref[...]
        ge, th = _gelu(u_ly)
        p = ge * hl
        rl = lax.rsqrt(_gmean(p * p, a64m) + EPS)
        nl = p * rl
        ext_cv[0:HALO, :] = projh_ref[:, 1536:2048] * projh_ref[:, 2048:2560] * halo_on
        ext_cv[HALO:HALO + ts, :] = u_c * u_v
        q = vl_ref[5:6, :] * ext_cv[pl.ds(6, ts), :]
        for k in range(1, 3):
            q = q + vl_ref[5 + k:6 + k, :] * ext_cv[pl.ds(6 + k, ts), :]
        yc = u_b * q
        rc = lax.rsqrt(_gmean(yc * yc, a64m) + EPS)
        nc = yc * rc

        accl_ref[9:10, :] += _colsum(dyl * nl)
        dnl = dyl * lg
        dp = rl * (dnl - nl * _gmean(dnl * nl, a64m))
        dproj_ref[:, 512:1024] = ((dp * hl) * _gelu_grad(u_ly, th)).astype(BF16)
        a_next = jnp.where(row == ts - 1, acar[0:1, :], pltpu.roll(a, ts - 1, 0))
        acum, gloc = _scan_rev(a_next, dp * ge, row)
        gbuf[...] = gloc + acum * gcar[0:1, :]
        gcar[0:1, :] = gbuf[0:1, :]
        ext_hl[0:HALO, :] = hlh_ref[...] * halo_on
        ext_hl[HALO:HALO + ts, :] = hl
        acar[...] = a[0:HALO, :]
        gt = gbuf[...]
        da = gt * ext_hl[pl.ds(HALO - 1, ts), :]
        dmult = gt * ig * xl
        di = gt * mult * xl
        dxl = gt * mult * ig
        dla = da * a - jnp.where(first_row, 0.0, dmult * a * a / msq)
        accl_ref[8:9, :] += _colsum(dla * ((-C_GATE) * r))
        dra = dla * ((-C_GATE) * sp) * r * (1.0 - r)
        dia = di * ig * (1.0 - ig)
        accd_ref[4:5, 0:D_LRU] += _colsum(dra)
        accd_ref[4:5, D_LRU:2 * D_LRU] += _colsum(dia)
        dgb_ref[:, 0:D_LRU] = dra.astype(BF16)
        dgb_ref[:, D_LRU:2 * D_LRU] = dia.astype(BF16)
        dxl = dxl + _dot_nt(dgb_ref[...], gab_ref[...])
        ggate_acc[...] += _dot_tn(xlb, dgb_ref[...])
        accl_ref[4:5, :] += _colsum(dxl)
        for k in range(4):
            accl_ref[k:k + 1, :] += _colsum(dxl * ext_lx[pl.ds(5 + k, ts), :])
        ext_dxl[0:ts, :] = dxl
        du_lx = vl_ref[0:1, :] * ext_dxl[pl.ds(3, ts), :]
        for k in range(1, 4):
            du_lx = du_lx + vl_ref[k:k + 1, :] * ext_dxl[pl.ds(3 - k, ts), :]
        ext_dxl[ts:ts + HALO, :] = ext_dxl[0:HALO, :]
        dproj_ref[:, 0:512] = du_lx.astype(BF16)

        accl_ref[10:11, :] += _colsum(dyv * nc)
        dnc = dyv * cg
        dyc = rc * (dnc - nc * _gmean(dnc * nc, a64m))
        dproj_ref[:, 1024:1536] = (dyc * q).astype(BF16)
        dq = dyc * u_b
        for k in range(3):
            accl_ref[5 + k:6 + k, :] += _colsum(dq * ext_cv[pl.ds(6 + k, ts), :])
        ext_dq[0:ts, :] = dq
        dcv = vl_ref[5:6, :] * ext_dq[pl.ds(2, ts), :]
        for k in range(1, 3):
            dcv = dcv + vl_ref[5 + k:6 + k, :] * ext_dq[pl.ds(2 - k, ts), :]
        ext_dq[ts:ts + HALO, :] = ext_dq[0:HALO, :]
        dproj_ref[:, 1536:2048] = (dcv * u_v).astype(BF16)
        dproj_ref[:, 2048:2560] = (dcv * u_c).astype(BF16)

        dh = _dot_nt(dproj_ref[:, 0:WIN_BLK], win_ref[0])
        for j in range(1, N_CHIP):
            dh = dh + _dot_nt(dproj_ref[:, j * WIN_BLK:(j + 1) * WIN_BLK], win_ref[j])
        for j in range(N_CHIP):
            gwin_acc[j] += _dot_tn(hb_ref[...], dproj_ref[:, j * WIN_BLK:(j + 1) * WIN_BLK])
        xt = x_ref[...]
        r1 = lax.rsqrt(jnp.mean(xt * xt, axis=-1, keepdims=True) + EPS)
        n1 = xt * r1
        accd_ref[1:2, :] += _colsum(dh * (n1 * g1))
        accd_ref[0:1, :] += _colsum(dh)
        dhn1 = dh * (1.0 + scale1)
        accd_ref[3:4, :] += _colsum(dhn1 * n1)
        dn1 = dhn1 * g1
        gx_ref[...] = dx1 + r1 * (dn1 - n1 * jnp.mean(dn1 * n1, axis=-1, keepdims=True))

        @pl.when(i == nt - 1)
        def _():
            outs = [pltpu.make_async_copy(acc, dst, sems.at[k]) for k, (acc, dst) in enumerate(
                ((gwin_acc, gwin_hbm), (gwout_acc, gwout_hbm), (ggate_acc, ggate_hbm)))]
            for cp in outs:
                cp.start()
            for cp in outs:
                cp.wait()

    tile = lambda w: pl.BlockSpec((ts, w), lambda i: (nt - 1 - i, 0))
    halo = lambda w: pl.BlockSpec((HALO, w), lambda i: (jnp.maximum((nt - 1 - i) * hpt - 1, 0), 0))
    ext = pltpu.VMEM((ts + HALO, D_LRU), F32)
    return pl.pallas_call(
        body, name="mix_bwd", grid=(nt,),
        in_specs=[SMEM, tile(D_MODEL), tile(D_MODEL), tile(D_MODEL), tile(D_IN), halo(D_IN), tile(D_LRU), halo(D_LRU),
                  tile(D_MODEL), tile(D_MODEL), _full((8, D_MODEL)), _full((8, D_MODEL)), _full((16, D_LRU)),
                  ANY, ANY, ANY, ANY, _full((D_LRU, 2 * D_LRU), True), _full((D_LRU, D_LRU), True)],
        out_specs=[tile(D_MODEL), _full((8, D_MODEL)), _full((16, D_LRU)), ANY, ANY, ANY],
        out_shape=[jax.ShapeDtypeStruct((s, D_MODEL), F32),
                   jax.ShapeDtypeStruct((8, D_MODEL), F32), jax.ShapeDtypeStruct((16, D_LRU), F32),
                   jax.ShapeDtypeStruct((N_CHIP, D_MODEL, WIN_BLK), F32), jax.ShapeDtypeStruct((D_MODEL, D_MODEL), F32),
                   jax.ShapeDtypeStruct((D_LRU, 2 * D_LRU), F32)],
        scratch_shapes=[pltpu.VMEM((N_CHIP, D_MODEL, WIN_BLK), BF16), pltpu.VMEM((D_MODEL, D_MODEL), BF16),
                        pltpu.VMEM((ts, D_IN), BF16), pltpu.VMEM((ts, 2 * D_LRU), BF16),
                        pltpu.VMEM((N_CHIP, D_MODEL, WIN_BLK), F32), pltpu.VMEM((D_MODEL, D_MODEL), F32),
                        pltpu.VMEM((D_LRU, 2 * D_LRU), F32),
                        ext, ext, ext, ext, ext, pltpu.VMEM((ts, D_LRU), F32),
                        pltpu.VMEM((HALO, D_LRU), F32), pltpu.VMEM((HALO, D_LRU), F32),
                        pltpu.SemaphoreType.DMA((2 * N_CHIP,))],
        compiler_params=pltpu.CompilerParams(dimension_semantics=("arbitrary",), vmem_limit_bytes=VMEM_LIMIT),
    )(chip, dx1, x, mixed, proj, proj, hl, hl, hb, ycat, mod, vecd, vecl, *win, *wout, gab, a64)


def _wgrad_split(name, a, b, a_blk, b_blk):
    s = a.shape[0]
    aw = a_blk or a.shape[1]
    bw = b_blk or b.shape[1]
    half = aw // 2

    def body(a_ref, b_ref, own_hbm, recv_hbm, buf, keep_sems, send_sems, recv_sems):
        j = pl.program_id(0)
        x, y, c, _ = _position()

        def copies(jj):
            slot = jj % 2
            keep = pltpu.make_async_copy(buf.at[slot, pl.ds(c * half, half), :], own_hbm.at[jj], keep_sems.at[slot])
            give = pltpu.make_async_remote_copy(
                src_ref=buf.at[slot, pl.ds((1 - c) * half, half), :], dst_ref=recv_hbm.at[jj],
                send_sem=send_sems.at[slot], recv_sem=recv_sems.at[jj],
                device_id=(x, y, 1 - c), device_id_type=MESH)
            return keep, give

        @pl.when(j >= 2)
        def _():
            keep, give = copies(j - 2)
            keep.wait()
            give.wait_send()

        buf[j % 2] = _dot_tn(a_ref[...], b_ref[...]).astype(BF16)
        keep, give = copies(j)
        keep.start()
        give.start()

        @pl.when(j == N_CHIP - 1)
        def _():
            for jj in (N_CHIP - 2, N_CHIP - 1):
                keep, give = copies(jj)
                keep.wait()
                give.wait_send()
            for jj in range(N_CHIP):
                copies(jj)[1].wait_recv()

    sds = jax.ShapeDtypeStruct((N_CHIP, half, bw), BF16)
    return pl.pallas_call(
        body, name=name, grid=(N_CHIP,),
        in_specs=[pl.BlockSpec((s, aw), (lambda j: (0, j)) if a_blk else (lambda j: (0, 0))),
                  pl.BlockSpec((s, bw), (lambda j: (0, j)) if b_blk else (lambda j: (0, 0)))],
        out_specs=[ANY, ANY], out_shape=[sds, sds],
        scratch_shapes=[pltpu.VMEM((2, aw, bw), BF16), pltpu.SemaphoreType.DMA((2,)), pltpu.SemaphoreType.DMA((2,)),
                        pltpu.SemaphoreType.DMA((N_CHIP,))],
        compiler_params=pltpu.CompilerParams(dimension_semantics=("arbitrary",), vmem_limit_bytes=VMEM_LIMIT),
    )(a, b)


def _mod_matmul(c_all, ada_w_loc):
    n = ada_w_loc.shape[1]
    cb = 512

    def body(c_ref, w_ref, o_ref):
        c = c_ref[...]
        sc = c * jax.nn.sigmoid(c)
        o_ref[...] = _dot(sc.astype(BF16), w_ref[...].astype(BF16))

    return pl.pallas_call(
        body, name="mod_matmul", grid=(n // cb,),
        in_specs=[_full((8, D_MODEL)), pl.BlockSpec((D_MODEL, cb), lambda j: (0, j))],
        out_specs=pl.BlockSpec((8, cb), lambda j: (0, j)),
        out_shape=jax.ShapeDtypeStruct((8, n), F32),
        compiler_params=pltpu.CompilerParams(dimension_semantics=("arbitrary",), vmem_limit_bytes=VMEM_LIMIT),
    )(c_all, ada_w_loc)


def _adam_math(w, g, m, v):
    m = ADAM_B1 * m + (1.0 - ADAM_B1) * g
    v = ADAM_B2 * v + (1.0 - ADAM_B2) * (g * g)
    m_hat = m / (1.0 - ADAM_B1 ** ADAM_STEP)
    v_hat = v / (1.0 - ADAM_B2 ** ADAM_STEP)
    delta = (-ADAM_LR) * (m_hat / (jnp.sqrt(v_hat) + ADAM_EPS) + ADAM_WD * w)
    return delta, m, v


def _adam(name, core, shards):
    n = len(shards)
    r, c = shards[0][0].shape
    half = r // 2
    rb = min(half, 128)
    nh = half // rb

    def body(core_ref, *refs):
        ins, outs = refs[:5 * n], refs[5 * n:]
        mine = (pl.program_id(0) // nh) == core_ref[0]
        for k in range(n):
            w_ref, go_ref, gs_ref, m_ref, v_ref = ins[5 * k:5 * k + 5]
            g_ref, d_ref, mo_ref, vo_ref = outs[4 * k:4 * k + 4]
            g = jnp.where(mine, go_ref[...], gs_ref[...])
            g_ref[...] = g
            d_ref[...], mo_ref[...], vo_ref[...] = _adam_math(w_ref[...], g, m_ref[...], v_ref[...])

    spec = pl.BlockSpec((rb, c), lambda i, core_ref: (i, 0))
    hspec = pl.BlockSpec((rb, c), lambda i, core_ref: (i % nh, 0))
    sds = jax.ShapeDtypeStruct((r, c), F32)
    res = pl.pallas_call(
        body, name=name,
        grid_spec=pltpu.PrefetchScalarGridSpec(
            num_scalar_prefetch=1, grid=(r // rb,),
            in_specs=[spec, hspec, hspec, spec, spec] * n, out_specs=[spec] * (4 * n)),
        out_shape=[sds] * (4 * n),
        compiler_params=pltpu.CompilerParams(dimension_semantics=("arbitrary",), vmem_limit_bytes=VMEM_LIMIT),
    )(core, *[t for s in shards for t in s])
    return [res[4 * k:4 * k + 4] for k in range(n)]


def _ada_grad_adam(sct, dmod_loc, w, m, v):
    r, c = w.shape
    rb = 128

    def body(s_ref, dm_ref, w_ref, m_ref, v_ref, g_ref, d_ref, mo_ref, vo_ref):
        g = s_ref[:, 0:1] * dm_ref[0:1, :]
        for b in range(1, 8):
            g = g + s_ref[:, b:b + 1] * dm_ref[b:b + 1, :]
        g_ref[...] = g
        d_ref[...], mo_ref[...], vo_ref[...] = _adam_math(w_ref[...], g, m_ref[...], v_ref[...])

    spec = pl.BlockSpec((rb, c), lambda i: (i, 0))
    sds = jax.ShapeDtypeStruct((r, c), F32)
    return pl.pallas_call(
        body, name="ada_grad_adam", grid=(r // rb,),
        in_specs=[pl.BlockSpec((rb, 8), lambda i: (i, 0)), _full((8, c)), spec, spec, spec],
        out_specs=[spec] * 4, out_shape=[sds] * 4,
        compiler_params=pltpu.CompilerParams(dimension_semantics=("arbitrary",), vmem_limit_bytes=VMEM_LIMIT),
    )(sct, dmod_loc, w, m, v)


def _position():
    x, y, c = lax.axis_index("x"), lax.axis_index("y"), lax.axis_index("c")
    chips = [(1 - x, y), (x, 1 - y), (1 - x, 1 - y)]
    return x, y, c, chips


def _ag8_run(ins, outs, send_sems, recv_sems, local_sems):
    na = len(ins)
    x, y, c, chips = _position()
    me, sibling = (x, y, c), (x, y, 1 - c)
    first, passed, local = [], [], []
    for a in range(na):
        m_per = ins[a].shape[0]

        def rows(px, py, pc, a=a, m_per=m_per):
            return outs[a].at[pl.ds((4 * px + 2 * py + pc) * m_per, m_per), :]

        def copy(k, block, to, src=None, a=a, rows=rows):
            return pltpu.make_async_remote_copy(
                src_ref=rows(*block) if src is None else src, dst_ref=rows(*block),
                send_sem=send_sems.at[7 * a + k], recv_sem=recv_sems.at[7 * a + k],
                device_id=to, device_id_type=MESH)

        mine = pltpu.make_async_copy(ins[a], rows(*me), local_sems.at[a])
        mine.start()
        local.append(mine)
        f = [copy(0, me, sibling, src=ins[a])]
        f += [copy(1 + j, me, (*chip, c), src=ins[a]) for j, chip in enumerate(chips)]
        for cp in f:
            cp.start()
        first.append((f, copy))
    for a in range(na):
        f, copy = first[a]
        p = [copy(4 + j, (*chip, c), sibling) for j, chip in enumerate(chips)]
        for j, chip in enumerate(chips):
            copy(1 + j, (*chip, c), me).wait_recv()
            p[j].start()
        passed.append(p)
    for a in range(na):
        f, copy = first[a]
        copy(0, sibling, me).wait_recv()
        for j, chip in enumerate(chips):
            copy(4 + j, (*chip, 1 - c), me).wait_recv()
        for cp in f + passed[a]:
            cp.wait_send()
        local[a].wait()


def _allgather8(name, arrs):
    na = len(arrs)

    def body(*refs):
        _ag8_run(refs[:na], refs[na:2 * na], *refs[2 * na:])

    return pl.pallas_call(
        body, name=name,
        out_shape=[jax.ShapeDtypeStruct((8 * a.shape[0], a.shape[1]), a.dtype) for a in arrs],
        in_specs=[VMEM] * na, out_specs=[VMEM] * na,
        scratch_shapes=[pltpu.SemaphoreType.DMA((7 * na,)), pltpu.SemaphoreType.DMA((7 * na,)),
                        pltpu.SemaphoreType.DMA((na,))],
        compiler_params=pltpu.CompilerParams(vmem_limit_bytes=VMEM_LIMIT),
    )(*arrs)


AG_SEMS = 7


def _ag_copies(ins, outs, send_sems, recv_sems):
    x, y, c, chips = _position()
    sibling = (x, y, 1 - c)
    xn, yn, dg = [2 * chip[0] + chip[1] for chip in chips]
    to_x, to_y = (1 - x, y, c), (x, 1 - y, c)
    res = []
    for a in range(len(ins)):
        half = ins[a].shape[0] // 2
        quarter = half // 2

        def copy(k, dst, to, src=None, a=a):
            return pltpu.make_async_remote_copy(
                src_ref=dst if src is None else src, dst_ref=dst,
                send_sem=send_sems.at[AG_SEMS * a + k], recv_sem=recv_sems.at[AG_SEMS * a + k],
                device_id=to, device_id_type=MESH)

        def rows(chip, pc, q=None, a=a, half=half, quarter=quarter):
            if q is None:
                return outs[a].at[chip, pl.ds(pc * half, half), :]
            return outs[a].at[chip, pl.ds(pc * half + q * quarter, quarter), :]

        own = ins[a].at[pl.ds(c * half, half), :]
        mine = rows(2 * x + y, c)
        res.append(dict(
            sends=[copy(0, mine, to_x, src=own), copy(1, mine, to_y, src=own)],
            from_x=copy(0, rows(xn, c), to_x), from_y=copy(1, rows(yn, c), to_y),
            relay_y=copy(2, rows(xn, c, 0), to_y), relay_x=copy(3, rows(yn, c, 1), to_x),
            from_y_relay=copy(2, rows(dg, c, 0), to_y), from_x_relay=copy(3, rows(dg, c, 1), to_x),
            pass_on=[copy(4, rows(xn, c), sibling), copy(5, rows(yn, c), sibling), copy(6, rows(dg, c), sibling)],
            from_sibling=[copy(4, rows(xn, 1 - c), sibling), copy(5, rows(yn, 1 - c), sibling),
                          copy(6, rows(dg, 1 - c), sibling)]))
    return res


def _ag_start(ins, outs, send_sems, recv_sems):
    for cps in _ag_copies(ins, outs, send_sems, recv_sems):
        for cp in cps["sends"]:
            cp.start()


def _ag_relay(ins, outs, send_sems, recv_sems, which):
    copies = _ag_copies(ins, outs, send_sems, recv_sems)
    for a in which:
        cps = copies[a]
        cps["from_x"].wait_recv()
        cps["relay_y"].start()
        cps["pass_on"][0].start()
        cps["from_y"].wait_recv()
        cps["relay_x"].start()
        cps["pass_on"][1].start()


def _ag_complete(ins, outs, send_sems, recv_sems):
    copies = _ag_copies(ins, outs, send_sems, recv_sems)
    for cps in copies:
        cps["from_y_relay"].wait_recv()
        cps["from_x_relay"].wait_recv()
        cps["pass_on"][2].start()
    for cps in copies:
        for cp in cps["from_sibling"]:
            cp.wait_recv()
        for cp in cps["sends"] + [cps["relay_y"], cps["relay_x"]] + cps["pass_on"]:
            cp.wait_send()


def _ag_finish(ins, outs, send_sems, recv_sems):
    _ag_relay(ins, outs, send_sems, recv_sems, range(len(ins)))
    _ag_complete(ins, outs, send_sems, recv_sems)


def _allgather_weights(name, collective_id, shards):
    na = len(shards)
    hbm = pltpu.MemorySpace.HBM
    ins = [jax.new_ref(s, memory_space=hbm) for s in shards]
    outs = [jax.empty_ref(jax.ShapeDtypeStruct((N_CHIP,) + s.shape, s.dtype), memory_space=hbm) for s in shards]

    @pl.kernel(mesh=plsc.ScalarSubcoreMesh(axis_name="sequencer", num_cores=1), name=name,
               scratch_types=(pltpu.SemaphoreType.DMA((AG_SEMS * na,)), pltpu.SemaphoreType.DMA((AG_SEMS * na,))),
               compiler_params=pltpu.CompilerParams(collective_id=collective_id))
    def launch(send_sems, recv_sems):
        x, y, c, _ = _position()
        peers = [(1 - x, y, c), (x, 1 - y, c), (x, y, 1 - c)]
        barrier = pltpu.get_barrier_semaphore()
        for peer in peers:
            pl.semaphore_signal(barrier, inc=1, device_id=peer, device_id_type=MESH)
        pl.semaphore_wait(barrier, len(peers))
        _ag_start(ins, outs, send_sems, recv_sems)
        _ag_finish(ins, outs, send_sems, recv_sems)

    launch()
    return [o[...] for o in outs]


def _swap_halves(name, grads, after=()):
    na, nw = len(grads), len(after)

    def body(*refs):
        ins, outs = refs[:na], refs[na + nw:2 * na + nw]
        send_sems, recv_sems = refs[2 * na + nw:]
        x, y, c, _ = _position()
        cps = []
        for a in range(na):
            half = ins[a].shape[1] // 2
            cp = pltpu.make_async_remote_copy(
                src_ref=ins[a].at[:, pl.ds((1 - c) * half, half), :], dst_ref=outs[a],
                send_sem=send_sems.at[a], recv_sem=recv_sems.at[a],
                device_id=(x, y, 1 - c), device_id_type=MESH)
            cp.start()
            cps.append(cp)
        for cp in cps:
            cp.wait()

    return pl.pallas_call(
        body, name=name,
        out_shape=[jax.ShapeDtypeStruct((g.shape[0], g.shape[1] // 2, g.shape[2]), g.dtype) for g in grads],
        in_specs=[ANY] * (na + nw), out_specs=[ANY] * na,
        scratch_shapes=[pltpu.SemaphoreType.DMA((na,)), pltpu.SemaphoreType.DMA((na,))],
    )(*grads, *after)


def _xchg_copies(ins, outs, send_sems, recv_sems):
    x, y, c, chips = _position()
    return [pltpu.make_async_remote_copy(
        src_ref=ins[a].at[2 * chip[0] + chip[1]], dst_ref=outs[a].at[j],
        send_sem=send_sems.at[3 * a + j], recv_sem=recv_sems.at[3 * a + j],
        device_id=(*chip, c), device_id_type=MESH) for a in range(len(ins)) for j, chip in enumerate(chips)]


def _exchange_chips(name, collective_id, parts):
    na = len(parts)
    hbm = pltpu.MemorySpace.HBM
    ins = [jax.new_ref(p, memory_space=hbm) for p in parts]
    outs = [jax.empty_ref(jax.ShapeDtypeStruct((3,) + p.shape[1:], p.dtype), memory_space=hbm) for p in parts]

    @pl.kernel(mesh=plsc.ScalarSubcoreMesh(axis_name="sequencer", num_cores=1), name=name,
               scratch_types=(pltpu.SemaphoreType.DMA((3 * na,)), pltpu.SemaphoreType.DMA((3 * na,))),
               compiler_params=pltpu.CompilerParams(collective_id=collective_id))
    def launch(send_sems, recv_sems):
        x, y, c, chips = _position()
        barrier = pltpu.get_barrier_semaphore()
        for chip in chips:
            pl.semaphore_signal(barrier, inc=1, device_id=(*chip, c), device_id_type=MESH)
        pl.semaphore_wait(barrier, len(chips))
        for cp in _xchg_copies(ins, outs, send_sems, recv_sems):
            cp.start()
        for cp in _xchg_copies(ins, outs, send_sems, recv_sems):
            cp.wait()

    launch()
    return [q[...] for q in outs]


def _swap_reduced(name, halves):
    na = len(halves)

    def body(*refs):
        ins, outs = refs[:na], refs[na:2 * na]
        send_sems, recv_sems = refs[2 * na:]
        x, y, c, _ = _position()
        cps = []
        for a in range(na):
            cp = pltpu.make_async_remote_copy(
                src_ref=ins[a], dst_ref=outs[a], send_sem=send_sems.at[a], recv_sem=recv_sems.at[a],
                device_id=(x, y, 1 - c), device_id_type=MESH)
            cp.start()
            cps.append(cp)
        for cp in cps:
            cp.wait()

    return pl.pallas_call(
        body, name=name,
        out_shape=[jax.ShapeDtypeStruct(h.shape, h.dtype) for h in halves],
        in_specs=[ANY] * na, out_specs=[ANY] * na,
        scratch_shapes=[pltpu.SemaphoreType.DMA((na,)), pltpu.SemaphoreType.DMA((na,))],
    )(*halves)


def _add_sibling(name, grad, recv, core):
    _, r, c = grad.shape
    half = r // 2
    rb = min(half, 256)
    nrb = half // rb

    def body(core_ref, g_ref, r_ref, o_ref):
        o_ref[...] = (g_ref[...].astype(F32) + r_ref[...].astype(F32)).astype(BF16)

    return pl.pallas_call(
        body, name=name,
        grid_spec=pltpu.PrefetchScalarGridSpec(
            num_scalar_prefetch=1, grid=(N_CHIP, nrb),
            in_specs=[pl.BlockSpec((1, rb, c), lambda j, i, core_ref: (j, core_ref[0] * nrb + i, 0)),
                      pl.BlockSpec((1, rb, c), lambda j, i, core_ref: (j, i, 0))],
            out_specs=pl.BlockSpec((1, rb, c), lambda j, i, core_ref: (j, i, 0))),
        out_shape=jax.ShapeDtypeStruct((N_CHIP, half, c), BF16),
        compiler_params=pltpu.CompilerParams(dimension_semantics=("arbitrary", "arbitrary"),
                                             vmem_limit_bytes=VMEM_LIMIT),
    )(core, grad, recv)


def _add_halves(name, pairs):
    n = len(pairs)
    _, half, c = pairs[0][0].shape
    rb = min(half, 256)

    def body(*refs):
        for k in range(n):
            refs[2 * n + k][...] = (refs[2 * k][...].astype(F32) + refs[2 * k + 1][...].astype(F32)).astype(BF16)

    spec = pl.BlockSpec((1, rb, c), lambda j, i: (j, i, 0))
    return pl.pallas_call(
        body, name=name, grid=(N_CHIP, half // rb), in_specs=[spec] * (2 * n), out_specs=[spec] * n,
        out_shape=[jax.ShapeDtypeStruct((N_CHIP, half, c), BF16)] * n,
        compiler_params=pltpu.CompilerParams(dimension_semantics=("arbitrary", "arbitrary"),
                                             vmem_limit_bytes=VMEM_LIMIT),
    )(*[t for p in pairs for t in p])


def _add_chips(name, chip, p, q):
    _, half, c = q.shape
    rb = min(half, 256)

    def body(chip_ref, p_ref, q_ref, o_ref):
        acc = p_ref[0].astype(F32)
        for j in range(3):
            acc = acc + q_ref[j].astype(F32)
        o_ref[...] = acc

    return pl.pallas_call(
        body, name=name,
        grid_spec=pltpu.PrefetchScalarGridSpec(
            num_scalar_prefetch=1, grid=(half // rb,),
            in_specs=[pl.BlockSpec((1, rb, c), lambda i, chip_ref: (chip_ref[0], i, 0)),
                      pl.BlockSpec((3, rb, c), lambda i, chip_ref: (0, i, 0))],
            out_specs=pl.BlockSpec((rb, c), lambda i, chip_ref: (i, 0))),
        out_shape=jax.ShapeDtypeStruct((half, c), F32),
        compiler_params=pltpu.CompilerParams(dimension_semantics=("arbitrary",), vmem_limit_bytes=VMEM_LIMIT),
    )(chip, p, q)


def _small_update(gad, gam, gl, gg, mychip, params):
    names = ["ada_b", "norm1_g", "lru_conv_b", "gate_a_w", "gate_a_b", "gate_x_w", "gate_x_b", "a_param",
             "lru_conv_w", "short_conv_w", "lru_out_g", "conv_out_g", "norm2_g", "final_g"]
    flat = [t for n in names for t in params[n]]
    nin = len(flat)

    def body(chip_ref, gad_ref, gam_ref, gl_ref, gg_ref, *refs):
        ins = {n: refs[3 * k:3 * k + 3] for k, n in enumerate(names)}
        outs = {n: refs[nin + 4 * k:nin + 4 * k + 4] for k, n in enumerate(names)}
        loss_ref, dmod_ref = refs[nin + 4 * len(names):nin + 4 * len(names) + 2]

        def dsum(ref, lo, n):
            per = ref.shape[0] // 8
            acc = ref[lo:lo + n, :].astype(F32)
            for dev in range(1, 8):
                acc = acc + ref[dev * per + lo:dev * per + lo + n, :].astype(F32)
            return acc

        def update(n, g):
            w_ref, m_ref, v_ref = ins[n]
            g_ref, d_ref, mo_ref, vo_ref = outs[n]
            g_ref[...] = g
            d_ref[...], mo_ref[...], vo_ref[...] = _adam_math(w_ref[...], g, m_ref[...], v_ref[...])

        d, dm, l, lw = refs[-4:]
        d[...] = dsum(gad_ref, 0, 8)
        dm[...] = dsum(gam_ref, 0, 8)
        l[...] = dsum(gl_ref, 0, 16)
        for dev in range(8):
            for k in range(3):
                dmod_ref[dev:dev + 1, k * D_MODEL:(k + 1) * D_MODEL] = gad_ref[dev * 8 + k:dev * 8 + k + 1, :]
                dmod_ref[dev:dev + 1, (3 + k) * D_MODEL:(4 + k) * D_MODEL] = gam_ref[dev * 8 + k:dev * 8 + k + 1, :]
        w_ref, m_ref, v_ref = ins["ada_b"]
        g_ref, d_ref, mo_ref, vo_ref = outs["ada_b"]
        for k in range(3):
            g_ref[:, k * D_MODEL:(k + 1) * D_MODEL] = d[k:k + 1, :]
            g_ref[:, (3 + k) * D_MODEL:(4 + k) * D_MODEL] = dm[k:k + 1, :]
        d_ref[...], mo_ref[...], vo_ref[...] = _adam_math(w_ref[...], g_ref[...], m_ref[...], v_ref[...])
        update("norm1_g", d[3:4, :])
        update("norm2_g", dm[3:4, :])
        update("final_g", dm[4:5, :])
        update("gate_a_b", d[4:5, 0:D_LRU])
        update("gate_x_b", d[4:5, D_LRU:2 * D_LRU])
        update("lru_conv_b", l[4:5, :])
        update("a_param", l[8:9, :] * jax.nn.sigmoid(ins["a_param"][0][...]))
        update("lru_out_g", l[9:10, :])
        update("conv_out_g", l[10:11, :])
        loss_ref[...] = jnp.broadcast_to(dm[5:6, 0:128], (8, 128))
        chip = chip_ref[0]
        acc = jnp.zeros((8, 128), F32)
        for j in range(N_CHIP):
            acc = acc + jnp.where(chip == j, l[0:8, j * 128:(j + 1) * 128], 0.0)
        lw[...] = acc
        update("lru_conv_w", lw[0:4, :])
        update("short_conv_w", lw[5:8, :])
        gates = dsum(gg_ref, 0, D_LRU)
        update("gate_a_w", gates[:, 0:HEAD])
        update("gate_x_w", gates[:, HEAD:2 * HEAD])

    out_shape = []
    for n in names:
        out_shape += [jax.ShapeDtypeStruct(params[n][0].shape, F32)] * 4
    out_shape += [jax.ShapeDtypeStruct((8, 128), F32), jax.ShapeDtypeStruct((8, 6 * D_MODEL), F32)]
    res = pl.pallas_call(
        body, name="small_update", out_shape=out_shape,
        in_specs=[SMEM] + [VMEM] * (4 + nin),
        out_specs=[VMEM] * len(out_shape),
        scratch_shapes=[pltpu.VMEM((8, D_MODEL), F32), pltpu.VMEM((8, D_MODEL), F32), pltpu.VMEM((16, D_LRU), F32),
                        pltpu.VMEM((8, 128), F32)],
        compiler_params=pltpu.CompilerParams(vmem_limit_bytes=VMEM_LIMIT),
    )(mychip, gad, gam, gl, gg, *flat)
    per = {n: res[4 * k:4 * k + 4] for k, n in enumerate(names)}
    return per, res[-2], res[-1]


def _block_diag(w):
    eye = jnp.eye(8, dtype=w.dtype)
    return (eye[:, None, :, None] * w[:, :, None, :]).reshape(8 * HEAD, 8 * HEAD)


def _diag_blocks(g):
    return jnp.concatenate([g[h * HEAD:(h + 1) * HEAD, h * HEAD:(h + 1) * HEAD] for h in range(8)], axis=0)


def kernel(x, c, ada_w, ada_b, norm1_g, w_in, lru_conv_w, lru_conv_b, gate_a_w, gate_a_b, gate_x_w, gate_x_b, a_param, short_conv_w, lru_out_g, conv_out_g, w_out, norm2_g, w_mlp1, w_mlp2, final_g, loss_target, m_ada_w, m_ada_b, m_norm1_g, m_w_in, m_lru_conv_w, m_lru_conv_b, m_gate_a_w, m_gate_a_b, m_gate_x_w, m_gate_x_b, m_a_param, m_short_conv_w, m_lru_out_g, m_conv_out_g, m_w_out, m_norm2_g, m_w_mlp1, m_w_mlp2, m_final_g, v_ada_w, v_ada_b, v_norm1_g, v_w_in, v_lru_conv_w, v_lru_conv_b, v_gate_a_w, v_gate_a_b, v_gate_x_w, v_gate_x_b, v_a_param, v_short_conv_w, v_lru_out_g, v_conv_out_g, v_w_out, v_norm2_g, v_w_mlp1, v_w_mlp2, v_final_g):
    xi, yi, ci = lax.axis_index("x"), lax.axis_index("y"), lax.axis_index("c")
    mychip = 2 * xi + yi
    me = 4 * xi + 2 * yi + ci

    own_in, own_out = w_in[0].astype(BF16), w_out[0].astype(BF16)
    win_all, wout_all = _allgather_weights("allgather_mixer_weights", 1, [own_in, own_out])
    own_w1, own_w2 = w_mlp1[0].astype(BF16), w_mlp2[0].astype(BF16)
    w1_all, w2_all = _allgather_weights("allgather_mlp_weights", 2, [own_w1, own_w2])

    c_blk = jnp.zeros((8, D_MODEL), F32).at[0:1].set(c)
    cw_blk = jnp.zeros((8, 128), F32).at[0:4].set(lru_conv_w[0]).at[4:7].set(short_conv_w[0])
    c_g, cw_g = _allgather8("allgather_cond", [c_blk, cw_blk])
    c_all = c_g.reshape(8, 8, D_MODEL)[:, 0]
    cw_g = cw_g.reshape(4, 2, 8, 128)[:, 0]
    lcw = cw_g[:, 0:4].transpose(1, 0, 2).reshape(4, D_LRU)
    scw = cw_g[:, 4:7].transpose(1, 0, 2).reshape(3, D_LRU)

    mod_loc = _mod_matmul(c_all, ada_w[0])
    (mod_g,) = _allgather8("allgather_mod", [mod_loc])
    mod_all = mod_g.reshape(4, 2, 8, 6 * D_MODEL // 4)[:, 0].transpose(1, 0, 2).reshape(8, 6 * D_MODEL) + ada_b
    mod_pad = jnp.pad(mod_all.reshape(8, 6, D_MODEL), ((0, 0), (0, 2), (0, 0)))
    mod = lax.dynamic_slice_in_dim(mod_pad, me, 1, axis=0).reshape(8, D_MODEL)

    win, wout = (win_all, own_in), (wout_all, own_out)
    chip = mychip.reshape(1).astype(jnp.int32)
    core = ci.reshape(1).astype(jnp.int32)

    vecd = jnp.concatenate([norm1_g, norm2_g, final_g[None, :], jnp.concatenate([gate_a_b, gate_x_b], axis=1),
                            jnp.zeros((4, D_MODEL), F32)], axis=0)
    vecl = jnp.concatenate([lcw, lru_conv_b, scw, a_param, lru_out_g, conv_out_g, jnp.zeros((5, D_LRU), F32)], axis=0)
    gab = jnp.concatenate([_block_diag(gate_a_w[0]), _block_diag(gate_x_w[0])], axis=1).astype(BF16)
    a64 = _block_diag(jnp.full((8, HEAD, HEAD), 1.0 / HEAD, F32)).astype(BF16)

    hb, proj, hl, ycat, mixed, x1 = _mix_fwd(chip, x[0], mod, vecd, vecl, win, wout, gab, a64)
    dx1, act, dz, dmo, h2b, accm = _mlp_fwd_bwd(
        chip, x1, loss_target[0], mod, vecd, (w1_all, own_w1), (w2_all, own_w2))

    def sibling_sum(tag, grads, after=()):
        recv = _swap_halves("rs_swap_halves_" + tag, grads, after)
        return [_add_sibling("rs_add_sibling_%s%d" % (tag, k), g, r, core) for k, (g, r) in enumerate(zip(grads, recv))]

    parts_mlp = list(_add_halves("rs_add_sibling_mlp", [_wgrad_split("wgrad_mlp1", h2b, dz, 0, FF_BLK),
                                                        _wgrad_split("wgrad_mlp2", act, dmo, FF_BLK, 0)]))
    q_w1, q_w2 = _exchange_chips("rs_exchange_mlp", 0, parts_mlp)
    grad_x, accd, accl, g_win, g_wout, g_gate = _mix_bwd(
        chip, dx1, x[0], mixed, proj, hl, hb, ycat, mod, vecd, vecl, win, wout, gab, a64)

    gg_blk = jnp.concatenate([_diag_blocks(g_gate[:, 0:D_LRU]), _diag_blocks(g_gate[:, D_LRU:2 * D_LRU])], axis=1)
    gad, gam, gl, gg = _allgather8("allgather_small_grads", [accd, accm, accl, gg_blk.astype(BF16)])
    parts_mix = sibling_sum("mix", [g_win, g_wout.reshape(N_CHIP, WOUT_BLK, D_MODEL)], after=[gad])

    def reduced(tag, parts, landed):
        own = [_add_chips("rs_add_chips_%s%d" % (tag, k), chip, p, q) for k, (p, q) in enumerate(zip(parts, landed))]
        return own, _swap_reduced("rs_swap_reduced_" + tag, own)

    landed_mix = _exchange_chips("rs_exchange_mix", 3, parts_mix)
    own_mlp, sib_mlp = reduced("mlp", parts_mlp, [q_w1, q_w2])
    res_w1, res_w2 = _adam("adam_mlp", core, [(w_mlp1[0], own_mlp[0], sib_mlp[0], m_w_mlp1[0], v_w_mlp1[0]),
                                              (w_mlp2[0], own_mlp[1], sib_mlp[1], m_w_mlp2[0], v_w_mlp2[0])])
    own_mix, sib_mix = reduced("mix", parts_mix, landed_mix)
    (res_win,) = _adam("adam_w_in", core, [(w_in[0], own_mix[0], sib_mix[0], m_w_in[0], v_w_in[0])])
    (res_wout,) = _adam("adam_w_out", core, [(w_out[0], own_mix[1], sib_mix[1], m_w_out[0], v_w_out[0])])

    params = {
        "ada_b": (ada_b, m_ada_b, v_ada_b), "norm1_g": (norm1_g, m_norm1_g, v_norm1_g),
        "lru_conv_b": (lru_conv_b, m_lru_conv_b, v_lru_conv_b),
        "gate_a_w": tuple(t.reshape(D_LRU, HEAD) for t in (gate_a_w, m_gate_a_w, v_gate_a_w)),
        "gate_a_b": (gate_a_b, m_gate_a_b, v_gate_a_b),
        "gate_x_w": tuple(t.reshape(D_LRU, HEAD) for t in (gate_x_w, m_gate_x_w, v_gate_x_w)),
        "gate_x_b": (gate_x_b, m_gate_x_b, v_gate_x_b), "a_param": (a_param, m_a_param, v_a_param),
        "lru_conv_w": tuple(t[0] for t in (lru_conv_w, m_lru_conv_w, v_lru_conv_w)),
        "short_conv_w": tuple(t[0] for t in (short_conv_w, m_short_conv_w, v_short_conv_w)),
        "lru_out_g": (lru_out_g, m_lru_out_g, v_lru_out_g), "conv_out_g": (conv_out_g, m_conv_out_g, v_conv_out_g),
        "norm2_g": (norm2_g, m_norm2_g, v_norm2_g),
        "final_g": tuple(t[None, :] for t in (final_g, m_final_g, v_final_g)),
    }
    small, loss_blk, dmod_cols = _small_update(gad, gam, gl, gg, chip, params)
    loss = loss_blk[0, 0]

    ncol = 6 * D_MODEL // N_CHIP
    dmod_loc = lax.dynamic_slice_in_dim(dmod_cols, mychip * ncol, ncol, axis=1)
    sct = (c_all * jax.nn.sigmoid(c_all)).T
    ada = _ada_grad_adam(sct, dmod_loc, ada_w[0], m_ada_w[0], v_ada_w[0])

    res = {"ada_w": ada, "w_in": res_win, "w_out": res_wout, "w_mlp1": res_w1, "w_mlp2": res_w2}
    res = {n: tuple(t[None] for t in r) for n, r in res.items()}
    shapes = {"gate_a_w": gate_a_w.shape, "gate_x_w": gate_x_w.shape, "lru_conv_w": lru_conv_w.shape,
              "short_conv_w": short_conv_w.shape, "final_g": final_g.shape}
    for n, t in small.items():
        res[n] = tuple(u.reshape(shapes[n]) if n in shapes else u for u in t)

    order = ["ada_w", "ada_b", "norm1_g", "w_in", "lru_conv_w", "lru_conv_b", "gate_a_w", "gate_a_b", "gate_x_w",
             "gate_x_b", "a_param", "short_conv_w", "lru_out_g", "conv_out_g", "w_out", "norm2_g", "w_mlp1",
             "w_mlp2", "final_g"]
    return (loss, grad_x[None], *[res[n][0] for n in order], *[res[n][1] for n in order],
            *[res[n][2] for n in order], *[res[n][3] for n in order])
```

```python
import jax
import jax.numpy as jnp
from jax import lax
from jax.experimental import pallas as pl
from jax.experimental.pallas import tpu as pltpu
from jax.experimental.pallas import tpu_sc as plsc

F32 = jnp.float32
BF16 = jnp.bfloat16

D_MODEL = 1024
D_LRU = 512
D_IN = 2560
D_FF = 4096
N_CHIP = 4
WIN_BLK = D_IN // N_CHIP
WOUT_BLK = D_MODEL // N_CHIP
FF_BLK = D_FF // N_CHIP
HEAD = 64
EPS = 1e-6
C_GATE = 8.0
TOKEN_TILE = 256
HALO = 8
VMEM_LIMIT = 60 * 1024 * 1024

ADAM_LR = 0.001
ADAM_B1 = 0.9
ADAM_B2 = 0.999
ADAM_EPS = 1e-08
ADAM_WD = 0.01
ADAM_STEP = 10

MESH = pl.DeviceIdType.MESH
ANY = pl.BlockSpec(memory_space=pl.ANY)
VMEM = pl.BlockSpec(memory_space=pltpu.VMEM)
SMEM = pl.BlockSpec(memory_space=pltpu.SMEM)


def _full(shape, single=False):
    nd = len(shape)
    if single:
        return pl.BlockSpec(shape, lambda *_: (0,) * nd, pipeline_mode=pl.Buffered(1))
    return pl.BlockSpec(shape, lambda *_: (0,) * nd)


def _dot(a, b):
    return jnp.dot(a, b, preferred_element_type=F32)


def _dot_nt(a, b):
    return lax.dot_general(a, b, (((1,), (1,)), ((), ())), preferred_element_type=F32)


def _dot_tn(a, b):
    return lax.dot_general(a, b, (((0,), (0,)), ((), ())), preferred_element_type=F32)


def _gmean(v, a64):
    hi = v.astype(BF16)
    lo = (v - hi.astype(F32)).astype(BF16)
    return _dot(hi, a64) + _dot(lo, a64)


def _gelu(x):
    u = 0.7978845608028654 * (x + 0.044715 * x * x * x)
    t = jnp.tanh(u)
    return 0.5 * x * (1.0 + t), t


def _gelu_grad(x, t):
    du = 0.7978845608028654 * (1.0 + 3.0 * 0.044715 * x * x)
    return 0.5 * (1.0 + t) + 0.5 * x * (1.0 - t * t) * du


def _log1p_pos(y):
    return jnp.where(y < 1e-2, y * (1.0 - y * (0.5 - y * (1.0 / 3.0 - y * 0.25))), jnp.log(1.0 + y))


def _softplus(a):
    return jnp.maximum(a, 0.0) + _log1p_pos(jnp.exp(-jnp.abs(a)))


def _neg_expm1(z):
    series = -z * (1.0 + z * (0.5 + z * (1.0 / 6.0 + z * (1.0 / 24.0 + z * (1.0 / 120.0)))))
    return jnp.where(z > -0.02, series, 1.0 - jnp.exp(z))


def _scan_fwd(a, b, row):
    n = a.shape[0]
    d = 1
    while d < n:
        m = row >= d
        b = jnp.where(m, a * pltpu.roll(b, d, 0) + b, b)
        a = jnp.where(m, a * pltpu.roll(a, d, 0), a)
        d *= 2
    return a, b


def _scan_rev(a, b, row):
    n = a.shape[0]
    d = 1
    while d < n:
        m = row < n - d
        b = jnp.where(m, b + a * pltpu.roll(b, n - d, 0), b)
        a = jnp.where(m, a * pltpu.roll(a, n - d, 0), a)
        d *= 2
    return a, b


def _colsum(v):
    return jnp.sum(v, axis=0, keepdims=True)


def _load_gathered(chip, gathered, own, slot, sems):
    copies = []
    for j in range(N_CHIP):
        @pl.when(chip == j)
        def _(j=j):
            pltpu.make_async_copy(own, slot(j), sems.at[j]).start()

        @pl.when(chip != j)
        def _(j=j):
            pltpu.make_async_copy(gathered.at[j], slot(j), sems.at[j]).start()

        copies.append(pltpu.make_async_copy(own, slot(j), sems.at[j]))
    return copies


def _lru_gates(xlb, gab, gbias, sp, first_row):
    g = _dot(xlb, gab) + gbias
    r = jax.nn.sigmoid(g[:, :D_LRU])
    ig = jax.nn.sigmoid(g[:, D_LRU:])
    la = (-C_GATE) * r * sp
    a = jnp.exp(la)
    msq = jnp.sqrt(_neg_expm1(2.0 * la))
    mult = jnp.where(first_row, 1.0, msq)
    return r, ig, a, msq, mult


def _mix_fwd(chip, x, mod, vecd, vecl, win, wout, gab, a64):
    s = x.shape[0]
    ts = TOKEN_TILE
    nt = s // ts

    def body(chip_ref, x_ref, mod_ref, vd_ref, vl_ref, win_hbm, win_own, wout_hbm, wout_own, gab_ref, a64_ref,
             hb_ref, proj_ref, hl_ref, ycat_ref, mixed_ref, x1_ref,
             win_ref, wout_ref, ext_lx, ext_cv, hcar, sems):
        i = pl.program_id(0)

        @pl.when(i == 0)
        def _():
            cps = _load_gathered(chip_ref[0], win_hbm, win_own, lambda j: win_ref.at[j], sems.at[pl.ds(0, N_CHIP)])
            cps += _load_gathered(chip_ref[0], wout_hbm, wout_own,
                                  lambda j: wout_ref.at[pl.ds(j * WOUT_BLK, WOUT_BLK), :],
                                  sems.at[pl.ds(N_CHIP, N_CHIP)])
            ext_lx[0:HALO, :] = jnp.zeros((HALO, D_LRU), F32)
            ext_cv[0:HALO, :] = jnp.zeros((HALO, D_LRU), F32)
            hcar[...] = jnp.zeros_like(hcar)
            for cp in cps:
                cp.wait()

        row = lax.broadcasted_iota(jnp.int32, (ts, D_LRU), 0)
        first_row = jnp.logical_and(row == 0, i == 0)
        xt = x_ref[...]
        shift1, scale1, gate1 = mod_ref[0:1, :], mod_ref[1:2, :], mod_ref[2:3, :]
        r1 = lax.rsqrt(jnp.mean(xt * xt, axis=-1, keepdims=True) + EPS)
        h = (xt * r1) * vd_ref[0:1, :] * (1.0 + scale1) + shift1
        hb = h.astype(BF16)
        hb_ref[...] = hb
        for j in range(N_CHIP):
            proj_ref[:, j * WIN_BLK:(j + 1) * WIN_BLK] = _dot(hb, win_ref[j])
        u_ly = proj_ref[:, 512:1024]
        u_b = proj_ref[:, 1024:1536]

        ext_lx[HALO:HALO + ts, :] = proj_ref[:, 0:512]
        xl = vl_ref[4:5, :] + vl_ref[0:1, :] * ext_lx[pl.ds(5, ts), :]
        for k in range(1, 4):
            xl = xl + vl_ref[k:k + 1, :] * ext_lx[pl.ds(5 + k, ts), :]
        ext_lx[0:HALO, :] = ext_lx[ts:ts + HALO, :]
        sp = _softplus(vl_ref[8:9, :])
        _, ig, a, _, mult = _lru_gates(xl.astype(BF16), gab_ref[...], vd_ref[3:4, :], sp, first_row)
        acum, hloc = _scan_fwd(a, mult * (ig * xl), row)
        hl = hloc + acum * hcar[0:1, :]
        hl_ref[...] = hl
        hcar[0:1, :] = hl_ref[ts - 1:ts, :]
        ge, _ = _gelu(u_ly)
        p = ge * hl
        y_lru = p * lax.rsqrt(_gmean(p * p, a64_ref[...]) + EPS) * vl_ref[9:10, :]
        ycat_ref[:, 0:512] = y_lru.astype(BF16)

        ext_cv[HALO:HALO + ts, :] = proj_ref[:, 1536:2048] * proj_ref[:, 2048:2560]
        q = vl_ref[5:6, :] * ext_cv[pl.ds(6, ts), :]
        for k in range(1, 3):
            q = q + vl_ref[5 + k:6 + k, :] * ext_cv[pl.ds(6 + k, ts), :]
        ext_cv[0:HALO, :] = ext_cv[ts:ts + HALO, :]
        yc = u_b * q
        y_conv = yc * lax.rsqrt(_gmean(yc * yc, a64_ref[...]) + EPS) * vl_ref[10:11, :]
        ycat_ref[:, 512:1024] = y_conv.astype(BF16)

        mixed = _dot(ycat_ref[...], wout_ref[...])
        mixed_ref[...] = mixed
        x1_ref[...] = xt + gate1 * mixed

    tile = lambda w: pl.BlockSpec((ts, w), lambda i: (i, 0))
    return pl.pallas_call(
        body, name="mix_fwd", grid=(nt,),
        in_specs=[SMEM, tile(D_MODEL), _full((8, D_MODEL)), _full((8, D_MODEL)), _full((16, D_LRU)),
                  ANY, ANY, ANY, ANY, _full((D_LRU, 2 * D_LRU), True), _full((D_LRU, D_LRU), True)],
        out_specs=[tile(D_MODEL), tile(D_IN), tile(D_LRU), tile(D_MODEL), tile(D_MODEL), tile(D_MODEL)],
        out_shape=[jax.ShapeDtypeStruct((s, D_MODEL), BF16), jax.ShapeDtypeStruct((s, D_IN), F32),
                   jax.ShapeDtypeStruct((s, D_LRU), F32), jax.ShapeDtypeStruct((s, D_MODEL), BF16),
                   jax.ShapeDtypeStruct((s, D_MODEL), F32), jax.ShapeDtypeStruct((s, D_MODEL), F32)],
        scratch_shapes=[pltpu.VMEM((N_CHIP, D_MODEL, WIN_BLK), BF16), pltpu.VMEM((D_MODEL, D_MODEL), BF16),
                        pltpu.VMEM((ts + HALO, D_LRU), F32), pltpu.VMEM((ts + HALO, D_LRU), F32),
                        pltpu.VMEM((HALO, D_LRU), F32), pltpu.SemaphoreType.DMA((2 * N_CHIP,))],
        compiler_params=pltpu.CompilerParams(dimension_semantics=("arbitrary",), vmem_limit_bytes=VMEM_LIMIT),
    )(chip, x, mod, vecd, vecl, *win, *wout, gab, a64)


def _mlp_fwd_bwd(chip, x1, target, mod, vecd, w1, w2):
    s = x1.shape[0]
    ts = TOKEN_TILE
    nt = s // ts

    def body(chip_ref, x1_ref, tg_ref, mod_ref, vd_ref, w1_hbm, w1_own, w2_hbm, w2_own,
             dx1_ref, act_ref, dz_ref, dmo_ref, h2_ref, acc_ref, w1_v, w2_v, rz_v, sems):
        i = pl.program_id(0)

        @pl.when(i == 0)
        def _():
            cps = _load_gathered(chip_ref[0], w1_hbm, w1_own, lambda j: w1_v.at[j], sems.at[pl.ds(0, N_CHIP)])
            cps += _load_gathered(chip_ref[0], w2_hbm, w2_own, lambda j: w2_v.at[j], sems.at[pl.ds(N_CHIP, N_CHIP)])
            acc_ref[...] = jnp.zeros_like(acc_ref)
            for cp in cps:
                cp.wait()

        xt = x1_ref[...]
        shift2, scale2, gate2 = mod_ref[3:4, :], mod_ref[4:5, :], mod_ref[5:6, :]
        g2, gf = vd_ref[1:2, :], vd_ref[2:3, :]
        r2 = lax.rsqrt(jnp.mean(xt * xt, axis=-1, keepdims=True) + EPS)
        n2 = xt * r2
        h2b = (n2 * g2 * (1.0 + scale2) + shift2).astype(BF16)
        h2_ref[...] = h2b
        for j in range(N_CHIP):
            rz_v[j] = jnp.maximum(_dot(h2b, w1_v[j]), 0.0)
        mo = jnp.zeros((ts, D_MODEL), F32)
        for j in range(N_CHIP):
            rz = rz_v[j]
            actb = (rz * rz).astype(BF16)
            act_ref[:, j * FF_BLK:(j + 1) * FF_BLK] = actb
            mo = mo + _dot(actb, w2_v[j])
        x2 = xt + gate2 * mo
        r3 = lax.rsqrt(jnp.mean(x2 * x2, axis=-1, keepdims=True) + EPS)
        n3 = x2 * r3
        e = n3 * gf - tg_ref[...]
        loss = (0.5 / D_MODEL) * jnp.sum(_colsum(e * e), axis=1, keepdims=True)
        dy = e * (1.0 / D_MODEL)
        acc_ref[4:5, :] += _colsum(dy * n3)
        acc_ref[5:6, :] += jnp.broadcast_to(loss, (1, D_MODEL))
        dn3 = dy * gf
        dx2 = r3 * (dn3 - n3 * jnp.mean(dn3 * n3, axis=-1, keepdims=True))
        acc_ref[2:3, :] += _colsum(dx2 * mo)
        dmob = (dx2 * gate2).astype(BF16)
        dmo_ref[...] = dmob
        for j in range(N_CHIP):
            dz_ref[:, j * FF_BLK:(j + 1) * FF_BLK] = (_dot_nt(dmob, w2_v[j]) * (2.0 * rz_v[j])).astype(BF16)
        dh2 = jnp.zeros((ts, D_MODEL), F32)
        for j in range(N_CHIP):
            dh2 = dh2 + _dot_nt(dz_ref[:, j * FF_BLK:(j + 1) * FF_BLK], w1_v[j])
        acc_ref[1:2, :] += _colsum(dh2 * (n2 * g2))
        acc_ref[0:1, :] += _colsum(dh2)
        dhn2 = dh2 * (1.0 + scale2)
        acc_ref[3:4, :] += _colsum(dhn2 * n2)
        dn2 = dhn2 * g2
        dx1_ref[...] = dx2 + r2 * (dn2 - n2 * jnp.mean(dn2 * n2, axis=-1, keepdims=True))

    tile = lambda w: pl.BlockSpec((ts, w), lambda i: (i, 0))
    return pl.pallas_call(
        body, name="mlp_fwd_bwd", grid=(nt,),
        in_specs=[SMEM, tile(D_MODEL), tile(D_MODEL), _full((8, D_MODEL)), _full((8, D_MODEL)), ANY, ANY, ANY, ANY],
        out_specs=[tile(D_MODEL), tile(D_FF), tile(D_FF), tile(D_MODEL), tile(D_MODEL), _full((8, D_MODEL))],
        out_shape=[jax.ShapeDtypeStruct((s, D_MODEL), F32), jax.ShapeDtypeStruct((s, D_FF), BF16),
                   jax.ShapeDtypeStruct((s, D_FF), BF16), jax.ShapeDtypeStruct((s, D_MODEL), BF16),
                   jax.ShapeDtypeStruct((s, D_MODEL), BF16), jax.ShapeDtypeStruct((8, D_MODEL), F32)],
        scratch_shapes=[pltpu.VMEM((N_CHIP, D_MODEL, FF_BLK), BF16), pltpu.VMEM((N_CHIP, FF_BLK, D_MODEL), BF16),
                        pltpu.VMEM((N_CHIP, ts, FF_BLK), F32), pltpu.SemaphoreType.DMA((2 * N_CHIP,))],
        compiler_params=pltpu.CompilerParams(dimension_semantics=("arbitrary",), vmem_limit_bytes=VMEM_LIMIT),
    )(chip, x1, target, mod, vecd, *w1, *w2)


def _mix_bwd(chip, dx1, x, mixed, proj, hl, hb, ycat, mod, vecd, vecl, win, wout, gab, a64):
    s = x.shape[0]
    ts = TOKEN_TILE
    nt = s // ts
    hpt = ts // HALO

    def body(chip_ref, dx1_ref, x_ref, mixed_ref, proj_ref, projh_ref, hl_ref, hlh_ref, hb_ref, ycat_ref,
             mod_ref, vd_ref, vl_ref, win_hbm, win_own, wout_hbm, wout_own, gab_ref, a64_ref,
             gx_ref, accd_ref, accl_ref, gwin_hbm, gwout_hbm, ggate_hbm,
             win_ref, wout_ref, dproj_ref, dgb_ref, gwin_acc, gwout_acc, ggate_acc,
             ext_lx, ext_cv, ext_hl, ext_dxl, ext_dq, gbuf, gcar, acar, sems):
        i = pl.program_id(0)
        ri = nt - 1 - i

        @pl.when(i == 0)
        def _():
            gwin_acc[...] = jnp.zeros_like(gwin_acc)
            gwout_acc[...] = jnp.zeros_like(gwout_acc)
            ggate_acc[...] = jnp.zeros_like(ggate_acc)
            cps = _load_gathered(chip_ref[0], win_hbm, win_own, lambda j: win_ref.at[j], sems.at[pl.ds(0, N_CHIP)])
            cps += _load_gathered(chip_ref[0], wout_hbm, wout_own,
                                  lambda j: wout_ref.at[pl.ds(j * WOUT_BLK, WOUT_BLK), :],
                                  sems.at[pl.ds(N_CHIP, N_CHIP)])
            for cp in cps:
                cp.wait()
            accd_ref[...] = jnp.zeros_like(accd_ref)
            accl_ref[...] = jnp.zeros_like(accl_ref)
            ext_dxl[ts:ts + HALO, :] = jnp.zeros((HALO, D_LRU), F32)
            ext_dq[ts:ts + HALO, :] = jnp.zeros((HALO, D_LRU), F32)
            gcar[...] = jnp.zeros_like(gcar)
            acar[...] = jnp.zeros_like(acar)

        row = lax.broadcasted_iota(jnp.int32, (ts, D_LRU), 0)
        first_row = jnp.logical_and(row == 0, ri == 0)
        halo_on = jnp.where(ri == 0, 0.0, 1.0)
        shift1, scale1, gate1 = mod_ref[0:1, :], mod_ref[1:2, :], mod_ref[2:3, :]
        g1 = vd_ref[0:1, :]
        a64m = a64_ref[...]
        lg, cg = vl_ref[9:10, :], vl_ref[10:11, :]

        dx1 = dx1_ref[...]
        accd_ref[2:3, :] += _colsum(dx1 * mixed_ref[...])
        dmb = (dx1 * gate1).astype(BF16)
        gwout_acc[...] += _dot_tn(ycat_ref[...], dmb)
        dycat = _dot_nt(dmb, wout_ref[...])
        dyl = dycat[:, 0:512]
        dyv = dycat[:, 512:1024]

        u_ly = proj_ref[:, 512:1024]
        u_b = proj_ref[:, 1024:1536]
        u_c = proj_ref[:, 1536:2048]
        u_v = proj_ref[:, 2048:2560]
        ext_lx[0:HALO, :] = projh_ref[:, 0:512] * halo_on
        ext_lx[HALO:HALO + ts, :] = proj_ref[:, 0:512]
        xl = vl_ref[4:5, :] + vl_ref[0:1, :] * ext_lx[pl.ds(5, ts), :]
        for k in range(1, 4):
            xl = xl + vl_ref[k:k + 1, :] * ext_lx[pl.ds(5 + k, ts), :]
        xlb = xl.astype(BF16)
        sp = _softplus(vl_ref[8:9, :])
        r, ig, a, msq, mult = _lru_gates(xlb, gab_ref[...], vd_ref[3:4, :], sp, first_row)
        hl = hl_ref[...]
        ge, th = _gelu(u_ly)
        p = ge * hl
        rl = lax.rsqrt(_gmean(p * p, a64m) + EPS)
        nl = p * rl
        ext_cv[0:HALO, :] = projh_ref[:, 1536:2048] * projh_ref[:, 2048:2560] * halo_on
        ext_cv[HALO:HALO + ts, :] = u_c * u_v
        q = vl_ref[5:6, :] * ext_cv[pl.ds(6, ts), :]
        for k in range(1, 3):
            q = q + vl_ref[5 + k:6 + k, :] * ext_cv[pl.ds(6 + k, ts), :]
        yc = u_b * q
        rc = lax.rsqrt(_gmean(yc * yc, a64m) + EPS)
        nc = yc * rc

        accl_ref[9:10, :] += _colsum(dyl * nl)
        dnl = dyl * lg
        dp = rl * (dnl - nl * _gmean(dnl * nl, a64m))
        dproj_ref[:, 512:1024] = ((dp * hl) * _gelu_grad(u_ly, th)).astype(BF16)
        a_next = jnp.where(row == ts - 1, acar[0:1, :], pltpu.roll(a, ts - 1, 0))
        acum, gloc = _scan_rev(a_next, dp * ge, row)
        gbuf[...] = gloc + acum * gcar[0:1, :]
        gcar[0:1, :] = gbuf[0:1, :]
        ext_hl[0:HALO, :] = hlh_ref[...] * halo_on
        ext_hl[HALO:HALO + ts, :] = hl
        acar[...] = a[0:HALO, :]
        gt = gbuf[...]
        da = gt * ext_hl[pl.ds(HALO - 1, ts), :]
        dmult = gt * ig * xl
        di = gt * mult * xl
        dxl = gt * mult * ig
        dla = da * a - jnp.where(first_row, 0.0, dmult * a * a / msq)
        accl_ref[8:9, :] += _colsum(dla * ((-C_GATE) * r))
        dra = dla * ((-C_GATE) * sp) * r * (1.0 - r)
        dia = di * ig * (1.0 - ig)
        accd_ref[4:5, 0:D_LRU] += _colsum(dra)
        accd_ref[4:5, D_LRU:2 * D_LRU] += _colsum(dia)
        dgb_ref[:, 0:D_LRU] = dra.astype(BF16)
        dgb_ref[:, D_LRU:2 * D_LRU] = dia.astype(BF16)
        dxl = dxl + _dot_nt(dgb_ref[...], gab_ref[...])
        ggate_acc[...] += _dot_tn(xlb, dgb_ref[...])
        accl_ref[4:5, :] += _colsum(dxl)
        for k in range(4):
            accl_ref[k:k + 1, :] += _colsum(dxl * ext_lx[pl.ds(5 + k, ts), :])
        ext_dxl[0:ts, :] = dxl
        du_lx = vl_ref[0:1, :] * ext_dxl[pl.ds(3, ts), :]
        for k in range(1, 4):
            du_lx = du_lx + vl_ref[k:k + 1, :] * ext_dxl[pl.ds(3 - k, ts), :]
        ext_dxl[ts:ts + HALO, :] = ext_dxl[0:HALO, :]
        dproj_ref[:, 0:512] = du_lx.astype(BF16)

        accl_ref[10:11, :] += _colsum(dyv * nc)
        dnc = dyv * cg
        dyc = rc * (dnc - nc * _gmean(dnc * nc, a64m))
        dproj_ref[:, 1024:1536] = (dyc * q).astype(BF16)
        dq = dyc * u_b
        for k in range(3):
            accl_ref[5 + k:6 + k, :] += _colsum(dq * ext_cv[pl.ds(6 + k, ts), :])
        ext_dq[0:ts, :] = dq
        dcv = vl_ref[5:6, :] * ext_dq[pl.ds(2, ts), :]
        for k in range(1, 3):
            dcv = dcv + vl_ref[5 + k:6 + k, :] * ext_dq[pl.ds(2 - k, ts), :]
        ext_dq[ts:ts + HALO, :] = ext_dq[0:HALO, :]
        dproj_ref[:, 1536:2048] = (dcv * u_v).astype(BF16)
        dproj_ref[:, 2048:2560] = (dcv * u_c).astype(BF16)

        dh = _dot_nt(dproj_ref[:, 0:WIN_BLK], win_ref[0])
        for j in range(1, N_CHIP):
            dh = dh + _dot_nt(dproj_ref[:, j * WIN_BLK:(j + 1) * WIN_BLK], win_ref[j])
        for j in range(N_CHIP):
            gwin_acc[j] += _dot_tn(hb_ref[...], dproj_ref[:, j * WIN_BLK:(j + 1) * WIN_BLK])
        xt = x_ref[...]
        r1 = lax.rsqrt(jnp.mean(xt * xt, axis=-1, keepdims=True) + EPS)
        n1 = xt * r1
        accd_ref[1:2, :] += _colsum(dh * (n1 * g1))
        accd_ref[0:1, :] += _colsum(dh)
        dhn1 = dh * (1.0 + scale1)
        accd_ref[3:4, :] += _colsum(dhn1 * n1)
        dn1 = dhn1 * g1
        gx_ref[...] = dx1 + r1 * (dn1 - n1 * jnp.mean(dn1 * n1, axis=-1, keepdims=True))

        @pl.when(i == nt - 1)
        def _():
            outs = [pltpu.make_async_copy(acc, dst, sems.at[k]) for k, (acc, dst) in enumerate(
                ((gwin_acc, gwin_hbm), (gwout_acc, gwout_hbm), (ggate_acc, ggate_hbm)))]
            for cp in outs:
                cp.start()
            for cp in outs:
                cp.wait()

    tile = lambda w: pl.BlockSpec((ts, w), lambda i: (nt - 1 - i, 0))
    halo = lambda w: pl.BlockSpec((HALO, w), lambda i: (jnp.maximum((nt - 1 - i) * hpt - 1, 0), 0))
    ext = pltpu.VMEM((ts + HALO, D_LRU), F32)
    return pl.pallas_call(
        body, name="mix_bwd", grid=(nt,),
        in_specs=[SMEM, tile(D_MODEL), tile(D_MODEL), tile(D_MODEL), tile(D_IN), halo(D_IN), tile(D_LRU), halo(D_LRU),
                  tile(D_MODEL), tile(D_MODEL), _full((8, D_MODEL)), _full((8, D_MODEL)), _full((16, D_LRU)),
                  ANY, ANY, ANY, ANY, _full((D_LRU, 2 * D_LRU), True), _full((D_LRU, D_LRU), True)],
        out_specs=[tile(D_MODEL), _full((8, D_MODEL)), _full((16, D_LRU)), ANY, ANY, ANY],
        out_shape=[jax.ShapeDtypeStruct((s, D_MODEL), F32),
                   jax.ShapeDtypeStruct((8, D_MODEL), F32), jax.ShapeDtypeStruct((16, D_LRU), F32),
                   jax.ShapeDtypeStruct((N_CHIP, D_MODEL, WIN_BLK), F32), jax.ShapeDtypeStruct((D_MODEL, D_MODEL), F32),
                   jax.ShapeDtypeStruct((D_LRU, 2 * D_LRU), F32)],
        scratch_shapes=[pltpu.VMEM((N_CHIP, D_MODEL, WIN_BLK), BF16), pltpu.VMEM((D_MODEL, D_MODEL), BF16),
                        pltpu.VMEM((ts, D_IN), BF16), pltpu.VMEM((ts, 2 * D_LRU), BF16),
                        pltpu.VMEM((N_CHIP, D_MODEL, WIN_BLK), F32), pltpu.VMEM((D_MODEL, D_MODEL), F32),
                        pltpu.VMEM((D_LRU, 2 * D_LRU), F32),
                        ext, ext, ext, ext, ext, pltpu.VMEM((ts, D_LRU), F32),
                        pltpu.VMEM((HALO, D_LRU), F32), pltpu.VMEM((HALO, D_LRU), F32),
                        pltpu.SemaphoreType.DMA((2 * N_CHIP,))],
        compiler_params=pltpu.CompilerParams(dimension_semantics=("arbitrary",), vmem_limit_bytes=VMEM_LIMIT),
    )(chip, dx1, x, mixed, proj, proj, hl, hl, hb, ycat, mod, vecd, vecl, *win, *wout, gab, a64)


def _wgrad_split(name, a, b, a_blk, b_blk):
    s = a.shape[0]
    aw = a_blk or a.shape[1]
    bw = b_blk or b.shape[1]
    half = aw // 2

    def body(a_ref, b_ref, own_hbm, recv_hbm, buf, keep_sems, send_sems, recv_sems):
        j = pl.program_id(0)
        x, y, c, _ = _position()

        def copies(jj):
            slot = jj % 2
            keep = pltpu.make_async_copy(buf.at[slot, pl.ds(c * half, half), :], own_hbm.at[jj], keep_sems.at[slot])
            give = pltpu.make_async_remote_copy(
                src_ref=buf.at[slot, pl.ds((1 - c) * half, half), :], dst_ref=recv_hbm.at[jj],
                send_sem=send_sems.at[slot], recv_sem=recv_sems.at[jj],
                device_id=(x, y, 1 - c), device_id_type=MESH)
            return keep, give

        @pl.when(j >= 2)
        def _():
            keep, give = copies(j - 2)
            keep.wait()
            give.wait_send()

        buf[j % 2] = _dot_tn(a_ref[...], b_ref[...]).astype(BF16)
        keep, give = copies(j)
        keep.start()
        give.start()

        @pl.when(j == N_CHIP - 1)
        def _():
            for jj in (N_CHIP - 2, N_CHIP - 1):
                keep, give = copies(jj)
                keep.wait()
                give.wait_send()
            for jj in range(N_CHIP):
                copies(jj)[1].wait_recv()

    sds = jax.ShapeDtypeStruct((N_CHIP, half, bw), BF16)
    return pl.pallas_call(
        body, name=name, grid=(N_CHIP,),
        in_specs=[pl.BlockSpec((s, aw), (lambda j: (0, j)) if a_blk else (lambda j: (0, 0))),
                  pl.BlockSpec((s, bw), (lambda j: (0, j)) if b_blk else (lambda j: (0, 0)))],
        out_specs=[ANY, ANY], out_shape=[sds, sds],
        scratch_shapes=[pltpu.VMEM((2, aw, bw), BF16), pltpu.SemaphoreType.DMA((2,)), pltpu.SemaphoreType.DMA((2,)),
                        pltpu.SemaphoreType.DMA((N_CHIP,))],
        compiler_params=pltpu.CompilerParams(dimension_semantics=("arbitrary",), vmem_limit_bytes=VMEM_LIMIT),
    )(a, b)


def _mod_matmul(c_all, ada_w_loc):
    n = ada_w_loc.shape[1]
    cb = 512

    def body(c_ref, w_ref, o_ref):
        c = c_ref[...]
        sc = c * jax.nn.sigmoid(c)
        o_ref[...] = _dot(sc.astype(BF16), w_ref[...].astype(BF16))

    return pl.pallas_call(
        body, name="mod_matmul", grid=(n // cb,),
        in_specs=[_full((8, D_MODEL)), pl.BlockSpec((D_MODEL, cb), lambda j: (0, j))],
        out_specs=pl.BlockSpec((8, cb), lambda j: (0, j)),
        out_shape=jax.ShapeDtypeStruct((8, n), F32),
        compiler_params=pltpu.CompilerParams(dimension_semantics=("arbitrary",), vmem_limit_bytes=VMEM_LIMIT),
    )(c_all, ada_w_loc)


def _adam_math(w, g, m, v):
    m = ADAM_B1 * m + (1.0 - ADAM_B1) * g
    v = ADAM_B2 * v + (1.0 - ADAM_B2) * (g * g)
    m_hat = m / (1.0 - ADAM_B1 ** ADAM_STEP)
    v_hat = v / (1.0 - ADAM_B2 ** ADAM_STEP)
    delta = (-ADAM_LR) * (m_hat / (jnp.sqrt(v_hat) + ADAM_EPS) + ADAM_WD * w)
    return delta, m, v


def _adam(name, core, shards):
    n = len(shards)
    r, c = shards[0][0].shape
    half = r // 2
    rb = min(half, 128)
    nh = half // rb

    def body(core_ref, *refs):
        ins, outs = refs[:5 * n], refs[5 * n:]
        mine = (pl.program_id(0) // nh) == core_ref[0]
        for k in range(n):
            w_ref, go_ref, gs_ref, m_ref, v_ref = ins[5 * k:5 * k + 5]
            g_ref, d_ref, mo_ref, vo_ref = outs[4 * k:4 * k + 4]
            g = jnp.where(mine, go_ref[...], gs_ref[...])
            g_ref[...] = g
            d_ref[...], mo_ref[...], vo_ref[...] = _adam_math(w_ref[...], g, m_ref[...], v_ref[...])

    spec = pl.BlockSpec((rb, c), lambda i, core_ref: (i, 0))
    hspec = pl.BlockSpec((rb, c), lambda i, core_ref: (i % nh, 0))
    sds = jax.ShapeDtypeStruct((r, c), F32)
    res = pl.pallas_call(
        body, name=name,
        grid_spec=pltpu.PrefetchScalarGridSpec(
            num_scalar_prefetch=1, grid=(r // rb,),
            in_specs=[spec, hspec, hspec, spec, spec] * n, out_specs=[spec] * (4 * n)),
        out_shape=[sds] * (4 * n),
        compiler_params=pltpu.CompilerParams(dimension_semantics=("arbitrary",), vmem_limit_bytes=VMEM_LIMIT),
    )(core, *[t for s in shards for t in s])
    return [res[4 * k:4 * k + 4] for k in range(n)]


def _ada_grad_adam(sct, dmod_loc, w, m, v):
    r, c = w.shape
    rb = 128

    def body(s_ref, dm_ref, w_ref, m_ref, v_ref, g_ref, d_ref, mo_ref, vo_ref):
        g = s_ref[:, 0:1] * dm_ref[0:1, :]
        for b in range(1, 8):
            g = g + s_ref[:, b:b + 1] * dm_ref[b:b + 1, :]
        g_ref[...] = g
        d_ref[...], mo_ref[...], vo_ref[...] = _adam_math(w_ref[...], g, m_ref[...], v_ref[...])

    spec = pl.BlockSpec((rb, c), lambda i: (i, 0))
    sds = jax.ShapeDtypeStruct((r, c), F32)
    return pl.pallas_call(
        body, name="ada_grad_adam", grid=(r // rb,),
        in_specs=[pl.BlockSpec((rb, 8), lambda i: (i, 0)), _full((8, c)), spec, spec, spec],
        out_specs=[spec] * 4, out_shape=[sds] * 4,
        compiler_params=pltpu.CompilerParams(dimension_semantics=("arbitrary",), vmem_limit_bytes=VMEM_LIMIT),
    )(sct, dmod_loc, w, m, v)


def _position():
    x, y, c = lax.axis_index("x"), lax.axis_index("y"), lax.axis_index("c")
    chips = [(1 - x, y), (x, 1 - y), (1 - x, 1 - y)]
    return x, y, c, chips


def _ag8_run(ins, outs, send_sems, recv_sems, local_sems):
    na = len(ins)
    x, y, c, chips = _position()
    me, sibling = (x, y, c), (x, y, 1 - c)
    first, passed, local = [], [], []
    for a in range(na):
        m_per = ins[a].shape[0]

        def rows(px, py, pc, a=a, m_per=m_per):
            return outs[a].at[pl.ds((4 * px + 2 * py + pc) * m_per, m_per), :]

        def copy(k, block, to, src=None, a=a, rows=rows):
            return pltpu.make_async_remote_copy(
                src_ref=rows(*block) if src is None else src, dst_ref=rows(*block),
                send_sem=send_sems.at[7 * a + k], recv_sem=recv_sems.at[7 * a + k],
                device_id=to, device_id_type=MESH)

        mine = pltpu.make_async_copy(ins[a], rows(*me), local_sems.at[a])
        mine.start()
        local.append(mine)
        f = [copy(0, me, sibling, src=ins[a])]
        f += [copy(1 + j, me, (*chip, c), src=ins[a]) for j, chip in enumerate(chips)]
        for cp in f:
            cp.start()
        first.append((f, copy))
    for a in range(na):
        f, copy = first[a]
        p = [copy(4 + j, (*chip, c), sibling) for j, chip in enumerate(chips)]
        for j, chip in enumerate(chips):
            copy(1 + j, (*chip, c), me).wait_recv()
            p[j].start()
        passed.append(p)
    for a in range(na):
        f, copy = first[a]
        copy(0, sibling, me).wait_recv()
        for j, chip in enumerate(chips):
            copy(4 + j, (*chip, 1 - c), me).wait_recv()
        for cp in f + passed[a]:
            cp.wait_send()
        local[a].wait()


def _allgather8(name, arrs):
    na = len(arrs)

    def body(*refs):
        _ag8_run(refs[:na], refs[na:2 * na], *refs[2 * na:])

    return pl.pallas_call(
        body, name=name,
        out_shape=[jax.ShapeDtypeStruct((8 * a.shape[0], a.shape[1]), a.dtype) for a in arrs],
        in_specs=[VMEM] * na, out_specs=[VMEM] * na,
        scratch_shapes=[pltpu.SemaphoreType.DMA((7 * na,)), pltpu.SemaphoreType.DMA((7 * na,)),
                        pltpu.SemaphoreType.DMA((na,))],
        compiler_params=pltpu.CompilerParams(vmem_limit_bytes=VMEM_LIMIT),
    )(*arrs)


AG_SEMS = 7


def _ag_copies(ins, outs, send_sems, recv_sems):
    x, y, c, chips = _position()
    sibling = (x, y, 1 - c)
    xn, yn, dg = [2 * chip[0] + chip[1] for chip in chips]
    to_x, to_y = (1 - x, y, c), (x, 1 - y, c)
    res = []
    for a in range(len(ins)):
        half = ins[a].shape[0] // 2
        quarter = half // 2

        def copy(k, dst, to, src=None, a=a):
            return pltpu.make_async_remote_copy(
                src_ref=dst if src is None else src, dst_ref=dst,
                send_sem=send_sems.at[AG_SEMS * a + k], recv_sem=recv_sems.at[AG_SEMS * a + k],
                device_id=to, device_id_type=MESH)

        def rows(chip, pc, q=None, a=a, half=half, quarter=quarter):
            if q is None:
                return outs[a].at[chip, pl.ds(pc * half, half), :]
            return outs[a].at[chip, pl.ds(pc * half + q * quarter, quarter), :]

        own = ins[a].at[pl.ds(c * half, half), :]
        mine = rows(2 * x + y, c)
        res.append(dict(
            sends=[copy(0, mine, to_x, src=own), copy(1, mine, to_y, src=own)],
            from_x=copy(0, rows(xn, c), to_x), from_y=copy(1, rows(yn, c), to_y),
            relay_y=copy(2, rows(xn, c, 0), to_y), relay_x=copy(3, rows(yn, c, 1), to_x),
            from_y_relay=copy(2, rows(dg, c, 0), to_y), from_x_relay=copy(3, rows(dg, c, 1), to_x),
            pass_on=[copy(4, rows(xn, c), sibling), copy(5, rows(yn, c), sibling), copy(6, rows(dg, c), sibling)],
            from_sibling=[copy(4, rows(xn, 1 - c), sibling), copy(5, rows(yn, 1 - c), sibling),
                          copy(6, rows(dg, 1 - c), sibling)]))
    return res


def _ag_start(ins, outs, send_sems, recv_sems):
    for cps in _ag_copies(ins, outs, send_sems, recv_sems):
        for cp in cps["sends"]:
            cp.start()


def _ag_relay(ins, outs, send_sems, recv_sems, which):
    copies = _ag_copies(ins, outs, send_sems, recv_sems)
    for a in which:
        cps = copies[a]
        cps["from_x"].wait_recv()
        cps["relay_y"].start()
        cps["pass_on"][0].start()
        cps["from_y"].wait_recv()
        cps["relay_x"].start()
        cps["pass_on"][1].start()


def _ag_complete(ins, outs, send_sems, recv_sems):
    copies = _ag_copies(ins, outs, send_sems, recv_sems)
    for cps in copies:
        cps["from_y_relay"].wait_recv()
        cps["from_x_relay"].wait_recv()
        cps["pass_on"][2].start()
    for cps in copies:
        for cp in cps["from_sibling"]:
            cp.wait_recv()
        for cp in cps["sends"] + [cps["relay_y"], cps["relay_x"]] + cps["pass_on"]:
            cp.wait_send()


def _ag_finish(ins, outs, send_sems, recv_sems):
    _ag_relay(ins, outs, send_sems, recv_sems, range(len(ins)))
    _ag_complete(ins, outs, send_sems, recv_sems)


def _allgather_weights(name, collective_id, shards):
    na = len(shards)
    hbm = pltpu.MemorySpace.HBM
    ins = [jax.new_ref(s, memory_space=hbm) for s in shards]
    outs = [jax.empty_ref(jax.ShapeDtypeStruct((N_CHIP,) + s.shape, s.dtype), memory_space=hbm) for s in shards]

    @pl.kernel(mesh=plsc.ScalarSubcoreMesh(axis_name="sequencer", num_cores=1), name=name,
               scratch_types=(pltpu.SemaphoreType.DMA((AG_SEMS * na,)), pltpu.SemaphoreType.DMA((AG_SEMS * na,))),
               compiler_params=pltpu.CompilerParams(collective_id=collective_id))
    def launch(send_sems, recv_sems):
        x, y, c, _ = _position()
        peers = [(1 - x, y, c), (x, 1 - y, c), (x, y, 1 - c)]
        barrier = pltpu.get_barrier_semaphore()
        for peer in peers:
            pl.semaphore_signal(barrier, inc=1, device_id=peer, device_id_type=MESH)
        pl.semaphore_wait(barrier, len(peers))
        _ag_start(ins, outs, send_sems, recv_sems)
        _ag_finish(ins, outs, send_sems, recv_sems)

    launch()
    return [o[...] for o in outs]


def _sibling_swap(name, arrs, split_rows, collective_id=None):
    na = len(arrs)
    shapes = [jax.ShapeDtypeStruct((a.shape[0], a.shape[1] // 2, a.shape[2]) if split_rows else a.shape, a.dtype)
              for a in arrs]

    def run(ins, outs, send_sems, recv_sems):
        x, y, c, _ = _position()
        cps = []
        for a in range(na):
            src = ins[a]
            if split_rows:
                half = src.shape[1] // 2
                src = src.at[:, pl.ds((1 - c) * half, half), :]
            cp = pltpu.make_async_remote_copy(
                src_ref=src, dst_ref=outs[a], send_sem=send_sems.at[a], recv_sem=recv_sems.at[a],
                device_id=(x, y, 1 - c), device_id_type=MESH)
            cp.start()
            cps.append(cp)
        for cp in cps:
            cp.wait()

    sems = (pltpu.SemaphoreType.DMA((na,)), pltpu.SemaphoreType.DMA((na,)))
    if collective_id is None:
        return pl.pallas_call(
            lambda *refs: run(refs[:na], refs[na:2 * na], *refs[2 * na:]), name=name, out_shape=shapes,
            in_specs=[ANY] * na, out_specs=[ANY] * na, scratch_shapes=list(sems))(*arrs)

    hbm = pltpu.MemorySpace.HBM
    ins = [jax.new_ref(a, memory_space=hbm) for a in arrs]
    outs = [jax.empty_ref(s, memory_space=hbm) for s in shapes]

    @pl.kernel(mesh=plsc.ScalarSubcoreMesh(axis_name="sequencer", num_cores=1), name=name, scratch_types=sems,
               compiler_params=pltpu.CompilerParams(collective_id=collective_id))
    def launch(send_sems, recv_sems):
        x, y, c, _ = _position()
        barrier = pltpu.get_barrier_semaphore()
        pl.semaphore_signal(barrier, inc=1, device_id=(x, y, 1 - c), device_id_type=MESH)
        pl.semaphore_wait(barrier, 1)
        run(ins, outs, send_sems, recv_sems)

    launch()
    return [o[...] for o in outs]


def _xchg_copies(ins, outs, send_sems, recv_sems):
    x, y, c, chips = _position()
    return [pltpu.make_async_remote_copy(
        src_ref=ins[a].at[2 * chip[0] + chip[1]], dst_ref=outs[a].at[j],
        send_sem=send_sems.at[3 * a + j], recv_sem=recv_sems.at[3 * a + j],
        device_id=(*chip, c), device_id_type=MESH) for a in range(len(ins)) for j, chip in enumerate(chips)]


def _exchange_chips(name, collective_id, parts):
    na = len(parts)
    hbm = pltpu.MemorySpace.HBM
    ins = [jax.new_ref(p, memory_space=hbm) for p in parts]
    outs = [jax.empty_ref(jax.ShapeDtypeStruct((3,) + p.shape[1:], p.dtype), memory_space=hbm) for p in parts]

    @pl.kernel(mesh=plsc.ScalarSubcoreMesh(axis_name="sequencer", num_cores=1), name=name,
               scratch_types=(pltpu.SemaphoreType.DMA((3 * na,)), pltpu.SemaphoreType.DMA((3 * na,))),
               compiler_params=pltpu.CompilerParams(collective_id=collective_id))
    def launch(send_sems, recv_sems):
        x, y, c, chips = _position()
        barrier = pltpu.get_barrier_semaphore()
        for chip in chips:
            pl.semaphore_signal(barrier, inc=1, device_id=(*chip, c), device_id_type=MESH)
        pl.semaphore_wait(barrier, len(chips))
        for cp in _xchg_copies(ins, outs, send_sems, recv_sems):
            cp.start()
        for cp in _xchg_copies(ins, outs, send_sems, recv_sems):
            cp.wait()

    launch()
    return [q[...] for q in outs]


def _add_sibling(name, grad, recv, core, after=()):
    _, r, c = grad.shape
    half = r // 2
    rb = min(half, 256)
    nrb = half // rb

    def body(core_ref, g_ref, r_ref, *refs):
        refs[-1][...] = (g_ref[...].astype(F32) + r_ref[...].astype(F32)).astype(BF16)

    return pl.pallas_call(
        body, name=name,
        grid_spec=pltpu.PrefetchScalarGridSpec(
            num_scalar_prefetch=1, grid=(N_CHIP, nrb),
            in_specs=[pl.BlockSpec((1, rb, c), lambda j, i, core_ref: (j, core_ref[0] * nrb + i, 0)),
                      pl.BlockSpec((1, rb, c), lambda j, i, core_ref: (j, i, 0))] + [ANY] * len(after),
            out_specs=pl.BlockSpec((1, rb, c), lambda j, i, core_ref: (j, i, 0))),
        out_shape=jax.ShapeDtypeStruct((N_CHIP, half, c), BF16),
        compiler_params=pltpu.CompilerParams(dimension_semantics=("arbitrary", "arbitrary"),
                                             vmem_limit_bytes=VMEM_LIMIT),
    )(core, grad, recv, *after)


def _add_halves(name, pairs):
    n = len(pairs)
    _, half, c = pairs[0][0].shape
    rb = min(half, 256)

    def body(*refs):
        for k in range(n):
            refs[2 * n + k][...] = (refs[2 * k][...].astype(F32) + refs[2 * k + 1][...].astype(F32)).astype(BF16)

    spec = pl.BlockSpec((1, rb, c), lambda j, i: (j, i, 0))
    return pl.pallas_call(
        body, name=name, grid=(N_CHIP, half // rb), in_specs=[spec] * (2 * n), out_specs=[spec] * n,
        out_shape=[jax.ShapeDtypeStruct((N_CHIP, half, c), BF16)] * n,
        compiler_params=pltpu.CompilerParams(dimension_semantics=("arbitrary", "arbitrary"),
                                             vmem_limit_bytes=VMEM_LIMIT),
    )(*[t for p in pairs for t in p])


def _add_chips(name, chip, p, q, after=()):
    _, half, c = q.shape
    rb = min(half, 256)

    def body(chip_ref, p_ref, q_ref, *refs):
        acc = p_ref[0].astype(F32)
        for j in range(3):
            acc = acc + q_ref[j].astype(F32)
        refs[-1][...] = acc

    return pl.pallas_call(
        body, name=name,
        grid_spec=pltpu.PrefetchScalarGridSpec(
            num_scalar_prefetch=1, grid=(half // rb,),
            in_specs=[pl.BlockSpec((1, rb, c), lambda i, chip_ref: (chip_ref[0], i, 0)),
                      pl.BlockSpec((3, rb, c), lambda i, chip_ref: (0, i, 0))] + [ANY] * len(after),
            out_specs=pl.BlockSpec((rb, c), lambda i, chip_ref: (i, 0))),
        out_shape=jax.ShapeDtypeStruct((half, c), F32),
        compiler_params=pltpu.CompilerParams(dimension_semantics=("arbitrary",), vmem_limit_bytes=VMEM_LIMIT),
    )(chip, p, q, *after)


def _small_update(gad, gam, gl, gg, mychip, params):
    names = ["ada_b", "norm1_g", "lru_conv_b", "gate_a_w", "gate_a_b", "gate_x_w", "gate_x_b", "a_param",
             "lru_conv_w", "short_conv_w", "lru_out_g", "conv_out_g", "norm2_g", "final_g"]
    flat = [t for n in names for t in params[n]]
    nin = len(flat)

    def body(chip_ref, gad_ref, gam_ref, gl_ref, gg_ref, *refs):
        ins = {n: refs[3 * k:3 * k + 3] for k, n in enumerate(names)}
        outs = {n: refs[nin + 4 * k:nin + 4 * k + 4] for k, n in enumerate(names)}
        loss_ref, dmod_ref = refs[nin + 4 * len(names):nin + 4 * len(names) + 2]

        def dsum(ref, lo, n):
            per = ref.shape[0] // 8
            acc = ref[lo:lo + n, :].astype(F32)
            for dev in range(1, 8):
                acc = acc + ref[dev * per + lo:dev * per + lo + n, :].astype(F32)
            return acc

        def update(n, g):
            w_ref, m_ref, v_ref = ins[n]
            g_ref, d_ref, mo_ref, vo_ref = outs[n]
            g_ref[...] = g
            d_ref[...], mo_ref[...], vo_ref[...] = _adam_math(w_ref[...], g, m_ref[...], v_ref[...])

        d, dm, l, lw = refs[-4:]
        d[...] = dsum(gad_ref, 0, 8)
        dm[...] = dsum(gam_ref, 0, 8)
        l[...] = dsum(gl_ref, 0, 16)
        for dev in range(8):
            for k in range(3):
                dmod_ref[dev:dev + 1, k * D_MODEL:(k + 1) * D_MODEL] = gad_ref[dev * 8 + k:dev * 8 + k + 1, :]
                dmod_ref[dev:dev + 1, (3 + k) * D_MODEL:(4 + k) * D_MODEL] = gam_ref[dev * 8 + k:dev * 8 + k + 1, :]
        w_ref, m_ref, v_ref = ins["ada_b"]
        g_ref, d_ref, mo_ref, vo_ref = outs["ada_b"]
        for k in range(3):
            g_ref[:, k * D_MODEL:(k + 1) * D_MODEL] = d[k:k + 1, :]
            g_ref[:, (3 + k) * D_MODEL:(4 + k) * D_MODEL] = dm[k:k + 1, :]
        d_ref[...], mo_ref[...], vo_ref[...] = _adam_math(w_ref[...], g_ref[...], m_ref[...], v_ref[...])
        update("norm1_g", d[3:4, :])
        update("norm2_g", dm[3:4, :])
        update("final_g", dm[4:5, :])
        update("gate_a_b", d[4:5, 0:D_LRU])
        update("gate_x_b", d[4:5, D_LRU:2 * D_LRU])
        update("lru_conv_b", l[4:5, :])
        update("a_param", l[8:9, :] * jax.nn.sigmoid(ins["a_param"][0][...]))
        update("lru_out_g", l[9:10, :])
        update("conv_out_g", l[10:11, :])
        loss_ref[...] = jnp.broadcast_to(dm[5:6, 0:128], (8, 128))
        chip = chip_ref[0]
        acc = jnp.zeros((8, 128), F32)
        for j in range(N_CHIP):
            acc = acc + jnp.where(chip == j, l[0:8, j * 128:(j + 1) * 128], 0.0)
        lw[...] = acc
        update("lru_conv_w", lw[0:4, :])
        update("short_conv_w", lw[5:8, :])
        gates = dsum(gg_ref, 0, D_LRU)
        update("gate_a_w", gates[:, 0:HEAD])
        update("gate_x_w", gates[:, HEAD:2 * HEAD])

    out_shape = []
    for n in names:
        out_shape += [jax.ShapeDtypeStruct(params[n][0].shape, F32)] * 4
    out_shape += [jax.ShapeDtypeStruct((8, 128), F32), jax.ShapeDtypeStruct((8, 6 * D_MODEL), F32)]
    res = pl.pallas_call(
        body, name="small_update", out_shape=out_shape,
        in_specs=[SMEM] + [VMEM] * (4 + nin),
        out_specs=[VMEM] * len(out_shape),
        scratch_shapes=[pltpu.VMEM((8, D_MODEL), F32), pltpu.VMEM((8, D_MODEL), F32), pltpu.VMEM((16, D_LRU), F32),
                        pltpu.VMEM((8, 128), F32)],
        compiler_params=pltpu.CompilerParams(vmem_limit_bytes=VMEM_LIMIT),
    )(mychip, gad, gam, gl, gg, *flat)
    per = {n: res[4 * k:4 * k + 4] for k, n in enumerate(names)}
    return per, res[-2], res[-1]


def _block_diag(w):
    eye = jnp.eye(8, dtype=w.dtype)
    return (eye[:, None, :, None] * w[:, :, None, :]).reshape(8 * HEAD, 8 * HEAD)


def _diag_blocks(g):
    return jnp.concatenate([g[h * HEAD:(h + 1) * HEAD, h * HEAD:(h + 1) * HEAD] for h in range(8)], axis=0)


def kernel(x, c, ada_w, ada_b, norm1_g, w_in, lru_conv_w, lru_conv_b, gate_a_w, gate_a_b, gate_x_w, gate_x_b, a_param, short_conv_w, lru_out_g, conv_out_g, w_out, norm2_g, w_mlp1, w_mlp2, final_g, loss_target, m_ada_w, m_ada_b, m_norm1_g, m_w_in, m_lru_conv_w, m_lru_conv_b, m_gate_a_w, m_gate_a_b, m_gate_x_w, m_gate_x_b, m_a_param, m_short_conv_w, m_lru_out_g, m_conv_out_g, m_w_out, m_norm2_g, m_w_mlp1, m_w_mlp2, m_final_g, v_ada_w, v_ada_b, v_norm1_g, v_w_in, v_lru_conv_w, v_lru_conv_b, v_gate_a_w, v_gate_a_b, v_gate_x_w, v_gate_x_b, v_a_param, v_short_conv_w, v_lru_out_g, v_conv_out_g, v_w_out, v_norm2_g, v_w_mlp1, v_w_mlp2, v_final_g):
    xi, yi, ci = lax.axis_index("x"), lax.axis_index("y"), lax.axis_index("c")
    mychip = 2 * xi + yi
    me = 4 * xi + 2 * yi + ci

    own_in, own_out = w_in[0].astype(BF16), w_out[0].astype(BF16)
    win_all, wout_all = _allgather_weights("allgather_mixer_weights", 1, [own_in, own_out])
    own_w1, own_w2 = w_mlp1[0].astype(BF16), w_mlp2[0].astype(BF16)
    w1_all, w2_all = _allgather_weights("allgather_mlp_weights", 2, [own_w1, own_w2])

    c_blk = jnp.zeros((8, D_MODEL), F32).at[0:1].set(c)
    cw_blk = jnp.zeros((8, 128), F32).at[0:4].set(lru_conv_w[0]).at[4:7].set(short_conv_w[0])
    c_g, cw_g = _allgather8("allgather_cond", [c_blk, cw_blk])
    c_all = c_g.reshape(8, 8, D_MODEL)[:, 0]
    cw_g = cw_g.reshape(4, 2, 8, 128)[:, 0]
    lcw = cw_g[:, 0:4].transpose(1, 0, 2).reshape(4, D_LRU)
    scw = cw_g[:, 4:7].transpose(1, 0, 2).reshape(3, D_LRU)

    mod_loc = _mod_matmul(c_all, ada_w[0])
    (mod_g,) = _allgather8("allgather_mod", [mod_loc])
    mod_all = mod_g.reshape(4, 2, 8, 6 * D_MODEL // 4)[:, 0].transpose(1, 0, 2).reshape(8, 6 * D_MODEL) + ada_b
    mod_pad = jnp.pad(mod_all.reshape(8, 6, D_MODEL), ((0, 0), (0, 2), (0, 0)))
    mod = lax.dynamic_slice_in_dim(mod_pad, me, 1, axis=0).reshape(8, D_MODEL)

    win, wout = (win_all, own_in), (wout_all, own_out)
    chip = mychip.reshape(1).astype(jnp.int32)
    core = ci.reshape(1).astype(jnp.int32)

    vecd = jnp.concatenate([norm1_g, norm2_g, final_g[None, :], jnp.concatenate([gate_a_b, gate_x_b], axis=1),
                            jnp.zeros((4, D_MODEL), F32)], axis=0)
    vecl = jnp.concatenate([lcw, lru_conv_b, scw, a_param, lru_out_g, conv_out_g, jnp.zeros((5, D_LRU), F32)], axis=0)
    gab = jnp.concatenate([_block_diag(gate_a_w[0]), _block_diag(gate_x_w[0])], axis=1).astype(BF16)
    a64 = _block_diag(jnp.full((8, HEAD, HEAD), 1.0 / HEAD, F32)).astype(BF16)

    hb, proj, hl, ycat, mixed, x1 = _mix_fwd(chip, x[0], mod, vecd, vecl, win, wout, gab, a64)
    dx1, act, dz, dmo, h2b, accm = _mlp_fwd_bwd(
        chip, x1, loss_target[0], mod, vecd, (w1_all, own_w1), (w2_all, own_w2))

    parts_mlp = list(_add_halves("rs_add_sibling_mlp", [_wgrad_split("wgrad_mlp1", h2b, dz, 0, FF_BLK),
                                                        _wgrad_split("wgrad_mlp2", act, dmo, FF_BLK, 0)]))
    q_w1, q_w2 = _exchange_chips("rs_exchange_mlp", 0, parts_mlp)
    grad_x, accd, accl, g_win, g_wout, g_gate = _mix_bwd(
        chip, dx1, x[0], mixed, proj, hl, hb, ycat, mod, vecd, vecl, win, wout, gab, a64)

    g_mix = [g_win, g_wout.reshape(N_CHIP, WOUT_BLK, D_MODEL)]
    recv_mix = _sibling_swap("rs_swap_halves_mix", g_mix, True, collective_id=4)
    own_mlp = [_add_chips("rs_add_chips_mlp%d" % k, chip, p, q) for k, (p, q) in enumerate(zip(parts_mlp, (q_w1, q_w2)))]
    sib_mlp = _sibling_swap("rs_swap_reduced_mlp", own_mlp, False, collective_id=5)
    gg_blk = jnp.concatenate([_diag_blocks(g_gate[:, 0:D_LRU]), _diag_blocks(g_gate[:, D_LRU:2 * D_LRU])], axis=1)
    gad, gam, gl, gg = _allgather8("allgather_small_grads", [accd, accm, accl, gg_blk.astype(BF16)])

    parts_mix = [_add_sibling("rs_add_sibling_mix%d" % k, g, r, core, after=[gad])
                 for k, (g, r) in enumerate(zip(g_mix, recv_mix))]
    landed_mix = _exchange_chips("rs_exchange_mix", 3, parts_mix)
    res_w1, res_w2 = _adam("adam_mlp", core, [(w_mlp1[0], own_mlp[0], sib_mlp[0], m_w_mlp1[0], v_w_mlp1[0]),
                                              (w_mlp2[0], own_mlp[1], sib_mlp[1], m_w_mlp2[0], v_w_mlp2[0])])
    own_mix = [_add_chips("rs_add_chips_mix%d" % k, chip, p, q, after=[res_w1[1]])
               for k, (p, q) in enumerate(zip(parts_mix, landed_mix))]
    sib_mix = _sibling_swap("rs_swap_reduced_mix", own_mix, False)
    (res_win,) = _adam("adam_w_in", core, [(w_in[0], own_mix[0], sib_mix[0], m_w_in[0], v_w_in[0])])
    (res_wout,) = _adam("adam_w_out", core, [(w_out[0], own_mix[1], sib_mix[1], m_w_out[0], v_w_out[0])])

    params = {
        "ada_b": (ada_b, m_ada_b, v_ada_b), "norm1_g": (norm1_g, m_norm1_g, v_norm1_g),
        "lru_conv_b": (lru_conv_b, m_lru_conv_b, v_lru_conv_b),
        "gate_a_w": tuple(t.reshape(D_LRU, HEAD) for t in (gate_a_w, m_gate_a_w, v_gate_a_w)),
        "gate_a_b": (gate_a_b, m_gate_a_b, v_gate_a_b),
        "gate_x_w": tuple(t.reshape(D_LRU, HEAD) for t in (gate_x_w, m_gate_x_w, v_gate_x_w)),
        "gate_x_b": (gate_x_b, m_gate_x_b, v_gate_x_b), "a_param": (a_param, m_a_param, v_a_param),
        "lru_conv_w": tuple(t[0] for t in (lru_conv_w, m_lru_conv_w, v_lru_conv_w)),
        "short_conv_w": tuple(t[0] for t in (short_conv_w, m_short_conv_w, v_short_conv_w)),
        "lru_out_g": (lru_out_g, m_lru_out_g, v_lru_out_g), "conv_out_g": (conv_out_g, m_conv_out_g, v_conv_out_g),
        "norm2_g": (norm2_g, m_norm2_g, v_norm2_g),
        "final_g": tuple(t[None, :] for t in (final_g, m_final_g, v_final_g)),
    }
    small, loss_blk, dmod_cols = _small_update(gad, gam, gl, gg, chip, params)
    loss = loss_blk[0, 0]

    ncol = 6 * D_MODEL // N_CHIP
    dmod_loc = lax.dynamic_slice_in_dim(dmod_cols, mychip * ncol, ncol, axis=1)
    sct = (c_all * jax.nn.sigmoid(c_all)).T
    ada = _ada_grad_adam(sct, dmod_loc, ada_w[0], m_ada_w[0], v_ada_w[0])

    res = {"ada_w": ada, "w_in": res_win, "w_out": res_wout, "w_mlp1": res_w1, "w_mlp2": res_w2}
    res = {n: tuple(t[None] for t in r) for n, r in res.items()}
    shapes = {"gate_a_w": gate_a_w.shape, "gate_x_w": gate_x_w.shape, "lru_conv_w": lru_conv_w.shape,
              "short_conv_w": short_conv_w.shape, "final_g": final_g.shape}
    for n, t in small.items():
        res[n] = tuple(u.reshape(shapes[n]) if n in shapes else u for u in t)

    order = ["ada_w", "ada_b", "norm1_g", "w_in", "lru_conv_w", "lru_conv_b", "gate_a_w", "gate_a_b", "gate_x_w",
             "gate_x_b", "a_param", "short_conv_w", "lru_out_g", "conv_out_g", "w_out", "norm2_g", "w_mlp1",
             "w_mlp2", "final_g"]
    return (loss, grad_x[None], *[res[n][0] for n in order], *[res[n][1] for n in order],
            *[res[n][2] for n in order], *[res[n][3] for n in order])
```

```python
import jax
import jax.numpy as jnp
from jax import lax
from jax.experimental import pallas as pl
from jax.experimental.pallas import tpu as pltpu
from jax.experimental.pallas import tpu_sc as plsc

F32 = jnp.float32
BF16 = jnp.bfloat16

D_MODEL = 1024
D_LRU = 512
D_IN = 2560
D_FF = 4096
N_CHIP = 4
WIN_BLK = D_IN // N_CHIP
WOUT_BLK = D_MODEL // N_CHIP
FF_BLK = D_FF // N_CHIP
HEAD = 64
EPS = 1e-6
C_GATE = 8.0
TOKEN_TILE = 256
HALO = 8
VMEM_LIMIT = 60 * 1024 * 1024

ADAM_LR = 0.001
ADAM_B1 = 0.9
ADAM_B2 = 0.999
ADAM_EPS = 1e-08
ADAM_WD = 0.01
ADAM_STEP = 10

MESH = pl.DeviceIdType.MESH
ANY = pl.BlockSpec(memory_space=pl.ANY)
VMEM = pl.BlockSpec(memory_space=pltpu.VMEM)
SMEM = pl.BlockSpec(memory_space=pltpu.SMEM)


def _full(shape, single=False):
    nd = len(shape)
    if single:
        return pl.BlockSpec(shape, lambda *_: (0,) * nd, pipeline_mode=pl.Buffered(1))
    return pl.BlockSpec(shape, lambda *_: (0,) * nd)


def _dot(a, b):
    return jnp.dot(a, b, preferred_element_type=F32)


def _dot_nt(a, b):
    return lax.dot_general(a, b, (((1,), (1,)), ((), ())), preferred_element_type=F32)


def _dot_tn(a, b):
    return lax.dot_general(a, b, (((0,), (0,)), ((), ())), preferred_element_type=F32)


def _gmean(v, a64):
    hi = v.astype(BF16)
    lo = (v - hi.astype(F32)).astype(BF16)
    return _dot(hi, a64) + _dot(lo, a64)


def _gelu(x):
    u = 0.7978845608028654 * (x + 0.044715 * x * x * x)
    t = jnp.tanh(u)
    return 0.5 * x * (1.0 + t), t


def _gelu_grad(x, t):
    du = 0.7978845608028654 * (1.0 + 3.0 * 0.044715 * x * x)
    return 0.5 * (1.0 + t) + 0.5 * x * (1.0 - t * t) * du


def _log1p_pos(y):
    return jnp.where(y < 1e-2, y * (1.0 - y * (0.5 - y * (1.0 / 3.0 - y * 0.25))), jnp.log(1.0 + y))


def _softplus(a):
    return jnp.maximum(a, 0.0) + _log1p_pos(jnp.exp(-jnp.abs(a)))


def _neg_expm1(z):
    series = -z * (1.0 + z * (0.5 + z * (1.0 / 6.0 + z * (1.0 / 24.0))))
    return jnp.where(z > -0.02, series, 1.0 - jnp.exp(z))


def _scan_fwd(a, b, row):
    n = a.shape[0]
    d = 1
    while d < n:
        m = row >= d
        b = jnp.where(m, a * pltpu.roll(b, d, 0) + b, b)
        a = jnp.where(m, a * pltpu.roll(a, d, 0), a)
        d *= 2
    return a, b


def _scan_rev(a, b, row):
    n = a.shape[0]
    d = 1
    while d < n:
        m = row < n - d
        b = jnp.where(m, b + a * pltpu.roll(b, n - d, 0), b)
        a = jnp.where(m, a * pltpu.roll(a, n - d, 0), a)
        d *= 2
    return a, b


def _colsum(v):
    return jnp.sum(v, axis=0, keepdims=True)


def _load_gathered(chip, gathered, own, slot, sems):
    copies = []
    for j in range(N_CHIP):
        @pl.when(chip == j)
        def _(j=j):
            pltpu.make_async_copy(own, slot(j), sems.at[j]).start()

        @pl.when(chip != j)
        def _(j=j):
            pltpu.make_async_copy(gathered.at[j], slot(j), sems.at[j]).start()

        copies.append(pltpu.make_async_copy(own, slot(j), sems.at[j]))
    return copies


def _lru_gates(xlb, gab, gbias, sp, first_row):
    g = _dot(xlb, gab) + gbias
    r = jax.nn.sigmoid(g[:, :D_LRU])
    ig = jax.nn.sigmoid(g[:, D_LRU:])
    la = (-C_GATE) * r * sp
    a = jnp.exp(la)
    msq = jnp.sqrt(_neg_expm1(2.0 * la))
    mult = jnp.where(first_row, 1.0, msq)
    return r, ig, a, msq, mult


def _mix_fwd(chip, x, mod, vecd, vecl, win, wout, gab, a64):
    s = x.shape[0]
    ts = TOKEN_TILE
    nt = s // ts

    def body(chip_ref, x_ref, mod_ref, vd_ref, vl_ref, win_hbm, win_own, wout_hbm, wout_own, gab_ref, a64_ref,
             hb_ref, proj_ref, hl_ref, ycat_ref, mixed_ref, x1_ref,
             win_ref, wout_ref, ext_lx, ext_cv, hcar, sems):
        i = pl.program_id(0)

        @pl.when(i == 0)
        def _():
            cps = _load_gathered(chip_ref[0], win_hbm, win_own, lambda j: win_ref.at[j], sems.at[pl.ds(0, N_CHIP)])
            cps += _load_gathered(chip_ref[0], wout_hbm, wout_own,
                                  lambda j: wout_ref.at[pl.ds(j * WOUT_BLK, WOUT_BLK), :],
                                  sems.at[pl.ds(N_CHIP, N_CHIP)])
            ext_lx[0:HALO, :] = jnp.zeros((HALO, D_LRU), F32)
            ext_cv[0:HALO, :] = jnp.zeros((HALO, D_LRU), F32)
            hcar[...] = jnp.zeros_like(hcar)
            for cp in cps:
                cp.wait()

        row = lax.broadcasted_iota(jnp.int32, (ts, D_LRU), 0)
        first_row = jnp.logical_and(row == 0, i == 0)
        xt = x_ref[...]
        shift1, scale1, gate1 = mod_ref[0:1, :], mod_ref[1:2, :], mod_ref[2:3, :]
        r1 = lax.rsqrt(jnp.mean(xt * xt, axis=-1, keepdims=True) + EPS)
        h = (xt * r1) * vd_ref[0:1, :] * (1.0 + scale1) + shift1
        hb = h.astype(BF16)
        hb_ref[...] = hb
        for j in range(N_CHIP):
            proj_ref[:, j * WIN_BLK:(j + 1) * WIN_BLK] = _dot(hb, win_ref[j])
        u_ly = proj_ref[:, 512:1024]
        u_b = proj_ref[:, 1024:1536]

        ext_lx[HALO:HALO + ts, :] = proj_ref[:, 0:512]
        xl = vl_ref[4:5, :] + vl_ref[0:1, :] * ext_lx[pl.ds(5, ts), :]
        for k in range(1, 4):
            xl = xl + vl_ref[k:k + 1, :] * ext_lx[pl.ds(5 + k, ts), :]
        ext_lx[0:HALO, :] = ext_lx[ts:ts + HALO, :]
        sp = _softplus(vl_ref[8:9, :])
        _, ig, a, _, mult = _lru_gates(xl.astype(BF16), gab_ref[...], vd_ref[3:4, :], sp, first_row)
        acum, hloc = _scan_fwd(a, mult * (ig * xl), row)
        hl = hloc + acum * hcar[0:1, :]
        hl_ref[...] = hl
        hcar[0:1, :] = hl_ref[ts - 1:ts, :]
        ge, _ = _gelu(u_ly)
        p = ge * hl
        y_lru = p * lax.rsqrt(_gmean(p * p, a64_ref[...]) + EPS) * vl_ref[9:10, :]
        ycat_ref[:, 0:512] = y_lru.astype(BF16)

        ext_cv[HALO:HALO + ts, :] = proj_ref[:, 1536:2048] * proj_ref[:, 2048:2560]
        q = vl_ref[5:6, :] * ext_cv[pl.ds(6, ts), :]
        for k in range(1, 3):
            q = q + vl_ref[5 + k:6 + k, :] * ext_cv[pl.ds(6 + k, ts), :]
        ext_cv[0:HALO, :] = ext_cv[ts:ts + HALO, :]
        yc = u_b * q
        y_conv = yc * lax.rsqrt(_gmean(yc * yc, a64_ref[...]) + EPS) * vl_ref[10:11, :]
        ycat_ref[:, 512:1024] = y_conv.astype(BF16)

        mixed = _dot(ycat_ref[...], wout_ref[...])
        mixed_ref[...] = mixed
        x1_ref[...] = xt + gate1 * mixed

    tile = lambda w: pl.BlockSpec((ts, w), lambda i: (i, 0))
    return pl.pallas_call(
        body, name="mix_fwd", grid=(nt,),
        in_specs=[SMEM, tile(D_MODEL), _full((8, D_MODEL)), _full((8, D_MODEL)), _full((16, D_LRU)),
                  ANY, ANY, ANY, ANY, _full((D_LRU, 2 * D_LRU), True), _full((D_LRU, D_LRU), True)],
        out_specs=[tile(D_MODEL), tile(D_IN), tile(D_LRU), tile(D_MODEL), tile(D_MODEL), tile(D_MODEL)],
        out_shape=[jax.ShapeDtypeStruct((s, D_MODEL), BF16), jax.ShapeDtypeStruct((s, D_IN), F32),
                   jax.ShapeDtypeStruct((s, D_LRU), F32), jax.ShapeDtypeStruct((s, D_MODEL), BF16),
                   jax.ShapeDtypeStruct((s, D_MODEL), F32), jax.ShapeDtypeStruct((s, D_MODEL), F32)],
        scratch_shapes=[pltpu.VMEM((N_CHIP, D_MODEL, WIN_BLK), BF16), pltpu.VMEM((D_MODEL, D_MODEL), BF16),
                        pltpu.VMEM((ts + HALO, D_LRU), F32), pltpu.VMEM((ts + HALO, D_LRU), F32),
                        pltpu.VMEM((HALO, D_LRU), F32), pltpu.SemaphoreType.DMA((2 * N_CHIP,))],
        compiler_params=pltpu.CompilerParams(dimension_semantics=("arbitrary",), vmem_limit_bytes=VMEM_LIMIT),
    )(chip, x, mod, vecd, vecl, *win, *wout, gab, a64)


def _mlp_fwd_bwd(chip, x1, target, mod, vecd, w1, w2):
    s = x1.shape[0]
    ts = TOKEN_TILE
    nt = s // ts

    def body(chip_ref, x1_ref, tg_ref, mod_ref, vd_ref, w1_hbm, w1_own, w2_hbm, w2_own,
             dx1_ref, act_ref, dz_ref, dmo_ref, h2_ref, acc_ref, w1_v, w2_v, rz_v, sems):
        i = pl.program_id(0)

        @pl.when(i == 0)
        def _():
            cps = _load_gathered(chip_ref[0], w1_hbm, w1_own, lambda j: w1_v.at[j], sems.at[pl.ds(0, N_CHIP)])
            cps += _load_gathered(chip_ref[0], w2_hbm, w2_own, lambda j: w2_v.at[j], sems.at[pl.ds(N_CHIP, N_CHIP)])
            acc_ref[...] = jnp.zeros_like(acc_ref)
            for cp in cps:
                cp.wait()

        xt = x1_ref[...]
        shift2, scale2, gate2 = mod_ref[3:4, :], mod_ref[4:5, :], mod_ref[5:6, :]
        g2, gf = vd_ref[1:2, :], vd_ref[2:3, :]
        r2 = lax.rsqrt(jnp.mean(xt * xt, axis=-1, keepdims=True) + EPS)
        n2 = xt * r2
        h2b = (n2 * g2 * (1.0 + scale2) + shift2).astype(BF16)
        h2_ref[...] = h2b
        for j in range(N_CHIP):
            rz_v[j] = jnp.maximum(_dot(h2b, w1_v[j]), 0.0)
        mo = jnp.zeros((ts, D_MODEL), F32)
        for j in range(N_CHIP):
            rz = rz_v[j]
            actb = (rz * rz).astype(BF16)
            act_ref[:, j * FF_BLK:(j + 1) * FF_BLK] = actb
            mo = mo + _dot(actb, w2_v[j])
        x2 = xt + gate2 * mo
        r3 = lax.rsqrt(jnp.mean(x2 * x2, axis=-1, keepdims=True) + EPS)
        n3 = x2 * r3
        e = n3 * gf - tg_ref[...]
        loss = (0.5 / D_MODEL) * jnp.sum(_colsum(e * e), axis=1, keepdims=True)
        dy = e * (1.0 / D_MODEL)
        acc_ref[4:5, :] += _colsum(dy * n3)
        acc_ref[5:6, :] += jnp.broadcast_to(loss, (1, D_MODEL))
        dn3 = dy * gf
        dx2 = r3 * (dn3 - n3 * jnp.mean(dn3 * n3, axis=-1, keepdims=True))
        acc_ref[2:3, :] += _colsum(dx2 * mo)
        dmob = (dx2 * gate2).astype(BF16)
        dmo_ref[...] = dmob
        for j in range(N_CHIP):
            dz_ref[:, j * FF_BLK:(j + 1) * FF_BLK] = (_dot_nt(dmob, w2_v[j]) * (2.0 * rz_v[j])).astype(BF16)
        dh2 = jnp.zeros((ts, D_MODEL), F32)
        for j in range(N_CHIP):
            dh2 = dh2 + _dot_nt(dz_ref[:, j * FF_BLK:(j + 1) * FF_BLK], w1_v[j])
        acc_ref[1:2, :] += _colsum(dh2 * (n2 * g2))
        acc_ref[0:1, :] += _colsum(dh2)
        dhn2 = dh2 * (1.0 + scale2)
        acc_ref[3:4, :] += _colsum(dhn2 * n2)
        dn2 = dhn2 * g2
        dx1_ref[...] = dx2 + r2 * (dn2 - n2 * jnp.mean(dn2 * n2, axis=-1, keepdims=True))

    tile = lambda w: pl.BlockSpec((ts, w), lambda i: (i, 0))
    return pl.pallas_call(
        body, name="mlp_fwd_bwd", grid=(nt,),
        in_specs=[SMEM, tile(D_MODEL), tile(D_MODEL), _full((8, D_MODEL)), _full((8, D_MODEL)), ANY, ANY, ANY, ANY],
        out_specs=[tile(D_MODEL), tile(D_FF), tile(D_FF), tile(D_MODEL), tile(D_MODEL), _full((8, D_MODEL))],
        out_shape=[jax.ShapeDtypeStruct((s, D_MODEL), F32), jax.ShapeDtypeStruct((s, D_FF), BF16),
                   jax.ShapeDtypeStruct((s, D_FF), BF16), jax.ShapeDtypeStruct((s, D_MODEL), BF16),
                   jax.ShapeDtypeStruct((s, D_MODEL), BF16), jax.ShapeDtypeStruct((8, D_MODEL), F32)],
        scratch_shapes=[pltpu.VMEM((N_CHIP, D_MODEL, FF_BLK), BF16), pltpu.VMEM((N_CHIP, FF_BLK, D_MODEL), BF16),
                        pltpu.VMEM((N_CHIP, ts, FF_BLK), F32), pltpu.SemaphoreType.DMA((2 * N_CHIP,))],
        compiler_params=pltpu.CompilerParams(dimension_semantics=("arbitrary",), vmem_limit_bytes=VMEM_LIMIT),
    )(chip, x1, target, mod, vecd, *w1, *w2)


def _mix_bwd(chip, dx1, x, mixed, proj, hl, hb, ycat, mod, vecd, vecl, win, wout, gab, a64):
    s = x.shape[0]
    ts = TOKEN_TILE
    nt = s // ts
    hpt = ts // HALO

    def body(chip_ref, dx1_ref, x_ref, mixed_ref, proj_ref, projh_ref, hl_ref, hlh_ref, hb_ref, ycat_ref,
             mod_ref, vd_ref, vl_ref, win_hbm, win_own, wout_hbm, wout_own, gab_ref, a64_ref,
             gx_ref, accd_ref, accl_ref, gwin_hbm, gwout_hbm, ggate_hbm,
             win_ref, wout_ref, dproj_ref, dgb_ref, gwin_acc, gwout_acc, ggate_acc,
             ext_lx, ext_cv, ext_hl, ext_dxl, ext_dq, gbuf, gcar, acar, sems):
        i = pl.program_id(0)
        ri = nt - 1 - i

        @pl.when(i == 0)
        def _():
            gwin_acc[...] = jnp.zeros_like(gwin_acc)
            gwout_acc[...] = jnp.zeros_like(gwout_acc)
            ggate_acc[...] = jnp.zeros_like(ggate_acc)
            cps = _load_gathered(chip_ref[0], win_hbm, win_own, lambda j: win_ref.at[j], sems.at[pl.ds(0, N_CHIP)])
            cps += _load_gathered(chip_ref[0], wout_hbm, wout_own,
                                  lambda j: wout_ref.at[pl.ds(j * WOUT_BLK, WOUT_BLK), :],
                                  sems.at[pl.ds(N_CHIP, N_CHIP)])
            for cp in cps:
                cp.wait()
            accd_ref[...] = jnp.zeros_like(accd_ref)
            accl_ref[...] = jnp.zeros_like(accl_ref)
            ext_dxl[ts:ts + HALO, :] = jnp.zeros((HALO, D_LRU), F32)
            ext_dq[ts:ts + HALO, :] = jnp.zeros((HALO, D_LRU), F32)
            gcar[...] = jnp.zeros_like(gcar)
            acar[...] = jnp.zeros_like(acar)

        row = lax.broadcasted_iota(jnp.int32, (ts, D_LRU), 0)
        first_row = jnp.logical_and(row == 0, ri == 0)
        halo_on = jnp.where(ri == 0, 0.0, 1.0)
        shift1, scale1, gate1 = mod_ref[0:1, :], mod_ref[1:2, :], mod_ref[2:3, :]
        g1 = vd_ref[0:1, :]
        a64m = a64_ref[...]
        lg, cg = vl_ref[9:10, :], vl_ref[10:11, :]

        dx1 = dx1_ref[...]
        accd_ref[2:3, :] += _colsum(dx1 * mixed_ref[...])
        dmb = (dx1 * gate1).astype(BF16)
        gwout_acc[...] += _dot_tn(ycat_ref[...], dmb)
        dycat = _dot_nt(dmb, wout_ref[...])
        dyl = dycat[:, 0:512]
        dyv = dycat[:, 512:1024]

        u_ly = proj_ref[:, 512:1024]
        u_b = proj_ref[:, 1024:1536]
        u_c = proj_ref[:, 1536:2048]
        u_v = proj_ref[:, 2048:2560]
        ext_lx[0:HALO, :] = projh_ref[:, 0:512] * halo_on
        ext_lx[HALO:HALO + ts, :] = proj_ref[:, 0:512]
        xl = vl_ref[4:5, :] + vl_ref[0:1, :] * ext_lx[pl.ds(5, ts), :]
        for k in range(1, 4):
            xl = xl + vl_ref[k:k + 1, :] * ext_lx[pl.ds(5 + k, ts), :]
        xlb = xl.astype(BF16)
        sp = _softplus(vl_ref[8:9, :])
        r, ig, a, msq, mult = _lru_gates(xlb, gab_ref[...], vd_ref[3:4, :], sp, first_row)
        hl = hl_ref[...]
        ge, th = _gelu(u_ly)
        p = ge * hl
        rl = lax.rsqrt(_gmean(p * p, a64m) + EPS)
        nl = p * rl
        ext_cv[0:HALO, :] = projh_ref[:, 1536:2048] * projh_ref[:, 2048:2560] * halo_on
        ext_cv[HALO:HALO + ts, :] = u_c * u_v
        q = vl_ref[5:6, :] * ext_cv[pl.ds(6, ts), :]
        for k in range(1, 3):
            q = q + vl_ref[5 + k:6 + k, :] * ext_cv[pl.ds(6 + k, ts), :]
        yc = u_b * q
        rc = lax.rsqrt(_gmean(yc * yc, a64m) + EPS)
        nc = yc * rc

        accl_ref[9:10, :] += _colsum(dyl * nl)
        dnl = dyl * lg
        dp = rl * (dnl - nl * _gmean(dnl * nl, a64m))
        dproj_ref[:, 512:1024] = ((dp * hl) * _gelu_grad(u_ly, th)).astype(BF16)
        a_next = jnp.where(row == ts - 1, acar[0:1, :], pltpu.roll(a, ts - 1, 0))
        acum, gloc = _scan_rev(a_next, dp * ge, row)
        gbuf[...] = gloc + acum * gcar[0:1, :]
        gcar[0:1, :] = gbuf[0:1, :]
        ext_hl[0:HALO, :] = hlh_ref[...] * halo_on
        ext_hl[HALO:HALO + ts, :] = hl
        acar[...] = a[0:HALO, :]
        gt = gbuf[...]
        da = gt * ext_hl[pl.ds(HALO - 1, ts), :]
        dmult = gt * ig * xl
        di = gt * mult * xl
        dxl = gt * mult * ig
        dla = da * a - jnp.where(first_row, 0.0, dmult * a * a / msq)
        accl_ref[8:9, :] += _colsum(dla * ((-C_GATE) * r))
        dra = dla * ((-C_GATE) * sp) * r * (1.0 - r)
        dia = di * ig * (1.0 - ig)
        accd_ref[4:5, 0:D_LRU] += _colsum(dra)
        accd_ref[4:5, D_LRU:2 * D_LRU] += _colsum(dia)
        dgb_ref[:, 0:D_LRU] = dra.astype(BF16)
        dgb_ref[:, D_LRU:2 * D_LRU] = dia.astype(BF16)
        dxl = dxl + _dot_nt(dgb_ref[...], gab_ref[...])
        ggate_acc[...] += _dot_tn(xlb, dgb_ref[...])
        accl_ref[4:5, :] += _colsum(dxl)
        for k in range(4):
            accl_ref[k:k + 1, :] += _colsum(dxl * ext_lx[pl.ds(5 + k, ts), :])
        ext_dxl[0:ts, :] = dxl
        du_lx = vl_ref[0:1, :] * ext_dxl[pl.ds(3, ts), :]
        for k in range(1, 4):
            du_lx = du_lx + vl_ref[k:k + 1, :] * ext_dxl[pl.ds(3 - k, ts), :]
        ext_dxl[ts:ts + HALO, :] = ext_dxl[0:HALO, :]
        dproj_ref[:, 0:512] = du_lx.astype(BF16)

        accl_ref[10:11, :] += _colsum(dyv * nc)
        dnc = dyv * cg
        dyc = rc * (dnc - nc * _gmean(dnc * nc, a64m))
        dproj_ref[:, 1024:1536] = (dyc * q).astype(BF16)
        dq = dyc * u_b
        for k in range(3):
            accl_ref[5 + k:6 + k, :] += _colsum(dq * ext_cv[pl.ds(6 + k, ts), :])
        ext_dq[0:ts, :] = dq
        dcv = vl_ref[5:6, :] * ext_dq[pl.ds(2, ts), :]
        for k in range(1, 3):
            dcv = dcv + vl_ref[5 + k:6 + k, :] * ext_dq[pl.ds(2 - k, ts), :]
        ext_dq[ts:ts + HALO, :] = ext_dq[0:HALO, :]
        dproj_ref[:, 1536:2048] = (dcv * u_v).astype(BF16)
        dproj_ref[:, 2048:2560] = (dcv * u_c).astype(BF16)

        dh = _dot_nt(dproj_ref[:, 0:WIN_BLK], win_ref[0])
        for j in range(1, N_CHIP):
            dh = dh + _dot_nt(dproj_ref[:, j * WIN_BLK:(j + 1) * WIN_BLK], win_ref[j])
        for j in range(N_CHIP):
            gwin_acc[j] += _dot_tn(hb_ref[...], dproj_ref[:, j * WIN_BLK:(j + 1) * WIN_BLK])
        xt = x_ref[...]
        r1 = lax.rsqrt(jnp.mean(xt * xt, axis=-1, keepdims=True) + EPS)
        n1 = xt * r1
        accd_ref[1:2, :] += _colsum(dh * (n1 * g1))
        accd_ref[0:1, :] += _colsum(dh)
        dhn1 = dh * (1.0 + scale1)
        accd_ref[3:4, :] += _colsum(dhn1 * n1)
        dn1 = dhn1 * g1
        gx_ref[...] = dx1 + r1 * (dn1 - n1 * jnp.mean(dn1 * n1, axis=-1, keepdims=True))

        @pl.when(i == nt - 1)
        def _():
            outs = [pltpu.make_async_copy(acc, dst, sems.at[k]) for k, (acc, dst) in enumerate(
                ((gwin_acc, gwin_hbm), (gwout_acc, gwout_hbm), (ggate_acc, ggate_hbm)))]
            for cp in outs:
                cp.start()
            for cp in outs:
                cp.wait()

    tile = lambda w: pl.BlockSpec((ts, w), lambda i: (nt - 1 - i, 0))
    halo = lambda w: pl.BlockSpec((HALO, w), lambda i: (jnp.maximum((nt - 1 - i) * hpt - 1, 0), 0))
    ext = pltpu.VMEM((ts + HALO, D_LRU), F32)
    return pl.pallas_call(
        body, name="mix_bwd", grid=(nt,),
        in_specs=[SMEM, tile(D_MODEL), tile(D_MODEL), tile(D_MODEL), tile(D_IN), halo(D_IN), tile(D_LRU), halo(D_LRU),
                  tile(D_MODEL), tile(D_MODEL), _full((8, D_MODEL)), _full((8, D_MODEL)), _full((16, D_LRU)),
                  ANY, ANY, ANY, ANY, _full((D_LRU, 2 * D_LRU), True), _full((D_LRU, D_LRU), True)],
        out_specs=[tile(D_MODEL), _full((8, D_MODEL)), _full((16, D_LRU)), ANY, ANY, ANY],
        out_shape=[jax.ShapeDtypeStruct((s, D_MODEL), F32),
                   jax.ShapeDtypeStruct((8, D_MODEL), F32), jax.ShapeDtypeStruct((16, D_LRU), F32),
                   jax.ShapeDtypeStruct((N_CHIP, D_MODEL, WIN_BLK), F32), jax.ShapeDtypeStruct((D_MODEL, D_MODEL), F32),
                   jax.ShapeDtypeStruct((D_LRU, 2 * D_LRU), F32)],
        scratch_shapes=[pltpu.VMEM((N_CHIP, D_MODEL, WIN_BLK), BF16), pltpu.VMEM((D_MODEL, D_MODEL), BF16),
                        pltpu.VMEM((ts, D_IN), BF16), pltpu.VMEM((ts, 2 * D_LRU), BF16),
                        pltpu.VMEM((N_CHIP, D_MODEL, WIN_BLK), F32), pltpu.VMEM((D_MODEL, D_MODEL), F32),
                        pltpu.VMEM((D_LRU, 2 * D_LRU), F32),
                        ext, ext, ext, ext, ext, pltpu.VMEM((ts, D_LRU), F32),
                        pltpu.VMEM((HALO, D_LRU), F32), pltpu.VMEM((HALO, D_LRU), F32),
                        pltpu.SemaphoreType.DMA((2 * N_CHIP,))],
        compiler_params=pltpu.CompilerParams(dimension_semantics=("arbitrary",), vmem_limit_bytes=VMEM_LIMIT),
    )(chip, dx1, x, mixed, proj, proj, hl, hl, hb, ycat, mod, vecd, vecl, *win, *wout, gab, a64)


def _wgrad_split(name, collective_id, a, b, a_blk, b_blk):
    s = a.shape[0]
    aw = a_blk or a.shape[1]
    bw = b_blk or b.shape[1]
    half = aw // 2

    def body(a_ref, b_ref, own_hbm, recv_hbm, buf, keep_sems, send_sems, recv_sems):
        j = pl.program_id(0)
        x, y, c, _ = _position()

        def copies(jj):
            slot = jj % 2
            keep = pltpu.make_async_copy(buf.at[slot, pl.ds(c * half, half), :], own_hbm.at[jj], keep_sems.at[slot])
            give = pltpu.make_async_remote_copy(
                src_ref=buf.at[slot, pl.ds((1 - c) * half, half), :], dst_ref=recv_hbm.at[jj],
                send_sem=send_sems.at[slot], recv_sem=recv_sems.at[jj],
                device_id=(x, y, 1 - c), device_id_type=MESH)
            return keep, give

        @pl.when(j == 0)
        def _():
            pl.semaphore_signal(pltpu.get_barrier_semaphore(), inc=1, device_id=(x, y, 1 - c), device_id_type=MESH)

        @pl.when(j >= 2)
        def _():
            keep, give = copies(j - 2)
            keep.wait()
            give.wait_send()

        buf[j % 2] = _dot_tn(a_ref[...], b_ref[...]).astype(BF16)

        @pl.when(j == 0)
        def _():
            pl.semaphore_wait(pltpu.get_barrier_semaphore(), 1)

        keep, give = copies(j)
        keep.start()
        give.start()

        @pl.when(j == N_CHIP - 1)
        def _():
            for jj in (N_CHIP - 2, N_CHIP - 1):
                keep, give = copies(jj)
                keep.wait()
                give.wait_send()
            for jj in range(N_CHIP):
                copies(jj)[1].wait_recv()

    sds = jax.ShapeDtypeStruct((N_CHIP, half, bw), BF16)
    return pl.pallas_call(
        body, name=name, grid=(N_CHIP,),
        in_specs=[pl.BlockSpec((s, aw), (lambda j: (0, j)) if a_blk else (lambda j: (0, 0))),
                  pl.BlockSpec((s, bw), (lambda j: (0, j)) if b_blk else (lambda j: (0, 0)))],
        out_specs=[ANY, ANY], out_shape=[sds, sds],
        scratch_shapes=[pltpu.VMEM((2, aw, bw), BF16), pltpu.SemaphoreType.DMA((2,)), pltpu.SemaphoreType.DMA((2,)),
                        pltpu.SemaphoreType.DMA((N_CHIP,))],
        compiler_params=pltpu.CompilerParams(dimension_semantics=("arbitrary",), vmem_limit_bytes=VMEM_LIMIT,
                                             collective_id=collective_id),
    )(a, b)


def _mod_matmul(c_all, ada_w_loc):
    n = ada_w_loc.shape[1]
    cb = 512

    def body(c_ref, w_ref, o_ref):
        c = c_ref[...]
        sc = c * jax.nn.sigmoid(c)
        o_ref[...] = _dot(sc.astype(BF16), w_ref[...].astype(BF16))

    return pl.pallas_call(
        body, name="mod_matmul", grid=(n // cb,),
        in_specs=[_full((8, D_MODEL)), pl.BlockSpec((D_MODEL, cb), lambda j: (0, j))],
        out_specs=pl.BlockSpec((8, cb), lambda j: (0, j)),
        out_shape=jax.ShapeDtypeStruct((8, n), F32),
        compiler_params=pltpu.CompilerParams(dimension_semantics=("arbitrary",), vmem_limit_bytes=VMEM_LIMIT),
    )(c_all, ada_w_loc)


def _adam_math(w, g, m, v):
    m = ADAM_B1 * m + (1.0 - ADAM_B1) * g
    v = ADAM_B2 * v + (1.0 - ADAM_B2) * (g * g)
    m_hat = m / (1.0 - ADAM_B1 ** ADAM_STEP)
    v_hat = v / (1.0 - ADAM_B2 ** ADAM_STEP)
    delta = (-ADAM_LR) * (m_hat / (jnp.sqrt(v_hat) + ADAM_EPS) + ADAM_WD * w)
    return delta, m, v


def _adam(name, core, shards):
    n = len(shards)
    r, c = shards[0][0].shape
    half = r // 2
    rb = min(half, 128)
    nh = half // rb

    def body(core_ref, *refs):
        ins, outs = refs[:5 * n], refs[5 * n:]
        mine = (pl.program_id(0) // nh) == core_ref[0]
        for k in range(n):
            w_ref, go_ref, gs_ref, m_ref, v_ref = ins[5 * k:5 * k + 5]
            g_ref, d_ref, mo_ref, vo_ref = outs[4 * k:4 * k + 4]
            g = jnp.where(mine, go_ref[...], gs_ref[...])
            g_ref[...] = g
            d_ref[...], mo_ref[...], vo_ref[...] = _adam_math(w_ref[...], g, m_ref[...], v_ref[...])

    spec = pl.BlockSpec((rb, c), lambda i, core_ref: (i, 0))
    hspec = pl.BlockSpec((rb, c), lambda i, core_ref: (i % nh, 0))
    sds = jax.ShapeDtypeStruct((r, c), F32)
    res = pl.pallas_call(
        body, name=name,
        grid_spec=pltpu.PrefetchScalarGridSpec(
            num_scalar_prefetch=1, grid=(r // rb,),
            in_specs=[spec, hspec, hspec, spec, spec] * n, out_specs=[spec] * (4 * n)),
        out_shape=[sds] * (4 * n),
        compiler_params=pltpu.CompilerParams(dimension_semantics=("arbitrary",), vmem_limit_bytes=VMEM_LIMIT),
    )(core, *[t for s in shards for t in s])
    return [res[4 * k:4 * k + 4] for k in range(n)]


def _ada_grad_adam(sct, dmod_loc, w, m, v):
    r, c = w.shape
    rb = 128

    def body(s_ref, dm_ref, w_ref, m_ref, v_ref, g_ref, d_ref, mo_ref, vo_ref):
        g = s_ref[:, 0:1] * dm_ref[0:1, :]
        for b in range(1, 8):
            g = g + s_ref[:, b:b + 1] * dm_ref[b:b + 1, :]
        g_ref[...] = g
        d_ref[...], mo_ref[...], vo_ref[...] = _adam_math(w_ref[...], g, m_ref[...], v_ref[...])

    spec = pl.BlockSpec((rb, c), lambda i: (i, 0))
    sds = jax.ShapeDtypeStruct((r, c), F32)
    return pl.pallas_call(
        body, name="ada_grad_adam", grid=(r // rb,),
        in_specs=[pl.BlockSpec((rb, 8), lambda i: (i, 0)), _full((8, c)), spec, spec, spec],
        out_specs=[spec] * 4, out_shape=[sds] * 4,
        compiler_params=pltpu.CompilerParams(dimension_semantics=("arbitrary",), vmem_limit_bytes=VMEM_LIMIT),
    )(sct, dmod_loc, w, m, v)


def _position():
    x, y, c = lax.axis_index("x"), lax.axis_index("y"), lax.axis_index("c")
    chips = [(1 - x, y), (x, 1 - y), (1 - x, 1 - y)]
    return x, y, c, chips


def _ag8_run(ins, outs, send_sems, recv_sems, local_sems):
    na = len(ins)
    x, y, c, chips = _position()
    me, sibling = (x, y, c), (x, y, 1 - c)
    first, passed, local = [], [], []
    for a in range(na):
        m_per = ins[a].shape[0]

        def rows(px, py, pc, a=a, m_per=m_per):
            return outs[a].at[pl.ds((4 * px + 2 * py + pc) * m_per, m_per), :]

        def copy(k, block, to, src=None, a=a, rows=rows):
            return pltpu.make_async_remote_copy(
                src_ref=rows(*block) if src is None else src, dst_ref=rows(*block),
                send_sem=send_sems.at[7 * a + k], recv_sem=recv_sems.at[7 * a + k],
                device_id=to, device_id_type=MESH)

        mine = pltpu.make_async_copy(ins[a], rows(*me), local_sems.at[a])
        mine.start()
        local.append(mine)
        f = [copy(0, me, sibling, src=ins[a])]
        f += [copy(1 + j, me, (*chip, c), src=ins[a]) for j, chip in enumerate(chips)]
        for cp in f:
            cp.start()
        first.append((f, copy))
    for a in range(na):
        f, copy = first[a]
        p = [copy(4 + j, (*chip, c), sibling) for j, chip in enumerate(chips)]
        for j, chip in enumerate(chips):
            copy(1 + j, (*chip, c), me).wait_recv()
            p[j].start()
        passed.append(p)
    for a in range(na):
        f, copy = first[a]
        copy(0, sibling, me).wait_recv()
        for j, chip in enumerate(chips):
            copy(4 + j, (*chip, 1 - c), me).wait_recv()
        for cp in f + passed[a]:
            cp.wait_send()
        local[a].wait()


def _allgather8(name, arrs):
    na = len(arrs)

    def body(*refs):
        _ag8_run(refs[:na], refs[na:2 * na], *refs[2 * na:])

    return pl.pallas_call(
        body, name=name,
        out_shape=[jax.ShapeDtypeStruct((8 * a.shape[0], a.shape[1]), a.dtype) for a in arrs],
        in_specs=[VMEM] * na, out_specs=[VMEM] * na,
        scratch_shapes=[pltpu.SemaphoreType.DMA((7 * na,)), pltpu.SemaphoreType.DMA((7 * na,)),
                        pltpu.SemaphoreType.DMA((na,))],
        compiler_params=pltpu.CompilerParams(vmem_limit_bytes=VMEM_LIMIT),
    )(*arrs)


AG_SEMS = 7


def _ag_copies(ins, outs, send_sems, recv_sems):
    x, y, c, chips = _position()
    sibling = (x, y, 1 - c)
    xn, yn, dg = [2 * chip[0] + chip[1] for chip in chips]
    to_x, to_y = (1 - x, y, c), (x, 1 - y, c)
    res = []
    for a in range(len(ins)):
        half = ins[a].shape[0] // 2
        quarter = half // 2

        def copy(k, dst, to, src=None, a=a):
            return pltpu.make_async_remote_copy(
                src_ref=dst if src is None else src, dst_ref=dst,
                send_sem=send_sems.at[AG_SEMS * a + k], recv_sem=recv_sems.at[AG_SEMS * a + k],
                device_id=to, device_id_type=MESH)

        def rows(chip, pc, q=None, a=a, half=half, quarter=quarter):
            if q is None:
                return outs[a].at[chip, pl.ds(pc * half, half), :]
            return outs[a].at[chip, pl.ds(pc * half + q * quarter, quarter), :]

        own = ins[a].at[pl.ds(c * half, half), :]
        mine = rows(2 * x + y, c)
        res.append(dict(
            sends=[copy(0, mine, to_x, src=own), copy(1, mine, to_y, src=own)],
            from_x=copy(0, rows(xn, c), to_x), from_y=copy(1, rows(yn, c), to_y),
            relay_y=copy(2, rows(xn, c, 0), to_y), relay_x=copy(3, rows(yn, c, 1), to_x),
            from_y_relay=copy(2, rows(dg, c, 0), to_y), from_x_relay=copy(3, rows(dg, c, 1), to_x),
            pass_on=[copy(4, rows(xn, c), sibling), copy(5, rows(yn, c), sibling), copy(6, rows(dg, c), sibling)],
            from_sibling=[copy(4, rows(xn, 1 - c), sibling), copy(5, rows(yn, 1 - c), sibling),
                          copy(6, rows(dg, 1 - c), sibling)]))
    return res


def _ag_start(ins, outs, send_sems, recv_sems):
    for cps in _ag_copies(ins, outs, send_sems, recv_sems):
        for cp in cps["sends"]:
            cp.start()


def _ag_relay(ins, outs, send_sems, recv_sems, which):
    copies = _ag_copies(ins, outs, send_sems, recv_sems)
    for a in which:
        cps = copies[a]
        cps["from_x"].wait_recv()
        cps["relay_y"].start()
        cps["pass_on"][0].start()
        cps["from_y"].wait_recv()
        cps["relay_x"].start()
        cps["pass_on"][1].start()


def _ag_complete(ins, outs, send_sems, recv_sems):
    copies = _ag_copies(ins, outs, send_sems, recv_sems)
    for cps in copies:
        cps["from_y_relay"].wait_recv()
        cps["from_x_relay"].wait_recv()
        cps["pass_on"][2].start()
    for cps in copies:
        for cp in cps["from_sibling"]:
            cp.wait_recv()
        for cp in cps["sends"] + [cps["relay_y"], cps["relay_x"]] + cps["pass_on"]:
            cp.wait_send()


def _ag_finish(ins, outs, send_sems, recv_sems):
    _ag_relay(ins, outs, send_sems, recv_sems, range(len(ins)))
    _ag_complete(ins, outs, send_sems, recv_sems)


def _allgather_weights(name, collective_id, shards):
    na = len(shards)
    hbm = pltpu.MemorySpace.HBM
    ins = [jax.new_ref(s, memory_space=hbm) for s in shards]
    outs = [jax.empty_ref(jax.ShapeDtypeStruct((N_CHIP,) + s.shape, s.dtype), memory_space=hbm) for s in shards]

    @pl.kernel(mesh=plsc.ScalarSubcoreMesh(axis_name="sequencer", num_cores=1), name=name,
               scratch_types=(pltpu.SemaphoreType.DMA((AG_SEMS * na,)), pltpu.SemaphoreType.DMA((AG_SEMS * na,))),
               compiler_params=pltpu.CompilerParams(collective_id=collective_id))
    def launch(send_sems, recv_sems):
        x, y, c, _ = _position()
        peers = [(1 - x, y, c), (x, 1 - y, c), (x, y, 1 - c)]
        barrier = pltpu.get_barrier_semaphore()
        for peer in peers:
            pl.semaphore_signal(barrier, inc=1, device_id=peer, device_id_type=MESH)
        pl.semaphore_wait(barrier, len(peers))
        _ag_start(ins, outs, send_sems, recv_sems)
        _ag_finish(ins, outs, send_sems, recv_sems)

    launch()
    return [o[...] for o in outs]


def _sibling_swap(name, arrs, split_rows, collective_id=None):
    na = len(arrs)
    shapes = [jax.ShapeDtypeStruct((a.shape[0], a.shape[1] // 2, a.shape[2]) if split_rows else a.shape, a.dtype)
              for a in arrs]

    def run(ins, outs, send_sems, recv_sems):
        x, y, c, _ = _position()
        cps = []
        for a in range(na):
            src = ins[a]
            if split_rows:
                half = src.shape[1] // 2
                src = src.at[:, pl.ds((1 - c) * half, half), :]
            cp = pltpu.make_async_remote_copy(
                src_ref=src, dst_ref=outs[a], send_sem=send_sems.at[a], recv_sem=recv_sems.at[a],
                device_id=(x, y, 1 - c), device_id_type=MESH)
            cp.start()
            cps.append(cp)
        for cp in cps:
            cp.wait()

    sems = (pltpu.SemaphoreType.DMA((na,)), pltpu.SemaphoreType.DMA((na,)))
    if collective_id is None:
        return pl.pallas_call(
            lambda *refs: run(refs[:na], refs[na:2 * na], *refs[2 * na:]), name=name, out_shape=shapes,
            in_specs=[ANY] * na, out_specs=[ANY] * na, scratch_shapes=list(sems))(*arrs)

    hbm = pltpu.MemorySpace.HBM
    ins = [jax.new_ref(a, memory_space=hbm) for a in arrs]
    outs = [jax.empty_ref(s, memory_space=hbm) for s in shapes]

    @pl.kernel(mesh=plsc.ScalarSubcoreMesh(axis_name="sequencer", num_cores=1), name=name, scratch_types=sems,
               compiler_params=pltpu.CompilerParams(collective_id=collective_id))
    def launch(send_sems, recv_sems):
        x, y, c, _ = _position()
        barrier = pltpu.get_barrier_semaphore()
        pl.semaphore_signal(barrier, inc=1, device_id=(x, y, 1 - c), device_id_type=MESH)
        pl.semaphore_wait(barrier, 1)
        run(ins, outs, send_sems, recv_sems)

    launch()
    return [o[...] for o in outs]


def _xchg_copies(ins, outs, send_sems, recv_sems):
    x, y, c, chips = _position()
    return [pltpu.make_async_remote_copy(
        src_ref=ins[a].at[2 * chip[0] + chip[1]], dst_ref=outs[a].at[j],
        send_sem=send_sems.at[3 * a + j], recv_sem=recv_sems.at[3 * a + j],
        device_id=(*chip, c), device_id_type=MESH) for a in range(len(ins)) for j, chip in enumerate(chips)]


def _exchange_chips(name, collective_id, parts):
    na = len(parts)
    hbm = pltpu.MemorySpace.HBM
    ins = [jax.new_ref(p, memory_space=hbm) for p in parts]
    outs = [jax.empty_ref(jax.ShapeDtypeStruct((3,) + p.shape[1:], p.dtype), memory_space=hbm) for p in parts]

    @pl.kernel(mesh=plsc.ScalarSubcoreMesh(axis_name="sequencer", num_cores=1), name=name,
               scratch_types=(pltpu.SemaphoreType.DMA((3 * na,)), pltpu.SemaphoreType.DMA((3 * na,))),
               compiler_params=pltpu.CompilerParams(collective_id=collective_id))
    def launch(send_sems, recv_sems):
        x, y, c, chips = _position()
        barrier = pltpu.get_barrier_semaphore()
        for chip in chips:
            pl.semaphore_signal(barrier, inc=1, device_id=(*chip, c), device_id_type=MESH)
        pl.semaphore_wait(barrier, len(chips))
        for cp in _xchg_copies(ins, outs, send_sems, recv_sems):
            cp.start()
        for cp in _xchg_copies(ins, outs, send_sems, recv_sems):
            cp.wait()

    launch()
    return [q[...] for q in outs]


def _add_sibling(name, grad, recv, core, after=()):
    _, r, c = grad.shape
    half = r // 2
    rb = min(half, 256)
    nrb = half // rb

    def body(core_ref, g_ref, r_ref, *refs):
        refs[-1][...] = (g_ref[...].astype(F32) + r_ref[...].astype(F32)).astype(BF16)

    return pl.pallas_call(
        body, name=name,
        grid_spec=pltpu.PrefetchScalarGridSpec(
            num_scalar_prefetch=1, grid=(N_CHIP, nrb),
            in_specs=[pl.BlockSpec((1, rb, c), lambda j, i, core_ref: (j, core_ref[0] * nrb + i, 0)),
                      pl.BlockSpec((1, rb, c), lambda j, i, core_ref: (j, i, 0))] + [ANY] * len(after),
            out_specs=pl.BlockSpec((1, rb, c), lambda j, i, core_ref: (j, i, 0))),
        out_shape=jax.ShapeDtypeStruct((N_CHIP, half, c), BF16),
        compiler_params=pltpu.CompilerParams(dimension_semantics=("arbitrary", "arbitrary"),
                                             vmem_limit_bytes=VMEM_LIMIT),
    )(core, grad, recv, *after)


def _add_halves(name, pairs):
    n = len(pairs)
    _, half, c = pairs[0][0].shape
    rb = min(half, 256)

    def body(*refs):
        for k in range(n):
            refs[2 * n + k][...] = (refs[2 * k][...].astype(F32) + refs[2 * k + 1][...].astype(F32)).astype(BF16)

    spec = pl.BlockSpec((1, rb, c), lambda j, i: (j, i, 0))
    return pl.pallas_call(
        body, name=name, grid=(N_CHIP, half // rb), in_specs=[spec] * (2 * n), out_specs=[spec] * n,
        out_shape=[jax.ShapeDtypeStruct((N_CHIP, half, c), BF16)] * n,
        compiler_params=pltpu.CompilerParams(dimension_semantics=("arbitrary", "arbitrary"),
                                             vmem_limit_bytes=VMEM_LIMIT),
    )(*[t for p in pairs for t in p])


def _add_chips(name, chip, p, q, after=()):
    _, half, c = q.shape
    rb = min(half, 256)

    def body(chip_ref, p_ref, q_ref, *refs):
        acc = p_ref[0].astype(F32)
        for j in range(3):
            acc = acc + q_ref[j].astype(F32)
        refs[-1][...] = acc

    return pl.pallas_call(
        body, name=name,
        grid_spec=pltpu.PrefetchScalarGridSpec(
            num_scalar_prefetch=1, grid=(half // rb,),
            in_specs=[pl.BlockSpec((1, rb, c), lambda i, chip_ref: (chip_ref[0], i, 0)),
                      pl.BlockSpec((3, rb, c), lambda i, chip_ref: (0, i, 0))] + [ANY] * len(after),
            out_specs=pl.BlockSpec((rb, c), lambda i, chip_ref: (i, 0))),
        out_shape=jax.ShapeDtypeStruct((half, c), F32),
        compiler_params=pltpu.CompilerParams(dimension_semantics=("arbitrary",), vmem_limit_bytes=VMEM_LIMIT),
    )(chip, p, q, *after)


def _small_update(gad, gam, gl, gg, mychip, params):
    names = ["ada_b", "norm1_g", "lru_conv_b", "gate_a_w", "gate_a_b", "gate_x_w", "gate_x_b", "a_param",
             "lru_conv_w", "short_conv_w", "lru_out_g", "conv_out_g", "norm2_g", "final_g"]
    flat = [t for n in names for t in params[n]]
    nin = len(flat)

    def body(chip_ref, gad_ref, gam_ref, gl_ref, gg_ref, *refs):
        ins = {n: refs[3 * k:3 * k + 3] for k, n in enumerate(names)}
        outs = {n: refs[nin + 4 * k:nin + 4 * k + 4] for k, n in enumerate(names)}
        loss_ref, dmod_ref = refs[nin + 4 * len(names):nin + 4 * len(names) + 2]

        def dsum(ref, lo, n):
            per = ref.shape[0] // 8
            acc = ref[lo:lo + n, :].astype(F32)
            for dev in range(1, 8):
                acc = acc + ref[dev * per + lo:dev * per + lo + n, :].astype(F32)
            return acc

        def update(n, g):
            w_ref, m_ref, v_ref = ins[n]
            g_ref, d_ref, mo_ref, vo_ref = outs[n]
            g_ref[...] = g
            d_ref[...], mo_ref[...], vo_ref[...] = _adam_math(w_ref[...], g, m_ref[...], v_ref[...])

        d, dm, l, lw = refs[-4:]
        d[...] = dsum(gad_ref, 0, 8)
        dm[...] = dsum(gam_ref, 0, 8)
        l[...] = dsum(gl_ref, 0, 16)
        for dev in range(8):
            for k in range(3):
                dmod_ref[dev:dev + 1, k * D_MODEL:(k + 1) * D_MODEL] = gad_ref[dev * 8 + k:dev * 8 + k + 1, :]
                dmod_ref[dev:dev + 1, (3 + k) * D_MODEL:(4 + k) * D_MODEL] = gam_ref[dev * 8 + k:dev * 8 + k + 1, :]
        w_ref, m_ref, v_ref = ins["ada_b"]
        g_ref, d_ref, mo_ref, vo_ref = outs["ada_b"]
        for k in range(3):
            g_ref[:, k * D_MODEL:(k + 1) * D_MODEL] = d[k:k + 1, :]
            g_ref[:, (3 + k) * D_MODEL:(4 + k) * D_MODEL] = dm[k:k + 1, :]
        d_ref[...], mo_ref[...], vo_ref[...] = _adam_math(w_ref[...], g_ref[...], m_ref[...], v_ref[...])
        update("norm1_g", d[3:4, :])
        update("norm2_g", dm[3:4, :])
        update("final_g", dm[4:5, :])
        update("gate_a_b", d[4:5, 0:D_LRU])
        update("gate_x_b", d[4:5, D_LRU:2 * D_LRU])
        update("lru_conv_b", l[4:5, :])
        update("a_param", l[8:9, :] * jax.nn.sigmoid(ins["a_param"][0][...]))
        update("lru_out_g", l[9:10, :])
        update("conv_out_g", l[10:11, :])
        loss_ref[...] = jnp.broadcast_to(dm[5:6, 0:128], (8, 128))
        chip = chip_ref[0]
        acc = jnp.zeros((8, 128), F32)
        for j in range(N_CHIP):
            acc = acc + jnp.where(chip == j, l[0:8, j * 128:(j + 1) * 128], 0.0)
        lw[...] = acc
        update("lru_conv_w", lw[0:4, :])
        update("short_conv_w", lw[5:8, :])
        gates = dsum(gg_ref, 0, D_LRU)
        update("gate_a_w", gates[:, 0:HEAD])
        update("gate_x_w", gates[:, HEAD:2 * HEAD])

    out_shape = []
    for n in names:
        out_shape += [jax.ShapeDtypeStruct(params[n][0].shape, F32)] * 4
    out_shape += [jax.ShapeDtypeStruct((8, 128), F32), jax.ShapeDtypeStruct((8, 6 * D_MODEL), F32)]
    res = pl.pallas_call(
        body, name="small_update", out_shape=out_shape,
        in_specs=[SMEM] + [VMEM] * (4 + nin),
        out_specs=[VMEM] * len(out_shape),
        scratch_shapes=[pltpu.VMEM((8, D_MODEL), F32), pltpu.VMEM((8, D_MODEL), F32), pltpu.VMEM((16, D_LRU), F32),
                        pltpu.VMEM((8, 128), F32)],
        compiler_params=pltpu.CompilerParams(vmem_limit_bytes=VMEM_LIMIT),
    )(mychip, gad, gam, gl, gg, *flat)
    per = {n: res[4 * k:4 * k + 4] for k, n in enumerate(names)}
    return per, res[-2], res[-1]


def _block_diag(w):
    eye = jnp.eye(8, dtype=w.dtype)
    return (eye[:, None, :, None] * w[:, :, None, :]).reshape(8 * HEAD, 8 * HEAD)


def _diag_blocks(g):
    return jnp.concatenate([g[h * HEAD:(h + 1) * HEAD, h * HEAD:(h + 1) * HEAD] for h in range(8)], axis=0)


def kernel(x, c, ada_w, ada_b, norm1_g, w_in, lru_conv_w, lru_conv_b, gate_a_w, gate_a_b, gate_x_w, gate_x_b, a_param, short_conv_w, lru_out_g, conv_out_g, w_out, norm2_g, w_mlp1, w_mlp2, final_g, loss_target, m_ada_w, m_ada_b, m_norm1_g, m_w_in, m_lru_conv_w, m_lru_conv_b, m_gate_a_w, m_gate_a_b, m_gate_x_w, m_gate_x_b, m_a_param, m_short_conv_w, m_lru_out_g, m_conv_out_g, m_w_out, m_norm2_g, m_w_mlp1, m_w_mlp2, m_final_g, v_ada_w, v_ada_b, v_norm1_g, v_w_in, v_lru_conv_w, v_lru_conv_b, v_gate_a_w, v_gate_a_b, v_gate_x_w, v_gate_x_b, v_a_param, v_short_conv_w, v_lru_out_g, v_conv_out_g, v_w_out, v_norm2_g, v_w_mlp1, v_w_mlp2, v_final_g):
    xi, yi, ci = lax.axis_index("x"), lax.axis_index("y"), lax.axis_index("c")
    mychip = 2 * xi + yi
    me = 4 * xi + 2 * yi + ci

    own_in, own_out = w_in[0].astype(BF16), w_out[0].astype(BF16)
    win_all, wout_all = _allgather_weights("allgather_mixer_weights", 1, [own_in, own_out])
    own_w1, own_w2 = w_mlp1[0].astype(BF16), w_mlp2[0].astype(BF16)
    w1_all, w2_all = _allgather_weights("allgather_mlp_weights", 2, [own_w1, own_w2])

    c_blk = jnp.zeros((8, D_MODEL), F32).at[0:1].set(c)
    cw_blk = jnp.zeros((8, 128), F32).at[0:4].set(lru_conv_w[0]).at[4:7].set(short_conv_w[0])
    c_g, cw_g = _allgather8("allgather_cond", [c_blk, cw_blk])
    c_all = c_g.reshape(8, 8, D_MODEL)[:, 0]
    cw_g = cw_g.reshape(4, 2, 8, 128)[:, 0]
    lcw = cw_g[:, 0:4].transpose(1, 0, 2).reshape(4, D_LRU)
    scw = cw_g[:, 4:7].transpose(1, 0, 2).reshape(3, D_LRU)

    mod_loc = _mod_matmul(c_all, ada_w[0])
    (mod_g,) = _allgather8("allgather_mod", [mod_loc])
    mod_all = mod_g.reshape(4, 2, 8, 6 * D_MODEL // 4)[:, 0].transpose(1, 0, 2).reshape(8, 6 * D_MODEL) + ada_b
    mod_pad = jnp.pad(mod_all.reshape(8, 6, D_MODEL), ((0, 0), (0, 2), (0, 0)))
    mod = lax.dynamic_slice_in_dim(mod_pad, me, 1, axis=0).reshape(8, D_MODEL)

    win, wout = (win_all, own_in), (wout_all, own_out)
    chip = mychip.reshape(1).astype(jnp.int32)
    core = ci.reshape(1).astype(jnp.int32)

    vecd = jnp.concatenate([norm1_g, norm2_g, final_g[None, :], jnp.concatenate([gate_a_b, gate_x_b], axis=1),
                            jnp.zeros((4, D_MODEL), F32)], axis=0)
    vecl = jnp.concatenate([lcw, lru_conv_b, scw, a_param, lru_out_g, conv_out_g, jnp.zeros((5, D_LRU), F32)], axis=0)
    gab = jnp.concatenate([_block_diag(gate_a_w[0]), _block_diag(gate_x_w[0])], axis=1).astype(BF16)
    a64 = _block_diag(jnp.full((8, HEAD, HEAD), 1.0 / HEAD, F32)).astype(BF16)

    hb, proj, hl, ycat, mixed, x1 = _mix_fwd(chip, x[0], mod, vecd, vecl, win, wout, gab, a64)
    dx1, act, dz, dmo, h2b, accm = _mlp_fwd_bwd(
        chip, x1, loss_target[0], mod, vecd, (w1_all, own_w1), (w2_all, own_w2))

    parts_mlp = list(_add_halves("rs_add_sibling_mlp", [_wgrad_split("wgrad_mlp1", 6, h2b, dz, 0, FF_BLK),
                                                        _wgrad_split("wgrad_mlp2", 7, act, dmo, FF_BLK, 0)]))
    q_w1, q_w2 = _exchange_chips("rs_exchange_mlp", 0, parts_mlp)
    grad_x, accd, accl, g_win, g_wout, g_gate = _mix_bwd(
        chip, dx1, x[0], mixed, proj, hl, hb, ycat, mod, vecd, vecl, win, wout, gab, a64)

    g_mix = [g_win, g_wout.reshape(N_CHIP, WOUT_BLK, D_MODEL)]
    recv_mix = _sibling_swap("rs_swap_halves_mix", g_mix, True, collective_id=4)
    own_mlp = [_add_chips("rs_add_chips_mlp%d" % k, chip, p, q) for k, (p, q) in enumerate(zip(parts_mlp, (q_w1, q_w2)))]
    sib_mlp = _sibling_swap("rs_swap_reduced_mlp", own_mlp, False, collective_id=5)
    gg_blk = jnp.concatenate([_diag_blocks(g_gate[:, 0:D_LRU]), _diag_blocks(g_gate[:, D_LRU:2 * D_LRU])], axis=1)
    gad, gam, gl, gg = _allgather8("allgather_small_grads", [accd, accm, accl, gg_blk.astype(BF16)])

    parts_mix = [_add_sibling("rs_add_sibling_mix%d" % k, g, r, core, after=[gad])
                 for k, (g, r) in enumerate(zip(g_mix, recv_mix))]
    landed_mix = _exchange_chips("rs_exchange_mix", 3, parts_mix)
    res_w1, res_w2 = _adam("adam_mlp", core, [(w_mlp1[0], own_mlp[0], sib_mlp[0], m_w_mlp1[0], v_w_mlp1[0]),
                                              (w_mlp2[0], own_mlp[1], sib_mlp[1], m_w_mlp2[0], v_w_mlp2[0])])
    own_mix = [_add_chips("rs_add_chips_mix%d" % k, chip, p, q, after=[res_w1[1]])
               for k, (p, q) in enumerate(zip(parts_mix, landed_mix))]
    sib_mix = _sibling_swap("rs_swap_reduced_mix", own_mix, False)
    (res_win,) = _adam("adam_w_in", core, [(w_in[0], own_mix[0], sib_mix[0], m_w_in[0], v_w_in[0])])
    (res_wout,) = _adam("adam_w_out", core, [(w_out[0], own_mix[1], sib_mix[1], m_w_out[0], v_w_out[0])])

    params = {
        "ada_b": (ada_b, m_ada_b, v_ada_b), "norm1_g": (norm1_g, m_norm1_g, v_norm1_g),
        "lru_conv_b": (lru_conv_b, m_lru_conv_b, v_lru_conv_b),
        "gate_a_w": tuple(t.reshape(D_LRU, HEAD) for t in (gate_a_w, m_gate_a_w, v_gate_a_w)),
        "gate_a_b": (gate_a_b, m_gate_a_b, v_gate_a_b),
        "gate_x_w": tuple(t.reshape(D_LRU, HEAD) for t in (gate_x_w, m_gate_x_w, v_gate_x_w)),
        "gate_x_b": (gate_x_b, m_gate_x_b, v_gate_x_b), "a_param": (a_param, m_a_param, v_a_param),
        "lru_conv_w": tuple(t[0] for t in (lru_conv_w, m_lru_conv_w, v_lru_conv_w)),
        "short_conv_w": tuple(t[0] for t in (short_conv_w, m_short_conv_w, v_short_conv_w)),
        "lru_out_g": (lru_out_g, m_lru_out_g, v_lru_out_g), "conv_out_g": (conv_out_g, m_conv_out_g, v_conv_out_g),
        "norm2_g": (norm2_g, m_norm2_g, v_norm2_g),
        "final_g": tuple(t[None, :] for t in (final_g, m_final_g, v_final_g)),
    }
    small, loss_blk, dmod_cols = _small_update(gad, gam, gl, gg, chip, params)
    loss = loss_blk[0, 0]

    ncol = 6 * D_MODEL // N_CHIP
    dmod_loc = lax.dynamic_slice_in_dim(dmod_cols, mychip * ncol, ncol, axis=1)
    sct = (c_all * jax.nn.sigmoid(c_all)).T
    ada = _ada_grad_adam(sct, dmod_loc, ada_w[0], m_ada_w[0], v_ada_w[0])

    res = {"ada_w": ada, "w_in": res_win, "w_out": res_wout, "w_mlp1": res_w1, "w_mlp2": res_w2}
    res = {n: tuple(t[None] for t in r) for n, r in res.items()}
    shapes = {"gate_a_w": gate_a_w.shape, "gate_x_w": gate_x_w.shape, "lru_conv_w": lru_conv_w.shape,
              "short_conv_w": short_conv_w.shape, "final_g": final_g.shape}
    for n, t in small.items():
        res[n] = tuple(u.reshape(shapes[n]) if n in shapes else u for u in t)

    order = ["ada_w", "ada_b", "norm1_g", "w_in", "lru_conv_w", "lru_conv_b", "gate_a_w", "gate_a_b", "gate_x_w",
             "gate_x_b", "a_param", "short_conv_w", "lru_out_g", "conv_out_g", "w_out", "norm2_g", "w_mlp1",
             "w_mlp2", "final_g"]
    return (loss, grad_x[None], *[res[n][0] for n in order], *[res[n][1] for n in order],
            *[res[n][2] for n in order], *[res[n][3] for n in order])
```

```python
import jax
import jax.numpy as jnp
from jax import lax
from jax.experimental import pallas as pl
from jax.experimental.pallas import tpu as pltpu
from jax.experimental.pallas import tpu_sc as plsc

F32 = jnp.float32
BF16 = jnp.bfloat16

D_MODEL = 1024
D_LRU = 512
D_IN = 2560
D_FF = 4096
N_CHIP = 4
WIN_BLK = D_IN // N_CHIP
WOUT_BLK = D_MODEL // N_CHIP
FF_BLK = D_FF // N_CHIP
HEAD = 64
EPS = 1e-6
C_GATE = 8.0
TOKEN_TILE = 256
HALO = 8
VMEM_LIMIT = 60 * 1024 * 1024

ADAM_LR = 0.001
ADAM_B1 = 0.9
ADAM_B2 = 0.999
ADAM_EPS = 1e-08
ADAM_WD = 0.01
ADAM_STEP = 10

MESH = pl.DeviceIdType.MESH
ANY = pl.BlockSpec(memory_space=pl.ANY)
VMEM = pl.BlockSpec(memory_space=pltpu.VMEM)
SMEM = pl.BlockSpec(memory_space=pltpu.SMEM)


def _full(shape, single=False):
    nd = len(shape)
    if single:
        return pl.BlockSpec(shape, lambda *_: (0,) * nd, pipeline_mode=pl.Buffered(1))
    return pl.BlockSpec(shape, lambda *_: (0,) * nd)


def _dot(a, b):
    return jnp.dot(a, b, preferred_element_type=F32)


def _dot_nt(a, b):
    return lax.dot_general(a, b, (((1,), (1,)), ((), ())), preferred_element_type=F32)


def _dot_tn(a, b):
    return lax.dot_general(a, b, (((0,), (0,)), ((), ())), preferred_element_type=F32)


def _gmean(v, a64):
    hi = v.astype(BF16)
    lo = (v - hi.astype(F32)).astype(BF16)
    return _dot(hi, a64) + _dot(lo, a64)


def _gelu(x):
    u = 0.7978845608028654 * (x + 0.044715 * x * x * x)
    t = jnp.tanh(u)
    return 0.5 * x * (1.0 + t), t


def _gelu_grad(x, t):
    du = 0.7978845608028654 * (1.0 + 3.0 * 0.044715 * x * x)
    return 0.5 * (1.0 + t) + 0.5 * x * (1.0 - t * t) * du


def _log1p_pos(y):
    return jnp.where(y < 1e-2, y * (1.0 - y * (0.5 - y * (1.0 / 3.0 - y * 0.25))), jnp.log(1.0 + y))


def _softplus(a):
    return jnp.maximum(a, 0.0) + _log1p_pos(jnp.exp(-jnp.abs(a)))


def _neg_expm1(z):
    series = -z * (1.0 + z * (0.5 + z * (1.0 / 6.0 + z * (1.0 / 24.0))))
    return jnp.where(z > -0.02, series, 1.0 - jnp.exp(z))


def _scan_fwd(a, b, row):
    n = a.shape[0]
    d = 1
    while d < n:
        m = row >= d
        b = jnp.where(m, a * pltpu.roll(b, d, 0) + b, b)
        a = jnp.where(m, a * pltpu.roll(a, d, 0), a)
        d *= 2
    return a, b


def _scan_rev(a, b, row):
    n = a.shape[0]
    d = 1
    while d < n:
        m = row < n - d
        b = jnp.where(m, b + a * pltpu.roll(b, n - d, 0), b)
        a = jnp.where(m, a * pltpu.roll(a, n - d, 0), a)
        d *= 2
    return a, b


def _colsum(v):
    return jnp.sum(v, axis=0, keepdims=True)


def _load_gathered(chip, gathered, own, slot, sems):
    copies = []
    for j in range(N_CHIP):
        @pl.when(chip == j)
        def _(j=j):
            pltpu.make_async_copy(own, slot(j), sems.at[j]).start()

        @pl.when(chip != j)
        def _(j=j):
            pltpu.make_async_copy(gathered.at[j], slot(j), sems.at[j]).start()

        copies.append(pltpu.make_async_copy(own, slot(j), sems.at[j]))
    return copies


def _lru_gates(xlb, gab, gbias, sp, first_row):
    g = _dot(xlb, gab) + gbias
    r = jax.nn.sigmoid(g[:, :D_LRU])
    ig = jax.nn.sigmoid(g[:, D_LRU:])
    la = (-C_GATE) * r * sp
    a = jnp.exp(la)
    msq = jnp.sqrt(_neg_expm1(2.0 * la))
    mult = jnp.where(first_row, 1.0, msq)
    return r, ig, a, msq, mult


def _mix_fwd(chip, x, mod, vecd, vecl, win, wout, gab, a64):
    s = x.shape[0]
    ts = TOKEN_TILE
    nt = s // ts

    def body(chip_ref, x_ref, mod_ref, vd_ref, vl_ref, win_hbm, win_own, wout_hbm, wout_own, gab_ref, a64_ref,
             hb_ref, proj_ref, hl_ref, ycat_ref, mixed_ref, x1_ref,
             win_ref, wout_ref, ext_lx, ext_cv, hcar, sems):
        i = pl.program_id(0)

        @pl.when(i == 0)
        def _():
            cps = _load_gathered(chip_ref[0], win_hbm, win_own, lambda j: win_ref.at[j], sems.at[pl.ds(0, N_CHIP)])
            cps += _load_gathered(chip_ref[0], wout_hbm, wout_own,
                                  lambda j: wout_ref.at[pl.ds(j * WOUT_BLK, WOUT_BLK), :],
                                  sems.at[pl.ds(N_CHIP, N_CHIP)])
            ext_lx[0:HALO, :] = jnp.zeros((HALO, D_LRU), F32)
            ext_cv[0:HALO, :] = jnp.zeros((HALO, D_LRU), F32)
            hcar[...] = jnp.zeros_like(hcar)
            for cp in cps:
                cp.wait()

        row = lax.broadcasted_iota(jnp.int32, (ts, D_LRU), 0)
        first_row = jnp.logical_and(row == 0, i == 0)
        xt = x_ref[...]
        shift1, scale1, gate1 = mod_ref[0:1, :], mod_ref[1:2, :], mod_ref[2:3, :]
        r1 = lax.rsqrt(jnp.mean(xt * xt, axis=-1, keepdims=True) + EPS)
        h = (xt * r1) * vd_ref[0:1, :] * (1.0 + scale1) + shift1
        hb = h.astype(BF16)
        hb_ref[...] = hb
        for j in range(N_CHIP):
            proj_ref[:, j * WIN_BLK:(j + 1) * WIN_BLK] = _dot(hb, win_ref[j])
        u_ly = proj_ref[:, 512:1024]
        u_b = proj_ref[:, 1024:1536]

        ext_lx[HALO:HALO + ts, :] = proj_ref[:, 0:512]
        xl = vl_ref[4:5, :] + vl_ref[0:1, :] * ext_lx[pl.ds(5, ts), :]
        for k in range(1, 4):
            xl = xl + vl_ref[k:k + 1, :] * ext_lx[pl.ds(5 + k, ts), :]
        ext_lx[0:HALO, :] = ext_lx[ts:ts + HALO, :]
        sp = _softplus(vl_ref[8:9, :])
        _, ig, a, _, mult = _lru_gates(xl.astype(BF16), gab_ref[...], vd_ref[3:4, :], sp, first_row)
        acum, hloc = _scan_fwd(a, mult * (ig * xl), row)
        hl = hloc + acum * hcar[0:1, :]
        hl_ref[...] = hl
        hcar[0:1, :] = hl_ref[ts - 1:ts, :]
        ge, _ = _gelu(u_ly)
        p = ge * hl
        y_lru = p * lax.rsqrt(_gmean(p * p, a64_ref[...]) + EPS) * vl_ref[9:10, :]
        ycat_ref[:, 0:512] = y_lru.astype(BF16)

        ext_cv[HALO:HALO + ts, :] = proj_ref[:, 1536:2048] * proj_ref[:, 2048:2560]
        q = vl_ref[5:6, :] * ext_cv[pl.ds(6, ts), :]
        for k in range(1, 3):
            q = q + vl_ref[5 + k:6 + k, :] * ext_cv[pl.ds(6 + k, ts), :]
        ext_cv[0:HALO, :] = ext_cv[ts:ts + HALO, :]
        yc = u_b * q
        y_conv = yc * lax.rsqrt(_gmean(yc * yc, a64_ref[...]) + EPS) * vl_ref[10:11, :]
        ycat_ref[:, 512:1024] = y_conv.astype(BF16)

        mixed = _dot(ycat_ref[...], wout_ref[...])
        mixed_ref[...] = mixed
        x1_ref[...] = xt + gate1 * mixed

    tile = lambda w: pl.BlockSpec((ts, w), lambda i: (i, 0))
    return pl.pallas_call(
        body, name="mix_fwd", grid=(nt,),
        in_specs=[SMEM, tile(D_MODEL), _full((8, D_MODEL)), _full((8, D_MODEL)), _full((16, D_LRU)),
                  ANY, ANY, ANY, ANY, _full((D_LRU, 2 * D_LRU), True), _full((D_LRU, D_LRU), True)],
        out_specs=[tile(D_MODEL), tile(D_IN), tile(D_LRU), tile(D_MODEL), tile(D_MODEL), tile(D_MODEL)],
        out_shape=[jax.ShapeDtypeStruct((s, D_MODEL), BF16), jax.ShapeDtypeStruct((s, D_IN), F32),
                   jax.ShapeDtypeStruct((s, D_LRU), F32), jax.ShapeDtypeStruct((s, D_MODEL), BF16),
                   jax.ShapeDtypeStruct((s, D_MODEL), F32), jax.ShapeDtypeStruct((s, D_MODEL), F32)],
        scratch_shapes=[pltpu.VMEM((N_CHIP, D_MODEL, WIN_BLK), BF16), pltpu.VMEM((D_MODEL, D_MODEL), BF16),
                        pltpu.VMEM((ts + HALO, D_LRU), F32), pltpu.VMEM((ts + HALO, D_LRU), F32),
                        pltpu.VMEM((HALO, D_LRU), F32), pltpu.SemaphoreType.DMA((2 * N_CHIP,))],
        compiler_params=pltpu.CompilerParams(dimension_semantics=("arbitrary",), vmem_limit_bytes=VMEM_LIMIT),
    )(chip, x, mod, vecd, vecl, *win, *wout, gab, a64)


def _mlp_fwd_bwd(chip, x1, target, mod, vecd, w1, w2):
    s = x1.shape[0]
    ts = TOKEN_TILE
    nt = s // ts

    def body(chip_ref, x1_ref, tg_ref, mod_ref, vd_ref, w1_hbm, w1_own, w2_hbm, w2_own,
             dx1_ref, act_ref, dz_ref, dmo_ref, h2_ref, acc_ref, w1_v, w2_v, rz_v, sems):
        i = pl.program_id(0)

        @pl.when(i == 0)
        def _():
            cps = _load_gathered(chip_ref[0], w1_hbm, w1_own, lambda j: w1_v.at[j], sems.at[pl.ds(0, N_CHIP)])
            cps += _load_gathered(chip_ref[0], w2_hbm, w2_own, lambda j: w2_v.at[j], sems.at[pl.ds(N_CHIP, N_CHIP)])
            acc_ref[...] = jnp.zeros_like(acc_ref)
            for cp in cps:
                cp.wait()

        xt = x1_ref[...]
        shift2, scale2, gate2 = mod_ref[3:4, :], mod_ref[4:5, :], mod_ref[5:6, :]
        g2, gf = vd_ref[1:2, :], vd_ref[2:3, :]
        r2 = lax.rsqrt(jnp.mean(xt * xt, axis=-1, keepdims=True) + EPS)
        n2 = xt * r2
        h2b = (n2 * g2 * (1.0 + scale2) + shift2).astype(BF16)
        h2_ref[...] = h2b
        for j in range(N_CHIP):
            rz_v[j] = jnp.maximum(_dot(h2b, w1_v[j]), 0.0)
        mo = jnp.zeros((ts, D_MODEL), F32)
        for j in range(N_CHIP):
            rz = rz_v[j]
            actb = (rz * rz).astype(BF16)
            act_ref[:, j * FF_BLK:(j + 1) * FF_BLK] = actb
            mo = mo + _dot(actb, w2_v[j])
        x2 = xt + gate2 * mo
        r3 = lax.rsqrt(jnp.mean(x2 * x2, axis=-1, keepdims=True) + EPS)
        n3 = x2 * r3
        e = n3 * gf - tg_ref[...]
        loss = (0.5 / D_MODEL) * jnp.sum(_colsum(e * e), axis=1, keepdims=True)
        dy = e * (1.0 / D_MODEL)
        acc_ref[4:5, :] += _colsum(dy * n3)
        acc_ref[5:6, :] += jnp.broadcast_to(loss, (1, D_MODEL))
        dn3 = dy * gf
        dx2 = r3 * (dn3 - n3 * jnp.mean(dn3 * n3, axis=-1, keepdims=True))
        acc_ref[2:3, :] += _colsum(dx2 * mo)
        dmob = (dx2 * gate2).astype(BF16)
        dmo_ref[...] = dmob
        for j in range(N_CHIP):
            dz_ref[:, j * FF_BLK:(j + 1) * FF_BLK] = (_dot_nt(dmob, w2_v[j]) * (2.0 * rz_v[j])).astype(BF16)
        dh2 = jnp.zeros((ts, D_MODEL), F32)
        for j in range(N_CHIP):
            dh2 = dh2 + _dot_nt(dz_ref[:, j * FF_BLK:(j + 1) * FF_BLK], w1_v[j])
        acc_ref[1:2, :] += _colsum(dh2 * (n2 * g2))
        acc_ref[0:1, :] += _colsum(dh2)
        dhn2 = dh2 * (1.0 + scale2)
        acc_ref[3:4, :] += _colsum(dhn2 * n2)
        dn2 = dhn2 * g2
        dx1_ref[...] = dx2 + r2 * (dn2 - n2 * jnp.mean(dn2 * n2, axis=-1, keepdims=True))

    tile = lambda w: pl.BlockSpec((ts, w), lambda i: (i, 0))
    return pl.pallas_call(
        body, name="mlp_fwd_bwd", grid=(nt,),
        in_specs=[SMEM, tile(D_MODEL), tile(D_MODEL), _full((8, D_MODEL)), _full((8, D_MODEL)), ANY, ANY, ANY, ANY],
        out_specs=[tile(D_MODEL), tile(D_FF), tile(D_FF), tile(D_MODEL), tile(D_MODEL), _full((8, D_MODEL))],
        out_shape=[jax.ShapeDtypeStruct((s, D_MODEL), F32), jax.ShapeDtypeStruct((s, D_FF), BF16),
                   jax.ShapeDtypeStruct((s, D_FF), BF16), jax.ShapeDtypeStruct((s, D_MODEL), BF16),
                   jax.ShapeDtypeStruct((s, D_MODEL), BF16), jax.ShapeDtypeStruct((8, D_MODEL), F32)],
        scratch_shapes=[pltpu.VMEM((N_CHIP, D_MODEL, FF_BLK), BF16), pltpu.VMEM((N_CHIP, FF_BLK, D_MODEL), BF16),
                        pltpu.VMEM((N_CHIP, ts, FF_BLK), F32), pltpu.SemaphoreType.DMA((2 * N_CHIP,))],
        compiler_params=pltpu.CompilerParams(dimension_semantics=("arbitrary",), vmem_limit_bytes=VMEM_LIMIT),
    )(chip, x1, target, mod, vecd, *w1, *w2)


def _mix_bwd(chip, dx1, x, mixed, proj, hl, hb, ycat, mod, vecd, vecl, win, wout, gab, a64):
    s = x.shape[0]
    ts = TOKEN_TILE
    nt = s // ts
    hpt = ts // HALO

    def body(chip_ref, dx1_ref, x_ref, mixed_ref, proj_ref, projh_ref, hl_ref, hlh_ref, hb_ref, ycat_ref,
             mod_ref, vd_ref, vl_ref, win_hbm, win_own, wout_hbm, wout_own, gab_ref, a64_ref,
             gx_ref, accd_ref, accl_ref, gwin_hbm, gwout_hbm, ggate_hbm,
             win_ref, wout_ref, dproj_ref, dgb_ref, gwin_acc, gwout_acc, ggate_acc,
             ext_lx, ext_cv, ext_hl, ext_dxl, ext_dq, gbuf, gcar, acar, sems):
        i = pl.program_id(0)
        ri = nt - 1 - i

        @pl.when(i == 0)
        def _():
            gwin_acc[...] = jnp.zeros_like(gwin_acc)
            gwout_acc[...] = jnp.zeros_like(gwout_acc)
            ggate_acc[...] = jnp.zeros_like(ggate_acc)
            cps = _load_gathered(chip_ref[0], win_hbm, win_own, lambda j: win_ref.at[j], sems.at[pl.ds(0, N_CHIP)])
            cps += _load_gathered(chip_ref[0], wout_hbm, wout_own,
                                  lambda j: wout_ref.at[pl.ds(j * WOUT_BLK, WOUT_BLK), :],
                                  sems.at[pl.ds(N_CHIP, N_CHIP)])
            for cp in cps:
                cp.wait()
            accd_ref[...] = jnp.zeros_like(accd_ref)
            accl_ref[...] = jnp.zeros_like(accl_ref)
            ext_dxl[ts:ts + HALO, :] = jnp.zeros((HALO, D_LRU), F32)
            ext_dq[ts:ts + HALO, :] = jnp.zeros((HALO, D_LRU), F32)
            gcar[...] = jnp.zeros_like(gcar)
            acar[...] = jnp.zeros_like(acar)

        row = lax.broadcasted_iota(jnp.int32, (ts, D_LRU), 0)
        first_row = jnp.logical_and(row == 0, ri == 0)
        halo_on = jnp.where(ri == 0, 0.0, 1.0)
        shift1, scale1, gate1 = mod_ref[0:1, :], mod_ref[1:2, :], mod_ref[2:3, :]
        g1 = vd_ref[0:1, :]
        a64m = a64_ref[...]
        lg, cg = vl_ref[9:10, :], vl_ref[10:11, :]

        dx1 = dx1_ref[...]
        accd_ref[2:3, :] += _colsum(dx1 * mixed_ref[...])
        dmb = (dx1 * gate1).astype(BF16)
        gwout_acc[...] += _dot_tn(ycat_ref[...], dmb)
        dycat = _dot_nt(dmb, wout_ref[...])
        dyl = dycat[:, 0:512]
        dyv = dycat[:, 512:1024]

        u_ly = proj_ref[:, 512:1024]
        u_b = proj_ref[:, 1024:1536]
        u_c = proj_ref[:, 1536:2048]
        u_v = proj_ref[:, 2048:2560]
        ext_lx[0:HALO, :] = projh_ref[:, 0:512] * halo_on
        ext_lx[HALO:HALO + ts, :] = proj_ref[:, 0:512]
        xl = vl_ref[4:5, :] + vl_ref[0:1, :] * ext_lx[pl.ds(5, ts), :]
        for k in range(1, 4):
            xl = xl + vl_ref[k:k + 1, :] * ext_lx[pl.ds(5 + k, ts), :]
        xlb = xl.astype(BF16)
        sp = _softplus(vl_ref[8:9, :])
        r, ig, a, msq, mult = _lru_gates(xlb, gab_ref[...], vd_ref[3:4, :], sp, first_row)
        hl = hl_ref[...]
        ge, th = _gelu(u_ly)
        p = ge * hl
        rl = lax.rsqrt(_gmean(p * p, a64m) + EPS)
        nl = p * rl
        ext_cv[0:HALO, :] = projh_ref[:, 1536:2048] * projh_ref[:, 2048:2560] * halo_on
        ext_cv[HALO:HALO + ts, :] = u_c * u_v
        q = vl_ref[5:6, :] * ext_cv[pl.ds(6, ts), :]
        for k in range(1, 3):
            q = q + vl_ref[5 + k:6 + k, :] * ext_cv[pl.ds(6 + k, ts), :]
        yc = u_b * q
        rc = lax.rsqrt(_gmean(yc * yc, a64m) + EPS)
        nc = yc * rc

        accl_ref[9:10, :] += _colsum(dyl * nl)
        dnl = dyl * lg
        dp = rl * (dnl - nl * _gmean(dnl * nl, a64m))
        dproj_ref[:, 512:1024] = ((dp * hl) * _gelu_grad(u_ly, th)).astype(BF16)
        a_next = jnp.where(row == ts - 1, acar[0:1, :], pltpu.roll(a, ts - 1, 0))
        acum, gloc = _scan_rev(a_next, dp * ge, row)
        gbuf[...] = gloc + acum * gcar[0:1, :]
        gcar[0:1, :] = gbuf[0:1, :]
        ext_hl[0:HALO, :] = hlh_ref[...] * halo_on
        ext_hl[HALO:HALO + ts, :] = hl
        acar[...] = a[0:HALO, :]
        gt = gbuf[...]
        da = gt * ext_hl[pl.ds(HALO - 1, ts), :]
        dmult = gt * ig * xl
        di = gt * mult * xl
        dxl = gt * mult * ig
        dla = da * a - jnp.where(first_row, 0.0, dmult * a * a / msq)
        accl_ref[8:9, :] += _colsum(dla * ((-C_GATE) * r))
        dra = dla * ((-C_GATE) * sp) * r * (1.0 - r)
        dia = di * ig * (1.0 - ig)
        accd_ref[4:5, 0:D_LRU] += _colsum(dra)
        accd_ref[4:5, D_LRU:2 * D_LRU] += _colsum(dia)
        dgb_ref[:, 0:D_LRU] = dra.astype(BF16)
        dgb_ref[:, D_LRU:2 * D_LRU] = dia.astype(BF16)
        dxl = dxl + _dot_nt(dgb_ref[...], gab_ref[...])
        ggate_acc[...] += _dot_tn(xlb, dgb_ref[...])
        accl_ref[4:5, :] += _colsum(dxl)
        for k in range(4):
            accl_ref[k:k + 1, :] += _colsum(dxl * ext_lx[pl.ds(5 + k, ts), :])
        ext_dxl[0:ts, :] = dxl
        du_lx = vl_ref[0:1, :] * ext_dxl[pl.ds(3, ts), :]
        for k in range(1, 4):
            du_lx = du_lx + vl_ref[k:k + 1, :] * ext_dxl[pl.ds(3 - k, ts), :]
        ext_dxl[ts:ts + HALO, :] = ext_dxl[0:HALO, :]
        dproj_ref[:, 0:512] = du_lx.astype(BF16)

        accl_ref[10:11, :] += _colsum(dyv * nc)
        dnc = dyv * cg
        dyc = rc * (dnc - nc * _gmean(dnc * nc, a64m))
        dproj_ref[:, 1024:1536] = (dyc * q).astype(BF16)
        dq = dyc * u_b
        for k in range(3):
            accl_ref[5 + k:6 + k, :] += _colsum(dq * ext_cv[pl.ds(6 + k, ts), :])
        ext_dq[0:ts, :] = dq
        dcv = vl_ref[5:6, :] * ext_dq[pl.ds(2, ts), :]
        for k in range(1, 3):
            dcv = dcv + vl_ref[5 + k:6 + k, :] * ext_dq[pl.ds(2 - k, ts), :]
        ext_dq[ts:ts + HALO, :] = ext_dq[0:HALO, :]
        dproj_ref[:, 1536:2048] = (dcv * u_v).astype(BF16)
        dproj_ref[:, 2048:2560] = (dcv * u_c).astype(BF16)

        dh = _dot_nt(dproj_ref[:, 0:WIN_BLK], win_ref[0])
        for j in range(1, N_CHIP):
            dh = dh + _dot_nt(dproj_ref[:, j * WIN_BLK:(j + 1) * WIN_BLK], win_ref[j])
        for j in range(N_CHIP):
            gwin_acc[j] += _dot_tn(hb_ref[...], dproj_ref[:, j * WIN_BLK:(j + 1) * WIN_BLK])
        xt = x_ref[...]
        r1 = lax.rsqrt(jnp.mean(xt * xt, axis=-1, keepdims=True) + EPS)
        n1 = xt * r1
        accd_ref[1:2, :] += _colsum(dh * (n1 * g1))
        accd_ref[0:1, :] += _colsum(dh)
        dhn1 = dh * (1.0 + scale1)
        accd_ref[3:4, :] += _colsum(dhn1 * n1)
        dn1 = dhn1 * g1
        gx_ref[...] = dx1 + r1 * (dn1 - n1 * jnp.mean(dn1 * n1, axis=-1, keepdims=True))

        @pl.when(i == nt - 1)
        def _():
            outs = [pltpu.make_async_copy(acc, dst, sems.at[k]) for k, (acc, dst) in enumerate(
                ((gwin_acc, gwin_hbm), (gwout_acc, gwout_hbm), (ggate_acc, ggate_hbm)))]
            for cp in outs:
                cp.start()
            for cp in outs:
                cp.wait()

    tile = lambda w: pl.BlockSpec((ts, w), lambda i: (nt - 1 - i, 0))
    halo = lambda w: pl.BlockSpec((HALO, w), lambda i: (jnp.maximum((nt - 1 - i) * hpt - 1, 0), 0))
    ext = pltpu.VMEM((ts + HALO, D_LRU), F32)
    return pl.pallas_call(
        body, name="mix_bwd", grid=(nt,),
        in_specs=[SMEM, tile(D_MODEL), tile(D_MODEL), tile(D_MODEL), tile(D_IN), halo(D_IN), tile(D_LRU), halo(D_LRU),
                  tile(D_MODEL), tile(D_MODEL), _full((8, D_MODEL)), _full((8, D_MODEL)), _full((16, D_LRU)),
                  ANY, ANY, ANY, ANY, _full((D_LRU, 2 * D_LRU), True), _full((D_LRU, D_LRU), True)],
        out_specs=[tile(D_MODEL), _full((8, D_MODEL)), _full((16, D_LRU)), ANY, ANY, ANY],
        out_shape=[jax.ShapeDtypeStruct((s, D_MODEL), F32),
                   jax.ShapeDtypeStruct((8, D_MODEL), F32), jax.ShapeDtypeStruct((16, D_LRU), F32),
                   jax.ShapeDtypeStruct((N_CHIP, D_MODEL, WIN_BLK), F32), jax.ShapeDtypeStruct((D_MODEL, D_MODEL), F32),
                   jax.ShapeDtypeStruct((D_LRU, 2 * D_LRU), F32)],
        scratch_shapes=[pltpu.VMEM((N_CHIP, D_MODEL, WIN_BLK), BF16), pltpu.VMEM((D_MODEL, D_MODEL), BF16),
                        pltpu.VMEM((ts, D_IN), BF16), pltpu.VMEM((ts, 2 * D_LRU), BF16),
                        pltpu.VMEM((N_CHIP, D_MODEL, WIN_BLK), F32), pltpu.VMEM((D_MODEL, D_MODEL), F32),
                        pltpu.VMEM((D_LRU, 2 * D_LRU), F32),
                        ext, ext, ext, ext, ext, pltpu.VMEM((ts, D_LRU), F32),
                        pltpu.VMEM((HALO, D_LRU), F32), pltpu.VMEM((HALO, D_LRU), F32),
                        pltpu.SemaphoreType.DMA((2 * N_CHIP,))],
        compiler_params=pltpu.CompilerParams(dimension_semantics=("arbitrary",), vmem_limit_bytes=VMEM_LIMIT),
    )(chip, dx1, x, mixed, proj, proj, hl, hl, hb, ycat, mod, vecd, vecl, *win, *wout, gab, a64)


def _wgrad_mlp(collective_id, h2b, dz, act, dmo):
    s = h2b.shape[0]
    nstep = 2 * N_CHIP
    half = FF_BLK // 2

    def body(h2_ref, dz_ref, act_ref, dmo_ref, own_hbm, recv_hbm, buf, keep_sems, send_sems, recv_sems):
        j = pl.program_id(0)
        x, y, c, _ = _position()

        def copies(jj):
            slot = jj % 2
            keep = pltpu.make_async_copy(buf.at[slot, pl.ds(c * half, half), :], own_hbm.at[jj], keep_sems.at[slot])
            give = pltpu.make_async_remote_copy(
                src_ref=buf.at[slot, pl.ds((1 - c) * half, half), :], dst_ref=recv_hbm.at[jj],
                send_sem=send_sems.at[slot], recv_sem=recv_sems.at[jj],
                device_id=(x, y, 1 - c), device_id_type=MESH)
            return keep, give

        @pl.when(j == 0)
        def _():
            pl.semaphore_signal(pltpu.get_barrier_semaphore(), inc=1, device_id=(x, y, 1 - c), device_id_type=MESH)

        @pl.when(j >= 2)
        def _():
            keep, give = copies(j - 2)
            keep.wait()
            give.wait_send()

        @pl.when(j < N_CHIP)
        def _():
            buf[j % 2] = _dot_tn(act_ref[...], dmo_ref[...]).astype(BF16)

        @pl.when(j >= N_CHIP)
        def _():
            buf[j % 2] = _dot_tn(h2_ref[...], dz_ref[...]).astype(BF16)

        @pl.when(j == 0)
        def _():
            pl.semaphore_wait(pltpu.get_barrier_semaphore(), 1)

        keep, give = copies(j)
        keep.start()
        give.start()

        @pl.when(j == nstep - 1)
        def _():
            for jj in (nstep - 2, nstep - 1):
                keep, give = copies(jj)
                keep.wait()
                give.wait_send()
            for jj in range(nstep):
                copies(jj)[1].wait_recv()

    sds = jax.ShapeDtypeStruct((nstep, half, D_MODEL), BF16)
    whole = pl.BlockSpec((s, D_MODEL), lambda j: (0, 0))
    return pl.pallas_call(
        body, name="wgrad_mlp", grid=(nstep,),
        in_specs=[whole, pl.BlockSpec((s, FF_BLK), lambda j: (0, jnp.maximum(j - N_CHIP, 0))),
                  pl.BlockSpec((s, FF_BLK), lambda j: (0, jnp.minimum(j, N_CHIP - 1))), whole],
        out_specs=[ANY, ANY], out_shape=[sds, sds],
        scratch_shapes=[pltpu.VMEM((2, FF_BLK, D_MODEL), BF16), pltpu.SemaphoreType.DMA((2,)),
                        pltpu.SemaphoreType.DMA((2,)), pltpu.SemaphoreType.DMA((nstep,))],
        compiler_params=pltpu.CompilerParams(dimension_semantics=("arbitrary",), vmem_limit_bytes=VMEM_LIMIT,
                                             collective_id=collective_id),
    )(h2b, dz, act, dmo)


def _mod_matmul(c_all, ada_w_loc):
    n = ada_w_loc.shape[1]
    cb = 512

    def body(c_ref, w_ref, o_ref):
        c = c_ref[...]
        sc = c * jax.nn.sigmoid(c)
        o_ref[...] = _dot(sc.astype(BF16), w_ref[...].astype(BF16))

    return pl.pallas_call(
        body, name="mod_matmul", grid=(n // cb,),
        in_specs=[_full((8, D_MODEL)), pl.BlockSpec((D_MODEL, cb), lambda j: (0, j))],
        out_specs=pl.BlockSpec((8, cb), lambda j: (0, j)),
        out_shape=jax.ShapeDtypeStruct((8, n), F32),
        compiler_params=pltpu.CompilerParams(dimension_semantics=("arbitrary",), vmem_limit_bytes=VMEM_LIMIT),
    )(c_all, ada_w_loc)


def _adam_math(w, g, m, v):
    m = ADAM_B1 * m + (1.0 - ADAM_B1) * g
    v = ADAM_B2 * v + (1.0 - ADAM_B2) * (g * g)
    m_hat = m / (1.0 - ADAM_B1 ** ADAM_STEP)
    v_hat = v / (1.0 - ADAM_B2 ** ADAM_STEP)
    delta = (-ADAM_LR) * (m_hat / (jnp.sqrt(v_hat) + ADAM_EPS) + ADAM_WD * w)
    return delta, m, v


def _adam(name, core, shards):
    n = len(shards)
    r, c = shards[0][0].shape
    half = r // 2
    rb = min(half, 128)
    nh = half // rb

    def body(core_ref, *refs):
        ins, outs = refs[:5 * n], refs[5 * n:]
        mine = (pl.program_id(0) // nh) == core_ref[0]
        for k in range(n):
            w_ref, go_ref, gs_ref, m_ref, v_ref = ins[5 * k:5 * k + 5]
            g_ref, d_ref, mo_ref, vo_ref = outs[4 * k:4 * k + 4]
            g = jnp.where(mine, go_ref[...], gs_ref[...])
            g_ref[...] = g
            d_ref[...], mo_ref[...], vo_ref[...] = _adam_math(w_ref[...], g, m_ref[...], v_ref[...])

    spec = pl.BlockSpec((rb, c), lambda i, core_ref: (i, 0))
    hspec = pl.BlockSpec((rb, c), lambda i, core_ref: (i % nh, 0))
    sds = jax.ShapeDtypeStruct((r, c), F32)
    res = pl.pallas_call(
        body, name=name,
        grid_spec=pltpu.PrefetchScalarGridSpec(
            num_scalar_prefetch=1, grid=(r // rb,),
            in_specs=[spec, hspec, hspec, spec, spec] * n, out_specs=[spec] * (4 * n)),
        out_shape=[sds] * (4 * n),
        compiler_params=pltpu.CompilerParams(dimension_semantics=("arbitrary",), vmem_limit_bytes=VMEM_LIMIT),
    )(core, *[t for s in shards for t in s])
    return [res[4 * k:4 * k + 4] for k in range(n)]


def _ada_grad_adam(sct, dmod_loc, w, m, v):
    r, c = w.shape
    rb = 128

    def body(s_ref, dm_ref, w_ref, m_ref, v_ref, g_ref, d_ref, mo_ref, vo_ref):
        g = s_ref[:, 0:1] * dm_ref[0:1, :]
        for b in range(1, 8):
            g = g + s_ref[:, b:b + 1] * dm_ref[b:b + 1, :]
        g_ref[...] = g
        d_ref[...], mo_ref[...], vo_ref[...] = _adam_math(w_ref[...], g, m_ref[...], v_ref[...])

    spec = pl.BlockSpec((rb, c), lambda i: (i, 0))
    sds = jax.ShapeDtypeStruct((r, c), F32)
    return pl.pallas_call(
        body, name="ada_grad_adam", grid=(r // rb,),
        in_specs=[pl.BlockSpec((rb, 8), lambda i: (i, 0)), _full((8, c)), spec, spec, spec],
        out_specs=[spec] * 4, out_shape=[sds] * 4,
        compiler_params=pltpu.CompilerParams(dimension_semantics=("arbitrary",), vmem_limit_bytes=VMEM_LIMIT),
    )(sct, dmod_loc, w, m, v)


def _position():
    x, y, c = lax.axis_index("x"), lax.axis_index("y"), lax.axis_index("c")
    chips = [(1 - x, y), (x, 1 - y), (1 - x, 1 - y)]
    return x, y, c, chips


def _ag8_run(ins, outs, send_sems, recv_sems, local_sems):
    na = len(ins)
    x, y, c, chips = _position()
    me, sibling = (x, y, c), (x, y, 1 - c)
    first, passed, local = [], [], []
    for a in range(na):
        m_per = ins[a].shape[0]

        def rows(px, py, pc, a=a, m_per=m_per):
            return outs[a].at[pl.ds((4 * px + 2 * py + pc) * m_per, m_per), :]

        def copy(k, block, to, src=None, a=a, rows=rows):
            return pltpu.make_async_remote_copy(
                src_ref=rows(*block) if src is None else src, dst_ref=rows(*block),
                send_sem=send_sems.at[7 * a + k], recv_sem=recv_sems.at[7 * a + k],
                device_id=to, device_id_type=MESH)

        mine = pltpu.make_async_copy(ins[a], rows(*me), local_sems.at[a])
        mine.start()
        local.append(mine)
        f = [copy(0, me, sibling, src=ins[a])]
        f += [copy(1 + j, me, (*chip, c), src=ins[a]) for j, chip in enumerate(chips)]
        for cp in f:
            cp.start()
        first.append((f, copy))
    for a in range(na):
        f, copy = first[a]
        p = [copy(4 + j, (*chip, c), sibling) for j, chip in enumerate(chips)]
        for j, chip in enumerate(chips):
            copy(1 + j, (*chip, c), me).wait_recv()
            p[j].start()
        passed.append(p)
    for a in range(na):
        f, copy = first[a]
        copy(0, sibling, me).wait_recv()
        for j, chip in enumerate(chips):
            copy(4 + j, (*chip, 1 - c), me).wait_recv()
        for cp in f + passed[a]:
            cp.wait_send()
        local[a].wait()


def _allgather8(name, arrs):
    na = len(arrs)

    def body(*refs):
        _ag8_run(refs[:na], refs[na:2 * na], *refs[2 * na:])

    return pl.pallas_call(
        body, name=name,
        out_shape=[jax.ShapeDtypeStruct((8 * a.shape[0], a.shape[1]), a.dtype) for a in arrs],
        in_specs=[VMEM] * na, out_specs=[VMEM] * na,
        scratch_shapes=[pltpu.SemaphoreType.DMA((7 * na,)), pltpu.SemaphoreType.DMA((7 * na,)),
                        pltpu.SemaphoreType.DMA((na,))],
        compiler_params=pltpu.CompilerParams(vmem_limit_bytes=VMEM_LIMIT),
    )(*arrs)


AG_SEMS = 7


def _ag_copies(ins, outs, send_sems, recv_sems):
    x, y, c, chips = _position()
    sibling = (x, y, 1 - c)
    xn, yn, dg = [2 * chip[0] + chip[1] for chip in chips]
    to_x, to_y = (1 - x, y, c), (x, 1 - y, c)
    res = []
    for a in range(len(ins)):
        half = ins[a].shape[0] // 2
        quarter = half // 2

        def copy(k, dst, to, src=None, a=a):
            return pltpu.make_async_remote_copy(
                src_ref=dst if src is None else src, dst_ref=dst,
                send_sem=send_sems.at[AG_SEMS * a + k], recv_sem=recv_sems.at[AG_SEMS * a + k],
                device_id=to, device_id_type=MESH)

        def rows(chip, pc, q=None, a=a, half=half, quarter=quarter):
            if q is None:
                return outs[a].at[chip, pl.ds(pc * half, half), :]
            return outs[a].at[chip, pl.ds(pc * half + q * quarter, quarter), :]

        own = ins[a].at[pl.ds(c * half, half), :]
        mine = rows(2 * x + y, c)
        res.append(dict(
            sends=[copy(0, mine, to_x, src=own), copy(1, mine, to_y, src=own)],
            from_x=copy(0, rows(xn, c), to_x), from_y=copy(1, rows(yn, c), to_y),
            relay_y=copy(2, rows(xn, c, 0), to_y), relay_x=copy(3, rows(yn, c, 1), to_x),
            from_y_relay=copy(2, rows(dg, c, 0), to_y), from_x_relay=copy(3, rows(dg, c, 1), to_x),
            pass_on=[copy(4, rows(xn, c), sibling), copy(5, rows(yn, c), sibling), copy(6, rows(dg, c), sibling)],
            from_sibling=[copy(4, rows(xn, 1 - c), sibling), copy(5, rows(yn, 1 - c), sibling),
                          copy(6, rows(dg, 1 - c), sibling)]))
    return res


def _ag_start(ins, outs, send_sems, recv_sems):
    for cps in _ag_copies(ins, outs, send_sems, recv_sems):
        for cp in cps["sends"]:
            cp.start()


def _ag_relay(ins, outs, send_sems, recv_sems, which):
    copies = _ag_copies(ins, outs, send_sems, recv_sems)
    for a in which:
        cps = copies[a]
        cps["from_x"].wait_recv()
        cps["relay_y"].start()
        cps["pass_on"][0].start()
        cps["from_y"].wait_recv()
        cps["relay_x"].start()
        cps["pass_on"][1].start()


def _ag_complete(ins, outs, send_sems, recv_sems):
    copies = _ag_copies(ins, outs, send_sems, recv_sems)
    for cps in copies:
        cps["from_y_relay"].wait_recv()
        cps["from_x_relay"].wait_recv()
        cps["pass_on"][2].start()
    for cps in copies:
        for cp in cps["from_sibling"]:
            cp.wait_recv()
        for cp in cps["sends"] + [cps["relay_y"], cps["relay_x"]] + cps["pass_on"]:
            cp.wait_send()


def _ag_finish(ins, outs, send_sems, recv_sems):
    _ag_relay(ins, outs, send_sems, recv_sems, range(len(ins)))
    _ag_complete(ins, outs, send_sems, recv_sems)


def _allgather_weights(name, collective_id, shards):
    na = len(shards)
    hbm = pltpu.MemorySpace.HBM
    ins = [jax.new_ref(s, memory_space=hbm) for s in shards]
    outs = [jax.empty_ref(jax.ShapeDtypeStruct((N_CHIP,) + s.shape, s.dtype), memory_space=hbm) for s in shards]

    @pl.kernel(mesh=plsc.ScalarSubcoreMesh(axis_name="sequencer", num_cores=1), name=name,
               scratch_types=(pltpu.SemaphoreType.DMA((AG_SEMS * na,)), pltpu.SemaphoreType.DMA((AG_SEMS * na,))),
               compiler_params=pltpu.CompilerParams(collective_id=collective_id))
    def launch(send_sems, recv_sems):
        x, y, c, _ = _position()
        peers = [(1 - x, y, c), (x, 1 - y, c), (x, y, 1 - c)]
        barrier = pltpu.get_barrier_semaphore()
        for peer in peers:
            pl.semaphore_signal(barrier, inc=1, device_id=peer, device_id_type=MESH)
        pl.semaphore_wait(barrier, len(peers))
        _ag_start(ins, outs, send_sems, recv_sems)
        _ag_finish(ins, outs, send_sems, recv_sems)

    launch()
    return [o[...] for o in outs]


def _sibling_swap(name, arrs, split_rows, collective_id=None):
    na = len(arrs)
    shapes = [jax.ShapeDtypeStruct((a.shape[0], a.shape[1] // 2, a.shape[2]) if split_rows else a.shape, a.dtype)
              for a in arrs]

    def run(ins, outs, send_sems, recv_sems):
        x, y, c, _ = _position()
        cps = []
        for a in range(na):
            src = ins[a]
            if split_rows:
                half = src.shape[1] // 2
                src = src.at[:, pl.ds((1 - c) * half, half), :]
            cp = pltpu.make_async_remote_copy(
                src_ref=src, dst_ref=outs[a], send_sem=send_sems.at[a], recv_sem=recv_sems.at[a],
                device_id=(x, y, 1 - c), device_id_type=MESH)
            cp.start()
            cps.append(cp)
        for cp in cps:
            cp.wait()

    sems = (pltpu.SemaphoreType.DMA((na,)), pltpu.SemaphoreType.DMA((na,)))
    if collective_id is None:
        return pl.pallas_call(
            lambda *refs: run(refs[:na], refs[na:2 * na], *refs[2 * na:]), name=name, out_shape=shapes,
            in_specs=[ANY] * na, out_specs=[ANY] * na, scratch_shapes=list(sems))(*arrs)

    hbm = pltpu.MemorySpace.HBM
    ins = [jax.new_ref(a, memory_space=hbm) for a in arrs]
    outs = [jax.empty_ref(s, memory_space=hbm) for s in shapes]

    @pl.kernel(mesh=plsc.ScalarSubcoreMesh(axis_name="sequencer", num_cores=1), name=name, scratch_types=sems,
               compiler_params=pltpu.CompilerParams(collective_id=collective_id))
    def launch(send_sems, recv_sems):
        x, y, c, _ = _position()
        barrier = pltpu.get_barrier_semaphore()
        pl.semaphore_signal(barrier, inc=1, device_id=(x, y, 1 - c), device_id_type=MESH)
        pl.semaphore_wait(barrier, 1)
        run(ins, outs, send_sems, recv_sems)

    launch()
    return [o[...] for o in outs]


def _xchg_copies(ins, outs, send_sems, recv_sems):
    x, y, c, chips = _position()
    return [pltpu.make_async_remote_copy(
        src_ref=ins[a].at[2 * chip[0] + chip[1]], dst_ref=outs[a].at[j],
        send_sem=send_sems.at[3 * a + j], recv_sem=recv_sems.at[3 * a + j],
        device_id=(*chip, c), device_id_type=MESH) for a in range(len(ins)) for j, chip in enumerate(chips)]


def _exchange_chips(name, collective_id, parts):
    na = len(parts)
    hbm = pltpu.MemorySpace.HBM
    ins = [jax.new_ref(p, memory_space=hbm) for p in parts]
    outs = [jax.empty_ref(jax.ShapeDtypeStruct((3,) + p.shape[1:], p.dtype), memory_space=hbm) for p in parts]

    @pl.kernel(mesh=plsc.ScalarSubcoreMesh(axis_name="sequencer", num_cores=1), name=name,
               scratch_types=(pltpu.SemaphoreType.DMA((3 * na,)), pltpu.SemaphoreType.DMA((3 * na,))),
               compiler_params=pltpu.CompilerParams(collective_id=collective_id))
    def launch(send_sems, recv_sems):
        x, y, c, chips = _position()
        barrier = pltpu.get_barrier_semaphore()
        for chip in chips:
            pl.semaphore_signal(barrier, inc=1, device_id=(*chip, c), device_id_type=MESH)
        pl.semaphore_wait(barrier, len(chips))
        for cp in _xchg_copies(ins, outs, send_sems, recv_sems):
            cp.start()
        for cp in _xchg_copies(ins, outs, send_sems, recv_sems):
            cp.wait()

    launch()
    return [q[...] for q in outs]


def _add_sibling(name, grad, recv, core, after=()):
    _, r, c = grad.shape
    half = r // 2
    rb = min(half, 256)
    nrb = half // rb

    def body(core_ref, g_ref, r_ref, *refs):
        refs[-1][...] = (g_ref[...].astype(F32) + r_ref[...].astype(F32)).astype(BF16)

    return pl.pallas_call(
        body, name=name,
        grid_spec=pltpu.PrefetchScalarGridSpec(
            num_scalar_prefetch=1, grid=(N_CHIP, nrb),
            in_specs=[pl.BlockSpec((1, rb, c), lambda j, i, core_ref: (j, core_ref[0] * nrb + i, 0)),
                      pl.BlockSpec((1, rb, c), lambda j, i, core_ref: (j, i, 0))] + [ANY] * len(after),
            out_specs=pl.BlockSpec((1, rb, c), lambda j, i, core_ref: (j, i, 0))),
        out_shape=jax.ShapeDtypeStruct((N_CHIP, half, c), BF16),
        compiler_params=pltpu.CompilerParams(dimension_semantics=("arbitrary", "arbitrary"),
                                             vmem_limit_bytes=VMEM_LIMIT),
    )(core, grad, recv, *after)


def _add_halves(name, own, recv):
    n = own.shape[0] // N_CHIP
    _, half, c = own.shape
    rb = min(half, 256)

    def body(*refs):
        for k in range(n):
            refs[2 * n + k][...] = (refs[k][...].astype(F32) + refs[n + k][...].astype(F32)).astype(BF16)

    group = [pl.BlockSpec((1, rb, c), lambda j, i, k=k: (N_CHIP * k + j, i, 0)) for k in range(n)]
    spec = pl.BlockSpec((1, rb, c), lambda j, i: (j, i, 0))
    return pl.pallas_call(
        body, name=name, grid=(N_CHIP, half // rb), in_specs=group + group, out_specs=[spec] * n,
        out_shape=[jax.ShapeDtypeStruct((N_CHIP, half, c), BF16)] * n,
        compiler_params=pltpu.CompilerParams(dimension_semantics=("arbitrary", "arbitrary"),
                                             vmem_limit_bytes=VMEM_LIMIT),
    )(*([own] * n + [recv] * n))


def _add_chips(name, chip, p, q, after=()):
    _, half, c = q.shape
    rb = min(half, 256)

    def body(chip_ref, p_ref, q_ref, *refs):
        acc = p_ref[0].astype(F32)
        for j in range(3):
            acc = acc + q_ref[j].astype(F32)
        refs[-1][...] = acc

    return pl.pallas_call(
        body, name=name,
        grid_spec=pltpu.PrefetchScalarGridSpec(
            num_scalar_prefetch=1, grid=(half // rb,),
            in_specs=[pl.BlockSpec((1, rb, c), lambda i, chip_ref: (chip_ref[0], i, 0)),
                      pl.BlockSpec((3, rb, c), lambda i, chip_ref: (0, i, 0))] + [ANY] * len(after),
            out_specs=pl.BlockSpec((rb, c), lambda i, chip_ref: (i, 0))),
        out_shape=jax.ShapeDtypeStruct((half, c), F32),
        compiler_params=pltpu.CompilerParams(dimension_semantics=("arbitrary",), vmem_limit_bytes=VMEM_LIMIT),
    )(chip, p, q, *after)


def _small_update(gad, gam, gl, gg, mychip, params):
    names = ["ada_b", "norm1_g", "lru_conv_b", "gate_a_w", "gate_a_b", "gate_x_w", "gate_x_b", "a_param",
             "lru_conv_w", "short_conv_w", "lru_out_g", "conv_out_g", "norm2_g", "final_g"]
    flat = [t for n in names for t in params[n]]
    nin = len(flat)

    def body(chip_ref, gad_ref, gam_ref, gl_ref, gg_ref, *refs):
        ins = {n: refs[3 * k:3 * k + 3] for k, n in enumerate(names)}
        outs = {n: refs[nin + 4 * k:nin + 4 * k + 4] for k, n in enumerate(names)}
        loss_ref, dmod_ref = refs[nin + 4 * len(names):nin + 4 * len(names) + 2]

        def dsum(ref, lo, n):
            per = ref.shape[0] // 8
            acc = ref[lo:lo + n, :].astype(F32)
            for dev in range(1, 8):
                acc = acc + ref[dev * per + lo:dev * per + lo + n, :].astype(F32)
            return acc

        def update(n, g):
            w_ref, m_ref, v_ref = ins[n]
            g_ref, d_ref, mo_ref, vo_ref = outs[n]
            g_ref[...] = g
            d_ref[...], mo_ref[...], vo_ref[...] = _adam_math(w_ref[...], g, m_ref[...], v_ref[...])

        d, dm, l, lw = refs[-4:]
        d[...] = dsum(gad_ref, 0, 8)
        dm[...] = dsum(gam_ref, 0, 8)
        l[...] = dsum(gl_ref, 0, 16)
        for dev in range(8):
            for k in range(3):
                dmod_ref[dev:dev + 1, k * D_MODEL:(k + 1) * D_MODEL] = gad_ref[dev * 8 + k:dev * 8 + k + 1, :]
                dmod_ref[dev:dev + 1, (3 + k) * D_MODEL:(4 + k) * D_MODEL] = gam_ref[dev * 8 + k:dev * 8 + k + 1, :]
        w_ref, m_ref, v_ref = ins["ada_b"]
        g_ref, d_ref, mo_ref, vo_ref = outs["ada_b"]
        for k in range(3):
            g_ref[:, k * D_MODEL:(k + 1) * D_MODEL] = d[k:k + 1, :]
            g_ref[:, (3 + k) * D_MODEL:(4 + k) * D_MODEL] = dm[k:k + 1, :]
        d_ref[...], mo_ref[...], vo_ref[...] = _adam_math(w_ref[...], g_ref[...], m_ref[...], v_ref[...])
        update("norm1_g", d[3:4, :])
        update("norm2_g", dm[3:4, :])
        update("final_g", dm[4:5, :])
        update("gate_a_b", d[4:5, 0:D_LRU])
        update("gate_x_b", d[4:5, D_LRU:2 * D_LRU])
        update("lru_conv_b", l[4:5, :])
        update("a_param", l[8:9, :] * jax.nn.sigmoid(ins["a_param"][0][...]))
        update("lru_out_g", l[9:10, :])
        update("conv_out_g", l[10:11, :])
        loss_ref[...] = jnp.broadcast_to(dm[5:6, 0:128], (8, 128))
        chip = chip_ref[0]
        acc = jnp.zeros((8, 128), F32)
        for j in range(N_CHIP):
            acc = acc + jnp.where(chip == j, l[0:8, j * 128:(j + 1) * 128], 0.0)
        lw[...] = acc
        update("lru_conv_w", lw[0:4, :])
        update("short_conv_w", lw[5:8, :])
        gates = dsum(gg_ref, 0, D_LRU)
        update("gate_a_w", gates[:, 0:HEAD])
        update("gate_x_w", gates[:, HEAD:2 * HEAD])

    out_shape = []
    for n in names:
        out_shape += [jax.ShapeDtypeStruct(params[n][0].shape, F32)] * 4
    out_shape += [jax.ShapeDtypeStruct((8, 128), F32), jax.ShapeDtypeStruct((8, 6 * D_MODEL), F32)]
    res = pl.pallas_call(
        body, name="small_update", out_shape=out_shape,
        in_specs=[SMEM] + [VMEM] * (4 + nin),
        out_specs=[VMEM] * len(out_shape),
        scratch_shapes=[pltpu.VMEM((8, D_MODEL), F32), pltpu.VMEM((8, D_MODEL), F32), pltpu.VMEM((16, D_LRU), F32),
                        pltpu.VMEM((8, 128), F32)],
        compiler_params=pltpu.CompilerParams(vmem_limit_bytes=VMEM_LIMIT),
    )(mychip, gad, gam, gl, gg, *flat)
    per = {n: res[4 * k:4 * k + 4] for k, n in enumerate(names)}
    return per, res[-2], res[-1]


def _block_diag(w):
    eye = jnp.eye(8, dtype=w.dtype)
    return (eye[:, None, :, None] * w[:, :, None, :]).reshape(8 * HEAD, 8 * HEAD)


def _diag_blocks(g):
    return jnp.concatenate([g[h * HEAD:(h + 1) * HEAD, h * HEAD:(h + 1) * HEAD] for h in range(8)], axis=0)


def kernel(x, c, ada_w, ada_b, norm1_g, w_in, lru_conv_w, lru_conv_b, gate_a_w, gate_a_b, gate_x_w, gate_x_b, a_param, short_conv_w, lru_out_g, conv_out_g, w_out, norm2_g, w_mlp1, w_mlp2, final_g, loss_target, m_ada_w, m_ada_b, m_norm1_g, m_w_in, m_lru_conv_w, m_lru_conv_b, m_gate_a_w, m_gate_a_b, m_gate_x_w, m_gate_x_b, m_a_param, m_short_conv_w, m_lru_out_g, m_conv_out_g, m_w_out, m_norm2_g, m_w_mlp1, m_w_mlp2, m_final_g, v_ada_w, v_ada_b, v_norm1_g, v_w_in, v_lru_conv_w, v_lru_conv_b, v_gate_a_w, v_gate_a_b, v_gate_x_w, v_gate_x_b, v_a_param, v_short_conv_w, v_lru_out_g, v_conv_out_g, v_w_out, v_norm2_g, v_w_mlp1, v_w_mlp2, v_final_g):
    xi, yi, ci = lax.axis_index("x"), lax.axis_index("y"), lax.axis_index("c")
    mychip = 2 * xi + yi
    me = 4 * xi + 2 * yi + ci

    own_in, own_out = w_in[0].astype(BF16), w_out[0].astype(BF16)
    win_all, wout_all = _allgather_weights("allgather_mixer_weights", 1, [own_in, own_out])
    own_w1, own_w2 = w_mlp1[0].astype(BF16), w_mlp2[0].astype(BF16)
    w1_all, w2_all = _allgather_weights("allgather_mlp_weights", 2, [own_w1, own_w2])

    c_blk = jnp.zeros((8, D_MODEL), F32).at[0:1].set(c)
    cw_blk = jnp.zeros((8, 128), F32).at[0:4].set(lru_conv_w[0]).at[4:7].set(short_conv_w[0])
    c_g, cw_g = _allgather8("allgather_cond", [c_blk, cw_blk])
    c_all = c_g.reshape(8, 8, D_MODEL)[:, 0]
    cw_g = cw_g.reshape(4, 2, 8, 128)[:, 0]
    lcw = cw_g[:, 0:4].transpose(1, 0, 2).reshape(4, D_LRU)
    scw = cw_g[:, 4:7].transpose(1, 0, 2).reshape(3, D_LRU)

    mod_loc = _mod_matmul(c_all, ada_w[0])
    (mod_g,) = _allgather8("allgather_mod", [mod_loc])
    mod_all = mod_g.reshape(4, 2, 8, 6 * D_MODEL // 4)[:, 0].transpose(1, 0, 2).reshape(8, 6 * D_MODEL) + ada_b
    mod_pad = jnp.pad(mod_all.reshape(8, 6, D_MODEL), ((0, 0), (0, 2), (0, 0)))
    mod = lax.dynamic_slice_in_dim(mod_pad, me, 1, axis=0).reshape(8, D_MODEL)

    win, wout = (win_all, own_in), (wout_all, own_out)
    chip = mychip.reshape(1).astype(jnp.int32)
    core = ci.reshape(1).astype(jnp.int32)

    vecd = jnp.concatenate([norm1_g, norm2_g, final_g[None, :], jnp.concatenate([gate_a_b, gate_x_b], axis=1),
                            jnp.zeros((4, D_MODEL), F32)], axis=0)
    vecl = jnp.concatenate([lcw, lru_conv_b, scw, a_param, lru_out_g, conv_out_g, jnp.zeros((5, D_LRU), F32)], axis=0)
    gab = jnp.concatenate([_block_diag(gate_a_w[0]), _block_diag(gate_x_w[0])], axis=1).astype(BF16)
    a64 = _block_diag(jnp.full((8, HEAD, HEAD), 1.0 / HEAD, F32)).astype(BF16)

    hb, proj, hl, ycat, mixed, x1 = _mix_fwd(chip, x[0], mod, vecd, vecl, win, wout, gab, a64)
    dx1, act, dz, dmo, h2b, accm = _mlp_fwd_bwd(
        chip, x1, loss_target[0], mod, vecd, (w1_all, own_w1), (w2_all, own_w2))

    p_w2, p_w1 = _add_halves("rs_add_sibling_mlp", *_wgrad_mlp(6, h2b, dz, act, dmo))
    parts_mlp = [p_w1, p_w2]
    q_w1, q_w2 = _exchange_chips("rs_exchange_mlp", 0, parts_mlp)
    grad_x, accd, accl, g_win, g_wout, g_gate = _mix_bwd(
        chip, dx1, x[0], mixed, proj, hl, hb, ycat, mod, vecd, vecl, win, wout, gab, a64)

    g_mix = [g_win, g_wout.reshape(N_CHIP, WOUT_BLK, D_MODEL)]
    recv_mix = _sibling_swap("rs_swap_halves_mix", g_mix, True, collective_id=4)
    own_mlp = [_add_chips("rs_add_chips_mlp%d" % k, chip, p, q) for k, (p, q) in enumerate(zip(parts_mlp, (q_w1, q_w2)))]
    sib_mlp = _sibling_swap("rs_swap_reduced_mlp", own_mlp, False, collective_id=5)
    gg_blk = jnp.concatenate([_diag_blocks(g_gate[:, 0:D_LRU]), _diag_blocks(g_gate[:, D_LRU:2 * D_LRU])], axis=1)
    gad, gam, gl, gg = _allgather8("allgather_small_grads", [accd, accm, accl, gg_blk.astype(BF16)])

    parts_mix = [_add_sibling("rs_add_sibling_mix%d" % k, g, r, core, after=[gad])
                 for k, (g, r) in enumerate(zip(g_mix, recv_mix))]
    landed_mix = _exchange_chips("rs_exchange_mix", 3, parts_mix)
    res_w1, res_w2 = _adam("adam_mlp", core, [(w_mlp1[0], own_mlp[0], sib_mlp[0], m_w_mlp1[0], v_w_mlp1[0]),
                                              (w_mlp2[0], own_mlp[1], sib_mlp[1], m_w_mlp2[0], v_w_mlp2[0])])
    own_mix = [_add_chips("rs_add_chips_mix%d" % k, chip, p, q, after=[res_w1[1]])
               for k, (p, q) in enumerate(zip(parts_mix, landed_mix))]
    sib_mix = _sibling_swap("rs_swap_reduced_mix", own_mix, False)
    (res_win,) = _adam("adam_w_in", core, [(w_in[0], own_mix[0], sib_mix[0], m_w_in[0], v_w_in[0])])
    (res_wout,) = _adam("adam_w_out", core, [(w_out[0], own_mix[1], sib_mix[1], m_w_out[0], v_w_out[0])])

    params = {
        "ada_b": (ada_b, m_ada_b, v_ada_b), "norm1_g": (norm1_g, m_norm1_g, v_norm1_g),
        "lru_conv_b": (lru_conv_b, m_lru_conv_b, v_lru_conv_b),
        "gate_a_w": tuple(t.reshape(D_LRU, HEAD) for t in (gate_a_w, m_gate_a_w, v_gate_a_w)),
        "gate_a_b": (gate_a_b, m_gate_a_b, v_gate_a_b),
        "gate_x_w": tuple(t.reshape(D_LRU, HEAD) for t in (gate_x_w, m_gate_x_w, v_gate_x_w)),
        "gate_x_b": (gate_x_b, m_gate_x_b, v_gate_x_b), "a_param": (a_param, m_a_param, v_a_param),
        "lru_conv_w": tuple(t[0] for t in (lru_conv_w, m_lru_conv_w, v_lru_conv_w)),
        "short_conv_w": tuple(t[0] for t in (short_conv_w, m_short_conv_w, v_short_conv_w)),
        "lru_out_g": (lru_out_g, m_lru_out_g, v_lru_out_g), "conv_out_g": (conv_out_g, m_conv_out_g, v_conv_out_g),
        "norm2_g": (norm2_g, m_norm2_g, v_norm2_g),
        "final_g": tuple(t[None, :] for t in (final_g, m_final_g, v_final_g)),
    }
    small, loss_blk, dmod_cols = _small_update(gad, gam, gl, gg, chip, params)
    loss = loss_blk[0, 0]

    ncol = 6 * D_MODEL // N_CHIP
    dmod_loc = lax.dynamic_slice_in_dim(dmod_cols, mychip * ncol, ncol, axis=1)
    sct = (c_all * jax.nn.sigmoid(c_all)).T
    ada = _ada_grad_adam(sct, dmod_loc, ada_w[0], m_ada_w[0], v_ada_w[0])

    res = {"ada_w": ada, "w_in": res_win, "w_out": res_wout, "w_mlp1": res_w1, "w_mlp2": res_w2}
    res = {n: tuple(t[None] for t in r) for n, r in res.items()}
    shapes = {"gate_a_w": gate_a_w.shape, "gate_x_w": gate_x_w.shape, "lru_conv_w": lru_conv_w.shape,
              "short_conv_w": short_conv_w.shape, "final_g": final_g.shape}
    for n, t in small.items():
        res[n] = tuple(u.reshape(shapes[n]) if n in shapes else u for u in t)

    order = ["ada_w", "ada_b", "norm1_g", "w_in", "lru_conv_w", "lru_conv_b", "gate_a_w", "gate_a_b", "gate_x_w",
             "gate_x_b", "a_param", "short_conv_w", "lru_out_g", "conv_out_g", "w_out", "norm2_g", "w_mlp1",
             "w_mlp2", "final_g"]
    return (loss, grad_x[None], *[res[n][0] for n in order], *[res[n][1] for n in order],
            *[res[n][2] for n in order], *[res[n][3] for n in order])
```

```python
import jax
import jax.numpy as jnp
from jax import lax
from jax.experimental import pallas as pl
from jax.experimental.pallas import tpu as pltpu
from jax.experimental.pallas import tpu_sc as plsc

F32 = jnp.float32
BF16 = jnp.bfloat16

D_MODEL = 1024
D_LRU = 512
D_IN = 2560
D_FF = 4096
N_CHIP = 4
WIN_BLK = D_IN // N_CHIP
WOUT_BLK = D_MODEL // N_CHIP
FF_BLK = D_FF // N_CHIP
HEAD = 64
EPS = 1e-6
C_GATE = 8.0
TOKEN_TILE = 256
HALO = 8
VMEM_LIMIT = 60 * 1024 * 1024

ADAM_LR = 0.001
ADAM_B1 = 0.9
ADAM_B2 = 0.999
ADAM_EPS = 1e-08
ADAM_WD = 0.01
ADAM_STEP = 10

MESH = pl.DeviceIdType.MESH
ANY = pl.BlockSpec(memory_space=pl.ANY)
VMEM = pl.BlockSpec(memory_space=pltpu.VMEM)
SMEM = pl.BlockSpec(memory_space=pltpu.SMEM)


def _full(shape, single=False):
    nd = len(shape)
    if single:
        return pl.BlockSpec(shape, lambda *_: (0,) * nd, pipeline_mode=pl.Buffered(1))
    return pl.BlockSpec(shape, lambda *_: (0,) * nd)


def _dot(a, b):
    return jnp.dot(a, b, preferred_element_type=F32)


def _dot_nt(a, b):
    return lax.dot_general(a, b, (((1,), (1,)), ((), ())), preferred_element_type=F32)


def _dot_tn(a, b):
    return lax.dot_general(a, b, (((0,), (0,)), ((), ())), preferred_element_type=F32)


def _gmean(v, a64):
    hi = v.astype(BF16)
    lo = (v - hi.astype(F32)).astype(BF16)
    return _dot(hi, a64) + _dot(lo, a64)


def _gelu(x):
    u = 0.7978845608028654 * (x + 0.044715 * x * x * x)
    t = jnp.tanh(u)
    return 0.5 * x * (1.0 + t), t


def _gelu_grad(x, t):
    du = 0.7978845608028654 * (1.0 + 3.0 * 0.044715 * x * x)
    return 0.5 * (1.0 + t) + 0.5 * x * (1.0 - t * t) * du


def _log1p_pos(y):
    return jnp.where(y < 1e-2, y * (1.0 - y * (0.5 - y * (1.0 / 3.0 - y * 0.25))), jnp.log(1.0 + y))


def _softplus(a):
    return jnp.maximum(a, 0.0) + _log1p_pos(jnp.exp(-jnp.abs(a)))


def _neg_expm1(z):
    series = -z * (1.0 + z * (0.5 + z * (1.0 / 6.0 + z * (1.0 / 24.0))))
    return jnp.where(z > -0.02, series, 1.0 - jnp.exp(z))


def _scan_fwd(a, b, row):
    n = a.shape[0]
    d = 1
    while d < n:
        m = row >= d
        b = jnp.where(m, a * pltpu.roll(b, d, 0) + b, b)
        a = jnp.where(m, a * pltpu.roll(a, d, 0), a)
        d *= 2
    return a, b


def _scan_rev(a, b, row):
    n = a.shape[0]
    d = 1
    while d < n:
        m = row < n - d
        b = jnp.where(m, b + a * pltpu.roll(b, n - d, 0), b)
        a = jnp.where(m, a * pltpu.roll(a, n - d, 0), a)
        d *= 2
    return a, b


def _colsum(v):
    return jnp.sum(v, axis=0, keepdims=True)


def _load_gathered(chip, gathered, own, slot, sems):
    copies = []
    for j in range(N_CHIP):
        @pl.when(chip == j)
        def _(j=j):
            pltpu.make_async_copy(own, slot(j), sems.at[j]).start()

        @pl.when(chip != j)
        def _(j=j):
            pltpu.make_async_copy(gathered.at[j], slot(j), sems.at[j]).start()

        copies.append(pltpu.make_async_copy(own, slot(j), sems.at[j]))
    return copies


def _lru_gates(xlb, gab, gbias, sp, first_row):
    g = _dot(xlb, gab) + gbias
    r = jax.nn.sigmoid(g[:, :D_LRU])
    ig = jax.nn.sigmoid(g[:, D_LRU:])
    la = (-C_GATE) * r * sp
    a = jnp.exp(la)
    msq = jnp.sqrt(_neg_expm1(2.0 * la))
    mult = jnp.where(first_row, 1.0, msq)
    return r, ig, a, msq, mult


def _mix_fwd(chip, x, mod, vecd, vecl, win, wout, gab, a64):
    s = x.shape[0]
    ts = TOKEN_TILE
    nt = s // ts

    def body(chip_ref, x_ref, mod_ref, vd_ref, vl_ref, win_hbm, win_own, wout_hbm, wout_own, gab_ref, a64_ref,
             hb_ref, proj_ref, hl_ref, ycat_ref, mixed_ref, x1_ref,
             win_ref, wout_ref, ext_lx, ext_cv, hcar, sems):
        i = pl.program_id(0)

        @pl.when(i == 0)
        def _():
            cps = _load_gathered(chip_ref[0], win_hbm, win_own, lambda j: win_ref.at[j], sems.at[pl.ds(0, N_CHIP)])
            cps += _load_gathered(chip_ref[0], wout_hbm, wout_own,
                                  lambda j: wout_ref.at[pl.ds(j * WOUT_BLK, WOUT_BLK), :],
                                  sems.at[pl.ds(N_CHIP, N_CHIP)])
            ext_lx[0:HALO, :] = jnp.zeros((HALO, D_LRU), F32)
            ext_cv[0:HALO, :] = jnp.zeros((HALO, D_LRU), F32)
            hcar[...] = jnp.zeros_like(hcar)
            for cp in cps:
                cp.wait()

        row = lax.broadcasted_iota(jnp.int32, (ts, D_LRU), 0)
        first_row = jnp.logical_and(row == 0, i == 0)
        xt = x_ref[...]
        shift1, scale1, gate1 = mod_ref[0:1, :], mod_ref[1:2, :], mod_ref[2:3, :]
        r1 = lax.rsqrt(jnp.mean(xt * xt, axis=-1, keepdims=True) + EPS)
        h = (xt * r1) * vd_ref[0:1, :] * (1.0 + scale1) + shift1
        hb = h.astype(BF16)
        hb_ref[...] = hb
        for j in range(N_CHIP):
            proj_ref[:, j * WIN_BLK:(j + 1) * WIN_BLK] = _dot(hb, win_ref[j])
        u_ly = proj_ref[:, 512:1024]
        u_b = proj_ref[:, 1024:1536]

        ext_lx[HALO:HALO + ts, :] = proj_ref[:, 0:512]
        xl = vl_ref[4:5, :] + vl_ref[0:1, :] * ext_lx[pl.ds(5, ts), :]
        for k in range(1, 4):
            xl = xl + vl_ref[k:k + 1, :] * ext_lx[pl.ds(5 + k, ts), :]
        ext_lx[0:HALO, :] = ext_lx[ts:ts + HALO, :]
        sp = _softplus(vl_ref[8:9, :])
        _, ig, a, _, mult = _lru_gates(xl.astype(BF16), gab_ref[...], vd_ref[3:4, :], sp, first_row)
        acum, hloc = _scan_fwd(a, mult * (ig * xl), row)
        hl = hloc + acum * hcar[0:1, :]
        hl_ref[...] = hl
        hcar[0:1, :] = hl_ref[ts - 1:ts, :]
        ge, _ = _gelu(u_ly)
        p = ge * hl
        y_lru = p * lax.rsqrt(_gmean(p * p, a64_ref[...]) + EPS) * vl_ref[9:10, :]
        ycat_ref[:, 0:512] = y_lru.astype(BF16)

        ext_cv[HALO:HALO + ts, :] = proj_ref[:, 1536:2048] * proj_ref[:, 2048:2560]
        q = vl_ref[5:6, :] * ext_cv[pl.ds(6, ts), :]
        for k in range(1, 3):
            q = q + vl_ref[5 + k:6 + k, :] * ext_cv[pl.ds(6 + k, ts), :]
        ext_cv[0:HALO, :] = ext_cv[ts:ts + HALO, :]
        yc = u_b * q
        y_conv = yc * lax.rsqrt(_gmean(yc * yc, a64_ref[...]) + EPS) * vl_ref[10:11, :]
        ycat_ref[:, 512:1024] = y_conv.astype(BF16)

        mixed = _dot(ycat_ref[...], wout_ref[...])
        mixed_ref[...] = mixed
        x1_ref[...] = xt + gate1 * mixed

    tile = lambda w: pl.BlockSpec((ts, w), lambda i: (i, 0))
    return pl.pallas_call(
        body, name="mix_fwd", grid=(nt,),
        in_specs=[SMEM, tile(D_MODEL), _full((8, D_MODEL)), _full((8, D_MODEL)), _full((16, D_LRU)),
                  ANY, ANY, ANY, ANY, _full((D_LRU, 2 * D_LRU), True), _full((D_LRU, D_LRU), True)],
        out_specs=[tile(D_MODEL), tile(D_IN), tile(D_LRU), tile(D_MODEL), tile(D_MODEL), tile(D_MODEL)],
        out_shape=[jax.ShapeDtypeStruct((s, D_MODEL), BF16), jax.ShapeDtypeStruct((s, D_IN), F32),
                   jax.ShapeDtypeStruct((s, D_LRU), F32), jax.ShapeDtypeStruct((s, D_MODEL), BF16),
                   jax.ShapeDtypeStruct((s, D_MODEL), F32), jax.ShapeDtypeStruct((s, D_MODEL), F32)],
        scratch_shapes=[pltpu.VMEM((N_CHIP, D_MODEL, WIN_BLK), BF16), pltpu.VMEM((D_MODEL, D_MODEL), BF16),
                        pltpu.VMEM((ts + HALO, D_LRU), F32), pltpu.VMEM((ts + HALO, D_LRU), F32),
                        pltpu.VMEM((HALO, D_LRU), F32), pltpu.SemaphoreType.DMA((2 * N_CHIP,))],
        compiler_params=pltpu.CompilerParams(dimension_semantics=("arbitrary",), vmem_limit_bytes=VMEM_LIMIT),
    )(chip, x, mod, vecd, vecl, *win, *wout, gab, a64)


def _mlp_fwd_bwd(chip, x1, target, mod, vecd, w1, w2):
    s = x1.shape[0]
    ts = TOKEN_TILE
    nt = s // ts

    def body(chip_ref, x1_ref, tg_ref, mod_ref, vd_ref, w1_hbm, w1_own, w2_hbm, w2_own,
             dx1_ref, act_ref, dz_ref, dmo_ref, h2_ref, acc_ref, w1_v, w2_v, rz_v, sems):
        i = pl.program_id(0)

        @pl.when(i == 0)
        def _():
            cps = _load_gathered(chip_ref[0], w1_hbm, w1_own, lambda j: w1_v.at[j], sems.at[pl.ds(0, N_CHIP)])
            cps += _load_gathered(chip_ref[0], w2_hbm, w2_own, lambda j: w2_v.at[j], sems.at[pl.ds(N_CHIP, N_CHIP)])
            acc_ref[...] = jnp.zeros_like(acc_ref)
            for cp in cps:
                cp.wait()

        xt = x1_ref[...]
        shift2, scale2, gate2 = mod_ref[3:4, :], mod_ref[4:5, :], mod_ref[5:6, :]
        g2, gf = vd_ref[1:2, :], vd_ref[2:3, :]
        r2 = lax.rsqrt(jnp.mean(xt * xt, axis=-1, keepdims=True) + EPS)
        n2 = xt * r2
        h2b = (n2 * g2 * (1.0 + scale2) + shift2).astype(BF16)
        h2_ref[...] = h2b
        for j in range(N_CHIP):
            rz_v[j] = jnp.maximum(_dot(h2b, w1_v[j]), 0.0)
        mo = jnp.zeros((ts, D_MODEL), F32)
        for j in range(N_CHIP):
            rz = rz_v[j]
            actb = (rz * rz).astype(BF16)
            act_ref[:, j * FF_BLK:(j + 1) * FF_BLK] = actb
            mo = mo + _dot(actb, w2_v[j])
        x2 = xt + gate2 * mo
        r3 = lax.rsqrt(jnp.mean(x2 * x2, axis=-1, keepdims=True) + EPS)
        n3 = x2 * r3
        e = n3 * gf - tg_ref[...]
        loss = (0.5 / D_MODEL) * jnp.sum(_colsum(e * e), axis=1, keepdims=True)
        dy = e * (1.0 / D_MODEL)
        acc_ref[4:5, :] += _colsum(dy * n3)
        acc_ref[5:6, :] += jnp.broadcast_to(loss, (1, D_MODEL))
        dn3 = dy * gf
        dx2 = r3 * (dn3 - n3 * jnp.mean(dn3 * n3, axis=-1, keepdims=True))
        acc_ref[2:3, :] += _colsum(dx2 * mo)
        dmob = (dx2 * gate2).astype(BF16)
        dmo_ref[...] = dmob
        for j in range(N_CHIP):
            dz_ref[:, j * FF_BLK:(j + 1) * FF_BLK] = (_dot_nt(dmob, w2_v[j]) * (2.0 * rz_v[j])).astype(BF16)
        dh2 = jnp.zeros((ts, D_MODEL), F32)
        for j in range(N_CHIP):
            dh2 = dh2 + _dot_nt(dz_ref[:, j * FF_BLK:(j + 1) * FF_BLK], w1_v[j])
        acc_ref[1:2, :] += _colsum(dh2 * (n2 * g2))
        acc_ref[0:1, :] += _colsum(dh2)
        dhn2 = dh2 * (1.0 + scale2)
        acc_ref[3:4, :] += _colsum(dhn2 * n2)
        dn2 = dhn2 * g2
        dx1_ref[...] = dx2 + r2 * (dn2 - n2 * jnp.mean(dn2 * n2, axis=-1, keepdims=True))

    tile = lambda w: pl.BlockSpec((ts, w), lambda i: (i, 0))
    return pl.pallas_call(
        body, name="mlp_fwd_bwd", grid=(nt,),
        in_specs=[SMEM, tile(D_MODEL), tile(D_MODEL), _full((8, D_MODEL)), _full((8, D_MODEL)), ANY, ANY, ANY, ANY],
        out_specs=[tile(D_MODEL), tile(D_FF), tile(D_FF), tile(D_MODEL), tile(D_MODEL), _full((8, D_MODEL))],
        out_shape=[jax.ShapeDtypeStruct((s, D_MODEL), F32), jax.ShapeDtypeStruct((s, D_FF), BF16),
                   jax.ShapeDtypeStruct((s, D_FF), BF16), jax.ShapeDtypeStruct((s, D_MODEL), BF16),
                   jax.ShapeDtypeStruct((s, D_MODEL), BF16), jax.ShapeDtypeStruct((8, D_MODEL), F32)],
        scratch_shapes=[pltpu.VMEM((N_CHIP, D_MODEL, FF_BLK), BF16), pltpu.VMEM((N_CHIP, FF_BLK, D_MODEL), BF16),
                        pltpu.VMEM((N_CHIP, ts, FF_BLK), F32), pltpu.SemaphoreType.DMA((2 * N_CHIP,))],
        compiler_params=pltpu.CompilerParams(dimension_semantics=("arbitrary",), vmem_limit_bytes=VMEM_LIMIT),
    )(chip, x1, target, mod, vecd, *w1, *w2)


def _mix_bwd(chip, dx1, x, mixed, proj, hl, hb, ycat, mod, vecd, vecl, win, wout, gab, a64):
    s = x.shape[0]
    ts = TOKEN_TILE
    nt = s // ts
    hpt = ts // HALO

    def body(chip_ref, dx1_ref, x_ref, mixed_ref, proj_ref, projh_ref, hl_ref, hlh_ref, hb_ref, ycat_ref,
             mod_ref, vd_ref, vl_ref, win_hbm, win_own, wout_hbm, wout_own, gab_ref, a64_ref,
             gx_ref, accd_ref, accl_ref, gwin_hbm, gwout_hbm, ggate_hbm,
             win_ref, wout_ref, dproj_ref, dgb_ref, gwin_acc, gwout_acc, ggate_acc,
             ext_lx, ext_cv, ext_hl, ext_dxl, ext_dq, gbuf, gcar, acar, sems):
        i = pl.program_id(0)
        ri = nt - 1 - i

        @pl.when(i == 0)
        def _():
            gwin_acc[...] = jnp.zeros_like(gwin_acc)
            gwout_acc[...] = jnp.zeros_like(gwout_acc)
            ggate_acc[...] = jnp.zeros_like(ggate_acc)
            cps = _load_gathered(chip_ref[0], win_hbm, win_own, lambda j: win_ref.at[j], sems.at[pl.ds(0, N_CHIP)])
            cps += _load_gathered(chip_ref[0], wout_hbm, wout_own,
                                  lambda j: wout_ref.at[pl.ds(j * WOUT_BLK, WOUT_BLK), :],
                                  sems.at[pl.ds(N_CHIP, N_CHIP)])
            for cp in cps:
                cp.wait()
            accd_ref[...] = jnp.zeros_like(accd_ref)
            accl_ref[...] = jnp.zeros_like(accl_ref)
            ext_dxl[ts:ts + HALO, :] = jnp.zeros((HALO, D_LRU), F32)
            ext_dq[ts:ts + HALO, :] = jnp.zeros((HALO, D_LRU), F32)
            gcar[...] = jnp.zeros_like(gcar)
            acar[...] = jnp.zeros_like(acar)

        row = lax.broadcasted_iota(jnp.int32, (ts, D_LRU), 0)
        first_row = jnp.logical_and(row == 0, ri == 0)
        halo_on = jnp.where(ri == 0, 0.0, 1.0)
        shift1, scale1, gate1 = mod_ref[0:1, :], mod_ref[1:2, :], mod_ref[2:3, :]
        g1 = vd_ref[0:1, :]
        a64m = a64_ref[...]
        lg, cg = vl_ref[9:10, :], vl_ref[10:11, :]

        dx1 = dx1_ref[...]
        accd_ref[2:3, :] += _colsum(dx1 * mixed_ref[...])
        dmb = (dx1 * gate1).astype(BF16)
        gwout_acc[...] += _dot_tn(ycat_ref[...], dmb)
        dycat = _dot_nt(dmb, wout_ref[...])
        dyl = dycat[:, 0:512]
        dyv = dycat[:, 512:1024]

        u_ly = proj_ref[:, 512:1024]
        u_b = proj_ref[:, 1024:1536]
        u_c = proj_ref[:, 1536:2048]
        u_v = proj_ref[:, 2048:2560]
        ext_lx[0:HALO, :] = projh_ref[:, 0:512] * halo_on
        ext_lx[HALO:HALO + ts, :] = proj_ref[:, 0:512]
        xl = vl_ref[4:5, :] + vl_ref[0:1, :] * ext_lx[pl.ds(5, ts), :]
        for k in range(1, 4):
            xl = xl + vl_ref[k:k + 1, :] * ext_lx[pl.ds(5 + k, ts), :]
        xlb = xl.astype(BF16)
        sp = _softplus(vl_ref[8:9, :])
        r, ig, a, msq, mult = _lru_gates(xlb, gab_ref[...], vd_ref[3:4, :], sp, first_row)
        hl = hl_ref[...]
        ge, th = _gelu(u_ly)
        p = ge * hl
        rl = lax.rsqrt(_gmean(p * p, a64m) + EPS)
        nl = p * rl
        ext_cv[0:HALO, :] = projh_ref[:, 1536:2048] * projh_ref[:, 2048:2560] * halo_on
        ext_cv[HALO:HALO + ts, :] = u_c * u_v
        q = vl_ref[5:6, :] * ext_cv[pl.ds(6, ts), :]
        for k in range(1, 3):
            q = q + vl_ref[5 + k:6 + k, :] * ext_cv[pl.ds(6 + k, ts), :]
        yc = u_b * q
        rc = lax.rsqrt(_gmean(yc * yc, a64m) + EPS)
        nc = yc * rc

        accl_ref[9:10, :] += _colsum(dyl * nl)
        dnl = dyl * lg
        dp = rl * (dnl - nl * _gmean(dnl * nl, a64m))
        dproj_ref[:, 512:1024] = ((dp * hl) * _gelu_grad(u_ly, th)).astype(BF16)
        a_next = jnp.where(row == ts - 1, acar[0:1, :], pltpu.roll(a, ts - 1, 0))
        acum, gloc = _scan_rev(a_next, dp * ge, row)
        gbuf[...] = gloc + acum * gcar[0:1, :]
        gcar[0:1, :] = gbuf[0:1, :]
        ext_hl[0:HALO, :] = hlh_ref[...] * halo_on
        ext_hl[HALO:HALO + ts, :] = hl
        acar[...] = a[0:HALO, :]
        gt = gbuf[...]
        da = gt * ext_hl[pl.ds(HALO - 1, ts), :]
        dmult = gt * ig * xl
        di = gt * mult * xl
        dxl = gt * mult * ig
        dla = da * a - jnp.where(first_row, 0.0, dmult * a * a / msq)
        accl_ref[8:9, :] += _colsum(dla * ((-C_GATE) * r))
        dra = dla * ((-C_GATE) * sp) * r * (1.0 - r)
        dia = di * ig * (1.0 - ig)
        accd_ref[4:5, 0:D_LRU] += _colsum(dra)
        accd_ref[4:5, D_LRU:2 * D_LRU] += _colsum(dia)
        dgb_ref[:, 0:D_LRU] = dra.astype(BF16)
        dgb_ref[:, D_LRU:2 * D_LRU] = dia.astype(BF16)
        dxl = dxl + _dot_nt(dgb_ref[...], gab_ref[...])
        ggate_acc[...] += _dot_tn(xlb, dgb_ref[...])
        accl_ref[4:5, :] += _colsum(dxl)
        for k in range(4):
            accl_ref[k:k + 1, :] += _colsum(dxl * ext_lx[pl.ds(5 + k, ts), :])
        ext_dxl[0:ts, :] = dxl
        du_lx = vl_ref[0:1, :] * ext_dxl[pl.ds(3, ts), :]
        for k in range(1, 4):
            du_lx = du_lx + vl_ref[k:k + 1, :] * ext_dxl[pl.ds(3 - k, ts), :]
        ext_dxl[ts:ts + HALO, :] = ext_dxl[0:HALO, :]
        dproj_ref[:, 0:512] = du_lx.astype(BF16)

        accl_ref[10:11, :] += _colsum(dyv * nc)
        dnc = dyv * cg
        dyc = rc * (dnc - nc * _gmean(dnc * nc, a64m))
        dproj_ref[:, 1024:1536] = (dyc * q).astype(BF16)
        dq = dyc * u_b
        for k in range(3):
            accl_ref[5 + k:6 + k, :] += _colsum(dq * ext_cv[pl.ds(6 + k, ts), :])
        ext_dq[0:ts, :] = dq
        dcv = vl_ref[5:6, :] * ext_dq[pl.ds(2, ts), :]
        for k in range(1, 3):
            dcv = dcv + vl_ref[5 + k:6 + k, :] * ext_dq[pl.ds(2 - k, ts), :]
        ext_dq[ts:ts + HALO, :] = ext_dq[0:HALO, :]
        dproj_ref[:, 1536:2048] = (dcv * u_v).astype(BF16)
        dproj_ref[:, 2048:2560] = (dcv * u_c).astype(BF16)

        dh = _dot_nt(dproj_ref[:, 0:WIN_BLK], win_ref[0])
        for j in range(1, N_CHIP):
            dh = dh + _dot_nt(dproj_ref[:, j * WIN_BLK:(j + 1) * WIN_BLK], win_ref[j])
        for j in range(N_CHIP):
            gwin_acc[j] += _dot_tn(hb_ref[...], dproj_ref[:, j * WIN_BLK:(j + 1) * WIN_BLK])
        xt = x_ref[...]
        r1 = lax.rsqrt(jnp.mean(xt * xt, axis=-1, keepdims=True) + EPS)
        n1 = xt * r1
        accd_ref[1:2, :] += _colsum(dh * (n1 * g1))
        accd_ref[0:1, :] += _colsum(dh)
        dhn1 = dh * (1.0 + scale1)
        accd_ref[3:4, :] += _colsum(dhn1 * n1)
        dn1 = dhn1 * g1
        gx_ref[...] = dx1 + r1 * (dn1 - n1 * jnp.mean(dn1 * n1, axis=-1, keepdims=True))

        @pl.when(i == nt - 1)
        def _():
            outs = [pltpu.make_async_copy(acc, dst, sems.at[k]) for k, (acc, dst) in enumerate(
                ((gwin_acc, gwin_hbm), (gwout_acc, gwout_hbm), (ggate_acc, ggate_hbm)))]
            for cp in outs:
                cp.start()
            for cp in outs:
                cp.wait()

    tile = lambda w: pl.BlockSpec((ts, w), lambda i: (nt - 1 - i, 0))
    halo = lambda w: pl.BlockSpec((HALO, w), lambda i: (jnp.maximum((nt - 1 - i) * hpt - 1, 0), 0))
    ext = pltpu.VMEM((ts + HALO, D_LRU), F32)
    return pl.pallas_call(
        body, name="mix_bwd", grid=(nt,),
        in_specs=[SMEM, tile(D_MODEL), tile(D_MODEL), tile(D_MODEL), tile(D_IN), halo(D_IN), tile(D_LRU), halo(D_LRU),
                  tile(D_MODEL), tile(D_MODEL), _full((8, D_MODEL)), _full((8, D_MODEL)), _full((16, D_LRU)),
                  ANY, ANY, ANY, ANY, _full((D_LRU, 2 * D_LRU), True), _full((D_LRU, D_LRU), True)],
        out_specs=[tile(D_MODEL), _full((8, D_MODEL)), _full((16, D_LRU)), ANY, ANY, ANY],
        out_shape=[jax.ShapeDtypeStruct((s, D_MODEL), F32),
                   jax.ShapeDtypeStruct((8, D_MODEL), F32), jax.ShapeDtypeStruct((16, D_LRU), F32),
                   jax.ShapeDtypeStruct((N_CHIP, D_MODEL, WIN_BLK), F32), jax.ShapeDtypeStruct((D_MODEL, D_MODEL), F32),
                   jax.ShapeDtypeStruct((D_LRU, 2 * D_LRU), F32)],
        scratch_shapes=[pltpu.VMEM((N_CHIP, D_MODEL, WIN_BLK), BF16), pltpu.VMEM((D_MODEL, D_MODEL), BF16),
                        pltpu.VMEM((ts, D_IN), BF16), pltpu.VMEM((ts, 2 * D_LRU), BF16),
                        pltpu.VMEM((N_CHIP, D_MODEL, WIN_BLK), F32), pltpu.VMEM((D_MODEL, D_MODEL), F32),
                        pltpu.VMEM((D_LRU, 2 * D_LRU), F32),
                        ext, ext, ext, ext, ext, pltpu.VMEM((ts, D_LRU), F32),
                        pltpu.VMEM((HALO, D_LRU), F32), pltpu.VMEM((HALO, D_LRU), F32),
                        pltpu.SemaphoreType.DMA((2 * N_CHIP,))],
        compiler_params=pltpu.CompilerParams(dimension_semantics=("arbitrary",), vmem_limit_bytes=VMEM_LIMIT),
    )(chip, dx1, x, mixed, proj, proj, hl, hl, hb, ycat, mod, vecd, vecl, *win, *wout, gab, a64)


def _wgrad_mlp(collective_id, h2b, dz, act, dmo):
    s = h2b.shape[0]
    nstep = 2 * N_CHIP
    half = FF_BLK // 2

    def body(h2_ref, dz_ref, act_ref, dmo_ref, own_hbm, recv_hbm, buf, keep_sems, send_sems, recv_sems):
        j = pl.program_id(0)
        x, y, c, _ = _position()

        def copies(jj):
            slot = jj % 2
            keep = pltpu.make_async_copy(buf.at[slot, pl.ds(c * half, half), :], own_hbm.at[jj], keep_sems.at[slot])
            give = pltpu.make_async_remote_copy(
                src_ref=buf.at[slot, pl.ds((1 - c) * half, half), :], dst_ref=recv_hbm.at[jj],
                send_sem=send_sems.at[slot], recv_sem=recv_sems.at[jj],
                device_id=(x, y, 1 - c), device_id_type=MESH)
            return keep, give

        @pl.when(j == 0)
        def _():
            pl.semaphore_signal(pltpu.get_barrier_semaphore(), inc=1, device_id=(x, y, 1 - c), device_id_type=MESH)

        @pl.when(j >= 2)
        def _():
            keep, give = copies(j - 2)
            keep.wait()
            give.wait_send()

        @pl.when(j < N_CHIP)
        def _():
            buf[j % 2] = _dot_tn(act_ref[...], dmo_ref[...]).astype(BF16)

        @pl.when(j >= N_CHIP)
        def _():
            buf[j % 2] = _dot_tn(h2_ref[...], dz_ref[...]).astype(BF16)

        @pl.when(j == 0)
        def _():
            pl.semaphore_wait(pltpu.get_barrier_semaphore(), 1)

        keep, give = copies(j)
        keep.start()
        give.start()

        @pl.when(j == nstep - 1)
        def _():
            for jj in (nstep - 2, nstep - 1):
                keep, give = copies(jj)
                keep.wait()
                give.wait_send()
            for jj in range(nstep):
                copies(jj)[1].wait_recv()

    sds = jax.ShapeDtypeStruct((nstep, half, D_MODEL), BF16)
    whole = pl.BlockSpec((s, D_MODEL), lambda j: (0, 0))
    return pl.pallas_call(
        body, name="wgrad_mlp", grid=(nstep,),
        in_specs=[whole, pl.BlockSpec((s, FF_BLK), lambda j: (0, jnp.maximum(j - N_CHIP, 0))),
                  pl.BlockSpec((s, FF_BLK), lambda j: (0, jnp.minimum(j, N_CHIP - 1))), whole],
        out_specs=[ANY, ANY], out_shape=[sds, sds],
        scratch_shapes=[pltpu.VMEM((2, FF_BLK, D_MODEL), BF16), pltpu.SemaphoreType.DMA((2,)),
                        pltpu.SemaphoreType.DMA((2,)), pltpu.SemaphoreType.DMA((nstep,))],
        compiler_params=pltpu.CompilerParams(dimension_semantics=("arbitrary",), vmem_limit_bytes=VMEM_LIMIT,
                                             collective_id=collective_id),
    )(h2b, dz, act, dmo)


def _mod_matmul(c_all, ada_w_loc):
    n = ada_w_loc.shape[1]
    cb = 512

    def body(c_ref, w_ref, o_ref):
        c = c_ref[...]
        sc = c * jax.nn.sigmoid(c)
        o_ref[...] = _dot(sc.astype(BF16), w_ref[...].astype(BF16))

    return pl.pallas_call(
        body, name="mod_matmul", grid=(n // cb,),
        in_specs=[_full((8, D_MODEL)), pl.BlockSpec((D_MODEL, cb), lambda j: (0, j))],
        out_specs=pl.BlockSpec((8, cb), lambda j: (0, j)),
        out_shape=jax.ShapeDtypeStruct((8, n), F32),
        compiler_params=pltpu.CompilerParams(dimension_semantics=("arbitrary",), vmem_limit_bytes=VMEM_LIMIT),
    )(c_all, ada_w_loc)


def _adam_math(w, g, m, v):
    m = ADAM_B1 * m + (1.0 - ADAM_B1) * g
    v = ADAM_B2 * v + (1.0 - ADAM_B2) * (g * g)
    m_hat = m / (1.0 - ADAM_B1 ** ADAM_STEP)
    v_hat = v / (1.0 - ADAM_B2 ** ADAM_STEP)
    delta = (-ADAM_LR) * (m_hat / (jnp.sqrt(v_hat) + ADAM_EPS) + ADAM_WD * w)
    return delta, m, v


def _adam(name, core, shards):
    n = len(shards)
    r, c = shards[0][0].shape
    half = r // 2
    rb = min(half, 128)
    nh = half // rb

    def body(core_ref, *refs):
        ins, outs = refs[:5 * n], refs[5 * n:]
        mine = (pl.program_id(0) // nh) == core_ref[0]
        for k in range(n):
            w_ref, go_ref, gs_ref, m_ref, v_ref = ins[5 * k:5 * k + 5]
            g_ref, d_ref, mo_ref, vo_ref = outs[4 * k:4 * k + 4]
            g = jnp.where(mine, go_ref[...], gs_ref[...])
            g_ref[...] = g
            d_ref[...], mo_ref[...], vo_ref[...] = _adam_math(w_ref[...], g, m_ref[...], v_ref[...])

    spec = pl.BlockSpec((rb, c), lambda i, core_ref: (i, 0))
    hspec = pl.BlockSpec((rb, c), lambda i, core_ref: (i % nh, 0))
    sds = jax.ShapeDtypeStruct((r, c), F32)
    res = pl.pallas_call(
        body, name=name,
        grid_spec=pltpu.PrefetchScalarGridSpec(
            num_scalar_prefetch=1, grid=(r // rb,),
            in_specs=[spec, hspec, hspec, spec, spec] * n, out_specs=[spec] * (4 * n)),
        out_shape=[sds] * (4 * n),
        compiler_params=pltpu.CompilerParams(dimension_semantics=("arbitrary",), vmem_limit_bytes=VMEM_LIMIT),
    )(core, *[t for s in shards for t in s])
    return [res[4 * k:4 * k + 4] for k in range(n)]


def _ada_grad_adam(sct, dmod_loc, w, m, v):
    r, c = w.shape
    rb = 128

    def body(s_ref, dm_ref, w_ref, m_ref, v_ref, g_ref, d_ref, mo_ref, vo_ref):
        g = s_ref[:, 0:1] * dm_ref[0:1, :]
        for b in range(1, 8):
            g = g + s_ref[:, b:b + 1] * dm_ref[b:b + 1, :]
        g_ref[...] = g
        d_ref[...], mo_ref[...], vo_ref[...] = _adam_math(w_ref[...], g, m_ref[...], v_ref[...])

    spec = pl.BlockSpec((rb, c), lambda i: (i, 0))
    sds = jax.ShapeDtypeStruct((r, c), F32)
    return pl.pallas_call(
        body, name="ada_grad_adam", grid=(r // rb,),
        in_specs=[pl.BlockSpec((rb, 8), lambda i: (i, 0)), _full((8, c)), spec, spec, spec],
        out_specs=[spec] * 4, out_shape=[sds] * 4,
        compiler_params=pltpu.CompilerParams(dimension_semantics=("arbitrary",), vmem_limit_bytes=VMEM_LIMIT),
    )(sct, dmod_loc, w, m, v)


def _position():
    x, y, c = lax.axis_index("x"), lax.axis_index("y"), lax.axis_index("c")
    chips = [(1 - x, y), (x, 1 - y), (1 - x, 1 - y)]
    return x, y, c, chips


def _ag8_run(ins, outs, send_sems, recv_sems, local_sems):
    na = len(ins)
    x, y, c, chips = _position()
    me, sibling = (x, y, c), (x, y, 1 - c)
    first, passed, local = [], [], []
    for a in range(na):
        m_per = ins[a].shape[0]

        def rows(px, py, pc, a=a, m_per=m_per):
            return outs[a].at[pl.ds((4 * px + 2 * py + pc) * m_per, m_per), :]

        def copy(k, block, to, src=None, a=a, rows=rows):
            return pltpu.make_async_remote_copy(
                src_ref=rows(*block) if src is None else src, dst_ref=rows(*block),
                send_sem=send_sems.at[7 * a + k], recv_sem=recv_sems.at[7 * a + k],
                device_id=to, device_id_type=MESH)

        mine = pltpu.make_async_copy(ins[a], rows(*me), local_sems.at[a])
        mine.start()
        local.append(mine)
        f = [copy(0, me, sibling, src=ins[a])]
        f += [copy(1 + j, me, (*chip, c), src=ins[a]) for j, chip in enumerate(chips)]
        for cp in f:
            cp.start()
        first.append((f, copy))
    for a in range(na):
        f, copy = first[a]
        p = [copy(4 + j, (*chip, c), sibling) for j, chip in enumerate(chips)]
        for j, chip in enumerate(chips):
            copy(1 + j, (*chip, c), me).wait_recv()
            p[j].start()
        passed.append(p)
    for a in range(na):
        f, copy = first[a]
        copy(0, sibling, me).wait_recv()
        for j, chip in enumerate(chips):
            copy(4 + j, (*chip, 1 - c), me).wait_recv()
        for cp in f + passed[a]:
            cp.wait_send()
        local[a].wait()


def _allgather8_seq(name, collective_id, arrs):
    na = len(arrs)
    hbm = pltpu.MemorySpace.HBM
    ins = [jax.new_ref(a, memory_space=hbm) for a in arrs]
    outs = [jax.empty_ref(jax.ShapeDtypeStruct((8 * a.shape[0], a.shape[1]), a.dtype), memory_space=hbm) for a in arrs]

    @pl.kernel(mesh=plsc.ScalarSubcoreMesh(axis_name="sequencer", num_cores=1), name=name,
               scratch_types=(pltpu.SemaphoreType.DMA((7 * na,)), pltpu.SemaphoreType.DMA((7 * na,)),
                              pltpu.SemaphoreType.DMA((na,))),
               compiler_params=pltpu.CompilerParams(collective_id=collective_id))
    def launch(send_sems, recv_sems, local_sems):
        x, y, c, chips = _position()
        peers = [(x, y, 1 - c)] + [(*chip, c) for chip in chips]
        barrier = pltpu.get_barrier_semaphore()
        for peer in peers:
            pl.semaphore_signal(barrier, inc=1, device_id=peer, device_id_type=MESH)
        pl.semaphore_wait(barrier, len(peers))
        _ag8_run(ins, outs, send_sems, recv_sems, local_sems)

    launch()
    return [o[...] for o in outs]


def _allgather8(name, arrs):
    na = len(arrs)

    def body(*refs):
        _ag8_run(refs[:na], refs[na:2 * na], *refs[2 * na:])

    return pl.pallas_call(
        body, name=name,
        out_shape=[jax.ShapeDtypeStruct((8 * a.shape[0], a.shape[1]), a.dtype) for a in arrs],
        in_specs=[VMEM] * na, out_specs=[VMEM] * na,
        scratch_shapes=[pltpu.SemaphoreType.DMA((7 * na,)), pltpu.SemaphoreType.DMA((7 * na,)),
                        pltpu.SemaphoreType.DMA((na,))],
        compiler_params=pltpu.CompilerParams(vmem_limit_bytes=VMEM_LIMIT),
    )(*arrs)


AG_SEMS = 7


def _ag_copies(ins, outs, send_sems, recv_sems):
    x, y, c, chips = _position()
    sibling = (x, y, 1 - c)
    xn, yn, dg = [2 * chip[0] + chip[1] for chip in chips]
    to_x, to_y = (1 - x, y, c), (x, 1 - y, c)
    res = []
    for a in range(len(ins)):
        half = ins[a].shape[0] // 2
        quarter = half // 2

        def copy(k, dst, to, src=None, a=a):
            return pltpu.make_async_remote_copy(
                src_ref=dst if src is None else src, dst_ref=dst,
                send_sem=send_sems.at[AG_SEMS * a + k], recv_sem=recv_sems.at[AG_SEMS * a + k],
                device_id=to, device_id_type=MESH)

        def rows(chip, pc, q=None, a=a, half=half, quarter=quarter):
            if q is None:
                return outs[a].at[chip, pl.ds(pc * half, half), :]
            return outs[a].at[chip, pl.ds(pc * half + q * quarter, quarter), :]

        own = ins[a].at[pl.ds(c * half, half), :]
        mine = rows(2 * x + y, c)
        res.append(dict(
            sends=[copy(0, mine, to_x, src=own), copy(1, mine, to_y, src=own)],
            from_x=copy(0, rows(xn, c), to_x), from_y=copy(1, rows(yn, c), to_y),
            relay_y=copy(2, rows(xn, c, 0), to_y), relay_x=copy(3, rows(yn, c, 1), to_x),
            from_y_relay=copy(2, rows(dg, c, 0), to_y), from_x_relay=copy(3, rows(dg, c, 1), to_x),
            pass_on=[copy(4, rows(xn, c), sibling), copy(5, rows(yn, c), sibling), copy(6, rows(dg, c), sibling)],
            from_sibling=[copy(4, rows(xn, 1 - c), sibling), copy(5, rows(yn, 1 - c), sibling),
                          copy(6, rows(dg, 1 - c), sibling)]))
    return res


def _ag_start(ins, outs, send_sems, recv_sems):
    for cps in _ag_copies(ins, outs, send_sems, recv_sems):
        for cp in cps["sends"]:
            cp.start()


def _ag_relay(ins, outs, send_sems, recv_sems, which):
    copies = _ag_copies(ins, outs, send_sems, recv_sems)
    for a in which:
        cps = copies[a]
        cps["from_x"].wait_recv()
        cps["relay_y"].start()
        cps["pass_on"][0].start()
        cps["from_y"].wait_recv()
        cps["relay_x"].start()
        cps["pass_on"][1].start()


def _ag_complete(ins, outs, send_sems, recv_sems):
    copies = _ag_copies(ins, outs, send_sems, recv_sems)
    for cps in copies:
        cps["from_y_relay"].wait_recv()
        cps["from_x_relay"].wait_recv()
        cps["pass_on"][2].start()
    for cps in copies:
        for cp in cps["from_sibling"]:
            cp.wait_recv()
        for cp in cps["sends"] + [cps["relay_y"], cps["relay_x"]] + cps["pass_on"]:
            cp.wait_send()


def _ag_finish(ins, outs, send_sems, recv_sems):
    _ag_relay(ins, outs, send_sems, recv_sems, range(len(ins)))
    _ag_complete(ins, outs, send_sems, recv_sems)


def _allgather_weights(name, collective_id, shards):
    na = len(shards)
    hbm = pltpu.MemorySpace.HBM
    ins = [jax.new_ref(s, memory_space=hbm) for s in shards]
    outs = [jax.empty_ref(jax.ShapeDtypeStruct((N_CHIP,) + s.shape, s.dtype), memory_space=hbm) for s in shards]

    @pl.kernel(mesh=plsc.ScalarSubcoreMesh(axis_name="sequencer", num_cores=1), name=name,
               scratch_types=(pltpu.SemaphoreType.DMA((AG_SEMS * na,)), pltpu.SemaphoreType.DMA((AG_SEMS * na,))),
               compiler_params=pltpu.CompilerParams(collective_id=collective_id))
    def launch(send_sems, recv_sems):
        x, y, c, _ = _position()
        peers = [(1 - x, y, c), (x, 1 - y, c), (x, y, 1 - c)]
        barrier = pltpu.get_barrier_semaphore()
        for peer in peers:
            pl.semaphore_signal(barrier, inc=1, device_id=peer, device_id_type=MESH)
        pl.semaphore_wait(barrier, len(peers))
        _ag_start(ins, outs, send_sems, recv_sems)
        _ag_finish(ins, outs, send_sems, recv_sems)

    launch()
    return [o[...] for o in outs]


def _sibling_swap(name, arrs, split_rows, collective_id=None):
    na = len(arrs)
    shapes = [jax.ShapeDtypeStruct((a.shape[0], a.shape[1] // 2, a.shape[2]) if split_rows else a.shape, a.dtype)
              for a in arrs]

    def run(ins, outs, send_sems, recv_sems):
        x, y, c, _ = _position()
        cps = []
        for a in range(na):
            src = ins[a]
            if split_rows:
                half = src.shape[1] // 2
                src = src.at[:, pl.ds((1 - c) * half, half), :]
            cp = pltpu.make_async_remote_copy(
                src_ref=src, dst_ref=outs[a], send_sem=send_sems.at[a], recv_sem=recv_sems.at[a],
                device_id=(x, y, 1 - c), device_id_type=MESH)
            cp.start()
            cps.append(cp)
        for cp in cps:
            cp.wait()

    sems = (pltpu.SemaphoreType.DMA((na,)), pltpu.SemaphoreType.DMA((na,)))
    if collective_id is None:
        return pl.pallas_call(
            lambda *refs: run(refs[:na], refs[na:2 * na], *refs[2 * na:]), name=name, out_shape=shapes,
            in_specs=[ANY] * na, out_specs=[ANY] * na, scratch_shapes=list(sems))(*arrs)

    hbm = pltpu.MemorySpace.HBM
    ins = [jax.new_ref(a, memory_space=hbm) for a in arrs]
    outs = [jax.empty_ref(s, memory_space=hbm) for s in shapes]

    @pl.kernel(mesh=plsc.ScalarSubcoreMesh(axis_name="sequencer", num_cores=1), name=name, scratch_types=sems,
               compiler_params=pltpu.CompilerParams(collective_id=collective_id))
    def launch(send_sems, recv_sems):
        x, y, c, _ = _position()
        barrier = pltpu.get_barrier_semaphore()
        pl.semaphore_signal(barrier, inc=1, device_id=(x, y, 1 - c), device_id_type=MESH)
        pl.semaphore_wait(barrier, 1)
        run(ins, outs, send_sems, recv_sems)

    launch()
    return [o[...] for o in outs]


def _xchg_copies(ins, outs, send_sems, recv_sems):
    x, y, c, chips = _position()
    return [pltpu.make_async_remote_copy(
        src_ref=ins[a].at[2 * chip[0] + chip[1]], dst_ref=outs[a].at[j],
        send_sem=send_sems.at[3 * a + j], recv_sem=recv_sems.at[3 * a + j],
        device_id=(*chip, c), device_id_type=MESH) for a in range(len(ins)) for j, chip in enumerate(chips)]


def _exchange_chips(name, collective_id, parts):
    na = len(parts)
    hbm = pltpu.MemorySpace.HBM
    ins = [jax.new_ref(p, memory_space=hbm) for p in parts]
    outs = [jax.empty_ref(jax.ShapeDtypeStruct((3,) + p.shape[1:], p.dtype), memory_space=hbm) for p in parts]

    @pl.kernel(mesh=plsc.ScalarSubcoreMesh(axis_name="sequencer", num_cores=1), name=name,
               scratch_types=(pltpu.SemaphoreType.DMA((3 * na,)), pltpu.SemaphoreType.DMA((3 * na,))),
               compiler_params=pltpu.CompilerParams(collective_id=collective_id))
    def launch(send_sems, recv_sems):
        x, y, c, chips = _position()
        barrier = pltpu.get_barrier_semaphore()
        for chip in chips:
            pl.semaphore_signal(barrier, inc=1, device_id=(*chip, c), device_id_type=MESH)
        pl.semaphore_wait(barrier, len(chips))
        for cp in _xchg_copies(ins, outs, send_sems, recv_sems):
            cp.start()
        for cp in _xchg_copies(ins, outs, send_sems, recv_sems):
            cp.wait()

    launch()
    return [q[...] for q in outs]


def _add_sibling(name, grad, recv, core, after=()):
    _, r, c = grad.shape
    half = r // 2
    rb = min(half, 256)
    nrb = half // rb

    def body(core_ref, g_ref, r_ref, *refs):
        refs[-1][...] = (g_ref[...].astype(F32) + r_ref[...].astype(F32)).astype(BF16)

    return pl.pallas_call(
        body, name=name,
        grid_spec=pltpu.PrefetchScalarGridSpec(
            num_scalar_prefetch=1, grid=(N_CHIP, nrb),
            in_specs=[pl.BlockSpec((1, rb, c), lambda j, i, core_ref: (j, core_ref[0] * nrb + i, 0)),
                      pl.BlockSpec((1, rb, c), lambda j, i, core_ref: (j, i, 0))] + [ANY] * len(after),
            out_specs=pl.BlockSpec((1, rb, c), lambda j, i, core_ref: (j, i, 0))),
        out_shape=jax.ShapeDtypeStruct((N_CHIP, half, c), BF16),
        compiler_params=pltpu.CompilerParams(dimension_semantics=("arbitrary", "arbitrary"),
                                             vmem_limit_bytes=VMEM_LIMIT),
    )(core, grad, recv, *after)


def _add_halves(name, own, recv):
    n = own.shape[0] // N_CHIP
    _, half, c = own.shape
    rb = min(half, 256)

    def body(*refs):
        for k in range(n):
            refs[2 * n + k][...] = (refs[k][...].astype(F32) + refs[n + k][...].astype(F32)).astype(BF16)

    group = [pl.BlockSpec((1, rb, c), lambda j, i, k=k: (N_CHIP * k + j, i, 0)) for k in range(n)]
    spec = pl.BlockSpec((1, rb, c), lambda j, i: (j, i, 0))
    return pl.pallas_call(
        body, name=name, grid=(N_CHIP, half // rb), in_specs=group + group, out_specs=[spec] * n,
        out_shape=[jax.ShapeDtypeStruct((N_CHIP, half, c), BF16)] * n,
        compiler_params=pltpu.CompilerParams(dimension_semantics=("arbitrary", "arbitrary"),
                                             vmem_limit_bytes=VMEM_LIMIT),
    )(*([own] * n + [recv] * n))


def _add_chips(name, chip, p, q, after=()):
    _, half, c = q.shape
    rb = min(half, 256)

    def body(chip_ref, p_ref, q_ref, *refs):
        acc = p_ref[0].astype(F32)
        for j in range(3):
            acc = acc + q_ref[j].astype(F32)
        refs[-1][...] = acc

    return pl.pallas_call(
        body, name=name,
        grid_spec=pltpu.PrefetchScalarGridSpec(
            num_scalar_prefetch=1, grid=(half // rb,),
            in_specs=[pl.BlockSpec((1, rb, c), lambda i, chip_ref: (chip_ref[0], i, 0)),
                      pl.BlockSpec((3, rb, c), lambda i, chip_ref: (0, i, 0))] + [ANY] * len(after),
            out_specs=pl.BlockSpec((rb, c), lambda i, chip_ref: (i, 0))),
        out_shape=jax.ShapeDtypeStruct((half, c), F32),
        compiler_params=pltpu.CompilerParams(dimension_semantics=("arbitrary",), vmem_limit_bytes=VMEM_LIMIT),
    )(chip, p, q, *after)


def _small_update(gad, gam, gl, gg, mychip, params):
    names = ["ada_b", "norm1_g", "lru_conv_b", "gate_a_w", "gate_a_b", "gate_x_w", "gate_x_b", "a_param",
             "lru_conv_w", "short_conv_w", "lru_out_g", "conv_out_g", "norm2_g", "final_g"]
    flat = [t for n in names for t in params[n]]
    nin = len(flat)

    def body(chip_ref, gad_ref, gam_ref, gl_ref, gg_ref, *refs):
        ins = {n: refs[3 * k:3 * k + 3] for k, n in enumerate(names)}
        outs = {n: refs[nin + 4 * k:nin + 4 * k + 4] for k, n in enumerate(names)}
        loss_ref, dmod_ref = refs[nin + 4 * len(names):nin + 4 * len(names) + 2]

        def dsum(ref, lo, n):
            per = ref.shape[0] // 8
            acc = ref[lo:lo + n, :].astype(F32)
            for dev in range(1, 8):
                acc = acc + ref[dev * per + lo:dev * per + lo + n, :].astype(F32)
            return acc

        def update(n, g):
            w_ref, m_ref, v_ref = ins[n]
            g_ref, d_ref, mo_ref, vo_ref = outs[n]
            g_ref[...] = g
            d_ref[...], mo_ref[...], vo_ref[...] = _adam_math(w_ref[...], g, m_ref[...], v_ref[...])

        d, dm, l, lw = refs[-4:]
        d[...] = dsum(gad_ref, 0, 8)
        dm[...] = dsum(gam_ref, 0, 8)
        l[...] = dsum(gl_ref, 0, 16)
        for dev in range(8):
            for k in range(3):
                dmod_ref[dev:dev + 1, k * D_MODEL:(k + 1) * D_MODEL] = gad_ref[dev * 8 + k:dev * 8 + k + 1, :]
                dmod_ref[dev:dev + 1, (3 + k) * D_MODEL:(4 + k) * D_MODEL] = gam_ref[dev * 8 + k:dev * 8 + k + 1, :]
        w_ref, m_ref, v_ref = ins["ada_b"]
        g_ref, d_ref, mo_ref, vo_ref = outs["ada_b"]
        for k in range(3):
            g_ref[:, k * D_MODEL:(k + 1) * D_MODEL] = d[k:k + 1, :]
            g_ref[:, (3 + k) * D_MODEL:(4 + k) * D_MODEL] = dm[k:k + 1, :]
        d_ref[...], mo_ref[...], vo_ref[...] = _adam_math(w_ref[...], g_ref[...], m_ref[...], v_ref[...])
        update("norm1_g", d[3:4, :])
        update("norm2_g", dm[3:4, :])
        update("final_g", dm[4:5, :])
        update("gate_a_b", d[4:5, 0:D_LRU])
        update("gate_x_b", d[4:5, D_LRU:2 * D_LRU])
        update("lru_conv_b", l[4:5, :])
        update("a_param", l[8:9, :] * jax.nn.sigmoid(ins["a_param"][0][...]))
        update("lru_out_g", l[9:10, :])
        update("conv_out_g", l[10:11, :])
        loss_ref[...] = jnp.broadcast_to(dm[5:6, 0:128], (8, 128))
        chip = chip_ref[0]
        acc = jnp.zeros((8, 128), F32)
        for j in range(N_CHIP):
            acc = acc + jnp.where(chip == j, l[0:8, j * 128:(j + 1) * 128], 0.0)
        lw[...] = acc
        update("lru_conv_w", lw[0:4, :])
        update("short_conv_w", lw[5:8, :])
        gates = dsum(gg_ref, 0, D_LRU)
        update("gate_a_w", gates[:, 0:HEAD])
        update("gate_x_w", gates[:, HEAD:2 * HEAD])

    out_shape = []
    for n in names:
        out_shape += [jax.ShapeDtypeStruct(params[n][0].shape, F32)] * 4
    out_shape += [jax.ShapeDtypeStruct((8, 128), F32), jax.ShapeDtypeStruct((8, 6 * D_MODEL), F32)]
    res = pl.pallas_call(
        body, name="small_update", out_shape=out_shape,
        in_specs=[SMEM] + [VMEM] * (4 + nin),
        out_specs=[VMEM] * len(out_shape),
        scratch_shapes=[pltpu.VMEM((8, D_MODEL), F32), pltpu.VMEM((8, D_MODEL), F32), pltpu.VMEM((16, D_LRU), F32),
                        pltpu.VMEM((8, 128), F32)],
        compiler_params=pltpu.CompilerParams(vmem_limit_bytes=VMEM_LIMIT),
    )(mychip, gad, gam, gl, gg, *flat)
    per = {n: res[4 * k:4 * k + 4] for k, n in enumerate(names)}
    return per, res[-2], res[-1]


def _block_diag(w):
    eye = jnp.eye(8, dtype=w.dtype)
    return (eye[:, None, :, None] * w[:, :, None, :]).reshape(8 * HEAD, 8 * HEAD)


def _diag_blocks(g):
    return jnp.concatenate([g[h * HEAD:(h + 1) * HEAD, h * HEAD:(h + 1) * HEAD] for h in range(8)], axis=0)


def kernel(x, c, ada_w, ada_b, norm1_g, w_in, lru_conv_w, lru_conv_b, gate_a_w, gate_a_b, gate_x_w, gate_x_b, a_param, short_conv_w, lru_out_g, conv_out_g, w_out, norm2_g, w_mlp1, w_mlp2, final_g, loss_target, m_ada_w, m_ada_b, m_norm1_g, m_w_in, m_lru_conv_w, m_lru_conv_b, m_gate_a_w, m_gate_a_b, m_gate_x_w, m_gate_x_b, m_a_param, m_short_conv_w, m_lru_out_g, m_conv_out_g, m_w_out, m_norm2_g, m_w_mlp1, m_w_mlp2, m_final_g, v_ada_w, v_ada_b, v_norm1_g, v_w_in, v_lru_conv_w, v_lru_conv_b, v_gate_a_w, v_gate_a_b, v_gate_x_w, v_gate_x_b, v_a_param, v_short_conv_w, v_lru_out_g, v_conv_out_g, v_w_out, v_norm2_g, v_w_mlp1, v_w_mlp2, v_final_g):
    xi, yi, ci = lax.axis_index("x"), lax.axis_index("y"), lax.axis_index("c")
    mychip = 2 * xi + yi
    me = 4 * xi + 2 * yi + ci

    own_in, own_out = w_in[0].astype(BF16), w_out[0].astype(BF16)
    win_all, wout_all = _allgather_weights("allgather_mixer_weights", 1, [own_in, own_out])
    own_w1, own_w2 = w_mlp1[0].astype(BF16), w_mlp2[0].astype(BF16)
    w1_all, w2_all = _allgather_weights("allgather_mlp_weights", 2, [own_w1, own_w2])

    c_blk = jnp.zeros((8, D_MODEL), F32).at[0:1].set(c)
    cw_blk = jnp.zeros((8, 128), F32).at[0:4].set(lru_conv_w[0]).at[4:7].set(short_conv_w[0])
    c_g, cw_g = _allgather8("allgather_cond", [c_blk, cw_blk])
    c_all = c_g.reshape(8, 8, D_MODEL)[:, 0]
    cw_g = cw_g.reshape(4, 2, 8, 128)[:, 0]
    lcw = cw_g[:, 0:4].transpose(1, 0, 2).reshape(4, D_LRU)
    scw = cw_g[:, 4:7].transpose(1, 0, 2).reshape(3, D_LRU)

    mod_loc = _mod_matmul(c_all, ada_w[0])
    (mod_g,) = _allgather8("allgather_mod", [mod_loc])
    mod_all = mod_g.reshape(4, 2, 8, 6 * D_MODEL // 4)[:, 0].transpose(1, 0, 2).reshape(8, 6 * D_MODEL) + ada_b
    mod_pad = jnp.pad(mod_all.reshape(8, 6, D_MODEL), ((0, 0), (0, 2), (0, 0)))
    mod = lax.dynamic_slice_in_dim(mod_pad, me, 1, axis=0).reshape(8, D_MODEL)

    win, wout = (win_all, own_in), (wout_all, own_out)
    chip = mychip.reshape(1).astype(jnp.int32)
    core = ci.reshape(1).astype(jnp.int32)

    vecd = jnp.concatenate([norm1_g, norm2_g, final_g[None, :], jnp.concatenate([gate_a_b, gate_x_b], axis=1),
                            jnp.zeros((4, D_MODEL), F32)], axis=0)
    vecl = jnp.concatenate([lcw, lru_conv_b, scw, a_param, lru_out_g, conv_out_g, jnp.zeros((5, D_LRU), F32)], axis=0)
    gab = jnp.concatenate([_block_diag(gate_a_w[0]), _block_diag(gate_x_w[0])], axis=1).astype(BF16)
    a64 = _block_diag(jnp.full((8, HEAD, HEAD), 1.0 / HEAD, F32)).astype(BF16)

    hb, proj, hl, ycat, mixed, x1 = _mix_fwd(chip, x[0], mod, vecd, vecl, win, wout, gab, a64)
    dx1, act, dz, dmo, h2b, accm = _mlp_fwd_bwd(
        chip, x1, loss_target[0], mod, vecd, (w1_all, own_w1), (w2_all, own_w2))

    p_w2, p_w1 = _add_halves("rs_add_sibling_mlp", *_wgrad_mlp(6, h2b, dz, act, dmo))
    parts_mlp = [p_w1, p_w2]
    q_w1, q_w2 = _exchange_chips("rs_exchange_mlp", 0, parts_mlp)
    grad_x, accd, accl, g_win, g_wout, g_gate = _mix_bwd(
        chip, dx1, x[0], mixed, proj, hl, hb, ycat, mod, vecd, vecl, win, wout, gab, a64)

    g_mix = [g_win, g_wout.reshape(N_CHIP, WOUT_BLK, D_MODEL)]
    recv_mix = _sibling_swap("rs_swap_halves_mix", g_mix, True, collective_id=4)
    own_mlp = [_add_chips("rs_add_chips_mlp%d" % k, chip, p, q) for k, (p, q) in enumerate(zip(parts_mlp, (q_w1, q_w2)))]
    sib_mlp = _sibling_swap("rs_swap_reduced_mlp", own_mlp, False, collective_id=5)
    gg_blk = jnp.concatenate([_diag_blocks(g_gate[:, 0:D_LRU]), _diag_blocks(g_gate[:, D_LRU:2 * D_LRU])], axis=1)
    gad, gam, gl, gg = _allgather8_seq("allgather_small_grads", 8, [accd, accm, accl, gg_blk.astype(BF16)])

    parts_mix = [_add_sibling("rs_add_sibling_mix%d" % k, g, r, core)
                 for k, (g, r) in enumerate(zip(g_mix, recv_mix))]
    landed_mix = _exchange_chips("rs_exchange_mix", 3, parts_mix)
    res_w1, res_w2 = _adam("adam_mlp", core, [(w_mlp1[0], own_mlp[0], sib_mlp[0], m_w_mlp1[0], v_w_mlp1[0]),
                                              (w_mlp2[0], own_mlp[1], sib_mlp[1], m_w_mlp2[0], v_w_mlp2[0])])
    own_mix = [_add_chips("rs_add_chips_mix%d" % k, chip, p, q, after=[res_w1[1]])
               for k, (p, q) in enumerate(zip(parts_mix, landed_mix))]
    sib_mix = _sibling_swap("rs_swap_reduced_mix", own_mix, False)
    (res_win,) = _adam("adam_w_in", core, [(w_in[0], own_mix[0], sib_mix[0], m_w_in[0], v_w_in[0])])
    (res_wout,) = _adam("adam_w_out", core, [(w_out[0], own_mix[1], sib_mix[1], m_w_out[0], v_w_out[0])])

    params = {
        "ada_b": (ada_b, m_ada_b, v_ada_b), "norm1_g": (norm1_g, m_norm1_g, v_norm1_g),
        "lru_conv_b": (lru_conv_b, m_lru_conv_b, v_lru_conv_b),
        "gate_a_w": tuple(t.reshape(D_LRU, HEAD) for t in (gate_a_w, m_gate_a_w, v_gate_a_w)),
        "gate_a_b": (gate_a_b, m_gate_a_b, v_gate_a_b),
        "gate_x_w": tuple(t.reshape(D_LRU, HEAD) for t in (gate_x_w, m_gate_x_w, v_gate_x_w)),
        "gate_x_b": (gate_x_b, m_gate_x_b, v_gate_x_b), "a_param": (a_param, m_a_param, v_a_param),
        "lru_conv_w": tuple(t[0] for t in (lru_conv_w, m_lru_conv_w, v_lru_conv_w)),
        "short_conv_w": tuple(t[0] for t in (short_conv_w, m_short_conv_w, v_short_conv_w)),
        "lru_out_g": (lru_out_g, m_lru_out_g, v_lru_out_g), "conv_out_g": (conv_out_g, m_conv_out_g, v_conv_out_g),
        "norm2_g": (norm2_g, m_norm2_g, v_norm2_g),
        "final_g": tuple(t[None, :] for t in (final_g, m_final_g, v_final_g)),
    }
    small, loss_blk, dmod_cols = _small_update(gad, gam, gl, gg, chip, params)
    loss = loss_blk[0, 0]

    ncol = 6 * D_MODEL // N_CHIP
    dmod_loc = lax.dynamic_slice_in_dim(dmod_cols, mychip * ncol, ncol, axis=1)
    sct = (c_all * jax.nn.sigmoid(c_all)).T
    ada = _ada_grad_adam(sct, dmod_loc, ada_w[0], m_ada_w[0], v_ada_w[0])

    res = {"ada_w": ada, "w_in": res_win, "w_out": res_wout, "w_mlp1": res_w1, "w_mlp2": res_w2}
    res = {n: tuple(t[None] for t in r) for n, r in res.items()}
    shapes = {"gate_a_w": gate_a_w.shape, "gate_x_w": gate_x_w.shape, "lru_conv_w": lru_conv_w.shape,
              "short_conv_w": short_conv_w.shape, "final_g": final_g.shape}
    for n, t in small.items():
        res[n] = tuple(u.reshape(shapes[n]) if n in shapes else u for u in t)

    order = ["ada_w", "ada_b", "norm1_g", "w_in", "lru_conv_w", "lru_conv_b", "gate_a_w", "gate_a_b", "gate_x_w",
             "gate_x_b", "a_param", "short_conv_w", "lru_out_g", "conv_out_g", "w_out", "norm2_g", "w_mlp1",
             "w_mlp2", "final_g"]
    return (loss, grad_x[None], *[res[n][0] for n in order], *[res[n][1] for n in order],
            *[res[n][2] for n in order], *[res[n][3] for n in order])
```

```python
import jax
import jax.numpy as jnp
from jax import lax
from jax.experimental import pallas as pl
from jax.experimental.pallas import tpu as pltpu
from jax.experimental.pallas import tpu_sc as plsc

F32 = jnp.float32
BF16 = jnp.bfloat16

D_MODEL = 1024
D_LRU = 512
D_IN = 2560
D_FF = 4096
N_CHIP = 4
WIN_BLK = D_IN // N_CHIP
WOUT_BLK = D_MODEL // N_CHIP
FF_BLK = D_FF // N_CHIP
HEAD = 64
EPS = 1e-6
C_GATE = 8.0
TOKEN_TILE = 256
HALO = 8
VMEM_LIMIT = 60 * 1024 * 1024

ADAM_LR = 0.001
ADAM_B1 = 0.9
ADAM_B2 = 0.999
ADAM_EPS = 1e-08
ADAM_WD = 0.01
ADAM_STEP = 10

MESH = pl.DeviceIdType.MESH
ANY = pl.BlockSpec(memory_space=pl.ANY)
VMEM = pl.BlockSpec(memory_space=pltpu.VMEM)
SMEM = pl.BlockSpec(memory_space=pltpu.SMEM)


def _full(shape, single=False):
    nd = len(shape)
    if single:
        return pl.BlockSpec(shape, lambda *_: (0,) * nd, pipeline_mode=pl.Buffered(1))
    return pl.BlockSpec(shape, lambda *_: (0,) * nd)


def _dot(a, b):
    return jnp.dot(a, b, preferred_element_type=F32)


def _dot_nt(a, b):
    return lax.dot_general(a, b, (((1,), (1,)), ((), ())), preferred_element_type=F32)


def _dot_tn(a, b):
    return lax.dot_general(a, b, (((0,), (0,)), ((), ())), preferred_element_type=F32)


def _gmean(v, a64):
    hi = v.astype(BF16)
    lo = (v - hi.astype(F32)).astype(BF16)
    return _dot(hi, a64) + _dot(lo, a64)


def _gelu(x):
    u = 0.7978845608028654 * (x + 0.044715 * x * x * x)
    t = jnp.tanh(u)
    return 0.5 * x * (1.0 + t), t


def _gelu_grad(x, t):
    du = 0.7978845608028654 * (1.0 + 3.0 * 0.044715 * x * x)
    return 0.5 * (1.0 + t) + 0.5 * x * (1.0 - t * t) * du


def _log1p_pos(y):
    return jnp.where(y < 1e-2, y * (1.0 - y * (0.5 - y * (1.0 / 3.0 - y * 0.25))), jnp.log(1.0 + y))


def _softplus(a):
    return jnp.maximum(a, 0.0) + _log1p_pos(jnp.exp(-jnp.abs(a)))


def _neg_expm1(z):
    series = -z * (1.0 + z * (0.5 + z * (1.0 / 6.0 + z * (1.0 / 24.0))))
    return jnp.where(z > -0.02, series, 1.0 - jnp.exp(z))


def _scan_fwd(a, b, row):
    n = a.shape[0]
    d = 1
    while d < n:
        m = row >= d
        b = jnp.where(m, a * pltpu.roll(b, d, 0) + b, b)
        a = jnp.where(m, a * pltpu.roll(a, d, 0), a)
        d *= 2
    return a, b


def _scan_rev(a, b, row):
    n = a.shape[0]
    d = 1
    while d < n:
        m = row < n - d
        b = jnp.where(m, b + a * pltpu.roll(b, n - d, 0), b)
        a = jnp.where(m, a * pltpu.roll(a, n - d, 0), a)
        d *= 2
    return a, b


def _colsum(v):
    return jnp.sum(v, axis=0, keepdims=True)


def _load_gathered(chip, gathered, own, slot, sems):
    copies = []
    for j in range(N_CHIP):
        @pl.when(chip == j)
        def _(j=j):
            pltpu.make_async_copy(own, slot(j), sems.at[j]).start()

        @pl.when(chip != j)
        def _(j=j):
            pltpu.make_async_copy(gathered.at[j], slot(j), sems.at[j]).start()

        copies.append(pltpu.make_async_copy(own, slot(j), sems.at[j]))
    return copies


def _lru_gates(xlb, gab, gbias, sp, first_row):
    g = _dot(xlb, gab) + gbias
    r = jax.nn.sigmoid(g[:, :D_LRU])
    ig = jax.nn.sigmoid(g[:, D_LRU:])
    la = (-C_GATE) * r * sp
    a = jnp.exp(la)
    msq = jnp.sqrt(_neg_expm1(2.0 * la))
    mult = jnp.where(first_row, 1.0, msq)
    return r, ig, a, msq, mult


def _mix_fwd(chip, x, mod, vecd, vecl, win, wout, gab, a64):
    s = x.shape[0]
    ts = TOKEN_TILE
    nt = s // ts

    def body(chip_ref, x_ref, mod_ref, vd_ref, vl_ref, win_hbm, win_own, wout_hbm, wout_own, gab_ref, a64_ref,
             hb_ref, proj_ref, hl_ref, ycat_ref, mixed_ref, x1_ref,
             win_ref, wout_ref, ext_lx, ext_cv, hcar, sems):
        i = pl.program_id(0)

        @pl.when(i == 0)
        def _():
            cps = _load_gathered(chip_ref[0], win_hbm, win_own, lambda j: win_ref.at[j], sems.at[pl.ds(0, N_CHIP)])
            cps += _load_gathered(chip_ref[0], wout_hbm, wout_own,
                                  lambda j: wout_ref.at[pl.ds(j * WOUT_BLK, WOUT_BLK), :],
                                  sems.at[pl.ds(N_CHIP, N_CHIP)])
            ext_lx[0:HALO, :] = jnp.zeros((HALO, D_LRU), F32)
            ext_cv[0:HALO, :] = jnp.zeros((HALO, D_LRU), F32)
            hcar[...] = jnp.zeros_like(hcar)
            for cp in cps:
                cp.wait()

        row = lax.broadcasted_iota(jnp.int32, (ts, D_LRU), 0)
        first_row = jnp.logical_and(row == 0, i == 0)
        xt = x_ref[...]
        shift1, scale1, gate1 = mod_ref[0:1, :], mod_ref[1:2, :], mod_ref[2:3, :]
        r1 = lax.rsqrt(jnp.mean(xt * xt, axis=-1, keepdims=True) + EPS)
        h = (xt * r1) * vd_ref[0:1, :] * (1.0 + scale1) + shift1
        hb = h.astype(BF16)
        hb_ref[...] = hb
        for j in range(N_CHIP):
            proj_ref[:, j * WIN_BLK:(j + 1) * WIN_BLK] = _dot(hb, win_ref[j])
        u_ly = proj_ref[:, 512:1024]
        u_b = proj_ref[:, 1024:1536]

        ext_lx[HALO:HALO + ts, :] = proj_ref[:, 0:512]
        xl = vl_ref[4:5, :] + vl_ref[0:1, :] * ext_lx[pl.ds(5, ts), :]
        for k in range(1, 4):
            xl = xl + vl_ref[k:k + 1, :] * ext_lx[pl.ds(5 + k, ts), :]
        ext_lx[0:HALO, :] = ext_lx[ts:ts + HALO, :]
        sp = _softplus(vl_ref[8:9, :])
        _, ig, a, _, mult = _lru_gates(xl.astype(BF16), gab_ref[...], vd_ref[3:4, :], sp, first_row)
        acum, hloc = _scan_fwd(a, mult * (ig * xl), row)
        hl = hloc + acum * hcar[0:1, :]
        hl_ref[...] = hl
        hcar[0:1, :] = hl_ref[ts - 1:ts, :]
        ge, _ = _gelu(u_ly)
        p = ge * hl
        y_lru = p * lax.rsqrt(_gmean(p * p, a64_ref[...]) + EPS) * vl_ref[9:10, :]
        ycat_ref[:, 0:512] = y_lru.astype(BF16)

        ext_cv[HALO:HALO + ts, :] = proj_ref[:, 1536:2048] * proj_ref[:, 2048:2560]
        q = vl_ref[5:6, :] * ext_cv[pl.ds(6, ts), :]
        for k in range(1, 3):
            q = q + vl_ref[5 + k:6 + k, :] * ext_cv[pl.ds(6 + k, ts), :]
        ext_cv[0:HALO, :] = ext_cv[ts:ts + HALO, :]
        yc = u_b * q
        y_conv = yc * lax.rsqrt(_gmean(yc * yc, a64_ref[...]) + EPS) * vl_ref[10:11, :]
        ycat_ref[:, 512:1024] = y_conv.astype(BF16)

        mixed = _dot(ycat_ref[...], wout_ref[...])
        mixed_ref[...] = mixed
        x1_ref[...] = xt + gate1 * mixed

    tile = lambda w: pl.BlockSpec((ts, w), lambda i: (i, 0))
    return pl.pallas_call(
        body, name="mix_fwd", grid=(nt,),
        in_specs=[SMEM, tile(D_MODEL), _full((8, D_MODEL)), _full((8, D_MODEL)), _full((16, D_LRU)),
                  ANY, ANY, ANY, ANY, _full((D_LRU, 2 * D_LRU), True), _full((D_LRU, D_LRU), True)],
        out_specs=[tile(D_MODEL), tile(D_IN), tile(D_LRU), tile(D_MODEL), tile(D_MODEL), tile(D_MODEL)],
        out_shape=[jax.ShapeDtypeStruct((s, D_MODEL), BF16), jax.ShapeDtypeStruct((s, D_IN), F32),
                   jax.ShapeDtypeStruct((s, D_LRU), F32), jax.ShapeDtypeStruct((s, D_MODEL), BF16),
                   jax.ShapeDtypeStruct((s, D_MODEL), F32), jax.ShapeDtypeStruct((s, D_MODEL), F32)],
        scratch_shapes=[pltpu.VMEM((N_CHIP, D_MODEL, WIN_BLK), BF16), pltpu.VMEM((D_MODEL, D_MODEL), BF16),
                        pltpu.VMEM((ts + HALO, D_LRU), F32), pltpu.VMEM((ts + HALO, D_LRU), F32),
                        pltpu.VMEM((HALO, D_LRU), F32), pltpu.SemaphoreType.DMA((2 * N_CHIP,))],
        compiler_params=pltpu.CompilerParams(dimension_semantics=("arbitrary",), vmem_limit_bytes=VMEM_LIMIT),
    )(chip, x, mod, vecd, vecl, *win, *wout, gab, a64)


def _mlp_fwd_bwd(chip, x1, target, mod, vecd, w1, w2):
    s = x1.shape[0]
    ts = TOKEN_TILE
    nt = s // ts

    def body(chip_ref, x1_ref, tg_ref, mod_ref, vd_ref, w1_hbm, w1_own, w2_hbm, w2_own,
             dx1_ref, act_ref, dz_ref, dmo_ref, h2_ref, acc_ref, w1_v, w2_v, rz_v, sems):
        i = pl.program_id(0)

        @pl.when(i == 0)
        def _():
            cps = _load_gathered(chip_ref[0], w1_hbm, w1_own, lambda j: w1_v.at[j], sems.at[pl.ds(0, N_CHIP)])
            cps += _load_gathered(chip_ref[0], w2_hbm, w2_own, lambda j: w2_v.at[j], sems.at[pl.ds(N_CHIP, N_CHIP)])
            acc_ref[...] = jnp.zeros_like(acc_ref)
            for cp in cps:
                cp.wait()

        xt = x1_ref[...]
        shift2, scale2, gate2 = mod_ref[3:4, :], mod_ref[4:5, :], mod_ref[5:6, :]
        g2, gf = vd_ref[1:2, :], vd_ref[2:3, :]
        r2 = lax.rsqrt(jnp.mean(xt * xt, axis=-1, keepdims=True) + EPS)
        n2 = xt * r2
        h2b = (n2 * g2 * (1.0 + scale2) + shift2).astype(BF16)
        h2_ref[...] = h2b
        for j in range(N_CHIP):
            rz_v[j] = jnp.maximum(_dot(h2b, w1_v[j]), 0.0)
        mo = jnp.zeros((ts, D_MODEL), F32)
        for j in range(N_CHIP):
            rz = rz_v[j]
            actb = (rz * rz).astype(BF16)
            act_ref[:, j * FF_BLK:(j + 1) * FF_BLK] = actb
            mo = mo + _dot(actb, w2_v[j])
        x2 = xt + gate2 * mo
        r3 = lax.rsqrt(jnp.mean(x2 * x2, axis=-1, keepdims=True) + EPS)
        n3 = x2 * r3
        e = n3 * gf - tg_ref[...]
        loss = (0.5 / D_MODEL) * jnp.sum(_colsum(e * e), axis=1, keepdims=True)
        dy = e * (1.0 / D_MODEL)
        acc_ref[4:5, :] += _colsum(dy * n3)
        acc_ref[5:6, :] += jnp.broadcast_to(loss, (1, D_MODEL))
        dn3 = dy * gf
        dx2 = r3 * (dn3 - n3 * jnp.mean(dn3 * n3, axis=-1, keepdims=True))
        acc_ref[2:3, :] += _colsum(dx2 * mo)
        dmob = (dx2 * gate2).astype(BF16)
        dmo_ref[...] = dmob
        for j in range(N_CHIP):
            dz_ref[:, j * FF_BLK:(j + 1) * FF_BLK] = (_dot_nt(dmob, w2_v[j]) * (2.0 * rz_v[j])).astype(BF16)
        dh2 = jnp.zeros((ts, D_MODEL), F32)
        for j in range(N_CHIP):
            dh2 = dh2 + _dot_nt(dz_ref[:, j * FF_BLK:(j + 1) * FF_BLK], w1_v[j])
        acc_ref[1:2, :] += _colsum(dh2 * (n2 * g2))
        acc_ref[0:1, :] += _colsum(dh2)
        dhn2 = dh2 * (1.0 + scale2)
        acc_ref[3:4, :] += _colsum(dhn2 * n2)
        dn2 = dhn2 * g2
        dx1_ref[...] = dx2 + r2 * (dn2 - n2 * jnp.mean(dn2 * n2, axis=-1, keepdims=True))

    tile = lambda w: pl.BlockSpec((ts, w), lambda i: (i, 0))
    return pl.pallas_call(
        body, name="mlp_fwd_bwd", grid=(nt,),
        in_specs=[SMEM, tile(D_MODEL), tile(D_MODEL), _full((8, D_MODEL)), _full((8, D_MODEL)), ANY, ANY, ANY, ANY],
        out_specs=[tile(D_MODEL), tile(D_FF), tile(D_FF), tile(D_MODEL), tile(D_MODEL), _full((8, D_MODEL))],
        out_shape=[jax.ShapeDtypeStruct((s, D_MODEL), F32), jax.ShapeDtypeStruct((s, D_FF), BF16),
                   jax.ShapeDtypeStruct((s, D_FF), BF16), jax.ShapeDtypeStruct((s, D_MODEL), BF16),
                   jax.ShapeDtypeStruct((s, D_MODEL), BF16), jax.ShapeDtypeStruct((8, D_MODEL), F32)],
        scratch_shapes=[pltpu.VMEM((N_CHIP, D_MODEL, FF_BLK), BF16), pltpu.VMEM((N_CHIP, FF_BLK, D_MODEL), BF16),
                        pltpu.VMEM((N_CHIP, ts, FF_BLK), F32), pltpu.SemaphoreType.DMA((2 * N_CHIP,))],
        compiler_params=pltpu.CompilerParams(dimension_semantics=("arbitrary",), vmem_limit_bytes=VMEM_LIMIT),
    )(chip, x1, target, mod, vecd, *w1, *w2)


def _mix_bwd(chip, dx1, x, mixed, proj, hl, hb, ycat, mod, vecd, vecl, win, wout, gab, a64):
    s = x.shape[0]
    ts = TOKEN_TILE
    nt = s // ts
    hpt = ts // HALO

    def body(chip_ref, dx1_ref, x_ref, mixed_ref, proj_ref, projh_ref, hl_ref, hlh_ref, hb_ref, ycat_ref,
             mod_ref, vd_ref, vl_ref, win_hbm, win_own, wout_hbm, wout_own, gab_ref, a64_ref,
             gx_ref, accd_ref, accl_ref, gwin_hbm, gwout_hbm, ggate_hbm,
             win_ref, wout_ref, dproj_ref, dgb_ref, gwin_acc, gwout_acc, ggate_acc,
             ext_lx, ext_cv, ext_hl, ext_dxl, ext_dq, gbuf, gcar, acar, sems):
        i = pl.program_id(0)
        ri = nt - 1 - i

        @pl.when(i == 0)
        def _():
            gwin_acc[...] = jnp.zeros_like(gwin_acc)
            gwout_acc[...] = jnp.zeros_like(gwout_acc)
            ggate_acc[...] = jnp.zeros_like(ggate_acc)
            cps = _load_gathered(chip_ref[0], win_hbm, win_own, lambda j: win_ref.at[j], sems.at[pl.ds(0, N_CHIP)])
            cps += _load_gathered(chip_ref[0], wout_hbm, wout_own,
                                  lambda j: wout_ref.at[pl.ds(j * WOUT_BLK, WOUT_BLK), :],
                                  sems.at[pl.ds(N_CHIP, N_CHIP)])
            for cp in cps:
                cp.wait()
            accd_ref[...] = jnp.zeros_like(accd_ref)
            accl_ref[...] = jnp.zeros_like(accl_ref)
            ext_dxl[ts:ts + HALO, :] = jnp.zeros((HALO, D_LRU), F32)
            ext_dq[ts:ts + HALO, :] = jnp.zeros((HALO, D_LRU), F32)
            gcar[...] = jnp.zeros_like(gcar)
            acar[...] = jnp.zeros_like(acar)

        row = lax.broadcasted_iota(jnp.int32, (ts, D_LRU), 0)
        first_row = jnp.logical_and(row == 0, ri == 0)
        halo_on = jnp.where(ri == 0, 0.0, 1.0)
        shift1, scale1, gate1 = mod_ref[0:1, :], mod_ref[1:2, :], mod_ref[2:3, :]
        g1 = vd_ref[0:1, :]
        a64m = a64_ref[...]
        lg, cg = vl_ref[9:10, :], vl_ref[10:11, :]

        dx1 = dx1_ref[...]
        accd_ref[2:3, :] += _colsum(dx1 * mixed_ref[...])
        dmb = (dx1 * gate1).astype(BF16)
        gwout_acc[...] += _dot_tn(ycat_ref[...], dmb)
        dycat = _dot_nt(dmb, wout_ref[...])
        dyl = dycat[:, 0:512]
        dyv = dycat[:, 512:1024]

        u_ly = proj_ref[:, 512:1024]
        u_b = proj_ref[:, 1024:1536]
        u_c = proj_ref[:, 1536:2048]
        u_v = proj_ref[:, 2048:2560]
        ext_lx[0:HALO, :] = projh_ref[:, 0:512] * halo_on
        ext_lx[HALO:HALO + ts, :] = proj_ref[:, 0:512]
        xl = vl_ref[4:5, :] + vl_ref[0:1, :] * ext_lx[pl.ds(5, ts), :]
        for k in range(1, 4):
            xl = xl + vl_ref[k:k + 1, :] * ext_lx[pl.ds(5 + k, ts), :]
        xlb = xl.astype(BF16)
        sp = _softplus(vl_ref[8:9, :])
        r, ig, a, msq, mult = _lru_gates(xlb, gab_ref[...], vd_ref[3:4, :], sp, first_row)
        hl = hl_ref[...]
        ge, th = _gelu(u_ly)
        p = ge * hl
        rl = lax.rsqrt(_gmean(p * p, a64m) + EPS)
        nl = p * rl
        ext_cv[0:HALO, :] = projh_ref[:, 1536:2048] * projh_ref[:, 2048:2560] * halo_on
        ext_cv[HALO:HALO + ts, :] = u_c * u_v
        q = vl_ref[5:6, :] * ext_cv[pl.ds(6, ts), :]
        for k in range(1, 3):
            q = q + vl_ref[5 + k:6 + k, :] * ext_cv[pl.ds(6 + k, ts), :]
        yc = u_b * q
        rc = lax.rsqrt(_gmean(yc * yc, a64m) + EPS)
        nc = yc * rc

        accl_ref[9:10, :] += _colsum(dyl * nl)
        dnl = dyl * lg
        dp = rl * (dnl - nl * _gmean(dnl * nl, a64m))
        dproj_ref[:, 512:1024] = ((dp * hl) * _gelu_grad(u_ly, th)).astype(BF16)
        a_next = jnp.where(row == ts - 1, acar[0:1, :], pltpu.roll(a, ts - 1, 0))
        acum, gloc = _scan_rev(a_next, dp * ge, row)
        gbuf[...] = gloc + acum * gcar[0:1, :]
        gcar[0:1, :] = gbuf[0:1, :]
        ext_hl[0:HALO, :] = hlh_ref[...] * halo_on
        ext_hl[HALO:HALO + ts, :] = hl
        acar[...] = a[0:HALO, :]
        gt = gbuf[...]
        da = gt * ext_hl[pl.ds(HALO - 1, ts), :]
        dmult = gt * ig * xl
        di = gt * mult * xl
        dxl = gt * mult * ig
        dla = da * a - jnp.where(first_row, 0.0, dmult * a * a / msq)
        accl_ref[8:9, :] += _colsum(dla * ((-C_GATE) * r))
        dra = dla * ((-C_GATE) * sp) * r * (1.0 - r)
        dia = di * ig * (1.0 - ig)
        accd_ref[4:5, 0:D_LRU] += _colsum(dra)
        accd_ref[4:5, D_LRU:2 * D_LRU] += _colsum(dia)
        dgb_ref[:, 0:D_LRU] = dra.astype(BF16)
        dgb_ref[:, D_LRU:2 * D_LRU] = dia.astype(BF16)
        dxl = dxl + _dot_nt(dgb_ref[...], gab_ref[...])
        ggate_acc[...] += _dot_tn(xlb, dgb_ref[...])
        accl_ref[4:5, :] += _colsum(dxl)
        for k in range(4):
            accl_ref[k:k + 1, :] += _colsum(dxl * ext_lx[pl.ds(5 + k, ts), :])
        ext_dxl[0:ts, :] = dxl
        du_lx = vl_ref[0:1, :] * ext_dxl[pl.ds(3, ts), :]
        for k in range(1, 4):
            du_lx = du_lx + vl_ref[k:k + 1, :] * ext_dxl[pl.ds(3 - k, ts), :]
        ext_dxl[ts:ts + HALO, :] = ext_dxl[0:HALO, :]
        dproj_ref[:, 0:512] = du_lx.astype(BF16)

        accl_ref[10:11, :] += _colsum(dyv * nc)
        dnc = dyv * cg
        dyc = rc * (dnc - nc * _gmean(dnc * nc, a64m))
        dproj_ref[:, 1024:1536] = (dyc * q).astype(BF16)
        dq = dyc * u_b
        for k in range(3):
            accl_ref[5 + k:6 + k, :] += _colsum(dq * ext_cv[pl.ds(6 + k, ts), :])
        ext_dq[0:ts, :] = dq
        dcv = vl_ref[5:6, :] * ext_dq[pl.ds(2, ts), :]
        for k in range(1, 3):
            dcv = dcv + vl_ref[5 + k:6 + k, :] * ext_dq[pl.ds(2 - k, ts), :]
        ext_dq[ts:ts + HALO, :] = ext_dq[0:HALO, :]
        dproj_ref[:, 1536:2048] = (dcv * u_v).astype(BF16)
        dproj_ref[:, 2048:2560] = (dcv * u_c).astype(BF16)

        dh = _dot_nt(dproj_ref[:, 0:WIN_BLK], win_ref[0])
        for j in range(1, N_CHIP):
            dh = dh + _dot_nt(dproj_ref[:, j * WIN_BLK:(j + 1) * WIN_BLK], win_ref[j])
        for j in range(N_CHIP):
            gwin_acc[j] += _dot_tn(hb_ref[...], dproj_ref[:, j * WIN_BLK:(j + 1) * WIN_BLK])
        xt = x_ref[...]
        r1 = lax.rsqrt(jnp.mean(xt * xt, axis=-1, keepdims=True) + EPS)
        n1 = xt * r1
        accd_ref[1:2, :] += _colsum(dh * (n1 * g1))
        accd_ref[0:1, :] += _colsum(dh)
        dhn1 = dh * (1.0 + scale1)
        accd_ref[3:4, :] += _colsum(dhn1 * n1)
        dn1 = dhn1 * g1
        gx_ref[...] = dx1 + r1 * (dn1 - n1 * jnp.mean(dn1 * n1, axis=-1, keepdims=True))

        @pl.when(i == nt - 1)
        def _():
            outs = [pltpu.make_async_copy(acc, dst, sems.at[k]) for k, (acc, dst) in enumerate(
                ((gwin_acc, gwin_hbm), (gwout_acc, gwout_hbm), (ggate_acc, ggate_hbm)))]
            for cp in outs:
                cp.start()
            for cp in outs:
                cp.wait()

    tile = lambda w: pl.BlockSpec((ts, w), lambda i: (nt - 1 - i, 0))
    halo = lambda w: pl.BlockSpec((HALO, w), lambda i: (jnp.maximum((nt - 1 - i) * hpt - 1, 0), 0))
    ext = pltpu.VMEM((ts + HALO, D_LRU), F32)
    return pl.pallas_call(
        body, name="mix_bwd", grid=(nt,),
        in_specs=[SMEM, tile(D_MODEL), tile(D_MODEL), tile(D_MODEL), tile(D_IN), halo(D_IN), tile(D_LRU), halo(D_LRU),
                  tile(D_MODEL), tile(D_MODEL), _full((8, D_MODEL)), _full((8, D_MODEL)), _full((16, D_LRU)),
                  ANY, ANY, ANY, ANY, _full((D_LRU, 2 * D_LRU), True), _full((D_LRU, D_LRU), True)],
        out_specs=[tile(D_MODEL), _full((8, D_MODEL)), _full((16, D_LRU)), ANY, ANY, ANY],
        out_shape=[jax.ShapeDtypeStruct((s, D_MODEL), F32),
                   jax.ShapeDtypeStruct((8, D_MODEL), F32), jax.ShapeDtypeStruct((16, D_LRU), F32),
                   jax.ShapeDtypeStruct((N_CHIP, D_MODEL, WIN_BLK), F32), jax.ShapeDtypeStruct((D_MODEL, D_MODEL), F32),
                   jax.ShapeDtypeStruct((D_LRU, 2 * D_LRU), F32)],
        scratch_shapes=[pltpu.VMEM((N_CHIP, D_MODEL, WIN_BLK), BF16), pltpu.VMEM((D_MODEL, D_MODEL), BF16),
                        pltpu.VMEM((ts, D_IN), BF16), pltpu.VMEM((ts, 2 * D_LRU), BF16),
                        pltpu.VMEM((N_CHIP, D_MODEL, WIN_BLK), F32), pltpu.VMEM((D_MODEL, D_MODEL), F32),
                        pltpu.VMEM((D_LRU, 2 * D_LRU), F32),
                        ext, ext, ext, ext, ext, pltpu.VMEM((ts, D_LRU), F32),
                        pltpu.VMEM((HALO, D_LRU), F32), pltpu.VMEM((HALO, D_LRU), F32),
                        pltpu.SemaphoreType.DMA((2 * N_CHIP,))],
        compiler_params=pltpu.CompilerParams(dimension_semantics=("arbitrary",), vmem_limit_bytes=VMEM_LIMIT),
    )(chip, dx1, x, mixed, proj, proj, hl, hl, hb, ycat, mod, vecd, vecl, *win, *wout, gab, a64)


def _wgrad_mlp(collective_id, h2b, dz, act, dmo):
    s = h2b.shape[0]
    nstep = 2 * N_CHIP
    half = FF_BLK // 2

    def body(h2_ref, dz_ref, act_ref, dmo_ref, own_hbm, recv_hbm, buf, keep_sems, send_sems, recv_sems):
        j = pl.program_id(0)
        x, y, c, _ = _position()

        def copies(jj):
            slot = jj % 2
            keep = pltpu.make_async_copy(buf.at[slot, pl.ds(c * half, half), :], own_hbm.at[jj], keep_sems.at[slot])
            give = pltpu.make_async_remote_copy(
                src_ref=buf.at[slot, pl.ds((1 - c) * half, half), :], dst_ref=recv_hbm.at[jj],
                send_sem=send_sems.at[slot], recv_sem=recv_sems.at[jj],
                device_id=(x, y, 1 - c), device_id_type=MESH)
            return keep, give

        @pl.when(j == 0)
        def _():
            pl.semaphore_signal(pltpu.get_barrier_semaphore(), inc=1, device_id=(x, y, 1 - c), device_id_type=MESH)

        @pl.when(j >= 2)
        def _():
            keep, give = copies(j - 2)
            keep.wait()
            give.wait_send()

        @pl.when(j < N_CHIP)
        def _():
            buf[j % 2] = _dot_tn(act_ref[...], dmo_ref[...]).astype(BF16)

        @pl.when(j >= N_CHIP)
        def _():
            buf[j % 2] = _dot_tn(h2_ref[...], dz_ref[...]).astype(BF16)

        @pl.when(j == 0)
        def _():
            pl.semaphore_wait(pltpu.get_barrier_semaphore(), 1)

        keep, give = copies(j)
        keep.start()
        give.start()

        @pl.when(j == nstep - 1)
        def _():
            for jj in (nstep - 2, nstep - 1):
                keep, give = copies(jj)
                keep.wait()
                give.wait_send()
            for jj in range(nstep):
                copies(jj)[1].wait_recv()

    sds = jax.ShapeDtypeStruct((nstep, half, D_MODEL), BF16)
    whole = pl.BlockSpec((s, D_MODEL), lambda j: (0, 0))
    return pl.pallas_call(
        body, name="wgrad_mlp", grid=(nstep,),
        in_specs=[whole, pl.BlockSpec((s, FF_BLK), lambda j: (0, jnp.maximum(j - N_CHIP, 0))),
                  pl.BlockSpec((s, FF_BLK), lambda j: (0, jnp.minimum(j, N_CHIP - 1))), whole],
        out_specs=[ANY, ANY], out_shape=[sds, sds],
        scratch_shapes=[pltpu.VMEM((2, FF_BLK, D_MODEL), BF16), pltpu.SemaphoreType.DMA((2,)),
                        pltpu.SemaphoreType.DMA((2,)), pltpu.SemaphoreType.DMA((nstep,))],
        compiler_params=pltpu.CompilerParams(dimension_semantics=("arbitrary",), vmem_limit_bytes=VMEM_LIMIT,
                                             collective_id=collective_id),
    )(h2b, dz, act, dmo)


def _mod_matmul(c_all, ada_w_loc):
    n = ada_w_loc.shape[1]
    cb = 512

    def body(c_ref, w_ref, o_ref):
        c = c_ref[...]
        sc = c * jax.nn.sigmoid(c)
        o_ref[...] = _dot(sc.astype(BF16), w_ref[...].astype(BF16))

    return pl.pallas_call(
        body, name="mod_matmul", grid=(n // cb,),
        in_specs=[_full((8, D_MODEL)), pl.BlockSpec((D_MODEL, cb), lambda j: (0, j))],
        out_specs=pl.BlockSpec((8, cb), lambda j: (0, j)),
        out_shape=jax.ShapeDtypeStruct((8, n), F32),
        compiler_params=pltpu.CompilerParams(dimension_semantics=("arbitrary",), vmem_limit_bytes=VMEM_LIMIT),
    )(c_all, ada_w_loc)


def _adam_math(w, g, m, v):
    m = ADAM_B1 * m + (1.0 - ADAM_B1) * g
    v = ADAM_B2 * v + (1.0 - ADAM_B2) * (g * g)
    m_hat = m / (1.0 - ADAM_B1 ** ADAM_STEP)
    v_hat = v / (1.0 - ADAM_B2 ** ADAM_STEP)
    delta = (-ADAM_LR) * (m_hat / (jnp.sqrt(v_hat) + ADAM_EPS) + ADAM_WD * w)
    return delta, m, v


def _adam(name, core, shards):
    n = len(shards)
    r, c = shards[0][0].shape
    half = r // 2
    rb = min(half, 128)
    nh = half // rb

    def body(core_ref, *refs):
        ins, outs = refs[:5 * n], refs[5 * n:]
        mine = (pl.program_id(0) // nh) == core_ref[0]
        for k in range(n):
            w_ref, go_ref, gs_ref, m_ref, v_ref = ins[5 * k:5 * k + 5]
            g_ref, d_ref, mo_ref, vo_ref = outs[4 * k:4 * k + 4]
            g = jnp.where(mine, go_ref[...], gs_ref[...])
            g_ref[...] = g
            d_ref[...], mo_ref[...], vo_ref[...] = _adam_math(w_ref[...], g, m_ref[...], v_ref[...])

    spec = pl.BlockSpec((rb, c), lambda i, core_ref: (i, 0))
    own = pl.BlockSpec((rb, c), lambda i, core_ref: (jnp.where(i // nh == core_ref[0], i % nh, 0), 0))
    sib = pl.BlockSpec((rb, c), lambda i, core_ref: (jnp.where(i // nh == core_ref[0], 0, i % nh), 0))
    sds = jax.ShapeDtypeStruct((r, c), F32)
    res = pl.pallas_call(
        body, name=name,
        grid_spec=pltpu.PrefetchScalarGridSpec(
            num_scalar_prefetch=1, grid=(r // rb,),
            in_specs=[spec, own, sib, spec, spec] * n, out_specs=[spec] * (4 * n)),
        out_shape=[sds] * (4 * n),
        compiler_params=pltpu.CompilerParams(dimension_semantics=("arbitrary",), vmem_limit_bytes=VMEM_LIMIT),
    )(core, *[t for s in shards for t in s])
    return [res[4 * k:4 * k + 4] for k in range(n)]


def _ada_grad_adam(sct, dmod_loc, w, m, v):
    r, c = w.shape
    rb = 256

    def body(s_ref, dm_ref, w_ref, m_ref, v_ref, g_ref, d_ref, mo_ref, vo_ref):
        g = s_ref[:, 0:1] * dm_ref[0:1, :]
        for b in range(1, 8):
            g = g + s_ref[:, b:b + 1] * dm_ref[b:b + 1, :]
        g_ref[...] = g
        d_ref[...], mo_ref[...], vo_ref[...] = _adam_math(w_ref[...], g, m_ref[...], v_ref[...])

    spec = pl.BlockSpec((rb, c), lambda i: (i, 0))
    sds = jax.ShapeDtypeStruct((r, c), F32)
    return pl.pallas_call(
        body, name="ada_grad_adam", grid=(r // rb,),
        in_specs=[pl.BlockSpec((rb, 8), lambda i: (i, 0)), _full((8, c)), spec, spec, spec],
        out_specs=[spec] * 4, out_shape=[sds] * 4,
        compiler_params=pltpu.CompilerParams(dimension_semantics=("arbitrary",), vmem_limit_bytes=VMEM_LIMIT),
    )(sct, dmod_loc, w, m, v)


def _position():
    x, y, c = lax.axis_index("x"), lax.axis_index("y"), lax.axis_index("c")
    chips = [(1 - x, y), (x, 1 - y), (1 - x, 1 - y)]
    return x, y, c, chips


def _ag8_run(ins, outs, send_sems, recv_sems, local_sems):
    na = len(ins)
    x, y, c, chips = _position()
    me, sibling = (x, y, c), (x, y, 1 - c)
    first, passed, local = [], [], []
    for a in range(na):
        m_per = ins[a].shape[0]

        def rows(px, py, pc, a=a, m_per=m_per):
            return outs[a].at[pl.ds((4 * px + 2 * py + pc) * m_per, m_per), :]

        def copy(k, block, to, src=None, a=a, rows=rows):
            return pltpu.make_async_remote_copy(
                src_ref=rows(*block) if src is None else src, dst_ref=rows(*block),
                send_sem=send_sems.at[7 * a + k], recv_sem=recv_sems.at[7 * a + k],
                device_id=to, device_id_type=MESH)

        mine = pltpu.make_async_copy(ins[a], rows(*me), local_sems.at[a])
        mine.start()
        local.append(mine)
        f = [copy(0, me, sibling, src=ins[a])]
        f += [copy(1 + j, me, (*chip, c), src=ins[a]) for j, chip in enumerate(chips)]
        for cp in f:
            cp.start()
        first.append((f, copy))
    for a in range(na):
        f, copy = first[a]
        p = [copy(4 + j, (*chip, c), sibling) for j, chip in enumerate(chips)]
        for j, chip in enumerate(chips):
            copy(1 + j, (*chip, c), me).wait_recv()
            p[j].start()
        passed.append(p)
    for a in range(na):
        f, copy = first[a]
        copy(0, sibling, me).wait_recv()
        for j, chip in enumerate(chips):
            copy(4 + j, (*chip, 1 - c), me).wait_recv()
        for cp in f + passed[a]:
            cp.wait_send()
        local[a].wait()


def _allgather8_seq(name, collective_id, arrs):
    na = len(arrs)
    hbm = pltpu.MemorySpace.HBM
    ins = [jax.new_ref(a, memory_space=hbm) for a in arrs]
    outs = [jax.empty_ref(jax.ShapeDtypeStruct((8 * a.shape[0], a.shape[1]), a.dtype), memory_space=hbm) for a in arrs]

    @pl.kernel(mesh=plsc.ScalarSubcoreMesh(axis_name="sequencer", num_cores=1), name=name,
               scratch_types=(pltpu.SemaphoreType.DMA((7 * na,)), pltpu.SemaphoreType.DMA((7 * na,)),
                              pltpu.SemaphoreType.DMA((na,))),
               compiler_params=pltpu.CompilerParams(collective_id=collective_id))
    def launch(send_sems, recv_sems, local_sems):
        x, y, c, chips = _position()
        peers = [(x, y, 1 - c)] + [(*chip, c) for chip in chips]
        barrier = pltpu.get_barrier_semaphore()
        for peer in peers:
            pl.semaphore_signal(barrier, inc=1, device_id=peer, device_id_type=MESH)
        pl.semaphore_wait(barrier, len(peers))
        _ag8_run(ins, outs, send_sems, recv_sems, local_sems)

    launch()
    return [o[...] for o in outs]


def _allgather8(name, arrs):
    na = len(arrs)

    def body(*refs):
        _ag8_run(refs[:na], refs[na:2 * na], *refs[2 * na:])

    return pl.pallas_call(
        body, name=name,
        out_shape=[jax.ShapeDtypeStruct((8 * a.shape[0], a.shape[1]), a.dtype) for a in arrs],
        in_specs=[VMEM] * na, out_specs=[VMEM] * na,
        scratch_shapes=[pltpu.SemaphoreType.DMA((7 * na,)), pltpu.SemaphoreType.DMA((7 * na,)),
                        pltpu.SemaphoreType.DMA((na,))],
        compiler_params=pltpu.CompilerParams(vmem_limit_bytes=VMEM_LIMIT),
    )(*arrs)


AG_SEMS = 7


def _ag_copies(ins, outs, send_sems, recv_sems):
    x, y, c, chips = _position()
    sibling = (x, y, 1 - c)
    xn, yn, dg = [2 * chip[0] + chip[1] for chip in chips]
    to_x, to_y = (1 - x, y, c), (x, 1 - y, c)
    res = []
    for a in range(len(ins)):
        half = ins[a].shape[0] // 2
        quarter = half // 2

        def copy(k, dst, to, src=None, a=a):
            return pltpu.make_async_remote_copy(
                src_ref=dst if src is None else src, dst_ref=dst,
                send_sem=send_sems.at[AG_SEMS * a + k], recv_sem=recv_sems.at[AG_SEMS * a + k],
                device_id=to, device_id_type=MESH)

        def rows(chip, pc, q=None, a=a, half=half, quarter=quarter):
            if q is None:
                return outs[a].at[chip, pl.ds(pc * half, half), :]
            return outs[a].at[chip, pl.ds(pc * half + q * quarter, quarter), :]

        own = ins[a].at[pl.ds(c * half, half), :]
        mine = rows(2 * x + y, c)
        res.append(dict(
            sends=[copy(0, mine, to_x, src=own), copy(1, mine, to_y, src=own)],
            from_x=copy(0, rows(xn, c), to_x), from_y=copy(1, rows(yn, c), to_y),
            relay_y=copy(2, rows(xn, c, 0), to_y), relay_x=copy(3, rows(yn, c, 1), to_x),
            from_y_relay=copy(2, rows(dg, c, 0), to_y), from_x_relay=copy(3, rows(dg, c, 1), to_x),
            pass_on=[copy(4, rows(xn, c), sibling), copy(5, rows(yn, c), sibling), copy(6, rows(dg, c), sibling)],
            from_sibling=[copy(4, rows(xn, 1 - c), sibling), copy(5, rows(yn, 1 - c), sibling),
                          copy(6, rows(dg, 1 - c), sibling)]))
    return res


def _ag_start(ins, outs, send_sems, recv_sems):
    for cps in _ag_copies(ins, outs, send_sems, recv_sems):
        for cp in cps["sends"]:
            cp.start()


def _ag_relay(ins, outs, send_sems, recv_sems, which):
    copies = _ag_copies(ins, outs, send_sems, recv_sems)
    for a in which:
        cps = copies[a]
        cps["from_x"].wait_recv()
        cps["relay_y"].start()
        cps["pass_on"][0].start()
        cps["from_y"].wait_recv()
        cps["relay_x"].start()
        cps["pass_on"][1].start()


def _ag_complete(ins, outs, send_sems, recv_sems):
    copies = _ag_copies(ins, outs, send_sems, recv_sems)
    for cps in copies:
        cps["from_y_relay"].wait_recv()
        cps["from_x_relay"].wait_recv()
        cps["pass_on"][2].start()
    for cps in copies:
        for cp in cps["from_sibling"]:
            cp.wait_recv()
        for cp in cps["sends"] + [cps["relay_y"], cps["relay_x"]] + cps["pass_on"]:
            cp.wait_send()


def _ag_finish(ins, outs, send_sems, recv_sems):
    _ag_relay(ins, outs, send_sems, recv_sems, range(len(ins)))
    _ag_complete(ins, outs, send_sems, recv_sems)


def _allgather_weights(name, collective_id, shards):
    na = len(shards)
    hbm = pltpu.MemorySpace.HBM
    ins = [jax.new_ref(s, memory_space=hbm) for s in shards]
    outs = [jax.empty_ref(jax.ShapeDtypeStruct((N_CHIP,) + s.shape, s.dtype), memory_space=hbm) for s in shards]

    @pl.kernel(mesh=plsc.ScalarSubcoreMesh(axis_name="sequencer", num_cores=1), name=name,
               scratch_types=(pltpu.SemaphoreType.DMA((AG_SEMS * na,)), pltpu.SemaphoreType.DMA((AG_SEMS * na,))),
               compiler_params=pltpu.CompilerParams(collective_id=collective_id))
    def launch(send_sems, recv_sems):
        x, y, c, _ = _position()
        peers = [(1 - x, y, c), (x, 1 - y, c), (x, y, 1 - c)]
        barrier = pltpu.get_barrier_semaphore()
        for peer in peers:
            pl.semaphore_signal(barrier, inc=1, device_id=peer, device_id_type=MESH)
        pl.semaphore_wait(barrier, len(peers))
        _ag_start(ins, outs, send_sems, recv_sems)
        _ag_finish(ins, outs, send_sems, recv_sems)

    launch()
    return [o[...] for o in outs]


def _sibling_swap(name, arrs, split_rows, collective_id=None):
    na = len(arrs)
    shapes = [jax.ShapeDtypeStruct((a.shape[0], a.shape[1] // 2, a.shape[2]) if split_rows else a.shape, a.dtype)
              for a in arrs]

    def run(ins, outs, send_sems, recv_sems):
        x, y, c, _ = _position()
        cps = []
        for a in range(na):
            src = ins[a]
            if split_rows:
                half = src.shape[1] // 2
                src = src.at[:, pl.ds((1 - c) * half, half), :]
            cp = pltpu.make_async_remote_copy(
                src_ref=src, dst_ref=outs[a], send_sem=send_sems.at[a], recv_sem=recv_sems.at[a],
                device_id=(x, y, 1 - c), device_id_type=MESH)
            cp.start()
            cps.append(cp)
        for cp in cps:
            cp.wait()

    sems = (pltpu.SemaphoreType.DMA((na,)), pltpu.SemaphoreType.DMA((na,)))
    if collective_id is None:
        return pl.pallas_call(
            lambda *refs: run(refs[:na], refs[na:2 * na], *refs[2 * na:]), name=name, out_shape=shapes,
            in_specs=[ANY] * na, out_specs=[ANY] * na, scratch_shapes=list(sems))(*arrs)

    hbm = pltpu.MemorySpace.HBM
    ins = [jax.new_ref(a, memory_space=hbm) for a in arrs]
    outs = [jax.empty_ref(s, memory_space=hbm) for s in shapes]

    @pl.kernel(mesh=plsc.ScalarSubcoreMesh(axis_name="sequencer", num_cores=1), name=name, scratch_types=sems,
               compiler_params=pltpu.CompilerParams(collective_id=collective_id))
    def launch(send_sems, recv_sems):
        x, y, c, _ = _position()
        barrier = pltpu.get_barrier_semaphore()
        pl.semaphore_signal(barrier, inc=1, device_id=(x, y, 1 - c), device_id_type=MESH)
        pl.semaphore_wait(barrier, 1)
        run(ins, outs, send_sems, recv_sems)

    launch()
    return [o[...] for o in outs]


def _xchg_copies(ins, outs, send_sems, recv_sems):
    x, y, c, chips = _position()
    return [pltpu.make_async_remote_copy(
        src_ref=ins[a].at[2 * chip[0] + chip[1]], dst_ref=outs[a].at[j],
        send_sem=send_sems.at[3 * a + j], recv_sem=recv_sems.at[3 * a + j],
        device_id=(*chip, c), device_id_type=MESH) for a in range(len(ins)) for j, chip in enumerate(chips)]


def _exchange_chips(name, collective_id, parts):
    na = len(parts)
    hbm = pltpu.MemorySpace.HBM
    ins = [jax.new_ref(p, memory_space=hbm) for p in parts]
    outs = [jax.empty_ref(jax.ShapeDtypeStruct((3,) + p.shape[1:], p.dtype), memory_space=hbm) for p in parts]

    @pl.kernel(mesh=plsc.ScalarSubcoreMesh(axis_name="sequencer", num_cores=1), name=name,
               scratch_types=(pltpu.SemaphoreType.DMA((3 * na,)), pltpu.SemaphoreType.DMA((3 * na,))),
               compiler_params=pltpu.CompilerParams(collective_id=collective_id))
    def launch(send_sems, recv_sems):
        x, y, c, chips = _position()
        barrier = pltpu.get_barrier_semaphore()
        for chip in chips:
            pl.semaphore_signal(barrier, inc=1, device_id=(*chip, c), device_id_type=MESH)
        pl.semaphore_wait(barrier, len(chips))
        for cp in _xchg_copies(ins, outs, send_sems, recv_sems):
            cp.start()
        for cp in _xchg_copies(ins, outs, send_sems, recv_sems):
            cp.wait()

    launch()
    return [q[...] for q in outs]


def _add_sibling(name, grad, recv, core, after=()):
    _, r, c = grad.shape
    half = r // 2
    rb = half
    nrb = half // rb

    def body(core_ref, g_ref, r_ref, *refs):
        refs[-1][...] = (g_ref[...].astype(F32) + r_ref[...].astype(F32)).astype(BF16)

    return pl.pallas_call(
        body, name=name,
        grid_spec=pltpu.PrefetchScalarGridSpec(
            num_scalar_prefetch=1, grid=(N_CHIP, nrb),
            in_specs=[pl.BlockSpec((1, rb, c), lambda j, i, core_ref: (j, core_ref[0] * nrb + i, 0)),
                      pl.BlockSpec((1, rb, c), lambda j, i, core_ref: (j, i, 0))] + [ANY] * len(after),
            out_specs=pl.BlockSpec((1, rb, c), lambda j, i, core_ref: (j, i, 0))),
        out_shape=jax.ShapeDtypeStruct((N_CHIP, half, c), BF16),
        compiler_params=pltpu.CompilerParams(dimension_semantics=("arbitrary", "arbitrary"),
                                             vmem_limit_bytes=VMEM_LIMIT),
    )(core, grad, recv, *after)


def _add_halves(name, own, recv):
    n = own.shape[0] // N_CHIP
    _, half, c = own.shape
    rb = half

    def body(*refs):
        for k in range(n):
            refs[2 * n + k][...] = (refs[k][...].astype(F32) + refs[n + k][...].astype(F32)).astype(BF16)

    group = [pl.BlockSpec((1, rb, c), lambda j, i, k=k: (N_CHIP * k + j, i, 0)) for k in range(n)]
    spec = pl.BlockSpec((1, rb, c), lambda j, i: (j, i, 0))
    return pl.pallas_call(
        body, name=name, grid=(N_CHIP, half // rb), in_specs=group + group, out_specs=[spec] * n,
        out_shape=[jax.ShapeDtypeStruct((N_CHIP, half, c), BF16)] * n,
        compiler_params=pltpu.CompilerParams(dimension_semantics=("arbitrary", "arbitrary"),
                                             vmem_limit_bytes=VMEM_LIMIT),
    )(*([own] * n + [recv] * n))


def _add_chips(name, chip, p, q, after=()):
    _, half, c = q.shape
    rb = max(half // 4, 16)

    def body(chip_ref, p_ref, q_ref, *refs):
        acc = p_ref[0].astype(F32)
        for j in range(3):
            acc = acc + q_ref[j].astype(F32)
        refs[-1][...] = acc

    return pl.pallas_call(
        body, name=name,
        grid_spec=pltpu.PrefetchScalarGridSpec(
            num_scalar_prefetch=1, grid=(half // rb,),
            in_specs=[pl.BlockSpec((1, rb, c), lambda i, chip_ref: (chip_ref[0], i, 0)),
                      pl.BlockSpec((3, rb, c), lambda i, chip_ref: (0, i, 0))] + [ANY] * len(after),
            out_specs=pl.BlockSpec((rb, c), lambda i, chip_ref: (i, 0))),
        out_shape=jax.ShapeDtypeStruct((half, c), F32),
        compiler_params=pltpu.CompilerParams(dimension_semantics=("arbitrary",), vmem_limit_bytes=VMEM_LIMIT),
    )(chip, p, q, *after)


def _small_update(gad, gam, gl, gg, mychip, params):
    names = ["ada_b", "norm1_g", "lru_conv_b", "gate_a_w", "gate_a_b", "gate_x_w", "gate_x_b", "a_param",
             "lru_conv_w", "short_conv_w", "lru_out_g", "conv_out_g", "norm2_g", "final_g"]
    flat = [t for n in names for t in params[n]]
    nin = len(flat)

    def body(chip_ref, gad_ref, gam_ref, gl_ref, gg_ref, *refs):
        ins = {n: refs[3 * k:3 * k + 3] for k, n in enumerate(names)}
        outs = {n: refs[nin + 4 * k:nin + 4 * k + 4] for k, n in enumerate(names)}
        loss_ref, dmod_ref = refs[nin + 4 * len(names):nin + 4 * len(names) + 2]

        def dsum(ref, lo, n):
            per = ref.shape[0] // 8
            acc = ref[lo:lo + n, :].astype(F32)
            for dev in range(1, 8):
                acc = acc + ref[dev * per + lo:dev * per + lo + n, :].astype(F32)
            return acc

        def update(n, g):
            w_ref, m_ref, v_ref = ins[n]
            g_ref, d_ref, mo_ref, vo_ref = outs[n]
            g_ref[...] = g
            d_ref[...], mo_ref[...], vo_ref[...] = _adam_math(w_ref[...], g, m_ref[...], v_ref[...])

        d, dm, l, lw = refs[-4:]
        d[...] = dsum(gad_ref, 0, 8)
        dm[...] = dsum(gam_ref, 0, 8)
        l[...] = dsum(gl_ref, 0, 16)
        for dev in range(8):
            for k in range(3):
                dmod_ref[dev:dev + 1, k * D_MODEL:(k + 1) * D_MODEL] = gad_ref[dev * 8 + k:dev * 8 + k + 1, :]
                dmod_ref[dev:dev + 1, (3 + k) * D_MODEL:(4 + k) * D_MODEL] = gam_ref[dev * 8 + k:dev * 8 + k + 1, :]
        w_ref, m_ref, v_ref = ins["ada_b"]
        g_ref, d_ref, mo_ref, vo_ref = outs["ada_b"]
        for k in range(3):
            g_ref[:, k * D_MODEL:(k + 1) * D_MODEL] = d[k:k + 1, :]
            g_ref[:, (3 + k) * D_MODEL:(4 + k) * D_MODEL] = dm[k:k + 1, :]
        d_ref[...], mo_ref[...], vo_ref[...] = _adam_math(w_ref[...], g_ref[...], m_ref[...], v_ref[...])
        update("norm1_g", d[3:4, :])
        update("norm2_g", dm[3:4, :])
        update("final_g", dm[4:5, :])
        update("gate_a_b", d[4:5, 0:D_LRU])
        update("gate_x_b", d[4:5, D_LRU:2 * D_LRU])
        update("lru_conv_b", l[4:5, :])
        update("a_param", l[8:9, :] * jax.nn.sigmoid(ins["a_param"][0][...]))
        update("lru_out_g", l[9:10, :])
        update("conv_out_g", l[10:11, :])
        loss_ref[...] = jnp.broadcast_to(dm[5:6, 0:128], (8, 128))
        chip = chip_ref[0]
        acc = jnp.zeros((8, 128), F32)
        for j in range(N_CHIP):
            acc = acc + jnp.where(chip == j, l[0:8, j * 128:(j + 1) * 128], 0.0)
        lw[...] = acc
        update("lru_conv_w", lw[0:4, :])
        update("short_conv_w", lw[5:8, :])
        gates = dsum(gg_ref, 0, D_LRU)
        update("gate_a_w", gates[:, 0:HEAD])
        update("gate_x_w", gates[:, HEAD:2 * HEAD])

    out_shape = []
    for n in names:
        out_shape += [jax.ShapeDtypeStruct(params[n][0].shape, F32)] * 4
    out_shape += [jax.ShapeDtypeStruct((8, 128), F32), jax.ShapeDtypeStruct((8, 6 * D_MODEL), F32)]
    res = pl.pallas_call(
        body, name="small_update", out_shape=out_shape,
        in_specs=[SMEM] + [VMEM] * (4 + nin),
        out_specs=[VMEM] * len(out_shape),
        scratch_shapes=[pltpu.VMEM((8, D_MODEL), F32), pltpu.VMEM((8, D_MODEL), F32), pltpu.VMEM((16, D_LRU), F32),
                        pltpu.VMEM((8, 128), F32)],
        compiler_params=pltpu.CompilerParams(vmem_limit_bytes=VMEM_LIMIT),
    )(mychip, gad, gam, gl, gg, *flat)
    per = {n: res[4 * k:4 * k + 4] for k, n in enumerate(names)}
    return per, res[-2], res[-1]


def _block_diag(w):
    eye = jnp.eye(8, dtype=w.dtype)
    return (eye[:, None, :, None] * w[:, :, None, :]).reshape(8 * HEAD, 8 * HEAD)


def _diag_blocks(g):
    return jnp.concatenate([g[h * HEAD:(h + 1) * HEAD, h * HEAD:(h + 1) * HEAD] for h in range(8)], axis=0)


def kernel(x, c, ada_w, ada_b, norm1_g, w_in, lru_conv_w, lru_conv_b, gate_a_w, gate_a_b, gate_x_w, gate_x_b, a_param, short_conv_w, lru_out_g, conv_out_g, w_out, norm2_g, w_mlp1, w_mlp2, final_g, loss_target, m_ada_w, m_ada_b, m_norm1_g, m_w_in, m_lru_conv_w, m_lru_conv_b, m_gate_a_w, m_gate_a_b, m_gate_x_w, m_gate_x_b, m_a_param, m_short_conv_w, m_lru_out_g, m_conv_out_g, m_w_out, m_norm2_g, m_w_mlp1, m_w_mlp2, m_final_g, v_ada_w, v_ada_b, v_norm1_g, v_w_in, v_lru_conv_w, v_lru_conv_b, v_gate_a_w, v_gate_a_b, v_gate_x_w, v_gate_x_b, v_a_param, v_short_conv_w, v_lru_out_g, v_conv_out_g, v_w_out, v_norm2_g, v_w_mlp1, v_w_mlp2, v_final_g):
    xi, yi, ci = lax.axis_index("x"), lax.axis_index("y"), lax.axis_index("c")
    mychip = 2 * xi + yi
    me = 4 * xi + 2 * yi + ci

    own_in, own_out = w_in[0].astype(BF16), w_out[0].astype(BF16)
    win_all, wout_all = _allgather_weights("allgather_mixer_weights", 1, [own_in, own_out])
    own_w1, own_w2 = w_mlp1[0].astype(BF16), w_mlp2[0].astype(BF16)
    w1_all, w2_all = _allgather_weights("allgather_mlp_weights", 2, [own_w1, own_w2])

    c_blk = jnp.zeros((8, D_MODEL), F32).at[0:1].set(c)
    cw_blk = jnp.zeros((8, 128), F32).at[0:4].set(lru_conv_w[0]).at[4:7].set(short_conv_w[0])
    c_g, cw_g = _allgather8("allgather_cond", [c_blk, cw_blk])
    c_all = c_g.reshape(8, 8, D_MODEL)[:, 0]
    cw_g = cw_g.reshape(4, 2, 8, 128)[:, 0]
    lcw = cw_g[:, 0:4].transpose(1, 0, 2).reshape(4, D_LRU)
    scw = cw_g[:, 4:7].transpose(1, 0, 2).reshape(3, D_LRU)

    mod_loc = _mod_matmul(c_all, ada_w[0])
    (mod_g,) = _allgather8("allgather_mod", [mod_loc])
    mod_all = mod_g.reshape(4, 2, 8, 6 * D_MODEL // 4)[:, 0].transpose(1, 0, 2).reshape(8, 6 * D_MODEL) + ada_b
    mod_pad = jnp.pad(mod_all.reshape(8, 6, D_MODEL), ((0, 0), (0, 2), (0, 0)))
    mod = lax.dynamic_slice_in_dim(mod_pad, me, 1, axis=0).reshape(8, D_MODEL)

    win, wout = (win_all, own_in), (wout_all, own_out)
    chip = mychip.reshape(1).astype(jnp.int32)
    core = ci.reshape(1).astype(jnp.int32)

    vecd = jnp.concatenate([norm1_g, norm2_g, final_g[None, :], jnp.concatenate([gate_a_b, gate_x_b], axis=1),
                            jnp.zeros((4, D_MODEL), F32)], axis=0)
    vecl = jnp.concatenate([lcw, lru_conv_b, scw, a_param, lru_out_g, conv_out_g, jnp.zeros((5, D_LRU), F32)], axis=0)
    gab = jnp.concatenate([_block_diag(gate_a_w[0]), _block_diag(gate_x_w[0])], axis=1).astype(BF16)
    a64 = _block_diag(jnp.full((8, HEAD, HEAD), 1.0 / HEAD, F32)).astype(BF16)

    hb, proj, hl, ycat, mixed, x1 = _mix_fwd(chip, x[0], mod, vecd, vecl, win, wout, gab, a64)
    dx1, act, dz, dmo, h2b, accm = _mlp_fwd_bwd(
        chip, x1, loss_target[0], mod, vecd, (w1_all, own_w1), (w2_all, own_w2))

    p_w2, p_w1 = _add_halves("rs_add_sibling_mlp", *_wgrad_mlp(6, h2b, dz, act, dmo))
    parts_mlp = [p_w1, p_w2]
    q_w1, q_w2 = _exchange_chips("rs_exchange_mlp", 0, parts_mlp)
    grad_x, accd, accl, g_win, g_wout, g_gate = _mix_bwd(
        chip, dx1, x[0], mixed, proj, hl, hb, ycat, mod, vecd, vecl, win, wout, gab, a64)

    g_mix = [g_win, g_wout.reshape(N_CHIP, WOUT_BLK, D_MODEL)]
    recv_mix = _sibling_swap("rs_swap_halves_mix", g_mix, True, collective_id=4)
    own_mlp = [_add_chips("rs_add_chips_mlp%d" % k, chip, p, q) for k, (p, q) in enumerate(zip(parts_mlp, (q_w1, q_w2)))]
    sib_mlp = _sibling_swap("rs_swap_reduced_mlp", own_mlp, False, collective_id=5)
    gg_blk = jnp.concatenate([_diag_blocks(g_gate[:, 0:D_LRU]), _diag_blocks(g_gate[:, D_LRU:2 * D_LRU])], axis=1)
    gad, gam, gl, gg = _allgather8_seq("allgather_small_grads", 8, [accd, accm, accl, gg_blk.astype(BF16)])

    parts_mix = [_add_sibling("rs_add_sibling_mix%d" % k, g, r, core)
                 for k, (g, r) in enumerate(zip(g_mix, recv_mix))]
    landed_mix = _exchange_chips("rs_exchange_mix", 3, parts_mix)
    res_w1, res_w2 = _adam("adam_mlp", core, [(w_mlp1[0], own_mlp[0], sib_mlp[0], m_w_mlp1[0], v_w_mlp1[0]),
                                              (w_mlp2[0], own_mlp[1], sib_mlp[1], m_w_mlp2[0], v_w_mlp2[0])])
    own_mix = [_add_chips("rs_add_chips_mix%d" % k, chip, p, q, after=[res_w1[1]])
               for k, (p, q) in enumerate(zip(parts_mix, landed_mix))]
    sib_mix = _sibling_swap("rs_swap_reduced_mix", own_mix, False)
    (res_win,) = _adam("adam_w_in", core, [(w_in[0], own_mix[0], sib_mix[0], m_w_in[0], v_w_in[0])])
    (res_wout,) = _adam("adam_w_out", core, [(w_out[0], own_mix[1], sib_mix[1], m_w_out[0], v_w_out[0])])

    params = {
        "ada_b": (ada_b, m_ada_b, v_ada_b), "norm1_g": (norm1_g, m_norm1_g, v_norm1_g),
        "lru_conv_b": (lru_conv_b, m_lru_conv_b, v_lru_conv_b),
        "gate_a_w": tuple(t.reshape(D_LRU, HEAD) for t in (gate_a_w, m_gate_a_w, v_gate_a_w)),
        "gate_a_b": (gate_a_b, m_gate_a_b, v_gate_a_b),
        "gate_x_w": tuple(t.reshape(D_LRU, HEAD) for t in (gate_x_w, m_gate_x_w, v_gate_x_w)),
        "gate_x_b": (gate_x_b, m_gate_x_b, v_gate_x_b), "a_param": (a_param, m_a_param, v_a_param),
        "lru_conv_w": tuple(t[0] for t in (lru_conv_w, m_lru_conv_w, v_lru_conv_w)),
        "short_conv_w": tuple(t[0] for t in (short_conv_w, m_short_conv_w, v_short_conv_w)),
        "lru_out_g": (lru_out_g, m_lru_out_g, v_lru_out_g), "conv_out_g": (conv_out_g, m_conv_out_g, v_conv_out_g),
        "norm2_g": (norm2_g, m_norm2_g, v_norm2_g),
        "final_g": tuple(t[None, :] for t in (final_g, m_final_g, v_final_g)),
    }
    small, loss_blk, dmod_cols = _small_update(gad, gam, gl, gg, chip, params)
    loss = loss_blk[0, 0]

    ncol = 6 * D_MODEL // N_CHIP
    dmod_loc = lax.dynamic_slice_in_dim(dmod_cols, mychip * ncol, ncol, axis=1)
    sct = (c_all * jax.nn.sigmoid(c_all)).T
    ada = _ada_grad_adam(sct, dmod_loc, ada_w[0], m_ada_w[0], v_ada_w[0])

    res = {"ada_w": ada, "w_in": res_win, "w_out": res_wout, "w_mlp1": res_w1, "w_mlp2": res_w2}
    res = {n: tuple(t[None] for t in r) for n, r in res.items()}
    shapes = {"gate_a_w": gate_a_w.shape, "gate_x_w": gate_x_w.shape, "lru_conv_w": lru_conv_w.shape,
              "short_conv_w": short_conv_w.shape, "final_g": final_g.shape}
    for n, t in small.items():
        res[n] = tuple(u.reshape(shapes[n]) if n in shapes else u for u in t)

    order = ["ada_w", "ada_b", "norm1_g", "w_in", "lru_conv_w", "lru_conv_b", "gate_a_w", "gate_a_b", "gate_x_w",
             "gate_x_b", "a_param", "short_conv_w", "lru_out_g", "conv_out_g", "w_out", "norm2_g", "w_mlp1",
             "w_mlp2", "final_g"]
    return (loss, grad_x[None], *[res[n][0] for n in order], *[res[n][1] for n in order],
            *[res[n][2] for n in order], *[res[n][3] for n in order])
```

```python
import jax
import jax.numpy as jnp
from jax import lax
from jax.experimental import pallas as pl
from jax.experimental.pallas import tpu as pltpu
from jax.experimental.pallas import tpu_sc as plsc

F32 = jnp.float32
BF16 = jnp.bfloat16

D_MODEL = 1024
D_LRU = 512
D_IN = 2560
D_FF = 4096
N_CHIP = 4
WIN_BLK = D_IN // N_CHIP
WOUT_BLK = D_MODEL // N_CHIP
FF_BLK = D_FF // N_CHIP
HEAD = 64
EPS = 1e-6
C_GATE = 8.0
TOKEN_TILE = 256
MIX_FWD_TILE = 512
HALO = 8
VMEM_LIMIT = 60 * 1024 * 1024

ADAM_LR = 0.001
ADAM_B1 = 0.9
ADAM_B2 = 0.999
ADAM_EPS = 1e-08
ADAM_WD = 0.01
ADAM_STEP = 10

MESH = pl.DeviceIdType.MESH
ANY = pl.BlockSpec(memory_space=pl.ANY)
VMEM = pl.BlockSpec(memory_space=pltpu.VMEM)
SMEM = pl.BlockSpec(memory_space=pltpu.SMEM)


def _full(shape, single=False):
    nd = len(shape)
    if single:
        return pl.BlockSpec(shape, lambda *_: (0,) * nd, pipeline_mode=pl.Buffered(1))
    return pl.BlockSpec(shape, lambda *_: (0,) * nd)


def _dot(a, b):
    return jnp.dot(a, b, preferred_element_type=F32)


def _dot_nt(a, b):
    return lax.dot_general(a, b, (((1,), (1,)), ((), ())), preferred_element_type=F32)


def _dot_tn(a, b):
    return lax.dot_general(a, b, (((0,), (0,)), ((), ())), preferred_element_type=F32)


def _gmean(v, a64):
    hi = v.astype(BF16)
    lo = (v - hi.astype(F32)).astype(BF16)
    return _dot(hi, a64) + _dot(lo, a64)


def _gelu(x):
    u = 0.7978845608028654 * (x + 0.044715 * x * x * x)
    t = jnp.tanh(u)
    return 0.5 * x * (1.0 + t), t


def _gelu_grad(x, t):
    du = 0.7978845608028654 * (1.0 + 3.0 * 0.044715 * x * x)
    return 0.5 * (1.0 + t) + 0.5 * x * (1.0 - t * t) * du


def _log1p_pos(y):
    return jnp.where(y < 1e-2, y * (1.0 - y * (0.5 - y * (1.0 / 3.0 - y * 0.25))), jnp.log(1.0 + y))


def _softplus(a):
    return jnp.maximum(a, 0.0) + _log1p_pos(jnp.exp(-jnp.abs(a)))


def _neg_expm1(z):
    series = -z * (1.0 + z * (0.5 + z * (1.0 / 6.0 + z * (1.0 / 24.0))))
    return jnp.where(z > -0.02, series, 1.0 - jnp.exp(z))


def _scan_fwd(a, b, row):
    n = a.shape[0]
    d = 1
    while d < n:
        m = row >= d
        b = jnp.where(m, a * pltpu.roll(b, d, 0) + b, b)
        a = jnp.where(m, a * pltpu.roll(a, d, 0), a)
        d *= 2
    return a, b


def _scan_rev(a, b, row):
    n = a.shape[0]
    d = 1
    while d < n:
        m = row < n - d
        b = jnp.where(m, b + a * pltpu.roll(b, n - d, 0), b)
        a = jnp.where(m, a * pltpu.roll(a, n - d, 0), a)
        d *= 2
    return a, b


def _colsum(v):
    return jnp.sum(v, axis=0, keepdims=True)


def _load_gathered(chip, gathered, own, slot, sems):
    copies = []
    for j in range(N_CHIP):
        @pl.when(chip == j)
        def _(j=j):
            pltpu.make_async_copy(own, slot(j), sems.at[j]).start()

        @pl.when(chip != j)
        def _(j=j):
            pltpu.make_async_copy(gathered.at[j], slot(j), sems.at[j]).start()

        copies.append(pltpu.make_async_copy(own, slot(j), sems.at[j]))
    return copies


def _lru_gates(xlb, gab, gbias, sp, first_row):
    g = _dot(xlb, gab) + gbias
    r = jax.nn.sigmoid(g[:, :D_LRU])
    ig = jax.nn.sigmoid(g[:, D_LRU:])
    la = (-C_GATE) * r * sp
    a = jnp.exp(la)
    msq = jnp.sqrt(_neg_expm1(2.0 * la))
    mult = jnp.where(first_row, 1.0, msq)
    return r, ig, a, msq, mult


def _mix_fwd(chip, x, mod, vecd, vecl, win, wout, gab, a64):
    s = x.shape[0]
    ts = MIX_FWD_TILE
    nt = s // ts

    def body(chip_ref, x_ref, mod_ref, vd_ref, vl_ref, win_hbm, win_own, wout_hbm, wout_own, gab_ref, a64_ref,
             hb_ref, proj_ref, hl_ref, ycat_ref, mixed_ref, x1_ref,
             win_ref, wout_ref, ext_lx, ext_cv, hcar, sems):
        i = pl.program_id(0)

        @pl.when(i == 0)
        def _():
            cps = _load_gathered(chip_ref[0], win_hbm, win_own, lambda j: win_ref.at[j], sems.at[pl.ds(0, N_CHIP)])
            cps += _load_gathered(chip_ref[0], wout_hbm, wout_own,
                                  lambda j: wout_ref.at[pl.ds(j * WOUT_BLK, WOUT_BLK), :],
                                  sems.at[pl.ds(N_CHIP, N_CHIP)])
            ext_lx[0:HALO, :] = jnp.zeros((HALO, D_LRU), F32)
            ext_cv[0:HALO, :] = jnp.zeros((HALO, D_LRU), F32)
            hcar[...] = jnp.zeros_like(hcar)
            for cp in cps:
                cp.wait()

        row = lax.broadcasted_iota(jnp.int32, (ts, D_LRU), 0)
        first_row = jnp.logical_and(row == 0, i == 0)
        xt = x_ref[...]
        shift1, scale1, gate1 = mod_ref[0:1, :], mod_ref[1:2, :], mod_ref[2:3, :]
        r1 = lax.rsqrt(jnp.mean(xt * xt, axis=-1, keepdims=True) + EPS)
        h = (xt * r1) * vd_ref[0:1, :] * (1.0 + scale1) + shift1
        hb = h.astype(BF16)
        hb_ref[...] = hb
        for j in range(N_CHIP):
            proj_ref[:, j * WIN_BLK:(j + 1) * WIN_BLK] = _dot(hb, win_ref[j])
        u_ly = proj_ref[:, 512:1024]
        u_b = proj_ref[:, 1024:1536]

        ext_lx[HALO:HALO + ts, :] = proj_ref[:, 0:512]
        xl = vl_ref[4:5, :] + vl_ref[0:1, :] * ext_lx[pl.ds(5, ts), :]
        for k in range(1, 4):
            xl = xl + vl_ref[k:k + 1, :] * ext_lx[pl.ds(5 + k, ts), :]
        ext_lx[0:HALO, :] = ext_lx[ts:ts + HALO, :]
        sp = _softplus(vl_ref[8:9, :])
        _, ig, a, _, mult = _lru_gates(xl.astype(BF16), gab_ref[...], vd_ref[3:4, :], sp, first_row)
        acum, hloc = _scan_fwd(a, mult * (ig * xl), row)
        hl = hloc + acum * hcar[0:1, :]
        hl_ref[...] = hl
        hcar[0:1, :] = hl_ref[ts - 1:ts, :]
        ge, _ = _gelu(u_ly)
        p = ge * hl
        y_lru = p * lax.rsqrt(_gmean(p * p, a64_ref[...]) + EPS) * vl_ref[9:10, :]
        ycat_ref[:, 0:512] = y_lru.astype(BF16)

        ext_cv[HALO:HALO + ts, :] = proj_ref[:, 1536:2048] * proj_ref[:, 2048:2560]
        q = vl_ref[5:6, :] * ext_cv[pl.ds(6, ts), :]
        for k in range(1, 3):
            q = q + vl_ref[5 + k:6 + k, :] * ext_cv[pl.ds(6 + k, ts), :]
        ext_cv[0:HALO, :] = ext_cv[ts:ts + HALO, :]
        yc = u_b * q
        y_conv = yc * lax.rsqrt(_gmean(yc * yc, a64_ref[...]) + EPS) * vl_ref[10:11, :]
        ycat_ref[:, 512:1024] = y_conv.astype(BF16)

        mixed = _dot(ycat_ref[...], wout_ref[...])
        mixed_ref[...] = mixed
        x1_ref[...] = xt + gate1 * mixed

    tile = lambda w: pl.BlockSpec((ts, w), lambda i: (i, 0))
    return pl.pallas_call(
        body, name="mix_fwd", grid=(nt,),
        in_specs=[SMEM, tile(D_MODEL), _full((8, D_MODEL)), _full((8, D_MODEL)), _full((16, D_LRU)),
                  ANY, ANY, ANY, ANY, _full((D_LRU, 2 * D_LRU), True), _full((D_LRU, D_LRU), True)],
        out_specs=[tile(D_MODEL), tile(D_IN), tile(D_LRU), tile(D_MODEL), tile(D_MODEL), tile(D_MODEL)],
        out_shape=[jax.ShapeDtypeStruct((s, D_MODEL), BF16), jax.ShapeDtypeStruct((s, D_IN), F32),
                   jax.ShapeDtypeStruct((s, D_LRU), F32), jax.ShapeDtypeStruct((s, D_MODEL), BF16),
                   jax.ShapeDtypeStruct((s, D_MODEL), F32), jax.ShapeDtypeStruct((s, D_MODEL), F32)],
        scratch_shapes=[pltpu.VMEM((N_CHIP, D_MODEL, WIN_BLK), BF16), pltpu.VMEM((D_MODEL, D_MODEL), BF16),
                        pltpu.VMEM((ts + HALO, D_LRU), F32), pltpu.VMEM((ts + HALO, D_LRU), F32),
                        pltpu.VMEM((HALO, D_LRU), F32), pltpu.SemaphoreType.DMA((2 * N_CHIP,))],
        compiler_params=pltpu.CompilerParams(dimension_semantics=("arbitrary",), vmem_limit_bytes=VMEM_LIMIT),
    )(chip, x, mod, vecd, vecl, *win, *wout, gab, a64)


def _mlp_fwd_bwd(chip, x1, target, mod, vecd, w1, w2):
    s = x1.shape[0]
    ts = TOKEN_TILE
    nt = s // ts

    def body(chip_ref, x1_ref, tg_ref, mod_ref, vd_ref, w1_hbm, w1_own, w2_hbm, w2_own,
             dx1_ref, act_ref, dz_ref, dmo_ref, h2_ref, acc_ref, w1_v, w2_v, rz_v, sems):
        i = pl.program_id(0)

        @pl.when(i == 0)
        def _():
            cps = _load_gathered(chip_ref[0], w1_hbm, w1_own, lambda j: w1_v.at[j], sems.at[pl.ds(0, N_CHIP)])
            cps += _load_gathered(chip_ref[0], w2_hbm, w2_own, lambda j: w2_v.at[j], sems.at[pl.ds(N_CHIP, N_CHIP)])
            acc_ref[...] = jnp.zeros_like(acc_ref)
            for cp in cps:
                cp.wait()

        xt = x1_ref[...]
        shift2, scale2, gate2 = mod_ref[3:4, :], mod_ref[4:5, :], mod_ref[5:6, :]
        g2, gf = vd_ref[1:2, :], vd_ref[2:3, :]
        r2 = lax.rsqrt(jnp.mean(xt * xt, axis=-1, keepdims=True) + EPS)
        n2 = xt * r2
        h2b = (n2 * g2 * (1.0 + scale2) + shift2).astype(BF16)
        h2_ref[...] = h2b
        for j in range(N_CHIP):
            rz_v[j] = jnp.maximum(_dot(h2b, w1_v[j]), 0.0)
        mo = jnp.zeros((ts, D_MODEL), F32)
        for j in range(N_CHIP):
            rz = rz_v[j]
            actb = (rz * rz).astype(BF16)
            act_ref[:, j * FF_BLK:(j + 1) * FF_BLK] = actb
            mo = mo + _dot(actb, w2_v[j])
        x2 = xt + gate2 * mo
        r3 = lax.rsqrt(jnp.mean(x2 * x2, axis=-1, keepdims=True) + EPS)
        n3 = x2 * r3
        e = n3 * gf - tg_ref[...]
        loss = (0.5 / D_MODEL) * jnp.sum(_colsum(e * e), axis=1, keepdims=True)
        dy = e * (1.0 / D_MODEL)
        acc_ref[4:5, :] += _colsum(dy * n3)
        acc_ref[5:6, :] += jnp.broadcast_to(loss, (1, D_MODEL))
        dn3 = dy * gf
        dx2 = r3 * (dn3 - n3 * jnp.mean(dn3 * n3, axis=-1, keepdims=True))
        acc_ref[2:3, :] += _colsum(dx2 * mo)
        dmob = (dx2 * gate2).astype(BF16)
        dmo_ref[...] = dmob
        for j in range(N_CHIP):
            dz_ref[:, j * FF_BLK:(j + 1) * FF_BLK] = (_dot_nt(dmob, w2_v[j]) * (2.0 * rz_v[j])).astype(BF16)
        dh2 = jnp.zeros((ts, D_MODEL), F32)
        for j in range(N_CHIP):
            dh2 = dh2 + _dot_nt(dz_ref[:, j * FF_BLK:(j + 1) * FF_BLK], w1_v[j])
        acc_ref[1:2, :] += _colsum(dh2 * (n2 * g2))
        acc_ref[0:1, :] += _colsum(dh2)
        dhn2 = dh2 * (1.0 + scale2)
        acc_ref[3:4, :] += _colsum(dhn2 * n2)
        dn2 = dhn2 * g2
        dx1_ref[...] = dx2 + r2 * (dn2 - n2 * jnp.mean(dn2 * n2, axis=-1, keepdims=True))

    tile = lambda w: pl.BlockSpec((ts, w), lambda i: (i, 0))
    return pl.pallas_call(
        body, name="mlp_fwd_bwd", grid=(nt,),
        in_specs=[SMEM, tile(D_MODEL), tile(D_MODEL), _full((8, D_MODEL)), _full((8, D_MODEL)), ANY, ANY, ANY, ANY],
        out_specs=[tile(D_MODEL), tile(D_FF), tile(D_FF), tile(D_MODEL), tile(D_MODEL), _full((8, D_MODEL))],
        out_shape=[jax.ShapeDtypeStruct((s, D_MODEL), F32), jax.ShapeDtypeStruct((s, D_FF), BF16),
                   jax.ShapeDtypeStruct((s, D_FF), BF16), jax.ShapeDtypeStruct((s, D_MODEL), BF16),
                   jax.ShapeDtypeStruct((s, D_MODEL), BF16), jax.ShapeDtypeStruct((8, D_MODEL), F32)],
        scratch_shapes=[pltpu.VMEM((N_CHIP, D_MODEL, FF_BLK), BF16), pltpu.VMEM((N_CHIP, FF_BLK, D_MODEL), BF16),
                        pltpu.VMEM((N_CHIP, ts, FF_BLK), F32), pltpu.SemaphoreType.DMA((2 * N_CHIP,))],
        compiler_params=pltpu.CompilerParams(dimension_semantics=("arbitrary",), vmem_limit_bytes=VMEM_LIMIT),
    )(chip, x1, target, mod, vecd, *w1, *w2)


def _mix_bwd(chip, dx1, x, mixed, proj, hl, hb, ycat, mod, vecd, vecl, win, wout, gab, a64):
    s = x.shape[0]
    ts = TOKEN_TILE
    nt = s // ts
    hpt = ts // HALO

    def body(chip_ref, dx1_ref, x_ref, mixed_ref, proj_ref, projh_ref, hl_ref, hlh_ref, hb_ref, ycat_ref,
             mod_ref, vd_ref, vl_ref, win_hbm, win_own, wout_hbm, wout_own, gab_ref, a64_ref,
             gx_ref, accd_ref, accl_ref, gwin_hbm, gwout_hbm, ggate_hbm,
             win_ref, wout_ref, dproj_ref, dgb_ref, gwin_acc, gwout_acc, ggate_acc,
             ext_lx, ext_cv, ext_hl, ext_dxl, ext_dq, gbuf, gcar, acar, sems):
        i = pl.program_id(0)
        ri = nt - 1 - i

        @pl.when(i == 0)
        def _():
            gwin_acc[...] = jnp.zeros_like(gwin_acc)
            gwout_acc[...] = jnp.zeros_like(gwout_acc)
            ggate_acc[...] = jnp.zeros_like(ggate_acc)
            cps = _load_gathered(chip_ref[0], win_hbm, win_own, lambda j: win_ref.at[j], sems.at[pl.ds(0, N_CHIP)])
            cps += _load_gathered(chip_ref[0], wout_hbm, wout_own,
                                  lambda j: wout_ref.at[pl.ds(j * WOUT_BLK, WOUT_BLK), :],
                                  sems.at[pl.ds(N_CHIP, N_CHIP)])
            for cp in cps:
                cp.wait()
            accd_ref[...] = jnp.zeros_like(accd_ref)
            accl_ref[...] = jnp.zeros_like(accl_ref)
            ext_dxl[ts:ts + HALO, :] = jnp.zeros((HALO, D_LRU), F32)
            ext_dq[ts:ts + HALO, :] = jnp.zeros((HALO, D_LRU), F32)
            gcar[...] = jnp.zeros_like(gcar)
            acar[...] = jnp.zeros_like(acar)

        row = lax.broadcasted_iota(jnp.int32, (ts, D_LRU), 0)
        first_row = jnp.logical_and(row == 0, ri == 0)
        halo_on = jnp.where(ri == 0, 0.0, 1.0)
        shift1, scale1, gate1 = mod_ref[0:1, :], mod_ref[1:2, :], mod_ref[2:3, :]
        g1 = vd_ref[0:1, :]
        a64m = a64_ref[...]
        lg, cg = vl_ref[9:10, :], vl_ref[10:11, :]

        dx1 = dx1_ref[...]
        accd_ref[2:3, :] += _colsum(dx1 * mixed_ref[...])
        dmb = (dx1 * gate1).astype(BF16)
        gwout_acc[...] += _dot_tn(ycat_ref[...], dmb)
        dycat = _dot_nt(dmb, wout_ref[...])
        dyl = dycat[:, 0:512]
        dyv = dycat[:, 512:1024]

        u_ly = proj_ref[:, 512:1024]
        u_b = proj_ref[:, 1024:1536]
        u_c = proj_ref[:, 1536:2048]
        u_v = proj_ref[:, 2048:2560]
        ext_lx[0:HALO, :] = projh_ref[:, 0:512] * halo_on
        ext_lx[HALO:HALO + ts, :] = proj_ref[:, 0:512]
        xl = vl_ref[4:5, :] + vl_ref[0:1, :] * ext_lx[pl.ds(5, ts), :]
        for k in range(1, 4):
            xl = xl + vl_ref[k:k + 1, :] * ext_lx[pl.ds(5 + k, ts), :]
        xlb = xl.astype(BF16)
        sp = _softplus(vl_ref[8:9, :])
        r, ig, a, msq, mult = _lru_gates(xlb, gab_ref[...], vd_ref[3:4, :], sp, first_row)
        hl = hl_ref[...]
        ge, th = _gelu(u_ly)
        p = ge * hl
        rl = lax.rsqrt(_gmean(p * p, a64m) + EPS)
        nl = p * rl
        ext_cv[0:HALO, :] = projh_ref[:, 1536:2048] * projh_ref[:, 2048:2560] * halo_on
        ext_cv[HALO:HALO + ts, :] = u_c * u_v
        q = vl_ref[5:6, :] * ext_cv[pl.ds(6, ts), :]
        for k in range(1, 3):
            q = q + vl_ref[5 + k:6 + k, :] * ext_cv[pl.ds(6 + k, ts), :]
        yc = u_b * q
        rc = lax.rsqrt(_gmean(yc * yc, a64m) + EPS)
        nc = yc * rc

        accl_ref[9:10, :] += _colsum(dyl * nl)
        dnl = dyl * lg
        dp = rl * (dnl - nl * _gmean(dnl * nl, a64m))
        dproj_ref[:, 512:1024] = ((dp * hl) * _gelu_grad(u_ly, th)).astype(BF16)
        a_next = jnp.where(row == ts - 1, acar[0:1, :], pltpu.roll(a, ts - 1, 0))
        acum, gloc = _scan_rev(a_next, dp * ge, row)
        gbuf[...] = gloc + acum * gcar[0:1, :]
        gcar[0:1, :] = gbuf[0:1, :]
        ext_hl[0:HALO, :] = hlh_ref[...] * halo_on
        ext_hl[HALO:HALO + ts, :] = hl
        acar[...] = a[0:HALO, :]
        gt = gbuf[...]
        da = gt * ext_hl[pl.ds(HALO - 1, ts), :]
        dmult = gt * ig * xl
        di = gt * mult * xl
        dxl = gt * mult * ig
        dla = da * a - jnp.where(first_row, 0.0, dmult * a * a / msq)
        accl_ref[8:9, :] += _colsum(dla * ((-C_GATE) * r))
        dra = dla * ((-C_GATE) * sp) * r * (1.0 - r)
        dia = di * ig * (1.0 - ig)
        accd_ref[4:5, 0:D_LRU] += _colsum(dra)
        accd_ref[4:5, D_LRU:2 * D_LRU] += _colsum(dia)
        dgb_ref[:, 0:D_LRU] = dra.astype(BF16)
        dgb_ref[:, D_LRU:2 * D_LRU] = dia.astype(BF16)
        dxl = dxl + _dot_nt(dgb_ref[...], gab_ref[...])
        ggate_acc[...] += _dot_tn(xlb, dgb_ref[...])
        accl_ref[4:5, :] += _colsum(dxl)
        for k in range(4):
            accl_ref[k:k + 1, :] += _colsum(dxl * ext_lx[pl.ds(5 + k, ts), :])
        ext_dxl[0:ts, :] = dxl
        du_lx = vl_ref[0:1, :] * ext_dxl[pl.ds(3, ts), :]
        for k in range(1, 4):
            du_lx = du_lx + vl_ref[k:k + 1, :] * ext_dxl[pl.ds(3 - k, ts), :]
        ext_dxl[ts:ts + HALO, :] = ext_dxl[0:HALO, :]
        dproj_ref[:, 0:512] = du_lx.astype(BF16)

        accl_ref[10:11, :] += _colsum(dyv * nc)
        dnc = dyv * cg
        dyc = rc * (dnc - nc * _gmean(dnc * nc, a64m))
        dproj_ref[:, 1024:1536] = (dyc * q).astype(BF16)
        dq = dyc * u_b
        for k in range(3):
            accl_ref[5 + k:6 + k, :] += _colsum(dq * ext_cv[pl.ds(6 + k, ts), :])
        ext_dq[0:ts, :] = dq
        dcv = vl_ref[5:6, :] * ext_dq[pl.ds(2, ts), :]
        for k in range(1, 3):
            dcv = dcv + vl_ref[5 + k:6 + k, :] * ext_dq[pl.ds(2 - k, ts), :]
        ext_dq[ts:ts + HALO, :] = ext_dq[0:HALO, :]
        dproj_ref[:, 1536:2048] = (dcv * u_v).astype(BF16)
        dproj_ref[:, 2048:2560] = (dcv * u_c).astype(BF16)

        dh = _dot_nt(dproj_ref[:, 0:WIN_BLK], win_ref[0])
        for j in range(1, N_CHIP):
            dh = dh + _dot_nt(dproj_ref[:, j * WIN_BLK:(j + 1) * WIN_BLK], win_ref[j])
        for j in range(N_CHIP):
            gwin_acc[j] += _dot_tn(hb_ref[...], dproj_ref[:, j * WIN_BLK:(j + 1) * WIN_BLK])
        xt = x_ref[...]
        r1 = lax.rsqrt(jnp.mean(xt * xt, axis=-1, keepdims=True) + EPS)
        n1 = xt * r1
        accd_ref[1:2, :] += _colsum(dh * (n1 * g1))
        accd_ref[0:1, :] += _colsum(dh)
        dhn1 = dh * (1.0 + scale1)
        accd_ref[3:4, :] += _colsum(dhn1 * n1)
        dn1 = dhn1 * g1
        gx_ref[...] = dx1 + r1 * (dn1 - n1 * jnp.mean(dn1 * n1, axis=-1, keepdims=True))

        @pl.when(i == nt - 1)
        def _():
            outs = [pltpu.make_async_copy(acc, dst, sems.at[k]) for k, (acc, dst) in enumerate(
                ((gwin_acc, gwin_hbm), (gwout_acc, gwout_hbm), (ggate_acc, ggate_hbm)))]
            for cp in outs:
                cp.start()
            for cp in outs:
                cp.wait()

    tile = lambda w: pl.BlockSpec((ts, w), lambda i: (nt - 1 - i, 0))
    halo = lambda w: pl.BlockSpec((HALO, w), lambda i: (jnp.maximum((nt - 1 - i) * hpt - 1, 0), 0))
    ext = pltpu.VMEM((ts + HALO, D_LRU), F32)
    return pl.pallas_call(
        body, name="mix_bwd", grid=(nt,),
        in_specs=[SMEM, tile(D_MODEL), tile(D_MODEL), tile(D_MODEL), tile(D_IN), halo(D_IN), tile(D_LRU), halo(D_LRU),
                  tile(D_MODEL), tile(D_MODEL), _full((8, D_MODEL)), _full((8, D_MODEL)), _full((16, D_LRU)),
                  ANY, ANY, ANY, ANY, _full((D_LRU, 2 * D_LRU), True), _full((D_LRU, D_LRU), True)],
        out_specs=[tile(D_MODEL), _full((8, D_MODEL)), _full((16, D_LRU)), ANY, ANY, ANY],
        out_shape=[jax.ShapeDtypeStruct((s, D_MODEL), F32),
                   jax.ShapeDtypeStruct((8, D_MODEL), F32), jax.ShapeDtypeStruct((16, D_LRU), F32),
                   jax.ShapeDtypeStruct((N_CHIP, D_MODEL, WIN_BLK), F32), jax.ShapeDtypeStruct((D_MODEL, D_MODEL), F32),
                   jax.ShapeDtypeStruct((D_LRU, 2 * D_LRU), F32)],
        scratch_shapes=[pltpu.VMEM((N_CHIP, D_MODEL, WIN_BLK), BF16), pltpu.VMEM((D_MODEL, D_MODEL), BF16),
                        pltpu.VMEM((ts, D_IN), BF16), pltpu.VMEM((ts, 2 * D_LRU), BF16),
                        pltpu.VMEM((N_CHIP, D_MODEL, WIN_BLK), F32), pltpu.VMEM((D_MODEL, D_MODEL), F32),
                        pltpu.VMEM((D_LRU, 2 * D_LRU), F32),
                        ext, ext, ext, ext, ext, pltpu.VMEM((ts, D_LRU), F32),
                        pltpu.VMEM((HALO, D_LRU), F32), pltpu.VMEM((HALO, D_LRU), F32),
                        pltpu.SemaphoreType.DMA((2 * N_CHIP,))],
        compiler_params=pltpu.CompilerParams(dimension_semantics=("arbitrary",), vmem_limit_bytes=VMEM_LIMIT),
    )(chip, dx1, x, mixed, proj, proj, hl, hl, hb, ycat, mod, vecd, vecl, *win, *wout, gab, a64)


def _wgrad_mlp(collective_id, h2b, dz, act, dmo):
    s = h2b.shape[0]
    nstep = 2 * N_CHIP
    half = FF_BLK // 2

    def body(h2_ref, dz_ref, act_ref, dmo_ref, own_hbm, recv_hbm, buf, keep_sems, send_sems, recv_sems):
        j = pl.program_id(0)
        x, y, c, _ = _position()

        def copies(jj):
            slot = jj % 2
            keep = pltpu.make_async_copy(buf.at[slot, pl.ds(c * half, half), :], own_hbm.at[jj], keep_sems.at[slot])
            give = pltpu.make_async_remote_copy(
                src_ref=buf.at[slot, pl.ds((1 - c) * half, half), :], dst_ref=recv_hbm.at[jj],
                send_sem=send_sems.at[slot], recv_sem=recv_sems.at[jj],
                device_id=(x, y, 1 - c), device_id_type=MESH)
            return keep, give

        @pl.when(j == 0)
        def _():
            pl.semaphore_signal(pltpu.get_barrier_semaphore(), inc=1, device_id=(x, y, 1 - c), device_id_type=MESH)

        @pl.when(j >= 2)
        def _():
            keep, give = copies(j - 2)
            keep.wait()
            give.wait_send()

        @pl.when(j < N_CHIP)
        def _():
            buf[j % 2] = _dot_tn(act_ref[...], dmo_ref[...]).astype(BF16)

        @pl.when(j >= N_CHIP)
        def _():
            buf[j % 2] = _dot_tn(h2_ref[...], dz_ref[...]).astype(BF16)

        @pl.when(j == 0)
        def _():
            pl.semaphore_wait(pltpu.get_barrier_semaphore(), 1)

        keep, give = copies(j)
        keep.start()
        give.start()

        @pl.when(j == nstep - 1)
        def _():
            for jj in (nstep - 2, nstep - 1):
                keep, give = copies(jj)
                keep.wait()
                give.wait_send()
            for jj in range(nstep):
                copies(jj)[1].wait_recv()

    sds = jax.ShapeDtypeStruct((nstep, half, D_MODEL), BF16)
    whole = pl.BlockSpec((s, D_MODEL), lambda j: (0, 0))
    return pl.pallas_call(
        body, name="wgrad_mlp", grid=(nstep,),
        in_specs=[whole, pl.BlockSpec((s, FF_BLK), lambda j: (0, jnp.maximum(j - N_CHIP, 0))),
                  pl.BlockSpec((s, FF_BLK), lambda j: (0, jnp.minimum(j, N_CHIP - 1))), whole],
        out_specs=[ANY, ANY], out_shape=[sds, sds],
        scratch_shapes=[pltpu.VMEM((2, FF_BLK, D_MODEL), BF16), pltpu.SemaphoreType.DMA((2,)),
                        pltpu.SemaphoreType.DMA((2,)), pltpu.SemaphoreType.DMA((nstep,))],
        compiler_params=pltpu.CompilerParams(dimension_semantics=("arbitrary",), vmem_limit_bytes=VMEM_LIMIT,
                                             collective_id=collective_id),
    )(h2b, dz, act, dmo)


def _mod_matmul(c_all, ada_w_loc):
    n = ada_w_loc.shape[1]
    cb = 512

    def body(c_ref, w_ref, o_ref):
        c = c_ref[...]
        sc = c * jax.nn.sigmoid(c)
        o_ref[...] = _dot(sc.astype(BF16), w_ref[...].astype(BF16))

    return pl.pallas_call(
        body, name="mod_matmul", grid=(n // cb,),
        in_specs=[_full((8, D_MODEL)), pl.BlockSpec((D_MODEL, cb), lambda j: (0, j))],
        out_specs=pl.BlockSpec((8, cb), lambda j: (0, j)),
        out_shape=jax.ShapeDtypeStruct((8, n), F32),
        compiler_params=pltpu.CompilerParams(dimension_semantics=("arbitrary",), vmem_limit_bytes=VMEM_LIMIT),
    )(c_all, ada_w_loc)


def _adam_math(w, g, m, v):
    m = ADAM_B1 * m + (1.0 - ADAM_B1) * g
    v = ADAM_B2 * v + (1.0 - ADAM_B2) * (g * g)
    m_hat = m / (1.0 - ADAM_B1 ** ADAM_STEP)
    v_hat = v / (1.0 - ADAM_B2 ** ADAM_STEP)
    delta = (-ADAM_LR) * (m_hat / (jnp.sqrt(v_hat) + ADAM_EPS) + ADAM_WD * w)
    return delta, m, v


def _adam(name, core, shards):
    n = len(shards)
    r, c = shards[0][0].shape
    half = r // 2
    rb = min(half, 128)
    nh = half // rb

    def body(core_ref, *refs):
        ins, outs = refs[:5 * n], refs[5 * n:]
        mine = (pl.program_id(0) // nh) == core_ref[0]
        for k in range(n):
            w_ref, go_ref, gs_ref, m_ref, v_ref = ins[5 * k:5 * k + 5]
            g_ref, d_ref, mo_ref, vo_ref = outs[4 * k:4 * k + 4]
            g = jnp.where(mine, go_ref[...], gs_ref[...])
            g_ref[...] = g
            d_ref[...], mo_ref[...], vo_ref[...] = _adam_math(w_ref[...], g, m_ref[...], v_ref[...])

    spec = pl.BlockSpec((rb, c), lambda i, core_ref: (i, 0))
    own = pl.BlockSpec((rb, c), lambda i, core_ref: (jnp.where(i // nh == core_ref[0], i % nh, 0), 0))
    sib = pl.BlockSpec((rb, c), lambda i, core_ref: (jnp.where(i // nh == core_ref[0], 0, i % nh), 0))
    sds = jax.ShapeDtypeStruct((r, c), F32)
    res = pl.pallas_call(
        body, name=name,
        grid_spec=pltpu.PrefetchScalarGridSpec(
            num_scalar_prefetch=1, grid=(r // rb,),
            in_specs=[spec, own, sib, spec, spec] * n, out_specs=[spec] * (4 * n)),
        out_shape=[sds] * (4 * n),
        compiler_params=pltpu.CompilerParams(dimension_semantics=("arbitrary",), vmem_limit_bytes=VMEM_LIMIT),
    )(core, *[t for s in shards for t in s])
    return [res[4 * k:4 * k + 4] for k in range(n)]


def _ada_grad_adam(chip, sct, dmod_cols, w, m, v):
    r, c = w.shape
    rb = 256

    def body(chip_ref, s_ref, dm_ref, w_ref, m_ref, v_ref, g_ref, d_ref, mo_ref, vo_ref):
        g = s_ref[:, 0:1] * dm_ref[0:1, :]
        for b in range(1, 8):
            g = g + s_ref[:, b:b + 1] * dm_ref[b:b + 1, :]
        g_ref[...] = g
        d_ref[...], mo_ref[...], vo_ref[...] = _adam_math(w_ref[...], g, m_ref[...], v_ref[...])

    spec = pl.BlockSpec((rb, c), lambda i, chip_ref: (i, 0))
    sds = jax.ShapeDtypeStruct((r, c), F32)
    return pl.pallas_call(
        body, name="ada_grad_adam",
        grid_spec=pltpu.PrefetchScalarGridSpec(
            num_scalar_prefetch=1, grid=(r // rb,),
            in_specs=[pl.BlockSpec((rb, 8), lambda i, chip_ref: (i, 0)),
                      pl.BlockSpec((8, c), lambda i, chip_ref: (0, chip_ref[0])), spec, spec, spec],
            out_specs=[spec] * 4),
        out_shape=[sds] * 4,
        compiler_params=pltpu.CompilerParams(dimension_semantics=("arbitrary",), vmem_limit_bytes=VMEM_LIMIT),
    )(chip, sct, dmod_cols, w, m, v)


def _position():
    x, y, c = lax.axis_index("x"), lax.axis_index("y"), lax.axis_index("c")
    chips = [(1 - x, y), (x, 1 - y), (1 - x, 1 - y)]
    return x, y, c, chips


def _ag8_run(ins, outs, send_sems, recv_sems, local_sems):
    na = len(ins)
    x, y, c, chips = _position()
    me, sibling = (x, y, c), (x, y, 1 - c)
    first, passed, local = [], [], []
    for a in range(na):
        m_per = ins[a].shape[0]

        def rows(px, py, pc, a=a, m_per=m_per):
            return outs[a].at[pl.ds((4 * px + 2 * py + pc) * m_per, m_per), :]

        def copy(k, block, to, src=None, a=a, rows=rows):
            return pltpu.make_async_remote_copy(
                src_ref=rows(*block) if src is None else src, dst_ref=rows(*block),
                send_sem=send_sems.at[7 * a + k], recv_sem=recv_sems.at[7 * a + k],
                device_id=to, device_id_type=MESH)

        mine = pltpu.make_async_copy(ins[a], rows(*me), local_sems.at[a])
        mine.start()
        local.append(mine)
        f = [copy(0, me, sibling, src=ins[a])]
        f += [copy(1 + j, me, (*chip, c), src=ins[a]) for j, chip in enumerate(chips)]
        for cp in f:
            cp.start()
        first.append((f, copy))
    for a in range(na):
        f, copy = first[a]
        p = [copy(4 + j, (*chip, c), sibling) for j, chip in enumerate(chips)]
        for j, chip in enumerate(chips):
            copy(1 + j, (*chip, c), me).wait_recv()
            p[j].start()
        passed.append(p)
    for a in range(na):
        f, copy = first[a]
        copy(0, sibling, me).wait_recv()
        for j, chip in enumerate(chips):
            copy(4 + j, (*chip, 1 - c), me).wait_recv()
        for cp in f + passed[a]:
            cp.wait_send()
        local[a].wait()


def _allgather8_seq(name, collective_id, arrs):
    na = len(arrs)
    hbm = pltpu.MemorySpace.HBM
    ins = [jax.new_ref(a, memory_space=hbm) for a in arrs]
    outs = [jax.empty_ref(jax.ShapeDtypeStruct((8 * a.shape[0], a.shape[1]), a.dtype), memory_space=hbm) for a in arrs]

    @pl.kernel(mesh=plsc.ScalarSubcoreMesh(axis_name="sequencer", num_cores=1), name=name,
               scratch_types=(pltpu.SemaphoreType.DMA((7 * na,)), pltpu.SemaphoreType.DMA((7 * na,)),
                              pltpu.SemaphoreType.DMA((na,))),
               compiler_params=pltpu.CompilerParams(collective_id=collective_id))
    def launch(send_sems, recv_sems, local_sems):
        x, y, c, chips = _position()
        peers = [(x, y, 1 - c)] + [(*chip, c) for chip in chips]
        barrier = pltpu.get_barrier_semaphore()
        for peer in peers:
            pl.semaphore_signal(barrier, inc=1, device_id=peer, device_id_type=MESH)
        pl.semaphore_wait(barrier, len(peers))
        _ag8_run(ins, outs, send_sems, recv_sems, local_sems)

    launch()
    return [o[...] for o in outs]


def _allgather8(name, arrs):
    na = len(arrs)

    def body(*refs):
        _ag8_run(refs[:na], refs[na:2 * na], *refs[2 * na:])

    return pl.pallas_call(
        body, name=name,
        out_shape=[jax.ShapeDtypeStruct((8 * a.shape[0], a.shape[1]), a.dtype) for a in arrs],
        in_specs=[VMEM] * na, out_specs=[VMEM] * na,
        scratch_shapes=[pltpu.SemaphoreType.DMA((7 * na,)), pltpu.SemaphoreType.DMA((7 * na,)),
                        pltpu.SemaphoreType.DMA((na,))],
        compiler_params=pltpu.CompilerParams(vmem_limit_bytes=VMEM_LIMIT),
    )(*arrs)


AG_SEMS = 7


def _ag_copies(ins, outs, send_sems, recv_sems):
    x, y, c, chips = _position()
    sibling = (x, y, 1 - c)
    xn, yn, dg = [2 * chip[0] + chip[1] for chip in chips]
    to_x, to_y = (1 - x, y, c), (x, 1 - y, c)
    res = []
    for a in range(len(ins)):
        half = ins[a].shape[0] // 2
        quarter = half // 2

        def copy(k, dst, to, src=None, a=a):
            return pltpu.make_async_remote_copy(
                src_ref=dst if src is None else src, dst_ref=dst,
                send_sem=send_sems.at[AG_SEMS * a + k], recv_sem=recv_sems.at[AG_SEMS * a + k],
                device_id=to, device_id_type=MESH)

        def rows(chip, pc, q=None, a=a, half=half, quarter=quarter):
            if q is None:
                return outs[a].at[chip, pl.ds(pc * half, half), :]
            return outs[a].at[chip, pl.ds(pc * half + q * quarter, quarter), :]

        own = ins[a].at[pl.ds(c * half, half), :]
        mine = rows(2 * x + y, c)
        res.append(dict(
            sends=[copy(0, mine, to_x, src=own), copy(1, mine, to_y, src=own)],
            from_x=copy(0, rows(xn, c), to_x), from_y=copy(1, rows(yn, c), to_y),
            relay_y=copy(2, rows(xn, c, 0), to_y), relay_x=copy(3, rows(yn, c, 1), to_x),
            from_y_relay=copy(2, rows(dg, c, 0), to_y), from_x_relay=copy(3, rows(dg, c, 1), to_x),
            pass_on=[copy(4, rows(xn, c), sibling), copy(5, rows(yn, c), sibling), copy(6, rows(dg, c), sibling)],
            from_sibling=[copy(4, rows(xn, 1 - c), sibling), copy(5, rows(yn, 1 - c), sibling),
                          copy(6, rows(dg, 1 - c), sibling)]))
    return res


def _ag_start(ins, outs, send_sems, recv_sems):
    for cps in _ag_copies(ins, outs, send_sems, recv_sems):
        for cp in cps["sends"]:
            cp.start()


def _ag_relay(ins, outs, send_sems, recv_sems, which):
    copies = _ag_copies(ins, outs, send_sems, recv_sems)
    for a in which:
        cps = copies[a]
        cps["from_x"].wait_recv()
        cps["relay_y"].start()
        cps["pass_on"][0].start()
        cps["from_y"].wait_recv()
        cps["relay_x"].start()
        cps["pass_on"][1].start()


def _ag_complete(ins, outs, send_sems, recv_sems):
    copies = _ag_copies(ins, outs, send_sems, recv_sems)
    for cps in copies:
        cps["from_y_relay"].wait_recv()
        cps["from_x_relay"].wait_recv()
        cps["pass_on"][2].start()
    for cps in copies:
        for cp in cps["from_sibling"]:
            cp.wait_recv()
        for cp in cps["sends"] + [cps["relay_y"], cps["relay_x"]] + cps["pass_on"]:
            cp.wait_send()


def _ag_finish(ins, outs, send_sems, recv_sems):
    _ag_relay(ins, outs, send_sems, recv_sems, range(len(ins)))
    _ag_complete(ins, outs, send_sems, recv_sems)


def _allgather_weights(name, collective_id, shards):
    na = len(shards)
    hbm = pltpu.MemorySpace.HBM
    ins = [jax.new_ref(s, memory_space=hbm) for s in shards]
    outs = [jax.empty_ref(jax.ShapeDtypeStruct((N_CHIP,) + s.shape, s.dtype), memory_space=hbm) for s in shards]

    @pl.kernel(mesh=plsc.ScalarSubcoreMesh(axis_name="sequencer", num_cores=1), name=name,
               scratch_types=(pltpu.SemaphoreType.DMA((AG_SEMS * na,)), pltpu.SemaphoreType.DMA((AG_SEMS * na,))),
               compiler_params=pltpu.CompilerParams(collective_id=collective_id))
    def launch(send_sems, recv_sems):
        x, y, c, _ = _position()
        peers = [(1 - x, y, c), (x, 1 - y, c), (x, y, 1 - c)]
        barrier = pltpu.get_barrier_semaphore()
        for peer in peers:
            pl.semaphore_signal(barrier, inc=1, device_id=peer, device_id_type=MESH)
        pl.semaphore_wait(barrier, len(peers))
        _ag_start(ins, outs, send_sems, recv_sems)
        _ag_finish(ins, outs, send_sems, recv_sems)

    launch()
    return [o[...] for o in outs]


def _sibling_swap(name, arrs, split_rows, collective_id=None):
    na = len(arrs)
    shapes = [jax.ShapeDtypeStruct((a.shape[0], a.shape[1] // 2, a.shape[2]) if split_rows else a.shape, a.dtype)
              for a in arrs]

    def run(ins, outs, send_sems, recv_sems):
        x, y, c, _ = _position()
        cps = []
        for a in range(na):
            src = ins[a]
            if split_rows:
                half = src.shape[1] // 2
                src = src.at[:, pl.ds((1 - c) * half, half), :]
            cp = pltpu.make_async_remote_copy(
                src_ref=src, dst_ref=outs[a], send_sem=send_sems.at[a], recv_sem=recv_sems.at[a],
                device_id=(x, y, 1 - c), device_id_type=MESH)
            cp.start()
            cps.append(cp)
        for cp in cps:
            cp.wait()

    sems = (pltpu.SemaphoreType.DMA((na,)), pltpu.SemaphoreType.DMA((na,)))
    if collective_id is None:
        return pl.pallas_call(
            lambda *refs: run(refs[:na], refs[na:2 * na], *refs[2 * na:]), name=name, out_shape=shapes,
            in_specs=[ANY] * na, out_specs=[ANY] * na, scratch_shapes=list(sems))(*arrs)

    hbm = pltpu.MemorySpace.HBM
    ins = [jax.new_ref(a, memory_space=hbm) for a in arrs]
    outs = [jax.empty_ref(s, memory_space=hbm) for s in shapes]

    @pl.kernel(mesh=plsc.ScalarSubcoreMesh(axis_name="sequencer", num_cores=1), name=name, scratch_types=sems,
               compiler_params=pltpu.CompilerParams(collective_id=collective_id))
    def launch(send_sems, recv_sems):
        x, y, c, _ = _position()
        barrier = pltpu.get_barrier_semaphore()
        pl.semaphore_signal(barrier, inc=1, device_id=(x, y, 1 - c), device_id_type=MESH)
        pl.semaphore_wait(barrier, 1)
        run(ins, outs, send_sems, recv_sems)

    launch()
    return [o[...] for o in outs]


def _xchg_copies(ins, outs, send_sems, recv_sems):
    x, y, c, chips = _position()
    return [pltpu.make_async_remote_copy(
        src_ref=ins[a].at[2 * chip[0] + chip[1]], dst_ref=outs[a].at[j],
        send_sem=send_sems.at[3 * a + j], recv_sem=recv_sems.at[3 * a + j],
        device_id=(*chip, c), device_id_type=MESH) for a in range(len(ins)) for j, chip in enumerate(chips)]


def _exchange_chips(name, collective_id, parts):
    na = len(parts)
    hbm = pltpu.MemorySpace.HBM
    ins = [jax.new_ref(p, memory_space=hbm) for p in parts]
    outs = [jax.empty_ref(jax.ShapeDtypeStruct((3,) + p.shape[1:], p.dtype), memory_space=hbm) for p in parts]

    @pl.kernel(mesh=plsc.ScalarSubcoreMesh(axis_name="sequencer", num_cores=1), name=name,
               scratch_types=(pltpu.SemaphoreType.DMA((3 * na,)), pltpu.SemaphoreType.DMA((3 * na,))),
               compiler_params=pltpu.CompilerParams(collective_id=collective_id))
    def launch(send_sems, recv_sems):
        x, y, c, chips = _position()
        barrier = pltpu.get_barrier_semaphore()
        for chip in chips:
            pl.semaphore_signal(barrier, inc=1, device_id=(*chip, c), device_id_type=MESH)
        pl.semaphore_wait(barrier, len(chips))
        for cp in _xchg_copies(ins, outs, send_sems, recv_sems):
            cp.start()
        for cp in _xchg_copies(ins, outs, send_sems, recv_sems):
            cp.wait()

    launch()
    return [q[...] for q in outs]


def _add_sibling(name, grad, recv, core, after=()):
    _, r, c = grad.shape
    half = r // 2
    rb = half
    nrb = half // rb

    def body(core_ref, g_ref, r_ref, *refs):
        refs[-1][...] = (g_ref[...].astype(F32) + r_ref[...].astype(F32)).astype(BF16)

    return pl.pallas_call(
        body, name=name,
        grid_spec=pltpu.PrefetchScalarGridSpec(
            num_scalar_prefetch=1, grid=(N_CHIP, nrb),
            in_specs=[pl.BlockSpec((1, rb, c), lambda j, i, core_ref: (j, core_ref[0] * nrb + i, 0)),
                      pl.BlockSpec((1, rb, c), lambda j, i, core_ref: (j, i, 0))] + [ANY] * len(after),
            out_specs=pl.BlockSpec((1, rb, c), lambda j, i, core_ref: (j, i, 0))),
        out_shape=jax.ShapeDtypeStruct((N_CHIP, half, c), BF16),
        compiler_params=pltpu.CompilerParams(dimension_semantics=("arbitrary", "arbitrary"),
                                             vmem_limit_bytes=VMEM_LIMIT),
    )(core, grad, recv, *after)


def _add_halves(name, own, recv):
    n = own.shape[0] // N_CHIP
    _, half, c = own.shape
    rb = half

    def body(*refs):
        for k in range(n):
            refs[2 * n + k][...] = (refs[k][...].astype(F32) + refs[n + k][...].astype(F32)).astype(BF16)

    group = [pl.BlockSpec((1, rb, c), lambda j, i, k=k: (N_CHIP * k + j, i, 0)) for k in range(n)]
    spec = pl.BlockSpec((1, rb, c), lambda j, i: (j, i, 0))
    return pl.pallas_call(
        body, name=name, grid=(N_CHIP, half // rb), in_specs=group + group, out_specs=[spec] * n,
        out_shape=[jax.ShapeDtypeStruct((N_CHIP, half, c), BF16)] * n,
        compiler_params=pltpu.CompilerParams(dimension_semantics=("arbitrary", "arbitrary"),
                                             vmem_limit_bytes=VMEM_LIMIT),
    )(*([own] * n + [recv] * n))


def _add_chips(name, chip, p, q, after=()):
    _, half, c = q.shape
    rb = min(half, 256)

    def body(chip_ref, p_ref, q_ref, *refs):
        acc = p_ref[0].astype(F32)
        for j in range(3):
            acc = acc + q_ref[j].astype(F32)
        refs[-1][...] = acc

    return pl.pallas_call(
        body, name=name,
        grid_spec=pltpu.PrefetchScalarGridSpec(
            num_scalar_prefetch=1, grid=(half // rb,),
            in_specs=[pl.BlockSpec((1, rb, c), lambda i, chip_ref: (chip_ref[0], i, 0)),
                      pl.BlockSpec((3, rb, c), lambda i, chip_ref: (0, i, 0))] + [ANY] * len(after),
            out_specs=pl.BlockSpec((rb, c), lambda i, chip_ref: (i, 0))),
        out_shape=jax.ShapeDtypeStruct((half, c), F32),
        compiler_params=pltpu.CompilerParams(dimension_semantics=("arbitrary",), vmem_limit_bytes=VMEM_LIMIT),
    )(chip, p, q, *after)


def _small_update(gad, gam, gl, gg, mychip, params):
    names = ["ada_b", "norm1_g", "lru_conv_b", "gate_a_w", "gate_a_b", "gate_x_w", "gate_x_b", "a_param",
             "lru_conv_w", "short_conv_w", "lru_out_g", "conv_out_g", "norm2_g", "final_g"]
    flat = [t for n in names for t in params[n]]
    nin = len(flat)

    def body(chip_ref, gad_ref, gam_ref, gl_ref, gg_ref, *refs):
        ins = {n: refs[3 * k:3 * k + 3] for k, n in enumerate(names)}
        outs = {n: refs[nin + 4 * k:nin + 4 * k + 4] for k, n in enumerate(names)}
        loss_ref, dmod_ref = refs[nin + 4 * len(names):nin + 4 * len(names) + 2]

        def dsum(ref, lo, n):
            per = ref.shape[0] // 8
            acc = ref[lo:lo + n, :].astype(F32)
            for dev in range(1, 8):
                acc = acc + ref[dev * per + lo:dev * per + lo + n, :].astype(F32)
            return acc

        def update(n, g):
            w_ref, m_ref, v_ref = ins[n]
            g_ref, d_ref, mo_ref, vo_ref = outs[n]
            g = g.reshape(w_ref.shape)
            g_ref[...] = g
            d_ref[...], mo_ref[...], vo_ref[...] = _adam_math(w_ref[...], g, m_ref[...], v_ref[...])

        d, dm, l, lw = refs[-4:]
        d[...] = dsum(gad_ref, 0, 8)
        dm[...] = dsum(gam_ref, 0, 8)
        l[...] = dsum(gl_ref, 0, 16)
        for dev in range(8):
            for k in range(3):
                dmod_ref[dev:dev + 1, k * D_MODEL:(k + 1) * D_MODEL] = gad_ref[dev * 8 + k:dev * 8 + k + 1, :]
                dmod_ref[dev:dev + 1, (3 + k) * D_MODEL:(4 + k) * D_MODEL] = gam_ref[dev * 8 + k:dev * 8 + k + 1, :]
        w_ref, m_ref, v_ref = ins["ada_b"]
        g_ref, d_ref, mo_ref, vo_ref = outs["ada_b"]
        for k in range(3):
            g_ref[:, k * D_MODEL:(k + 1) * D_MODEL] = d[k:k + 1, :]
            g_ref[:, (3 + k) * D_MODEL:(4 + k) * D_MODEL] = dm[k:k + 1, :]
        d_ref[...], mo_ref[...], vo_ref[...] = _adam_math(w_ref[...], g_ref[...], m_ref[...], v_ref[...])
        update("norm1_g", d[3:4, :])
        update("norm2_g", dm[3:4, :])
        update("final_g", dm[4:5, :])
        update("gate_a_b", d[4:5, 0:D_LRU])
        update("gate_x_b", d[4:5, D_LRU:2 * D_LRU])
        update("lru_conv_b", l[4:5, :])
        update("a_param", l[8:9, :] * jax.nn.sigmoid(ins["a_param"][0][...]))
        update("lru_out_g", l[9:10, :])
        update("conv_out_g", l[10:11, :])
        loss_ref[...] = jnp.broadcast_to(dm[5:6, 0:128], (8, 128))
        chip = chip_ref[0]
        acc = jnp.zeros((8, 128), F32)
        for j in range(N_CHIP):
            acc = acc + jnp.where(chip == j, l[0:8, j * 128:(j + 1) * 128], 0.0)
        lw[...] = acc
        update("lru_conv_w", lw[0:4, :])
        update("short_conv_w", lw[5:8, :])
        gates = dsum(gg_ref, 0, D_LRU)
        update("gate_a_w", gates[:, 0:HEAD])
        update("gate_x_w", gates[:, HEAD:2 * HEAD])

    out_shape = []
    for n in names:
        out_shape += [jax.ShapeDtypeStruct(params[n][0].shape, F32)] * 4
    out_shape += [jax.ShapeDtypeStruct((8, 128), F32), jax.ShapeDtypeStruct((8, 6 * D_MODEL), F32)]
    res = pl.pallas_call(
        body, name="small_update", out_shape=out_shape,
        in_specs=[SMEM] + [VMEM] * (4 + nin),
        out_specs=[VMEM] * len(out_shape),
        scratch_shapes=[pltpu.VMEM((8, D_MODEL), F32), pltpu.VMEM((8, D_MODEL), F32), pltpu.VMEM((16, D_LRU), F32),
                        pltpu.VMEM((8, 128), F32)],
        compiler_params=pltpu.CompilerParams(vmem_limit_bytes=VMEM_LIMIT),
    )(mychip, gad, gam, gl, gg, *flat)
    per = {n: res[4 * k:4 * k + 4] for k, n in enumerate(names)}
    return per, res[-2], res[-1]


def _block_diag(w):
    eye = jnp.eye(8, dtype=w.dtype)
    return (eye[:, None, :, None] * w[:, :, None, :]).reshape(8 * HEAD, 8 * HEAD)


def _diag_blocks(g):
    return jnp.concatenate([g[h * HEAD:(h + 1) * HEAD, h * HEAD:(h + 1) * HEAD] for h in range(8)], axis=0)


def kernel(x, c, ada_w, ada_b, norm1_g, w_in, lru_conv_w, lru_conv_b, gate_a_w, gate_a_b, gate_x_w, gate_x_b, a_param, short_conv_w, lru_out_g, conv_out_g, w_out, norm2_g, w_mlp1, w_mlp2, final_g, loss_target, m_ada_w, m_ada_b, m_norm1_g, m_w_in, m_lru_conv_w, m_lru_conv_b, m_gate_a_w, m_gate_a_b, m_gate_x_w, m_gate_x_b, m_a_param, m_short_conv_w, m_lru_out_g, m_conv_out_g, m_w_out, m_norm2_g, m_w_mlp1, m_w_mlp2, m_final_g, v_ada_w, v_ada_b, v_norm1_g, v_w_in, v_lru_conv_w, v_lru_conv_b, v_gate_a_w, v_gate_a_b, v_gate_x_w, v_gate_x_b, v_a_param, v_short_conv_w, v_lru_out_g, v_conv_out_g, v_w_out, v_norm2_g, v_w_mlp1, v_w_mlp2, v_final_g):
    xi, yi, ci = lax.axis_index("x"), lax.axis_index("y"), lax.axis_index("c")
    mychip = 2 * xi + yi
    me = 4 * xi + 2 * yi + ci

    own_in, own_out = w_in[0].astype(BF16), w_out[0].astype(BF16)
    win_all, wout_all = _allgather_weights("allgather_mixer_weights", 1, [own_in, own_out])
    own_w1, own_w2 = w_mlp1[0].astype(BF16), w_mlp2[0].astype(BF16)
    w1_all, w2_all = _allgather_weights("allgather_mlp_weights", 2, [own_w1, own_w2])

    c_blk = jnp.zeros((8, D_MODEL), F32).at[0:1].set(c)
    cw_blk = jnp.zeros((8, 128), F32).at[0:4].set(lru_conv_w[0]).at[4:7].set(short_conv_w[0])
    c_g, cw_g = _allgather8("allgather_cond", [c_blk, cw_blk])
    c_all = c_g.reshape(8, 8, D_MODEL)[:, 0]
    cw_g = cw_g.reshape(4, 2, 8, 128)[:, 0]
    lcw = cw_g[:, 0:4].transpose(1, 0, 2).reshape(4, D_LRU)
    scw = cw_g[:, 4:7].transpose(1, 0, 2).reshape(3, D_LRU)

    mod_loc = _mod_matmul(c_all, ada_w[0])
    (mod_g,) = _allgather8("allgather_mod", [mod_loc])
    mod_all = mod_g.reshape(4, 2, 8, 6 * D_MODEL // 4)[:, 0].transpose(1, 0, 2).reshape(8, 6 * D_MODEL) + ada_b
    mod_pad = jnp.pad(mod_all.reshape(8, 6, D_MODEL), ((0, 0), (0, 2), (0, 0)))
    mod = lax.dynamic_slice_in_dim(mod_pad, me, 1, axis=0).reshape(8, D_MODEL)

    win, wout = (win_all, own_in), (wout_all, own_out)
    chip = mychip.reshape(1).astype(jnp.int32)
    core = ci.reshape(1).astype(jnp.int32)

    vecd = jnp.concatenate([norm1_g, norm2_g, final_g[None, :], jnp.concatenate([gate_a_b, gate_x_b], axis=1),
                            jnp.zeros((4, D_MODEL), F32)], axis=0)
    vecl = jnp.concatenate([lcw, lru_conv_b, scw, a_param, lru_out_g, conv_out_g, jnp.zeros((5, D_LRU), F32)], axis=0)
    gab = jnp.concatenate([_block_diag(gate_a_w[0]), _block_diag(gate_x_w[0])], axis=1).astype(BF16)
    a64 = _block_diag(jnp.full((8, HEAD, HEAD), 1.0 / HEAD, F32)).astype(BF16)

    hb, proj, hl, ycat, mixed, x1 = _mix_fwd(chip, x[0], mod, vecd, vecl, win, wout, gab, a64)
    dx1, act, dz, dmo, h2b, accm = _mlp_fwd_bwd(
        chip, x1, loss_target[0], mod, vecd, (w1_all, own_w1), (w2_all, own_w2))

    p_w2, p_w1 = _add_halves("rs_add_sibling_mlp", *_wgrad_mlp(6, h2b, dz, act, dmo))
    parts_mlp = [p_w1, p_w2]
    q_w1, q_w2 = _exchange_chips("rs_exchange_mlp", 0, parts_mlp)
    grad_x, accd, accl, g_win, g_wout, g_gate = _mix_bwd(
        chip, dx1, x[0], mixed, proj, hl, hb, ycat, mod, vecd, vecl, win, wout, gab, a64)

    g_mix = [g_win, g_wout.reshape(N_CHIP, WOUT_BLK, D_MODEL)]
    recv_mix = _sibling_swap("rs_swap_halves_mix", g_mix, True, collective_id=4)
    own_mlp = [_add_chips("rs_add_chips_mlp%d" % k, chip, p, q) for k, (p, q) in enumerate(zip(parts_mlp, (q_w1, q_w2)))]
    sib_mlp = _sibling_swap("rs_swap_reduced_mlp", own_mlp, False, collective_id=5)
    gg_blk = jnp.concatenate([_diag_blocks(g_gate[:, 0:D_LRU]), _diag_blocks(g_gate[:, D_LRU:2 * D_LRU])], axis=1)
    gad, gam, gl, gg = _allgather8_seq("allgather_small_grads", 8, [accd, accm, accl, gg_blk.astype(BF16)])

    parts_mix = [_add_sibling("rs_add_sibling_mix%d" % k, g, r, core)
                 for k, (g, r) in enumerate(zip(g_mix, recv_mix))]
    landed_mix = _exchange_chips("rs_exchange_mix", 3, parts_mix)
    res_w1, res_w2 = _adam("adam_mlp", core, [(w_mlp1[0], own_mlp[0], sib_mlp[0], m_w_mlp1[0], v_w_mlp1[0]),
                                              (w_mlp2[0], own_mlp[1], sib_mlp[1], m_w_mlp2[0], v_w_mlp2[0])])
    own_mix = [_add_chips("rs_add_chips_mix%d" % k, chip, p, q, after=[res_w1[1]])
               for k, (p, q) in enumerate(zip(parts_mix, landed_mix))]
    sib_mix = _sibling_swap("rs_swap_reduced_mix", own_mix, False)
    (res_win,) = _adam("adam_w_in", core, [(w_in[0], own_mix[0], sib_mix[0], m_w_in[0], v_w_in[0])])
    (res_wout,) = _adam("adam_w_out", core, [(w_out[0], own_mix[1], sib_mix[1], m_w_out[0], v_w_out[0])])

    params = {
        "ada_b": (ada_b, m_ada_b, v_ada_b), "norm1_g": (norm1_g, m_norm1_g, v_norm1_g),
        "lru_conv_b": (lru_conv_b, m_lru_conv_b, v_lru_conv_b),
        "gate_a_w": tuple(t.reshape(D_LRU, HEAD) for t in (gate_a_w, m_gate_a_w, v_gate_a_w)),
        "gate_a_b": (gate_a_b, m_gate_a_b, v_gate_a_b),
        "gate_x_w": tuple(t.reshape(D_LRU, HEAD) for t in (gate_x_w, m_gate_x_w, v_gate_x_w)),
        "gate_x_b": (gate_x_b, m_gate_x_b, v_gate_x_b), "a_param": (a_param, m_a_param, v_a_param),
        "lru_conv_w": (lru_conv_w, m_lru_conv_w, v_lru_conv_w),
        "short_conv_w": (short_conv_w, m_short_conv_w, v_short_conv_w),
        "lru_out_g": (lru_out_g, m_lru_out_g, v_lru_out_g), "conv_out_g": (conv_out_g, m_conv_out_g, v_conv_out_g),
        "norm2_g": (norm2_g, m_norm2_g, v_norm2_g),
        "final_g": tuple(t[None, :] for t in (final_g, m_final_g, v_final_g)),
    }
    small, loss_blk, dmod_cols = _small_update(gad, gam, gl, gg, chip, params)
    loss = loss_blk[0, 0]

    sct = (c_all * jax.nn.sigmoid(c_all)).T
    ada = _ada_grad_adam(chip, sct, dmod_cols, ada_w[0], m_ada_w[0], v_ada_w[0])

    res = {"ada_w": ada, "w_in": res_win, "w_out": res_wout, "w_mlp1": res_w1, "w_mlp2": res_w2}
    res = {n: tuple(t[None] for t in r) for n, r in res.items()}
    shapes = {"gate_a_w": gate_a_w.shape, "gate_x_w": gate_x_w.shape, "lru_conv_w": lru_conv_w.shape,
              "short_conv_w": short_conv_w.shape, "final_g": final_g.shape}
    for n, t in small.items():
        res[n] = tuple(u.reshape(shapes[n]) if n in shapes else u for u in t)

    order = ["ada_w", "ada_b", "norm1_g", "w_in", "lru_conv_w", "lru_conv_b", "gate_a_w", "gate_a_b", "gate_x_w",
             "gate_x_b", "a_param", "short_conv_w", "lru_out_g", "conv_out_g", "w_out", "norm2_g", "w_mlp1",
             "w_mlp2", "final_g"]
    return (loss, grad_x[None], *[res[n][0] for n in order], *[res[n][1] for n in order],
            *[res[n][2] for n in order], *[res[n][3] for n in order])
```

```python
import jax
import jax.numpy as jnp
from jax import lax
from jax.experimental import pallas as pl
from jax.experimental.pallas import tpu as pltpu
from jax.experimental.pallas import tpu_sc as plsc

F32 = jnp.float32
BF16 = jnp.bfloat16

D_MODEL = 1024
D_LRU = 512
D_IN = 2560
D_FF = 4096
N_CHIP = 4
WIN_BLK = D_IN // N_CHIP
WOUT_BLK = D_MODEL // N_CHIP
FF_BLK = D_FF // N_CHIP
HEAD = 64
EPS = 1e-6
C_GATE = 8.0
TOKEN_TILE = 256
MIX_FWD_TILE = 512
HALO = 8
VMEM_LIMIT = 60 * 1024 * 1024

ADAM_LR = 0.001
ADAM_B1 = 0.9
ADAM_B2 = 0.999
ADAM_EPS = 1e-08
ADAM_WD = 0.01
ADAM_STEP = 10

MESH = pl.DeviceIdType.MESH
ANY = pl.BlockSpec(memory_space=pl.ANY)
VMEM = pl.BlockSpec(memory_space=pltpu.VMEM)
SMEM = pl.BlockSpec(memory_space=pltpu.SMEM)


def _full(shape, single=False):
    nd = len(shape)
    if single:
        return pl.BlockSpec(shape, lambda *_: (0,) * nd, pipeline_mode=pl.Buffered(1))
    return pl.BlockSpec(shape, lambda *_: (0,) * nd)


def _dot(a, b):
    return jnp.dot(a, b, preferred_element_type=F32)


def _dot_nt(a, b):
    return lax.dot_general(a, b, (((1,), (1,)), ((), ())), preferred_element_type=F32)


def _dot_tn(a, b):
    return lax.dot_general(a, b, (((0,), (0,)), ((), ())), preferred_element_type=F32)


def _gmean(v, a64):
    hi = v.astype(BF16)
    lo = (v - hi.astype(F32)).astype(BF16)
    return _dot(hi, a64) + _dot(lo, a64)


def _gelu(x):
    u = 0.7978845608028654 * (x + 0.044715 * x * x * x)
    t = jnp.tanh(u)
    return 0.5 * x * (1.0 + t), t


def _gelu_grad(x, t):
    du = 0.7978845608028654 * (1.0 + 3.0 * 0.044715 * x * x)
    return 0.5 * (1.0 + t) + 0.5 * x * (1.0 - t * t) * du


def _log1p_pos(y):
    return jnp.where(y < 1e-2, y * (1.0 - y * (0.5 - y * (1.0 / 3.0 - y * 0.25))), jnp.log(1.0 + y))


def _softplus(a):
    return jnp.maximum(a, 0.0) + _log1p_pos(jnp.exp(-jnp.abs(a)))


def _neg_expm1(z):
    series = -z * (1.0 + z * (0.5 + z * (1.0 / 6.0 + z * (1.0 / 24.0))))
    return jnp.where(z > -0.02, series, 1.0 - jnp.exp(z))


def _scan_fwd(a, b, row):
    n = a.shape[0]
    d = 1
    while d < n:
        m = row >= d
        b = jnp.where(m, a * pltpu.roll(b, d, 0) + b, b)
        a = jnp.where(m, a * pltpu.roll(a, d, 0), a)
        d *= 2
    return a, b


def _scan_rev(a, b, row):
    n = a.shape[0]
    d = 1
    while d < n:
        m = row < n - d
        b = jnp.where(m, b + a * pltpu.roll(b, n - d, 0), b)
        a = jnp.where(m, a * pltpu.roll(a, n - d, 0), a)
        d *= 2
    return a, b


def _colsum(v):
    return jnp.sum(v, axis=0, keepdims=True)


def _load_gathered(chip, gathered, own, slot, sems):
    copies = []
    for j in range(N_CHIP):
        @pl.when(chip == j)
        def _(j=j):
            pltpu.make_async_copy(own, slot(j), sems.at[j]).start()

        @pl.when(chip != j)
        def _(j=j):
            pltpu.make_async_copy(gathered.at[j], slot(j), sems.at[j]).start()

        copies.append(pltpu.make_async_copy(own, slot(j), sems.at[j]))
    return copies


def _lru_gates(xlb, gab, gbias, sp, first_row):
    g = _dot(xlb, gab) + gbias
    r = jax.nn.sigmoid(g[:, :D_LRU])
    ig = jax.nn.sigmoid(g[:, D_LRU:])
    la = (-C_GATE) * r * sp
    a = jnp.exp(la)
    msq = jnp.sqrt(_neg_expm1(2.0 * la))
    mult = jnp.where(first_row, 1.0, msq)
    return r, ig, a, msq, mult


def _mix_fwd(chip, x, mod, vecd, vecl, win, wout, gab, a64):
    s = x.shape[0]
    ts = MIX_FWD_TILE
    nt = s // ts

    def body(chip_ref, x_ref, mod_ref, vd_ref, vl_ref, win_hbm, win_own, wout_hbm, wout_own, gab_ref, a64_ref,
             hb_ref, proj_ref, hl_ref, ycat_ref, mixed_ref, x1_ref,
             win_ref, wout_ref, ext_lx, ext_cv, hcar, sems):
        i = pl.program_id(0)

        @pl.when(i == 0)
        def _():
            cps = _load_gathered(chip_ref[0], win_hbm, win_own, lambda j: win_ref.at[j], sems.at[pl.ds(0, N_CHIP)])
            cps += _load_gathered(chip_ref[0], wout_hbm, wout_own,
                                  lambda j: wout_ref.at[pl.ds(j * WOUT_BLK, WOUT_BLK), :],
                                  sems.at[pl.ds(N_CHIP, N_CHIP)])
            ext_lx[0:HALO, :] = jnp.zeros((HALO, D_LRU), F32)
            ext_cv[0:HALO, :] = jnp.zeros((HALO, D_LRU), F32)
            hcar[...] = jnp.zeros_like(hcar)
            for cp in cps:
                cp.wait()

        row = lax.broadcasted_iota(jnp.int32, (ts, D_LRU), 0)
        first_row = jnp.logical_and(row == 0, i == 0)
        xt = x_ref[...]
        shift1, scale1, gate1 = mod_ref[0:1, :], mod_ref[1:2, :], mod_ref[2:3, :]
        r1 = lax.rsqrt(jnp.mean(xt * xt, axis=-1, keepdims=True) + EPS)
        h = (xt * r1) * vd_ref[0:1, :] * (1.0 + scale1) + shift1
        hb = h.astype(BF16)
        hb_ref[...] = hb
        for j in range(N_CHIP):
            proj_ref[:, j * WIN_BLK:(j + 1) * WIN_BLK] = _dot(hb, win_ref[j])
        u_ly = proj_ref[:, 512:1024]
        u_b = proj_ref[:, 1024:1536]

        ext_lx[HALO:HALO + ts, :] = proj_ref[:, 0:512]
        xl = vl_ref[4:5, :] + vl_ref[0:1, :] * ext_lx[pl.ds(5, ts), :]
        for k in range(1, 4):
            xl = xl + vl_ref[k:k + 1, :] * ext_lx[pl.ds(5 + k, ts), :]
        ext_lx[0:HALO, :] = ext_lx[ts:ts + HALO, :]
        sp = _softplus(vl_ref[8:9, :])
        _, ig, a, _, mult = _lru_gates(xl.astype(BF16), gab_ref[...], vd_ref[3:4, :], sp, first_row)
        acum, hloc = _scan_fwd(a, mult * (ig * xl), row)
        hl = hloc + acum * hcar[0:1, :]
        hl_ref[...] = hl
        hcar[0:1, :] = hl_ref[ts - 1:ts, :]
        ge, _ = _gelu(u_ly)
        p = ge * hl
        y_lru = p * lax.rsqrt(_gmean(p * p, a64_ref[...]) + EPS) * vl_ref[9:10, :]
        ycat_ref[:, 0:512] = y_lru.astype(BF16)

        ext_cv[HALO:HALO + ts, :] = proj_ref[:, 1536:2048] * proj_ref[:, 2048:2560]
        q = vl_ref[5:6, :] * ext_cv[pl.ds(6, ts), :]
        for k in range(1, 3):
            q = q + vl_ref[5 + k:6 + k, :] * ext_cv[pl.ds(6 + k, ts), :]
        ext_cv[0:HALO, :] = ext_cv[ts:ts + HALO, :]
        yc = u_b * q
        y_conv = yc * lax.rsqrt(_gmean(yc * yc, a64_ref[...]) + EPS) * vl_ref[10:11, :]
        ycat_ref[:, 512:1024] = y_conv.astype(BF16)

        mixed = _dot(ycat_ref[...], wout_ref[...])
        mixed_ref[...] = mixed
        x1_ref[...] = xt + gate1 * mixed

    tile = lambda w: pl.BlockSpec((ts, w), lambda i: (i, 0))
    return pl.pallas_call(
        body, name="mix_fwd", grid=(nt,),
        in_specs=[SMEM, tile(D_MODEL), _full((8, D_MODEL)), _full((8, D_MODEL)), _full((16, D_LRU)),
                  ANY, ANY, ANY, ANY, _full((D_LRU, 2 * D_LRU), True), _full((D_LRU, D_LRU), True)],
        out_specs=[tile(D_MODEL), tile(D_IN), tile(D_LRU), tile(D_MODEL), tile(D_MODEL), tile(D_MODEL)],
        out_shape=[jax.ShapeDtypeStruct((s, D_MODEL), BF16), jax.ShapeDtypeStruct((s, D_IN), F32),
                   jax.ShapeDtypeStruct((s, D_LRU), F32), jax.ShapeDtypeStruct((s, D_MODEL), BF16),
                   jax.ShapeDtypeStruct((s, D_MODEL), F32), jax.ShapeDtypeStruct((s, D_MODEL), F32)],
        scratch_shapes=[pltpu.VMEM((N_CHIP, D_MODEL, WIN_BLK), BF16), pltpu.VMEM((D_MODEL, D_MODEL), BF16),
                        pltpu.VMEM((ts + HALO, D_LRU), F32), pltpu.VMEM((ts + HALO, D_LRU), F32),
                        pltpu.VMEM((HALO, D_LRU), F32), pltpu.SemaphoreType.DMA((2 * N_CHIP,))],
        compiler_params=pltpu.CompilerParams(dimension_semantics=("arbitrary",), vmem_limit_bytes=VMEM_LIMIT),
    )(chip, x, mod, vecd, vecl, *win, *wout, gab, a64)


def _mlp_fwd_bwd(chip, x1, target, mod, vecd, w1, w2):
    s = x1.shape[0]
    ts = TOKEN_TILE
    nt = s // ts

    def body(chip_ref, x1_ref, tg_ref, mod_ref, vd_ref, w1_hbm, w1_own, w2_hbm, w2_own,
             dx1_ref, act_ref, dz_ref, dmo_ref, h2_ref, acc_ref, w1_v, w2_v, rz_v, sems):
        i = pl.program_id(0)

        @pl.when(i == 0)
        def _():
            cps = _load_gathered(chip_ref[0], w1_hbm, w1_own, lambda j: w1_v.at[j], sems.at[pl.ds(0, N_CHIP)])
            cps += _load_gathered(chip_ref[0], w2_hbm, w2_own, lambda j: w2_v.at[j], sems.at[pl.ds(N_CHIP, N_CHIP)])
            acc_ref[...] = jnp.zeros_like(acc_ref)
            for cp in cps:
                cp.wait()

        xt = x1_ref[...]
        shift2, scale2, gate2 = mod_ref[3:4, :], mod_ref[4:5, :], mod_ref[5:6, :]
        g2, gf = vd_ref[1:2, :], vd_ref[2:3, :]
        r2 = lax.rsqrt(jnp.mean(xt * xt, axis=-1, keepdims=True) + EPS)
        n2 = xt * r2
        h2b = (n2 * g2 * (1.0 + scale2) + shift2).astype(BF16)
        h2_ref[...] = h2b
        for j in range(N_CHIP):
            rz_v[j] = jnp.maximum(_dot(h2b, w1_v[j]), 0.0)
        mo = jnp.zeros((ts, D_MODEL), F32)
        for j in range(N_CHIP):
            rz = rz_v[j]
            actb = (rz * rz).astype(BF16)
            act_ref[:, j * FF_BLK:(j + 1) * FF_BLK] = actb
            mo = mo + _dot(actb, w2_v[j])
        x2 = xt + gate2 * mo
        r3 = lax.rsqrt(jnp.mean(x2 * x2, axis=-1, keepdims=True) + EPS)
        n3 = x2 * r3
        e = n3 * gf - tg_ref[...]
        loss = (0.5 / D_MODEL) * jnp.sum(_colsum(e * e), axis=1, keepdims=True)
        dy = e * (1.0 / D_MODEL)
        acc_ref[4:5, :] += _colsum(dy * n3)
        acc_ref[5:6, :] += jnp.broadcast_to(loss, (1, D_MODEL))
        dn3 = dy * gf
        dx2 = r3 * (dn3 - n3 * jnp.mean(dn3 * n3, axis=-1, keepdims=True))
        acc_ref[2:3, :] += _colsum(dx2 * mo)
        dmob = (dx2 * gate2).astype(BF16)
        dmo_ref[...] = dmob
        for j in range(N_CHIP):
            dz_ref[:, j * FF_BLK:(j + 1) * FF_BLK] = (_dot_nt(dmob, w2_v[j]) * (2.0 * rz_v[j])).astype(BF16)
        dh2 = jnp.zeros((ts, D_MODEL), F32)
        for j in range(N_CHIP):
            dh2 = dh2 + _dot_nt(dz_ref[:, j * FF_BLK:(j + 1) * FF_BLK], w1_v[j])
        acc_ref[1:2, :] += _colsum(dh2 * (n2 * g2))
        acc_ref[0:1, :] += _colsum(dh2)
        dhn2 = dh2 * (1.0 + scale2)
        acc_ref[3:4, :] += _colsum(dhn2 * n2)
        dn2 = dhn2 * g2
        dx1_ref[...] = dx2 + r2 * (dn2 - n2 * jnp.mean(dn2 * n2, axis=-1, keepdims=True))

    tile = lambda w: pl.BlockSpec((ts, w), lambda i: (i, 0))
    return pl.pallas_call(
        body, name="mlp_fwd_bwd", grid=(nt,),
        in_specs=[SMEM, tile(D_MODEL), tile(D_MODEL), _full((8, D_MODEL)), _full((8, D_MODEL)), ANY, ANY, ANY, ANY],
        out_specs=[tile(D_MODEL), tile(D_FF), tile(D_FF), tile(D_MODEL), tile(D_MODEL), _full((8, D_MODEL))],
        out_shape=[jax.ShapeDtypeStruct((s, D_MODEL), F32), jax.ShapeDtypeStruct((s, D_FF), BF16),
                   jax.ShapeDtypeStruct((s, D_FF), BF16), jax.ShapeDtypeStruct((s, D_MODEL), BF16),
                   jax.ShapeDtypeStruct((s, D_MODEL), BF16), jax.ShapeDtypeStruct((8, D_MODEL), F32)],
        scratch_shapes=[pltpu.VMEM((N_CHIP, D_MODEL, FF_BLK), BF16), pltpu.VMEM((N_CHIP, FF_BLK, D_MODEL), BF16),
                        pltpu.VMEM((N_CHIP, ts, FF_BLK), F32), pltpu.SemaphoreType.DMA((2 * N_CHIP,))],
        compiler_params=pltpu.CompilerParams(dimension_semantics=("arbitrary",), vmem_limit_bytes=VMEM_LIMIT),
    )(chip, x1, target, mod, vecd, *w1, *w2)


def _mix_bwd(chip, dx1, x, mixed, proj, hl, hb, ycat, mod, vecd, vecl, win, wout, gab, a64):
    s = x.shape[0]
    ts = TOKEN_TILE
    nt = s // ts
    hpt = ts // HALO

    def body(chip_ref, dx1_ref, x_ref, mixed_ref, proj_ref, projh_ref, hl_ref, hlh_ref, hb_ref, ycat_ref,
             mod_ref, vd_ref, vl_ref, win_hbm, win_own, wout_hbm, wout_own, gab_ref, a64_ref,
             gx_ref, accd_ref, accl_ref, gwin_hbm, gwout_hbm, ggate_hbm,
             win_ref, wout_ref, dproj_ref, dgb_ref, gwin_acc, gwout_acc, ggate_acc,
             ext_lx, ext_cv, ext_hl, ext_dxl, ext_dq, gbuf, gcar, acar, sems):
        i = pl.program_id(0)
        ri = nt - 1 - i

        @pl.when(i == 0)
        def _():
            gwin_acc[...] = jnp.zeros_like(gwin_acc)
            gwout_acc[...] = jnp.zeros_like(gwout_acc)
            ggate_acc[...] = jnp.zeros_like(ggate_acc)
            cps = _load_gathered(chip_ref[0], win_hbm, win_own, lambda j: win_ref.at[j], sems.at[pl.ds(0, N_CHIP)])
            cps += _load_gathered(chip_ref[0], wout_hbm, wout_own,
                                  lambda j: wout_ref.at[pl.ds(j * WOUT_BLK, WOUT_BLK), :],
                                  sems.at[pl.ds(N_CHIP, N_CHIP)])
            for cp in cps:
                cp.wait()
            accd_ref[...] = jnp.zeros_like(accd_ref)
            accl_ref[...] = jnp.zeros_like(accl_ref)
            ext_dxl[ts:ts + HALO, :] = jnp.zeros((HALO, D_LRU), F32)
            ext_dq[ts:ts + HALO, :] = jnp.zeros((HALO, D_LRU), F32)
            gcar[...] = jnp.zeros_like(gcar)
            acar[...] = jnp.zeros_like(acar)

        row = lax.broadcasted_iota(jnp.int32, (ts, D_LRU), 0)
        first_row = jnp.logical_and(row == 0, ri == 0)
        halo_on = jnp.where(ri == 0, 0.0, 1.0)
        shift1, scale1, gate1 = mod_ref[0:1, :], mod_ref[1:2, :], mod_ref[2:3, :]
        g1 = vd_ref[0:1, :]
        a64m = a64_ref[...]
        lg, cg = vl_ref[9:10, :], vl_ref[10:11, :]

        dx1 = dx1_ref[...]
        accd_ref[2:3, :] += _colsum(dx1 * mixed_ref[...])
        dmb = (dx1 * gate1).astype(BF16)
        gwout_acc[...] += _dot_tn(ycat_ref[...], dmb)
        dycat = _dot_nt(dmb, wout_ref[...])
        dyl = dycat[:, 0:512]
        dyv = dycat[:, 512:1024]

        u_ly = proj_ref[:, 512:1024]
        u_b = proj_ref[:, 1024:1536]
        u_c = proj_ref[:, 1536:2048]
        u_v = proj_ref[:, 2048:2560]
        ext_lx[0:HALO, :] = projh_ref[:, 0:512] * halo_on
        ext_lx[HALO:HALO + ts, :] = proj_ref[:, 0:512]
        xl = vl_ref[4:5, :] + vl_ref[0:1, :] * ext_lx[pl.ds(5, ts), :]
        for k in range(1, 4):
            xl = xl + vl_ref[k:k + 1, :] * ext_lx[pl.ds(5 + k, ts), :]
        xlb = xl.astype(BF16)
        sp = _softplus(vl_ref[8:9, :])
        r, ig, a, msq, mult = _lru_gates(xlb, gab_ref[...], vd_ref[3:4, :], sp, first_row)
        hl = hl_ref[...]
        ge, th = _gelu(u_ly)
        p = ge * hl
        rl = lax.rsqrt(_gmean(p * p, a64m) + EPS)
        nl = p * rl
        ext_cv[0:HALO, :] = projh_ref[:, 1536:2048] * projh_ref[:, 2048:2560] * halo_on
        ext_cv[HALO:HALO + ts, :] = u_c * u_v
        q = vl_ref[5:6, :] * ext_cv[pl.ds(6, ts), :]
        for k in range(1, 3):
            q = q + vl_ref[5 + k:6 + k, :] * ext_cv[pl.ds(6 + k, ts), :]
        yc = u_b * q
        rc = lax.rsqrt(_gmean(yc * yc, a64m) + EPS)
        nc = yc * rc

        accl_ref[9:10, :] += _colsum(dyl * nl)
        dnl = dyl * lg
        dp = rl * (dnl - nl * _gmean(dnl * nl, a64m))
        dproj_ref[:, 512:1024] = ((dp * hl) * _gelu_grad(u_ly, th)).astype(BF16)
        a_next = jnp.where(row == ts - 1, acar[0:1, :], pltpu.roll(a, ts - 1, 0))
        acum, gloc = _scan_rev(a_next, dp * ge, row)
        gbuf[...] = gloc + acum * gcar[0:1, :]
        gcar[0:1, :] = gbuf[0:1, :]
        ext_hl[0:HALO, :] = hlh_ref[...] * halo_on
        ext_hl[HALO:HALO + ts, :] = hl
        acar[...] = a[0:HALO, :]
        gt = gbuf[...]
        da = gt * ext_hl[pl.ds(HALO - 1, ts), :]
        dmult = gt * ig * xl
        di = gt * mult * xl
        dxl = gt * mult * ig
        dla = da * a - jnp.where(first_row, 0.0, dmult * a * a / msq)
        accl_ref[8:9, :] += _colsum(dla * ((-C_GATE) * r))
        dra = dla * ((-C_GATE) * sp) * r * (1.0 - r)
        dia = di * ig * (1.0 - ig)
        accd_ref[4:5, 0:D_LRU] += _colsum(dra)
        accd_ref[4:5, D_LRU:2 * D_LRU] += _colsum(dia)
        dgb_ref[:, 0:D_LRU] = dra.astype(BF16)
        dgb_ref[:, D_LRU:2 * D_LRU] = dia.astype(BF16)
        dxl = dxl + _dot_nt(dgb_ref[...], gab_ref[...])
        ggate_acc[...] += _dot_tn(xlb, dgb_ref[...])
        accl_ref[4:5, :] += _colsum(dxl)
        for k in range(4):
            accl_ref[k:k + 1, :] += _colsum(dxl * ext_lx[pl.ds(5 + k, ts), :])
        ext_dxl[0:ts, :] = dxl
        du_lx = vl_ref[0:1, :] * ext_dxl[pl.ds(3, ts), :]
        for k in range(1, 4):
            du_lx = du_lx + vl_ref[k:k + 1, :] * ext_dxl[pl.ds(3 - k, ts), :]
        ext_dxl[ts:ts + HALO, :] = ext_dxl[0:HALO, :]
        dproj_ref[:, 0:512] = du_lx.astype(BF16)

        accl_ref[10:11, :] += _colsum(dyv * nc)
        dnc = dyv * cg
        dyc = rc * (dnc - nc * _gmean(dnc * nc, a64m))
        dproj_ref[:, 1024:1536] = (dyc * q).astype(BF16)
        dq = dyc * u_b
        for k in range(3):
            accl_ref[5 + k:6 + k, :] += _colsum(dq * ext_cv[pl.ds(6 + k, ts), :])
        ext_dq[0:ts, :] = dq
        dcv = vl_ref[5:6, :] * ext_dq[pl.ds(2, ts), :]
        for k in range(1, 3):
            dcv = dcv + vl_ref[5 + k:6 + k, :] * ext_dq[pl.ds(2 - k, ts), :]
        ext_dq[ts:ts + HALO, :] = ext_dq[0:HALO, :]
        dproj_ref[:, 1536:2048] = (dcv * u_v).astype(BF16)
        dproj_ref[:, 2048:2560] = (dcv * u_c).astype(BF16)

        dh = _dot_nt(dproj_ref[:, 0:WIN_BLK], win_ref[0])
        for j in range(1, N_CHIP):
            dh = dh + _dot_nt(dproj_ref[:, j * WIN_BLK:(j + 1) * WIN_BLK], win_ref[j])
        for j in range(N_CHIP):
            gwin_acc[j] += _dot_tn(hb_ref[...], dproj_ref[:, j * WIN_BLK:(j + 1) * WIN_BLK])
        xt = x_ref[...]
        r1 = lax.rsqrt(jnp.mean(xt * xt, axis=-1, keepdims=True) + EPS)
        n1 = xt * r1
        accd_ref[1:2, :] += _colsum(dh * (n1 * g1))
        accd_ref[0:1, :] += _colsum(dh)
        dhn1 = dh * (1.0 + scale1)
        accd_ref[3:4, :] += _colsum(dhn1 * n1)
        dn1 = dhn1 * g1
        gx_ref[...] = dx1 + r1 * (dn1 - n1 * jnp.mean(dn1 * n1, axis=-1, keepdims=True))

        @pl.when(i == nt - 1)
        def _():
            outs = [pltpu.make_async_copy(acc, dst, sems.at[k]) for k, (acc, dst) in enumerate(
                ((gwin_acc, gwin_hbm), (gwout_acc, gwout_hbm), (ggate_acc, ggate_hbm)))]
            for cp in outs:
                cp.start()
            for cp in outs:
                cp.wait()

    tile = lambda w: pl.BlockSpec((ts, w), lambda i: (nt - 1 - i, 0))
    halo = lambda w: pl.BlockSpec((HALO, w), lambda i: (jnp.maximum((nt - 1 - i) * hpt - 1, 0), 0))
    ext = pltpu.VMEM((ts + HALO, D_LRU), F32)
    return pl.pallas_call(
        body, name="mix_bwd", grid=(nt,),
        in_specs=[SMEM, tile(D_MODEL), tile(D_MODEL), tile(D_MODEL), tile(D_IN), halo(D_IN), tile(D_LRU), halo(D_LRU),
                  tile(D_MODEL), tile(D_MODEL), _full((8, D_MODEL)), _full((8, D_MODEL)), _full((16, D_LRU)),
                  ANY, ANY, ANY, ANY, _full((D_LRU, 2 * D_LRU), True), _full((D_LRU, D_LRU), True)],
        out_specs=[tile(D_MODEL), _full((8, D_MODEL)), _full((16, D_LRU)), ANY, ANY, ANY],
        out_shape=[jax.ShapeDtypeStruct((s, D_MODEL), F32),
                   jax.ShapeDtypeStruct((8, D_MODEL), F32), jax.ShapeDtypeStruct((16, D_LRU), F32),
                   jax.ShapeDtypeStruct((N_CHIP, D_MODEL, WIN_BLK), F32), jax.ShapeDtypeStruct((D_MODEL, D_MODEL), F32),
                   jax.ShapeDtypeStruct((D_LRU, 2 * D_LRU), F32)],
        scratch_shapes=[pltpu.VMEM((N_CHIP, D_MODEL, WIN_BLK), BF16), pltpu.VMEM((D_MODEL, D_MODEL), BF16),
                        pltpu.VMEM((ts, D_IN), BF16), pltpu.VMEM((ts, 2 * D_LRU), BF16),
                        pltpu.VMEM((N_CHIP, D_MODEL, WIN_BLK), F32), pltpu.VMEM((D_MODEL, D_MODEL), F32),
                        pltpu.VMEM((D_LRU, 2 * D_LRU), F32),
                        ext, ext, ext, ext, ext, pltpu.VMEM((ts, D_LRU), F32),
                        pltpu.VMEM((HALO, D_LRU), F32), pltpu.VMEM((HALO, D_LRU), F32),
                        pltpu.SemaphoreType.DMA((2 * N_CHIP,))],
        compiler_params=pltpu.CompilerParams(dimension_semantics=("arbitrary",), vmem_limit_bytes=VMEM_LIMIT),
    )(chip, dx1, x, mixed, proj, proj, hl, hl, hb, ycat, mod, vecd, vecl, *win, *wout, gab, a64)


def _wgrad_mlp(collective_id, h2b, dz, act, dmo):
    s = h2b.shape[0]
    nstep = 2 * N_CHIP
    half = FF_BLK // 2

    def body(h2_ref, dz_ref, act_ref, dmo_ref, p1_hbm, p2_hbm, buf, landed, summed, send_sems, recv_sems, out_sems):
        j = pl.program_id(0)
        x, y, c, _ = _position()

        def give(jj):
            return pltpu.make_async_remote_copy(
                src_ref=buf.at[jj % 2, pl.ds((1 - c) * half, half), :], dst_ref=landed.at[jj],
                send_sem=send_sems.at[jj % 2], recv_sem=recv_sems.at[jj],
                device_id=(x, y, 1 - c), device_id_type=MESH)

        def write_out(jj, dst):
            return pltpu.make_async_copy(summed.at[jj % 2], dst, out_sems.at[jj % 2])

        def add_sibling(jj):
            give(jj).wait_recv()
            own = buf[jj % 2, pl.ds(pl.multiple_of(c * half, half), half), :]
            summed[jj % 2] = (own.astype(F32) + landed[jj].astype(F32)).astype(BF16)

        @pl.when(j == 0)
        def _():
            pl.semaphore_signal(pltpu.get_barrier_semaphore(), inc=1, device_id=(x, y, 1 - c), device_id_type=MESH)

        @pl.when(j >= 2)
        def _():
            give(j - 2).wait_send()

        @pl.when(j < N_CHIP)
        def _():
            buf[j % 2] = _dot_tn(act_ref[...], dmo_ref[...]).astype(BF16)

        @pl.when(j >= N_CHIP)
        def _():
            buf[j % 2] = _dot_tn(h2_ref[...], dz_ref[...]).astype(BF16)

        @pl.when(j == 0)
        def _():
            pl.semaphore_wait(pltpu.get_barrier_semaphore(), 1)

        give(j).start()

        @pl.when(j >= 1)
        def _():
            jm = j - 1

            @pl.when(jm >= 2)
            def _():
                write_out(jm - 2, p2_hbm.at[0]).wait()

            add_sibling(jm)

            @pl.when(jm < N_CHIP)
            def _():
                write_out(jm, p2_hbm.at[jm]).start()

            @pl.when(jm >= N_CHIP)
            def _():
                write_out(jm, p1_hbm.at[jm - N_CHIP]).start()

        @pl.when(j == nstep - 1)
        def _():
            last = nstep - 1
            write_out(last - 2, p1_hbm.at[0]).wait()
            add_sibling(last)
            write_out(last, p1_hbm.at[N_CHIP - 1]).start()
            for jj in (last - 1, last):
                give(jj).wait_send()
                write_out(jj, p1_hbm.at[0]).wait()

    sds = jax.ShapeDtypeStruct((N_CHIP, half, D_MODEL), BF16)
    whole = pl.BlockSpec((s, D_MODEL), lambda j: (0, 0))
    return pl.pallas_call(
        body, name="wgrad_mlp", grid=(nstep,),
        in_specs=[whole, pl.BlockSpec((s, FF_BLK), lambda j: (0, jnp.maximum(j - N_CHIP, 0))),
                  pl.BlockSpec((s, FF_BLK), lambda j: (0, jnp.minimum(j, N_CHIP - 1))), whole],
        out_specs=[ANY, ANY], out_shape=[sds, sds],
        scratch_shapes=[pltpu.VMEM((2, FF_BLK, D_MODEL), BF16), pltpu.VMEM((nstep, half, D_MODEL), BF16),
                        pltpu.VMEM((2, half, D_MODEL), BF16), pltpu.SemaphoreType.DMA((2,)),
                        pltpu.SemaphoreType.DMA((nstep,)), pltpu.SemaphoreType.DMA((2,))],
        compiler_params=pltpu.CompilerParams(dimension_semantics=("arbitrary",), vmem_limit_bytes=VMEM_LIMIT,
                                             collective_id=collective_id),
    )(h2b, dz, act, dmo)


def _mod_matmul(c_all, ada_w_loc):
    n = ada_w_loc.shape[1]
    cb = 512

    def body(c_ref, w_ref, o_ref):
        c = c_ref[...]
        sc = c * jax.nn.sigmoid(c)
        o_ref[...] = _dot(sc.astype(BF16), w_ref[...].astype(BF16))

    return pl.pallas_call(
        body, name="mod_matmul", grid=(n // cb,),
        in_specs=[_full((8, D_MODEL)), pl.BlockSpec((D_MODEL, cb), lambda j: (0, j))],
        out_specs=pl.BlockSpec((8, cb), lambda j: (0, j)),
        out_shape=jax.ShapeDtypeStruct((8, n), F32),
        compiler_params=pltpu.CompilerParams(dimension_semantics=("arbitrary",), vmem_limit_bytes=VMEM_LIMIT),
    )(c_all, ada_w_loc)


def _adam_math(w, g, m, v):
    m = ADAM_B1 * m + (1.0 - ADAM_B1) * g
    v = ADAM_B2 * v + (1.0 - ADAM_B2) * (g * g)
    m_hat = m / (1.0 - ADAM_B1 ** ADAM_STEP)
    v_hat = v / (1.0 - ADAM_B2 ** ADAM_STEP)
    delta = (-ADAM_LR) * (m_hat / (jnp.sqrt(v_hat) + ADAM_EPS) + ADAM_WD * w)
    return delta, m, v


def _adam(name, core, shards):
    n = len(shards)
    r, c = shards[0][0].shape
    half = r // 2
    rb = min(half, 128)
    nh = half // rb

    def body(core_ref, *refs):
        ins, outs = refs[:5 * n], refs[5 * n:]
        mine = (pl.program_id(0) // nh) == core_ref[0]
        for k in range(n):
            w_ref, go_ref, gs_ref, m_ref, v_ref = ins[5 * k:5 * k + 5]
            g_ref, d_ref, mo_ref, vo_ref = outs[4 * k:4 * k + 4]
            g = jnp.where(mine, go_ref[...], gs_ref[...])
            g_ref[...] = g
            d_ref[...], mo_ref[...], vo_ref[...] = _adam_math(w_ref[...], g, m_ref[...], v_ref[...])

    spec = pl.BlockSpec((rb, c), lambda i, core_ref: (i, 0))
    own = pl.BlockSpec((rb, c), lambda i, core_ref: (jnp.where(i // nh == core_ref[0], i % nh, 0), 0))
    sib = pl.BlockSpec((rb, c), lambda i, core_ref: (jnp.where(i // nh == core_ref[0], 0, i % nh), 0))
    sds = jax.ShapeDtypeStruct((r, c), F32)
    res = pl.pallas_call(
        body, name=name,
        grid_spec=pltpu.PrefetchScalarGridSpec(
            num_scalar_prefetch=1, grid=(r // rb,),
            in_specs=[spec, own, sib, spec, spec] * n, out_specs=[spec] * (4 * n)),
        out_shape=[sds] * (4 * n),
        compiler_params=pltpu.CompilerParams(dimension_semantics=("arbitrary",), vmem_limit_bytes=VMEM_LIMIT),
    )(core, *[t for s in shards for t in s])
    return [res[4 * k:4 * k + 4] for k in range(n)]


def _ada_grad_adam(chip, sct, dmod_cols, w, m, v):
    r, c = w.shape
    rb = 256

    def body(chip_ref, s_ref, dm_ref, w_ref, m_ref, v_ref, g_ref, d_ref, mo_ref, vo_ref):
        g = s_ref[:, 0:1] * dm_ref[0:1, :]
        for b in range(1, 8):
            g = g + s_ref[:, b:b + 1] * dm_ref[b:b + 1, :]
        g_ref[...] = g
        d_ref[...], mo_ref[...], vo_ref[...] = _adam_math(w_ref[...], g, m_ref[...], v_ref[...])

    spec = pl.BlockSpec((rb, c), lambda i, chip_ref: (i, 0))
    sds = jax.ShapeDtypeStruct((r, c), F32)
    return pl.pallas_call(
        body, name="ada_grad_adam",
        grid_spec=pltpu.PrefetchScalarGridSpec(
            num_scalar_prefetch=1, grid=(r // rb,),
            in_specs=[pl.BlockSpec((rb, 8), lambda i, chip_ref: (i, 0)),
                      pl.BlockSpec((8, c), lambda i, chip_ref: (0, chip_ref[0])), spec, spec, spec],
            out_specs=[spec] * 4),
        out_shape=[sds] * 4,
        compiler_params=pltpu.CompilerParams(dimension_semantics=("arbitrary",), vmem_limit_bytes=VMEM_LIMIT),
    )(chip, sct, dmod_cols, w, m, v)


def _position():
    x, y, c = lax.axis_index("x"), lax.axis_index("y"), lax.axis_index("c")
    chips = [(1 - x, y), (x, 1 - y), (1 - x, 1 - y)]
    return x, y, c, chips


def _ag8_run(ins, outs, send_sems, recv_sems, local_sems):
    na = len(ins)
    x, y, c, chips = _position()
    me, sibling = (x, y, c), (x, y, 1 - c)
    first, passed, local = [], [], []
    for a in range(na):
        m_per = ins[a].shape[0]

        def rows(px, py, pc, a=a, m_per=m_per):
            return outs[a].at[pl.ds((4 * px + 2 * py + pc) * m_per, m_per), :]

        def copy(k, block, to, src=None, a=a, rows=rows):
            return pltpu.make_async_remote_copy(
                src_ref=rows(*block) if src is None else src, dst_ref=rows(*block),
                send_sem=send_sems.at[7 * a + k], recv_sem=recv_sems.at[7 * a + k],
                device_id=to, device_id_type=MESH)

        mine = pltpu.make_async_copy(ins[a], rows(*me), local_sems.at[a])
        mine.start()
        local.append(mine)
        f = [copy(0, me, sibling, src=ins[a])]
        f += [copy(1 + j, me, (*chip, c), src=ins[a]) for j, chip in enumerate(chips)]
        for cp in f:
            cp.start()
        first.append((f, copy))
    for a in range(na):
        f, copy = first[a]
        p = [copy(4 + j, (*chip, c), sibling) for j, chip in enumerate(chips)]
        for j, chip in enumerate(chips):
            copy(1 + j, (*chip, c), me).wait_recv()
            p[j].start()
        passed.append(p)
    for a in range(na):
        f, copy = first[a]
        copy(0, sibling, me).wait_recv()
        for j, chip in enumerate(chips):
            copy(4 + j, (*chip, 1 - c), me).wait_recv()
        for cp in f + passed[a]:
            cp.wait_send()
        local[a].wait()


def _allgather8_seq(name, collective_id, arrs):
    na = len(arrs)
    hbm = pltpu.MemorySpace.HBM
    ins = [jax.new_ref(a, memory_space=hbm) for a in arrs]
    outs = [jax.empty_ref(jax.ShapeDtypeStruct((8 * a.shape[0], a.shape[1]), a.dtype), memory_space=hbm) for a in arrs]

    @pl.kernel(mesh=plsc.ScalarSubcoreMesh(axis_name="sequencer", num_cores=1), name=name,
               scratch_types=(pltpu.SemaphoreType.DMA((7 * na,)), pltpu.SemaphoreType.DMA((7 * na,)),
                              pltpu.SemaphoreType.DMA((na,))),
               compiler_params=pltpu.CompilerParams(collective_id=collective_id))
    def launch(send_sems, recv_sems, local_sems):
        x, y, c, chips = _position()
        peers = [(x, y, 1 - c)] + [(*chip, c) for chip in chips]
        barrier = pltpu.get_barrier_semaphore()
        for peer in peers:
            pl.semaphore_signal(barrier, inc=1, device_id=peer, device_id_type=MESH)
        pl.semaphore_wait(barrier, len(peers))
        _ag8_run(ins, outs, send_sems, recv_sems, local_sems)

    launch()
    return [o[...] for o in outs]


def _allgather8(name, arrs):
    na = len(arrs)

    def body(*refs):
        _ag8_run(refs[:na], refs[na:2 * na], *refs[2 * na:])

    return pl.pallas_call(
        body, name=name,
        out_shape=[jax.ShapeDtypeStruct((8 * a.shape[0], a.shape[1]), a.dtype) for a in arrs],
        in_specs=[VMEM] * na, out_specs=[VMEM] * na,
        scratch_shapes=[pltpu.SemaphoreType.DMA((7 * na,)), pltpu.SemaphoreType.DMA((7 * na,)),
                        pltpu.SemaphoreType.DMA((na,))],
        compiler_params=pltpu.CompilerParams(vmem_limit_bytes=VMEM_LIMIT),
    )(*arrs)


AG_SEMS = 7


def _ag_copies(ins, outs, send_sems, recv_sems):
    x, y, c, chips = _position()
    sibling = (x, y, 1 - c)
    xn, yn, dg = [2 * chip[0] + chip[1] for chip in chips]
    to_x, to_y = (1 - x, y, c), (x, 1 - y, c)
    res = []
    for a in range(len(ins)):
        half = ins[a].shape[0] // 2
        quarter = half // 2

        def copy(k, dst, to, src=None, a=a):
            return pltpu.make_async_remote_copy(
                src_ref=dst if src is None else src, dst_ref=dst,
                send_sem=send_sems.at[AG_SEMS * a + k], recv_sem=recv_sems.at[AG_SEMS * a + k],
                device_id=to, device_id_type=MESH)

        def rows(chip, pc, q=None, a=a, half=half, quarter=quarter):
            if q is None:
                return outs[a].at[chip, pl.ds(pc * half, half), :]
            return outs[a].at[chip, pl.ds(pc * half + q * quarter, quarter), :]

        own = ins[a].at[pl.ds(c * half, half), :]
        mine = rows(2 * x + y, c)
        res.append(dict(
            sends=[copy(0, mine, to_x, src=own), copy(1, mine, to_y, src=own)],
            from_x=copy(0, rows(xn, c), to_x), from_y=copy(1, rows(yn, c), to_y),
            relay_y=copy(2, rows(xn, c, 0), to_y), relay_x=copy(3, rows(yn, c, 1), to_x),
            from_y_relay=copy(2, rows(dg, c, 0), to_y), from_x_relay=copy(3, rows(dg, c, 1), to_x),
            pass_on=[copy(4, rows(xn, c), sibling), copy(5, rows(yn, c), sibling), copy(6, rows(dg, c), sibling)],
            from_sibling=[copy(4, rows(xn, 1 - c), sibling), copy(5, rows(yn, 1 - c), sibling),
                          copy(6, rows(dg, 1 - c), sibling)]))
    return res


def _ag_start(ins, outs, send_sems, recv_sems):
    for cps in _ag_copies(ins, outs, send_sems, recv_sems):
        for cp in cps["sends"]:
            cp.start()


def _ag_relay(ins, outs, send_sems, recv_sems, which):
    copies = _ag_copies(ins, outs, send_sems, recv_sems)
    for a in which:
        cps = copies[a]
        cps["from_x"].wait_recv()
        cps["relay_y"].start()
        cps["pass_on"][0].start()
        cps["from_y"].wait_recv()
        cps["relay_x"].start()
        cps["pass_on"][1].start()


def _ag_complete(ins, outs, send_sems, recv_sems):
    copies = _ag_copies(ins, outs, send_sems, recv_sems)
    for cps in copies:
        cps["from_y_relay"].wait_recv()
        cps["from_x_relay"].wait_recv()
        cps["pass_on"][2].start()
    for cps in copies:
        for cp in cps["from_sibling"]:
            cp.wait_recv()
        for cp in cps["sends"] + [cps["relay_y"], cps["relay_x"]] + cps["pass_on"]:
            cp.wait_send()


def _ag_finish(ins, outs, send_sems, recv_sems):
    _ag_relay(ins, outs, send_sems, recv_sems, range(len(ins)))
    _ag_complete(ins, outs, send_sems, recv_sems)


def _allgather_weights(name, collective_id, shards):
    na = len(shards)
    hbm = pltpu.MemorySpace.HBM
    ins = [jax.new_ref(s, memory_space=hbm) for s in shards]
    outs = [jax.empty_ref(jax.ShapeDtypeStruct((N_CHIP,) + s.shape, s.dtype), memory_space=hbm) for s in shards]

    @pl.kernel(mesh=plsc.ScalarSubcoreMesh(axis_name="sequencer", num_cores=1), name=name,
               scratch_types=(pltpu.SemaphoreType.DMA((AG_SEMS * na,)), pltpu.SemaphoreType.DMA((AG_SEMS * na,))),
               compiler_params=pltpu.CompilerParams(collective_id=collective_id))
    def launch(send_sems, recv_sems):
        x, y, c, _ = _position()
        peers = [(1 - x, y, c), (x, 1 - y, c), (x, y, 1 - c)]
        barrier = pltpu.get_barrier_semaphore()
        for peer in peers:
            pl.semaphore_signal(barrier, inc=1, device_id=peer, device_id_type=MESH)
        pl.semaphore_wait(barrier, len(peers))
        _ag_start(ins, outs, send_sems, recv_sems)
        _ag_finish(ins, outs, send_sems, recv_sems)

    launch()
    return [o[...] for o in outs]


def _sibling_swap(name, arrs, split_rows, collective_id=None):
    na = len(arrs)
    shapes = [jax.ShapeDtypeStruct((a.shape[0], a.shape[1] // 2, a.shape[2]) if split_rows else a.shape, a.dtype)
              for a in arrs]

    def run(ins, outs, send_sems, recv_sems):
        x, y, c, _ = _position()
        cps = []
        for a in range(na):
            src = ins[a]
            if split_rows:
                half = src.shape[1] // 2
                src = src.at[:, pl.ds((1 - c) * half, half), :]
            cp = pltpu.make_async_remote_copy(
                src_ref=src, dst_ref=outs[a], send_sem=send_sems.at[a], recv_sem=recv_sems.at[a],
                device_id=(x, y, 1 - c), device_id_type=MESH)
            cp.start()
            cps.append(cp)
        for cp in cps:
            cp.wait()

    sems = (pltpu.SemaphoreType.DMA((na,)), pltpu.SemaphoreType.DMA((na,)))
    if collective_id is None:
        return pl.pallas_call(
            lambda *refs: run(refs[:na], refs[na:2 * na], *refs[2 * na:]), name=name, out_shape=shapes,
            in_specs=[ANY] * na, out_specs=[ANY] * na, scratch_shapes=list(sems))(*arrs)

    hbm = pltpu.MemorySpace.HBM
    ins = [jax.new_ref(a, memory_space=hbm) for a in arrs]
    outs = [jax.empty_ref(s, memory_space=hbm) for s in shapes]

    @pl.kernel(mesh=plsc.ScalarSubcoreMesh(axis_name="sequencer", num_cores=1), name=name, scratch_types=sems,
               compiler_params=pltpu.CompilerParams(collective_id=collective_id))
    def launch(send_sems, recv_sems):
        x, y, c, _ = _position()
        barrier = pltpu.get_barrier_semaphore()
        pl.semaphore_signal(barrier, inc=1, device_id=(x, y, 1 - c), device_id_type=MESH)
        pl.semaphore_wait(barrier, 1)
        run(ins, outs, send_sems, recv_sems)

    launch()
    return [o[...] for o in outs]


def _xchg_copies(ins, outs, send_sems, recv_sems):
    x, y, c, chips = _position()
    return [pltpu.make_async_remote_copy(
        src_ref=ins[a].at[2 * chip[0] + chip[1]], dst_ref=outs[a].at[j],
        send_sem=send_sems.at[3 * a + j], recv_sem=recv_sems.at[3 * a + j],
        device_id=(*chip, c), device_id_type=MESH) for a in range(len(ins)) for j, chip in enumerate(chips)]


def _exchange_chips(name, collective_id, parts):
    na = len(parts)
    hbm = pltpu.MemorySpace.HBM
    ins = [jax.new_ref(p, memory_space=hbm) for p in parts]
    outs = [jax.empty_ref(jax.ShapeDtypeStruct((3,) + p.shape[1:], p.dtype), memory_space=hbm) for p in parts]

    @pl.kernel(mesh=plsc.ScalarSubcoreMesh(axis_name="sequencer", num_cores=1), name=name,
               scratch_types=(pltpu.SemaphoreType.DMA((3 * na,)), pltpu.SemaphoreType.DMA((3 * na,))),
               compiler_params=pltpu.CompilerParams(collective_id=collective_id))
    def launch(send_sems, recv_sems):
        x, y, c, chips = _position()
        barrier = pltpu.get_barrier_semaphore()
        for chip in chips:
            pl.semaphore_signal(barrier, inc=1, device_id=(*chip, c), device_id_type=MESH)
        pl.semaphore_wait(barrier, len(chips))
        for cp in _xchg_copies(ins, outs, send_sems, recv_sems):
            cp.start()
        for cp in _xchg_copies(ins, outs, send_sems, recv_sems):
            cp.wait()

    launch()
    return [q[...] for q in outs]


def _add_sibling(name, grad, recv, core, after=()):
    _, r, c = grad.shape
    half = r // 2
    rb = half
    nrb = half // rb

    def body(core_ref, g_ref, r_ref, *refs):
        refs[-1][...] = (g_ref[...].astype(F32) + r_ref[...].astype(F32)).astype(BF16)

    return pl.pallas_call(
        body, name=name,
        grid_spec=pltpu.PrefetchScalarGridSpec(
            num_scalar_prefetch=1, grid=(N_CHIP, nrb),
            in_specs=[pl.BlockSpec((1, rb, c), lambda j, i, core_ref: (j, core_ref[0] * nrb + i, 0)),
                      pl.BlockSpec((1, rb, c), lambda j, i, core_ref: (j, i, 0))] + [ANY] * len(after),
            out_specs=pl.BlockSpec((1, rb, c), lambda j, i, core_ref: (j, i, 0))),
        out_shape=jax.ShapeDtypeStruct((N_CHIP, half, c), BF16),
        compiler_params=pltpu.CompilerParams(dimension_semantics=("arbitrary", "arbitrary"),
                                             vmem_limit_bytes=VMEM_LIMIT),
    )(core, grad, recv, *after)


def _add_chips(name, chip, p, q, after=()):
    _, half, c = q.shape
    rb = min(half, 256)

    def body(chip_ref, p_ref, q_ref, *refs):
        acc = p_ref[0].astype(F32)
        for j in range(3):
            acc = acc + q_ref[j].astype(F32)
        refs[-1][...] = acc

    return pl.pallas_call(
        body, name=name,
        grid_spec=pltpu.PrefetchScalarGridSpec(
            num_scalar_prefetch=1, grid=(half // rb,),
            in_specs=[pl.BlockSpec((1, rb, c), lambda i, chip_ref: (chip_ref[0], i, 0)),
                      pl.BlockSpec((3, rb, c), lambda i, chip_ref: (0, i, 0))] + [ANY] * len(after),
            out_specs=pl.BlockSpec((rb, c), lambda i, chip_ref: (i, 0))),
        out_shape=jax.ShapeDtypeStruct((half, c), F32),
        compiler_params=pltpu.CompilerParams(dimension_semantics=("arbitrary",), vmem_limit_bytes=VMEM_LIMIT),
    )(chip, p, q, *after)


def _small_update(gad, gam, gl, gg, mychip, params):
    names = ["ada_b", "norm1_g", "lru_conv_b", "gate_a_w", "gate_a_b", "gate_x_w", "gate_x_b", "a_param",
             "lru_conv_w", "short_conv_w", "lru_out_g", "conv_out_g", "norm2_g", "final_g"]
    flat = [t for n in names for t in params[n]]
    nin = len(flat)

    def body(chip_ref, gad_ref, gam_ref, gl_ref, gg_ref, *refs):
        ins = {n: refs[3 * k:3 * k + 3] for k, n in enumerate(names)}
        outs = {n: refs[nin + 4 * k:nin + 4 * k + 4] for k, n in enumerate(names)}
        loss_ref, dmod_ref = refs[nin + 4 * len(names):nin + 4 * len(names) + 2]

        def dsum(ref, lo, n):
            per = ref.shape[0] // 8
            acc = ref[lo:lo + n, :].astype(F32)
            for dev in range(1, 8):
                acc = acc + ref[dev * per + lo:dev * per + lo + n, :].astype(F32)
            return acc

        def update(n, g):
            w_ref, m_ref, v_ref = ins[n]
            g_ref, d_ref, mo_ref, vo_ref = outs[n]
            g = g.reshape(w_ref.shape)
            g_ref[...] = g
            d_ref[...], mo_ref[...], vo_ref[...] = _adam_math(w_ref[...], g, m_ref[...], v_ref[...])

        d, dm, l, lw = refs[-4:]
        d[...] = dsum(gad_ref, 0, 8)
        dm[...] = dsum(gam_ref, 0, 8)
        l[...] = dsum(gl_ref, 0, 16)
        for dev in range(8):
            for k in range(3):
                dmod_ref[dev:dev + 1, k * D_MODEL:(k + 1) * D_MODEL] = gad_ref[dev * 8 + k:dev * 8 + k + 1, :]
                dmod_ref[dev:dev + 1, (3 + k) * D_MODEL:(4 + k) * D_MODEL] = gam_ref[dev * 8 + k:dev * 8 + k + 1, :]
        w_ref, m_ref, v_ref = ins["ada_b"]
        g_ref, d_ref, mo_ref, vo_ref = outs["ada_b"]
        for k in range(3):
            g_ref[:, k * D_MODEL:(k + 1) * D_MODEL] = d[k:k + 1, :]
            g_ref[:, (3 + k) * D_MODEL:(4 + k) * D_MODEL] = dm[k:k + 1, :]
        d_ref[...], mo_ref[...], vo_ref[...] = _adam_math(w_ref[...], g_ref[...], m_ref[...], v_ref[...])
        update("norm1_g", d[3:4, :])
        update("norm2_g", dm[3:4, :])
        update("final_g", dm[4:5, :])
        update("gate_a_b", d[4:5, 0:D_LRU])
        update("gate_x_b", d[4:5, D_LRU:2 * D_LRU])
        update("lru_conv_b", l[4:5, :])
        update("a_param", l[8:9, :] * jax.nn.sigmoid(ins["a_param"][0][...]))
        update("lru_out_g", l[9:10, :])
        update("conv_out_g", l[10:11, :])
        loss_ref[...] = jnp.broadcast_to(dm[5:6, 0:128], (8, 128))
        chip = chip_ref[0]
        acc = jnp.zeros((8, 128), F32)
        for j in range(N_CHIP):
            acc = acc + jnp.where(chip == j, l[0:8, j * 128:(j + 1) * 128], 0.0)
        lw[...] = acc
        update("lru_conv_w", lw[0:4, :])
        update("short_conv_w", lw[5:8, :])
        gates = dsum(gg_ref, 0, D_LRU)
        update("gate_a_w", gates[:, 0:HEAD])
        update("gate_x_w", gates[:, HEAD:2 * HEAD])

    out_shape = []
    for n in names:
        out_shape += [jax.ShapeDtypeStruct(params[n][0].shape, F32)] * 4
    out_shape += [jax.ShapeDtypeStruct((8, 128), F32), jax.ShapeDtypeStruct((8, 6 * D_MODEL), F32)]
    res = pl.pallas_call(
        body, name="small_update", out_shape=out_shape,
        in_specs=[SMEM] + [VMEM] * (4 + nin),
        out_specs=[VMEM] * len(out_shape),
        scratch_shapes=[pltpu.VMEM((8, D_MODEL), F32), pltpu.VMEM((8, D_MODEL), F32), pltpu.VMEM((16, D_LRU), F32),
                        pltpu.VMEM((8, 128), F32)],
        compiler_params=pltpu.CompilerParams(vmem_limit_bytes=VMEM_LIMIT),
    )(mychip, gad, gam, gl, gg, *flat)
    per = {n: res[4 * k:4 * k + 4] for k, n in enumerate(names)}
    return per, res[-2], res[-1]


def _block_diag(w):
    eye = jnp.eye(8, dtype=w.dtype)
    return (eye[:, None, :, None] * w[:, :, None, :]).reshape(8 * HEAD, 8 * HEAD)


def _diag_blocks(g):
    return jnp.concatenate([g[h * HEAD:(h + 1) * HEAD, h * HEAD:(h + 1) * HEAD] for h in range(8)], axis=0)


def kernel(x, c, ada_w, ada_b, norm1_g, w_in, lru_conv_w, lru_conv_b, gate_a_w, gate_a_b, gate_x_w, gate_x_b, a_param, short_conv_w, lru_out_g, conv_out_g, w_out, norm2_g, w_mlp1, w_mlp2, final_g, loss_target, m_ada_w, m_ada_b, m_norm1_g, m_w_in, m_lru_conv_w, m_lru_conv_b, m_gate_a_w, m_gate_a_b, m_gate_x_w, m_gate_x_b, m_a_param, m_short_conv_w, m_lru_out_g, m_conv_out_g, m_w_out, m_norm2_g, m_w_mlp1, m_w_mlp2, m_final_g, v_ada_w, v_ada_b, v_norm1_g, v_w_in, v_lru_conv_w, v_lru_conv_b, v_gate_a_w, v_gate_a_b, v_gate_x_w, v_gate_x_b, v_a_param, v_short_conv_w, v_lru_out_g, v_conv_out_g, v_w_out, v_norm2_g, v_w_mlp1, v_w_mlp2, v_final_g):
    xi, yi, ci = lax.axis_index("x"), lax.axis_index("y"), lax.axis_index("c")
    mychip = 2 * xi + yi
    me = 4 * xi + 2 * yi + ci

    own_in, own_out = w_in[0].astype(BF16), w_out[0].astype(BF16)
    win_all, wout_all = _allgather_weights("allgather_mixer_weights", 1, [own_in, own_out])
    own_w1, own_w2 = w_mlp1[0].astype(BF16), w_mlp2[0].astype(BF16)
    w1_all, w2_all = _allgather_weights("allgather_mlp_weights", 2, [own_w1, own_w2])

    c_blk = jnp.zeros((8, D_MODEL), F32).at[0:1].set(c)
    cw_blk = jnp.zeros((8, 128), F32).at[0:4].set(lru_conv_w[0]).at[4:7].set(short_conv_w[0])
    c_g, cw_g = _allgather8("allgather_cond", [c_blk, cw_blk])
    c_all = c_g.reshape(8, 8, D_MODEL)[:, 0]
    cw_g = cw_g.reshape(4, 2, 8, 128)[:, 0]
    lcw = cw_g[:, 0:4].transpose(1, 0, 2).reshape(4, D_LRU)
    scw = cw_g[:, 4:7].transpose(1, 0, 2).reshape(3, D_LRU)

    mod_loc = _mod_matmul(c_all, ada_w[0])
    (mod_g,) = _allgather8("allgather_mod", [mod_loc])
    mod_all = mod_g.reshape(4, 2, 8, 6 * D_MODEL // 4)[:, 0].transpose(1, 0, 2).reshape(8, 6 * D_MODEL) + ada_b
    mod_pad = jnp.pad(mod_all.reshape(8, 6, D_MODEL), ((0, 0), (0, 2), (0, 0)))
    mod = lax.dynamic_slice_in_dim(mod_pad, me, 1, axis=0).reshape(8, D_MODEL)

    win, wout = (win_all, own_in), (wout_all, own_out)
    chip = mychip.reshape(1).astype(jnp.int32)
    core = ci.reshape(1).astype(jnp.int32)

    vecd = jnp.concatenate([norm1_g, norm2_g, final_g[None, :], jnp.concatenate([gate_a_b, gate_x_b], axis=1),
                            jnp.zeros((4, D_MODEL), F32)], axis=0)
    vecl = jnp.concatenate([lcw, lru_conv_b, scw, a_param, lru_out_g, conv_out_g, jnp.zeros((5, D_LRU), F32)], axis=0)
    gab = jnp.concatenate([_block_diag(gate_a_w[0]), _block_diag(gate_x_w[0])], axis=1).astype(BF16)
    a64 = _block_diag(jnp.full((8, HEAD, HEAD), 1.0 / HEAD, F32)).astype(BF16)

    hb, proj, hl, ycat, mixed, x1 = _mix_fwd(chip, x[0], mod, vecd, vecl, win, wout, gab, a64)
    dx1, act, dz, dmo, h2b, accm = _mlp_fwd_bwd(
        chip, x1, loss_target[0], mod, vecd, (w1_all, own_w1), (w2_all, own_w2))

    parts_mlp = list(_wgrad_mlp(6, h2b, dz, act, dmo))
    q_w1, q_w2 = _exchange_chips("rs_exchange_mlp", 0, parts_mlp)
    grad_x, accd, accl, g_win, g_wout, g_gate = _mix_bwd(
        chip, dx1, x[0], mixed, proj, hl, hb, ycat, mod, vecd, vecl, win, wout, gab, a64)

    g_mix = [g_win, g_wout.reshape(N_CHIP, WOUT_BLK, D_MODEL)]
    recv_mix = _sibling_swap("rs_swap_halves_mix", g_mix, True, collective_id=4)
    own_mlp = [_add_chips("rs_add_chips_mlp%d" % k, chip, p, q) for k, (p, q) in enumerate(zip(parts_mlp, (q_w1, q_w2)))]
    sib_mlp = _sibling_swap("rs_swap_reduced_mlp", own_mlp, False, collective_id=5)
    gg_blk = jnp.concatenate([_diag_blocks(g_gate[:, 0:D_LRU]), _diag_blocks(g_gate[:, D_LRU:2 * D_LRU])], axis=1)
    gad, gam, gl, gg = _allgather8_seq("allgather_small_grads", 8, [accd, accm, accl, gg_blk.astype(BF16)])

    parts_mix = [_add_sibling("rs_add_sibling_mix%d" % k, g, r, core)
                 for k, (g, r) in enumerate(zip(g_mix, recv_mix))]
    landed_mix = _exchange_chips("rs_exchange_mix", 3, parts_mix)
    res_w1, res_w2 = _adam("adam_mlp", core, [(w_mlp1[0], own_mlp[0], sib_mlp[0], m_w_mlp1[0], v_w_mlp1[0]),
                                              (w_mlp2[0], own_mlp[1], sib_mlp[1], m_w_mlp2[0], v_w_mlp2[0])])
    own_mix = [_add_chips("rs_add_chips_mix%d" % k, chip, p, q, after=[res_w1[1]])
               for k, (p, q) in enumerate(zip(parts_mix, landed_mix))]
    sib_mix = _sibling_swap("rs_swap_reduced_mix", own_mix, False)
    (res_win,) = _adam("adam_w_in", core, [(w_in[0], own_mix[0], sib_mix[0], m_w_in[0], v_w_in[0])])
    (res_wout,) = _adam("adam_w_out", core, [(w_out[0], own_mix[1], sib_mix[1], m_w_out[0], v_w_out[0])])

    params = {
        "ada_b": (ada_b, m_ada_b, v_ada_b), "norm1_g": (norm1_g, m_norm1_g, v_norm1_g),
        "lru_conv_b": (lru_conv_b, m_lru_conv_b, v_lru_conv_b),
        "gate_a_w": tuple(t.reshape(D_LRU, HEAD) for t in (gate_a_w, m_gate_a_w, v_gate_a_w)),
        "gate_a_b": (gate_a_b, m_gate_a_b, v_gate_a_b),
        "gate_x_w": tuple(t.reshape(D_LRU, HEAD) for t in (gate_x_w, m_gate_x_w, v_gate_x_w)),
        "gate_x_b": (gate_x_b, m_gate_x_b, v_gate_x_b), "a_param": (a_param, m_a_param, v_a_param),
        "lru_conv_w": (lru_conv_w, m_lru_conv_w, v_lru_conv_w),
        "short_conv_w": (short_conv_w, m_short_conv_w, v_short_conv_w),
        "lru_out_g": (lru_out_g, m_lru_out_g, v_lru_out_g), "conv_out_g": (conv_out_g, m_conv_out_g, v_conv_out_g),
        "norm2_g": (norm2_g, m_norm2_g, v_norm2_g),
        "final_g": tuple(t[None, :] for t in (final_g, m_final_g, v_final_g)),
    }
    small, loss_blk, dmod_cols = _small_update(gad, gam, gl, gg, chip, params)
    loss = loss_blk[0, 0]

    sct = (c_all * jax.nn.sigmoid(c_all)).T
    ada = _ada_grad_adam(chip, sct, dmod_cols, ada_w[0], m_ada_w[0], v_ada_w[0])

    res = {"ada_w": ada, "w_in": res_win, "w_out": res_wout, "w_mlp1": res_w1, "w_mlp2": res_w2}
    res = {n: tuple(t[None] for t in r) for n, r in res.items()}
    shapes = {"gate_a_w": gate_a_w.shape, "gate_x_w": gate_x_w.shape, "lru_conv_w": lru_conv_w.shape,
              "short_conv_w": short_conv_w.shape, "final_g": final_g.shape}
    for n, t in small.items():
        res[n] = tuple(u.reshape(shapes[n]) if n in shapes else u for u in t)

    order = ["ada_w", "ada_b", "norm1_g", "w_in", "lru_conv_w", "lru_conv_b", "gate_a_w", "gate_a_b", "gate_x_w",
             "gate_x_b", "a_param", "short_conv_w", "lru_out_g", "conv_out_g", "w_out", "norm2_g", "w_mlp1",
             "w_mlp2", "final_g"]
    return (loss, grad_x[None], *[res[n][0] for n in order], *[res[n][1] for n in order],
            *[res[n][2] for n in order], *[res[n][3] for n in order])
```

```python
import jax
import jax.numpy as jnp
from jax import lax
from jax.experimental import pallas as pl
from jax.experimental.pallas import tpu as pltpu
from jax.experimental.pallas import tpu_sc as plsc

F32 = jnp.float32
BF16 = jnp.bfloat16

D_MODEL = 1024
D_LRU = 512
D_IN = 2560
D_FF = 4096
N_CHIP = 4
WIN_BLK = D_IN // N_CHIP
WOUT_BLK = D_MODEL // N_CHIP
FF_BLK = D_FF // N_CHIP
HEAD = 64
EPS = 1e-6
C_GATE = 8.0
TOKEN_TILE = 256
MIX_FWD_TILE = 512
HALO = 8
VMEM_LIMIT = 60 * 1024 * 1024

ADAM_LR = 0.001
ADAM_B1 = 0.9
ADAM_B2 = 0.999
ADAM_EPS = 1e-08
ADAM_WD = 0.01
ADAM_STEP = 10

MESH = pl.DeviceIdType.MESH
ANY = pl.BlockSpec(memory_space=pl.ANY)
VMEM = pl.BlockSpec(memory_space=pltpu.VMEM)
SMEM = pl.BlockSpec(memory_space=pltpu.SMEM)


def _full(shape, single=False):
    nd = len(shape)
    if single:
        return pl.BlockSpec(shape, lambda *_: (0,) * nd, pipeline_mode=pl.Buffered(1))
    return pl.BlockSpec(shape, lambda *_: (0,) * nd)


def _dot(a, b):
    return jnp.dot(a, b, preferred_element_type=F32)


def _dot_nt(a, b):
    return lax.dot_general(a, b, (((1,), (1,)), ((), ())), preferred_element_type=F32)


def _dot_tn(a, b):
    return lax.dot_general(a, b, (((0,), (0,)), ((), ())), preferred_element_type=F32)


def _gmean(v, a64):
    hi = v.astype(BF16)
    lo = (v - hi.astype(F32)).astype(BF16)
    return _dot(hi, a64) + _dot(lo, a64)


def _gelu(x):
    u = 0.7978845608028654 * (x + 0.044715 * x * x * x)
    t = jnp.tanh(u)
    return 0.5 * x * (1.0 + t), t


def _gelu_grad(x, t):
    du = 0.7978845608028654 * (1.0 + 3.0 * 0.044715 * x * x)
    return 0.5 * (1.0 + t) + 0.5 * x * (1.0 - t * t) * du


def _log1p_pos(y):
    return jnp.where(y < 1e-2, y * (1.0 - y * (0.5 - y * (1.0 / 3.0 - y * 0.25))), jnp.log(1.0 + y))


def _softplus(a):
    return jnp.maximum(a, 0.0) + _log1p_pos(jnp.exp(-jnp.abs(a)))


def _neg_expm1(z):
    series = -z * (1.0 + z * (0.5 + z * (1.0 / 6.0 + z * (1.0 / 24.0))))
    return jnp.where(z > -0.02, series, 1.0 - jnp.exp(z))


def _scan_fwd(a, b, row):
    n = a.shape[0]
    d = 1
    while d < n:
        m = row >= d
        b = jnp.where(m, a * pltpu.roll(b, d, 0) + b, b)
        a = jnp.where(m, a * pltpu.roll(a, d, 0), a)
        d *= 2
    return a, b


def _scan_rev(a, b, row):
    n = a.shape[0]
    d = 1
    while d < n:
        m = row < n - d
        b = jnp.where(m, b + a * pltpu.roll(b, n - d, 0), b)
        a = jnp.where(m, a * pltpu.roll(a, n - d, 0), a)
        d *= 2
    return a, b


def _colsum(v):
    return jnp.sum(v, axis=0, keepdims=True)


def _load_gathered(chip, gathered, own, slot, sems):
    copies = []
    for j in range(N_CHIP):
        @pl.when(chip == j)
        def _(j=j):
            pltpu.make_async_copy(own, slot(j), sems.at[j]).start()

        @pl.when(chip != j)
        def _(j=j):
            pltpu.make_async_copy(gathered.at[j], slot(j), sems.at[j]).start()

        copies.append(pltpu.make_async_copy(own, slot(j), sems.at[j]))
    return copies


def _gathered_waits(own, slot, sems):
    return [pltpu.make_async_copy(own, slot(j), sems.at[j]) for j in range(N_CHIP)]


def _lru_gates(xlb, gab, gbias, sp, first_row):
    g = _dot(xlb, gab) + gbias
    r = jax.nn.sigmoid(g[:, :D_LRU])
    ig = jax.nn.sigmoid(g[:, D_LRU:])
    la = (-C_GATE) * r * sp
    a = jnp.exp(la)
    msq = jnp.sqrt(_neg_expm1(2.0 * la))
    mult = jnp.where(first_row, 1.0, msq)
    return r, ig, a, msq, mult


def _mix_fwd(chip, x, mod, vecd, vecl, win, wout, gab, a64):
    s = x.shape[0]
    ts = MIX_FWD_TILE
    nt = s // ts

    def body(chip_ref, x_ref, mod_ref, vd_ref, vl_ref, win_hbm, win_own, wout_hbm, wout_own, gab_ref, a64_ref,
             hb_ref, proj_ref, hl_ref, ycat_ref, mixed_ref, x1_ref,
             win_ref, wout_ref, ext_lx, ext_cv, hcar, sems):
        i = pl.program_id(0)

        @pl.when(i == 0)
        def _():
            cps = _load_gathered(chip_ref[0], win_hbm, win_own, lambda j: win_ref.at[j], sems.at[pl.ds(0, N_CHIP)])
            cps += _load_gathered(chip_ref[0], wout_hbm, wout_own,
                                  lambda j: wout_ref.at[pl.ds(j * WOUT_BLK, WOUT_BLK), :],
                                  sems.at[pl.ds(N_CHIP, N_CHIP)])
            ext_lx[0:HALO, :] = jnp.zeros((HALO, D_LRU), F32)
            ext_cv[0:HALO, :] = jnp.zeros((HALO, D_LRU), F32)
            hcar[...] = jnp.zeros_like(hcar)
            for cp in cps:
                cp.wait()

        row = lax.broadcasted_iota(jnp.int32, (ts, D_LRU), 0)
        first_row = jnp.logical_and(row == 0, i == 0)
        xt = x_ref[...]
        shift1, scale1, gate1 = mod_ref[0:1, :], mod_ref[1:2, :], mod_ref[2:3, :]
        r1 = lax.rsqrt(jnp.mean(xt * xt, axis=-1, keepdims=True) + EPS)
        h = (xt * r1) * vd_ref[0:1, :] * (1.0 + scale1) + shift1
        hb = h.astype(BF16)
        hb_ref[...] = hb
        for j in range(N_CHIP):
            proj_ref[:, j * WIN_BLK:(j + 1) * WIN_BLK] = _dot(hb, win_ref[j])
        u_ly = proj_ref[:, 512:1024]
        u_b = proj_ref[:, 1024:1536]

        ext_lx[HALO:HALO + ts, :] = proj_ref[:, 0:512]
        xl = vl_ref[4:5, :] + vl_ref[0:1, :] * ext_lx[pl.ds(5, ts), :]
        for k in range(1, 4):
            xl = xl + vl_ref[k:k + 1, :] * ext_lx[pl.ds(5 + k, ts), :]
        ext_lx[0:HALO, :] = ext_lx[ts:ts + HALO, :]
        sp = _softplus(vl_ref[8:9, :])
        _, ig, a, _, mult = _lru_gates(xl.astype(BF16), gab_ref[...], vd_ref[3:4, :], sp, first_row)
        acum, hloc = _scan_fwd(a, mult * (ig * xl), row)
        hl = hloc + acum * hcar[0:1, :]
        hl_ref[...] = hl
        hcar[0:1, :] = hl_ref[ts - 1:ts, :]
        ge, _ = _gelu(u_ly)
        p = ge * hl
        y_lru = p * lax.rsqrt(_gmean(p * p, a64_ref[...]) + EPS) * vl_ref[9:10, :]
        ycat_ref[:, 0:512] = y_lru.astype(BF16)

        ext_cv[HALO:HALO + ts, :] = proj_ref[:, 1536:2048] * proj_ref[:, 2048:2560]
        q = vl_ref[5:6, :] * ext_cv[pl.ds(6, ts), :]
        for k in range(1, 3):
            q = q + vl_ref[5 + k:6 + k, :] * ext_cv[pl.ds(6 + k, ts), :]
        ext_cv[0:HALO, :] = ext_cv[ts:ts + HALO, :]
        yc = u_b * q
        y_conv = yc * lax.rsqrt(_gmean(yc * yc, a64_ref[...]) + EPS) * vl_ref[10:11, :]
        ycat_ref[:, 512:1024] = y_conv.astype(BF16)

        mixed = _dot(ycat_ref[...], wout_ref[...])
        mixed_ref[...] = mixed
        x1_ref[...] = xt + gate1 * mixed

    tile = lambda w: pl.BlockSpec((ts, w), lambda i: (i, 0))
    return pl.pallas_call(
        body, name="mix_fwd", grid=(nt,),
        in_specs=[SMEM, tile(D_MODEL), _full((8, D_MODEL)), _full((8, D_MODEL)), _full((16, D_LRU)),
                  ANY, ANY, ANY, ANY, _full((D_LRU, 2 * D_LRU), True), _full((D_LRU, D_LRU), True)],
        out_specs=[tile(D_MODEL), tile(D_IN), tile(D_LRU), tile(D_MODEL), tile(D_MODEL), tile(D_MODEL)],
        out_shape=[jax.ShapeDtypeStruct((s, D_MODEL), BF16), jax.ShapeDtypeStruct((s, D_IN), F32),
                   jax.ShapeDtypeStruct((s, D_LRU), F32), jax.ShapeDtypeStruct((s, D_MODEL), BF16),
                   jax.ShapeDtypeStruct((s, D_MODEL), F32), jax.ShapeDtypeStruct((s, D_MODEL), F32)],
        scratch_shapes=[pltpu.VMEM((N_CHIP, D_MODEL, WIN_BLK), BF16), pltpu.VMEM((D_MODEL, D_MODEL), BF16),
                        pltpu.VMEM((ts + HALO, D_LRU), F32), pltpu.VMEM((ts + HALO, D_LRU), F32),
                        pltpu.VMEM((HALO, D_LRU), F32), pltpu.SemaphoreType.DMA((2 * N_CHIP,))],
        compiler_params=pltpu.CompilerParams(dimension_semantics=("arbitrary",), vmem_limit_bytes=VMEM_LIMIT),
    )(chip, x, mod, vecd, vecl, *win, *wout, gab, a64)


def _mlp_fwd_bwd(chip, x1, target, mod, vecd, w1, w2):
    s = x1.shape[0]
    ts = TOKEN_TILE
    nt = s // ts

    def body(chip_ref, x1_ref, tg_ref, mod_ref, vd_ref, w1_hbm, w1_own, w2_hbm, w2_own,
             dx1_ref, act_ref, dz_ref, dmo_ref, h2_ref, acc_ref, w1_v, w2_v, rz_v, sems):
        i = pl.program_id(0)

        @pl.when(i == 0)
        def _():
            cps = _load_gathered(chip_ref[0], w1_hbm, w1_own, lambda j: w1_v.at[j], sems.at[pl.ds(0, N_CHIP)])
            _load_gathered(chip_ref[0], w2_hbm, w2_own, lambda j: w2_v.at[j], sems.at[pl.ds(N_CHIP, N_CHIP)])
            acc_ref[...] = jnp.zeros_like(acc_ref)
            for cp in cps:
                cp.wait()

        xt = x1_ref[...]
        shift2, scale2, gate2 = mod_ref[3:4, :], mod_ref[4:5, :], mod_ref[5:6, :]
        g2, gf = vd_ref[1:2, :], vd_ref[2:3, :]
        r2 = lax.rsqrt(jnp.mean(xt * xt, axis=-1, keepdims=True) + EPS)
        n2 = xt * r2
        h2b = (n2 * g2 * (1.0 + scale2) + shift2).astype(BF16)
        h2_ref[...] = h2b
        for j in range(N_CHIP):
            rz_v[j] = jnp.maximum(_dot(h2b, w1_v[j]), 0.0)

        @pl.when(i == 0)
        def _():
            for cp in _gathered_waits(w2_own, lambda j: w2_v.at[j], sems.at[pl.ds(N_CHIP, N_CHIP)]):
                cp.wait()

        mo = jnp.zeros((ts, D_MODEL), F32)
        for j in range(N_CHIP):
            rz = rz_v[j]
            actb = (rz * rz).astype(BF16)
            act_ref[:, j * FF_BLK:(j + 1) * FF_BLK] = actb
            mo = mo + _dot(actb, w2_v[j])
        x2 = xt + gate2 * mo
        r3 = lax.rsqrt(jnp.mean(x2 * x2, axis=-1, keepdims=True) + EPS)
        n3 = x2 * r3
        e = n3 * gf - tg_ref[...]
        loss = (0.5 / D_MODEL) * jnp.sum(_colsum(e * e), axis=1, keepdims=True)
        dy = e * (1.0 / D_MODEL)
        acc_ref[4:5, :] += _colsum(dy * n3)
        acc_ref[5:6, :] += jnp.broadcast_to(loss, (1, D_MODEL))
        dn3 = dy * gf
        dx2 = r3 * (dn3 - n3 * jnp.mean(dn3 * n3, axis=-1, keepdims=True))
        acc_ref[2:3, :] += _colsum(dx2 * mo)
        dmob = (dx2 * gate2).astype(BF16)
        dmo_ref[...] = dmob
        for j in range(N_CHIP):
            dz_ref[:, j * FF_BLK:(j + 1) * FF_BLK] = (_dot_nt(dmob, w2_v[j]) * (2.0 * rz_v[j])).astype(BF16)
        dh2 = jnp.zeros((ts, D_MODEL), F32)
        for j in range(N_CHIP):
            dh2 = dh2 + _dot_nt(dz_ref[:, j * FF_BLK:(j + 1) * FF_BLK], w1_v[j])
        acc_ref[1:2, :] += _colsum(dh2 * (n2 * g2))
        acc_ref[0:1, :] += _colsum(dh2)
        dhn2 = dh2 * (1.0 + scale2)
        acc_ref[3:4, :] += _colsum(dhn2 * n2)
        dn2 = dhn2 * g2
        dx1_ref[...] = dx2 + r2 * (dn2 - n2 * jnp.mean(dn2 * n2, axis=-1, keepdims=True))

    tile = lambda w: pl.BlockSpec((ts, w), lambda i: (i, 0))
    return pl.pallas_call(
        body, name="mlp_fwd_bwd", grid=(nt,),
        in_specs=[SMEM, tile(D_MODEL), tile(D_MODEL), _full((8, D_MODEL)), _full((8, D_MODEL)), ANY, ANY, ANY, ANY],
        out_specs=[tile(D_MODEL), tile(D_FF), tile(D_FF), tile(D_MODEL), tile(D_MODEL), _full((8, D_MODEL))],
        out_shape=[jax.ShapeDtypeStruct((s, D_MODEL), F32), jax.ShapeDtypeStruct((s, D_FF), BF16),
                   jax.ShapeDtypeStruct((s, D_FF), BF16), jax.ShapeDtypeStruct((s, D_MODEL), BF16),
                   jax.ShapeDtypeStruct((s, D_MODEL), BF16), jax.ShapeDtypeStruct((8, D_MODEL), F32)],
        scratch_shapes=[pltpu.VMEM((N_CHIP, D_MODEL, FF_BLK), BF16), pltpu.VMEM((N_CHIP, FF_BLK, D_MODEL), BF16),
                        pltpu.VMEM((N_CHIP, ts, FF_BLK), F32), pltpu.SemaphoreType.DMA((2 * N_CHIP,))],
        compiler_params=pltpu.CompilerParams(dimension_semantics=("arbitrary",), vmem_limit_bytes=VMEM_LIMIT),
    )(chip, x1, target, mod, vecd, *w1, *w2)


def _mix_bwd(chip, dx1, x, mixed, proj, hl, hb, ycat, mod, vecd, vecl, win, wout, gab, a64):
    s = x.shape[0]
    ts = TOKEN_TILE
    nt = s // ts
    hpt = ts // HALO

    def body(chip_ref, dx1_ref, x_ref, mixed_ref, proj_ref, projh_ref, hl_ref, hlh_ref, hb_ref, ycat_ref,
             mod_ref, vd_ref, vl_ref, win_hbm, win_own, wout_hbm, wout_own, gab_ref, a64_ref,
             gx_ref, accd_ref, accl_ref, gwin_hbm, gwout_hbm, ggate_hbm,
             win_ref, wout_ref, dproj_ref, dgb_ref, gwin_acc, gwout_acc, ggate_acc,
             ext_lx, ext_cv, ext_hl, ext_dxl, ext_dq, gbuf, gcar, acar, sems):
        i = pl.program_id(0)
        ri = nt - 1 - i

        @pl.when(i == 0)
        def _():
            cps = _load_gathered(chip_ref[0], wout_hbm, wout_own,
                                 lambda j: wout_ref.at[pl.ds(j * WOUT_BLK, WOUT_BLK), :],
                                 sems.at[pl.ds(N_CHIP, N_CHIP)])
            _load_gathered(chip_ref[0], win_hbm, win_own, lambda j: win_ref.at[j], sems.at[pl.ds(0, N_CHIP)])
            gwin_acc[...] = jnp.zeros_like(gwin_acc)
            gwout_acc[...] = jnp.zeros_like(gwout_acc)
            ggate_acc[...] = jnp.zeros_like(ggate_acc)
            for cp in cps:
                cp.wait()
            accd_ref[...] = jnp.zeros_like(accd_ref)
            accl_ref[...] = jnp.zeros_like(accl_ref)
            ext_dxl[ts:ts + HALO, :] = jnp.zeros((HALO, D_LRU), F32)
            ext_dq[ts:ts + HALO, :] = jnp.zeros((HALO, D_LRU), F32)
            gcar[...] = jnp.zeros_like(gcar)
            acar[...] = jnp.zeros_like(acar)

        row = lax.broadcasted_iota(jnp.int32, (ts, D_LRU), 0)
        first_row = jnp.logical_and(row == 0, ri == 0)
        halo_on = jnp.where(ri == 0, 0.0, 1.0)
        shift1, scale1, gate1 = mod_ref[0:1, :], mod_ref[1:2, :], mod_ref[2:3, :]
        g1 = vd_ref[0:1, :]
        a64m = a64_ref[...]
        lg, cg = vl_ref[9:10, :], vl_ref[10:11, :]

        dx1 = dx1_ref[...]
        accd_ref[2:3, :] += _colsum(dx1 * mixed_ref[...])
        dmb = (dx1 * gate1).astype(BF16)
        gwout_acc[...] += _dot_tn(ycat_ref[...], dmb)
        dycat = _dot_nt(dmb, wout_ref[...])
        dyl = dycat[:, 0:512]
        dyv = dycat[:, 512:1024]

        u_ly = proj_ref[:, 512:1024]
        u_b = proj_ref[:, 1024:1536]
        u_c = proj_ref[:, 1536:2048]
        u_v = proj_ref[:, 2048:2560]
        ext_lx[0:HALO, :] = projh_ref[:, 0:512] * halo_on
        ext_lx[HALO:HALO + ts, :] = proj_ref[:, 0:512]
        xl = vl_ref[4:5, :] + vl_ref[0:1, :] * ext_lx[pl.ds(5, ts), :]
        for k in range(1, 4):
            xl = xl + vl_ref[k:k + 1, :] * ext_lx[pl.ds(5 + k, ts), :]
        xlb = xl.astype(BF16)
        sp = _softplus(vl_ref[8:9, :])
        r, ig, a, msq, mult = _lru_gates(xlb, gab_ref[...], vd_ref[3:4, :], sp, first_row)
        hl = hl_ref[...]
        ge, th = _gelu(u_ly)
        p = ge * hl
        rl = lax.rsqrt(_gmean(p * p, a64m) + EPS)
        nl = p * rl
        ext_cv[0:HALO, :] = projh_ref[:, 1536:2048] * projh_ref[:, 2048:2560] * halo_on
        ext_cv[HALO:HALO + ts, :] = u_c * u_v
        q = vl_ref[5:6, :] * ext_cv[pl.ds(6, ts), :]
        for k in range(1, 3):
            q = q + vl_ref[5 + k:6 + k, :] * ext_cv[pl.ds(6 + k, ts), :]
        yc = u_b * q
        rc = lax.rsqrt(_gmean(yc * yc, a64m) + EPS)
        nc = yc * rc

        accl_ref[9:10, :] += _colsum(dyl * nl)
        dnl = dyl * lg
        dp = rl * (dnl - nl * _gmean(dnl * nl, a64m))
        dproj_ref[:, 512:1024] = ((dp * hl) * _gelu_grad(u_ly, th)).astype(BF16)
        a_next = jnp.where(row == ts - 1, acar[0:1, :], pltpu.roll(a, ts - 1, 0))
        acum, gloc = _scan_rev(a_next, dp * ge, row)
        gbuf[...] = gloc + acum * gcar[0:1, :]
        gcar[0:1, :] = gbuf[0:1, :]
        ext_hl[0:HALO, :] = hlh_ref[...] * halo_on
        ext_hl[HALO:HALO + ts, :] = hl
        acar[...] = a[0:HALO, :]
        gt = gbuf[...]
        da = gt * ext_hl[pl.ds(HALO - 1, ts), :]
        dmult = gt * ig * xl
        di = gt * mult * xl
        dxl = gt * mult * ig
        dla = da * a - jnp.where(first_row, 0.0, dmult * a * a / msq)
        accl_ref[8:9, :] += _colsum(dla * ((-C_GATE) * r))
        dra = dla * ((-C_GATE) * sp) * r * (1.0 - r)
        dia = di * ig * (1.0 - ig)
        accd_ref[4:5, 0:D_LRU] += _colsum(dra)
        accd_ref[4:5, D_LRU:2 * D_LRU] += _colsum(dia)
        dgb_ref[:, 0:D_LRU] = dra.astype(BF16)
        dgb_ref[:, D_LRU:2 * D_LRU] = dia.astype(BF16)
        dxl = dxl + _dot_nt(dgb_ref[...], gab_ref[...])
        ggate_acc[...] += _dot_tn(xlb, dgb_ref[...])
        accl_ref[4:5, :] += _colsum(dxl)
        for k in range(4):
            accl_ref[k:k + 1, :] += _colsum(dxl * ext_lx[pl.ds(5 + k, ts), :])
        ext_dxl[0:ts, :] = dxl
        du_lx = vl_ref[0:1, :] * ext_dxl[pl.ds(3, ts), :]
        for k in range(1, 4):
            du_lx = du_lx + vl_ref[k:k + 1, :] * ext_dxl[pl.ds(3 - k, ts), :]
        ext_dxl[ts:ts + HALO, :] = ext_dxl[0:HALO, :]
        dproj_ref[:, 0:512] = du_lx.astype(BF16)

        accl_ref[10:11, :] += _colsum(dyv * nc)
        dnc = dyv * cg
        dyc = rc * (dnc - nc * _gmean(dnc * nc, a64m))
        dproj_ref[:, 1024:1536] = (dyc * q).astype(BF16)
        dq = dyc * u_b
        for k in range(3):
            accl_ref[5 + k:6 + k, :] += _colsum(dq * ext_cv[pl.ds(6 + k, ts), :])
        ext_dq[0:ts, :] = dq
        dcv = vl_ref[5:6, :] * ext_dq[pl.ds(2, ts), :]
        for k in range(1, 3):
            dcv = dcv + vl_ref[5 + k:6 + k, :] * ext_dq[pl.ds(2 - k, ts), :]
        ext_dq[ts:ts + HALO, :] = ext_dq[0:HALO, :]
        dproj_ref[:, 1536:2048] = (dcv * u_v).astype(BF16)
        dproj_ref[:, 2048:2560] = (dcv * u_c).astype(BF16)

        @pl.when(i == 0)
        def _():
            for cp in _gathered_waits(win_own, lambda j: win_ref.at[j], sems.at[pl.ds(0, N_CHIP)]):
                cp.wait()

        dh = _dot_nt(dproj_ref[:, 0:WIN_BLK], win_ref[0])
        for j in range(1, N_CHIP):
            dh = dh + _dot_nt(dproj_ref[:, j * WIN_BLK:(j + 1) * WIN_BLK], win_ref[j])
        for j in range(N_CHIP):
            gwin_acc[j] += _dot_tn(hb_ref[...], dproj_ref[:, j * WIN_BLK:(j + 1) * WIN_BLK])
        xt = x_ref[...]
        r1 = lax.rsqrt(jnp.mean(xt * xt, axis=-1, keepdims=True) + EPS)
        n1 = xt * r1
        accd_ref[1:2, :] += _colsum(dh * (n1 * g1))
        accd_ref[0:1, :] += _colsum(dh)
        dhn1 = dh * (1.0 + scale1)
        accd_ref[3:4, :] += _colsum(dhn1 * n1)
        dn1 = dhn1 * g1
        gx_ref[...] = dx1 + r1 * (dn1 - n1 * jnp.mean(dn1 * n1, axis=-1, keepdims=True))

        @pl.when(i == nt - 1)
        def _():
            outs = [pltpu.make_async_copy(acc, dst, sems.at[k]) for k, (acc, dst) in enumerate(
                ((gwin_acc, gwin_hbm), (gwout_acc, gwout_hbm), (ggate_acc, ggate_hbm)))]
            for cp in outs:
                cp.start()
            for cp in outs:
                cp.wait()

    tile = lambda w: pl.BlockSpec((ts, w), lambda i: (nt - 1 - i, 0))
    halo = lambda w: pl.BlockSpec((HALO, w), lambda i: (jnp.maximum((nt - 1 - i) * hpt - 1, 0), 0))
    ext = pltpu.VMEM((ts + HALO, D_LRU), F32)
    return pl.pallas_call(
        body, name="mix_bwd", grid=(nt,),
        in_specs=[SMEM, tile(D_MODEL), tile(D_MODEL), tile(D_MODEL), tile(D_IN), halo(D_IN), tile(D_LRU), halo(D_LRU),
                  tile(D_MODEL), tile(D_MODEL), _full((8, D_MODEL)), _full((8, D_MODEL)), _full((16, D_LRU)),
                  ANY, ANY, ANY, ANY, _full((D_LRU, 2 * D_LRU), True), _full((D_LRU, D_LRU), True)],
        out_specs=[tile(D_MODEL), _full((8, D_MODEL)), _full((16, D_LRU)), ANY, ANY, ANY],
        out_shape=[jax.ShapeDtypeStruct((s, D_MODEL), F32),
                   jax.ShapeDtypeStruct((8, D_MODEL), F32), jax.ShapeDtypeStruct((16, D_LRU), F32),
                   jax.ShapeDtypeStruct((N_CHIP, D_MODEL, WIN_BLK), F32), jax.ShapeDtypeStruct((D_MODEL, D_MODEL), F32),
                   jax.ShapeDtypeStruct((D_LRU, 2 * D_LRU), F32)],
        scratch_shapes=[pltpu.VMEM((N_CHIP, D_MODEL, WIN_BLK), BF16), pltpu.VMEM((D_MODEL, D_MODEL), BF16),
                        pltpu.VMEM((ts, D_IN), BF16), pltpu.VMEM((ts, 2 * D_LRU), BF16),
                        pltpu.VMEM((N_CHIP, D_MODEL, WIN_BLK), F32), pltpu.VMEM((D_MODEL, D_MODEL), F32),
                        pltpu.VMEM((D_LRU, 2 * D_LRU), F32),
                        ext, ext, ext, ext, ext, pltpu.VMEM((ts, D_LRU), F32),
                        pltpu.VMEM((HALO, D_LRU), F32), pltpu.VMEM((HALO, D_LRU), F32),
                        pltpu.SemaphoreType.DMA((2 * N_CHIP,))],
        compiler_params=pltpu.CompilerParams(dimension_semantics=("arbitrary",), vmem_limit_bytes=VMEM_LIMIT),
    )(chip, dx1, x, mixed, proj, proj, hl, hl, hb, ycat, mod, vecd, vecl, *win, *wout, gab, a64)


def _wgrad_mlp(collective_id, h2b, dz, act, dmo):
    s = h2b.shape[0]
    nstep = 2 * N_CHIP
    half = FF_BLK // 2

    def body(h2_ref, dz_ref, act_ref, dmo_ref, p1_hbm, p2_hbm, buf, landed, summed, send_sems, recv_sems, out_sems):
        j = pl.program_id(0)
        x, y, c, _ = _position()

        def give(jj):
            return pltpu.make_async_remote_copy(
                src_ref=buf.at[jj % 2, pl.ds((1 - c) * half, half), :], dst_ref=landed.at[jj],
                send_sem=send_sems.at[jj % 2], recv_sem=recv_sems.at[jj],
                device_id=(x, y, 1 - c), device_id_type=MESH)

        def write_out(jj, dst):
            return pltpu.make_async_copy(summed.at[jj % 2], dst, out_sems.at[jj % 2])

        def add_sibling(jj):
            give(jj).wait_recv()
            own = buf[jj % 2, pl.ds(pl.multiple_of(c * half, half), half), :]
            summed[jj % 2] = (own.astype(F32) + landed[jj].astype(F32)).astype(BF16)

        @pl.when(j == 0)
        def _():
            pl.semaphore_signal(pltpu.get_barrier_semaphore(), inc=1, device_id=(x, y, 1 - c), device_id_type=MESH)

        @pl.when(j >= 2)
        def _():
            give(j - 2).wait_send()

        @pl.when(j < N_CHIP)
        def _():
            buf[j % 2] = _dot_tn(act_ref[...], dmo_ref[...]).astype(BF16)

        @pl.when(j >= N_CHIP)
        def _():
            buf[j % 2] = _dot_tn(h2_ref[...], dz_ref[...]).astype(BF16)

        @pl.when(j == 0)
        def _():
            pl.semaphore_wait(pltpu.get_barrier_semaphore(), 1)

        give(j).start()

        @pl.when(j >= 1)
        def _():
            jm = j - 1

            @pl.when(jm >= 2)
            def _():
                write_out(jm - 2, p2_hbm.at[0]).wait()

            add_sibling(jm)

            @pl.when(jm < N_CHIP)
            def _():
                write_out(jm, p2_hbm.at[jm]).start()

            @pl.when(jm >= N_CHIP)
            def _():
                write_out(jm, p1_hbm.at[jm - N_CHIP]).start()

        @pl.when(j == nstep - 1)
        def _():
            last = nstep - 1
            write_out(last - 2, p1_hbm.at[0]).wait()
            add_sibling(last)
            write_out(last, p1_hbm.at[N_CHIP - 1]).start()
            for jj in (last - 1, last):
                give(jj).wait_send()
                write_out(jj, p1_hbm.at[0]).wait()

    sds = jax.ShapeDtypeStruct((N_CHIP, half, D_MODEL), BF16)
    whole = pl.BlockSpec((s, D_MODEL), lambda j: (0, 0))
    return pl.pallas_call(
        body, name="wgrad_mlp", grid=(nstep,),
        in_specs=[whole, pl.BlockSpec((s, FF_BLK), lambda j: (0, jnp.maximum(j - N_CHIP, 0))),
                  pl.BlockSpec((s, FF_BLK), lambda j: (0, jnp.minimum(j, N_CHIP - 1))), whole],
        out_specs=[ANY, ANY], out_shape=[sds, sds],
        scratch_shapes=[pltpu.VMEM((2, FF_BLK, D_MODEL), BF16), pltpu.VMEM((nstep, half, D_MODEL), BF16),
                        pltpu.VMEM((2, half, D_MODEL), BF16), pltpu.SemaphoreType.DMA((2,)),
                        pltpu.SemaphoreType.DMA((nstep,)), pltpu.SemaphoreType.DMA((2,))],
        compiler_params=pltpu.CompilerParams(dimension_semantics=("arbitrary",), vmem_limit_bytes=VMEM_LIMIT,
                                             collective_id=collective_id),
    )(h2b, dz, act, dmo)


def _mod_matmul(c_all, ada_w_loc):
    n = ada_w_loc.shape[1]
    cb = 512

    def body(c_ref, w_ref, o_ref):
        c = c_ref[...]
        sc = c * jax.nn.sigmoid(c)
        o_ref[...] = _dot(sc.astype(BF16), w_ref[...].astype(BF16))

    return pl.pallas_call(
        body, name="mod_matmul", grid=(n // cb,),
        in_specs=[_full((8, D_MODEL)), pl.BlockSpec((D_MODEL, cb), lambda j: (0, j))],
        out_specs=pl.BlockSpec((8, cb), lambda j: (0, j)),
        out_shape=jax.ShapeDtypeStruct((8, n), F32),
        compiler_params=pltpu.CompilerParams(dimension_semantics=("arbitrary",), vmem_limit_bytes=VMEM_LIMIT),
    )(c_all, ada_w_loc)


def _adam_math(w, g, m, v):
    m = ADAM_B1 * m + (1.0 - ADAM_B1) * g
    v = ADAM_B2 * v + (1.0 - ADAM_B2) * (g * g)
    m_hat = m / (1.0 - ADAM_B1 ** ADAM_STEP)
    v_hat = v / (1.0 - ADAM_B2 ** ADAM_STEP)
    delta = (-ADAM_LR) * (m_hat / (jnp.sqrt(v_hat) + ADAM_EPS) + ADAM_WD * w)
    return delta, m, v


def _adam(name, core, shards):
    n = len(shards)
    r, c = shards[0][0].shape
    half = r // 2
    rb = min(half, 128)
    nh = half // rb

    def body(core_ref, *refs):
        ins, outs = refs[:5 * n], refs[5 * n:]
        mine = (pl.program_id(0) // nh) == core_ref[0]
        for k in range(n):
            w_ref, go_ref, gs_ref, m_ref, v_ref = ins[5 * k:5 * k + 5]
            g_ref, d_ref, mo_ref, vo_ref = outs[4 * k:4 * k + 4]
            g = jnp.where(mine, go_ref[...], gs_ref[...])
            g_ref[...] = g
            d_ref[...], mo_ref[...], vo_ref[...] = _adam_math(w_ref[...], g, m_ref[...], v_ref[...])

    spec = pl.BlockSpec((rb, c), lambda i, core_ref: (i, 0))
    own = pl.BlockSpec((rb, c), lambda i, core_ref: (jnp.where(i // nh == core_ref[0], i % nh, 0), 0))
    sib = pl.BlockSpec((rb, c), lambda i, core_ref: (jnp.where(i // nh == core_ref[0], 0, i % nh), 0))
    sds = jax.ShapeDtypeStruct((r, c), F32)
    res = pl.pallas_call(
        body, name=name,
        grid_spec=pltpu.PrefetchScalarGridSpec(
            num_scalar_prefetch=1, grid=(r // rb,),
            in_specs=[spec, own, sib, spec, spec] * n, out_specs=[spec] * (4 * n)),
        out_shape=[sds] * (4 * n),
        compiler_params=pltpu.CompilerParams(dimension_semantics=("arbitrary",), vmem_limit_bytes=VMEM_LIMIT),
    )(core, *[t for s in shards for t in s])
    return [res[4 * k:4 * k + 4] for k in range(n)]


def _ada_grad_adam(chip, sct, dmod_cols, w, m, v):
    r, c = w.shape
    rb = 256

    def body(chip_ref, s_ref, dm_ref, w_ref, m_ref, v_ref, g_ref, d_ref, mo_ref, vo_ref):
        g = s_ref[:, 0:1] * dm_ref[0:1, :]
        for b in range(1, 8):
            g = g + s_ref[:, b:b + 1] * dm_ref[b:b + 1, :]
        g_ref[...] = g
        d_ref[...], mo_ref[...], vo_ref[...] = _adam_math(w_ref[...], g, m_ref[...], v_ref[...])

    spec = pl.BlockSpec((rb, c), lambda i, chip_ref: (i, 0))
    sds = jax.ShapeDtypeStruct((r, c), F32)
    return pl.pallas_call(
        body, name="ada_grad_adam",
        grid_spec=pltpu.PrefetchScalarGridSpec(
            num_scalar_prefetch=1, grid=(r // rb,),
            in_specs=[pl.BlockSpec((rb, 8), lambda i, chip_ref: (i, 0)),
                      pl.BlockSpec((8, c), lambda i, chip_ref: (0, chip_ref[0])), spec, spec, spec],
            out_specs=[spec] * 4),
        out_shape=[sds] * 4,
        compiler_params=pltpu.CompilerParams(dimension_semantics=("arbitrary",), vmem_limit_bytes=VMEM_LIMIT),
    )(chip, sct, dmod_cols, w, m, v)


def _position():
    x, y, c = lax.axis_index("x"), lax.axis_index("y"), lax.axis_index("c")
    chips = [(1 - x, y), (x, 1 - y), (1 - x, 1 - y)]
    return x, y, c, chips


def _ag8_run(ins, outs, send_sems, recv_sems, local_sems):
    na = len(ins)
    x, y, c, chips = _position()
    me, sibling = (x, y, c), (x, y, 1 - c)
    first, passed, local = [], [], []
    for a in range(na):
        m_per = ins[a].shape[0]

        def rows(px, py, pc, a=a, m_per=m_per):
            return outs[a].at[pl.ds((4 * px + 2 * py + pc) * m_per, m_per), :]

        def copy(k, block, to, src=None, a=a, rows=rows):
            return pltpu.make_async_remote_copy(
                src_ref=rows(*block) if src is None else src, dst_ref=rows(*block),
                send_sem=send_sems.at[7 * a + k], recv_sem=recv_sems.at[7 * a + k],
                device_id=to, device_id_type=MESH)

        mine = pltpu.make_async_copy(ins[a], rows(*me), local_sems.at[a])
        mine.start()
        local.append(mine)
        f = [copy(0, me, sibling, src=ins[a])]
        f += [copy(1 + j, me, (*chip, c), src=ins[a]) for j, chip in enumerate(chips)]
        for cp in f:
            cp.start()
        first.append((f, copy))
    for a in range(na):
        f, copy = first[a]
        p = [copy(4 + j, (*chip, c), sibling) for j, chip in enumerate(chips)]
        for j, chip in enumerate(chips):
            copy(1 + j, (*chip, c), me).wait_recv()
            p[j].start()
        passed.append(p)
    for a in range(na):
        f, copy = first[a]
        copy(0, sibling, me).wait_recv()
        for j, chip in enumerate(chips):
            copy(4 + j, (*chip, 1 - c), me).wait_recv()
        for cp in f + passed[a]:
            cp.wait_send()
        local[a].wait()


def _allgather8_seq(name, collective_id, arrs):
    na = len(arrs)
    hbm = pltpu.MemorySpace.HBM
    ins = [jax.new_ref(a, memory_space=hbm) for a in arrs]
    outs = [jax.empty_ref(jax.ShapeDtypeStruct((8 * a.shape[0], a.shape[1]), a.dtype), memory_space=hbm) for a in arrs]

    @pl.kernel(mesh=plsc.ScalarSubcoreMesh(axis_name="sequencer", num_cores=1), name=name,
               scratch_types=(pltpu.SemaphoreType.DMA((7 * na,)), pltpu.SemaphoreType.DMA((7 * na,)),
                              pltpu.SemaphoreType.DMA((na,))),
               compiler_params=pltpu.CompilerParams(collective_id=collective_id))
    def launch(send_sems, recv_sems, local_sems):
        x, y, c, chips = _position()
        peers = [(x, y, 1 - c)] + [(*chip, c) for chip in chips]
        barrier = pltpu.get_barrier_semaphore()
        for peer in peers:
            pl.semaphore_signal(barrier, inc=1, device_id=peer, device_id_type=MESH)
        pl.semaphore_wait(barrier, len(peers))
        _ag8_run(ins, outs, send_sems, recv_sems, local_sems)

    launch()
    return [o[...] for o in outs]


def _allgather8(name, arrs):
    na = len(arrs)

    def body(*refs):
        _ag8_run(refs[:na], refs[na:2 * na], *refs[2 * na:])

    return pl.pallas_call(
        body, name=name,
        out_shape=[jax.ShapeDtypeStruct((8 * a.shape[0], a.shape[1]), a.dtype) for a in arrs],
        in_specs=[VMEM] * na, out_specs=[VMEM] * na,
        scratch_shapes=[pltpu.SemaphoreType.DMA((7 * na,)), pltpu.SemaphoreType.DMA((7 * na,)),
                        pltpu.SemaphoreType.DMA((na,))],
        compiler_params=pltpu.CompilerParams(vmem_limit_bytes=VMEM_LIMIT),
    )(*arrs)


AG_SEMS = 7


def _ag_copies(ins, outs, send_sems, recv_sems):
    x, y, c, chips = _position()
    sibling = (x, y, 1 - c)
    xn, yn, dg = [2 * chip[0] + chip[1] for chip in chips]
    to_x, to_y = (1 - x, y, c), (x, 1 - y, c)
    res = []
    for a in range(len(ins)):
        half = ins[a].shape[0] // 2
        quarter = half // 2

        def copy(k, dst, to, src=None, a=a):
            return pltpu.make_async_remote_copy(
                src_ref=dst if src is None else src, dst_ref=dst,
                send_sem=send_sems.at[AG_SEMS * a + k], recv_sem=recv_sems.at[AG_SEMS * a + k],
                device_id=to, device_id_type=MESH)

        def rows(chip, pc, q=None, a=a, half=half, quarter=quarter):
            if q is None:
                return outs[a].at[chip, pl.ds(pc * half, half), :]
            return outs[a].at[chip, pl.ds(pc * half + q * quarter, quarter), :]

        own = ins[a].at[pl.ds(c * half, half), :]
        mine = rows(2 * x + y, c)
        res.append(dict(
            sends=[copy(0, mine, to_x, src=own), copy(1, mine, to_y, src=own)],
            from_x=copy(0, rows(xn, c), to_x), from_y=copy(1, rows(yn, c), to_y),
            relay_y=copy(2, rows(xn, c, 0), to_y), relay_x=copy(3, rows(yn, c, 1), to_x),
            from_y_relay=copy(2, rows(dg, c, 0), to_y), from_x_relay=copy(3, rows(dg, c, 1), to_x),
            pass_on=[copy(4, rows(xn, c), sibling), copy(5, rows(yn, c), sibling), copy(6, rows(dg, c), sibling)],
            from_sibling=[copy(4, rows(xn, 1 - c), sibling), copy(5, rows(yn, 1 - c), sibling),
                          copy(6, rows(dg, 1 - c), sibling)]))
    return res


def _ag_start(ins, outs, send_sems, recv_sems):
    for cps in _ag_copies(ins, outs, send_sems, recv_sems):
        for cp in cps["sends"]:
            cp.start()


def _ag_relay(ins, outs, send_sems, recv_sems, which):
    copies = _ag_copies(ins, outs, send_sems, recv_sems)
    for a in which:
        cps = copies[a]
        cps["from_x"].wait_recv()
        cps["relay_y"].start()
        cps["pass_on"][0].start()
        cps["from_y"].wait_recv()
        cps["relay_x"].start()
        cps["pass_on"][1].start()


def _ag_complete(ins, outs, send_sems, recv_sems):
    copies = _ag_copies(ins, outs, send_sems, recv_sems)
    for cps in copies:
        cps["from_y_relay"].wait_recv()
        cps["from_x_relay"].wait_recv()
        cps["pass_on"][2].start()
    for cps in copies:
        for cp in cps["from_sibling"]:
            cp.wait_recv()
        for cp in cps["sends"] + [cps["relay_y"], cps["relay_x"]] + cps["pass_on"]:
            cp.wait_send()


def _ag_finish(ins, outs, send_sems, recv_sems):
    _ag_relay(ins, outs, send_sems, recv_sems, range(len(ins)))
    _ag_complete(ins, outs, send_sems, recv_sems)


def _allgather_weights(name, collective_id, shards):
    na = len(shards)
    hbm = pltpu.MemorySpace.HBM
    ins = [jax.new_ref(s, memory_space=hbm) for s in shards]
    outs = [jax.empty_ref(jax.ShapeDtypeStruct((N_CHIP,) + s.shape, s.dtype), memory_space=hbm) for s in shards]

    @pl.kernel(mesh=plsc.ScalarSubcoreMesh(axis_name="sequencer", num_cores=1), name=name,
               scratch_types=(pltpu.SemaphoreType.DMA((AG_SEMS * na,)), pltpu.SemaphoreType.DMA((AG_SEMS * na,))),
               compiler_params=pltpu.CompilerParams(collective_id=collective_id))
    def launch(send_sems, recv_sems):
        x, y, c, _ = _position()
        peers = [(1 - x, y, c), (x, 1 - y, c), (x, y, 1 - c)]
        barrier = pltpu.get_barrier_semaphore()
        for peer in peers:
            pl.semaphore_signal(barrier, inc=1, device_id=peer, device_id_type=MESH)
        pl.semaphore_wait(barrier, len(peers))
        _ag_start(ins, outs, send_sems, recv_sems)
        _ag_finish(ins, outs, send_sems, recv_sems)

    launch()
    return [o[...] for o in outs]


def _sibling_swap(name, arrs, split_rows, collective_id=None):
    na = len(arrs)
    shapes = [jax.ShapeDtypeStruct((a.shape[0], a.shape[1] // 2, a.shape[2]) if split_rows else a.shape, a.dtype)
              for a in arrs]

    def run(ins, outs, send_sems, recv_sems):
        x, y, c, _ = _position()
        cps = []
        for a in range(na):
            src = ins[a]
            if split_rows:
                half = src.shape[1] // 2
                src = src.at[:, pl.ds((1 - c) * half, half), :]
            cp = pltpu.make_async_remote_copy(
                src_ref=src, dst_ref=outs[a], send_sem=send_sems.at[a], recv_sem=recv_sems.at[a],
                device_id=(x, y, 1 - c), device_id_type=MESH)
            cp.start()
            cps.append(cp)
        for cp in cps:
            cp.wait()

    sems = (pltpu.SemaphoreType.DMA((na,)), pltpu.SemaphoreType.DMA((na,)))
    if collective_id is None:
        return pl.pallas_call(
            lambda *refs: run(refs[:na], refs[na:2 * na], *refs[2 * na:]), name=name, out_shape=shapes,
            in_specs=[ANY] * na, out_specs=[ANY] * na, scratch_shapes=list(sems))(*arrs)

    hbm = pltpu.MemorySpace.HBM
    ins = [jax.new_ref(a, memory_space=hbm) for a in arrs]
    outs = [jax.empty_ref(s, memory_space=hbm) for s in shapes]

    @pl.kernel(mesh=plsc.ScalarSubcoreMesh(axis_name="sequencer", num_cores=1), name=name, scratch_types=sems,
               compiler_params=pltpu.CompilerParams(collective_id=collective_id))
    def launch(send_sems, recv_sems):
        x, y, c, _ = _position()
        barrier = pltpu.get_barrier_semaphore()
        pl.semaphore_signal(barrier, inc=1, device_id=(x, y, 1 - c), device_id_type=MESH)
        pl.semaphore_wait(barrier, 1)
        run(ins, outs, send_sems, recv_sems)

    launch()
    return [o[...] for o in outs]


def _xchg_copies(ins, outs, send_sems, recv_sems):
    x, y, c, chips = _position()
    return [pltpu.make_async_remote_copy(
        src_ref=ins[a].at[2 * chip[0] + chip[1]], dst_ref=outs[a].at[j],
        send_sem=send_sems.at[3 * a + j], recv_sem=recv_sems.at[3 * a + j],
        device_id=(*chip, c), device_id_type=MESH) for a in range(len(ins)) for j, chip in enumerate(chips)]


def _exchange_chips(name, collective_id, parts):
    na = len(parts)
    hbm = pltpu.MemorySpace.HBM
    ins = [jax.new_ref(p, memory_space=hbm) for p in parts]
    outs = [jax.empty_ref(jax.ShapeDtypeStruct((3,) + p.shape[1:], p.dtype), memory_space=hbm) for p in parts]

    @pl.kernel(mesh=plsc.ScalarSubcoreMesh(axis_name="sequencer", num_cores=1), name=name,
               scratch_types=(pltpu.SemaphoreType.DMA((3 * na,)), pltpu.SemaphoreType.DMA((3 * na,))),
               compiler_params=pltpu.CompilerParams(collective_id=collective_id))
    def launch(send_sems, recv_sems):
        x, y, c, chips = _position()
        barrier = pltpu.get_barrier_semaphore()
        for chip in chips:
            pl.semaphore_signal(barrier, inc=1, device_id=(*chip, c), device_id_type=MESH)
        pl.semaphore_wait(barrier, len(chips))
        for cp in _xchg_copies(ins, outs, send_sems, recv_sems):
            cp.start()
        for cp in _xchg_copies(ins, outs, send_sems, recv_sems):
            cp.wait()

    launch()
    return [q[...] for q in outs]


def _add_sibling(name, grad, recv, core, after=()):
    _, r, c = grad.shape
    half = r // 2
    rb = half
    nrb = half // rb

    def body(core_ref, g_ref, r_ref, *refs):
        refs[-1][...] = (g_ref[...].astype(F32) + r_ref[...].astype(F32)).astype(BF16)

    return pl.pallas_call(
        body, name=name,
        grid_spec=pltpu.PrefetchScalarGridSpec(
            num_scalar_prefetch=1, grid=(N_CHIP, nrb),
            in_specs=[pl.BlockSpec((1, rb, c), lambda j, i, core_ref: (j, core_ref[0] * nrb + i, 0)),
                      pl.BlockSpec((1, rb, c), lambda j, i, core_ref: (j, i, 0))] + [ANY] * len(after),
            out_specs=pl.BlockSpec((1, rb, c), lambda j, i, core_ref: (j, i, 0))),
        out_shape=jax.ShapeDtypeStruct((N_CHIP, half, c), BF16),
        compiler_params=pltpu.CompilerParams(dimension_semantics=("arbitrary", "arbitrary"),
                                             vmem_limit_bytes=VMEM_LIMIT),
    )(core, grad, recv, *after)


def _add_chips(name, chip, p, q, after=()):
    _, half, c = q.shape
    rb = min(half, 256)

    def body(chip_ref, p_ref, q_ref, *refs):
        acc = p_ref[0].astype(F32)
        for j in range(3):
            acc = acc + q_ref[j].astype(F32)
        refs[-1][...] = acc

    return pl.pallas_call(
        body, name=name,
        grid_spec=pltpu.PrefetchScalarGridSpec(
            num_scalar_prefetch=1, grid=(half // rb,),
            in_specs=[pl.BlockSpec((1, rb, c), lambda i, chip_ref: (chip_ref[0], i, 0)),
                      pl.BlockSpec((3, rb, c), lambda i, chip_ref: (0, i, 0))] + [ANY] * len(after),
            out_specs=pl.BlockSpec((rb, c), lambda i, chip_ref: (i, 0))),
        out_shape=jax.ShapeDtypeStruct((half, c), F32),
        compiler_params=pltpu.CompilerParams(dimension_semantics=("arbitrary",), vmem_limit_bytes=VMEM_LIMIT),
    )(chip, p, q, *after)


def _small_update(gad, gam, gl, gg, mychip, params):
    names = ["ada_b", "norm1_g", "lru_conv_b", "gate_a_w", "gate_a_b", "gate_x_w", "gate_x_b", "a_param",
             "lru_conv_w", "short_conv_w", "lru_out_g", "conv_out_g", "norm2_g", "final_g"]
    flat = [t for n in names for t in params[n]]
    nin = len(flat)

    def body(chip_ref, gad_ref, gam_ref, gl_ref, gg_ref, *refs):
        ins = {n: refs[3 * k:3 * k + 3] for k, n in enumerate(names)}
        outs = {n: refs[nin + 4 * k:nin + 4 * k + 4] for k, n in enumerate(names)}
        loss_ref, dmod_ref = refs[nin + 4 * len(names):nin + 4 * len(names) + 2]

        def dsum(ref, lo, n):
            per = ref.shape[0] // 8
            acc = ref[lo:lo + n, :].astype(F32)
            for dev in range(1, 8):
                acc = acc + ref[dev * per + lo:dev * per + lo + n, :].astype(F32)
            return acc

        def update(n, g):
            w_ref, m_ref, v_ref = ins[n]
            g_ref, d_ref, mo_ref, vo_ref = outs[n]
            g = g.reshape(w_ref.shape)
            g_ref[...] = g
            d_ref[...], mo_ref[...], vo_ref[...] = _adam_math(w_ref[...], g, m_ref[...], v_ref[...])

        d, dm, l, lw = refs[-4:]
        d[...] = dsum(gad_ref, 0, 8)
        dm[...] = dsum(gam_ref, 0, 8)
        l[...] = dsum(gl_ref, 0, 16)
        for dev in range(8):
            for k in range(3):
                dmod_ref[dev:dev + 1, k * D_MODEL:(k + 1) * D_MODEL] = gad_ref[dev * 8 + k:dev * 8 + k + 1, :]
                dmod_ref[dev:dev + 1, (3 + k) * D_MODEL:(4 + k) * D_MODEL] = gam_ref[dev * 8 + k:dev * 8 + k + 1, :]
        w_ref, m_ref, v_ref = ins["ada_b"]
        g_ref, d_ref, mo_ref, vo_ref = outs["ada_b"]
        for k in range(3):
            g_ref[:, k * D_MODEL:(k + 1) * D_MODEL] = d[k:k + 1, :]
            g_ref[:, (3 + k) * D_MODEL:(4 + k) * D_MODEL] = dm[k:k + 1, :]
        d_ref[...], mo_ref[...], vo_ref[...] = _adam_math(w_ref[...], g_ref[...], m_ref[...], v_ref[...])
        update("norm1_g", d[3:4, :])
        update("norm2_g", dm[3:4, :])
        update("final_g", dm[4:5, :])
        update("gate_a_b", d[4:5, 0:D_LRU])
        update("gate_x_b", d[4:5, D_LRU:2 * D_LRU])
        update("lru_conv_b", l[4:5, :])
        update("a_param", l[8:9, :] * jax.nn.sigmoid(ins["a_param"][0][...]))
        update("lru_out_g", l[9:10, :])
        update("conv_out_g", l[10:11, :])
        loss_ref[...] = jnp.broadcast_to(dm[5:6, 0:128], (8, 128))
        chip = chip_ref[0]
        acc = jnp.zeros((8, 128), F32)
        for j in range(N_CHIP):
            acc = acc + jnp.where(chip == j, l[0:8, j * 128:(j + 1) * 128], 0.0)
        lw[...] = acc
        update("lru_conv_w", lw[0:4, :])
        update("short_conv_w", lw[5:8, :])
        gates = dsum(gg_ref, 0, D_LRU)
        update("gate_a_w", gates[:, 0:HEAD])
        update("gate_x_w", gates[:, HEAD:2 * HEAD])

    out_shape = []
    for n in names:
        out_shape += [jax.ShapeDtypeStruct(params[n][0].shape, F32)] * 4
    out_shape += [jax.ShapeDtypeStruct((8, 128), F32), jax.ShapeDtypeStruct((8, 6 * D_MODEL), F32)]
    res = pl.pallas_call(
        body, name="small_update", out_shape=out_shape,
        in_specs=[SMEM] + [VMEM] * (4 + nin),
        out_specs=[VMEM] * len(out_shape),
        scratch_shapes=[pltpu.VMEM((8, D_MODEL), F32), pltpu.VMEM((8, D_MODEL), F32), pltpu.VMEM((16, D_LRU), F32),
                        pltpu.VMEM((8, 128), F32)],
        compiler_params=pltpu.CompilerParams(vmem_limit_bytes=VMEM_LIMIT),
    )(mychip, gad, gam, gl, gg, *flat)
    per = {n: res[4 * k:4 * k + 4] for k, n in enumerate(names)}
    return per, res[-2], res[-1]


def _block_diag(w):
    eye = jnp.eye(8, dtype=w.dtype)
    return (eye[:, None, :, None] * w[:, :, None, :]).reshape(8 * HEAD, 8 * HEAD)


def _diag_blocks(g):
    return jnp.concatenate([g[h * HEAD:(h + 1) * HEAD, h * HEAD:(h + 1) * HEAD] for h in range(8)], axis=0)


def kernel(x, c, ada_w, ada_b, norm1_g, w_in, lru_conv_w, lru_conv_b, gate_a_w, gate_a_b, gate_x_w, gate_x_b, a_param, short_conv_w, lru_out_g, conv_out_g, w_out, norm2_g, w_mlp1, w_mlp2, final_g, loss_target, m_ada_w, m_ada_b, m_norm1_g, m_w_in, m_lru_conv_w, m_lru_conv_b, m_gate_a_w, m_gate_a_b, m_gate_x_w, m_gate_x_b, m_a_param, m_short_conv_w, m_lru_out_g, m_conv_out_g, m_w_out, m_norm2_g, m_w_mlp1, m_w_mlp2, m_final_g, v_ada_w, v_ada_b, v_norm1_g, v_w_in, v_lru_conv_w, v_lru_conv_b, v_gate_a_w, v_gate_a_b, v_gate_x_w, v_gate_x_b, v_a_param, v_short_conv_w, v_lru_out_g, v_conv_out_g, v_w_out, v_norm2_g, v_w_mlp1, v_w_mlp2, v_final_g):
    xi, yi, ci = lax.axis_index("x"), lax.axis_index("y"), lax.axis_index("c")
    mychip = 2 * xi + yi
    me = 4 * xi + 2 * yi + ci

    own_in, own_out = w_in[0].astype(BF16), w_out[0].astype(BF16)
    win_all, wout_all = _allgather_weights("allgather_mixer_weights", 1, [own_in, own_out])
    own_w1, own_w2 = w_mlp1[0].astype(BF16), w_mlp2[0].astype(BF16)
    w1_all, w2_all = _allgather_weights("allgather_mlp_weights", 2, [own_w1, own_w2])

    c_blk = jnp.zeros((8, D_MODEL), F32).at[0:1].set(c)
    cw_blk = jnp.zeros((8, 128), F32).at[0:4].set(lru_conv_w[0]).at[4:7].set(short_conv_w[0])
    c_g, cw_g = _allgather8("allgather_cond", [c_blk, cw_blk])
    c_all = c_g.reshape(8, 8, D_MODEL)[:, 0]
    cw_g = cw_g.reshape(4, 2, 8, 128)[:, 0]
    lcw = cw_g[:, 0:4].transpose(1, 0, 2).reshape(4, D_LRU)
    scw = cw_g[:, 4:7].transpose(1, 0, 2).reshape(3, D_LRU)

    mod_loc = _mod_matmul(c_all, ada_w[0])
    (mod_g,) = _allgather8("allgather_mod", [mod_loc])
    mod_all = mod_g.reshape(4, 2, 8, 6 * D_MODEL // 4)[:, 0].transpose(1, 0, 2).reshape(8, 6 * D_MODEL) + ada_b
    mod_pad = jnp.pad(mod_all.reshape(8, 6, D_MODEL), ((0, 0), (0, 2), (0, 0)))
    mod = lax.dynamic_slice_in_dim(mod_pad, me, 1, axis=0).reshape(8, D_MODEL)

    win, wout = (win_all, own_in), (wout_all, own_out)
    chip = mychip.reshape(1).astype(jnp.int32)
    core = ci.reshape(1).astype(jnp.int32)

    vecd = jnp.concatenate([norm1_g, norm2_g, final_g[None, :], jnp.concatenate([gate_a_b, gate_x_b], axis=1),
                            jnp.zeros((4, D_MODEL), F32)], axis=0)
    vecl = jnp.concatenate([lcw, lru_conv_b, scw, a_param, lru_out_g, conv_out_g, jnp.zeros((5, D_LRU), F32)], axis=0)
    gab = jnp.concatenate([_block_diag(gate_a_w[0]), _block_diag(gate_x_w[0])], axis=1).astype(BF16)
    a64 = _block_diag(jnp.full((8, HEAD, HEAD), 1.0 / HEAD, F32)).astype(BF16)

    hb, proj, hl, ycat, mixed, x1 = _mix_fwd(chip, x[0], mod, vecd, vecl, win, wout, gab, a64)
    dx1, act, dz, dmo, h2b, accm = _mlp_fwd_bwd(
        chip, x1, loss_target[0], mod, vecd, (w1_all, own_w1), (w2_all, own_w2))

    parts_mlp = list(_wgrad_mlp(6, h2b, dz, act, dmo))
    q_w1, q_w2 = _exchange_chips("rs_exchange_mlp", 0, parts_mlp)
    grad_x, accd, accl, g_win, g_wout, g_gate = _mix_bwd(
        chip, dx1, x[0], mixed, proj, hl, hb, ycat, mod, vecd, vecl, win, wout, gab, a64)

    g_mix = [g_win, g_wout.reshape(N_CHIP, WOUT_BLK, D_MODEL)]
    recv_mix = _sibling_swap("rs_swap_halves_mix", g_mix, True, collective_id=4)
    own_mlp = [_add_chips("rs_add_chips_mlp%d" % k, chip, p, q) for k, (p, q) in enumerate(zip(parts_mlp, (q_w1, q_w2)))]
    sib_mlp = _sibling_swap("rs_swap_reduced_mlp", own_mlp, False, collective_id=5)
    gg_blk = jnp.concatenate([_diag_blocks(g_gate[:, 0:D_LRU]), _diag_blocks(g_gate[:, D_LRU:2 * D_LRU])], axis=1)
    gad, gam, gl, gg = _allgather8_seq("allgather_small_grads", 8, [accd, accm, accl, gg_blk.astype(BF16)])

    parts_mix = [_add_sibling("rs_add_sibling_mix%d" % k, g, r, core)
                 for k, (g, r) in enumerate(zip(g_mix, recv_mix))]
    landed_mix = _exchange_chips("rs_exchange_mix", 3, parts_mix)
    res_w1, res_w2 = _adam("adam_mlp", core, [(w_mlp1[0], own_mlp[0], sib_mlp[0], m_w_mlp1[0], v_w_mlp1[0]),
                                              (w_mlp2[0], own_mlp[1], sib_mlp[1], m_w_mlp2[0], v_w_mlp2[0])])
    own_mix = [_add_chips("rs_add_chips_mix%d" % k, chip, p, q, after=[res_w1[1]])
               for k, (p, q) in enumerate(zip(parts_mix, landed_mix))]
    sib_mix = _sibling_swap("rs_swap_reduced_mix", own_mix, False)
    (res_win,) = _adam("adam_w_in", core, [(w_in[0], own_mix[0], sib_mix[0], m_w_in[0], v_w_in[0])])
    (res_wout,) = _adam("adam_w_out", core, [(w_out[0], own_mix[1], sib_mix[1], m_w_out[0], v_w_out[0])])

    params = {
        "ada_b": (ada_b, m_ada_b, v_ada_b), "norm1_g": (norm1_g, m_norm1_g, v_norm1_g),
        "lru_conv_b": (lru_conv_b, m_lru_conv_b, v_lru_conv_b),
        "gate_a_w": tuple(t.reshape(D_LRU, HEAD) for t in (gate_a_w, m_gate_a_w, v_gate_a_w)),
        "gate_a_b": (gate_a_b, m_gate_a_b, v_gate_a_b),
        "gate_x_w": tuple(t.reshape(D_LRU, HEAD) for t in (gate_x_w, m_gate_x_w, v_gate_x_w)),
        "gate_x_b": (gate_x_b, m_gate_x_b, v_gate_x_b), "a_param": (a_param, m_a_param, v_a_param),
        "lru_conv_w": (lru_conv_w, m_lru_conv_w, v_lru_conv_w),
        "short_conv_w": (short_conv_w, m_short_conv_w, v_short_conv_w),
        "lru_out_g": (lru_out_g, m_lru_out_g, v_lru_out_g), "conv_out_g": (conv_out_g, m_conv_out_g, v_conv_out_g),
        "norm2_g": (norm2_g, m_norm2_g, v_norm2_g),
        "final_g": tuple(t[None, :] for t in (final_g, m_final_g, v_final_g)),
    }
    small, loss_blk, dmod_cols = _small_update(gad, gam, gl, gg, chip, params)
    loss = loss_blk[0, 0]

    sct = (c_all * jax.nn.sigmoid(c_all)).T
    ada = _ada_grad_adam(chip, sct, dmod_cols, ada_w[0], m_ada_w[0], v_ada_w[0])

    res = {"ada_w": ada, "w_in": res_win, "w_out": res_wout, "w_mlp1": res_w1, "w_mlp2": res_w2}
    res = {n: tuple(t[None] for t in r) for n, r in res.items()}
    shapes = {"gate_a_w": gate_a_w.shape, "gate_x_w": gate_x_w.shape, "lru_conv_w": lru_conv_w.shape,
              "short_conv_w": short_conv_w.shape, "final_g": final_g.shape}
    for n, t in small.items():
        res[n] = tuple(u.reshape(shapes[n]) if n in shapes else u for u in t)

    order = ["ada_w", "ada_b", "norm1_g", "w_in", "lru_conv_w", "lru_conv_b", "gate_a_w", "gate_a_b", "gate_x_w",
             "gate_x_b", "a_param", "short_conv_w", "lru_out_g", "conv_out_g", "w_out", "norm2_g", "w_mlp1",
             "w_mlp2", "final_g"]
    return (loss, grad_x[None], *[res[n][0] for n in order], *[res[n][1] for n in order],
            *[res[n][2] for n in order], *[res[n][3] for n in order])
```

```python
import jax
import jax.numpy as jnp
from jax import lax
from jax.experimental import pallas as pl
from jax.experimental.pallas import tpu as pltpu
from jax.experimental.pallas import tpu_sc as plsc

F32 = jnp.float32
BF16 = jnp.bfloat16

D_MODEL = 1024
D_LRU = 512
D_IN = 2560
D_FF = 4096
N_CHIP = 4
WIN_BLK = D_IN // N_CHIP
WOUT_BLK = D_MODEL // N_CHIP
FF_BLK = D_FF // N_CHIP
HEAD = 64
EPS = 1e-6
C_GATE = 8.0
TOKEN_TILE = 256
MIX_FWD_TILE = 512
HALO = 8
VMEM_LIMIT = 60 * 1024 * 1024

ADAM_LR = 0.001
ADAM_B1 = 0.9
ADAM_B2 = 0.999
ADAM_EPS = 1e-08
ADAM_WD = 0.01
ADAM_STEP = 10

MESH = pl.DeviceIdType.MESH
ANY = pl.BlockSpec(memory_space=pl.ANY)
VMEM = pl.BlockSpec(memory_space=pltpu.VMEM)
SMEM = pl.BlockSpec(memory_space=pltpu.SMEM)


def _full(shape, single=False):
    nd = len(shape)
    if single:
        return pl.BlockSpec(shape, lambda *_: (0,) * nd, pipeline_mode=pl.Buffered(1))
    return pl.BlockSpec(shape, lambda *_: (0,) * nd)


def _dot(a, b):
    return jnp.dot(a, b, preferred_element_type=F32)


def _dot_nt(a, b):
    return lax.dot_general(a, b, (((1,), (1,)), ((), ())), preferred_element_type=F32)


def _dot_tn(a, b):
    return lax.dot_general(a, b, (((0,), (0,)), ((), ())), preferred_element_type=F32)


def _gmean(v, a64):
    hi = v.astype(BF16)
    lo = (v - hi.astype(F32)).astype(BF16)
    return _dot(hi, a64) + _dot(lo, a64)


def _gelu(x):
    u = 0.7978845608028654 * (x + 0.044715 * x * x * x)
    t = jnp.tanh(u)
    return 0.5 * x * (1.0 + t), t


def _gelu_grad(x, t):
    du = 0.7978845608028654 * (1.0 + 3.0 * 0.044715 * x * x)
    return 0.5 * (1.0 + t) + 0.5 * x * (1.0 - t * t) * du


def _log1p_pos(y):
    return jnp.where(y < 1e-2, y * (1.0 - y * (0.5 - y * (1.0 / 3.0 - y * 0.25))), jnp.log(1.0 + y))


def _softplus(a):
    return jnp.maximum(a, 0.0) + _log1p_pos(jnp.exp(-jnp.abs(a)))


def _neg_expm1(z):
    series = -z * (1.0 + z * (0.5 + z * (1.0 / 6.0 + z * (1.0 / 24.0))))
    return jnp.where(z > -0.02, series, 1.0 - jnp.exp(z))


def _scan_fwd(a, b, row):
    n = a.shape[0]
    d = 1
    while d < n:
        m = row >= d
        b = jnp.where(m, a * pltpu.roll(b, d, 0) + b, b)
        a = jnp.where(m, a * pltpu.roll(a, d, 0), a)
        d *= 2
    return a, b


def _scan_rev(a, b, row):
    n = a.shape[0]
    d = 1
    while d < n:
        m = row < n - d
        b = jnp.where(m, b + a * pltpu.roll(b, n - d, 0), b)
        a = jnp.where(m, a * pltpu.roll(a, n - d, 0), a)
        d *= 2
    return a, b


def _colsum(v):
    return jnp.sum(v, axis=0, keepdims=True)


def _load_gathered(chip, gathered, own, slot, sems):
    copies = []
    for j in range(N_CHIP):
        @pl.when(chip == j)
        def _(j=j):
            pltpu.make_async_copy(own, slot(j), sems.at[j]).start()

        @pl.when(chip != j)
        def _(j=j):
            pltpu.make_async_copy(gathered.at[j], slot(j), sems.at[j]).start()

        copies.append(pltpu.make_async_copy(own, slot(j), sems.at[j]))
    return copies


def _lru_gates(xlb, gab, gbias, sp, first_row):
    g = _dot(xlb, gab) + gbias
    r = jax.nn.sigmoid(g[:, :D_LRU])
    ig = jax.nn.sigmoid(g[:, D_LRU:])
    la = (-C_GATE) * r * sp
    a = jnp.exp(la)
    msq = jnp.sqrt(_neg_expm1(2.0 * la))
    mult = jnp.where(first_row, 1.0, msq)
    return r, ig, a, msq, mult


def _mix_fwd(chip, x, mod, vecd, vecl, win, wout, gab, a64):
    s = x.shape[0]
    ts = MIX_FWD_TILE
    nt = s // ts

    def body(chip_ref, x_ref, mod_ref, vd_ref, vl_ref, win_hbm, win_own, wout_hbm, wout_own, gab_ref, a64_ref,
             hb_ref, proj_ref, hl_ref, ycat_ref, mixed_ref, x1_ref,
             win_ref, wout_ref, ext_lx, ext_cv, hcar, sems):
        i = pl.program_id(0)

        @pl.when(i == 0)
        def _():
            cps = _load_gathered(chip_ref[0], win_hbm, win_own, lambda j: win_ref.at[j], sems.at[pl.ds(0, N_CHIP)])
            cps += _load_gathered(chip_ref[0], wout_hbm, wout_own,
                                  lambda j: wout_ref.at[pl.ds(j * WOUT_BLK, WOUT_BLK), :],
                                  sems.at[pl.ds(N_CHIP, N_CHIP)])
            ext_lx[0:HALO, :] = jnp.zeros((HALO, D_LRU), F32)
            ext_cv[0:HALO, :] = jnp.zeros((HALO, D_LRU), F32)
            hcar[...] = jnp.zeros_like(hcar)
            for cp in cps:
                cp.wait()

        row = lax.broadcasted_iota(jnp.int32, (ts, D_LRU), 0)
        first_row = jnp.logical_and(row == 0, i == 0)
        xt = x_ref[...]
        shift1, scale1, gate1 = mod_ref[0:1, :], mod_ref[1:2, :], mod_ref[2:3, :]
        r1 = lax.rsqrt(jnp.mean(xt * xt, axis=-1, keepdims=True) + EPS)
        h = (xt * r1) * vd_ref[0:1, :] * (1.0 + scale1) + shift1
        hb = h.astype(BF16)
        hb_ref[...] = hb
        for j in range(N_CHIP):
            proj_ref[:, j * WIN_BLK:(j + 1) * WIN_BLK] = _dot(hb, win_ref[j])
        u_ly = proj_ref[:, 512:1024]
        u_b = proj_ref[:, 1024:1536]

        ext_lx[HALO:HALO + ts, :] = proj_ref[:, 0:512]
        xl = vl_ref[4:5, :] + vl_ref[0:1, :] * ext_lx[pl.ds(5, ts), :]
        for k in range(1, 4):
            xl = xl + vl_ref[k:k + 1, :] * ext_lx[pl.ds(5 + k, ts), :]
        ext_lx[0:HALO, :] = ext_lx[ts:ts + HALO, :]
        sp = _softplus(vl_ref[8:9, :])
        _, ig, a, _, mult = _lru_gates(xl.astype(BF16), gab_ref[...], vd_ref[3:4, :], sp, first_row)
        acum, hloc = _scan_fwd(a, mult * (ig * xl), row)
        hl = hloc + acum * hcar[0:1, :]
        hl_ref[...] = hl
        hcar[0:1, :] = hl_ref[ts - 1:ts, :]
        ge, _ = _gelu(u_ly)
        p = ge * hl
        y_lru = p * lax.rsqrt(_gmean(p * p, a64_ref[...]) + EPS) * vl_ref[9:10, :]
        ycat_ref[:, 0:512] = y_lru.astype(BF16)

        ext_cv[HALO:HALO + ts, :] = proj_ref[:, 1536:2048] * proj_ref[:, 2048:2560]
        q = vl_ref[5:6, :] * ext_cv[pl.ds(6, ts), :]
        for k in range(1, 3):
            q = q + vl_ref[5 + k:6 + k, :] * ext_cv[pl.ds(6 + k, ts), :]
        ext_cv[0:HALO, :] = ext_cv[ts:ts + HALO, :]
        yc = u_b * q
        y_conv = yc * lax.rsqrt(_gmean(yc * yc, a64_ref[...]) + EPS) * vl_ref[10:11, :]
        ycat_ref[:, 512:1024] = y_conv.astype(BF16)

        mixed = _dot(ycat_ref[...], wout_ref[...])
        mixed_ref[...] = mixed
        x1_ref[...] = xt + gate1 * mixed

    tile = lambda w: pl.BlockSpec((ts, w), lambda i: (i, 0))
    return pl.pallas_call(
        body, name="mix_fwd", grid=(nt,),
        in_specs=[SMEM, tile(D_MODEL), _full((8, D_MODEL)), _full((8, D_MODEL)), _full((16, D_LRU)),
                  ANY, ANY, ANY, ANY, _full((D_LRU, 2 * D_LRU), True), _full((D_LRU, D_LRU), True)],
        out_specs=[tile(D_MODEL), tile(D_IN), tile(D_LRU), tile(D_MODEL), tile(D_MODEL), tile(D_MODEL)],
        out_shape=[jax.ShapeDtypeStruct((s, D_MODEL), BF16), jax.ShapeDtypeStruct((s, D_IN), F32),
                   jax.ShapeDtypeStruct((s, D_LRU), F32), jax.ShapeDtypeStruct((s, D_MODEL), BF16),
                   jax.ShapeDtypeStruct((s, D_MODEL), F32), jax.ShapeDtypeStruct((s, D_MODEL), F32)],
        scratch_shapes=[pltpu.VMEM((N_CHIP, D_MODEL, WIN_BLK), BF16), pltpu.VMEM((D_MODEL, D_MODEL), BF16),
                        pltpu.VMEM((ts + HALO, D_LRU), F32), pltpu.VMEM((ts + HALO, D_LRU), F32),
                        pltpu.VMEM((HALO, D_LRU), F32), pltpu.SemaphoreType.DMA((2 * N_CHIP,))],
        compiler_params=pltpu.CompilerParams(dimension_semantics=("arbitrary",), vmem_limit_bytes=VMEM_LIMIT),
    )(chip, x, mod, vecd, vecl, *win, *wout, gab, a64)


def _mlp_fwd_bwd(chip, x1, target, mod, vecd, w1, w2):
    s = x1.shape[0]
    ts = TOKEN_TILE
    nt = s // ts

    def body(chip_ref, x1_ref, tg_ref, mod_ref, vd_ref, w1_hbm, w1_own, w2_hbm, w2_own,
             dx1_ref, act_ref, dz_ref, dmo_ref, h2_ref, acc_ref, w1_v, w2_v, rz_v, sems):
        i = pl.program_id(0)

        @pl.when(i == 0)
        def _():
            cps = _load_gathered(chip_ref[0], w1_hbm, w1_own, lambda j: w1_v.at[j], sems.at[pl.ds(0, N_CHIP)])
            cps += _load_gathered(chip_ref[0], w2_hbm, w2_own, lambda j: w2_v.at[j], sems.at[pl.ds(N_CHIP, N_CHIP)])
            acc_ref[...] = jnp.zeros_like(acc_ref)
            for cp in cps:
                cp.wait()

        xt = x1_ref[...]
        shift2, scale2, gate2 = mod_ref[3:4, :], mod_ref[4:5, :], mod_ref[5:6, :]
        g2, gf = vd_ref[1:2, :], vd_ref[2:3, :]
        r2 = lax.rsqrt(jnp.mean(xt * xt, axis=-1, keepdims=True) + EPS)
        n2 = xt * r2
        h2b = (n2 * g2 * (1.0 + scale2) + shift2).astype(BF16)
        h2_ref[...] = h2b
        for j in range(N_CHIP):
            rz_v[j] = jnp.maximum(_dot(h2b, w1_v[j]), 0.0)
        mo = jnp.zeros((ts, D_MODEL), F32)
        for j in range(N_CHIP):
            rz = rz_v[j]
            actb = (rz * rz).astype(BF16)
            act_ref[:, j * FF_BLK:(j + 1) * FF_BLK] = actb
            mo = mo + _dot(actb, w2_v[j])
        x2 = xt + gate2 * mo
        r3 = lax.rsqrt(jnp.mean(x2 * x2, axis=-1, keepdims=True) + EPS)
        n3 = x2 * r3
        e = n3 * gf - tg_ref[...]
        loss = (0.5 / D_MODEL) * jnp.sum(_colsum(e * e), axis=1, keepdims=True)
        dy = e * (1.0 / D_MODEL)
        acc_ref[4:5, :] += _colsum(dy * n3)
        acc_ref[5:6, :] += jnp.broadcast_to(loss, (1, D_MODEL))
        dn3 = dy * gf
        dx2 = r3 * (dn3 - n3 * jnp.mean(dn3 * n3, axis=-1, keepdims=True))
        acc_ref[2:3, :] += _colsum(dx2 * mo)
        dmob = (dx2 * gate2).astype(BF16)
        dmo_ref[...] = dmob
        for j in range(N_CHIP):
            dz_ref[:, j * FF_BLK:(j + 1) * FF_BLK] = (_dot_nt(dmob, w2_v[j]) * (2.0 * rz_v[j])).astype(BF16)
        dh2 = jnp.zeros((ts, D_MODEL), F32)
        for j in range(N_CHIP):
            dh2 = dh2 + _dot_nt(dz_ref[:, j * FF_BLK:(j + 1) * FF_BLK], w1_v[j])
        acc_ref[1:2, :] += _colsum(dh2 * (n2 * g2))
        acc_ref[0:1, :] += _colsum(dh2)
        dhn2 = dh2 * (1.0 + scale2)
        acc_ref[3:4, :] += _colsum(dhn2 * n2)
        dn2 = dhn2 * g2
        dx1_ref[...] = dx2 + r2 * (dn2 - n2 * jnp.mean(dn2 * n2, axis=-1, keepdims=True))

    tile = lambda w: pl.BlockSpec((ts, w), lambda i: (i, 0))
    return pl.pallas_call(
        body, name="mlp_fwd_bwd", grid=(nt,),
        in_specs=[SMEM, tile(D_MODEL), tile(D_MODEL), _full((8, D_MODEL)), _full((8, D_MODEL)), ANY, ANY, ANY, ANY],
        out_specs=[tile(D_MODEL), tile(D_FF), tile(D_FF), tile(D_MODEL), tile(D_MODEL), _full((8, D_MODEL))],
        out_shape=[jax.ShapeDtypeStruct((s, D_MODEL), F32), jax.ShapeDtypeStruct((s, D_FF), BF16),
                   jax.ShapeDtypeStruct((s, D_FF), BF16), jax.ShapeDtypeStruct((s, D_MODEL), BF16),
                   jax.ShapeDtypeStruct((s, D_MODEL), BF16), jax.ShapeDtypeStruct((8, D_MODEL), F32)],
        scratch_shapes=[pltpu.VMEM((N_CHIP, D_MODEL, FF_BLK), BF16), pltpu.VMEM((N_CHIP, FF_BLK, D_MODEL), BF16),
                        pltpu.VMEM((N_CHIP, ts, FF_BLK), F32), pltpu.SemaphoreType.DMA((2 * N_CHIP,))],
        compiler_params=pltpu.CompilerParams(dimension_semantics=("arbitrary",), vmem_limit_bytes=VMEM_LIMIT),
    )(chip, x1, target, mod, vecd, *w1, *w2)


def _mix_bwd(chip, dx1, x, mixed, proj, hl, hb, ycat, mod, vecd, vecl, win, wout, gab, a64):
    s = x.shape[0]
    ts = TOKEN_TILE
    nt = s // ts
    hpt = ts // HALO

    def body(chip_ref, dx1_ref, x_ref, mixed_ref, proj_ref, projh_ref, hl_ref, hlh_ref, hb_ref, ycat_ref,
             mod_ref, vd_ref, vl_ref, win_hbm, win_own, wout_hbm, wout_own, gab_ref, a64_ref,
             gx_ref, accd_ref, accl_ref, gwin_hbm, gwout_hbm, ggate_hbm,
             win_ref, wout_ref, dproj_ref, dgb_ref, gwin_acc, gwout_acc, ggate_acc,
             ext_lx, ext_cv, ext_hl, ext_dxl, ext_dq, gbuf, gcar, acar, sems):
        i = pl.program_id(0)
        ri = nt - 1 - i

        @pl.when(i == 0)
        def _():
            gwin_acc[...] = jnp.zeros_like(gwin_acc)
            gwout_acc[...] = jnp.zeros_like(gwout_acc)
            ggate_acc[...] = jnp.zeros_like(ggate_acc)
            cps = _load_gathered(chip_ref[0], win_hbm, win_own, lambda j: win_ref.at[j], sems.at[pl.ds(0, N_CHIP)])
            cps += _load_gathered(chip_ref[0], wout_hbm, wout_own,
                                  lambda j: wout_ref.at[pl.ds(j * WOUT_BLK, WOUT_BLK), :],
                                  sems.at[pl.ds(N_CHIP, N_CHIP)])
            for cp in cps:
                cp.wait()
            accd_ref[...] = jnp.zeros_like(accd_ref)
            accl_ref[...] = jnp.zeros_like(accl_ref)
            ext_dxl[ts:ts + HALO, :] = jnp.zeros((HALO, D_LRU), F32)
            ext_dq[ts:ts + HALO, :] = jnp.zeros((HALO, D_LRU), F32)
            gcar[...] = jnp.zeros_like(gcar)
            acar[...] = jnp.zeros_like(acar)

        row = lax.broadcasted_iota(jnp.int32, (ts, D_LRU), 0)
        first_row = jnp.logical_and(row == 0, ri == 0)
        halo_on = jnp.where(ri == 0, 0.0, 1.0)
        shift1, scale1, gate1 = mod_ref[0:1, :], mod_ref[1:2, :], mod_ref[2:3, :]
        g1 = vd_ref[0:1, :]
        a64m = a64_ref[...]
        lg, cg = vl_ref[9:10, :], vl_ref[10:11, :]

        dx1 = dx1_ref[...]
        accd_ref[2:3, :] += _colsum(dx1 * mixed_ref[...])
        dmb = (dx1 * gate1).astype(BF16)
        gwout_acc[...] += _dot_tn(ycat_ref[...], dmb)
        dycat = _dot_nt(dmb, wout_ref[...])
        dyl = dycat[:, 0:512]
        dyv = dycat[:, 512:1024]

        u_ly = proj_ref[:, 512:1024]
        u_b = proj_ref[:, 1024:1536]
        u_c = proj_ref[:, 1536:2048]
        u_v = proj_ref[:, 2048:2560]
        ext_lx[0:HALO, :] = projh_ref[:, 0:512] * halo_on
        ext_lx[HALO:HALO + ts, :] = proj_ref[:, 0:512]
        xl = vl_ref[4:5, :] + vl_ref[0:1, :] * ext_lx[pl.ds(5, ts), :]
        for k in range(1, 4):
            xl = xl + vl_ref[k:k + 1, :] * ext_lx[pl.ds(5 + k, ts), :]
        xlb = xl.astype(BF16)
        sp = _softplus(vl_ref[8:9, :])
        r, ig, a, msq, mult = _lru_gates(xlb, gab_ref[...], vd_ref[3:4, :], sp, first_row)
        hl = hl_ref[...]
        ge, th = _gelu(u_ly)
        p = ge * hl
        rl = lax.rsqrt(_gmean(p * p, a64m) + EPS)
        nl = p * rl
        ext_cv[0:HALO, :] = projh_ref[:, 1536:2048] * projh_ref[:, 2048:2560] * halo_on
        ext_cv[HALO:HALO + ts, :] = u_c * u_v
        q = vl_ref[5:6, :] * ext_cv[pl.ds(6, ts), :]
        for k in range(1, 3):
            q = q + vl_ref[5 + k:6 + k, :] * ext_cv[pl.ds(6 + k, ts), :]
        yc = u_b * q
        rc = lax.rsqrt(_gmean(yc * yc, a64m) + EPS)
        nc = yc * rc

        accl_ref[9:10, :] += _colsum(dyl * nl)
        dnl = dyl * lg
        dp = rl * (dnl - nl * _gmean(dnl * nl, a64m))
        dproj_ref[:, 512:1024] = ((dp * hl) * _gelu_grad(u_ly, th)).astype(BF16)
        a_next = jnp.where(row == ts - 1, acar[0:1, :], pltpu.roll(a, ts - 1, 0))
        acum, gloc = _scan_rev(a_next, dp * ge, row)
        gbuf[...] = gloc + acum * gcar[0:1, :]
        gcar[0:1, :] = gbuf[0:1, :]
        ext_hl[0:HALO, :] = hlh_ref[...] * halo_on
        ext_hl[HALO:HALO + ts, :] = hl
        acar[...] = a[0:HALO, :]
        gt = gbuf[...]
        da = gt * ext_hl[pl.ds(HALO - 1, ts), :]
        dmult = gt * ig * xl
        di = gt * mult * xl
        dxl = gt * mult * ig
        dla = da * a - jnp.where(first_row, 0.0, dmult * a * a / msq)
        accl_ref[8:9, :] += _colsum(dla * ((-C_GATE) * r))
        dra = dla * ((-C_GATE) * sp) * r * (1.0 - r)
        dia = di * ig * (1.0 - ig)
        accd_ref[4:5, 0:D_LRU] += _colsum(dra)
        accd_ref[4:5, D_LRU:2 * D_LRU] += _colsum(dia)
        dgb_ref[:, 0:D_LRU] = dra.astype(BF16)
        dgb_ref[:, D_LRU:2 * D_LRU] = dia.astype(BF16)
        dxl = dxl + _dot_nt(dgb_ref[...], gab_ref[...])
        ggate_acc[...] += _dot_tn(xlb, dgb_ref[...])
        accl_ref[4:5, :] += _colsum(dxl)
        for k in range(4):
            accl_ref[k:k + 1, :] += _colsum(dxl * ext_lx[pl.ds(5 + k, ts), :])
        ext_dxl[0:ts, :] = dxl
        du_lx = vl_ref[0:1, :] * ext_dxl[pl.ds(3, ts), :]
        for k in range(1, 4):
            du_lx = du_lx + vl_ref[k:k + 1, :] * ext_dxl[pl.ds(3 - k, ts), :]
        ext_dxl[ts:ts + HALO, :] = ext_dxl[0:HALO, :]
        dproj_ref[:, 0:512] = du_lx.astype(BF16)

        accl_ref[10:11, :] += _colsum(dyv * nc)
        dnc = dyv * cg
        dyc = rc * (dnc - nc * _gmean(dnc * nc, a64m))
        dproj_ref[:, 1024:1536] = (dyc * q).astype(BF16)
        dq = dyc * u_b
        for k in range(3):
            accl_ref[5 + k:6 + k, :] += _colsum(dq * ext_cv[pl.ds(6 + k, ts), :])
        ext_dq[0:ts, :] = dq
        dcv = vl_ref[5:6, :] * ext_dq[pl.ds(2, ts), :]
        for k in range(1, 3):
            dcv = dcv + vl_ref[5 + k:6 + k, :] * ext_dq[pl.ds(2 - k, ts), :]
        ext_dq[ts:ts + HALO, :] = ext_dq[0:HALO, :]
        dproj_ref[:, 1536:2048] = (dcv * u_v).astype(BF16)
        dproj_ref[:, 2048:2560] = (dcv * u_c).astype(BF16)

        dh = _dot_nt(dproj_ref[:, 0:WIN_BLK], win_ref[0])
        for j in range(1, N_CHIP):
            dh = dh + _dot_nt(dproj_ref[:, j * WIN_BLK:(j + 1) * WIN_BLK], win_ref[j])
        for j in range(N_CHIP):
            gwin_acc[j] += _dot_tn(hb_ref[...], dproj_ref[:, j * WIN_BLK:(j + 1) * WIN_BLK])
        xt = x_ref[...]
        r1 = lax.rsqrt(jnp.mean(xt * xt, axis=-1, keepdims=True) + EPS)
        n1 = xt * r1
        accd_ref[1:2, :] += _colsum(dh * (n1 * g1))
        accd_ref[0:1, :] += _colsum(dh)
        dhn1 = dh * (1.0 + scale1)
        accd_ref[3:4, :] += _colsum(dhn1 * n1)
        dn1 = dhn1 * g1
        gx_ref[...] = dx1 + r1 * (dn1 - n1 * jnp.mean(dn1 * n1, axis=-1, keepdims=True))

        @pl.when(i == nt - 1)
        def _():
            outs = [pltpu.make_async_copy(acc, dst, sems.at[k]) for k, (acc, dst) in enumerate(
                ((gwin_acc, gwin_hbm), (gwout_acc, gwout_hbm), (ggate_acc, ggate_hbm)))]
            for cp in outs:
                cp.start()
            for cp in outs:
                cp.wait()

    tile = lambda w: pl.BlockSpec((ts, w), lambda i: (nt - 1 - i, 0))
    halo = lambda w: pl.BlockSpec((HALO, w), lambda i: (jnp.maximum((nt - 1 - i) * hpt - 1, 0), 0))
    ext = pltpu.VMEM((ts + HALO, D_LRU), F32)
    return pl.pallas_call(
        body, name="mix_bwd", grid=(nt,),
        in_specs=[SMEM, tile(D_MODEL), tile(D_MODEL), tile(D_MODEL), tile(D_IN), halo(D_IN), tile(D_LRU), halo(D_LRU),
                  tile(D_MODEL), tile(D_MODEL), _full((8, D_MODEL)), _full((8, D_MODEL)), _full((16, D_LRU)),
                  ANY, ANY, ANY, ANY, _full((D_LRU, 2 * D_LRU), True), _full((D_LRU, D_LRU), True)],
        out_specs=[tile(D_MODEL), _full((8, D_MODEL)), _full((16, D_LRU)), ANY, ANY, ANY],
        out_shape=[jax.ShapeDtypeStruct((s, D_MODEL), F32),
                   jax.ShapeDtypeStruct((8, D_MODEL), F32), jax.ShapeDtypeStruct((16, D_LRU), F32),
                   jax.ShapeDtypeStruct((N_CHIP, D_MODEL, WIN_BLK), F32), jax.ShapeDtypeStruct((D_MODEL, D_MODEL), F32),
                   jax.ShapeDtypeStruct((D_LRU, 2 * D_LRU), F32)],
        scratch_shapes=[pltpu.VMEM((N_CHIP, D_MODEL, WIN_BLK), BF16), pltpu.VMEM((D_MODEL, D_MODEL), BF16),
                        pltpu.VMEM((ts, D_IN), BF16), pltpu.VMEM((ts, 2 * D_LRU), BF16),
                        pltpu.VMEM((N_CHIP, D_MODEL, WIN_BLK), F32), pltpu.VMEM((D_MODEL, D_MODEL), F32),
                        pltpu.VMEM((D_LRU, 2 * D_LRU), F32),
                        ext, ext, ext, ext, ext, pltpu.VMEM((ts, D_LRU), F32),
                        pltpu.VMEM((HALO, D_LRU), F32), pltpu.VMEM((HALO, D_LRU), F32),
                        pltpu.SemaphoreType.DMA((2 * N_CHIP,))],
        compiler_params=pltpu.CompilerParams(dimension_semantics=("arbitrary",), vmem_limit_bytes=VMEM_LIMIT),
    )(chip, dx1, x, mixed, proj, proj, hl, hl, hb, ycat, mod, vecd, vecl, *win, *wout, gab, a64)


def _wgrad_mlp(collective_id, h2b, dz, act, dmo):
    s = h2b.shape[0]
    nstep = 2 * N_CHIP
    half = FF_BLK // 2

    def body(h2_ref, dz_ref, act_ref, dmo_ref, p1_hbm, p2_hbm, buf, landed, summed, send_sems, recv_sems, out_sems):
        j = pl.program_id(0)
        x, y, c, _ = _position()

        def give(jj):
            return pltpu.make_async_remote_copy(
                src_ref=buf.at[jj % 2, pl.ds((1 - c) * half, half), :], dst_ref=landed.at[jj],
                send_sem=send_sems.at[jj % 2], recv_sem=recv_sems.at[jj],
                device_id=(x, y, 1 - c), device_id_type=MESH)

        def write_out(jj, dst):
            return pltpu.make_async_copy(summed.at[jj % 2], dst, out_sems.at[jj % 2])

        def add_sibling(jj):
            give(jj).wait_recv()
            own = buf[jj % 2, pl.ds(pl.multiple_of(c * half, half), half), :]
            summed[jj % 2] = (own.astype(F32) + landed[jj].astype(F32)).astype(BF16)

        @pl.when(j == 0)
        def _():
            pl.semaphore_signal(pltpu.get_barrier_semaphore(), inc=1, device_id=(x, y, 1 - c), device_id_type=MESH)

        @pl.when(j >= 2)
        def _():
            give(j - 2).wait_send()

        @pl.when(j < N_CHIP)
        def _():
            buf[j % 2] = _dot_tn(act_ref[...], dmo_ref[...]).astype(BF16)

        @pl.when(j >= N_CHIP)
        def _():
            buf[j % 2] = _dot_tn(h2_ref[...], dz_ref[...]).astype(BF16)

        @pl.when(j == 0)
        def _():
            pl.semaphore_wait(pltpu.get_barrier_semaphore(), 1)

        give(j).start()

        @pl.when(j >= 1)
        def _():
            jm = j - 1

            @pl.when(jm >= 2)
            def _():
                write_out(jm - 2, p2_hbm.at[0]).wait()

            add_sibling(jm)

            @pl.when(jm < N_CHIP)
            def _():
                write_out(jm, p2_hbm.at[jm]).start()

            @pl.when(jm >= N_CHIP)
            def _():
                write_out(jm, p1_hbm.at[jm - N_CHIP]).start()

        @pl.when(j == nstep - 1)
        def _():
            last = nstep - 1
            write_out(last - 2, p1_hbm.at[0]).wait()
            add_sibling(last)
            write_out(last, p1_hbm.at[N_CHIP - 1]).start()
            for jj in (last - 1, last):
                give(jj).wait_send()
                write_out(jj, p1_hbm.at[0]).wait()

    sds = jax.ShapeDtypeStruct((N_CHIP, half, D_MODEL), BF16)
    whole = pl.BlockSpec((s, D_MODEL), lambda j: (0, 0))
    return pl.pallas_call(
        body, name="wgrad_mlp", grid=(nstep,),
        in_specs=[whole, pl.BlockSpec((s, FF_BLK), lambda j: (0, jnp.maximum(j - N_CHIP, 0))),
                  pl.BlockSpec((s, FF_BLK), lambda j: (0, jnp.minimum(j, N_CHIP - 1))), whole],
        out_specs=[ANY, ANY], out_shape=[sds, sds],
        scratch_shapes=[pltpu.VMEM((2, FF_BLK, D_MODEL), BF16), pltpu.VMEM((nstep, half, D_MODEL), BF16),
                        pltpu.VMEM((2, half, D_MODEL), BF16), pltpu.SemaphoreType.DMA((2,)),
                        pltpu.SemaphoreType.DMA((nstep,)), pltpu.SemaphoreType.DMA((2,))],
        compiler_params=pltpu.CompilerParams(dimension_semantics=("arbitrary",), vmem_limit_bytes=VMEM_LIMIT,
                                             collective_id=collective_id),
    )(h2b, dz, act, dmo)


def _mod_matmul(c_all, ada_w_loc):
    n = ada_w_loc.shape[1]
    cb = 512

    def body(c_ref, w_ref, o_ref):
        c = c_ref[...]
        sc = c * jax.nn.sigmoid(c)
        o_ref[...] = _dot(sc.astype(BF16), w_ref[...].astype(BF16))

    return pl.pallas_call(
        body, name="mod_matmul", grid=(n // cb,),
        in_specs=[_full((8, D_MODEL)), pl.BlockSpec((D_MODEL, cb), lambda j: (0, j))],
        out_specs=pl.BlockSpec((8, cb), lambda j: (0, j)),
        out_shape=jax.ShapeDtypeStruct((8, n), F32),
        compiler_params=pltpu.CompilerParams(dimension_semantics=("arbitrary",), vmem_limit_bytes=VMEM_LIMIT),
    )(c_all, ada_w_loc)


def _adam_math(w, g, m, v):
    m = ADAM_B1 * m + (1.0 - ADAM_B1) * g
    v = ADAM_B2 * v + (1.0 - ADAM_B2) * (g * g)
    m_hat = m / (1.0 - ADAM_B1 ** ADAM_STEP)
    v_hat = v / (1.0 - ADAM_B2 ** ADAM_STEP)
    delta = (-ADAM_LR) * (m_hat / (jnp.sqrt(v_hat) + ADAM_EPS) + ADAM_WD * w)
    return delta, m, v


def _adam(name, core, shards):
    n = len(shards)
    r, c = shards[0][0].shape
    half = r // 2
    rb = min(half, 128)
    nh = half // rb

    def body(core_ref, *refs):
        ins, outs = refs[:5 * n], refs[5 * n:]
        mine = (pl.program_id(0) // nh) == core_ref[0]
        for k in range(n):
            w_ref, go_ref, gs_ref, m_ref, v_ref = ins[5 * k:5 * k + 5]
            g_ref, d_ref, mo_ref, vo_ref = outs[4 * k:4 * k + 4]
            g = jnp.where(mine, go_ref[...], gs_ref[...])
            g_ref[...] = g
            d_ref[...], mo_ref[...], vo_ref[...] = _adam_math(w_ref[...], g, m_ref[...], v_ref[...])

    spec = pl.BlockSpec((rb, c), lambda i, core_ref: (i, 0))
    own = pl.BlockSpec((rb, c), lambda i, core_ref: (jnp.where(i // nh == core_ref[0], i % nh, 0), 0))
    sib = pl.BlockSpec((rb, c), lambda i, core_ref: (jnp.where(i // nh == core_ref[0], 0, i % nh), 0))
    sds = jax.ShapeDtypeStruct((r, c), F32)
    res = pl.pallas_call(
        body, name=name,
        grid_spec=pltpu.PrefetchScalarGridSpec(
            num_scalar_prefetch=1, grid=(r // rb,),
            in_specs=[spec, own, sib, spec, spec] * n, out_specs=[spec] * (4 * n)),
        out_shape=[sds] * (4 * n),
        compiler_params=pltpu.CompilerParams(dimension_semantics=("arbitrary",), vmem_limit_bytes=VMEM_LIMIT),
    )(core, *[t for s in shards for t in s])
    return [res[4 * k:4 * k + 4] for k in range(n)]


def _ada_grad_adam(chip, sct, dmod_cols, w, m, v):
    r, c = w.shape
    rb = 256

    def body(chip_ref, s_ref, dm_ref, w_ref, m_ref, v_ref, g_ref, d_ref, mo_ref, vo_ref):
        g = s_ref[:, 0:1] * dm_ref[0:1, :]
        for b in range(1, 8):
            g = g + s_ref[:, b:b + 1] * dm_ref[b:b + 1, :]
        g_ref[...] = g
        d_ref[...], mo_ref[...], vo_ref[...] = _adam_math(w_ref[...], g, m_ref[...], v_ref[...])

    spec = pl.BlockSpec((rb, c), lambda i, chip_ref: (i, 0))
    sds = jax.ShapeDtypeStruct((r, c), F32)
    return pl.pallas_call(
        body, name="ada_grad_adam",
        grid_spec=pltpu.PrefetchScalarGridSpec(
            num_scalar_prefetch=1, grid=(r // rb,),
            in_specs=[pl.BlockSpec((rb, 8), lambda i, chip_ref: (i, 0)),
                      pl.BlockSpec((8, c), lambda i, chip_ref: (0, chip_ref[0])), spec, spec, spec],
            out_specs=[spec] * 4),
        out_shape=[sds] * 4,
        compiler_params=pltpu.CompilerParams(dimension_semantics=("arbitrary",), vmem_limit_bytes=VMEM_LIMIT),
    )(chip, sct, dmod_cols, w, m, v)


def _position():
    x, y, c = lax.axis_index("x"), lax.axis_index("y"), lax.axis_index("c")
    chips = [(1 - x, y), (x, 1 - y), (1 - x, 1 - y)]
    return x, y, c, chips


def _ag8_run(ins, outs, send_sems, recv_sems, local_sems):
    na = len(ins)
    x, y, c, chips = _position()
    me, sibling = (x, y, c), (x, y, 1 - c)
    first, passed, local = [], [], []
    for a in range(na):
        m_per = ins[a].shape[0]

        def rows(px, py, pc, a=a, m_per=m_per):
            return outs[a].at[pl.ds((4 * px + 2 * py + pc) * m_per, m_per), :]

        def copy(k, block, to, src=None, a=a, rows=rows):
            return pltpu.make_async_remote_copy(
                src_ref=rows(*block) if src is None else src, dst_ref=rows(*block),
                send_sem=send_sems.at[7 * a + k], recv_sem=recv_sems.at[7 * a + k],
                device_id=to, device_id_type=MESH)

        mine = pltpu.make_async_copy(ins[a], rows(*me), local_sems.at[a])
        mine.start()
        local.append(mine)
        f = [copy(0, me, sibling, src=ins[a])]
        f += [copy(1 + j, me, (*chip, c), src=ins[a]) for j, chip in enumerate(chips)]
        for cp in f:
            cp.start()
        first.append((f, copy))
    for a in range(na):
        f, copy = first[a]
        p = [copy(4 + j, (*chip, c), sibling) for j, chip in enumerate(chips)]
        for j, chip in enumerate(chips):
            copy(1 + j, (*chip, c), me).wait_recv()
            p[j].start()
        passed.append(p)
    for a in range(na):
        f, copy = first[a]
        copy(0, sibling, me).wait_recv()
        for j, chip in enumerate(chips):
            copy(4 + j, (*chip, 1 - c), me).wait_recv()
        for cp in f + passed[a]:
            cp.wait_send()
        local[a].wait()


def _allgather8_seq(name, collective_id, arrs):
    na = len(arrs)
    hbm = pltpu.MemorySpace.HBM
    ins = [jax.new_ref(a, memory_space=hbm) for a in arrs]
    outs = [jax.empty_ref(jax.ShapeDtypeStruct((8 * a.shape[0], a.shape[1]), a.dtype), memory_space=hbm) for a in arrs]

    @pl.kernel(mesh=plsc.ScalarSubcoreMesh(axis_name="sequencer", num_cores=1), name=name,
               scratch_types=(pltpu.SemaphoreType.DMA((7 * na,)), pltpu.SemaphoreType.DMA((7 * na,)),
                              pltpu.SemaphoreType.DMA((na,))),
               compiler_params=pltpu.CompilerParams(collective_id=collective_id))
    def launch(send_sems, recv_sems, local_sems):
        x, y, c, chips = _position()
        peers = [(x, y, 1 - c)] + [(*chip, c) for chip in chips]
        barrier = pltpu.get_barrier_semaphore()
        for peer in peers:
            pl.semaphore_signal(barrier, inc=1, device_id=peer, device_id_type=MESH)
        pl.semaphore_wait(barrier, len(peers))
        _ag8_run(ins, outs, send_sems, recv_sems, local_sems)

    launch()
    return [o[...] for o in outs]


def _allgather8(name, arrs):
    na = len(arrs)

    def body(*refs):
        _ag8_run(refs[:na], refs[na:2 * na], *refs[2 * na:])

    return pl.pallas_call(
        body, name=name,
        out_shape=[jax.ShapeDtypeStruct((8 * a.shape[0], a.shape[1]), a.dtype) for a in arrs],
        in_specs=[VMEM] * na, out_specs=[VMEM] * na,
        scratch_shapes=[pltpu.SemaphoreType.DMA((7 * na,)), pltpu.SemaphoreType.DMA((7 * na,)),
                        pltpu.SemaphoreType.DMA((na,))],
        compiler_params=pltpu.CompilerParams(vmem_limit_bytes=VMEM_LIMIT),
    )(*arrs)


AG_SEMS = 7
AG_CHUNKS = 2


def _ag_copies(ins, outs, send_sems, recv_sems):
    x, y, c, chips = _position()
    sibling = (x, y, 1 - c)
    xn, yn, dg = [2 * chip[0] + chip[1] for chip in chips]
    to_x, to_y = (1 - x, y, c), (x, 1 - y, c)
    res = []
    for a in range(len(ins)):
        half = ins[a].shape[0] // 2
        piece = half // AG_CHUNKS
        for p in range(AG_CHUNKS):
            def copy(k, dst, to, src=None, base=AG_SEMS * (AG_CHUNKS * a + p)):
                return pltpu.make_async_remote_copy(
                    src_ref=dst if src is None else src, dst_ref=dst,
                    send_sem=send_sems.at[base + k], recv_sem=recv_sems.at[base + k],
                    device_id=to, device_id_type=MESH)

            def rows(chip, pc, q=None, a=a, start=p * piece, half=half, piece=piece):
                if q is None:
                    return outs[a].at[chip, pl.ds(pc * half + start, piece), :]
                return outs[a].at[chip, pl.ds(pc * half + start + q * (piece // 2), piece // 2), :]

            own = ins[a].at[pl.ds(c * half + p * piece, piece), :]
            mine = rows(2 * x + y, c)
            res.append(dict(
                sends=[copy(0, mine, to_x, src=own), copy(1, mine, to_y, src=own)],
                from_x=copy(0, rows(xn, c), to_x), from_y=copy(1, rows(yn, c), to_y),
                relay_y=copy(2, rows(xn, c, 0), to_y), relay_x=copy(3, rows(yn, c, 1), to_x),
                from_y_relay=copy(2, rows(dg, c, 0), to_y), from_x_relay=copy(3, rows(dg, c, 1), to_x),
                pass_on=[copy(4, rows(xn, c), sibling), copy(5, rows(yn, c), sibling), copy(6, rows(dg, c), sibling)],
                from_sibling=[copy(4, rows(xn, 1 - c), sibling), copy(5, rows(yn, 1 - c), sibling),
                              copy(6, rows(dg, 1 - c), sibling)]))
    return res


def _ag_start(ins, outs, send_sems, recv_sems):
    for cps in _ag_copies(ins, outs, send_sems, recv_sems):
        for cp in cps["sends"]:
            cp.start()


def _ag_relay(ins, outs, send_sems, recv_sems):
    for cps in _ag_copies(ins, outs, send_sems, recv_sems):
        cps["from_x"].wait_recv()
        cps["relay_y"].start()
        cps["pass_on"][0].start()
        cps["from_y"].wait_recv()
        cps["relay_x"].start()
        cps["pass_on"][1].start()


def _ag_complete(ins, outs, send_sems, recv_sems):
    copies = _ag_copies(ins, outs, send_sems, recv_sems)
    for cps in copies:
        cps["from_y_relay"].wait_recv()
        cps["from_x_relay"].wait_recv()
        cps["pass_on"][2].start()
    for cps in copies:
        for cp in cps["from_sibling"]:
            cp.wait_recv()
        for cp in cps["sends"] + [cps["relay_y"], cps["relay_x"]] + cps["pass_on"]:
            cp.wait_send()


def _ag_finish(ins, outs, send_sems, recv_sems):
    _ag_relay(ins, outs, send_sems, recv_sems)
    _ag_complete(ins, outs, send_sems, recv_sems)


def _allgather_weights(name, collective_id, shards):
    na = len(shards)
    hbm = pltpu.MemorySpace.HBM
    ins = [jax.new_ref(s, memory_space=hbm) for s in shards]
    outs = [jax.empty_ref(jax.ShapeDtypeStruct((N_CHIP,) + s.shape, s.dtype), memory_space=hbm) for s in shards]

    @pl.kernel(mesh=plsc.ScalarSubcoreMesh(axis_name="sequencer", num_cores=1), name=name,
               scratch_types=(pltpu.SemaphoreType.DMA((AG_SEMS * AG_CHUNKS * na,)),
                              pltpu.SemaphoreType.DMA((AG_SEMS * AG_CHUNKS * na,))),
               compiler_params=pltpu.CompilerParams(collective_id=collective_id))
    def launch(send_sems, recv_sems):
        x, y, c, _ = _position()
        peers = [(1 - x, y, c), (x, 1 - y, c), (x, y, 1 - c)]
        barrier = pltpu.get_barrier_semaphore()
        for peer in peers:
            pl.semaphore_signal(barrier, inc=1, device_id=peer, device_id_type=MESH)
        pl.semaphore_wait(barrier, len(peers))
        _ag_start(ins, outs, send_sems, recv_sems)
        _ag_finish(ins, outs, send_sems, recv_sems)

    launch()
    return [o[...] for o in outs]


def _sibling_swap(name, arrs, split_rows, collective_id=None):
    na = len(arrs)
    shapes = [jax.ShapeDtypeStruct((a.shape[0], a.shape[1] // 2, a.shape[2]) if split_rows else a.shape, a.dtype)
              for a in arrs]

    def run(ins, outs, send_sems, recv_sems):
        x, y, c, _ = _position()
        cps = []
        for a in range(na):
            src = ins[a]
            if split_rows:
                half = src.shape[1] // 2
                src = src.at[:, pl.ds((1 - c) * half, half), :]
            cp = pltpu.make_async_remote_copy(
                src_ref=src, dst_ref=outs[a], send_sem=send_sems.at[a], recv_sem=recv_sems.at[a],
                device_id=(x, y, 1 - c), device_id_type=MESH)
            cp.start()
            cps.append(cp)
        for cp in cps:
            cp.wait()

    sems = (pltpu.SemaphoreType.DMA((na,)), pltpu.SemaphoreType.DMA((na,)))
    if collective_id is None:
        return pl.pallas_call(
            lambda *refs: run(refs[:na], refs[na:2 * na], *refs[2 * na:]), name=name, out_shape=shapes,
            in_specs=[ANY] * na, out_specs=[ANY] * na, scratch_shapes=list(sems))(*arrs)

    hbm = pltpu.MemorySpace.HBM
    ins = [jax.new_ref(a, memory_space=hbm) for a in arrs]
    outs = [jax.empty_ref(s, memory_space=hbm) for s in shapes]

    @pl.kernel(mesh=plsc.ScalarSubcoreMesh(axis_name="sequencer", num_cores=1), name=name, scratch_types=sems,
               compiler_params=pltpu.CompilerParams(collective_id=collective_id))
    def launch(send_sems, recv_sems):
        x, y, c, _ = _position()
        barrier = pltpu.get_barrier_semaphore()
        pl.semaphore_signal(barrier, inc=1, device_id=(x, y, 1 - c), device_id_type=MESH)
        pl.semaphore_wait(barrier, 1)
        run(ins, outs, send_sems, recv_sems)

    launch()
    return [o[...] for o in outs]


def _xchg_copies(ins, outs, send_sems, recv_sems):
    x, y, c, chips = _position()
    return [pltpu.make_async_remote_copy(
        src_ref=ins[a].at[2 * chip[0] + chip[1]], dst_ref=outs[a].at[j],
        send_sem=send_sems.at[3 * a + j], recv_sem=recv_sems.at[3 * a + j],
        device_id=(*chip, c), device_id_type=MESH) for a in range(len(ins)) for j, chip in enumerate(chips)]


def _exchange_chips(name, collective_id, parts):
    na = len(parts)
    hbm = pltpu.MemorySpace.HBM
    ins = [jax.new_ref(p, memory_space=hbm) for p in parts]
    outs = [jax.empty_ref(jax.ShapeDtypeStruct((3,) + p.shape[1:], p.dtype), memory_space=hbm) for p in parts]

    @pl.kernel(mesh=plsc.ScalarSubcoreMesh(axis_name="sequencer", num_cores=1), name=name,
               scratch_types=(pltpu.SemaphoreType.DMA((3 * na,)), pltpu.SemaphoreType.DMA((3 * na,))),
               compiler_params=pltpu.CompilerParams(collective_id=collective_id))
    def launch(send_sems, recv_sems):
        x, y, c, chips = _position()
        barrier = pltpu.get_barrier_semaphore()
        for chip in chips:
            pl.semaphore_signal(barrier, inc=1, device_id=(*chip, c), device_id_type=MESH)
        pl.semaphore_wait(barrier, len(chips))
        for cp in _xchg_copies(ins, outs, send_sems, recv_sems):
            cp.start()
        for cp in _xchg_copies(ins, outs, send_sems, recv_sems):
            cp.wait()

    launch()
    return [q[...] for q in outs]


def _add_sibling(name, grad, recv, core, after=()):
    _, r, c = grad.shape
    half = r // 2
    rb = half
    nrb = half // rb

    def body(core_ref, g_ref, r_ref, *refs):
        refs[-1][...] = (g_ref[...].astype(F32) + r_ref[...].astype(F32)).astype(BF16)

    return pl.pallas_call(
        body, name=name,
        grid_spec=pltpu.PrefetchScalarGridSpec(
            num_scalar_prefetch=1, grid=(N_CHIP, nrb),
            in_specs=[pl.BlockSpec((1, rb, c), lambda j, i, core_ref: (j, core_ref[0] * nrb + i, 0)),
                      pl.BlockSpec((1, rb, c), lambda j, i, core_ref: (j, i, 0))] + [ANY] * len(after),
            out_specs=pl.BlockSpec((1, rb, c), lambda j, i, core_ref: (j, i, 0))),
        out_shape=jax.ShapeDtypeStruct((N_CHIP, half, c), BF16),
        compiler_params=pltpu.CompilerParams(dimension_semantics=("arbitrary", "arbitrary"),
                                             vmem_limit_bytes=VMEM_LIMIT),
    )(core, grad, recv, *after)


def _add_chips(name, chip, p, q, after=()):
    _, half, c = q.shape
    rb = min(half, 256)

    def body(chip_ref, p_ref, q_ref, *refs):
        acc = p_ref[0].astype(F32)
        for j in range(3):
            acc = acc + q_ref[j].astype(F32)
        refs[-1][...] = acc

    return pl.pallas_call(
        body, name=name,
        grid_spec=pltpu.PrefetchScalarGridSpec(
            num_scalar_prefetch=1, grid=(half // rb,),
            in_specs=[pl.BlockSpec((1, rb, c), lambda i, chip_ref: (chip_ref[0], i, 0)),
                      pl.BlockSpec((3, rb, c), lambda i, chip_ref: (0, i, 0))] + [ANY] * len(after),
            out_specs=pl.BlockSpec((rb, c), lambda i, chip_ref: (i, 0))),
        out_shape=jax.ShapeDtypeStruct((half, c), F32),
        compiler_params=pltpu.CompilerParams(dimension_semantics=("arbitrary",), vmem_limit_bytes=VMEM_LIMIT),
    )(chip, p, q, *after)


def _small_update(gad, gam, gl, gg, mychip, params):
    names = ["ada_b", "norm1_g", "lru_conv_b", "gate_a_w", "gate_a_b", "gate_x_w", "gate_x_b", "a_param",
             "lru_conv_w", "short_conv_w", "lru_out_g", "conv_out_g", "norm2_g", "final_g"]
    flat = [t for n in names for t in params[n]]
    nin = len(flat)

    def body(chip_ref, gad_ref, gam_ref, gl_ref, gg_ref, *refs):
        ins = {n: refs[3 * k:3 * k + 3] for k, n in enumerate(names)}
        outs = {n: refs[nin + 4 * k:nin + 4 * k + 4] for k, n in enumerate(names)}
        loss_ref, dmod_ref = refs[nin + 4 * len(names):nin + 4 * len(names) + 2]

        def dsum(ref, lo, n):
            per = ref.shape[0] // 8
            acc = ref[lo:lo + n, :].astype(F32)
            for dev in range(1, 8):
                acc = acc + ref[dev * per + lo:dev * per + lo + n, :].astype(F32)
            return acc

        def update(n, g):
            w_ref, m_ref, v_ref = ins[n]
            g_ref, d_ref, mo_ref, vo_ref = outs[n]
            g = g.reshape(w_ref.shape)
            g_ref[...] = g
            d_ref[...], mo_ref[...], vo_ref[...] = _adam_math(w_ref[...], g, m_ref[...], v_ref[...])

        d, dm, l, lw = refs[-4:]
        d[...] = dsum(gad_ref, 0, 8)
        dm[...] = dsum(gam_ref, 0, 8)
        l[...] = dsum(gl_ref, 0, 16)
        for dev in range(8):
            for k in range(3):
                dmod_ref[dev:dev + 1, k * D_MODEL:(k + 1) * D_MODEL] = gad_ref[dev * 8 + k:dev * 8 + k + 1, :]
                dmod_ref[dev:dev + 1, (3 + k) * D_MODEL:(4 + k) * D_MODEL] = gam_ref[dev * 8 + k:dev * 8 + k + 1, :]
        w_ref, m_ref, v_ref = ins["ada_b"]
        g_ref, d_ref, mo_ref, vo_ref = outs["ada_b"]
        for k in range(3):
            g_ref[:, k * D_MODEL:(k + 1) * D_MODEL] = d[k:k + 1, :]
            g_ref[:, (3 + k) * D_MODEL:(4 + k) * D_MODEL] = dm[k:k + 1, :]
        d_ref[...], mo_ref[...], vo_ref[...] = _adam_math(w_ref[...], g_ref[...], m_ref[...], v_ref[...])
        update("norm1_g", d[3:4, :])
        update("norm2_g", dm[3:4, :])
        update("final_g", dm[4:5, :])
        update("gate_a_b", d[4:5, 0:D_LRU])
        update("gate_x_b", d[4:5, D_LRU:2 * D_LRU])
        update("lru_conv_b", l[4:5, :])
        update("a_param", l[8:9, :] * jax.nn.sigmoid(ins["a_param"][0][...]))
        update("lru_out_g", l[9:10, :])
        update("conv_out_g", l[10:11, :])
        loss_ref[...] = jnp.broadcast_to(dm[5:6, 0:128], (8, 128))
        chip = chip_ref[0]
        acc = jnp.zeros((8, 128), F32)
        for j in range(N_CHIP):
            acc = acc + jnp.where(chip == j, l[0:8, j * 128:(j + 1) * 128], 0.0)
        lw[...] = acc
        update("lru_conv_w", lw[0:4, :])
        update("short_conv_w", lw[5:8, :])
        gates = dsum(gg_ref, 0, D_LRU)
        update("gate_a_w", gates[:, 0:HEAD])
        update("gate_x_w", gates[:, HEAD:2 * HEAD])

    out_shape = []
    for n in names:
        out_shape += [jax.ShapeDtypeStruct(params[n][0].shape, F32)] * 4
    out_shape += [jax.ShapeDtypeStruct((8, 128), F32), jax.ShapeDtypeStruct((8, 6 * D_MODEL), F32)]
    res = pl.pallas_call(
        body, name="small_update", out_shape=out_shape,
        in_specs=[SMEM] + [VMEM] * (4 + nin),
        out_specs=[VMEM] * len(out_shape),
        scratch_shapes=[pltpu.VMEM((8, D_MODEL), F32), pltpu.VMEM((8, D_MODEL), F32), pltpu.VMEM((16, D_LRU), F32),
                        pltpu.VMEM((8, 128), F32)],
        compiler_params=pltpu.CompilerParams(vmem_limit_bytes=VMEM_LIMIT),
    )(mychip, gad, gam, gl, gg, *flat)
    per = {n: res[4 * k:4 * k + 4] for k, n in enumerate(names)}
    return per, res[-2], res[-1]


def _block_diag(w):
    eye = jnp.eye(8, dtype=w.dtype)
    return (eye[:, None, :, None] * w[:, :, None, :]).reshape(8 * HEAD, 8 * HEAD)


def _diag_blocks(g):
    return jnp.concatenate([g[h * HEAD:(h + 1) * HEAD, h * HEAD:(h + 1) * HEAD] for h in range(8)], axis=0)


def kernel(x, c, ada_w, ada_b, norm1_g, w_in, lru_conv_w, lru_conv_b, gate_a_w, gate_a_b, gate_x_w, gate_x_b, a_param, short_conv_w, lru_out_g, conv_out_g, w_out, norm2_g, w_mlp1, w_mlp2, final_g, loss_target, m_ada_w, m_ada_b, m_norm1_g, m_w_in, m_lru_conv_w, m_lru_conv_b, m_gate_a_w, m_gate_a_b, m_gate_x_w, m_gate_x_b, m_a_param, m_short_conv_w, m_lru_out_g, m_conv_out_g, m_w_out, m_norm2_g, m_w_mlp1, m_w_mlp2, m_final_g, v_ada_w, v_ada_b, v_norm1_g, v_w_in, v_lru_conv_w, v_lru_conv_b, v_gate_a_w, v_gate_a_b, v_gate_x_w, v_gate_x_b, v_a_param, v_short_conv_w, v_lru_out_g, v_conv_out_g, v_w_out, v_norm2_g, v_w_mlp1, v_w_mlp2, v_final_g):
    xi, yi, ci = lax.axis_index("x"), lax.axis_index("y"), lax.axis_index("c")
    mychip = 2 * xi + yi
    me = 4 * xi + 2 * yi + ci

    own_in, own_out = w_in[0].astype(BF16), w_out[0].astype(BF16)
    win_all, wout_all = _allgather_weights("allgather_mixer_weights", 1, [own_in, own_out])
    own_w1, own_w2 = w_mlp1[0].astype(BF16), w_mlp2[0].astype(BF16)
    w1_all, w2_all = _allgather_weights("allgather_mlp_weights", 2, [own_w1, own_w2])

    c_blk = jnp.zeros((8, D_MODEL), F32).at[0:1].set(c)
    cw_blk = jnp.zeros((8, 128), F32).at[0:4].set(lru_conv_w[0]).at[4:7].set(short_conv_w[0])
    c_g, cw_g = _allgather8("allgather_cond", [c_blk, cw_blk])
    c_all = c_g.reshape(8, 8, D_MODEL)[:, 0]
    cw_g = cw_g.reshape(4, 2, 8, 128)[:, 0]
    lcw = cw_g[:, 0:4].transpose(1, 0, 2).reshape(4, D_LRU)
    scw = cw_g[:, 4:7].transpose(1, 0, 2).reshape(3, D_LRU)

    mod_loc = _mod_matmul(c_all, ada_w[0])
    (mod_g,) = _allgather8("allgather_mod", [mod_loc])
    mod_all = mod_g.reshape(4, 2, 8, 6 * D_MODEL // 4)[:, 0].transpose(1, 0, 2).reshape(8, 6 * D_MODEL) + ada_b
    mod_pad = jnp.pad(mod_all.reshape(8, 6, D_MODEL), ((0, 0), (0, 2), (0, 0)))
    mod = lax.dynamic_slice_in_dim(mod_pad, me, 1, axis=0).reshape(8, D_MODEL)

    win, wout = (win_all, own_in), (wout_all, own_out)
    chip = mychip.reshape(1).astype(jnp.int32)
    core = ci.reshape(1).astype(jnp.int32)

    vecd = jnp.concatenate([norm1_g, norm2_g, final_g[None, :], jnp.concatenate([gate_a_b, gate_x_b], axis=1),
                            jnp.zeros((4, D_MODEL), F32)], axis=0)
    vecl = jnp.concatenate([lcw, lru_conv_b, scw, a_param, lru_out_g, conv_out_g, jnp.zeros((5, D_LRU), F32)], axis=0)
    gab = jnp.concatenate([_block_diag(gate_a_w[0]), _block_diag(gate_x_w[0])], axis=1).astype(BF16)
    a64 = _block_diag(jnp.full((8, HEAD, HEAD), 1.0 / HEAD, F32)).astype(BF16)

    hb, proj, hl, ycat, mixed, x1 = _mix_fwd(chip, x[0], mod, vecd, vecl, win, wout, gab, a64)
    dx1, act, dz, dmo, h2b, accm = _mlp_fwd_bwd(
        chip, x1, loss_target[0], mod, vecd, (w1_all, own_w1), (w2_all, own_w2))

    parts_mlp = list(_wgrad_mlp(6, h2b, dz, act, dmo))
    q_w1, q_w2 = _exchange_chips("rs_exchange_mlp", 0, parts_mlp)
    grad_x, accd, accl, g_win, g_wout, g_gate = _mix_bwd(
        chip, dx1, x[0], mixed, proj, hl, hb, ycat, mod, vecd, vecl, win, wout, gab, a64)

    g_mix = [g_win, g_wout.reshape(N_CHIP, WOUT_BLK, D_MODEL)]
    recv_mix = _sibling_swap("rs_swap_halves_mix", g_mix, True, collective_id=4)
    own_mlp = [_add_chips("rs_add_chips_mlp%d" % k, chip, p, q) for k, (p, q) in enumerate(zip(parts_mlp, (q_w1, q_w2)))]
    sib_mlp = _sibling_swap("rs_swap_reduced_mlp", own_mlp, False, collective_id=5)
    gg_blk = jnp.concatenate([_diag_blocks(g_gate[:, 0:D_LRU]), _diag_blocks(g_gate[:, D_LRU:2 * D_LRU])], axis=1)
    gad, gam, gl, gg = _allgather8_seq("allgather_small_grads", 8, [accd, accm, accl, gg_blk.astype(BF16)])

    parts_mix = [_add_sibling("rs_add_sibling_mix%d" % k, g, r, core)
                 for k, (g, r) in enumerate(zip(g_mix, recv_mix))]
    landed_mix = _exchange_chips("rs_exchange_mix", 3, parts_mix)
    res_w1, res_w2 = _adam("adam_mlp", core, [(w_mlp1[0], own_mlp[0], sib_mlp[0], m_w_mlp1[0], v_w_mlp1[0]),
                                              (w_mlp2[0], own_mlp[1], sib_mlp[1], m_w_mlp2[0], v_w_mlp2[0])])
    own_mix = [_add_chips("rs_add_chips_mix%d" % k, chip, p, q, after=[res_w1[1]])
               for k, (p, q) in enumerate(zip(parts_mix, landed_mix))]
    sib_mix = _sibling_swap("rs_swap_reduced_mix", own_mix, False)
    (res_win,) = _adam("adam_w_in", core, [(w_in[0], own_mix[0], sib_mix[0], m_w_in[0], v_w_in[0])])
    (res_wout,) = _adam("adam_w_out", core, [(w_out[0], own_mix[1], sib_mix[1], m_w_out[0], v_w_out[0])])

    params = {
        "ada_b": (ada_b, m_ada_b, v_ada_b), "norm1_g": (norm1_g, m_norm1_g, v_norm1_g),
        "lru_conv_b": (lru_conv_b, m_lru_conv_b, v_lru_conv_b),
        "gate_a_w": tuple(t.reshape(D_LRU, HEAD) for t in (gate_a_w, m_gate_a_w, v_gate_a_w)),
        "gate_a_b": (gate_a_b, m_gate_a_b, v_gate_a_b),
        "gate_x_w": tuple(t.reshape(D_LRU, HEAD) for t in (gate_x_w, m_gate_x_w, v_gate_x_w)),
        "gate_x_b": (gate_x_b, m_gate_x_b, v_gate_x_b), "a_param": (a_param, m_a_param, v_a_param),
        "lru_conv_w": (lru_conv_w, m_lru_conv_w, v_lru_conv_w),
        "short_conv_w": (short_conv_w, m_short_conv_w, v_short_conv_w),
        "lru_out_g": (lru_out_g, m_lru_out_g, v_lru_out_g), "conv_out_g": (conv_out_g, m_conv_out_g, v_conv_out_g),
        "norm2_g": (norm2_g, m_norm2_g, v_norm2_g),
        "final_g": tuple(t[None, :] for t in (final_g, m_final_g, v_final_g)),
    }
    small, loss_blk, dmod_cols = _small_update(gad, gam, gl, gg, chip, params)
    loss = loss_blk[0, 0]

    sct = (c_all * jax.nn.sigmoid(c_all)).T
    ada = _ada_grad_adam(chip, sct, dmod_cols, ada_w[0], m_ada_w[0], v_ada_w[0])

    res = {"ada_w": ada, "w_in": res_win, "w_out": res_wout, "w_mlp1": res_w1, "w_mlp2": res_w2}
    res = {n: tuple(t[None] for t in r) for n, r in res.items()}
    shapes = {"gate_a_w": gate_a_w.shape, "gate_x_w": gate_x_w.shape, "lru_conv_w": lru_conv_w.shape,
              "short_conv_w": short_conv_w.shape, "final_g": final_g.shape}
    for n, t in small.items():
        res[n] = tuple(u.reshape(shapes[n]) if n in shapes else u for u in t)

    order = ["ada_w", "ada_b", "norm1_g", "w_in", "lru_conv_w", "lru_conv_b", "gate_a_w", "gate_a_b", "gate_x_w",
             "gate_x_b", "a_param", "short_conv_w", "lru_out_g", "conv_out_g", "w_out", "norm2_g", "w_mlp1",
             "w_mlp2", "final_g"]
    return (loss, grad_x[None], *[res[n][0] for n in order], *[res[n][1] for n in order],
            *[res[n][2] for n in order], *[res[n][3] for n in order])
```

```python
import jax
import jax.numpy as jnp
from jax import lax
from jax.experimental import pallas as pl
from jax.experimental.pallas import tpu as pltpu
from jax.experimental.pallas import tpu_sc as plsc

F32 = jnp.float32
BF16 = jnp.bfloat16

D_MODEL = 1024
D_LRU = 512
D_IN = 2560
D_FF = 4096
N_CHIP = 4
WIN_BLK = D_IN // N_CHIP
WOUT_BLK = D_MODEL // N_CHIP
FF_BLK = D_FF // N_CHIP
HEAD = 64
EPS = 1e-6
C_GATE = 8.0
TOKEN_TILE = 256
MIX_FWD_TILE = 512
HALO = 8
VMEM_LIMIT = 60 * 1024 * 1024

ADAM_LR = 0.001
ADAM_B1 = 0.9
ADAM_B2 = 0.999
ADAM_EPS = 1e-08
ADAM_WD = 0.01
ADAM_STEP = 10

MESH = pl.DeviceIdType.MESH
ANY = pl.BlockSpec(memory_space=pl.ANY)
VMEM = pl.BlockSpec(memory_space=pltpu.VMEM)
SMEM = pl.BlockSpec(memory_space=pltpu.SMEM)


def _full(shape, single=False):
    nd = len(shape)
    if single:
        return pl.BlockSpec(shape, lambda *_: (0,) * nd, pipeline_mode=pl.Buffered(1))
    return pl.BlockSpec(shape, lambda *_: (0,) * nd)


def _dot(a, b):
    return jnp.dot(a, b, preferred_element_type=F32)


def _dot_nt(a, b):
    return lax.dot_general(a, b, (((1,), (1,)), ((), ())), preferred_element_type=F32)


def _dot_tn(a, b):
    return lax.dot_general(a, b, (((0,), (0,)), ((), ())), preferred_element_type=F32)


def _gmean(v, a64):
    hi = v.astype(BF16)
    lo = (v - hi.astype(F32)).astype(BF16)
    return _dot(hi, a64) + _dot(lo, a64)


def _gelu(x):
    u = 0.7978845608028654 * (x + 0.044715 * x * x * x)
    t = jnp.tanh(u)
    return 0.5 * x * (1.0 + t), t


def _gelu_grad(x, t):
    du = 0.7978845608028654 * (1.0 + 3.0 * 0.044715 * x * x)
    return 0.5 * (1.0 + t) + 0.5 * x * (1.0 - t * t) * du


def _log1p_pos(y):
    return jnp.where(y < 1e-2, y * (1.0 - y * (0.5 - y * (1.0 / 3.0 - y * 0.25))), jnp.log(1.0 + y))


def _softplus(a):
    return jnp.maximum(a, 0.0) + _log1p_pos(jnp.exp(-jnp.abs(a)))


def _neg_expm1(z):
    series = -z * (1.0 + z * (0.5 + z * (1.0 / 6.0 + z * (1.0 / 24.0))))
    return jnp.where(z > -0.02, series, 1.0 - jnp.exp(z))


def _scan_fwd(a, b, row):
    n = a.shape[0]
    d = 1
    while d < n:
        m = row >= d
        b = jnp.where(m, a * pltpu.roll(b, d, 0) + b, b)
        a = jnp.where(m, a * pltpu.roll(a, d, 0), a)
        d *= 2
    return a, b


def _scan_rev(a, b, row):
    n = a.shape[0]
    d = 1
    while d < n:
        m = row < n - d
        b = jnp.where(m, b + a * pltpu.roll(b, n - d, 0), b)
        a = jnp.where(m, a * pltpu.roll(a, n - d, 0), a)
        d *= 2
    return a, b


def _colsum(v):
    return jnp.sum(v, axis=0, keepdims=True)


def _load_gathered(chip, gathered, own, slot, sems):
    copies = []
    for j in range(N_CHIP):
        @pl.when(chip == j)
        def _(j=j):
            pltpu.make_async_copy(own, slot(j), sems.at[j]).start()

        @pl.when(chip != j)
        def _(j=j):
            pltpu.make_async_copy(gathered.at[j], slot(j), sems.at[j]).start()

        copies.append(pltpu.make_async_copy(own, slot(j), sems.at[j]))
    return copies


def _lru_gates(xlb, gab, gbias, sp, first_row):
    g = _dot(xlb, gab) + gbias
    r = jax.nn.sigmoid(g[:, :D_LRU])
    ig = jax.nn.sigmoid(g[:, D_LRU:])
    la = (-C_GATE) * r * sp
    a = jnp.exp(la)
    msq = jnp.sqrt(_neg_expm1(2.0 * la))
    mult = jnp.where(first_row, 1.0, msq)
    return r, ig, a, msq, mult


def _mix_fwd(chip, x, mod, vecd, vecl, win, wout, gab, a64):
    s = x.shape[0]
    ts = MIX_FWD_TILE
    nt = s // ts

    def body(chip_ref, x_ref, mod_ref, vd_ref, vl_ref, win_hbm, win_own, wout_hbm, wout_own, gab_ref, a64_ref,
             hb_ref, proj_ref, hl_ref, ycat_ref, mixed_ref, x1_ref,
             win_ref, wout_ref, ext_lx, ext_cv, hcar, sems):
        i = pl.program_id(0)

        @pl.when(i == 0)
        def _():
            cps = _load_gathered(chip_ref[0], win_hbm, win_own, lambda j: win_ref.at[j], sems.at[pl.ds(0, N_CHIP)])
            cps += _load_gathered(chip_ref[0], wout_hbm, wout_own,
                                  lambda j: wout_ref.at[pl.ds(j * WOUT_BLK, WOUT_BLK), :],
                                  sems.at[pl.ds(N_CHIP, N_CHIP)])
            ext_lx[0:HALO, :] = jnp.zeros((HALO, D_LRU), F32)
            ext_cv[0:HALO, :] = jnp.zeros((HALO, D_LRU), F32)
            hcar[...] = jnp.zeros_like(hcar)
            for cp in cps:
                cp.wait()

        row = lax.broadcasted_iota(jnp.int32, (ts, D_LRU), 0)
        first_row = jnp.logical_and(row == 0, i == 0)
        xt = x_ref[...]
        shift1, scale1, gate1 = mod_ref[0:1, :], mod_ref[1:2, :], mod_ref[2:3, :]
        r1 = lax.rsqrt(jnp.mean(xt * xt, axis=-1, keepdims=True) + EPS)
        h = (xt * r1) * vd_ref[0:1, :] * (1.0 + scale1) + shift1
        hb = h.astype(BF16)
        hb_ref[...] = hb
        for j in range(N_CHIP):
            proj_ref[:, j * WIN_BLK:(j + 1) * WIN_BLK] = _dot(hb, win_ref[j])
        u_ly = proj_ref[:, 512:1024]
        u_b = proj_ref[:, 1024:1536]

        ext_lx[HALO:HALO + ts, :] = proj_ref[:, 0:512]
        xl = vl_ref[4:5, :] + vl_ref[0:1, :] * ext_lx[pl.ds(5, ts), :]
        for k in range(1, 4):
            xl = xl + vl_ref[k:k + 1, :] * ext_lx[pl.ds(5 + k, ts), :]
        ext_lx[0:HALO, :] = ext_lx[ts:ts + HALO, :]
        sp = _softplus(vl_ref[8:9, :])
        _, ig, a, _, mult = _lru_gates(xl.astype(BF16), gab_ref[...], vd_ref[3:4, :], sp, first_row)
        acum, hloc = _scan_fwd(a, mult * (ig * xl), row)
        hl = hloc + acum * hcar[0:1, :]
        hl_ref[...] = hl
        hcar[0:1, :] = hl_ref[ts - 1:ts, :]
        ge, _ = _gelu(u_ly)
        p = ge * hl
        y_lru = p * lax.rsqrt(_gmean(p * p, a64_ref[...]) + EPS) * vl_ref[9:10, :]
        ycat_ref[:, 0:512] = y_lru.astype(BF16)

        ext_cv[HALO:HALO + ts, :] = proj_ref[:, 1536:2048] * proj_ref[:, 2048:2560]
        q = vl_ref[5:6, :] * ext_cv[pl.ds(6, ts), :]
        for k in range(1, 3):
            q = q + vl_ref[5 + k:6 + k, :] * ext_cv[pl.ds(6 + k, ts), :]
        ext_cv[0:HALO, :] = ext_cv[ts:ts + HALO, :]
        yc = u_b * q
        y_conv = yc * lax.rsqrt(_gmean(yc * yc, a64_ref[...]) + EPS) * vl_ref[10:11, :]
        ycat_ref[:, 512:1024] = y_conv.astype(BF16)

        mixed = _dot(ycat_ref[...], wout_ref[...])
        mixed_ref[...] = mixed
        x1_ref[...] = xt + gate1 * mixed

    tile = lambda w: pl.BlockSpec((ts, w), lambda i: (i, 0))
    return pl.pallas_call(
        body, name="mix_fwd", grid=(nt,),
        in_specs=[SMEM, tile(D_MODEL), _full((8, D_MODEL)), _full((8, D_MODEL)), _full((16, D_LRU)),
                  ANY, ANY, ANY, ANY, _full((D_LRU, 2 * D_LRU), True), _full((D_LRU, D_LRU), True)],
        out_specs=[tile(D_MODEL), tile(D_IN), tile(D_LRU), tile(D_MODEL), tile(D_MODEL), tile(D_MODEL)],
        out_shape=[jax.ShapeDtypeStruct((s, D_MODEL), BF16), jax.ShapeDtypeStruct((s, D_IN), F32),
                   jax.ShapeDtypeStruct((s, D_LRU), F32), jax.ShapeDtypeStruct((s, D_MODEL), BF16),
                   jax.ShapeDtypeStruct((s, D_MODEL), F32), jax.ShapeDtypeStruct((s, D_MODEL), F32)],
        scratch_shapes=[pltpu.VMEM((N_CHIP, D_MODEL, WIN_BLK), BF16), pltpu.VMEM((D_MODEL, D_MODEL), BF16),
                        pltpu.VMEM((ts + HALO, D_LRU), F32), pltpu.VMEM((ts + HALO, D_LRU), F32),
                        pltpu.VMEM((HALO, D_LRU), F32), pltpu.SemaphoreType.DMA((2 * N_CHIP,))],
        compiler_params=pltpu.CompilerParams(dimension_semantics=("arbitrary",), vmem_limit_bytes=VMEM_LIMIT),
    )(chip, x, mod, vecd, vecl, *win, *wout, gab, a64)


def _mlp_up(chip, x1, mod, vecd, w1):
    s = x1.shape[0]
    ts = MIX_FWD_TILE
    nt = s // ts

    def body(chip_ref, x1_ref, mod_ref, vd_ref, w1_hbm, w1_own, h2_ref, rz_ref, w1_v, sems):
        @pl.when(pl.program_id(0) == 0)
        def _():
            for cp in _load_gathered(chip_ref[0], w1_hbm, w1_own, lambda j: w1_v.at[j], sems):
                cp.wait()

        xt = x1_ref[...]
        r2 = lax.rsqrt(jnp.mean(xt * xt, axis=-1, keepdims=True) + EPS)
        h2b = ((xt * r2) * vd_ref[1:2, :] * (1.0 + mod_ref[4:5, :]) + mod_ref[3:4, :]).astype(BF16)
        h2_ref[...] = h2b
        for j in range(N_CHIP):
            rz_ref[:, j * FF_BLK:(j + 1) * FF_BLK] = jnp.maximum(_dot(h2b, w1_v[j]), 0.0).astype(BF16)

    tile = lambda w: pl.BlockSpec((ts, w), lambda i: (i, 0))
    return pl.pallas_call(
        body, name="mlp_up", grid=(nt,),
        in_specs=[SMEM, tile(D_MODEL), _full((8, D_MODEL)), _full((8, D_MODEL)), ANY, ANY],
        out_specs=[tile(D_MODEL), tile(D_FF)],
        out_shape=[jax.ShapeDtypeStruct((s, D_MODEL), BF16), jax.ShapeDtypeStruct((s, D_FF), BF16)],
        scratch_shapes=[pltpu.VMEM((N_CHIP, D_MODEL, FF_BLK), BF16), pltpu.SemaphoreType.DMA((N_CHIP,))],
        compiler_params=pltpu.CompilerParams(dimension_semantics=("arbitrary",), vmem_limit_bytes=VMEM_LIMIT),
    )(chip, x1, mod, vecd, *w1)


def _mlp_fwd_bwd(chip, x1, rz, target, mod, vecd, w1, w2):
    s = x1.shape[0]
    ts = TOKEN_TILE
    nt = s // ts

    def body(chip_ref, x1_ref, rz_ref, tg_ref, mod_ref, vd_ref, w1_hbm, w1_own, w2_hbm, w2_own,
             dx1_ref, act_ref, dz_ref, dmo_ref, acc_ref, w1_v, w2_v, sems):
        i = pl.program_id(0)

        @pl.when(i == 0)
        def _():
            cps = _load_gathered(chip_ref[0], w1_hbm, w1_own, lambda j: w1_v.at[j], sems.at[pl.ds(0, N_CHIP)])
            cps += _load_gathered(chip_ref[0], w2_hbm, w2_own, lambda j: w2_v.at[j], sems.at[pl.ds(N_CHIP, N_CHIP)])
            acc_ref[...] = jnp.zeros_like(acc_ref)
            for cp in cps:
                cp.wait()

        xt = x1_ref[...]
        shift2, scale2, gate2 = mod_ref[3:4, :], mod_ref[4:5, :], mod_ref[5:6, :]
        g2, gf = vd_ref[1:2, :], vd_ref[2:3, :]
        r2 = lax.rsqrt(jnp.mean(xt * xt, axis=-1, keepdims=True) + EPS)
        n2 = xt * r2
        mo = jnp.zeros((ts, D_MODEL), F32)
        for j in range(N_CHIP):
            rz = rz_ref[:, j * FF_BLK:(j + 1) * FF_BLK].astype(F32)
            actb = (rz * rz).astype(BF16)
            act_ref[:, j * FF_BLK:(j + 1) * FF_BLK] = actb
            mo = mo + _dot(actb, w2_v[j])
        x2 = xt + gate2 * mo
        r3 = lax.rsqrt(jnp.mean(x2 * x2, axis=-1, keepdims=True) + EPS)
        n3 = x2 * r3
        e = n3 * gf - tg_ref[...]
        loss = (0.5 / D_MODEL) * jnp.sum(_colsum(e * e), axis=1, keepdims=True)
        dy = e * (1.0 / D_MODEL)
        acc_ref[4:5, :] += _colsum(dy * n3)
        acc_ref[5:6, :] += jnp.broadcast_to(loss, (1, D_MODEL))
        dn3 = dy * gf
        dx2 = r3 * (dn3 - n3 * jnp.mean(dn3 * n3, axis=-1, keepdims=True))
        acc_ref[2:3, :] += _colsum(dx2 * mo)
        dmob = (dx2 * gate2).astype(BF16)
        dmo_ref[...] = dmob
        for j in range(N_CHIP):
            dz_ref[:, j * FF_BLK:(j + 1) * FF_BLK] = (
                _dot_nt(dmob, w2_v[j]) * (2.0 * rz_ref[:, j * FF_BLK:(j + 1) * FF_BLK].astype(F32))).astype(BF16)
        dh2 = jnp.zeros((ts, D_MODEL), F32)
        for j in range(N_CHIP):
            dh2 = dh2 + _dot_nt(dz_ref[:, j * FF_BLK:(j + 1) * FF_BLK], w1_v[j])
        acc_ref[1:2, :] += _colsum(dh2 * (n2 * g2))
        acc_ref[0:1, :] += _colsum(dh2)
        dhn2 = dh2 * (1.0 + scale2)
        acc_ref[3:4, :] += _colsum(dhn2 * n2)
        dn2 = dhn2 * g2
        dx1_ref[...] = dx2 + r2 * (dn2 - n2 * jnp.mean(dn2 * n2, axis=-1, keepdims=True))

    tile = lambda w: pl.BlockSpec((ts, w), lambda i: (i, 0))
    return pl.pallas_call(
        body, name="mlp_fwd_bwd", grid=(nt,),
        in_specs=[SMEM, tile(D_MODEL), tile(D_FF), tile(D_MODEL), _full((8, D_MODEL)), _full((8, D_MODEL)),
                  ANY, ANY, ANY, ANY],
        out_specs=[tile(D_MODEL), tile(D_FF), tile(D_FF), tile(D_MODEL), _full((8, D_MODEL))],
        out_shape=[jax.ShapeDtypeStruct((s, D_MODEL), F32), jax.ShapeDtypeStruct((s, D_FF), BF16),
                   jax.ShapeDtypeStruct((s, D_FF), BF16), jax.ShapeDtypeStruct((s, D_MODEL), BF16),
                   jax.ShapeDtypeStruct((8, D_MODEL), F32)],
        scratch_shapes=[pltpu.VMEM((N_CHIP, D_MODEL, FF_BLK), BF16), pltpu.VMEM((N_CHIP, FF_BLK, D_MODEL), BF16),
                        pltpu.SemaphoreType.DMA((2 * N_CHIP,))],
        compiler_params=pltpu.CompilerParams(dimension_semantics=("arbitrary",), vmem_limit_bytes=VMEM_LIMIT),
    )(chip, x1, rz, target, mod, vecd, *w1, *w2)


def _mix_bwd(chip, dx1, x, mixed, proj, hl, hb, ycat, mod, vecd, vecl, win, wout, gab, a64):
    s = x.shape[0]
    ts = TOKEN_TILE
    nt = s // ts
    hpt = ts // HALO

    def body(chip_ref, dx1_ref, x_ref, mixed_ref, proj_ref, projh_ref, hl_ref, hlh_ref, hb_ref, ycat_ref,
             mod_ref, vd_ref, vl_ref, win_hbm, win_own, wout_hbm, wout_own, gab_ref, a64_ref,
             gx_ref, accd_ref, accl_ref, gwin_hbm, gwout_hbm, ggate_hbm,
             win_ref, wout_ref, dproj_ref, dgb_ref, gwin_acc, gwout_acc, ggate_acc,
             ext_lx, ext_cv, ext_hl, ext_dxl, ext_dq, gbuf, gcar, acar, sems):
        i = pl.program_id(0)
        ri = nt - 1 - i

        @pl.when(i == 0)
        def _():
            gwin_acc[...] = jnp.zeros_like(gwin_acc)
            gwout_acc[...] = jnp.zeros_like(gwout_acc)
            ggate_acc[...] = jnp.zeros_like(ggate_acc)
            cps = _load_gathered(chip_ref[0], win_hbm, win_own, lambda j: win_ref.at[j], sems.at[pl.ds(0, N_CHIP)])
            cps += _load_gathered(chip_ref[0], wout_hbm, wout_own,
                                  lambda j: wout_ref.at[pl.ds(j * WOUT_BLK, WOUT_BLK), :],
                                  sems.at[pl.ds(N_CHIP, N_CHIP)])
            for cp in cps:
                cp.wait()
            accd_ref[...] = jnp.zeros_like(accd_ref)
            accl_ref[...] = jnp.zeros_like(accl_ref)
            ext_dxl[ts:ts + HALO, :] = jnp.zeros((HALO, D_LRU), F32)
            ext_dq[ts:ts + HALO, :] = jnp.zeros((HALO, D_LRU), F32)
            gcar[...] = jnp.zeros_like(gcar)
            acar[...] = jnp.zeros_like(acar)

        row = lax.broadcasted_iota(jnp.int32, (ts, D_LRU), 0)
        first_row = jnp.logical_and(row == 0, ri == 0)
        halo_on = jnp.where(ri == 0, 0.0, 1.0)
        shift1, scale1, gate1 = mod_ref[0:1, :], mod_ref[1:2, :], mod_ref[2:3, :]
        g1 = vd_ref[0:1, :]
        a64m = a64_ref[...]
        lg, cg = vl_ref[9:10, :], vl_ref[10:11, :]

        dx1 = dx1_ref[...]
        accd_ref[2:3, :] += _colsum(dx1 * mixed_ref[...])
        dmb = (dx1 * gate1).astype(BF16)
        gwout_acc[...] += _dot_tn(ycat_ref[...], dmb)
        dycat = _dot_nt(dmb, wout_ref[...])
        dyl = dycat[:, 0:512]
        dyv = dycat[:, 512:1024]

        u_ly = proj_ref[:, 512:1024]
        u_b = proj_ref[:, 1024:1536]
        u_c = proj_ref[:, 1536:2048]
        u_v = proj_ref[:, 2048:2560]
        ext_lx[0:HALO, :] = projh_ref[:, 0:512] * halo_on
        ext_lx[HALO:HALO + ts, :] = proj_ref[:, 0:512]
        xl = vl_ref[4:5, :] + vl_ref[0:1, :] * ext_lx[pl.ds(5, ts), :]
        for k in range(1, 4):
            xl = xl + vl_ref[k:k + 1, :] * ext_lx[pl.ds(5 + k, ts), :]
        xlb = xl.astype(BF16)
        sp = _softplus(vl_ref[8:9, :])
        r, ig, a, msq, mult = _lru_gates(xlb, gab_ref[...], vd_ref[3:4, :], sp, first_row)
        hl = hl_ref[...]
        ge, th = _gelu(u_ly)
        p = ge * hl
        rl = lax.rsqrt(_gmean(p * p, a64m) + EPS)
        nl = p * rl
        ext_cv[0:HALO, :] = projh_ref[:, 1536:2048] * projh_ref[:, 2048:2560] * halo_on
        ext_cv[HALO:HALO + ts, :] = u_c * u_v
        q = vl_ref[5:6, :] * ext_cv[pl.ds(6, ts), :]
        for k in range(1, 3):
            q = q + vl_ref[5 + k:6 + k, :] * ext_cv[pl.ds(6 + k, ts), :]
        yc = u_b * q
        rc = lax.rsqrt(_gmean(yc * yc, a64m) + EPS)
        nc = yc * rc

        accl_ref[9:10, :] += _colsum(dyl * nl)
        dnl = dyl * lg
        dp = rl * (dnl - nl * _gmean(dnl * nl, a64m))
        dproj_ref[:, 512:1024] = ((dp * hl) * _gelu_grad(u_ly, th)).astype(BF16)
        a_next = jnp.where(row == ts - 1, acar[0:1, :], pltpu.roll(a, ts - 1, 0))
        acum, gloc = _scan_rev(a_next, dp * ge, row)
        gbuf[...] = gloc + acum * gcar[0:1, :]
        gcar[0:1, :] = gbuf[0:1, :]
        ext_hl[0:HALO, :] = hlh_ref[...] * halo_on
        ext_hl[HALO:HALO + ts, :] = hl
        acar[...] = a[0:HALO, :]
        gt = gbuf[...]
        da = gt * ext_hl[pl.ds(HALO - 1, ts), :]
        dmult = gt * ig * xl
        di = gt * mult * xl
        dxl = gt * mult * ig
        dla = da * a - jnp.where(first_row, 0.0, dmult * a * a / msq)
        accl_ref[8:9, :] += _colsum(dla * ((-C_GATE) * r))
        dra = dla * ((-C_GATE) * sp) * r * (1.0 - r)
        dia = di * ig * (1.0 - ig)
        accd_ref[4:5, 0:D_LRU] += _colsum(dra)
        accd_ref[4:5, D_LRU:2 * D_LRU] += _colsum(dia)
        dgb_ref[:, 0:D_LRU] = dra.astype(BF16)
        dgb_ref[:, D_LRU:2 * D_LRU] = dia.astype(BF16)
        dxl = dxl + _dot_nt(dgb_ref[...], gab_ref[...])
        ggate_acc[...] += _dot_tn(xlb, dgb_ref[...])
        accl_ref[4:5, :] += _colsum(dxl)
        for k in range(4):
            accl_ref[k:k + 1, :] += _colsum(dxl * ext_lx[pl.ds(5 + k, ts), :])
        ext_dxl[0:ts, :] = dxl
        du_lx = vl_ref[0:1, :] * ext_dxl[pl.ds(3, ts), :]
        for k in range(1, 4):
            du_lx = du_lx + vl_ref[k:k + 1, :] * ext_dxl[pl.ds(3 - k, ts), :]
        ext_dxl[ts:ts + HALO, :] = ext_dxl[0:HALO, :]
        dproj_ref[:, 0:512] = du_lx.astype(BF16)

        accl_ref[10:11, :] += _colsum(dyv * nc)
        dnc = dyv * cg
        dyc = rc * (dnc - nc * _gmean(dnc * nc, a64m))
        dproj_ref[:, 1024:1536] = (dyc * q).astype(BF16)
        dq = dyc * u_b
        for k in range(3):
            accl_ref[5 + k:6 + k, :] += _colsum(dq * ext_cv[pl.ds(6 + k, ts), :])
        ext_dq[0:ts, :] = dq
        dcv = vl_ref[5:6, :] * ext_dq[pl.ds(2, ts), :]
        for k in range(1, 3):
            dcv = dcv + vl_ref[5 + k:6 + k, :] * ext_dq[pl.ds(2 - k, ts), :]
        ext_dq[ts:ts + HALO, :] = ext_dq[0:HALO, :]
        dproj_ref[:, 1536:2048] = (dcv * u_v).astype(BF16)
        dproj_ref[:, 2048:2560] = (dcv * u_c).astype(BF16)

        dh = _dot_nt(dproj_ref[:, 0:WIN_BLK], win_ref[0])
        for j in range(1, N_CHIP):
            dh = dh + _dot_nt(dproj_ref[:, j * WIN_BLK:(j + 1) * WIN_BLK], win_ref[j])
        for j in range(N_CHIP):
            gwin_acc[j] += _dot_tn(hb_ref[...], dproj_ref[:, j * WIN_BLK:(j + 1) * WIN_BLK])
        xt = x_ref[...]
        r1 = lax.rsqrt(jnp.mean(xt * xt, axis=-1, keepdims=True) + EPS)
        n1 = xt * r1
        accd_ref[1:2, :] += _colsum(dh * (n1 * g1))
        accd_ref[0:1, :] += _colsum(dh)
        dhn1 = dh * (1.0 + scale1)
        accd_ref[3:4, :] += _colsum(dhn1 * n1)
        dn1 = dhn1 * g1
        gx_ref[...] = dx1 + r1 * (dn1 - n1 * jnp.mean(dn1 * n1, axis=-1, keepdims=True))

        @pl.when(i == nt - 1)
        def _():
            outs = [pltpu.make_async_copy(acc, dst, sems.at[k]) for k, (acc, dst) in enumerate(
                ((gwin_acc, gwin_hbm), (gwout_acc, gwout_hbm), (ggate_acc, ggate_hbm)))]
            for cp in outs:
                cp.start()
            for cp in outs:
                cp.wait()

    tile = lambda w: pl.BlockSpec((ts, w), lambda i: (nt - 1 - i, 0))
    halo = lambda w: pl.BlockSpec((HALO, w), lambda i: (jnp.maximum((nt - 1 - i) * hpt - 1, 0), 0))
    ext = pltpu.VMEM((ts + HALO, D_LRU), F32)
    return pl.pallas_call(
        body, name="mix_bwd", grid=(nt,),
        in_specs=[SMEM, tile(D_MODEL), tile(D_MODEL), tile(D_MODEL), tile(D_IN), halo(D_IN), tile(D_LRU), halo(D_LRU),
                  tile(D_MODEL), tile(D_MODEL), _full((8, D_MODEL)), _full((8, D_MODEL)), _full((16, D_LRU)),
                  ANY, ANY, ANY, ANY, _full((D_LRU, 2 * D_LRU), True), _full((D_LRU, D_LRU), True)],
        out_specs=[tile(D_MODEL), _full((8, D_MODEL)), _full((16, D_LRU)), ANY, ANY, ANY],
        out_shape=[jax.ShapeDtypeStruct((s, D_MODEL), F32),
                   jax.ShapeDtypeStruct((8, D_MODEL), F32), jax.ShapeDtypeStruct((16, D_LRU), F32),
                   jax.ShapeDtypeStruct((N_CHIP, D_MODEL, WIN_BLK), F32), jax.ShapeDtypeStruct((D_MODEL, D_MODEL), F32),
                   jax.ShapeDtypeStruct((D_LRU, 2 * D_LRU), F32)],
        scratch_shapes=[pltpu.VMEM((N_CHIP, D_MODEL, WIN_BLK), BF16), pltpu.VMEM((D_MODEL, D_MODEL), BF16),
                        pltpu.VMEM((ts, D_IN), BF16), pltpu.VMEM((ts, 2 * D_LRU), BF16),
                        pltpu.VMEM((N_CHIP, D_MODEL, WIN_BLK), F32), pltpu.VMEM((D_MODEL, D_MODEL), F32),
                        pltpu.VMEM((D_LRU, 2 * D_LRU), F32),
                        ext, ext, ext, ext, ext, pltpu.VMEM((ts, D_LRU), F32),
                        pltpu.VMEM((HALO, D_LRU), F32), pltpu.VMEM((HALO, D_LRU), F32),
                        pltpu.SemaphoreType.DMA((2 * N_CHIP,))],
        compiler_params=pltpu.CompilerParams(dimension_semantics=("arbitrary",), vmem_limit_bytes=VMEM_LIMIT),
    )(chip, dx1, x, mixed, proj, proj, hl, hl, hb, ycat, mod, vecd, vecl, *win, *wout, gab, a64)


def _wgrad_mlp(collective_id, h2b, dz, act, dmo):
    s = h2b.shape[0]
    nstep = 2 * N_CHIP
    half = FF_BLK // 2

    def body(h2_ref, dz_ref, act_ref, dmo_ref, p1_hbm, p2_hbm, buf, landed, summed, send_sems, recv_sems, out_sems):
        j = pl.program_id(0)
        x, y, c, _ = _position()

        def give(jj):
            return pltpu.make_async_remote_copy(
                src_ref=buf.at[jj % 2, pl.ds((1 - c) * half, half), :], dst_ref=landed.at[jj],
                send_sem=send_sems.at[jj % 2], recv_sem=recv_sems.at[jj],
                device_id=(x, y, 1 - c), device_id_type=MESH)

        def write_out(jj, dst):
            return pltpu.make_async_copy(summed.at[jj % 2], dst, out_sems.at[jj % 2])

        def add_sibling(jj):
            give(jj).wait_recv()
            own = buf[jj % 2, pl.ds(pl.multiple_of(c * half, half), half), :]
            summed[jj % 2] = (own.astype(F32) + landed[jj].astype(F32)).astype(BF16)

        @pl.when(j == 0)
        def _():
            pl.semaphore_signal(pltpu.get_barrier_semaphore(), inc=1, device_id=(x, y, 1 - c), device_id_type=MESH)

        @pl.when(j >= 2)
        def _():
            give(j - 2).wait_send()

        @pl.when(j < N_CHIP)
        def _():
            buf[j % 2] = _dot_tn(act_ref[...], dmo_ref[...]).astype(BF16)

        @pl.when(j >= N_CHIP)
        def _():
            buf[j % 2] = _dot_tn(h2_ref[...], dz_ref[...]).astype(BF16)

        @pl.when(j == 0)
        def _():
            pl.semaphore_wait(pltpu.get_barrier_semaphore(), 1)

        give(j).start()

        @pl.when(j >= 1)
        def _():
            jm = j - 1

            @pl.when(jm >= 2)
            def _():
                write_out(jm - 2, p2_hbm.at[0]).wait()

            add_sibling(jm)

            @pl.when(jm < N_CHIP)
            def _():
                write_out(jm, p2_hbm.at[jm]).start()

            @pl.when(jm >= N_CHIP)
            def _():
                write_out(jm, p1_hbm.at[jm - N_CHIP]).start()

        @pl.when(j == nstep - 1)
        def _():
            last = nstep - 1
            write_out(last - 2, p1_hbm.at[0]).wait()
            add_sibling(last)
            write_out(last, p1_hbm.at[N_CHIP - 1]).start()
            for jj in (last - 1, last):
                give(jj).wait_send()
                write_out(jj, p1_hbm.at[0]).wait()

    sds = jax.ShapeDtypeStruct((N_CHIP, half, D_MODEL), BF16)
    whole = pl.BlockSpec((s, D_MODEL), lambda j: (0, 0))
    return pl.pallas_call(
        body, name="wgrad_mlp", grid=(nstep,),
        in_specs=[whole, pl.BlockSpec((s, FF_BLK), lambda j: (0, jnp.maximum(j - N_CHIP, 0))),
                  pl.BlockSpec((s, FF_BLK), lambda j: (0, jnp.minimum(j, N_CHIP - 1))), whole],
        out_specs=[ANY, ANY], out_shape=[sds, sds],
        scratch_shapes=[pltpu.VMEM((2, FF_BLK, D_MODEL), BF16), pltpu.VMEM((nstep, half, D_MODEL), BF16),
                        pltpu.VMEM((2, half, D_MODEL), BF16), pltpu.SemaphoreType.DMA((2,)),
                        pltpu.SemaphoreType.DMA((nstep,)), pltpu.SemaphoreType.DMA((2,))],
        compiler_params=pltpu.CompilerParams(dimension_semantics=("arbitrary",), vmem_limit_bytes=VMEM_LIMIT,
                                             collective_id=collective_id),
    )(h2b, dz, act, dmo)


def _mod_matmul(c_all, ada_w_loc):
    n = ada_w_loc.shape[1]
    cb = 512

    def body(c_ref, w_ref, o_ref):
        c = c_ref[...]
        sc = c * jax.nn.sigmoid(c)
        o_ref[...] = _dot(sc.astype(BF16), w_ref[...].astype(BF16))

    return pl.pallas_call(
        body, name="mod_matmul", grid=(n // cb,),
        in_specs=[_full((8, D_MODEL)), pl.BlockSpec((D_MODEL, cb), lambda j: (0, j))],
        out_specs=pl.BlockSpec((8, cb), lambda j: (0, j)),
        out_shape=jax.ShapeDtypeStruct((8, n), F32),
        compiler_params=pltpu.CompilerParams(dimension_semantics=("arbitrary",), vmem_limit_bytes=VMEM_LIMIT),
    )(c_all, ada_w_loc)


def _adam_math(w, g, m, v):
    m = ADAM_B1 * m + (1.0 - ADAM_B1) * g
    v = ADAM_B2 * v + (1.0 - ADAM_B2) * (g * g)
    m_hat = m / (1.0 - ADAM_B1 ** ADAM_STEP)
    v_hat = v / (1.0 - ADAM_B2 ** ADAM_STEP)
    delta = (-ADAM_LR) * (m_hat / (jnp.sqrt(v_hat) + ADAM_EPS) + ADAM_WD * w)
    return delta, m, v


def _adam(name, core, shards):
    n = len(shards)
    r, c = shards[0][0].shape
    half = r // 2
    rb = min(half, 128)
    nh = half // rb

    def body(core_ref, *refs):
        ins, outs = refs[:5 * n], refs[5 * n:]
        mine = (pl.program_id(0) // nh) == core_ref[0]
        for k in range(n):
            w_ref, go_ref, gs_ref, m_ref, v_ref = ins[5 * k:5 * k + 5]
            g_ref, d_ref, mo_ref, vo_ref = outs[4 * k:4 * k + 4]
            g = jnp.where(mine, go_ref[...], gs_ref[...])
            g_ref[...] = g
            d_ref[...], mo_ref[...], vo_ref[...] = _adam_math(w_ref[...], g, m_ref[...], v_ref[...])

    spec = pl.BlockSpec((rb, c), lambda i, core_ref: (i, 0))
    own = pl.BlockSpec((rb, c), lambda i, core_ref: (jnp.where(i // nh == core_ref[0], i % nh, 0), 0))
    sib = pl.BlockSpec((rb, c), lambda i, core_ref: (jnp.where(i // nh == core_ref[0], 0, i % nh), 0))
    sds = jax.ShapeDtypeStruct((r, c), F32)
    res = pl.pallas_call(
        body, name=name,
        grid_spec=pltpu.PrefetchScalarGridSpec(
            num_scalar_prefetch=1, grid=(r // rb,),
            in_specs=[spec, own, sib, spec, spec] * n, out_specs=[spec] * (4 * n)),
        out_shape=[sds] * (4 * n),
        compiler_params=pltpu.CompilerParams(dimension_semantics=("arbitrary",), vmem_limit_bytes=VMEM_LIMIT),
    )(core, *[t for s in shards for t in s])
    return [res[4 * k:4 * k + 4] for k in range(n)]


def _ada_grad_adam(chip, sct, dmod_cols, w, m, v):
    r, c = w.shape
    rb = 256

    def body(chip_ref, s_ref, dm_ref, w_ref, m_ref, v_ref, g_ref, d_ref, mo_ref, vo_ref):
        g = s_ref[:, 0:1] * dm_ref[0:1, :]
        for b in range(1, 8):
            g = g + s_ref[:, b:b + 1] * dm_ref[b:b + 1, :]
        g_ref[...] = g
        d_ref[...], mo_ref[...], vo_ref[...] = _adam_math(w_ref[...], g, m_ref[...], v_ref[...])

    spec = pl.BlockSpec((rb, c), lambda i, chip_ref: (i, 0))
    sds = jax.ShapeDtypeStruct((r, c), F32)
    return pl.pallas_call(
        body, name="ada_grad_adam",
        grid_spec=pltpu.PrefetchScalarGridSpec(
            num_scalar_prefetch=1, grid=(r // rb,),
            in_specs=[pl.BlockSpec((rb, 8), lambda i, chip_ref: (i, 0)),
                      pl.BlockSpec((8, c), lambda i, chip_ref: (0, chip_ref[0])), spec, spec, spec],
            out_specs=[spec] * 4),
        out_shape=[sds] * 4,
        compiler_params=pltpu.CompilerParams(dimension_semantics=("arbitrary",), vmem_limit_bytes=VMEM_LIMIT),
    )(chip, sct, dmod_cols, w, m, v)


def _position():
    x, y, c = lax.axis_index("x"), lax.axis_index("y"), lax.axis_index("c")
    chips = [(1 - x, y), (x, 1 - y), (1 - x, 1 - y)]
    return x, y, c, chips


def _ag8_run(ins, outs, send_sems, recv_sems, local_sems):
    na = len(ins)
    x, y, c, chips = _position()
    me, sibling = (x, y, c), (x, y, 1 - c)
    first, passed, local = [], [], []
    for a in range(na):
        m_per = ins[a].shape[0]

        def rows(px, py, pc, a=a, m_per=m_per):
            return outs[a].at[pl.ds((4 * px + 2 * py + pc) * m_per, m_per), :]

        def copy(k, block, to, src=None, a=a, rows=rows):
            return pltpu.make_async_remote_copy(
                src_ref=rows(*block) if src is None else src, dst_ref=rows(*block),
                send_sem=send_sems.at[7 * a + k], recv_sem=recv_sems.at[7 * a + k],
                device_id=to, device_id_type=MESH)

        mine = pltpu.make_async_copy(ins[a], rows(*me), local_sems.at[a])
        mine.start()
        local.append(mine)
        f = [copy(0, me, sibling, src=ins[a])]
        f += [copy(1 + j, me, (*chip, c), src=ins[a]) for j, chip in enumerate(chips)]
        for cp in f:
            cp.start()
        first.append((f, copy))
    for a in range(na):
        f, copy = first[a]
        p = [copy(4 + j, (*chip, c), sibling) for j, chip in enumerate(chips)]
        for j, chip in enumerate(chips):
            copy(1 + j, (*chip, c), me).wait_recv()
            p[j].start()
        passed.append(p)
    for a in range(na):
        f, copy = first[a]
        copy(0, sibling, me).wait_recv()
        for j, chip in enumerate(chips):
            copy(4 + j, (*chip, 1 - c), me).wait_recv()
        for cp in f + passed[a]:
            cp.wait_send()
        local[a].wait()


def _allgather8_seq(name, collective_id, arrs):
    na = len(arrs)
    hbm = pltpu.MemorySpace.HBM
    ins = [jax.new_ref(a, memory_space=hbm) for a in arrs]
    outs = [jax.empty_ref(jax.ShapeDtypeStruct((8 * a.shape[0], a.shape[1]), a.dtype), memory_space=hbm) for a in arrs]

    @pl.kernel(mesh=plsc.ScalarSubcoreMesh(axis_name="sequencer", num_cores=1), name=name,
               scratch_types=(pltpu.SemaphoreType.DMA((7 * na,)), pltpu.SemaphoreType.DMA((7 * na,)),
                              pltpu.SemaphoreType.DMA((na,))),
               compiler_params=pltpu.CompilerParams(collective_id=collective_id))
    def launch(send_sems, recv_sems, local_sems):
        x, y, c, chips = _position()
        peers = [(x, y, 1 - c)] + [(*chip, c) for chip in chips]
        barrier = pltpu.get_barrier_semaphore()
        for peer in peers:
            pl.semaphore_signal(barrier, inc=1, device_id=peer, device_id_type=MESH)
        pl.semaphore_wait(barrier, len(peers))
        _ag8_run(ins, outs, send_sems, recv_sems, local_sems)

    launch()
    return [o[...] for o in outs]


def _allgather8(name, arrs):
    na = len(arrs)

    def body(*refs):
        _ag8_run(refs[:na], refs[na:2 * na], *refs[2 * na:])

    return pl.pallas_call(
        body, name=name,
        out_shape=[jax.ShapeDtypeStruct((8 * a.shape[0], a.shape[1]), a.dtype) for a in arrs],
        in_specs=[VMEM] * na, out_specs=[VMEM] * na,
        scratch_shapes=[pltpu.SemaphoreType.DMA((7 * na,)), pltpu.SemaphoreType.DMA((7 * na,)),
                        pltpu.SemaphoreType.DMA((na,))],
        compiler_params=pltpu.CompilerParams(vmem_limit_bytes=VMEM_LIMIT),
    )(*arrs)


AG_SEMS = 7
AG_CHUNKS = 2


def _ag_copies(ins, outs, send_sems, recv_sems):
    x, y, c, chips = _position()
    sibling = (x, y, 1 - c)
    xn, yn, dg = [2 * chip[0] + chip[1] for chip in chips]
    to_x, to_y = (1 - x, y, c), (x, 1 - y, c)
    res = []
    for a in range(len(ins)):
        half = ins[a].shape[0] // 2
        piece = half // AG_CHUNKS
        for p in range(AG_CHUNKS):
            def copy(k, dst, to, src=None, base=AG_SEMS * (AG_CHUNKS * a + p)):
                return pltpu.make_async_remote_copy(
                    src_ref=dst if src is None else src, dst_ref=dst,
                    send_sem=send_sems.at[base + k], recv_sem=recv_sems.at[base + k],
                    device_id=to, device_id_type=MESH)

            def rows(chip, pc, q=None, a=a, start=p * piece, half=half, piece=piece):
                if q is None:
                    return outs[a].at[chip, pl.ds(pc * half + start, piece), :]
                return outs[a].at[chip, pl.ds(pc * half + start + q * (piece // 2), piece // 2), :]

            own = ins[a].at[pl.ds(c * half + p * piece, piece), :]
            mine = rows(2 * x + y, c)
            res.append(dict(
                sends=[copy(0, mine, to_x, src=own), copy(1, mine, to_y, src=own)],
                from_x=copy(0, rows(xn, c), to_x), from_y=copy(1, rows(yn, c), to_y),
                relay_y=copy(2, rows(xn, c, 0), to_y), relay_x=copy(3, rows(yn, c, 1), to_x),
                from_y_relay=copy(2, rows(dg, c, 0), to_y), from_x_relay=copy(3, rows(dg, c, 1), to_x),
                pass_on=[copy(4, rows(xn, c), sibling), copy(5, rows(yn, c), sibling), copy(6, rows(dg, c), sibling)],
                from_sibling=[copy(4, rows(xn, 1 - c), sibling), copy(5, rows(yn, 1 - c), sibling),
                              copy(6, rows(dg, 1 - c), sibling)]))
    return res


def _ag_start(ins, outs, send_sems, recv_sems):
    for cps in _ag_copies(ins, outs, send_sems, recv_sems):
        for cp in cps["sends"]:
            cp.start()


def _ag_relay(ins, outs, send_sems, recv_sems):
    for cps in _ag_copies(ins, outs, send_sems, recv_sems):
        cps["from_x"].wait_recv()
        cps["relay_y"].start()
        cps["pass_on"][0].start()
        cps["from_y"].wait_recv()
        cps["relay_x"].start()
        cps["pass_on"][1].start()


def _ag_complete(ins, outs, send_sems, recv_sems):
    copies = _ag_copies(ins, outs, send_sems, recv_sems)
    for cps in copies:
        cps["from_y_relay"].wait_recv()
        cps["from_x_relay"].wait_recv()
        cps["pass_on"][2].start()
    for cps in copies:
        for cp in cps["from_sibling"]:
            cp.wait_recv()
        for cp in cps["sends"] + [cps["relay_y"], cps["relay_x"]] + cps["pass_on"]:
            cp.wait_send()


def _ag_finish(ins, outs, send_sems, recv_sems):
    _ag_relay(ins, outs, send_sems, recv_sems)
    _ag_complete(ins, outs, send_sems, recv_sems)


def _allgather_weights(name, collective_id, shards):
    na = len(shards)
    hbm = pltpu.MemorySpace.HBM
    ins = [jax.new_ref(s, memory_space=hbm) for s in shards]
    outs = [jax.empty_ref(jax.ShapeDtypeStruct((N_CHIP,) + s.shape, s.dtype), memory_space=hbm) for s in shards]

    @pl.kernel(mesh=plsc.ScalarSubcoreMesh(axis_name="sequencer", num_cores=1), name=name,
               scratch_types=(pltpu.SemaphoreType.DMA((AG_SEMS * AG_CHUNKS * na,)),
                              pltpu.SemaphoreType.DMA((AG_SEMS * AG_CHUNKS * na,))),
               compiler_params=pltpu.CompilerParams(collective_id=collective_id))
    def launch(send_sems, recv_sems):
        x, y, c, _ = _position()
        peers = [(1 - x, y, c), (x, 1 - y, c), (x, y, 1 - c)]
        barrier = pltpu.get_barrier_semaphore()
        for peer in peers:
            pl.semaphore_signal(barrier, inc=1, device_id=peer, device_id_type=MESH)
        pl.semaphore_wait(barrier, len(peers))
        _ag_start(ins, outs, send_sems, recv_sems)
        _ag_finish(ins, outs, send_sems, recv_sems)

    launch()
    return [o[...] for o in outs]


def _sibling_swap(name, arrs, split_rows, collective_id=None):
    na = len(arrs)
    shapes = [jax.ShapeDtypeStruct((a.shape[0], a.shape[1] // 2, a.shape[2]) if split_rows else a.shape, a.dtype)
              for a in arrs]

    def run(ins, outs, send_sems, recv_sems):
        x, y, c, _ = _position()
        cps = []
        for a in range(na):
            src = ins[a]
            if split_rows:
                half = src.shape[1] // 2
                src = src.at[:, pl.ds((1 - c) * half, half), :]
            cp = pltpu.make_async_remote_copy(
                src_ref=src, dst_ref=outs[a], send_sem=send_sems.at[a], recv_sem=recv_sems.at[a],
                device_id=(x, y, 1 - c), device_id_type=MESH)
            cp.start()
            cps.append(cp)
        for cp in cps:
            cp.wait()

    sems = (pltpu.SemaphoreType.DMA((na,)), pltpu.SemaphoreType.DMA((na,)))
    if collective_id is None:
        return pl.pallas_call(
            lambda *refs: run(refs[:na], refs[na:2 * na], *refs[2 * na:]), name=name, out_shape=shapes,
            in_specs=[ANY] * na, out_specs=[ANY] * na, scratch_shapes=list(sems))(*arrs)

    hbm = pltpu.MemorySpace.HBM
    ins = [jax.new_ref(a, memory_space=hbm) for a in arrs]
    outs = [jax.empty_ref(s, memory_space=hbm) for s in shapes]

    @pl.kernel(mesh=plsc.ScalarSubcoreMesh(axis_name="sequencer", num_cores=1), name=name, scratch_types=sems,
               compiler_params=pltpu.CompilerParams(collective_id=collective_id))
    def launch(send_sems, recv_sems):
        x, y, c, _ = _position()
        barrier = pltpu.get_barrier_semaphore()
        pl.semaphore_signal(barrier, inc=1, device_id=(x, y, 1 - c), device_id_type=MESH)
        pl.semaphore_wait(barrier, 1)
        run(ins, outs, send_sems, recv_sems)

    launch()
    return [o[...] for o in outs]


def _xchg_copies(ins, outs, send_sems, recv_sems):
    x, y, c, chips = _position()
    return [pltpu.make_async_remote_copy(
        src_ref=ins[a].at[2 * chip[0] + chip[1]], dst_ref=outs[a].at[j],
        send_sem=send_sems.at[3 * a + j], recv_sem=recv_sems.at[3 * a + j],
        device_id=(*chip, c), device_id_type=MESH) for a in range(len(ins)) for j, chip in enumerate(chips)]


def _exchange_chips(name, collective_id, parts):
    na = len(parts)
    hbm = pltpu.MemorySpace.HBM
    ins = [jax.new_ref(p, memory_space=hbm) for p in parts]
    outs = [jax.empty_ref(jax.ShapeDtypeStruct((3,) + p.shape[1:], p.dtype), memory_space=hbm) for p in parts]

    @pl.kernel(mesh=plsc.ScalarSubcoreMesh(axis_name="sequencer", num_cores=1), name=name,
               scratch_types=(pltpu.SemaphoreType.DMA((3 * na,)), pltpu.SemaphoreType.DMA((3 * na,))),
               compiler_params=pltpu.CompilerParams(collective_id=collective_id))
    def launch(send_sems, recv_sems):
        x, y, c, chips = _position()
        barrier = pltpu.get_barrier_semaphore()
        for chip in chips:
            pl.semaphore_signal(barrier, inc=1, device_id=(*chip, c), device_id_type=MESH)
        pl.semaphore_wait(barrier, len(chips))
        for cp in _xchg_copies(ins, outs, send_sems, recv_sems):
            cp.start()
        for cp in _xchg_copies(ins, outs, send_sems, recv_sems):
            cp.wait()

    launch()
    return [q[...] for q in outs]


def _add_sibling(name, grad, recv, core, after=()):
    _, r, c = grad.shape
    half = r // 2
    rb = half
    nrb = half // rb

    def body(core_ref, g_ref, r_ref, *refs):
        refs[-1][...] = (g_ref[...].astype(F32) + r_ref[...].astype(F32)).astype(BF16)

    return pl.pallas_call(
        body, name=name,
        grid_spec=pltpu.PrefetchScalarGridSpec(
            num_scalar_prefetch=1, grid=(N_CHIP, nrb),
            in_specs=[pl.BlockSpec((1, rb, c), lambda j, i, core_ref: (j, core_ref[0] * nrb + i, 0)),
                      pl.BlockSpec((1, rb, c), lambda j, i, core_ref: (j, i, 0))] + [ANY] * len(after),
            out_specs=pl.BlockSpec((1, rb, c), lambda j, i, core_ref: (j, i, 0))),
        out_shape=jax.ShapeDtypeStruct((N_CHIP, half, c), BF16),
        compiler_params=pltpu.CompilerParams(dimension_semantics=("arbitrary", "arbitrary"),
                                             vmem_limit_bytes=VMEM_LIMIT),
    )(core, grad, recv, *after)


def _add_chips(name, chip, p, q, after=()):
    _, half, c = q.shape
    rb = min(half, 256)

    def body(chip_ref, p_ref, q_ref, *refs):
        acc = p_ref[0].astype(F32)
        for j in range(3):
            acc = acc + q_ref[j].astype(F32)
        refs[-1][...] = acc

    return pl.pallas_call(
        body, name=name,
        grid_spec=pltpu.PrefetchScalarGridSpec(
            num_scalar_prefetch=1, grid=(half // rb,),
            in_specs=[pl.BlockSpec((1, rb, c), lambda i, chip_ref: (chip_ref[0], i, 0)),
                      pl.BlockSpec((3, rb, c), lambda i, chip_ref: (0, i, 0))] + [ANY] * len(after),
            out_specs=pl.BlockSpec((rb, c), lambda i, chip_ref: (i, 0))),
        out_shape=jax.ShapeDtypeStruct((half, c), F32),
        compiler_params=pltpu.CompilerParams(dimension_semantics=("arbitrary",), vmem_limit_bytes=VMEM_LIMIT),
    )(chip, p, q, *after)


def _small_update(gad, gam, gl, gg, mychip, params):
    names = ["ada_b", "norm1_g", "lru_conv_b", "gate_a_w", "gate_a_b", "gate_x_w", "gate_x_b", "a_param",
             "lru_conv_w", "short_conv_w", "lru_out_g", "conv_out_g", "norm2_g", "final_g"]
    flat = [t for n in names for t in params[n]]
    nin = len(flat)

    def body(chip_ref, gad_ref, gam_ref, gl_ref, gg_ref, *refs):
        ins = {n: refs[3 * k:3 * k + 3] for k, n in enumerate(names)}
        outs = {n: refs[nin + 4 * k:nin + 4 * k + 4] for k, n in enumerate(names)}
        loss_ref, dmod_ref = refs[nin + 4 * len(names):nin + 4 * len(names) + 2]

        def dsum(ref, lo, n):
            per = ref.shape[0] // 8
            acc = ref[lo:lo + n, :].astype(F32)
            for dev in range(1, 8):
                acc = acc + ref[dev * per + lo:dev * per + lo + n, :].astype(F32)
            return acc

        def update(n, g):
            w_ref, m_ref, v_ref = ins[n]
            g_ref, d_ref, mo_ref, vo_ref = outs[n]
            g = g.reshape(w_ref.shape)
            g_ref[...] = g
            d_ref[...], mo_ref[...], vo_ref[...] = _adam_math(w_ref[...], g, m_ref[...], v_ref[...])

        d, dm, l, lw = refs[-4:]
        d[...] = dsum(gad_ref, 0, 8)
        dm[...] = dsum(gam_ref, 0, 8)
        l[...] = dsum(gl_ref, 0, 16)
        for dev in range(8):
            for k in range(3):
                dmod_ref[dev:dev + 1, k * D_MODEL:(k + 1) * D_MODEL] = gad_ref[dev * 8 + k:dev * 8 + k + 1, :]
                dmod_ref[dev:dev + 1, (3 + k) * D_MODEL:(4 + k) * D_MODEL] = gam_ref[dev * 8 + k:dev * 8 + k + 1, :]
        w_ref, m_ref, v_ref = ins["ada_b"]
        g_ref, d_ref, mo_ref, vo_ref = outs["ada_b"]
        for k in range(3):
            g_ref[:, k * D_MODEL:(k + 1) * D_MODEL] = d[k:k + 1, :]
            g_ref[:, (3 + k) * D_MODEL:(4 + k) * D_MODEL] = dm[k:k + 1, :]
        d_ref[...], mo_ref[...], vo_ref[...] = _adam_math(w_ref[...], g_ref[...], m_ref[...], v_ref[...])
        update("norm1_g", d[3:4, :])
        update("norm2_g", dm[3:4, :])
        update("final_g", dm[4:5, :])
        update("gate_a_b", d[4:5, 0:D_LRU])
        update("gate_x_b", d[4:5, D_LRU:2 * D_LRU])
        update("lru_conv_b", l[4:5, :])
        update("a_param", l[8:9, :] * jax.nn.sigmoid(ins["a_param"][0][...]))
        update("lru_out_g", l[9:10, :])
        update("conv_out_g", l[10:11, :])
        loss_ref[...] = jnp.broadcast_to(dm[5:6, 0:128], (8, 128))
        chip = chip_ref[0]
        acc = jnp.zeros((8, 128), F32)
        for j in range(N_CHIP):
            acc = acc + jnp.where(chip == j, l[0:8, j * 128:(j + 1) * 128], 0.0)
        lw[...] = acc
        update("lru_conv_w", lw[0:4, :])
        update("short_conv_w", lw[5:8, :])
        gates = dsum(gg_ref, 0, D_LRU)
        update("gate_a_w", gates[:, 0:HEAD])
        update("gate_x_w", gates[:, HEAD:2 * HEAD])

    out_shape = []
    for n in names:
        out_shape += [jax.ShapeDtypeStruct(params[n][0].shape, F32)] * 4
    out_shape += [jax.ShapeDtypeStruct((8, 128), F32), jax.ShapeDtypeStruct((8, 6 * D_MODEL), F32)]
    res = pl.pallas_call(
        body, name="small_update", out_shape=out_shape,
        in_specs=[SMEM] + [VMEM] * (4 + nin),
        out_specs=[VMEM] * len(out_shape),
        scratch_shapes=[pltpu.VMEM((8, D_MODEL), F32), pltpu.VMEM((8, D_MODEL), F32), pltpu.VMEM((16, D_LRU), F32),
                        pltpu.VMEM((8, 128), F32)],
        compiler_params=pltpu.CompilerParams(vmem_limit_bytes=VMEM_LIMIT),
    )(mychip, gad, gam, gl, gg, *flat)
    per = {n: res[4 * k:4 * k + 4] for k, n in enumerate(names)}
    return per, res[-2], res[-1]


def _block_diag(w):
    eye = jnp.eye(8, dtype=w.dtype)
    return (eye[:, None, :, None] * w[:, :, None, :]).reshape(8 * HEAD, 8 * HEAD)


def _diag_blocks(g):
    return jnp.concatenate([g[h * HEAD:(h + 1) * HEAD, h * HEAD:(h + 1) * HEAD] for h in range(8)], axis=0)


def kernel(x, c, ada_w, ada_b, norm1_g, w_in, lru_conv_w, lru_conv_b, gate_a_w, gate_a_b, gate_x_w, gate_x_b, a_param, short_conv_w, lru_out_g, conv_out_g, w_out, norm2_g, w_mlp1, w_mlp2, final_g, loss_target, m_ada_w, m_ada_b, m_norm1_g, m_w_in, m_lru_conv_w, m_lru_conv_b, m_gate_a_w, m_gate_a_b, m_gate_x_w, m_gate_x_b, m_a_param, m_short_conv_w, m_lru_out_g, m_conv_out_g, m_w_out, m_norm2_g, m_w_mlp1, m_w_mlp2, m_final_g, v_ada_w, v_ada_b, v_norm1_g, v_w_in, v_lru_conv_w, v_lru_conv_b, v_gate_a_w, v_gate_a_b, v_gate_x_w, v_gate_x_b, v_a_param, v_short_conv_w, v_lru_out_g, v_conv_out_g, v_w_out, v_norm2_g, v_w_mlp1, v_w_mlp2, v_final_g):
    xi, yi, ci = lax.axis_index("x"), lax.axis_index("y"), lax.axis_index("c")
    mychip = 2 * xi + yi
    me = 4 * xi + 2 * yi + ci

    own_in, own_out = w_in[0].astype(BF16), w_out[0].astype(BF16)
    win_all, wout_all = _allgather_weights("allgather_mixer_weights", 1, [own_in, own_out])
    own_w1, own_w2 = w_mlp1[0].astype(BF16), w_mlp2[0].astype(BF16)
    (w1_all,) = _allgather_weights("allgather_mlp1_weights", 2, [own_w1])
    (w2_all,) = _allgather_weights("allgather_mlp2_weights", 9, [own_w2])

    c_blk = jnp.zeros((8, D_MODEL), F32).at[0:1].set(c)
    cw_blk = jnp.zeros((8, 128), F32).at[0:4].set(lru_conv_w[0]).at[4:7].set(short_conv_w[0])
    c_g, cw_g = _allgather8("allgather_cond", [c_blk, cw_blk])
    c_all = c_g.reshape(8, 8, D_MODEL)[:, 0]
    cw_g = cw_g.reshape(4, 2, 8, 128)[:, 0]
    lcw = cw_g[:, 0:4].transpose(1, 0, 2).reshape(4, D_LRU)
    scw = cw_g[:, 4:7].transpose(1, 0, 2).reshape(3, D_LRU)

    mod_loc = _mod_matmul(c_all, ada_w[0])
    (mod_g,) = _allgather8("allgather_mod", [mod_loc])
    mod_all = mod_g.reshape(4, 2, 8, 6 * D_MODEL // 4)[:, 0].transpose(1, 0, 2).reshape(8, 6 * D_MODEL) + ada_b
    mod_pad = jnp.pad(mod_all.reshape(8, 6, D_MODEL), ((0, 0), (0, 2), (0, 0)))
    mod = lax.dynamic_slice_in_dim(mod_pad, me, 1, axis=0).reshape(8, D_MODEL)

    win, wout = (win_all, own_in), (wout_all, own_out)
    chip = mychip.reshape(1).astype(jnp.int32)
    core = ci.reshape(1).astype(jnp.int32)

    vecd = jnp.concatenate([norm1_g, norm2_g, final_g[None, :], jnp.concatenate([gate_a_b, gate_x_b], axis=1),
                            jnp.zeros((4, D_MODEL), F32)], axis=0)
    vecl = jnp.concatenate([lcw, lru_conv_b, scw, a_param, lru_out_g, conv_out_g, jnp.zeros((5, D_LRU), F32)], axis=0)
    gab = jnp.concatenate([_block_diag(gate_a_w[0]), _block_diag(gate_x_w[0])], axis=1).astype(BF16)
    a64 = _block_diag(jnp.full((8, HEAD, HEAD), 1.0 / HEAD, F32)).astype(BF16)

    hb, proj, hl, ycat, mixed, x1 = _mix_fwd(chip, x[0], mod, vecd, vecl, win, wout, gab, a64)
    h2b, rz = _mlp_up(chip, x1, mod, vecd, (w1_all, own_w1))
    dx1, act, dz, dmo, accm = _mlp_fwd_bwd(
        chip, x1, rz, loss_target[0], mod, vecd, (w1_all, own_w1), (w2_all, own_w2))

    parts_mlp = list(_wgrad_mlp(6, h2b, dz, act, dmo))
    q_w1, q_w2 = _exchange_chips("rs_exchange_mlp", 0, parts_mlp)
    grad_x, accd, accl, g_win, g_wout, g_gate = _mix_bwd(
        chip, dx1, x[0], mixed, proj, hl, hb, ycat, mod, vecd, vecl, win, wout, gab, a64)

    g_mix = [g_win, g_wout.reshape(N_CHIP, WOUT_BLK, D_MODEL)]
    recv_mix = _sibling_swap("rs_swap_halves_mix", g_mix, True, collective_id=4)
    own_mlp = [_add_chips("rs_add_chips_mlp%d" % k, chip, p, q) for k, (p, q) in enumerate(zip(parts_mlp, (q_w1, q_w2)))]
    sib_mlp = _sibling_swap("rs_swap_reduced_mlp", own_mlp, False, collective_id=5)
    gg_blk = jnp.concatenate([_diag_blocks(g_gate[:, 0:D_LRU]), _diag_blocks(g_gate[:, D_LRU:2 * D_LRU])], axis=1)
    gad, gam, gl, gg = _allgather8_seq("allgather_small_grads", 8, [accd, accm, accl, gg_blk.astype(BF16)])

    parts_mix = [_add_sibling("rs_add_sibling_mix%d" % k, g, r, core)
                 for k, (g, r) in enumerate(zip(g_mix, recv_mix))]
    landed_mix = _exchange_chips("rs_exchange_mix", 3, parts_mix)
    res_w1, res_w2 = _adam("adam_mlp", core, [(w_mlp1[0], own_mlp[0], sib_mlp[0], m_w_mlp1[0], v_w_mlp1[0]),
                                              (w_mlp2[0], own_mlp[1], sib_mlp[1], m_w_mlp2[0], v_w_mlp2[0])])
    own_mix = [_add_chips("rs_add_chips_mix%d" % k, chip, p, q, after=[res_w1[1]])
               for k, (p, q) in enumerate(zip(parts_mix, landed_mix))]
    sib_mix = _sibling_swap("rs_swap_reduced_mix", own_mix, False)
    (res_win,) = _adam("adam_w_in", core, [(w_in[0], own_mix[0], sib_mix[0], m_w_in[0], v_w_in[0])])
    (res_wout,) = _adam("adam_w_out", core, [(w_out[0], own_mix[1], sib_mix[1], m_w_out[0], v_w_out[0])])

    params = {
        "ada_b": (ada_b, m_ada_b, v_ada_b), "norm1_g": (norm1_g, m_norm1_g, v_norm1_g),
        "lru_conv_b": (lru_conv_b, m_lru_conv_b, v_lru_conv_b),
        "gate_a_w": tuple(t.reshape(D_LRU, HEAD) for t in (gate_a_w, m_gate_a_w, v_gate_a_w)),
        "gate_a_b": (gate_a_b, m_gate_a_b, v_gate_a_b),
        "gate_x_w": tuple(t.reshape(D_LRU, HEAD) for t in (gate_x_w, m_gate_x_w, v_gate_x_w)),
        "gate_x_b": (gate_x_b, m_gate_x_b, v_gate_x_b), "a_param": (a_param, m_a_param, v_a_param),
        "lru_conv_w": (lru_conv_w, m_lru_conv_w, v_lru_conv_w),
        "short_conv_w": (short_conv_w, m_short_conv_w, v_short_conv_w),
        "lru_out_g": (lru_out_g, m_lru_out_g, v_lru_out_g), "conv_out_g": (conv_out_g, m_conv_out_g, v_conv_out_g),
        "norm2_g": (norm2_g, m_norm2_g, v_norm2_g),
        "final_g": tuple(t[None, :] for t in (final_g, m_final_g, v_final_g)),
    }
    small, loss_blk, dmod_cols = _small_update(gad, gam, gl, gg, chip, params)
    loss = loss_blk[0, 0]

    sct = (c_all * jax.nn.sigmoid(c_all)).T
    ada = _ada_grad_adam(chip, sct, dmod_cols, ada_w[0], m_ada_w[0], v_ada_w[0])

    res = {"ada_w": ada, "w_in": res_win, "w_out": res_wout, "w_mlp1": res_w1, "w_mlp2": res_w2}
    res = {n: tuple(t[None] for t in r) for n, r in res.items()}
    shapes = {"gate_a_w": gate_a_w.shape, "gate_x_w": gate_x_w.shape, "lru_conv_w": lru_conv_w.shape,
              "short_conv_w": short_conv_w.shape, "final_g": final_g.shape}
    for n, t in small.items():
        res[n] = tuple(u.reshape(shapes[n]) if n in shapes else u for u in t)

    order = ["ada_w", "ada_b", "norm1_g", "w_in", "lru_conv_w", "lru_conv_b", "gate_a_w", "gate_a_b", "gate_x_w",
             "gate_x_b", "a_param", "short_conv_w", "lru_out_g", "conv_out_g", "w_out", "norm2_g", "w_mlp1",
             "w_mlp2", "final_g"]
    return (loss, grad_x[None], *[res[n][0] for n in order], *[res[n][1] for n in order],
            *[res[n][2] for n in order], *[res[n][3] for n in order])
```

```python
import jax
import jax.numpy as jnp
from jax import lax
from jax.experimental import pallas as pl
from jax.experimental.pallas import tpu as pltpu
from jax.experimental.pallas import tpu_sc as plsc

F32 = jnp.float32
BF16 = jnp.bfloat16

D_MODEL = 1024
D_LRU = 512
D_IN = 2560
D_FF = 4096
N_CHIP = 4
WIN_BLK = D_IN // N_CHIP
WOUT_BLK = D_MODEL // N_CHIP
FF_BLK = D_FF // N_CHIP
HEAD = 64
EPS = 1e-6
C_GATE = 8.0
TOKEN_TILE = 256
MIX_FWD_TILE = 512
HALO = 8
VMEM_LIMIT = 60 * 1024 * 1024

ADAM_LR = 0.001
ADAM_B1 = 0.9
ADAM_B2 = 0.999
ADAM_EPS = 1e-08
ADAM_WD = 0.01
ADAM_STEP = 10

MESH = pl.DeviceIdType.MESH
ANY = pl.BlockSpec(memory_space=pl.ANY)
VMEM = pl.BlockSpec(memory_space=pltpu.VMEM)
SMEM = pl.BlockSpec(memory_space=pltpu.SMEM)


def _full(shape, single=False):
    nd = len(shape)
    if single:
        return pl.BlockSpec(shape, lambda *_: (0,) * nd, pipeline_mode=pl.Buffered(1))
    return pl.BlockSpec(shape, lambda *_: (0,) * nd)


def _dot(a, b):
    return jnp.dot(a, b, preferred_element_type=F32)


def _dot_nt(a, b):
    return lax.dot_general(a, b, (((1,), (1,)), ((), ())), preferred_element_type=F32)


def _dot_tn(a, b):
    return lax.dot_general(a, b, (((0,), (0,)), ((), ())), preferred_element_type=F32)


def _gmean(v, a64):
    hi = v.astype(BF16)
    lo = (v - hi.astype(F32)).astype(BF16)
    return _dot(hi, a64) + _dot(lo, a64)


def _gelu(x):
    u = 0.7978845608028654 * (x + 0.044715 * x * x * x)
    t = jnp.tanh(u)
    return 0.5 * x * (1.0 + t), t


def _gelu_grad(x, t):
    du = 0.7978845608028654 * (1.0 + 3.0 * 0.044715 * x * x)
    return 0.5 * (1.0 + t) + 0.5 * x * (1.0 - t * t) * du


def _log1p_pos(y):
    return jnp.where(y < 1e-2, y * (1.0 - y * (0.5 - y * (1.0 / 3.0 - y * 0.25))), jnp.log(1.0 + y))


def _softplus(a):
    return jnp.maximum(a, 0.0) + _log1p_pos(jnp.exp(-jnp.abs(a)))


def _neg_expm1(z):
    series = -z * (1.0 + z * (0.5 + z * (1.0 / 6.0 + z * (1.0 / 24.0))))
    return jnp.where(z > -0.02, series, 1.0 - jnp.exp(z))


def _scan_fwd(a, b, row):
    n = a.shape[0]
    d = 1
    while d < n:
        m = row >= d
        b = jnp.where(m, a * pltpu.roll(b, d, 0) + b, b)
        a = jnp.where(m, a * pltpu.roll(a, d, 0), a)
        d *= 2
    return a, b


def _scan_rev(a, b, row):
    n = a.shape[0]
    d = 1
    while d < n:
        m = row < n - d
        b = jnp.where(m, b + a * pltpu.roll(b, n - d, 0), b)
        a = jnp.where(m, a * pltpu.roll(a, n - d, 0), a)
        d *= 2
    return a, b


def _colsum(v):
    return jnp.sum(v, axis=0, keepdims=True)


def _load_gathered(chip, gathered, own, slot, sems):
    copies = []
    for j in range(N_CHIP):
        @pl.when(chip == j)
        def _(j=j):
            pltpu.make_async_copy(own, slot(j), sems.at[j]).start()

        @pl.when(chip != j)
        def _(j=j):
            pltpu.make_async_copy(gathered.at[j], slot(j), sems.at[j]).start()

        copies.append(pltpu.make_async_copy(own, slot(j), sems.at[j]))
    return copies


def _lru_gates(xlb, gab, gbias, sp, first_row):
    g = _dot(xlb, gab) + gbias
    r = jax.nn.sigmoid(g[:, :D_LRU])
    ig = jax.nn.sigmoid(g[:, D_LRU:])
    la = (-C_GATE) * r * sp
    a = jnp.exp(la)
    msq = jnp.sqrt(_neg_expm1(2.0 * la))
    mult = jnp.where(first_row, 1.0, msq)
    return r, ig, a, msq, mult


def _mix_fwd(chip, x, mod, vecd, vecl, win, wout, gab, a64):
    s = x.shape[0]
    ts = MIX_FWD_TILE
    nt = s // ts

    def body(chip_ref, x_ref, mod_ref, vd_ref, vl_ref, win_hbm, win_own, wout_hbm, wout_own, gab_ref, a64_ref,
             hb_ref, proj_ref, hl_ref, ycat_ref, mixed_ref, x1_ref,
             win_ref, wout_ref, ext_lx, ext_cv, hcar, sems):
        i = pl.program_id(0)

        @pl.when(i == 0)
        def _():
            cps = _load_gathered(chip_ref[0], win_hbm, win_own, lambda j: win_ref.at[j], sems.at[pl.ds(0, N_CHIP)])
            cps += _load_gathered(chip_ref[0], wout_hbm, wout_own,
                                  lambda j: wout_ref.at[pl.ds(j * WOUT_BLK, WOUT_BLK), :],
                                  sems.at[pl.ds(N_CHIP, N_CHIP)])
            ext_lx[0:HALO, :] = jnp.zeros((HALO, D_LRU), F32)
            ext_cv[0:HALO, :] = jnp.zeros((HALO, D_LRU), F32)
            hcar[...] = jnp.zeros_like(hcar)
            for cp in cps:
                cp.wait()

        row = lax.broadcasted_iota(jnp.int32, (ts, D_LRU), 0)
        first_row = jnp.logical_and(row == 0, i == 0)
        xt = x_ref[...]
        shift1, scale1, gate1 = mod_ref[0:1, :], mod_ref[1:2, :], mod_ref[2:3, :]
        r1 = lax.rsqrt(jnp.mean(xt * xt, axis=-1, keepdims=True) + EPS)
        h = (xt * r1) * vd_ref[0:1, :] * (1.0 + scale1) + shift1
        hb = h.astype(BF16)
        hb_ref[...] = hb
        for j in range(N_CHIP):
            proj_ref[:, j * WIN_BLK:(j + 1) * WIN_BLK] = _dot(hb, win_ref[j])
        u_ly = proj_ref[:, 512:1024]
        u_b = proj_ref[:, 1024:1536]

        ext_lx[HALO:HALO + ts, :] = proj_ref[:, 0:512]
        xl = vl_ref[4:5, :] + vl_ref[0:1, :] * ext_lx[pl.ds(5, ts), :]
        for k in range(1, 4):
            xl = xl + vl_ref[k:k + 1, :] * ext_lx[pl.ds(5 + k, ts), :]
        ext_lx[0:HALO, :] = ext_lx[ts:ts + HALO, :]
        sp = _softplus(vl_ref[8:9, :])
        _, ig, a, _, mult = _lru_gates(xl.astype(BF16), gab_ref[...], vd_ref[3:4, :], sp, first_row)
        acum, hloc = _scan_fwd(a, mult * (ig * xl), row)
        hl = hloc + acum * hcar[0:1, :]
        hl_ref[...] = hl
        hcar[0:1, :] = hl_ref[ts - 1:ts, :]
        ge, _ = _gelu(u_ly)
        p = ge * hl
        y_lru = p * lax.rsqrt(_gmean(p * p, a64_ref[...]) + EPS) * vl_ref[9:10, :]
        ycat_ref[:, 0:512] = y_lru.astype(BF16)

        ext_cv[HALO:HALO + ts, :] = proj_ref[:, 1536:2048] * proj_ref[:, 2048:2560]
        q = vl_ref[5:6, :] * ext_cv[pl.ds(6, ts), :]
        for k in range(1, 3):
            q = q + vl_ref[5 + k:6 + k, :] * ext_cv[pl.ds(6 + k, ts), :]
        ext_cv[0:HALO, :] = ext_cv[ts:ts + HALO, :]
        yc = u_b * q
        y_conv = yc * lax.rsqrt(_gmean(yc * yc, a64_ref[...]) + EPS) * vl_ref[10:11, :]
        ycat_ref[:, 512:1024] = y_conv.astype(BF16)

        mixed = _dot(ycat_ref[...], wout_ref[...])
        mixed_ref[...] = mixed
        x1_ref[...] = xt + gate1 * mixed

    tile = lambda w: pl.BlockSpec((ts, w), lambda i: (i, 0))
    return pl.pallas_call(
        body, name="mix_fwd", grid=(nt,),
        in_specs=[SMEM, tile(D_MODEL), _full((8, D_MODEL)), _full((8, D_MODEL)), _full((16, D_LRU)),
                  ANY, ANY, ANY, ANY, _full((D_LRU, 2 * D_LRU), True), _full((D_LRU, D_LRU), True)],
        out_specs=[tile(D_MODEL), tile(D_IN), tile(D_LRU), tile(D_MODEL), tile(D_MODEL), tile(D_MODEL)],
        out_shape=[jax.ShapeDtypeStruct((s, D_MODEL), BF16), jax.ShapeDtypeStruct((s, D_IN), F32),
                   jax.ShapeDtypeStruct((s, D_LRU), F32), jax.ShapeDtypeStruct((s, D_MODEL), BF16),
                   jax.ShapeDtypeStruct((s, D_MODEL), F32), jax.ShapeDtypeStruct((s, D_MODEL), F32)],
        scratch_shapes=[pltpu.VMEM((N_CHIP, D_MODEL, WIN_BLK), BF16), pltpu.VMEM((D_MODEL, D_MODEL), BF16),
                        pltpu.VMEM((ts + HALO, D_LRU), F32), pltpu.VMEM((ts + HALO, D_LRU), F32),
                        pltpu.VMEM((HALO, D_LRU), F32), pltpu.SemaphoreType.DMA((2 * N_CHIP,))],
        compiler_params=pltpu.CompilerParams(dimension_semantics=("arbitrary",), vmem_limit_bytes=VMEM_LIMIT),
    )(chip, x, mod, vecd, vecl, *win, *wout, gab, a64)


def _mlp_fwd_bwd(chip, x1, target, mod, vecd, w1, w2):
    s = x1.shape[0]
    ts = TOKEN_TILE
    nt = s // ts

    def body(chip_ref, x1_ref, tg_ref, mod_ref, vd_ref, w1_hbm, w1_own, w2_hbm, w2_own,
             dx1_ref, act_ref, dz_ref, dmo_ref, h2_ref, acc_ref, w1_v, w2_v, rz_v, sems):
        i = pl.program_id(0)

        @pl.when(i == 0)
        def _():
            cps = _load_gathered(chip_ref[0], w1_hbm, w1_own, lambda j: w1_v.at[j], sems.at[pl.ds(0, N_CHIP)])
            cps += _load_gathered(chip_ref[0], w2_hbm, w2_own, lambda j: w2_v.at[j], sems.at[pl.ds(N_CHIP, N_CHIP)])
            acc_ref[...] = jnp.zeros_like(acc_ref)
            for cp in cps:
                cp.wait()

        xt = x1_ref[...]
        shift2, scale2, gate2 = mod_ref[3:4, :], mod_ref[4:5, :], mod_ref[5:6, :]
        g2, gf = vd_ref[1:2, :], vd_ref[2:3, :]
        r2 = lax.rsqrt(jnp.mean(xt * xt, axis=-1, keepdims=True) + EPS)
        n2 = xt * r2
        h2b = (n2 * g2 * (1.0 + scale2) + shift2).astype(BF16)
        h2_ref[...] = h2b
        for j in range(N_CHIP):
            rz_v[j] = jnp.maximum(_dot(h2b, w1_v[j]), 0.0)
        mo = jnp.zeros((ts, D_MODEL), F32)
        for j in range(N_CHIP):
            rz = rz_v[j]
            actb = (rz * rz).astype(BF16)
            act_ref[:, j * FF_BLK:(j + 1) * FF_BLK] = actb
            mo = mo + _dot(actb, w2_v[j])
        x2 = xt + gate2 * mo
        r3 = lax.rsqrt(jnp.mean(x2 * x2, axis=-1, keepdims=True) + EPS)
        n3 = x2 * r3
        e = n3 * gf - tg_ref[...]
        loss = (0.5 / D_MODEL) * jnp.sum(_colsum(e * e), axis=1, keepdims=True)
        dy = e * (1.0 / D_MODEL)
        acc_ref[4:5, :] += _colsum(dy * n3)
        acc_ref[5:6, :] += jnp.broadcast_to(loss, (1, D_MODEL))
        dn3 = dy * gf
        dx2 = r3 * (dn3 - n3 * jnp.mean(dn3 * n3, axis=-1, keepdims=True))
        acc_ref[2:3, :] += _colsum(dx2 * mo)
        dmob = (dx2 * gate2).astype(BF16)
        dmo_ref[...] = dmob
        for j in range(N_CHIP):
            dz_ref[:, j * FF_BLK:(j + 1) * FF_BLK] = (_dot_nt(dmob, w2_v[j]) * (2.0 * rz_v[j])).astype(BF16)
        dh2 = jnp.zeros((ts, D_MODEL), F32)
        for j in range(N_CHIP):
            dh2 = dh2 + _dot_nt(dz_ref[:, j * FF_BLK:(j + 1) * FF_BLK], w1_v[j])
        acc_ref[1:2, :] += _colsum(dh2 * (n2 * g2))
        acc_ref[0:1, :] += _colsum(dh2)
        dhn2 = dh2 * (1.0 + scale2)
        acc_ref[3:4, :] += _colsum(dhn2 * n2)
        dn2 = dhn2 * g2
        dx1_ref[...] = dx2 + r2 * (dn2 - n2 * jnp.mean(dn2 * n2, axis=-1, keepdims=True))

    tile = lambda w: pl.BlockSpec((ts, w), lambda i: (i, 0))
    return pl.pallas_call(
        body, name="mlp_fwd_bwd", grid=(nt,),
        in_specs=[SMEM, tile(D_MODEL), tile(D_MODEL), _full((8, D_MODEL)), _full((8, D_MODEL)), ANY, ANY, ANY, ANY],
        out_specs=[tile(D_MODEL), tile(D_FF), tile(D_FF), tile(D_MODEL), tile(D_MODEL), _full((8, D_MODEL))],
        out_shape=[jax.ShapeDtypeStruct((s, D_MODEL), F32), jax.ShapeDtypeStruct((s, D_FF), BF16),
                   jax.ShapeDtypeStruct((s, D_FF), BF16), jax.ShapeDtypeStruct((s, D_MODEL), BF16),
                   jax.ShapeDtypeStruct((s, D_MODEL), BF16), jax.ShapeDtypeStruct((8, D_MODEL), F32)],
        scratch_shapes=[pltpu.VMEM((N_CHIP, D_MODEL, FF_BLK), BF16), pltpu.VMEM((N_CHIP, FF_BLK, D_MODEL), BF16),
                        pltpu.VMEM((N_CHIP, ts, FF_BLK), F32), pltpu.SemaphoreType.DMA((2 * N_CHIP,))],
        compiler_params=pltpu.CompilerParams(dimension_semantics=("arbitrary",), vmem_limit_bytes=VMEM_LIMIT),
    )(chip, x1, target, mod, vecd, *w1, *w2)


def _mix_bwd(chip, dx1, x, mixed, proj, hl, hb, ycat, mod, vecd, vecl, win, wout, gab, a64):
    s = x.shape[0]
    ts = TOKEN_TILE
    nt = s // ts
    hpt = ts // HALO

    def body(chip_ref, dx1_ref, x_ref, mixed_ref, proj_ref, projh_ref, hl_ref, hlh_ref, hb_ref, ycat_ref,
             mod_ref, vd_ref, vl_ref, win_hbm, win_own, wout_hbm, wout_own, gab_ref, a64_ref,
             gx_ref, accd_ref, accl_ref, gwin_hbm, gwout_hbm, ggate_hbm,
             win_ref, wout_ref, dproj_ref, dgb_ref, gwin_acc, gwout_acc, ggate_acc,
             ext_lx, ext_cv, ext_hl, ext_dxl, ext_dq, gbuf, gcar, acar, sems):
        i = pl.program_id(0)
        ri = nt - 1 - i

        @pl.when(i == 0)
        def _():
            gwin_acc[...] = jnp.zeros_like(gwin_acc)
            gwout_acc[...] = jnp.zeros_like(gwout_acc)
            ggate_acc[...] = jnp.zeros_like(ggate_acc)
            cps = _load_gathered(chip_ref[0], win_hbm, win_own, lambda j: win_ref.at[j], sems.at[pl.ds(0, N_CHIP)])
            cps += _load_gathered(chip_ref[0], wout_hbm, wout_own,
                                  lambda j: wout_ref.at[pl.ds(j * WOUT_BLK, WOUT_BLK), :],
                                  sems.at[pl.ds(N_CHIP, N_CHIP)])
            for cp in cps:
                cp.wait()
            accd_ref[...] = jnp.zeros_like(accd_ref)
            accl_ref[...] = jnp.zeros_like(accl_ref)
            ext_dxl[ts:ts + HALO, :] = jnp.zeros((HALO, D_LRU), F32)
            ext_dq[ts:ts + HALO, :] = jnp.zeros((HALO, D_LRU), F32)
            gcar[...] = jnp.zeros_like(gcar)
            acar[...] = jnp.zeros_like(acar)

        row = lax.broadcasted_iota(jnp.int32, (ts, D_LRU), 0)
        first_row = jnp.logical_and(row == 0, ri == 0)
        halo_on = jnp.where(ri == 0, 0.0, 1.0)
        shift1, scale1, gate1 = mod_ref[0:1, :], mod_ref[1:2, :], mod_ref[2:3, :]
        g1 = vd_ref[0:1, :]
        a64m = a64_ref[...]
        lg, cg = vl_ref[9:10, :], vl_ref[10:11, :]

        dx1 = dx1_ref[...]
        accd_ref[2:3, :] += _colsum(dx1 * mixed_ref[...])
        dmb = (dx1 * gate1).astype(BF16)
        gwout_acc[...] += _dot_tn(ycat_ref[...], dmb)
        dycat = _dot_nt(dmb, wout_ref[...])
        dyl = dycat[:, 0:512]
        dyv = dycat[:, 512:1024]

        u_ly = proj_ref[:, 512:1024]
        u_b = proj_ref[:, 1024:1536]
        u_c = proj_ref[:, 1536:2048]
        u_v = proj_ref[:, 2048:2560]
        ext_lx[0:HALO, :] = projh_ref[:, 0:512] * halo_on
        ext_lx[HALO:HALO + ts, :] = proj_ref[:, 0:512]
        xl = vl_ref[4:5, :] + vl_ref[0:1, :] * ext_lx[pl.ds(5, ts), :]
        for k in range(1, 4):
            xl = xl + vl_ref[k:k + 1, :] * ext_lx[pl.ds(5 + k, ts), :]
        xlb = xl.astype(BF16)
        sp = _softplus(vl_ref[8:9, :])
        r, ig, a, msq, mult = _lru_gates(xlb, gab_ref[...], vd_ref[3:4, :], sp, first_row)
        hl = hl_ref[...]
        ge, th = _gelu(u_ly)
        p = ge * hl
        rl = lax.rsqrt(_gmean(p * p, a64m) + EPS)
        nl = p * rl
        ext_cv[0:HALO, :] = projh_ref[:, 1536:2048] * projh_ref[:, 2048:2560] * halo_on
        ext_cv[HALO:HALO + ts, :] = u_c * u_v
        q = vl_ref[5:6, :] * ext_cv[pl.ds(6, ts), :]
        for k in range(1, 3):
            q = q + vl_ref[5 + k:6 + k, :] * ext_cv[pl.ds(6 + k, ts), :]
        yc = u_b * q
        rc = lax.rsqrt(_gmean(yc * yc, a64m) + EPS)
        nc = yc * rc

        accl_ref[9:10, :] += _colsum(dyl * nl)
        dnl = dyl * lg
        dp = rl * (dnl - nl * _gmean(dnl * nl, a64m))
        dproj_ref[:, 512:1024] = ((dp * hl) * _gelu_grad(u_ly, th)).astype(BF16)
        a_next = jnp.where(row == ts - 1, acar[0:1, :], pltpu.roll(a, ts - 1, 0))
        acum, gloc = _scan_rev(a_next, dp * ge, row)
        gbuf[...] = gloc + acum * gcar[0:1, :]
        gcar[0:1, :] = gbuf[0:1, :]
        ext_hl[0:HALO, :] = hlh_ref[...] * halo_on
        ext_hl[HALO:HALO + ts, :] = hl
        acar[...] = a[0:HALO, :]
        gt = gbuf[...]
        da = gt * ext_hl[pl.ds(HALO - 1, ts), :]
        dmult = gt * ig * xl
        di = gt * mult * xl
        dxl = gt * mult * ig
        dla = da * a - jnp.where(first_row, 0.0, dmult * a * a / msq)
        accl_ref[8:9, :] += _colsum(dla * ((-C_GATE) * r))
        dra = dla * ((-C_GATE) * sp) * r * (1.0 - r)
        dia = di * ig * (1.0 - ig)
        accd_ref[4:5, 0:D_LRU] += _colsum(dra)
        accd_ref[4:5, D_LRU:2 * D_LRU] += _colsum(dia)
        dgb_ref[:, 0:D_LRU] = dra.astype(BF16)
        dgb_ref[:, D_LRU:2 * D_LRU] = dia.astype(BF16)
        dxl = dxl + _dot_nt(dgb_ref[...], gab_ref[...])
        ggate_acc[...] += _dot_tn(xlb, dgb_ref[...])
        accl_ref[4:5, :] += _colsum(dxl)
        for k in range(4):
            accl_ref[k:k + 1, :] += _colsum(dxl * ext_lx[pl.ds(5 + k, ts), :])
        ext_dxl[0:ts, :] = dxl
        du_lx = vl_ref[0:1, :] * ext_dxl[pl.ds(3, ts), :]
        for k in range(1, 4):
            du_lx = du_lx + vl_ref[k:k + 1, :] * ext_dxl[pl.ds(3 - k, ts), :]
        ext_dxl[ts:ts + HALO, :] = ext_dxl[0:HALO, :]
        dproj_ref[:, 0:512] = du_lx.astype(BF16)

        accl_ref[10:11, :] += _colsum(dyv * nc)
        dnc = dyv * cg
        dyc = rc * (dnc - nc * _gmean(dnc * nc, a64m))
        dproj_ref[:, 1024:1536] = (dyc * q).astype(BF16)
        dq = dyc * u_b
        for k in range(3):
            accl_ref[5 + k:6 + k, :] += _colsum(dq * ext_cv[pl.ds(6 + k, ts), :])
        ext_dq[0:ts, :] = dq
        dcv = vl_ref[5:6, :] * ext_dq[pl.ds(2, ts), :]
        for k in range(1, 3):
            dcv = dcv + vl_ref[5 + k:6 + k, :] * ext_dq[pl.ds(2 - k, ts), :]
        ext_dq[ts:ts + HALO, :] = ext_dq[0:HALO, :]
        dproj_ref[:, 1536:2048] = (dcv * u_v).astype(BF16)
        dproj_ref[:, 2048:2560] = (dcv * u_c).astype(BF16)

        dh = _dot_nt(dproj_ref[:, 0:WIN_BLK], win_ref[0])
        for j in range(1, N_CHIP):
            dh = dh + _dot_nt(dproj_ref[:, j * WIN_BLK:(j + 1) * WIN_BLK], win_ref[j])
        for j in range(N_CHIP):
            gwin_acc[j] += _dot_tn(hb_ref[...], dproj_ref[:, j * WIN_BLK:(j + 1) * WIN_BLK])
        xt = x_ref[...]
        r1 = lax.rsqrt(jnp.mean(xt * xt, axis=-1, keepdims=True) + EPS)
        n1 = xt * r1
        accd_ref[1:2, :] += _colsum(dh * (n1 * g1))
        accd_ref[0:1, :] += _colsum(dh)
        dhn1 = dh * (1.0 + scale1)
        accd_ref[3:4, :] += _colsum(dhn1 * n1)
        dn1 = dhn1 * g1
        gx_ref[...] = dx1 + r1 * (dn1 - n1 * jnp.mean(dn1 * n1, axis=-1, keepdims=True))

        @pl.when(i == nt - 1)
        def _():
            outs = [pltpu.make_async_copy(acc, dst, sems.at[k]) for k, (acc, dst) in enumerate(
                ((gwin_acc, gwin_hbm), (gwout_acc, gwout_hbm), (ggate_acc, ggate_hbm)))]
            for cp in outs:
                cp.start()
            for cp in outs:
                cp.wait()

    tile = lambda w: pl.BlockSpec((ts, w), lambda i: (nt - 1 - i, 0))
    halo = lambda w: pl.BlockSpec((HALO, w), lambda i: (jnp.maximum((nt - 1 - i) * hpt - 1, 0), 0))
    ext = pltpu.VMEM((ts + HALO, D_LRU), F32)
    return pl.pallas_call(
        body, name="mix_bwd", grid=(nt,),
        in_specs=[SMEM, tile(D_MODEL), tile(D_MODEL), tile(D_MODEL), tile(D_IN), halo(D_IN), tile(D_LRU), halo(D_LRU),
                  tile(D_MODEL), tile(D_MODEL), _full((8, D_MODEL)), _full((8, D_MODEL)), _full((16, D_LRU)),
                  ANY, ANY, ANY, ANY, _full((D_LRU, 2 * D_LRU), True), _full((D_LRU, D_LRU), True)],
        out_specs=[tile(D_MODEL), _full((8, D_MODEL)), _full((16, D_LRU)), ANY, ANY, ANY],
        out_shape=[jax.ShapeDtypeStruct((s, D_MODEL), F32),
                   jax.ShapeDtypeStruct((8, D_MODEL), F32), jax.ShapeDtypeStruct((16, D_LRU), F32),
                   jax.ShapeDtypeStruct((N_CHIP, D_MODEL, WIN_BLK), F32), jax.ShapeDtypeStruct((D_MODEL, D_MODEL), F32),
                   jax.ShapeDtypeStruct((D_LRU, 2 * D_LRU), F32)],
        scratch_shapes=[pltpu.VMEM((N_CHIP, D_MODEL, WIN_BLK), BF16), pltpu.VMEM((D_MODEL, D_MODEL), BF16),
                        pltpu.VMEM((ts, D_IN), BF16), pltpu.VMEM((ts, 2 * D_LRU), BF16),
                        pltpu.VMEM((N_CHIP, D_MODEL, WIN_BLK), F32), pltpu.VMEM((D_MODEL, D_MODEL), F32),
                        pltpu.VMEM((D_LRU, 2 * D_LRU), F32),
                        ext, ext, ext, ext, ext, pltpu.VMEM((ts, D_LRU), F32),
                        pltpu.VMEM((HALO, D_LRU), F32), pltpu.VMEM((HALO, D_LRU), F32),
                        pltpu.SemaphoreType.DMA((2 * N_CHIP,))],
        compiler_params=pltpu.CompilerParams(dimension_semantics=("arbitrary",), vmem_limit_bytes=VMEM_LIMIT),
    )(chip, dx1, x, mixed, proj, proj, hl, hl, hb, ycat, mod, vecd, vecl, *win, *wout, gab, a64)


def _wgrad_mlp(collective_id, h2b, dz, act, dmo):
    s = h2b.shape[0]
    nstep = 2 * N_CHIP
    half = FF_BLK // 2

    def body(h2_ref, dz_ref, act_ref, dmo_ref, p1_hbm, p2_hbm, buf, landed, summed, send_sems, recv_sems, out_sems):
        j = pl.program_id(0)
        x, y, c, _ = _position()

        def give(jj):
            return pltpu.make_async_remote_copy(
                src_ref=buf.at[jj % 2, pl.ds((1 - c) * half, half), :], dst_ref=landed.at[jj],
                send_sem=send_sems.at[jj % 2], recv_sem=recv_sems.at[jj],
                device_id=(x, y, 1 - c), device_id_type=MESH)

        def write_out(jj, dst):
            return pltpu.make_async_copy(summed.at[jj % 2], dst, out_sems.at[jj % 2])

        def add_sibling(jj):
            give(jj).wait_recv()
            own = buf[jj % 2, pl.ds(pl.multiple_of(c * half, half), half), :]
            summed[jj % 2] = (own.astype(F32) + landed[jj].astype(F32)).astype(BF16)

        @pl.when(j == 0)
        def _():
            pl.semaphore_signal(pltpu.get_barrier_semaphore(), inc=1, device_id=(x, y, 1 - c), device_id_type=MESH)

        @pl.when(j >= 2)
        def _():
            give(j - 2).wait_send()

        @pl.when(j < N_CHIP)
        def _():
            buf[j % 2] = _dot_tn(act_ref[...], dmo_ref[...]).astype(BF16)

        @pl.when(j >= N_CHIP)
        def _():
            buf[j % 2] = _dot_tn(h2_ref[...], dz_ref[...]).astype(BF16)

        @pl.when(j == 0)
        def _():
            pl.semaphore_wait(pltpu.get_barrier_semaphore(), 1)

        give(j).start()

        @pl.when(j >= 1)
        def _():
            jm = j - 1

            @pl.when(jm >= 2)
            def _():
                write_out(jm - 2, p2_hbm.at[0]).wait()

            add_sibling(jm)

            @pl.when(jm < N_CHIP)
            def _():
                write_out(jm, p2_hbm.at[jm]).start()

            @pl.when(jm >= N_CHIP)
            def _():
                write_out(jm, p1_hbm.at[jm - N_CHIP]).start()

        @pl.when(j == nstep - 1)
        def _():
            last = nstep - 1
            write_out(last - 2, p1_hbm.at[0]).wait()
            add_sibling(last)
            write_out(last, p1_hbm.at[N_CHIP - 1]).start()
            for jj in (last - 1, last):
                give(jj).wait_send()
                write_out(jj, p1_hbm.at[0]).wait()

    sds = jax.ShapeDtypeStruct((N_CHIP, half, D_MODEL), BF16)
    whole = pl.BlockSpec((s, D_MODEL), lambda j: (0, 0))
    return pl.pallas_call(
        body, name="wgrad_mlp", grid=(nstep,),
        in_specs=[whole, pl.BlockSpec((s, FF_BLK), lambda j: (0, jnp.maximum(j - N_CHIP, 0))),
                  pl.BlockSpec((s, FF_BLK), lambda j: (0, jnp.minimum(j, N_CHIP - 1))), whole],
        out_specs=[ANY, ANY], out_shape=[sds, sds],
        scratch_shapes=[pltpu.VMEM((2, FF_BLK, D_MODEL), BF16), pltpu.VMEM((nstep, half, D_MODEL), BF16),
                        pltpu.VMEM((2, half, D_MODEL), BF16), pltpu.SemaphoreType.DMA((2,)),
                        pltpu.SemaphoreType.DMA((nstep,)), pltpu.SemaphoreType.DMA((2,))],
        compiler_params=pltpu.CompilerParams(dimension_semantics=("arbitrary",), vmem_limit_bytes=VMEM_LIMIT,
                                             collective_id=collective_id),
    )(h2b, dz, act, dmo)


def _mod_matmul(c_all, ada_w_loc):
    n = ada_w_loc.shape[1]
    cb = 512

    def body(c_ref, w_ref, o_ref):
        c = c_ref[...]
        sc = c * jax.nn.sigmoid(c)
        o_ref[...] = _dot(sc.astype(BF16), w_ref[...].astype(BF16))

    return pl.pallas_call(
        body, name="mod_matmul", grid=(n // cb,),
        in_specs=[_full((8, D_MODEL)), pl.BlockSpec((D_MODEL, cb), lambda j: (0, j))],
        out_specs=pl.BlockSpec((8, cb), lambda j: (0, j)),
        out_shape=jax.ShapeDtypeStruct((8, n), F32),
        compiler_params=pltpu.CompilerParams(dimension_semantics=("arbitrary",), vmem_limit_bytes=VMEM_LIMIT),
    )(c_all, ada_w_loc)


def _adam_math(w, g, m, v):
    m = ADAM_B1 * m + (1.0 - ADAM_B1) * g
    v = ADAM_B2 * v + (1.0 - ADAM_B2) * (g * g)
    m_hat = m / (1.0 - ADAM_B1 ** ADAM_STEP)
    v_hat = v / (1.0 - ADAM_B2 ** ADAM_STEP)
    delta = (-ADAM_LR) * (m_hat / (jnp.sqrt(v_hat) + ADAM_EPS) + ADAM_WD * w)
    return delta, m, v


def _adam(name, core, shards):
    n = len(shards)
    r, c = shards[0][0].shape
    half = r // 2
    rb = min(half, 128)
    nh = half // rb

    def body(core_ref, *refs):
        ins, outs = refs[:5 * n], refs[5 * n:]
        mine = (pl.program_id(0) // nh) == core_ref[0]
        for k in range(n):
            w_ref, go_ref, gs_ref, m_ref, v_ref = ins[5 * k:5 * k + 5]
            g_ref, d_ref, mo_ref, vo_ref = outs[4 * k:4 * k + 4]
            g = jnp.where(mine, go_ref[...], gs_ref[...])
            g_ref[...] = g
            d_ref[...], mo_ref[...], vo_ref[...] = _adam_math(w_ref[...], g, m_ref[...], v_ref[...])

    spec = pl.BlockSpec((rb, c), lambda i, core_ref: (i, 0))
    own = pl.BlockSpec((rb, c), lambda i, core_ref: (jnp.where(i // nh == core_ref[0], i % nh, 0), 0))
    sib = pl.BlockSpec((rb, c), lambda i, core_ref: (jnp.where(i // nh == core_ref[0], 0, i % nh), 0))
    sds = jax.ShapeDtypeStruct((r, c), F32)
    res = pl.pallas_call(
        body, name=name,
        grid_spec=pltpu.PrefetchScalarGridSpec(
            num_scalar_prefetch=1, grid=(r // rb,),
            in_specs=[spec, own, sib, spec, spec] * n, out_specs=[spec] * (4 * n)),
        out_shape=[sds] * (4 * n),
        compiler_params=pltpu.CompilerParams(dimension_semantics=("arbitrary",), vmem_limit_bytes=VMEM_LIMIT),
    )(core, *[t for s in shards for t in s])
    return [res[4 * k:4 * k + 4] for k in range(n)]


def _ada_grad_adam(chip, sct, dmod_cols, w, m, v):
    r, c = w.shape
    rb = 256

    def body(chip_ref, s_ref, dm_ref, w_ref, m_ref, v_ref, g_ref, d_ref, mo_ref, vo_ref):
        g = s_ref[:, 0:1] * dm_ref[0:1, :]
        for b in range(1, 8):
            g = g + s_ref[:, b:b + 1] * dm_ref[b:b + 1, :]
        g_ref[...] = g
        d_ref[...], mo_ref[...], vo_ref[...] = _adam_math(w_ref[...], g, m_ref[...], v_ref[...])

    spec = pl.BlockSpec((rb, c), lambda i, chip_ref: (i, 0))
    sds = jax.ShapeDtypeStruct((r, c), F32)
    return pl.pallas_call(
        body, name="ada_grad_adam",
        grid_spec=pltpu.PrefetchScalarGridSpec(
            num_scalar_prefetch=1, grid=(r // rb,),
            in_specs=[pl.BlockSpec((rb, 8), lambda i, chip_ref: (i, 0)),
                      pl.BlockSpec((8, c), lambda i, chip_ref: (0, chip_ref[0])), spec, spec, spec],
            out_specs=[spec] * 4),
        out_shape=[sds] * 4,
        compiler_params=pltpu.CompilerParams(dimension_semantics=("arbitrary",), vmem_limit_bytes=VMEM_LIMIT),
    )(chip, sct, dmod_cols, w, m, v)


def _position():
    x, y, c = lax.axis_index("x"), lax.axis_index("y"), lax.axis_index("c")
    chips = [(1 - x, y), (x, 1 - y), (1 - x, 1 - y)]
    return x, y, c, chips


def _ag8_run(ins, outs, send_sems, recv_sems, local_sems):
    na = len(ins)
    x, y, c, chips = _position()
    me, sibling = (x, y, c), (x, y, 1 - c)
    first, passed, local = [], [], []
    for a in range(na):
        m_per = ins[a].shape[0]

        def rows(px, py, pc, a=a, m_per=m_per):
            return outs[a].at[pl.ds((4 * px + 2 * py + pc) * m_per, m_per), :]

        def copy(k, block, to, src=None, a=a, rows=rows):
            return pltpu.make_async_remote_copy(
                src_ref=rows(*block) if src is None else src, dst_ref=rows(*block),
                send_sem=send_sems.at[7 * a + k], recv_sem=recv_sems.at[7 * a + k],
                device_id=to, device_id_type=MESH)

        mine = pltpu.make_async_copy(ins[a], rows(*me), local_sems.at[a])
        mine.start()
        local.append(mine)
        f = [copy(0, me, sibling, src=ins[a])]
        f += [copy(1 + j, me, (*chip, c), src=ins[a]) for j, chip in enumerate(chips)]
        for cp in f:
            cp.start()
        first.append((f, copy))
    for a in range(na):
        f, copy = first[a]
        p = [copy(4 + j, (*chip, c), sibling) for j, chip in enumerate(chips)]
        for j, chip in enumerate(chips):
            copy(1 + j, (*chip, c), me).wait_recv()
            p[j].start()
        passed.append(p)
    for a in range(na):
        f, copy = first[a]
        copy(0, sibling, me).wait_recv()
        for j, chip in enumerate(chips):
            copy(4 + j, (*chip, 1 - c), me).wait_recv()
        for cp in f + passed[a]:
            cp.wait_send()
        local[a].wait()


def _allgather8_seq(name, collective_id, arrs):
    na = len(arrs)
    hbm = pltpu.MemorySpace.HBM
    ins = [jax.new_ref(a, memory_space=hbm) for a in arrs]
    outs = [jax.empty_ref(jax.ShapeDtypeStruct((8 * a.shape[0], a.shape[1]), a.dtype), memory_space=hbm) for a in arrs]

    @pl.kernel(mesh=plsc.ScalarSubcoreMesh(axis_name="sequencer", num_cores=1), name=name,
               scratch_types=(pltpu.SemaphoreType.DMA((7 * na,)), pltpu.SemaphoreType.DMA((7 * na,)),
                              pltpu.SemaphoreType.DMA((na,))),
               compiler_params=pltpu.CompilerParams(collective_id=collective_id))
    def launch(send_sems, recv_sems, local_sems):
        x, y, c, chips = _position()
        peers = [(x, y, 1 - c)] + [(*chip, c) for chip in chips]
        barrier = pltpu.get_barrier_semaphore()
        for peer in peers:
            pl.semaphore_signal(barrier, inc=1, device_id=peer, device_id_type=MESH)
        pl.semaphore_wait(barrier, len(peers))
        _ag8_run(ins, outs, send_sems, recv_sems, local_sems)

    launch()
    return [o[...] for o in outs]


def _allgather8(name, arrs):
    na = len(arrs)

    def body(*refs):
        _ag8_run(refs[:na], refs[na:2 * na], *refs[2 * na:])

    return pl.pallas_call(
        body, name=name,
        out_shape=[jax.ShapeDtypeStruct((8 * a.shape[0], a.shape[1]), a.dtype) for a in arrs],
        in_specs=[VMEM] * na, out_specs=[VMEM] * na,
        scratch_shapes=[pltpu.SemaphoreType.DMA((7 * na,)), pltpu.SemaphoreType.DMA((7 * na,)),
                        pltpu.SemaphoreType.DMA((na,))],
        compiler_params=pltpu.CompilerParams(vmem_limit_bytes=VMEM_LIMIT),
    )(*arrs)


AG_SEMS = 7
AG_CHUNKS = 2


def _ag_copies(ins, outs, send_sems, recv_sems):
    x, y, c, chips = _position()
    sibling = (x, y, 1 - c)
    xn, yn, dg = [2 * chip[0] + chip[1] for chip in chips]
    to_x, to_y = (1 - x, y, c), (x, 1 - y, c)
    res = []
    for a in range(len(ins)):
        half = ins[a].shape[0] // 2
        piece = half // AG_CHUNKS
        for p in range(AG_CHUNKS):
            def copy(k, dst, to, src=None, base=AG_SEMS * (AG_CHUNKS * a + p)):
                return pltpu.make_async_remote_copy(
                    src_ref=dst if src is None else src, dst_ref=dst,
                    send_sem=send_sems.at[base + k], recv_sem=recv_sems.at[base + k],
                    device_id=to, device_id_type=MESH)

            def rows(chip, pc, q=None, a=a, start=p * piece, half=half, piece=piece):
                if q is None:
                    return outs[a].at[chip, pl.ds(pc * half + start, piece), :]
                return outs[a].at[chip, pl.ds(pc * half + start + q * (piece // 2), piece // 2), :]

            own = ins[a].at[pl.ds(c * half + p * piece, piece), :]
            mine = rows(2 * x + y, c)
            res.append(dict(
                sends=[copy(0, mine, to_x, src=own), copy(1, mine, to_y, src=own)],
                from_x=copy(0, rows(xn, c), to_x), from_y=copy(1, rows(yn, c), to_y),
                relay_y=copy(2, rows(xn, c, 0), to_y), relay_x=copy(3, rows(yn, c, 1), to_x),
                from_y_relay=copy(2, rows(dg, c, 0), to_y), from_x_relay=copy(3, rows(dg, c, 1), to_x),
                pass_on=[copy(4, rows(xn, c), sibling), copy(5, rows(yn, c), sibling), copy(6, rows(dg, c), sibling)],
                from_sibling=[copy(4, rows(xn, 1 - c), sibling), copy(5, rows(yn, 1 - c), sibling),
                              copy(6, rows(dg, 1 - c), sibling)]))
    return res


def _ag_start(ins, outs, send_sems, recv_sems):
    for cps in _ag_copies(ins, outs, send_sems, recv_sems):
        for cp in cps["sends"]:
            cp.start()


def _ag_relay(ins, outs, send_sems, recv_sems):
    for cps in _ag_copies(ins, outs, send_sems, recv_sems):
        cps["from_x"].wait_recv()
        cps["relay_y"].start()
        cps["pass_on"][0].start()
        cps["from_y"].wait_recv()
        cps["relay_x"].start()
        cps["pass_on"][1].start()


def _ag_complete(ins, outs, send_sems, recv_sems):
    copies = _ag_copies(ins, outs, send_sems, recv_sems)
    for cps in copies:
        cps["from_y_relay"].wait_recv()
        cps["from_x_relay"].wait_recv()
        cps["pass_on"][2].start()
    for cps in copies:
        for cp in cps["from_sibling"]:
            cp.wait_recv()
        for cp in cps["sends"] + [cps["relay_y"], cps["relay_x"]] + cps["pass_on"]:
            cp.wait_send()


def _ag_finish(ins, outs, send_sems, recv_sems):
    _ag_relay(ins, outs, send_sems, recv_sems)
    _ag_complete(ins, outs, send_sems, recv_sems)


def _allgather_weights(name, collective_id, shards):
    na = len(shards)
    hbm = pltpu.MemorySpace.HBM
    ins = [jax.new_ref(s, memory_space=hbm) for s in shards]
    outs = [jax.empty_ref(jax.ShapeDtypeStruct((N_CHIP,) + s.shape, s.dtype), memory_space=hbm) for s in shards]

    @pl.kernel(mesh=plsc.ScalarSubcoreMesh(axis_name="sequencer", num_cores=1), name=name,
               scratch_types=(pltpu.SemaphoreType.DMA((AG_SEMS * AG_CHUNKS * na,)),
                              pltpu.SemaphoreType.DMA((AG_SEMS * AG_CHUNKS * na,))),
               compiler_params=pltpu.CompilerParams(collective_id=collective_id))
    def launch(send_sems, recv_sems):
        x, y, c, _ = _position()
        peers = [(1 - x, y, c), (x, 1 - y, c), (x, y, 1 - c)]
        barrier = pltpu.get_barrier_semaphore()
        for peer in peers:
            pl.semaphore_signal(barrier, inc=1, device_id=peer, device_id_type=MESH)
        pl.semaphore_wait(barrier, len(peers))
        _ag_start(ins, outs, send_sems, recv_sems)
        _ag_finish(ins, outs, send_sems, recv_sems)

    launch()
    return [o[...] for o in outs]


def _sibling_swap(name, arrs, split_rows, collective_id=None):
    na = len(arrs)
    shapes = [jax.ShapeDtypeStruct((a.shape[0], a.shape[1] // 2, a.shape[2]) if split_rows else a.shape, a.dtype)
              for a in arrs]

    def run(ins, outs, send_sems, recv_sems):
        x, y, c, _ = _position()
        cps = []
        for a in range(na):
            src = ins[a]
            if split_rows:
                half = src.shape[1] // 2
                src = src.at[:, pl.ds((1 - c) * half, half), :]
            cp = pltpu.make_async_remote_copy(
                src_ref=src, dst_ref=outs[a], send_sem=send_sems.at[a], recv_sem=recv_sems.at[a],
                device_id=(x, y, 1 - c), device_id_type=MESH)
            cp.start()
            cps.append(cp)
        for cp in cps:
            cp.wait()

    sems = (pltpu.SemaphoreType.DMA((na,)), pltpu.SemaphoreType.DMA((na,)))
    if collective_id is None:
        return pl.pallas_call(
            lambda *refs: run(refs[:na], refs[na:2 * na], *refs[2 * na:]), name=name, out_shape=shapes,
            in_specs=[ANY] * na, out_specs=[ANY] * na, scratch_shapes=list(sems))(*arrs)

    hbm = pltpu.MemorySpace.HBM
    ins = [jax.new_ref(a, memory_space=hbm) for a in arrs]
    outs = [jax.empty_ref(s, memory_space=hbm) for s in shapes]

    @pl.kernel(mesh=plsc.ScalarSubcoreMesh(axis_name="sequencer", num_cores=1), name=name, scratch_types=sems,
               compiler_params=pltpu.CompilerParams(collective_id=collective_id))
    def launch(send_sems, recv_sems):
        x, y, c, _ = _position()
        barrier = pltpu.get_barrier_semaphore()
        pl.semaphore_signal(barrier, inc=1, device_id=(x, y, 1 - c), device_id_type=MESH)
        pl.semaphore_wait(barrier, 1)
        run(ins, outs, send_sems, recv_sems)

    launch()
    return [o[...] for o in outs]


def _xchg_copies(ins, outs, send_sems, recv_sems):
    x, y, c, chips = _position()
    return [pltpu.make_async_remote_copy(
        src_ref=ins[a].at[2 * chip[0] + chip[1]], dst_ref=outs[a].at[j],
        send_sem=send_sems.at[3 * a + j], recv_sem=recv_sems.at[3 * a + j],
        device_id=(*chip, c), device_id_type=MESH) for a in range(len(ins)) for j, chip in enumerate(chips)]


def _exchange_chips(name, collective_id, parts):
    na = len(parts)
    hbm = pltpu.MemorySpace.HBM
    ins = [jax.new_ref(p, memory_space=hbm) for p in parts]
    outs = [jax.empty_ref(jax.ShapeDtypeStruct((3,) + p.shape[1:], p.dtype), memory_space=hbm) for p in parts]

    @pl.kernel(mesh=plsc.ScalarSubcoreMesh(axis_name="sequencer", num_cores=1), name=name,
               scratch_types=(pltpu.SemaphoreType.DMA((3 * na,)), pltpu.SemaphoreType.DMA((3 * na,))),
               compiler_params=pltpu.CompilerParams(collective_id=collective_id))
    def launch(send_sems, recv_sems):
        x, y, c, chips = _position()
        barrier = pltpu.get_barrier_semaphore()
        for chip in chips:
            pl.semaphore_signal(barrier, inc=1, device_id=(*chip, c), device_id_type=MESH)
        pl.semaphore_wait(barrier, len(chips))
        for cp in _xchg_copies(ins, outs, send_sems, recv_sems):
            cp.start()
        for cp in _xchg_copies(ins, outs, send_sems, recv_sems):
            cp.wait()

    launch()
    return [q[...] for q in outs]


def _add_sibling(name, grad, recv, core, after=()):
    _, r, c = grad.shape
    half = r // 2
    rb = half
    nrb = half // rb

    def body(core_ref, g_ref, r_ref, *refs):
        refs[-1][...] = (g_ref[...].astype(F32) + r_ref[...].astype(F32)).astype(BF16)

    return pl.pallas_call(
        body, name=name,
        grid_spec=pltpu.PrefetchScalarGridSpec(
            num_scalar_prefetch=1, grid=(N_CHIP, nrb),
            in_specs=[pl.BlockSpec((1, rb, c), lambda j, i, core_ref: (j, core_ref[0] * nrb + i, 0)),
                      pl.BlockSpec((1, rb, c), lambda j, i, core_ref: (j, i, 0))] + [ANY] * len(after),
            out_specs=pl.BlockSpec((1, rb, c), lambda j, i, core_ref: (j, i, 0))),
        out_shape=jax.ShapeDtypeStruct((N_CHIP, half, c), BF16),
        compiler_params=pltpu.CompilerParams(dimension_semantics=("arbitrary", "arbitrary"),
                                             vmem_limit_bytes=VMEM_LIMIT),
    )(core, grad, recv, *after)


def _add_chips(name, chip, p, q, after=()):
    _, half, c = q.shape
    rb = min(half, 256)

    def body(chip_ref, p_ref, q_ref, *refs):
        acc = p_ref[0].astype(F32)
        for j in range(3):
            acc = acc + q_ref[j].astype(F32)
        refs[-1][...] = acc

    return pl.pallas_call(
        body, name=name,
        grid_spec=pltpu.PrefetchScalarGridSpec(
            num_scalar_prefetch=1, grid=(half // rb,),
            in_specs=[pl.BlockSpec((1, rb, c), lambda i, chip_ref: (chip_ref[0], i, 0)),
                      pl.BlockSpec((3, rb, c), lambda i, chip_ref: (0, i, 0))] + [ANY] * len(after),
            out_specs=pl.BlockSpec((rb, c), lambda i, chip_ref: (i, 0))),
        out_shape=jax.ShapeDtypeStruct((half, c), F32),
        compiler_params=pltpu.CompilerParams(dimension_semantics=("arbitrary",), vmem_limit_bytes=VMEM_LIMIT),
    )(chip, p, q, *after)


def _small_update(gad, gam, gl, gg, mychip, params):
    names = ["ada_b", "norm1_g", "lru_conv_b", "gate_a_w", "gate_a_b", "gate_x_w", "gate_x_b", "a_param",
             "lru_conv_w", "short_conv_w", "lru_out_g", "conv_out_g", "norm2_g", "final_g"]
    flat = [t for n in names for t in params[n]]
    nin = len(flat)

    def body(chip_ref, gad_ref, gam_ref, gl_ref, gg_ref, *refs):
        ins = {n: refs[3 * k:3 * k + 3] for k, n in enumerate(names)}
        outs = {n: refs[nin + 4 * k:nin + 4 * k + 4] for k, n in enumerate(names)}
        loss_ref, dmod_ref = refs[nin + 4 * len(names):nin + 4 * len(names) + 2]

        def dsum(ref, lo, n):
            per = ref.shape[0] // 8
            acc = ref[lo:lo + n, :].astype(F32)
            for dev in range(1, 8):
                acc = acc + ref[dev * per + lo:dev * per + lo + n, :].astype(F32)
            return acc

        def update(n, g):
            w_ref, m_ref, v_ref = ins[n]
            g_ref, d_ref, mo_ref, vo_ref = outs[n]
            g = g.reshape(w_ref.shape)
            g_ref[...] = g
            d_ref[...], mo_ref[...], vo_ref[...] = _adam_math(w_ref[...], g, m_ref[...], v_ref[...])

        d, dm, l, lw, rows = refs[-5:]
        d[...] = dsum(gad_ref, 0, 8)
        dm[...] = dsum(gam_ref, 0, 8)
        l[...] = dsum(gl_ref, 0, 16)
        for dev in range(8):
            for k in range(3):
                dmod_ref[dev:dev + 1, k * D_MODEL:(k + 1) * D_MODEL] = gad_ref[dev * 8 + k:dev * 8 + k + 1, :]
                dmod_ref[dev:dev + 1, (3 + k) * D_MODEL:(4 + k) * D_MODEL] = gam_ref[dev * 8 + k:dev * 8 + k + 1, :]
        w_ref, m_ref, v_ref = ins["ada_b"]
        g_ref, d_ref, mo_ref, vo_ref = outs["ada_b"]
        for k in range(3):
            g_ref[:, k * D_MODEL:(k + 1) * D_MODEL] = d[k:k + 1, :]
            g_ref[:, (3 + k) * D_MODEL:(4 + k) * D_MODEL] = dm[k:k + 1, :]
        d_ref[...], mo_ref[...], vo_ref[...] = _adam_math(w_ref[...], g_ref[...], m_ref[...], v_ref[...])
        update("norm1_g", d[3:4, :])
        update("norm2_g", dm[3:4, :])
        update("final_g", dm[4:5, :])
        update("gate_a_b", d[4:5, 0:D_LRU])
        update("gate_x_b", d[4:5, D_LRU:2 * D_LRU])
        update("lru_conv_b", l[4:5, :])
        update("a_param", l[8:9, :] * jax.nn.sigmoid(ins["a_param"][0][...]))
        update("lru_out_g", l[9:10, :])
        update("conv_out_g", l[10:11, :])
        loss_ref[...] = dm[5:6, 0:1]
        chip = chip_ref[0]
        acc = jnp.zeros((8, 128), F32)
        for j in range(N_CHIP):
            acc = acc + jnp.where(chip == j, l[0:8, j * 128:(j + 1) * 128], 0.0)
        lw[...] = acc
        update("lru_conv_w", lw[0:4, :])
        for s, ref in enumerate(ins["short_conv_w"]):
            for k in range(3):
                rows[s, k:k + 1, :] = ref[k]
        g = lw[5:8, :]
        new = (g,) + tuple(_adam_math(rows[0, 0:3, :], g, rows[1, 0:3, :], rows[2, 0:3, :]))
        for o_ref, val in zip(outs["short_conv_w"], new):
            rows[3, 0:3, :] = val
            for k in range(3):
                o_ref[k] = rows[3, k:k + 1, :]
        gates = dsum(gg_ref, 0, D_LRU)
        update("gate_a_w", gates[:, 0:HEAD])
        update("gate_x_w", gates[:, HEAD:2 * HEAD])

    out_shape = []
    for n in names:
        out_shape += [jax.ShapeDtypeStruct(params[n][0].shape, F32)] * 4
    out_shape += [jax.ShapeDtypeStruct((1, 1), F32), jax.ShapeDtypeStruct((8, 6 * D_MODEL), F32)]
    res = pl.pallas_call(
        body, name="small_update", out_shape=out_shape,
        in_specs=[SMEM] + [VMEM] * (4 + nin),
        out_specs=[VMEM] * len(out_shape),
        scratch_shapes=[pltpu.VMEM((8, D_MODEL), F32), pltpu.VMEM((8, D_MODEL), F32), pltpu.VMEM((16, D_LRU), F32),
                        pltpu.VMEM((8, 128), F32), pltpu.VMEM((4, 8, 128), F32)],
        compiler_params=pltpu.CompilerParams(vmem_limit_bytes=VMEM_LIMIT),
    )(mychip, gad, gam, gl, gg, *flat)
    per = {n: res[4 * k:4 * k + 4] for k, n in enumerate(names)}
    return per, res[-2], res[-1]


def _block_diag(w):
    eye = jnp.eye(8, dtype=w.dtype)
    return (eye[:, None, :, None] * w[:, :, None, :]).reshape(8 * HEAD, 8 * HEAD)


def _diag_blocks(g):
    return jnp.concatenate([g[h * HEAD:(h + 1) * HEAD, h * HEAD:(h + 1) * HEAD] for h in range(8)], axis=0)


def kernel(x, c, ada_w, ada_b, norm1_g, w_in, lru_conv_w, lru_conv_b, gate_a_w, gate_a_b, gate_x_w, gate_x_b, a_param, short_conv_w, lru_out_g, conv_out_g, w_out, norm2_g, w_mlp1, w_mlp2, final_g, loss_target, m_ada_w, m_ada_b, m_norm1_g, m_w_in, m_lru_conv_w, m_lru_conv_b, m_gate_a_w, m_gate_a_b, m_gate_x_w, m_gate_x_b, m_a_param, m_short_conv_w, m_lru_out_g, m_conv_out_g, m_w_out, m_norm2_g, m_w_mlp1, m_w_mlp2, m_final_g, v_ada_w, v_ada_b, v_norm1_g, v_w_in, v_lru_conv_w, v_lru_conv_b, v_gate_a_w, v_gate_a_b, v_gate_x_w, v_gate_x_b, v_a_param, v_short_conv_w, v_lru_out_g, v_conv_out_g, v_w_out, v_norm2_g, v_w_mlp1, v_w_mlp2, v_final_g):
    xi, yi, ci = lax.axis_index("x"), lax.axis_index("y"), lax.axis_index("c")
    mychip = 2 * xi + yi
    me = 4 * xi + 2 * yi + ci

    own_in, own_out = w_in[0].astype(BF16), w_out[0].astype(BF16)
    win_all, wout_all = _allgather_weights("allgather_mixer_weights", 1, [own_in, own_out])
    own_w1, own_w2 = w_mlp1[0].astype(BF16), w_mlp2[0].astype(BF16)
    w1_all, w2_all = _allgather_weights("allgather_mlp_weights", 2, [own_w1, own_w2])

    c_blk = jnp.zeros((8, D_MODEL), F32).at[0:1].set(c)
    cw_blk = jnp.zeros((8, 128), F32).at[0:4].set(lru_conv_w[0]).at[4:7].set(short_conv_w[0])
    c_g, cw_g = _allgather8("allgather_cond", [c_blk, cw_blk])
    c_all = c_g.reshape(8, 8, D_MODEL)[:, 0]
    cw_g = cw_g.reshape(4, 2, 8, 128)[:, 0]
    lcw = cw_g[:, 0:4].transpose(1, 0, 2).reshape(4, D_LRU)
    scw = cw_g[:, 4:7].transpose(1, 0, 2).reshape(3, D_LRU)

    mod_loc = _mod_matmul(c_all, ada_w[0])
    (mod_g,) = _allgather8("allgather_mod", [mod_loc])
    mod_all = mod_g.reshape(4, 2, 8, 6 * D_MODEL // 4)[:, 0].transpose(1, 0, 2).reshape(8, 6 * D_MODEL) + ada_b
    mod_pad = jnp.pad(mod_all.reshape(8, 6, D_MODEL), ((0, 0), (0, 2), (0, 0)))
    mod = lax.dynamic_slice_in_dim(mod_pad, me, 1, axis=0).reshape(8, D_MODEL)

    win, wout = (win_all, own_in), (wout_all, own_out)
    chip = mychip.reshape(1).astype(jnp.int32)
    core = ci.reshape(1).astype(jnp.int32)

    vecd = jnp.concatenate([norm1_g, norm2_g, final_g[None, :], jnp.concatenate([gate_a_b, gate_x_b], axis=1),
                            jnp.zeros((4, D_MODEL), F32)], axis=0)
    vecl = jnp.concatenate([lcw, lru_conv_b, scw, a_param, lru_out_g, conv_out_g, jnp.zeros((5, D_LRU), F32)], axis=0)
    gab = jnp.concatenate([_block_diag(gate_a_w[0]), _block_diag(gate_x_w[0])], axis=1).astype(BF16)
    a64 = _block_diag(jnp.full((8, HEAD, HEAD), 1.0 / HEAD, F32)).astype(BF16)

    hb, proj, hl, ycat, mixed, x1 = _mix_fwd(chip, x[0], mod, vecd, vecl, win, wout, gab, a64)
    dx1, act, dz, dmo, h2b, accm = _mlp_fwd_bwd(
        chip, x1, loss_target[0], mod, vecd, (w1_all, own_w1), (w2_all, own_w2))

    parts_mlp = list(_wgrad_mlp(6, h2b, dz, act, dmo))
    q_w1, q_w2 = _exchange_chips("rs_exchange_mlp", 0, parts_mlp)
    grad_x, accd, accl, g_win, g_wout, g_gate = _mix_bwd(
        chip, dx1, x[0], mixed, proj, hl, hb, ycat, mod, vecd, vecl, win, wout, gab, a64)

    g_mix = [g_win, g_wout.reshape(N_CHIP, WOUT_BLK, D_MODEL)]
    recv_mix = _sibling_swap("rs_swap_halves_mix", g_mix, True, collective_id=4)
    own_mlp = [_add_chips("rs_add_chips_mlp%d" % k, chip, p, q) for k, (p, q) in enumerate(zip(parts_mlp, (q_w1, q_w2)))]
    sib_mlp = _sibling_swap("rs_swap_reduced_mlp", own_mlp, False, collective_id=5)
    gg_blk = jnp.concatenate([_diag_blocks(g_gate[:, 0:D_LRU]), _diag_blocks(g_gate[:, D_LRU:2 * D_LRU])], axis=1)
    gad, gam, gl, gg = _allgather8_seq("allgather_small_grads", 8, [accd, accm, accl, gg_blk.astype(BF16)])

    parts_mix = [_add_sibling("rs_add_sibling_mix%d" % k, g, r, core)
                 for k, (g, r) in enumerate(zip(g_mix, recv_mix))]
    landed_mix = _exchange_chips("rs_exchange_mix", 3, parts_mix)
    res_w1, res_w2 = _adam("adam_mlp", core, [(w_mlp1[0], own_mlp[0], sib_mlp[0], m_w_mlp1[0], v_w_mlp1[0]),
                                              (w_mlp2[0], own_mlp[1], sib_mlp[1], m_w_mlp2[0], v_w_mlp2[0])])
    own_mix = [_add_chips("rs_add_chips_mix%d" % k, chip, p, q, after=[res_w1[1]])
               for k, (p, q) in enumerate(zip(parts_mix, landed_mix))]
    sib_mix = _sibling_swap("rs_swap_reduced_mix", own_mix, False)
    (res_win,) = _adam("adam_w_in", core, [(w_in[0], own_mix[0], sib_mix[0], m_w_in[0], v_w_in[0])])
    (res_wout,) = _adam("adam_w_out", core, [(w_out[0], own_mix[1], sib_mix[1], m_w_out[0], v_w_out[0])])

    params = {
        "ada_b": (ada_b, m_ada_b, v_ada_b), "norm1_g": (norm1_g, m_norm1_g, v_norm1_g),
        "lru_conv_b": (lru_conv_b, m_lru_conv_b, v_lru_conv_b),
        "gate_a_w": tuple(t.reshape(D_LRU, HEAD) for t in (gate_a_w, m_gate_a_w, v_gate_a_w)),
        "gate_a_b": (gate_a_b, m_gate_a_b, v_gate_a_b),
        "gate_x_w": tuple(t.reshape(D_LRU, HEAD) for t in (gate_x_w, m_gate_x_w, v_gate_x_w)),
        "gate_x_b": (gate_x_b, m_gate_x_b, v_gate_x_b), "a_param": (a_param, m_a_param, v_a_param),
        "lru_conv_w": (lru_conv_w, m_lru_conv_w, v_lru_conv_w),
        "short_conv_w": tuple(t.reshape(3, 1, D_LRU // N_CHIP) for t in (short_conv_w, m_short_conv_w, v_short_conv_w)),
        "lru_out_g": (lru_out_g, m_lru_out_g, v_lru_out_g), "conv_out_g": (conv_out_g, m_conv_out_g, v_conv_out_g),
        "norm2_g": (norm2_g, m_norm2_g, v_norm2_g),
        "final_g": tuple(t[None, :] for t in (final_g, m_final_g, v_final_g)),
    }
    small, loss_blk, dmod_cols = _small_update(gad, gam, gl, gg, chip, params)
    loss = loss_blk.reshape(())

    sct = (c_all * jax.nn.sigmoid(c_all)).T
    ada = _ada_grad_adam(chip, sct, dmod_cols, ada_w[0], m_ada_w[0], v_ada_w[0])

    res = {"ada_w": ada, "w_in": res_win, "w_out": res_wout, "w_mlp1": res_w1, "w_mlp2": res_w2}
    res = {n: tuple(t[None] for t in r) for n, r in res.items()}
    shapes = {"gate_a_w": gate_a_w.shape, "gate_x_w": gate_x_w.shape, "lru_conv_w": lru_conv_w.shape,
              "short_conv_w": short_conv_w.shape, "final_g": final_g.shape}
    for n, t in small.items():
        res[n] = tuple(u.reshape(shapes[n]) if n in shapes else u for u in t)

    order = ["ada_w", "ada_b", "norm1_g", "w_in", "lru_conv_w", "lru_conv_b", "gate_a_w", "gate_a_b", "gate_x_w",
             "gate_x_b", "a_param", "short_conv_w", "lru_out_g", "conv_out_g", "w_out", "norm2_g", "w_mlp1",
             "w_mlp2", "final_g"]
    return (loss, grad_x[None], *[res[n][0] for n in order], *[res[n][1] for n in order],
            *[res[n][2] for n in order], *[res[n][3] for n in order])
```

```python
import jax
import jax.numpy as jnp
from jax import lax
from jax.experimental import pallas as pl
from jax.experimental.pallas import tpu as pltpu
from jax.experimental.pallas import tpu_sc as plsc

F32 = jnp.float32
BF16 = jnp.bfloat16

D_MODEL = 1024
D_LRU = 512
D_IN = 2560
D_FF = 4096
N_CHIP = 4
WIN_BLK = D_IN // N_CHIP
WOUT_BLK = D_MODEL // N_CHIP
FF_BLK = D_FF // N_CHIP
HEAD = 64
EPS = 1e-6
C_GATE = 8.0
TOKEN_TILE = 256
MIX_FWD_TILE = 512
HALO = 8
VMEM_LIMIT = 60 * 1024 * 1024

ADAM_LR = 0.001
ADAM_B1 = 0.9
ADAM_B2 = 0.999
ADAM_EPS = 1e-08
ADAM_WD = 0.01
ADAM_STEP = 10

MESH = pl.DeviceIdType.MESH
ANY = pl.BlockSpec(memory_space=pl.ANY)
VMEM = pl.BlockSpec(memory_space=pltpu.VMEM)
SMEM = pl.BlockSpec(memory_space=pltpu.SMEM)


def _full(shape, single=False):
    nd = len(shape)
    if single:
        return pl.BlockSpec(shape, lambda *_: (0,) * nd, pipeline_mode=pl.Buffered(1))
    return pl.BlockSpec(shape, lambda *_: (0,) * nd)


def _dot(a, b):
    return jnp.dot(a, b, preferred_element_type=F32)


def _dot_nt(a, b):
    return lax.dot_general(a, b, (((1,), (1,)), ((), ())), preferred_element_type=F32)


def _dot_tn(a, b):
    return lax.dot_general(a, b, (((0,), (0,)), ((), ())), preferred_element_type=F32)


def _gmean(v, a64):
    hi = v.astype(BF16)
    lo = (v - hi.astype(F32)).astype(BF16)
    return _dot(hi, a64) + _dot(lo, a64)


def _gelu(x):
    u = 0.7978845608028654 * (x + 0.044715 * x * x * x)
    t = jnp.tanh(u)
    return 0.5 * x * (1.0 + t), t


def _gelu_grad(x, t):
    du = 0.7978845608028654 * (1.0 + 3.0 * 0.044715 * x * x)
    return 0.5 * (1.0 + t) + 0.5 * x * (1.0 - t * t) * du


def _log1p_pos(y):
    return jnp.where(y < 1e-2, y * (1.0 - y * (0.5 - y * (1.0 / 3.0 - y * 0.25))), jnp.log(1.0 + y))


def _softplus(a):
    return jnp.maximum(a, 0.0) + _log1p_pos(jnp.exp(-jnp.abs(a)))


def _neg_expm1(z):
    series = -z * (1.0 + z * (0.5 + z * (1.0 / 6.0 + z * (1.0 / 24.0))))
    return jnp.where(z > -0.02, series, 1.0 - jnp.exp(z))


def _scan_fwd(a, b, row):
    n = a.shape[0]
    d = 1
    while d < n:
        m = row >= d
        b = jnp.where(m, a * pltpu.roll(b, d, 0) + b, b)
        a = jnp.where(m, a * pltpu.roll(a, d, 0), a)
        d *= 2
    return a, b


def _scan_rev(a, b, row):
    n = a.shape[0]
    d = 1
    while d < n:
        m = row < n - d
        b = jnp.where(m, b + a * pltpu.roll(b, n - d, 0), b)
        a = jnp.where(m, a * pltpu.roll(a, n - d, 0), a)
        d *= 2
    return a, b


def _colsum(v):
    return jnp.sum(v, axis=0, keepdims=True)


def _load_gathered(chip, gathered, own, slot, sems):
    copies = []
    for j in range(N_CHIP):
        @pl.when(chip == j)
        def _(j=j):
            pltpu.make_async_copy(own, slot(j), sems.at[j]).start()

        @pl.when(chip != j)
        def _(j=j):
            pltpu.make_async_copy(gathered.at[j], slot(j), sems.at[j]).start()

        copies.append(pltpu.make_async_copy(own, slot(j), sems.at[j]))
    return copies


def _lru_gates(xlb, gab, gbias, sp, first_row):
    g = _dot(xlb, gab) + gbias
    r = jax.nn.sigmoid(g[:, :D_LRU])
    ig = jax.nn.sigmoid(g[:, D_LRU:])
    la = (-C_GATE) * r * sp
    a = jnp.exp(la)
    msq = jnp.sqrt(_neg_expm1(2.0 * la))
    mult = jnp.where(first_row, 1.0, msq)
    return r, ig, a, msq, mult


def _mix_fwd(chip, x, mod, vecd, vecl, win, wout, gab, a64):
    s = x.shape[0]
    ts = MIX_FWD_TILE
    nt = s // ts

    def body(chip_ref, x_ref, mod_ref, vd_ref, vl_ref, win_hbm, win_own, wout_hbm, wout_own, gab_ref, a64_ref,
             hb_ref, proj_ref, hl_ref, ycat_ref, mixed_ref, x1_ref,
             win_ref, wout_ref, ext_lx, ext_cv, hcar, sems):
        i = pl.program_id(0)

        @pl.when(i == 0)
        def _():
            cps = _load_gathered(chip_ref[0], win_hbm, win_own, lambda j: win_ref.at[j], sems.at[pl.ds(0, N_CHIP)])
            cps += _load_gathered(chip_ref[0], wout_hbm, wout_own,
                                  lambda j: wout_ref.at[pl.ds(j * WOUT_BLK, WOUT_BLK), :],
                                  sems.at[pl.ds(N_CHIP, N_CHIP)])
            ext_lx[0:HALO, :] = jnp.zeros((HALO, D_LRU), F32)
            ext_cv[0:HALO, :] = jnp.zeros((HALO, D_LRU), F32)
            hcar[...] = jnp.zeros_like(hcar)
            for cp in cps:
                cp.wait()

        row = lax.broadcasted_iota(jnp.int32, (ts, D_LRU), 0)
        first_row = jnp.logical_and(row == 0, i == 0)
        xt = x_ref[...]
        shift1, scale1, gate1 = mod_ref[0:1, :], mod_ref[1:2, :], mod_ref[2:3, :]
        r1 = lax.rsqrt(jnp.mean(xt * xt, axis=-1, keepdims=True) + EPS)
        h = (xt * r1) * vd_ref[0:1, :] * (1.0 + scale1) + shift1
        hb = h.astype(BF16)
        hb_ref[...] = hb
        for j in range(N_CHIP):
            proj_ref[:, j * WIN_BLK:(j + 1) * WIN_BLK] = _dot(hb, win_ref[j])
        u_ly = proj_ref[:, 512:1024]
        u_b = proj_ref[:, 1024:1536]

        ext_lx[HALO:HALO + ts, :] = proj_ref[:, 0:512]
        xl = vl_ref[4:5, :] + vl_ref[0:1, :] * ext_lx[pl.ds(5, ts), :]
        for k in range(1, 4):
            xl = xl + vl_ref[k:k + 1, :] * ext_lx[pl.ds(5 + k, ts), :]
        ext_lx[0:HALO, :] = ext_lx[ts:ts + HALO, :]
        sp = _softplus(vl_ref[8:9, :])
        _, ig, a, _, mult = _lru_gates(xl.astype(BF16), gab_ref[...], vd_ref[3:4, :], sp, first_row)
        acum, hloc = _scan_fwd(a, mult * (ig * xl), row)
        hl = hloc + acum * hcar[0:1, :]
        hl_ref[...] = hl
        hcar[0:1, :] = hl_ref[ts - 1:ts, :]
        ge, _ = _gelu(u_ly)
        p = ge * hl
        y_lru = p * lax.rsqrt(_gmean(p * p, a64_ref[...]) + EPS) * vl_ref[9:10, :]
        ycat_ref[:, 0:512] = y_lru.astype(BF16)

        ext_cv[HALO:HALO + ts, :] = proj_ref[:, 1536:2048] * proj_ref[:, 2048:2560]
        q = vl_ref[5:6, :] * ext_cv[pl.ds(6, ts), :]
        for k in range(1, 3):
            q = q + vl_ref[5 + k:6 + k, :] * ext_cv[pl.ds(6 + k, ts), :]
        ext_cv[0:HALO, :] = ext_cv[ts:ts + HALO, :]
        yc = u_b * q
        y_conv = yc * lax.rsqrt(_gmean(yc * yc, a64_ref[...]) + EPS) * vl_ref[10:11, :]
        ycat_ref[:, 512:1024] = y_conv.astype(BF16)

        mixed = _dot(ycat_ref[...], wout_ref[...])
        mixed_ref[...] = mixed
        x1_ref[...] = xt + gate1 * mixed

    tile = lambda w: pl.BlockSpec((ts, w), lambda i: (i, 0))
    return pl.pallas_call(
        body, name="mix_fwd", grid=(nt,),
        in_specs=[SMEM, tile(D_MODEL), _full((8, D_MODEL)), _full((8, D_MODEL)), _full((16, D_LRU)),
                  ANY, ANY, ANY, ANY, _full((D_LRU, 2 * D_LRU), True), _full((D_LRU, D_LRU), True)],
        out_specs=[tile(D_MODEL), tile(D_IN), tile(D_LRU), tile(D_MODEL), tile(D_MODEL), tile(D_MODEL)],
        out_shape=[jax.ShapeDtypeStruct((s, D_MODEL), BF16), jax.ShapeDtypeStruct((s, D_IN), F32),
                   jax.ShapeDtypeStruct((s, D_LRU), F32), jax.ShapeDtypeStruct((s, D_MODEL), BF16),
                   jax.ShapeDtypeStruct((s, D_MODEL), F32), jax.ShapeDtypeStruct((s, D_MODEL), F32)],
        scratch_shapes=[pltpu.VMEM((N_CHIP, D_MODEL, WIN_BLK), BF16), pltpu.VMEM((D_MODEL, D_MODEL), BF16),
                        pltpu.VMEM((ts + HALO, D_LRU), F32), pltpu.VMEM((ts + HALO, D_LRU), F32),
                        pltpu.VMEM((HALO, D_LRU), F32), pltpu.SemaphoreType.DMA((2 * N_CHIP,))],
        compiler_params=pltpu.CompilerParams(dimension_semantics=("arbitrary",), vmem_limit_bytes=VMEM_LIMIT),
    )(chip, x, mod, vecd, vecl, *win, *wout, gab, a64)


def _mlp_fwd_bwd(chip, x1, target, mod, vecd, w1, w2):
    s = x1.shape[0]
    ts = TOKEN_TILE
    nt = s // ts

    def body(chip_ref, x1_ref, tg_ref, mod_ref, vd_ref, w1_hbm, w1_own, w2_hbm, w2_own,
             dx1_ref, act_ref, dz_ref, dmo_ref, h2_ref, acc_ref, w1_v, w2_v, rz_v, sems):
        i = pl.program_id(0)

        @pl.when(i == 0)
        def _():
            cps = _load_gathered(chip_ref[0], w1_hbm, w1_own, lambda j: w1_v.at[j], sems.at[pl.ds(0, N_CHIP)])
            cps += _load_gathered(chip_ref[0], w2_hbm, w2_own, lambda j: w2_v.at[j], sems.at[pl.ds(N_CHIP, N_CHIP)])
            acc_ref[...] = jnp.zeros_like(acc_ref)
            for cp in cps:
                cp.wait()

        xt = x1_ref[...]
        shift2, scale2, gate2 = mod_ref[3:4, :], mod_ref[4:5, :], mod_ref[5:6, :]
        g2, gf = vd_ref[1:2, :], vd_ref[2:3, :]
        r2 = lax.rsqrt(jnp.mean(xt * xt, axis=-1, keepdims=True) + EPS)
        n2 = xt * r2
        h2b = (n2 * g2 * (1.0 + scale2) + shift2).astype(BF16)
        h2_ref[...] = h2b
        for j in range(N_CHIP):
            rz_v[j] = jnp.maximum(_dot(h2b, w1_v[j]), 0.0)
        mo = jnp.zeros((ts, D_MODEL), F32)
        for j in range(N_CHIP):
            rz = rz_v[j]
            actb = (rz * rz).astype(BF16)
            act_ref[:, j * FF_BLK:(j + 1) * FF_BLK] = actb
            mo = mo + _dot(actb, w2_v[j])
        x2 = xt + gate2 * mo
        r3 = lax.rsqrt(jnp.mean(x2 * x2, axis=-1, keepdims=True) + EPS)
        n3 = x2 * r3
        e = n3 * gf - tg_ref[...]
        loss = (0.5 / D_MODEL) * jnp.sum(_colsum(e * e), axis=1, keepdims=True)
        dy = e * (1.0 / D_MODEL)
        acc_ref[4:5, :] += _colsum(dy * n3)
        acc_ref[5:6, :] += jnp.broadcast_to(loss, (1, D_MODEL))
        dn3 = dy * gf
        dx2 = r3 * (dn3 - n3 * jnp.mean(dn3 * n3, axis=-1, keepdims=True))
        acc_ref[2:3, :] += _colsum(dx2 * mo)
        dmob = (dx2 * gate2).astype(BF16)
        dmo_ref[...] = dmob
        for j in range(N_CHIP):
            dz_ref[:, j * FF_BLK:(j + 1) * FF_BLK] = (_dot_nt(dmob, w2_v[j]) * (2.0 * rz_v[j])).astype(BF16)
        dh2 = jnp.zeros((ts, D_MODEL), F32)
        for j in range(N_CHIP):
            dh2 = dh2 + _dot_nt(dz_ref[:, j * FF_BLK:(j + 1) * FF_BLK], w1_v[j])
        acc_ref[1:2, :] += _colsum(dh2 * (n2 * g2))
        acc_ref[0:1, :] += _colsum(dh2)
        dhn2 = dh2 * (1.0 + scale2)
        acc_ref[3:4, :] += _colsum(dhn2 * n2)
        dn2 = dhn2 * g2
        dx1_ref[...] = dx2 + r2 * (dn2 - n2 * jnp.mean(dn2 * n2, axis=-1, keepdims=True))

    tile = lambda w: pl.BlockSpec((ts, w), lambda i: (i, 0))
    return pl.pallas_call(
        body, name="mlp_fwd_bwd", grid=(nt,),
        in_specs=[SMEM, tile(D_MODEL), tile(D_MODEL), _full((8, D_MODEL)), _full((8, D_MODEL)), ANY, ANY, ANY, ANY],
        out_specs=[tile(D_MODEL), tile(D_FF), tile(D_FF), tile(D_MODEL), tile(D_MODEL), _full((8, D_MODEL))],
        out_shape=[jax.ShapeDtypeStruct((s, D_MODEL), F32), jax.ShapeDtypeStruct((s, D_FF), BF16),
                   jax.ShapeDtypeStruct((s, D_FF), BF16), jax.ShapeDtypeStruct((s, D_MODEL), BF16),
                   jax.ShapeDtypeStruct((s, D_MODEL), BF16), jax.ShapeDtypeStruct((8, D_MODEL), F32)],
        scratch_shapes=[pltpu.VMEM((N_CHIP, D_MODEL, FF_BLK), BF16), pltpu.VMEM((N_CHIP, FF_BLK, D_MODEL), BF16),
                        pltpu.VMEM((N_CHIP, ts, FF_BLK), F32), pltpu.SemaphoreType.DMA((2 * N_CHIP,))],
        compiler_params=pltpu.CompilerParams(dimension_semantics=("arbitrary",), vmem_limit_bytes=VMEM_LIMIT),
    )(chip, x1, target, mod, vecd, *w1, *w2)


def _mix_bwd(chip, dx1, x, mixed, proj, hl, hb, ycat, mod, vecd, vecl, win, wout, gab, a64):
    s = x.shape[0]
    ts = TOKEN_TILE
    nt = s // ts
    hpt = ts // HALO

    def body(chip_ref, dx1_ref, x_ref, mixed_ref, proj_ref, projh_ref, hl_ref, hlh_ref, hb_ref, ycat_ref,
             mod_ref, vd_ref, vl_ref, win_hbm, win_own, wout_hbm, wout_own, gab_ref, a64_ref,
             gx_ref, accd_ref, accl_ref, gwin_hbm, gwout_hbm, ggate_hbm,
             win_ref, wout_ref, dproj_ref, dgb_ref, gwin_acc, gwout_acc, ggate_acc,
             ext_lx, ext_cv, ext_hl, ext_dxl, ext_dq, gbuf, gcar, acar, sems):
        i = pl.program_id(0)
        ri = nt - 1 - i

        @pl.when(i == 0)
        def _():
            gwin_acc[...] = jnp.zeros_like(gwin_acc)
            gwout_acc[...] = jnp.zeros_like(gwout_acc)
            ggate_acc[...] = jnp.zeros_like(ggate_acc)
            cps = _load_gathered(chip_ref[0], win_hbm, win_own, lambda j: win_ref.at[j], sems.at[pl.ds(0, N_CHIP)])
            cps += _load_gathered(chip_ref[0], wout_hbm, wout_own,
                                  lambda j: wout_ref.at[pl.ds(j * WOUT_BLK, WOUT_BLK), :],
                                  sems.at[pl.ds(N_CHIP, N_CHIP)])
            for cp in cps:
                cp.wait()
            accd_ref[...] = jnp.zeros_like(accd_ref)
            accl_ref[...] = jnp.zeros_like(accl_ref)
            ext_dxl[ts:ts + HALO, :] = jnp.zeros((HALO, D_LRU), F32)
            ext_dq[ts:ts + HALO, :] = jnp.zeros((HALO, D_LRU), F32)
            gcar[...] = jnp.zeros_like(gcar)
            acar[...] = jnp.zeros_like(acar)

        row = lax.broadcasted_iota(jnp.int32, (ts, D_LRU), 0)
        first_row = jnp.logical_and(row == 0, ri == 0)
        halo_on = jnp.where(ri == 0, 0.0, 1.0)
        shift1, scale1, gate1 = mod_ref[0:1, :], mod_ref[1:2, :], mod_ref[2:3, :]
        g1 = vd_ref[0:1, :]
        a64m = a64_ref[...]
        lg, cg = vl_ref[9:10, :], vl_ref[10:11, :]

        dx1 = dx1_ref[...]
        accd_ref[2:3, :] += _colsum(dx1 * mixed_ref[...])
        dmb = (dx1 * gate1).astype(BF16)
        gwout_acc[...] += _dot_tn(ycat_ref[...], dmb)
        dycat = _dot_nt(dmb, wout_ref[...])
        dyl = dycat[:, 0:512]
        dyv = dycat[:, 512:1024]

        u_ly = proj_ref[:, 512:1024]
        u_b = proj_ref[:, 1024:1536]
        u_c = proj_ref[:, 1536:2048]
        u_v = proj_ref[:, 2048:2560]
        ext_lx[0:HALO, :] = projh_ref[:, 0:512] * halo_on
        ext_lx[HALO:HALO + ts, :] = proj_ref[:, 0:512]
        xl = vl_ref[4:5, :] + vl_ref[0:1, :] * ext_lx[pl.ds(5, ts), :]
        for k in range(1, 4):
            xl = xl + vl_ref[k:k + 1, :] * ext_lx[pl.ds(5 + k, ts), :]
        xlb = xl.astype(BF16)
        sp = _softplus(vl_ref[8:9, :])
        r, ig, a, msq, mult = _lru_gates(xlb, gab_ref[...], vd_ref[3:4, :], sp, first_row)
        hl = hl_ref[...]
        ge, th = _gelu(u_ly)
        p = ge * hl
        rl = lax.rsqrt(_gmean(p * p, a64m) + EPS)
        nl = p * rl
        ext_cv[0:HALO, :] = projh_ref[:, 1536:2048] * projh_ref[:, 2048:2560] * halo_on
        ext_cv[HALO:HALO + ts, :] = u_c * u_v
        q = vl_ref[5:6, :] * ext_cv[pl.ds(6, ts), :]
        for k in range(1, 3):
            q = q + vl_ref[5 + k:6 + k, :] * ext_cv[pl.ds(6 + k, ts), :]
        yc = u_b * q
        rc = lax.rsqrt(_gmean(yc * yc, a64m) + EPS)
        nc = yc * rc

        accl_ref[9:10, :] += _colsum(dyl * nl)
        dnl = dyl * lg
        dp = rl * (dnl - nl * _gmean(dnl * nl, a64m))
        dproj_ref[:, 512:1024] = ((dp * hl) * _gelu_grad(u_ly, th)).astype(BF16)
        a_next = jnp.where(row == ts - 1, acar[0:1, :], pltpu.roll(a, ts - 1, 0))
        acum, gloc = _scan_rev(a_next, dp * ge, row)
        gbuf[...] = gloc + acum * gcar[0:1, :]
        gcar[0:1, :] = gbuf[0:1, :]
        ext_hl[0:HALO, :] = hlh_ref[...] * halo_on
        ext_hl[HALO:HALO + ts, :] = hl
        acar[...] = a[0:HALO, :]
        gt = gbuf[...]
        da = gt * ext_hl[pl.ds(HALO - 1, ts), :]
        dmult = gt * ig * xl
        di = gt * mult * xl
        dxl = gt * mult * ig
        dla = da * a - jnp.where(first_row, 0.0, dmult * a * a / msq)
        accl_ref[8:9, :] += _colsum(dla * ((-C_GATE) * r))
        dra = dla * ((-C_GATE) * sp) * r * (1.0 - r)
        dia = di * ig * (1.0 - ig)
        accd_ref[4:5, 0:D_LRU] += _colsum(dra)
        accd_ref[4:5, D_LRU:2 * D_LRU] += _colsum(dia)
        dgb_ref[:, 0:D_LRU] = dra.astype(BF16)
        dgb_ref[:, D_LRU:2 * D_LRU] = dia.astype(BF16)
        dxl = dxl + _dot_nt(dgb_ref[...], gab_ref[...])
        ggate_acc[...] += _dot_tn(xlb, dgb_ref[...])
        accl_ref[4:5, :] += _colsum(dxl)
        for k in range(4):
            accl_ref[k:k + 1, :] += _colsum(dxl * ext_lx[pl.ds(5 + k, ts), :])
        ext_dxl[0:ts, :] = dxl
        du_lx = vl_ref[0:1, :] * ext_dxl[pl.ds(3, ts), :]
        for k in range(1, 4):
            du_lx = du_lx + vl_ref[k:k + 1, :] * ext_dxl[pl.ds(3 - k, ts), :]
        ext_dxl[ts:ts + HALO, :] = ext_dxl[0:HALO, :]
        dproj_ref[:, 0:512] = du_lx.astype(BF16)

        accl_ref[10:11, :] += _colsum(dyv * nc)
        dnc = dyv * cg
        dyc = rc * (dnc - nc * _gmean(dnc * nc, a64m))
        dproj_ref[:, 1024:1536] = (dyc * q).astype(BF16)
        dq = dyc * u_b
        for k in range(3):
            accl_ref[5 + k:6 + k, :] += _colsum(dq * ext_cv[pl.ds(6 + k, ts), :])
        ext_dq[0:ts, :] = dq
        dcv = vl_ref[5:6, :] * ext_dq[pl.ds(2, ts), :]
        for k in range(1, 3):
            dcv = dcv + vl_ref[5 + k:6 + k, :] * ext_dq[pl.ds(2 - k, ts), :]
        ext_dq[ts:ts + HALO, :] = ext_dq[0:HALO, :]
        dproj_ref[:, 1536:2048] = (dcv * u_v).astype(BF16)
        dproj_ref[:, 2048:2560] = (dcv * u_c).astype(BF16)

        dh = _dot_nt(dproj_ref[:, 0:WIN_BLK], win_ref[0])
        for j in range(1, N_CHIP):
            dh = dh + _dot_nt(dproj_ref[:, j * WIN_BLK:(j + 1) * WIN_BLK], win_ref[j])
        for j in range(N_CHIP):
            gwin_acc[j] += _dot_tn(hb_ref[...], dproj_ref[:, j * WIN_BLK:(j + 1) * WIN_BLK])
        xt = x_ref[...]
        r1 = lax.rsqrt(jnp.mean(xt * xt, axis=-1, keepdims=True) + EPS)
        n1 = xt * r1
        accd_ref[1:2, :] += _colsum(dh * (n1 * g1))
        accd_ref[0:1, :] += _colsum(dh)
        dhn1 = dh * (1.0 + scale1)
        accd_ref[3:4, :] += _colsum(dhn1 * n1)
        dn1 = dhn1 * g1
        gx_ref[...] = dx1 + r1 * (dn1 - n1 * jnp.mean(dn1 * n1, axis=-1, keepdims=True))

        @pl.when(i == nt - 1)
        def _():
            outs = [pltpu.make_async_copy(acc, dst, sems.at[k]) for k, (acc, dst) in enumerate(
                ((gwin_acc, gwin_hbm), (gwout_acc, gwout_hbm), (ggate_acc, ggate_hbm)))]
            for cp in outs:
                cp.start()
            for cp in outs:
                cp.wait()

    tile = lambda w: pl.BlockSpec((ts, w), lambda i: (nt - 1 - i, 0))
    halo = lambda w: pl.BlockSpec((HALO, w), lambda i: (jnp.maximum((nt - 1 - i) * hpt - 1, 0), 0))
    ext = pltpu.VMEM((ts + HALO, D_LRU), F32)
    return pl.pallas_call(
        body, name="mix_bwd", grid=(nt,),
        in_specs=[SMEM, tile(D_MODEL), tile(D_MODEL), tile(D_MODEL), tile(D_IN), halo(D_IN), tile(D_LRU), halo(D_LRU),
                  tile(D_MODEL), tile(D_MODEL), _full((8, D_MODEL)), _full((8, D_MODEL)), _full((16, D_LRU)),
                  ANY, ANY, ANY, ANY, _full((D_LRU, 2 * D_LRU), True), _full((D_LRU, D_LRU), True)],
        out_specs=[tile(D_MODEL), _full((8, D_MODEL)), _full((16, D_LRU)), ANY, ANY, ANY],
        out_shape=[jax.ShapeDtypeStruct((s, D_MODEL), F32),
                   jax.ShapeDtypeStruct((8, D_MODEL), F32), jax.ShapeDtypeStruct((16, D_LRU), F32),
                   jax.ShapeDtypeStruct((N_CHIP, D_MODEL, WIN_BLK), F32), jax.ShapeDtypeStruct((D_MODEL, D_MODEL), F32),
                   jax.ShapeDtypeStruct((D_LRU, 2 * D_LRU), F32)],
        scratch_shapes=[pltpu.VMEM((N_CHIP, D_MODEL, WIN_BLK), BF16), pltpu.VMEM((D_MODEL, D_MODEL), BF16),
                        pltpu.VMEM((ts, D_IN), BF16), pltpu.VMEM((ts, 2 * D_LRU), BF16),
                        pltpu.VMEM((N_CHIP, D_MODEL, WIN_BLK), F32), pltpu.VMEM((D_MODEL, D_MODEL), F32),
                        pltpu.VMEM((D_LRU, 2 * D_LRU), F32),
                        ext, ext, ext, ext, ext, pltpu.VMEM((ts, D_LRU), F32),
                        pltpu.VMEM((HALO, D_LRU), F32), pltpu.VMEM((HALO, D_LRU), F32),
                        pltpu.SemaphoreType.DMA((2 * N_CHIP,))],
        compiler_params=pltpu.CompilerParams(dimension_semantics=("arbitrary",), vmem_limit_bytes=VMEM_LIMIT),
    )(chip, dx1, x, mixed, proj, proj, hl, hl, hb, ycat, mod, vecd, vecl, *win, *wout, gab, a64)


def _wgrad_mlp(collective_id, h2b, dz, act, dmo):
    s = h2b.shape[0]
    nstep = 2 * N_CHIP
    half = FF_BLK // 2

    def body(h2_ref, dz_ref, act_ref, dmo_ref, p1_hbm, p2_hbm, buf, landed, summed, send_sems, recv_sems, out_sems):
        j = pl.program_id(0)
        x, y, c, _ = _position()

        def give(jj):
            return pltpu.make_async_remote_copy(
                src_ref=buf.at[jj % 2, pl.ds((1 - c) * half, half), :], dst_ref=landed.at[jj],
                send_sem=send_sems.at[jj % 2], recv_sem=recv_sems.at[jj],
                device_id=(x, y, 1 - c), device_id_type=MESH)

        def write_out(jj, dst):
            return pltpu.make_async_copy(summed.at[jj % 2], dst, out_sems.at[jj % 2])

        def add_sibling(jj):
            give(jj).wait_recv()
            own = buf[jj % 2, pl.ds(pl.multiple_of(c * half, half), half), :]
            summed[jj % 2] = (own.astype(F32) + landed[jj].astype(F32)).astype(BF16)

        @pl.when(j == 0)
        def _():
            pl.semaphore_signal(pltpu.get_barrier_semaphore(), inc=1, device_id=(x, y, 1 - c), device_id_type=MESH)

        @pl.when(j >= 2)
        def _():
            give(j - 2).wait_send()

        @pl.when(j < N_CHIP)
        def _():
            buf[j % 2] = _dot_tn(act_ref[...], dmo_ref[...]).astype(BF16)

        @pl.when(j >= N_CHIP)
        def _():
            buf[j % 2] = _dot_tn(h2_ref[...], dz_ref[...]).astype(BF16)

        @pl.when(j == 0)
        def _():
            pl.semaphore_wait(pltpu.get_barrier_semaphore(), 1)

        give(j).start()

        @pl.when(j >= 1)
        def _():
            jm = j - 1

            @pl.when(jm >= 2)
            def _():
                write_out(jm - 2, p2_hbm.at[0]).wait()

            add_sibling(jm)

            @pl.when(jm < N_CHIP)
            def _():
                write_out(jm, p2_hbm.at[jm]).start()

            @pl.when(jm >= N_CHIP)
            def _():
                write_out(jm, p1_hbm.at[jm - N_CHIP]).start()

        @pl.when(j == nstep - 1)
        def _():
            last = nstep - 1
            write_out(last - 2, p1_hbm.at[0]).wait()
            add_sibling(last)
            write_out(last, p1_hbm.at[N_CHIP - 1]).start()
            for jj in (last - 1, last):
                give(jj).wait_send()
                write_out(jj, p1_hbm.at[0]).wait()

    sds = jax.ShapeDtypeStruct((N_CHIP, half, D_MODEL), BF16)
    whole = pl.BlockSpec((s, D_MODEL), lambda j: (0, 0))
    return pl.pallas_call(
        body, name="wgrad_mlp", grid=(nstep,),
        in_specs=[whole, pl.BlockSpec((s, FF_BLK), lambda j: (0, jnp.maximum(j - N_CHIP, 0))),
                  pl.BlockSpec((s, FF_BLK), lambda j: (0, jnp.minimum(j, N_CHIP - 1))), whole],
        out_specs=[ANY, ANY], out_shape=[sds, sds],
        scratch_shapes=[pltpu.VMEM((2, FF_BLK, D_MODEL), BF16), pltpu.VMEM((nstep, half, D_MODEL), BF16),
                        pltpu.VMEM((2, half, D_MODEL), BF16), pltpu.SemaphoreType.DMA((2,)),
                        pltpu.SemaphoreType.DMA((nstep,)), pltpu.SemaphoreType.DMA((2,))],
        compiler_params=pltpu.CompilerParams(dimension_semantics=("arbitrary",), vmem_limit_bytes=VMEM_LIMIT,
                                             collective_id=collective_id),
    )(h2b, dz, act, dmo)


def _mod_matmul(c_all, ada_w_loc):
    n = ada_w_loc.shape[1]
    cb = 512

    def body(c_ref, w_ref, o_ref):
        c = c_ref[...]
        sc = c * jax.nn.sigmoid(c)
        o_ref[...] = _dot(sc.astype(BF16), w_ref[...].astype(BF16))

    return pl.pallas_call(
        body, name="mod_matmul", grid=(n // cb,),
        in_specs=[_full((8, D_MODEL)), pl.BlockSpec((D_MODEL, cb), lambda j: (0, j))],
        out_specs=pl.BlockSpec((8, cb), lambda j: (0, j)),
        out_shape=jax.ShapeDtypeStruct((8, n), F32),
        compiler_params=pltpu.CompilerParams(dimension_semantics=("arbitrary",), vmem_limit_bytes=VMEM_LIMIT),
    )(c_all, ada_w_loc)


def _adam_math(w, g, m, v):
    m = ADAM_B1 * m + (1.0 - ADAM_B1) * g
    v = ADAM_B2 * v + (1.0 - ADAM_B2) * (g * g)
    m_hat = m / (1.0 - ADAM_B1 ** ADAM_STEP)
    v_hat = v / (1.0 - ADAM_B2 ** ADAM_STEP)
    delta = (-ADAM_LR) * (m_hat / (jnp.sqrt(v_hat) + ADAM_EPS) + ADAM_WD * w)
    return delta, m, v


def _adam(name, core, shards):
    n = len(shards)
    r, c = shards[0][0].shape
    half = r // 2
    rb = min(half, 128)
    nh = half // rb

    def body(core_ref, *refs):
        ins, outs = refs[:5 * n], refs[5 * n:]
        mine = (pl.program_id(0) // nh) == core_ref[0]
        for k in range(n):
            w_ref, go_ref, gs_ref, m_ref, v_ref = ins[5 * k:5 * k + 5]
            g_ref, d_ref, mo_ref, vo_ref = outs[4 * k:4 * k + 4]
            g = jnp.where(mine, go_ref[...], gs_ref[...])
            g_ref[...] = g
            d_ref[...], mo_ref[...], vo_ref[...] = _adam_math(w_ref[...], g, m_ref[...], v_ref[...])

    spec = pl.BlockSpec((rb, c), lambda i, core_ref: (i, 0))
    own = pl.BlockSpec((rb, c), lambda i, core_ref: (jnp.where(i // nh == core_ref[0], i % nh, 0), 0))
    sib = pl.BlockSpec((rb, c), lambda i, core_ref: (jnp.where(i // nh == core_ref[0], 0, i % nh), 0))
    sds = jax.ShapeDtypeStruct((r, c), F32)
    res = pl.pallas_call(
        body, name=name,
        grid_spec=pltpu.PrefetchScalarGridSpec(
            num_scalar_prefetch=1, grid=(r // rb,),
            in_specs=[spec, own, sib, spec, spec] * n, out_specs=[spec] * (4 * n)),
        out_shape=[sds] * (4 * n),
        compiler_params=pltpu.CompilerParams(dimension_semantics=("arbitrary",), vmem_limit_bytes=VMEM_LIMIT),
    )(core, *[t for s in shards for t in s])
    return [res[4 * k:4 * k + 4] for k in range(n)]


def _ada_grad_adam(chip, sct, dmod_cols, w, m, v):
    r, c = w.shape
    rb = 256

    def body(chip_ref, s_ref, dm_ref, w_ref, m_ref, v_ref, g_ref, d_ref, mo_ref, vo_ref):
        g = s_ref[:, 0:1] * dm_ref[0:1, :]
        for b in range(1, 8):
            g = g + s_ref[:, b:b + 1] * dm_ref[b:b + 1, :]
        g_ref[...] = g
        d_ref[...], mo_ref[...], vo_ref[...] = _adam_math(w_ref[...], g, m_ref[...], v_ref[...])

    spec = pl.BlockSpec((rb, c), lambda i, chip_ref: (i, 0))
    sds = jax.ShapeDtypeStruct((r, c), F32)
    return pl.pallas_call(
        body, name="ada_grad_adam",
        grid_spec=pltpu.PrefetchScalarGridSpec(
            num_scalar_prefetch=1, grid=(r // rb,),
            in_specs=[pl.BlockSpec((rb, 8), lambda i, chip_ref: (i, 0)),
                      pl.BlockSpec((8, c), lambda i, chip_ref: (0, chip_ref[0])), spec, spec, spec],
            out_specs=[spec] * 4),
        out_shape=[sds] * 4,
        compiler_params=pltpu.CompilerParams(dimension_semantics=("arbitrary",), vmem_limit_bytes=VMEM_LIMIT),
    )(chip, sct, dmod_cols, w, m, v)


def _position():
    x, y, c = lax.axis_index("x"), lax.axis_index("y"), lax.axis_index("c")
    chips = [(1 - x, y), (x, 1 - y), (1 - x, 1 - y)]
    return x, y, c, chips


def _ag8_run(ins, outs, send_sems, recv_sems, local_sems):
    na = len(ins)
    x, y, c, chips = _position()
    me, sibling = (x, y, c), (x, y, 1 - c)
    first, passed, local = [], [], []
    for a in range(na):
        m_per = ins[a].shape[0]

        def rows(px, py, pc, a=a, m_per=m_per):
            return outs[a].at[pl.ds((4 * px + 2 * py + pc) * m_per, m_per), :]

        def copy(k, block, to, src=None, a=a, rows=rows):
            return pltpu.make_async_remote_copy(
                src_ref=rows(*block) if src is None else src, dst_ref=rows(*block),
                send_sem=send_sems.at[7 * a + k], recv_sem=recv_sems.at[7 * a + k],
                device_id=to, device_id_type=MESH)

        mine = pltpu.make_async_copy(ins[a], rows(*me), local_sems.at[a])
        mine.start()
        local.append(mine)
        f = [copy(0, me, sibling, src=ins[a])]
        f += [copy(1 + j, me, (*chip, c), src=ins[a]) for j, chip in enumerate(chips)]
        for cp in f:
            cp.start()
        first.append((f, copy))
    for a in range(na):
        f, copy = first[a]
        p = [copy(4 + j, (*chip, c), sibling) for j, chip in enumerate(chips)]
        for j, chip in enumerate(chips):
            copy(1 + j, (*chip, c), me).wait_recv()
            p[j].start()
        passed.append(p)
    for a in range(na):
        f, copy = first[a]
        copy(0, sibling, me).wait_recv()
        for j, chip in enumerate(chips):
            copy(4 + j, (*chip, 1 - c), me).wait_recv()
        for cp in f + passed[a]:
            cp.wait_send()
        local[a].wait()


def _allgather8(name, arrs):
    na = len(arrs)

    def body(*refs):
        _ag8_run(refs[:na], refs[na:2 * na], *refs[2 * na:])

    return pl.pallas_call(
        body, name=name,
        out_shape=[jax.ShapeDtypeStruct((8 * a.shape[0], a.shape[1]), a.dtype) for a in arrs],
        in_specs=[VMEM] * na, out_specs=[VMEM] * na,
        scratch_shapes=[pltpu.SemaphoreType.DMA((7 * na,)), pltpu.SemaphoreType.DMA((7 * na,)),
                        pltpu.SemaphoreType.DMA((na,))],
        compiler_params=pltpu.CompilerParams(vmem_limit_bytes=VMEM_LIMIT),
    )(*arrs)


AG_SEMS = 7
AG_CHUNKS = 2


def _ag_copies(ins, outs, send_sems, recv_sems):
    x, y, c, chips = _position()
    sibling = (x, y, 1 - c)
    xn, yn, dg = [2 * chip[0] + chip[1] for chip in chips]
    to_x, to_y = (1 - x, y, c), (x, 1 - y, c)
    res = []
    for a in range(len(ins)):
        half = ins[a].shape[0] // 2
        piece = half // AG_CHUNKS
        for p in range(AG_CHUNKS):
            def copy(k, dst, to, src=None, base=AG_SEMS * (AG_CHUNKS * a + p)):
                return pltpu.make_async_remote_copy(
                    src_ref=dst if src is None else src, dst_ref=dst,
                    send_sem=send_sems.at[base + k], recv_sem=recv_sems.at[base + k],
                    device_id=to, device_id_type=MESH)

            def rows(chip, pc, q=None, a=a, start=p * piece, half=half, piece=piece):
                if q is None:
                    return outs[a].at[chip, pl.ds(pc * half + start, piece), :]
                return outs[a].at[chip, pl.ds(pc * half + start + q * (piece // 2), piece // 2), :]

            own = ins[a].at[pl.ds(c * half + p * piece, piece), :]
            mine = rows(2 * x + y, c)
            res.append(dict(
                sends=[copy(0, mine, to_x, src=own), copy(1, mine, to_y, src=own)],
                from_x=copy(0, rows(xn, c), to_x), from_y=copy(1, rows(yn, c), to_y),
                relay_y=copy(2, rows(xn, c, 0), to_y), relay_x=copy(3, rows(yn, c, 1), to_x),
                from_y_relay=copy(2, rows(dg, c, 0), to_y), from_x_relay=copy(3, rows(dg, c, 1), to_x),
                pass_on=[copy(4, rows(xn, c), sibling), copy(5, rows(yn, c), sibling), copy(6, rows(dg, c), sibling)],
                from_sibling=[copy(4, rows(xn, 1 - c), sibling), copy(5, rows(yn, 1 - c), sibling),
                              copy(6, rows(dg, 1 - c), sibling)]))
    return res


def _ag_start(ins, outs, send_sems, recv_sems):
    for cps in _ag_copies(ins, outs, send_sems, recv_sems):
        for cp in cps["sends"]:
            cp.start()


def _ag_relay(ins, outs, send_sems, recv_sems):
    for cps in _ag_copies(ins, outs, send_sems, recv_sems):
        cps["from_x"].wait_recv()
        cps["relay_y"].start()
        cps["pass_on"][0].start()
        cps["from_y"].wait_recv()
        cps["relay_x"].start()
        cps["pass_on"][1].start()


def _ag_complete(ins, outs, send_sems, recv_sems):
    copies = _ag_copies(ins, outs, send_sems, recv_sems)
    for cps in copies:
        cps["from_y_relay"].wait_recv()
        cps["from_x_relay"].wait_recv()
        cps["pass_on"][2].start()
    for cps in copies:
        for cp in cps["from_sibling"]:
            cp.wait_recv()
        for cp in cps["sends"] + [cps["relay_y"], cps["relay_x"]] + cps["pass_on"]:
            cp.wait_send()


def _ag_finish(ins, outs, send_sems, recv_sems):
    _ag_relay(ins, outs, send_sems, recv_sems)
    _ag_complete(ins, outs, send_sems, recv_sems)


def _allgather_weights(name, collective_id, shards):
    na = len(shards)
    hbm = pltpu.MemorySpace.HBM
    ins = [jax.new_ref(s, memory_space=hbm) for s in shards]
    outs = [jax.empty_ref(jax.ShapeDtypeStruct((N_CHIP,) + s.shape, s.dtype), memory_space=hbm) for s in shards]

    @pl.kernel(mesh=plsc.ScalarSubcoreMesh(axis_name="sequencer", num_cores=1), name=name,
               scratch_types=(pltpu.SemaphoreType.DMA((AG_SEMS * AG_CHUNKS * na,)),
                              pltpu.SemaphoreType.DMA((AG_SEMS * AG_CHUNKS * na,))),
               compiler_params=pltpu.CompilerParams(collective_id=collective_id))
    def launch(send_sems, recv_sems):
        x, y, c, _ = _position()
        peers = [(1 - x, y, c), (x, 1 - y, c), (x, y, 1 - c)]
        barrier = pltpu.get_barrier_semaphore()
        for peer in peers:
            pl.semaphore_signal(barrier, inc=1, device_id=peer, device_id_type=MESH)
        pl.semaphore_wait(barrier, len(peers))
        _ag_start(ins, outs, send_sems, recv_sems)
        _ag_finish(ins, outs, send_sems, recv_sems)

    launch()
    return [o[...] for o in outs]


def _swap_copies(ins, outs, send_sems, recv_sems, split_rows):
    x, y, c, _ = _position()
    cps = []
    for a in range(len(ins)):
        src = ins[a]
        if split_rows:
            half = src.shape[1] // 2
            src = src.at[:, pl.ds((1 - c) * half, half), :]
        cps.append(pltpu.make_async_remote_copy(
            src_ref=src, dst_ref=outs[a], send_sem=send_sems.at[a], recv_sem=recv_sems.at[a],
            device_id=(x, y, 1 - c), device_id_type=MESH))
    return cps


def _swap_and_gather_seq(name, collective_id, swap_arrs, gather_arrs):
    ns, ng = len(swap_arrs), len(gather_arrs)
    hbm = pltpu.MemorySpace.HBM
    s_ins = [jax.new_ref(a, memory_space=hbm) for a in swap_arrs]
    s_outs = [jax.empty_ref(jax.ShapeDtypeStruct(a.shape, a.dtype), memory_space=hbm) for a in swap_arrs]
    g_ins = [jax.new_ref(a, memory_space=hbm) for a in gather_arrs]
    g_outs = [jax.empty_ref(jax.ShapeDtypeStruct((8 * a.shape[0], a.shape[1]), a.dtype), memory_space=hbm)
              for a in gather_arrs]

    @pl.kernel(mesh=plsc.ScalarSubcoreMesh(axis_name="sequencer", num_cores=1), name=name,
               scratch_types=(pltpu.SemaphoreType.DMA((ns,)), pltpu.SemaphoreType.DMA((ns,)),
                              pltpu.SemaphoreType.DMA((7 * ng,)), pltpu.SemaphoreType.DMA((7 * ng,)),
                              pltpu.SemaphoreType.DMA((ng,))),
               compiler_params=pltpu.CompilerParams(collective_id=collective_id))
    def launch(swap_send, swap_recv, send_sems, recv_sems, local_sems):
        x, y, c, chips = _position()
        peers = [(x, y, 1 - c)] + [(*chip, c) for chip in chips]
        barrier = pltpu.get_barrier_semaphore()
        for peer in peers:
            pl.semaphore_signal(barrier, inc=1, device_id=peer, device_id_type=MESH)
        pl.semaphore_wait(barrier, len(peers))
        for cp in _swap_copies(s_ins, s_outs, swap_send, swap_recv, False):
            cp.start()
        _ag8_run(g_ins, g_outs, send_sems, recv_sems, local_sems)
        for cp in _swap_copies(s_ins, s_outs, swap_send, swap_recv, False):
            cp.wait()

    launch()
    return [o[...] for o in s_outs], [o[...] for o in g_outs]


def _sibling_swap(name, arrs, split_rows, collective_id=None):
    na = len(arrs)
    shapes = [jax.ShapeDtypeStruct((a.shape[0], a.shape[1] // 2, a.shape[2]) if split_rows else a.shape, a.dtype)
              for a in arrs]

    def run(ins, outs, send_sems, recv_sems):
        for cp in _swap_copies(ins, outs, send_sems, recv_sems, split_rows):
            cp.start()
        for cp in _swap_copies(ins, outs, send_sems, recv_sems, split_rows):
            cp.wait()

    sems = (pltpu.SemaphoreType.DMA((na,)), pltpu.SemaphoreType.DMA((na,)))
    if collective_id is None:
        return pl.pallas_call(
            lambda *refs: run(refs[:na], refs[na:2 * na], *refs[2 * na:]), name=name, out_shape=shapes,
            in_specs=[ANY] * na, out_specs=[ANY] * na, scratch_shapes=list(sems))(*arrs)

    hbm = pltpu.MemorySpace.HBM
    ins = [jax.new_ref(a, memory_space=hbm) for a in arrs]
    outs = [jax.empty_ref(s, memory_space=hbm) for s in shapes]

    @pl.kernel(mesh=plsc.ScalarSubcoreMesh(axis_name="sequencer", num_cores=1), name=name, scratch_types=sems,
               compiler_params=pltpu.CompilerParams(collective_id=collective_id))
    def launch(send_sems, recv_sems):
        x, y, c, _ = _position()
        barrier = pltpu.get_barrier_semaphore()
        pl.semaphore_signal(barrier, inc=1, device_id=(x, y, 1 - c), device_id_type=MESH)
        pl.semaphore_wait(barrier, 1)
        run(ins, outs, send_sems, recv_sems)

    launch()
    return [o[...] for o in outs]


def _xchg_copies(ins, outs, send_sems, recv_sems):
    x, y, c, chips = _position()
    return [pltpu.make_async_remote_copy(
        src_ref=ins[a].at[2 * chip[0] + chip[1]], dst_ref=outs[a].at[j],
        send_sem=send_sems.at[3 * a + j], recv_sem=recv_sems.at[3 * a + j],
        device_id=(*chip, c), device_id_type=MESH) for a in range(len(ins)) for j, chip in enumerate(chips)]


def _exchange_chips(name, collective_id, parts):
    na = len(parts)
    hbm = pltpu.MemorySpace.HBM
    ins = [jax.new_ref(p, memory_space=hbm) for p in parts]
    outs = [jax.empty_ref(jax.ShapeDtypeStruct((3,) + p.shape[1:], p.dtype), memory_space=hbm) for p in parts]

    @pl.kernel(mesh=plsc.ScalarSubcoreMesh(axis_name="sequencer", num_cores=1), name=name,
               scratch_types=(pltpu.SemaphoreType.DMA((3 * na,)), pltpu.SemaphoreType.DMA((3 * na,))),
               compiler_params=pltpu.CompilerParams(collective_id=collective_id))
    def launch(send_sems, recv_sems):
        x, y, c, chips = _position()
        barrier = pltpu.get_barrier_semaphore()
        for chip in chips:
            pl.semaphore_signal(barrier, inc=1, device_id=(*chip, c), device_id_type=MESH)
        pl.semaphore_wait(barrier, len(chips))
        for cp in _xchg_copies(ins, outs, send_sems, recv_sems):
            cp.start()
        for cp in _xchg_copies(ins, outs, send_sems, recv_sems):
            cp.wait()

    launch()
    return [q[...] for q in outs]


def _add_sibling(name, grad, recv, core, after=()):
    _, r, c = grad.shape
    half = r // 2
    rb = half
    nrb = half // rb

    def body(core_ref, g_ref, r_ref, *refs):
        refs[-1][...] = (g_ref[...].astype(F32) + r_ref[...].astype(F32)).astype(BF16)

    return pl.pallas_call(
        body, name=name,
        grid_spec=pltpu.PrefetchScalarGridSpec(
            num_scalar_prefetch=1, grid=(N_CHIP, nrb),
            in_specs=[pl.BlockSpec((1, rb, c), lambda j, i, core_ref: (j, core_ref[0] * nrb + i, 0)),
                      pl.BlockSpec((1, rb, c), lambda j, i, core_ref: (j, i, 0))] + [ANY] * len(after),
            out_specs=pl.BlockSpec((1, rb, c), lambda j, i, core_ref: (j, i, 0))),
        out_shape=jax.ShapeDtypeStruct((N_CHIP, half, c), BF16),
        compiler_params=pltpu.CompilerParams(dimension_semantics=("arbitrary", "arbitrary"),
                                             vmem_limit_bytes=VMEM_LIMIT),
    )(core, grad, recv, *after)


def _add_chips(name, chip, p, q, after=()):
    _, half, c = q.shape
    rb = min(half, 256)

    def body(chip_ref, p_ref, q_ref, *refs):
        acc = p_ref[0].astype(F32)
        for j in range(3):
            acc = acc + q_ref[j].astype(F32)
        refs[-1][...] = acc

    return pl.pallas_call(
        body, name=name,
        grid_spec=pltpu.PrefetchScalarGridSpec(
            num_scalar_prefetch=1, grid=(half // rb,),
            in_specs=[pl.BlockSpec((1, rb, c), lambda i, chip_ref: (chip_ref[0], i, 0)),
                      pl.BlockSpec((3, rb, c), lambda i, chip_ref: (0, i, 0))] + [ANY] * len(after),
            out_specs=pl.BlockSpec((rb, c), lambda i, chip_ref: (i, 0))),
        out_shape=jax.ShapeDtypeStruct((half, c), F32),
        compiler_params=pltpu.CompilerParams(dimension_semantics=("arbitrary",), vmem_limit_bytes=VMEM_LIMIT),
    )(chip, p, q, *after)


def _small_update(gad, gam, gl, gg, mychip, params):
    names = ["ada_b", "norm1_g", "lru_conv_b", "gate_a_w", "gate_a_b", "gate_x_w", "gate_x_b", "a_param",
             "lru_conv_w", "short_conv_w", "lru_out_g", "conv_out_g", "norm2_g", "final_g"]
    flat = [t for n in names for t in params[n]]
    nin = len(flat)

    def body(chip_ref, gad_ref, gam_ref, gl_ref, gg_ref, *refs):
        ins = {n: refs[3 * k:3 * k + 3] for k, n in enumerate(names)}
        outs = {n: refs[nin + 4 * k:nin + 4 * k + 4] for k, n in enumerate(names)}
        loss_ref, dmod_ref = refs[nin + 4 * len(names):nin + 4 * len(names) + 2]

        def dsum(ref, lo, n):
            per = ref.shape[0] // 8
            acc = ref[lo:lo + n, :].astype(F32)
            for dev in range(1, 8):
                acc = acc + ref[dev * per + lo:dev * per + lo + n, :].astype(F32)
            return acc

        def update(n, g):
            w_ref, m_ref, v_ref = ins[n]
            g_ref, d_ref, mo_ref, vo_ref = outs[n]
            g = g.reshape(w_ref.shape)
            g_ref[...] = g
            d_ref[...], mo_ref[...], vo_ref[...] = _adam_math(w_ref[...], g, m_ref[...], v_ref[...])

        d, dm, l, lw, rows = refs[-5:]
        d[...] = dsum(gad_ref, 0, 8)
        dm[...] = dsum(gam_ref, 0, 8)
        l[...] = dsum(gl_ref, 0, 16)
        for dev in range(8):
            for k in range(3):
                dmod_ref[dev:dev + 1, k * D_MODEL:(k + 1) * D_MODEL] = gad_ref[dev * 8 + k:dev * 8 + k + 1, :]
                dmod_ref[dev:dev + 1, (3 + k) * D_MODEL:(4 + k) * D_MODEL] = gam_ref[dev * 8 + k:dev * 8 + k + 1, :]
        w_ref, m_ref, v_ref = ins["ada_b"]
        g_ref, d_ref, mo_ref, vo_ref = outs["ada_b"]
        for k in range(3):
            g_ref[:, k * D_MODEL:(k + 1) * D_MODEL] = d[k:k + 1, :]
            g_ref[:, (3 + k) * D_MODEL:(4 + k) * D_MODEL] = dm[k:k + 1, :]
        d_ref[...], mo_ref[...], vo_ref[...] = _adam_math(w_ref[...], g_ref[...], m_ref[...], v_ref[...])
        update("norm1_g", d[3:4, :])
        update("norm2_g", dm[3:4, :])
        update("final_g", dm[4:5, :])
        update("gate_a_b", d[4:5, 0:D_LRU])
        update("gate_x_b", d[4:5, D_LRU:2 * D_LRU])
        update("lru_conv_b", l[4:5, :])
        update("a_param", l[8:9, :] * jax.nn.sigmoid(ins["a_param"][0][...]))
        update("lru_out_g", l[9:10, :])
        update("conv_out_g", l[10:11, :])
        loss_ref[...] = dm[5:6, 0:1]
        chip = chip_ref[0]
        acc = jnp.zeros((8, 128), F32)
        for j in range(N_CHIP):
            acc = acc + jnp.where(chip == j, l[0:8, j * 128:(j + 1) * 128], 0.0)
        lw[...] = acc
        update("lru_conv_w", lw[0:4, :])
        for s, ref in enumerate(ins["short_conv_w"]):
            for k in range(3):
                rows[s, k:k + 1, :] = ref[k]
        g = lw[5:8, :]
        new = (g,) + tuple(_adam_math(rows[0, 0:3, :], g, rows[1, 0:3, :], rows[2, 0:3, :]))
        for o_ref, val in zip(outs["short_conv_w"], new):
            rows[3, 0:3, :] = val
            for k in range(3):
                o_ref[k] = rows[3, k:k + 1, :]
        gates = dsum(gg_ref, 0, D_LRU)
        update("gate_a_w", gates[:, 0:HEAD])
        update("gate_x_w", gates[:, HEAD:2 * HEAD])

    out_shape = []
    for n in names:
        out_shape += [jax.ShapeDtypeStruct(params[n][0].shape, F32)] * 4
    out_shape += [jax.ShapeDtypeStruct((1, 1), F32), jax.ShapeDtypeStruct((8, 6 * D_MODEL), F32)]
    res = pl.pallas_call(
        body, name="small_update", out_shape=out_shape,
        in_specs=[SMEM] + [VMEM] * (4 + nin),
        out_specs=[VMEM] * len(out_shape),
        scratch_shapes=[pltpu.VMEM((8, D_MODEL), F32), pltpu.VMEM((8, D_MODEL), F32), pltpu.VMEM((16, D_LRU), F32),
                        pltpu.VMEM((8, 128), F32), pltpu.VMEM((4, 8, 128), F32)],
        compiler_params=pltpu.CompilerParams(vmem_limit_bytes=VMEM_LIMIT),
    )(mychip, gad, gam, gl, gg, *flat)
    per = {n: res[4 * k:4 * k + 4] for k, n in enumerate(names)}
    return per, res[-2], res[-1]


def _block_diag(w):
    eye = jnp.eye(8, dtype=w.dtype)
    return (eye[:, None, :, None] * w[:, :, None, :]).reshape(8 * HEAD, 8 * HEAD)


def _diag_blocks(g):
    return jnp.concatenate([g[h * HEAD:(h + 1) * HEAD, h * HEAD:(h + 1) * HEAD] for h in range(8)], axis=0)


def kernel(x, c, ada_w, ada_b, norm1_g, w_in, lru_conv_w, lru_conv_b, gate_a_w, gate_a_b, gate_x_w, gate_x_b, a_param, short_conv_w, lru_out_g, conv_out_g, w_out, norm2_g, w_mlp1, w_mlp2, final_g, loss_target, m_ada_w, m_ada_b, m_norm1_g, m_w_in, m_lru_conv_w, m_lru_conv_b, m_gate_a_w, m_gate_a_b, m_gate_x_w, m_gate_x_b, m_a_param, m_short_conv_w, m_lru_out_g, m_conv_out_g, m_w_out, m_norm2_g, m_w_mlp1, m_w_mlp2, m_final_g, v_ada_w, v_ada_b, v_norm1_g, v_w_in, v_lru_conv_w, v_lru_conv_b, v_gate_a_w, v_gate_a_b, v_gate_x_w, v_gate_x_b, v_a_param, v_short_conv_w, v_lru_out_g, v_conv_out_g, v_w_out, v_norm2_g, v_w_mlp1, v_w_mlp2, v_final_g):
    xi, yi, ci = lax.axis_index("x"), lax.axis_index("y"), lax.axis_index("c")
    mychip = 2 * xi + yi
    me = 4 * xi + 2 * yi + ci

    own_in, own_out = w_in[0].astype(BF16), w_out[0].astype(BF16)
    win_all, wout_all = _allgather_weights("allgather_mixer_weights", 1, [own_in, own_out])
    own_w1, own_w2 = w_mlp1[0].astype(BF16), w_mlp2[0].astype(BF16)
    w1_all, w2_all = _allgather_weights("allgather_mlp_weights", 2, [own_w1, own_w2])

    c_blk = jnp.zeros((8, D_MODEL), F32).at[0:1].set(c)
    cw_blk = jnp.zeros((8, 128), F32).at[0:4].set(lru_conv_w[0]).at[4:7].set(short_conv_w[0])
    c_g, cw_g = _allgather8("allgather_cond", [c_blk, cw_blk])
    c_all = c_g.reshape(8, 8, D_MODEL)[:, 0]
    cw_g = cw_g.reshape(4, 2, 8, 128)[:, 0]
    lcw = cw_g[:, 0:4].transpose(1, 0, 2).reshape(4, D_LRU)
    scw = cw_g[:, 4:7].transpose(1, 0, 2).reshape(3, D_LRU)

    mod_loc = _mod_matmul(c_all, ada_w[0])
    (mod_g,) = _allgather8("allgather_mod", [mod_loc])
    mod_all = mod_g.reshape(4, 2, 8, 6 * D_MODEL // 4)[:, 0].transpose(1, 0, 2).reshape(8, 6 * D_MODEL) + ada_b
    mod_pad = jnp.pad(mod_all.reshape(8, 6, D_MODEL), ((0, 0), (0, 2), (0, 0)))
    mod = lax.dynamic_slice_in_dim(mod_pad, me, 1, axis=0).reshape(8, D_MODEL)

    win, wout = (win_all, own_in), (wout_all, own_out)
    chip = mychip.reshape(1).astype(jnp.int32)
    core = ci.reshape(1).astype(jnp.int32)

    vecd = jnp.concatenate([norm1_g, norm2_g, final_g[None, :], jnp.concatenate([gate_a_b, gate_x_b], axis=1),
                            jnp.zeros((4, D_MODEL), F32)], axis=0)
    vecl = jnp.concatenate([lcw, lru_conv_b, scw, a_param, lru_out_g, conv_out_g, jnp.zeros((5, D_LRU), F32)], axis=0)
    gab = jnp.concatenate([_block_diag(gate_a_w[0]), _block_diag(gate_x_w[0])], axis=1).astype(BF16)
    a64 = _block_diag(jnp.full((8, HEAD, HEAD), 1.0 / HEAD, F32)).astype(BF16)

    hb, proj, hl, ycat, mixed, x1 = _mix_fwd(chip, x[0], mod, vecd, vecl, win, wout, gab, a64)
    dx1, act, dz, dmo, h2b, accm = _mlp_fwd_bwd(
        chip, x1, loss_target[0], mod, vecd, (w1_all, own_w1), (w2_all, own_w2))

    parts_mlp = list(_wgrad_mlp(6, h2b, dz, act, dmo))
    q_w1, q_w2 = _exchange_chips("rs_exchange_mlp", 0, parts_mlp)
    grad_x, accd, accl, g_win, g_wout, g_gate = _mix_bwd(
        chip, dx1, x[0], mixed, proj, hl, hb, ycat, mod, vecd, vecl, win, wout, gab, a64)

    g_mix = [g_win, g_wout.reshape(N_CHIP, WOUT_BLK, D_MODEL)]
    recv_mix = _sibling_swap("rs_swap_halves_mix", g_mix, True, collective_id=4)
    own_mlp = [_add_chips("rs_add_chips_mlp%d" % k, chip, p, q) for k, (p, q) in enumerate(zip(parts_mlp, (q_w1, q_w2)))]
    gg_blk = jnp.concatenate([_diag_blocks(g_gate[:, 0:D_LRU]), _diag_blocks(g_gate[:, D_LRU:2 * D_LRU])], axis=1)
    sib_mlp, (gad, gam, gl, gg) = _swap_and_gather_seq(
        "rs_swap_reduced_mlp_allgather_small_grads", 5, own_mlp, [accd, accm, accl, gg_blk.astype(BF16)])

    parts_mix = [_add_sibling("rs_add_sibling_mix%d" % k, g, r, core)
                 for k, (g, r) in enumerate(zip(g_mix, recv_mix))]
    landed_mix = _exchange_chips("rs_exchange_mix", 3, parts_mix)
    res_w1, res_w2 = _adam("adam_mlp", core, [(w_mlp1[0], own_mlp[0], sib_mlp[0], m_w_mlp1[0], v_w_mlp1[0]),
                                              (w_mlp2[0], own_mlp[1], sib_mlp[1], m_w_mlp2[0], v_w_mlp2[0])])

    params = {
        "ada_b": (ada_b, m_ada_b, v_ada_b), "norm1_g": (norm1_g, m_norm1_g, v_norm1_g),
        "lru_conv_b": (lru_conv_b, m_lru_conv_b, v_lru_conv_b),
        "gate_a_w": tuple(t.reshape(D_LRU, HEAD) for t in (gate_a_w, m_gate_a_w, v_gate_a_w)),
        "gate_a_b": (gate_a_b, m_gate_a_b, v_gate_a_b),
        "gate_x_w": tuple(t.reshape(D_LRU, HEAD) for t in (gate_x_w, m_gate_x_w, v_gate_x_w)),
        "gate_x_b": (gate_x_b, m_gate_x_b, v_gate_x_b), "a_param": (a_param, m_a_param, v_a_param),
        "lru_conv_w": (lru_conv_w, m_lru_conv_w, v_lru_conv_w),
        "short_conv_w": tuple(t.reshape(3, 1, D_LRU // N_CHIP) for t in (short_conv_w, m_short_conv_w, v_short_conv_w)),
        "lru_out_g": (lru_out_g, m_lru_out_g, v_lru_out_g), "conv_out_g": (conv_out_g, m_conv_out_g, v_conv_out_g),
        "norm2_g": (norm2_g, m_norm2_g, v_norm2_g),
        "final_g": tuple(t[None, :] for t in (final_g, m_final_g, v_final_g)),
    }
    small, loss_blk, dmod_cols = _small_update(gad, gam, gl, gg, chip, params)
    loss = loss_blk.reshape(())

    sct = (c_all * jax.nn.sigmoid(c_all)).T
    ada = _ada_grad_adam(chip, sct, dmod_cols, ada_w[0], m_ada_w[0], v_ada_w[0])

    own_mix = [_add_chips("rs_add_chips_mix%d" % k, chip, p, q, after=[res_w1[1], ada[1]])
               for k, (p, q) in enumerate(zip(parts_mix, landed_mix))]
    sib_mix = _sibling_swap("rs_swap_reduced_mix", own_mix, False)
    (res_win,) = _adam("adam_w_in", core, [(w_in[0], own_mix[0], sib_mix[0], m_w_in[0], v_w_in[0])])
    (res_wout,) = _adam("adam_w_out", core, [(w_out[0], own_mix[1], sib_mix[1], m_w_out[0], v_w_out[0])])

    res = {"ada_w": ada, "w_in": res_win, "w_out": res_wout, "w_mlp1": res_w1, "w_mlp2": res_w2}
    res = {n: tuple(t[None] for t in r) for n, r in res.items()}
    shapes = {"gate_a_w": gate_a_w.shape, "gate_x_w": gate_x_w.shape, "lru_conv_w": lru_conv_w.shape,
              "short_conv_w": short_conv_w.shape, "final_g": final_g.shape}
    for n, t in small.items():
        res[n] = tuple(u.reshape(shapes[n]) if n in shapes else u for u in t)

    order = ["ada_w", "ada_b", "norm1_g", "w_in", "lru_conv_w", "lru_conv_b", "gate_a_w", "gate_a_b", "gate_x_w",
             "gate_x_b", "a_param", "short_conv_w", "lru_out_g", "conv_out_g", "w_out", "norm2_g", "w_mlp1",
             "w_mlp2", "final_g"]
    return (loss, grad_x[None], *[res[n][0] for n in order], *[res[n][1] for n in order],
            *[res[n][2] for n in order], *[res[n][3] for n in order])
```

```python
import jax
import jax.numpy as jnp
from jax import lax
from jax.experimental import pallas as pl
from jax.experimental.pallas import tpu as pltpu
from jax.experimental.pallas import tpu_sc as plsc

F32 = jnp.float32
BF16 = jnp.bfloat16

D_MODEL = 1024
D_LRU = 512
D_IN = 2560
D_FF = 4096
N_CHIP = 4
WIN_BLK = D_IN // N_CHIP
WOUT_BLK = D_MODEL // N_CHIP
FF_BLK = D_FF // N_CHIP
HEAD = 64
EPS = 1e-6
C_GATE = 8.0
TOKEN_TILE = 256
MIX_FWD_TILE = 512
HALO = 8
VMEM_LIMIT = 60 * 1024 * 1024

ADAM_LR = 0.001
ADAM_B1 = 0.9
ADAM_B2 = 0.999
ADAM_EPS = 1e-08
ADAM_WD = 0.01
ADAM_STEP = 10

MESH = pl.DeviceIdType.MESH
ANY = pl.BlockSpec(memory_space=pl.ANY)
VMEM = pl.BlockSpec(memory_space=pltpu.VMEM)
SMEM = pl.BlockSpec(memory_space=pltpu.SMEM)


def _full(shape, single=False):
    nd = len(shape)
    if single:
        return pl.BlockSpec(shape, lambda *_: (0,) * nd, pipeline_mode=pl.Buffered(1))
    return pl.BlockSpec(shape, lambda *_: (0,) * nd)


def _dot(a, b):
    return jnp.dot(a, b, preferred_element_type=F32)


def _dot_nt(a, b):
    return lax.dot_general(a, b, (((1,), (1,)), ((), ())), preferred_element_type=F32)


def _dot_tn(a, b):
    return lax.dot_general(a, b, (((0,), (0,)), ((), ())), preferred_element_type=F32)


def _gmean(v, a64):
    hi = v.astype(BF16)
    lo = (v - hi.astype(F32)).astype(BF16)
    return _dot(hi, a64) + _dot(lo, a64)


def _gelu(x):
    u = 0.7978845608028654 * (x + 0.044715 * x * x * x)
    t = jnp.tanh(u)
    return 0.5 * x * (1.0 + t), t


def _gelu_grad(x, t):
    du = 0.7978845608028654 * (1.0 + 3.0 * 0.044715 * x * x)
    return 0.5 * (1.0 + t) + 0.5 * x * (1.0 - t * t) * du


def _log1p_pos(y):
    return jnp.where(y < 1e-2, y * (1.0 - y * (0.5 - y * (1.0 / 3.0 - y * 0.25))), jnp.log(1.0 + y))


def _softplus(a):
    return jnp.maximum(a, 0.0) + _log1p_pos(jnp.exp(-jnp.abs(a)))


def _neg_expm1(z):
    series = -z * (1.0 + z * (0.5 + z * (1.0 / 6.0 + z * (1.0 / 24.0))))
    return jnp.where(z > -0.02, series, 1.0 - jnp.exp(z))


def _scan_fwd(a, b, row):
    n = a.shape[0]
    d = 1
    while d < n:
        m = row >= d
        b = jnp.where(m, a * pltpu.roll(b, d, 0) + b, b)
        a = jnp.where(m, a * pltpu.roll(a, d, 0), a)
        d *= 2
    return a, b


def _scan_rev(a, b, row):
    n = a.shape[0]
    d = 1
    while d < n:
        m = row < n - d
        b = jnp.where(m, b + a * pltpu.roll(b, n - d, 0), b)
        a = jnp.where(m, a * pltpu.roll(a, n - d, 0), a)
        d *= 2
    return a, b


def _colsum(v):
    return jnp.sum(v, axis=0, keepdims=True)


def _load_gathered(chip, gathered, own, slot, sems):
    copies = []
    for j in range(N_CHIP):
        @pl.when(chip == j)
        def _(j=j):
            pltpu.make_async_copy(own, slot(j), sems.at[j]).start()

        @pl.when(chip != j)
        def _(j=j):
            pltpu.make_async_copy(gathered.at[j], slot(j), sems.at[j]).start()

        copies.append(pltpu.make_async_copy(own, slot(j), sems.at[j]))
    return copies


def _lru_gates(xlb, gab, gbias, sp, first_row):
    g = _dot(xlb, gab) + gbias
    r = jax.nn.sigmoid(g[:, :D_LRU])
    ig = jax.nn.sigmoid(g[:, D_LRU:])
    la = (-C_GATE) * r * sp
    a = jnp.exp(la)
    msq = jnp.sqrt(_neg_expm1(2.0 * la))
    mult = jnp.where(first_row, 1.0, msq)
    return r, ig, a, msq, mult


def _mix_fwd(chip, x, mod, vecd, vecl, win, wout, gab, a64):
    s = x.shape[0]
    ts = MIX_FWD_TILE
    nt = s // ts

    def body(chip_ref, x_ref, mod_ref, vd_ref, vl_ref, win_hbm, win_own, wout_hbm, wout_own, gab_ref, a64_ref,
             hb_ref, proj_ref, hl_ref, ycat_ref, mixed_ref, x1_ref,
             win_ref, wout_ref, ext_lx, ext_cv, hcar, sems):
        i = pl.program_id(0)

        @pl.when(i == 0)
        def _():
            cps = _load_gathered(chip_ref[0], win_hbm, win_own, lambda j: win_ref.at[j], sems.at[pl.ds(0, N_CHIP)])
            cps += _load_gathered(chip_ref[0], wout_hbm, wout_own,
                                  lambda j: wout_ref.at[pl.ds(j * WOUT_BLK, WOUT_BLK), :],
                                  sems.at[pl.ds(N_CHIP, N_CHIP)])
            ext_lx[0:HALO, :] = jnp.zeros((HALO, D_LRU), F32)
            ext_cv[0:HALO, :] = jnp.zeros((HALO, D_LRU), F32)
            hcar[...] = jnp.zeros_like(hcar)
            for cp in cps:
                cp.wait()

        row = lax.broadcasted_iota(jnp.int32, (ts, D_LRU), 0)
        first_row = jnp.logical_and(row == 0, i == 0)
        xt = x_ref[...]
        shift1, scale1, gate1 = mod_ref[0:1, :], mod_ref[1:2, :], mod_ref[2:3, :]
        r1 = lax.rsqrt(jnp.mean(xt * xt, axis=-1, keepdims=True) + EPS)
        h = (xt * r1) * vd_ref[0:1, :] * (1.0 + scale1) + shift1
        hb = h.astype(BF16)
        hb_ref[...] = hb
        for j in range(N_CHIP):
            proj_ref[:, j * WIN_BLK:(j + 1) * WIN_BLK] = _dot(hb, win_ref[j])
        u_ly = proj_ref[:, 512:1024]
        u_b = proj_ref[:, 1024:1536]

        ext_lx[HALO:HALO + ts, :] = proj_ref[:, 0:512]
        xl = vl_ref[4:5, :] + vl_ref[0:1, :] * ext_lx[pl.ds(5, ts), :]
        for k in range(1, 4):
            xl = xl + vl_ref[k:k + 1, :] * ext_lx[pl.ds(5 + k, ts), :]
        ext_lx[0:HALO, :] = ext_lx[ts:ts + HALO, :]
        sp = _softplus(vl_ref[8:9, :])
        _, ig, a, _, mult = _lru_gates(xl.astype(BF16), gab_ref[...], vd_ref[3:4, :], sp, first_row)
        acum, hloc = _scan_fwd(a, mult * (ig * xl), row)
        hl = hloc + acum * hcar[0:1, :]
        hl_ref[...] = hl
        hcar[0:1, :] = hl_ref[ts - 1:ts, :]
        ge, _ = _gelu(u_ly)
        p = ge * hl
        y_lru = p * lax.rsqrt(_gmean(p * p, a64_ref[...]) + EPS) * vl_ref[9:10, :]
        ycat_ref[:, 0:512] = y_lru.astype(BF16)

        ext_cv[HALO:HALO + ts, :] = proj_ref[:, 1536:2048] * proj_ref[:, 2048:2560]
        q = vl_ref[5:6, :] * ext_cv[pl.ds(6, ts), :]
        for k in range(1, 3):
            q = q + vl_ref[5 + k:6 + k, :] * ext_cv[pl.ds(6 + k, ts), :]
        ext_cv[0:HALO, :] = ext_cv[ts:ts + HALO, :]
        yc = u_b * q
        y_conv = yc * lax.rsqrt(_gmean(yc * yc, a64_ref[...]) + EPS) * vl_ref[10:11, :]
        ycat_ref[:, 512:1024] = y_conv.astype(BF16)

        mixed = _dot(ycat_ref[...], wout_ref[...])
        mixed_ref[...] = mixed
        x1_ref[...] = xt + gate1 * mixed

    tile = lambda w: pl.BlockSpec((ts, w), lambda i: (i, 0))
    return pl.pallas_call(
        body, name="mix_fwd", grid=(nt,),
        in_specs=[SMEM, tile(D_MODEL), _full((8, D_MODEL)), _full((8, D_MODEL)), _full((16, D_LRU)),
                  ANY, ANY, ANY, ANY, _full((D_LRU, 2 * D_LRU), True), _full((D_LRU, D_LRU), True)],
        out_specs=[tile(D_MODEL), tile(D_IN), tile(D_LRU), tile(D_MODEL), tile(D_MODEL), tile(D_MODEL)],
        out_shape=[jax.ShapeDtypeStruct((s, D_MODEL), BF16), jax.ShapeDtypeStruct((s, D_IN), F32),
                   jax.ShapeDtypeStruct((s, D_LRU), F32), jax.ShapeDtypeStruct((s, D_MODEL), BF16),
                   jax.ShapeDtypeStruct((s, D_MODEL), F32), jax.ShapeDtypeStruct((s, D_MODEL), F32)],
        scratch_shapes=[pltpu.VMEM((N_CHIP, D_MODEL, WIN_BLK), BF16), pltpu.VMEM((D_MODEL, D_MODEL), BF16),
                        pltpu.VMEM((ts + HALO, D_LRU), F32), pltpu.VMEM((ts + HALO, D_LRU), F32),
                        pltpu.VMEM((HALO, D_LRU), F32), pltpu.SemaphoreType.DMA((2 * N_CHIP,))],
        compiler_params=pltpu.CompilerParams(dimension_semantics=("arbitrary",), vmem_limit_bytes=VMEM_LIMIT),
    )(chip, x, mod, vecd, vecl, *win, *wout, gab, a64)


def _mlp_fwd_bwd(chip, x1, target, mod, vecd, w1, w2):
    s = x1.shape[0]
    ts = TOKEN_TILE
    nt = s // ts

    def body(chip_ref, x1_ref, tg_ref, mod_ref, vd_ref, w1_hbm, w1_own, w2_hbm, w2_own,
             dx1_ref, act_ref, dz_ref, dmo_ref, h2_ref, acc_ref, w1_v, w2_v, rz_v, sems):
        i = pl.program_id(0)

        @pl.when(i == 0)
        def _():
            cps = _load_gathered(chip_ref[0], w1_hbm, w1_own, lambda j: w1_v.at[j], sems.at[pl.ds(0, N_CHIP)])
            cps += _load_gathered(chip_ref[0], w2_hbm, w2_own, lambda j: w2_v.at[j], sems.at[pl.ds(N_CHIP, N_CHIP)])
            acc_ref[...] = jnp.zeros_like(acc_ref)
            for cp in cps:
                cp.wait()

        xt = x1_ref[...]
        shift2, scale2, gate2 = mod_ref[3:4, :], mod_ref[4:5, :], mod_ref[5:6, :]
        g2, gf = vd_ref[1:2, :], vd_ref[2:3, :]
        r2 = lax.rsqrt(jnp.mean(xt * xt, axis=-1, keepdims=True) + EPS)
        n2 = xt * r2
        h2b = (n2 * g2 * (1.0 + scale2) + shift2).astype(BF16)
        h2_ref[...] = h2b
        for j in range(N_CHIP):
            rz_v[j] = jnp.maximum(_dot(h2b, w1_v[j]), 0.0)
        mo = jnp.zeros((ts, D_MODEL), F32)
        for j in range(N_CHIP):
            rz = rz_v[j]
            actb = (rz * rz).astype(BF16)
            act_ref[:, j * FF_BLK:(j + 1) * FF_BLK] = actb
            mo = mo + _dot(actb, w2_v[j])
        x2 = xt + gate2 * mo
        r3 = lax.rsqrt(jnp.mean(x2 * x2, axis=-1, keepdims=True) + EPS)
        n3 = x2 * r3
        e = n3 * gf - tg_ref[...]
        loss = (0.5 / D_MODEL) * jnp.sum(_colsum(e * e), axis=1, keepdims=True)
        dy = e * (1.0 / D_MODEL)
        acc_ref[4:5, :] += _colsum(dy * n3)
        acc_ref[5:6, :] += jnp.broadcast_to(loss, (1, D_MODEL))
        dn3 = dy * gf
        dx2 = r3 * (dn3 - n3 * jnp.mean(dn3 * n3, axis=-1, keepdims=True))
        acc_ref[2:3, :] += _colsum(dx2 * mo)
        dmob = (dx2 * gate2).astype(BF16)
        dmo_ref[...] = dmob
        for j in range(N_CHIP):
            dz_ref[:, j * FF_BLK:(j + 1) * FF_BLK] = (_dot_nt(dmob, w2_v[j]) * (2.0 * rz_v[j])).astype(BF16)
        dh2 = jnp.zeros((ts, D_MODEL), F32)
        for j in range(N_CHIP):
            dh2 = dh2 + _dot_nt(dz_ref[:, j * FF_BLK:(j + 1) * FF_BLK], w1_v[j])
        acc_ref[1:2, :] += _colsum(dh2 * (n2 * g2))
        acc_ref[0:1, :] += _colsum(dh2)
        dhn2 = dh2 * (1.0 + scale2)
        acc_ref[3:4, :] += _colsum(dhn2 * n2)
        dn2 = dhn2 * g2
        dx1_ref[...] = dx2 + r2 * (dn2 - n2 * jnp.mean(dn2 * n2, axis=-1, keepdims=True))

    tile = lambda w: pl.BlockSpec((ts, w), lambda i: (i, 0))
    return pl.pallas_call(
        body, name="mlp_fwd_bwd", grid=(nt,),
        in_specs=[SMEM, tile(D_MODEL), tile(D_MODEL), _full((8, D_MODEL)), _full((8, D_MODEL)), ANY, ANY, ANY, ANY],
        out_specs=[tile(D_MODEL), tile(D_FF), tile(D_FF), tile(D_MODEL), tile(D_MODEL), _full((8, D_MODEL))],
        out_shape=[jax.ShapeDtypeStruct((s, D_MODEL), F32), jax.ShapeDtypeStruct((s, D_FF), BF16),
                   jax.ShapeDtypeStruct((s, D_FF), BF16), jax.ShapeDtypeStruct((s, D_MODEL), BF16),
                   jax.ShapeDtypeStruct((s, D_MODEL), BF16), jax.ShapeDtypeStruct((8, D_MODEL), F32)],
        scratch_shapes=[pltpu.VMEM((N_CHIP, D_MODEL, FF_BLK), BF16), pltpu.VMEM((N_CHIP, FF_BLK, D_MODEL), BF16),
                        pltpu.VMEM((N_CHIP, ts, FF_BLK), F32), pltpu.SemaphoreType.DMA((2 * N_CHIP,))],
        compiler_params=pltpu.CompilerParams(dimension_semantics=("arbitrary",), vmem_limit_bytes=VMEM_LIMIT),
    )(chip, x1, target, mod, vecd, *w1, *w2)


def _mix_bwd(chip, dx1, x, mixed, proj, hl, hb, ycat, mod, vecd, vecl, win, wout, gab, a64):
    s = x.shape[0]
    ts = TOKEN_TILE
    nt = s // ts
    hpt = ts // HALO

    def body(chip_ref, dx1_ref, x_ref, mixed_ref, proj_ref, projh_ref, hl_ref, hlh_ref, hb_ref, ycat_ref,
             mod_ref, vd_ref, vl_ref, win_hbm, win_own, wout_hbm, wout_own, gab_ref, a64_ref,
             gx_ref, accd_ref, accl_ref, gwin_hbm, gwout_hbm, ggate_hbm,
             win_ref, wout_ref, dproj_ref, dgb_ref, gwin_acc, gwout_acc, ggate_acc,
             ext_lx, ext_cv, ext_hl, ext_dxl, ext_dq, gbuf, gcar, acar, sems):
        i = pl.program_id(0)
        ri = nt - 1 - i

        @pl.when(i == 0)
        def _():
            gwin_acc[...] = jnp.zeros_like(gwin_acc)
            gwout_acc[...] = jnp.zeros_like(gwout_acc)
            ggate_acc[...] = jnp.zeros_like(ggate_acc)
            cps = _load_gathered(chip_ref[0], win_hbm, win_own, lambda j: win_ref.at[j], sems.at[pl.ds(0, N_CHIP)])
            cps += _load_gathered(chip_ref[0], wout_hbm, wout_own,
                                  lambda j: wout_ref.at[pl.ds(j * WOUT_BLK, WOUT_BLK), :],
                                  sems.at[pl.ds(N_CHIP, N_CHIP)])
            for cp in cps:
                cp.wait()
            accd_ref[...] = jnp.zeros_like(accd_ref)
            accl_ref[...] = jnp.zeros_like(accl_ref)
            ext_dxl[ts:ts + HALO, :] = jnp.zeros((HALO, D_LRU), F32)
            ext_dq[ts:ts + HALO, :] = jnp.zeros((HALO, D_LRU), F32)
            gcar[...] = jnp.zeros_like(gcar)
            acar[...] = jnp.zeros_like(acar)

        row = lax.broadcasted_iota(jnp.int32, (ts, D_LRU), 0)
        first_row = jnp.logical_and(row == 0, ri == 0)
        halo_on = jnp.where(ri == 0, 0.0, 1.0)
        shift1, scale1, gate1 = mod_ref[0:1, :], mod_ref[1:2, :], mod_ref[2:3, :]
        g1 = vd_ref[0:1, :]
        a64m = a64_ref[...]
        lg, cg = vl_ref[9:10, :], vl_ref[10:11, :]

        dx1 = dx1_ref[...]
        accd_ref[2:3, :] += _colsum(dx1 * mixed_ref[...])
        dmb = (dx1 * gate1).astype(BF16)
        gwout_acc[...] += _dot_tn(ycat_ref[...], dmb)
        dycat = _dot_nt(dmb, wout_ref[...])
        dyl = dycat[:, 0:512]
        dyv = dycat[:, 512:1024]

        u_ly = proj_ref[:, 512:1024]
        u_b = proj_ref[:, 1024:1536]
        u_c = proj_ref[:, 1536:2048]
        u_v = proj_ref[:, 2048:2560]
        ext_lx[0:HALO, :] = projh_ref[:, 0:512] * halo_on
        ext_lx[HALO:HALO + ts, :] = proj_ref[:, 0:512]
        xl = vl_ref[4:5, :] + vl_ref[0:1, :] * ext_lx[pl.ds(5, ts), :]
        for k in range(1, 4):
            xl = xl + vl_ref[k:k + 1, :] * ext_lx[pl.ds(5 + k, ts), :]
        xlb = xl.astype(BF16)
        sp = _softplus(vl_ref[8:9, :])
        r, ig, a, msq, mult = _lru_gates(xlb, gab_ref[...], vd_ref[3:4, :], sp, first_row)
        hl = hl_ref[...]
        ge, th = _gelu(u_ly)
        p = ge * hl
        rl = lax.rsqrt(_gmean(p * p, a64m) + EPS)
        nl = p * rl
        ext_cv[0:HALO, :] = projh_ref[:, 1536:2048] * projh_ref[:, 2048:2560] * halo_on
        ext_cv[HALO:HALO + ts, :] = u_c * u_v
        q = vl_ref[5:6, :] * ext_cv[pl.ds(6, ts), :]
        for k in range(1, 3):
            q = q + vl_ref[5 + k:6 + k, :] * ext_cv[pl.ds(6 + k, ts), :]
        yc = u_b * q
        rc = lax.rsqrt(_gmean(yc * yc, a64m) + EPS)
        nc = yc * rc

        accl_ref[9:10, :] += _colsum(dyl * nl)
        dnl = dyl * lg
        dp = rl * (dnl - nl * _gmean(dnl * nl, a64m))
        dproj_ref[:, 512:1024] = ((dp * hl) * _gelu_grad(u_ly, th)).astype(BF16)
        a_next = jnp.where(row == ts - 1, acar[0:1, :], pltpu.roll(a, ts - 1, 0))
        acum, gloc = _scan_rev(a_next, dp * ge, row)
        gbuf[...] = gloc + acum * gcar[0:1, :]
        gcar[0:1, :] = gbuf[0:1, :]
        ext_hl[0:HALO, :] = hlh_ref[...] * halo_on
        ext_hl[HALO:HALO + ts, :] = hl
        acar[...] = a[0:HALO, :]
        gt = gbuf[...]
        da = gt * ext_hl[pl.ds(HALO - 1, ts), :]
        dmult = gt * ig * xl
        di = gt * mult * xl
        dxl = gt * mult * ig
        dla = da * a - jnp.where(first_row, 0.0, dmult * a * a / msq)
        accl_ref[8:9, :] += _colsum(dla * ((-C_GATE) * r))
        dra = dla * ((-C_GATE) * sp) * r * (1.0 - r)
        dia = di * ig * (1.0 - ig)
        accd_ref[4:5, 0:D_LRU] += _colsum(dra)
        accd_ref[4:5, D_LRU:2 * D_LRU] += _colsum(dia)
        dgb_ref[:, 0:D_LRU] = dra.astype(BF16)
        dgb_ref[:, D_LRU:2 * D_LRU] = dia.astype(BF16)
        dxl = dxl + _dot_nt(dgb_ref[...], gab_ref[...])
        ggate_acc[...] += _dot_tn(xlb, dgb_ref[...])
        accl_ref[4:5, :] += _colsum(dxl)
        for k in range(4):
            accl_ref[k:k + 1, :] += _colsum(dxl * ext_lx[pl.ds(5 + k, ts), :])
        ext_dxl[0:ts, :] = dxl
        du_lx = vl_ref[0:1, :] * ext_dxl[pl.ds(3, ts), :]
        for k in range(1, 4):
            du_lx = du_lx + vl_ref[k:k + 1, :] * ext_dxl[pl.ds(3 - k, ts), :]
        ext_dxl[ts:ts + HALO, :] = ext_dxl[0:HALO, :]
        dproj_ref[:, 0:512] = du_lx.astype(BF16)

        accl_ref[10:11, :] += _colsum(dyv * nc)
        dnc = dyv * cg
        dyc = rc * (dnc - nc * _gmean(dnc * nc, a64m))
        dproj_ref[:, 1024:1536] = (dyc * q).astype(BF16)
        dq = dyc * u_b
        for k in range(3):
            accl_ref[5 + k:6 + k, :] += _colsum(dq * ext_cv[pl.ds(6 + k, ts), :])
        ext_dq[0:ts, :] = dq
        dcv = vl_ref[5:6, :] * ext_dq[pl.ds(2, ts), :]
        for k in range(1, 3):
            dcv = dcv + vl_ref[5 + k:6 + k, :] * ext_dq[pl.ds(2 - k, ts), :]
        ext_dq[ts:ts + HALO, :] = ext_dq[0:HALO, :]
        dproj_ref[:, 1536:2048] = (dcv * u_v).astype(BF16)
        dproj_ref[:, 2048:2560] = (dcv * u_c).astype(BF16)

        dh = _dot_nt(dproj_ref[:, 0:WIN_BLK], win_ref[0])
        for j in range(1, N_CHIP):
            dh = dh + _dot_nt(dproj_ref[:, j * WIN_BLK:(j + 1) * WIN_BLK], win_ref[j])
        for j in range(N_CHIP):
            gwin_acc[j] += _dot_tn(hb_ref[...], dproj_ref[:, j * WIN_BLK:(j + 1) * WIN_BLK])
        xt = x_ref[...]
        r1 = lax.rsqrt(jnp.mean(xt * xt, axis=-1, keepdims=True) + EPS)
        n1 = xt * r1
        accd_ref[1:2, :] += _colsum(dh * (n1 * g1))
        accd_ref[0:1, :] += _colsum(dh)
        dhn1 = dh * (1.0 + scale1)
        accd_ref[3:4, :] += _colsum(dhn1 * n1)
        dn1 = dhn1 * g1
        gx_ref[...] = dx1 + r1 * (dn1 - n1 * jnp.mean(dn1 * n1, axis=-1, keepdims=True))

        @pl.when(i == nt - 1)
        def _():
            outs = [pltpu.make_async_copy(acc, dst, sems.at[k]) for k, (acc, dst) in enumerate(
                ((gwin_acc, gwin_hbm), (gwout_acc, gwout_hbm), (ggate_acc, ggate_hbm)))]
            for cp in outs:
                cp.start()
            for cp in outs:
                cp.wait()

    tile = lambda w: pl.BlockSpec((ts, w), lambda i: (nt - 1 - i, 0))
    halo = lambda w: pl.BlockSpec((HALO, w), lambda i: (jnp.maximum((nt - 1 - i) * hpt - 1, 0), 0))
    ext = pltpu.VMEM((ts + HALO, D_LRU), F32)
    return pl.pallas_call(
        body, name="mix_bwd", grid=(nt,),
        in_specs=[SMEM, tile(D_MODEL), tile(D_MODEL), tile(D_MODEL), tile(D_IN), halo(D_IN), tile(D_LRU), halo(D_LRU),
                  tile(D_MODEL), tile(D_MODEL), _full((8, D_MODEL)), _full((8, D_MODEL)), _full((16, D_LRU)),
                  ANY, ANY, ANY, ANY, _full((D_LRU, 2 * D_LRU), True), _full((D_LRU, D_LRU), True)],
        out_specs=[tile(D_MODEL), _full((8, D_MODEL)), _full((16, D_LRU)), ANY, ANY, ANY],
        out_shape=[jax.ShapeDtypeStruct((s, D_MODEL), F32),
                   jax.ShapeDtypeStruct((8, D_MODEL), F32), jax.ShapeDtypeStruct((16, D_LRU), F32),
                   jax.ShapeDtypeStruct((N_CHIP, D_MODEL, WIN_BLK), F32), jax.ShapeDtypeStruct((D_MODEL, D_MODEL), F32),
                   jax.ShapeDtypeStruct((D_LRU, 2 * D_LRU), F32)],
        scratch_shapes=[pltpu.VMEM((N_CHIP, D_MODEL, WIN_BLK), BF16), pltpu.VMEM((D_MODEL, D_MODEL), BF16),
                        pltpu.VMEM((ts, D_IN), BF16), pltpu.VMEM((ts, 2 * D_LRU), BF16),
                        pltpu.VMEM((N_CHIP, D_MODEL, WIN_BLK), F32), pltpu.VMEM((D_MODEL, D_MODEL), F32),
                        pltpu.VMEM((D_LRU, 2 * D_LRU), F32),
                        ext, ext, ext, ext, ext, pltpu.VMEM((ts, D_LRU), F32),
                        pltpu.VMEM((HALO, D_LRU), F32), pltpu.VMEM((HALO, D_LRU), F32),
                        pltpu.SemaphoreType.DMA((2 * N_CHIP,))],
        compiler_params=pltpu.CompilerParams(dimension_semantics=("arbitrary",), vmem_limit_bytes=VMEM_LIMIT),
    )(chip, dx1, x, mixed, proj, proj, hl, hl, hb, ycat, mod, vecd, vecl, *win, *wout, gab, a64)


def _wgrad_mlp(collective_id, h2b, dz, act, dmo):
    s = h2b.shape[0]
    nstep = 2 * N_CHIP
    half = FF_BLK // 2

    def body(h2_ref, dz_ref, act_ref, dmo_ref, p1_hbm, p2_hbm, buf, landed, summed, send_sems, recv_sems, out_sems):
        j = pl.program_id(0)
        x, y, c, _ = _position()

        def give(jj):
            return pltpu.make_async_remote_copy(
                src_ref=buf.at[jj % 2, pl.ds((1 - c) * half, half), :], dst_ref=landed.at[jj],
                send_sem=send_sems.at[jj % 2], recv_sem=recv_sems.at[jj],
                device_id=(x, y, 1 - c), device_id_type=MESH)

        def write_out(jj, dst):
            return pltpu.make_async_copy(summed.at[jj % 2], dst, out_sems.at[jj % 2])

        def add_sibling(jj):
            give(jj).wait_recv()
            own = buf[jj % 2, pl.ds(pl.multiple_of(c * half, half), half), :]
            summed[jj % 2] = (own.astype(F32) + landed[jj].astype(F32)).astype(BF16)

        @pl.when(j == 0)
        def _():
            pl.semaphore_signal(pltpu.get_barrier_semaphore(), inc=1, device_id=(x, y, 1 - c), device_id_type=MESH)

        @pl.when(j >= 2)
        def _():
            give(j - 2).wait_send()

        @pl.when(j < N_CHIP)
        def _():
            buf[j % 2] = _dot_tn(act_ref[...], dmo_ref[...]).astype(BF16)

        @pl.when(j >= N_CHIP)
        def _():
            buf[j % 2] = _dot_tn(h2_ref[...], dz_ref[...]).astype(BF16)

        @pl.when(j == 0)
        def _():
            pl.semaphore_wait(pltpu.get_barrier_semaphore(), 1)

        give(j).start()

        @pl.when(j >= 1)
        def _():
            jm = j - 1

            @pl.when(jm >= 2)
            def _():
                write_out(jm - 2, p2_hbm.at[0]).wait()

            add_sibling(jm)

            @pl.when(jm < N_CHIP)
            def _():
                write_out(jm, p2_hbm.at[jm]).start()

            @pl.when(jm >= N_CHIP)
            def _():
                write_out(jm, p1_hbm.at[jm - N_CHIP]).start()

        @pl.when(j == nstep - 1)
        def _():
            last = nstep - 1
            write_out(last - 2, p1_hbm.at[0]).wait()
            add_sibling(last)
            write_out(last, p1_hbm.at[N_CHIP - 1]).start()
            for jj in (last - 1, last):
                give(jj).wait_send()
                write_out(jj, p1_hbm.at[0]).wait()

    sds = jax.ShapeDtypeStruct((N_CHIP, half, D_MODEL), BF16)
    whole = pl.BlockSpec((s, D_MODEL), lambda j: (0, 0))
    return pl.pallas_call(
        body, name="wgrad_mlp", grid=(nstep,),
        in_specs=[whole, pl.BlockSpec((s, FF_BLK), lambda j: (0, jnp.maximum(j - N_CHIP, 0))),
                  pl.BlockSpec((s, FF_BLK), lambda j: (0, jnp.minimum(j, N_CHIP - 1))), whole],
        out_specs=[ANY, ANY], out_shape=[sds, sds],
        scratch_shapes=[pltpu.VMEM((2, FF_BLK, D_MODEL), BF16), pltpu.VMEM((nstep, half, D_MODEL), BF16),
                        pltpu.VMEM((2, half, D_MODEL), BF16), pltpu.SemaphoreType.DMA((2,)),
                        pltpu.SemaphoreType.DMA((nstep,)), pltpu.SemaphoreType.DMA((2,))],
        compiler_params=pltpu.CompilerParams(dimension_semantics=("arbitrary",), vmem_limit_bytes=VMEM_LIMIT,
                                             collective_id=collective_id),
    )(h2b, dz, act, dmo)


def _mod_matmul(c_all, ada_w_loc):
    n = ada_w_loc.shape[1]
    cb = 512

    def body(c_ref, w_ref, o_ref):
        c = c_ref[...]
        sc = c * jax.nn.sigmoid(c)
        o_ref[...] = _dot(sc.astype(BF16), w_ref[...].astype(BF16))

    return pl.pallas_call(
        body, name="mod_matmul", grid=(n // cb,),
        in_specs=[_full((8, D_MODEL)), pl.BlockSpec((D_MODEL, cb), lambda j: (0, j))],
        out_specs=pl.BlockSpec((8, cb), lambda j: (0, j)),
        out_shape=jax.ShapeDtypeStruct((8, n), F32),
        compiler_params=pltpu.CompilerParams(dimension_semantics=("arbitrary",), vmem_limit_bytes=VMEM_LIMIT),
    )(c_all, ada_w_loc)


def _adam_math(w, g, m, v):
    m = ADAM_B1 * m + (1.0 - ADAM_B1) * g
    v = ADAM_B2 * v + (1.0 - ADAM_B2) * (g * g)
    m_hat = m / (1.0 - ADAM_B1 ** ADAM_STEP)
    v_hat = v / (1.0 - ADAM_B2 ** ADAM_STEP)
    delta = (-ADAM_LR) * (m_hat / (jnp.sqrt(v_hat) + ADAM_EPS) + ADAM_WD * w)
    return delta, m, v


def _adam(name, core, shards):
    n = len(shards)
    r, c = shards[0][0].shape
    half = r // 2
    rb = min(half, 128)
    nh = half // rb

    def body(core_ref, *refs):
        ins, outs = refs[:5 * n], refs[5 * n:]
        mine = (pl.program_id(0) // nh) == core_ref[0]
        for k in range(n):
            w_ref, go_ref, gs_ref, m_ref, v_ref = ins[5 * k:5 * k + 5]
            g_ref, d_ref, mo_ref, vo_ref = outs[4 * k:4 * k + 4]
            g = jnp.where(mine, go_ref[...], gs_ref[...])
            g_ref[...] = g
            d_ref[...], mo_ref[...], vo_ref[...] = _adam_math(w_ref[...], g, m_ref[...], v_ref[...])

    spec = pl.BlockSpec((rb, c), lambda i, core_ref: (i, 0))
    own = pl.BlockSpec((rb, c), lambda i, core_ref: (jnp.where(i // nh == core_ref[0], i % nh, 0), 0))
    sib = pl.BlockSpec((rb, c), lambda i, core_ref: (jnp.where(i // nh == core_ref[0], 0, i % nh), 0))
    sds = jax.ShapeDtypeStruct((r, c), F32)
    res = pl.pallas_call(
        body, name=name,
        grid_spec=pltpu.PrefetchScalarGridSpec(
            num_scalar_prefetch=1, grid=(r // rb,),
            in_specs=[spec, own, sib, spec, spec] * n, out_specs=[spec] * (4 * n)),
        out_shape=[sds] * (4 * n),
        compiler_params=pltpu.CompilerParams(dimension_semantics=("arbitrary",), vmem_limit_bytes=VMEM_LIMIT),
    )(core, *[t for s in shards for t in s])
    return [res[4 * k:4 * k + 4] for k in range(n)]


def _ada_grad_adam(chip, sct, dmod_cols, w, m, v):
    r, c = w.shape
    rb = 256

    def body(chip_ref, s_ref, dm_ref, w_ref, m_ref, v_ref, g_ref, d_ref, mo_ref, vo_ref):
        g = s_ref[:, 0:1] * dm_ref[0:1, :]
        for b in range(1, 8):
            g = g + s_ref[:, b:b + 1] * dm_ref[b:b + 1, :]
        g_ref[...] = g
        d_ref[...], mo_ref[...], vo_ref[...] = _adam_math(w_ref[...], g, m_ref[...], v_ref[...])

    spec = pl.BlockSpec((rb, c), lambda i, chip_ref: (i, 0))
    sds = jax.ShapeDtypeStruct((r, c), F32)
    return pl.pallas_call(
        body, name="ada_grad_adam",
        grid_spec=pltpu.PrefetchScalarGridSpec(
            num_scalar_prefetch=1, grid=(r // rb,),
            in_specs=[pl.BlockSpec((rb, 8), lambda i, chip_ref: (i, 0)),
                      pl.BlockSpec((8, c), lambda i, chip_ref: (0, chip_ref[0])), spec, spec, spec],
            out_specs=[spec] * 4),
        out_shape=[sds] * 4,
        compiler_params=pltpu.CompilerParams(dimension_semantics=("arbitrary",), vmem_limit_bytes=VMEM_LIMIT),
    )(chip, sct, dmod_cols, w, m, v)


def _position():
    x, y, c = lax.axis_index("x"), lax.axis_index("y"), lax.axis_index("c")
    chips = [(1 - x, y), (x, 1 - y), (1 - x, 1 - y)]
    return x, y, c, chips


def _ag8_run(ins, outs, send_sems, recv_sems, local_sems):
    na = len(ins)
    x, y, c, chips = _position()
    me, sibling = (x, y, c), (x, y, 1 - c)
    first, passed, local = [], [], []
    for a in range(na):
        m_per = ins[a].shape[0]

        def rows(px, py, pc, a=a, m_per=m_per):
            return outs[a].at[pl.ds((4 * px + 2 * py + pc) * m_per, m_per), :]

        def copy(k, block, to, src=None, a=a, rows=rows):
            return pltpu.make_async_remote_copy(
                src_ref=rows(*block) if src is None else src, dst_ref=rows(*block),
                send_sem=send_sems.at[7 * a + k], recv_sem=recv_sems.at[7 * a + k],
                device_id=to, device_id_type=MESH)

        mine = pltpu.make_async_copy(ins[a], rows(*me), local_sems.at[a])
        mine.start()
        local.append(mine)
        f = [copy(0, me, sibling, src=ins[a])]
        f += [copy(1 + j, me, (*chip, c), src=ins[a]) for j, chip in enumerate(chips)]
        for cp in f:
            cp.start()
        first.append((f, copy))
    for a in range(na):
        f, copy = first[a]
        p = [copy(4 + j, (*chip, c), sibling) for j, chip in enumerate(chips)]
        for j, chip in enumerate(chips):
            copy(1 + j, (*chip, c), me).wait_recv()
            p[j].start()
        passed.append(p)
    for a in range(na):
        f, copy = first[a]
        copy(0, sibling, me).wait_recv()
        for j, chip in enumerate(chips):
            copy(4 + j, (*chip, 1 - c), me).wait_recv()
        for cp in f + passed[a]:
            cp.wait_send()
        local[a].wait()


def _allgather8(name, arrs):
    na = len(arrs)

    def body(*refs):
        _ag8_run(refs[:na], refs[na:2 * na], *refs[2 * na:])

    return pl.pallas_call(
        body, name=name,
        out_shape=[jax.ShapeDtypeStruct((8 * a.shape[0], a.shape[1]), a.dtype) for a in arrs],
        in_specs=[VMEM] * na, out_specs=[VMEM] * na,
        scratch_shapes=[pltpu.SemaphoreType.DMA((7 * na,)), pltpu.SemaphoreType.DMA((7 * na,)),
                        pltpu.SemaphoreType.DMA((na,))],
        compiler_params=pltpu.CompilerParams(vmem_limit_bytes=VMEM_LIMIT),
    )(*arrs)


AG_SEMS = 7
AG_CHUNKS = 2


def _ag_copies(ins, outs, send_sems, recv_sems):
    x, y, c, chips = _position()
    sibling = (x, y, 1 - c)
    xn, yn, dg = [2 * chip[0] + chip[1] for chip in chips]
    to_x, to_y = (1 - x, y, c), (x, 1 - y, c)
    res = []
    for a in range(len(ins)):
        half = ins[a].shape[0] // 2
        piece = half // AG_CHUNKS
        for p in range(AG_CHUNKS):
            def copy(k, dst, to, src=None, base=AG_SEMS * (AG_CHUNKS * a + p)):
                return pltpu.make_async_remote_copy(
                    src_ref=dst if src is None else src, dst_ref=dst,
                    send_sem=send_sems.at[base + k], recv_sem=recv_sems.at[base + k],
                    device_id=to, device_id_type=MESH)

            def rows(chip, pc, q=None, a=a, start=p * piece, half=half, piece=piece):
                if q is None:
                    return outs[a].at[chip, pl.ds(pc * half + start, piece), :]
                return outs[a].at[chip, pl.ds(pc * half + start + q * (piece // 2), piece // 2), :]

            own = ins[a].at[pl.ds(c * half + p * piece, piece), :]
            mine = rows(2 * x + y, c)
            res.append(dict(
                sends=[copy(0, mine, to_x, src=own), copy(1, mine, to_y, src=own)],
                from_x=copy(0, rows(xn, c), to_x), from_y=copy(1, rows(yn, c), to_y),
                relay_y=copy(2, rows(xn, c, 0), to_y), relay_x=copy(3, rows(yn, c, 1), to_x),
                from_y_relay=copy(2, rows(dg, c, 0), to_y), from_x_relay=copy(3, rows(dg, c, 1), to_x),
                pass_on=[copy(4, rows(xn, c), sibling), copy(5, rows(yn, c), sibling), copy(6, rows(dg, c), sibling)],
                from_sibling=[copy(4, rows(xn, 1 - c), sibling), copy(5, rows(yn, 1 - c), sibling),
                              copy(6, rows(dg, 1 - c), sibling)]))
    return res


def _ag_start(ins, outs, send_sems, recv_sems):
    for cps in _ag_copies(ins, outs, send_sems, recv_sems):
        for cp in cps["sends"]:
            cp.start()


def _ag_relay(ins, outs, send_sems, recv_sems):
    for cps in _ag_copies(ins, outs, send_sems, recv_sems):
        cps["from_x"].wait_recv()
        cps["relay_y"].start()
        cps["pass_on"][0].start()
        cps["from_y"].wait_recv()
        cps["relay_x"].start()
        cps["pass_on"][1].start()


def _ag_complete(ins, outs, send_sems, recv_sems):
    copies = _ag_copies(ins, outs, send_sems, recv_sems)
    for cps in copies:
        cps["from_y_relay"].wait_recv()
        cps["from_x_relay"].wait_recv()
        cps["pass_on"][2].start()
    for cps in copies:
        for cp in cps["from_sibling"]:
            cp.wait_recv()
        for cp in cps["sends"] + [cps["relay_y"], cps["relay_x"]] + cps["pass_on"]:
            cp.wait_send()


def _ag_finish(ins, outs, send_sems, recv_sems):
    _ag_relay(ins, outs, send_sems, recv_sems)
    _ag_complete(ins, outs, send_sems, recv_sems)


def _allgather_weights(name, collective_id, shards):
    na = len(shards)
    hbm = pltpu.MemorySpace.HBM
    ins = [jax.new_ref(s, memory_space=hbm) for s in shards]
    outs = [jax.empty_ref(jax.ShapeDtypeStruct((N_CHIP,) + s.shape, s.dtype), memory_space=hbm) for s in shards]

    @pl.kernel(mesh=plsc.ScalarSubcoreMesh(axis_name="sequencer", num_cores=1), name=name,
               scratch_types=(pltpu.SemaphoreType.DMA((AG_SEMS * AG_CHUNKS * na,)),
                              pltpu.SemaphoreType.DMA((AG_SEMS * AG_CHUNKS * na,))),
               compiler_params=pltpu.CompilerParams(collective_id=collective_id))
    def launch(send_sems, recv_sems):
        x, y, c, _ = _position()
        peers = [(1 - x, y, c), (x, 1 - y, c), (x, y, 1 - c)]
        barrier = pltpu.get_barrier_semaphore()
        for peer in peers:
            pl.semaphore_signal(barrier, inc=1, device_id=peer, device_id_type=MESH)
        pl.semaphore_wait(barrier, len(peers))
        _ag_start(ins, outs, send_sems, recv_sems)
        _ag_finish(ins, outs, send_sems, recv_sems)

    launch()
    return [o[...] for o in outs]


def _swap_copies(ins, outs, send_sems, recv_sems, split_rows):
    x, y, c, _ = _position()
    cps = []
    for a in range(len(ins)):
        src = ins[a]
        if split_rows:
            half = src.shape[1] // 2
            src = src.at[:, pl.ds((1 - c) * half, half), :]
        cps.append(pltpu.make_async_remote_copy(
            src_ref=src, dst_ref=outs[a], send_sem=send_sems.at[a], recv_sem=recv_sems.at[a],
            device_id=(x, y, 1 - c), device_id_type=MESH))
    return cps


def _swap_and_gather_seq(name, collective_id, swap_arrs, gather_arrs):
    ns, ng = len(swap_arrs), len(gather_arrs)
    hbm = pltpu.MemorySpace.HBM
    s_ins = [jax.new_ref(a, memory_space=hbm) for a in swap_arrs]
    s_outs = [jax.empty_ref(jax.ShapeDtypeStruct(a.shape, a.dtype), memory_space=hbm) for a in swap_arrs]
    g_ins = [jax.new_ref(a, memory_space=hbm) for a in gather_arrs]
    g_outs = [jax.empty_ref(jax.ShapeDtypeStruct((8 * a.shape[0], a.shape[1]), a.dtype), memory_space=hbm)
              for a in gather_arrs]

    @pl.kernel(mesh=plsc.ScalarSubcoreMesh(axis_name="sequencer", num_cores=1), name=name,
               scratch_types=(pltpu.SemaphoreType.DMA((ns,)), pltpu.SemaphoreType.DMA((ns,)),
                              pltpu.SemaphoreType.DMA((7 * ng,)), pltpu.SemaphoreType.DMA((7 * ng,)),
                              pltpu.SemaphoreType.DMA((ng,))),
               compiler_params=pltpu.CompilerParams(collective_id=collective_id))
    def launch(swap_send, swap_recv, send_sems, recv_sems, local_sems):
        x, y, c, chips = _position()
        peers = [(x, y, 1 - c)] + [(*chip, c) for chip in chips]
        barrier = pltpu.get_barrier_semaphore()
        for peer in peers:
            pl.semaphore_signal(barrier, inc=1, device_id=peer, device_id_type=MESH)
        pl.semaphore_wait(barrier, len(peers))
        for cp in _swap_copies(s_ins, s_outs, swap_send, swap_recv, False):
            cp.start()
        _ag8_run(g_ins, g_outs, send_sems, recv_sems, local_sems)
        for cp in _swap_copies(s_ins, s_outs, swap_send, swap_recv, False):
            cp.wait()

    launch()
    return [o[...] for o in s_outs], [o[...] for o in g_outs]


def _sibling_swap(name, arrs, split_rows, collective_id=None):
    na = len(arrs)
    shapes = [jax.ShapeDtypeStruct((a.shape[0], a.shape[1] // 2, a.shape[2]) if split_rows else a.shape, a.dtype)
              for a in arrs]

    def run(ins, outs, send_sems, recv_sems):
        for cp in _swap_copies(ins, outs, send_sems, recv_sems, split_rows):
            cp.start()
        for cp in _swap_copies(ins, outs, send_sems, recv_sems, split_rows):
            cp.wait()

    sems = (pltpu.SemaphoreType.DMA((na,)), pltpu.SemaphoreType.DMA((na,)))
    if collective_id is None:
        return pl.pallas_call(
            lambda *refs: run(refs[:na], refs[na:2 * na], *refs[2 * na:]), name=name, out_shape=shapes,
            in_specs=[ANY] * na, out_specs=[ANY] * na, scratch_shapes=list(sems))(*arrs)

    hbm = pltpu.MemorySpace.HBM
    ins = [jax.new_ref(a, memory_space=hbm) for a in arrs]
    outs = [jax.empty_ref(s, memory_space=hbm) for s in shapes]

    @pl.kernel(mesh=plsc.ScalarSubcoreMesh(axis_name="sequencer", num_cores=1), name=name, scratch_types=sems,
               compiler_params=pltpu.CompilerParams(collective_id=collective_id))
    def launch(send_sems, recv_sems):
        x, y, c, _ = _position()
        barrier = pltpu.get_barrier_semaphore()
        pl.semaphore_signal(barrier, inc=1, device_id=(x, y, 1 - c), device_id_type=MESH)
        pl.semaphore_wait(barrier, 1)
        run(ins, outs, send_sems, recv_sems)

    launch()
    return [o[...] for o in outs]


def _xchg_copies(ins, outs, send_sems, recv_sems):
    x, y, c, chips = _position()
    return [pltpu.make_async_remote_copy(
        src_ref=ins[a].at[2 * chip[0] + chip[1]], dst_ref=outs[a].at[j],
        send_sem=send_sems.at[3 * a + j], recv_sem=recv_sems.at[3 * a + j],
        device_id=(*chip, c), device_id_type=MESH) for a in range(len(ins)) for j, chip in enumerate(chips)]


def _exchange_chips(name, collective_id, parts):
    na = len(parts)
    hbm = pltpu.MemorySpace.HBM
    ins = [jax.new_ref(p, memory_space=hbm) for p in parts]
    outs = [jax.empty_ref(jax.ShapeDtypeStruct((3,) + p.shape[1:], p.dtype), memory_space=hbm) for p in parts]

    @pl.kernel(mesh=plsc.ScalarSubcoreMesh(axis_name="sequencer", num_cores=1), name=name,
               scratch_types=(pltpu.SemaphoreType.DMA((3 * na,)), pltpu.SemaphoreType.DMA((3 * na,))),
               compiler_params=pltpu.CompilerParams(collective_id=collective_id))
    def launch(send_sems, recv_sems):
        x, y, c, chips = _position()
        barrier = pltpu.get_barrier_semaphore()
        for chip in chips:
            pl.semaphore_signal(barrier, inc=1, device_id=(*chip, c), device_id_type=MESH)
        pl.semaphore_wait(barrier, len(chips))
        for cp in _xchg_copies(ins, outs, send_sems, recv_sems):
            cp.start()
        for cp in _xchg_copies(ins, outs, send_sems, recv_sems):
            cp.wait()

    launch()
    return [q[...] for q in outs]


def _add_sibling(name, grads, recvs, core):
    n = len(grads)

    def body(core_ref, *refs):
        for k in range(n):
            g_ref, r_ref, o_ref = refs[2 * k], refs[2 * k + 1], refs[2 * n + k]
            o_ref[...] = (g_ref[...].astype(F32) + r_ref[...].astype(F32)).astype(BF16)

    in_specs, out_specs, out_shape = [], [], []
    for g in grads:
        _, r, c = g.shape
        half = r // 2
        in_specs += [pl.BlockSpec((1, half, c), lambda j, core_ref: (j, core_ref[0], 0)),
                     pl.BlockSpec((1, half, c), lambda j, core_ref: (j, 0, 0))]
        out_specs.append(pl.BlockSpec((1, half, c), lambda j, core_ref: (j, 0, 0)))
        out_shape.append(jax.ShapeDtypeStruct((N_CHIP, half, c), BF16))
    return pl.pallas_call(
        body, name=name,
        grid_spec=pltpu.PrefetchScalarGridSpec(
            num_scalar_prefetch=1, grid=(N_CHIP,), in_specs=in_specs, out_specs=out_specs),
        out_shape=out_shape,
        compiler_params=pltpu.CompilerParams(dimension_semantics=("arbitrary",), vmem_limit_bytes=VMEM_LIMIT),
    )(core, *[t for pair in zip(grads, recvs) for t in pair])


ADD_CHIPS_STEPS = 2


def _add_chips(name, chip, ps, qs, after=()):
    n = len(ps)

    def body(chip_ref, *refs):
        for k in range(n):
            p_ref, q_ref, o_ref = refs[2 * k], refs[2 * k + 1], refs[len(refs) - n + k]
            acc = p_ref[0].astype(F32)
            for j in range(3):
                acc = acc + q_ref[j].astype(F32)
            o_ref[...] = acc

    in_specs, out_specs, out_shape = [], [], []
    for q in qs:
        _, half, c = q.shape
        rb = half // ADD_CHIPS_STEPS
        in_specs += [pl.BlockSpec((1, rb, c), lambda i, chip_ref: (chip_ref[0], i, 0)),
                     pl.BlockSpec((3, rb, c), lambda i, chip_ref: (0, i, 0))]
        out_specs.append(pl.BlockSpec((rb, c), lambda i, chip_ref: (i, 0)))
        out_shape.append(jax.ShapeDtypeStruct((half, c), F32))
    return pl.pallas_call(
        body, name=name,
        grid_spec=pltpu.PrefetchScalarGridSpec(
            num_scalar_prefetch=1, grid=(ADD_CHIPS_STEPS,), in_specs=in_specs + [ANY] * len(after),
            out_specs=out_specs),
        out_shape=out_shape,
        compiler_params=pltpu.CompilerParams(dimension_semantics=("arbitrary",), vmem_limit_bytes=VMEM_LIMIT),
    )(chip, *[t for pair in zip(ps, qs) for t in pair], *after)


def _small_update(gad, gam, gl, gg, mychip, params):
    names = ["ada_b", "norm1_g", "lru_conv_b", "gate_a_w", "gate_a_b", "gate_x_w", "gate_x_b", "a_param",
             "lru_conv_w", "short_conv_w", "lru_out_g", "conv_out_g", "norm2_g", "final_g"]
    flat = [t for n in names for t in params[n]]
    nin = len(flat)

    def body(chip_ref, gad_ref, gam_ref, gl_ref, gg_ref, *refs):
        ins = {n: refs[3 * k:3 * k + 3] for k, n in enumerate(names)}
        outs = {n: refs[nin + 4 * k:nin + 4 * k + 4] for k, n in enumerate(names)}
        loss_ref, dmod_ref = refs[nin + 4 * len(names):nin + 4 * len(names) + 2]

        def dsum(ref, lo, n):
            per = ref.shape[0] // 8
            acc = ref[lo:lo + n, :].astype(F32)
            for dev in range(1, 8):
                acc = acc + ref[dev * per + lo:dev * per + lo + n, :].astype(F32)
            return acc

        def update(n, g):
            w_ref, m_ref, v_ref = ins[n]
            g_ref, d_ref, mo_ref, vo_ref = outs[n]
            g = g.reshape(w_ref.shape)
            g_ref[...] = g
            d_ref[...], mo_ref[...], vo_ref[...] = _adam_math(w_ref[...], g, m_ref[...], v_ref[...])

        d, dm, l, lw, rows = refs[-5:]
        d[...] = dsum(gad_ref, 0, 8)
        dm[...] = dsum(gam_ref, 0, 8)
        l[...] = dsum(gl_ref, 0, 16)
        for dev in range(8):
            for k in range(3):
                dmod_ref[dev:dev + 1, k * D_MODEL:(k + 1) * D_MODEL] = gad_ref[dev * 8 + k:dev * 8 + k + 1, :]
                dmod_ref[dev:dev + 1, (3 + k) * D_MODEL:(4 + k) * D_MODEL] = gam_ref[dev * 8 + k:dev * 8 + k + 1, :]
        w_ref, m_ref, v_ref = ins["ada_b"]
        g_ref, d_ref, mo_ref, vo_ref = outs["ada_b"]
        for k in range(3):
            g_ref[:, k * D_MODEL:(k + 1) * D_MODEL] = d[k:k + 1, :]
            g_ref[:, (3 + k) * D_MODEL:(4 + k) * D_MODEL] = dm[k:k + 1, :]
        d_ref[...], mo_ref[...], vo_ref[...] = _adam_math(w_ref[...], g_ref[...], m_ref[...], v_ref[...])
        update("norm1_g", d[3:4, :])
        update("norm2_g", dm[3:4, :])
        update("final_g", dm[4:5, :])
        update("gate_a_b", d[4:5, 0:D_LRU])
        update("gate_x_b", d[4:5, D_LRU:2 * D_LRU])
        update("lru_conv_b", l[4:5, :])
        update("a_param", l[8:9, :] * jax.nn.sigmoid(ins["a_param"][0][...]))
        update("lru_out_g", l[9:10, :])
        update("conv_out_g", l[10:11, :])
        loss_ref[...] = dm[5:6, 0:1]
        chip = chip_ref[0]
        acc = jnp.zeros((8, 128), F32)
        for j in range(N_CHIP):
            acc = acc + jnp.where(chip == j, l[0:8, j * 128:(j + 1) * 128], 0.0)
        lw[...] = acc
        update("lru_conv_w", lw[0:4, :])
        for s, ref in enumerate(ins["short_conv_w"]):
            for k in range(3):
                rows[s, k:k + 1, :] = ref[k]
        g = lw[5:8, :]
        new = (g,) + tuple(_adam_math(rows[0, 0:3, :], g, rows[1, 0:3, :], rows[2, 0:3, :]))
        for o_ref, val in zip(outs["short_conv_w"], new):
            rows[3, 0:3, :] = val
            for k in range(3):
                o_ref[k] = rows[3, k:k + 1, :]
        gates = dsum(gg_ref, 0, D_LRU)
        update("gate_a_w", gates[:, 0:HEAD])
        update("gate_x_w", gates[:, HEAD:2 * HEAD])

    out_shape = []
    for n in names:
        out_shape += [jax.ShapeDtypeStruct(params[n][0].shape, F32)] * 4
    out_shape += [jax.ShapeDtypeStruct((1, 1), F32), jax.ShapeDtypeStruct((8, 6 * D_MODEL), F32)]
    res = pl.pallas_call(
        body, name="small_update", out_shape=out_shape,
        in_specs=[SMEM] + [VMEM] * (4 + nin),
        out_specs=[VMEM] * len(out_shape),
        scratch_shapes=[pltpu.VMEM((8, D_MODEL), F32), pltpu.VMEM((8, D_MODEL), F32), pltpu.VMEM((16, D_LRU), F32),
                        pltpu.VMEM((8, 128), F32), pltpu.VMEM((4, 8, 128), F32)],
        compiler_params=pltpu.CompilerParams(vmem_limit_bytes=VMEM_LIMIT),
    )(mychip, gad, gam, gl, gg, *flat)
    per = {n: res[4 * k:4 * k + 4] for k, n in enumerate(names)}
    return per, res[-2], res[-1]


def _block_diag(w):
    eye = jnp.eye(8, dtype=w.dtype)
    return (eye[:, None, :, None] * w[:, :, None, :]).reshape(8 * HEAD, 8 * HEAD)


def _diag_blocks(g):
    return jnp.concatenate([g[h * HEAD:(h + 1) * HEAD, h * HEAD:(h + 1) * HEAD] for h in range(8)], axis=0)


def kernel(x, c, ada_w, ada_b, norm1_g, w_in, lru_conv_w, lru_conv_b, gate_a_w, gate_a_b, gate_x_w, gate_x_b, a_param, short_conv_w, lru_out_g, conv_out_g, w_out, norm2_g, w_mlp1, w_mlp2, final_g, loss_target, m_ada_w, m_ada_b, m_norm1_g, m_w_in, m_lru_conv_w, m_lru_conv_b, m_gate_a_w, m_gate_a_b, m_gate_x_w, m_gate_x_b, m_a_param, m_short_conv_w, m_lru_out_g, m_conv_out_g, m_w_out, m_norm2_g, m_w_mlp1, m_w_mlp2, m_final_g, v_ada_w, v_ada_b, v_norm1_g, v_w_in, v_lru_conv_w, v_lru_conv_b, v_gate_a_w, v_gate_a_b, v_gate_x_w, v_gate_x_b, v_a_param, v_short_conv_w, v_lru_out_g, v_conv_out_g, v_w_out, v_norm2_g, v_w_mlp1, v_w_mlp2, v_final_g):
    xi, yi, ci = lax.axis_index("x"), lax.axis_index("y"), lax.axis_index("c")
    mychip = 2 * xi + yi
    me = 4 * xi + 2 * yi + ci

    own_in, own_out = w_in[0].astype(BF16), w_out[0].astype(BF16)
    win_all, wout_all = _allgather_weights("allgather_mixer_weights", 1, [own_in, own_out])
    own_w1, own_w2 = w_mlp1[0].astype(BF16), w_mlp2[0].astype(BF16)
    w1_all, w2_all = _allgather_weights("allgather_mlp_weights", 2, [own_w1, own_w2])

    c_blk = jnp.zeros((8, D_MODEL), F32).at[0:1].set(c)
    cw_blk = jnp.zeros((8, 128), F32).at[0:4].set(lru_conv_w[0]).at[4:7].set(short_conv_w[0])
    c_g, cw_g = _allgather8("allgather_cond", [c_blk, cw_blk])
    c_all = c_g.reshape(8, 8, D_MODEL)[:, 0]
    cw_g = cw_g.reshape(4, 2, 8, 128)[:, 0]
    lcw = cw_g[:, 0:4].transpose(1, 0, 2).reshape(4, D_LRU)
    scw = cw_g[:, 4:7].transpose(1, 0, 2).reshape(3, D_LRU)

    mod_loc = _mod_matmul(c_all, ada_w[0])
    (mod_g,) = _allgather8("allgather_mod", [mod_loc])
    mod_all = mod_g.reshape(4, 2, 8, 6 * D_MODEL // 4)[:, 0].transpose(1, 0, 2).reshape(8, 6 * D_MODEL) + ada_b
    mod_pad = jnp.pad(mod_all.reshape(8, 6, D_MODEL), ((0, 0), (0, 2), (0, 0)))
    mod = lax.dynamic_slice_in_dim(mod_pad, me, 1, axis=0).reshape(8, D_MODEL)

    win, wout = (win_all, own_in), (wout_all, own_out)
    chip = mychip.reshape(1).astype(jnp.int32)
    core = ci.reshape(1).astype(jnp.int32)

    vecd = jnp.concatenate([norm1_g, norm2_g, final_g[None, :], jnp.concatenate([gate_a_b, gate_x_b], axis=1),
                            jnp.zeros((4, D_MODEL), F32)], axis=0)
    vecl = jnp.concatenate([lcw, lru_conv_b, scw, a_param, lru_out_g, conv_out_g, jnp.zeros((5, D_LRU), F32)], axis=0)
    gab = jnp.concatenate([_block_diag(gate_a_w[0]), _block_diag(gate_x_w[0])], axis=1).astype(BF16)
    a64 = _block_diag(jnp.full((8, HEAD, HEAD), 1.0 / HEAD, F32)).astype(BF16)

    hb, proj, hl, ycat, mixed, x1 = _mix_fwd(chip, x[0], mod, vecd, vecl, win, wout, gab, a64)
    dx1, act, dz, dmo, h2b, accm = _mlp_fwd_bwd(
        chip, x1, loss_target[0], mod, vecd, (w1_all, own_w1), (w2_all, own_w2))

    parts_mlp = list(_wgrad_mlp(6, h2b, dz, act, dmo))
    q_w1, q_w2 = _exchange_chips("rs_exchange_mlp", 0, parts_mlp)
    grad_x, accd, accl, g_win, g_wout, g_gate = _mix_bwd(
        chip, dx1, x[0], mixed, proj, hl, hb, ycat, mod, vecd, vecl, win, wout, gab, a64)

    g_mix = [g_win, g_wout.reshape(N_CHIP, WOUT_BLK, D_MODEL)]
    recv_mix = _sibling_swap("rs_swap_halves_mix", g_mix, True, collective_id=4)
    own_mlp = _add_chips("rs_add_chips_mlp", chip, parts_mlp, (q_w1, q_w2))
    gg_blk = jnp.concatenate([_diag_blocks(g_gate[:, 0:D_LRU]), _diag_blocks(g_gate[:, D_LRU:2 * D_LRU])], axis=1)
    sib_mlp, (gad, gam, gl, gg) = _swap_and_gather_seq(
        "rs_swap_reduced_mlp_allgather_small_grads", 5, own_mlp, [accd, accm, accl, gg_blk.astype(BF16)])

    parts_mix = _add_sibling("rs_add_sibling_mix", g_mix, recv_mix, core)
    landed_mix = _exchange_chips("rs_exchange_mix", 3, parts_mix)
    res_w1, res_w2 = _adam("adam_mlp", core, [(w_mlp1[0], own_mlp[0], sib_mlp[0], m_w_mlp1[0], v_w_mlp1[0]),
                                              (w_mlp2[0], own_mlp[1], sib_mlp[1], m_w_mlp2[0], v_w_mlp2[0])])

    params = {
        "ada_b": (ada_b, m_ada_b, v_ada_b), "norm1_g": (norm1_g, m_norm1_g, v_norm1_g),
        "lru_conv_b": (lru_conv_b, m_lru_conv_b, v_lru_conv_b),
        "gate_a_w": tuple(t.reshape(D_LRU, HEAD) for t in (gate_a_w, m_gate_a_w, v_gate_a_w)),
        "gate_a_b": (gate_a_b, m_gate_a_b, v_gate_a_b),
        "gate_x_w": tuple(t.reshape(D_LRU, HEAD) for t in (gate_x_w, m_gate_x_w, v_gate_x_w)),
        "gate_x_b": (gate_x_b, m_gate_x_b, v_gate_x_b), "a_param": (a_param, m_a_param, v_a_param),
        "lru_conv_w": (lru_conv_w, m_lru_conv_w, v_lru_conv_w),
        "short_conv_w": tuple(t.reshape(3, 1, D_LRU // N_CHIP) for t in (short_conv_w, m_short_conv_w, v_short_conv_w)),
        "lru_out_g": (lru_out_g, m_lru_out_g, v_lru_out_g), "conv_out_g": (conv_out_g, m_conv_out_g, v_conv_out_g),
        "norm2_g": (norm2_g, m_norm2_g, v_norm2_g),
        "final_g": tuple(t[None, :] for t in (final_g, m_final_g, v_final_g)),
    }
    small, loss_blk, dmod_cols = _small_update(gad, gam, gl, gg, chip, params)
    loss = loss_blk.reshape(())

    sct = (c_all * jax.nn.sigmoid(c_all)).T
    ada = _ada_grad_adam(chip, sct, dmod_cols, ada_w[0], m_ada_w[0], v_ada_w[0])

    own_mix = _add_chips("rs_add_chips_mix", chip, parts_mix, landed_mix, after=[res_w1[1], ada[1]])
    sib_mix = _sibling_swap("rs_swap_reduced_mix", own_mix, False)
    (res_win,) = _adam("adam_w_in", core, [(w_in[0], own_mix[0], sib_mix[0], m_w_in[0], v_w_in[0])])
    (res_wout,) = _adam("adam_w_out", core, [(w_out[0], own_mix[1], sib_mix[1], m_w_out[0], v_w_out[0])])

    res = {"ada_w": ada, "w_in": res_win, "w_out": res_wout, "w_mlp1": res_w1, "w_mlp2": res_w2}
    res = {n: tuple(t[None] for t in r) for n, r in res.items()}
    shapes = {"gate_a_w": gate_a_w.shape, "gate_x_w": gate_x_w.shape, "lru_conv_w": lru_conv_w.shape,
              "short_conv_w": short_conv_w.shape, "final_g": final_g.shape}
    for n, t in small.items():
        res[n] = tuple(u.reshape(shapes[n]) if n in shapes else u for u in t)

    order = ["ada_w", "ada_b", "norm1_g", "w_in", "lru_conv_w", "lru_conv_b", "gate_a_w", "gate_a_b", "gate_x_w",
             "gate_x_b", "a_param", "short_conv_w", "lru_out_g", "conv_out_g", "w_out", "norm2_g", "w_mlp1",
             "w_mlp2", "final_g"]
    return (loss, grad_x[None], *[res[n][0] for n in order], *[res[n][1] for n in order],
            *[res[n][2] for n in order], *[res[n][3] for n in order])
```

```python
import jax
import jax.numpy as jnp
from jax import lax
from jax.experimental import pallas as pl
from jax.experimental.pallas import tpu as pltpu
from jax.experimental.pallas import tpu_sc as plsc

F32 = jnp.float32
BF16 = jnp.bfloat16

D_MODEL = 1024
D_LRU = 512
D_IN = 2560
D_FF = 4096
N_CHIP = 4
WIN_BLK = D_IN // N_CHIP
WOUT_BLK = D_MODEL // N_CHIP
FF_BLK = D_FF // N_CHIP
HEAD = 64
EPS = 1e-6
C_GATE = 8.0
TOKEN_TILE = 256
MIX_FWD_TILE = 512
HALO = 8
VMEM_LIMIT = 60 * 1024 * 1024

ADAM_LR = 0.001
ADAM_B1 = 0.9
ADAM_B2 = 0.999
ADAM_EPS = 1e-08
ADAM_WD = 0.01
ADAM_STEP = 10

MESH = pl.DeviceIdType.MESH
ANY = pl.BlockSpec(memory_space=pl.ANY)
VMEM = pl.BlockSpec(memory_space=pltpu.VMEM)
SMEM = pl.BlockSpec(memory_space=pltpu.SMEM)


def _full(shape, single=False):
    nd = len(shape)
    if single:
        return pl.BlockSpec(shape, lambda *_: (0,) * nd, pipeline_mode=pl.Buffered(1))
    return pl.BlockSpec(shape, lambda *_: (0,) * nd)


def _dot(a, b):
    return jnp.dot(a, b, preferred_element_type=F32)


def _dot_nt(a, b):
    return lax.dot_general(a, b, (((1,), (1,)), ((), ())), preferred_element_type=F32)


def _dot_tn(a, b):
    return lax.dot_general(a, b, (((0,), (0,)), ((), ())), preferred_element_type=F32)


def _gmean(v, a64):
    hi = v.astype(BF16)
    lo = (v - hi.astype(F32)).astype(BF16)
    return _dot(hi, a64) + _dot(lo, a64)


def _gelu(x):
    u = 0.7978845608028654 * (x + 0.044715 * x * x * x)
    t = jnp.tanh(u)
    return 0.5 * x * (1.0 + t), t


def _gelu_grad(x, t):
    du = 0.7978845608028654 * (1.0 + 3.0 * 0.044715 * x * x)
    return 0.5 * (1.0 + t) + 0.5 * x * (1.0 - t * t) * du


def _log1p_pos(y):
    return jnp.where(y < 1e-2, y * (1.0 - y * (0.5 - y * (1.0 / 3.0 - y * 0.25))), jnp.log(1.0 + y))


def _softplus(a):
    return jnp.maximum(a, 0.0) + _log1p_pos(jnp.exp(-jnp.abs(a)))


def _neg_expm1(z):
    series = -z * (1.0 + z * (0.5 + z * (1.0 / 6.0 + z * (1.0 / 24.0))))
    return jnp.where(z > -0.02, series, 1.0 - jnp.exp(z))


def _scan_fwd(a, b, row):
    n = a.shape[0]
    d = 1
    while d < n:
        m = row >= d
        b = jnp.where(m, a * pltpu.roll(b, d, 0) + b, b)
        a = jnp.where(m, a * pltpu.roll(a, d, 0), a)
        d *= 2
    return a, b


def _scan_rev(a, b, row):
    n = a.shape[0]
    d = 1
    while d < n:
        m = row < n - d
        b = jnp.where(m, b + a * pltpu.roll(b, n - d, 0), b)
        a = jnp.where(m, a * pltpu.roll(a, n - d, 0), a)
        d *= 2
    return a, b


def _colsum(v):
    return jnp.sum(v, axis=0, keepdims=True)


def _load_gathered(chip, gathered, own, slot, sems):
    copies = []
    for j in range(N_CHIP):
        @pl.when(chip == j)
        def _(j=j):
            pltpu.make_async_copy(own, slot(j), sems.at[j]).start()

        @pl.when(chip != j)
        def _(j=j):
            pltpu.make_async_copy(gathered.at[j], slot(j), sems.at[j]).start()

        copies.append(pltpu.make_async_copy(own, slot(j), sems.at[j]))
    return copies


def _lru_gates(xlb, gab, gbias, sp, first_row):
    g = _dot(xlb, gab) + gbias
    r = jax.nn.sigmoid(g[:, :D_LRU])
    ig = jax.nn.sigmoid(g[:, D_LRU:])
    la = (-C_GATE) * r * sp
    a = jnp.exp(la)
    msq = jnp.sqrt(_neg_expm1(2.0 * la))
    mult = jnp.where(first_row, 1.0, msq)
    return r, ig, a, msq, mult


def _mix_fwd(chip, x, mod, vecd, vecl, win, wout, gab, a64):
    s = x.shape[0]
    ts = MIX_FWD_TILE
    nt = s // ts

    def body(chip_ref, x_ref, mod_ref, vd_ref, vl_ref, win_hbm, win_own, wout_hbm, wout_own, gab_ref, a64_ref,
             hb_ref, proj_ref, hl_ref, ycat_ref, mixed_ref, x1_ref,
             win_ref, wout_ref, ext_lx, ext_cv, hcar, sems):
        i = pl.program_id(0)

        @pl.when(i == 0)
        def _():
            cps = _load_gathered(chip_ref[0], win_hbm, win_own, lambda j: win_ref.at[j], sems.at[pl.ds(0, N_CHIP)])
            cps += _load_gathered(chip_ref[0], wout_hbm, wout_own,
                                  lambda j: wout_ref.at[pl.ds(j * WOUT_BLK, WOUT_BLK), :],
                                  sems.at[pl.ds(N_CHIP, N_CHIP)])
            ext_lx[0:HALO, :] = jnp.zeros((HALO, D_LRU), F32)
            ext_cv[0:HALO, :] = jnp.zeros((HALO, D_LRU), F32)
            hcar[...] = jnp.zeros_like(hcar)
            for cp in cps:
                cp.wait()

        row = lax.broadcasted_iota(jnp.int32, (ts, D_LRU), 0)
        first_row = jnp.logical_and(row == 0, i == 0)
        xt = x_ref[...]
        shift1, scale1, gate1 = mod_ref[0:1, :], mod_ref[1:2, :], mod_ref[2:3, :]
        r1 = lax.rsqrt(jnp.mean(xt * xt, axis=-1, keepdims=True) + EPS)
        h = (xt * r1) * vd_ref[0:1, :] * (1.0 + scale1) + shift1
        hb = h.astype(BF16)
        hb_ref[...] = hb
        for j in range(N_CHIP):
            proj_ref[:, j * WIN_BLK:(j + 1) * WIN_BLK] = _dot(hb, win_ref[j])
        u_ly = proj_ref[:, 512:1024]
        u_b = proj_ref[:, 1024:1536]

        ext_lx[HALO:HALO + ts, :] = proj_ref[:, 0:512]
        xl = vl_ref[4:5, :] + vl_ref[0:1, :] * ext_lx[pl.ds(5, ts), :]
        for k in range(1, 4):
            xl = xl + vl_ref[k:k + 1, :] * ext_lx[pl.ds(5 + k, ts), :]
        ext_lx[0:HALO, :] = ext_lx[ts:ts + HALO, :]
        sp = _softplus(vl_ref[8:9, :])
        _, ig, a, _, mult = _lru_gates(xl.astype(BF16), gab_ref[...], vd_ref[3:4, :], sp, first_row)
        acum, hloc = _scan_fwd(a, mult * (ig * xl), row)
        hl = hloc + acum * hcar[0:1, :]
        hl_ref[...] = hl
        hcar[0:1, :] = hl_ref[ts - 1:ts, :]
        ge, _ = _gelu(u_ly)
        p = ge * hl
        y_lru = p * lax.rsqrt(_gmean(p * p, a64_ref[...]) + EPS) * vl_ref[9:10, :]
        ycat_ref[:, 0:512] = y_lru.astype(BF16)

        ext_cv[HALO:HALO + ts, :] = proj_ref[:, 1536:2048] * proj_ref[:, 2048:2560]
        q = vl_ref[5:6, :] * ext_cv[pl.ds(6, ts), :]
        for k in range(1, 3):
            q = q + vl_ref[5 + k:6 + k, :] * ext_cv[pl.ds(6 + k, ts), :]
        ext_cv[0:HALO, :] = ext_cv[ts:ts + HALO, :]
        yc = u_b * q
        y_conv = yc * lax.rsqrt(_gmean(yc * yc, a64_ref[...]) + EPS) * vl_ref[10:11, :]
        ycat_ref[:, 512:1024] = y_conv.astype(BF16)

        mixed = _dot(ycat_ref[...], wout_ref[...])
        mixed_ref[...] = mixed
        x1_ref[...] = xt + gate1 * mixed

    tile = lambda w: pl.BlockSpec((ts, w), lambda i: (i, 0))
    return pl.pallas_call(
        body, name="mix_fwd", grid=(nt,),
        in_specs=[SMEM, tile(D_MODEL), _full((8, D_MODEL)), _full((8, D_MODEL)), _full((16, D_LRU)),
                  ANY, ANY, ANY, ANY, _full((D_LRU, 2 * D_LRU), True), _full((D_LRU, D_LRU), True)],
        out_specs=[tile(D_MODEL), tile(D_IN), tile(D_LRU), tile(D_MODEL), tile(D_MODEL), tile(D_MODEL)],
        out_shape=[jax.ShapeDtypeStruct((s, D_MODEL), BF16), jax.ShapeDtypeStruct((s, D_IN), F32),
                   jax.ShapeDtypeStruct((s, D_LRU), F32), jax.ShapeDtypeStruct((s, D_MODEL), BF16),
                   jax.ShapeDtypeStruct((s, D_MODEL), F32), jax.ShapeDtypeStruct((s, D_MODEL), F32)],
        scratch_shapes=[pltpu.VMEM((N_CHIP, D_MODEL, WIN_BLK), BF16), pltpu.VMEM((D_MODEL, D_MODEL), BF16),
                        pltpu.VMEM((ts + HALO, D_LRU), F32), pltpu.VMEM((ts + HALO, D_LRU), F32),
                        pltpu.VMEM((HALO, D_LRU), F32), pltpu.SemaphoreType.DMA((2 * N_CHIP,))],
        compiler_params=pltpu.CompilerParams(dimension_semantics=("arbitrary",), vmem_limit_bytes=VMEM_LIMIT),
    )(chip, x, mod, vecd, vecl, *win, *wout, gab, a64)


def _mlp_fwd_bwd(chip, x1, target, mod, vecd, w1, w2):
    s = x1.shape[0]
    ts = TOKEN_TILE
    nt = s // ts

    def body(chip_ref, x1_ref, tg_ref, mod_ref, vd_ref, w1_hbm, w1_own, w2_hbm, w2_own,
             dx1_ref, act_ref, dz_ref, dmo_ref, h2_ref, acc_ref, w1_v, w2_v, rz_v, sems):
        i = pl.program_id(0)

        @pl.when(i == 0)
        def _():
            cps = _load_gathered(chip_ref[0], w1_hbm, w1_own, lambda j: w1_v.at[j], sems.at[pl.ds(0, N_CHIP)])
            cps += _load_gathered(chip_ref[0], w2_hbm, w2_own, lambda j: w2_v.at[j], sems.at[pl.ds(N_CHIP, N_CHIP)])
            acc_ref[...] = jnp.zeros_like(acc_ref)
            for cp in cps:
                cp.wait()

        xt = x1_ref[...]
        shift2, scale2, gate2 = mod_ref[3:4, :], mod_ref[4:5, :], mod_ref[5:6, :]
        g2, gf = vd_ref[1:2, :], vd_ref[2:3, :]
        r2 = lax.rsqrt(jnp.mean(xt * xt, axis=-1, keepdims=True) + EPS)
        n2 = xt * r2
        h2b = (n2 * g2 * (1.0 + scale2) + shift2).astype(BF16)
        h2_ref[...] = h2b
        for j in range(N_CHIP):
            rz_v[j] = jnp.maximum(_dot(h2b, w1_v[j]), 0.0)
        mo = jnp.zeros((ts, D_MODEL), F32)
        for j in range(N_CHIP):
            rz = rz_v[j]
            actb = (rz * rz).astype(BF16)
            act_ref[:, j * FF_BLK:(j + 1) * FF_BLK] = actb
            mo = mo + _dot(actb, w2_v[j])
        x2 = xt + gate2 * mo
        r3 = lax.rsqrt(jnp.mean(x2 * x2, axis=-1, keepdims=True) + EPS)
        n3 = x2 * r3
        e = n3 * gf - tg_ref[...]
        loss = (0.5 / D_MODEL) * jnp.sum(_colsum(e * e), axis=1, keepdims=True)
        dy = e * (1.0 / D_MODEL)
        acc_ref[4:5, :] += _colsum(dy * n3)
        acc_ref[5:6, :] += jnp.broadcast_to(loss, (1, D_MODEL))
        dn3 = dy * gf
        dx2 = r3 * (dn3 - n3 * jnp.mean(dn3 * n3, axis=-1, keepdims=True))
        acc_ref[2:3, :] += _colsum(dx2 * mo)
        dmob = (dx2 * gate2).astype(BF16)
        dmo_ref[...] = dmob
        for j in range(N_CHIP):
            dz_ref[:, j * FF_BLK:(j + 1) * FF_BLK] = (_dot_nt(dmob, w2_v[j]) * (2.0 * rz_v[j])).astype(BF16)
        dh2 = jnp.zeros((ts, D_MODEL), F32)
        for j in range(N_CHIP):
            dh2 = dh2 + _dot_nt(dz_ref[:, j * FF_BLK:(j + 1) * FF_BLK], w1_v[j])
        acc_ref[1:2, :] += _colsum(dh2 * (n2 * g2))
        acc_ref[0:1, :] += _colsum(dh2)
        dhn2 = dh2 * (1.0 + scale2)
        acc_ref[3:4, :] += _colsum(dhn2 * n2)
        dn2 = dhn2 * g2
        dx1_ref[...] = dx2 + r2 * (dn2 - n2 * jnp.mean(dn2 * n2, axis=-1, keepdims=True))

    tile = lambda w: pl.BlockSpec((ts, w), lambda i: (i, 0))
    return pl.pallas_call(
        body, name="mlp_fwd_bwd", grid=(nt,),
        in_specs=[SMEM, tile(D_MODEL), tile(D_MODEL), _full((8, D_MODEL)), _full((8, D_MODEL)), ANY, ANY, ANY, ANY],
        out_specs=[tile(D_MODEL), tile(D_FF), tile(D_FF), tile(D_MODEL), tile(D_MODEL), _full((8, D_MODEL))],
        out_shape=[jax.ShapeDtypeStruct((s, D_MODEL), F32), jax.ShapeDtypeStruct((s, D_FF), BF16),
                   jax.ShapeDtypeStruct((s, D_FF), BF16), jax.ShapeDtypeStruct((s, D_MODEL), BF16),
                   jax.ShapeDtypeStruct((s, D_MODEL), BF16), jax.ShapeDtypeStruct((8, D_MODEL), F32)],
        scratch_shapes=[pltpu.VMEM((N_CHIP, D_MODEL, FF_BLK), BF16), pltpu.VMEM((N_CHIP, FF_BLK, D_MODEL), BF16),
                        pltpu.VMEM((N_CHIP, ts, FF_BLK), F32), pltpu.SemaphoreType.DMA((2 * N_CHIP,))],
        compiler_params=pltpu.CompilerParams(dimension_semantics=("arbitrary",), vmem_limit_bytes=VMEM_LIMIT),
    )(chip, x1, target, mod, vecd, *w1, *w2)


def _mix_bwd(chip, dx1, x, mixed, proj, hl, hb, ycat, mod, vecd, vecl, win, wout, gab, a64):
    s = x.shape[0]
    ts = TOKEN_TILE
    nt = s // ts
    hpt = ts // HALO

    def body(chip_ref, dx1_ref, x_ref, mixed_ref, proj_ref, projh_ref, hl_ref, hlh_ref, hb_ref, ycat_ref,
             mod_ref, vd_ref, vl_ref, win_hbm, win_own, wout_hbm, wout_own, gab_ref, a64_ref,
             gx_ref, accd_ref, accl_ref, gwin_hbm, gwout_hbm, ggate_hbm,
             win_ref, wout_ref, dproj_ref, dgb_ref, gwin_acc, gwout_acc, ggate_acc,
             ext_lx, ext_cv, ext_hl, ext_dxl, ext_dq, gbuf, gcar, acar, sems):
        i = pl.program_id(0)
        ri = nt - 1 - i

        @pl.when(i == 0)
        def _():
            gwin_acc[...] = jnp.zeros_like(gwin_acc)
            gwout_acc[...] = jnp.zeros_like(gwout_acc)
            ggate_acc[...] = jnp.zeros_like(ggate_acc)
            cps = _load_gathered(chip_ref[0], win_hbm, win_own, lambda j: win_ref.at[j], sems.at[pl.ds(0, N_CHIP)])
            cps += _load_gathered(chip_ref[0], wout_hbm, wout_own,
                                  lambda j: wout_ref.at[pl.ds(j * WOUT_BLK, WOUT_BLK), :],
                                  sems.at[pl.ds(N_CHIP, N_CHIP)])
            for cp in cps:
                cp.wait()
            accd_ref[...] = jnp.zeros_like(accd_ref)
            accl_ref[...] = jnp.zeros_like(accl_ref)
            ext_dxl[ts:ts + HALO, :] = jnp.zeros((HALO, D_LRU), F32)
            ext_dq[ts:ts + HALO, :] = jnp.zeros((HALO, D_LRU), F32)
            gcar[...] = jnp.zeros_like(gcar)
            acar[...] = jnp.zeros_like(acar)

        row = lax.broadcasted_iota(jnp.int32, (ts, D_LRU), 0)
        first_row = jnp.logical_and(row == 0, ri == 0)
        halo_on = jnp.where(ri == 0, 0.0, 1.0)
        shift1, scale1, gate1 = mod_ref[0:1, :], mod_ref[1:2, :], mod_ref[2:3, :]
        g1 = vd_ref[0:1, :]
        a64m = a64_ref[...]
        lg, cg = vl_ref[9:10, :], vl_ref[10:11, :]

        dx1 = dx1_ref[...]
        accd_ref[2:3, :] += _colsum(dx1 * mixed_ref[...])
        dmb = (dx1 * gate1).astype(BF16)
        gwout_acc[...] += _dot_tn(ycat_ref[...], dmb)
        dycat = _dot_nt(dmb, wout_ref[...])
        dyl = dycat[:, 0:512]
        dyv = dycat[:, 512:1024]

        u_ly = proj_ref[:, 512:1024]
        u_b = proj_ref[:, 1024:1536]
        u_c = proj_ref[:, 1536:2048]
        u_v = proj_ref[:, 2048:2560]
        ext_lx[0:HALO, :] = projh_ref[:, 0:512] * halo_on
        ext_lx[HALO:HALO + ts, :] = proj_ref[:, 0:512]
        xl = vl_ref[4:5, :] + vl_ref[0:1, :] * ext_lx[pl.ds(5, ts), :]
        for k in range(1, 4):
            xl = xl + vl_ref[k:k + 1, :] * ext_lx[pl.ds(5 + k, ts), :]
        xlb = xl.astype(BF16)
        sp = _softplus(vl_ref[8:9, :])
        r, ig, a, msq, mult = _lru_gates(xlb, gab_ref[...], vd_ref[3:4, :], sp, first_row)
        hl = hl_ref[...]
        ge, th = _gelu(u_ly)
        p = ge * hl
        rl = lax.rsqrt(_gmean(p * p, a64m) + EPS)
        nl = p * rl
        ext_cv[0:HALO, :] = projh_ref[:, 1536:2048] * projh_ref[:, 2048:2560] * halo_on
        ext_cv[HALO:HALO + ts, :] = u_c * u_v
        q = vl_ref[5:6, :] * ext_cv[pl.ds(6, ts), :]
        for k in range(1, 3):
            q = q + vl_ref[5 + k:6 + k, :] * ext_cv[pl.ds(6 + k, ts), :]
        yc = u_b * q
        rc = lax.rsqrt(_gmean(yc * yc, a64m) + EPS)
        nc = yc * rc

        accl_ref[9:10, :] += _colsum(dyl * nl)
        dnl = dyl * lg
        dp = rl * (dnl - nl * _gmean(dnl * nl, a64m))
        dproj_ref[:, 512:1024] = ((dp * hl) * _gelu_grad(u_ly, th)).astype(BF16)
        a_next = jnp.where(row == ts - 1, acar[0:1, :], pltpu.roll(a, ts - 1, 0))
        acum, gloc = _scan_rev(a_next, dp * ge, row)
        gbuf[...] = gloc + acum * gcar[0:1, :]
        gcar[0:1, :] = gbuf[0:1, :]
        ext_hl[0:HALO, :] = hlh_ref[...] * halo_on
        ext_hl[HALO:HALO + ts, :] = hl
        acar[...] = a[0:HALO, :]
        gt = gbuf[...]
        da = gt * ext_hl[pl.ds(HALO - 1, ts), :]
        dmult = gt * ig * xl
        di = gt * mult * xl
        dxl = gt * mult * ig
        dla = da * a - jnp.where(first_row, 0.0, dmult * a * a / msq)
        accl_ref[8:9, :] += _colsum(dla * ((-C_GATE) * r))
        dra = dla * ((-C_GATE) * sp) * r * (1.0 - r)
        dia = di * ig * (1.0 - ig)
        accd_ref[4:5, 0:D_LRU] += _colsum(dra)
        accd_ref[4:5, D_LRU:2 * D_LRU] += _colsum(dia)
        dgb_ref[:, 0:D_LRU] = dra.astype(BF16)
        dgb_ref[:, D_LRU:2 * D_LRU] = dia.astype(BF16)
        dxl = dxl + _dot_nt(dgb_ref[...], gab_ref[...])
        ggate_acc[...] += _dot_tn(xlb, dgb_ref[...])
        accl_ref[4:5, :] += _colsum(dxl)
        for k in range(4):
            accl_ref[k:k + 1, :] += _colsum(dxl * ext_lx[pl.ds(5 + k, ts), :])
        ext_dxl[0:ts, :] = dxl
        du_lx = vl_ref[0:1, :] * ext_dxl[pl.ds(3, ts), :]
        for k in range(1, 4):
            du_lx = du_lx + vl_ref[k:k + 1, :] * ext_dxl[pl.ds(3 - k, ts), :]
        ext_dxl[ts:ts + HALO, :] = ext_dxl[0:HALO, :]
        dproj_ref[:, 0:512] = du_lx.astype(BF16)

        accl_ref[10:11, :] += _colsum(dyv * nc)
        dnc = dyv * cg
        dyc = rc * (dnc - nc * _gmean(dnc * nc, a64m))
        dproj_ref[:, 1024:1536] = (dyc * q).astype(BF16)
        dq = dyc * u_b
        for k in range(3):
            accl_ref[5 + k:6 + k, :] += _colsum(dq * ext_cv[pl.ds(6 + k, ts), :])
        ext_dq[0:ts, :] = dq
        dcv = vl_ref[5:6, :] * ext_dq[pl.ds(2, ts), :]
        for k in range(1, 3):
            dcv = dcv + vl_ref[5 + k:6 + k, :] * ext_dq[pl.ds(2 - k, ts), :]
        ext_dq[ts:ts + HALO, :] = ext_dq[0:HALO, :]
        dproj_ref[:, 1536:2048] = (dcv * u_v).astype(BF16)
        dproj_ref[:, 2048:2560] = (dcv * u_c).astype(BF16)

        dh = _dot_nt(dproj_ref[:, 0:WIN_BLK], win_ref[0])
        for j in range(1, N_CHIP):
            dh = dh + _dot_nt(dproj_ref[:, j * WIN_BLK:(j + 1) * WIN_BLK], win_ref[j])
        for j in range(N_CHIP):
            gwin_acc[j] += _dot_tn(hb_ref[...], dproj_ref[:, j * WIN_BLK:(j + 1) * WIN_BLK])
        xt = x_ref[...]
        r1 = lax.rsqrt(jnp.mean(xt * xt, axis=-1, keepdims=True) + EPS)
        n1 = xt * r1
        accd_ref[1:2, :] += _colsum(dh * (n1 * g1))
        accd_ref[0:1, :] += _colsum(dh)
        dhn1 = dh * (1.0 + scale1)
        accd_ref[3:4, :] += _colsum(dhn1 * n1)
        dn1 = dhn1 * g1
        gx_ref[...] = dx1 + r1 * (dn1 - n1 * jnp.mean(dn1 * n1, axis=-1, keepdims=True))

        @pl.when(i == nt - 1)
        def _():
            outs = [pltpu.make_async_copy(acc, dst, sems.at[k]) for k, (acc, dst) in enumerate(
                ((gwin_acc, gwin_hbm), (gwout_acc, gwout_hbm), (ggate_acc, ggate_hbm)))]
            for cp in outs:
                cp.start()
            for cp in outs:
                cp.wait()

    tile = lambda w: pl.BlockSpec((ts, w), lambda i: (nt - 1 - i, 0))
    halo = lambda w: pl.BlockSpec((HALO, w), lambda i: (jnp.maximum((nt - 1 - i) * hpt - 1, 0), 0))
    ext = pltpu.VMEM((ts + HALO, D_LRU), F32)
    return pl.pallas_call(
        body, name="mix_bwd", grid=(nt,),
        in_specs=[SMEM, tile(D_MODEL), tile(D_MODEL), tile(D_MODEL), tile(D_IN), halo(D_IN), tile(D_LRU), halo(D_LRU),
                  tile(D_MODEL), tile(D_MODEL), _full((8, D_MODEL)), _full((8, D_MODEL)), _full((16, D_LRU)),
                  ANY, ANY, ANY, ANY, _full((D_LRU, 2 * D_LRU), True), _full((D_LRU, D_LRU), True)],
        out_specs=[tile(D_MODEL), _full((8, D_MODEL)), _full((16, D_LRU)), ANY, ANY, ANY],
        out_shape=[jax.ShapeDtypeStruct((s, D_MODEL), F32),
                   jax.ShapeDtypeStruct((8, D_MODEL), F32), jax.ShapeDtypeStruct((16, D_LRU), F32),
                   jax.ShapeDtypeStruct((N_CHIP, D_MODEL, WIN_BLK), F32), jax.ShapeDtypeStruct((D_MODEL, D_MODEL), F32),
                   jax.ShapeDtypeStruct((D_LRU, 2 * D_LRU), F32)],
        scratch_shapes=[pltpu.VMEM((N_CHIP, D_MODEL, WIN_BLK), BF16), pltpu.VMEM((D_MODEL, D_MODEL), BF16),
                        pltpu.VMEM((ts, D_IN), BF16), pltpu.VMEM((ts, 2 * D_LRU), BF16),
                        pltpu.VMEM((N_CHIP, D_MODEL, WIN_BLK), F32), pltpu.VMEM((D_MODEL, D_MODEL), F32),
                        pltpu.VMEM((D_LRU, 2 * D_LRU), F32),
                        ext, ext, ext, ext, ext, pltpu.VMEM((ts, D_LRU), F32),
                        pltpu.VMEM((HALO, D_LRU), F32), pltpu.VMEM((HALO, D_LRU), F32),
                        pltpu.SemaphoreType.DMA((2 * N_CHIP,))],
        compiler_params=pltpu.CompilerParams(dimension_semantics=("arbitrary",), vmem_limit_bytes=VMEM_LIMIT),
    )(chip, dx1, x, mixed, proj, proj, hl, hl, hb, ycat, mod, vecd, vecl, *win, *wout, gab, a64)


def _wgrad_mlp(collective_id, h2b, dz, act, dmo):
    s = h2b.shape[0]
    nstep = 2 * N_CHIP
    half = FF_BLK // 2

    def body(h2_ref, dz_ref, act_ref, dmo_ref, p1_hbm, p2_hbm, buf, landed, summed, send_sems, recv_sems, out_sems):
        j = pl.program_id(0)
        x, y, c, _ = _position()

        def give(jj):
            return pltpu.make_async_remote_copy(
                src_ref=buf.at[jj % 2, pl.ds((1 - c) * half, half), :], dst_ref=landed.at[jj],
                send_sem=send_sems.at[jj % 2], recv_sem=recv_sems.at[jj],
                device_id=(x, y, 1 - c), device_id_type=MESH)

        def write_out(jj, dst):
            return pltpu.make_async_copy(summed.at[jj % 2], dst, out_sems.at[jj % 2])

        def add_sibling(jj):
            give(jj).wait_recv()
            own = buf[jj % 2, pl.ds(pl.multiple_of(c * half, half), half), :]
            summed[jj % 2] = (own.astype(F32) + landed[jj].astype(F32)).astype(BF16)

        @pl.when(j == 0)
        def _():
            pl.semaphore_signal(pltpu.get_barrier_semaphore(), inc=1, device_id=(x, y, 1 - c), device_id_type=MESH)

        @pl.when(j >= 2)
        def _():
            give(j - 2).wait_send()

        @pl.when(j < N_CHIP)
        def _():
            buf[j % 2] = _dot_tn(act_ref[...], dmo_ref[...]).astype(BF16)

        @pl.when(j >= N_CHIP)
        def _():
            buf[j % 2] = _dot_tn(h2_ref[...], dz_ref[...]).astype(BF16)

        @pl.when(j == 0)
        def _():
            pl.semaphore_wait(pltpu.get_barrier_semaphore(), 1)

        give(j).start()

        @pl.when(j >= 1)
        def _():
            jm = j - 1

            @pl.when(jm >= 2)
            def _():
                write_out(jm - 2, p2_hbm.at[0]).wait()

            add_sibling(jm)

            @pl.when(jm < N_CHIP)
            def _():
                write_out(jm, p2_hbm.at[jm]).start()

            @pl.when(jm >= N_CHIP)
            def _():
                write_out(jm, p1_hbm.at[jm - N_CHIP]).start()

        @pl.when(j == nstep - 1)
        def _():
            last = nstep - 1
            write_out(last - 2, p1_hbm.at[0]).wait()
            add_sibling(last)
            write_out(last, p1_hbm.at[N_CHIP - 1]).start()
            for jj in (last - 1, last):
                give(jj).wait_send()
                write_out(jj, p1_hbm.at[0]).wait()

    sds = jax.ShapeDtypeStruct((N_CHIP, half, D_MODEL), BF16)
    whole = pl.BlockSpec((s, D_MODEL), lambda j: (0, 0))
    return pl.pallas_call(
        body, name="wgrad_mlp", grid=(nstep,),
        in_specs=[whole, pl.BlockSpec((s, FF_BLK), lambda j: (0, jnp.maximum(j - N_CHIP, 0))),
                  pl.BlockSpec((s, FF_BLK), lambda j: (0, jnp.minimum(j, N_CHIP - 1))), whole],
        out_specs=[ANY, ANY], out_shape=[sds, sds],
        scratch_shapes=[pltpu.VMEM((2, FF_BLK, D_MODEL), BF16), pltpu.VMEM((nstep, half, D_MODEL), BF16),
                        pltpu.VMEM((2, half, D_MODEL), BF16), pltpu.SemaphoreType.DMA((2,)),
                        pltpu.SemaphoreType.DMA((nstep,)), pltpu.SemaphoreType.DMA((2,))],
        compiler_params=pltpu.CompilerParams(dimension_semantics=("arbitrary",), vmem_limit_bytes=VMEM_LIMIT,
                                             collective_id=collective_id),
    )(h2b, dz, act, dmo)


def _mod_matmul(c_all, ada_w_loc):
    n = ada_w_loc.shape[1]
    cb = 512

    def body(c_ref, w_ref, o_ref):
        c = c_ref[...]
        sc = c * jax.nn.sigmoid(c)
        o_ref[...] = _dot(sc.astype(BF16), w_ref[...].astype(BF16))

    return pl.pallas_call(
        body, name="mod_matmul", grid=(n // cb,),
        in_specs=[_full((8, D_MODEL)), pl.BlockSpec((D_MODEL, cb), lambda j: (0, j))],
        out_specs=pl.BlockSpec((8, cb), lambda j: (0, j)),
        out_shape=jax.ShapeDtypeStruct((8, n), F32),
        compiler_params=pltpu.CompilerParams(dimension_semantics=("arbitrary",), vmem_limit_bytes=VMEM_LIMIT),
    )(c_all, ada_w_loc)


def _adam_math(w, g, m, v):
    m = ADAM_B1 * m + (1.0 - ADAM_B1) * g
    v = ADAM_B2 * v + (1.0 - ADAM_B2) * (g * g)
    m_hat = m / (1.0 - ADAM_B1 ** ADAM_STEP)
    v_hat = v / (1.0 - ADAM_B2 ** ADAM_STEP)
    delta = (-ADAM_LR) * (m_hat / (jnp.sqrt(v_hat) + ADAM_EPS) + ADAM_WD * w)
    return delta, m, v


def _adam(name, core, shards):
    n = len(shards)
    r, c = shards[0][0].shape
    half = r // 2
    rb = min(half, 128)
    nh = half // rb

    def body(core_ref, *refs):
        ins, outs = refs[:5 * n], refs[5 * n:]
        mine = (pl.program_id(0) // nh) == core_ref[0]
        for k in range(n):
            w_ref, go_ref, gs_ref, m_ref, v_ref = ins[5 * k:5 * k + 5]
            g_ref, d_ref, mo_ref, vo_ref = outs[4 * k:4 * k + 4]
            g = jnp.where(mine, go_ref[...], gs_ref[...])
            g_ref[...] = g
            d_ref[...], mo_ref[...], vo_ref[...] = _adam_math(w_ref[...], g, m_ref[...], v_ref[...])

    spec = pl.BlockSpec((rb, c), lambda i, core_ref: (i, 0))
    own = pl.BlockSpec((rb, c), lambda i, core_ref: (jnp.where(i // nh == core_ref[0], i % nh, 0), 0))
    sib = pl.BlockSpec((rb, c), lambda i, core_ref: (jnp.where(i // nh == core_ref[0], 0, i % nh), 0))
    sds = jax.ShapeDtypeStruct((r, c), F32)
    res = pl.pallas_call(
        body, name=name,
        grid_spec=pltpu.PrefetchScalarGridSpec(
            num_scalar_prefetch=1, grid=(r // rb,),
            in_specs=[spec, own, sib, spec, spec] * n, out_specs=[spec] * (4 * n)),
        out_shape=[sds] * (4 * n),
        compiler_params=pltpu.CompilerParams(dimension_semantics=("arbitrary",), vmem_limit_bytes=VMEM_LIMIT),
    )(core, *[t for s in shards for t in s])
    return [res[4 * k:4 * k + 4] for k in range(n)]


def _ada_grad_adam(chip, sct, dmod_cols, w, m, v):
    r, c = w.shape
    rb = 256

    def body(chip_ref, s_ref, dm_ref, w_ref, m_ref, v_ref, g_ref, d_ref, mo_ref, vo_ref):
        g = s_ref[:, 0:1] * dm_ref[0:1, :]
        for b in range(1, 8):
            g = g + s_ref[:, b:b + 1] * dm_ref[b:b + 1, :]
        g_ref[...] = g
        d_ref[...], mo_ref[...], vo_ref[...] = _adam_math(w_ref[...], g, m_ref[...], v_ref[...])

    spec = pl.BlockSpec((rb, c), lambda i, chip_ref: (i, 0))
    sds = jax.ShapeDtypeStruct((r, c), F32)
    return pl.pallas_call(
        body, name="ada_grad_adam",
        grid_spec=pltpu.PrefetchScalarGridSpec(
            num_scalar_prefetch=1, grid=(r // rb,),
            in_specs=[pl.BlockSpec((rb, 8), lambda i, chip_ref: (i, 0)),
                      pl.BlockSpec((8, c), lambda i, chip_ref: (0, chip_ref[0])), spec, spec, spec],
            out_specs=[spec] * 4),
        out_shape=[sds] * 4,
        compiler_params=pltpu.CompilerParams(dimension_semantics=("arbitrary",), vmem_limit_bytes=VMEM_LIMIT),
    )(chip, sct, dmod_cols, w, m, v)


def _position():
    x, y, c = lax.axis_index("x"), lax.axis_index("y"), lax.axis_index("c")
    chips = [(1 - x, y), (x, 1 - y), (1 - x, 1 - y)]
    return x, y, c, chips


def _ag8_run(ins, outs, send_sems, recv_sems, local_sems):
    na = len(ins)
    x, y, c, chips = _position()
    me, sibling = (x, y, c), (x, y, 1 - c)
    first, passed, local = [], [], []
    for a in range(na):
        m_per = ins[a].shape[0]

        def rows(px, py, pc, a=a, m_per=m_per):
            return outs[a].at[pl.ds((4 * px + 2 * py + pc) * m_per, m_per), :]

        def copy(k, block, to, src=None, a=a, rows=rows):
            return pltpu.make_async_remote_copy(
                src_ref=rows(*block) if src is None else src, dst_ref=rows(*block),
                send_sem=send_sems.at[7 * a + k], recv_sem=recv_sems.at[7 * a + k],
                device_id=to, device_id_type=MESH)

        mine = pltpu.make_async_copy(ins[a], rows(*me), local_sems.at[a])
        mine.start()
        local.append(mine)
        f = [copy(0, me, sibling, src=ins[a])]
        f += [copy(1 + j, me, (*chip, c), src=ins[a]) for j, chip in enumerate(chips)]
        for cp in f:
            cp.start()
        first.append((f, copy))
    for a in range(na):
        f, copy = first[a]
        p = [copy(4 + j, (*chip, c), sibling) for j, chip in enumerate(chips)]
        for j, chip in enumerate(chips):
            copy(1 + j, (*chip, c), me).wait_recv()
            p[j].start()
        passed.append(p)
    for a in range(na):
        f, copy = first[a]
        copy(0, sibling, me).wait_recv()
        for j, chip in enumerate(chips):
            copy(4 + j, (*chip, 1 - c), me).wait_recv()
        for cp in f + passed[a]:
            cp.wait_send()
        local[a].wait()


def _allgather8(name, arrs):
    na = len(arrs)

    def body(*refs):
        _ag8_run(refs[:na], refs[na:2 * na], *refs[2 * na:])

    return pl.pallas_call(
        body, name=name,
        out_shape=[jax.ShapeDtypeStruct((8 * a.shape[0], a.shape[1]), a.dtype) for a in arrs],
        in_specs=[VMEM] * na, out_specs=[VMEM] * na,
        scratch_shapes=[pltpu.SemaphoreType.DMA((7 * na,)), pltpu.SemaphoreType.DMA((7 * na,)),
                        pltpu.SemaphoreType.DMA((na,))],
        compiler_params=pltpu.CompilerParams(vmem_limit_bytes=VMEM_LIMIT),
    )(*arrs)


AG_SEMS = 7
AG_CHUNKS = 2


def _ag_copies(ins, outs, send_sems, recv_sems):
    x, y, c, chips = _position()
    sibling = (x, y, 1 - c)
    xn, yn, dg = [2 * chip[0] + chip[1] for chip in chips]
    to_x, to_y = (1 - x, y, c), (x, 1 - y, c)
    res = []
    for a in range(len(ins)):
        half = ins[a].shape[0] // 2
        piece = half // AG_CHUNKS
        for p in range(AG_CHUNKS):
            def copy(k, dst, to, src=None, base=AG_SEMS * (AG_CHUNKS * a + p)):
                return pltpu.make_async_remote_copy(
                    src_ref=dst if src is None else src, dst_ref=dst,
                    send_sem=send_sems.at[base + k], recv_sem=recv_sems.at[base + k],
                    device_id=to, device_id_type=MESH)

            def rows(chip, pc, q=None, a=a, start=p * piece, half=half, piece=piece):
                if q is None:
                    return outs[a].at[chip, pl.ds(pc * half + start, piece), :]
                return outs[a].at[chip, pl.ds(pc * half + start + q * (piece // 2), piece // 2), :]

            own = ins[a].at[pl.ds(c * half + p * piece, piece), :]
            mine = rows(2 * x + y, c)
            res.append(dict(
                sends=[copy(0, mine, to_x, src=own), copy(1, mine, to_y, src=own)],
                from_x=copy(0, rows(xn, c), to_x), from_y=copy(1, rows(yn, c), to_y),
                relay_y=copy(2, rows(xn, c, 0), to_y), relay_x=copy(3, rows(yn, c, 1), to_x),
                from_y_relay=copy(2, rows(dg, c, 0), to_y), from_x_relay=copy(3, rows(dg, c, 1), to_x),
                pass_on=[copy(4, rows(xn, c), sibling), copy(5, rows(yn, c), sibling), copy(6, rows(dg, c), sibling)],
                from_sibling=[copy(4, rows(xn, 1 - c), sibling), copy(5, rows(yn, 1 - c), sibling),
                              copy(6, rows(dg, 1 - c), sibling)]))
    return res


def _ag_start(ins, outs, send_sems, recv_sems):
    for cps in _ag_copies(ins, outs, send_sems, recv_sems):
        for cp in cps["sends"]:
            cp.start()


def _ag_relay(ins, outs, send_sems, recv_sems):
    for cps in _ag_copies(ins, outs, send_sems, recv_sems):
        cps["from_x"].wait_recv()
        cps["relay_y"].start()
        cps["pass_on"][0].start()
        cps["from_y"].wait_recv()
        cps["relay_x"].start()
        cps["pass_on"][1].start()


def _ag_complete(ins, outs, send_sems, recv_sems):
    copies = _ag_copies(ins, outs, send_sems, recv_sems)
    for cps in copies:
        cps["from_y_relay"].wait_recv()
        cps["from_x_relay"].wait_recv()
        cps["pass_on"][2].start()
    for cps in copies:
        for cp in cps["from_sibling"]:
            cp.wait_recv()
        for cp in cps["sends"] + [cps["relay_y"], cps["relay_x"]] + cps["pass_on"]:
            cp.wait_send()


def _ag_finish(ins, outs, send_sems, recv_sems):
    _ag_relay(ins, outs, send_sems, recv_sems)
    _ag_complete(ins, outs, send_sems, recv_sems)


def _allgather_weights(name, collective_id, shards):
    na = len(shards)
    hbm = pltpu.MemorySpace.HBM
    ins = [jax.new_ref(s, memory_space=hbm) for s in shards]
    outs = [jax.empty_ref(jax.ShapeDtypeStruct((N_CHIP,) + s.shape, s.dtype), memory_space=hbm) for s in shards]

    @pl.kernel(mesh=plsc.ScalarSubcoreMesh(axis_name="sequencer", num_cores=1), name=name,
               scratch_types=(pltpu.SemaphoreType.DMA((AG_SEMS * AG_CHUNKS * na,)),
                              pltpu.SemaphoreType.DMA((AG_SEMS * AG_CHUNKS * na,))),
               compiler_params=pltpu.CompilerParams(collective_id=collective_id))
    def launch(send_sems, recv_sems):
        x, y, c, _ = _position()
        peers = [(1 - x, y, c), (x, 1 - y, c), (x, y, 1 - c)]
        barrier = pltpu.get_barrier_semaphore()
        for peer in peers:
            pl.semaphore_signal(barrier, inc=1, device_id=peer, device_id_type=MESH)
        pl.semaphore_wait(barrier, len(peers))
        _ag_start(ins, outs, send_sems, recv_sems)
        _ag_finish(ins, outs, send_sems, recv_sems)

    launch()
    return [o[...] for o in outs]


def _swap_copies(ins, outs, send_sems, recv_sems, split_rows):
    x, y, c, _ = _position()
    cps = []
    for a in range(len(ins)):
        src = ins[a]
        if split_rows:
            half = src.shape[1] // 2
            src = src.at[:, pl.ds((1 - c) * half, half), :]
        cps.append(pltpu.make_async_remote_copy(
            src_ref=src, dst_ref=outs[a], send_sem=send_sems.at[a], recv_sem=recv_sems.at[a],
            device_id=(x, y, 1 - c), device_id_type=MESH))
    return cps


def _swap_and_gather_seq(name, collective_id, swap_arrs, gather_arrs):
    ns, ng = len(swap_arrs), len(gather_arrs)
    hbm = pltpu.MemorySpace.HBM
    s_ins = [jax.new_ref(a, memory_space=hbm) for a in swap_arrs]
    s_outs = [jax.empty_ref(jax.ShapeDtypeStruct(a.shape, a.dtype), memory_space=hbm) for a in swap_arrs]
    g_ins = [jax.new_ref(a, memory_space=hbm) for a in gather_arrs]
    g_outs = [jax.empty_ref(jax.ShapeDtypeStruct((8 * a.shape[0], a.shape[1]), a.dtype), memory_space=hbm)
              for a in gather_arrs]

    @pl.kernel(mesh=plsc.ScalarSubcoreMesh(axis_name="sequencer", num_cores=1), name=name,
               scratch_types=(pltpu.SemaphoreType.DMA((ns,)), pltpu.SemaphoreType.DMA((ns,)),
                              pltpu.SemaphoreType.DMA((7 * ng,)), pltpu.SemaphoreType.DMA((7 * ng,)),
                              pltpu.SemaphoreType.DMA((ng,))),
               compiler_params=pltpu.CompilerParams(collective_id=collective_id))
    def launch(swap_send, swap_recv, send_sems, recv_sems, local_sems):
        x, y, c, chips = _position()
        peers = [(x, y, 1 - c)] + [(*chip, c) for chip in chips]
        barrier = pltpu.get_barrier_semaphore()
        for peer in peers:
            pl.semaphore_signal(barrier, inc=1, device_id=peer, device_id_type=MESH)
        pl.semaphore_wait(barrier, len(peers))
        for cp in _swap_copies(s_ins, s_outs, swap_send, swap_recv, False):
            cp.start()
        _ag8_run(g_ins, g_outs, send_sems, recv_sems, local_sems)
        for cp in _swap_copies(s_ins, s_outs, swap_send, swap_recv, False):
            cp.wait()

    launch()
    return [o[...] for o in s_outs], [o[...] for o in g_outs]


def _sibling_swap(name, arrs, split_rows, collective_id=None):
    na = len(arrs)
    shapes = [jax.ShapeDtypeStruct((a.shape[0], a.shape[1] // 2, a.shape[2]) if split_rows else a.shape, a.dtype)
              for a in arrs]

    def run(ins, outs, send_sems, recv_sems):
        for cp in _swap_copies(ins, outs, send_sems, recv_sems, split_rows):
            cp.start()
        for cp in _swap_copies(ins, outs, send_sems, recv_sems, split_rows):
            cp.wait()

    sems = (pltpu.SemaphoreType.DMA((na,)), pltpu.SemaphoreType.DMA((na,)))
    if collective_id is None:
        return pl.pallas_call(
            lambda *refs: run(refs[:na], refs[na:2 * na], *refs[2 * na:]), name=name, out_shape=shapes,
            in_specs=[ANY] * na, out_specs=[ANY] * na, scratch_shapes=list(sems))(*arrs)

    hbm = pltpu.MemorySpace.HBM
    ins = [jax.new_ref(a, memory_space=hbm) for a in arrs]
    outs = [jax.empty_ref(s, memory_space=hbm) for s in shapes]

    @pl.kernel(mesh=plsc.ScalarSubcoreMesh(axis_name="sequencer", num_cores=1), name=name, scratch_types=sems,
               compiler_params=pltpu.CompilerParams(collective_id=collective_id))
    def launch(send_sems, recv_sems):
        x, y, c, _ = _position()
        barrier = pltpu.get_barrier_semaphore()
        pl.semaphore_signal(barrier, inc=1, device_id=(x, y, 1 - c), device_id_type=MESH)
        pl.semaphore_wait(barrier, 1)
        run(ins, outs, send_sems, recv_sems)

    launch()
    return [o[...] for o in outs]


def _xchg_copies(ins, outs, send_sems, recv_sems):
    x, y, c, chips = _position()
    return [pltpu.make_async_remote_copy(
        src_ref=ins[a].at[2 * chip[0] + chip[1]], dst_ref=outs[a].at[j],
        send_sem=send_sems.at[3 * a + j], recv_sem=recv_sems.at[3 * a + j],
        device_id=(*chip, c), device_id_type=MESH) for a in range(len(ins)) for j, chip in enumerate(chips)]


def _exchange_chips(name, collective_id, parts):
    na = len(parts)
    hbm = pltpu.MemorySpace.HBM
    ins = [jax.new_ref(p, memory_space=hbm) for p in parts]
    outs = [jax.empty_ref(jax.ShapeDtypeStruct((3,) + p.shape[1:], p.dtype), memory_space=hbm) for p in parts]

    @pl.kernel(mesh=plsc.ScalarSubcoreMesh(axis_name="sequencer", num_cores=1), name=name,
               scratch_types=(pltpu.SemaphoreType.DMA((3 * na,)), pltpu.SemaphoreType.DMA((3 * na,))),
               compiler_params=pltpu.CompilerParams(collective_id=collective_id))
    def launch(send_sems, recv_sems):
        x, y, c, chips = _position()
        barrier = pltpu.get_barrier_semaphore()
        for chip in chips:
            pl.semaphore_signal(barrier, inc=1, device_id=(*chip, c), device_id_type=MESH)
        pl.semaphore_wait(barrier, len(chips))
        for cp in _xchg_copies(ins, outs, send_sems, recv_sems):
            cp.start()
        for cp in _xchg_copies(ins, outs, send_sems, recv_sems):
            cp.wait()

    launch()
    return [q[...] for q in outs]


def _add_sibling(name, grad, recv, core, after=()):
    _, r, c = grad.shape
    half = r // 2
    rb = half
    nrb = half // rb

    def body(core_ref, g_ref, r_ref, *refs):
        refs[-1][...] = (g_ref[...].astype(F32) + r_ref[...].astype(F32)).astype(BF16)

    return pl.pallas_call(
        body, name=name,
        grid_spec=pltpu.PrefetchScalarGridSpec(
            num_scalar_prefetch=1, grid=(N_CHIP, nrb),
            in_specs=[pl.BlockSpec((1, rb, c), lambda j, i, core_ref: (j, core_ref[0] * nrb + i, 0)),
                      pl.BlockSpec((1, rb, c), lambda j, i, core_ref: (j, i, 0))] + [ANY] * len(after),
            out_specs=pl.BlockSpec((1, rb, c), lambda j, i, core_ref: (j, i, 0))),
        out_shape=jax.ShapeDtypeStruct((N_CHIP, half, c), BF16),
        compiler_params=pltpu.CompilerParams(dimension_semantics=("arbitrary", "arbitrary"),
                                             vmem_limit_bytes=VMEM_LIMIT),
    )(core, grad, recv, *after)


def _add_chips(name, chip, p, q, after=()):
    _, half, c = q.shape
    rb = min(half, 256)

    def body(chip_ref, p_ref, q_ref, *refs):
        acc = p_ref[0].astype(F32)
        for j in range(3):
            acc = acc + q_ref[j].astype(F32)
        refs[-1][...] = acc

    return pl.pallas_call(
        body, name=name,
        grid_spec=pltpu.PrefetchScalarGridSpec(
            num_scalar_prefetch=1, grid=(half // rb,),
            in_specs=[pl.BlockSpec((1, rb, c), lambda i, chip_ref: (chip_ref[0], i, 0)),
                      pl.BlockSpec((3, rb, c), lambda i, chip_ref: (0, i, 0))] + [ANY] * len(after),
            out_specs=pl.BlockSpec((rb, c), lambda i, chip_ref: (i, 0))),
        out_shape=jax.ShapeDtypeStruct((half, c), F32),
        compiler_params=pltpu.CompilerParams(dimension_semantics=("arbitrary",), vmem_limit_bytes=VMEM_LIMIT),
    )(chip, p, q, *after)


def _add_chips_swap(name, chip, ps, qs, after=()):
    n = len(ps)

    def body(chip_ref, *refs):
        own, sib = refs[len(refs) - 2 * n - 2:len(refs) - n - 2], refs[len(refs) - n - 2:len(refs) - 2]
        send_sems, recv_sems = refs[-2:]
        for k in range(n):
            acc = refs[2 * k][0].astype(F32)
            for j in range(3):
                acc = acc + refs[2 * k + 1][j].astype(F32)
            own[k][...] = acc
        for cp in _swap_copies(own, sib, send_sems, recv_sems, False):
            cp.start()
        for cp in _swap_copies(own, sib, send_sems, recv_sems, False):
            cp.wait()

    in_specs, own_specs, shapes = [], [], []
    for q in qs:
        _, half, c = q.shape
        in_specs += [pl.BlockSpec((1, half, c), lambda i, chip_ref: (chip_ref[0], 0, 0)),
                     pl.BlockSpec((3, half, c), lambda i, chip_ref: (0, 0, 0))]
        own_specs.append(pl.BlockSpec((half, c), lambda i, chip_ref: (0, 0)))
        shapes.append(jax.ShapeDtypeStruct((half, c), F32))
    res = pl.pallas_call(
        body, name=name,
        grid_spec=pltpu.PrefetchScalarGridSpec(
            num_scalar_prefetch=1, grid=(1,), in_specs=in_specs + [ANY] * len(after),
            out_specs=own_specs + [ANY] * n,
            scratch_shapes=[pltpu.SemaphoreType.DMA((n,)), pltpu.SemaphoreType.DMA((n,))]),
        out_shape=shapes + shapes,
        compiler_params=pltpu.CompilerParams(dimension_semantics=("arbitrary",), vmem_limit_bytes=VMEM_LIMIT),
    )(chip, *[t for pair in zip(ps, qs) for t in pair], *after)
    return res[:n], res[n:]


def _small_update(gad, gam, gl, gg, mychip, params):
    names = ["ada_b", "norm1_g", "lru_conv_b", "gate_a_w", "gate_a_b", "gate_x_w", "gate_x_b", "a_param",
             "lru_conv_w", "short_conv_w", "lru_out_g", "conv_out_g", "norm2_g", "final_g"]
    flat = [t for n in names for t in params[n]]
    nin = len(flat)

    def body(chip_ref, gad_ref, gam_ref, gl_ref, gg_ref, *refs):
        ins = {n: refs[3 * k:3 * k + 3] for k, n in enumerate(names)}
        outs = {n: refs[nin + 4 * k:nin + 4 * k + 4] for k, n in enumerate(names)}
        loss_ref, dmod_ref = refs[nin + 4 * len(names):nin + 4 * len(names) + 2]

        def dsum(ref, lo, n):
            per = ref.shape[0] // 8
            acc = ref[lo:lo + n, :].astype(F32)
            for dev in range(1, 8):
                acc = acc + ref[dev * per + lo:dev * per + lo + n, :].astype(F32)
            return acc

        def update(n, g):
            w_ref, m_ref, v_ref = ins[n]
            g_ref, d_ref, mo_ref, vo_ref = outs[n]
            g = g.reshape(w_ref.shape)
            g_ref[...] = g
            d_ref[...], mo_ref[...], vo_ref[...] = _adam_math(w_ref[...], g, m_ref[...], v_ref[...])

        d, dm, l, lw, rows = refs[-5:]
        d[...] = dsum(gad_ref, 0, 8)
        dm[...] = dsum(gam_ref, 0, 8)
        l[...] = dsum(gl_ref, 0, 16)
        for dev in range(8):
            for k in range(3):
                dmod_ref[dev:dev + 1, k * D_MODEL:(k + 1) * D_MODEL] = gad_ref[dev * 8 + k:dev * 8 + k + 1, :]
                dmod_ref[dev:dev + 1, (3 + k) * D_MODEL:(4 + k) * D_MODEL] = gam_ref[dev * 8 + k:dev * 8 + k + 1, :]
        w_ref, m_ref, v_ref = ins["ada_b"]
        g_ref, d_ref, mo_ref, vo_ref = outs["ada_b"]
        for k in range(3):
            g_ref[:, k * D_MODEL:(k + 1) * D_MODEL] = d[k:k + 1, :]
            g_ref[:, (3 + k) * D_MODEL:(4 + k) * D_MODEL] = dm[k:k + 1, :]
        d_ref[...], mo_ref[...], vo_ref[...] = _adam_math(w_ref[...], g_ref[...], m_ref[...], v_ref[...])
        update("norm1_g", d[3:4, :])
        update("norm2_g", dm[3:4, :])
        update("final_g", dm[4:5, :])
        update("gate_a_b", d[4:5, 0:D_LRU])
        update("gate_x_b", d[4:5, D_LRU:2 * D_LRU])
        update("lru_conv_b", l[4:5, :])
        update("a_param", l[8:9, :] * jax.nn.sigmoid(ins["a_param"][0][...]))
        update("lru_out_g", l[9:10, :])
        update("conv_out_g", l[10:11, :])
        loss_ref[...] = dm[5:6, 0:1]
        chip = chip_ref[0]
        acc = jnp.zeros((8, 128), F32)
        for j in range(N_CHIP):
            acc = acc + jnp.where(chip == j, l[0:8, j * 128:(j + 1) * 128], 0.0)
        lw[...] = acc
        update("lru_conv_w", lw[0:4, :])
        for s, ref in enumerate(ins["short_conv_w"]):
            for k in range(3):
                rows[s, k:k + 1, :] = ref[k]
        g = lw[5:8, :]
        new = (g,) + tuple(_adam_math(rows[0, 0:3, :], g, rows[1, 0:3, :], rows[2, 0:3, :]))
        for o_ref, val in zip(outs["short_conv_w"], new):
            rows[3, 0:3, :] = val
            for k in range(3):
                o_ref[k] = rows[3, k:k + 1, :]
        gates = dsum(gg_ref, 0, D_LRU)
        update("gate_a_w", gates[:, 0:HEAD])
        update("gate_x_w", gates[:, HEAD:2 * HEAD])

    out_shape = []
    for n in names:
        out_shape += [jax.ShapeDtypeStruct(params[n][0].shape, F32)] * 4
    out_shape += [jax.ShapeDtypeStruct((1, 1), F32), jax.ShapeDtypeStruct((8, 6 * D_MODEL), F32)]
    res = pl.pallas_call(
        body, name="small_update", out_shape=out_shape,
        in_specs=[SMEM] + [VMEM] * (4 + nin),
        out_specs=[VMEM] * len(out_shape),
        scratch_shapes=[pltpu.VMEM((8, D_MODEL), F32), pltpu.VMEM((8, D_MODEL), F32), pltpu.VMEM((16, D_LRU), F32),
                        pltpu.VMEM((8, 128), F32), pltpu.VMEM((4, 8, 128), F32)],
        compiler_params=pltpu.CompilerParams(vmem_limit_bytes=VMEM_LIMIT),
    )(mychip, gad, gam, gl, gg, *flat)
    per = {n: res[4 * k:4 * k + 4] for k, n in enumerate(names)}
    return per, res[-2], res[-1]


def _block_diag(w):
    eye = jnp.eye(8, dtype=w.dtype)
    return (eye[:, None, :, None] * w[:, :, None, :]).reshape(8 * HEAD, 8 * HEAD)


def _diag_blocks(g):
    return jnp.concatenate([g[h * HEAD:(h + 1) * HEAD, h * HEAD:(h + 1) * HEAD] for h in range(8)], axis=0)


def kernel(x, c, ada_w, ada_b, norm1_g, w_in, lru_conv_w, lru_conv_b, gate_a_w, gate_a_b, gate_x_w, gate_x_b, a_param, short_conv_w, lru_out_g, conv_out_g, w_out, norm2_g, w_mlp1, w_mlp2, final_g, loss_target, m_ada_w, m_ada_b, m_norm1_g, m_w_in, m_lru_conv_w, m_lru_conv_b, m_gate_a_w, m_gate_a_b, m_gate_x_w, m_gate_x_b, m_a_param, m_short_conv_w, m_lru_out_g, m_conv_out_g, m_w_out, m_norm2_g, m_w_mlp1, m_w_mlp2, m_final_g, v_ada_w, v_ada_b, v_norm1_g, v_w_in, v_lru_conv_w, v_lru_conv_b, v_gate_a_w, v_gate_a_b, v_gate_x_w, v_gate_x_b, v_a_param, v_short_conv_w, v_lru_out_g, v_conv_out_g, v_w_out, v_norm2_g, v_w_mlp1, v_w_mlp2, v_final_g):
    xi, yi, ci = lax.axis_index("x"), lax.axis_index("y"), lax.axis_index("c")
    mychip = 2 * xi + yi
    me = 4 * xi + 2 * yi + ci

    own_in, own_out = w_in[0].astype(BF16), w_out[0].astype(BF16)
    win_all, wout_all = _allgather_weights("allgather_mixer_weights", 1, [own_in, own_out])
    own_w1, own_w2 = w_mlp1[0].astype(BF16), w_mlp2[0].astype(BF16)
    w1_all, w2_all = _allgather_weights("allgather_mlp_weights", 2, [own_w1, own_w2])

    c_blk = jnp.zeros((8, D_MODEL), F32).at[0:1].set(c)
    cw_blk = jnp.zeros((8, 128), F32).at[0:4].set(lru_conv_w[0]).at[4:7].set(short_conv_w[0])
    c_g, cw_g = _allgather8("allgather_cond", [c_blk, cw_blk])
    c_all = c_g.reshape(8, 8, D_MODEL)[:, 0]
    cw_g = cw_g.reshape(4, 2, 8, 128)[:, 0]
    lcw = cw_g[:, 0:4].transpose(1, 0, 2).reshape(4, D_LRU)
    scw = cw_g[:, 4:7].transpose(1, 0, 2).reshape(3, D_LRU)

    mod_loc = _mod_matmul(c_all, ada_w[0])
    (mod_g,) = _allgather8("allgather_mod", [mod_loc])
    mod_all = mod_g.reshape(4, 2, 8, 6 * D_MODEL // 4)[:, 0].transpose(1, 0, 2).reshape(8, 6 * D_MODEL) + ada_b
    mod_pad = jnp.pad(mod_all.reshape(8, 6, D_MODEL), ((0, 0), (0, 2), (0, 0)))
    mod = lax.dynamic_slice_in_dim(mod_pad, me, 1, axis=0).reshape(8, D_MODEL)

    win, wout = (win_all, own_in), (wout_all, own_out)
    chip = mychip.reshape(1).astype(jnp.int32)
    core = ci.reshape(1).astype(jnp.int32)

    vecd = jnp.concatenate([norm1_g, norm2_g, final_g[None, :], jnp.concatenate([gate_a_b, gate_x_b], axis=1),
                            jnp.zeros((4, D_MODEL), F32)], axis=0)
    vecl = jnp.concatenate([lcw, lru_conv_b, scw, a_param, lru_out_g, conv_out_g, jnp.zeros((5, D_LRU), F32)], axis=0)
    gab = jnp.concatenate([_block_diag(gate_a_w[0]), _block_diag(gate_x_w[0])], axis=1).astype(BF16)
    a64 = _block_diag(jnp.full((8, HEAD, HEAD), 1.0 / HEAD, F32)).astype(BF16)

    hb, proj, hl, ycat, mixed, x1 = _mix_fwd(chip, x[0], mod, vecd, vecl, win, wout, gab, a64)
    dx1, act, dz, dmo, h2b, accm = _mlp_fwd_bwd(
        chip, x1, loss_target[0], mod, vecd, (w1_all, own_w1), (w2_all, own_w2))

    parts_mlp = list(_wgrad_mlp(6, h2b, dz, act, dmo))
    q_w1, q_w2 = _exchange_chips("rs_exchange_mlp", 0, parts_mlp)
    grad_x, accd, accl, g_win, g_wout, g_gate = _mix_bwd(
        chip, dx1, x[0], mixed, proj, hl, hb, ycat, mod, vecd, vecl, win, wout, gab, a64)

    g_mix = [g_win, g_wout.reshape(N_CHIP, WOUT_BLK, D_MODEL)]
    recv_mix = _sibling_swap("rs_swap_halves_mix", g_mix, True, collective_id=4)
    own_mlp = [_add_chips("rs_add_chips_mlp%d" % k, chip, p, q) for k, (p, q) in enumerate(zip(parts_mlp, (q_w1, q_w2)))]
    gg_blk = jnp.concatenate([_diag_blocks(g_gate[:, 0:D_LRU]), _diag_blocks(g_gate[:, D_LRU:2 * D_LRU])], axis=1)
    sib_mlp, (gad, gam, gl, gg) = _swap_and_gather_seq(
        "rs_swap_reduced_mlp_allgather_small_grads", 5, own_mlp, [accd, accm, accl, gg_blk.astype(BF16)])

    parts_mix = [_add_sibling("rs_add_sibling_mix%d" % k, g, r, core)
                 for k, (g, r) in enumerate(zip(g_mix, recv_mix))]
    landed_mix = _exchange_chips("rs_exchange_mix", 3, parts_mix)
    res_w1, res_w2 = _adam("adam_mlp", core, [(w_mlp1[0], own_mlp[0], sib_mlp[0], m_w_mlp1[0], v_w_mlp1[0]),
                                              (w_mlp2[0], own_mlp[1], sib_mlp[1], m_w_mlp2[0], v_w_mlp2[0])])

    params = {
        "ada_b": (ada_b, m_ada_b, v_ada_b), "norm1_g": (norm1_g, m_norm1_g, v_norm1_g),
        "lru_conv_b": (lru_conv_b, m_lru_conv_b, v_lru_conv_b),
        "gate_a_w": tuple(t.reshape(D_LRU, HEAD) for t in (gate_a_w, m_gate_a_w, v_gate_a_w)),
        "gate_a_b": (gate_a_b, m_gate_a_b, v_gate_a_b),
        "gate_x_w": tuple(t.reshape(D_LRU, HEAD) for t in (gate_x_w, m_gate_x_w, v_gate_x_w)),
        "gate_x_b": (gate_x_b, m_gate_x_b, v_gate_x_b), "a_param": (a_param, m_a_param, v_a_param),
        "lru_conv_w": (lru_conv_w, m_lru_conv_w, v_lru_conv_w),
        "short_conv_w": tuple(t.reshape(3, 1, D_LRU // N_CHIP) for t in (short_conv_w, m_short_conv_w, v_short_conv_w)),
        "lru_out_g": (lru_out_g, m_lru_out_g, v_lru_out_g), "conv_out_g": (conv_out_g, m_conv_out_g, v_conv_out_g),
        "norm2_g": (norm2_g, m_norm2_g, v_norm2_g),
        "final_g": tuple(t[None, :] for t in (final_g, m_final_g, v_final_g)),
    }
    small, loss_blk, dmod_cols = _small_update(gad, gam, gl, gg, chip, params)
    loss = loss_blk.reshape(())

    sct = (c_all * jax.nn.sigmoid(c_all)).T
    ada = _ada_grad_adam(chip, sct, dmod_cols, ada_w[0], m_ada_w[0], v_ada_w[0])

    own_mix, sib_mix = _add_chips_swap("rs_add_chips_swap_mix", chip, parts_mix, landed_mix, after=[res_w1[1], ada[1]])
    (res_win,) = _adam("adam_w_in", core, [(w_in[0], own_mix[0], sib_mix[0], m_w_in[0], v_w_in[0])])
    (res_wout,) = _adam("adam_w_out", core, [(w_out[0], own_mix[1], sib_mix[1], m_w_out[0], v_w_out[0])])

    res = {"ada_w": ada, "w_in": res_win, "w_out": res_wout, "w_mlp1": res_w1, "w_mlp2": res_w2}
    res = {n: tuple(t[None] for t in r) for n, r in res.items()}
    shapes = {"gate_a_w": gate_a_w.shape, "gate_x_w": gate_x_w.shape, "lru_conv_w": lru_conv_w.shape,
              "short_conv_w": short_conv_w.shape, "final_g": final_g.shape}
    for n, t in small.items():
        res[n] = tuple(u.reshape(shapes[n]) if n in shapes else u for u in t)

    order = ["ada_w", "ada_b", "norm1_g", "w_in", "lru_conv_w", "lru_conv_b", "gate_a_w", "gate_a_b", "gate_x_w",
             "gate_x_b", "a_param", "short_conv_w", "lru_out_g", "conv_out_g", "w_out", "norm2_g", "w_mlp1",
             "w_mlp2", "final_g"]
    return (loss, grad_x[None], *[res[n][0] for n in order], *[res[n][1] for n in order],
            *[res[n][2] for n in order], *[res[n][3] for n in order])
```

```python
import jax
import jax.numpy as jnp
from jax import lax
from jax.experimental import pallas as pl
from jax.experimental.pallas import tpu as pltpu
from jax.experimental.pallas import tpu_sc as plsc

F32 = jnp.float32
BF16 = jnp.bfloat16

D_MODEL = 1024
D_LRU = 512
D_IN = 2560
D_FF = 4096
N_CHIP = 4
WIN_BLK = D_IN // N_CHIP
WOUT_BLK = D_MODEL // N_CHIP
FF_BLK = D_FF // N_CHIP
HEAD = 64
EPS = 1e-6
C_GATE = 8.0
TOKEN_TILE = 256
MIX_FWD_TILE = 512
HALO = 8
VMEM_LIMIT = 60 * 1024 * 1024

ADAM_LR = 0.001
ADAM_B1 = 0.9
ADAM_B2 = 0.999
ADAM_EPS = 1e-08
ADAM_WD = 0.01
ADAM_STEP = 10

MESH = pl.DeviceIdType.MESH
ANY = pl.BlockSpec(memory_space=pl.ANY)
VMEM = pl.BlockSpec(memory_space=pltpu.VMEM)
SMEM = pl.BlockSpec(memory_space=pltpu.SMEM)


def _full(shape, single=False):
    nd = len(shape)
    if single:
        return pl.BlockSpec(shape, lambda *_: (0,) * nd, pipeline_mode=pl.Buffered(1))
    return pl.BlockSpec(shape, lambda *_: (0,) * nd)


def _dot(a, b):
    return jnp.dot(a, b, preferred_element_type=F32)


def _dot_nt(a, b):
    return lax.dot_general(a, b, (((1,), (1,)), ((), ())), preferred_element_type=F32)


def _dot_tn(a, b):
    return lax.dot_general(a, b, (((0,), (0,)), ((), ())), preferred_element_type=F32)


def _gmean(v, a64):
    hi = v.astype(BF16)
    lo = (v - hi.astype(F32)).astype(BF16)
    return _dot(hi, a64) + _dot(lo, a64)


def _gelu(x):
    u = 0.7978845608028654 * (x + 0.044715 * x * x * x)
    t = jnp.tanh(u)
    return 0.5 * x * (1.0 + t), t


def _gelu_grad(x, t):
    du = 0.7978845608028654 * (1.0 + 3.0 * 0.044715 * x * x)
    return 0.5 * (1.0 + t) + 0.5 * x * (1.0 - t * t) * du


def _log1p_pos(y):
    return jnp.where(y < 1e-2, y * (1.0 - y * (0.5 - y * (1.0 / 3.0 - y * 0.25))), jnp.log(1.0 + y))


def _softplus(a):
    return jnp.maximum(a, 0.0) + _log1p_pos(jnp.exp(-jnp.abs(a)))


def _neg_expm1(z):
    series = -z * (1.0 + z * (0.5 + z * (1.0 / 6.0 + z * (1.0 / 24.0))))
    return jnp.where(z > -0.02, series, 1.0 - jnp.exp(z))


def _scan_fwd(a, b, row):
    n = a.shape[0]
    d = 1
    while d < n:
        m = row >= d
        b = jnp.where(m, a * pltpu.roll(b, d, 0) + b, b)
        a = jnp.where(m, a * pltpu.roll(a, d, 0), a)
        d *= 2
    return a, b


def _scan_rev(a, b, row):
    n = a.shape[0]
    d = 1
    while d < n:
        m = row < n - d
        b = jnp.where(m, b + a * pltpu.roll(b, n - d, 0), b)
        a = jnp.where(m, a * pltpu.roll(a, n - d, 0), a)
        d *= 2
    return a, b


def _colsum(v):
    return jnp.sum(v, axis=0, keepdims=True)


def _load_gathered(chip, gathered, own, slot, sems):
    copies = []
    for j in range(N_CHIP):
        @pl.when(chip == j)
        def _(j=j):
            pltpu.make_async_copy(own, slot(j), sems.at[j]).start()

        @pl.when(chip != j)
        def _(j=j):
            pltpu.make_async_copy(gathered.at[j], slot(j), sems.at[j]).start()

        copies.append(pltpu.make_async_copy(own, slot(j), sems.at[j]))
    return copies


def _lru_gates(xlb, gab, gbias, sp, first_row):
    g = _dot(xlb, gab) + gbias
    r = jax.nn.sigmoid(g[:, :D_LRU])
    ig = jax.nn.sigmoid(g[:, D_LRU:])
    la = (-C_GATE) * r * sp
    a = jnp.exp(la)
    msq = jnp.sqrt(_neg_expm1(2.0 * la))
    mult = jnp.where(first_row, 1.0, msq)
    return r, ig, a, msq, mult


def _mix_fwd(chip, x, mod, vecd, vecl, win, wout, gab, a64):
    s = x.shape[0]
    ts = MIX_FWD_TILE
    nt = s // ts

    def body(chip_ref, x_ref, mod_ref, vd_ref, vl_ref, win_hbm, win_own, wout_hbm, wout_own, gab_ref, a64_ref,
             hb_ref, proj_ref, hl_ref, ycat_ref, mixed_ref, x1_ref,
             win_ref, wout_ref, ext_lx, ext_cv, hcar, sems):
        i = pl.program_id(0)

        @pl.when(i == 0)
        def _():
            cps = _load_gathered(chip_ref[0], win_hbm, win_own, lambda j: win_ref.at[j], sems.at[pl.ds(0, N_CHIP)])
            cps += _load_gathered(chip_ref[0], wout_hbm, wout_own,
                                  lambda j: wout_ref.at[pl.ds(j * WOUT_BLK, WOUT_BLK), :],
                                  sems.at[pl.ds(N_CHIP, N_CHIP)])
            ext_lx[0:HALO, :] = jnp.zeros((HALO, D_LRU), F32)
            ext_cv[0:HALO, :] = jnp.zeros((HALO, D_LRU), F32)
            hcar[...] = jnp.zeros_like(hcar)
            for cp in cps:
                cp.wait()

        row = lax.broadcasted_iota(jnp.int32, (ts, D_LRU), 0)
        first_row = jnp.logical_and(row == 0, i == 0)
        xt = x_ref[...]
        shift1, scale1, gate1 = mod_ref[0:1, :], mod_ref[1:2, :], mod_ref[2:3, :]
        r1 = lax.rsqrt(jnp.mean(xt * xt, axis=-1, keepdims=True) + EPS)
        h = (xt * r1) * vd_ref[0:1, :] * (1.0 + scale1) + shift1
        hb = h.astype(BF16)
        hb_ref[...] = hb
        for j in range(N_CHIP):
            proj_ref[:, j * WIN_BLK:(j + 1) * WIN_BLK] = _dot(hb, win_ref[j])
        u_ly = proj_ref[:, 512:1024]
        u_b = proj_ref[:, 1024:1536]

        ext_lx[HALO:HALO + ts, :] = proj_ref[:, 0:512]
        xl = vl_ref[4:5, :] + vl_ref[0:1, :] * ext_lx[pl.ds(5, ts), :]
        for k in range(1, 4):
            xl = xl + vl_ref[k:k + 1, :] * ext_lx[pl.ds(5 + k, ts), :]
        ext_lx[0:HALO, :] = ext_lx[ts:ts + HALO, :]
        sp = _softplus(vl_ref[8:9, :])
        _, ig, a, _, mult = _lru_gates(xl.astype(BF16), gab_ref[...], vd_ref[3:4, :], sp, first_row)
        acum, hloc = _scan_fwd(a, mult * (ig * xl), row)
        hl = hloc + acum * hcar[0:1, :]
        hl_ref[...] = hl
        hcar[0:1, :] = hl_ref[ts - 1:ts, :]
        ge, _ = _gelu(u_ly)
        p = ge * hl
        y_lru = p * lax.rsqrt(_gmean(p * p, a64_ref[...]) + EPS) * vl_ref[9:10, :]
        ycat_ref[:, 0:512] = y_lru.astype(BF16)

        ext_cv[HALO:HALO + ts, :] = proj_ref[:, 1536:2048] * proj_ref[:, 2048:2560]
        q = vl_ref[5:6, :] * ext_cv[pl.ds(6, ts), :]
        for k in range(1, 3):
            q = q + vl_ref[5 + k:6 + k, :] * ext_cv[pl.ds(6 + k, ts), :]
        ext_cv[0:HALO, :] = ext_cv[ts:ts + HALO, :]
        yc = u_b * q
        y_conv = yc * lax.rsqrt(_gmean(yc * yc, a64_ref[...]) + EPS) * vl_ref[10:11, :]
        ycat_ref[:, 512:1024] = y_conv.astype(BF16)

        mixed = _dot(ycat_ref[...], wout_ref[...])
        mixed_ref[...] = mixed
        x1_ref[...] = xt + gate1 * mixed

    tile = lambda w: pl.BlockSpec((ts, w), lambda i: (i, 0))
    return pl.pallas_call(
        body, name="mix_fwd", grid=(nt,),
        in_specs=[SMEM, tile(D_MODEL), _full((8, D_MODEL)), _full((8, D_MODEL)), _full((16, D_LRU)),
                  ANY, ANY, ANY, ANY, _full((D_LRU, 2 * D_LRU), True), _full((D_LRU, D_LRU), True)],
        out_specs=[tile(D_MODEL), tile(D_IN), tile(D_LRU), tile(D_MODEL), tile(D_MODEL), tile(D_MODEL)],
        out_shape=[jax.ShapeDtypeStruct((s, D_MODEL), BF16), jax.ShapeDtypeStruct((s, D_IN), F32),
                   jax.ShapeDtypeStruct((s, D_LRU), F32), jax.ShapeDtypeStruct((s, D_MODEL), BF16),
                   jax.ShapeDtypeStruct((s, D_MODEL), F32), jax.ShapeDtypeStruct((s, D_MODEL), F32)],
        scratch_shapes=[pltpu.VMEM((N_CHIP, D_MODEL, WIN_BLK), BF16), pltpu.VMEM((D_MODEL, D_MODEL), BF16),
                        pltpu.VMEM((ts + HALO, D_LRU), F32), pltpu.VMEM((ts + HALO, D_LRU), F32),
                        pltpu.VMEM((HALO, D_LRU), F32), pltpu.SemaphoreType.DMA((2 * N_CHIP,))],
        compiler_params=pltpu.CompilerParams(dimension_semantics=("arbitrary",), vmem_limit_bytes=VMEM_LIMIT),
    )(chip, x, mod, vecd, vecl, *win, *wout, gab, a64)


def _mlp_fwd_bwd(chip, x1, target, mod, vecd, w1, w2):
    s = x1.shape[0]
    ts = TOKEN_TILE
    nt = s // ts

    def body(chip_ref, x1_ref, tg_ref, mod_ref, vd_ref, w1_hbm, w1_own, w2_hbm, w2_own,
             dx1_ref, act_ref, dz_ref, dmo_ref, h2_ref, acc_ref, w1_v, w2_v, rz_v, sems):
        i = pl.program_id(0)

        @pl.when(i == 0)
        def _():
            cps = _load_gathered(chip_ref[0], w1_hbm, w1_own, lambda j: w1_v.at[j], sems.at[pl.ds(0, N_CHIP)])
            cps += _load_gathered(chip_ref[0], w2_hbm, w2_own, lambda j: w2_v.at[j], sems.at[pl.ds(N_CHIP, N_CHIP)])
            acc_ref[...] = jnp.zeros_like(acc_ref)
            for cp in cps:
                cp.wait()

        xt = x1_ref[...]
        shift2, scale2, gate2 = mod_ref[3:4, :], mod_ref[4:5, :], mod_ref[5:6, :]
        g2, gf = vd_ref[1:2, :], vd_ref[2:3, :]
        r2 = lax.rsqrt(jnp.mean(xt * xt, axis=-1, keepdims=True) + EPS)
        n2 = xt * r2
        h2b = (n2 * g2 * (1.0 + scale2) + shift2).astype(BF16)
        h2_ref[...] = h2b
        for j in range(N_CHIP):
            rz_v[j] = jnp.maximum(_dot(h2b, w1_v[j]), 0.0)
        mo = jnp.zeros((ts, D_MODEL), F32)
        for j in range(N_CHIP):
            rz = rz_v[j]
            actb = (rz * rz).astype(BF16)
            act_ref[:, j * FF_BLK:(j + 1) * FF_BLK] = actb
            mo = mo + _dot(actb, w2_v[j])
        x2 = xt + gate2 * mo
        r3 = lax.rsqrt(jnp.mean(x2 * x2, axis=-1, keepdims=True) + EPS)
        n3 = x2 * r3
        e = n3 * gf - tg_ref[...]
        loss = (0.5 / D_MODEL) * jnp.sum(_colsum(e * e), axis=1, keepdims=True)
        dy = e * (1.0 / D_MODEL)
        acc_ref[4:5, :] += _colsum(dy * n3)
        acc_ref[5:6, :] += jnp.broadcast_to(loss, (1, D_MODEL))
        dn3 = dy * gf
        dx2 = r3 * (dn3 - n3 * jnp.mean(dn3 * n3, axis=-1, keepdims=True))
        acc_ref[2:3, :] += _colsum(dx2 * mo)
        dmob = (dx2 * gate2).astype(BF16)
        dmo_ref[...] = dmob
        for j in range(N_CHIP):
            dz_ref[:, j * FF_BLK:(j + 1) * FF_BLK] = (_dot_nt(dmob, w2_v[j]) * (2.0 * rz_v[j])).astype(BF16)
        dh2 = jnp.zeros((ts, D_MODEL), F32)
        for j in range(N_CHIP):
            dh2 = dh2 + _dot_nt(dz_ref[:, j * FF_BLK:(j + 1) * FF_BLK], w1_v[j])
        acc_ref[1:2, :] += _colsum(dh2 * (n2 * g2))
        acc_ref[0:1, :] += _colsum(dh2)
        dhn2 = dh2 * (1.0 + scale2)
        acc_ref[3:4, :] += _colsum(dhn2 * n2)
        dn2 = dhn2 * g2
        dx1_ref[...] = dx2 + r2 * (dn2 - n2 * jnp.mean(dn2 * n2, axis=-1, keepdims=True))

    tile = lambda w: pl.BlockSpec((ts, w), lambda i: (i, 0))
    return pl.pallas_call(
        body, name="mlp_fwd_bwd", grid=(nt,),
        in_specs=[SMEM, tile(D_MODEL), tile(D_MODEL), _full((8, D_MODEL)), _full((8, D_MODEL)), ANY, ANY, ANY, ANY],
        out_specs=[tile(D_MODEL), tile(D_FF), tile(D_FF), tile(D_MODEL), tile(D_MODEL), _full((8, D_MODEL))],
        out_shape=[jax.ShapeDtypeStruct((s, D_MODEL), F32), jax.ShapeDtypeStruct((s, D_FF), BF16),
                   jax.ShapeDtypeStruct((s, D_FF), BF16), jax.ShapeDtypeStruct((s, D_MODEL), BF16),
                   jax.ShapeDtypeStruct((s, D_MODEL), BF16), jax.ShapeDtypeStruct((8, D_MODEL), F32)],
        scratch_shapes=[pltpu.VMEM((N_CHIP, D_MODEL, FF_BLK), BF16), pltpu.VMEM((N_CHIP, FF_BLK, D_MODEL), BF16),
                        pltpu.VMEM((N_CHIP, ts, FF_BLK), F32), pltpu.SemaphoreType.DMA((2 * N_CHIP,))],
        compiler_params=pltpu.CompilerParams(dimension_semantics=("arbitrary",), vmem_limit_bytes=VMEM_LIMIT),
    )(chip, x1, target, mod, vecd, *w1, *w2)


def _mix_bwd(chip, dx1, x, mixed, proj, hl, hb, ycat, mod, vecd, vecl, win, wout, gab, a64):
    s = x.shape[0]
    ts = TOKEN_TILE
    nt = s // ts
    hpt = ts // HALO

    def body(chip_ref, dx1_ref, x_ref, mixed_ref, proj_ref, projh_ref, hl_ref, hlh_ref, hb_ref, ycat_ref,
             mod_ref, vd_ref, vl_ref, win_hbm, win_own, wout_hbm, wout_own, gab_ref, a64_ref,
             gx_ref, accd_ref, accl_ref, gwin_hbm, gwout_hbm, ggate_hbm,
             win_ref, wout_ref, dproj_ref, dgb_ref, gwin_acc, gwout_acc, ggate_acc,
             ext_lx, ext_cv, ext_hl, ext_dxl, ext_dq, gbuf, gcar, acar, sems):
        i = pl.program_id(0)
        ri = nt - 1 - i

        @pl.when(i == 0)
        def _():
            gwin_acc[...] = jnp.zeros_like(gwin_acc)
            gwout_acc[...] = jnp.zeros_like(gwout_acc)
            ggate_acc[...] = jnp.zeros_like(ggate_acc)
            cps = _load_gathered(chip_ref[0], win_hbm, win_own, lambda j: win_ref.at[j], sems.at[pl.ds(0, N_CHIP)])
            cps += _load_gathered(chip_ref[0], wout_hbm, wout_own,
                                  lambda j: wout_ref.at[pl.ds(j * WOUT_BLK, WOUT_BLK), :],
                                  sems.at[pl.ds(N_CHIP, N_CHIP)])
            for cp in cps:
                cp.wait()
            accd_ref[...] = jnp.zeros_like(accd_ref)
            accl_ref[...] = jnp.zeros_like(accl_ref)
            ext_dxl[ts:ts + HALO, :] = jnp.zeros((HALO, D_LRU), F32)
            ext_dq[ts:ts + HALO, :] = jnp.zeros((HALO, D_LRU), F32)
            gcar[...] = jnp.zeros_like(gcar)
            acar[...] = jnp.zeros_like(acar)

        row = lax.broadcasted_iota(jnp.int32, (ts, D_LRU), 0)
        first_row = jnp.logical_and(row == 0, ri == 0)
        halo_on = jnp.where(ri == 0, 0.0, 1.0)
        shift1, scale1, gate1 = mod_ref[0:1, :], mod_ref[1:2, :], mod_ref[2:3, :]
        g1 = vd_ref[0:1, :]
        a64m = a64_ref[...]
        lg, cg = vl_ref[9:10, :], vl_ref[10:11, :]

        dx1 = dx1_ref[...]
        accd_ref[2:3, :] += _colsum(dx1 * mixed_ref[...])
        dmb = (dx1 * gate1).astype(BF16)
        gwout_acc[...] += _dot_tn(ycat_ref[...], dmb)
        dycat = _dot_nt(dmb, wout_ref[...])
        dyl = dycat[:, 0:512]
        dyv = dycat[:, 512:1024]

        u_ly = proj_ref[:, 512:1024]
        u_b = proj_ref[:, 1024:1536]
        u_c = proj_ref[:, 1536:2048]
        u_v = proj_ref[:, 2048:2560]
        ext_lx[0:HALO, :] = projh_ref[:, 0:512] * halo_on
        ext_lx[HALO:HALO + ts, :] = proj_ref[:, 0:512]
        xl = vl_ref[4:5, :] + vl_ref[0:1, :] * ext_lx[pl.ds(5, ts), :]
        for k in range(1, 4):
            xl = xl + vl_ref[k:k + 1, :] * ext_lx[pl.ds(5 + k, ts), :]
        xlb = xl.astype(BF16)
        sp = _softplus(vl_ref[8:9, :])
        r, ig, a, msq, mult = _lru_gates(xlb, gab_ref[...], vd_ref[3:4, :], sp, first_row)
        hl = hl_ref[...]
        ge, th = _gelu(u_ly)
        p = ge * hl
        rl = lax.rsqrt(_gmean(p * p, a64m) + EPS)
        nl = p * rl
        ext_cv[0:HALO, :] = projh_ref[:, 1536:2048] * projh_ref[:, 2048:2560] * halo_on
        ext_cv[HALO:HALO + ts, :] = u_c * u_v
        q = vl_ref[5:6, :] * ext_cv[pl.ds(6, ts), :]
        for k in range(1, 3):
            q = q + vl_ref[5 + k:6 + k, :] * ext_cv[pl.ds(6 + k, ts), :]
        yc = u_b * q
        rc = lax.rsqrt(_gmean(yc * yc, a64m) + EPS)
        nc = yc * rc

        accl_ref[9:10, :] += _colsum(dyl * nl)
        dnl = dyl * lg
        dp = rl * (dnl - nl * _gmean(dnl * nl, a64m))
        dproj_ref[:, 512:1024] = ((dp * hl) * _gelu_grad(u_ly, th)).astype(BF16)
        a_next = jnp.where(row == ts - 1, acar[0:1, :], pltpu.roll(a, ts - 1, 0))
        acum, gloc = _scan_rev(a_next, dp * ge, row)
        gbuf[...] = gloc + acum * gcar[0:1, :]
        gcar[0:1, :] = gbuf[0:1, :]
        ext_hl[0:HALO, :] = hlh_ref[...] * halo_on
        ext_hl[HALO:HALO + ts, :] = hl
        acar[...] = a[0:HALO, :]
        gt = gbuf[...]
        da = gt * ext_hl[pl.ds(HALO - 1, ts), :]
        dmult = gt * ig * xl
        di = gt * mult * xl
        dxl = gt * mult * ig
        dla = da * a - jnp.where(first_row, 0.0, dmult * a * a / msq)
        accl_ref[8:9, :] += _colsum(dla * ((-C_GATE) * r))
        dra = dla * ((-C_GATE) * sp) * r * (1.0 - r)
        dia = di * ig * (1.0 - ig)
        accd_ref[4:5, 0:D_LRU] += _colsum(dra)
        accd_ref[4:5, D_LRU:2 * D_LRU] += _colsum(dia)
        dgb_ref[:, 0:D_LRU] = dra.astype(BF16)
        dgb_ref[:, D_LRU:2 * D_LRU] = dia.astype(BF16)
        dxl = dxl + _dot_nt(dgb_ref[...], gab_ref[...])
        ggate_acc[...] += _dot_tn(xlb, dgb_ref[...])
        accl_ref[4:5, :] += _colsum(dxl)
        for k in range(4):
            accl_ref[k:k + 1, :] += _colsum(dxl * ext_lx[pl.ds(5 + k, ts), :])
        ext_dxl[0:ts, :] = dxl
        du_lx = vl_ref[0:1, :] * ext_dxl[pl.ds(3, ts), :]
        for k in range(1, 4):
            du_lx = du_lx + vl_ref[k:k + 1, :] * ext_dxl[pl.ds(3 - k, ts), :]
        ext_dxl[ts:ts + HALO, :] = ext_dxl[0:HALO, :]
        dproj_ref[:, 0:512] = du_lx.astype(BF16)

        accl_ref[10:11, :] += _colsum(dyv * nc)
        dnc = dyv * cg
        dyc = rc * (dnc - nc * _gmean(dnc * nc, a64m))
        dproj_ref[:, 1024:1536] = (dyc * q).astype(BF16)
        dq = dyc * u_b
        for k in range(3):
            accl_ref[5 + k:6 + k, :] += _colsum(dq * ext_cv[pl.ds(6 + k, ts), :])
        ext_dq[0:ts, :] = dq
        dcv = vl_ref[5:6, :] * ext_dq[pl.ds(2, ts), :]
        for k in range(1, 3):
            dcv = dcv + vl_ref[5 + k:6 + k, :] * ext_dq[pl.ds(2 - k, ts), :]
        ext_dq[ts:ts + HALO, :] = ext_dq[0:HALO, :]
        dproj_ref[:, 1536:2048] = (dcv * u_v).astype(BF16)
        dproj_ref[:, 2048:2560] = (dcv * u_c).astype(BF16)

        dh = _dot_nt(dproj_ref[:, 0:WIN_BLK], win_ref[0])
        for j in range(1, N_CHIP):
            dh = dh + _dot_nt(dproj_ref[:, j * WIN_BLK:(j + 1) * WIN_BLK], win_ref[j])
        for j in range(N_CHIP):
            gwin_acc[j] += _dot_tn(hb_ref[...], dproj_ref[:, j * WIN_BLK:(j + 1) * WIN_BLK])
        xt = x_ref[...]
        r1 = lax.rsqrt(jnp.mean(xt * xt, axis=-1, keepdims=True) + EPS)
        n1 = xt * r1
        accd_ref[1:2, :] += _colsum(dh * (n1 * g1))
        accd_ref[0:1, :] += _colsum(dh)
        dhn1 = dh * (1.0 + scale1)
        accd_ref[3:4, :] += _colsum(dhn1 * n1)
        dn1 = dhn1 * g1
        gx_ref[...] = dx1 + r1 * (dn1 - n1 * jnp.mean(dn1 * n1, axis=-1, keepdims=True))

        @pl.when(i == nt - 1)
        def _():
            outs = [pltpu.make_async_copy(acc, dst, sems.at[k]) for k, (acc, dst) in enumerate(
                ((gwin_acc, gwin_hbm), (gwout_acc, gwout_hbm), (ggate_acc, ggate_hbm)))]
            for cp in outs:
                cp.start()
            for cp in outs:
                cp.wait()

    tile = lambda w: pl.BlockSpec((ts, w), lambda i: (nt - 1 - i, 0))
    halo = lambda w: pl.BlockSpec((HALO, w), lambda i: (jnp.maximum((nt - 1 - i) * hpt - 1, 0), 0))
    ext = pltpu.VMEM((ts + HALO, D_LRU), F32)
    return pl.pallas_call(
        body, name="mix_bwd", grid=(nt,),
        in_specs=[SMEM, tile(D_MODEL), tile(D_MODEL), tile(D_MODEL), tile(D_IN), halo(D_IN), tile(D_LRU), halo(D_LRU),
                  tile(D_MODEL), tile(D_MODEL), _full((8, D_MODEL)), _full((8, D_MODEL)), _full((16, D_LRU)),
                  ANY, ANY, ANY, ANY, _full((D_LRU, 2 * D_LRU), True), _full((D_LRU, D_LRU), True)],
        out_specs=[tile(D_MODEL), _full((8, D_MODEL)), _full((16, D_LRU)), ANY, ANY, ANY],
        out_shape=[jax.ShapeDtypeStruct((s, D_MODEL), F32),
                   jax.ShapeDtypeStruct((8, D_MODEL), F32), jax.ShapeDtypeStruct((16, D_LRU), F32),
                   jax.ShapeDtypeStruct((N_CHIP, D_MODEL, WIN_BLK), F32), jax.ShapeDtypeStruct((D_MODEL, D_MODEL), F32),
                   jax.ShapeDtypeStruct((D_LRU, 2 * D_LRU), F32)],
        scratch_shapes=[pltpu.VMEM((N_CHIP, D_MODEL, WIN_BLK), BF16), pltpu.VMEM((D_MODEL, D_MODEL), BF16),
                        pltpu.VMEM((ts, D_IN), BF16), pltpu.VMEM((ts, 2 * D_LRU), BF16),
                        pltpu.VMEM((N_CHIP, D_MODEL, WIN_BLK), F32), pltpu.VMEM((D_MODEL, D_MODEL), F32),
                        pltpu.VMEM((D_LRU, 2 * D_LRU), F32),
                        ext, ext, ext, ext, ext, pltpu.VMEM((ts, D_LRU), F32),
                        pltpu.VMEM((HALO, D_LRU), F32), pltpu.VMEM((HALO, D_LRU), F32),
                        pltpu.SemaphoreType.DMA((2 * N_CHIP,))],
        compiler_params=pltpu.CompilerParams(dimension_semantics=("arbitrary",), vmem_limit_bytes=VMEM_LIMIT),
    )(chip, dx1, x, mixed, proj, proj, hl, hl, hb, ycat, mod, vecd, vecl, *win, *wout, gab, a64)


def _wgrad_mlp(collective_id, h2b, dz, act, dmo):
    s = h2b.shape[0]
    nstep = 2 * N_CHIP
    half = FF_BLK // 2

    def body(h2_ref, dz_ref, act_ref, dmo_ref, p1_hbm, p2_hbm, buf, landed, summed, send_sems, recv_sems, out_sems):
        j = pl.program_id(0)
        x, y, c, _ = _position()

        def give(jj):
            return pltpu.make_async_remote_copy(
                src_ref=buf.at[jj % 2, pl.ds((1 - c) * half, half), :], dst_ref=landed.at[jj],
                send_sem=send_sems.at[jj % 2], recv_sem=recv_sems.at[jj],
                device_id=(x, y, 1 - c), device_id_type=MESH)

        def write_out(jj, dst):
            return pltpu.make_async_copy(summed.at[jj % 2], dst, out_sems.at[jj % 2])

        def add_sibling(jj):
            give(jj).wait_recv()
            own = buf[jj % 2, pl.ds(pl.multiple_of(c * half, half), half), :]
            summed[jj % 2] = (own.astype(F32) + landed[jj].astype(F32)).astype(BF16)

        @pl.when(j == 0)
        def _():
            pl.semaphore_signal(pltpu.get_barrier_semaphore(), inc=1, device_id=(x, y, 1 - c), device_id_type=MESH)

        @pl.when(j >= 2)
        def _():
            give(j - 2).wait_send()

        @pl.when(j < N_CHIP)
        def _():
            buf[j % 2] = _dot_tn(act_ref[...], dmo_ref[...]).astype(BF16)

        @pl.when(j >= N_CHIP)
        def _():
            buf[j % 2] = _dot_tn(h2_ref[...], dz_ref[...]).astype(BF16)

        @pl.when(j == 0)
        def _():
            pl.semaphore_wait(pltpu.get_barrier_semaphore(), 1)

        give(j).start()

        @pl.when(j >= 1)
        def _():
            jm = j - 1

            @pl.when(jm >= 2)
            def _():
                write_out(jm - 2, p2_hbm.at[0]).wait()

            add_sibling(jm)

            @pl.when(jm < N_CHIP)
            def _():
                write_out(jm, p2_hbm.at[jm]).start()

            @pl.when(jm >= N_CHIP)
            def _():
                write_out(jm, p1_hbm.at[jm - N_CHIP]).start()

        @pl.when(j == nstep - 1)
        def _():
            last = nstep - 1
            write_out(last - 2, p1_hbm.at[0]).wait()
            add_sibling(last)
            write_out(last, p1_hbm.at[N_CHIP - 1]).start()
            for jj in (last - 1, last):
                give(jj).wait_send()
                write_out(jj, p1_hbm.at[0]).wait()

    sds = jax.ShapeDtypeStruct((N_CHIP, half, D_MODEL), BF16)
    whole = pl.BlockSpec((s, D_MODEL), lambda j: (0, 0))
    return pl.pallas_call(
        body, name="wgrad_mlp", grid=(nstep,),
        in_specs=[whole, pl.BlockSpec((s, FF_BLK), lambda j: (0, jnp.maximum(j - N_CHIP, 0))),
                  pl.BlockSpec((s, FF_BLK), lambda j: (0, jnp.minimum(j, N_CHIP - 1))), whole],
        out_specs=[ANY, ANY], out_shape=[sds, sds],
        scratch_shapes=[pltpu.VMEM((2, FF_BLK, D_MODEL), BF16), pltpu.VMEM((nstep, half, D_MODEL), BF16),
                        pltpu.VMEM((2, half, D_MODEL), BF16), pltpu.SemaphoreType.DMA((2,)),
                        pltpu.SemaphoreType.DMA((nstep,)), pltpu.SemaphoreType.DMA((2,))],
        compiler_params=pltpu.CompilerParams(dimension_semantics=("arbitrary",), vmem_limit_bytes=VMEM_LIMIT,
                                             collective_id=collective_id),
    )(h2b, dz, act, dmo)


def _mod_matmul(c_all, ada_w_loc):
    n = ada_w_loc.shape[1]
    cb = 512

    def body(c_ref, w_ref, o_ref):
        c = c_ref[...]
        sc = c * jax.nn.sigmoid(c)
        o_ref[...] = _dot(sc.astype(BF16), w_ref[...].astype(BF16))

    return pl.pallas_call(
        body, name="mod_matmul", grid=(n // cb,),
        in_specs=[_full((8, D_MODEL)), pl.BlockSpec((D_MODEL, cb), lambda j: (0, j))],
        out_specs=pl.BlockSpec((8, cb), lambda j: (0, j)),
        out_shape=jax.ShapeDtypeStruct((8, n), F32),
        compiler_params=pltpu.CompilerParams(dimension_semantics=("arbitrary",), vmem_limit_bytes=VMEM_LIMIT),
    )(c_all, ada_w_loc)


def _adam_math(w, g, m, v):
    m = ADAM_B1 * m + (1.0 - ADAM_B1) * g
    v = ADAM_B2 * v + (1.0 - ADAM_B2) * (g * g)
    m_hat = m / (1.0 - ADAM_B1 ** ADAM_STEP)
    v_hat = v / (1.0 - ADAM_B2 ** ADAM_STEP)
    delta = (-ADAM_LR) * (m_hat / (jnp.sqrt(v_hat) + ADAM_EPS) + ADAM_WD * w)
    return delta, m, v


def _adam(name, core, shards):
    n = len(shards)
    r, c = shards[0][0].shape
    half = r // 2
    rb = min(half, 128)
    nh = half // rb

    def body(core_ref, *refs):
        ins, outs = refs[:5 * n], refs[5 * n:]
        mine = (pl.program_id(0) // nh) == core_ref[0]
        for k in range(n):
            w_ref, go_ref, gs_ref, m_ref, v_ref = ins[5 * k:5 * k + 5]
            g_ref, d_ref, mo_ref, vo_ref = outs[4 * k:4 * k + 4]
            g = jnp.where(mine, go_ref[...], gs_ref[...])
            g_ref[...] = g
            d_ref[...], mo_ref[...], vo_ref[...] = _adam_math(w_ref[...], g, m_ref[...], v_ref[...])

    spec = pl.BlockSpec((rb, c), lambda i, core_ref: (i, 0))
    own = pl.BlockSpec((rb, c), lambda i, core_ref: (jnp.where(i // nh == core_ref[0], i % nh, 0), 0))
    sib = pl.BlockSpec((rb, c), lambda i, core_ref: (jnp.where(i // nh == core_ref[0], 0, i % nh), 0))
    sds = jax.ShapeDtypeStruct((r, c), F32)
    res = pl.pallas_call(
        body, name=name,
        grid_spec=pltpu.PrefetchScalarGridSpec(
            num_scalar_prefetch=1, grid=(r // rb,),
            in_specs=[spec, own, sib, spec, spec] * n, out_specs=[spec] * (4 * n)),
        out_shape=[sds] * (4 * n),
        compiler_params=pltpu.CompilerParams(dimension_semantics=("arbitrary",), vmem_limit_bytes=VMEM_LIMIT),
    )(core, *[t for s in shards for t in s])
    return [res[4 * k:4 * k + 4] for k in range(n)]


def _ada_grad_adam(chip, sct, dmod_cols, w, m, v):
    r, c = w.shape
    rb = 256

    def body(chip_ref, s_ref, dm_ref, w_ref, m_ref, v_ref, g_ref, d_ref, mo_ref, vo_ref):
        g = s_ref[:, 0:1] * dm_ref[0:1, :]
        for b in range(1, 8):
            g = g + s_ref[:, b:b + 1] * dm_ref[b:b + 1, :]
        g_ref[...] = g
        d_ref[...], mo_ref[...], vo_ref[...] = _adam_math(w_ref[...], g, m_ref[...], v_ref[...])

    spec = pl.BlockSpec((rb, c), lambda i, chip_ref: (i, 0))
    sds = jax.ShapeDtypeStruct((r, c), F32)
    return pl.pallas_call(
        body, name="ada_grad_adam",
        grid_spec=pltpu.PrefetchScalarGridSpec(
            num_scalar_prefetch=1, grid=(r // rb,),
            in_specs=[pl.BlockSpec((rb, 8), lambda i, chip_ref: (i, 0)),
                      pl.BlockSpec((8, c), lambda i, chip_ref: (0, chip_ref[0])), spec, spec, spec],
            out_specs=[spec] * 4),
        out_shape=[sds] * 4,
        compiler_params=pltpu.CompilerParams(dimension_semantics=("arbitrary",), vmem_limit_bytes=VMEM_LIMIT),
    )(chip, sct, dmod_cols, w, m, v)


def _position():
    x, y, c = lax.axis_index("x"), lax.axis_index("y"), lax.axis_index("c")
    chips = [(1 - x, y), (x, 1 - y), (1 - x, 1 - y)]
    return x, y, c, chips


def _ag8_run(ins, outs, send_sems, recv_sems, local_sems):
    na = len(ins)
    x, y, c, chips = _position()
    me, sibling = (x, y, c), (x, y, 1 - c)
    first, passed, local = [], [], []
    for a in range(na):
        m_per = ins[a].shape[0]

        def rows(px, py, pc, a=a, m_per=m_per):
            return outs[a].at[pl.ds((4 * px + 2 * py + pc) * m_per, m_per), :]

        def copy(k, block, to, src=None, a=a, rows=rows):
            return pltpu.make_async_remote_copy(
                src_ref=rows(*block) if src is None else src, dst_ref=rows(*block),
                send_sem=send_sems.at[7 * a + k], recv_sem=recv_sems.at[7 * a + k],
                device_id=to, device_id_type=MESH)

        mine = pltpu.make_async_copy(ins[a], rows(*me), local_sems.at[a])
        mine.start()
        local.append(mine)
        f = [copy(0, me, sibling, src=ins[a])]
        f += [copy(1 + j, me, (*chip, c), src=ins[a]) for j, chip in enumerate(chips)]
        for cp in f:
            cp.start()
        first.append((f, copy))
    for a in range(na):
        f, copy = first[a]
        p = [copy(4 + j, (*chip, c), sibling) for j, chip in enumerate(chips)]
        for j, chip in enumerate(chips):
            copy(1 + j, (*chip, c), me).wait_recv()
            p[j].start()
        passed.append(p)
    for a in range(na):
        f, copy = first[a]
        copy(0, sibling, me).wait_recv()
        for j, chip in enumerate(chips):
            copy(4 + j, (*chip, 1 - c), me).wait_recv()
        for cp in f + passed[a]:
            cp.wait_send()
        local[a].wait()


def _allgather8(name, arrs):
    na = len(arrs)

    def body(*refs):
        _ag8_run(refs[:na], refs[na:2 * na], *refs[2 * na:])

    return pl.pallas_call(
        body, name=name,
        out_shape=[jax.ShapeDtypeStruct((8 * a.shape[0], a.shape[1]), a.dtype) for a in arrs],
        in_specs=[VMEM] * na, out_specs=[VMEM] * na,
        scratch_shapes=[pltpu.SemaphoreType.DMA((7 * na,)), pltpu.SemaphoreType.DMA((7 * na,)),
                        pltpu.SemaphoreType.DMA((na,))],
        compiler_params=pltpu.CompilerParams(vmem_limit_bytes=VMEM_LIMIT),
    )(*arrs)


AG_SEMS = 7
AG_CHUNKS = 2


def _ag_copies(ins, outs, send_sems, recv_sems):
    x, y, c, chips = _position()
    sibling = (x, y, 1 - c)
    xn, yn, dg = [2 * chip[0] + chip[1] for chip in chips]
    to_x, to_y = (1 - x, y, c), (x, 1 - y, c)
    res = []
    for a in range(len(ins)):
        half = ins[a].shape[0] // 2
        piece = half // AG_CHUNKS
        for p in range(AG_CHUNKS):
            def copy(k, dst, to, src=None, base=AG_SEMS * (AG_CHUNKS * a + p)):
                return pltpu.make_async_remote_copy(
                    src_ref=dst if src is None else src, dst_ref=dst,
                    send_sem=send_sems.at[base + k], recv_sem=recv_sems.at[base + k],
                    device_id=to, device_id_type=MESH)

            def rows(chip, pc, q=None, a=a, start=p * piece, half=half, piece=piece):
                if q is None:
                    return outs[a].at[chip, pl.ds(pc * half + start, piece), :]
                return outs[a].at[chip, pl.ds(pc * half + start + q * (piece // 2), piece // 2), :]

            own = ins[a].at[pl.ds(c * half + p * piece, piece), :]
            mine = rows(2 * x + y, c)
            res.append(dict(
                sends=[copy(0, mine, to_x, src=own), copy(1, mine, to_y, src=own)],
                from_x=copy(0, rows(xn, c), to_x), from_y=copy(1, rows(yn, c), to_y),
                relay_y=copy(2, rows(xn, c, 0), to_y), relay_x=copy(3, rows(yn, c, 1), to_x),
                from_y_relay=copy(2, rows(dg, c, 0), to_y), from_x_relay=copy(3, rows(dg, c, 1), to_x),
                pass_on=[copy(4, rows(xn, c), sibling), copy(5, rows(yn, c), sibling), copy(6, rows(dg, c), sibling)],
                from_sibling=[copy(4, rows(xn, 1 - c), sibling), copy(5, rows(yn, 1 - c), sibling),
                              copy(6, rows(dg, 1 - c), sibling)]))
    return res


def _ag_start(ins, outs, send_sems, recv_sems):
    for cps in _ag_copies(ins, outs, send_sems, recv_sems):
        for cp in cps["sends"]:
            cp.start()


def _ag_relay(ins, outs, send_sems, recv_sems):
    for cps in _ag_copies(ins, outs, send_sems, recv_sems):
        cps["from_x"].wait_recv()
        cps["relay_y"].start()
        cps["pass_on"][0].start()
        cps["from_y"].wait_recv()
        cps["relay_x"].start()
        cps["pass_on"][1].start()


def _ag_complete(ins, outs, send_sems, recv_sems):
    copies = _ag_copies(ins, outs, send_sems, recv_sems)
    for cps in copies:
        cps["from_y_relay"].wait_recv()
        cps["from_x_relay"].wait_recv()
        cps["pass_on"][2].start()
    for cps in copies:
        for cp in cps["from_sibling"]:
            cp.wait_recv()
        for cp in cps["sends"] + [cps["relay_y"], cps["relay_x"]] + cps["pass_on"]:
            cp.wait_send()


def _ag_finish(ins, outs, send_sems, recv_sems):
    _ag_relay(ins, outs, send_sems, recv_sems)
    _ag_complete(ins, outs, send_sems, recv_sems)


def _allgather_weights(name, collective_id, shards):
    na = len(shards)
    hbm = pltpu.MemorySpace.HBM
    ins = [jax.new_ref(s, memory_space=hbm) for s in shards]
    outs = [jax.empty_ref(jax.ShapeDtypeStruct((N_CHIP,) + s.shape, s.dtype), memory_space=hbm) for s in shards]

    @pl.kernel(mesh=plsc.ScalarSubcoreMesh(axis_name="sequencer", num_cores=1), name=name,
               scratch_types=(pltpu.SemaphoreType.DMA((AG_SEMS * AG_CHUNKS * na,)),
                              pltpu.SemaphoreType.DMA((AG_SEMS * AG_CHUNKS * na,))),
               compiler_params=pltpu.CompilerParams(collective_id=collective_id))
    def launch(send_sems, recv_sems):
        x, y, c, _ = _position()
        peers = [(1 - x, y, c), (x, 1 - y, c), (x, y, 1 - c)]
        barrier = pltpu.get_barrier_semaphore()
        for peer in peers:
            pl.semaphore_signal(barrier, inc=1, device_id=peer, device_id_type=MESH)
        pl.semaphore_wait(barrier, len(peers))
        _ag_start(ins, outs, send_sems, recv_sems)
        _ag_finish(ins, outs, send_sems, recv_sems)

    launch()
    return [o[...] for o in outs]


def _swap_copies(ins, outs, send_sems, recv_sems, split_rows):
    x, y, c, _ = _position()
    cps = []
    for a in range(len(ins)):
        src = ins[a]
        if split_rows:
            half = src.shape[1] // 2
            src = src.at[:, pl.ds((1 - c) * half, half), :]
        cps.append(pltpu.make_async_remote_copy(
            src_ref=src, dst_ref=outs[a], send_sem=send_sems.at[a], recv_sem=recv_sems.at[a],
            device_id=(x, y, 1 - c), device_id_type=MESH))
    return cps


def _swap_and_gather_seq(name, collective_id, swap_arrs, gather_arrs):
    ns, ng = len(swap_arrs), len(gather_arrs)
    hbm = pltpu.MemorySpace.HBM
    s_ins = [jax.new_ref(a, memory_space=hbm) for a in swap_arrs]
    s_outs = [jax.empty_ref(jax.ShapeDtypeStruct(a.shape, a.dtype), memory_space=hbm) for a in swap_arrs]
    g_ins = [jax.new_ref(a, memory_space=hbm) for a in gather_arrs]
    g_outs = [jax.empty_ref(jax.ShapeDtypeStruct((8 * a.shape[0], a.shape[1]), a.dtype), memory_space=hbm)
              for a in gather_arrs]

    @pl.kernel(mesh=plsc.ScalarSubcoreMesh(axis_name="sequencer", num_cores=1), name=name,
               scratch_types=(pltpu.SemaphoreType.DMA((ns,)), pltpu.SemaphoreType.DMA((ns,)),
                              pltpu.SemaphoreType.DMA((7 * ng,)), pltpu.SemaphoreType.DMA((7 * ng,)),
                              pltpu.SemaphoreType.DMA((ng,))),
               compiler_params=pltpu.CompilerParams(collective_id=collective_id))
    def launch(swap_send, swap_recv, send_sems, recv_sems, local_sems):
        x, y, c, chips = _position()
        peers = [(x, y, 1 - c)] + [(*chip, c) for chip in chips]
        barrier = pltpu.get_barrier_semaphore()
        for peer in peers:
            pl.semaphore_signal(barrier, inc=1, device_id=peer, device_id_type=MESH)
        pl.semaphore_wait(barrier, len(peers))
        for cp in _swap_copies(s_ins, s_outs, swap_send, swap_recv, False):
            cp.start()
        _ag8_run(g_ins, g_outs, send_sems, recv_sems, local_sems)
        for cp in _swap_copies(s_ins, s_outs, swap_send, swap_recv, False):
            cp.wait()

    launch()
    return [o[...] for o in s_outs], [o[...] for o in g_outs]


def _sibling_swap(name, arrs, split_rows, collective_id=None):
    na = len(arrs)
    shapes = [jax.ShapeDtypeStruct((a.shape[0], a.shape[1] // 2, a.shape[2]) if split_rows else a.shape, a.dtype)
              for a in arrs]

    def run(ins, outs, send_sems, recv_sems):
        for cp in _swap_copies(ins, outs, send_sems, recv_sems, split_rows):
            cp.start()
        for cp in _swap_copies(ins, outs, send_sems, recv_sems, split_rows):
            cp.wait()

    sems = (pltpu.SemaphoreType.DMA((na,)), pltpu.SemaphoreType.DMA((na,)))
    if collective_id is None:
        return pl.pallas_call(
            lambda *refs: run(refs[:na], refs[na:2 * na], *refs[2 * na:]), name=name, out_shape=shapes,
            in_specs=[ANY] * na, out_specs=[ANY] * na, scratch_shapes=list(sems))(*arrs)

    hbm = pltpu.MemorySpace.HBM
    ins = [jax.new_ref(a, memory_space=hbm) for a in arrs]
    outs = [jax.empty_ref(s, memory_space=hbm) for s in shapes]

    @pl.kernel(mesh=plsc.ScalarSubcoreMesh(axis_name="sequencer", num_cores=1), name=name, scratch_types=sems,
               compiler_params=pltpu.CompilerParams(collective_id=collective_id))
    def launch(send_sems, recv_sems):
        x, y, c, _ = _position()
        barrier = pltpu.get_barrier_semaphore()
        pl.semaphore_signal(barrier, inc=1, device_id=(x, y, 1 - c), device_id_type=MESH)
        pl.semaphore_wait(barrier, 1)
        run(ins, outs, send_sems, recv_sems)

    launch()
    return [o[...] for o in outs]


def _xchg_copies(ins, outs, send_sems, recv_sems):
    x, y, c, chips = _position()
    return [pltpu.make_async_remote_copy(
        src_ref=ins[a].at[2 * chip[0] + chip[1]], dst_ref=outs[a].at[j],
        send_sem=send_sems.at[3 * a + j], recv_sem=recv_sems.at[3 * a + j],
        device_id=(*chip, c), device_id_type=MESH) for a in range(len(ins)) for j, chip in enumerate(chips)]


def _exchange_chips(name, collective_id, parts):
    na = len(parts)
    hbm = pltpu.MemorySpace.HBM
    ins = [jax.new_ref(p, memory_space=hbm) for p in parts]
    outs = [jax.empty_ref(jax.ShapeDtypeStruct((3,) + p.shape[1:], p.dtype), memory_space=hbm) for p in parts]

    @pl.kernel(mesh=plsc.ScalarSubcoreMesh(axis_name="sequencer", num_cores=1), name=name,
               scratch_types=(pltpu.SemaphoreType.DMA((3 * na,)), pltpu.SemaphoreType.DMA((3 * na,))),
               compiler_params=pltpu.CompilerParams(collective_id=collective_id))
    def launch(send_sems, recv_sems):
        x, y, c, chips = _position()
        barrier = pltpu.get_barrier_semaphore()
        for chip in chips:
            pl.semaphore_signal(barrier, inc=1, device_id=(*chip, c), device_id_type=MESH)
        pl.semaphore_wait(barrier, len(chips))
        for cp in _xchg_copies(ins, outs, send_sems, recv_sems):
            cp.start()
        for cp in _xchg_copies(ins, outs, send_sems, recv_sems):
            cp.wait()

    launch()
    return [q[...] for q in outs]


def _add_sibling(name, grad, recv, core, after=()):
    _, r, c = grad.shape
    half = r // 2
    rb = half
    nrb = half // rb

    def body(core_ref, g_ref, r_ref, *refs):
        refs[-1][...] = (g_ref[...].astype(F32) + r_ref[...].astype(F32)).astype(BF16)

    return pl.pallas_call(
        body, name=name,
        grid_spec=pltpu.PrefetchScalarGridSpec(
            num_scalar_prefetch=1, grid=(N_CHIP, nrb),
            in_specs=[pl.BlockSpec((1, rb, c), lambda j, i, core_ref: (j, core_ref[0] * nrb + i, 0)),
                      pl.BlockSpec((1, rb, c), lambda j, i, core_ref: (j, i, 0))] + [ANY] * len(after),
            out_specs=pl.BlockSpec((1, rb, c), lambda j, i, core_ref: (j, i, 0))),
        out_shape=jax.ShapeDtypeStruct((N_CHIP, half, c), BF16),
        compiler_params=pltpu.CompilerParams(dimension_semantics=("arbitrary", "arbitrary"),
                                             vmem_limit_bytes=VMEM_LIMIT),
    )(core, grad, recv, *after)


def _add_chips(name, chip, p, q, after=()):
    _, half, c = q.shape
    rb = min(half, 256)

    def body(chip_ref, p_ref, q_ref, *refs):
        acc = p_ref[0].astype(F32)
        for j in range(3):
            acc = acc + q_ref[j].astype(F32)
        refs[-1][...] = acc

    return pl.pallas_call(
        body, name=name,
        grid_spec=pltpu.PrefetchScalarGridSpec(
            num_scalar_prefetch=1, grid=(half // rb,),
            in_specs=[pl.BlockSpec((1, rb, c), lambda i, chip_ref: (chip_ref[0], i, 0)),
                      pl.BlockSpec((3, rb, c), lambda i, chip_ref: (0, i, 0))] + [ANY] * len(after),
            out_specs=pl.BlockSpec((rb, c), lambda i, chip_ref: (i, 0))),
        out_shape=jax.ShapeDtypeStruct((half, c), F32),
        compiler_params=pltpu.CompilerParams(dimension_semantics=("arbitrary",), vmem_limit_bytes=VMEM_LIMIT),
    )(chip, p, q, *after)


ADD_SWAP_STEPS = 2


def _add_chips_swap(name, chip, ps, qs, after=()):
    n = len(ps)
    steps = ADD_SWAP_STEPS

    def body(chip_ref, *refs):
        own, sib = refs[len(refs) - 2 * n - 2:len(refs) - n - 2], refs[len(refs) - n - 2:len(refs) - 2]
        send_sems, recv_sems = refs[-2:]
        i = pl.program_id(0)
        x, y, c, _ = _position()

        def copy(k, s):
            rb = own[k].shape[0] // steps
            rows = pl.ds(s * rb if isinstance(s, int) else pl.multiple_of(s * rb, rb), rb)
            return pltpu.make_async_remote_copy(
                src_ref=own[k].at[rows, :], dst_ref=sib[k].at[rows, :],
                send_sem=send_sems.at[k * steps + s], recv_sem=recv_sems.at[k * steps + s],
                device_id=(x, y, 1 - c), device_id_type=MESH)

        for k in range(n):
            acc = refs[2 * k][0].astype(F32)
            for j in range(3):
                acc = acc + refs[2 * k + 1][j].astype(F32)
            rb = own[k].shape[0] // steps
            own[k][pl.ds(pl.multiple_of(i * rb, rb), rb), :] = acc
            copy(k, i).start()

        @pl.when(i == steps - 1)
        def _():
            for k in range(n):
                for s in range(steps):
                    copy(k, s).wait()

    in_specs, own_specs, shapes = [], [], []
    for q in qs:
        _, half, c = q.shape
        rb = half // steps
        in_specs += [pl.BlockSpec((1, rb, c), lambda i, chip_ref: (chip_ref[0], i, 0)),
                     pl.BlockSpec((3, rb, c), lambda i, chip_ref: (0, i, 0))]
        own_specs.append(pl.BlockSpec((half, c), lambda i, chip_ref: (0, 0)))
        shapes.append(jax.ShapeDtypeStruct((half, c), F32))
    res = pl.pallas_call(
        body, name=name,
        grid_spec=pltpu.PrefetchScalarGridSpec(
            num_scalar_prefetch=1, grid=(steps,), in_specs=in_specs + [ANY] * len(after),
            out_specs=own_specs + [ANY] * n,
            scratch_shapes=[pltpu.SemaphoreType.DMA((n * steps,)), pltpu.SemaphoreType.DMA((n * steps,))]),
        out_shape=shapes + shapes,
        compiler_params=pltpu.CompilerParams(dimension_semantics=("arbitrary",), vmem_limit_bytes=VMEM_LIMIT),
    )(chip, *[t for pair in zip(ps, qs) for t in pair], *after)
    return res[:n], res[n:]


def _small_update(gad, gam, gl, gg, mychip, params):
    names = ["ada_b", "norm1_g", "lru_conv_b", "gate_a_w", "gate_a_b", "gate_x_w", "gate_x_b", "a_param",
             "lru_conv_w", "short_conv_w", "lru_out_g", "conv_out_g", "norm2_g", "final_g"]
    flat = [t for n in names for t in params[n]]
    nin = len(flat)

    def body(chip_ref, gad_ref, gam_ref, gl_ref, gg_ref, *refs):
        ins = {n: refs[3 * k:3 * k + 3] for k, n in enumerate(names)}
        outs = {n: refs[nin + 4 * k:nin + 4 * k + 4] for k, n in enumerate(names)}
        loss_ref, dmod_ref = refs[nin + 4 * len(names):nin + 4 * len(names) + 2]

        def dsum(ref, lo, n):
            per = ref.shape[0] // 8
            acc = ref[lo:lo + n, :].astype(F32)
            for dev in range(1, 8):
                acc = acc + ref[dev * per + lo:dev * per + lo + n, :].astype(F32)
            return acc

        def update(n, g):
            w_ref, m_ref, v_ref = ins[n]
            g_ref, d_ref, mo_ref, vo_ref = outs[n]
            g = g.reshape(w_ref.shape)
            g_ref[...] = g
            d_ref[...], mo_ref[...], vo_ref[...] = _adam_math(w_ref[...], g, m_ref[...], v_ref[...])

        d, dm, l, lw, rows = refs[-5:]
        d[...] = dsum(gad_ref, 0, 8)
        dm[...] = dsum(gam_ref, 0, 8)
        l[...] = dsum(gl_ref, 0, 16)
        for dev in range(8):
            for k in range(3):
                dmod_ref[dev:dev + 1, k * D_MODEL:(k + 1) * D_MODEL] = gad_ref[dev * 8 + k:dev * 8 + k + 1, :]
                dmod_ref[dev:dev + 1, (3 + k) * D_MODEL:(4 + k) * D_MODEL] = gam_ref[dev * 8 + k:dev * 8 + k + 1, :]
        w_ref, m_ref, v_ref = ins["ada_b"]
        g_ref, d_ref, mo_ref, vo_ref = outs["ada_b"]
        for k in range(3):
            g_ref[:, k * D_MODEL:(k + 1) * D_MODEL] = d[k:k + 1, :]
            g_ref[:, (3 + k) * D_MODEL:(4 + k) * D_MODEL] = dm[k:k + 1, :]
        d_ref[...], mo_ref[...], vo_ref[...] = _adam_math(w_ref[...], g_ref[...], m_ref[...], v_ref[...])
        update("norm1_g", d[3:4, :])
        update("norm2_g", dm[3:4, :])
        update("final_g", dm[4:5, :])
        update("gate_a_b", d[4:5, 0:D_LRU])
        update("gate_x_b", d[4:5, D_LRU:2 * D_LRU])
        update("lru_conv_b", l[4:5, :])
        update("a_param", l[8:9, :] * jax.nn.sigmoid(ins["a_param"][0][...]))
        update("lru_out_g", l[9:10, :])
        update("conv_out_g", l[10:11, :])
        loss_ref[...] = dm[5:6, 0:1]
        chip = chip_ref[0]
        acc = jnp.zeros((8, 128), F32)
        for j in range(N_CHIP):
            acc = acc + jnp.where(chip == j, l[0:8, j * 128:(j + 1) * 128], 0.0)
        lw[...] = acc
        update("lru_conv_w", lw[0:4, :])
        for s, ref in enumerate(ins["short_conv_w"]):
            for k in range(3):
                rows[s, k:k + 1, :] = ref[k]
        g = lw[5:8, :]
        new = (g,) + tuple(_adam_math(rows[0, 0:3, :], g, rows[1, 0:3, :], rows[2, 0:3, :]))
        for o_ref, val in zip(outs["short_conv_w"], new):
            rows[3, 0:3, :] = val
            for k in range(3):
                o_ref[k] = rows[3, k:k + 1, :]
        gates = dsum(gg_ref, 0, D_LRU)
        update("gate_a_w", gates[:, 0:HEAD])
        update("gate_x_w", gates[:, HEAD:2 * HEAD])

    out_shape = []
    for n in names:
        out_shape += [jax.ShapeDtypeStruct(params[n][0].shape, F32)] * 4
    out_shape += [jax.ShapeDtypeStruct((1, 1), F32), jax.ShapeDtypeStruct((8, 6 * D_MODEL), F32)]
    res = pl.pallas_call(
        body, name="small_update", out_shape=out_shape,
        in_specs=[SMEM] + [VMEM] * (4 + nin),
        out_specs=[VMEM] * len(out_shape),
        scratch_shapes=[pltpu.VMEM((8, D_MODEL), F32), pltpu.VMEM((8, D_MODEL), F32), pltpu.VMEM((16, D_LRU), F32),
                        pltpu.VMEM((8, 128), F32), pltpu.VMEM((4, 8, 128), F32)],
        compiler_params=pltpu.CompilerParams(vmem_limit_bytes=VMEM_LIMIT),
    )(mychip, gad, gam, gl, gg, *flat)
    per = {n: res[4 * k:4 * k + 4] for k, n in enumerate(names)}
    return per, res[-2], res[-1]


def _block_diag(w):
    eye = jnp.eye(8, dtype=w.dtype)
    return (eye[:, None, :, None] * w[:, :, None, :]).reshape(8 * HEAD, 8 * HEAD)


def _diag_blocks(g):
    return jnp.concatenate([g[h * HEAD:(h + 1) * HEAD, h * HEAD:(h + 1) * HEAD] for h in range(8)], axis=0)


def kernel(x, c, ada_w, ada_b, norm1_g, w_in, lru_conv_w, lru_conv_b, gate_a_w, gate_a_b, gate_x_w, gate_x_b, a_param, short_conv_w, lru_out_g, conv_out_g, w_out, norm2_g, w_mlp1, w_mlp2, final_g, loss_target, m_ada_w, m_ada_b, m_norm1_g, m_w_in, m_lru_conv_w, m_lru_conv_b, m_gate_a_w, m_gate_a_b, m_gate_x_w, m_gate_x_b, m_a_param, m_short_conv_w, m_lru_out_g, m_conv_out_g, m_w_out, m_norm2_g, m_w_mlp1, m_w_mlp2, m_final_g, v_ada_w, v_ada_b, v_norm1_g, v_w_in, v_lru_conv_w, v_lru_conv_b, v_gate_a_w, v_gate_a_b, v_gate_x_w, v_gate_x_b, v_a_param, v_short_conv_w, v_lru_out_g, v_conv_out_g, v_w_out, v_norm2_g, v_w_mlp1, v_w_mlp2, v_final_g):
    xi, yi, ci = lax.axis_index("x"), lax.axis_index("y"), lax.axis_index("c")
    mychip = 2 * xi + yi
    me = 4 * xi + 2 * yi + ci

    own_in, own_out = w_in[0].astype(BF16), w_out[0].astype(BF16)
    win_all, wout_all = _allgather_weights("allgather_mixer_weights", 1, [own_in, own_out])
    own_w1, own_w2 = w_mlp1[0].astype(BF16), w_mlp2[0].astype(BF16)
    w1_all, w2_all = _allgather_weights("allgather_mlp_weights", 2, [own_w1, own_w2])

    c_blk = jnp.zeros((8, D_MODEL), F32).at[0:1].set(c)
    cw_blk = jnp.zeros((8, 128), F32).at[0:4].set(lru_conv_w[0]).at[4:7].set(short_conv_w[0])
    c_g, cw_g = _allgather8("allgather_cond", [c_blk, cw_blk])
    c_all = c_g.reshape(8, 8, D_MODEL)[:, 0]
    cw_g = cw_g.reshape(4, 2, 8, 128)[:, 0]
    lcw = cw_g[:, 0:4].transpose(1, 0, 2).reshape(4, D_LRU)
    scw = cw_g[:, 4:7].transpose(1, 0, 2).reshape(3, D_LRU)

    mod_loc = _mod_matmul(c_all, ada_w[0])
    (mod_g,) = _allgather8("allgather_mod", [mod_loc])
    mod_all = mod_g.reshape(4, 2, 8, 6 * D_MODEL // 4)[:, 0].transpose(1, 0, 2).reshape(8, 6 * D_MODEL) + ada_b
    mod_pad = jnp.pad(mod_all.reshape(8, 6, D_MODEL), ((0, 0), (0, 2), (0, 0)))
    mod = lax.dynamic_slice_in_dim(mod_pad, me, 1, axis=0).reshape(8, D_MODEL)

    win, wout = (win_all, own_in), (wout_all, own_out)
    chip = mychip.reshape(1).astype(jnp.int32)
    core = ci.reshape(1).astype(jnp.int32)

    vecd = jnp.concatenate([norm1_g, norm2_g, final_g[None, :], jnp.concatenate([gate_a_b, gate_x_b], axis=1),
                            jnp.zeros((4, D_MODEL), F32)], axis=0)
    vecl = jnp.concatenate([lcw, lru_conv_b, scw, a_param, lru_out_g, conv_out_g, jnp.zeros((5, D_LRU), F32)], axis=0)
    gab = jnp.concatenate([_block_diag(gate_a_w[0]), _block_diag(gate_x_w[0])], axis=1).astype(BF16)
    a64 = _block_diag(jnp.full((8, HEAD, HEAD), 1.0 / HEAD, F32)).astype(BF16)

    hb, proj, hl, ycat, mixed, x1 = _mix_fwd(chip, x[0], mod, vecd, vecl, win, wout, gab, a64)
    dx1, act, dz, dmo, h2b, accm = _mlp_fwd_bwd(
        chip, x1, loss_target[0], mod, vecd, (w1_all, own_w1), (w2_all, own_w2))

    parts_mlp = list(_wgrad_mlp(6, h2b, dz, act, dmo))
    q_w1, q_w2 = _exchange_chips("rs_exchange_mlp", 0, parts_mlp)
    grad_x, accd, accl, g_win, g_wout, g_gate = _mix_bwd(
        chip, dx1, x[0], mixed, proj, hl, hb, ycat, mod, vecd, vecl, win, wout, gab, a64)

    g_mix = [g_win, g_wout.reshape(N_CHIP, WOUT_BLK, D_MODEL)]
    recv_mix = _sibling_swap("rs_swap_halves_mix", g_mix, True, collective_id=4)
    own_mlp = [_add_chips("rs_add_chips_mlp%d" % k, chip, p, q) for k, (p, q) in enumerate(zip(parts_mlp, (q_w1, q_w2)))]
    gg_blk = jnp.concatenate([_diag_blocks(g_gate[:, 0:D_LRU]), _diag_blocks(g_gate[:, D_LRU:2 * D_LRU])], axis=1)
    sib_mlp, (gad, gam, gl, gg) = _swap_and_gather_seq(
        "rs_swap_reduced_mlp_allgather_small_grads", 5, own_mlp, [accd, accm, accl, gg_blk.astype(BF16)])

    parts_mix = [_add_sibling("rs_add_sibling_mix%d" % k, g, r, core)
                 for k, (g, r) in enumerate(zip(g_mix, recv_mix))]
    landed_mix = _exchange_chips("rs_exchange_mix", 3, parts_mix)
    res_w1, res_w2 = _adam("adam_mlp", core, [(w_mlp1[0], own_mlp[0], sib_mlp[0], m_w_mlp1[0], v_w_mlp1[0]),
                                              (w_mlp2[0], own_mlp[1], sib_mlp[1], m_w_mlp2[0], v_w_mlp2[0])])

    params = {
        "ada_b": (ada_b, m_ada_b, v_ada_b), "norm1_g": (norm1_g, m_norm1_g, v_norm1_g),
        "lru_conv_b": (lru_conv_b, m_lru_conv_b, v_lru_conv_b),
        "gate_a_w": tuple(t.reshape(D_LRU, HEAD) for t in (gate_a_w, m_gate_a_w, v_gate_a_w)),
        "gate_a_b": (gate_a_b, m_gate_a_b, v_gate_a_b),
        "gate_x_w": tuple(t.reshape(D_LRU, HEAD) for t in (gate_x_w, m_gate_x_w, v_gate_x_w)),
        "gate_x_b": (gate_x_b, m_gate_x_b, v_gate_x_b), "a_param": (a_param, m_a_param, v_a_param),
        "lru_conv_w": (lru_conv_w, m_lru_conv_w, v_lru_conv_w),
        "short_conv_w": tuple(t.reshape(3, 1, D_LRU // N_CHIP) for t in (short_conv_w, m_short_conv_w, v_short_conv_w)),
        "lru_out_g": (lru_out_g, m_lru_out_g, v_lru_out_g), "conv_out_g": (conv_out_g, m_conv_out_g, v_conv_out_g),
        "norm2_g": (norm2_g, m_norm2_g, v_norm2_g),
        "final_g": tuple(t[None, :] for t in (final_g, m_final_g, v_final_g)),
    }
    small, loss_blk, dmod_cols = _small_update(gad, gam, gl, gg, chip, params)
    loss = loss_blk.reshape(())

    sct = (c_all * jax.nn.sigmoid(c_all)).T
    ada = _ada_grad_adam(chip, sct, dmod_cols, ada_w[0], m_ada_w[0], v_ada_w[0])

    own_mix, sib_mix = _add_chips_swap("rs_add_chips_swap_mix", chip, parts_mix, landed_mix, after=[res_w1[1], ada[1]])
    (res_win,) = _adam("adam_w_in", core, [(w_in[0], own_mix[0], sib_mix[0], m_w_in[0], v_w_in[0])])
    (res_wout,) = _adam("adam_w_out", core, [(w_out[0], own_mix[1], sib_mix[1], m_w_out[0], v_w_out[0])])

    res = {"ada_w": ada, "w_in": res_win, "w_out": res_wout, "w_mlp1": res_w1, "w_mlp2": res_w2}
    res = {n: tuple(t[None] for t in r) for n, r in res.items()}
    shapes = {"gate_a_w": gate_a_w.shape, "gate_x_w": gate_x_w.shape, "lru_conv_w": lru_conv_w.shape,
              "short_conv_w": short_conv_w.shape, "final_g": final_g.shape}
    for n, t in small.items():
        res[n] = tuple(u.reshape(shapes[n]) if n in shapes else u for u in t)

    order = ["ada_w", "ada_b", "norm1_g", "w_in", "lru_conv_w", "lru_conv_b", "gate_a_w", "gate_a_b", "gate_x_w",
             "gate_x_b", "a_param", "short_conv_w", "lru_out_g", "conv_out_g", "w_out", "norm2_g", "w_mlp1",
             "w_mlp2", "final_g"]
    return (loss, grad_x[None], *[res[n][0] for n in order], *[res[n][1] for n in order],
            *[res[n][2] for n in order], *[res[n][3] for n in order])
```
